```python
import math
import jax, jax.numpy as jnp
from jax import lax
import numpy as np

D_MODEL = 1024
BATCH = 8
SEQ = 4096
DEPTH = 2

CHUNK = 64
N_MEM = 256
MEM_HEADS = 4
MEM_HEAD_DIM = D_MODEL // MEM_HEADS
D_MIX = D_MODEL
POOL_WINDOWS = (2, 4, 8, 16)
POOL_WIDTH = D_MIX // 4
POOL_GROUP = POOL_WIDTH // len(POOL_WINDOWS)
QK_NOPE = 128
QK_ROPE = 64
V_HEAD = 128
MLA_HEADS = (D_MIX - POOL_WIDTH) // V_HEAD
Q_LORA = 256
KV_LORA = 128
ROPE_BASE = 10000.0
D_FF = 2816
Q_BLOCK = 128
D_IN = POOL_WIDTH + Q_LORA + KV_LORA + QK_ROPE
ALPHA = (2 * DEPTH) ** 0.25
BETA = (8 * DEPTH) ** -0.25
LN_EPS = 1e-5
RMS_EPS = 1e-6
NEG_INF = -1e30

kernel_name = 'hybrid_pool_mla_macaron_deepnorm'


def layer_norm(x, g, b):
    xf = x.astype(jnp.float32)
    mu = jnp.mean(xf, axis=-1, keepdims=True)
    var = jnp.mean(jnp.square(xf - mu), axis=-1, keepdims=True)
    y = (xf - mu) * lax.rsqrt(var + LN_EPS) * g.astype(jnp.float32) + b.astype(jnp.float32)
    return y.astype(x.dtype)


def rms_norm(x, g):
    xf = x.astype(jnp.float32)
    y = xf * lax.rsqrt(jnp.mean(jnp.square(xf), axis=-1, keepdims=True) + RMS_EPS)
    return (y * g.astype(jnp.float32)).astype(x.dtype)


def swiglu(x, w13, w2):
    gate, up = jnp.split(x @ w13, 2, axis=-1)
    return (jax.nn.silu(gate) * up) @ w2


def rope(x, positions):
    half = QK_ROPE // 2
    inv_freq = ROPE_BASE ** (-jnp.arange(half, dtype=jnp.float32) / half)
    ang = positions.astype(jnp.float32)[..., None] * inv_freq
    ang = ang.reshape(ang.shape[:2] + (1,) * (x.ndim - 3) + (half,))
    cos, sin = jnp.cos(ang), jnp.sin(ang)
    xf = x.astype(jnp.float32)
    x1, x2 = xf[..., :half], xf[..., half:]
    out = jnp.concatenate([x1 * cos - x2 * sin, x2 * cos + x1 * sin], axis=-1)
    return out.astype(x.dtype)


def pool_mixer(u, pool_w, pool_scale):
    B, S, _ = u.shape
    uf = u.astype(jnp.float32)
    cs = jnp.cumsum(uf, axis=1)
    t = jnp.arange(S)
    means = []
    for g, w in enumerate(POOL_WINDOWS):
        csg = cs[..., g * POOL_GROUP:(g + 1) * POOL_GROUP]
        prev = jnp.pad(csg[:, :S - w], ((0, 0), (w, 0), (0, 0)))
        cnt = jnp.minimum(t + 1, w).astype(jnp.float32)[None, :, None]
        means.append((csg - prev) / cnt)
    d = (jnp.concatenate(means, axis=-1) - uf).astype(u.dtype)
    d = d.reshape(B, S, len(POOL_WINDOWS), POOL_GROUP)
    y = jnp.einsum('bsgc,gcd->bsgd', d, pool_w).reshape(B, S, POOL_WIDTH)
    return y * pool_scale


def mla_mixer(c_q, c_kv, k_pe, positions, q_norm_g, w_uq, kv_norm_g, w_ukv):
    B, S, _ = c_q.shape
    H = MLA_HEADS
    q = (rms_norm(c_q, q_norm_g) @ w_uq).reshape(B, S, H, QK_NOPE + QK_ROPE)
    q = jnp.concatenate([q[..., :QK_NOPE], rope(q[..., QK_NOPE:], positions)], axis=-1)
    kv = (rms_norm(c_kv, kv_norm_g) @ w_ukv).reshape(B, S, H, QK_NOPE + V_HEAD)
    k_nope, v = kv[..., :QK_NOPE], kv[..., QK_NOPE:]
    k_rot = rope(k_pe, positions)
    k = jnp.concatenate([k_nope, jnp.broadcast_to(k_rot[:, :, None, :], (B, S, H, QK_ROPE))], axis=-1)
    scale = (QK_NOPE + QK_ROPE) ** -0.5
    nb = S // Q_BLOCK
    q_blocks = q.reshape(B, nb, Q_BLOCK, H, QK_NOPE + QK_ROPE).transpose(1, 0, 2, 3, 4)
    q_idx = jnp.arange(S).reshape(nb, Q_BLOCK)
    k_chunk = jnp.arange(S) // CHUNK

    def attend(args):
        qb, qi = args
        s = jnp.einsum('bqhd,bkhd->bhqk', qb, k, preferred_element_type=jnp.float32) * scale
        mask = (qi[:, None] // CHUNK) >= k_chunk[None, :]
        s = jnp.where(mask[None, None], s, NEG_INF)
        p = jax.nn.softmax(s, axis=-1).astype(v.dtype)
        return jnp.einsum('bhqk,bkhd->bqhd', p, v)

    o = lax.map(attend, (q_blocks, q_idx))
    return o.transpose(1, 0, 2, 3, 4).reshape(B, S, H * V_HEAD)


def memory_cross_attention(x, mem, wq, wkv, wo):
    B, S, _ = x.shape
    q = (x @ wq).reshape(B, S, MEM_HEADS, MEM_HEAD_DIM)
    k, v = jnp.split(mem @ wkv, 2, axis=-1)
    k = k.reshape(B, mem.shape[1], MEM_HEADS, MEM_HEAD_DIM)
    v = v.reshape(B, mem.shape[1], MEM_HEADS, MEM_HEAD_DIM)
    s = jnp.einsum('bshd,bmhd->bhsm', q, k, preferred_element_type=jnp.float32) * MEM_HEAD_DIM ** -0.5
    p = jax.nn.softmax(s, axis=-1).astype(v.dtype)
    o = jnp.einsum('bhsm,bmhd->bshd', p, v).reshape(B, S, D_MODEL)
    return o @ wo


def _fwd_setup_inputs(seed: int = 0) -> dict:
    key = jax.random.key(seed)
    ks = jax.random.split(key, 24)

    def nrm(k, shape, scale):
        return jax.random.normal(k, shape, jnp.float32) * scale

    x = nrm(ks[0], (BATCH, SEQ, D_MODEL), 1.0)
    mem = nrm(ks[1], (BATCH, N_MEM, D_MODEL), 1.0)
    start = jax.random.randint(ks[2], (BATCH, 1), 0, 8192, dtype=jnp.int32)
    positions = (start + jnp.arange(SEQ, dtype=jnp.int32)[None, :]).astype(jnp.int32)
    L = DEPTH
    return {
        'x': x,
        'mem': mem,
        'positions': positions,
        'ln_g': 1.0 + nrm(ks[3], (L, 4, D_MODEL), 0.05),
        'ln_b': nrm(ks[4], (L, 4, D_MODEL), 0.02),
        'ffn1_w13': nrm(ks[5], (L, D_MODEL, 2 * D_FF), D_MODEL ** -0.5),
        'ffn1_w2': nrm(ks[6], (L, D_FF, D_MODEL), BETA * D_FF ** -0.5),
        'w_in': nrm(ks[7], (L, D_MODEL, D_IN), D_MODEL ** -0.5),
        'pool_w': nrm(ks[8], (L, len(POOL_WINDOWS), POOL_GROUP, POOL_GROUP), POOL_GROUP ** -0.5),
        'pool_scale': 1.0 + nrm(ks[9], (L, POOL_WIDTH), 0.1),
        'q_norm_g': 1.0 + nrm(ks[10], (L, Q_LORA), 0.05),
        'w_uq': nrm(ks[11], (L, Q_LORA, MLA_HEADS * (QK_NOPE + QK_ROPE)), Q_LORA ** -0.5),
        'kv_norm_g': 1.0 + nrm(ks[12], (L, KV_LORA), 0.05),
        'w_ukv': nrm(ks[13], (L, KV_LORA, MLA_HEADS * (QK_NOPE + V_HEAD)), KV_LORA ** -0.5),
        'w_out': nrm(ks[14], (L, D_MIX, D_MODEL), BETA * D_MIX ** -0.5),
        'mem_wq': nrm(ks[15], (L, D_MODEL, D_MODEL), D_MODEL ** -0.5),
        'mem_wkv': nrm(ks[16], (L, D_MODEL, 2 * D_MODEL), D_MODEL ** -0.5),
        'mem_wo': nrm(ks[17], (L, D_MODEL, D_MODEL), BETA * D_MODEL ** -0.5),
        'ffn2_w13': nrm(ks[18], (L, D_MODEL, 2 * D_FF), D_MODEL ** -0.5),
        'ffn2_w2': nrm(ks[19], (L, D_FF, D_MODEL), BETA * D_FF ** -0.5),
    }


def _fwd_reference(x, mem, positions, ln_g, ln_b, ffn1_w13, ffn1_w2, w_in, pool_w, pool_scale,
              q_norm_g, w_uq, kv_norm_g, w_ukv, w_out, mem_wq, mem_wkv, mem_wo,
              ffn2_w13, ffn2_w2):
    for l in range(DEPTH):
        x = layer_norm(ALPHA * x + 0.5 * swiglu(x, ffn1_w13[l], ffn1_w2[l]), ln_g[l, 0], ln_b[l, 0])
        h = x @ w_in[l]
        o0 = POOL_WIDTH
        o1 = o0 + Q_LORA
        o2 = o1 + KV_LORA
        u_pool, c_q, c_kv, k_pe = h[..., :o0], h[..., o0:o1], h[..., o1:o2], h[..., o2:]
        y_pool = pool_mixer(u_pool, pool_w[l], pool_scale[l])
        y_mla = mla_mixer(c_q, c_kv, k_pe, positions, q_norm_g[l], w_uq[l],
                          kv_norm_g[l], w_ukv[l])
        y_mix = jnp.concatenate([y_pool, y_mla], axis=-1) @ w_out[l]
        x = layer_norm(ALPHA * x + y_mix, ln_g[l, 1], ln_b[l, 1])
        y_mem = memory_cross_attention(x, mem, mem_wq[l], mem_wkv[l], mem_wo[l])
        x = layer_norm(ALPHA * x + y_mem, ln_g[l, 2], ln_b[l, 2])
        x = layer_norm(ALPHA * x + 0.5 * swiglu(x, ffn2_w13[l], ffn2_w2[l]), ln_g[l, 3], ln_b[l, 3])
    return x


import jax as _jax
import jax.numpy as _jnp

TWIN_FORMAT = 'train_step'
FWD_PARAMS = ['x', 'mem', 'positions', 'ln_g', 'ln_b', 'ffn1_w13', 'ffn1_w2', 'w_in', 'pool_w', 'pool_scale', 'q_norm_g', 'w_uq', 'kv_norm_g', 'w_ukv', 'w_out', 'mem_wq', 'mem_wkv', 'mem_wo', 'ffn2_w13', 'ffn2_w2']
TWIN_WEIGHTS = ['ln_g', 'ln_b', 'ffn1_w13', 'ffn1_w2', 'w_in', 'pool_w', 'pool_scale', 'q_norm_g', 'w_uq', 'kv_norm_g', 'w_ukv', 'w_out', 'mem_wq', 'mem_wkv', 'mem_wo', 'ffn2_w13', 'ffn2_w2']
TWIN_DIFF_INPUT = 'x'
TWIN_INPUTS = ['x', 'mem', 'positions', 'ln_g', 'ln_b', 'ffn1_w13', 'ffn1_w2', 'w_in', 'pool_w', 'pool_scale', 'q_norm_g', 'w_uq', 'kv_norm_g', 'w_ukv', 'w_out', 'mem_wq', 'mem_wkv', 'mem_wo', 'ffn2_w13', 'ffn2_w2', 'loss_target', 'm_ln_g', 'm_ln_b', 'm_ffn1_w13', 'm_ffn1_w2', 'm_w_in', 'm_pool_w', 'm_pool_scale', 'm_q_norm_g', 'm_w_uq', 'm_kv_norm_g', 'm_w_ukv', 'm_w_out', 'm_mem_wq', 'm_mem_wkv', 'm_mem_wo', 'm_ffn2_w13', 'm_ffn2_w2', 'v_ln_g', 'v_ln_b', 'v_ffn1_w13', 'v_ffn1_w2', 'v_w_in', 'v_pool_w', 'v_pool_scale', 'v_q_norm_g', 'v_w_uq', 'v_kv_norm_g', 'v_w_ukv', 'v_w_out', 'v_mem_wq', 'v_mem_wkv', 'v_mem_wo', 'v_ffn2_w13', 'v_ffn2_w2']
TWIN_OUTPUTS = ['loss', 'grad_x', 'grad_ln_g', 'grad_ln_b', 'grad_ffn1_w13', 'grad_ffn1_w2', 'grad_w_in', 'grad_pool_w', 'grad_pool_scale', 'grad_q_norm_g', 'grad_w_uq', 'grad_kv_norm_g', 'grad_w_ukv', 'grad_w_out', 'grad_mem_wq', 'grad_mem_wkv', 'grad_mem_wo', 'grad_ffn2_w13', 'grad_ffn2_w2', 'delta_ln_g', 'delta_ln_b', 'delta_ffn1_w13', 'delta_ffn1_w2', 'delta_w_in', 'delta_pool_w', 'delta_pool_scale', 'delta_q_norm_g', 'delta_w_uq', 'delta_kv_norm_g', 'delta_w_ukv', 'delta_w_out', 'delta_mem_wq', 'delta_mem_wkv', 'delta_mem_wo', 'delta_ffn2_w13', 'delta_ffn2_w2', 'new_m_ln_g', 'new_m_ln_b', 'new_m_ffn1_w13', 'new_m_ffn1_w2', 'new_m_w_in', 'new_m_pool_w', 'new_m_pool_scale', 'new_m_q_norm_g', 'new_m_w_uq', 'new_m_kv_norm_g', 'new_m_w_ukv', 'new_m_w_out', 'new_m_mem_wq', 'new_m_mem_wkv', 'new_m_mem_wo', 'new_m_ffn2_w13', 'new_m_ffn2_w2', 'new_v_ln_g', 'new_v_ln_b', 'new_v_ffn1_w13', 'new_v_ffn1_w2', 'new_v_w_in', 'new_v_pool_w', 'new_v_pool_scale', 'new_v_q_norm_g', 'new_v_w_uq', 'new_v_kv_norm_g', 'new_v_w_ukv', 'new_v_w_out', 'new_v_mem_wq', 'new_v_mem_wkv', 'new_v_mem_wo', 'new_v_ffn2_w13', 'new_v_ffn2_w2']
TWIN_LEAF_KINDS = {'loss': 'loss', 'grad_x': 'grad_x', 'grad_ln_g': 'grad_w', 'grad_ln_b': 'grad_w', 'grad_ffn1_w13': 'grad_w', 'grad_ffn1_w2': 'grad_w', 'grad_w_in': 'grad_w', 'grad_pool_w': 'grad_w', 'grad_pool_scale': 'grad_w', 'grad_q_norm_g': 'grad_w', 'grad_w_uq': 'grad_w', 'grad_kv_norm_g': 'grad_w', 'grad_w_ukv': 'grad_w', 'grad_w_out': 'grad_w', 'grad_mem_wq': 'grad_w', 'grad_mem_wkv': 'grad_w', 'grad_mem_wo': 'grad_w', 'grad_ffn2_w13': 'grad_w', 'grad_ffn2_w2': 'grad_w', 'delta_ln_g': 'delta_w', 'delta_ln_b': 'delta_w', 'delta_ffn1_w13': 'delta_w', 'delta_ffn1_w2': 'delta_w', 'delta_w_in': 'delta_w', 'delta_pool_w': 'delta_w', 'delta_pool_scale': 'delta_w', 'delta_q_norm_g': 'delta_w', 'delta_w_uq': 'delta_w', 'delta_kv_norm_g': 'delta_w', 'delta_w_ukv': 'delta_w', 'delta_w_out': 'delta_w', 'delta_mem_wq': 'delta_w', 'delta_mem_wkv': 'delta_w', 'delta_mem_wo': 'delta_w', 'delta_ffn2_w13': 'delta_w', 'delta_ffn2_w2': 'delta_w', 'new_m_ln_g': 'new_m', 'new_m_ln_b': 'new_m', 'new_m_ffn1_w13': 'new_m', 'new_m_ffn1_w2': 'new_m', 'new_m_w_in': 'new_m', 'new_m_pool_w': 'new_m', 'new_m_pool_scale': 'new_m', 'new_m_q_norm_g': 'new_m', 'new_m_w_uq': 'new_m', 'new_m_kv_norm_g': 'new_m', 'new_m_w_ukv': 'new_m', 'new_m_w_out': 'new_m', 'new_m_mem_wq': 'new_m', 'new_m_mem_wkv': 'new_m', 'new_m_mem_wo': 'new_m', 'new_m_ffn2_w13': 'new_m', 'new_m_ffn2_w2': 'new_m', 'new_v_ln_g': 'new_v', 'new_v_ln_b': 'new_v', 'new_v_ffn1_w13': 'new_v', 'new_v_ffn1_w2': 'new_v', 'new_v_w_in': 'new_v', 'new_v_pool_w': 'new_v', 'new_v_pool_scale': 'new_v', 'new_v_q_norm_g': 'new_v', 'new_v_w_uq': 'new_v', 'new_v_kv_norm_g': 'new_v', 'new_v_w_ukv': 'new_v', 'new_v_w_out': 'new_v', 'new_v_mem_wq': 'new_v', 'new_v_mem_wkv': 'new_v', 'new_v_mem_wo': 'new_v', 'new_v_ffn2_w13': 'new_v', 'new_v_ffn2_w2': 'new_v'}


def _forward(args):
    return _fwd_reference(*[args[k] for k in FWD_PARAMS])


def _output_shape():
    out = _jax.eval_shape(lambda: _forward(_fwd_setup_inputs(0)))
    return out.shape, out.dtype

N_MICROBATCH = 1
ADAM_LR = 0.001
ADAM_B1 = 0.9
ADAM_B2 = 0.999
ADAM_EPS = 1e-08
ADAM_WD = 0.01
ADAM_STEP = 10
PER_EXAMPLE_BATCH_AXIS = {'x': 0, 'mem': 0, 'positions': 0, 'loss_target': 0}
SHARED_INPUTS = []
_WEIGHT_DTYPES = {'ln_g': _jnp.float32, 'ln_b': _jnp.float32, 'ffn1_w13': _jnp.float32, 'ffn1_w2': _jnp.float32, 'w_in': _jnp.float32, 'pool_w': _jnp.float32, 'pool_scale': _jnp.float32, 'q_norm_g': _jnp.float32, 'w_uq': _jnp.float32, 'kv_norm_g': _jnp.float32, 'w_ukv': _jnp.float32, 'w_out': _jnp.float32, 'mem_wq': _jnp.float32, 'mem_wkv': _jnp.float32, 'mem_wo': _jnp.float32, 'ffn2_w13': _jnp.float32, 'ffn2_w2': _jnp.float32}
MOMENT_SCALE = {'ln_g': 1.199758e+01, 'ln_b': 7.049663e-01, 'ffn1_w13': 1.161869e-02, 'ffn1_w2': 3.796650e-02, 'w_in': 3.742522e-02, 'pool_w': 5.746766e-02, 'pool_scale': 5.550974e-02, 'q_norm_g': 1.717187e-02, 'w_uq': 8.183049e-03, 'kv_norm_g': 3.880083e-02, 'w_ukv': 9.932966e-03, 'w_out': 5.812993e-02, 'mem_wq': 6.690580e-03, 'mem_wkv': 7.318677e-03, 'mem_wo': 1.585082e-02, 'ffn2_w13': 1.154704e-02, 'ffn2_w2': 3.769139e-02}


def _to_microbatches(a, axis):
    t = _jnp.moveaxis(a, axis, 0)
    t = t.reshape((N_MICROBATCH, t.shape[0] // N_MICROBATCH) + t.shape[1:])
    return _jnp.moveaxis(t, 1, axis + 1)


def setup_inputs(seed: int = 0) -> dict:
    inp = _fwd_setup_inputs(seed)
    key = _jax.random.fold_in(_jax.random.key(seed), 7919)
    shape, _ = _output_shape()
    out = dict(inp)
    out["loss_target"] = _jax.random.normal(_jax.random.fold_in(key, 0), shape, _jnp.float32)
    for i, name in enumerate(TWIN_WEIGHTS):
        w = inp[name].astype(_jnp.float32)
        if MOMENT_SCALE is None:
            s = _jnp.sqrt(_jnp.mean(_jnp.square(w)) + 1e-30)
        else:
            s = MOMENT_SCALE[name]
        km, kv = _jax.random.split(_jax.random.fold_in(key, i + 1))
        out[name] = w
        out["m_" + name] = s * _jax.random.normal(km, w.shape, _jnp.float32)
        out["v_" + name] = (s * s) * _jax.random.uniform(kv, w.shape, _jnp.float32, 0.5, 1.5)
    if N_MICROBATCH > 1:
        for name, axis in PER_EXAMPLE_BATCH_AXIS.items():
            out[name] = _to_microbatches(out[name], axis)
    return {'x': out['x'], 'mem': out['mem'], 'positions': out['positions'], 'ln_g': out['ln_g'], 'ln_b': out['ln_b'], 'ffn1_w13': out['ffn1_w13'], 'ffn1_w2': out['ffn1_w2'], 'w_in': out['w_in'], 'pool_w': out['pool_w'], 'pool_scale': out['pool_scale'], 'q_norm_g': out['q_norm_g'], 'w_uq': out['w_uq'], 'kv_norm_g': out['kv_norm_g'], 'w_ukv': out['w_ukv'], 'w_out': out['w_out'], 'mem_wq': out['mem_wq'], 'mem_wkv': out['mem_wkv'], 'mem_wo': out['mem_wo'], 'ffn2_w13': out['ffn2_w13'], 'ffn2_w2': out['ffn2_w2'], 'loss_target': out['loss_target'], 'm_ln_g': out['m_ln_g'], 'm_ln_b': out['m_ln_b'], 'm_ffn1_w13': out['m_ffn1_w13'], 'm_ffn1_w2': out['m_ffn1_w2'], 'm_w_in': out['m_w_in'], 'm_pool_w': out['m_pool_w'], 'm_pool_scale': out['m_pool_scale'], 'm_q_norm_g': out['m_q_norm_g'], 'm_w_uq': out['m_w_uq'], 'm_kv_norm_g': out['m_kv_norm_g'], 'm_w_ukv': out['m_w_ukv'], 'm_w_out': out['m_w_out'], 'm_mem_wq': out['m_mem_wq'], 'm_mem_wkv': out['m_mem_wkv'], 'm_mem_wo': out['m_mem_wo'], 'm_ffn2_w13': out['m_ffn2_w13'], 'm_ffn2_w2': out['m_ffn2_w2'], 'v_ln_g': out['v_ln_g'], 'v_ln_b': out['v_ln_b'], 'v_ffn1_w13': out['v_ffn1_w13'], 'v_ffn1_w2': out['v_ffn1_w2'], 'v_w_in': out['v_w_in'], 'v_pool_w': out['v_pool_w'], 'v_pool_scale': out['v_pool_scale'], 'v_q_norm_g': out['v_q_norm_g'], 'v_w_uq': out['v_w_uq'], 'v_kv_norm_g': out['v_kv_norm_g'], 'v_w_ukv': out['v_w_ukv'], 'v_w_out': out['v_w_out'], 'v_mem_wq': out['v_mem_wq'], 'v_mem_wkv': out['v_mem_wkv'], 'v_mem_wo': out['v_mem_wo'], 'v_ffn2_w13': out['v_ffn2_w13'], 'v_ffn2_w2': out['v_ffn2_w2']}


def _loss(weights, diff, rest, loss_target):
    with _jax.named_scope("forward"):
        args = {**rest, TWIN_DIFF_INPUT: diff, **{k: w.astype(_WEIGHT_DTYPES[k]) for k, w in weights.items()}}
        y = _forward(args)
    with _jax.named_scope("loss_head"):
        err = _jnp.square(y.astype(_jnp.float32) - loss_target)
        return 0.5 * _jnp.sum(_jnp.mean(err, axis=-1)) if err.ndim else 0.5 * err


def _adamw(w, g, m, v):
    m = ADAM_B1 * m + (1.0 - ADAM_B1) * g
    v = ADAM_B2 * v + (1.0 - ADAM_B2) * _jnp.square(g)
    m_hat = m / (1.0 - ADAM_B1 ** ADAM_STEP)
    v_hat = v / (1.0 - ADAM_B2 ** ADAM_STEP)
    delta = -ADAM_LR * (m_hat / (_jnp.sqrt(v_hat) + ADAM_EPS) + ADAM_WD * w)
    return delta, m, v


def reference(x, mem, positions, ln_g, ln_b, ffn1_w13, ffn1_w2, w_in, pool_w, pool_scale, q_norm_g, w_uq, kv_norm_g, w_ukv, w_out, mem_wq, mem_wkv, mem_wo, ffn2_w13, ffn2_w2, loss_target, m_ln_g, m_ln_b, m_ffn1_w13, m_ffn1_w2, m_w_in, m_pool_w, m_pool_scale, m_q_norm_g, m_w_uq, m_kv_norm_g, m_w_ukv, m_w_out, m_mem_wq, m_mem_wkv, m_mem_wo, m_ffn2_w13, m_ffn2_w2, v_ln_g, v_ln_b, v_ffn1_w13, v_ffn1_w2, v_w_in, v_pool_w, v_pool_scale, v_q_norm_g, v_w_uq, v_kv_norm_g, v_w_ukv, v_w_out, v_mem_wq, v_mem_wkv, v_mem_wo, v_ffn2_w13, v_ffn2_w2):
    given = dict(x=x, mem=mem, positions=positions, ln_g=ln_g, ln_b=ln_b, ffn1_w13=ffn1_w13, ffn1_w2=ffn1_w2, w_in=w_in, pool_w=pool_w, pool_scale=pool_scale, q_norm_g=q_norm_g, w_uq=w_uq, kv_norm_g=kv_norm_g, w_ukv=w_ukv, w_out=w_out, mem_wq=mem_wq, mem_wkv=mem_wkv, mem_wo=mem_wo, ffn2_w13=ffn2_w13, ffn2_w2=ffn2_w2, loss_target=loss_target, m_ln_g=m_ln_g, m_ln_b=m_ln_b, m_ffn1_w13=m_ffn1_w13, m_ffn1_w2=m_ffn1_w2, m_w_in=m_w_in, m_pool_w=m_pool_w, m_pool_scale=m_pool_scale, m_q_norm_g=m_q_norm_g, m_w_uq=m_w_uq, m_kv_norm_g=m_kv_norm_g, m_w_ukv=m_w_ukv, m_w_out=m_w_out, m_mem_wq=m_mem_wq, m_mem_wkv=m_mem_wkv, m_mem_wo=m_mem_wo, m_ffn2_w13=m_ffn2_w13, m_ffn2_w2=m_ffn2_w2, v_ln_g=v_ln_g, v_ln_b=v_ln_b, v_ffn1_w13=v_ffn1_w13, v_ffn1_w2=v_ffn1_w2, v_w_in=v_w_in, v_pool_w=v_pool_w, v_pool_scale=v_pool_scale, v_q_norm_g=v_q_norm_g, v_w_uq=v_w_uq, v_kv_norm_g=v_kv_norm_g, v_w_ukv=v_w_ukv, v_w_out=v_w_out, v_mem_wq=v_mem_wq, v_mem_wkv=v_mem_wkv, v_mem_wo=v_mem_wo, v_ffn2_w13=v_ffn2_w13, v_ffn2_w2=v_ffn2_w2)
    weights = {n: given[n] for n in TWIN_WEIGHTS}
    shared = {n: given[n] for n in SHARED_INPUTS}
    per_example = {n: given[n] for n in ['x', 'mem', 'positions']}
    grad_fn = _jax.value_and_grad(_loss, argnums=(0, 1))

    def one_microbatch(ex, loss_target):
        ex = dict(ex)
        diff = ex.pop(TWIN_DIFF_INPUT)
        return grad_fn(weights, diff, {**shared, **ex}, loss_target)

    if N_MICROBATCH == 1:
        loss, (grad_w, grad_x) = one_microbatch(per_example, given["loss_target"])
    else:
        def body(carry, xs):
            loss_sum, grad_sum = carry
            l_k, (gw_k, gx_k) = one_microbatch(xs[0], xs[1])
            with _jax.named_scope("update"):
                return (loss_sum + l_k, _jax.tree.map(_jnp.add, grad_sum, gw_k)), gx_k

        init = (_jnp.zeros((), _jnp.float32), _jax.tree.map(_jnp.zeros_like, weights))
        (loss, grad_w), grad_x = _jax.lax.scan(body, init, (per_example, given["loss_target"]))
    with _jax.named_scope("update"):
        delta_w, new_m, new_v = {}, {}, {}
        for n in TWIN_WEIGHTS:
            delta_w[n], new_m[n], new_v[n] = _adamw(weights[n], grad_w[n], given["m_" + n], given["v_" + n])
    return (loss, grad_x, *[grad_w[n] for n in TWIN_WEIGHTS], *[delta_w[n] for n in TWIN_WEIGHTS],
            *[new_m[n] for n in TWIN_WEIGHTS], *[new_v[n] for n in TWIN_WEIGHTS])
```

```python
import functools
import math

import jax
import jax.numpy as jnp
from jax import lax
from jax.experimental import pallas as pl
from jax.experimental.pallas import tpu as pltpu

f32 = jnp.float32
bf16 = jnp.bfloat16
SDS = jax.ShapeDtypeStruct
MESH = pl.DeviceIdType.MESH

D_MODEL = 1024
DEPTH = 2
N_MEM = 256
MEM_HEADS = 4
MEM_HEAD_DIM = D_MODEL // MEM_HEADS
POOL_WINDOWS = (2, 4, 8, 16)
POOL_WIDTH = 256
POOL_GROUP = 64
QK_NOPE = 128
QK_ROPE = 64
V_HEAD = 128
MLA_HEADS = 6
Q_LORA = 256
KV_LORA = 128
ROPE_BASE = 10000.0
D_FF = 2816
D_IN = POOL_WIDTH + Q_LORA + KV_LORA + QK_ROPE
ALPHA = (2 * DEPTH) ** 0.25
LN_EPS = 1e-5
RMS_EPS = 1e-6
NEG_INF = -1e30
MLA_SCALE = (QK_NOPE + QK_ROPE) ** -0.5
MEM_SCALE = MEM_HEAD_DIM ** -0.5
ADAM_LR = 0.001
ADAM_B1 = 0.9
ADAM_B2 = 0.999
ADAM_EPS = 1e-08
ADAM_WD = 0.01
ADAM_STEP = 10

N_CHIPS = 4
V7X_VMEM_LIMIT = 56 * 2**20
HALO = 16

_NT = (((1,), (1,)), ((), ()))
_TN = (((0,), (0,)), ((), ()))


def _dot(a, b):
    return jnp.dot(a, b, preferred_element_type=f32)


def _dot_nt(a, b):
    return lax.dot_general(a, b, _NT, preferred_element_type=f32)


def _dot_tn(a, b):
    return lax.dot_general(a, b, _TN, preferred_element_type=f32)


def _cp(*sem):
    return pltpu.CompilerParams(dimension_semantics=sem if sem else None, vmem_limit_bytes=V7X_VMEM_LIMIT)


def _tile(n, t):
    t = min(n, t)
    assert n % t == 0, (n, t)
    return t


def _row_tile(rows, cols, itemsize=4, target=2 * 2**20):
    best = None
    for t in range(16, rows + 1, 16):
        if rows % t == 0 and t * cols * itemsize <= target:
            best = t
    return best if best is not None else rows


def ffn_up(name, xb, w13, l):
    S = xb.shape[0]
    ns = w13.shape[3]
    tm = _tile(S, 512)

    def body(x_ref, wg_ref, wu_ref, g_ref, u_ref, a_ref):
        x = x_ref[...]
        g = _dot(x, wg_ref[0, 0])
        u = _dot(x, wu_ref[0, 0])
        a = g * jax.nn.sigmoid(g) * u
        g_ref[...] = g.astype(bf16)
        u_ref[...] = u.astype(bf16)
        a_ref[...] = a.astype(bf16)

    out = SDS((S, 2 * ns), bf16)
    return pl.pallas_call(
        body, name=name, grid=(2, S // tm),
        in_specs=[pl.BlockSpec((tm, D_MODEL), lambda j, i: (i, 0)),
                  pl.BlockSpec((1, 1, D_MODEL, ns), lambda j, i: (l, j, 0, 0)),
                  pl.BlockSpec((1, 1, D_MODEL, ns), lambda j, i: (l, j + 2, 0, 0))],
        out_specs=[pl.BlockSpec((tm, ns), lambda j, i: (i, j))] * 3,
        out_shape=[out, out, out],
        compiler_params=_cp("parallel", "parallel"),
    )(xb, w13, w13)


def proj_res_ln(name, parts, ws, wl, x, g, b, rscale):
    S = x.shape[0]
    tm = _tile(S, 256)
    n = len(parts)

    def body(*refs):
        p_refs, w_refs = refs[:n], refs[n:2 * n]
        x_ref, g_ref, b_ref, z_ref, y_ref, yb_ref = refs[2 * n:]
        acc = _dot(p_refs[0][...], w_refs[0][0])
        for k in range(1, n):
            acc = acc + _dot(p_refs[k][...], w_refs[k][0])
        if rscale != 1.0:
            acc = rscale * acc
        z = ALPHA * x_ref[...] + acc
        mu = jnp.mean(z, axis=-1, keepdims=True)
        zc = z - mu
        var = jnp.mean(zc * zc, axis=-1, keepdims=True)
        y = zc * lax.rsqrt(var + LN_EPS) * g_ref[...] + b_ref[...]
        z_ref[...] = z
        y_ref[...] = y
        yb_ref[...] = y.astype(bf16)

    row = lambda i: (i, 0)
    in_specs = [pl.BlockSpec((tm, p.shape[1]), row) for p in parts]
    in_specs += [pl.BlockSpec((1,) + w.shape[1:], functools.partial(lambda li, i: (li, 0, 0), li)) for w, li in zip(ws, wl)]
    in_specs += [pl.BlockSpec((tm, D_MODEL), row), pl.BlockSpec((1, D_MODEL), lambda i: (0, 0)),
                 pl.BlockSpec((1, D_MODEL), lambda i: (0, 0))]
    return pl.pallas_call(
        body, name=name, grid=(S // tm,), in_specs=in_specs,
        out_specs=[pl.BlockSpec((tm, D_MODEL), row)] * 3,
        out_shape=[SDS((S, D_MODEL), f32), SDS((S, D_MODEL), f32), SDS((S, D_MODEL), bf16)],
        compiler_params=_cp("parallel"),
    )(*parts, *ws, x, g, b)


def ln_bwd(name, dy, z, g, rscale):
    S = dy.shape[0]
    tm = _tile(S, 512)

    def body(dy_ref, z_ref, g_ref, dzb_ref, dres_ref, dg_ref, db_ref):
        z = z_ref[...]
        mu = jnp.mean(z, axis=-1, keepdims=True)
        zc = z - mu
        rstd = lax.rsqrt(jnp.mean(zc * zc, axis=-1, keepdims=True) + LN_EPS)
        xhat = zc * rstd
        dyv = dy_ref[...]
        dxh = dyv * g_ref[...]
        m1 = jnp.mean(dxh, axis=-1, keepdims=True)
        m2 = jnp.mean(dxh * xhat, axis=-1, keepdims=True)
        dz = rstd * (dxh - m1 - xhat * m2)
        dzb_ref[...] = (rscale * dz).astype(bf16)
        dres_ref[...] = ALPHA * dz

        @pl.when(pl.program_id(0) == 0)
        def _():
            dg_ref[...] = jnp.zeros_like(dg_ref)
            db_ref[...] = jnp.zeros_like(db_ref)

        dg_ref[...] += jnp.sum(dyv * xhat, axis=0, keepdims=True)
        db_ref[...] += jnp.sum(dyv, axis=0, keepdims=True)

    row = lambda i: (i, 0)
    vec = pl.BlockSpec((1, D_MODEL), lambda i: (0, 0))
    return pl.pallas_call(
        body, name=name, grid=(S // tm,),
        in_specs=[pl.BlockSpec((tm, D_MODEL), row), pl.BlockSpec((tm, D_MODEL), row), vec],
        out_specs=[pl.BlockSpec((tm, D_MODEL), row), pl.BlockSpec((tm, D_MODEL), row), vec, vec],
        out_shape=[SDS((S, D_MODEL), bf16), SDS((S, D_MODEL), f32), SDS((1, D_MODEL), f32), SDS((1, D_MODEL), f32)],
        compiler_params=_cp("arbitrary"),
    )(dy, z, g)


def ffn_bwd_da(name, drb, w2, l, gate, up):
    S = drb.shape[0]
    tm = _tile(S, 256)
    nh = D_FF // 2

    def body(dr_ref, w_ref, g_ref, u_ref, dh_ref):
        dr = dr_ref[...]
        for j in range(2):
            cols = slice(j * nh, (j + 1) * nh)
            da = _dot_nt(dr, w_ref[0, cols, :])
            g = g_ref[:, cols].astype(f32)
            u = u_ref[:, cols].astype(f32)
            sg = jax.nn.sigmoid(g)
            dh_ref[:, cols] = (da * u * (sg * (1.0 + g * (1.0 - sg)))).astype(bf16)
            dh_ref[:, D_FF + j * nh:D_FF + (j + 1) * nh] = (da * (g * sg)).astype(bf16)

    row = lambda i: (i, 0)
    return pl.pallas_call(
        body, name=name, grid=(S // tm,),
        in_specs=[pl.BlockSpec((tm, D_MODEL), row), pl.BlockSpec((1, D_FF, D_MODEL), lambda i: (l, 0, 0)),
                  pl.BlockSpec((tm, D_FF), row), pl.BlockSpec((tm, D_FF), row)],
        out_specs=pl.BlockSpec((tm, 2 * D_FF), row),
        out_shape=SDS((S, 2 * D_FF), bf16),
        compiler_params=_cp("parallel"),
    )(drb, w2, gate, up)


def ffn_dx(name, dh, w13, l, res):
    S = dh.shape[0]
    ns = w13.shape[3]
    tm = _tile(S, 1024)

    def body(dh_ref, w_ref, r_ref, o_ref):
        @pl.when(pl.program_id(1) == 0)
        def _():
            o_ref[...] = r_ref[...]

        o_ref[...] += _dot_nt(dh_ref[...], w_ref[0, 0])

    return pl.pallas_call(
        body, name=name, grid=(S // tm, N_CHIPS),
        in_specs=[pl.BlockSpec((tm, ns), lambda i, j: (i, j)),
                  pl.BlockSpec((1, 1, D_MODEL, ns), lambda i, j: (l, j, 0, 0)),
                  pl.BlockSpec((tm, D_MODEL), lambda i, j: (i, 0))],
        out_specs=pl.BlockSpec((tm, D_MODEL), lambda i, j: (i, 0)),
        out_shape=SDS((S, D_MODEL), f32),
        compiler_params=_cp("parallel", "arbitrary"),
    )(dh, w13, res)


def mm_nt_res(name, dys, ws, wl, res, out_dtype):
    S = dys[0].shape[0]
    K = ws[0].shape[1]
    tm = _tile(S, 512)
    n = len(dys)

    def body(*refs):
        dy_refs, w_refs = refs[:n], refs[n:2 * n]
        o_ref = refs[-1]
        acc = _dot_nt(dy_refs[0][...], w_refs[0][0])
        for k in range(1, n):
            acc = acc + _dot_nt(dy_refs[k][...], w_refs[k][0])
        if res is not None:
            acc = acc + refs[2 * n][...]
        o_ref[...] = acc.astype(out_dtype)

    row = lambda i: (i, 0)
    in_specs = [pl.BlockSpec((tm, d.shape[1]), row) for d in dys]
    in_specs += [pl.BlockSpec((1,) + w.shape[1:], functools.partial(lambda li, i: (li, 0, 0), li)) for w, li in zip(ws, wl)]
    args = list(dys) + list(ws)
    if res is not None:
        in_specs.append(pl.BlockSpec((tm, K), row))
        args.append(res)
    return pl.pallas_call(
        body, name=name, grid=(S // tm,), in_specs=in_specs,
        out_specs=pl.BlockSpec((tm, K), row), out_shape=SDS((S, K), out_dtype),
        compiler_params=_cp("parallel"),
    )(*args)


def mm_tn(name, x, dy, mode="plain", layer=0, prev=None):
    S, K = x.shape
    N = dy.shape[1]
    ts = _tile(S, 512)
    if mode == "shard":
        tn = N // N_CHIPS
    else:
        tn = N
        while K * tn * 4 > 6 * 2**20 and tn % 256 == 0:
            tn //= 2
    nn = N // tn
    lead = {"plain": (), "nat": (0,), "shard": (0, 0)}[mode] + (slice(None), slice(None))

    def body(x_ref, dy_ref, *rest):
        o_ref = rest[-1]
        acc = _dot_tn(x_ref[...].astype(bf16), dy_ref[...].astype(bf16))

        @pl.when(pl.program_id(1) == 0)
        def _():
            o_ref[lead] = acc

        @pl.when(pl.program_id(1) != 0)
        def _():
            o_ref[lead] += acc

    in_specs = [pl.BlockSpec((ts, K), lambda n, s: (s, 0)), pl.BlockSpec((ts, tn), lambda n, s: (s, n))]
    args = [x, dy]
    if mode == "plain":
        out_spec = pl.BlockSpec((K, tn), lambda n, s: (0, n))
        out_shape = SDS((K, N), f32)
    elif mode == "nat":
        out_spec = pl.BlockSpec((1, K, tn), lambda n, s: (layer, 0, n))
        out_shape = SDS((DEPTH, K, N), f32)
    else:
        out_spec = pl.BlockSpec((1, 1, K, tn), lambda n, s: (layer, n, 0, 0))
        out_shape = SDS((DEPTH, N_CHIPS, K, tn), f32)
    aliases = {}
    if prev is not None:
        in_specs.append(pl.BlockSpec(memory_space=pl.ANY))
        args.append(prev)
        aliases = {2: 0}
    return pl.pallas_call(
        body, name=name, grid=(nn, S // ts), in_specs=in_specs, out_specs=out_spec, out_shape=out_shape,
        input_output_aliases=aliases, compiler_params=_cp("parallel", "arbitrary"),
    )(*args)


def mm_nn_shard(name, x, w, l):
    S, K = x.shape
    ns = w.shape[3]

    def body(x_ref, w_ref, o_ref):
        o_ref[...] = _dot(x_ref[...], w_ref[0, 0]).astype(bf16)

    return pl.pallas_call(
        body, name=name, grid=(N_CHIPS,),
        in_specs=[pl.BlockSpec((S, K), lambda j: (0, 0)), pl.BlockSpec((1, 1, K, ns), lambda j: (l, j, 0, 0))],
        out_specs=pl.BlockSpec((S, ns), lambda j: (0, j)), out_shape=SDS((S, N_CHIPS * ns), bf16),
        compiler_params=_cp("parallel"),
    )(x, w)


def loss_grad(name, y, t):
    S = y.shape[0]
    tm = _tile(S, 512)

    def body(y_ref, t_ref, dy_ref, loss_ref):
        e = y_ref[...] - t_ref[...]
        dy_ref[...] = e * (1.0 / D_MODEL)

        @pl.when(pl.program_id(0) == 0)
        def _():
            loss_ref[...] = jnp.zeros_like(loss_ref)

        loss_ref[...] += jnp.full(loss_ref.shape, (0.5 / D_MODEL) * jnp.sum(e * e), f32)

    row = lambda i: (i, 0)
    return pl.pallas_call(
        body, name=name, grid=(S // tm,),
        in_specs=[pl.BlockSpec((tm, D_MODEL), row)] * 2,
        out_specs=[pl.BlockSpec((tm, D_MODEL), row), pl.BlockSpec((8, 128), lambda i: (0, 0))],
        out_shape=[SDS((S, D_MODEL), f32), SDS((8, 128), f32)],
        compiler_params=_cp("arbitrary"),
    )(y, t)


def _half_sum(t):
    return t + pltpu.roll(t, 64, axis=1)


def mix_pre(name, xb, w_in, wq, wkv, l, gq, gkv, cs):
    S = xb.shape[0]
    tm = _tile(S, 256)
    H = MLA_HEADS
    W_EXT = w_in.shape[2]

    def body(x_ref, win_ref, wq_ref, wkv_ref, gq_ref, gkv_ref, cs_ref,
             u_ref, cq_ref, ckv_ref, cqn_ref, ckvn_ref, q_ref, k_ref, v_ref):
        h = _dot(x_ref[...], win_ref[0])
        u_ref[...] = h[:, :256]
        cq = h[:, 256:512]
        ckv = h[:, 512:640]
        cq_ref[...] = cq
        ckv_ref[...] = ckv
        cqn = (cq * lax.rsqrt(jnp.mean(cq * cq, axis=-1, keepdims=True) + RMS_EPS) * gq_ref[...]).astype(bf16)
        ckvn = (ckv * lax.rsqrt(jnp.mean(ckv * ckv, axis=-1, keepdims=True) + RMS_EPS) * gkv_ref[...]).astype(bf16)
        cqn_ref[...] = cqn
        ckvn_ref[...] = ckvn
        csv = cs_ref[...]
        lane = lax.broadcasted_iota(jnp.int32, (tm, 128), 1)
        kr = jnp.where(lane < 64, _half_sum(h[:, 640:768] * csv), 0.0).astype(bf16)
        kv = _dot(ckvn, wkv_ref[0])
        for hd in range(H):
            qe = _dot(cqn, wq_ref[0, hd])
            q_ref[hd, :, :128] = qe[:, :128].astype(bf16)
            q_ref[hd, :, 128:] = _half_sum(qe[:, 128:] * csv).astype(bf16)
            k_ref[hd, :, :128] = kv[:, 256 * hd:256 * hd + 128].astype(bf16)
            k_ref[hd, :, 128:] = kr
            v_ref[hd] = kv[:, 256 * hd + 128:256 * hd + 256].astype(bf16)

    row = lambda i: (i, 0)
    hrow = lambda i: (0, i, 0)
    return pl.pallas_call(
        body, name=name, grid=(S // tm,),
        in_specs=[pl.BlockSpec((tm, D_MODEL), row),
                  pl.BlockSpec((1, D_MODEL, W_EXT), lambda i: (l, 0, 0)),
                  pl.BlockSpec((1, H, Q_LORA, 256), lambda i: (l, 0, 0, 0)),
                  pl.BlockSpec((1, KV_LORA, H * 256), lambda i: (l, 0, 0)),
                  pl.BlockSpec((1, Q_LORA), lambda i: (0, 0)), pl.BlockSpec((1, KV_LORA), lambda i: (0, 0)),
                  pl.BlockSpec((tm, 128), row)],
        out_specs=[pl.BlockSpec((tm, 256), row), pl.BlockSpec((tm, Q_LORA), row), pl.BlockSpec((tm, KV_LORA), row),
                   pl.BlockSpec((tm, Q_LORA), row), pl.BlockSpec((tm, KV_LORA), row),
                   pl.BlockSpec((H, tm, 256), hrow), pl.BlockSpec((H, tm, 256), hrow), pl.BlockSpec((H, tm, 128), hrow)],
        out_shape=[SDS((S, 256), f32), SDS((S, Q_LORA), f32), SDS((S, KV_LORA), f32),
                   SDS((S, Q_LORA), bf16), SDS((S, KV_LORA), bf16),
                   SDS((H, S, 256), bf16), SDS((H, S, 256), bf16), SDS((H, S, 128), bf16)],
        compiler_params=_cp("parallel"),
    )(xb, w_in, wq, wkv, gq, gkv, cs)


def _group_select(col, a2, a4, a8, a16):
    return jnp.where(col < 64, a2, jnp.where(col < 128, a4, jnp.where(col < 192, a8, a16)))


def pool_fwd(name, u, wbd, scale):
    S = u.shape[0]
    tm = _tile(S, 512)
    hb = tm // HALO

    def body(u_ref, halo_ref, w_ref, s_ref, d_ref, y_ref):
        i = pl.program_id(0)
        cur = u_ref[...]
        halo = jnp.where(i > 0, halo_ref[...], 0.0)
        ext = jnp.concatenate([halo, cur], axis=0)
        s2 = ext + pltpu.roll(ext, 1, axis=0)
        s4 = s2 + pltpu.roll(s2, 2, axis=0)
        s8 = s4 + pltpu.roll(s4, 4, axis=0)
        s16 = s8 + pltpu.roll(s8, 8, axis=0)
        t1 = (i * tm + 1 + lax.broadcasted_iota(jnp.int32, (tm, 1), 0)).astype(f32)
        col = lax.broadcasted_iota(jnp.int32, (tm, 256), 1)
        m = _group_select(col, s2[HALO:] / jnp.minimum(t1, 2.0), s4[HALO:] / jnp.minimum(t1, 4.0),
                          s8[HALO:] / jnp.minimum(t1, 8.0), s16[HALO:] / jnp.minimum(t1, 16.0))
        d = (m - cur).astype(bf16)
        d_ref[...] = d
        y_ref[...] = (_dot(d, w_ref[...]) * s_ref[...]).astype(bf16)

    row = lambda i: (i, 0)
    return pl.pallas_call(
        body, name=name, grid=(S // tm,),
        in_specs=[pl.BlockSpec((tm, 256), row), pl.BlockSpec((HALO, 256), lambda i: (jnp.maximum(i * hb - 1, 0), 0)),
                  pl.BlockSpec((256, 256), lambda i: (0, 0)), pl.BlockSpec((1, 256), lambda i: (0, 0))],
        out_specs=[pl.BlockSpec((tm, 256), row)] * 2,
        out_shape=[SDS((S, 256), bf16), SDS((S, 256), bf16)],
        compiler_params=_cp("parallel"),
    )(u, u, wbd, scale)


def pool_bwd(name, dyp, d, wbd, scale):
    S = dyp.shape[0]
    tm = _tile(S, 512)
    hb = tm // HALO
    n_ext = tm + HALO

    def fwd_sum(e, steps):
        k = 1
        for _ in range(steps):
            e = e + pltpu.roll(e, n_ext - k, axis=0)
            k *= 2
        return e

    def body(dy_ref, halo_ref, d_ref, w_ref, s_ref, du_ref, dyw_ref, ds_ref):
        i = pl.program_id(0)
        sc = s_ref[...]
        w = w_ref[...]
        cur = dy_ref[...].astype(f32)
        halo = jnp.where(i < pl.num_programs(0) - 1, halo_ref[...].astype(f32), 0.0)
        dyw = jnp.concatenate([cur, halo], axis=0) * sc
        dyw_ref[...] = dyw[:tm].astype(bf16)
        dd = _dot_nt(dyw.astype(bf16), w)
        t1 = (i * tm + 1 + lax.broadcasted_iota(jnp.int32, (n_ext, 1), 0)).astype(f32)
        f2 = fwd_sum(dd / jnp.minimum(t1, 2.0), 1)
        f4 = fwd_sum(dd / jnp.minimum(t1, 4.0), 2)
        f8 = fwd_sum(dd / jnp.minimum(t1, 8.0), 3)
        f16 = fwd_sum(dd / jnp.minimum(t1, 16.0), 4)
        col = lax.broadcasted_iota(jnp.int32, (tm, 256), 1)
        du_ref[...] = (_group_select(col, f2[:tm], f4[:tm], f8[:tm], f16[:tm]) - dd[:tm]).astype(bf16)

        @pl.when(i == 0)
        def _():
            ds_ref[...] = jnp.zeros_like(ds_ref)

        ds_ref[...] += jnp.sum(cur * _dot(d_ref[...], w), axis=0, keepdims=True)

    row = lambda i: (i, 0)
    nhb = S // HALO
    return pl.pallas_call(
        body, name=name, grid=(S // tm,),
        in_specs=[pl.BlockSpec((tm, 256), row), pl.BlockSpec((HALO, 256), lambda i: (jnp.minimum((i + 1) * hb, nhb - 1), 0)),
                  pl.BlockSpec((tm, 256), row), pl.BlockSpec((256, 256), lambda i: (0, 0)),
                  pl.BlockSpec((1, 256), lambda i: (0, 0))],
        out_specs=[pl.BlockSpec((tm, 256), row), pl.BlockSpec((tm, 256), row), pl.BlockSpec((1, 256), lambda i: (0, 0))],
        out_shape=[SDS((S, 256), bf16), SDS((S, 256), bf16), SDS((1, 256), f32)],
        compiler_params=_cp("arbitrary"),
    )(dyp, dyp, d, wbd, scale)


def _chunk_mask(i, j, tq):
    qpos = i * tq + lax.broadcasted_iota(jnp.int32, (tq, tq), 0)
    kpos = j * tq + lax.broadcasted_iota(jnp.int32, (tq, tq), 1)
    return (qpos // 64) >= (kpos // 64)


def mla_attn_fwd(name, q, k, v):
    H, S, _ = q.shape
    tq = _tile(S, 512)
    nq = S // tq

    def body(q_ref, k_ref, v_ref, o_ref, lse_ref, m_sc, l_sc, acc_sc):
        i, j = pl.program_id(1), pl.program_id(2)

        @pl.when(j == 0)
        def _():
            m_sc[...] = jnp.full_like(m_sc, NEG_INF)
            l_sc[...] = jnp.zeros_like(l_sc)
            acc_sc[...] = jnp.zeros_like(acc_sc)

        @pl.when(j <= i)
        def _():
            s = _dot_nt(q_ref[0], k_ref[0]) * MLA_SCALE
            s = jnp.where(jnp.logical_or(j < i, _chunk_mask(i, j, tq)), s, NEG_INF)
            m_prev = m_sc[...]
            m_new = jnp.maximum(m_prev, jnp.max(s, axis=-1, keepdims=True))
            p = jnp.exp(s - m_new[:, :1])
            a = jnp.exp(m_prev - m_new)
            l_sc[...] = a * l_sc[...] + jnp.sum(p, axis=-1, keepdims=True)
            acc_sc[...] = a * acc_sc[...] + _dot(p.astype(bf16), v_ref[0])
            m_sc[...] = m_new

        @pl.when(j == i)
        def _():
            o_ref[...] = (acc_sc[...] / l_sc[...]).astype(bf16)
            lse_ref[0] = m_sc[...] + jnp.log(l_sc[...])

    return pl.pallas_call(
        body, name=name, grid=(H, nq, nq),
        in_specs=[pl.BlockSpec((1, tq, 256), lambda h, i, j: (h, i, 0)),
                  pl.BlockSpec((1, tq, 256), lambda h, i, j: (h, jnp.minimum(i, j), 0)),
                  pl.BlockSpec((1, tq, 128), lambda h, i, j: (h, jnp.minimum(i, j), 0))],
        out_specs=[pl.BlockSpec((tq, 128), lambda h, i, j: (i, h)), pl.BlockSpec((1, tq, 128), lambda h, i, j: (h, i, 0))],
        out_shape=[SDS((S, H * 128), bf16), SDS((H, S, 128), f32)],
        scratch_shapes=[pltpu.VMEM((tq, 128), f32), pltpu.VMEM((tq, 128), f32), pltpu.VMEM((tq, 128), f32)],
        compiler_params=_cp("parallel", "parallel", "arbitrary"),
    )(q, k, v)


def mla_attn_bwd(name, q, k, v, o, do, lse):
    H, S, _ = q.shape
    tq = _tile(S, 512)
    nq = S // tq

    def body(q_ref, k_ref, v_ref, o_ref, do_ref, lse_ref, dq_ref, dk_ref, dv_ref, dk_sc, dv_sc):
        j, i = pl.program_id(1), pl.program_id(2)

        @pl.when(jnp.logical_and(j == 0, i == 0))
        def _():
            dq_ref[...] = jnp.zeros_like(dq_ref)

        @pl.when(i == j)
        def _():
            dk_sc[...] = jnp.zeros_like(dk_sc)
            dv_sc[...] = jnp.zeros_like(dv_sc)

        @pl.when(i >= j)
        def _():
            qv, kv_, dov = q_ref[0], k_ref[0], do_ref[...]
            s = _dot_nt(qv, kv_) * MLA_SCALE
            s = jnp.where(jnp.logical_or(j < i, _chunk_mask(i, j, tq)), s, NEG_INF)
            p = jnp.exp(s - lse_ref[0][:, :1])
            dv_sc[...] += _dot_tn(p.astype(bf16), dov)
            dp = _dot_nt(dov, v_ref[0])
            delta = jnp.sum(dov.astype(f32) * o_ref[...].astype(f32), axis=-1, keepdims=True)
            ds = (p * (dp - delta) * MLA_SCALE).astype(bf16)
            dk_sc[...] += _dot_tn(ds, qv)
            rows = pl.ds(pl.multiple_of(i * tq, tq), tq)
            dq_ref[0, rows, :] += _dot(ds, kv_)

        @pl.when(i == nq - 1)
        def _():
            dk_ref[0] = dk_sc[...]
            dv_ref[0] = dv_sc[...]

    qi = lambda h, j, i: (h, jnp.maximum(i, j), 0)
    kj = lambda h, j, i: (h, j, 0)
    return pl.pallas_call(
        body, name=name, grid=(H, nq, nq),
        in_specs=[pl.BlockSpec((1, tq, 256), qi), pl.BlockSpec((1, tq, 256), kj), pl.BlockSpec((1, tq, 128), kj),
                  pl.BlockSpec((tq, 128), lambda h, j, i: (jnp.maximum(i, j), h)),
                  pl.BlockSpec((tq, 128), lambda h, j, i: (jnp.maximum(i, j), h)),
                  pl.BlockSpec((1, tq, 128), qi)],
        out_specs=[pl.BlockSpec((1, S, 256), lambda h, j, i: (h, 0, 0)), pl.BlockSpec((1, tq, 256), kj),
                   pl.BlockSpec((1, tq, 128), kj)],
        out_shape=[SDS((H, S, 256), f32), SDS((H, S, 256), f32), SDS((H, S, 128), f32)],
        scratch_shapes=[pltpu.VMEM((tq, 256), f32), pltpu.VMEM((tq, 128), f32)],
        compiler_params=_cp("parallel", "arbitrary", "arbitrary"),
    )(q, k, v, o, do, lse)


def mix_post_bwd(name, dq, dk, dv, wq, wkv, l, cq, ckv, gq, gkv, cs):
    H, S, _ = dq.shape
    tm = _tile(S, 256)

    def rms_bwd(dyn, c, g):
        r = lax.rsqrt(jnp.mean(c * c, axis=-1, keepdims=True) + RMS_EPS)
        ch = c * r
        dyg = dyn * g
        dc = r * (dyg - ch * jnp.mean(dyg * ch, axis=-1, keepdims=True))
        return dc, jnp.sum(dyn * ch, axis=0, keepdims=True)

    def body(dq_ref, dk_ref, dv_ref, wq_ref, wkv_ref, cq_ref, ckv_ref, gq_ref, gkv_ref, cs_ref,
             dqe_ref, dkv_ref, dh_ref, dgq_ref, dgkv_ref):
        csv = cs_ref[...]
        lane = lax.broadcasted_iota(jnp.int32, (tm, 128), 1)
        dcqn = jnp.zeros((tm, Q_LORA), f32)
        dkr = jnp.zeros((tm, 128), f32)
        for hd in range(H):
            dqh = dq_ref[hd]
            dqe = jnp.concatenate([dqh[:, :128], _half_sum(dqh[:, 128:]) * csv], axis=1).astype(bf16)
            dqe_ref[:, 256 * hd:256 * hd + 256] = dqe
            dcqn = dcqn + _dot_nt(dqe, wq_ref[0, hd])
            dkh = dk_ref[hd]
            dkv_ref[:, 256 * hd:256 * hd + 128] = dkh[:, :128].astype(bf16)
            dkv_ref[:, 256 * hd + 128:256 * hd + 256] = dv_ref[hd].astype(bf16)
            dkr = dkr + dkh[:, 128:]
        dckvn = _dot_nt(dkv_ref[...], wkv_ref[0])
        dblk = _half_sum(jnp.where(lane < 64, dkr, 0.0)) * csv
        dcq, dgq = rms_bwd(dcqn, cq_ref[...], gq_ref[...])
        dckv, dgkv = rms_bwd(dckvn, ckv_ref[...], gkv_ref[...])
        dh_ref[:, :256] = dcq.astype(bf16)
        dh_ref[:, 256:384] = dckv.astype(bf16)
        dh_ref[:, 384:] = dblk.astype(bf16)

        @pl.when(pl.program_id(0) == 0)
        def _():
            dgq_ref[...] = jnp.zeros_like(dgq_ref)
            dgkv_ref[...] = jnp.zeros_like(dgkv_ref)

        dgq_ref[...] += dgq
        dgkv_ref[...] += dgkv

    row = lambda i: (i, 0)
    hrow = lambda i: (0, i, 0)
    return pl.pallas_call(
        body, name=name, grid=(S // tm,),
        in_specs=[pl.BlockSpec((H, tm, 256), hrow), pl.BlockSpec((H, tm, 256), hrow), pl.BlockSpec((H, tm, 128), hrow),
                  pl.BlockSpec((1, H, Q_LORA, 256), lambda i: (l, 0, 0, 0)),
                  pl.BlockSpec((1, KV_LORA, H * 256), lambda i: (l, 0, 0)),
                  pl.BlockSpec((tm, Q_LORA), row), pl.BlockSpec((tm, KV_LORA), row),
                  pl.BlockSpec((1, Q_LORA), lambda i: (0, 0)), pl.BlockSpec((1, KV_LORA), lambda i: (0, 0)),
                  pl.BlockSpec((tm, 128), row)],
        out_specs=[pl.BlockSpec((tm, H * 256), row), pl.BlockSpec((tm, H * 256), row), pl.BlockSpec((tm, 512), row),
                   pl.BlockSpec((1, Q_LORA), lambda i: (0, 0)), pl.BlockSpec((1, KV_LORA), lambda i: (0, 0))],
        out_shape=[SDS((S, H * 256), bf16), SDS((S, H * 256), bf16), SDS((S, 512), bf16),
                   SDS((1, Q_LORA), f32), SDS((1, KV_LORA), f32)],
        compiler_params=_cp("arbitrary"),
    )(dq, dk, dv, wq, wkv, cq, ckv, gq, gkv, cs)


def _cross_probs(qb, kv_ref, hd):
    cols = slice(hd * MEM_HEAD_DIM, (hd + 1) * MEM_HEAD_DIM)
    s = _dot_nt(qb[:, cols], kv_ref[:, cols]) * MEM_SCALE
    e = jnp.exp(s - jnp.max(s, axis=-1, keepdims=True))
    return e / jnp.sum(e, axis=-1, keepdims=True)


def cross_fwd(name, xb, xf, wq, wo, l, kv, g, b):
    S = xb.shape[0]
    tm = _tile(S, 256)
    M = kv.shape[0]

    def body(x_ref, xf_ref, wq_ref, wo_ref, k_ref, v_ref, g_ref, b_ref, q_ref, o_ref, z_ref, y_ref, yb_ref):
        qb = _dot(x_ref[...], wq_ref[0]).astype(bf16)
        q_ref[...] = qb
        for hd in range(MEM_HEADS):
            cols = slice(hd * MEM_HEAD_DIM, (hd + 1) * MEM_HEAD_DIM)
            p = _cross_probs(qb, k_ref, hd)
            o_ref[:, cols] = _dot(p.astype(bf16), v_ref[:, cols]).astype(bf16)
        z = ALPHA * xf_ref[...] + _dot(o_ref[...], wo_ref[0])
        mu = jnp.mean(z, axis=-1, keepdims=True)
        zc = z - mu
        var = jnp.mean(zc * zc, axis=-1, keepdims=True)
        y = zc * lax.rsqrt(var + LN_EPS) * g_ref[...] + b_ref[...]
        z_ref[...] = z
        y_ref[...] = y
        yb_ref[...] = y.astype(bf16)

    row = lambda i: (i, 0)
    wspec = pl.BlockSpec((1, D_MODEL, D_MODEL), lambda i: (l, 0, 0))
    vec = pl.BlockSpec((1, D_MODEL), lambda i: (0, 0))
    blk = pl.BlockSpec((tm, D_MODEL), row)
    return pl.pallas_call(
        body, name=name, grid=(S // tm,),
        in_specs=[blk, blk, wspec, wspec, pl.BlockSpec((M, D_MODEL), lambda i: (0, 0)),
                  pl.BlockSpec((M, D_MODEL), lambda i: (0, 1)), vec, vec],
        out_specs=[blk] * 5,
        out_shape=[SDS((S, D_MODEL), bf16), SDS((S, D_MODEL), bf16), SDS((S, D_MODEL), f32), SDS((S, D_MODEL), f32),
                   SDS((S, D_MODEL), bf16)],
        compiler_params=_cp("parallel"),
    )(xb, xf, wq, wo, kv, kv, g, b)


def cross_bwd(name, dzb, wo, l, qb, kv):
    S = dzb.shape[0]
    tm = _tile(S, 256)
    M = kv.shape[0]

    def body(dz_ref, wo_ref, q_ref, k_ref, v_ref, dq_ref, dkv_ref):
        @pl.when(pl.program_id(0) == 0)
        def _():
            dkv_ref[...] = jnp.zeros_like(dkv_ref)

        do = _dot_nt(dz_ref[...], wo_ref[0]).astype(bf16)
        qv = q_ref[...]
        for hd in range(MEM_HEADS):
            cols = slice(hd * MEM_HEAD_DIM, (hd + 1) * MEM_HEAD_DIM)
            vcols = slice(D_MODEL + hd * MEM_HEAD_DIM, D_MODEL + (hd + 1) * MEM_HEAD_DIM)
            p = _cross_probs(qv, k_ref, hd)
            doh = do[:, cols]
            dkv_ref[:, vcols] += _dot_tn(p.astype(bf16), doh)
            dp = _dot_nt(doh, v_ref[:, cols])
            ds = (p * (dp - jnp.sum(dp * p, axis=-1, keepdims=True)) * MEM_SCALE).astype(bf16)
            dq_ref[:, cols] = _dot(ds, k_ref[:, cols]).astype(bf16)
            dkv_ref[:, cols] += _dot_tn(ds, qv[:, cols])

    row = lambda i: (i, 0)
    blk = pl.BlockSpec((tm, D_MODEL), row)
    return pl.pallas_call(
        body, name=name, grid=(S // tm,),
        in_specs=[blk, pl.BlockSpec((1, D_MODEL, D_MODEL), lambda i: (l, 0, 0)), blk,
                  pl.BlockSpec((M, D_MODEL), lambda i: (0, 0)), pl.BlockSpec((M, D_MODEL), lambda i: (0, 1))],
        out_specs=[blk, pl.BlockSpec((M, 2 * D_MODEL), lambda i: (0, 0))],
        out_shape=[SDS((S, D_MODEL), bf16), SDS((M, 2 * D_MODEL), f32)],
        compiler_params=_cp("arbitrary"),
    )(dzb, wo, qb, kv, kv)


def adamw(name, w, g, m, v):
    shape = w.shape
    cols = shape[-1]
    rows = math.prod(shape[:-1])
    tr = _row_tile(rows, cols, target=2**20)
    c1 = 1.0 - ADAM_B1 ** ADAM_STEP
    c2 = 1.0 - ADAM_B2 ** ADAM_STEP

    def body(w_ref, g_ref, m_ref, v_ref, d_ref, nm_ref, nv_ref):
        gv = g_ref[...]
        nm = ADAM_B1 * m_ref[...] + (1.0 - ADAM_B1) * gv
        nv = ADAM_B2 * v_ref[...] + (1.0 - ADAM_B2) * (gv * gv)
        d_ref[...] = -ADAM_LR * ((nm / c1) / (jnp.sqrt(nv / c2) + ADAM_EPS) + ADAM_WD * w_ref[...])
        nm_ref[...] = nm
        nv_ref[...] = nv

    blk = pl.BlockSpec((tr, cols), lambda i: (i, 0))
    flat = SDS((rows, cols), f32)
    outs = pl.pallas_call(
        body, name=name, grid=(rows // tr,), in_specs=[blk] * 4, out_specs=[blk] * 3, out_shape=[flat] * 3,
        compiler_params=_cp("parallel"),
    )(*[a.reshape(rows, cols) for a in (w, g, m, v)])
    return [o.reshape(shape) for o in outs]


def _me():
    return lax.axis_index("x"), lax.axis_index("y"), lax.axis_index("c")


def _other_chips(x, y):
    return [(1 - x, y), (x, 1 - y), (1 - x, 1 - y)]


def gather_weights(shards):
    n = len(shards)

    def body(*refs):
        in_refs, out_refs = refs[:n], refs[n:2 * n]
        send_sems, recv_sems, local_sems = refs[2 * n:]
        x, y, c = _me()
        q = 2 * x + y
        sibling = (x, y, 1 - c)
        chips = _other_chips(x, y)
        started = []
        locals_ = []
        for a in range(n):
            for lyr in range(DEPTH):
                cp = pltpu.make_async_copy(in_refs[a].at[lyr], out_refs[a].at[lyr, q], local_sems.at[a * DEPTH + lyr])
                cp.start()
                locals_.append(cp)

        def rcopy(a, k, src, dst, to):
            return pltpu.make_async_remote_copy(src_ref=src, dst_ref=dst, send_sem=send_sems.at[a * 6 + k],
                                                recv_sem=recv_sems.at[a * 6 + k], device_id=to, device_id_type=MESH)

        for a in range(n):
            for k, (cx, cy) in enumerate(chips):
                cp = rcopy(a, k, in_refs[a].at[c], out_refs[a].at[c, q], (cx, cy, c))
                cp.start()
                started.append(cp)
        for a in range(n):
            for k, (cx, cy) in enumerate(chips):
                land = out_refs[a].at[c, 2 * cx + cy]
                rcopy(a, k, land, land, sibling).wait_recv()
                fwd = rcopy(a, 3 + k, land, land, sibling)
                fwd.start()
                started.append(fwd)
        for a in range(n):
            for k, (cx, cy) in enumerate(chips):
                land = out_refs[a].at[1 - c, 2 * cx + cy]
                rcopy(a, 3 + k, land, land, sibling).wait_recv()
        for cp in started:
            cp.wait_send()
        for cp in locals_:
            cp.wait()

    any_spec = pl.BlockSpec(memory_space=pl.ANY)
    return pl.pallas_call(
        body, name="gather_weights",
        in_specs=[any_spec] * n, out_specs=[any_spec] * n,
        out_shape=[SDS((DEPTH, N_CHIPS) + s.shape[1:], s.dtype) for s in shards],
        scratch_shapes=[pltpu.SemaphoreType.DMA((6 * n,)), pltpu.SemaphoreType.DMA((6 * n,)),
                        pltpu.SemaphoreType.DMA((DEPTH * n,))],
    )(*shards)


def pair_send_other_layer(grads):
    n = len(grads)

    def body(*refs):
        in_refs, out_refs = refs[:n], refs[n:2 * n]
        send_sems, recv_sems = refs[2 * n:]
        x, y, c = _me()
        cps = [pltpu.make_async_remote_copy(src_ref=in_refs[a].at[1 - c], dst_ref=out_refs[a], send_sem=send_sems.at[a],
                                            recv_sem=recv_sems.at[a], device_id=(x, y, 1 - c), device_id_type=MESH)
               for a in range(n)]
        for cp in cps:
            cp.start()
        for cp in cps:
            cp.wait()

    any_spec = pl.BlockSpec(memory_space=pl.ANY)
    return pl.pallas_call(
        body, name="pair_send_other_layer", in_specs=[any_spec] * n, out_specs=[any_spec] * n,
        out_shape=[SDS(g.shape[1:], f32) for g in grads],
        scratch_shapes=[pltpu.SemaphoreType.DMA((n,)), pltpu.SemaphoreType.DMA((n,))],
    )(*grads)


def pair_add(name, g, recv, c_arr):
    shard = g.shape[2:]
    cols = shard[-1]
    rows = N_CHIPS * math.prod(shard[:-1])
    tr = _row_tile(rows, cols)

    def body(c_ref, g_ref, r_ref, o_ref):
        o_ref[...] = (g_ref[0] + r_ref[...]).astype(bf16)

    out = pl.pallas_call(
        body, name=name,
        grid_spec=pltpu.PrefetchScalarGridSpec(
            num_scalar_prefetch=1, grid=(rows // tr,),
            in_specs=[pl.BlockSpec((1, tr, cols), lambda i, c_ref: (c_ref[0], i, 0)),
                      pl.BlockSpec((tr, cols), lambda i, c_ref: (i, 0))],
            out_specs=pl.BlockSpec((tr, cols), lambda i, c_ref: (i, 0))),
        out_shape=SDS((rows, cols), bf16), compiler_params=_cp("parallel"),
    )(c_arr, g.reshape(DEPTH, rows, cols), recv.reshape(rows, cols))
    return out.reshape((N_CHIPS,) + shard)


def chip_exchange(psums):
    n = len(psums)

    def body(*refs):
        in_refs, out_refs = refs[:n], refs[n:2 * n]
        send_sems, recv_sems = refs[2 * n:]
        x, y, c = _me()
        chips = _other_chips(x, y)
        cps = [pltpu.make_async_remote_copy(src_ref=in_refs[a].at[2 * cx + cy], dst_ref=out_refs[a].at[k],
                                            send_sem=send_sems.at[3 * a + k], recv_sem=recv_sems.at[3 * a + k],
                                            device_id=(cx, cy, c), device_id_type=MESH)
               for a in range(n) for k, (cx, cy) in enumerate(chips)]
        for cp in cps:
            cp.start()
        for cp in cps:
            cp.wait()

    any_spec = pl.BlockSpec(memory_space=pl.ANY)
    return pl.pallas_call(
        body, name="chip_exchange", in_specs=[any_spec] * n, out_specs=[any_spec] * n,
        out_shape=[SDS((3,) + p.shape[1:], bf16) for p in psums],
        scratch_shapes=[pltpu.SemaphoreType.DMA((3 * n,)), pltpu.SemaphoreType.DMA((3 * n,))],
    )(*psums)


def chip_add(name, psum, recv, q_arr):
    shard = psum.shape[1:]
    cols = shard[-1]
    rows = math.prod(shard[:-1])
    tr = _row_tile(rows, cols)

    def body(q_ref, p_ref, r_ref, o_ref):
        o_ref[...] = ((p_ref[0].astype(f32) + r_ref[0].astype(f32)) + r_ref[1].astype(f32)) + r_ref[2].astype(f32)

    out = pl.pallas_call(
        body, name=name,
        grid_spec=pltpu.PrefetchScalarGridSpec(
            num_scalar_prefetch=1, grid=(rows // tr,),
            in_specs=[pl.BlockSpec((1, tr, cols), lambda i, q_ref: (q_ref[0], i, 0)),
                      pl.BlockSpec((3, tr, cols), lambda i, q_ref: (0, i, 0))],
            out_specs=pl.BlockSpec((tr, cols), lambda i, q_ref: (i, 0))),
        out_shape=SDS((rows, cols), f32), compiler_params=_cp("parallel"),
    )(q_arr, psum.reshape(N_CHIPS, rows, cols), recv.reshape(3, rows, cols))
    return out.reshape(shard)


def pair_share(sums):
    n = len(sums)

    def body(*refs):
        in_refs, out_refs = refs[:n], refs[n:2 * n]
        send_sems, recv_sems, local_sems = refs[2 * n:]
        x, y, c = _me()
        locals_ = [pltpu.make_async_copy(in_refs[a], out_refs[a].at[c], local_sems.at[a]) for a in range(n)]
        cps = [pltpu.make_async_remote_copy(src_ref=in_refs[a], dst_ref=out_refs[a].at[c], send_sem=send_sems.at[a],
                                            recv_sem=recv_sems.at[a], device_id=(x, y, 1 - c), device_id_type=MESH)
               for a in range(n)]
        for cp in locals_ + cps:
            cp.start()
        for cp in cps:
            cp.wait_send()
        for a in range(n):
            land = out_refs[a].at[1 - c]
            pltpu.make_async_remote_copy(src_ref=land, dst_ref=land, send_sem=send_sems.at[a], recv_sem=recv_sems.at[a],
                                         device_id=(x, y, 1 - c), device_id_type=MESH).wait_recv()
        for cp in locals_:
            cp.wait()

    any_spec = pl.BlockSpec(memory_space=pl.ANY)
    return pl.pallas_call(
        body, name="pair_share", in_specs=[any_spec] * n, out_specs=[any_spec] * n,
        out_shape=[SDS((DEPTH,) + s.shape, f32) for s in sums],
        scratch_shapes=[pltpu.SemaphoreType.DMA((n,)), pltpu.SemaphoreType.DMA((n,)), pltpu.SemaphoreType.DMA((n,))],
    )(*sums)


def allsum_small(name, v):
    R = v.shape[0]

    def body(v_ref, o_ref, all_ref, send_sems, recv_sems, local_sem):
        x, y, c = _me()
        me, sibling = (x, y, c), (x, y, 1 - c)
        chips = _other_chips(x, y)

        def rows(px, py, pc):
            return all_ref.at[4 * px + 2 * py + pc]

        def copy(k, block, to, src=None):
            return pltpu.make_async_remote_copy(
                src_ref=rows(*block) if src is None else src, dst_ref=rows(*block),
                send_sem=send_sems.at[k], recv_sem=recv_sems.at[k], device_id=to, device_id_type=MESH)

        mine = pltpu.make_async_copy(v_ref, rows(*me), local_sem)
        mine.start()
        first = [copy(0, me, sibling, src=v_ref)]
        first += [copy(1 + j, me, (*chip, c), src=v_ref) for j, chip in enumerate(chips)]
        for cp in first:
            cp.start()
        passed = [copy(4 + j, (*chip, c), sibling) for j, chip in enumerate(chips)]
        for j, chip in enumerate(chips):
            copy(1 + j, (*chip, c), me).wait_recv()
            passed[j].start()
        copy(0, sibling, me).wait_recv()
        for j, chip in enumerate(chips):
            copy(4 + j, (*chip, 1 - c), me).wait_recv()
        for cp in first + passed:
            cp.wait_send()
        mine.wait()
        acc = all_ref[0]
        for d in range(1, 8):
            acc = acc + all_ref[d]
        o_ref[...] = acc

    return pl.pallas_call(
        body, name=name,
        in_specs=[pl.BlockSpec(memory_space=pltpu.VMEM)], out_specs=pl.BlockSpec(memory_space=pltpu.VMEM),
        out_shape=SDS((R, 128), f32),
        scratch_shapes=[pltpu.VMEM((8, R, 128), f32), pltpu.SemaphoreType.DMA((7,)), pltpu.SemaphoreType.DMA((7,)),
                        pltpu.SemaphoreType.DMA],
        compiler_params=pltpu.CompilerParams(vmem_limit_bytes=V7X_VMEM_LIMIT),
    )(v)


def reduce_grads(grads):
    x, y, c = _me()
    c_arr = jnp.reshape(c, (1,)).astype(jnp.int32)
    q_arr = jnp.reshape(2 * x + y, (1,)).astype(jnp.int32)
    recv = pair_send_other_layer(grads)
    psums = [pair_add(f"pair_add_{a}", g, r, c_arr) for a, (g, r) in enumerate(zip(grads, recv))]
    got = chip_exchange(psums)
    sums = [chip_add(f"chip_add_{a}", p, r, q_arr) for a, (p, r) in enumerate(zip(psums, got))]
    return pair_share(sums)


def _swap_half(r):
    return jnp.concatenate([-r[..., 32:], r[..., :32]], axis=-1)


def _unswap_add(p, qg):
    return p + jnp.concatenate([qg[..., 32:], -qg[..., :32]], axis=-1)


def _block_diag(pw):
    L = pw.shape[0]
    out = jnp.zeros((L, 256, 256), pw.dtype)
    for gi in range(4):
        out = out.at[:, 64 * gi:64 * gi + 64, 64 * gi:64 * gi + 64].set(pw[:, gi])
    return out


def _to_col_shards(w):
    *lead, K, N = w.shape
    nl = len(lead)
    return w.reshape(*lead, K, N_CHIPS, N // N_CHIPS).transpose(*range(nl), nl + 1, nl, nl + 2)


def _from_col_shards(w):
    *lead, C, K, n = w.shape
    nl = len(lead)
    return w.transpose(*range(nl), nl + 1, nl, nl + 2).reshape(*lead, K, C * n)


def kernel(x, mem, positions, ln_g, ln_b, ffn1_w13, ffn1_w2, w_in, pool_w, pool_scale, q_norm_g, w_uq, kv_norm_g, w_ukv, w_out, mem_wq, mem_wkv, mem_wo, ffn2_w13, ffn2_w2, loss_target, m_ln_g, m_ln_b, m_ffn1_w13, m_ffn1_w2, m_w_in, m_pool_w, m_pool_scale, m_q_norm_g, m_w_uq, m_kv_norm_g, m_w_ukv, m_w_out, m_mem_wq, m_mem_wkv, m_mem_wo, m_ffn2_w13, m_ffn2_w2, v_ln_g, v_ln_b, v_ffn1_w13, v_ffn1_w2, v_w_in, v_pool_w, v_pool_scale, v_q_norm_g, v_w_uq, v_kv_norm_g, v_w_ukv, v_w_out, v_mem_wq, v_mem_wkv, v_mem_wo, v_ffn2_w13, v_ffn2_w2):
    L = DEPTH
    S = x.shape[1]
    qx, qy, _ = _me()
    chip = 2 * qx + qy

    big = [ffn1_w13, ffn1_w2, w_in, w_uq, w_ukv, w_out, mem_wq, mem_wkv, mem_wo, ffn2_w13, ffn2_w2]
    (g_f1w13, g_f1w2, g_win, g_wuq, g_wukv, g_wout, g_mwq, g_mwkv, g_mwo, g_f2w13, g_f2w2) = gather_weights(
        [w.astype(bf16) for w in big])
    f1w2 = g_f1w2.reshape(L, D_FF, D_MODEL)
    f2w2 = g_f2w2.reshape(L, D_FF, D_MODEL)
    win = g_win.reshape(L, D_MODEL, D_IN)
    win_ext = jnp.concatenate([win, _swap_half(win[..., D_IN - QK_ROPE:])], axis=-1)
    wuq = _from_col_shards(g_wuq).reshape(L, Q_LORA, MLA_HEADS, QK_NOPE + QK_ROPE)
    wq_ext = jnp.concatenate([wuq, _swap_half(wuq[..., QK_NOPE:])], axis=-1).transpose(0, 2, 1, 3)
    wukv = _from_col_shards(g_wukv)
    wout = g_wout.reshape(L, D_MODEL, D_MODEL)
    wout_pool, wout_mla = wout[:, :POOL_WIDTH], wout[:, POOL_WIDTH:]
    mwq = g_mwq.reshape(L, D_MODEL, D_MODEL)
    mwo = g_mwo.reshape(L, D_MODEL, D_MODEL)
    wbd = _block_diag(pool_w.astype(bf16))

    ln_pad = jnp.zeros((2, L, 4, N_CHIPS, D_MODEL // N_CHIPS), f32)
    ln_pad = lax.dynamic_update_slice(ln_pad, jnp.stack([ln_g, ln_b])[:, :, :, None, :], (0, 0, 0, chip, 0))
    ln_full = allsum_small("allsum_ln", ln_pad.reshape(-1, 128)) * 0.5
    ln_full = ln_full.reshape(2, L, 4, D_MODEL)
    lng, lnb = ln_full[0], ln_full[1]

    half = QK_ROPE // 2
    inv_freq = ROPE_BASE ** (-jnp.arange(half, dtype=f32) / half)
    ang = positions[0].astype(f32)[:, None] * inv_freq
    cos, sin = jnp.cos(ang), jnp.sin(ang)
    cs = jnp.concatenate([cos, cos, sin, sin], axis=-1)

    memb = mem[0].astype(bf16)
    xf = x[0]
    xb = xf.astype(bf16)
    vec = lambda a: a.reshape(1, -1)

    saved = []
    for l in range(L):
        sv = {}
        sv["x0b"] = xb
        gate, up, act = ffn_up(f"ffn1_up_{l}", xb, g_f1w13, l)
        z1, x1f, x1b = proj_res_ln(f"ffn1_down_{l}", [act], [f1w2], [l], xf, vec(lng[l, 0]), vec(lnb[l, 0]), 0.5)
        sv.update(gate1=gate, up1=up, act1=act, z1=z1, x1b=x1b)
        u, cq, ckv, cqn, ckvn, q, k, v = mix_pre(f"mix_pre_{l}", x1b, win_ext, wq_ext, wukv, l,
                                                   vec(q_norm_g[l]), vec(kv_norm_g[l]), cs)
        dpool, ypool = pool_fwd(f"pool_fwd_{l}", u, wbd[l], vec(pool_scale[l]))
        o, lse = mla_attn_fwd(f"mla_fwd_{l}", q, k, v)
        z2, x2f, x2b = proj_res_ln(f"mix_out_{l}", [ypool, o], [wout_pool, wout_mla], [l, l], x1f,
                                   vec(lng[l, 1]), vec(lnb[l, 1]), 1.0)
        sv.update(cq=cq, ckv=ckv, cqn=cqn, ckvn=ckvn, q=q, k=k, v=v, dpool=dpool, ypool=ypool, o=o, lse=lse, z2=z2, x2b=x2b)
        kvm = mm_nn_shard(f"mem_kv_{l}", memb, g_mwkv, l)
        cq_, co_, z3, x3f, x3b = cross_fwd(f"cross_fwd_{l}", x2b, x2f, mwq, mwo, l, kvm, vec(lng[l, 2]), vec(lnb[l, 2]))
        sv.update(kvm=kvm, crq=cq_, cro=co_, z3=z3, x3b=x3b)
        gate, up, act = ffn_up(f"ffn2_up_{l}", x3b, g_f2w13, l)
        z4, xf, xb = proj_res_ln(f"ffn2_down_{l}", [act], [f2w2], [l], x3f, vec(lng[l, 3]), vec(lnb[l, 3]), 0.5)
        sv.update(gate2=gate, up2=up, act2=act, z4=z4)
        saved.append(sv)

    dy, loss_blk = loss_grad("loss_grad", xf, loss_target[0])
    loss = lax.psum(loss_blk[0, 0], ("x", "y", "c"))

    G = dict(f1w13=None, f1w2=None, mwq=None, mwkv=None, mwo=None, f2w13=None, f2w2=None)
    small = {k_: [None] * L for k_ in ("win", "wuq", "wukv", "wout", "pool_w", "pool_scale", "gq", "gkv", "lng", "lnb")}
    for l in reversed(range(L)):
        sv = saved[l]
        dlg, dlb = [None] * 4, [None] * 4
        dzb, dres, dlg[3], dlb[3] = ln_bwd(f"ln4_bwd_{l}", dy, sv["z4"], vec(lng[l, 3]), 0.5)
        dh = ffn_bwd_da(f"ffn2_bwd_da_{l}", dzb, f2w2, l, sv["gate2"], sv["up2"])
        G["f2w2"] = mm_tn(f"ffn2_dw2_{l}", sv["act2"], dzb, "nat", l, G["f2w2"])
        G["f2w13"] = mm_tn(f"ffn2_dw13_{l}", sv["x3b"], dh, "shard", l, G["f2w13"])
        dy = ffn_dx(f"ffn2_dx_{l}", dh, g_f2w13, l, dres)
        dzb, dres, dlg[2], dlb[2] = ln_bwd(f"ln3_bwd_{l}", dy, sv["z3"], vec(lng[l, 2]), 1.0)
        dqc, dkvm = cross_bwd(f"cross_bwd_{l}", dzb, mwo, l, sv["crq"], sv["kvm"])
        G["mwo"] = mm_tn(f"cross_dwo_{l}", sv["cro"], dzb, "nat", l, G["mwo"])
        G["mwq"] = mm_tn(f"cross_dwq_{l}", sv["x2b"], dqc, "nat", l, G["mwq"])
        G["mwkv"] = mm_tn(f"cross_dwkv_{l}", memb, dkvm, "shard", l, G["mwkv"])
        dy = mm_nt_res(f"cross_dx_{l}", [dqc], [mwq], [l], dres, f32)
        dzb, dres, dlg[1], dlb[1] = ln_bwd(f"ln2_bwd_{l}", dy, sv["z2"], vec(lng[l, 1]), 1.0)
        dyp = mm_nt_res(f"mix_dpool_{l}", [dzb], [wout_pool], [l], None, bf16)
        do = mm_nt_res(f"mix_do_{l}", [dzb], [wout_mla], [l], None, bf16)
        dwo_p = mm_tn(f"mix_dwout_pool_{l}", sv["ypool"], dzb)
        dwo_m = mm_tn(f"mix_dwout_mla_{l}", sv["o"], dzb)
        small["wout"][l] = jnp.concatenate([dwo_p, dwo_m], axis=0)
        dq, dk, dv = mla_attn_bwd(f"mla_bwd_{l}", sv["q"], sv["k"], sv["v"], sv["o"], do, sv["lse"])
        dqe, dkv, dh_rest, dgq, dgkv = mix_post_bwd(f"mix_post_bwd_{l}", dq, dk, dv, wq_ext, wukv, l, sv["cq"], sv["ckv"],
                                                     vec(q_norm_g[l]), vec(kv_norm_g[l]), cs)
        du, dyw, dscale = pool_bwd(f"pool_bwd_{l}", dyp, sv["dpool"], wbd[l], vec(pool_scale[l]))
        dwq_e = mm_tn(f"mix_dwuq_{l}", sv["cqn"], dqe).reshape(Q_LORA, MLA_HEADS, 256)
        small["wuq"][l] = jnp.concatenate(
            [dwq_e[..., :QK_NOPE], _unswap_add(dwq_e[..., QK_NOPE:QK_NOPE + QK_ROPE], dwq_e[..., QK_NOPE + QK_ROPE:])],
            axis=-1).reshape(Q_LORA, MLA_HEADS * (QK_NOPE + QK_ROPE))
        small["wukv"][l] = mm_tn(f"mix_dwukv_{l}", sv["ckvn"], dkv)
        dwbd = mm_tn(f"pool_dw_{l}", sv["dpool"], dyw)
        small["pool_w"][l] = jnp.stack([dwbd[64 * gi:64 * gi + 64, 64 * gi:64 * gi + 64] for gi in range(4)])
        small["pool_scale"][l], small["gq"][l], small["gkv"][l] = dscale[0], dgq[0], dgkv[0]
        dh_ext = jnp.concatenate([du, dh_rest], axis=1)
        dwin_e = mm_tn(f"mix_dwin_{l}", sv["x1b"], dh_ext)
        small["win"][l] = jnp.concatenate(
            [dwin_e[:, :D_IN - QK_ROPE], _unswap_add(dwin_e[:, D_IN - QK_ROPE:D_IN], dwin_e[:, D_IN:])], axis=-1)
        dy = mm_nt_res(f"mix_dx_{l}", [dh_ext], [win_ext], [l], dres, f32)
        dzb, dres, dlg[0], dlb[0] = ln_bwd(f"ln1_bwd_{l}", dy, sv["z1"], vec(lng[l, 0]), 0.5)
        dh = ffn_bwd_da(f"ffn1_bwd_da_{l}", dzb, f1w2, l, sv["gate1"], sv["up1"])
        G["f1w2"] = mm_tn(f"ffn1_dw2_{l}", sv["act1"], dzb, "nat", l, G["f1w2"])
        G["f1w13"] = mm_tn(f"ffn1_dw13_{l}", sv["x0b"], dh, "shard", l, G["f1w13"])
        dy = ffn_dx(f"ffn1_dx_{l}", dh, g_f1w13, l, dres)
        small["lng"][l] = jnp.concatenate(dlg, axis=0)
        small["lnb"][l] = jnp.concatenate(dlb, axis=0)
    grad_x = dy[None]

    row_shards = lambda a, K: a.reshape(L, N_CHIPS, K // N_CHIPS, a.shape[-1])
    g_list = [G["f1w13"], row_shards(G["f1w2"], D_FF),
              jnp.stack(small["win"]).reshape(L, N_CHIPS, D_MODEL // N_CHIPS, D_IN),
              _to_col_shards(jnp.stack(small["wuq"])), _to_col_shards(jnp.stack(small["wukv"])),
              jnp.stack(small["wout"]).reshape(L, N_CHIPS, D_MODEL // N_CHIPS, D_MODEL),
              row_shards(G["mwq"], D_MODEL), G["mwkv"], row_shards(G["mwo"], D_MODEL),
              G["f2w13"], row_shards(G["f2w2"], D_FF)]
    big_grads = reduce_grads(g_list)

    rep = [jnp.stack(small["pool_w"]).reshape(-1), jnp.stack(small["pool_scale"]).reshape(-1),
           jnp.stack(small["gq"]).reshape(-1), jnp.stack(small["gkv"]).reshape(-1),
           jnp.stack(small["lng"]).reshape(-1), jnp.stack(small["lnb"]).reshape(-1)]
    sizes = [r.shape[0] for r in rep]
    packed = jnp.concatenate(rep)
    pad = (-packed.shape[0]) % 1024
    tot = allsum_small("allsum_small_grads", jnp.pad(packed, (0, pad)).reshape(-1, 128)).reshape(-1)
    offs = [0]
    for s_ in sizes:
        offs.append(offs[-1] + s_)
    parts = [tot[offs[i]:offs[i + 1]] for i in range(len(sizes))]
    g_pool_w = parts[0].reshape(pool_w.shape)
    g_pool_scale = parts[1].reshape(pool_scale.shape)
    g_gq = parts[2].reshape(q_norm_g.shape)
    g_gkv = parts[3].reshape(kv_norm_g.shape)
    shard_cols = lambda a: lax.dynamic_slice_in_dim(a.reshape(L, 4, D_MODEL), chip * (D_MODEL // N_CHIPS),
                                                    D_MODEL // N_CHIPS, axis=2)
    g_lng, g_lnb = shard_cols(parts[4]), shard_cols(parts[5])

    (gf1w13, gf1w2, gwin, gwuq, gwukv, gwout, gmwq, gmwkv, gmwo, gf2w13, gf2w2) = big_grads
    grads = [g_lng, g_lnb, gf1w13, gf1w2, gwin, g_pool_w, g_pool_scale, g_gq, gwuq, g_gkv, gwukv, gwout, gmwq, gmwkv,
             gmwo, gf2w13, gf2w2]
    ws = [ln_g, ln_b, ffn1_w13, ffn1_w2, w_in, pool_w, pool_scale, q_norm_g, w_uq, kv_norm_g, w_ukv, w_out, mem_wq,
          mem_wkv, mem_wo, ffn2_w13, ffn2_w2]
    ms = [m_ln_g, m_ln_b, m_ffn1_w13, m_ffn1_w2, m_w_in, m_pool_w, m_pool_scale, m_q_norm_g, m_w_uq, m_kv_norm_g, m_w_ukv,
          m_w_out, m_mem_wq, m_mem_wkv, m_mem_wo, m_ffn2_w13, m_ffn2_w2]
    vs = [v_ln_g, v_ln_b, v_ffn1_w13, v_ffn1_w2, v_w_in, v_pool_w, v_pool_scale, v_q_norm_g, v_w_uq, v_kv_norm_g, v_w_ukv,
          v_w_out, v_mem_wq, v_mem_wkv, v_mem_wo, v_ffn2_w13, v_ffn2_w2]
    deltas, new_ms, new_vs = [], [], []
    for a, (w_, g_, m_, v_) in enumerate(zip(ws, grads, ms, vs)):
        d_, nm_, nv_ = adamw(f"adamw_{a}", w_, g_.reshape(w_.shape), m_, v_)
        deltas.append(d_)
        new_ms.append(nm_)
        new_vs.append(nv_)
    grads = [g_.reshape(w_.shape) for g_, w_ in zip(grads, ws)]
    return (loss, grad_x, *grads, *deltas, *new_ms, *new_vs)
```

```python
import functools
import math

import jax
import jax.numpy as jnp
from jax import lax
from jax.experimental import pallas as pl
from jax.experimental.pallas import tpu as pltpu

f32 = jnp.float32
bf16 = jnp.bfloat16
SDS = jax.ShapeDtypeStruct
MESH = pl.DeviceIdType.MESH

D_MODEL = 1024
DEPTH = 2
N_MEM = 256
MEM_HEADS = 4
MEM_HEAD_DIM = D_MODEL // MEM_HEADS
POOL_WINDOWS = (2, 4, 8, 16)
POOL_WIDTH = 256
POOL_GROUP = 64
QK_NOPE = 128
QK_ROPE = 64
V_HEAD = 128
MLA_HEADS = 6
Q_LORA = 256
KV_LORA = 128
ROPE_BASE = 10000.0
D_FF = 2816
D_IN = POOL_WIDTH + Q_LORA + KV_LORA + QK_ROPE
ALPHA = (2 * DEPTH) ** 0.25
LN_EPS = 1e-5
RMS_EPS = 1e-6
NEG_INF = -1e30
MLA_SCALE = (QK_NOPE + QK_ROPE) ** -0.5
MEM_SCALE = MEM_HEAD_DIM ** -0.5
ADAM_LR = 0.001
ADAM_B1 = 0.9
ADAM_B2 = 0.999
ADAM_EPS = 1e-08
ADAM_WD = 0.01
ADAM_STEP = 10

N_CHIPS = 4
V7X_VMEM_LIMIT = 56 * 2**20
HALO = 16

_NT = (((1,), (1,)), ((), ()))
_TN = (((0,), (0,)), ((), ()))


def _dot(a, b):
    return jnp.dot(a, b, preferred_element_type=f32)


def _dot_nt(a, b):
    return lax.dot_general(a, b, _NT, preferred_element_type=f32)


def _dot_tn(a, b):
    return lax.dot_general(a, b, _TN, preferred_element_type=f32)


def _cp(*sem):
    return pltpu.CompilerParams(dimension_semantics=sem if sem else None, vmem_limit_bytes=V7X_VMEM_LIMIT)


def _tile(n, t):
    t = min(n, t)
    assert n % t == 0, (n, t)
    return t


def _row_tile(rows, cols, itemsize=4, target=2 * 2**20):
    best = None
    for t in range(16, rows + 1, 16):
        if rows % t == 0 and t * cols * itemsize <= target:
            best = t
    return best if best is not None else rows


def ffn_up(name, xb, w13, l):
    S = xb.shape[0]
    ns = w13.shape[3]
    tm = _tile(S, 512)

    def body(x_ref, wg_ref, wu_ref, g_ref, u_ref, a_ref):
        x = x_ref[...]
        g = _dot(x, wg_ref[0, 0])
        u = _dot(x, wu_ref[0, 0])
        a = g * jax.nn.sigmoid(g) * u
        g_ref[...] = g.astype(bf16)
        u_ref[...] = u.astype(bf16)
        a_ref[...] = a.astype(bf16)

    out = SDS((S, 2 * ns), bf16)
    return pl.pallas_call(
        body, name=name, grid=(2, S // tm),
        in_specs=[pl.BlockSpec((tm, D_MODEL), lambda j, i: (i, 0)),
                  pl.BlockSpec((1, 1, D_MODEL, ns), lambda j, i: (l, j, 0, 0)),
                  pl.BlockSpec((1, 1, D_MODEL, ns), lambda j, i: (l, j + 2, 0, 0))],
        out_specs=[pl.BlockSpec((tm, ns), lambda j, i: (i, j))] * 3,
        out_shape=[out, out, out],
        compiler_params=_cp("parallel", "parallel"),
    )(xb, w13, w13)


def proj_res_ln(name, parts, ws, wl, x, g, b, rscale):
    S = x.shape[0]
    tm = _tile(S, 256)
    n = len(parts)

    def body(*refs):
        p_refs, w_refs = refs[:n], refs[n:2 * n]
        x_ref, g_ref, b_ref, z_ref, y_ref, yb_ref = refs[2 * n:]
        acc = _dot(p_refs[0][...], w_refs[0][0])
        for k in range(1, n):
            acc = acc + _dot(p_refs[k][...], w_refs[k][0])
        if rscale != 1.0:
            acc = rscale * acc
        z = ALPHA * x_ref[...] + acc
        mu = jnp.mean(z, axis=-1, keepdims=True)
        zc = z - mu
        var = jnp.mean(zc * zc, axis=-1, keepdims=True)
        y = zc * lax.rsqrt(var + LN_EPS) * g_ref[...] + b_ref[...]
        z_ref[...] = z
        y_ref[...] = y
        yb_ref[...] = y.astype(bf16)

    row = lambda i: (i, 0)
    in_specs = [pl.BlockSpec((tm, p.shape[1]), row) for p in parts]
    in_specs += [pl.BlockSpec((1,) + w.shape[1:], functools.partial(lambda li, i: (li, 0, 0), li)) for w, li in zip(ws, wl)]
    in_specs += [pl.BlockSpec((tm, D_MODEL), row), pl.BlockSpec((1, D_MODEL), lambda i: (0, 0)),
                 pl.BlockSpec((1, D_MODEL), lambda i: (0, 0))]
    return pl.pallas_call(
        body, name=name, grid=(S // tm,), in_specs=in_specs,
        out_specs=[pl.BlockSpec((tm, D_MODEL), row)] * 3,
        out_shape=[SDS((S, D_MODEL), f32), SDS((S, D_MODEL), f32), SDS((S, D_MODEL), bf16)],
        compiler_params=_cp("parallel"),
    )(*parts, *ws, x, g, b)


def ln_bwd(name, dy, z, g, rscale):
    S = dy.shape[0]
    tm = _tile(S, 512)

    def body(dy_ref, z_ref, g_ref, dzb_ref, dres_ref, dg_ref, db_ref):
        z = z_ref[...]
        mu = jnp.mean(z, axis=-1, keepdims=True)
        zc = z - mu
        rstd = lax.rsqrt(jnp.mean(zc * zc, axis=-1, keepdims=True) + LN_EPS)
        xhat = zc * rstd
        dyv = dy_ref[...]
        dxh = dyv * g_ref[...]
        m1 = jnp.mean(dxh, axis=-1, keepdims=True)
        m2 = jnp.mean(dxh * xhat, axis=-1, keepdims=True)
        dz = rstd * (dxh - m1 - xhat * m2)
        dzb_ref[...] = (rscale * dz).astype(bf16)
        dres_ref[...] = ALPHA * dz

        @pl.when(pl.program_id(0) == 0)
        def _():
            dg_ref[...] = jnp.zeros_like(dg_ref)
            db_ref[...] = jnp.zeros_like(db_ref)

        dg_ref[...] += jnp.sum(dyv * xhat, axis=0, keepdims=True)
        db_ref[...] += jnp.sum(dyv, axis=0, keepdims=True)

    row = lambda i: (i, 0)
    vec = pl.BlockSpec((1, D_MODEL), lambda i: (0, 0))
    return pl.pallas_call(
        body, name=name, grid=(S // tm,),
        in_specs=[pl.BlockSpec((tm, D_MODEL), row), pl.BlockSpec((tm, D_MODEL), row), vec],
        out_specs=[pl.BlockSpec((tm, D_MODEL), row), pl.BlockSpec((tm, D_MODEL), row), vec, vec],
        out_shape=[SDS((S, D_MODEL), bf16), SDS((S, D_MODEL), f32), SDS((1, D_MODEL), f32), SDS((1, D_MODEL), f32)],
        compiler_params=_cp("arbitrary"),
    )(dy, z, g)


def ffn_bwd_da(name, drb, w2, l, gate, up):
    S = drb.shape[0]
    tm = _tile(S, 256)
    nh = D_FF // 2

    def body(dr_ref, w_ref, g_ref, u_ref, dh_ref):
        dr = dr_ref[...]
        for j in range(2):
            cols = slice(j * nh, (j + 1) * nh)
            da = _dot_nt(dr, w_ref[0, cols, :])
            g = g_ref[:, cols].astype(f32)
            u = u_ref[:, cols].astype(f32)
            sg = jax.nn.sigmoid(g)
            dh_ref[:, cols] = (da * u * (sg * (1.0 + g * (1.0 - sg)))).astype(bf16)
            dh_ref[:, D_FF + j * nh:D_FF + (j + 1) * nh] = (da * (g * sg)).astype(bf16)

    row = lambda i: (i, 0)
    return pl.pallas_call(
        body, name=name, grid=(S // tm,),
        in_specs=[pl.BlockSpec((tm, D_MODEL), row), pl.BlockSpec((1, D_FF, D_MODEL), lambda i: (l, 0, 0)),
                  pl.BlockSpec((tm, D_FF), row), pl.BlockSpec((tm, D_FF), row)],
        out_specs=pl.BlockSpec((tm, 2 * D_FF), row),
        out_shape=SDS((S, 2 * D_FF), bf16),
        compiler_params=_cp("parallel"),
    )(drb, w2, gate, up)


def ffn_dx(name, dh, w13, l, res):
    S = dh.shape[0]
    ns = w13.shape[3]
    tm = _tile(S, 1024)

    def body(dh_ref, w_ref, r_ref, o_ref):
        @pl.when(pl.program_id(1) == 0)
        def _():
            o_ref[...] = r_ref[...]

        o_ref[...] += _dot_nt(dh_ref[...], w_ref[0, 0])

    return pl.pallas_call(
        body, name=name, grid=(S // tm, N_CHIPS),
        in_specs=[pl.BlockSpec((tm, ns), lambda i, j: (i, j)),
                  pl.BlockSpec((1, 1, D_MODEL, ns), lambda i, j: (l, j, 0, 0)),
                  pl.BlockSpec((tm, D_MODEL), lambda i, j: (i, 0))],
        out_specs=pl.BlockSpec((tm, D_MODEL), lambda i, j: (i, 0)),
        out_shape=SDS((S, D_MODEL), f32),
        compiler_params=_cp("parallel", "arbitrary"),
    )(dh, w13, res)


def mm_nt_res(name, dys, ws, wl, res, out_dtype):
    S = dys[0].shape[0]
    K = ws[0].shape[1]
    tm = _tile(S, 512)
    n = len(dys)

    def body(*refs):
        dy_refs, w_refs = refs[:n], refs[n:2 * n]
        o_ref = refs[-1]
        acc = _dot_nt(dy_refs[0][...], w_refs[0][0])
        for k in range(1, n):
            acc = acc + _dot_nt(dy_refs[k][...], w_refs[k][0])
        if res is not None:
            acc = acc + refs[2 * n][...]
        o_ref[...] = acc.astype(out_dtype)

    row = lambda i: (i, 0)
    in_specs = [pl.BlockSpec((tm, d.shape[1]), row) for d in dys]
    in_specs += [pl.BlockSpec((1,) + w.shape[1:], functools.partial(lambda li, i: (li, 0, 0), li)) for w, li in zip(ws, wl)]
    args = list(dys) + list(ws)
    if res is not None:
        in_specs.append(pl.BlockSpec((tm, K), row))
        args.append(res)
    return pl.pallas_call(
        body, name=name, grid=(S // tm,), in_specs=in_specs,
        out_specs=pl.BlockSpec((tm, K), row), out_shape=SDS((S, K), out_dtype),
        compiler_params=_cp("parallel"),
    )(*args)


def mm_tn(name, x, dy, mode="plain", layer=0, prev=None):
    S, K = x.shape
    N = dy.shape[1]
    ts = _tile(S, 512)
    if mode == "shard":
        tn = N // N_CHIPS
    else:
        tn = N
        while K * tn * 4 > 6 * 2**20 and tn % 256 == 0:
            tn //= 2
    nn = N // tn
    lead = {"plain": (), "nat": (0,), "shard": (0, 0)}[mode] + (slice(None), slice(None))

    def body(x_ref, dy_ref, *rest):
        o_ref = rest[-1]
        acc = _dot_tn(x_ref[...].astype(bf16), dy_ref[...].astype(bf16))

        @pl.when(pl.program_id(1) == 0)
        def _():
            o_ref[lead] = acc

        @pl.when(pl.program_id(1) != 0)
        def _():
            o_ref[lead] += acc

    in_specs = [pl.BlockSpec((ts, K), lambda n, s: (s, 0)), pl.BlockSpec((ts, tn), lambda n, s: (s, n))]
    args = [x, dy]
    if mode == "plain":
        out_spec = pl.BlockSpec((K, tn), lambda n, s: (0, n))
        out_shape = SDS((K, N), f32)
    elif mode == "nat":
        out_spec = pl.BlockSpec((1, K, tn), lambda n, s: (layer, 0, n))
        out_shape = SDS((DEPTH, K, N), f32)
    else:
        out_spec = pl.BlockSpec((1, 1, K, tn), lambda n, s: (layer, n, 0, 0))
        out_shape = SDS((DEPTH, N_CHIPS, K, tn), f32)
    aliases = {}
    if prev is not None:
        in_specs.append(pl.BlockSpec(memory_space=pl.ANY))
        args.append(prev)
        aliases = {2: 0}
    return pl.pallas_call(
        body, name=name, grid=(nn, S // ts), in_specs=in_specs, out_specs=out_spec, out_shape=out_shape,
        input_output_aliases=aliases, compiler_params=_cp("parallel", "arbitrary"),
    )(*args)


def mm_nn_shard(name, x, w, l):
    S, K = x.shape
    ns = w.shape[3]

    def body(x_ref, w_ref, o_ref):
        o_ref[...] = _dot(x_ref[...], w_ref[0, 0]).astype(bf16)

    return pl.pallas_call(
        body, name=name, grid=(N_CHIPS,),
        in_specs=[pl.BlockSpec((S, K), lambda j: (0, 0)), pl.BlockSpec((1, 1, K, ns), lambda j: (l, j, 0, 0))],
        out_specs=pl.BlockSpec((S, ns), lambda j: (0, j)), out_shape=SDS((S, N_CHIPS * ns), bf16),
        compiler_params=_cp("parallel"),
    )(x, w)


def loss_grad(name, y, t):
    S = y.shape[0]
    tm = _tile(S, 512)

    def body(y_ref, t_ref, dy_ref, loss_ref):
        e = y_ref[...] - t_ref[...]
        dy_ref[...] = e * (1.0 / D_MODEL)

        @pl.when(pl.program_id(0) == 0)
        def _():
            loss_ref[...] = jnp.zeros_like(loss_ref)

        loss_ref[...] += jnp.full(loss_ref.shape, (0.5 / D_MODEL) * jnp.sum(e * e), f32)

    row = lambda i: (i, 0)
    return pl.pallas_call(
        body, name=name, grid=(S // tm,),
        in_specs=[pl.BlockSpec((tm, D_MODEL), row)] * 2,
        out_specs=[pl.BlockSpec((tm, D_MODEL), row), pl.BlockSpec((8, 128), lambda i: (0, 0))],
        out_shape=[SDS((S, D_MODEL), f32), SDS((8, 128), f32)],
        compiler_params=_cp("arbitrary"),
    )(y, t)


def _half_sum(t):
    return t + pltpu.roll(t, 64, axis=1)


def mix_pre(name, xb, w_in, wq, wkv, l, gq, gkv, cs):
    S = xb.shape[0]
    tm = _tile(S, 256)
    H = MLA_HEADS
    W_EXT = w_in.shape[2]

    def body(x_ref, win_ref, wq_ref, wkv_ref, gq_ref, gkv_ref, cs_ref,
             u_ref, cq_ref, ckv_ref, cqn_ref, ckvn_ref, q_ref, k_ref, v_ref):
        h = _dot(x_ref[...], win_ref[0])
        u_ref[...] = h[:, :256]
        cq = h[:, 256:512]
        ckv = h[:, 512:640]
        cq_ref[...] = cq
        ckv_ref[...] = ckv
        cqn = (cq * lax.rsqrt(jnp.mean(cq * cq, axis=-1, keepdims=True) + RMS_EPS) * gq_ref[...]).astype(bf16)
        ckvn = (ckv * lax.rsqrt(jnp.mean(ckv * ckv, axis=-1, keepdims=True) + RMS_EPS) * gkv_ref[...]).astype(bf16)
        cqn_ref[...] = cqn
        ckvn_ref[...] = ckvn
        csv = cs_ref[...]
        lane = lax.broadcasted_iota(jnp.int32, (tm, 128), 1)
        kr = jnp.where(lane < 64, _half_sum(h[:, 640:768] * csv), 0.0).astype(bf16)
        kv = _dot(ckvn, wkv_ref[0])
        for hd in range(H):
            qe = _dot(cqn, wq_ref[0, hd])
            q_ref[hd, :, :128] = qe[:, :128].astype(bf16)
            q_ref[hd, :, 128:] = _half_sum(qe[:, 128:] * csv).astype(bf16)
            k_ref[hd, :, :128] = kv[:, 256 * hd:256 * hd + 128].astype(bf16)
            k_ref[hd, :, 128:] = kr
            v_ref[hd] = kv[:, 256 * hd + 128:256 * hd + 256].astype(bf16)

    row = lambda i: (i, 0)
    hrow = lambda i: (0, i, 0)
    return pl.pallas_call(
        body, name=name, grid=(S // tm,),
        in_specs=[pl.BlockSpec((tm, D_MODEL), row),
                  pl.BlockSpec((1, D_MODEL, W_EXT), lambda i: (l, 0, 0)),
                  pl.BlockSpec((1, H, Q_LORA, 256), lambda i: (l, 0, 0, 0)),
                  pl.BlockSpec((1, KV_LORA, H * 256), lambda i: (l, 0, 0)),
                  pl.BlockSpec((1, Q_LORA), lambda i: (0, 0)), pl.BlockSpec((1, KV_LORA), lambda i: (0, 0)),
                  pl.BlockSpec((tm, 128), row)],
        out_specs=[pl.BlockSpec((tm, 256), row), pl.BlockSpec((tm, Q_LORA), row), pl.BlockSpec((tm, KV_LORA), row),
                   pl.BlockSpec((tm, Q_LORA), row), pl.BlockSpec((tm, KV_LORA), row),
                   pl.BlockSpec((H, tm, 256), hrow), pl.BlockSpec((H, tm, 256), hrow), pl.BlockSpec((H, tm, 128), hrow)],
        out_shape=[SDS((S, 256), f32), SDS((S, Q_LORA), f32), SDS((S, KV_LORA), f32),
                   SDS((S, Q_LORA), bf16), SDS((S, KV_LORA), bf16),
                   SDS((H, S, 256), bf16), SDS((H, S, 256), bf16), SDS((H, S, 128), bf16)],
        compiler_params=_cp("parallel"),
    )(xb, w_in, wq, wkv, gq, gkv, cs)


def _group_select(col, a2, a4, a8, a16):
    return jnp.where(col < 64, a2, jnp.where(col < 128, a4, jnp.where(col < 192, a8, a16)))


def pool_fwd(name, u, wbd, scale):
    S = u.shape[0]
    tm = _tile(S, 512)
    hb = tm // HALO

    def body(u_ref, halo_ref, w_ref, s_ref, d_ref, y_ref):
        i = pl.program_id(0)
        cur = u_ref[...]
        halo = jnp.where(i > 0, halo_ref[...], 0.0)
        ext = jnp.concatenate([halo, cur], axis=0)
        s2 = ext + pltpu.roll(ext, 1, axis=0)
        s4 = s2 + pltpu.roll(s2, 2, axis=0)
        s8 = s4 + pltpu.roll(s4, 4, axis=0)
        s16 = s8 + pltpu.roll(s8, 8, axis=0)
        t1 = (i * tm + 1 + lax.broadcasted_iota(jnp.int32, (tm, 1), 0)).astype(f32)
        col = lax.broadcasted_iota(jnp.int32, (tm, 256), 1)
        m = _group_select(col, s2[HALO:] / jnp.minimum(t1, 2.0), s4[HALO:] / jnp.minimum(t1, 4.0),
                          s8[HALO:] / jnp.minimum(t1, 8.0), s16[HALO:] / jnp.minimum(t1, 16.0))
        d = (m - cur).astype(bf16)
        d_ref[...] = d
        y_ref[...] = (_dot(d, w_ref[...]) * s_ref[...]).astype(bf16)

    row = lambda i: (i, 0)
    return pl.pallas_call(
        body, name=name, grid=(S // tm,),
        in_specs=[pl.BlockSpec((tm, 256), row), pl.BlockSpec((HALO, 256), lambda i: (jnp.maximum(i * hb - 1, 0), 0)),
                  pl.BlockSpec((256, 256), lambda i: (0, 0)), pl.BlockSpec((1, 256), lambda i: (0, 0))],
        out_specs=[pl.BlockSpec((tm, 256), row)] * 2,
        out_shape=[SDS((S, 256), bf16), SDS((S, 256), bf16)],
        compiler_params=_cp("parallel"),
    )(u, u, wbd, scale)


def pool_bwd(name, dyp, d, wbd, scale):
    S = dyp.shape[0]
    tm = _tile(S, 512)
    hb = tm // HALO
    n_ext = tm + HALO

    def fwd_sum(e, steps):
        k = 1
        for _ in range(steps):
            e = e + pltpu.roll(e, n_ext - k, axis=0)
            k *= 2
        return e

    def body(dy_ref, halo_ref, d_ref, w_ref, s_ref, du_ref, dyw_ref, ds_ref):
        i = pl.program_id(0)
        sc = s_ref[...]
        w = w_ref[...]
        cur = dy_ref[...].astype(f32)
        halo = jnp.where(i < pl.num_programs(0) - 1, halo_ref[...].astype(f32), 0.0)
        dyw = jnp.concatenate([cur, halo], axis=0) * sc
        dyw_ref[...] = dyw[:tm].astype(bf16)
        dd = _dot_nt(dyw.astype(bf16), w)
        t1 = (i * tm + 1 + lax.broadcasted_iota(jnp.int32, (n_ext, 1), 0)).astype(f32)
        f2 = fwd_sum(dd / jnp.minimum(t1, 2.0), 1)
        f4 = fwd_sum(dd / jnp.minimum(t1, 4.0), 2)
        f8 = fwd_sum(dd / jnp.minimum(t1, 8.0), 3)
        f16 = fwd_sum(dd / jnp.minimum(t1, 16.0), 4)
        col = lax.broadcasted_iota(jnp.int32, (tm, 256), 1)
        du_ref[...] = (_group_select(col, f2[:tm], f4[:tm], f8[:tm], f16[:tm]) - dd[:tm]).astype(bf16)

        @pl.when(i == 0)
        def _():
            ds_ref[...] = jnp.zeros_like(ds_ref)

        ds_ref[...] += jnp.sum(cur * _dot(d_ref[...], w), axis=0, keepdims=True)

    row = lambda i: (i, 0)
    nhb = S // HALO
    return pl.pallas_call(
        body, name=name, grid=(S // tm,),
        in_specs=[pl.BlockSpec((tm, 256), row), pl.BlockSpec((HALO, 256), lambda i: (jnp.minimum((i + 1) * hb, nhb - 1), 0)),
                  pl.BlockSpec((tm, 256), row), pl.BlockSpec((256, 256), lambda i: (0, 0)),
                  pl.BlockSpec((1, 256), lambda i: (0, 0))],
        out_specs=[pl.BlockSpec((tm, 256), row), pl.BlockSpec((tm, 256), row), pl.BlockSpec((1, 256), lambda i: (0, 0))],
        out_shape=[SDS((S, 256), bf16), SDS((S, 256), bf16), SDS((1, 256), f32)],
        compiler_params=_cp("arbitrary"),
    )(dyp, dyp, d, wbd, scale)


def _diag_mask(tq):
    rc = lax.broadcasted_iota(jnp.int32, (tq, 1), 0) // 64
    cc = lax.broadcasted_iota(jnp.int32, (1, tq), 1) // 64
    return rc >= cc


MLA_SCALE_LOG2 = MLA_SCALE * math.log2(math.e)


def mla_attn_fwd(name, q, k, v):
    H, S, _ = q.shape
    tq = _tile(S, 512)
    nq = S // tq

    def body(q_ref, k_ref, v_ref, o_ref, lse_ref, m_sc, l_sc, acc_sc):
        i, j = pl.program_id(1), pl.program_id(2)

        @pl.when(j == 0)
        def _():
            m_sc[...] = jnp.full_like(m_sc, NEG_INF)
            l_sc[...] = jnp.zeros_like(l_sc)
            acc_sc[...] = jnp.zeros_like(acc_sc)

        def step(masked):
            s = _dot_nt(q_ref[0], k_ref[0])
            if masked:
                s = jnp.where(_diag_mask(tq), s, NEG_INF)
            m_prev = m_sc[...]
            m_new = jnp.maximum(m_prev, jnp.max(s, axis=-1, keepdims=True))
            p = jnp.exp2((s - m_new[:, :1]) * MLA_SCALE_LOG2)
            a = jnp.exp2((m_prev - m_new) * MLA_SCALE_LOG2)
            l_sc[...] = a * l_sc[...] + jnp.sum(p, axis=-1, keepdims=True)
            acc_sc[...] = a * acc_sc[...] + _dot(p.astype(bf16), v_ref[0])
            m_sc[...] = m_new

        @pl.when(j < i)
        def _():
            step(False)

        @pl.when(j == i)
        def _():
            step(True)
            o_ref[...] = (acc_sc[...] / l_sc[...]).astype(bf16)
            lse_ref[0] = m_sc[...] * MLA_SCALE_LOG2 + jnp.log2(l_sc[...])

    return pl.pallas_call(
        body, name=name, grid=(H, nq, nq),
        in_specs=[pl.BlockSpec((1, tq, 256), lambda h, i, j: (h, i, 0)),
                  pl.BlockSpec((1, tq, 256), lambda h, i, j: (h, jnp.minimum(i, j), 0)),
                  pl.BlockSpec((1, tq, 128), lambda h, i, j: (h, jnp.minimum(i, j), 0))],
        out_specs=[pl.BlockSpec((tq, 128), lambda h, i, j: (i, h)), pl.BlockSpec((1, tq, 128), lambda h, i, j: (h, i, 0))],
        out_shape=[SDS((S, H * 128), bf16), SDS((H, S, 128), f32)],
        scratch_shapes=[pltpu.VMEM((tq, 128), f32), pltpu.VMEM((tq, 128), f32), pltpu.VMEM((tq, 128), f32)],
        compiler_params=_cp("parallel", "parallel", "arbitrary"),
    )(q, k, v)


def mla_attn_bwd(name, q, k, v, o, do, lse):
    H, S, _ = q.shape
    tq = _tile(S, 512)
    nq = S // tq

    def body(q_ref, k_ref, v_ref, o_ref, do_ref, lse_ref, dq_ref, dk_ref, dv_ref, dk_sc, dv_sc):
        j, i = pl.program_id(1), pl.program_id(2)

        @pl.when(jnp.logical_and(j == 0, i == 0))
        def _():
            dq_ref[...] = jnp.zeros_like(dq_ref)

        @pl.when(i == j)
        def _():
            dk_sc[...] = jnp.zeros_like(dk_sc)
            dv_sc[...] = jnp.zeros_like(dv_sc)

        def step(masked):
            qv, kv_, dov = q_ref[0], k_ref[0], do_ref[...]
            s = _dot_nt(qv, kv_)
            if masked:
                s = jnp.where(_diag_mask(tq), s, NEG_INF)
            p = jnp.exp2(s * MLA_SCALE_LOG2 - lse_ref[0][:, :1])
            dv_sc[...] += _dot_tn(p.astype(bf16), dov)
            dp = _dot_nt(dov, v_ref[0])
            delta = jnp.sum(dov.astype(f32) * o_ref[...].astype(f32), axis=-1, keepdims=True)
            ds = (p * (dp - delta)).astype(bf16)
            dk_sc[...] += _dot_tn(ds, qv)
            rows = pl.ds(pl.multiple_of(i * tq, tq), tq)
            dq_ref[0, rows, :] += _dot(ds, kv_)

        @pl.when(i > j)
        def _():
            step(False)

        @pl.when(i == j)
        def _():
            step(True)

        @pl.when(i == nq - 1)
        def _():
            dk_ref[0] = dk_sc[...] * MLA_SCALE
            dv_ref[0] = dv_sc[...]

        @pl.when(jnp.logical_and(j == nq - 1, i == nq - 1))
        def _():
            dq_ref[...] = dq_ref[...] * MLA_SCALE

    qi = lambda h, j, i: (h, jnp.maximum(i, j), 0)
    kj = lambda h, j, i: (h, j, 0)
    return pl.pallas_call(
        body, name=name, grid=(H, nq, nq),
        in_specs=[pl.BlockSpec((1, tq, 256), qi), pl.BlockSpec((1, tq, 256), kj), pl.BlockSpec((1, tq, 128), kj),
                  pl.BlockSpec((tq, 128), lambda h, j, i: (jnp.maximum(i, j), h)),
                  pl.BlockSpec((tq, 128), lambda h, j, i: (jnp.maximum(i, j), h)),
                  pl.BlockSpec((1, tq, 128), qi)],
        out_specs=[pl.BlockSpec((1, S, 256), lambda h, j, i: (h, 0, 0)), pl.BlockSpec((1, tq, 256), kj),
                   pl.BlockSpec((1, tq, 128), kj)],
        out_shape=[SDS((H, S, 256), f32), SDS((H, S, 256), f32), SDS((H, S, 128), f32)],
        scratch_shapes=[pltpu.VMEM((tq, 256), f32), pltpu.VMEM((tq, 128), f32)],
        compiler_params=_cp("parallel", "arbitrary", "arbitrary"),
    )(q, k, v, o, do, lse)


def mix_post_bwd(name, dq, dk, dv, wq, wkv, l, cq, ckv, gq, gkv, cs):
    H, S, _ = dq.shape
    tm = _tile(S, 256)

    def rms_bwd(dyn, c, g):
        r = lax.rsqrt(jnp.mean(c * c, axis=-1, keepdims=True) + RMS_EPS)
        ch = c * r
        dyg = dyn * g
        dc = r * (dyg - ch * jnp.mean(dyg * ch, axis=-1, keepdims=True))
        return dc, jnp.sum(dyn * ch, axis=0, keepdims=True)

    def body(dq_ref, dk_ref, dv_ref, wq_ref, wkv_ref, cq_ref, ckv_ref, gq_ref, gkv_ref, cs_ref,
             dqe_ref, dkv_ref, dh_ref, dgq_ref, dgkv_ref):
        csv = cs_ref[...]
        lane = lax.broadcasted_iota(jnp.int32, (tm, 128), 1)
        dcqn = jnp.zeros((tm, Q_LORA), f32)
        dkr = jnp.zeros((tm, 128), f32)
        for hd in range(H):
            dqh = dq_ref[hd]
            dqe = jnp.concatenate([dqh[:, :128], _half_sum(dqh[:, 128:]) * csv], axis=1).astype(bf16)
            dqe_ref[:, 256 * hd:256 * hd + 256] = dqe
            dcqn = dcqn + _dot_nt(dqe, wq_ref[0, hd])
            dkh = dk_ref[hd]
            dkv_ref[:, 256 * hd:256 * hd + 128] = dkh[:, :128].astype(bf16)
            dkv_ref[:, 256 * hd + 128:256 * hd + 256] = dv_ref[hd].astype(bf16)
            dkr = dkr + dkh[:, 128:]
        dckvn = _dot_nt(dkv_ref[...], wkv_ref[0])
        dblk = _half_sum(jnp.where(lane < 64, dkr, 0.0)) * csv
        dcq, dgq = rms_bwd(dcqn, cq_ref[...], gq_ref[...])
        dckv, dgkv = rms_bwd(dckvn, ckv_ref[...], gkv_ref[...])
        dh_ref[:, :256] = dcq.astype(bf16)
        dh_ref[:, 256:384] = dckv.astype(bf16)
        dh_ref[:, 384:] = dblk.astype(bf16)

        @pl.when(pl.program_id(0) == 0)
        def _():
            dgq_ref[...] = jnp.zeros_like(dgq_ref)
            dgkv_ref[...] = jnp.zeros_like(dgkv_ref)

        dgq_ref[...] += dgq
        dgkv_ref[...] += dgkv

    row = lambda i: (i, 0)
    hrow = lambda i: (0, i, 0)
    return pl.pallas_call(
        body, name=name, grid=(S // tm,),
        in_specs=[pl.BlockSpec((H, tm, 256), hrow), pl.BlockSpec((H, tm, 256), hrow), pl.BlockSpec((H, tm, 128), hrow),
                  pl.BlockSpec((1, H, Q_LORA, 256), lambda i: (l, 0, 0, 0)),
                  pl.BlockSpec((1, KV_LORA, H * 256), lambda i: (l, 0, 0)),
                  pl.BlockSpec((tm, Q_LORA), row), pl.BlockSpec((tm, KV_LORA), row),
                  pl.BlockSpec((1, Q_LORA), lambda i: (0, 0)), pl.BlockSpec((1, KV_LORA), lambda i: (0, 0)),
                  pl.BlockSpec((tm, 128), row)],
        out_specs=[pl.BlockSpec((tm, H * 256), row), pl.BlockSpec((tm, H * 256), row), pl.BlockSpec((tm, 512), row),
                   pl.BlockSpec((1, Q_LORA), lambda i: (0, 0)), pl.BlockSpec((1, KV_LORA), lambda i: (0, 0))],
        out_shape=[SDS((S, H * 256), bf16), SDS((S, H * 256), bf16), SDS((S, 512), bf16),
                   SDS((1, Q_LORA), f32), SDS((1, KV_LORA), f32)],
        compiler_params=_cp("arbitrary"),
    )(dq, dk, dv, wq, wkv, cq, ckv, gq, gkv, cs)


def _cross_probs(qb, kv_ref, hd):
    cols = slice(hd * MEM_HEAD_DIM, (hd + 1) * MEM_HEAD_DIM)
    s = _dot_nt(qb[:, cols], kv_ref[:, cols]) * MEM_SCALE
    e = jnp.exp(s - jnp.max(s, axis=-1, keepdims=True))
    return e / jnp.sum(e, axis=-1, keepdims=True)


def cross_fwd(name, xb, xf, wq, wo, l, kv, g, b):
    S = xb.shape[0]
    tm = _tile(S, 256)
    M = kv.shape[0]

    def body(x_ref, xf_ref, wq_ref, wo_ref, k_ref, v_ref, g_ref, b_ref, q_ref, o_ref, z_ref, y_ref, yb_ref):
        qb = _dot(x_ref[...], wq_ref[0]).astype(bf16)
        q_ref[...] = qb
        for hd in range(MEM_HEADS):
            cols = slice(hd * MEM_HEAD_DIM, (hd + 1) * MEM_HEAD_DIM)
            p = _cross_probs(qb, k_ref, hd)
            o_ref[:, cols] = _dot(p.astype(bf16), v_ref[:, cols]).astype(bf16)
        z = ALPHA * xf_ref[...] + _dot(o_ref[...], wo_ref[0])
        mu = jnp.mean(z, axis=-1, keepdims=True)
        zc = z - mu
        var = jnp.mean(zc * zc, axis=-1, keepdims=True)
        y = zc * lax.rsqrt(var + LN_EPS) * g_ref[...] + b_ref[...]
        z_ref[...] = z
        y_ref[...] = y
        yb_ref[...] = y.astype(bf16)

    row = lambda i: (i, 0)
    wspec = pl.BlockSpec((1, D_MODEL, D_MODEL), lambda i: (l, 0, 0))
    vec = pl.BlockSpec((1, D_MODEL), lambda i: (0, 0))
    blk = pl.BlockSpec((tm, D_MODEL), row)
    return pl.pallas_call(
        body, name=name, grid=(S // tm,),
        in_specs=[blk, blk, wspec, wspec, pl.BlockSpec((M, D_MODEL), lambda i: (0, 0)),
                  pl.BlockSpec((M, D_MODEL), lambda i: (0, 1)), vec, vec],
        out_specs=[blk] * 5,
        out_shape=[SDS((S, D_MODEL), bf16), SDS((S, D_MODEL), bf16), SDS((S, D_MODEL), f32), SDS((S, D_MODEL), f32),
                   SDS((S, D_MODEL), bf16)],
        compiler_params=_cp("parallel"),
    )(xb, xf, wq, wo, kv, kv, g, b)


def cross_bwd(name, dzb, wo, l, qb, kv):
    S = dzb.shape[0]
    tm = _tile(S, 256)
    M = kv.shape[0]

    def body(dz_ref, wo_ref, q_ref, k_ref, v_ref, dq_ref, dkv_ref):
        @pl.when(pl.program_id(0) == 0)
        def _():
            dkv_ref[...] = jnp.zeros_like(dkv_ref)

        do = _dot_nt(dz_ref[...], wo_ref[0]).astype(bf16)
        qv = q_ref[...]
        for hd in range(MEM_HEADS):
            cols = slice(hd * MEM_HEAD_DIM, (hd + 1) * MEM_HEAD_DIM)
            vcols = slice(D_MODEL + hd * MEM_HEAD_DIM, D_MODEL + (hd + 1) * MEM_HEAD_DIM)
            p = _cross_probs(qv, k_ref, hd)
            doh = do[:, cols]
            dkv_ref[:, vcols] += _dot_tn(p.astype(bf16), doh)
            dp = _dot_nt(doh, v_ref[:, cols])
            ds = (p * (dp - jnp.sum(dp * p, axis=-1, keepdims=True)) * MEM_SCALE).astype(bf16)
            dq_ref[:, cols] = _dot(ds, k_ref[:, cols]).astype(bf16)
            dkv_ref[:, cols] += _dot_tn(ds, qv[:, cols])

    row = lambda i: (i, 0)
    blk = pl.BlockSpec((tm, D_MODEL), row)
    return pl.pallas_call(
        body, name=name, grid=(S // tm,),
        in_specs=[blk, pl.BlockSpec((1, D_MODEL, D_MODEL), lambda i: (l, 0, 0)), blk,
                  pl.BlockSpec((M, D_MODEL), lambda i: (0, 0)), pl.BlockSpec((M, D_MODEL), lambda i: (0, 1))],
        out_specs=[blk, pl.BlockSpec((M, 2 * D_MODEL), lambda i: (0, 0))],
        out_shape=[SDS((S, D_MODEL), bf16), SDS((M, 2 * D_MODEL), f32)],
        compiler_params=_cp("arbitrary"),
    )(dzb, wo, qb, kv, kv)


def adamw(name, w, g, m, v):
    shape = w.shape
    cols = shape[-1]
    rows = math.prod(shape[:-1])
    tr = _row_tile(rows, cols, target=2**20)
    c1 = 1.0 - ADAM_B1 ** ADAM_STEP
    c2 = 1.0 - ADAM_B2 ** ADAM_STEP

    def body(w_ref, g_ref, m_ref, v_ref, d_ref, nm_ref, nv_ref):
        gv = g_ref[...]
        nm = ADAM_B1 * m_ref[...] + (1.0 - ADAM_B1) * gv
        nv = ADAM_B2 * v_ref[...] + (1.0 - ADAM_B2) * (gv * gv)
        d_ref[...] = -ADAM_LR * ((nm / c1) / (jnp.sqrt(nv / c2) + ADAM_EPS) + ADAM_WD * w_ref[...])
        nm_ref[...] = nm
        nv_ref[...] = nv

    blk = pl.BlockSpec((tr, cols), lambda i: (i, 0))
    flat = SDS((rows, cols), f32)
    outs = pl.pallas_call(
        body, name=name, grid=(rows // tr,), in_specs=[blk] * 4, out_specs=[blk] * 3, out_shape=[flat] * 3,
        compiler_params=_cp("parallel"),
    )(*[a.reshape(rows, cols) for a in (w, g, m, v)])
    return [o.reshape(shape) for o in outs]


def _me():
    return lax.axis_index("x"), lax.axis_index("y"), lax.axis_index("c")


def _other_chips(x, y):
    return [(1 - x, y), (x, 1 - y), (1 - x, 1 - y)]


def gather_weights(shards):
    n = len(shards)

    def body(*refs):
        in_refs, out_refs = refs[:n], refs[n:2 * n]
        send_sems, recv_sems, local_sems = refs[2 * n:]
        x, y, c = _me()
        q = 2 * x + y
        sibling = (x, y, 1 - c)
        chips = _other_chips(x, y)
        started = []
        locals_ = []
        for a in range(n):
            for lyr in range(DEPTH):
                cp = pltpu.make_async_copy(in_refs[a].at[lyr], out_refs[a].at[lyr, q], local_sems.at[a * DEPTH + lyr])
                cp.start()
                locals_.append(cp)

        def rcopy(a, k, src, dst, to):
            return pltpu.make_async_remote_copy(src_ref=src, dst_ref=dst, send_sem=send_sems.at[a * 6 + k],
                                                recv_sem=recv_sems.at[a * 6 + k], device_id=to, device_id_type=MESH)

        for a in range(n):
            for k, (cx, cy) in enumerate(chips):
                cp = rcopy(a, k, in_refs[a].at[c], out_refs[a].at[c, q], (cx, cy, c))
                cp.start()
                started.append(cp)
        for a in range(n):
            for k, (cx, cy) in enumerate(chips):
                land = out_refs[a].at[c, 2 * cx + cy]
                rcopy(a, k, land, land, sibling).wait_recv()
                fwd = rcopy(a, 3 + k, land, land, sibling)
                fwd.start()
                started.append(fwd)
        for a in range(n):
            for k, (cx, cy) in enumerate(chips):
                land = out_refs[a].at[1 - c, 2 * cx + cy]
                rcopy(a, 3 + k, land, land, sibling).wait_recv()
        for cp in started:
            cp.wait_send()
        for cp in locals_:
            cp.wait()

    any_spec = pl.BlockSpec(memory_space=pl.ANY)
    return pl.pallas_call(
        body, name="gather_weights",
        in_specs=[any_spec] * n, out_specs=[any_spec] * n,
        out_shape=[SDS((DEPTH, N_CHIPS) + s.shape[1:], s.dtype) for s in shards],
        scratch_shapes=[pltpu.SemaphoreType.DMA((6 * n,)), pltpu.SemaphoreType.DMA((6 * n,)),
                        pltpu.SemaphoreType.DMA((DEPTH * n,))],
    )(*shards)


def pair_send_other_layer(grads):
    n = len(grads)

    def body(*refs):
        in_refs, out_refs = refs[:n], refs[n:2 * n]
        send_sems, recv_sems = refs[2 * n:]
        x, y, c = _me()
        cps = [pltpu.make_async_remote_copy(src_ref=in_refs[a].at[1 - c], dst_ref=out_refs[a], send_sem=send_sems.at[a],
                                            recv_sem=recv_sems.at[a], device_id=(x, y, 1 - c), device_id_type=MESH)
               for a in range(n)]
        for cp in cps:
            cp.start()
        for cp in cps:
            cp.wait()

    any_spec = pl.BlockSpec(memory_space=pl.ANY)
    return pl.pallas_call(
        body, name="pair_send_other_layer", in_specs=[any_spec] * n, out_specs=[any_spec] * n,
        out_shape=[SDS(g.shape[1:], f32) for g in grads],
        scratch_shapes=[pltpu.SemaphoreType.DMA((n,)), pltpu.SemaphoreType.DMA((n,))],
    )(*grads)


def pair_add(name, g, recv, c_arr):
    shard = g.shape[2:]
    cols = shard[-1]
    rows = N_CHIPS * math.prod(shard[:-1])
    tr = _row_tile(rows, cols)

    def body(c_ref, g_ref, r_ref, o_ref):
        o_ref[...] = (g_ref[0] + r_ref[...]).astype(bf16)

    out = pl.pallas_call(
        body, name=name,
        grid_spec=pltpu.PrefetchScalarGridSpec(
            num_scalar_prefetch=1, grid=(rows // tr,),
            in_specs=[pl.BlockSpec((1, tr, cols), lambda i, c_ref: (c_ref[0], i, 0)),
                      pl.BlockSpec((tr, cols), lambda i, c_ref: (i, 0))],
            out_specs=pl.BlockSpec((tr, cols), lambda i, c_ref: (i, 0))),
        out_shape=SDS((rows, cols), bf16), compiler_params=_cp("parallel"),
    )(c_arr, g.reshape(DEPTH, rows, cols), recv.reshape(rows, cols))
    return out.reshape((N_CHIPS,) + shard)


def chip_exchange(psums):
    n = len(psums)

    def body(*refs):
        in_refs, out_refs = refs[:n], refs[n:2 * n]
        send_sems, recv_sems = refs[2 * n:]
        x, y, c = _me()
        chips = _other_chips(x, y)
        cps = [pltpu.make_async_remote_copy(src_ref=in_refs[a].at[2 * cx + cy], dst_ref=out_refs[a].at[k],
                                            send_sem=send_sems.at[3 * a + k], recv_sem=recv_sems.at[3 * a + k],
                                            device_id=(cx, cy, c), device_id_type=MESH)
               for a in range(n) for k, (cx, cy) in enumerate(chips)]
        for cp in cps:
            cp.start()
        for cp in cps:
            cp.wait()

    any_spec = pl.BlockSpec(memory_space=pl.ANY)
    return pl.pallas_call(
        body, name="chip_exchange", in_specs=[any_spec] * n, out_specs=[any_spec] * n,
        out_shape=[SDS((3,) + p.shape[1:], bf16) for p in psums],
        scratch_shapes=[pltpu.SemaphoreType.DMA((3 * n,)), pltpu.SemaphoreType.DMA((3 * n,))],
    )(*psums)


def chip_add(name, psum, recv, qc_arr):
    shard = psum.shape[1:]
    cols = shard[-1]
    rows = math.prod(shard[:-1])
    tr = _row_tile(rows, cols)

    def body(qc_ref, p_ref, r_ref, o_ref):
        o_ref[0] = ((p_ref[0].astype(f32) + r_ref[0].astype(f32)) + r_ref[1].astype(f32)) + r_ref[2].astype(f32)

    out = pl.pallas_call(
        body, name=name,
        grid_spec=pltpu.PrefetchScalarGridSpec(
            num_scalar_prefetch=1, grid=(rows // tr,),
            in_specs=[pl.BlockSpec((1, tr, cols), lambda i, qc_ref: (qc_ref[0], i, 0)),
                      pl.BlockSpec((3, tr, cols), lambda i, qc_ref: (0, i, 0))],
            out_specs=pl.BlockSpec((1, tr, cols), lambda i, qc_ref: (qc_ref[1], i, 0))),
        out_shape=SDS((DEPTH, rows, cols), f32), compiler_params=_cp("parallel"),
    )(qc_arr, psum.reshape(N_CHIPS, rows, cols), recv.reshape(3, rows, cols))
    return out.reshape((DEPTH,) + shard)


def pair_share(sums):
    n = len(sums)

    def body(*refs):
        out_refs = refs[n:2 * n]
        send_sems, recv_sems = refs[2 * n:]
        x, y, c = _me()
        sibling = (x, y, 1 - c)
        cps = [pltpu.make_async_remote_copy(src_ref=out_refs[a].at[c], dst_ref=out_refs[a].at[c], send_sem=send_sems.at[a],
                                            recv_sem=recv_sems.at[a], device_id=sibling, device_id_type=MESH)
               for a in range(n)]
        for cp in cps:
            cp.start()
        for cp in cps:
            cp.wait_send()
        for a in range(n):
            land = out_refs[a].at[1 - c]
            pltpu.make_async_remote_copy(src_ref=land, dst_ref=land, send_sem=send_sems.at[a], recv_sem=recv_sems.at[a],
                                         device_id=sibling, device_id_type=MESH).wait_recv()

    any_spec = pl.BlockSpec(memory_space=pl.ANY)
    return pl.pallas_call(
        body, name="pair_share", in_specs=[any_spec] * n, out_specs=[any_spec] * n,
        out_shape=[SDS(s.shape, f32) for s in sums], input_output_aliases={a: a for a in range(n)},
        scratch_shapes=[pltpu.SemaphoreType.DMA((n,)), pltpu.SemaphoreType.DMA((n,))],
    )(*sums)


def allsum_small(name, v):
    R = v.shape[0]

    def body(v_ref, o_ref, all_ref, send_sems, recv_sems, local_sem):
        x, y, c = _me()
        me, sibling = (x, y, c), (x, y, 1 - c)
        chips = _other_chips(x, y)

        def rows(px, py, pc):
            return all_ref.at[4 * px + 2 * py + pc]

        def copy(k, block, to, src=None):
            return pltpu.make_async_remote_copy(
                src_ref=rows(*block) if src is None else src, dst_ref=rows(*block),
                send_sem=send_sems.at[k], recv_sem=recv_sems.at[k], device_id=to, device_id_type=MESH)

        mine = pltpu.make_async_copy(v_ref, rows(*me), local_sem)
        mine.start()
        first = [copy(0, me, sibling, src=v_ref)]
        first += [copy(1 + j, me, (*chip, c), src=v_ref) for j, chip in enumerate(chips)]
        for cp in first:
            cp.start()
        passed = [copy(4 + j, (*chip, c), sibling) for j, chip in enumerate(chips)]
        for j, chip in enumerate(chips):
            copy(1 + j, (*chip, c), me).wait_recv()
            passed[j].start()
        copy(0, sibling, me).wait_recv()
        for j, chip in enumerate(chips):
            copy(4 + j, (*chip, 1 - c), me).wait_recv()
        for cp in first + passed:
            cp.wait_send()
        mine.wait()
        acc = all_ref[0]
        for d in range(1, 8):
            acc = acc + all_ref[d]
        o_ref[...] = acc

    return pl.pallas_call(
        body, name=name,
        in_specs=[pl.BlockSpec(memory_space=pltpu.VMEM)], out_specs=pl.BlockSpec(memory_space=pltpu.VMEM),
        out_shape=SDS((R, 128), f32),
        scratch_shapes=[pltpu.VMEM((8, R, 128), f32), pltpu.SemaphoreType.DMA((7,)), pltpu.SemaphoreType.DMA((7,)),
                        pltpu.SemaphoreType.DMA],
        compiler_params=pltpu.CompilerParams(vmem_limit_bytes=V7X_VMEM_LIMIT),
    )(v)


def reduce_grads(grads):
    x, y, c = _me()
    c_arr = jnp.reshape(c, (1,)).astype(jnp.int32)
    qc_arr = jnp.stack([2 * x + y, c]).astype(jnp.int32)
    recv = pair_send_other_layer(grads)
    psums = [pair_add(f"pair_add_{a}", g, r, c_arr) for a, (g, r) in enumerate(zip(grads, recv))]
    got = chip_exchange(psums)
    sums = [chip_add(f"chip_add_{a}", p, r, qc_arr) for a, (p, r) in enumerate(zip(psums, got))]
    return pair_share(sums)


def _swap_half(r):
    return jnp.concatenate([-r[..., 32:], r[..., :32]], axis=-1)


def _unswap_add(p, qg):
    return p + jnp.concatenate([qg[..., 32:], -qg[..., :32]], axis=-1)


def _block_diag(pw):
    L = pw.shape[0]
    out = jnp.zeros((L, 256, 256), pw.dtype)
    for gi in range(4):
        out = out.at[:, 64 * gi:64 * gi + 64, 64 * gi:64 * gi + 64].set(pw[:, gi])
    return out


def _to_col_shards(w):
    *lead, K, N = w.shape
    nl = len(lead)
    return w.reshape(*lead, K, N_CHIPS, N // N_CHIPS).transpose(*range(nl), nl + 1, nl, nl + 2)


def _from_col_shards(w):
    *lead, C, K, n = w.shape
    nl = len(lead)
    return w.transpose(*range(nl), nl + 1, nl, nl + 2).reshape(*lead, K, C * n)


def kernel(x, mem, positions, ln_g, ln_b, ffn1_w13, ffn1_w2, w_in, pool_w, pool_scale, q_norm_g, w_uq, kv_norm_g, w_ukv, w_out, mem_wq, mem_wkv, mem_wo, ffn2_w13, ffn2_w2, loss_target, m_ln_g, m_ln_b, m_ffn1_w13, m_ffn1_w2, m_w_in, m_pool_w, m_pool_scale, m_q_norm_g, m_w_uq, m_kv_norm_g, m_w_ukv, m_w_out, m_mem_wq, m_mem_wkv, m_mem_wo, m_ffn2_w13, m_ffn2_w2, v_ln_g, v_ln_b, v_ffn1_w13, v_ffn1_w2, v_w_in, v_pool_w, v_pool_scale, v_q_norm_g, v_w_uq, v_kv_norm_g, v_w_ukv, v_w_out, v_mem_wq, v_mem_wkv, v_mem_wo, v_ffn2_w13, v_ffn2_w2):
    L = DEPTH
    S = x.shape[1]
    qx, qy, _ = _me()
    chip = 2 * qx + qy

    big = [ffn1_w13, ffn1_w2, w_in, w_uq, w_ukv, w_out, mem_wq, mem_wkv, mem_wo, ffn2_w13, ffn2_w2]
    (g_f1w13, g_f1w2, g_win, g_wuq, g_wukv, g_wout, g_mwq, g_mwkv, g_mwo, g_f2w13, g_f2w2) = gather_weights(
        [w.astype(bf16) for w in big])
    f1w2 = g_f1w2.reshape(L, D_FF, D_MODEL)
    f2w2 = g_f2w2.reshape(L, D_FF, D_MODEL)
    win = g_win.reshape(L, D_MODEL, D_IN)
    win_ext = jnp.concatenate([win, _swap_half(win[..., D_IN - QK_ROPE:])], axis=-1)
    wuq = _from_col_shards(g_wuq).reshape(L, Q_LORA, MLA_HEADS, QK_NOPE + QK_ROPE)
    wq_ext = jnp.concatenate([wuq, _swap_half(wuq[..., QK_NOPE:])], axis=-1).transpose(0, 2, 1, 3)
    wukv = _from_col_shards(g_wukv)
    wout = g_wout.reshape(L, D_MODEL, D_MODEL)
    wout_pool, wout_mla = wout[:, :POOL_WIDTH], wout[:, POOL_WIDTH:]
    mwq = g_mwq.reshape(L, D_MODEL, D_MODEL)
    mwo = g_mwo.reshape(L, D_MODEL, D_MODEL)
    wbd = _block_diag(pool_w.astype(bf16))

    ln_pad = jnp.zeros((2, L, 4, N_CHIPS, D_MODEL // N_CHIPS), f32)
    ln_pad = lax.dynamic_update_slice(ln_pad, jnp.stack([ln_g, ln_b])[:, :, :, None, :], (0, 0, 0, chip, 0))
    ln_full = allsum_small("allsum_ln", ln_pad.reshape(-1, 128)) * 0.5
    ln_full = ln_full.reshape(2, L, 4, D_MODEL)
    lng, lnb = ln_full[0], ln_full[1]

    half = QK_ROPE // 2
    inv_freq = ROPE_BASE ** (-jnp.arange(half, dtype=f32) / half)
    ang = positions[0].astype(f32)[:, None] * inv_freq
    cos, sin = jnp.cos(ang), jnp.sin(ang)
    cs = jnp.concatenate([cos, cos, sin, sin], axis=-1)

    memb = mem[0].astype(bf16)
    xf = x[0]
    xb = xf.astype(bf16)
    vec = lambda a: a.reshape(1, -1)

    saved = []
    for l in range(L):
        sv = {}
        sv["x0b"] = xb
        gate, up, act = ffn_up(f"ffn1_up_{l}", xb, g_f1w13, l)
        z1, x1f, x1b = proj_res_ln(f"ffn1_down_{l}", [act], [f1w2], [l], xf, vec(lng[l, 0]), vec(lnb[l, 0]), 0.5)
        sv.update(gate1=gate, up1=up, act1=act, z1=z1, x1b=x1b)
        u, cq, ckv, cqn, ckvn, q, k, v = mix_pre(f"mix_pre_{l}", x1b, win_ext, wq_ext, wukv, l,
                                                   vec(q_norm_g[l]), vec(kv_norm_g[l]), cs)
        dpool, ypool = pool_fwd(f"pool_fwd_{l}", u, wbd[l], vec(pool_scale[l]))
        o, lse = mla_attn_fwd(f"mla_fwd_{l}", q, k, v)
        z2, x2f, x2b = proj_res_ln(f"mix_out_{l}", [ypool, o], [wout_pool, wout_mla], [l, l], x1f,
                                   vec(lng[l, 1]), vec(lnb[l, 1]), 1.0)
        sv.update(cq=cq, ckv=ckv, cqn=cqn, ckvn=ckvn, q=q, k=k, v=v, dpool=dpool, ypool=ypool, o=o, lse=lse, z2=z2, x2b=x2b)
        kvm = mm_nn_shard(f"mem_kv_{l}", memb, g_mwkv, l)
        cq_, co_, z3, x3f, x3b = cross_fwd(f"cross_fwd_{l}", x2b, x2f, mwq, mwo, l, kvm, vec(lng[l, 2]), vec(lnb[l, 2]))
        sv.update(kvm=kvm, crq=cq_, cro=co_, z3=z3, x3b=x3b)
        gate, up, act = ffn_up(f"ffn2_up_{l}", x3b, g_f2w13, l)
        z4, xf, xb = proj_res_ln(f"ffn2_down_{l}", [act], [f2w2], [l], x3f, vec(lng[l, 3]), vec(lnb[l, 3]), 0.5)
        sv.update(gate2=gate, up2=up, act2=act, z4=z4)
        saved.append(sv)

    dy, loss_blk = loss_grad("loss_grad", xf, loss_target[0])
    loss = lax.psum(loss_blk[0, 0], ("x", "y", "c"))

    G = dict(f1w13=None, f1w2=None, mwq=None, mwkv=None, mwo=None, f2w13=None, f2w2=None)
    small = {k_: [None] * L for k_ in ("win", "wuq", "wukv", "wout", "pool_w", "pool_scale", "gq", "gkv", "lng", "lnb")}
    for l in reversed(range(L)):
        sv = saved[l]
        dlg, dlb = [None] * 4, [None] * 4
        dzb, dres, dlg[3], dlb[3] = ln_bwd(f"ln4_bwd_{l}", dy, sv["z4"], vec(lng[l, 3]), 0.5)
        dh = ffn_bwd_da(f"ffn2_bwd_da_{l}", dzb, f2w2, l, sv["gate2"], sv["up2"])
        G["f2w2"] = mm_tn(f"ffn2_dw2_{l}", sv["act2"], dzb, "nat", l, G["f2w2"])
        G["f2w13"] = mm_tn(f"ffn2_dw13_{l}", sv["x3b"], dh, "shard", l, G["f2w13"])
        dy = ffn_dx(f"ffn2_dx_{l}", dh, g_f2w13, l, dres)
        dzb, dres, dlg[2], dlb[2] = ln_bwd(f"ln3_bwd_{l}", dy, sv["z3"], vec(lng[l, 2]), 1.0)
        dqc, dkvm = cross_bwd(f"cross_bwd_{l}", dzb, mwo, l, sv["crq"], sv["kvm"])
        G["mwo"] = mm_tn(f"cross_dwo_{l}", sv["cro"], dzb, "nat", l, G["mwo"])
        G["mwq"] = mm_tn(f"cross_dwq_{l}", sv["x2b"], dqc, "nat", l, G["mwq"])
        G["mwkv"] = mm_tn(f"cross_dwkv_{l}", memb, dkvm, "shard", l, G["mwkv"])
        dy = mm_nt_res(f"cross_dx_{l}", [dqc], [mwq], [l], dres, f32)
        dzb, dres, dlg[1], dlb[1] = ln_bwd(f"ln2_bwd_{l}", dy, sv["z2"], vec(lng[l, 1]), 1.0)
        dyp = mm_nt_res(f"mix_dpool_{l}", [dzb], [wout_pool], [l], None, bf16)
        do = mm_nt_res(f"mix_do_{l}", [dzb], [wout_mla], [l], None, bf16)
        dwo_p = mm_tn(f"mix_dwout_pool_{l}", sv["ypool"], dzb)
        dwo_m = mm_tn(f"mix_dwout_mla_{l}", sv["o"], dzb)
        small["wout"][l] = jnp.concatenate([dwo_p, dwo_m], axis=0)
        dq, dk, dv = mla_attn_bwd(f"mla_bwd_{l}", sv["q"], sv["k"], sv["v"], sv["o"], do, sv["lse"])
        dqe, dkv, dh_rest, dgq, dgkv = mix_post_bwd(f"mix_post_bwd_{l}", dq, dk, dv, wq_ext, wukv, l, sv["cq"], sv["ckv"],
                                                     vec(q_norm_g[l]), vec(kv_norm_g[l]), cs)
        du, dyw, dscale = pool_bwd(f"pool_bwd_{l}", dyp, sv["dpool"], wbd[l], vec(pool_scale[l]))
        dwq_e = mm_tn(f"mix_dwuq_{l}", sv["cqn"], dqe).reshape(Q_LORA, MLA_HEADS, 256)
        small["wuq"][l] = jnp.concatenate(
            [dwq_e[..., :QK_NOPE], _unswap_add(dwq_e[..., QK_NOPE:QK_NOPE + QK_ROPE], dwq_e[..., QK_NOPE + QK_ROPE:])],
            axis=-1).reshape(Q_LORA, MLA_HEADS * (QK_NOPE + QK_ROPE))
        small["wukv"][l] = mm_tn(f"mix_dwukv_{l}", sv["ckvn"], dkv)
        dwbd = mm_tn(f"pool_dw_{l}", sv["dpool"], dyw)
        small["pool_w"][l] = jnp.stack([dwbd[64 * gi:64 * gi + 64, 64 * gi:64 * gi + 64] for gi in range(4)])
        small["pool_scale"][l], small["gq"][l], small["gkv"][l] = dscale[0], dgq[0], dgkv[0]
        dh_ext = jnp.concatenate([du, dh_rest], axis=1)
        dwin_e = mm_tn(f"mix_dwin_{l}", sv["x1b"], dh_ext)
        small["win"][l] = jnp.concatenate(
            [dwin_e[:, :D_IN - QK_ROPE], _unswap_add(dwin_e[:, D_IN - QK_ROPE:D_IN], dwin_e[:, D_IN:])], axis=-1)
        dy = mm_nt_res(f"mix_dx_{l}", [dh_ext], [win_ext], [l], dres, f32)
        dzb, dres, dlg[0], dlb[0] = ln_bwd(f"ln1_bwd_{l}", dy, sv["z1"], vec(lng[l, 0]), 0.5)
        dh = ffn_bwd_da(f"ffn1_bwd_da_{l}", dzb, f1w2, l, sv["gate1"], sv["up1"])
        G["f1w2"] = mm_tn(f"ffn1_dw2_{l}", sv["act1"], dzb, "nat", l, G["f1w2"])
        G["f1w13"] = mm_tn(f"ffn1_dw13_{l}", sv["x0b"], dh, "shard", l, G["f1w13"])
        dy = ffn_dx(f"ffn1_dx_{l}", dh, g_f1w13, l, dres)
        small["lng"][l] = jnp.concatenate(dlg, axis=0)
        small["lnb"][l] = jnp.concatenate(dlb, axis=0)
    grad_x = dy[None]

    row_shards = lambda a, K: a.reshape(L, N_CHIPS, K // N_CHIPS, a.shape[-1])
    g_list = [G["f1w13"], row_shards(G["f1w2"], D_FF),
              jnp.stack(small["win"]).reshape(L, N_CHIPS, D_MODEL // N_CHIPS, D_IN),
              _to_col_shards(jnp.stack(small["wuq"])), _to_col_shards(jnp.stack(small["wukv"])),
              jnp.stack(small["wout"]).reshape(L, N_CHIPS, D_MODEL // N_CHIPS, D_MODEL),
              row_shards(G["mwq"], D_MODEL), G["mwkv"], row_shards(G["mwo"], D_MODEL),
              G["f2w13"], row_shards(G["f2w2"], D_FF)]
    big_grads = reduce_grads(g_list)

    rep = [jnp.stack(small["pool_w"]).reshape(-1), jnp.stack(small["pool_scale"]).reshape(-1),
           jnp.stack(small["gq"]).reshape(-1), jnp.stack(small["gkv"]).reshape(-1),
           jnp.stack(small["lng"]).reshape(-1), jnp.stack(small["lnb"]).reshape(-1)]
    sizes = [r.shape[0] for r in rep]
    packed = jnp.concatenate(rep)
    pad = (-packed.shape[0]) % 1024
    tot = allsum_small("allsum_small_grads", jnp.pad(packed, (0, pad)).reshape(-1, 128)).reshape(-1)
    offs = [0]
    for s_ in sizes:
        offs.append(offs[-1] + s_)
    parts = [tot[offs[i]:offs[i + 1]] for i in range(len(sizes))]
    g_pool_w = parts[0].reshape(pool_w.shape)
    g_pool_scale = parts[1].reshape(pool_scale.shape)
    g_gq = parts[2].reshape(q_norm_g.shape)
    g_gkv = parts[3].reshape(kv_norm_g.shape)
    shard_cols = lambda a: lax.dynamic_slice_in_dim(a.reshape(L, 4, D_MODEL), chip * (D_MODEL // N_CHIPS),
                                                    D_MODEL // N_CHIPS, axis=2)
    g_lng, g_lnb = shard_cols(parts[4]), shard_cols(parts[5])

    (gf1w13, gf1w2, gwin, gwuq, gwukv, gwout, gmwq, gmwkv, gmwo, gf2w13, gf2w2) = big_grads
    grads = [g_lng, g_lnb, gf1w13, gf1w2, gwin, g_pool_w, g_pool_scale, g_gq, gwuq, g_gkv, gwukv, gwout, gmwq, gmwkv,
             gmwo, gf2w13, gf2w2]
    ws = [ln_g, ln_b, ffn1_w13, ffn1_w2, w_in, pool_w, pool_scale, q_norm_g, w_uq, kv_norm_g, w_ukv, w_out, mem_wq,
          mem_wkv, mem_wo, ffn2_w13, ffn2_w2]
    ms = [m_ln_g, m_ln_b, m_ffn1_w13, m_ffn1_w2, m_w_in, m_pool_w, m_pool_scale, m_q_norm_g, m_w_uq, m_kv_norm_g, m_w_ukv,
          m_w_out, m_mem_wq, m_mem_wkv, m_mem_wo, m_ffn2_w13, m_ffn2_w2]
    vs = [v_ln_g, v_ln_b, v_ffn1_w13, v_ffn1_w2, v_w_in, v_pool_w, v_pool_scale, v_q_norm_g, v_w_uq, v_kv_norm_g, v_w_ukv,
          v_w_out, v_mem_wq, v_mem_wkv, v_mem_wo, v_ffn2_w13, v_ffn2_w2]
    deltas, new_ms, new_vs = [], [], []
    for a, (w_, g_, m_, v_) in enumerate(zip(ws, grads, ms, vs)):
        d_, nm_, nv_ = adamw(f"adamw_{a}", w_, g_.reshape(w_.shape), m_, v_)
        deltas.append(d_)
        new_ms.append(nm_)
        new_vs.append(nv_)
    grads = [g_.reshape(w_.shape) for g_, w_ in zip(grads, ws)]
    return (loss, grad_x, *grads, *deltas, *new_ms, *new_vs)
```

```python
import functools
import math

import jax
import jax.numpy as jnp
from jax import lax
from jax.experimental import pallas as pl
from jax.experimental.pallas import tpu as pltpu

f32 = jnp.float32
bf16 = jnp.bfloat16
SDS = jax.ShapeDtypeStruct
MESH = pl.DeviceIdType.MESH

D_MODEL = 1024
DEPTH = 2
N_MEM = 256
MEM_HEADS = 4
MEM_HEAD_DIM = D_MODEL // MEM_HEADS
POOL_WINDOWS = (2, 4, 8, 16)
POOL_WIDTH = 256
POOL_GROUP = 64
QK_NOPE = 128
QK_ROPE = 64
V_HEAD = 128
MLA_HEADS = 6
Q_LORA = 256
KV_LORA = 128
ROPE_BASE = 10000.0
D_FF = 2816
D_IN = POOL_WIDTH + Q_LORA + KV_LORA + QK_ROPE
ALPHA = (2 * DEPTH) ** 0.25
LN_EPS = 1e-5
RMS_EPS = 1e-6
NEG_INF = -1e30
MLA_SCALE = (QK_NOPE + QK_ROPE) ** -0.5
MEM_SCALE = MEM_HEAD_DIM ** -0.5
ADAM_LR = 0.001
ADAM_B1 = 0.9
ADAM_B2 = 0.999
ADAM_EPS = 1e-08
ADAM_WD = 0.01
ADAM_STEP = 10

N_CHIPS = 4
V7X_VMEM_LIMIT = 56 * 2**20
HALO = 16

_NT = (((1,), (1,)), ((), ()))
_TN = (((0,), (0,)), ((), ()))


def _dot(a, b):
    return jnp.dot(a, b, preferred_element_type=f32)


def _dot_nt(a, b):
    return lax.dot_general(a, b, _NT, preferred_element_type=f32)


def _dot_tn(a, b):
    return lax.dot_general(a, b, _TN, preferred_element_type=f32)


def _cp(*sem):
    return pltpu.CompilerParams(dimension_semantics=sem if sem else None, vmem_limit_bytes=V7X_VMEM_LIMIT)


def _tile(n, t):
    t = min(n, t)
    assert n % t == 0, (n, t)
    return t


def _row_tile(rows, cols, itemsize=4, target=2 * 2**20):
    best = None
    for t in range(16, rows + 1, 16):
        if rows % t == 0 and t * cols * itemsize <= target:
            best = t
    return best if best is not None else rows


def ffn_up(name, xb, w13, l):
    S = xb.shape[0]
    ns = w13.shape[3]
    tm = _tile(S, 512)

    def body(x_ref, wg_ref, wu_ref, g_ref, u_ref, a_ref):
        x = x_ref[...]
        g = _dot(x, wg_ref[0, 0])
        u = _dot(x, wu_ref[0, 0])
        a = g * jax.nn.sigmoid(g) * u
        g_ref[...] = g.astype(bf16)
        u_ref[...] = u.astype(bf16)
        a_ref[...] = a.astype(bf16)

    out = SDS((S, 2 * ns), bf16)
    return pl.pallas_call(
        body, name=name, grid=(2, S // tm),
        in_specs=[pl.BlockSpec((tm, D_MODEL), lambda j, i: (i, 0)),
                  pl.BlockSpec((1, 1, D_MODEL, ns), lambda j, i: (l, j, 0, 0)),
                  pl.BlockSpec((1, 1, D_MODEL, ns), lambda j, i: (l, j + 2, 0, 0))],
        out_specs=[pl.BlockSpec((tm, ns), lambda j, i: (i, j))] * 3,
        out_shape=[out, out, out],
        compiler_params=_cp("parallel", "parallel"),
    )(xb, w13, w13)


def proj_res_ln(name, parts, ws, wl, x, g, b, rscale):
    S = x.shape[0]
    tm = _tile(S, 256)
    n = len(parts)

    def body(*refs):
        p_refs, w_refs = refs[:n], refs[n:2 * n]
        x_ref, g_ref, b_ref, z_ref, y_ref, yb_ref = refs[2 * n:]
        acc = _dot(p_refs[0][...], w_refs[0][0])
        for k in range(1, n):
            acc = acc + _dot(p_refs[k][...], w_refs[k][0])
        if rscale != 1.0:
            acc = rscale * acc
        z = ALPHA * x_ref[...] + acc
        mu = jnp.mean(z, axis=-1, keepdims=True)
        zc = z - mu
        var = jnp.mean(zc * zc, axis=-1, keepdims=True)
        y = zc * lax.rsqrt(var + LN_EPS) * g_ref[...] + b_ref[...]
        z_ref[...] = z
        y_ref[...] = y
        yb_ref[...] = y.astype(bf16)

    row = lambda i: (i, 0)
    in_specs = [pl.BlockSpec((tm, p.shape[1]), row) for p in parts]
    in_specs += [pl.BlockSpec((1,) + w.shape[1:], functools.partial(lambda li, i: (li, 0, 0), li)) for w, li in zip(ws, wl)]
    in_specs += [pl.BlockSpec((tm, D_MODEL), row), pl.BlockSpec((1, D_MODEL), lambda i: (0, 0)),
                 pl.BlockSpec((1, D_MODEL), lambda i: (0, 0))]
    return pl.pallas_call(
        body, name=name, grid=(S // tm,), in_specs=in_specs,
        out_specs=[pl.BlockSpec((tm, D_MODEL), row)] * 3,
        out_shape=[SDS((S, D_MODEL), f32), SDS((S, D_MODEL), f32), SDS((S, D_MODEL), bf16)],
        compiler_params=_cp("parallel"),
    )(*parts, *ws, x, g, b)


def ln_bwd(name, dy, z, g, rscale):
    S = dy.shape[0]
    tm = _tile(S, 512)

    def body(dy_ref, z_ref, g_ref, dzb_ref, dres_ref, dg_ref, db_ref):
        z = z_ref[...]
        mu = jnp.mean(z, axis=-1, keepdims=True)
        zc = z - mu
        rstd = lax.rsqrt(jnp.mean(zc * zc, axis=-1, keepdims=True) + LN_EPS)
        xhat = zc * rstd
        dyv = dy_ref[...]
        dxh = dyv * g_ref[...]
        m1 = jnp.mean(dxh, axis=-1, keepdims=True)
        m2 = jnp.mean(dxh * xhat, axis=-1, keepdims=True)
        dz = rstd * (dxh - m1 - xhat * m2)
        dzb_ref[...] = (rscale * dz).astype(bf16)
        dres_ref[...] = ALPHA * dz

        @pl.when(pl.program_id(0) == 0)
        def _():
            dg_ref[...] = jnp.zeros_like(dg_ref)
            db_ref[...] = jnp.zeros_like(db_ref)

        dg_ref[...] += jnp.sum(dyv * xhat, axis=0, keepdims=True)
        db_ref[...] += jnp.sum(dyv, axis=0, keepdims=True)

    row = lambda i: (i, 0)
    vec = pl.BlockSpec((1, D_MODEL), lambda i: (0, 0))
    return pl.pallas_call(
        body, name=name, grid=(S // tm,),
        in_specs=[pl.BlockSpec((tm, D_MODEL), row), pl.BlockSpec((tm, D_MODEL), row), vec],
        out_specs=[pl.BlockSpec((tm, D_MODEL), row), pl.BlockSpec((tm, D_MODEL), row), vec, vec],
        out_shape=[SDS((S, D_MODEL), bf16), SDS((S, D_MODEL), f32), SDS((1, D_MODEL), f32), SDS((1, D_MODEL), f32)],
        compiler_params=_cp("arbitrary"),
    )(dy, z, g)


def ffn_bwd_da(name, drb, w2, l, gate, up):
    S = drb.shape[0]
    tm = _tile(S, 256)
    nh = D_FF // 2

    def body(dr_ref, w_ref, g_ref, u_ref, dh_ref):
        dr = dr_ref[...]
        for j in range(2):
            cols = slice(j * nh, (j + 1) * nh)
            da = _dot_nt(dr, w_ref[0, cols, :])
            g = g_ref[:, cols].astype(f32)
            u = u_ref[:, cols].astype(f32)
            sg = jax.nn.sigmoid(g)
            dh_ref[:, cols] = (da * u * (sg * (1.0 + g * (1.0 - sg)))).astype(bf16)
            dh_ref[:, D_FF + j * nh:D_FF + (j + 1) * nh] = (da * (g * sg)).astype(bf16)

    row = lambda i: (i, 0)
    return pl.pallas_call(
        body, name=name, grid=(S // tm,),
        in_specs=[pl.BlockSpec((tm, D_MODEL), row), pl.BlockSpec((1, D_FF, D_MODEL), lambda i: (l, 0, 0)),
                  pl.BlockSpec((tm, D_FF), row), pl.BlockSpec((tm, D_FF), row)],
        out_specs=pl.BlockSpec((tm, 2 * D_FF), row),
        out_shape=SDS((S, 2 * D_FF), bf16),
        compiler_params=_cp("parallel"),
    )(drb, w2, gate, up)


def ffn_dx(name, dh, w13, l, res):
    S = dh.shape[0]
    ns = w13.shape[3]
    tm = _tile(S, 1024)

    def body(dh_ref, w_ref, r_ref, o_ref):
        @pl.when(pl.program_id(1) == 0)
        def _():
            o_ref[...] = r_ref[...]

        o_ref[...] += _dot_nt(dh_ref[...], w_ref[0, 0])

    return pl.pallas_call(
        body, name=name, grid=(S // tm, N_CHIPS),
        in_specs=[pl.BlockSpec((tm, ns), lambda i, j: (i, j)),
                  pl.BlockSpec((1, 1, D_MODEL, ns), lambda i, j: (l, j, 0, 0)),
                  pl.BlockSpec((tm, D_MODEL), lambda i, j: (i, 0))],
        out_specs=pl.BlockSpec((tm, D_MODEL), lambda i, j: (i, 0)),
        out_shape=SDS((S, D_MODEL), f32),
        compiler_params=_cp("parallel", "arbitrary"),
    )(dh, w13, res)


def mm_nt_res(name, dys, ws, wl, res, out_dtype):
    S = dys[0].shape[0]
    K = ws[0].shape[1]
    tm = _tile(S, 512)
    n = len(dys)

    def body(*refs):
        dy_refs, w_refs = refs[:n], refs[n:2 * n]
        o_ref = refs[-1]
        acc = _dot_nt(dy_refs[0][...], w_refs[0][0])
        for k in range(1, n):
            acc = acc + _dot_nt(dy_refs[k][...], w_refs[k][0])
        if res is not None:
            acc = acc + refs[2 * n][...]
        o_ref[...] = acc.astype(out_dtype)

    row = lambda i: (i, 0)
    in_specs = [pl.BlockSpec((tm, d.shape[1]), row) for d in dys]
    in_specs += [pl.BlockSpec((1,) + w.shape[1:], functools.partial(lambda li, i: (li, 0, 0), li)) for w, li in zip(ws, wl)]
    args = list(dys) + list(ws)
    if res is not None:
        in_specs.append(pl.BlockSpec((tm, K), row))
        args.append(res)
    return pl.pallas_call(
        body, name=name, grid=(S // tm,), in_specs=in_specs,
        out_specs=pl.BlockSpec((tm, K), row), out_shape=SDS((S, K), out_dtype),
        compiler_params=_cp("parallel"),
    )(*args)


def mm_tn(name, x, dy, col_shards=False):
    S, K = x.shape
    N = dy.shape[1]
    ts = _tile(S, 512)
    if col_shards:
        tn = N // N_CHIPS
    else:
        tn = N
        while K * tn * 4 > 6 * 2**20 and tn % 256 == 0:
            tn //= 2
    nn = N // tn
    lead = ((0,) if col_shards else ()) + (slice(None), slice(None))

    def body(x_ref, dy_ref, o_ref):
        acc = _dot_tn(x_ref[...].astype(bf16), dy_ref[...].astype(bf16))

        @pl.when(pl.program_id(1) == 0)
        def _():
            o_ref[lead] = acc

        @pl.when(pl.program_id(1) != 0)
        def _():
            o_ref[lead] += acc

    if col_shards:
        out_spec = pl.BlockSpec((1, K, tn), lambda n, s: (n, 0, 0))
        out_shape = SDS((N_CHIPS, K, tn), f32)
    else:
        out_spec = pl.BlockSpec((K, tn), lambda n, s: (0, n))
        out_shape = SDS((K, N), f32)
    return pl.pallas_call(
        body, name=name, grid=(nn, S // ts),
        in_specs=[pl.BlockSpec((ts, K), lambda n, s: (s, 0)), pl.BlockSpec((ts, tn), lambda n, s: (s, n))],
        out_specs=out_spec, out_shape=out_shape, compiler_params=_cp("parallel", "arbitrary"),
    )(x, dy)


def mm_nn_shard(name, x, w, l):
    S, K = x.shape
    ns = w.shape[3]

    def body(x_ref, w_ref, o_ref):
        o_ref[...] = _dot(x_ref[...], w_ref[0, 0]).astype(bf16)

    return pl.pallas_call(
        body, name=name, grid=(N_CHIPS,),
        in_specs=[pl.BlockSpec((S, K), lambda j: (0, 0)), pl.BlockSpec((1, 1, K, ns), lambda j: (l, j, 0, 0))],
        out_specs=pl.BlockSpec((S, ns), lambda j: (0, j)), out_shape=SDS((S, N_CHIPS * ns), bf16),
        compiler_params=_cp("parallel"),
    )(x, w)


def loss_grad(name, y, t):
    S = y.shape[0]
    tm = _tile(S, 512)

    def body(y_ref, t_ref, dy_ref, loss_ref):
        e = y_ref[...] - t_ref[...]
        dy_ref[...] = e * (1.0 / D_MODEL)

        @pl.when(pl.program_id(0) == 0)
        def _():
            loss_ref[...] = jnp.zeros_like(loss_ref)

        loss_ref[...] += jnp.full(loss_ref.shape, (0.5 / D_MODEL) * jnp.sum(e * e), f32)

    row = lambda i: (i, 0)
    return pl.pallas_call(
        body, name=name, grid=(S // tm,),
        in_specs=[pl.BlockSpec((tm, D_MODEL), row)] * 2,
        out_specs=[pl.BlockSpec((tm, D_MODEL), row), pl.BlockSpec((8, 128), lambda i: (0, 0))],
        out_shape=[SDS((S, D_MODEL), f32), SDS((8, 128), f32)],
        compiler_params=_cp("arbitrary"),
    )(y, t)


def _half_sum(t):
    return t + pltpu.roll(t, 64, axis=1)


def mix_pre(name, xb, w_in, wq, wkv, l, gq, gkv, cs):
    S = xb.shape[0]
    tm = _tile(S, 256)
    H = MLA_HEADS
    W_EXT = w_in.shape[2]

    def body(x_ref, win_ref, wq_ref, wkv_ref, gq_ref, gkv_ref, cs_ref,
             u_ref, cq_ref, ckv_ref, cqn_ref, ckvn_ref, q_ref, k_ref, v_ref):
        h = _dot(x_ref[...], win_ref[0])
        u_ref[...] = h[:, :256]
        cq = h[:, 256:512]
        ckv = h[:, 512:640]
        cq_ref[...] = cq
        ckv_ref[...] = ckv
        cqn = (cq * lax.rsqrt(jnp.mean(cq * cq, axis=-1, keepdims=True) + RMS_EPS) * gq_ref[...]).astype(bf16)
        ckvn = (ckv * lax.rsqrt(jnp.mean(ckv * ckv, axis=-1, keepdims=True) + RMS_EPS) * gkv_ref[...]).astype(bf16)
        cqn_ref[...] = cqn
        ckvn_ref[...] = ckvn
        csv = cs_ref[...]
        lane = lax.broadcasted_iota(jnp.int32, (tm, 128), 1)
        kr = jnp.where(lane < 64, _half_sum(h[:, 640:768] * csv), 0.0).astype(bf16)
        kv = _dot(ckvn, wkv_ref[0])
        for hd in range(H):
            qe = _dot(cqn, wq_ref[0, hd])
            q_ref[hd, :, :128] = qe[:, :128].astype(bf16)
            q_ref[hd, :, 128:] = _half_sum(qe[:, 128:] * csv).astype(bf16)
            k_ref[hd, :, :128] = kv[:, 256 * hd:256 * hd + 128].astype(bf16)
            k_ref[hd, :, 128:] = kr
            v_ref[hd] = kv[:, 256 * hd + 128:256 * hd + 256].astype(bf16)

    row = lambda i: (i, 0)
    hrow = lambda i: (0, i, 0)
    return pl.pallas_call(
        body, name=name, grid=(S // tm,),
        in_specs=[pl.BlockSpec((tm, D_MODEL), row),
                  pl.BlockSpec((1, D_MODEL, W_EXT), lambda i: (l, 0, 0)),
                  pl.BlockSpec((1, H, Q_LORA, 256), lambda i: (l, 0, 0, 0)),
                  pl.BlockSpec((1, KV_LORA, H * 256), lambda i: (l, 0, 0)),
                  pl.BlockSpec((1, Q_LORA), lambda i: (0, 0)), pl.BlockSpec((1, KV_LORA), lambda i: (0, 0)),
                  pl.BlockSpec((tm, 128), row)],
        out_specs=[pl.BlockSpec((tm, 256), row), pl.BlockSpec((tm, Q_LORA), row), pl.BlockSpec((tm, KV_LORA), row),
                   pl.BlockSpec((tm, Q_LORA), row), pl.BlockSpec((tm, KV_LORA), row),
                   pl.BlockSpec((H, tm, 256), hrow), pl.BlockSpec((H, tm, 256), hrow), pl.BlockSpec((H, tm, 128), hrow)],
        out_shape=[SDS((S, 256), f32), SDS((S, Q_LORA), f32), SDS((S, KV_LORA), f32),
                   SDS((S, Q_LORA), bf16), SDS((S, KV_LORA), bf16),
                   SDS((H, S, 256), bf16), SDS((H, S, 256), bf16), SDS((H, S, 128), bf16)],
        compiler_params=_cp("parallel"),
    )(xb, w_in, wq, wkv, gq, gkv, cs)


def _group_select(col, a2, a4, a8, a16):
    return jnp.where(col < 64, a2, jnp.where(col < 128, a4, jnp.where(col < 192, a8, a16)))


def pool_fwd(name, u, wbd, scale):
    S = u.shape[0]
    tm = _tile(S, 512)
    hb = tm // HALO

    def body(u_ref, halo_ref, w_ref, s_ref, d_ref, y_ref):
        i = pl.program_id(0)
        cur = u_ref[...]
        halo = jnp.where(i > 0, halo_ref[...], 0.0)
        ext = jnp.concatenate([halo, cur], axis=0)
        s2 = ext + pltpu.roll(ext, 1, axis=0)
        s4 = s2 + pltpu.roll(s2, 2, axis=0)
        s8 = s4 + pltpu.roll(s4, 4, axis=0)
        s16 = s8 + pltpu.roll(s8, 8, axis=0)
        t1 = (i * tm + 1 + lax.broadcasted_iota(jnp.int32, (tm, 1), 0)).astype(f32)
        col = lax.broadcasted_iota(jnp.int32, (tm, 256), 1)
        m = _group_select(col, s2[HALO:] / jnp.minimum(t1, 2.0), s4[HALO:] / jnp.minimum(t1, 4.0),
                          s8[HALO:] / jnp.minimum(t1, 8.0), s16[HALO:] / jnp.minimum(t1, 16.0))
        d = (m - cur).astype(bf16)
        d_ref[...] = d
        y_ref[...] = (_dot(d, w_ref[...]) * s_ref[...]).astype(bf16)

    row = lambda i: (i, 0)
    return pl.pallas_call(
        body, name=name, grid=(S // tm,),
        in_specs=[pl.BlockSpec((tm, 256), row), pl.BlockSpec((HALO, 256), lambda i: (jnp.maximum(i * hb - 1, 0), 0)),
                  pl.BlockSpec((256, 256), lambda i: (0, 0)), pl.BlockSpec((1, 256), lambda i: (0, 0))],
        out_specs=[pl.BlockSpec((tm, 256), row)] * 2,
        out_shape=[SDS((S, 256), bf16), SDS((S, 256), bf16)],
        compiler_params=_cp("parallel"),
    )(u, u, wbd, scale)


def pool_bwd(name, dyp, d, wbd, scale):
    S = dyp.shape[0]
    tm = _tile(S, 512)
    hb = tm // HALO
    n_ext = tm + HALO

    def fwd_sum(e, steps):
        k = 1
        for _ in range(steps):
            e = e + pltpu.roll(e, n_ext - k, axis=0)
            k *= 2
        return e

    def body(dy_ref, halo_ref, d_ref, w_ref, s_ref, du_ref, dyw_ref, ds_ref):
        i = pl.program_id(0)
        sc = s_ref[...]
        w = w_ref[...]
        cur = dy_ref[...].astype(f32)
        halo = jnp.where(i < pl.num_programs(0) - 1, halo_ref[...].astype(f32), 0.0)
        dyw = jnp.concatenate([cur, halo], axis=0) * sc
        dyw_ref[...] = dyw[:tm].astype(bf16)
        dd = _dot_nt(dyw.astype(bf16), w)
        t1 = (i * tm + 1 + lax.broadcasted_iota(jnp.int32, (n_ext, 1), 0)).astype(f32)
        f2 = fwd_sum(dd / jnp.minimum(t1, 2.0), 1)
        f4 = fwd_sum(dd / jnp.minimum(t1, 4.0), 2)
        f8 = fwd_sum(dd / jnp.minimum(t1, 8.0), 3)
        f16 = fwd_sum(dd / jnp.minimum(t1, 16.0), 4)
        col = lax.broadcasted_iota(jnp.int32, (tm, 256), 1)
        du_ref[...] = (_group_select(col, f2[:tm], f4[:tm], f8[:tm], f16[:tm]) - dd[:tm]).astype(bf16)

        @pl.when(i == 0)
        def _():
            ds_ref[...] = jnp.zeros_like(ds_ref)

        ds_ref[...] += jnp.sum(cur * _dot(d_ref[...], w), axis=0, keepdims=True)

    row = lambda i: (i, 0)
    nhb = S // HALO
    return pl.pallas_call(
        body, name=name, grid=(S // tm,),
        in_specs=[pl.BlockSpec((tm, 256), row), pl.BlockSpec((HALO, 256), lambda i: (jnp.minimum((i + 1) * hb, nhb - 1), 0)),
                  pl.BlockSpec((tm, 256), row), pl.BlockSpec((256, 256), lambda i: (0, 0)),
                  pl.BlockSpec((1, 256), lambda i: (0, 0))],
        out_specs=[pl.BlockSpec((tm, 256), row), pl.BlockSpec((tm, 256), row), pl.BlockSpec((1, 256), lambda i: (0, 0))],
        out_shape=[SDS((S, 256), bf16), SDS((S, 256), bf16), SDS((1, 256), f32)],
        compiler_params=_cp("arbitrary"),
    )(dyp, dyp, d, wbd, scale)


def _diag_mask(tq):
    rc = lax.broadcasted_iota(jnp.int32, (tq, 1), 0) // 64
    cc = lax.broadcasted_iota(jnp.int32, (1, tq), 1) // 64
    return rc >= cc


MLA_SCALE_LOG2 = MLA_SCALE * math.log2(math.e)


def mla_attn_fwd(name, q, k, v):
    H, S, _ = q.shape
    tq = _tile(S, 512)
    nq = S // tq

    def body(q_ref, k_ref, v_ref, o_ref, lse_ref, m_sc, l_sc, acc_sc):
        i, j = pl.program_id(1), pl.program_id(2)

        @pl.when(j == 0)
        def _():
            m_sc[...] = jnp.full_like(m_sc, NEG_INF)
            l_sc[...] = jnp.zeros_like(l_sc)
            acc_sc[...] = jnp.zeros_like(acc_sc)

        def step(masked):
            s = _dot_nt(q_ref[0], k_ref[0])
            if masked:
                s = jnp.where(_diag_mask(tq), s, NEG_INF)
            m_prev = m_sc[...]
            m_new = jnp.maximum(m_prev, jnp.max(s, axis=-1, keepdims=True))
            p = jnp.exp2((s - m_new[:, :1]) * MLA_SCALE_LOG2)
            a = jnp.exp2((m_prev - m_new) * MLA_SCALE_LOG2)
            l_sc[...] = a * l_sc[...] + jnp.sum(p, axis=-1, keepdims=True)
            acc_sc[...] = a * acc_sc[...] + _dot(p.astype(bf16), v_ref[0])
            m_sc[...] = m_new

        @pl.when(j < i)
        def _():
            step(False)

        @pl.when(j == i)
        def _():
            step(True)
            o_ref[...] = (acc_sc[...] / l_sc[...]).astype(bf16)
            lse_ref[0] = m_sc[...] * MLA_SCALE_LOG2 + jnp.log2(l_sc[...])

    return pl.pallas_call(
        body, name=name, grid=(H, nq, nq),
        in_specs=[pl.BlockSpec((1, tq, 256), lambda h, i, j: (h, i, 0)),
                  pl.BlockSpec((1, tq, 256), lambda h, i, j: (h, jnp.minimum(i, j), 0)),
                  pl.BlockSpec((1, tq, 128), lambda h, i, j: (h, jnp.minimum(i, j), 0))],
        out_specs=[pl.BlockSpec((tq, 128), lambda h, i, j: (i, h)), pl.BlockSpec((1, tq, 128), lambda h, i, j: (h, i, 0))],
        out_shape=[SDS((S, H * 128), bf16), SDS((H, S, 128), f32)],
        scratch_shapes=[pltpu.VMEM((tq, 128), f32), pltpu.VMEM((tq, 128), f32), pltpu.VMEM((tq, 128), f32)],
        compiler_params=_cp("parallel", "parallel", "arbitrary"),
    )(q, k, v)


def mla_attn_bwd(name, q, k, v, o, do, lse):
    H, S, _ = q.shape
    tq = _tile(S, 512)
    nq = S // tq

    def body(q_ref, k_ref, v_ref, o_ref, do_ref, lse_ref, dq_ref, dk_ref, dv_ref, dk_sc, dv_sc):
        j, i = pl.program_id(1), pl.program_id(2)

        @pl.when(jnp.logical_and(j == 0, i == 0))
        def _():
            dq_ref[...] = jnp.zeros_like(dq_ref)

        @pl.when(i == j)
        def _():
            dk_sc[...] = jnp.zeros_like(dk_sc)
            dv_sc[...] = jnp.zeros_like(dv_sc)

        def step(masked):
            qv, kv_, dov = q_ref[0], k_ref[0], do_ref[...]
            s = _dot_nt(qv, kv_)
            if masked:
                s = jnp.where(_diag_mask(tq), s, NEG_INF)
            p = jnp.exp2(s * MLA_SCALE_LOG2 - lse_ref[0][:, :1])
            dv_sc[...] += _dot_tn(p.astype(bf16), dov)
            dp = _dot_nt(dov, v_ref[0])
            delta = jnp.sum(dov.astype(f32) * o_ref[...].astype(f32), axis=-1, keepdims=True)
            ds = (p * (dp - delta)).astype(bf16)
            dk_sc[...] += _dot_tn(ds, qv)
            rows = pl.ds(pl.multiple_of(i * tq, tq), tq)
            dq_ref[0, rows, :] += _dot(ds, kv_)

        @pl.when(i > j)
        def _():
            step(False)

        @pl.when(i == j)
        def _():
            step(True)

        @pl.when(i == nq - 1)
        def _():
            dk_ref[0] = dk_sc[...] * MLA_SCALE
            dv_ref[0] = dv_sc[...]

        @pl.when(jnp.logical_and(j == nq - 1, i == nq - 1))
        def _():
            dq_ref[...] = dq_ref[...] * MLA_SCALE

    qi = lambda h, j, i: (h, jnp.maximum(i, j), 0)
    kj = lambda h, j, i: (h, j, 0)
    return pl.pallas_call(
        body, name=name, grid=(H, nq, nq),
        in_specs=[pl.BlockSpec((1, tq, 256), qi), pl.BlockSpec((1, tq, 256), kj), pl.BlockSpec((1, tq, 128), kj),
                  pl.BlockSpec((tq, 128), lambda h, j, i: (jnp.maximum(i, j), h)),
                  pl.BlockSpec((tq, 128), lambda h, j, i: (jnp.maximum(i, j), h)),
                  pl.BlockSpec((1, tq, 128), qi)],
        out_specs=[pl.BlockSpec((1, S, 256), lambda h, j, i: (h, 0, 0)), pl.BlockSpec((1, tq, 256), kj),
                   pl.BlockSpec((1, tq, 128), kj)],
        out_shape=[SDS((H, S, 256), f32), SDS((H, S, 256), f32), SDS((H, S, 128), f32)],
        scratch_shapes=[pltpu.VMEM((tq, 256), f32), pltpu.VMEM((tq, 128), f32)],
        compiler_params=_cp("parallel", "arbitrary", "arbitrary"),
    )(q, k, v, o, do, lse)


def mix_post_bwd(name, dq, dk, dv, wq, wkv, l, cq, ckv, gq, gkv, cs):
    H, S, _ = dq.shape
    tm = _tile(S, 256)

    def rms_bwd(dyn, c, g):
        r = lax.rsqrt(jnp.mean(c * c, axis=-1, keepdims=True) + RMS_EPS)
        ch = c * r
        dyg = dyn * g
        dc = r * (dyg - ch * jnp.mean(dyg * ch, axis=-1, keepdims=True))
        return dc, jnp.sum(dyn * ch, axis=0, keepdims=True)

    def body(dq_ref, dk_ref, dv_ref, wq_ref, wkv_ref, cq_ref, ckv_ref, gq_ref, gkv_ref, cs_ref,
             dqe_ref, dkv_ref, dh_ref, dgq_ref, dgkv_ref):
        csv = cs_ref[...]
        lane = lax.broadcasted_iota(jnp.int32, (tm, 128), 1)
        dcqn = jnp.zeros((tm, Q_LORA), f32)
        dkr = jnp.zeros((tm, 128), f32)
        for hd in range(H):
            dqh = dq_ref[hd]
            dqe = jnp.concatenate([dqh[:, :128], _half_sum(dqh[:, 128:]) * csv], axis=1).astype(bf16)
            dqe_ref[:, 256 * hd:256 * hd + 256] = dqe
            dcqn = dcqn + _dot_nt(dqe, wq_ref[0, hd])
            dkh = dk_ref[hd]
            dkv_ref[:, 256 * hd:256 * hd + 128] = dkh[:, :128].astype(bf16)
            dkv_ref[:, 256 * hd + 128:256 * hd + 256] = dv_ref[hd].astype(bf16)
            dkr = dkr + dkh[:, 128:]
        dckvn = _dot_nt(dkv_ref[...], wkv_ref[0])
        dblk = _half_sum(jnp.where(lane < 64, dkr, 0.0)) * csv
        dcq, dgq = rms_bwd(dcqn, cq_ref[...], gq_ref[...])
        dckv, dgkv = rms_bwd(dckvn, ckv_ref[...], gkv_ref[...])
        dh_ref[:, :256] = dcq.astype(bf16)
        dh_ref[:, 256:384] = dckv.astype(bf16)
        dh_ref[:, 384:] = dblk.astype(bf16)

        @pl.when(pl.program_id(0) == 0)
        def _():
            dgq_ref[...] = jnp.zeros_like(dgq_ref)
            dgkv_ref[...] = jnp.zeros_like(dgkv_ref)

        dgq_ref[...] += dgq
        dgkv_ref[...] += dgkv

    row = lambda i: (i, 0)
    hrow = lambda i: (0, i, 0)
    return pl.pallas_call(
        body, name=name, grid=(S // tm,),
        in_specs=[pl.BlockSpec((H, tm, 256), hrow), pl.BlockSpec((H, tm, 256), hrow), pl.BlockSpec((H, tm, 128), hrow),
                  pl.BlockSpec((1, H, Q_LORA, 256), lambda i: (l, 0, 0, 0)),
                  pl.BlockSpec((1, KV_LORA, H * 256), lambda i: (l, 0, 0)),
                  pl.BlockSpec((tm, Q_LORA), row), pl.BlockSpec((tm, KV_LORA), row),
                  pl.BlockSpec((1, Q_LORA), lambda i: (0, 0)), pl.BlockSpec((1, KV_LORA), lambda i: (0, 0)),
                  pl.BlockSpec((tm, 128), row)],
        out_specs=[pl.BlockSpec((tm, H * 256), row), pl.BlockSpec((tm, H * 256), row), pl.BlockSpec((tm, 512), row),
                   pl.BlockSpec((1, Q_LORA), lambda i: (0, 0)), pl.BlockSpec((1, KV_LORA), lambda i: (0, 0))],
        out_shape=[SDS((S, H * 256), bf16), SDS((S, H * 256), bf16), SDS((S, 512), bf16),
                   SDS((1, Q_LORA), f32), SDS((1, KV_LORA), f32)],
        compiler_params=_cp("arbitrary"),
    )(dq, dk, dv, wq, wkv, cq, ckv, gq, gkv, cs)


def _cross_probs(qb, kv_ref, hd):
    cols = slice(hd * MEM_HEAD_DIM, (hd + 1) * MEM_HEAD_DIM)
    s = _dot_nt(qb[:, cols], kv_ref[:, cols]) * MEM_SCALE
    e = jnp.exp(s - jnp.max(s, axis=-1, keepdims=True))
    return e / jnp.sum(e, axis=-1, keepdims=True)


def cross_fwd(name, xb, xf, wq, wo, l, kv, g, b):
    S = xb.shape[0]
    tm = _tile(S, 256)
    M = kv.shape[0]

    def body(x_ref, xf_ref, wq_ref, wo_ref, k_ref, v_ref, g_ref, b_ref, q_ref, o_ref, z_ref, y_ref, yb_ref):
        qb = _dot(x_ref[...], wq_ref[0]).astype(bf16)
        q_ref[...] = qb
        for hd in range(MEM_HEADS):
            cols = slice(hd * MEM_HEAD_DIM, (hd + 1) * MEM_HEAD_DIM)
            p = _cross_probs(qb, k_ref, hd)
            o_ref[:, cols] = _dot(p.astype(bf16), v_ref[:, cols]).astype(bf16)
        z = ALPHA * xf_ref[...] + _dot(o_ref[...], wo_ref[0])
        mu = jnp.mean(z, axis=-1, keepdims=True)
        zc = z - mu
        var = jnp.mean(zc * zc, axis=-1, keepdims=True)
        y = zc * lax.rsqrt(var + LN_EPS) * g_ref[...] + b_ref[...]
        z_ref[...] = z
        y_ref[...] = y
        yb_ref[...] = y.astype(bf16)

    row = lambda i: (i, 0)
    wspec = pl.BlockSpec((1, D_MODEL, D_MODEL), lambda i: (l, 0, 0))
    vec = pl.BlockSpec((1, D_MODEL), lambda i: (0, 0))
    blk = pl.BlockSpec((tm, D_MODEL), row)
    return pl.pallas_call(
        body, name=name, grid=(S // tm,),
        in_specs=[blk, blk, wspec, wspec, pl.BlockSpec((M, D_MODEL), lambda i: (0, 0)),
                  pl.BlockSpec((M, D_MODEL), lambda i: (0, 1)), vec, vec],
        out_specs=[blk] * 5,
        out_shape=[SDS((S, D_MODEL), bf16), SDS((S, D_MODEL), bf16), SDS((S, D_MODEL), f32), SDS((S, D_MODEL), f32),
                   SDS((S, D_MODEL), bf16)],
        compiler_params=_cp("parallel"),
    )(xb, xf, wq, wo, kv, kv, g, b)


def cross_bwd(name, dzb, wo, l, qb, kv):
    S = dzb.shape[0]
    tm = _tile(S, 256)
    M = kv.shape[0]

    def body(dz_ref, wo_ref, q_ref, k_ref, v_ref, dq_ref, dkv_ref):
        @pl.when(pl.program_id(0) == 0)
        def _():
            dkv_ref[...] = jnp.zeros_like(dkv_ref)

        do = _dot_nt(dz_ref[...], wo_ref[0]).astype(bf16)
        qv = q_ref[...]
        for hd in range(MEM_HEADS):
            cols = slice(hd * MEM_HEAD_DIM, (hd + 1) * MEM_HEAD_DIM)
            vcols = slice(D_MODEL + hd * MEM_HEAD_DIM, D_MODEL + (hd + 1) * MEM_HEAD_DIM)
            p = _cross_probs(qv, k_ref, hd)
            doh = do[:, cols]
            dkv_ref[:, vcols] += _dot_tn(p.astype(bf16), doh)
            dp = _dot_nt(doh, v_ref[:, cols])
            ds = (p * (dp - jnp.sum(dp * p, axis=-1, keepdims=True)) * MEM_SCALE).astype(bf16)
            dq_ref[:, cols] = _dot(ds, k_ref[:, cols]).astype(bf16)
            dkv_ref[:, cols] += _dot_tn(ds, qv[:, cols])

    row = lambda i: (i, 0)
    blk = pl.BlockSpec((tm, D_MODEL), row)
    return pl.pallas_call(
        body, name=name, grid=(S // tm,),
        in_specs=[blk, pl.BlockSpec((1, D_MODEL, D_MODEL), lambda i: (l, 0, 0)), blk,
                  pl.BlockSpec((M, D_MODEL), lambda i: (0, 0)), pl.BlockSpec((M, D_MODEL), lambda i: (0, 1))],
        out_specs=[blk, pl.BlockSpec((M, 2 * D_MODEL), lambda i: (0, 0))],
        out_shape=[SDS((S, D_MODEL), bf16), SDS((M, 2 * D_MODEL), f32)],
        compiler_params=_cp("arbitrary"),
    )(dzb, wo, qb, kv, kv)


def adamw(name, w, g, m, v):
    shape = w.shape
    cols = shape[-1]
    rows = math.prod(shape[:-1])
    tr = _row_tile(rows, cols, target=2**20)
    c1 = 1.0 - ADAM_B1 ** ADAM_STEP
    c2 = 1.0 - ADAM_B2 ** ADAM_STEP

    def body(w_ref, g_ref, m_ref, v_ref, d_ref, nm_ref, nv_ref):
        gv = g_ref[...]
        nm = ADAM_B1 * m_ref[...] + (1.0 - ADAM_B1) * gv
        nv = ADAM_B2 * v_ref[...] + (1.0 - ADAM_B2) * (gv * gv)
        d_ref[...] = -ADAM_LR * ((nm / c1) / (jnp.sqrt(nv / c2) + ADAM_EPS) + ADAM_WD * w_ref[...])
        nm_ref[...] = nm
        nv_ref[...] = nv

    blk = pl.BlockSpec((tr, cols), lambda i: (i, 0))
    flat = SDS((rows, cols), f32)
    outs = pl.pallas_call(
        body, name=name, grid=(rows // tr,), in_specs=[blk] * 4, out_specs=[blk] * 3, out_shape=[flat] * 3,
        compiler_params=_cp("parallel"),
    )(*[a.reshape(rows, cols) for a in (w, g, m, v)])
    return [o.reshape(shape) for o in outs]


def _me():
    return lax.axis_index("x"), lax.axis_index("y"), lax.axis_index("c")


def _other_chips(x, y):
    return [(1 - x, y), (x, 1 - y), (1 - x, 1 - y)]


def gather_weights(shards):
    n = len(shards)

    def body(*refs):
        in_refs, out_refs = refs[:n], refs[n:2 * n]
        send_sems, recv_sems, local_sems = refs[2 * n:]
        x, y, c = _me()
        q = 2 * x + y
        sibling = (x, y, 1 - c)
        chips = _other_chips(x, y)
        started = []
        locals_ = []
        for a in range(n):
            for lyr in range(DEPTH):
                cp = pltpu.make_async_copy(in_refs[a].at[lyr], out_refs[a].at[lyr, q], local_sems.at[a * DEPTH + lyr])
                cp.start()
                locals_.append(cp)

        def rcopy(a, k, src, dst, to):
            return pltpu.make_async_remote_copy(src_ref=src, dst_ref=dst, send_sem=send_sems.at[a * 6 + k],
                                                recv_sem=recv_sems.at[a * 6 + k], device_id=to, device_id_type=MESH)

        for a in range(n):
            for k, (cx, cy) in enumerate(chips):
                cp = rcopy(a, k, in_refs[a].at[c], out_refs[a].at[c, q], (cx, cy, c))
                cp.start()
                started.append(cp)
        for a in range(n):
            for k, (cx, cy) in enumerate(chips):
                land = out_refs[a].at[c, 2 * cx + cy]
                rcopy(a, k, land, land, sibling).wait_recv()
                fwd = rcopy(a, 3 + k, land, land, sibling)
                fwd.start()
                started.append(fwd)
        for a in range(n):
            for k, (cx, cy) in enumerate(chips):
                land = out_refs[a].at[1 - c, 2 * cx + cy]
                rcopy(a, 3 + k, land, land, sibling).wait_recv()
        for cp in started:
            cp.wait_send()
        for cp in locals_:
            cp.wait()

    any_spec = pl.BlockSpec(memory_space=pl.ANY)
    return pl.pallas_call(
        body, name="gather_weights",
        in_specs=[any_spec] * n, out_specs=[any_spec] * n,
        out_shape=[SDS((DEPTH, N_CHIPS) + s.shape[1:], s.dtype) for s in shards],
        scratch_shapes=[pltpu.SemaphoreType.DMA((6 * n,)), pltpu.SemaphoreType.DMA((6 * n,)),
                        pltpu.SemaphoreType.DMA((DEPTH * n,))],
    )(*shards)


def pair_send_other_layer(grads):
    n = len(grads)

    def body(*refs):
        in_refs, out_refs = refs[:n], refs[n:2 * n]
        send_sems, recv_sems = refs[2 * n:]
        x, y, c = _me()
        cps = [pltpu.make_async_remote_copy(src_ref=in_refs[a].at[1 - c], dst_ref=out_refs[a], send_sem=send_sems.at[a],
                                            recv_sem=recv_sems.at[a], device_id=(x, y, 1 - c), device_id_type=MESH)
               for a in range(n)]
        for cp in cps:
            cp.start()
        for cp in cps:
            cp.wait()

    any_spec = pl.BlockSpec(memory_space=pl.ANY)
    return pl.pallas_call(
        body, name="pair_send_other_layer", in_specs=[any_spec] * n, out_specs=[any_spec] * n,
        out_shape=[SDS(g.shape[1:], f32) for g in grads],
        scratch_shapes=[pltpu.SemaphoreType.DMA((n,)), pltpu.SemaphoreType.DMA((n,))],
    )(*grads)


def pair_add(name, g, recv, c_arr):
    shard = g.shape[2:]
    cols = shard[-1]
    rows = N_CHIPS * math.prod(shard[:-1])
    tr = _row_tile(rows, cols)

    def body(c_ref, g_ref, r_ref, o_ref):
        o_ref[...] = (g_ref[0] + r_ref[...]).astype(bf16)

    out = pl.pallas_call(
        body, name=name,
        grid_spec=pltpu.PrefetchScalarGridSpec(
            num_scalar_prefetch=1, grid=(rows // tr,),
            in_specs=[pl.BlockSpec((1, tr, cols), lambda i, c_ref: (c_ref[0], i, 0)),
                      pl.BlockSpec((tr, cols), lambda i, c_ref: (i, 0))],
            out_specs=pl.BlockSpec((tr, cols), lambda i, c_ref: (i, 0))),
        out_shape=SDS((rows, cols), bf16), compiler_params=_cp("parallel"),
    )(c_arr, g.reshape(DEPTH, rows, cols), recv.reshape(rows, cols))
    return out.reshape((N_CHIPS,) + shard)


def chip_exchange(psums):
    n = len(psums)

    def body(*refs):
        in_refs, out_refs = refs[:n], refs[n:2 * n]
        send_sems, recv_sems = refs[2 * n:]
        x, y, c = _me()
        chips = _other_chips(x, y)
        cps = [pltpu.make_async_remote_copy(src_ref=in_refs[a].at[2 * cx + cy], dst_ref=out_refs[a].at[k],
                                            send_sem=send_sems.at[3 * a + k], recv_sem=recv_sems.at[3 * a + k],
                                            device_id=(cx, cy, c), device_id_type=MESH)
               for a in range(n) for k, (cx, cy) in enumerate(chips)]
        for cp in cps:
            cp.start()
        for cp in cps:
            cp.wait()

    any_spec = pl.BlockSpec(memory_space=pl.ANY)
    return pl.pallas_call(
        body, name="chip_exchange", in_specs=[any_spec] * n, out_specs=[any_spec] * n,
        out_shape=[SDS((3,) + p.shape[1:], bf16) for p in psums],
        scratch_shapes=[pltpu.SemaphoreType.DMA((3 * n,)), pltpu.SemaphoreType.DMA((3 * n,))],
    )(*psums)


def chip_add(name, psum, recv, qc_arr):
    shard = psum.shape[1:]
    cols = shard[-1]
    rows = math.prod(shard[:-1])
    tr = _row_tile(rows, cols)

    def body(qc_ref, p_ref, r_ref, o_ref):
        o_ref[0] = ((p_ref[0].astype(f32) + r_ref[0].astype(f32)) + r_ref[1].astype(f32)) + r_ref[2].astype(f32)

    out = pl.pallas_call(
        body, name=name,
        grid_spec=pltpu.PrefetchScalarGridSpec(
            num_scalar_prefetch=1, grid=(rows // tr,),
            in_specs=[pl.BlockSpec((1, tr, cols), lambda i, qc_ref: (qc_ref[0], i, 0)),
                      pl.BlockSpec((3, tr, cols), lambda i, qc_ref: (0, i, 0))],
            out_specs=pl.BlockSpec((1, tr, cols), lambda i, qc_ref: (qc_ref[1], i, 0))),
        out_shape=SDS((DEPTH, rows, cols), f32), compiler_params=_cp("parallel"),
    )(qc_arr, psum.reshape(N_CHIPS, rows, cols), recv.reshape(3, rows, cols))
    return out.reshape((DEPTH,) + shard)


def pair_share(sums):
    n = len(sums)

    def body(*refs):
        out_refs = refs[n:2 * n]
        send_sems, recv_sems = refs[2 * n:]
        x, y, c = _me()
        sibling = (x, y, 1 - c)
        cps = [pltpu.make_async_remote_copy(src_ref=out_refs[a].at[c], dst_ref=out_refs[a].at[c], send_sem=send_sems.at[a],
                                            recv_sem=recv_sems.at[a], device_id=sibling, device_id_type=MESH)
               for a in range(n)]
        for cp in cps:
            cp.start()
        for cp in cps:
            cp.wait_send()
        for a in range(n):
            land = out_refs[a].at[1 - c]
            pltpu.make_async_remote_copy(src_ref=land, dst_ref=land, send_sem=send_sems.at[a], recv_sem=recv_sems.at[a],
                                         device_id=sibling, device_id_type=MESH).wait_recv()

    any_spec = pl.BlockSpec(memory_space=pl.ANY)
    return pl.pallas_call(
        body, name="pair_share", in_specs=[any_spec] * n, out_specs=[any_spec] * n,
        out_shape=[SDS(s.shape, f32) for s in sums], input_output_aliases={a: a for a in range(n)},
        scratch_shapes=[pltpu.SemaphoreType.DMA((n,)), pltpu.SemaphoreType.DMA((n,))],
    )(*sums)


def allsum_small(name, v):
    R = v.shape[0]

    def body(v_ref, o_ref, all_ref, send_sems, recv_sems, local_sem):
        x, y, c = _me()
        me, sibling = (x, y, c), (x, y, 1 - c)
        chips = _other_chips(x, y)

        def rows(px, py, pc):
            return all_ref.at[4 * px + 2 * py + pc]

        def copy(k, block, to, src=None):
            return pltpu.make_async_remote_copy(
                src_ref=rows(*block) if src is None else src, dst_ref=rows(*block),
                send_sem=send_sems.at[k], recv_sem=recv_sems.at[k], device_id=to, device_id_type=MESH)

        mine = pltpu.make_async_copy(v_ref, rows(*me), local_sem)
        mine.start()
        first = [copy(0, me, sibling, src=v_ref)]
        first += [copy(1 + j, me, (*chip, c), src=v_ref) for j, chip in enumerate(chips)]
        for cp in first:
            cp.start()
        passed = [copy(4 + j, (*chip, c), sibling) for j, chip in enumerate(chips)]
        for j, chip in enumerate(chips):
            copy(1 + j, (*chip, c), me).wait_recv()
            passed[j].start()
        copy(0, sibling, me).wait_recv()
        for j, chip in enumerate(chips):
            copy(4 + j, (*chip, 1 - c), me).wait_recv()
        for cp in first + passed:
            cp.wait_send()
        mine.wait()
        acc = all_ref[0]
        for d in range(1, 8):
            acc = acc + all_ref[d]
        o_ref[...] = acc

    return pl.pallas_call(
        body, name=name,
        in_specs=[pl.BlockSpec(memory_space=pltpu.VMEM)], out_specs=pl.BlockSpec(memory_space=pltpu.VMEM),
        out_shape=SDS((R, 128), f32),
        scratch_shapes=[pltpu.VMEM((8, R, 128), f32), pltpu.SemaphoreType.DMA((7,)), pltpu.SemaphoreType.DMA((7,)),
                        pltpu.SemaphoreType.DMA],
        compiler_params=pltpu.CompilerParams(vmem_limit_bytes=V7X_VMEM_LIMIT),
    )(v)


def reduce_grads(grads):
    x, y, c = _me()
    c_arr = jnp.reshape(c, (1,)).astype(jnp.int32)
    qc_arr = jnp.stack([2 * x + y, c]).astype(jnp.int32)
    recv = pair_send_other_layer(grads)
    psums = [pair_add(f"pair_add_{a}", g, r, c_arr) for a, (g, r) in enumerate(zip(grads, recv))]
    got = chip_exchange(psums)
    sums = [chip_add(f"chip_add_{a}", p, r, qc_arr) for a, (p, r) in enumerate(zip(psums, got))]
    return pair_share(sums)


def _swap_half(r):
    return jnp.concatenate([-r[..., 32:], r[..., :32]], axis=-1)


def _unswap_add(p, qg):
    return p + jnp.concatenate([qg[..., 32:], -qg[..., :32]], axis=-1)


def _block_diag(pw):
    L = pw.shape[0]
    out = jnp.zeros((L, 256, 256), pw.dtype)
    for gi in range(4):
        out = out.at[:, 64 * gi:64 * gi + 64, 64 * gi:64 * gi + 64].set(pw[:, gi])
    return out


def _to_col_shards(w):
    *lead, K, N = w.shape
    nl = len(lead)
    return w.reshape(*lead, K, N_CHIPS, N // N_CHIPS).transpose(*range(nl), nl + 1, nl, nl + 2)


def _from_col_shards(w):
    *lead, C, K, n = w.shape
    nl = len(lead)
    return w.transpose(*range(nl), nl + 1, nl, nl + 2).reshape(*lead, K, C * n)


def _step_serial_comm(x, mem, positions, ln_g, ln_b, ffn1_w13, ffn1_w2, w_in, pool_w, pool_scale, q_norm_g, w_uq, kv_norm_g, w_ukv, w_out, mem_wq, mem_wkv, mem_wo, ffn2_w13, ffn2_w2, loss_target, m_ln_g, m_ln_b, m_ffn1_w13, m_ffn1_w2, m_w_in, m_pool_w, m_pool_scale, m_q_norm_g, m_w_uq, m_kv_norm_g, m_w_ukv, m_w_out, m_mem_wq, m_mem_wkv, m_mem_wo, m_ffn2_w13, m_ffn2_w2, v_ln_g, v_ln_b, v_ffn1_w13, v_ffn1_w2, v_w_in, v_pool_w, v_pool_scale, v_q_norm_g, v_w_uq, v_kv_norm_g, v_w_ukv, v_w_out, v_mem_wq, v_mem_wkv, v_mem_wo, v_ffn2_w13, v_ffn2_w2):
    L = DEPTH
    S = x.shape[1]
    qx, qy, _ = _me()
    chip = 2 * qx + qy

    big = [ffn1_w13, ffn1_w2, w_in, w_uq, w_ukv, w_out, mem_wq, mem_wkv, mem_wo, ffn2_w13, ffn2_w2]
    (g_f1w13, g_f1w2, g_win, g_wuq, g_wukv, g_wout, g_mwq, g_mwkv, g_mwo, g_f2w13, g_f2w2) = gather_weights(
        [w.astype(bf16) for w in big])
    f1w2 = g_f1w2.reshape(L, D_FF, D_MODEL)
    f2w2 = g_f2w2.reshape(L, D_FF, D_MODEL)
    win = g_win.reshape(L, D_MODEL, D_IN)
    win_ext = jnp.concatenate([win, _swap_half(win[..., D_IN - QK_ROPE:])], axis=-1)
    wuq = _from_col_shards(g_wuq).reshape(L, Q_LORA, MLA_HEADS, QK_NOPE + QK_ROPE)
    wq_ext = jnp.concatenate([wuq, _swap_half(wuq[..., QK_NOPE:])], axis=-1).transpose(0, 2, 1, 3)
    wukv = _from_col_shards(g_wukv)
    wout = g_wout.reshape(L, D_MODEL, D_MODEL)
    wout_pool, wout_mla = wout[:, :POOL_WIDTH], wout[:, POOL_WIDTH:]
    mwq = g_mwq.reshape(L, D_MODEL, D_MODEL)
    mwo = g_mwo.reshape(L, D_MODEL, D_MODEL)
    wbd = _block_diag(pool_w.astype(bf16))

    ln_pad = jnp.zeros((2, L, 4, N_CHIPS, D_MODEL // N_CHIPS), f32)
    ln_pad = lax.dynamic_update_slice(ln_pad, jnp.stack([ln_g, ln_b])[:, :, :, None, :], (0, 0, 0, chip, 0))
    ln_full = allsum_small("allsum_ln", ln_pad.reshape(-1, 128)) * 0.5
    ln_full = ln_full.reshape(2, L, 4, D_MODEL)
    lng, lnb = ln_full[0], ln_full[1]

    half = QK_ROPE // 2
    inv_freq = ROPE_BASE ** (-jnp.arange(half, dtype=f32) / half)
    ang = positions[0].astype(f32)[:, None] * inv_freq
    cos, sin = jnp.cos(ang), jnp.sin(ang)
    cs = jnp.concatenate([cos, cos, sin, sin], axis=-1)

    memb = mem[0].astype(bf16)
    xf = x[0]
    xb = xf.astype(bf16)
    vec = lambda a: a.reshape(1, -1)

    saved = []
    for l in range(L):
        sv = {}
        sv["x0b"] = xb
        gate, up, act = ffn_up(f"ffn1_up_{l}", xb, g_f1w13, l)
        z1, x1f, x1b = proj_res_ln(f"ffn1_down_{l}", [act], [f1w2], [l], xf, vec(lng[l, 0]), vec(lnb[l, 0]), 0.5)
        sv.update(gate1=gate, up1=up, act1=act, z1=z1, x1b=x1b)
        u, cq, ckv, cqn, ckvn, q, k, v = mix_pre(f"mix_pre_{l}", x1b, win_ext, wq_ext, wukv, l,
                                                   vec(q_norm_g[l]), vec(kv_norm_g[l]), cs)
        dpool, ypool = pool_fwd(f"pool_fwd_{l}", u, wbd[l], vec(pool_scale[l]))
        o, lse = mla_attn_fwd(f"mla_fwd_{l}", q, k, v)
        z2, x2f, x2b = proj_res_ln(f"mix_out_{l}", [ypool, o], [wout_pool, wout_mla], [l, l], x1f,
                                   vec(lng[l, 1]), vec(lnb[l, 1]), 1.0)
        sv.update(cq=cq, ckv=ckv, cqn=cqn, ckvn=ckvn, q=q, k=k, v=v, dpool=dpool, ypool=ypool, o=o, lse=lse, z2=z2, x2b=x2b)
        kvm = mm_nn_shard(f"mem_kv_{l}", memb, g_mwkv, l)
        cq_, co_, z3, x3f, x3b = cross_fwd(f"cross_fwd_{l}", x2b, x2f, mwq, mwo, l, kvm, vec(lng[l, 2]), vec(lnb[l, 2]))
        sv.update(kvm=kvm, crq=cq_, cro=co_, z3=z3, x3b=x3b)
        gate, up, act = ffn_up(f"ffn2_up_{l}", x3b, g_f2w13, l)
        z4, xf, xb = proj_res_ln(f"ffn2_down_{l}", [act], [f2w2], [l], x3f, vec(lng[l, 3]), vec(lnb[l, 3]), 0.5)
        sv.update(gate2=gate, up2=up, act2=act, z4=z4)
        saved.append(sv)

    dy, loss_blk = loss_grad("loss_grad", xf, loss_target[0])
    loss = lax.psum(loss_blk[0, 0], ("x", "y", "c"))

    G = dict(f1w13=None, f1w2=None, mwq=None, mwkv=None, mwo=None, f2w13=None, f2w2=None)
    small = {k_: [None] * L for k_ in ("win", "wuq", "wukv", "wout", "pool_w", "pool_scale", "gq", "gkv", "lng", "lnb")}
    for l in reversed(range(L)):
        sv = saved[l]
        dlg, dlb = [None] * 4, [None] * 4
        dzb, dres, dlg[3], dlb[3] = ln_bwd(f"ln4_bwd_{l}", dy, sv["z4"], vec(lng[l, 3]), 0.5)
        dh = ffn_bwd_da(f"ffn2_bwd_da_{l}", dzb, f2w2, l, sv["gate2"], sv["up2"])
        G["f2w2"] = mm_tn(f"ffn2_dw2_{l}", sv["act2"], dzb, "nat", l, G["f2w2"])
        G["f2w13"] = mm_tn(f"ffn2_dw13_{l}", sv["x3b"], dh, "shard", l, G["f2w13"])
        dy = ffn_dx(f"ffn2_dx_{l}", dh, g_f2w13, l, dres)
        dzb, dres, dlg[2], dlb[2] = ln_bwd(f"ln3_bwd_{l}", dy, sv["z3"], vec(lng[l, 2]), 1.0)
        dqc, dkvm = cross_bwd(f"cross_bwd_{l}", dzb, mwo, l, sv["crq"], sv["kvm"])
        G["mwo"] = mm_tn(f"cross_dwo_{l}", sv["cro"], dzb, "nat", l, G["mwo"])
        G["mwq"] = mm_tn(f"cross_dwq_{l}", sv["x2b"], dqc, "nat", l, G["mwq"])
        G["mwkv"] = mm_tn(f"cross_dwkv_{l}", memb, dkvm, "shard", l, G["mwkv"])
        dy = mm_nt_res(f"cross_dx_{l}", [dqc], [mwq], [l], dres, f32)
        dzb, dres, dlg[1], dlb[1] = ln_bwd(f"ln2_bwd_{l}", dy, sv["z2"], vec(lng[l, 1]), 1.0)
        dyp = mm_nt_res(f"mix_dpool_{l}", [dzb], [wout_pool], [l], None, bf16)
        do = mm_nt_res(f"mix_do_{l}", [dzb], [wout_mla], [l], None, bf16)
        dwo_p = mm_tn(f"mix_dwout_pool_{l}", sv["ypool"], dzb)
        dwo_m = mm_tn(f"mix_dwout_mla_{l}", sv["o"], dzb)
        small["wout"][l] = jnp.concatenate([dwo_p, dwo_m], axis=0)
        dq, dk, dv = mla_attn_bwd(f"mla_bwd_{l}", sv["q"], sv["k"], sv["v"], sv["o"], do, sv["lse"])
        dqe, dkv, dh_rest, dgq, dgkv = mix_post_bwd(f"mix_post_bwd_{l}", dq, dk, dv, wq_ext, wukv, l, sv["cq"], sv["ckv"],
                                                     vec(q_norm_g[l]), vec(kv_norm_g[l]), cs)
        du, dyw, dscale = pool_bwd(f"pool_bwd_{l}", dyp, sv["dpool"], wbd[l], vec(pool_scale[l]))
        dwq_e = mm_tn(f"mix_dwuq_{l}", sv["cqn"], dqe).reshape(Q_LORA, MLA_HEADS, 256)
        small["wuq"][l] = jnp.concatenate(
            [dwq_e[..., :QK_NOPE], _unswap_add(dwq_e[..., QK_NOPE:QK_NOPE + QK_ROPE], dwq_e[..., QK_NOPE + QK_ROPE:])],
            axis=-1).reshape(Q_LORA, MLA_HEADS * (QK_NOPE + QK_ROPE))
        small["wukv"][l] = mm_tn(f"mix_dwukv_{l}", sv["ckvn"], dkv)
        dwbd = mm_tn(f"pool_dw_{l}", sv["dpool"], dyw)
        small["pool_w"][l] = jnp.stack([dwbd[64 * gi:64 * gi + 64, 64 * gi:64 * gi + 64] for gi in range(4)])
        small["pool_scale"][l], small["gq"][l], small["gkv"][l] = dscale[0], dgq[0], dgkv[0]
        dh_ext = jnp.concatenate([du, dh_rest], axis=1)
        dwin_e = mm_tn(f"mix_dwin_{l}", sv["x1b"], dh_ext)
        small["win"][l] = jnp.concatenate(
            [dwin_e[:, :D_IN - QK_ROPE], _unswap_add(dwin_e[:, D_IN - QK_ROPE:D_IN], dwin_e[:, D_IN:])], axis=-1)
        dy = mm_nt_res(f"mix_dx_{l}", [dh_ext], [win_ext], [l], dres, f32)
        dzb, dres, dlg[0], dlb[0] = ln_bwd(f"ln1_bwd_{l}", dy, sv["z1"], vec(lng[l, 0]), 0.5)
        dh = ffn_bwd_da(f"ffn1_bwd_da_{l}", dzb, f1w2, l, sv["gate1"], sv["up1"])
        G["f1w2"] = mm_tn(f"ffn1_dw2_{l}", sv["act1"], dzb, "nat", l, G["f1w2"])
        G["f1w13"] = mm_tn(f"ffn1_dw13_{l}", sv["x0b"], dh, "shard", l, G["f1w13"])
        dy = ffn_dx(f"ffn1_dx_{l}", dh, g_f1w13, l, dres)
        small["lng"][l] = jnp.concatenate(dlg, axis=0)
        small["lnb"][l] = jnp.concatenate(dlb, axis=0)
    grad_x = dy[None]

    row_shards = lambda a, K: a.reshape(L, N_CHIPS, K // N_CHIPS, a.shape[-1])
    g_list = [G["f1w13"], row_shards(G["f1w2"], D_FF),
              jnp.stack(small["win"]).reshape(L, N_CHIPS, D_MODEL // N_CHIPS, D_IN),
              _to_col_shards(jnp.stack(small["wuq"])), _to_col_shards(jnp.stack(small["wukv"])),
              jnp.stack(small["wout"]).reshape(L, N_CHIPS, D_MODEL // N_CHIPS, D_MODEL),
              row_shards(G["mwq"], D_MODEL), G["mwkv"], row_shards(G["mwo"], D_MODEL),
              G["f2w13"], row_shards(G["f2w2"], D_FF)]
    big_grads = reduce_grads(g_list)

    rep = [jnp.stack(small["pool_w"]).reshape(-1), jnp.stack(small["pool_scale"]).reshape(-1),
           jnp.stack(small["gq"]).reshape(-1), jnp.stack(small["gkv"]).reshape(-1),
           jnp.stack(small["lng"]).reshape(-1), jnp.stack(small["lnb"]).reshape(-1)]
    sizes = [r.shape[0] for r in rep]
    packed = jnp.concatenate(rep)
    pad = (-packed.shape[0]) % 1024
    tot = allsum_small("allsum_small_grads", jnp.pad(packed, (0, pad)).reshape(-1, 128)).reshape(-1)
    offs = [0]
    for s_ in sizes:
        offs.append(offs[-1] + s_)
    parts = [tot[offs[i]:offs[i + 1]] for i in range(len(sizes))]
    g_pool_w = parts[0].reshape(pool_w.shape)
    g_pool_scale = parts[1].reshape(pool_scale.shape)
    g_gq = parts[2].reshape(q_norm_g.shape)
    g_gkv = parts[3].reshape(kv_norm_g.shape)
    shard_cols = lambda a: lax.dynamic_slice_in_dim(a.reshape(L, 4, D_MODEL), chip * (D_MODEL // N_CHIPS),
                                                    D_MODEL // N_CHIPS, axis=2)
    g_lng, g_lnb = shard_cols(parts[4]), shard_cols(parts[5])

    (gf1w13, gf1w2, gwin, gwuq, gwukv, gwout, gmwq, gmwkv, gmwo, gf2w13, gf2w2) = big_grads
    grads = [g_lng, g_lnb, gf1w13, gf1w2, gwin, g_pool_w, g_pool_scale, g_gq, gwuq, g_gkv, gwukv, gwout, gmwq, gmwkv,
             gmwo, gf2w13, gf2w2]
    ws = [ln_g, ln_b, ffn1_w13, ffn1_w2, w_in, pool_w, pool_scale, q_norm_g, w_uq, kv_norm_g, w_ukv, w_out, mem_wq,
          mem_wkv, mem_wo, ffn2_w13, ffn2_w2]
    ms = [m_ln_g, m_ln_b, m_ffn1_w13, m_ffn1_w2, m_w_in, m_pool_w, m_pool_scale, m_q_norm_g, m_w_uq, m_kv_norm_g, m_w_ukv,
          m_w_out, m_mem_wq, m_mem_wkv, m_mem_wo, m_ffn2_w13, m_ffn2_w2]
    vs = [v_ln_g, v_ln_b, v_ffn1_w13, v_ffn1_w2, v_w_in, v_pool_w, v_pool_scale, v_q_norm_g, v_w_uq, v_kv_norm_g, v_w_ukv,
          v_w_out, v_mem_wq, v_mem_wkv, v_mem_wo, v_ffn2_w13, v_ffn2_w2]
    deltas, new_ms, new_vs = [], [], []
    for a, (w_, g_, m_, v_) in enumerate(zip(ws, grads, ms, vs)):
        d_, nm_, nv_ = adamw(f"adamw_{a}", w_, g_.reshape(w_.shape), m_, v_)
        deltas.append(d_)
        new_ms.append(nm_)
        new_vs.append(nv_)
    grads = [g_.reshape(w_.shape) for g_, w_ in zip(grads, ws)]
    return (loss, grad_x, *grads, *deltas, *new_ms, *new_vs)


_HBM_SPEC = pl.BlockSpec(memory_space=pltpu.HBM)
_SEM_SPEC = pl.BlockSpec(memory_space=pltpu.SEMAPHORE)
_ANY_SPEC = pl.BlockSpec(memory_space=pl.ANY)
_DATAFLOW = pltpu.SideEffectType.DATAFLOW_SIDE_EFFECTING


def _split_call(name, body_fn, bufs, sems_in, sems_out_sizes, after):
    nb, ni, no = len(bufs), len(sems_in), len(sems_out_sizes)
    has_after = after is not None

    def body(*refs):
        k = nb + ni + (1 if has_after else 0)
        body_fn(refs[:nb], refs[nb:nb + ni], refs[k:k + no])
        refs[-1][...] = jnp.zeros((8, 128), f32)

    outs = pl.pallas_call(
        body, name=name,
        in_specs=[_HBM_SPEC] * nb + [_SEM_SPEC] * ni + ([_ANY_SPEC] if has_after else []),
        out_specs=[_SEM_SPEC] * no + [_HBM_SPEC] * nb + [pl.BlockSpec(memory_space=pltpu.VMEM)],
        out_shape=[pltpu.SemaphoreType.DMA((s,)) for s in sems_out_sizes]
        + [pltpu.HBM(b.shape, b.dtype) for b in bufs] + [SDS((8, 128), f32)],
        input_output_aliases={i: no + i for i in range(nb)},
        compiler_params=pltpu.CompilerParams(has_side_effects=_DATAFLOW),
    )(*[pltpu.with_memory_space_constraint(b, pltpu.HBM) for b in bufs], *sems_in, *([after] if has_after else []))
    return list(outs[no:no + nb]), list(outs[:no]), outs[-1]


def _rcopy(src, dst, ssem, rsem, to):
    return pltpu.make_async_remote_copy(src_ref=src, dst_ref=dst, send_sem=ssem, recv_sem=rsem, device_id=to,
                                        device_id_type=MESH)


def gather_start(name, groups, after):
    flat = [b for bufs, _ in groups for b in bufs]
    sizes = [3 * len(bufs) for bufs, _ in groups for _ in range(2)]

    def body_fn(b_in, s_in, s_out):
        x, y, c = _me()
        q = 2 * x + y
        chips = _other_chips(x, y)
        pos = 0
        for gi, (bufs, owner) in enumerate(groups):
            refs = b_in[pos:pos + len(bufs)]
            pos += len(bufs)

            @pl.when(c == owner)
            def _(refs=refs, send=s_out[2 * gi], recv=s_out[2 * gi + 1]):
                for a, r in enumerate(refs):
                    for k, (cx, cy) in enumerate(chips):
                        _rcopy(r.at[q], r.at[q], send.at[3 * a + k], recv.at[3 * a + k], (cx, cy, c)).start()

    outs, sems, token = _split_call(name, body_fn, flat, [], sizes, after)
    res, pos = [], 0
    for gi, (bufs, owner) in enumerate(groups):
        res.append((outs[pos:pos + len(bufs)], sems[2 * gi], sems[2 * gi + 1], owner))
        pos += len(bufs)
    return res, token


def gather_forward(name, grp, after):
    bufs, send, recv, owner = grp
    n3 = 3 * len(bufs)

    def body_fn(b_in, s_in, s_out):
        x, y, c = _me()
        q = 2 * x + y
        sibling = (x, y, 1 - c)
        chips = _other_chips(x, y)

        @pl.when(c == owner)
        def _():
            for a, r in enumerate(b_in):
                for k, (cx, cy) in enumerate(chips):
                    i = 3 * a + k
                    land = r.at[2 * cx + cy]
                    _rcopy(r.at[q], r.at[q], s_in[0].at[i], s_in[1].at[i], (cx, cy, c)).wait_send()
                    _rcopy(land, land, s_in[0].at[i], s_in[1].at[i], (cx, cy, c)).wait_recv()
                    _rcopy(land, land, s_out[0].at[i], s_out[1].at[i], sibling).start()

    outs, sems, token = _split_call(name, body_fn, bufs, [send, recv], [n3, n3], after)
    return (outs, sems[0], sems[1], owner), token


def gather_finish(name, grp, after):
    bufs, fsend, frecv, owner = grp

    def body_fn(b_in, s_in, s_out):
        x, y, c = _me()
        sibling = (x, y, 1 - c)
        chips = _other_chips(x, y)

        def each(wait):
            for a, r in enumerate(b_in):
                for k, (cx, cy) in enumerate(chips):
                    land = r.at[2 * cx + cy]
                    wait(_rcopy(land, land, s_in[0].at[3 * a + k], s_in[1].at[3 * a + k], sibling))

        @pl.when(c == owner)
        def _():
            each(lambda cp: cp.wait_send())

        @pl.when(c != owner)
        def _():
            each(lambda cp: cp.wait_recv())

    outs, _, _ = _split_call(name, body_fn, bufs, [fsend, frecv], [], after)
    return outs


def pair_send_start(name, gs, owner, after):
    n = len(gs)
    lands = [lax.empty(g.shape, g.dtype) for g in gs]

    def body_fn(b_in, s_in, s_out):
        x, y, c = _me()

        @pl.when(c == 1 - owner)
        def _():
            for a in range(n):
                _rcopy(b_in[a], b_in[n + a], s_out[0].at[a], s_out[1].at[a], (x, y, owner)).start()

    outs, sems, token = _split_call(name, body_fn, list(gs) + lands, [], [n, n], after)
    return (outs[:n], outs[n:], sems[0], sems[1], owner), token


def pair_send_wait(name, st, after):
    gs, lands, send, recv, owner = st
    n = len(gs)

    def body_fn(b_in, s_in, s_out):
        x, y, c = _me()

        @pl.when(c == 1 - owner)
        def _():
            for a in range(n):
                _rcopy(b_in[a], b_in[n + a], s_in[0].at[a], s_in[1].at[a], (x, y, owner)).wait_send()

        @pl.when(c == owner)
        def _():
            for a in range(n):
                _rcopy(b_in[a], b_in[n + a], s_in[0].at[a], s_in[1].at[a], (x, y, 1 - owner)).wait_recv()

    outs, _, _ = _split_call(name, body_fn, list(gs) + list(lands), [send, recv], [], after)
    return outs[:n], outs[n:]


def chip_exchange_start(name, psums, owner, after):
    n = len(psums)
    lands = [lax.empty((3,) + p.shape[1:], p.dtype) for p in psums]

    def body_fn(b_in, s_in, s_out):
        x, y, c = _me()
        chips = _other_chips(x, y)

        @pl.when(c == owner)
        def _():
            for a in range(n):
                for k, (cx, cy) in enumerate(chips):
                    _rcopy(b_in[a].at[2 * cx + cy], b_in[n + a].at[k], s_out[0].at[3 * a + k], s_out[1].at[3 * a + k],
                           (cx, cy, c)).start()

    outs, sems, token = _split_call(name, body_fn, list(psums) + lands, [], [3 * n, 3 * n], after)
    return (outs[:n], outs[n:], sems[0], sems[1], owner), token


def chip_exchange_wait(name, st, after):
    psums, lands, send, recv, owner = st
    n = len(psums)

    def body_fn(b_in, s_in, s_out):
        x, y, c = _me()
        chips = _other_chips(x, y)

        @pl.when(c == owner)
        def _():
            for a in range(n):
                for k, (cx, cy) in enumerate(chips):
                    cp = _rcopy(b_in[a].at[2 * cx + cy], b_in[n + a].at[k], s_in[0].at[3 * a + k], s_in[1].at[3 * a + k],
                                (cx, cy, c))
                    cp.wait_send()
                    cp.wait_recv()

    outs, _, _ = _split_call(name, body_fn, list(psums) + list(lands), [send, recv], [], after)
    return outs[:n], outs[n:]


def pair_sum(name, g, recv):
    shape = g.shape
    cols = shape[-1]
    rows = math.prod(shape[:-1])
    tr = _row_tile(rows, cols)

    def body(g_ref, r_ref, o_ref):
        o_ref[...] = (g_ref[...] + r_ref[...]).astype(bf16)

    blk = pl.BlockSpec((tr, cols), lambda i: (i, 0))
    out = pl.pallas_call(
        body, name=name, grid=(rows // tr,), in_specs=[blk, blk], out_specs=blk, out_shape=SDS((rows, cols), bf16),
        compiler_params=_cp("parallel"),
    )(g.reshape(rows, cols), recv.reshape(rows, cols))
    return out.reshape(shape)


def chip_sum(name, psum, recv, q_arr, layer, prev):
    shard = psum.shape[1:]
    cols = shard[-1]
    rows = math.prod(shard[:-1])
    tr = _row_tile(rows, cols)

    def body(q_ref, p_ref, r_ref, *rest):
        rest[-1][0] = ((p_ref[0].astype(f32) + r_ref[0].astype(f32)) + r_ref[1].astype(f32)) + r_ref[2].astype(f32)

    in_specs = [pl.BlockSpec((1, tr, cols), lambda i, q_ref: (q_ref[0], i, 0)),
                pl.BlockSpec((3, tr, cols), lambda i, q_ref: (0, i, 0))]
    args = [q_arr, psum.reshape(N_CHIPS, rows, cols), recv.reshape(3, rows, cols)]
    aliases = {}
    if prev is not None:
        in_specs.append(pl.BlockSpec(memory_space=pl.ANY))
        args.append(prev.reshape(DEPTH, rows, cols))
        aliases = {3: 0}
    out = pl.pallas_call(
        body, name=name,
        grid_spec=pltpu.PrefetchScalarGridSpec(
            num_scalar_prefetch=1, grid=(rows // tr,), in_specs=in_specs,
            out_specs=pl.BlockSpec((1, tr, cols), lambda i, q_ref: (layer, i, 0))),
        out_shape=SDS((DEPTH, rows, cols), f32), input_output_aliases=aliases, compiler_params=_cp("parallel"),
    )(*args)
    return out.reshape((DEPTH,) + shard)


W_NAMES = ("f1w13", "f1w2", "win", "wuq", "wukv", "wout", "mwq", "mwkv", "mwo", "f2w13", "f2w2")
MIX_NAMES = ("win", "wuq", "wukv")
MID_NAMES = ("wout", "mwq", "mwkv", "mwo")
FFN2_NAMES = ("f2w13", "f2w2")


def kernel(x, mem, positions, ln_g, ln_b, ffn1_w13, ffn1_w2, w_in, pool_w, pool_scale, q_norm_g, w_uq, kv_norm_g, w_ukv, w_out, mem_wq, mem_wkv, mem_wo, ffn2_w13, ffn2_w2, loss_target, m_ln_g, m_ln_b, m_ffn1_w13, m_ffn1_w2, m_w_in, m_pool_w, m_pool_scale, m_q_norm_g, m_w_uq, m_kv_norm_g, m_w_ukv, m_w_out, m_mem_wq, m_mem_wkv, m_mem_wo, m_ffn2_w13, m_ffn2_w2, v_ln_g, v_ln_b, v_ffn1_w13, v_ffn1_w2, v_w_in, v_pool_w, v_pool_scale, v_q_norm_g, v_w_uq, v_kv_norm_g, v_w_ukv, v_w_out, v_mem_wq, v_mem_wkv, v_mem_wo, v_ffn2_w13, v_ffn2_w2):
    L = DEPTH
    qx, qy, _ = _me()
    chip = 2 * qx + qy
    vec = lambda a: a.reshape(1, -1)

    shards = dict(zip(W_NAMES, (ffn1_w13, ffn1_w2, w_in, w_uq, w_ukv, w_out, mem_wq, mem_wkv, mem_wo, ffn2_w13, ffn2_w2)))

    def place(sh):
        return lax.dynamic_update_slice(jnp.zeros((N_CHIPS,) + sh.shape, bf16), sh.astype(bf16)[None],
                                        (chip,) + (0,) * sh.ndim)

    bufs = [{n: place(shards[n][l]) for n in W_NAMES} for l in range(L)]
    gw = [dict(), dict()]
    (g0,), tok = gather_start("gather_a_start", [([bufs[0]["f1w13"], bufs[0]["f1w2"]], 0)], None)
    g0, tok = gather_forward("gather_a_forward", g0, None)
    gw[0]["f1w13"], gw[0]["f1w2"] = gather_finish("gather_a_finish", g0, None)
    (g_mix, g_mid, g_ffn2, g_l1), _ = gather_start(
        "gather_b_start",
        [([bufs[0][n] for n in MIX_NAMES], 0), ([bufs[0][n] for n in MID_NAMES], 0), ([bufs[0][n] for n in FFN2_NAMES], 0),
         ([bufs[1][n] for n in W_NAMES], 1)], tok)

    ln_pad = jnp.zeros((2, L, 4, N_CHIPS, D_MODEL // N_CHIPS), f32)
    ln_pad = lax.dynamic_update_slice(ln_pad, jnp.stack([ln_g, ln_b])[:, :, :, None, :], (0, 0, 0, chip, 0))
    ln_full = allsum_small("allsum_ln", ln_pad.reshape(-1, 128)) * 0.5
    ln_full = ln_full.reshape(2, L, 4, D_MODEL)
    lng, lnb = ln_full[0], ln_full[1]

    half = QK_ROPE // 2
    inv_freq = ROPE_BASE ** (-jnp.arange(half, dtype=f32) / half)
    ang = positions[0].astype(f32)[:, None] * inv_freq
    cos, sin = jnp.cos(ang), jnp.sin(ang)
    cs = jnp.concatenate([cos, cos, sin, sin], axis=-1)

    memb = mem[0].astype(bf16)
    xf = x[0]
    xb = xf.astype(bf16)

    saved, W = [], [None, None]
    for l in range(L):
        sv = {}
        if l == 1:
            gl1 = gather_finish("gather_l1_finish", g_l1, xb)
            gw[1] = dict(zip(W_NAMES, gl1))
        sv["x0b"] = xb
        f1w13 = gw[l]["f1w13"][None]
        gate, up, act = ffn_up(f"ffn1_up_{l}", xb, f1w13, 0)
        if l == 0:
            g_mix, _ = gather_forward("gather_mix_forward", g_mix, act)
        z1, x1f, x1b = proj_res_ln(f"ffn1_down_{l}", [act], [gw[l]["f1w2"].reshape(1, D_FF, D_MODEL)], [0], xf,
                                   vec(lng[l, 0]), vec(lnb[l, 0]), 0.5)
        sv.update(gate1=gate, up1=up, act1=act, z1=z1, x1b=x1b)
        if l == 0:
            gw[0].update(zip(MIX_NAMES, gather_finish("gather_mix_finish", g_mix, x1b)))
            g_mid, _ = gather_forward("gather_mid_forward", g_mid, x1b)
        win = gw[l]["win"].reshape(D_MODEL, D_IN)
        win_ext = jnp.concatenate([win, _swap_half(win[:, D_IN - QK_ROPE:])], axis=-1)[None]
        wuq = _from_col_shards(gw[l]["wuq"]).reshape(Q_LORA, MLA_HEADS, QK_NOPE + QK_ROPE)
        wq_ext = jnp.concatenate([wuq, _swap_half(wuq[..., QK_NOPE:])], axis=-1).transpose(1, 0, 2)[None]
        wukv = _from_col_shards(gw[l]["wukv"])[None]
        wbd = _block_diag(pool_w[l][None].astype(bf16))[0]
        u, cq, ckv, cqn, ckvn, q, k, v = mix_pre(f"mix_pre_{l}", x1b, win_ext, wq_ext, wukv, 0,
                                                   vec(q_norm_g[l]), vec(kv_norm_g[l]), cs)
        dpool, ypool = pool_fwd(f"pool_fwd_{l}", u, wbd, vec(pool_scale[l]))
        o, lse = mla_attn_fwd(f"mla_fwd_{l}", q, k, v)
        if l == 0:
            gw[0].update(zip(MID_NAMES, gather_finish("gather_mid_finish", g_mid, o)))
            g_ffn2, _ = gather_forward("gather_ffn2_forward", g_ffn2, o)
        wout = gw[l]["wout"].reshape(D_MODEL, D_MODEL)
        wout_pool, wout_mla = wout[None, :POOL_WIDTH], wout[None, POOL_WIDTH:]
        mwq = gw[l]["mwq"].reshape(1, D_MODEL, D_MODEL)
        mwo = gw[l]["mwo"].reshape(1, D_MODEL, D_MODEL)
        mwkv = gw[l]["mwkv"][None]
        z2, x2f, x2b = proj_res_ln(f"mix_out_{l}", [ypool, o], [wout_pool, wout_mla], [0, 0], x1f,
                                   vec(lng[l, 1]), vec(lnb[l, 1]), 1.0)
        sv.update(cq=cq, ckv=ckv, cqn=cqn, ckvn=ckvn, q=q, k=k, v=v, dpool=dpool, ypool=ypool, o=o, lse=lse, z2=z2, x2b=x2b)
        kvm = mm_nn_shard(f"mem_kv_{l}", memb, mwkv, 0)
        cq_, co_, z3, x3f, x3b = cross_fwd(f"cross_fwd_{l}", x2b, x2f, mwq, mwo, 0, kvm, vec(lng[l, 2]), vec(lnb[l, 2]))
        sv.update(kvm=kvm, crq=cq_, cro=co_, z3=z3, x3b=x3b)
        if l == 0:
            gw[0].update(zip(FFN2_NAMES, gather_finish("gather_ffn2_finish", g_ffn2, x3b)))
            g_l1, _ = gather_forward("gather_l1_forward", g_l1, x3b)
        f2w13 = gw[l]["f2w13"][None]
        f2w2 = gw[l]["f2w2"].reshape(1, D_FF, D_MODEL)
        gate, up, act = ffn_up(f"ffn2_up_{l}", x3b, f2w13, 0)
        z4, xf, xb = proj_res_ln(f"ffn2_down_{l}", [act], [f2w2], [0], x3f, vec(lng[l, 3]), vec(lnb[l, 3]), 0.5)
        sv.update(gate2=gate, up2=up, act2=act, z4=z4)
        W[l] = dict(f1w13=f1w13, f1w2=gw[l]["f1w2"].reshape(1, D_FF, D_MODEL), win_ext=win_ext, wq_ext=wq_ext, wukv=wukv,
                    wbd=wbd, wout_pool=wout_pool, wout_mla=wout_mla, mwq=mwq, mwo=mwo, f2w13=f2w13, f2w2=f2w2)
        saved.append(sv)

    dy, loss_blk = loss_grad("loss_grad", xf, loss_target[0])
    loss = lax.psum(loss_blk[0, 0], ("x", "y", "c"))

    row_shards = lambda a: a.reshape(N_CHIPS, a.shape[0] // N_CHIPS, a.shape[1])
    small = {k_: [None] * L for k_ in ("pool_w", "pool_scale", "gq", "gkv", "lng", "lnb")}
    G = [None, None]
    q_arr = jnp.reshape(chip, (1,)).astype(jnp.int32)
    pend_pair, pend_chip, psums1 = None, None, None
    for l in reversed(range(L)):
        sv, w = saved[l], W[l]
        g = {}
        dlg, dlb = [None] * 4, [None] * 4
        dzb, dres, dlg[3], dlb[3] = ln_bwd(f"ln4_bwd_{l}", dy, sv["z4"], vec(lng[l, 3]), 0.5)
        dh = ffn_bwd_da(f"ffn2_bwd_da_{l}", dzb, w["f2w2"], 0, sv["gate2"], sv["up2"])
        if l == 0:
            gs1, lands1 = pair_send_wait("pair_send_wait_1", pend_pair, dh)
            psums1 = [pair_sum(f"pair_sum_1_{a}", g_, r_) for a, (g_, r_) in enumerate(zip(gs1, lands1))]
            pend_chip, _ = chip_exchange_start("chip_exchange_start_1", psums1, 1, None)
        g["f2w2"] = row_shards(mm_tn(f"ffn2_dw2_{l}", sv["act2"], dzb))
        g["f2w13"] = mm_tn(f"ffn2_dw13_{l}", sv["x3b"], dh, True)
        dy = ffn_dx(f"ffn2_dx_{l}", dh, w["f2w13"], 0, dres)
        dzb, dres, dlg[2], dlb[2] = ln_bwd(f"ln3_bwd_{l}", dy, sv["z3"], vec(lng[l, 2]), 1.0)
        dqc, dkvm = cross_bwd(f"cross_bwd_{l}", dzb, w["mwo"], 0, sv["crq"], sv["kvm"])
        g["mwo"] = row_shards(mm_tn(f"cross_dwo_{l}", sv["cro"], dzb))
        g["mwq"] = row_shards(mm_tn(f"cross_dwq_{l}", sv["x2b"], dqc))
        g["mwkv"] = mm_tn(f"cross_dwkv_{l}", memb, dkvm, True)
        dy = mm_nt_res(f"cross_dx_{l}", [dqc], [w["mwq"]], [0], dres, f32)
        dzb, dres, dlg[1], dlb[1] = ln_bwd(f"ln2_bwd_{l}", dy, sv["z2"], vec(lng[l, 1]), 1.0)
        dyp = mm_nt_res(f"mix_dpool_{l}", [dzb], [w["wout_pool"]], [0], None, bf16)
        do = mm_nt_res(f"mix_do_{l}", [dzb], [w["wout_mla"]], [0], None, bf16)
        dwo_p = mm_tn(f"mix_dwout_pool_{l}", sv["ypool"], dzb)
        dwo_m = mm_tn(f"mix_dwout_mla_{l}", sv["o"], dzb)
        g["wout"] = row_shards(jnp.concatenate([dwo_p, dwo_m], axis=0))
        dq, dk, dv = mla_attn_bwd(f"mla_bwd_{l}", sv["q"], sv["k"], sv["v"], sv["o"], do, sv["lse"])
        dqe, dkv, dh_rest, dgq, dgkv = mix_post_bwd(f"mix_post_bwd_{l}", dq, dk, dv, w["wq_ext"], w["wukv"], 0, sv["cq"],
                                                     sv["ckv"], vec(q_norm_g[l]), vec(kv_norm_g[l]), cs)
        du, dyw, dscale = pool_bwd(f"pool_bwd_{l}", dyp, sv["dpool"], w["wbd"], vec(pool_scale[l]))
        dwq_e = mm_tn(f"mix_dwuq_{l}", sv["cqn"], dqe).reshape(Q_LORA, MLA_HEADS, 256)
        g["wuq"] = _to_col_shards(jnp.concatenate(
            [dwq_e[..., :QK_NOPE], _unswap_add(dwq_e[..., QK_NOPE:QK_NOPE + QK_ROPE], dwq_e[..., QK_NOPE + QK_ROPE:])],
            axis=-1).reshape(Q_LORA, MLA_HEADS * (QK_NOPE + QK_ROPE)))
        g["wukv"] = _to_col_shards(mm_tn(f"mix_dwukv_{l}", sv["ckvn"], dkv))
        dwbd = mm_tn(f"pool_dw_{l}", sv["dpool"], dyw)
        small["pool_w"][l] = jnp.stack([dwbd[64 * gi:64 * gi + 64, 64 * gi:64 * gi + 64] for gi in range(4)])
        small["pool_scale"][l], small["gq"][l], small["gkv"][l] = dscale[0], dgq[0], dgkv[0]
        dh_ext = jnp.concatenate([du, dh_rest], axis=1)
        dwin_e = mm_tn(f"mix_dwin_{l}", sv["x1b"], dh_ext)
        g["win"] = row_shards(jnp.concatenate(
            [dwin_e[:, :D_IN - QK_ROPE], _unswap_add(dwin_e[:, D_IN - QK_ROPE:D_IN], dwin_e[:, D_IN:])], axis=-1))
        dy = mm_nt_res(f"mix_dx_{l}", [dh_ext], [w["win_ext"]], [0], dres, f32)
        dzb, dres, dlg[0], dlb[0] = ln_bwd(f"ln1_bwd_{l}", dy, sv["z1"], vec(lng[l, 0]), 0.5)
        dh = ffn_bwd_da(f"ffn1_bwd_da_{l}", dzb, w["f1w2"], 0, sv["gate1"], sv["up1"])
        g["f1w2"] = row_shards(mm_tn(f"ffn1_dw2_{l}", sv["act1"], dzb))
        g["f1w13"] = mm_tn(f"ffn1_dw13_{l}", sv["x0b"], dh, True)
        dy = ffn_dx(f"ffn1_dx_{l}", dh, w["f1w13"], 0, dres)
        small["lng"][l] = jnp.concatenate(dlg, axis=0)
        small["lnb"][l] = jnp.concatenate(dlb, axis=0)
        G[l] = [g[n] for n in W_NAMES]
        if l == 1:
            pend_pair, _ = pair_send_start("pair_send_start_1", G[1], 1, None)
    grad_x = dy[None]

    psums1, lands1 = chip_exchange_wait("chip_exchange_wait_1", pend_chip, dy)
    sums = [chip_sum(f"chip_sum_1_{a}", p_, r_, q_arr, 1, None) for a, (p_, r_) in enumerate(zip(psums1, lands1))]
    pend_pair, _ = pair_send_start("pair_send_start_0", G[0], 0, None)
    gs0, lands0 = pair_send_wait("pair_send_wait_0", pend_pair, None)
    psums0 = [pair_sum(f"pair_sum_0_{a}", g_, r_) for a, (g_, r_) in enumerate(zip(gs0, lands0))]
    pend_chip, _ = chip_exchange_start("chip_exchange_start_0", psums0, 0, None)
    psums0, lands0 = chip_exchange_wait("chip_exchange_wait_0", pend_chip, None)
    sums = [chip_sum(f"chip_sum_0_{a}", p_, r_, q_arr, 0, s_) for a, (p_, r_, s_) in enumerate(zip(psums0, lands0, sums))]
    big_grads = pair_share(sums)

    rep = [jnp.stack(small["pool_w"]).reshape(-1), jnp.stack(small["pool_scale"]).reshape(-1),
           jnp.stack(small["gq"]).reshape(-1), jnp.stack(small["gkv"]).reshape(-1),
           jnp.stack(small["lng"]).reshape(-1), jnp.stack(small["lnb"]).reshape(-1)]
    sizes = [r.shape[0] for r in rep]
    packed = jnp.concatenate(rep)
    pad = (-packed.shape[0]) % 1024
    tot = allsum_small("allsum_small_grads", jnp.pad(packed, (0, pad)).reshape(-1, 128)).reshape(-1)
    offs = [0]
    for s_ in sizes:
        offs.append(offs[-1] + s_)
    parts = [tot[offs[i]:offs[i + 1]] for i in range(len(sizes))]
    g_pool_w = parts[0].reshape(pool_w.shape)
    g_pool_scale = parts[1].reshape(pool_scale.shape)
    g_gq = parts[2].reshape(q_norm_g.shape)
    g_gkv = parts[3].reshape(kv_norm_g.shape)
    shard_cols = lambda a: lax.dynamic_slice_in_dim(a.reshape(L, 4, D_MODEL), chip * (D_MODEL // N_CHIPS),
                                                    D_MODEL // N_CHIPS, axis=2)
    g_lng, g_lnb = shard_cols(parts[4]), shard_cols(parts[5])

    (gf1w13, gf1w2, gwin, gwuq, gwukv, gwout, gmwq, gmwkv, gmwo, gf2w13, gf2w2) = big_grads
    grads = [g_lng, g_lnb, gf1w13, gf1w2, gwin, g_pool_w, g_pool_scale, g_gq, gwuq, g_gkv, gwukv, gwout, gmwq, gmwkv,
             gmwo, gf2w13, gf2w2]
    ws = [ln_g, ln_b, ffn1_w13, ffn1_w2, w_in, pool_w, pool_scale, q_norm_g, w_uq, kv_norm_g, w_ukv, w_out, mem_wq,
          mem_wkv, mem_wo, ffn2_w13, ffn2_w2]
    ms = [m_ln_g, m_ln_b, m_ffn1_w13, m_ffn1_w2, m_w_in, m_pool_w, m_pool_scale, m_q_norm_g, m_w_uq, m_kv_norm_g, m_w_ukv,
          m_w_out, m_mem_wq, m_mem_wkv, m_mem_wo, m_ffn2_w13, m_ffn2_w2]
    vs = [v_ln_g, v_ln_b, v_ffn1_w13, v_ffn1_w2, v_w_in, v_pool_w, v_pool_scale, v_q_norm_g, v_w_uq, v_kv_norm_g, v_w_ukv,
          v_w_out, v_mem_wq, v_mem_wkv, v_mem_wo, v_ffn2_w13, v_ffn2_w2]
    deltas, new_ms, new_vs = [], [], []
    for a, (w_, g_, m_, v_) in enumerate(zip(ws, grads, ms, vs)):
        d_, nm_, nv_ = adamw(f"adamw_{a}", w_, g_.reshape(w_.shape), m_, v_)
        deltas.append(d_)
        new_ms.append(nm_)
        new_vs.append(nv_)
    grads = [g_.reshape(w_.shape) for g_, w_ in zip(grads, ws)]
    return (loss, grad_x, *grads, *deltas, *new_ms, *new_vs)
```

```python
import functools
import math

import jax
import jax.numpy as jnp
from jax import lax
from jax.experimental import pallas as pl
from jax.experimental.pallas import tpu as pltpu

f32 = jnp.float32
bf16 = jnp.bfloat16
SDS = jax.ShapeDtypeStruct
MESH = pl.DeviceIdType.MESH

D_MODEL = 1024
DEPTH = 2
N_MEM = 256
MEM_HEADS = 4
MEM_HEAD_DIM = D_MODEL // MEM_HEADS
POOL_WINDOWS = (2, 4, 8, 16)
POOL_WIDTH = 256
POOL_GROUP = 64
QK_NOPE = 128
QK_ROPE = 64
V_HEAD = 128
MLA_HEADS = 6
Q_LORA = 256
KV_LORA = 128
ROPE_BASE = 10000.0
D_FF = 2816
D_IN = POOL_WIDTH + Q_LORA + KV_LORA + QK_ROPE
ALPHA = (2 * DEPTH) ** 0.25
LN_EPS = 1e-5
RMS_EPS = 1e-6
NEG_INF = -1e30
MLA_SCALE = (QK_NOPE + QK_ROPE) ** -0.5
MEM_SCALE = MEM_HEAD_DIM ** -0.5
ADAM_LR = 0.001
ADAM_B1 = 0.9
ADAM_B2 = 0.999
ADAM_EPS = 1e-08
ADAM_WD = 0.01
ADAM_STEP = 10

N_CHIPS = 4
V7X_VMEM_LIMIT = 56 * 2**20
HALO = 16

_NT = (((1,), (1,)), ((), ()))
_TN = (((0,), (0,)), ((), ()))


def _dot(a, b):
    return jnp.dot(a, b, preferred_element_type=f32)


def _dot_nt(a, b):
    return lax.dot_general(a, b, _NT, preferred_element_type=f32)


def _dot_tn(a, b):
    return lax.dot_general(a, b, _TN, preferred_element_type=f32)


def _cp(*sem):
    return pltpu.CompilerParams(dimension_semantics=sem if sem else None, vmem_limit_bytes=V7X_VMEM_LIMIT)


def _tile(n, t):
    t = min(n, t)
    assert n % t == 0, (n, t)
    return t


def _row_tile(rows, cols, itemsize=4, target=2 * 2**20):
    best = None
    for t in range(16, rows + 1, 16):
        if rows % t == 0 and t * cols * itemsize <= target:
            best = t
    return best if best is not None else rows


def ffn_up(name, xb, w13, l):
    S = xb.shape[0]
    ns = w13.shape[3]
    tm = _tile(S, 512)

    def body(x_ref, wg_ref, wu_ref, g_ref, u_ref, a_ref):
        x = x_ref[...]
        g = _dot(x, wg_ref[0, 0])
        u = _dot(x, wu_ref[0, 0])
        a = g * jax.nn.sigmoid(g) * u
        g_ref[...] = g.astype(bf16)
        u_ref[...] = u.astype(bf16)
        a_ref[...] = a.astype(bf16)

    out = SDS((S, 2 * ns), bf16)
    return pl.pallas_call(
        body, name=name, grid=(2, S // tm),
        in_specs=[pl.BlockSpec((tm, D_MODEL), lambda j, i: (i, 0)),
                  pl.BlockSpec((1, 1, D_MODEL, ns), lambda j, i: (l, j, 0, 0)),
                  pl.BlockSpec((1, 1, D_MODEL, ns), lambda j, i: (l, j + 2, 0, 0))],
        out_specs=[pl.BlockSpec((tm, ns), lambda j, i: (i, j))] * 3,
        out_shape=[out, out, out],
        compiler_params=_cp("parallel", "parallel"),
    )(xb, w13, w13)


def proj_res_ln(name, parts, ws, wl, x, g, b, rscale):
    S = x.shape[0]
    tm = _tile(S, 256)
    n = len(parts)

    def body(*refs):
        p_refs, w_refs = refs[:n], refs[n:2 * n]
        x_ref, g_ref, b_ref, z_ref, y_ref, yb_ref = refs[2 * n:]
        acc = _dot(p_refs[0][...], w_refs[0][0])
        for k in range(1, n):
            acc = acc + _dot(p_refs[k][...], w_refs[k][0])
        if rscale != 1.0:
            acc = rscale * acc
        z = ALPHA * x_ref[...] + acc
        mu = jnp.mean(z, axis=-1, keepdims=True)
        zc = z - mu
        var = jnp.mean(zc * zc, axis=-1, keepdims=True)
        y = zc * lax.rsqrt(var + LN_EPS) * g_ref[...] + b_ref[...]
        z_ref[...] = z
        y_ref[...] = y
        yb_ref[...] = y.astype(bf16)

    row = lambda i: (i, 0)
    in_specs = [pl.BlockSpec((tm, p.shape[1]), row) for p in parts]
    in_specs += [pl.BlockSpec((1,) + w.shape[1:], functools.partial(lambda li, i: (li, 0, 0), li)) for w, li in zip(ws, wl)]
    in_specs += [pl.BlockSpec((tm, D_MODEL), row), pl.BlockSpec((1, D_MODEL), lambda i: (0, 0)),
                 pl.BlockSpec((1, D_MODEL), lambda i: (0, 0))]
    return pl.pallas_call(
        body, name=name, grid=(S // tm,), in_specs=in_specs,
        out_specs=[pl.BlockSpec((tm, D_MODEL), row)] * 3,
        out_shape=[SDS((S, D_MODEL), f32), SDS((S, D_MODEL), f32), SDS((S, D_MODEL), bf16)],
        compiler_params=_cp("parallel"),
    )(*parts, *ws, x, g, b)


def ln_bwd(name, dy, z, g, rscale):
    S = dy.shape[0]
    tm = _tile(S, 512)

    def body(dy_ref, z_ref, g_ref, dzb_ref, dres_ref, dg_ref, db_ref):
        z = z_ref[...]
        mu = jnp.mean(z, axis=-1, keepdims=True)
        zc = z - mu
        rstd = lax.rsqrt(jnp.mean(zc * zc, axis=-1, keepdims=True) + LN_EPS)
        xhat = zc * rstd
        dyv = dy_ref[...]
        dxh = dyv * g_ref[...]
        m1 = jnp.mean(dxh, axis=-1, keepdims=True)
        m2 = jnp.mean(dxh * xhat, axis=-1, keepdims=True)
        dz = rstd * (dxh - m1 - xhat * m2)
        dzb_ref[...] = (rscale * dz).astype(bf16)
        dres_ref[...] = ALPHA * dz

        @pl.when(pl.program_id(0) == 0)
        def _():
            dg_ref[...] = jnp.zeros_like(dg_ref)
            db_ref[...] = jnp.zeros_like(db_ref)

        dg_ref[...] += jnp.sum(dyv * xhat, axis=0, keepdims=True)
        db_ref[...] += jnp.sum(dyv, axis=0, keepdims=True)

    row = lambda i: (i, 0)
    vec = pl.BlockSpec((1, D_MODEL), lambda i: (0, 0))
    return pl.pallas_call(
        body, name=name, grid=(S // tm,),
        in_specs=[pl.BlockSpec((tm, D_MODEL), row), pl.BlockSpec((tm, D_MODEL), row), vec],
        out_specs=[pl.BlockSpec((tm, D_MODEL), row), pl.BlockSpec((tm, D_MODEL), row), vec, vec],
        out_shape=[SDS((S, D_MODEL), bf16), SDS((S, D_MODEL), f32), SDS((1, D_MODEL), f32), SDS((1, D_MODEL), f32)],
        compiler_params=_cp("arbitrary"),
    )(dy, z, g)


def ffn_bwd_da(name, drb, w2, l, gate, up):
    S = drb.shape[0]
    tm = _tile(S, 256)
    nh = D_FF // 2

    def body(dr_ref, w_ref, g_ref, u_ref, dh_ref):
        dr = dr_ref[...]
        for j in range(2):
            cols = slice(j * nh, (j + 1) * nh)
            da = _dot_nt(dr, w_ref[0, cols, :])
            g = g_ref[:, cols].astype(f32)
            u = u_ref[:, cols].astype(f32)
            sg = jax.nn.sigmoid(g)
            dh_ref[:, cols] = (da * u * (sg * (1.0 + g * (1.0 - sg)))).astype(bf16)
            dh_ref[:, D_FF + j * nh:D_FF + (j + 1) * nh] = (da * (g * sg)).astype(bf16)

    row = lambda i: (i, 0)
    return pl.pallas_call(
        body, name=name, grid=(S // tm,),
        in_specs=[pl.BlockSpec((tm, D_MODEL), row), pl.BlockSpec((1, D_FF, D_MODEL), lambda i: (l, 0, 0)),
                  pl.BlockSpec((tm, D_FF), row), pl.BlockSpec((tm, D_FF), row)],
        out_specs=pl.BlockSpec((tm, 2 * D_FF), row),
        out_shape=SDS((S, 2 * D_FF), bf16),
        compiler_params=_cp("parallel"),
    )(drb, w2, gate, up)


def ffn_dx(name, dh, w13, l, res):
    S = dh.shape[0]
    ns = w13.shape[3]
    tm = _tile(S, 1024)

    def body(dh_ref, w_ref, r_ref, o_ref):
        @pl.when(pl.program_id(1) == 0)
        def _():
            o_ref[...] = r_ref[...]

        o_ref[...] += _dot_nt(dh_ref[...], w_ref[0, 0])

    return pl.pallas_call(
        body, name=name, grid=(S // tm, N_CHIPS),
        in_specs=[pl.BlockSpec((tm, ns), lambda i, j: (i, j)),
                  pl.BlockSpec((1, 1, D_MODEL, ns), lambda i, j: (l, j, 0, 0)),
                  pl.BlockSpec((tm, D_MODEL), lambda i, j: (i, 0))],
        out_specs=pl.BlockSpec((tm, D_MODEL), lambda i, j: (i, 0)),
        out_shape=SDS((S, D_MODEL), f32),
        compiler_params=_cp("parallel", "arbitrary"),
    )(dh, w13, res)


def mm_nt_res(name, dys, ws, wl, res, out_dtype):
    S = dys[0].shape[0]
    K = ws[0].shape[1]
    tm = _tile(S, 512)
    n = len(dys)

    def body(*refs):
        dy_refs, w_refs = refs[:n], refs[n:2 * n]
        o_ref = refs[-1]
        acc = _dot_nt(dy_refs[0][...], w_refs[0][0])
        for k in range(1, n):
            acc = acc + _dot_nt(dy_refs[k][...], w_refs[k][0])
        if res is not None:
            acc = acc + refs[2 * n][...]
        o_ref[...] = acc.astype(out_dtype)

    row = lambda i: (i, 0)
    in_specs = [pl.BlockSpec((tm, d.shape[1]), row) for d in dys]
    in_specs += [pl.BlockSpec((1,) + w.shape[1:], functools.partial(lambda li, i: (li, 0, 0), li)) for w, li in zip(ws, wl)]
    args = list(dys) + list(ws)
    if res is not None:
        in_specs.append(pl.BlockSpec((tm, K), row))
        args.append(res)
    return pl.pallas_call(
        body, name=name, grid=(S // tm,), in_specs=in_specs,
        out_specs=pl.BlockSpec((tm, K), row), out_shape=SDS((S, K), out_dtype),
        compiler_params=_cp("parallel"),
    )(*args)


def mm_tn(name, x, dy, col_shards=False):
    S, K = x.shape
    N = dy.shape[1]
    ts = _tile(S, 512)
    if col_shards:
        tn = N // N_CHIPS
    else:
        tn = N
        while K * tn * 4 > 6 * 2**20 and tn % 256 == 0:
            tn //= 2
    nn = N // tn
    lead = ((0,) if col_shards else ()) + (slice(None), slice(None))

    def body(x_ref, dy_ref, o_ref):
        acc = _dot_tn(x_ref[...].astype(bf16), dy_ref[...].astype(bf16))

        @pl.when(pl.program_id(1) == 0)
        def _():
            o_ref[lead] = acc

        @pl.when(pl.program_id(1) != 0)
        def _():
            o_ref[lead] += acc

    if col_shards:
        out_spec = pl.BlockSpec((1, K, tn), lambda n, s: (n, 0, 0))
        out_shape = SDS((N_CHIPS, K, tn), f32)
    else:
        out_spec = pl.BlockSpec((K, tn), lambda n, s: (0, n))
        out_shape = SDS((K, N), f32)
    return pl.pallas_call(
        body, name=name, grid=(nn, S // ts),
        in_specs=[pl.BlockSpec((ts, K), lambda n, s: (s, 0)), pl.BlockSpec((ts, tn), lambda n, s: (s, n))],
        out_specs=out_spec, out_shape=out_shape, compiler_params=_cp("parallel", "arbitrary"),
    )(x, dy)


def mm_nn_shard(name, x, w, l):
    S, K = x.shape
    ns = w.shape[3]

    def body(x_ref, w_ref, o_ref):
        o_ref[...] = _dot(x_ref[...], w_ref[0, 0]).astype(bf16)

    return pl.pallas_call(
        body, name=name, grid=(N_CHIPS,),
        in_specs=[pl.BlockSpec((S, K), lambda j: (0, 0)), pl.BlockSpec((1, 1, K, ns), lambda j: (l, j, 0, 0))],
        out_specs=pl.BlockSpec((S, ns), lambda j: (0, j)), out_shape=SDS((S, N_CHIPS * ns), bf16),
        compiler_params=_cp("parallel"),
    )(x, w)


def loss_grad(name, y, t):
    S = y.shape[0]
    tm = _tile(S, 512)

    def body(y_ref, t_ref, dy_ref, loss_ref):
        e = y_ref[...] - t_ref[...]
        dy_ref[...] = e * (1.0 / D_MODEL)

        @pl.when(pl.program_id(0) == 0)
        def _():
            loss_ref[...] = jnp.zeros_like(loss_ref)

        loss_ref[...] += jnp.full(loss_ref.shape, (0.5 / D_MODEL) * jnp.sum(e * e), f32)

    row = lambda i: (i, 0)
    return pl.pallas_call(
        body, name=name, grid=(S // tm,),
        in_specs=[pl.BlockSpec((tm, D_MODEL), row)] * 2,
        out_specs=[pl.BlockSpec((tm, D_MODEL), row), pl.BlockSpec((8, 128), lambda i: (0, 0))],
        out_shape=[SDS((S, D_MODEL), f32), SDS((8, 128), f32)],
        compiler_params=_cp("arbitrary"),
    )(y, t)


def _half_sum(t):
    return t + pltpu.roll(t, 64, axis=1)


def mix_pre(name, xb, w_in, wq, wkv, l, gq, gkv, cs):
    S = xb.shape[0]
    tm = _tile(S, 256)
    H = MLA_HEADS
    W_EXT = w_in.shape[2]

    def body(x_ref, win_ref, wq_ref, wkv_ref, gq_ref, gkv_ref, cs_ref,
             u_ref, cq_ref, ckv_ref, cqn_ref, ckvn_ref, q_ref, k_ref, v_ref):
        h = _dot(x_ref[...], win_ref[0])
        u_ref[...] = h[:, :256]
        cq = h[:, 256:512]
        ckv = h[:, 512:640]
        cq_ref[...] = cq
        ckv_ref[...] = ckv
        cqn = (cq * lax.rsqrt(jnp.mean(cq * cq, axis=-1, keepdims=True) + RMS_EPS) * gq_ref[...]).astype(bf16)
        ckvn = (ckv * lax.rsqrt(jnp.mean(ckv * ckv, axis=-1, keepdims=True) + RMS_EPS) * gkv_ref[...]).astype(bf16)
        cqn_ref[...] = cqn
        ckvn_ref[...] = ckvn
        csv = cs_ref[...]
        lane = lax.broadcasted_iota(jnp.int32, (tm, 128), 1)
        kr = jnp.where(lane < 64, _half_sum(h[:, 640:768] * csv), 0.0).astype(bf16)
        kv = _dot(ckvn, wkv_ref[0])
        for hd in range(H):
            qe = _dot(cqn, wq_ref[0, hd])
            q_ref[hd, :, :128] = qe[:, :128].astype(bf16)
            q_ref[hd, :, 128:] = _half_sum(qe[:, 128:] * csv).astype(bf16)
            k_ref[hd, :, :128] = kv[:, 256 * hd:256 * hd + 128].astype(bf16)
            k_ref[hd, :, 128:] = kr
            v_ref[hd] = kv[:, 256 * hd + 128:256 * hd + 256].astype(bf16)

    row = lambda i: (i, 0)
    hrow = lambda i: (0, i, 0)
    return pl.pallas_call(
        body, name=name, grid=(S // tm,),
        in_specs=[pl.BlockSpec((tm, D_MODEL), row),
                  pl.BlockSpec((1, D_MODEL, W_EXT), lambda i: (l, 0, 0)),
                  pl.BlockSpec((1, H, Q_LORA, 256), lambda i: (l, 0, 0, 0)),
                  pl.BlockSpec((1, KV_LORA, H * 256), lambda i: (l, 0, 0)),
                  pl.BlockSpec((1, Q_LORA), lambda i: (0, 0)), pl.BlockSpec((1, KV_LORA), lambda i: (0, 0)),
                  pl.BlockSpec((tm, 128), row)],
        out_specs=[pl.BlockSpec((tm, 256), row), pl.BlockSpec((tm, Q_LORA), row), pl.BlockSpec((tm, KV_LORA), row),
                   pl.BlockSpec((tm, Q_LORA), row), pl.BlockSpec((tm, KV_LORA), row),
                   pl.BlockSpec((H, tm, 256), hrow), pl.BlockSpec((H, tm, 256), hrow), pl.BlockSpec((H, tm, 128), hrow)],
        out_shape=[SDS((S, 256), f32), SDS((S, Q_LORA), f32), SDS((S, KV_LORA), f32),
                   SDS((S, Q_LORA), bf16), SDS((S, KV_LORA), bf16),
                   SDS((H, S, 256), bf16), SDS((H, S, 256), bf16), SDS((H, S, 128), bf16)],
        compiler_params=_cp("parallel"),
    )(xb, w_in, wq, wkv, gq, gkv, cs)


def _group_select(col, a2, a4, a8, a16):
    return jnp.where(col < 64, a2, jnp.where(col < 128, a4, jnp.where(col < 192, a8, a16)))


def pool_fwd(name, u, wbd, scale):
    S = u.shape[0]
    tm = _tile(S, 512)
    hb = tm // HALO

    def body(u_ref, halo_ref, w_ref, s_ref, d_ref, y_ref):
        i = pl.program_id(0)
        cur = u_ref[...]
        halo = jnp.where(i > 0, halo_ref[...], 0.0)
        ext = jnp.concatenate([halo, cur], axis=0)
        s2 = ext + pltpu.roll(ext, 1, axis=0)
        s4 = s2 + pltpu.roll(s2, 2, axis=0)
        s8 = s4 + pltpu.roll(s4, 4, axis=0)
        s16 = s8 + pltpu.roll(s8, 8, axis=0)
        t1 = (i * tm + 1 + lax.broadcasted_iota(jnp.int32, (tm, 1), 0)).astype(f32)
        col = lax.broadcasted_iota(jnp.int32, (tm, 256), 1)
        m = _group_select(col, s2[HALO:] / jnp.minimum(t1, 2.0), s4[HALO:] / jnp.minimum(t1, 4.0),
                          s8[HALO:] / jnp.minimum(t1, 8.0), s16[HALO:] / jnp.minimum(t1, 16.0))
        d = (m - cur).astype(bf16)
        d_ref[...] = d
        y_ref[...] = (_dot(d, w_ref[...]) * s_ref[...]).astype(bf16)

    row = lambda i: (i, 0)
    return pl.pallas_call(
        body, name=name, grid=(S // tm,),
        in_specs=[pl.BlockSpec((tm, 256), row), pl.BlockSpec((HALO, 256), lambda i: (jnp.maximum(i * hb - 1, 0), 0)),
                  pl.BlockSpec((256, 256), lambda i: (0, 0)), pl.BlockSpec((1, 256), lambda i: (0, 0))],
        out_specs=[pl.BlockSpec((tm, 256), row)] * 2,
        out_shape=[SDS((S, 256), bf16), SDS((S, 256), bf16)],
        compiler_params=_cp("parallel"),
    )(u, u, wbd, scale)


def pool_bwd(name, dyp, d, wbd, scale):
    S = dyp.shape[0]
    tm = _tile(S, 512)
    hb = tm // HALO
    n_ext = tm + HALO

    def fwd_sum(e, steps):
        k = 1
        for _ in range(steps):
            e = e + pltpu.roll(e, n_ext - k, axis=0)
            k *= 2
        return e

    def body(dy_ref, halo_ref, d_ref, w_ref, s_ref, du_ref, dyw_ref, ds_ref):
        i = pl.program_id(0)
        sc = s_ref[...]
        w = w_ref[...]
        cur = dy_ref[...].astype(f32)
        halo = jnp.where(i < pl.num_programs(0) - 1, halo_ref[...].astype(f32), 0.0)
        dyw = jnp.concatenate([cur, halo], axis=0) * sc
        dyw_ref[...] = dyw[:tm].astype(bf16)
        dd = _dot_nt(dyw.astype(bf16), w)
        t1 = (i * tm + 1 + lax.broadcasted_iota(jnp.int32, (n_ext, 1), 0)).astype(f32)
        f2 = fwd_sum(dd / jnp.minimum(t1, 2.0), 1)
        f4 = fwd_sum(dd / jnp.minimum(t1, 4.0), 2)
        f8 = fwd_sum(dd / jnp.minimum(t1, 8.0), 3)
        f16 = fwd_sum(dd / jnp.minimum(t1, 16.0), 4)
        col = lax.broadcasted_iota(jnp.int32, (tm, 256), 1)
        du_ref[...] = (_group_select(col, f2[:tm], f4[:tm], f8[:tm], f16[:tm]) - dd[:tm]).astype(bf16)

        @pl.when(i == 0)
        def _():
            ds_ref[...] = jnp.zeros_like(ds_ref)

        ds_ref[...] += jnp.sum(cur * _dot(d_ref[...], w), axis=0, keepdims=True)

    row = lambda i: (i, 0)
    nhb = S // HALO
    return pl.pallas_call(
        body, name=name, grid=(S // tm,),
        in_specs=[pl.BlockSpec((tm, 256), row), pl.BlockSpec((HALO, 256), lambda i: (jnp.minimum((i + 1) * hb, nhb - 1), 0)),
                  pl.BlockSpec((tm, 256), row), pl.BlockSpec((256, 256), lambda i: (0, 0)),
                  pl.BlockSpec((1, 256), lambda i: (0, 0))],
        out_specs=[pl.BlockSpec((tm, 256), row), pl.BlockSpec((tm, 256), row), pl.BlockSpec((1, 256), lambda i: (0, 0))],
        out_shape=[SDS((S, 256), bf16), SDS((S, 256), bf16), SDS((1, 256), f32)],
        compiler_params=_cp("arbitrary"),
    )(dyp, dyp, d, wbd, scale)


def _diag_mask(tq):
    rc = lax.broadcasted_iota(jnp.int32, (tq, 1), 0) // 64
    cc = lax.broadcasted_iota(jnp.int32, (1, tq), 1) // 64
    return rc >= cc


MLA_SCALE_LOG2 = MLA_SCALE * math.log2(math.e)


def mla_attn_fwd(name, q, k, v):
    H, S, _ = q.shape
    tq = _tile(S, 512)
    nq = S // tq

    def body(q_ref, k_ref, v_ref, o_ref, lse_ref, m_sc, l_sc, acc_sc):
        i, j = pl.program_id(1), pl.program_id(2)

        @pl.when(j == 0)
        def _():
            m_sc[...] = jnp.full_like(m_sc, NEG_INF)
            l_sc[...] = jnp.zeros_like(l_sc)
            acc_sc[...] = jnp.zeros_like(acc_sc)

        def step(masked):
            s = _dot_nt(q_ref[0], k_ref[0])
            if masked:
                s = jnp.where(_diag_mask(tq), s, NEG_INF)
            m_prev = m_sc[...]
            m_new = jnp.maximum(m_prev, jnp.max(s, axis=-1, keepdims=True))
            p = jnp.exp2((s - m_new[:, :1]) * MLA_SCALE_LOG2)
            a = jnp.exp2((m_prev - m_new) * MLA_SCALE_LOG2)
            l_sc[...] = a * l_sc[...] + jnp.sum(p, axis=-1, keepdims=True)
            acc_sc[...] = a * acc_sc[...] + _dot(p.astype(bf16), v_ref[0])
            m_sc[...] = m_new

        @pl.when(j < i)
        def _():
            step(False)

        @pl.when(j == i)
        def _():
            step(True)
            o_ref[...] = (acc_sc[...] / l_sc[...]).astype(bf16)
            lse_ref[0] = m_sc[...] * MLA_SCALE_LOG2 + jnp.log2(l_sc[...])

    return pl.pallas_call(
        body, name=name, grid=(H, nq, nq),
        in_specs=[pl.BlockSpec((1, tq, 256), lambda h, i, j: (h, i, 0)),
                  pl.BlockSpec((1, tq, 256), lambda h, i, j: (h, jnp.minimum(i, j), 0)),
                  pl.BlockSpec((1, tq, 128), lambda h, i, j: (h, jnp.minimum(i, j), 0))],
        out_specs=[pl.BlockSpec((tq, 128), lambda h, i, j: (i, h)), pl.BlockSpec((1, tq, 128), lambda h, i, j: (h, i, 0))],
        out_shape=[SDS((S, H * 128), bf16), SDS((H, S, 128), f32)],
        scratch_shapes=[pltpu.VMEM((tq, 128), f32), pltpu.VMEM((tq, 128), f32), pltpu.VMEM((tq, 128), f32)],
        compiler_params=_cp("parallel", "parallel", "arbitrary"),
    )(q, k, v)


def mla_attn_bwd(name, q, k, v, o, do, lse):
    H, S, _ = q.shape
    tq = _tile(S, 512)
    nq = S // tq

    def body(q_ref, k_ref, v_ref, o_ref, do_ref, lse_ref, dq_ref, dk_ref, dv_ref, dk_sc, dv_sc):
        j, i = pl.program_id(1), pl.program_id(2)

        @pl.when(jnp.logical_and(j == 0, i == 0))
        def _():
            dq_ref[...] = jnp.zeros_like(dq_ref)

        @pl.when(i == j)
        def _():
            dk_sc[...] = jnp.zeros_like(dk_sc)
            dv_sc[...] = jnp.zeros_like(dv_sc)

        def step(masked):
            qv, kv_, dov = q_ref[0], k_ref[0], do_ref[...]
            s = _dot_nt(qv, kv_)
            if masked:
                s = jnp.where(_diag_mask(tq), s, NEG_INF)
            p = jnp.exp2(s * MLA_SCALE_LOG2 - lse_ref[0][:, :1])
            dv_sc[...] += _dot_tn(p.astype(bf16), dov)
            dp = _dot_nt(dov, v_ref[0])
            delta = jnp.sum(dov.astype(f32) * o_ref[...].astype(f32), axis=-1, keepdims=True)
            ds = (p * (dp - delta)).astype(bf16)
            dk_sc[...] += _dot_tn(ds, qv)
            rows = pl.ds(pl.multiple_of(i * tq, tq), tq)
            dq_ref[0, rows, :] += _dot(ds, kv_)

        @pl.when(i > j)
        def _():
            step(False)

        @pl.when(i == j)
        def _():
            step(True)

        @pl.when(i == nq - 1)
        def _():
            dk_ref[0] = dk_sc[...] * MLA_SCALE
            dv_ref[0] = dv_sc[...]

        @pl.when(jnp.logical_and(j == nq - 1, i == nq - 1))
        def _():
            dq_ref[...] = dq_ref[...] * MLA_SCALE

    qi = lambda h, j, i: (h, jnp.maximum(i, j), 0)
    kj = lambda h, j, i: (h, j, 0)
    return pl.pallas_call(
        body, name=name, grid=(H, nq, nq),
        in_specs=[pl.BlockSpec((1, tq, 256), qi), pl.BlockSpec((1, tq, 256), kj), pl.BlockSpec((1, tq, 128), kj),
                  pl.BlockSpec((tq, 128), lambda h, j, i: (jnp.maximum(i, j), h)),
                  pl.BlockSpec((tq, 128), lambda h, j, i: (jnp.maximum(i, j), h)),
                  pl.BlockSpec((1, tq, 128), qi)],
        out_specs=[pl.BlockSpec((1, S, 256), lambda h, j, i: (h, 0, 0)), pl.BlockSpec((1, tq, 256), kj),
                   pl.BlockSpec((1, tq, 128), kj)],
        out_shape=[SDS((H, S, 256), f32), SDS((H, S, 256), f32), SDS((H, S, 128), f32)],
        scratch_shapes=[pltpu.VMEM((tq, 256), f32), pltpu.VMEM((tq, 128), f32)],
        compiler_params=_cp("parallel", "arbitrary", "arbitrary"),
    )(q, k, v, o, do, lse)


def mix_post_bwd(name, dq, dk, dv, wq, wkv, l, cq, ckv, gq, gkv, cs):
    H, S, _ = dq.shape
    tm = _tile(S, 256)

    def rms_bwd(dyn, c, g):
        r = lax.rsqrt(jnp.mean(c * c, axis=-1, keepdims=True) + RMS_EPS)
        ch = c * r
        dyg = dyn * g
        dc = r * (dyg - ch * jnp.mean(dyg * ch, axis=-1, keepdims=True))
        return dc, jnp.sum(dyn * ch, axis=0, keepdims=True)

    def body(dq_ref, dk_ref, dv_ref, wq_ref, wkv_ref, cq_ref, ckv_ref, gq_ref, gkv_ref, cs_ref,
             dqe_ref, dkv_ref, dh_ref, dgq_ref, dgkv_ref):
        csv = cs_ref[...]
        lane = lax.broadcasted_iota(jnp.int32, (tm, 128), 1)
        dcqn = jnp.zeros((tm, Q_LORA), f32)
        dkr = jnp.zeros((tm, 128), f32)
        for hd in range(H):
            dqh = dq_ref[hd]
            dqe = jnp.concatenate([dqh[:, :128], _half_sum(dqh[:, 128:]) * csv], axis=1).astype(bf16)
            dqe_ref[:, 256 * hd:256 * hd + 256] = dqe
            dcqn = dcqn + _dot_nt(dqe, wq_ref[0, hd])
            dkh = dk_ref[hd]
            dkv_ref[:, 256 * hd:256 * hd + 128] = dkh[:, :128].astype(bf16)
            dkv_ref[:, 256 * hd + 128:256 * hd + 256] = dv_ref[hd].astype(bf16)
            dkr = dkr + dkh[:, 128:]
        dckvn = _dot_nt(dkv_ref[...], wkv_ref[0])
        dblk = _half_sum(jnp.where(lane < 64, dkr, 0.0)) * csv
        dcq, dgq = rms_bwd(dcqn, cq_ref[...], gq_ref[...])
        dckv, dgkv = rms_bwd(dckvn, ckv_ref[...], gkv_ref[...])
        dh_ref[:, :256] = dcq.astype(bf16)
        dh_ref[:, 256:384] = dckv.astype(bf16)
        dh_ref[:, 384:] = dblk.astype(bf16)

        @pl.when(pl.program_id(0) == 0)
        def _():
            dgq_ref[...] = jnp.zeros_like(dgq_ref)
            dgkv_ref[...] = jnp.zeros_like(dgkv_ref)

        dgq_ref[...] += dgq
        dgkv_ref[...] += dgkv

    row = lambda i: (i, 0)
    hrow = lambda i: (0, i, 0)
    return pl.pallas_call(
        body, name=name, grid=(S // tm,),
        in_specs=[pl.BlockSpec((H, tm, 256), hrow), pl.BlockSpec((H, tm, 256), hrow), pl.BlockSpec((H, tm, 128), hrow),
                  pl.BlockSpec((1, H, Q_LORA, 256), lambda i: (l, 0, 0, 0)),
                  pl.BlockSpec((1, KV_LORA, H * 256), lambda i: (l, 0, 0)),
                  pl.BlockSpec((tm, Q_LORA), row), pl.BlockSpec((tm, KV_LORA), row),
                  pl.BlockSpec((1, Q_LORA), lambda i: (0, 0)), pl.BlockSpec((1, KV_LORA), lambda i: (0, 0)),
                  pl.BlockSpec((tm, 128), row)],
        out_specs=[pl.BlockSpec((tm, H * 256), row), pl.BlockSpec((tm, H * 256), row), pl.BlockSpec((tm, 512), row),
                   pl.BlockSpec((1, Q_LORA), lambda i: (0, 0)), pl.BlockSpec((1, KV_LORA), lambda i: (0, 0))],
        out_shape=[SDS((S, H * 256), bf16), SDS((S, H * 256), bf16), SDS((S, 512), bf16),
                   SDS((1, Q_LORA), f32), SDS((1, KV_LORA), f32)],
        compiler_params=_cp("arbitrary"),
    )(dq, dk, dv, wq, wkv, cq, ckv, gq, gkv, cs)


def _cross_probs(qb, kv_ref, hd):
    cols = slice(hd * MEM_HEAD_DIM, (hd + 1) * MEM_HEAD_DIM)
    s = _dot_nt(qb[:, cols], kv_ref[:, cols]) * MEM_SCALE
    e = jnp.exp(s - jnp.max(s, axis=-1, keepdims=True))
    return e / jnp.sum(e, axis=-1, keepdims=True)


def cross_fwd(name, xb, xf, wq, wo, l, kv, g, b):
    S = xb.shape[0]
    tm = _tile(S, 256)
    M = kv.shape[0]

    def body(x_ref, xf_ref, wq_ref, wo_ref, k_ref, v_ref, g_ref, b_ref, q_ref, o_ref, z_ref, y_ref, yb_ref):
        qb = _dot(x_ref[...], wq_ref[0]).astype(bf16)
        q_ref[...] = qb
        for hd in range(MEM_HEADS):
            cols = slice(hd * MEM_HEAD_DIM, (hd + 1) * MEM_HEAD_DIM)
            p = _cross_probs(qb, k_ref, hd)
            o_ref[:, cols] = _dot(p.astype(bf16), v_ref[:, cols]).astype(bf16)
        z = ALPHA * xf_ref[...] + _dot(o_ref[...], wo_ref[0])
        mu = jnp.mean(z, axis=-1, keepdims=True)
        zc = z - mu
        var = jnp.mean(zc * zc, axis=-1, keepdims=True)
        y = zc * lax.rsqrt(var + LN_EPS) * g_ref[...] + b_ref[...]
        z_ref[...] = z
        y_ref[...] = y
        yb_ref[...] = y.astype(bf16)

    row = lambda i: (i, 0)
    wspec = pl.BlockSpec((1, D_MODEL, D_MODEL), lambda i: (l, 0, 0))
    vec = pl.BlockSpec((1, D_MODEL), lambda i: (0, 0))
    blk = pl.BlockSpec((tm, D_MODEL), row)
    return pl.pallas_call(
        body, name=name, grid=(S // tm,),
        in_specs=[blk, blk, wspec, wspec, pl.BlockSpec((M, D_MODEL), lambda i: (0, 0)),
                  pl.BlockSpec((M, D_MODEL), lambda i: (0, 1)), vec, vec],
        out_specs=[blk] * 5,
        out_shape=[SDS((S, D_MODEL), bf16), SDS((S, D_MODEL), bf16), SDS((S, D_MODEL), f32), SDS((S, D_MODEL), f32),
                   SDS((S, D_MODEL), bf16)],
        compiler_params=_cp("parallel"),
    )(xb, xf, wq, wo, kv, kv, g, b)


def cross_bwd(name, dzb, wo, l, qb, kv):
    S = dzb.shape[0]
    tm = _tile(S, 256)
    M = kv.shape[0]

    def body(dz_ref, wo_ref, q_ref, k_ref, v_ref, dq_ref, dkv_ref):
        @pl.when(pl.program_id(0) == 0)
        def _():
            dkv_ref[...] = jnp.zeros_like(dkv_ref)

        do = _dot_nt(dz_ref[...], wo_ref[0]).astype(bf16)
        qv = q_ref[...]
        for hd in range(MEM_HEADS):
            cols = slice(hd * MEM_HEAD_DIM, (hd + 1) * MEM_HEAD_DIM)
            vcols = slice(D_MODEL + hd * MEM_HEAD_DIM, D_MODEL + (hd + 1) * MEM_HEAD_DIM)
            p = _cross_probs(qv, k_ref, hd)
            doh = do[:, cols]
            dkv_ref[:, vcols] += _dot_tn(p.astype(bf16), doh)
            dp = _dot_nt(doh, v_ref[:, cols])
            ds = (p * (dp - jnp.sum(dp * p, axis=-1, keepdims=True)) * MEM_SCALE).astype(bf16)
            dq_ref[:, cols] = _dot(ds, k_ref[:, cols]).astype(bf16)
            dkv_ref[:, cols] += _dot_tn(ds, qv[:, cols])

    row = lambda i: (i, 0)
    blk = pl.BlockSpec((tm, D_MODEL), row)
    return pl.pallas_call(
        body, name=name, grid=(S // tm,),
        in_specs=[blk, pl.BlockSpec((1, D_MODEL, D_MODEL), lambda i: (l, 0, 0)), blk,
                  pl.BlockSpec((M, D_MODEL), lambda i: (0, 0)), pl.BlockSpec((M, D_MODEL), lambda i: (0, 1))],
        out_specs=[blk, pl.BlockSpec((M, 2 * D_MODEL), lambda i: (0, 0))],
        out_shape=[SDS((S, D_MODEL), bf16), SDS((M, 2 * D_MODEL), f32)],
        compiler_params=_cp("arbitrary"),
    )(dzb, wo, qb, kv, kv)


def adamw(name, w, g, m, v):
    shape = w.shape
    cols = shape[-1]
    rows = math.prod(shape[:-1])
    tr = _row_tile(rows, cols, target=2**20)
    c1 = 1.0 - ADAM_B1 ** ADAM_STEP
    c2 = 1.0 - ADAM_B2 ** ADAM_STEP

    def body(w_ref, g_ref, m_ref, v_ref, d_ref, nm_ref, nv_ref):
        gv = g_ref[...]
        nm = ADAM_B1 * m_ref[...] + (1.0 - ADAM_B1) * gv
        nv = ADAM_B2 * v_ref[...] + (1.0 - ADAM_B2) * (gv * gv)
        d_ref[...] = -ADAM_LR * ((nm / c1) / (jnp.sqrt(nv / c2) + ADAM_EPS) + ADAM_WD * w_ref[...])
        nm_ref[...] = nm
        nv_ref[...] = nv

    blk = pl.BlockSpec((tr, cols), lambda i: (i, 0))
    flat = SDS((rows, cols), f32)
    outs = pl.pallas_call(
        body, name=name, grid=(rows // tr,), in_specs=[blk] * 4, out_specs=[blk] * 3, out_shape=[flat] * 3,
        compiler_params=_cp("parallel"),
    )(*[a.reshape(rows, cols) for a in (w, g, m, v)])
    return [o.reshape(shape) for o in outs]


def _me():
    return lax.axis_index("x"), lax.axis_index("y"), lax.axis_index("c")


def _other_chips(x, y):
    return [(1 - x, y), (x, 1 - y), (1 - x, 1 - y)]


def gather_weights(shards):
    n = len(shards)

    def body(*refs):
        in_refs, out_refs = refs[:n], refs[n:2 * n]
        send_sems, recv_sems, local_sems = refs[2 * n:]
        x, y, c = _me()
        q = 2 * x + y
        sibling = (x, y, 1 - c)
        chips = _other_chips(x, y)
        started = []
        locals_ = []
        for a in range(n):
            for lyr in range(DEPTH):
                cp = pltpu.make_async_copy(in_refs[a].at[lyr], out_refs[a].at[lyr, q], local_sems.at[a * DEPTH + lyr])
                cp.start()
                locals_.append(cp)

        def rcopy(a, k, src, dst, to):
            return pltpu.make_async_remote_copy(src_ref=src, dst_ref=dst, send_sem=send_sems.at[a * 6 + k],
                                                recv_sem=recv_sems.at[a * 6 + k], device_id=to, device_id_type=MESH)

        for a in range(n):
            for k, (cx, cy) in enumerate(chips):
                cp = rcopy(a, k, in_refs[a].at[c], out_refs[a].at[c, q], (cx, cy, c))
                cp.start()
                started.append(cp)
        for a in range(n):
            for k, (cx, cy) in enumerate(chips):
                land = out_refs[a].at[c, 2 * cx + cy]
                rcopy(a, k, land, land, sibling).wait_recv()
                fwd = rcopy(a, 3 + k, land, land, sibling)
                fwd.start()
                started.append(fwd)
        for a in range(n):
            for k, (cx, cy) in enumerate(chips):
                land = out_refs[a].at[1 - c, 2 * cx + cy]
                rcopy(a, 3 + k, land, land, sibling).wait_recv()
        for cp in started:
            cp.wait_send()
        for cp in locals_:
            cp.wait()

    any_spec = pl.BlockSpec(memory_space=pl.ANY)
    return pl.pallas_call(
        body, name="gather_weights",
        in_specs=[any_spec] * n, out_specs=[any_spec] * n,
        out_shape=[SDS((DEPTH, N_CHIPS) + s.shape[1:], s.dtype) for s in shards],
        scratch_shapes=[pltpu.SemaphoreType.DMA((6 * n,)), pltpu.SemaphoreType.DMA((6 * n,)),
                        pltpu.SemaphoreType.DMA((DEPTH * n,))],
    )(*shards)


def pair_send_other_layer(grads):
    n = len(grads)

    def body(*refs):
        in_refs, out_refs = refs[:n], refs[n:2 * n]
        send_sems, recv_sems = refs[2 * n:]
        x, y, c = _me()
        cps = [pltpu.make_async_remote_copy(src_ref=in_refs[a].at[1 - c], dst_ref=out_refs[a], send_sem=send_sems.at[a],
                                            recv_sem=recv_sems.at[a], device_id=(x, y, 1 - c), device_id_type=MESH)
               for a in range(n)]
        for cp in cps:
            cp.start()
        for cp in cps:
            cp.wait()

    any_spec = pl.BlockSpec(memory_space=pl.ANY)
    return pl.pallas_call(
        body, name="pair_send_other_layer", in_specs=[any_spec] * n, out_specs=[any_spec] * n,
        out_shape=[SDS(g.shape[1:], f32) for g in grads],
        scratch_shapes=[pltpu.SemaphoreType.DMA((n,)), pltpu.SemaphoreType.DMA((n,))],
    )(*grads)


def pair_add(name, g, recv, c_arr):
    shard = g.shape[2:]
    cols = shard[-1]
    rows = N_CHIPS * math.prod(shard[:-1])
    tr = _row_tile(rows, cols)

    def body(c_ref, g_ref, r_ref, o_ref):
        o_ref[...] = (g_ref[0] + r_ref[...]).astype(bf16)

    out = pl.pallas_call(
        body, name=name,
        grid_spec=pltpu.PrefetchScalarGridSpec(
            num_scalar_prefetch=1, grid=(rows // tr,),
            in_specs=[pl.BlockSpec((1, tr, cols), lambda i, c_ref: (c_ref[0], i, 0)),
                      pl.BlockSpec((tr, cols), lambda i, c_ref: (i, 0))],
            out_specs=pl.BlockSpec((tr, cols), lambda i, c_ref: (i, 0))),
        out_shape=SDS((rows, cols), bf16), compiler_params=_cp("parallel"),
    )(c_arr, g.reshape(DEPTH, rows, cols), recv.reshape(rows, cols))
    return out.reshape((N_CHIPS,) + shard)


def chip_exchange(psums):
    n = len(psums)

    def body(*refs):
        in_refs, out_refs = refs[:n], refs[n:2 * n]
        send_sems, recv_sems = refs[2 * n:]
        x, y, c = _me()
        chips = _other_chips(x, y)
        cps = [pltpu.make_async_remote_copy(src_ref=in_refs[a].at[2 * cx + cy], dst_ref=out_refs[a].at[k],
                                            send_sem=send_sems.at[3 * a + k], recv_sem=recv_sems.at[3 * a + k],
                                            device_id=(cx, cy, c), device_id_type=MESH)
               for a in range(n) for k, (cx, cy) in enumerate(chips)]
        for cp in cps:
            cp.start()
        for cp in cps:
            cp.wait()

    any_spec = pl.BlockSpec(memory_space=pl.ANY)
    return pl.pallas_call(
        body, name="chip_exchange", in_specs=[any_spec] * n, out_specs=[any_spec] * n,
        out_shape=[SDS((3,) + p.shape[1:], bf16) for p in psums],
        scratch_shapes=[pltpu.SemaphoreType.DMA((3 * n,)), pltpu.SemaphoreType.DMA((3 * n,))],
    )(*psums)


def chip_add(name, psum, recv, qc_arr):
    shard = psum.shape[1:]
    cols = shard[-1]
    rows = math.prod(shard[:-1])
    tr = _row_tile(rows, cols)

    def body(qc_ref, p_ref, r_ref, o_ref):
        o_ref[0] = ((p_ref[0].astype(f32) + r_ref[0].astype(f32)) + r_ref[1].astype(f32)) + r_ref[2].astype(f32)

    out = pl.pallas_call(
        body, name=name,
        grid_spec=pltpu.PrefetchScalarGridSpec(
            num_scalar_prefetch=1, grid=(rows // tr,),
            in_specs=[pl.BlockSpec((1, tr, cols), lambda i, qc_ref: (qc_ref[0], i, 0)),
                      pl.BlockSpec((3, tr, cols), lambda i, qc_ref: (0, i, 0))],
            out_specs=pl.BlockSpec((1, tr, cols), lambda i, qc_ref: (qc_ref[1], i, 0))),
        out_shape=SDS((DEPTH, rows, cols), f32), compiler_params=_cp("parallel"),
    )(qc_arr, psum.reshape(N_CHIPS, rows, cols), recv.reshape(3, rows, cols))
    return out.reshape((DEPTH,) + shard)


def pair_share(sums):
    n = len(sums)

    def body(*refs):
        out_refs = refs[n:2 * n]
        send_sems, recv_sems = refs[2 * n:]
        x, y, c = _me()
        sibling = (x, y, 1 - c)
        cps = [pltpu.make_async_remote_copy(src_ref=out_refs[a].at[c], dst_ref=out_refs[a].at[c], send_sem=send_sems.at[a],
                                            recv_sem=recv_sems.at[a], device_id=sibling, device_id_type=MESH)
               for a in range(n)]
        for cp in cps:
            cp.start()
        for cp in cps:
            cp.wait_send()
        for a in range(n):
            land = out_refs[a].at[1 - c]
            pltpu.make_async_remote_copy(src_ref=land, dst_ref=land, send_sem=send_sems.at[a], recv_sem=recv_sems.at[a],
                                         device_id=sibling, device_id_type=MESH).wait_recv()

    any_spec = pl.BlockSpec(memory_space=pl.ANY)
    return pl.pallas_call(
        body, name="pair_share", in_specs=[any_spec] * n, out_specs=[any_spec] * n,
        out_shape=[SDS(s.shape, f32) for s in sums], input_output_aliases={a: a for a in range(n)},
        scratch_shapes=[pltpu.SemaphoreType.DMA((n,)), pltpu.SemaphoreType.DMA((n,))],
    )(*sums)


def allsum_small(name, v):
    R = v.shape[0]

    def body(v_ref, o_ref, all_ref, send_sems, recv_sems, local_sem):
        x, y, c = _me()
        me, sibling = (x, y, c), (x, y, 1 - c)
        chips = _other_chips(x, y)

        def rows(px, py, pc):
            return all_ref.at[4 * px + 2 * py + pc]

        def copy(k, block, to, src=None):
            return pltpu.make_async_remote_copy(
                src_ref=rows(*block) if src is None else src, dst_ref=rows(*block),
                send_sem=send_sems.at[k], recv_sem=recv_sems.at[k], device_id=to, device_id_type=MESH)

        mine = pltpu.make_async_copy(v_ref, rows(*me), local_sem)
        mine.start()
        first = [copy(0, me, sibling, src=v_ref)]
        first += [copy(1 + j, me, (*chip, c), src=v_ref) for j, chip in enumerate(chips)]
        for cp in first:
            cp.start()
        passed = [copy(4 + j, (*chip, c), sibling) for j, chip in enumerate(chips)]
        for j, chip in enumerate(chips):
            copy(1 + j, (*chip, c), me).wait_recv()
            passed[j].start()
        copy(0, sibling, me).wait_recv()
        for j, chip in enumerate(chips):
            copy(4 + j, (*chip, 1 - c), me).wait_recv()
        for cp in first + passed:
            cp.wait_send()
        mine.wait()
        acc = all_ref[0]
        for d in range(1, 8):
            acc = acc + all_ref[d]
        o_ref[...] = acc

    return pl.pallas_call(
        body, name=name,
        in_specs=[pl.BlockSpec(memory_space=pltpu.VMEM)], out_specs=pl.BlockSpec(memory_space=pltpu.VMEM),
        out_shape=SDS((R, 128), f32),
        scratch_shapes=[pltpu.VMEM((8, R, 128), f32), pltpu.SemaphoreType.DMA((7,)), pltpu.SemaphoreType.DMA((7,)),
                        pltpu.SemaphoreType.DMA],
        compiler_params=pltpu.CompilerParams(vmem_limit_bytes=V7X_VMEM_LIMIT),
    )(v)


def reduce_grads(grads):
    x, y, c = _me()
    c_arr = jnp.reshape(c, (1,)).astype(jnp.int32)
    qc_arr = jnp.stack([2 * x + y, c]).astype(jnp.int32)
    recv = pair_send_other_layer(grads)
    psums = [pair_add(f"pair_add_{a}", g, r, c_arr) for a, (g, r) in enumerate(zip(grads, recv))]
    got = chip_exchange(psums)
    sums = [chip_add(f"chip_add_{a}", p, r, qc_arr) for a, (p, r) in enumerate(zip(psums, got))]
    return pair_share(sums)


def _swap_half(r):
    return jnp.concatenate([-r[..., 32:], r[..., :32]], axis=-1)


def _unswap_add(p, qg):
    return p + jnp.concatenate([qg[..., 32:], -qg[..., :32]], axis=-1)


def _block_diag(pw):
    L = pw.shape[0]
    out = jnp.zeros((L, 256, 256), pw.dtype)
    for gi in range(4):
        out = out.at[:, 64 * gi:64 * gi + 64, 64 * gi:64 * gi + 64].set(pw[:, gi])
    return out


def _to_col_shards(w):
    *lead, K, N = w.shape
    nl = len(lead)
    return w.reshape(*lead, K, N_CHIPS, N // N_CHIPS).transpose(*range(nl), nl + 1, nl, nl + 2)


def _from_col_shards(w):
    *lead, C, K, n = w.shape
    nl = len(lead)
    return w.transpose(*range(nl), nl + 1, nl, nl + 2).reshape(*lead, K, C * n)


def _step_serial_comm(x, mem, positions, ln_g, ln_b, ffn1_w13, ffn1_w2, w_in, pool_w, pool_scale, q_norm_g, w_uq, kv_norm_g, w_ukv, w_out, mem_wq, mem_wkv, mem_wo, ffn2_w13, ffn2_w2, loss_target, m_ln_g, m_ln_b, m_ffn1_w13, m_ffn1_w2, m_w_in, m_pool_w, m_pool_scale, m_q_norm_g, m_w_uq, m_kv_norm_g, m_w_ukv, m_w_out, m_mem_wq, m_mem_wkv, m_mem_wo, m_ffn2_w13, m_ffn2_w2, v_ln_g, v_ln_b, v_ffn1_w13, v_ffn1_w2, v_w_in, v_pool_w, v_pool_scale, v_q_norm_g, v_w_uq, v_kv_norm_g, v_w_ukv, v_w_out, v_mem_wq, v_mem_wkv, v_mem_wo, v_ffn2_w13, v_ffn2_w2):
    L = DEPTH
    S = x.shape[1]
    qx, qy, _ = _me()
    chip = 2 * qx + qy

    big = [ffn1_w13, ffn1_w2, w_in, w_uq, w_ukv, w_out, mem_wq, mem_wkv, mem_wo, ffn2_w13, ffn2_w2]
    (g_f1w13, g_f1w2, g_win, g_wuq, g_wukv, g_wout, g_mwq, g_mwkv, g_mwo, g_f2w13, g_f2w2) = gather_weights(
        [w.astype(bf16) for w in big])
    f1w2 = g_f1w2.reshape(L, D_FF, D_MODEL)
    f2w2 = g_f2w2.reshape(L, D_FF, D_MODEL)
    win = g_win.reshape(L, D_MODEL, D_IN)
    win_ext = jnp.concatenate([win, _swap_half(win[..., D_IN - QK_ROPE:])], axis=-1)
    wuq = _from_col_shards(g_wuq).reshape(L, Q_LORA, MLA_HEADS, QK_NOPE + QK_ROPE)
    wq_ext = jnp.concatenate([wuq, _swap_half(wuq[..., QK_NOPE:])], axis=-1).transpose(0, 2, 1, 3)
    wukv = _from_col_shards(g_wukv)
    wout = g_wout.reshape(L, D_MODEL, D_MODEL)
    wout_pool, wout_mla = wout[:, :POOL_WIDTH], wout[:, POOL_WIDTH:]
    mwq = g_mwq.reshape(L, D_MODEL, D_MODEL)
    mwo = g_mwo.reshape(L, D_MODEL, D_MODEL)
    wbd = _block_diag(pool_w.astype(bf16))

    ln_pad = jnp.zeros((2, L, 4, N_CHIPS, D_MODEL // N_CHIPS), f32)
    ln_pad = lax.dynamic_update_slice(ln_pad, jnp.stack([ln_g, ln_b])[:, :, :, None, :], (0, 0, 0, chip, 0))
    ln_full = allsum_small("allsum_ln", ln_pad.reshape(-1, 128)) * 0.5
    ln_full = ln_full.reshape(2, L, 4, D_MODEL)
    lng, lnb = ln_full[0], ln_full[1]

    half = QK_ROPE // 2
    inv_freq = ROPE_BASE ** (-jnp.arange(half, dtype=f32) / half)
    ang = positions[0].astype(f32)[:, None] * inv_freq
    cos, sin = jnp.cos(ang), jnp.sin(ang)
    cs = jnp.concatenate([cos, cos, sin, sin], axis=-1)

    memb = mem[0].astype(bf16)
    xf = x[0]
    xb = xf.astype(bf16)
    vec = lambda a: a.reshape(1, -1)

    saved = []
    for l in range(L):
        sv = {}
        sv["x0b"] = xb
        gate, up, act = ffn_up(f"ffn1_up_{l}", xb, g_f1w13, l)
        z1, x1f, x1b = proj_res_ln(f"ffn1_down_{l}", [act], [f1w2], [l], xf, vec(lng[l, 0]), vec(lnb[l, 0]), 0.5)
        sv.update(gate1=gate, up1=up, act1=act, z1=z1, x1b=x1b)
        u, cq, ckv, cqn, ckvn, q, k, v = mix_pre(f"mix_pre_{l}", x1b, win_ext, wq_ext, wukv, l,
                                                   vec(q_norm_g[l]), vec(kv_norm_g[l]), cs)
        dpool, ypool = pool_fwd(f"pool_fwd_{l}", u, wbd[l], vec(pool_scale[l]))
        o, lse = mla_attn_fwd(f"mla_fwd_{l}", q, k, v)
        z2, x2f, x2b = proj_res_ln(f"mix_out_{l}", [ypool, o], [wout_pool, wout_mla], [l, l], x1f,
                                   vec(lng[l, 1]), vec(lnb[l, 1]), 1.0)
        sv.update(cq=cq, ckv=ckv, cqn=cqn, ckvn=ckvn, q=q, k=k, v=v, dpool=dpool, ypool=ypool, o=o, lse=lse, z2=z2, x2b=x2b)
        kvm = mm_nn_shard(f"mem_kv_{l}", memb, g_mwkv, l)
        cq_, co_, z3, x3f, x3b = cross_fwd(f"cross_fwd_{l}", x2b, x2f, mwq, mwo, l, kvm, vec(lng[l, 2]), vec(lnb[l, 2]))
        sv.update(kvm=kvm, crq=cq_, cro=co_, z3=z3, x3b=x3b)
        gate, up, act = ffn_up(f"ffn2_up_{l}", x3b, g_f2w13, l)
        z4, xf, xb = proj_res_ln(f"ffn2_down_{l}", [act], [f2w2], [l], x3f, vec(lng[l, 3]), vec(lnb[l, 3]), 0.5)
        sv.update(gate2=gate, up2=up, act2=act, z4=z4)
        saved.append(sv)

    dy, loss_blk = loss_grad("loss_grad", xf, loss_target[0])
    loss = lax.psum(loss_blk[0, 0], ("x", "y", "c"))

    G = dict(f1w13=None, f1w2=None, mwq=None, mwkv=None, mwo=None, f2w13=None, f2w2=None)
    small = {k_: [None] * L for k_ in ("win", "wuq", "wukv", "wout", "pool_w", "pool_scale", "gq", "gkv", "lng", "lnb")}
    for l in reversed(range(L)):
        sv = saved[l]
        dlg, dlb = [None] * 4, [None] * 4
        dzb, dres, dlg[3], dlb[3] = ln_bwd(f"ln4_bwd_{l}", dy, sv["z4"], vec(lng[l, 3]), 0.5)
        dh = ffn_bwd_da(f"ffn2_bwd_da_{l}", dzb, f2w2, l, sv["gate2"], sv["up2"])
        G["f2w2"] = mm_tn(f"ffn2_dw2_{l}", sv["act2"], dzb, "nat", l, G["f2w2"])
        G["f2w13"] = mm_tn(f"ffn2_dw13_{l}", sv["x3b"], dh, "shard", l, G["f2w13"])
        dy = ffn_dx(f"ffn2_dx_{l}", dh, g_f2w13, l, dres)
        dzb, dres, dlg[2], dlb[2] = ln_bwd(f"ln3_bwd_{l}", dy, sv["z3"], vec(lng[l, 2]), 1.0)
        dqc, dkvm = cross_bwd(f"cross_bwd_{l}", dzb, mwo, l, sv["crq"], sv["kvm"])
        G["mwo"] = mm_tn(f"cross_dwo_{l}", sv["cro"], dzb, "nat", l, G["mwo"])
        G["mwq"] = mm_tn(f"cross_dwq_{l}", sv["x2b"], dqc, "nat", l, G["mwq"])
        G["mwkv"] = mm_tn(f"cross_dwkv_{l}", memb, dkvm, "shard", l, G["mwkv"])
        dy = mm_nt_res(f"cross_dx_{l}", [dqc], [mwq], [l], dres, f32)
        dzb, dres, dlg[1], dlb[1] = ln_bwd(f"ln2_bwd_{l}", dy, sv["z2"], vec(lng[l, 1]), 1.0)
        dyp = mm_nt_res(f"mix_dpool_{l}", [dzb], [wout_pool], [l], None, bf16)
        do = mm_nt_res(f"mix_do_{l}", [dzb], [wout_mla], [l], None, bf16)
        dwo_p = mm_tn(f"mix_dwout_pool_{l}", sv["ypool"], dzb)
        dwo_m = mm_tn(f"mix_dwout_mla_{l}", sv["o"], dzb)
        small["wout"][l] = jnp.concatenate([dwo_p, dwo_m], axis=0)
        dq, dk, dv = mla_attn_bwd(f"mla_bwd_{l}", sv["q"], sv["k"], sv["v"], sv["o"], do, sv["lse"])
        dqe, dkv, dh_rest, dgq, dgkv = mix_post_bwd(f"mix_post_bwd_{l}", dq, dk, dv, wq_ext, wukv, l, sv["cq"], sv["ckv"],
                                                     vec(q_norm_g[l]), vec(kv_norm_g[l]), cs)
        du, dyw, dscale = pool_bwd(f"pool_bwd_{l}", dyp, sv["dpool"], wbd[l], vec(pool_scale[l]))
        dwq_e = mm_tn(f"mix_dwuq_{l}", sv["cqn"], dqe).reshape(Q_LORA, MLA_HEADS, 256)
        small["wuq"][l] = jnp.concatenate(
            [dwq_e[..., :QK_NOPE], _unswap_add(dwq_e[..., QK_NOPE:QK_NOPE + QK_ROPE], dwq_e[..., QK_NOPE + QK_ROPE:])],
            axis=-1).reshape(Q_LORA, MLA_HEADS * (QK_NOPE + QK_ROPE))
        small["wukv"][l] = mm_tn(f"mix_dwukv_{l}", sv["ckvn"], dkv)
        dwbd = mm_tn(f"pool_dw_{l}", sv["dpool"], dyw)
        small["pool_w"][l] = jnp.stack([dwbd[64 * gi:64 * gi + 64, 64 * gi:64 * gi + 64] for gi in range(4)])
        small["pool_scale"][l], small["gq"][l], small["gkv"][l] = dscale[0], dgq[0], dgkv[0]
        dh_ext = jnp.concatenate([du, dh_rest], axis=1)
        dwin_e = mm_tn(f"mix_dwin_{l}", sv["x1b"], dh_ext)
        small["win"][l] = jnp.concatenate(
            [dwin_e[:, :D_IN - QK_ROPE], _unswap_add(dwin_e[:, D_IN - QK_ROPE:D_IN], dwin_e[:, D_IN:])], axis=-1)
        dy = mm_nt_res(f"mix_dx_{l}", [dh_ext], [win_ext], [l], dres, f32)
        dzb, dres, dlg[0], dlb[0] = ln_bwd(f"ln1_bwd_{l}", dy, sv["z1"], vec(lng[l, 0]), 0.5)
        dh = ffn_bwd_da(f"ffn1_bwd_da_{l}", dzb, f1w2, l, sv["gate1"], sv["up1"])
        G["f1w2"] = mm_tn(f"ffn1_dw2_{l}", sv["act1"], dzb, "nat", l, G["f1w2"])
        G["f1w13"] = mm_tn(f"ffn1_dw13_{l}", sv["x0b"], dh, "shard", l, G["f1w13"])
        dy = ffn_dx(f"ffn1_dx_{l}", dh, g_f1w13, l, dres)
        small["lng"][l] = jnp.concatenate(dlg, axis=0)
        small["lnb"][l] = jnp.concatenate(dlb, axis=0)
    grad_x = dy[None]

    row_shards = lambda a, K: a.reshape(L, N_CHIPS, K // N_CHIPS, a.shape[-1])
    g_list = [G["f1w13"], row_shards(G["f1w2"], D_FF),
              jnp.stack(small["win"]).reshape(L, N_CHIPS, D_MODEL // N_CHIPS, D_IN),
              _to_col_shards(jnp.stack(small["wuq"])), _to_col_shards(jnp.stack(small["wukv"])),
              jnp.stack(small["wout"]).reshape(L, N_CHIPS, D_MODEL // N_CHIPS, D_MODEL),
              row_shards(G["mwq"], D_MODEL), G["mwkv"], row_shards(G["mwo"], D_MODEL),
              G["f2w13"], row_shards(G["f2w2"], D_FF)]
    big_grads = reduce_grads(g_list)

    rep = [jnp.stack(small["pool_w"]).reshape(-1), jnp.stack(small["pool_scale"]).reshape(-1),
           jnp.stack(small["gq"]).reshape(-1), jnp.stack(small["gkv"]).reshape(-1),
           jnp.stack(small["lng"]).reshape(-1), jnp.stack(small["lnb"]).reshape(-1)]
    sizes = [r.shape[0] for r in rep]
    packed = jnp.concatenate(rep)
    pad = (-packed.shape[0]) % 1024
    tot = allsum_small("allsum_small_grads", jnp.pad(packed, (0, pad)).reshape(-1, 128)).reshape(-1)
    offs = [0]
    for s_ in sizes:
        offs.append(offs[-1] + s_)
    parts = [tot[offs[i]:offs[i + 1]] for i in range(len(sizes))]
    g_pool_w = parts[0].reshape(pool_w.shape)
    g_pool_scale = parts[1].reshape(pool_scale.shape)
    g_gq = parts[2].reshape(q_norm_g.shape)
    g_gkv = parts[3].reshape(kv_norm_g.shape)
    shard_cols = lambda a: lax.dynamic_slice_in_dim(a.reshape(L, 4, D_MODEL), chip * (D_MODEL // N_CHIPS),
                                                    D_MODEL // N_CHIPS, axis=2)
    g_lng, g_lnb = shard_cols(parts[4]), shard_cols(parts[5])

    (gf1w13, gf1w2, gwin, gwuq, gwukv, gwout, gmwq, gmwkv, gmwo, gf2w13, gf2w2) = big_grads
    grads = [g_lng, g_lnb, gf1w13, gf1w2, gwin, g_pool_w, g_pool_scale, g_gq, gwuq, g_gkv, gwukv, gwout, gmwq, gmwkv,
             gmwo, gf2w13, gf2w2]
    ws = [ln_g, ln_b, ffn1_w13, ffn1_w2, w_in, pool_w, pool_scale, q_norm_g, w_uq, kv_norm_g, w_ukv, w_out, mem_wq,
          mem_wkv, mem_wo, ffn2_w13, ffn2_w2]
    ms = [m_ln_g, m_ln_b, m_ffn1_w13, m_ffn1_w2, m_w_in, m_pool_w, m_pool_scale, m_q_norm_g, m_w_uq, m_kv_norm_g, m_w_ukv,
          m_w_out, m_mem_wq, m_mem_wkv, m_mem_wo, m_ffn2_w13, m_ffn2_w2]
    vs = [v_ln_g, v_ln_b, v_ffn1_w13, v_ffn1_w2, v_w_in, v_pool_w, v_pool_scale, v_q_norm_g, v_w_uq, v_kv_norm_g, v_w_ukv,
          v_w_out, v_mem_wq, v_mem_wkv, v_mem_wo, v_ffn2_w13, v_ffn2_w2]
    deltas, new_ms, new_vs = [], [], []
    for a, (w_, g_, m_, v_) in enumerate(zip(ws, grads, ms, vs)):
        d_, nm_, nv_ = adamw(f"adamw_{a}", w_, g_.reshape(w_.shape), m_, v_)
        deltas.append(d_)
        new_ms.append(nm_)
        new_vs.append(nv_)
    grads = [g_.reshape(w_.shape) for g_, w_ in zip(grads, ws)]
    return (loss, grad_x, *grads, *deltas, *new_ms, *new_vs)


_HBM_SPEC = pl.BlockSpec(memory_space=pltpu.HBM)
_SEM_SPEC = pl.BlockSpec(memory_space=pltpu.SEMAPHORE)
_ANY_SPEC = pl.BlockSpec(memory_space=pl.ANY)
_DATAFLOW = pltpu.SideEffectType.DATAFLOW_SIDE_EFFECTING


def _split_call(name, body_fn, bufs, sems_in, sems_out_sizes, after):
    nb, ni, no = len(bufs), len(sems_in), len(sems_out_sizes)
    has_after = after is not None

    def body(*refs):
        k = nb + ni + (1 if has_after else 0)
        body_fn(refs[:nb], refs[nb:nb + ni], refs[k:k + no])
        refs[-1][...] = jnp.zeros((8, 128), f32)

    outs = pl.pallas_call(
        body, name=name,
        in_specs=[_HBM_SPEC] * nb + [_SEM_SPEC] * ni + ([_ANY_SPEC] if has_after else []),
        out_specs=[_SEM_SPEC] * no + [_HBM_SPEC] * nb + [pl.BlockSpec(memory_space=pltpu.VMEM)],
        out_shape=[pltpu.SemaphoreType.DMA((s,)) for s in sems_out_sizes]
        + [pltpu.HBM(b.shape, b.dtype) for b in bufs] + [SDS((8, 128), f32)],
        input_output_aliases={i: no + i for i in range(nb)},
        compiler_params=pltpu.CompilerParams(has_side_effects=_DATAFLOW),
    )(*[pltpu.with_memory_space_constraint(b, pltpu.HBM) for b in bufs], *sems_in, *([after] if has_after else []))
    return list(outs[no:no + nb]), list(outs[:no]), outs[-1]


def _rcopy(src, dst, ssem, rsem, to):
    return pltpu.make_async_remote_copy(src_ref=src, dst_ref=dst, send_sem=ssem, recv_sem=rsem, device_id=to,
                                        device_id_type=MESH)


def gather_start(name, groups, after):
    flat = [b for bufs, _ in groups for b in bufs]
    sizes = [3 * len(bufs) for bufs, _ in groups for _ in range(2)]

    def body_fn(b_in, s_in, s_out):
        x, y, c = _me()
        q = 2 * x + y
        chips = _other_chips(x, y)
        pos = 0
        for gi, (bufs, owner) in enumerate(groups):
            refs = b_in[pos:pos + len(bufs)]
            pos += len(bufs)

            @pl.when(c == owner)
            def _(refs=refs, send=s_out[2 * gi], recv=s_out[2 * gi + 1]):
                for a, r in enumerate(refs):
                    for k, (cx, cy) in enumerate(chips):
                        _rcopy(r.at[q], r.at[q], send.at[3 * a + k], recv.at[3 * a + k], (cx, cy, c)).start()

    outs, sems, token = _split_call(name, body_fn, flat, [], sizes, after)
    res, pos = [], 0
    for gi, (bufs, owner) in enumerate(groups):
        res.append((outs[pos:pos + len(bufs)], sems[2 * gi], sems[2 * gi + 1], owner))
        pos += len(bufs)
    return res, token


def gather_forward(name, grp, after):
    bufs, send, recv, owner = grp
    n3 = 3 * len(bufs)

    def body_fn(b_in, s_in, s_out):
        x, y, c = _me()
        q = 2 * x + y
        sibling = (x, y, 1 - c)
        chips = _other_chips(x, y)

        @pl.when(c == owner)
        def _():
            for a, r in enumerate(b_in):
                for k, (cx, cy) in enumerate(chips):
                    i = 3 * a + k
                    land = r.at[2 * cx + cy]
                    _rcopy(r.at[q], r.at[q], s_in[0].at[i], s_in[1].at[i], (cx, cy, c)).wait_send()
                    _rcopy(land, land, s_in[0].at[i], s_in[1].at[i], (cx, cy, c)).wait_recv()
                    _rcopy(land, land, s_out[0].at[i], s_out[1].at[i], sibling).start()

    outs, sems, token = _split_call(name, body_fn, bufs, [send, recv], [n3, n3], after)
    return (outs, sems[0], sems[1], owner), token


def gather_finish(name, grp, after):
    bufs, fsend, frecv, owner = grp

    def body_fn(b_in, s_in, s_out):
        x, y, c = _me()
        sibling = (x, y, 1 - c)
        chips = _other_chips(x, y)

        def each(wait):
            for a, r in enumerate(b_in):
                for k, (cx, cy) in enumerate(chips):
                    land = r.at[2 * cx + cy]
                    wait(_rcopy(land, land, s_in[0].at[3 * a + k], s_in[1].at[3 * a + k], sibling))

        @pl.when(c == owner)
        def _():
            each(lambda cp: cp.wait_send())

        @pl.when(c != owner)
        def _():
            each(lambda cp: cp.wait_recv())

    outs, _, _ = _split_call(name, body_fn, bufs, [fsend, frecv], [], after)
    return outs


def pair_send_start(name, gs, owner, after):
    n = len(gs)
    lands = [lax.empty(g.shape, g.dtype) for g in gs]

    def body_fn(b_in, s_in, s_out):
        x, y, c = _me()

        @pl.when(c == 1 - owner)
        def _():
            for a in range(n):
                _rcopy(b_in[a], b_in[n + a], s_out[0].at[a], s_out[1].at[a], (x, y, owner)).start()

    outs, sems, token = _split_call(name, body_fn, list(gs) + lands, [], [n, n], after)
    return (outs[:n], outs[n:], sems[0], sems[1], owner), token


def pair_send_wait(name, st, after):
    gs, lands, send, recv, owner = st
    n = len(gs)

    def body_fn(b_in, s_in, s_out):
        x, y, c = _me()

        @pl.when(c == 1 - owner)
        def _():
            for a in range(n):
                _rcopy(b_in[a], b_in[n + a], s_in[0].at[a], s_in[1].at[a], (x, y, owner)).wait_send()

        @pl.when(c == owner)
        def _():
            for a in range(n):
                _rcopy(b_in[a], b_in[n + a], s_in[0].at[a], s_in[1].at[a], (x, y, 1 - owner)).wait_recv()

    outs, _, _ = _split_call(name, body_fn, list(gs) + list(lands), [send, recv], [], after)
    return outs[:n], outs[n:]


def chip_exchange_start(name, psums, owner, after):
    n = len(psums)
    lands = [lax.empty((3,) + p.shape[1:], p.dtype) for p in psums]

    def body_fn(b_in, s_in, s_out):
        x, y, c = _me()
        chips = _other_chips(x, y)

        @pl.when(c == owner)
        def _():
            for a in range(n):
                for k, (cx, cy) in enumerate(chips):
                    _rcopy(b_in[a].at[2 * cx + cy], b_in[n + a].at[k], s_out[0].at[3 * a + k], s_out[1].at[3 * a + k],
                           (cx, cy, c)).start()

    outs, sems, token = _split_call(name, body_fn, list(psums) + lands, [], [3 * n, 3 * n], after)
    return (outs[:n], outs[n:], sems[0], sems[1], owner), token


def chip_exchange_wait(name, st, after):
    psums, lands, send, recv, owner = st
    n = len(psums)

    def body_fn(b_in, s_in, s_out):
        x, y, c = _me()
        chips = _other_chips(x, y)

        @pl.when(c == owner)
        def _():
            for a in range(n):
                for k, (cx, cy) in enumerate(chips):
                    cp = _rcopy(b_in[a].at[2 * cx + cy], b_in[n + a].at[k], s_in[0].at[3 * a + k], s_in[1].at[3 * a + k],
                                (cx, cy, c))
                    cp.wait_send()
                    cp.wait_recv()

    outs, _, _ = _split_call(name, body_fn, list(psums) + list(lands), [send, recv], [], after)
    return outs[:n], outs[n:]


def pair_sum(name, g, recv):
    shape = g.shape
    cols = shape[-1]
    rows = math.prod(shape[:-1])
    tr = _row_tile(rows, cols)

    def body(g_ref, r_ref, o_ref):
        o_ref[...] = (g_ref[...] + r_ref[...]).astype(bf16)

    blk = pl.BlockSpec((tr, cols), lambda i: (i, 0))
    out = pl.pallas_call(
        body, name=name, grid=(rows // tr,), in_specs=[blk, blk], out_specs=blk, out_shape=SDS((rows, cols), bf16),
        compiler_params=_cp("parallel"),
    )(g.reshape(rows, cols), recv.reshape(rows, cols))
    return out.reshape(shape)


def chip_sum(name, psum, recv, q_arr, layer, prev):
    shard = psum.shape[1:]
    cols = shard[-1]
    rows = math.prod(shard[:-1])
    tr = _row_tile(rows, cols)

    def body(q_ref, p_ref, r_ref, *rest):
        rest[-1][0] = ((p_ref[0].astype(f32) + r_ref[0].astype(f32)) + r_ref[1].astype(f32)) + r_ref[2].astype(f32)

    in_specs = [pl.BlockSpec((1, tr, cols), lambda i, q_ref: (q_ref[0], i, 0)),
                pl.BlockSpec((3, tr, cols), lambda i, q_ref: (0, i, 0))]
    args = [q_arr, psum.reshape(N_CHIPS, rows, cols), recv.reshape(3, rows, cols)]
    aliases = {}
    if prev is not None:
        in_specs.append(pl.BlockSpec(memory_space=pl.ANY))
        args.append(prev.reshape(DEPTH, rows, cols))
        aliases = {3: 0}
    out = pl.pallas_call(
        body, name=name,
        grid_spec=pltpu.PrefetchScalarGridSpec(
            num_scalar_prefetch=1, grid=(rows // tr,), in_specs=in_specs,
            out_specs=pl.BlockSpec((1, tr, cols), lambda i, q_ref: (layer, i, 0))),
        out_shape=SDS((DEPTH, rows, cols), f32), input_output_aliases=aliases, compiler_params=_cp("parallel"),
    )(*args)
    return out.reshape((DEPTH,) + shard)


def _after(x, *deps):
    return lax.optimization_barrier((x, *deps))[0]


W_NAMES = ("f1w13", "f1w2", "win", "wuq", "wukv", "wout", "mwq", "mwkv", "mwo", "f2w13", "f2w2")
MIX_NAMES = ("win", "wuq", "wukv")
MID_NAMES = ("wout", "mwq", "mwkv", "mwo")
FFN2_NAMES = ("f2w13", "f2w2")


def kernel(x, mem, positions, ln_g, ln_b, ffn1_w13, ffn1_w2, w_in, pool_w, pool_scale, q_norm_g, w_uq, kv_norm_g, w_ukv, w_out, mem_wq, mem_wkv, mem_wo, ffn2_w13, ffn2_w2, loss_target, m_ln_g, m_ln_b, m_ffn1_w13, m_ffn1_w2, m_w_in, m_pool_w, m_pool_scale, m_q_norm_g, m_w_uq, m_kv_norm_g, m_w_ukv, m_w_out, m_mem_wq, m_mem_wkv, m_mem_wo, m_ffn2_w13, m_ffn2_w2, v_ln_g, v_ln_b, v_ffn1_w13, v_ffn1_w2, v_w_in, v_pool_w, v_pool_scale, v_q_norm_g, v_w_uq, v_kv_norm_g, v_w_ukv, v_w_out, v_mem_wq, v_mem_wkv, v_mem_wo, v_ffn2_w13, v_ffn2_w2):
    L = DEPTH
    qx, qy, _ = _me()
    chip = 2 * qx + qy
    vec = lambda a: a.reshape(1, -1)

    shards = dict(zip(W_NAMES, (ffn1_w13, ffn1_w2, w_in, w_uq, w_ukv, w_out, mem_wq, mem_wkv, mem_wo, ffn2_w13, ffn2_w2)))

    def place(sh):
        return lax.dynamic_update_slice(jnp.zeros((N_CHIPS,) + sh.shape, bf16), sh.astype(bf16)[None],
                                        (chip,) + (0,) * sh.ndim)

    bufs = [{n: place(shards[n][l]) for n in W_NAMES} for l in range(L)]
    gw = [dict(), dict()]
    (g0,), tok = gather_start("gather_a_start", [([bufs[0]["f1w13"], bufs[0]["f1w2"]], 0)], None)
    g0, tok = gather_forward("gather_a_forward", g0, None)
    gw[0]["f1w13"], gw[0]["f1w2"] = gather_finish("gather_a_finish", g0, None)
    (g_mix, g_mid, g_ffn2, g_l1), tok_b = gather_start(
        "gather_b_start",
        [([bufs[0][n] for n in MIX_NAMES], 0), ([bufs[0][n] for n in MID_NAMES], 0), ([bufs[0][n] for n in FFN2_NAMES], 0),
         ([bufs[1][n] for n in W_NAMES], 1)], tok)

    ln_pad = jnp.zeros((2, L, 4, N_CHIPS, D_MODEL // N_CHIPS), f32)
    ln_pad = lax.dynamic_update_slice(ln_pad, jnp.stack([ln_g, ln_b])[:, :, :, None, :], (0, 0, 0, chip, 0))
    ln_full = allsum_small("allsum_ln", ln_pad.reshape(-1, 128)) * 0.5
    ln_full = ln_full.reshape(2, L, 4, D_MODEL)
    lng, lnb = ln_full[0], ln_full[1]

    half = QK_ROPE // 2
    inv_freq = ROPE_BASE ** (-jnp.arange(half, dtype=f32) / half)
    ang = positions[0].astype(f32)[:, None] * inv_freq
    cos, sin = jnp.cos(ang), jnp.sin(ang)
    cs = jnp.concatenate([cos, cos, sin, sin], axis=-1)

    memb = mem[0].astype(bf16)
    xf = x[0]
    xb = _after(xf.astype(bf16), tok_b)

    saved, W = [], [None, None]
    for l in range(L):
        sv = {}
        if l == 1:
            gl1 = gather_finish("gather_l1_finish", g_l1, xb)
            gw[1] = dict(zip(W_NAMES, gl1))
        sv["x0b"] = xb
        f1w13 = gw[l]["f1w13"][None]
        gate, up, act = ffn_up(f"ffn1_up_{l}", xb, f1w13, 0)
        if l == 0:
            g_mix, _ = gather_forward("gather_mix_forward", g_mix, act)
        z1, x1f, x1b = proj_res_ln(f"ffn1_down_{l}", [act], [gw[l]["f1w2"].reshape(1, D_FF, D_MODEL)], [0], xf,
                                   vec(lng[l, 0]), vec(lnb[l, 0]), 0.5)
        sv.update(gate1=gate, up1=up, act1=act, z1=z1, x1b=x1b)
        if l == 0:
            gw[0].update(zip(MIX_NAMES, gather_finish("gather_mix_finish", g_mix, x1b)))
            g_mid, _ = gather_forward("gather_mid_forward", g_mid, x1b)
        win = gw[l]["win"].reshape(D_MODEL, D_IN)
        win_ext = jnp.concatenate([win, _swap_half(win[:, D_IN - QK_ROPE:])], axis=-1)[None]
        wuq = _from_col_shards(gw[l]["wuq"]).reshape(Q_LORA, MLA_HEADS, QK_NOPE + QK_ROPE)
        wq_ext = jnp.concatenate([wuq, _swap_half(wuq[..., QK_NOPE:])], axis=-1).transpose(1, 0, 2)[None]
        wukv = _from_col_shards(gw[l]["wukv"])[None]
        wbd = _block_diag(pool_w[l][None].astype(bf16))[0]
        u, cq, ckv, cqn, ckvn, q, k, v = mix_pre(f"mix_pre_{l}", x1b, win_ext, wq_ext, wukv, 0,
                                                   vec(q_norm_g[l]), vec(kv_norm_g[l]), cs)
        dpool, ypool = pool_fwd(f"pool_fwd_{l}", u, wbd, vec(pool_scale[l]))
        o, lse = mla_attn_fwd(f"mla_fwd_{l}", q, k, v)
        if l == 0:
            gw[0].update(zip(MID_NAMES, gather_finish("gather_mid_finish", g_mid, o)))
            g_ffn2, tok_f = gather_forward("gather_ffn2_forward", g_ffn2, o)
            g_l1, tok_l = gather_forward("gather_l1_forward", g_l1, o)
            o = _after(o, tok_f, tok_l)
        wout = gw[l]["wout"].reshape(D_MODEL, D_MODEL)
        wout_pool, wout_mla = wout[None, :POOL_WIDTH], wout[None, POOL_WIDTH:]
        mwq = gw[l]["mwq"].reshape(1, D_MODEL, D_MODEL)
        mwo = gw[l]["mwo"].reshape(1, D_MODEL, D_MODEL)
        mwkv = gw[l]["mwkv"][None]
        z2, x2f, x2b = proj_res_ln(f"mix_out_{l}", [ypool, o], [wout_pool, wout_mla], [0, 0], x1f,
                                   vec(lng[l, 1]), vec(lnb[l, 1]), 1.0)
        sv.update(cq=cq, ckv=ckv, cqn=cqn, ckvn=ckvn, q=q, k=k, v=v, dpool=dpool, ypool=ypool, o=o, lse=lse, z2=z2, x2b=x2b)
        kvm = mm_nn_shard(f"mem_kv_{l}", memb, mwkv, 0)
        cq_, co_, z3, x3f, x3b = cross_fwd(f"cross_fwd_{l}", x2b, x2f, mwq, mwo, 0, kvm, vec(lng[l, 2]), vec(lnb[l, 2]))
        sv.update(kvm=kvm, crq=cq_, cro=co_, z3=z3, x3b=x3b)
        if l == 0:
            gw[0].update(zip(FFN2_NAMES, gather_finish("gather_ffn2_finish", g_ffn2, x3b)))
        f2w13 = gw[l]["f2w13"][None]
        f2w2 = gw[l]["f2w2"].reshape(1, D_FF, D_MODEL)
        gate, up, act = ffn_up(f"ffn2_up_{l}", x3b, f2w13, 0)
        z4, xf, xb = proj_res_ln(f"ffn2_down_{l}", [act], [f2w2], [0], x3f, vec(lng[l, 3]), vec(lnb[l, 3]), 0.5)
        sv.update(gate2=gate, up2=up, act2=act, z4=z4)
        W[l] = dict(f1w13=f1w13, f1w2=gw[l]["f1w2"].reshape(1, D_FF, D_MODEL), win_ext=win_ext, wq_ext=wq_ext, wukv=wukv,
                    wbd=wbd, wout_pool=wout_pool, wout_mla=wout_mla, mwq=mwq, mwo=mwo, f2w13=f2w13, f2w2=f2w2)
        saved.append(sv)

    dy, loss_blk = loss_grad("loss_grad", xf, loss_target[0])
    loss = lax.psum(loss_blk[0, 0], ("x", "y", "c"))

    row_shards = lambda a: a.reshape(N_CHIPS, a.shape[0] // N_CHIPS, a.shape[1])
    small = {k_: [None] * L for k_ in ("pool_w", "pool_scale", "gq", "gkv", "lng", "lnb")}
    q_arr = jnp.reshape(chip, (1,)).astype(jnp.int32)
    rest_names = [n for n in W_NAMES if n not in ("f1w13", "f1w2")]

    def red_begin(tag, gs, owner):
        return pair_send_start(f"pair_send_start_{tag}", gs, owner, None)

    def red_mid(tag, st, owner, after):
        gs_, lands_ = pair_send_wait(f"pair_send_wait_{tag}", st, after)
        ps = [pair_sum(f"pair_sum_{tag}_{a}", g_, r_) for a, (g_, r_) in enumerate(zip(gs_, lands_))]
        return chip_exchange_start(f"chip_exchange_start_{tag}", ps, owner, None)

    def red_end(tag, st, layer, prevs, after):
        ps, lands_ = chip_exchange_wait(f"chip_exchange_wait_{tag}", st, after)
        return [chip_sum(f"chip_sum_{tag}_{a}", p_, r_, q_arr, layer, s_)
                for a, (p_, r_, s_) in enumerate(zip(ps, lands_, prevs))]

    st_p1 = st_c1 = st_pa = st_ca = None
    for l in reversed(range(L)):
        sv, w = saved[l], W[l]
        g = {}
        dlg, dlb = [None] * 4, [None] * 4
        dzb, dres, dlg[3], dlb[3] = ln_bwd(f"ln4_bwd_{l}", dy, sv["z4"], vec(lng[l, 3]), 0.5)
        dh = ffn_bwd_da(f"ffn2_bwd_da_{l}", dzb, w["f2w2"], 0, sv["gate2"], sv["up2"])
        g["f2w2"] = row_shards(mm_tn(f"ffn2_dw2_{l}", sv["act2"], dzb))
        g["f2w13"] = mm_tn(f"ffn2_dw13_{l}", sv["x3b"], dh, True)
        dy = ffn_dx(f"ffn2_dx_{l}", dh, w["f2w13"], 0, dres)
        if l == 0:
            st_c1, tok = red_mid("l1", st_p1, 1, dy)
            dy = _after(dy, tok)
        dzb, dres, dlg[2], dlb[2] = ln_bwd(f"ln3_bwd_{l}", dy, sv["z3"], vec(lng[l, 2]), 1.0)
        dqc, dkvm = cross_bwd(f"cross_bwd_{l}", dzb, w["mwo"], 0, sv["crq"], sv["kvm"])
        g["mwo"] = row_shards(mm_tn(f"cross_dwo_{l}", sv["cro"], dzb))
        g["mwq"] = row_shards(mm_tn(f"cross_dwq_{l}", sv["x2b"], dqc))
        g["mwkv"] = mm_tn(f"cross_dwkv_{l}", memb, dkvm, True)
        dy = mm_nt_res(f"cross_dx_{l}", [dqc], [w["mwq"]], [0], dres, f32)
        dzb, dres, dlg[1], dlb[1] = ln_bwd(f"ln2_bwd_{l}", dy, sv["z2"], vec(lng[l, 1]), 1.0)
        dyp = mm_nt_res(f"mix_dpool_{l}", [dzb], [w["wout_pool"]], [0], None, bf16)
        do = mm_nt_res(f"mix_do_{l}", [dzb], [w["wout_mla"]], [0], None, bf16)
        dwo_p = mm_tn(f"mix_dwout_pool_{l}", sv["ypool"], dzb)
        dwo_m = mm_tn(f"mix_dwout_mla_{l}", sv["o"], dzb)
        g["wout"] = row_shards(jnp.concatenate([dwo_p, dwo_m], axis=0))
        dq, dk, dv = mla_attn_bwd(f"mla_bwd_{l}", sv["q"], sv["k"], sv["v"], sv["o"], do, sv["lse"])
        dqe, dkv, dh_rest, dgq, dgkv = mix_post_bwd(f"mix_post_bwd_{l}", dq, dk, dv, w["wq_ext"], w["wukv"], 0, sv["cq"],
                                                     sv["ckv"], vec(q_norm_g[l]), vec(kv_norm_g[l]), cs)
        du, dyw, dscale = pool_bwd(f"pool_bwd_{l}", dyp, sv["dpool"], w["wbd"], vec(pool_scale[l]))
        dwq_e = mm_tn(f"mix_dwuq_{l}", sv["cqn"], dqe).reshape(Q_LORA, MLA_HEADS, 256)
        g["wuq"] = _to_col_shards(jnp.concatenate(
            [dwq_e[..., :QK_NOPE], _unswap_add(dwq_e[..., QK_NOPE:QK_NOPE + QK_ROPE], dwq_e[..., QK_NOPE + QK_ROPE:])],
            axis=-1).reshape(Q_LORA, MLA_HEADS * (QK_NOPE + QK_ROPE)))
        g["wukv"] = _to_col_shards(mm_tn(f"mix_dwukv_{l}", sv["ckvn"], dkv))
        dwbd = mm_tn(f"pool_dw_{l}", sv["dpool"], dyw)
        small["pool_w"][l] = jnp.stack([dwbd[64 * gi:64 * gi + 64, 64 * gi:64 * gi + 64] for gi in range(4)])
        small["pool_scale"][l], small["gq"][l], small["gkv"][l] = dscale[0], dgq[0], dgkv[0]
        dh_ext = jnp.concatenate([du, dh_rest], axis=1)
        dwin_e = mm_tn(f"mix_dwin_{l}", sv["x1b"], dh_ext)
        g["win"] = row_shards(jnp.concatenate(
            [dwin_e[:, :D_IN - QK_ROPE], _unswap_add(dwin_e[:, D_IN - QK_ROPE:D_IN], dwin_e[:, D_IN:])], axis=-1))
        dy = mm_nt_res(f"mix_dx_{l}", [dh_ext], [w["win_ext"]], [0], dres, f32)
        if l == 0:
            st_pa, tok = red_begin("a0", [g[n] for n in rest_names], 0)
            dy = _after(dy, tok)
        dzb, dres, dlg[0], dlb[0] = ln_bwd(f"ln1_bwd_{l}", dy, sv["z1"], vec(lng[l, 0]), 0.5)
        dh = ffn_bwd_da(f"ffn1_bwd_da_{l}", dzb, w["f1w2"], 0, sv["gate1"], sv["up1"])
        dy = ffn_dx(f"ffn1_dx_{l}", dh, w["f1w13"], 0, dres)
        if l == 0:
            st_ca, tok = red_mid("a0", st_pa, 0, dy)
            dzb, dh = _after((dzb, dh), tok)
        g["f1w2"] = row_shards(mm_tn(f"ffn1_dw2_{l}", sv["act1"], dzb))
        g["f1w13"] = mm_tn(f"ffn1_dw13_{l}", sv["x0b"], dh, True)
        small["lng"][l] = jnp.concatenate(dlg, axis=0)
        small["lnb"][l] = jnp.concatenate(dlb, axis=0)
        if l == 1:
            st_p1, tok = red_begin("l1", [g[n] for n in W_NAMES], 1)
            dy = _after(dy, tok)
    grad_x = dy[None]

    st_pb, _ = red_begin("b0", [g["f1w13"], g["f1w2"]], 0)
    sums1 = dict(zip(W_NAMES, red_end("l1", st_c1, 1, [None] * len(W_NAMES), g["f1w13"])))
    st_cb, _ = red_mid("b0", st_pb, 0, sums1["f1w13"])
    sums0 = dict(zip(rest_names, red_end("a0", st_ca, 0, [sums1[n] for n in rest_names], sums1["f1w2"])))
    sums0["f1w13"], sums0["f1w2"] = red_end("b0", st_cb, 0, [sums1["f1w13"], sums1["f1w2"]], sums0["f2w2"])
    big_grads = pair_share([sums0[n] for n in W_NAMES])

    rep = [jnp.stack(small["pool_w"]).reshape(-1), jnp.stack(small["pool_scale"]).reshape(-1),
           jnp.stack(small["gq"]).reshape(-1), jnp.stack(small["gkv"]).reshape(-1),
           jnp.stack(small["lng"]).reshape(-1), jnp.stack(small["lnb"]).reshape(-1)]
    sizes = [r.shape[0] for r in rep]
    packed = jnp.concatenate(rep)
    pad = (-packed.shape[0]) % 1024
    tot = allsum_small("allsum_small_grads", jnp.pad(packed, (0, pad)).reshape(-1, 128)).reshape(-1)
    offs = [0]
    for s_ in sizes:
        offs.append(offs[-1] + s_)
    parts = [tot[offs[i]:offs[i + 1]] for i in range(len(sizes))]
    g_pool_w = parts[0].reshape(pool_w.shape)
    g_pool_scale = parts[1].reshape(pool_scale.shape)
    g_gq = parts[2].reshape(q_norm_g.shape)
    g_gkv = parts[3].reshape(kv_norm_g.shape)
    shard_cols = lambda a: lax.dynamic_slice_in_dim(a.reshape(L, 4, D_MODEL), chip * (D_MODEL // N_CHIPS),
                                                    D_MODEL // N_CHIPS, axis=2)
    g_lng, g_lnb = shard_cols(parts[4]), shard_cols(parts[5])

    (gf1w13, gf1w2, gwin, gwuq, gwukv, gwout, gmwq, gmwkv, gmwo, gf2w13, gf2w2) = big_grads
    grads = [g_lng, g_lnb, gf1w13, gf1w2, gwin, g_pool_w, g_pool_scale, g_gq, gwuq, g_gkv, gwukv, gwout, gmwq, gmwkv,
             gmwo, gf2w13, gf2w2]
    ws = [ln_g, ln_b, ffn1_w13, ffn1_w2, w_in, pool_w, pool_scale, q_norm_g, w_uq, kv_norm_g, w_ukv, w_out, mem_wq,
          mem_wkv, mem_wo, ffn2_w13, ffn2_w2]
    ms = [m_ln_g, m_ln_b, m_ffn1_w13, m_ffn1_w2, m_w_in, m_pool_w, m_pool_scale, m_q_norm_g, m_w_uq, m_kv_norm_g, m_w_ukv,
          m_w_out, m_mem_wq, m_mem_wkv, m_mem_wo, m_ffn2_w13, m_ffn2_w2]
    vs = [v_ln_g, v_ln_b, v_ffn1_w13, v_ffn1_w2, v_w_in, v_pool_w, v_pool_scale, v_q_norm_g, v_w_uq, v_kv_norm_g, v_w_ukv,
          v_w_out, v_mem_wq, v_mem_wkv, v_mem_wo, v_ffn2_w13, v_ffn2_w2]
    deltas, new_ms, new_vs = [], [], []
    for a, (w_, g_, m_, v_) in enumerate(zip(ws, grads, ms, vs)):
        d_, nm_, nv_ = adamw(f"adamw_{a}", w_, g_.reshape(w_.shape), m_, v_)
        deltas.append(d_)
        new_ms.append(nm_)
        new_vs.append(nv_)
    grads = [g_.reshape(w_.shape) for g_, w_ in zip(grads, ws)]
    return (loss, grad_x, *grads, *deltas, *new_ms, *new_vs)
```

```python
import functools
import math

import jax
import jax.numpy as jnp
from jax import lax
from jax.experimental import pallas as pl
from jax.experimental.pallas import tpu as pltpu

f32 = jnp.float32
bf16 = jnp.bfloat16
SDS = jax.ShapeDtypeStruct
MESH = pl.DeviceIdType.MESH

D_MODEL = 1024
DEPTH = 2
N_MEM = 256
MEM_HEADS = 4
MEM_HEAD_DIM = D_MODEL // MEM_HEADS
POOL_WINDOWS = (2, 4, 8, 16)
POOL_WIDTH = 256
POOL_GROUP = 64
QK_NOPE = 128
QK_ROPE = 64
V_HEAD = 128
MLA_HEADS = 6
Q_LORA = 256
KV_LORA = 128
ROPE_BASE = 10000.0
D_FF = 2816
D_IN = POOL_WIDTH + Q_LORA + KV_LORA + QK_ROPE
ALPHA = (2 * DEPTH) ** 0.25
LN_EPS = 1e-5
RMS_EPS = 1e-6
NEG_INF = -1e30
MLA_SCALE = (QK_NOPE + QK_ROPE) ** -0.5
MEM_SCALE = MEM_HEAD_DIM ** -0.5
ADAM_LR = 0.001
ADAM_B1 = 0.9
ADAM_B2 = 0.999
ADAM_EPS = 1e-08
ADAM_WD = 0.01
ADAM_STEP = 10

N_CHIPS = 4
V7X_VMEM_LIMIT = 56 * 2**20
HALO = 16

_NT = (((1,), (1,)), ((), ()))
_TN = (((0,), (0,)), ((), ()))


def _dot(a, b):
    return jnp.dot(a, b, preferred_element_type=f32)


def _dot_nt(a, b):
    return lax.dot_general(a, b, _NT, preferred_element_type=f32)


def _dot_tn(a, b):
    return lax.dot_general(a, b, _TN, preferred_element_type=f32)


def _cp(*sem):
    return pltpu.CompilerParams(dimension_semantics=sem if sem else None, vmem_limit_bytes=V7X_VMEM_LIMIT)


_DEP_SPEC = pl.BlockSpec(memory_space=pl.ANY)


def _with_deps(body, n_in, deps):
    nd = len(deps)
    if not nd:
        return body

    def wrapped(*refs):
        return body(*refs[:n_in], *refs[n_in + nd:])

    return wrapped


def _tile(n, t):
    t = min(n, t)
    assert n % t == 0, (n, t)
    return t


def _row_tile(rows, cols, itemsize=4, target=2 * 2**20):
    best = None
    for t in range(16, rows + 1, 16):
        if rows % t == 0 and t * cols * itemsize <= target:
            best = t
    return best if best is not None else rows


def ffn_up(name, xb, w13, l, deps=()):
    S = xb.shape[0]
    ns = w13.shape[3]
    tm = _tile(S, 512)

    def body(x_ref, wg_ref, wu_ref, g_ref, u_ref, a_ref):
        x = x_ref[...]
        g = _dot(x, wg_ref[0, 0])
        u = _dot(x, wu_ref[0, 0])
        a = g * jax.nn.sigmoid(g) * u
        g_ref[...] = g.astype(bf16)
        u_ref[...] = u.astype(bf16)
        a_ref[...] = a.astype(bf16)

    out = SDS((S, 2 * ns), bf16)
    return pl.pallas_call(
        _with_deps(body, 3, deps), name=name, grid=(2, S // tm),
        in_specs=[pl.BlockSpec((tm, D_MODEL), lambda j, i: (i, 0)),
                  pl.BlockSpec((1, 1, D_MODEL, ns), lambda j, i: (l, j, 0, 0)),
                  pl.BlockSpec((1, 1, D_MODEL, ns), lambda j, i: (l, j + 2, 0, 0))] + [_DEP_SPEC] * len(deps),
        out_specs=[pl.BlockSpec((tm, ns), lambda j, i: (i, j))] * 3,
        out_shape=[out, out, out],
        compiler_params=_cp("parallel", "parallel"),
    )(xb, w13, w13, *deps)


def proj_res_ln(name, parts, ws, wl, x, g, b, rscale, deps=()):
    S = x.shape[0]
    tm = _tile(S, 256)
    n = len(parts)

    def body(*refs):
        p_refs, w_refs = refs[:n], refs[n:2 * n]
        x_ref, g_ref, b_ref, z_ref, y_ref, yb_ref = refs[2 * n:]
        acc = _dot(p_refs[0][...], w_refs[0][0])
        for k in range(1, n):
            acc = acc + _dot(p_refs[k][...], w_refs[k][0])
        if rscale != 1.0:
            acc = rscale * acc
        z = ALPHA * x_ref[...] + acc
        mu = jnp.mean(z, axis=-1, keepdims=True)
        zc = z - mu
        var = jnp.mean(zc * zc, axis=-1, keepdims=True)
        y = zc * lax.rsqrt(var + LN_EPS) * g_ref[...] + b_ref[...]
        z_ref[...] = z
        y_ref[...] = y
        yb_ref[...] = y.astype(bf16)

    row = lambda i: (i, 0)
    in_specs = [pl.BlockSpec((tm, p.shape[1]), row) for p in parts]
    in_specs += [pl.BlockSpec((1,) + w.shape[1:], functools.partial(lambda li, i: (li, 0, 0), li)) for w, li in zip(ws, wl)]
    in_specs += [pl.BlockSpec((tm, D_MODEL), row), pl.BlockSpec((1, D_MODEL), lambda i: (0, 0)),
                 pl.BlockSpec((1, D_MODEL), lambda i: (0, 0))] + [_DEP_SPEC] * len(deps)
    return pl.pallas_call(
        _with_deps(body, 2 * n + 3, deps), name=name, grid=(S // tm,), in_specs=in_specs,
        out_specs=[pl.BlockSpec((tm, D_MODEL), row)] * 3,
        out_shape=[SDS((S, D_MODEL), f32), SDS((S, D_MODEL), f32), SDS((S, D_MODEL), bf16)],
        compiler_params=_cp("parallel"),
    )(*parts, *ws, x, g, b, *deps)


def ln_bwd(name, dy, z, g, rscale, deps=()):
    S = dy.shape[0]
    tm = _tile(S, 512)

    def body(dy_ref, z_ref, g_ref, dzb_ref, dres_ref, dg_ref, db_ref):
        z = z_ref[...]
        mu = jnp.mean(z, axis=-1, keepdims=True)
        zc = z - mu
        rstd = lax.rsqrt(jnp.mean(zc * zc, axis=-1, keepdims=True) + LN_EPS)
        xhat = zc * rstd
        dyv = dy_ref[...]
        dxh = dyv * g_ref[...]
        m1 = jnp.mean(dxh, axis=-1, keepdims=True)
        m2 = jnp.mean(dxh * xhat, axis=-1, keepdims=True)
        dz = rstd * (dxh - m1 - xhat * m2)
        dzb_ref[...] = (rscale * dz).astype(bf16)
        dres_ref[...] = ALPHA * dz

        @pl.when(pl.program_id(0) == 0)
        def _():
            dg_ref[...] = jnp.zeros_like(dg_ref)
            db_ref[...] = jnp.zeros_like(db_ref)

        dg_ref[...] += jnp.sum(dyv * xhat, axis=0, keepdims=True)
        db_ref[...] += jnp.sum(dyv, axis=0, keepdims=True)

    row = lambda i: (i, 0)
    vec = pl.BlockSpec((1, D_MODEL), lambda i: (0, 0))
    return pl.pallas_call(
        _with_deps(body, 3, deps), name=name, grid=(S // tm,),
        in_specs=[pl.BlockSpec((tm, D_MODEL), row), pl.BlockSpec((tm, D_MODEL), row), vec] + [_DEP_SPEC] * len(deps),
        out_specs=[pl.BlockSpec((tm, D_MODEL), row), pl.BlockSpec((tm, D_MODEL), row), vec, vec],
        out_shape=[SDS((S, D_MODEL), bf16), SDS((S, D_MODEL), f32), SDS((1, D_MODEL), f32), SDS((1, D_MODEL), f32)],
        compiler_params=_cp("arbitrary"),
    )(dy, z, g, *deps)


def ffn_bwd_da(name, drb, w2, l, gate, up):
    S = drb.shape[0]
    tm = _tile(S, 256)
    nh = D_FF // 2

    def body(dr_ref, w_ref, g_ref, u_ref, dh_ref):
        dr = dr_ref[...]
        for j in range(2):
            cols = slice(j * nh, (j + 1) * nh)
            da = _dot_nt(dr, w_ref[0, cols, :])
            g = g_ref[:, cols].astype(f32)
            u = u_ref[:, cols].astype(f32)
            sg = jax.nn.sigmoid(g)
            dh_ref[:, cols] = (da * u * (sg * (1.0 + g * (1.0 - sg)))).astype(bf16)
            dh_ref[:, D_FF + j * nh:D_FF + (j + 1) * nh] = (da * (g * sg)).astype(bf16)

    row = lambda i: (i, 0)
    return pl.pallas_call(
        body, name=name, grid=(S // tm,),
        in_specs=[pl.BlockSpec((tm, D_MODEL), row), pl.BlockSpec((1, D_FF, D_MODEL), lambda i: (l, 0, 0)),
                  pl.BlockSpec((tm, D_FF), row), pl.BlockSpec((tm, D_FF), row)],
        out_specs=pl.BlockSpec((tm, 2 * D_FF), row),
        out_shape=SDS((S, 2 * D_FF), bf16),
        compiler_params=_cp("parallel"),
    )(drb, w2, gate, up)


def ffn_dx(name, dh, w13, l, res):
    S = dh.shape[0]
    ns = w13.shape[3]
    tm = _tile(S, 1024)

    def body(dh_ref, w_ref, r_ref, o_ref):
        @pl.when(pl.program_id(1) == 0)
        def _():
            o_ref[...] = r_ref[...]

        o_ref[...] += _dot_nt(dh_ref[...], w_ref[0, 0])

    return pl.pallas_call(
        body, name=name, grid=(S // tm, N_CHIPS),
        in_specs=[pl.BlockSpec((tm, ns), lambda i, j: (i, j)),
                  pl.BlockSpec((1, 1, D_MODEL, ns), lambda i, j: (l, j, 0, 0)),
                  pl.BlockSpec((tm, D_MODEL), lambda i, j: (i, 0))],
        out_specs=pl.BlockSpec((tm, D_MODEL), lambda i, j: (i, 0)),
        out_shape=SDS((S, D_MODEL), f32),
        compiler_params=_cp("parallel", "arbitrary"),
    )(dh, w13, res)


def mm_nt_res(name, dys, ws, wl, res, out_dtype):
    S = dys[0].shape[0]
    K = ws[0].shape[1]
    tm = _tile(S, 512)
    n = len(dys)

    def body(*refs):
        dy_refs, w_refs = refs[:n], refs[n:2 * n]
        o_ref = refs[-1]
        acc = _dot_nt(dy_refs[0][...], w_refs[0][0])
        for k in range(1, n):
            acc = acc + _dot_nt(dy_refs[k][...], w_refs[k][0])
        if res is not None:
            acc = acc + refs[2 * n][...]
        o_ref[...] = acc.astype(out_dtype)

    row = lambda i: (i, 0)
    in_specs = [pl.BlockSpec((tm, d.shape[1]), row) for d in dys]
    in_specs += [pl.BlockSpec((1,) + w.shape[1:], functools.partial(lambda li, i: (li, 0, 0), li)) for w, li in zip(ws, wl)]
    args = list(dys) + list(ws)
    if res is not None:
        in_specs.append(pl.BlockSpec((tm, K), row))
        args.append(res)
    return pl.pallas_call(
        body, name=name, grid=(S // tm,), in_specs=in_specs,
        out_specs=pl.BlockSpec((tm, K), row), out_shape=SDS((S, K), out_dtype),
        compiler_params=_cp("parallel"),
    )(*args)


def mm_tn(name, x, dy, col_shards=False, deps=()):
    S, K = x.shape
    N = dy.shape[1]
    ts = _tile(S, 512)
    if col_shards:
        tn = N // N_CHIPS
    else:
        tn = N
        while K * tn * 4 > 6 * 2**20 and tn % 256 == 0:
            tn //= 2
    nn = N // tn
    lead = ((0,) if col_shards else ()) + (slice(None), slice(None))

    def body(x_ref, dy_ref, o_ref):
        acc = _dot_tn(x_ref[...].astype(bf16), dy_ref[...].astype(bf16))

        @pl.when(pl.program_id(1) == 0)
        def _():
            o_ref[lead] = acc

        @pl.when(pl.program_id(1) != 0)
        def _():
            o_ref[lead] += acc

    if col_shards:
        out_spec = pl.BlockSpec((1, K, tn), lambda n, s: (n, 0, 0))
        out_shape = SDS((N_CHIPS, K, tn), f32)
    else:
        out_spec = pl.BlockSpec((K, tn), lambda n, s: (0, n))
        out_shape = SDS((K, N), f32)
    return pl.pallas_call(
        _with_deps(body, 2, deps), name=name, grid=(nn, S // ts),
        in_specs=[pl.BlockSpec((ts, K), lambda n, s: (s, 0)), pl.BlockSpec((ts, tn), lambda n, s: (s, n))]
        + [_DEP_SPEC] * len(deps),
        out_specs=out_spec, out_shape=out_shape, compiler_params=_cp("parallel", "arbitrary"),
    )(x, dy, *deps)


def mm_nn_shard(name, x, w, l):
    S, K = x.shape
    ns = w.shape[3]

    def body(x_ref, w_ref, o_ref):
        o_ref[...] = _dot(x_ref[...], w_ref[0, 0]).astype(bf16)

    return pl.pallas_call(
        body, name=name, grid=(N_CHIPS,),
        in_specs=[pl.BlockSpec((S, K), lambda j: (0, 0)), pl.BlockSpec((1, 1, K, ns), lambda j: (l, j, 0, 0))],
        out_specs=pl.BlockSpec((S, ns), lambda j: (0, j)), out_shape=SDS((S, N_CHIPS * ns), bf16),
        compiler_params=_cp("parallel"),
    )(x, w)


def loss_grad(name, y, t):
    S = y.shape[0]
    tm = _tile(S, 512)

    def body(y_ref, t_ref, dy_ref, loss_ref):
        e = y_ref[...] - t_ref[...]
        dy_ref[...] = e * (1.0 / D_MODEL)

        @pl.when(pl.program_id(0) == 0)
        def _():
            loss_ref[...] = jnp.zeros_like(loss_ref)

        loss_ref[...] += jnp.full(loss_ref.shape, (0.5 / D_MODEL) * jnp.sum(e * e), f32)

    row = lambda i: (i, 0)
    return pl.pallas_call(
        body, name=name, grid=(S // tm,),
        in_specs=[pl.BlockSpec((tm, D_MODEL), row)] * 2,
        out_specs=[pl.BlockSpec((tm, D_MODEL), row), pl.BlockSpec((8, 128), lambda i: (0, 0))],
        out_shape=[SDS((S, D_MODEL), f32), SDS((8, 128), f32)],
        compiler_params=_cp("arbitrary"),
    )(y, t)


def _half_sum(t):
    return t + pltpu.roll(t, 64, axis=1)


def mix_pre(name, xb, w_in, wq, wkv, l, gq, gkv, cs):
    S = xb.shape[0]
    tm = _tile(S, 256)
    H = MLA_HEADS
    W_EXT = w_in.shape[2]

    def body(x_ref, win_ref, wq_ref, wkv_ref, gq_ref, gkv_ref, cs_ref,
             u_ref, cq_ref, ckv_ref, cqn_ref, ckvn_ref, q_ref, k_ref, v_ref):
        h = _dot(x_ref[...], win_ref[0])
        u_ref[...] = h[:, :256]
        cq = h[:, 256:512]
        ckv = h[:, 512:640]
        cq_ref[...] = cq
        ckv_ref[...] = ckv
        cqn = (cq * lax.rsqrt(jnp.mean(cq * cq, axis=-1, keepdims=True) + RMS_EPS) * gq_ref[...]).astype(bf16)
        ckvn = (ckv * lax.rsqrt(jnp.mean(ckv * ckv, axis=-1, keepdims=True) + RMS_EPS) * gkv_ref[...]).astype(bf16)
        cqn_ref[...] = cqn
        ckvn_ref[...] = ckvn
        csv = cs_ref[...]
        lane = lax.broadcasted_iota(jnp.int32, (tm, 128), 1)
        kr = jnp.where(lane < 64, _half_sum(h[:, 640:768] * csv), 0.0).astype(bf16)
        kv = _dot(ckvn, wkv_ref[0])
        for hd in range(H):
            qe = _dot(cqn, wq_ref[0, hd])
            q_ref[hd, :, :128] = qe[:, :128].astype(bf16)
            q_ref[hd, :, 128:] = _half_sum(qe[:, 128:] * csv).astype(bf16)
            k_ref[hd, :, :128] = kv[:, 256 * hd:256 * hd + 128].astype(bf16)
            k_ref[hd, :, 128:] = kr
            v_ref[hd] = kv[:, 256 * hd + 128:256 * hd + 256].astype(bf16)

    row = lambda i: (i, 0)
    hrow = lambda i: (0, i, 0)
    return pl.pallas_call(
        body, name=name, grid=(S // tm,),
        in_specs=[pl.BlockSpec((tm, D_MODEL), row),
                  pl.BlockSpec((1, D_MODEL, W_EXT), lambda i: (l, 0, 0)),
                  pl.BlockSpec((1, H, Q_LORA, 256), lambda i: (l, 0, 0, 0)),
                  pl.BlockSpec((1, KV_LORA, H * 256), lambda i: (l, 0, 0)),
                  pl.BlockSpec((1, Q_LORA), lambda i: (0, 0)), pl.BlockSpec((1, KV_LORA), lambda i: (0, 0)),
                  pl.BlockSpec((tm, 128), row)],
        out_specs=[pl.BlockSpec((tm, 256), row), pl.BlockSpec((tm, Q_LORA), row), pl.BlockSpec((tm, KV_LORA), row),
                   pl.BlockSpec((tm, Q_LORA), row), pl.BlockSpec((tm, KV_LORA), row),
                   pl.BlockSpec((H, tm, 256), hrow), pl.BlockSpec((H, tm, 256), hrow), pl.BlockSpec((H, tm, 128), hrow)],
        out_shape=[SDS((S, 256), f32), SDS((S, Q_LORA), f32), SDS((S, KV_LORA), f32),
                   SDS((S, Q_LORA), bf16), SDS((S, KV_LORA), bf16),
                   SDS((H, S, 256), bf16), SDS((H, S, 256), bf16), SDS((H, S, 128), bf16)],
        compiler_params=_cp("parallel"),
    )(xb, w_in, wq, wkv, gq, gkv, cs)


def _group_select(col, a2, a4, a8, a16):
    return jnp.where(col < 64, a2, jnp.where(col < 128, a4, jnp.where(col < 192, a8, a16)))


def pool_fwd(name, u, wbd, scale):
    S = u.shape[0]
    tm = _tile(S, 512)
    hb = tm // HALO

    def body(u_ref, halo_ref, w_ref, s_ref, d_ref, y_ref):
        i = pl.program_id(0)
        cur = u_ref[...]
        halo = jnp.where(i > 0, halo_ref[...], 0.0)
        ext = jnp.concatenate([halo, cur], axis=0)
        s2 = ext + pltpu.roll(ext, 1, axis=0)
        s4 = s2 + pltpu.roll(s2, 2, axis=0)
        s8 = s4 + pltpu.roll(s4, 4, axis=0)
        s16 = s8 + pltpu.roll(s8, 8, axis=0)
        t1 = (i * tm + 1 + lax.broadcasted_iota(jnp.int32, (tm, 1), 0)).astype(f32)
        col = lax.broadcasted_iota(jnp.int32, (tm, 256), 1)
        m = _group_select(col, s2[HALO:] / jnp.minimum(t1, 2.0), s4[HALO:] / jnp.minimum(t1, 4.0),
                          s8[HALO:] / jnp.minimum(t1, 8.0), s16[HALO:] / jnp.minimum(t1, 16.0))
        d = (m - cur).astype(bf16)
        d_ref[...] = d
        y_ref[...] = (_dot(d, w_ref[...]) * s_ref[...]).astype(bf16)

    row = lambda i: (i, 0)
    return pl.pallas_call(
        body, name=name, grid=(S // tm,),
        in_specs=[pl.BlockSpec((tm, 256), row), pl.BlockSpec((HALO, 256), lambda i: (jnp.maximum(i * hb - 1, 0), 0)),
                  pl.BlockSpec((256, 256), lambda i: (0, 0)), pl.BlockSpec((1, 256), lambda i: (0, 0))],
        out_specs=[pl.BlockSpec((tm, 256), row)] * 2,
        out_shape=[SDS((S, 256), bf16), SDS((S, 256), bf16)],
        compiler_params=_cp("parallel"),
    )(u, u, wbd, scale)


def pool_bwd(name, dyp, d, wbd, scale):
    S = dyp.shape[0]
    tm = _tile(S, 512)
    hb = tm // HALO
    n_ext = tm + HALO

    def fwd_sum(e, steps):
        k = 1
        for _ in range(steps):
            e = e + pltpu.roll(e, n_ext - k, axis=0)
            k *= 2
        return e

    def body(dy_ref, halo_ref, d_ref, w_ref, s_ref, du_ref, dyw_ref, ds_ref):
        i = pl.program_id(0)
        sc = s_ref[...]
        w = w_ref[...]
        cur = dy_ref[...].astype(f32)
        halo = jnp.where(i < pl.num_programs(0) - 1, halo_ref[...].astype(f32), 0.0)
        dyw = jnp.concatenate([cur, halo], axis=0) * sc
        dyw_ref[...] = dyw[:tm].astype(bf16)
        dd = _dot_nt(dyw.astype(bf16), w)
        t1 = (i * tm + 1 + lax.broadcasted_iota(jnp.int32, (n_ext, 1), 0)).astype(f32)
        f2 = fwd_sum(dd / jnp.minimum(t1, 2.0), 1)
        f4 = fwd_sum(dd / jnp.minimum(t1, 4.0), 2)
        f8 = fwd_sum(dd / jnp.minimum(t1, 8.0), 3)
        f16 = fwd_sum(dd / jnp.minimum(t1, 16.0), 4)
        col = lax.broadcasted_iota(jnp.int32, (tm, 256), 1)
        du_ref[...] = (_group_select(col, f2[:tm], f4[:tm], f8[:tm], f16[:tm]) - dd[:tm]).astype(bf16)

        @pl.when(i == 0)
        def _():
            ds_ref[...] = jnp.zeros_like(ds_ref)

        ds_ref[...] += jnp.sum(cur * _dot(d_ref[...], w), axis=0, keepdims=True)

    row = lambda i: (i, 0)
    nhb = S // HALO
    return pl.pallas_call(
        body, name=name, grid=(S // tm,),
        in_specs=[pl.BlockSpec((tm, 256), row), pl.BlockSpec((HALO, 256), lambda i: (jnp.minimum((i + 1) * hb, nhb - 1), 0)),
                  pl.BlockSpec((tm, 256), row), pl.BlockSpec((256, 256), lambda i: (0, 0)),
                  pl.BlockSpec((1, 256), lambda i: (0, 0))],
        out_specs=[pl.BlockSpec((tm, 256), row), pl.BlockSpec((tm, 256), row), pl.BlockSpec((1, 256), lambda i: (0, 0))],
        out_shape=[SDS((S, 256), bf16), SDS((S, 256), bf16), SDS((1, 256), f32)],
        compiler_params=_cp("arbitrary"),
    )(dyp, dyp, d, wbd, scale)


def _diag_mask(tq):
    rc = lax.broadcasted_iota(jnp.int32, (tq, 1), 0) // 64
    cc = lax.broadcasted_iota(jnp.int32, (1, tq), 1) // 64
    return rc >= cc


MLA_SCALE_LOG2 = MLA_SCALE * math.log2(math.e)


def mla_attn_fwd(name, q, k, v):
    H, S, _ = q.shape
    tq = _tile(S, 512)
    nq = S // tq

    def body(q_ref, k_ref, v_ref, o_ref, lse_ref, m_sc, l_sc, acc_sc):
        i, j = pl.program_id(1), pl.program_id(2)

        @pl.when(j == 0)
        def _():
            m_sc[...] = jnp.full_like(m_sc, NEG_INF)
            l_sc[...] = jnp.zeros_like(l_sc)
            acc_sc[...] = jnp.zeros_like(acc_sc)

        def step(masked):
            s = _dot_nt(q_ref[0], k_ref[0])
            if masked:
                s = jnp.where(_diag_mask(tq), s, NEG_INF)
            m_prev = m_sc[...]
            m_new = jnp.maximum(m_prev, jnp.max(s, axis=-1, keepdims=True))
            p = jnp.exp2((s - m_new[:, :1]) * MLA_SCALE_LOG2)
            a = jnp.exp2((m_prev - m_new) * MLA_SCALE_LOG2)
            l_sc[...] = a * l_sc[...] + jnp.sum(p, axis=-1, keepdims=True)
            acc_sc[...] = a * acc_sc[...] + _dot(p.astype(bf16), v_ref[0])
            m_sc[...] = m_new

        @pl.when(j < i)
        def _():
            step(False)

        @pl.when(j == i)
        def _():
            step(True)
            o_ref[...] = (acc_sc[...] / l_sc[...]).astype(bf16)
            lse_ref[0] = m_sc[...] * MLA_SCALE_LOG2 + jnp.log2(l_sc[...])

    return pl.pallas_call(
        body, name=name, grid=(H, nq, nq),
        in_specs=[pl.BlockSpec((1, tq, 256), lambda h, i, j: (h, i, 0)),
                  pl.BlockSpec((1, tq, 256), lambda h, i, j: (h, jnp.minimum(i, j), 0)),
                  pl.BlockSpec((1, tq, 128), lambda h, i, j: (h, jnp.minimum(i, j), 0))],
        out_specs=[pl.BlockSpec((tq, 128), lambda h, i, j: (i, h)), pl.BlockSpec((1, tq, 128), lambda h, i, j: (h, i, 0))],
        out_shape=[SDS((S, H * 128), bf16), SDS((H, S, 128), f32)],
        scratch_shapes=[pltpu.VMEM((tq, 128), f32), pltpu.VMEM((tq, 128), f32), pltpu.VMEM((tq, 128), f32)],
        compiler_params=_cp("parallel", "parallel", "arbitrary"),
    )(q, k, v)


def mla_attn_bwd(name, q, k, v, o, do, lse):
    H, S, _ = q.shape
    tq = _tile(S, 512)
    nq = S // tq

    def body(q_ref, k_ref, v_ref, o_ref, do_ref, lse_ref, dq_ref, dk_ref, dv_ref, dk_sc, dv_sc):
        j, i = pl.program_id(1), pl.program_id(2)

        @pl.when(jnp.logical_and(j == 0, i == 0))
        def _():
            dq_ref[...] = jnp.zeros_like(dq_ref)

        @pl.when(i == j)
        def _():
            dk_sc[...] = jnp.zeros_like(dk_sc)
            dv_sc[...] = jnp.zeros_like(dv_sc)

        def step(masked):
            qv, kv_, dov = q_ref[0], k_ref[0], do_ref[...]
            s = _dot_nt(qv, kv_)
            if masked:
                s = jnp.where(_diag_mask(tq), s, NEG_INF)
            p = jnp.exp2(s * MLA_SCALE_LOG2 - lse_ref[0][:, :1])
            dv_sc[...] += _dot_tn(p.astype(bf16), dov)
            dp = _dot_nt(dov, v_ref[0])
            delta = jnp.sum(dov.astype(f32) * o_ref[...].astype(f32), axis=-1, keepdims=True)
            ds = (p * (dp - delta)).astype(bf16)
            dk_sc[...] += _dot_tn(ds, qv)
            rows = pl.ds(pl.multiple_of(i * tq, tq), tq)
            dq_ref[0, rows, :] += _dot(ds, kv_)

        @pl.when(i > j)
        def _():
            step(False)

        @pl.when(i == j)
        def _():
            step(True)

        @pl.when(i == nq - 1)
        def _():
            dk_ref[0] = dk_sc[...] * MLA_SCALE
            dv_ref[0] = dv_sc[...]

        @pl.when(jnp.logical_and(j == nq - 1, i == nq - 1))
        def _():
            dq_ref[...] = dq_ref[...] * MLA_SCALE

    qi = lambda h, j, i: (h, jnp.maximum(i, j), 0)
    kj = lambda h, j, i: (h, j, 0)
    return pl.pallas_call(
        body, name=name, grid=(H, nq, nq),
        in_specs=[pl.BlockSpec((1, tq, 256), qi), pl.BlockSpec((1, tq, 256), kj), pl.BlockSpec((1, tq, 128), kj),
                  pl.BlockSpec((tq, 128), lambda h, j, i: (jnp.maximum(i, j), h)),
                  pl.BlockSpec((tq, 128), lambda h, j, i: (jnp.maximum(i, j), h)),
                  pl.BlockSpec((1, tq, 128), qi)],
        out_specs=[pl.BlockSpec((1, S, 256), lambda h, j, i: (h, 0, 0)), pl.BlockSpec((1, tq, 256), kj),
                   pl.BlockSpec((1, tq, 128), kj)],
        out_shape=[SDS((H, S, 256), f32), SDS((H, S, 256), f32), SDS((H, S, 128), f32)],
        scratch_shapes=[pltpu.VMEM((tq, 256), f32), pltpu.VMEM((tq, 128), f32)],
        compiler_params=_cp("parallel", "arbitrary", "arbitrary"),
    )(q, k, v, o, do, lse)


def mix_post_bwd(name, dq, dk, dv, wq, wkv, l, cq, ckv, gq, gkv, cs):
    H, S, _ = dq.shape
    tm = _tile(S, 256)

    def rms_bwd(dyn, c, g):
        r = lax.rsqrt(jnp.mean(c * c, axis=-1, keepdims=True) + RMS_EPS)
        ch = c * r
        dyg = dyn * g
        dc = r * (dyg - ch * jnp.mean(dyg * ch, axis=-1, keepdims=True))
        return dc, jnp.sum(dyn * ch, axis=0, keepdims=True)

    def body(dq_ref, dk_ref, dv_ref, wq_ref, wkv_ref, cq_ref, ckv_ref, gq_ref, gkv_ref, cs_ref,
             dqe_ref, dkv_ref, dh_ref, dgq_ref, dgkv_ref):
        csv = cs_ref[...]
        lane = lax.broadcasted_iota(jnp.int32, (tm, 128), 1)
        dcqn = jnp.zeros((tm, Q_LORA), f32)
        dkr = jnp.zeros((tm, 128), f32)
        for hd in range(H):
            dqh = dq_ref[hd]
            dqe = jnp.concatenate([dqh[:, :128], _half_sum(dqh[:, 128:]) * csv], axis=1).astype(bf16)
            dqe_ref[:, 256 * hd:256 * hd + 256] = dqe
            dcqn = dcqn + _dot_nt(dqe, wq_ref[0, hd])
            dkh = dk_ref[hd]
            dkv_ref[:, 256 * hd:256 * hd + 128] = dkh[:, :128].astype(bf16)
            dkv_ref[:, 256 * hd + 128:256 * hd + 256] = dv_ref[hd].astype(bf16)
            dkr = dkr + dkh[:, 128:]
        dckvn = _dot_nt(dkv_ref[...], wkv_ref[0])
        dblk = _half_sum(jnp.where(lane < 64, dkr, 0.0)) * csv
        dcq, dgq = rms_bwd(dcqn, cq_ref[...], gq_ref[...])
        dckv, dgkv = rms_bwd(dckvn, ckv_ref[...], gkv_ref[...])
        dh_ref[:, :256] = dcq.astype(bf16)
        dh_ref[:, 256:384] = dckv.astype(bf16)
        dh_ref[:, 384:] = dblk.astype(bf16)

        @pl.when(pl.program_id(0) == 0)
        def _():
            dgq_ref[...] = jnp.zeros_like(dgq_ref)
            dgkv_ref[...] = jnp.zeros_like(dgkv_ref)

        dgq_ref[...] += dgq
        dgkv_ref[...] += dgkv

    row = lambda i: (i, 0)
    hrow = lambda i: (0, i, 0)
    return pl.pallas_call(
        body, name=name, grid=(S // tm,),
        in_specs=[pl.BlockSpec((H, tm, 256), hrow), pl.BlockSpec((H, tm, 256), hrow), pl.BlockSpec((H, tm, 128), hrow),
                  pl.BlockSpec((1, H, Q_LORA, 256), lambda i: (l, 0, 0, 0)),
                  pl.BlockSpec((1, KV_LORA, H * 256), lambda i: (l, 0, 0)),
                  pl.BlockSpec((tm, Q_LORA), row), pl.BlockSpec((tm, KV_LORA), row),
                  pl.BlockSpec((1, Q_LORA), lambda i: (0, 0)), pl.BlockSpec((1, KV_LORA), lambda i: (0, 0)),
                  pl.BlockSpec((tm, 128), row)],
        out_specs=[pl.BlockSpec((tm, H * 256), row), pl.BlockSpec((tm, H * 256), row), pl.BlockSpec((tm, 512), row),
                   pl.BlockSpec((1, Q_LORA), lambda i: (0, 0)), pl.BlockSpec((1, KV_LORA), lambda i: (0, 0))],
        out_shape=[SDS((S, H * 256), bf16), SDS((S, H * 256), bf16), SDS((S, 512), bf16),
                   SDS((1, Q_LORA), f32), SDS((1, KV_LORA), f32)],
        compiler_params=_cp("arbitrary"),
    )(dq, dk, dv, wq, wkv, cq, ckv, gq, gkv, cs)


def _cross_probs(qb, kv_ref, hd):
    cols = slice(hd * MEM_HEAD_DIM, (hd + 1) * MEM_HEAD_DIM)
    s = _dot_nt(qb[:, cols], kv_ref[:, cols]) * MEM_SCALE
    e = jnp.exp(s - jnp.max(s, axis=-1, keepdims=True))
    return e / jnp.sum(e, axis=-1, keepdims=True)


def cross_fwd(name, xb, xf, wq, wo, l, kv, g, b):
    S = xb.shape[0]
    tm = _tile(S, 256)
    M = kv.shape[0]

    def body(x_ref, xf_ref, wq_ref, wo_ref, k_ref, v_ref, g_ref, b_ref, q_ref, o_ref, z_ref, y_ref, yb_ref):
        qb = _dot(x_ref[...], wq_ref[0]).astype(bf16)
        q_ref[...] = qb
        for hd in range(MEM_HEADS):
            cols = slice(hd * MEM_HEAD_DIM, (hd + 1) * MEM_HEAD_DIM)
            p = _cross_probs(qb, k_ref, hd)
            o_ref[:, cols] = _dot(p.astype(bf16), v_ref[:, cols]).astype(bf16)
        z = ALPHA * xf_ref[...] + _dot(o_ref[...], wo_ref[0])
        mu = jnp.mean(z, axis=-1, keepdims=True)
        zc = z - mu
        var = jnp.mean(zc * zc, axis=-1, keepdims=True)
        y = zc * lax.rsqrt(var + LN_EPS) * g_ref[...] + b_ref[...]
        z_ref[...] = z
        y_ref[...] = y
        yb_ref[...] = y.astype(bf16)

    row = lambda i: (i, 0)
    wspec = pl.BlockSpec((1, D_MODEL, D_MODEL), lambda i: (l, 0, 0))
    vec = pl.BlockSpec((1, D_MODEL), lambda i: (0, 0))
    blk = pl.BlockSpec((tm, D_MODEL), row)
    return pl.pallas_call(
        body, name=name, grid=(S // tm,),
        in_specs=[blk, blk, wspec, wspec, pl.BlockSpec((M, D_MODEL), lambda i: (0, 0)),
                  pl.BlockSpec((M, D_MODEL), lambda i: (0, 1)), vec, vec],
        out_specs=[blk] * 5,
        out_shape=[SDS((S, D_MODEL), bf16), SDS((S, D_MODEL), bf16), SDS((S, D_MODEL), f32), SDS((S, D_MODEL), f32),
                   SDS((S, D_MODEL), bf16)],
        compiler_params=_cp("parallel"),
    )(xb, xf, wq, wo, kv, kv, g, b)


def cross_bwd(name, dzb, wo, l, qb, kv):
    S = dzb.shape[0]
    tm = _tile(S, 256)
    M = kv.shape[0]

    def body(dz_ref, wo_ref, q_ref, k_ref, v_ref, dq_ref, dkv_ref):
        @pl.when(pl.program_id(0) == 0)
        def _():
            dkv_ref[...] = jnp.zeros_like(dkv_ref)

        do = _dot_nt(dz_ref[...], wo_ref[0]).astype(bf16)
        qv = q_ref[...]
        for hd in range(MEM_HEADS):
            cols = slice(hd * MEM_HEAD_DIM, (hd + 1) * MEM_HEAD_DIM)
            vcols = slice(D_MODEL + hd * MEM_HEAD_DIM, D_MODEL + (hd + 1) * MEM_HEAD_DIM)
            p = _cross_probs(qv, k_ref, hd)
            doh = do[:, cols]
            dkv_ref[:, vcols] += _dot_tn(p.astype(bf16), doh)
            dp = _dot_nt(doh, v_ref[:, cols])
            ds = (p * (dp - jnp.sum(dp * p, axis=-1, keepdims=True)) * MEM_SCALE).astype(bf16)
            dq_ref[:, cols] = _dot(ds, k_ref[:, cols]).astype(bf16)
            dkv_ref[:, cols] += _dot_tn(ds, qv[:, cols])

    row = lambda i: (i, 0)
    blk = pl.BlockSpec((tm, D_MODEL), row)
    return pl.pallas_call(
        body, name=name, grid=(S // tm,),
        in_specs=[blk, pl.BlockSpec((1, D_MODEL, D_MODEL), lambda i: (l, 0, 0)), blk,
                  pl.BlockSpec((M, D_MODEL), lambda i: (0, 0)), pl.BlockSpec((M, D_MODEL), lambda i: (0, 1))],
        out_specs=[blk, pl.BlockSpec((M, 2 * D_MODEL), lambda i: (0, 0))],
        out_shape=[SDS((S, D_MODEL), bf16), SDS((M, 2 * D_MODEL), f32)],
        compiler_params=_cp("arbitrary"),
    )(dzb, wo, qb, kv, kv)


def adamw(name, w, g, m, v):
    shape = w.shape
    cols = shape[-1]
    rows = math.prod(shape[:-1])
    tr = _row_tile(rows, cols, target=2**20)
    c1 = 1.0 - ADAM_B1 ** ADAM_STEP
    c2 = 1.0 - ADAM_B2 ** ADAM_STEP

    def body(w_ref, g_ref, m_ref, v_ref, d_ref, nm_ref, nv_ref):
        gv = g_ref[...]
        nm = ADAM_B1 * m_ref[...] + (1.0 - ADAM_B1) * gv
        nv = ADAM_B2 * v_ref[...] + (1.0 - ADAM_B2) * (gv * gv)
        d_ref[...] = -ADAM_LR * ((nm / c1) / (jnp.sqrt(nv / c2) + ADAM_EPS) + ADAM_WD * w_ref[...])
        nm_ref[...] = nm
        nv_ref[...] = nv

    blk = pl.BlockSpec((tr, cols), lambda i: (i, 0))
    flat = SDS((rows, cols), f32)
    outs = pl.pallas_call(
        body, name=name, grid=(rows // tr,), in_specs=[blk] * 4, out_specs=[blk] * 3, out_shape=[flat] * 3,
        compiler_params=_cp("parallel"),
    )(*[a.reshape(rows, cols) for a in (w, g, m, v)])
    return [o.reshape(shape) for o in outs]


def _me():
    return lax.axis_index("x"), lax.axis_index("y"), lax.axis_index("c")


def _other_chips(x, y):
    return [(1 - x, y), (x, 1 - y), (1 - x, 1 - y)]


def gather_weights(shards):
    n = len(shards)

    def body(*refs):
        in_refs, out_refs = refs[:n], refs[n:2 * n]
        send_sems, recv_sems, local_sems = refs[2 * n:]
        x, y, c = _me()
        q = 2 * x + y
        sibling = (x, y, 1 - c)
        chips = _other_chips(x, y)
        started = []
        locals_ = []
        for a in range(n):
            for lyr in range(DEPTH):
                cp = pltpu.make_async_copy(in_refs[a].at[lyr], out_refs[a].at[lyr, q], local_sems.at[a * DEPTH + lyr])
                cp.start()
                locals_.append(cp)

        def rcopy(a, k, src, dst, to):
            return pltpu.make_async_remote_copy(src_ref=src, dst_ref=dst, send_sem=send_sems.at[a * 6 + k],
                                                recv_sem=recv_sems.at[a * 6 + k], device_id=to, device_id_type=MESH)

        for a in range(n):
            for k, (cx, cy) in enumerate(chips):
                cp = rcopy(a, k, in_refs[a].at[c], out_refs[a].at[c, q], (cx, cy, c))
                cp.start()
                started.append(cp)
        for a in range(n):
            for k, (cx, cy) in enumerate(chips):
                land = out_refs[a].at[c, 2 * cx + cy]
                rcopy(a, k, land, land, sibling).wait_recv()
                fwd = rcopy(a, 3 + k, land, land, sibling)
                fwd.start()
                started.append(fwd)
        for a in range(n):
            for k, (cx, cy) in enumerate(chips):
                land = out_refs[a].at[1 - c, 2 * cx + cy]
                rcopy(a, 3 + k, land, land, sibling).wait_recv()
        for cp in started:
            cp.wait_send()
        for cp in locals_:
            cp.wait()

    any_spec = pl.BlockSpec(memory_space=pl.ANY)
    return pl.pallas_call(
        body, name="gather_weights",
        in_specs=[any_spec] * n, out_specs=[any_spec] * n,
        out_shape=[SDS((DEPTH, N_CHIPS) + s.shape[1:], s.dtype) for s in shards],
        scratch_shapes=[pltpu.SemaphoreType.DMA((6 * n,)), pltpu.SemaphoreType.DMA((6 * n,)),
                        pltpu.SemaphoreType.DMA((DEPTH * n,))],
    )(*shards)


def pair_send_other_layer(grads):
    n = len(grads)

    def body(*refs):
        in_refs, out_refs = refs[:n], refs[n:2 * n]
        send_sems, recv_sems = refs[2 * n:]
        x, y, c = _me()
        cps = [pltpu.make_async_remote_copy(src_ref=in_refs[a].at[1 - c], dst_ref=out_refs[a], send_sem=send_sems.at[a],
                                            recv_sem=recv_sems.at[a], device_id=(x, y, 1 - c), device_id_type=MESH)
               for a in range(n)]
        for cp in cps:
            cp.start()
        for cp in cps:
            cp.wait()

    any_spec = pl.BlockSpec(memory_space=pl.ANY)
    return pl.pallas_call(
        body, name="pair_send_other_layer", in_specs=[any_spec] * n, out_specs=[any_spec] * n,
        out_shape=[SDS(g.shape[1:], f32) for g in grads],
        scratch_shapes=[pltpu.SemaphoreType.DMA((n,)), pltpu.SemaphoreType.DMA((n,))],
    )(*grads)


def pair_add(name, g, recv, c_arr):
    shard = g.shape[2:]
    cols = shard[-1]
    rows = N_CHIPS * math.prod(shard[:-1])
    tr = _row_tile(rows, cols)

    def body(c_ref, g_ref, r_ref, o_ref):
        o_ref[...] = (g_ref[0] + r_ref[...]).astype(bf16)

    out = pl.pallas_call(
        body, name=name,
        grid_spec=pltpu.PrefetchScalarGridSpec(
            num_scalar_prefetch=1, grid=(rows // tr,),
            in_specs=[pl.BlockSpec((1, tr, cols), lambda i, c_ref: (c_ref[0], i, 0)),
                      pl.BlockSpec((tr, cols), lambda i, c_ref: (i, 0))],
            out_specs=pl.BlockSpec((tr, cols), lambda i, c_ref: (i, 0))),
        out_shape=SDS((rows, cols), bf16), compiler_params=_cp("parallel"),
    )(c_arr, g.reshape(DEPTH, rows, cols), recv.reshape(rows, cols))
    return out.reshape((N_CHIPS,) + shard)


def chip_exchange(psums):
    n = len(psums)

    def body(*refs):
        in_refs, out_refs = refs[:n], refs[n:2 * n]
        send_sems, recv_sems = refs[2 * n:]
        x, y, c = _me()
        chips = _other_chips(x, y)
        cps = [pltpu.make_async_remote_copy(src_ref=in_refs[a].at[2 * cx + cy], dst_ref=out_refs[a].at[k],
                                            send_sem=send_sems.at[3 * a + k], recv_sem=recv_sems.at[3 * a + k],
                                            device_id=(cx, cy, c), device_id_type=MESH)
               for a in range(n) for k, (cx, cy) in enumerate(chips)]
        for cp in cps:
            cp.start()
        for cp in cps:
            cp.wait()

    any_spec = pl.BlockSpec(memory_space=pl.ANY)
    return pl.pallas_call(
        body, name="chip_exchange", in_specs=[any_spec] * n, out_specs=[any_spec] * n,
        out_shape=[SDS((3,) + p.shape[1:], bf16) for p in psums],
        scratch_shapes=[pltpu.SemaphoreType.DMA((3 * n,)), pltpu.SemaphoreType.DMA((3 * n,))],
    )(*psums)


def chip_add(name, psum, recv, qc_arr):
    shard = psum.shape[1:]
    cols = shard[-1]
    rows = math.prod(shard[:-1])
    tr = _row_tile(rows, cols)

    def body(qc_ref, p_ref, r_ref, o_ref):
        o_ref[0] = ((p_ref[0].astype(f32) + r_ref[0].astype(f32)) + r_ref[1].astype(f32)) + r_ref[2].astype(f32)

    out = pl.pallas_call(
        body, name=name,
        grid_spec=pltpu.PrefetchScalarGridSpec(
            num_scalar_prefetch=1, grid=(rows // tr,),
            in_specs=[pl.BlockSpec((1, tr, cols), lambda i, qc_ref: (qc_ref[0], i, 0)),
                      pl.BlockSpec((3, tr, cols), lambda i, qc_ref: (0, i, 0))],
            out_specs=pl.BlockSpec((1, tr, cols), lambda i, qc_ref: (qc_ref[1], i, 0))),
        out_shape=SDS((DEPTH, rows, cols), f32), compiler_params=_cp("parallel"),
    )(qc_arr, psum.reshape(N_CHIPS, rows, cols), recv.reshape(3, rows, cols))
    return out.reshape((DEPTH,) + shard)


def pair_share(sums):
    n = len(sums)

    def body(*refs):
        out_refs = refs[n:2 * n]
        send_sems, recv_sems = refs[2 * n:]
        x, y, c = _me()
        sibling = (x, y, 1 - c)
        cps = [pltpu.make_async_remote_copy(src_ref=out_refs[a].at[c], dst_ref=out_refs[a].at[c], send_sem=send_sems.at[a],
                                            recv_sem=recv_sems.at[a], device_id=sibling, device_id_type=MESH)
               for a in range(n)]
        for cp in cps:
            cp.start()
        for cp in cps:
            cp.wait_send()
        for a in range(n):
            land = out_refs[a].at[1 - c]
            pltpu.make_async_remote_copy(src_ref=land, dst_ref=land, send_sem=send_sems.at[a], recv_sem=recv_sems.at[a],
                                         device_id=sibling, device_id_type=MESH).wait_recv()

    any_spec = pl.BlockSpec(memory_space=pl.ANY)
    return pl.pallas_call(
        body, name="pair_share", in_specs=[any_spec] * n, out_specs=[any_spec] * n,
        out_shape=[SDS(s.shape, f32) for s in sums], input_output_aliases={a: a for a in range(n)},
        scratch_shapes=[pltpu.SemaphoreType.DMA((n,)), pltpu.SemaphoreType.DMA((n,))],
    )(*sums)


def allsum_small(name, v):
    R = v.shape[0]

    def body(v_ref, o_ref, all_ref, send_sems, recv_sems, local_sem):
        x, y, c = _me()
        me, sibling = (x, y, c), (x, y, 1 - c)
        chips = _other_chips(x, y)

        def rows(px, py, pc):
            return all_ref.at[4 * px + 2 * py + pc]

        def copy(k, block, to, src=None):
            return pltpu.make_async_remote_copy(
                src_ref=rows(*block) if src is None else src, dst_ref=rows(*block),
                send_sem=send_sems.at[k], recv_sem=recv_sems.at[k], device_id=to, device_id_type=MESH)

        mine = pltpu.make_async_copy(v_ref, rows(*me), local_sem)
        mine.start()
        first = [copy(0, me, sibling, src=v_ref)]
        first += [copy(1 + j, me, (*chip, c), src=v_ref) for j, chip in enumerate(chips)]
        for cp in first:
            cp.start()
        passed = [copy(4 + j, (*chip, c), sibling) for j, chip in enumerate(chips)]
        for j, chip in enumerate(chips):
            copy(1 + j, (*chip, c), me).wait_recv()
            passed[j].start()
        copy(0, sibling, me).wait_recv()
        for j, chip in enumerate(chips):
            copy(4 + j, (*chip, 1 - c), me).wait_recv()
        for cp in first + passed:
            cp.wait_send()
        mine.wait()
        acc = all_ref[0]
        for d in range(1, 8):
            acc = acc + all_ref[d]
        o_ref[...] = acc

    return pl.pallas_call(
        body, name=name,
        in_specs=[pl.BlockSpec(memory_space=pltpu.VMEM)], out_specs=pl.BlockSpec(memory_space=pltpu.VMEM),
        out_shape=SDS((R, 128), f32),
        scratch_shapes=[pltpu.VMEM((8, R, 128), f32), pltpu.SemaphoreType.DMA((7,)), pltpu.SemaphoreType.DMA((7,)),
                        pltpu.SemaphoreType.DMA],
        compiler_params=pltpu.CompilerParams(vmem_limit_bytes=V7X_VMEM_LIMIT),
    )(v)


def reduce_grads(grads):
    x, y, c = _me()
    c_arr = jnp.reshape(c, (1,)).astype(jnp.int32)
    qc_arr = jnp.stack([2 * x + y, c]).astype(jnp.int32)
    recv = pair_send_other_layer(grads)
    psums = [pair_add(f"pair_add_{a}", g, r, c_arr) for a, (g, r) in enumerate(zip(grads, recv))]
    got = chip_exchange(psums)
    sums = [chip_add(f"chip_add_{a}", p, r, qc_arr) for a, (p, r) in enumerate(zip(psums, got))]
    return pair_share(sums)


def _swap_half(r):
    return jnp.concatenate([-r[..., 32:], r[..., :32]], axis=-1)


def _unswap_add(p, qg):
    return p + jnp.concatenate([qg[..., 32:], -qg[..., :32]], axis=-1)


def _block_diag(pw):
    L = pw.shape[0]
    out = jnp.zeros((L, 256, 256), pw.dtype)
    for gi in range(4):
        out = out.at[:, 64 * gi:64 * gi + 64, 64 * gi:64 * gi + 64].set(pw[:, gi])
    return out


def _to_col_shards(w):
    *lead, K, N = w.shape
    nl = len(lead)
    return w.reshape(*lead, K, N_CHIPS, N // N_CHIPS).transpose(*range(nl), nl + 1, nl, nl + 2)


def _from_col_shards(w):
    *lead, C, K, n = w.shape
    nl = len(lead)
    return w.transpose(*range(nl), nl + 1, nl, nl + 2).reshape(*lead, K, C * n)


def _step_serial_comm(x, mem, positions, ln_g, ln_b, ffn1_w13, ffn1_w2, w_in, pool_w, pool_scale, q_norm_g, w_uq, kv_norm_g, w_ukv, w_out, mem_wq, mem_wkv, mem_wo, ffn2_w13, ffn2_w2, loss_target, m_ln_g, m_ln_b, m_ffn1_w13, m_ffn1_w2, m_w_in, m_pool_w, m_pool_scale, m_q_norm_g, m_w_uq, m_kv_norm_g, m_w_ukv, m_w_out, m_mem_wq, m_mem_wkv, m_mem_wo, m_ffn2_w13, m_ffn2_w2, v_ln_g, v_ln_b, v_ffn1_w13, v_ffn1_w2, v_w_in, v_pool_w, v_pool_scale, v_q_norm_g, v_w_uq, v_kv_norm_g, v_w_ukv, v_w_out, v_mem_wq, v_mem_wkv, v_mem_wo, v_ffn2_w13, v_ffn2_w2):
    L = DEPTH
    S = x.shape[1]
    qx, qy, _ = _me()
    chip = 2 * qx + qy

    big = [ffn1_w13, ffn1_w2, w_in, w_uq, w_ukv, w_out, mem_wq, mem_wkv, mem_wo, ffn2_w13, ffn2_w2]
    (g_f1w13, g_f1w2, g_win, g_wuq, g_wukv, g_wout, g_mwq, g_mwkv, g_mwo, g_f2w13, g_f2w2) = gather_weights(
        [w.astype(bf16) for w in big])
    f1w2 = g_f1w2.reshape(L, D_FF, D_MODEL)
    f2w2 = g_f2w2.reshape(L, D_FF, D_MODEL)
    win = g_win.reshape(L, D_MODEL, D_IN)
    win_ext = jnp.concatenate([win, _swap_half(win[..., D_IN - QK_ROPE:])], axis=-1)
    wuq = _from_col_shards(g_wuq).reshape(L, Q_LORA, MLA_HEADS, QK_NOPE + QK_ROPE)
    wq_ext = jnp.concatenate([wuq, _swap_half(wuq[..., QK_NOPE:])], axis=-1).transpose(0, 2, 1, 3)
    wukv = _from_col_shards(g_wukv)
    wout = g_wout.reshape(L, D_MODEL, D_MODEL)
    wout_pool, wout_mla = wout[:, :POOL_WIDTH], wout[:, POOL_WIDTH:]
    mwq = g_mwq.reshape(L, D_MODEL, D_MODEL)
    mwo = g_mwo.reshape(L, D_MODEL, D_MODEL)
    wbd = _block_diag(pool_w.astype(bf16))

    ln_pad = jnp.zeros((2, L, 4, N_CHIPS, D_MODEL // N_CHIPS), f32)
    ln_pad = lax.dynamic_update_slice(ln_pad, jnp.stack([ln_g, ln_b])[:, :, :, None, :], (0, 0, 0, chip, 0))
    ln_full = allsum_small("allsum_ln", ln_pad.reshape(-1, 128)) * 0.5
    ln_full = ln_full.reshape(2, L, 4, D_MODEL)
    lng, lnb = ln_full[0], ln_full[1]

    half = QK_ROPE // 2
    inv_freq = ROPE_BASE ** (-jnp.arange(half, dtype=f32) / half)
    ang = positions[0].astype(f32)[:, None] * inv_freq
    cos, sin = jnp.cos(ang), jnp.sin(ang)
    cs = jnp.concatenate([cos, cos, sin, sin], axis=-1)

    memb = mem[0].astype(bf16)
    xf = x[0]
    xb = xf.astype(bf16)
    vec = lambda a: a.reshape(1, -1)

    saved = []
    for l in range(L):
        sv = {}
        sv["x0b"] = xb
        gate, up, act = ffn_up(f"ffn1_up_{l}", xb, g_f1w13, l)
        z1, x1f, x1b = proj_res_ln(f"ffn1_down_{l}", [act], [f1w2], [l], xf, vec(lng[l, 0]), vec(lnb[l, 0]), 0.5)
        sv.update(gate1=gate, up1=up, act1=act, z1=z1, x1b=x1b)
        u, cq, ckv, cqn, ckvn, q, k, v = mix_pre(f"mix_pre_{l}", x1b, win_ext, wq_ext, wukv, l,
                                                   vec(q_norm_g[l]), vec(kv_norm_g[l]), cs)
        dpool, ypool = pool_fwd(f"pool_fwd_{l}", u, wbd[l], vec(pool_scale[l]))
        o, lse = mla_attn_fwd(f"mla_fwd_{l}", q, k, v)
        z2, x2f, x2b = proj_res_ln(f"mix_out_{l}", [ypool, o], [wout_pool, wout_mla], [l, l], x1f,
                                   vec(lng[l, 1]), vec(lnb[l, 1]), 1.0)
        sv.update(cq=cq, ckv=ckv, cqn=cqn, ckvn=ckvn, q=q, k=k, v=v, dpool=dpool, ypool=ypool, o=o, lse=lse, z2=z2, x2b=x2b)
        kvm = mm_nn_shard(f"mem_kv_{l}", memb, g_mwkv, l)
        cq_, co_, z3, x3f, x3b = cross_fwd(f"cross_fwd_{l}", x2b, x2f, mwq, mwo, l, kvm, vec(lng[l, 2]), vec(lnb[l, 2]))
        sv.update(kvm=kvm, crq=cq_, cro=co_, z3=z3, x3b=x3b)
        gate, up, act = ffn_up(f"ffn2_up_{l}", x3b, g_f2w13, l)
        z4, xf, xb = proj_res_ln(f"ffn2_down_{l}", [act], [f2w2], [l], x3f, vec(lng[l, 3]), vec(lnb[l, 3]), 0.5)
        sv.update(gate2=gate, up2=up, act2=act, z4=z4)
        saved.append(sv)

    dy, loss_blk = loss_grad("loss_grad", xf, loss_target[0])
    loss = lax.psum(loss_blk[0, 0], ("x", "y", "c"))

    G = dict(f1w13=None, f1w2=None, mwq=None, mwkv=None, mwo=None, f2w13=None, f2w2=None)
    small = {k_: [None] * L for k_ in ("win", "wuq", "wukv", "wout", "pool_w", "pool_scale", "gq", "gkv", "lng", "lnb")}
    for l in reversed(range(L)):
        sv = saved[l]
        dlg, dlb = [None] * 4, [None] * 4
        dzb, dres, dlg[3], dlb[3] = ln_bwd(f"ln4_bwd_{l}", dy, sv["z4"], vec(lng[l, 3]), 0.5)
        dh = ffn_bwd_da(f"ffn2_bwd_da_{l}", dzb, f2w2, l, sv["gate2"], sv["up2"])
        G["f2w2"] = mm_tn(f"ffn2_dw2_{l}", sv["act2"], dzb, "nat", l, G["f2w2"])
        G["f2w13"] = mm_tn(f"ffn2_dw13_{l}", sv["x3b"], dh, "shard", l, G["f2w13"])
        dy = ffn_dx(f"ffn2_dx_{l}", dh, g_f2w13, l, dres)
        dzb, dres, dlg[2], dlb[2] = ln_bwd(f"ln3_bwd_{l}", dy, sv["z3"], vec(lng[l, 2]), 1.0)
        dqc, dkvm = cross_bwd(f"cross_bwd_{l}", dzb, mwo, l, sv["crq"], sv["kvm"])
        G["mwo"] = mm_tn(f"cross_dwo_{l}", sv["cro"], dzb, "nat", l, G["mwo"])
        G["mwq"] = mm_tn(f"cross_dwq_{l}", sv["x2b"], dqc, "nat", l, G["mwq"])
        G["mwkv"] = mm_tn(f"cross_dwkv_{l}", memb, dkvm, "shard", l, G["mwkv"])
        dy = mm_nt_res(f"cross_dx_{l}", [dqc], [mwq], [l], dres, f32)
        dzb, dres, dlg[1], dlb[1] = ln_bwd(f"ln2_bwd_{l}", dy, sv["z2"], vec(lng[l, 1]), 1.0)
        dyp = mm_nt_res(f"mix_dpool_{l}", [dzb], [wout_pool], [l], None, bf16)
        do = mm_nt_res(f"mix_do_{l}", [dzb], [wout_mla], [l], None, bf16)
        dwo_p = mm_tn(f"mix_dwout_pool_{l}", sv["ypool"], dzb)
        dwo_m = mm_tn(f"mix_dwout_mla_{l}", sv["o"], dzb)
        small["wout"][l] = jnp.concatenate([dwo_p, dwo_m], axis=0)
        dq, dk, dv = mla_attn_bwd(f"mla_bwd_{l}", sv["q"], sv["k"], sv["v"], sv["o"], do, sv["lse"])
        dqe, dkv, dh_rest, dgq, dgkv = mix_post_bwd(f"mix_post_bwd_{l}", dq, dk, dv, wq_ext, wukv, l, sv["cq"], sv["ckv"],
                                                     vec(q_norm_g[l]), vec(kv_norm_g[l]), cs)
        du, dyw, dscale = pool_bwd(f"pool_bwd_{l}", dyp, sv["dpool"], wbd[l], vec(pool_scale[l]))
        dwq_e = mm_tn(f"mix_dwuq_{l}", sv["cqn"], dqe).reshape(Q_LORA, MLA_HEADS, 256)
        small["wuq"][l] = jnp.concatenate(
            [dwq_e[..., :QK_NOPE], _unswap_add(dwq_e[..., QK_NOPE:QK_NOPE + QK_ROPE], dwq_e[..., QK_NOPE + QK_ROPE:])],
            axis=-1).reshape(Q_LORA, MLA_HEADS * (QK_NOPE + QK_ROPE))
        small["wukv"][l] = mm_tn(f"mix_dwukv_{l}", sv["ckvn"], dkv)
        dwbd = mm_tn(f"pool_dw_{l}", sv["dpool"], dyw)
        small["pool_w"][l] = jnp.stack([dwbd[64 * gi:64 * gi + 64, 64 * gi:64 * gi + 64] for gi in range(4)])
        small["pool_scale"][l], small["gq"][l], small["gkv"][l] = dscale[0], dgq[0], dgkv[0]
        dh_ext = jnp.concatenate([du, dh_rest], axis=1)
        dwin_e = mm_tn(f"mix_dwin_{l}", sv["x1b"], dh_ext)
        small["win"][l] = jnp.concatenate(
            [dwin_e[:, :D_IN - QK_ROPE], _unswap_add(dwin_e[:, D_IN - QK_ROPE:D_IN], dwin_e[:, D_IN:])], axis=-1)
        dy = mm_nt_res(f"mix_dx_{l}", [dh_ext], [win_ext], [l], dres, f32)
        dzb, dres, dlg[0], dlb[0] = ln_bwd(f"ln1_bwd_{l}", dy, sv["z1"], vec(lng[l, 0]), 0.5)
        dh = ffn_bwd_da(f"ffn1_bwd_da_{l}", dzb, f1w2, l, sv["gate1"], sv["up1"])
        G["f1w2"] = mm_tn(f"ffn1_dw2_{l}", sv["act1"], dzb, "nat", l, G["f1w2"])
        G["f1w13"] = mm_tn(f"ffn1_dw13_{l}", sv["x0b"], dh, "shard", l, G["f1w13"])
        dy = ffn_dx(f"ffn1_dx_{l}", dh, g_f1w13, l, dres)
        small["lng"][l] = jnp.concatenate(dlg, axis=0)
        small["lnb"][l] = jnp.concatenate(dlb, axis=0)
    grad_x = dy[None]

    row_shards = lambda a, K: a.reshape(L, N_CHIPS, K // N_CHIPS, a.shape[-1])
    g_list = [G["f1w13"], row_shards(G["f1w2"], D_FF),
              jnp.stack(small["win"]).reshape(L, N_CHIPS, D_MODEL // N_CHIPS, D_IN),
              _to_col_shards(jnp.stack(small["wuq"])), _to_col_shards(jnp.stack(small["wukv"])),
              jnp.stack(small["wout"]).reshape(L, N_CHIPS, D_MODEL // N_CHIPS, D_MODEL),
              row_shards(G["mwq"], D_MODEL), G["mwkv"], row_shards(G["mwo"], D_MODEL),
              G["f2w13"], row_shards(G["f2w2"], D_FF)]
    big_grads = reduce_grads(g_list)

    rep = [jnp.stack(small["pool_w"]).reshape(-1), jnp.stack(small["pool_scale"]).reshape(-1),
           jnp.stack(small["gq"]).reshape(-1), jnp.stack(small["gkv"]).reshape(-1),
           jnp.stack(small["lng"]).reshape(-1), jnp.stack(small["lnb"]).reshape(-1)]
    sizes = [r.shape[0] for r in rep]
    packed = jnp.concatenate(rep)
    pad = (-packed.shape[0]) % 1024
    tot = allsum_small("allsum_small_grads", jnp.pad(packed, (0, pad)).reshape(-1, 128)).reshape(-1)
    offs = [0]
    for s_ in sizes:
        offs.append(offs[-1] + s_)
    parts = [tot[offs[i]:offs[i + 1]] for i in range(len(sizes))]
    g_pool_w = parts[0].reshape(pool_w.shape)
    g_pool_scale = parts[1].reshape(pool_scale.shape)
    g_gq = parts[2].reshape(q_norm_g.shape)
    g_gkv = parts[3].reshape(kv_norm_g.shape)
    shard_cols = lambda a: lax.dynamic_slice_in_dim(a.reshape(L, 4, D_MODEL), chip * (D_MODEL // N_CHIPS),
                                                    D_MODEL // N_CHIPS, axis=2)
    g_lng, g_lnb = shard_cols(parts[4]), shard_cols(parts[5])

    (gf1w13, gf1w2, gwin, gwuq, gwukv, gwout, gmwq, gmwkv, gmwo, gf2w13, gf2w2) = big_grads
    grads = [g_lng, g_lnb, gf1w13, gf1w2, gwin, g_pool_w, g_pool_scale, g_gq, gwuq, g_gkv, gwukv, gwout, gmwq, gmwkv,
             gmwo, gf2w13, gf2w2]
    ws = [ln_g, ln_b, ffn1_w13, ffn1_w2, w_in, pool_w, pool_scale, q_norm_g, w_uq, kv_norm_g, w_ukv, w_out, mem_wq,
          mem_wkv, mem_wo, ffn2_w13, ffn2_w2]
    ms = [m_ln_g, m_ln_b, m_ffn1_w13, m_ffn1_w2, m_w_in, m_pool_w, m_pool_scale, m_q_norm_g, m_w_uq, m_kv_norm_g, m_w_ukv,
          m_w_out, m_mem_wq, m_mem_wkv, m_mem_wo, m_ffn2_w13, m_ffn2_w2]
    vs = [v_ln_g, v_ln_b, v_ffn1_w13, v_ffn1_w2, v_w_in, v_pool_w, v_pool_scale, v_q_norm_g, v_w_uq, v_kv_norm_g, v_w_ukv,
          v_w_out, v_mem_wq, v_mem_wkv, v_mem_wo, v_ffn2_w13, v_ffn2_w2]
    deltas, new_ms, new_vs = [], [], []
    for a, (w_, g_, m_, v_) in enumerate(zip(ws, grads, ms, vs)):
        d_, nm_, nv_ = adamw(f"adamw_{a}", w_, g_.reshape(w_.shape), m_, v_)
        deltas.append(d_)
        new_ms.append(nm_)
        new_vs.append(nv_)
    grads = [g_.reshape(w_.shape) for g_, w_ in zip(grads, ws)]
    return (loss, grad_x, *grads, *deltas, *new_ms, *new_vs)


_HBM_SPEC = pl.BlockSpec(memory_space=pltpu.HBM)
_SEM_SPEC = pl.BlockSpec(memory_space=pltpu.SEMAPHORE)
_ANY_SPEC = pl.BlockSpec(memory_space=pl.ANY)
_DATAFLOW = pltpu.SideEffectType.DATAFLOW_SIDE_EFFECTING


def _split_call(name, body_fn, bufs, sems_in, sems_out_sizes, after):
    nb, ni, no = len(bufs), len(sems_in), len(sems_out_sizes)
    has_after = after is not None

    def body(*refs):
        k = nb + ni + (1 if has_after else 0)
        body_fn(refs[:nb], refs[nb:nb + ni], refs[k:k + no])
        refs[-1][...] = jnp.zeros((8, 128), f32)

    outs = pl.pallas_call(
        body, name=name,
        in_specs=[_HBM_SPEC] * nb + [_SEM_SPEC] * ni + ([_ANY_SPEC] if has_after else []),
        out_specs=[_SEM_SPEC] * no + [_HBM_SPEC] * nb + [pl.BlockSpec(memory_space=pltpu.VMEM)],
        out_shape=[pltpu.SemaphoreType.DMA((s,)) for s in sems_out_sizes]
        + [pltpu.HBM(b.shape, b.dtype) for b in bufs] + [SDS((8, 128), f32)],
        input_output_aliases={i: no + i for i in range(nb)},
        compiler_params=pltpu.CompilerParams(has_side_effects=_DATAFLOW),
    )(*[pltpu.with_memory_space_constraint(b, pltpu.HBM) for b in bufs], *sems_in, *([after] if has_after else []))
    return list(outs[no:no + nb]), list(outs[:no]), outs[-1]


def _rcopy(src, dst, ssem, rsem, to):
    return pltpu.make_async_remote_copy(src_ref=src, dst_ref=dst, send_sem=ssem, recv_sem=rsem, device_id=to,
                                        device_id_type=MESH)


def gather_start(name, groups, after):
    flat = [b for bufs, _ in groups for b in bufs]
    sizes = [3 * len(bufs) for bufs, _ in groups for _ in range(2)]

    def body_fn(b_in, s_in, s_out):
        x, y, c = _me()
        q = 2 * x + y
        chips = _other_chips(x, y)
        pos = 0
        for gi, (bufs, owner) in enumerate(groups):
            refs = b_in[pos:pos + len(bufs)]
            pos += len(bufs)

            @pl.when(c == owner)
            def _(refs=refs, send=s_out[2 * gi], recv=s_out[2 * gi + 1]):
                for a, r in enumerate(refs):
                    for k, (cx, cy) in enumerate(chips):
                        _rcopy(r.at[q], r.at[q], send.at[3 * a + k], recv.at[3 * a + k], (cx, cy, c)).start()

    outs, sems, token = _split_call(name, body_fn, flat, [], sizes, after)
    res, pos = [], 0
    for gi, (bufs, owner) in enumerate(groups):
        res.append((outs[pos:pos + len(bufs)], sems[2 * gi], sems[2 * gi + 1], owner))
        pos += len(bufs)
    return res, token


def gather_forward(name, grp, after):
    bufs, send, recv, owner = grp
    n3 = 3 * len(bufs)

    def body_fn(b_in, s_in, s_out):
        x, y, c = _me()
        q = 2 * x + y
        sibling = (x, y, 1 - c)
        chips = _other_chips(x, y)

        @pl.when(c == owner)
        def _():
            for a, r in enumerate(b_in):
                for k, (cx, cy) in enumerate(chips):
                    i = 3 * a + k
                    land = r.at[2 * cx + cy]
                    _rcopy(r.at[q], r.at[q], s_in[0].at[i], s_in[1].at[i], (cx, cy, c)).wait_send()
                    _rcopy(land, land, s_in[0].at[i], s_in[1].at[i], (cx, cy, c)).wait_recv()
                    _rcopy(land, land, s_out[0].at[i], s_out[1].at[i], sibling).start()

    outs, sems, token = _split_call(name, body_fn, bufs, [send, recv], [n3, n3], after)
    return (outs, sems[0], sems[1], owner), token


def gather_finish(name, grp, after):
    bufs, fsend, frecv, owner = grp

    def body_fn(b_in, s_in, s_out):
        x, y, c = _me()
        sibling = (x, y, 1 - c)
        chips = _other_chips(x, y)

        def each(wait):
            for a, r in enumerate(b_in):
                for k, (cx, cy) in enumerate(chips):
                    land = r.at[2 * cx + cy]
                    wait(_rcopy(land, land, s_in[0].at[3 * a + k], s_in[1].at[3 * a + k], sibling))

        @pl.when(c == owner)
        def _():
            each(lambda cp: cp.wait_send())

        @pl.when(c != owner)
        def _():
            each(lambda cp: cp.wait_recv())

    outs, _, _ = _split_call(name, body_fn, bufs, [fsend, frecv], [], after)
    return outs


def pair_send_start(name, gs, owner, after):
    n = len(gs)
    lands = [lax.empty(g.shape, g.dtype) for g in gs]

    def body_fn(b_in, s_in, s_out):
        x, y, c = _me()

        @pl.when(c == 1 - owner)
        def _():
            for a in range(n):
                _rcopy(b_in[a], b_in[n + a], s_out[0].at[a], s_out[1].at[a], (x, y, owner)).start()

    outs, sems, token = _split_call(name, body_fn, list(gs) + lands, [], [n, n], after)
    return (outs[:n], outs[n:], sems[0], sems[1], owner), token


def pair_send_wait(name, st, after):
    gs, lands, send, recv, owner = st
    n = len(gs)

    def body_fn(b_in, s_in, s_out):
        x, y, c = _me()

        @pl.when(c == 1 - owner)
        def _():
            for a in range(n):
                _rcopy(b_in[a], b_in[n + a], s_in[0].at[a], s_in[1].at[a], (x, y, owner)).wait_send()

        @pl.when(c == owner)
        def _():
            for a in range(n):
                _rcopy(b_in[a], b_in[n + a], s_in[0].at[a], s_in[1].at[a], (x, y, 1 - owner)).wait_recv()

    outs, _, _ = _split_call(name, body_fn, list(gs) + list(lands), [send, recv], [], after)
    return outs[:n], outs[n:]


def chip_exchange_start(name, psums, owner, after):
    n = len(psums)
    lands = [lax.empty((3,) + p.shape[1:], p.dtype) for p in psums]

    def body_fn(b_in, s_in, s_out):
        x, y, c = _me()
        chips = _other_chips(x, y)

        @pl.when(c == owner)
        def _():
            for a in range(n):
                for k, (cx, cy) in enumerate(chips):
                    _rcopy(b_in[a].at[2 * cx + cy], b_in[n + a].at[k], s_out[0].at[3 * a + k], s_out[1].at[3 * a + k],
                           (cx, cy, c)).start()

    outs, sems, token = _split_call(name, body_fn, list(psums) + lands, [], [3 * n, 3 * n], after)
    return (outs[:n], outs[n:], sems[0], sems[1], owner), token


def chip_exchange_wait(name, st, after):
    psums, lands, send, recv, owner = st
    n = len(psums)

    def body_fn(b_in, s_in, s_out):
        x, y, c = _me()
        chips = _other_chips(x, y)

        @pl.when(c == owner)
        def _():
            for a in range(n):
                for k, (cx, cy) in enumerate(chips):
                    cp = _rcopy(b_in[a].at[2 * cx + cy], b_in[n + a].at[k], s_in[0].at[3 * a + k], s_in[1].at[3 * a + k],
                                (cx, cy, c))
                    cp.wait_send()
                    cp.wait_recv()

    outs, _, _ = _split_call(name, body_fn, list(psums) + list(lands), [send, recv], [], after)
    return outs[:n], outs[n:]


def pair_sum(name, g, recv):
    shape = g.shape
    cols = shape[-1]
    rows = math.prod(shape[:-1])
    tr = _row_tile(rows, cols)

    def body(g_ref, r_ref, o_ref):
        o_ref[...] = (g_ref[...] + r_ref[...]).astype(bf16)

    blk = pl.BlockSpec((tr, cols), lambda i: (i, 0))
    out = pl.pallas_call(
        body, name=name, grid=(rows // tr,), in_specs=[blk, blk], out_specs=blk, out_shape=SDS((rows, cols), bf16),
        compiler_params=_cp("parallel"),
    )(g.reshape(rows, cols), recv.reshape(rows, cols))
    return out.reshape(shape)


def chip_sum(name, psum, recv, q_arr, layer, prev):
    shard = psum.shape[1:]
    cols = shard[-1]
    rows = math.prod(shard[:-1])
    tr = _row_tile(rows, cols)

    def body(q_ref, p_ref, r_ref, *rest):
        rest[-1][0] = ((p_ref[0].astype(f32) + r_ref[0].astype(f32)) + r_ref[1].astype(f32)) + r_ref[2].astype(f32)

    in_specs = [pl.BlockSpec((1, tr, cols), lambda i, q_ref: (q_ref[0], i, 0)),
                pl.BlockSpec((3, tr, cols), lambda i, q_ref: (0, i, 0))]
    args = [q_arr, psum.reshape(N_CHIPS, rows, cols), recv.reshape(3, rows, cols)]
    aliases = {}
    if prev is not None:
        in_specs.append(pl.BlockSpec(memory_space=pl.ANY))
        args.append(prev.reshape(DEPTH, rows, cols))
        aliases = {3: 0}
    out = pl.pallas_call(
        body, name=name,
        grid_spec=pltpu.PrefetchScalarGridSpec(
            num_scalar_prefetch=1, grid=(rows // tr,), in_specs=in_specs,
            out_specs=pl.BlockSpec((1, tr, cols), lambda i, q_ref: (layer, i, 0))),
        out_shape=SDS((DEPTH, rows, cols), f32), input_output_aliases=aliases, compiler_params=_cp("parallel"),
    )(*args)
    return out.reshape((DEPTH,) + shard)


def _after(x, *deps):
    return lax.optimization_barrier((x, *deps))[0]


W_NAMES = ("f1w13", "f1w2", "win", "wuq", "wukv", "wout", "mwq", "mwkv", "mwo", "f2w13", "f2w2")
MIX_NAMES = ("win", "wuq", "wukv")
MID_NAMES = ("wout", "mwq", "mwkv", "mwo")
FFN2_NAMES = ("f2w13", "f2w2")


def kernel(x, mem, positions, ln_g, ln_b, ffn1_w13, ffn1_w2, w_in, pool_w, pool_scale, q_norm_g, w_uq, kv_norm_g, w_ukv, w_out, mem_wq, mem_wkv, mem_wo, ffn2_w13, ffn2_w2, loss_target, m_ln_g, m_ln_b, m_ffn1_w13, m_ffn1_w2, m_w_in, m_pool_w, m_pool_scale, m_q_norm_g, m_w_uq, m_kv_norm_g, m_w_ukv, m_w_out, m_mem_wq, m_mem_wkv, m_mem_wo, m_ffn2_w13, m_ffn2_w2, v_ln_g, v_ln_b, v_ffn1_w13, v_ffn1_w2, v_w_in, v_pool_w, v_pool_scale, v_q_norm_g, v_w_uq, v_kv_norm_g, v_w_ukv, v_w_out, v_mem_wq, v_mem_wkv, v_mem_wo, v_ffn2_w13, v_ffn2_w2):
    L = DEPTH
    qx, qy, _ = _me()
    chip = 2 * qx + qy
    vec = lambda a: a.reshape(1, -1)

    shards = dict(zip(W_NAMES, (ffn1_w13, ffn1_w2, w_in, w_uq, w_ukv, w_out, mem_wq, mem_wkv, mem_wo, ffn2_w13, ffn2_w2)))

    def place(sh):
        return lax.dynamic_update_slice(jnp.zeros((N_CHIPS,) + sh.shape, bf16), sh.astype(bf16)[None],
                                        (chip,) + (0,) * sh.ndim)

    bufs = [{n: place(shards[n][l]) for n in W_NAMES} for l in range(L)]
    gw = [dict(), dict()]
    (g0,), tok = gather_start("gather_a_start", [([bufs[0]["f1w13"], bufs[0]["f1w2"]], 0)], None)
    g0, tok = gather_forward("gather_a_forward", g0, None)
    gw[0]["f1w13"], gw[0]["f1w2"] = gather_finish("gather_a_finish", g0, None)
    (g_mix, g_mid, g_ffn2, g_l1), tok_b = gather_start(
        "gather_b_start",
        [([bufs[0][n] for n in MIX_NAMES], 0), ([bufs[0][n] for n in MID_NAMES], 0), ([bufs[0][n] for n in FFN2_NAMES], 0),
         ([bufs[1][n] for n in W_NAMES], 1)], tok)

    ln_pad = jnp.zeros((2, L, 4, N_CHIPS, D_MODEL // N_CHIPS), f32)
    ln_pad = lax.dynamic_update_slice(ln_pad, jnp.stack([ln_g, ln_b])[:, :, :, None, :], (0, 0, 0, chip, 0))
    ln_full = allsum_small("allsum_ln", ln_pad.reshape(-1, 128)) * 0.5
    ln_full = ln_full.reshape(2, L, 4, D_MODEL)
    lng, lnb = ln_full[0], ln_full[1]

    half = QK_ROPE // 2
    inv_freq = ROPE_BASE ** (-jnp.arange(half, dtype=f32) / half)
    ang = positions[0].astype(f32)[:, None] * inv_freq
    cos, sin = jnp.cos(ang), jnp.sin(ang)
    cs = jnp.concatenate([cos, cos, sin, sin], axis=-1)

    memb = mem[0].astype(bf16)
    xf = x[0]
    xb = xf.astype(bf16)
    dep = (tok_b,)

    saved, W = [], [None, None]
    for l in range(L):
        sv = {}
        if l == 1:
            gl1 = gather_finish("gather_l1_finish", g_l1, xb)
            gw[1] = dict(zip(W_NAMES, gl1))
        sv["x0b"] = xb
        f1w13 = gw[l]["f1w13"][None]
        gate, up, act = ffn_up(f"ffn1_up_{l}", xb, f1w13, 0, dep)
        dep = ()
        if l == 0:
            g_mix, _ = gather_forward("gather_mix_forward", g_mix, act)
        z1, x1f, x1b = proj_res_ln(f"ffn1_down_{l}", [act], [gw[l]["f1w2"].reshape(1, D_FF, D_MODEL)], [0], xf,
                                   vec(lng[l, 0]), vec(lnb[l, 0]), 0.5)
        sv.update(gate1=gate, up1=up, act1=act, z1=z1, x1b=x1b)
        if l == 0:
            gw[0].update(zip(MIX_NAMES, gather_finish("gather_mix_finish", g_mix, x1b)))
            g_mid, _ = gather_forward("gather_mid_forward", g_mid, x1b)
        win = gw[l]["win"].reshape(D_MODEL, D_IN)
        win_ext = jnp.concatenate([win, _swap_half(win[:, D_IN - QK_ROPE:])], axis=-1)[None]
        wuq = _from_col_shards(gw[l]["wuq"]).reshape(Q_LORA, MLA_HEADS, QK_NOPE + QK_ROPE)
        wq_ext = jnp.concatenate([wuq, _swap_half(wuq[..., QK_NOPE:])], axis=-1).transpose(1, 0, 2)[None]
        wukv = _from_col_shards(gw[l]["wukv"])[None]
        wbd = _block_diag(pool_w[l][None].astype(bf16))[0]
        u, cq, ckv, cqn, ckvn, q, k, v = mix_pre(f"mix_pre_{l}", x1b, win_ext, wq_ext, wukv, 0,
                                                   vec(q_norm_g[l]), vec(kv_norm_g[l]), cs)
        dpool, ypool = pool_fwd(f"pool_fwd_{l}", u, wbd, vec(pool_scale[l]))
        o, lse = mla_attn_fwd(f"mla_fwd_{l}", q, k, v)
        if l == 0:
            gw[0].update(zip(MID_NAMES, gather_finish("gather_mid_finish", g_mid, o)))
            g_ffn2, tok_f = gather_forward("gather_ffn2_forward", g_ffn2, o)
            g_l1, tok_l = gather_forward("gather_l1_forward", g_l1, o)
            dep = (tok_f, tok_l)
        wout = gw[l]["wout"].reshape(D_MODEL, D_MODEL)
        wout_pool, wout_mla = wout[None, :POOL_WIDTH], wout[None, POOL_WIDTH:]
        mwq = gw[l]["mwq"].reshape(1, D_MODEL, D_MODEL)
        mwo = gw[l]["mwo"].reshape(1, D_MODEL, D_MODEL)
        mwkv = gw[l]["mwkv"][None]
        z2, x2f, x2b = proj_res_ln(f"mix_out_{l}", [ypool, o], [wout_pool, wout_mla], [0, 0], x1f,
                                   vec(lng[l, 1]), vec(lnb[l, 1]), 1.0, dep)
        dep = ()
        sv.update(cq=cq, ckv=ckv, cqn=cqn, ckvn=ckvn, q=q, k=k, v=v, dpool=dpool, ypool=ypool, o=o, lse=lse, z2=z2, x2b=x2b)
        kvm = mm_nn_shard(f"mem_kv_{l}", memb, mwkv, 0)
        cq_, co_, z3, x3f, x3b = cross_fwd(f"cross_fwd_{l}", x2b, x2f, mwq, mwo, 0, kvm, vec(lng[l, 2]), vec(lnb[l, 2]))
        sv.update(kvm=kvm, crq=cq_, cro=co_, z3=z3, x3b=x3b)
        if l == 0:
            gw[0].update(zip(FFN2_NAMES, gather_finish("gather_ffn2_finish", g_ffn2, x3b)))
        f2w13 = gw[l]["f2w13"][None]
        f2w2 = gw[l]["f2w2"].reshape(1, D_FF, D_MODEL)
        gate, up, act = ffn_up(f"ffn2_up_{l}", x3b, f2w13, 0)
        z4, xf, xb = proj_res_ln(f"ffn2_down_{l}", [act], [f2w2], [0], x3f, vec(lng[l, 3]), vec(lnb[l, 3]), 0.5)
        sv.update(gate2=gate, up2=up, act2=act, z4=z4)
        W[l] = dict(f1w13=f1w13, f1w2=gw[l]["f1w2"].reshape(1, D_FF, D_MODEL), win_ext=win_ext, wq_ext=wq_ext, wukv=wukv,
                    wbd=wbd, wout_pool=wout_pool, wout_mla=wout_mla, mwq=mwq, mwo=mwo, f2w13=f2w13, f2w2=f2w2)
        saved.append(sv)

    dy, loss_blk = loss_grad("loss_grad", xf, loss_target[0])
    loss = lax.psum(loss_blk[0, 0], ("x", "y", "c"))

    row_shards = lambda a: a.reshape(N_CHIPS, a.shape[0] // N_CHIPS, a.shape[1])
    small = {k_: [None] * L for k_ in ("pool_w", "pool_scale", "gq", "gkv", "lng", "lnb")}
    q_arr = jnp.reshape(chip, (1,)).astype(jnp.int32)
    rest_names = [n for n in W_NAMES if n not in ("f1w13", "f1w2")]

    def red_begin(tag, gs, owner):
        return pair_send_start(f"pair_send_start_{tag}", gs, owner, None)

    def red_mid(tag, st, owner, after):
        gs_, lands_ = pair_send_wait(f"pair_send_wait_{tag}", st, after)
        ps = [pair_sum(f"pair_sum_{tag}_{a}", g_, r_) for a, (g_, r_) in enumerate(zip(gs_, lands_))]
        return chip_exchange_start(f"chip_exchange_start_{tag}", ps, owner, None)

    def red_end(tag, st, layer, prevs, after):
        ps, lands_ = chip_exchange_wait(f"chip_exchange_wait_{tag}", st, after)
        return [chip_sum(f"chip_sum_{tag}_{a}", p_, r_, q_arr, layer, s_)
                for a, (p_, r_, s_) in enumerate(zip(ps, lands_, prevs))]

    st_p1 = st_c1 = st_pa = st_ca = None
    for l in reversed(range(L)):
        sv, w = saved[l], W[l]
        g = {}
        dlg, dlb = [None] * 4, [None] * 4
        dzb, dres, dlg[3], dlb[3] = ln_bwd(f"ln4_bwd_{l}", dy, sv["z4"], vec(lng[l, 3]), 0.5, dep)
        dep = ()
        dh = ffn_bwd_da(f"ffn2_bwd_da_{l}", dzb, w["f2w2"], 0, sv["gate2"], sv["up2"])
        g["f2w2"] = row_shards(mm_tn(f"ffn2_dw2_{l}", sv["act2"], dzb))
        g["f2w13"] = mm_tn(f"ffn2_dw13_{l}", sv["x3b"], dh, True)
        dy = ffn_dx(f"ffn2_dx_{l}", dh, w["f2w13"], 0, dres)
        if l == 0:
            st_c1, tok = red_mid("l1", st_p1, 1, dy)
            dep = (tok, g["f2w2"], g["f2w13"])
        dzb, dres, dlg[2], dlb[2] = ln_bwd(f"ln3_bwd_{l}", dy, sv["z3"], vec(lng[l, 2]), 1.0, dep)
        dep = ()
        dqc, dkvm = cross_bwd(f"cross_bwd_{l}", dzb, w["mwo"], 0, sv["crq"], sv["kvm"])
        g["mwo"] = row_shards(mm_tn(f"cross_dwo_{l}", sv["cro"], dzb))
        g["mwq"] = row_shards(mm_tn(f"cross_dwq_{l}", sv["x2b"], dqc))
        g["mwkv"] = mm_tn(f"cross_dwkv_{l}", memb, dkvm, True)
        dy = mm_nt_res(f"cross_dx_{l}", [dqc], [w["mwq"]], [0], dres, f32)
        dzb, dres, dlg[1], dlb[1] = ln_bwd(f"ln2_bwd_{l}", dy, sv["z2"], vec(lng[l, 1]), 1.0)
        dyp = mm_nt_res(f"mix_dpool_{l}", [dzb], [w["wout_pool"]], [0], None, bf16)
        do = mm_nt_res(f"mix_do_{l}", [dzb], [w["wout_mla"]], [0], None, bf16)
        dwo_p = mm_tn(f"mix_dwout_pool_{l}", sv["ypool"], dzb)
        dwo_m = mm_tn(f"mix_dwout_mla_{l}", sv["o"], dzb)
        g["wout"] = row_shards(jnp.concatenate([dwo_p, dwo_m], axis=0))
        dq, dk, dv = mla_attn_bwd(f"mla_bwd_{l}", sv["q"], sv["k"], sv["v"], sv["o"], do, sv["lse"])
        dqe, dkv, dh_rest, dgq, dgkv = mix_post_bwd(f"mix_post_bwd_{l}", dq, dk, dv, w["wq_ext"], w["wukv"], 0, sv["cq"],
                                                     sv["ckv"], vec(q_norm_g[l]), vec(kv_norm_g[l]), cs)
        du, dyw, dscale = pool_bwd(f"pool_bwd_{l}", dyp, sv["dpool"], w["wbd"], vec(pool_scale[l]))
        dwq_e = mm_tn(f"mix_dwuq_{l}", sv["cqn"], dqe).reshape(Q_LORA, MLA_HEADS, 256)
        g["wuq"] = _to_col_shards(jnp.concatenate(
            [dwq_e[..., :QK_NOPE], _unswap_add(dwq_e[..., QK_NOPE:QK_NOPE + QK_ROPE], dwq_e[..., QK_NOPE + QK_ROPE:])],
            axis=-1).reshape(Q_LORA, MLA_HEADS * (QK_NOPE + QK_ROPE)))
        g["wukv"] = _to_col_shards(mm_tn(f"mix_dwukv_{l}", sv["ckvn"], dkv))
        dwbd = mm_tn(f"pool_dw_{l}", sv["dpool"], dyw)
        small["pool_w"][l] = jnp.stack([dwbd[64 * gi:64 * gi + 64, 64 * gi:64 * gi + 64] for gi in range(4)])
        small["pool_scale"][l], small["gq"][l], small["gkv"][l] = dscale[0], dgq[0], dgkv[0]
        dh_ext = jnp.concatenate([du, dh_rest], axis=1)
        dwin_e = mm_tn(f"mix_dwin_{l}", sv["x1b"], dh_ext)
        g["win"] = row_shards(jnp.concatenate(
            [dwin_e[:, :D_IN - QK_ROPE], _unswap_add(dwin_e[:, D_IN - QK_ROPE:D_IN], dwin_e[:, D_IN:])], axis=-1))
        dy = mm_nt_res(f"mix_dx_{l}", [dh_ext], [w["win_ext"]], [0], dres, f32)
        if l == 0:
            st_pa, tok = red_begin("a0", [g[n] for n in rest_names], 0)
            dep = (tok,)
        dzb, dres, dlg[0], dlb[0] = ln_bwd(f"ln1_bwd_{l}", dy, sv["z1"], vec(lng[l, 0]), 0.5, dep)
        dep = ()
        dh = ffn_bwd_da(f"ffn1_bwd_da_{l}", dzb, w["f1w2"], 0, sv["gate1"], sv["up1"])
        dy = ffn_dx(f"ffn1_dx_{l}", dh, w["f1w13"], 0, dres)
        if l == 0:
            st_ca, tok = red_mid("a0", st_pa, 0, dy)
            dep = (tok,)
        g["f1w2"] = row_shards(mm_tn(f"ffn1_dw2_{l}", sv["act1"], dzb, False, dep))
        g["f1w13"] = mm_tn(f"ffn1_dw13_{l}", sv["x0b"], dh, True, dep)
        dep = ()
        small["lng"][l] = jnp.concatenate(dlg, axis=0)
        small["lnb"][l] = jnp.concatenate(dlb, axis=0)
        if l == 1:
            st_p1, tok = red_begin("l1", [g[n] for n in W_NAMES], 1)
            dep = (tok,)
    grad_x = dy[None]

    st_pb, _ = red_begin("b0", [g["f1w13"], g["f1w2"]], 0)
    sums1 = dict(zip(W_NAMES, red_end("l1", st_c1, 1, [None] * len(W_NAMES), g["f1w13"])))
    st_cb, _ = red_mid("b0", st_pb, 0, sums1["f1w13"])
    sums0 = dict(zip(rest_names, red_end("a0", st_ca, 0, [sums1[n] for n in rest_names], sums1["f1w2"])))
    sums0["f1w13"], sums0["f1w2"] = red_end("b0", st_cb, 0, [sums1["f1w13"], sums1["f1w2"]], sums0["f2w2"])
    big_grads = pair_share([sums0[n] for n in W_NAMES])

    rep = [jnp.stack(small["pool_w"]).reshape(-1), jnp.stack(small["pool_scale"]).reshape(-1),
           jnp.stack(small["gq"]).reshape(-1), jnp.stack(small["gkv"]).reshape(-1),
           jnp.stack(small["lng"]).reshape(-1), jnp.stack(small["lnb"]).reshape(-1)]
    sizes = [r.shape[0] for r in rep]
    packed = jnp.concatenate(rep)
    pad = (-packed.shape[0]) % 1024
    tot = allsum_small("allsum_small_grads", jnp.pad(packed, (0, pad)).reshape(-1, 128)).reshape(-1)
    offs = [0]
    for s_ in sizes:
        offs.append(offs[-1] + s_)
    parts = [tot[offs[i]:offs[i + 1]] for i in range(len(sizes))]
    g_pool_w = parts[0].reshape(pool_w.shape)
    g_pool_scale = parts[1].reshape(pool_scale.shape)
    g_gq = parts[2].reshape(q_norm_g.shape)
    g_gkv = parts[3].reshape(kv_norm_g.shape)
    shard_cols = lambda a: lax.dynamic_slice_in_dim(a.reshape(L, 4, D_MODEL), chip * (D_MODEL // N_CHIPS),
                                                    D_MODEL // N_CHIPS, axis=2)
    g_lng, g_lnb = shard_cols(parts[4]), shard_cols(parts[5])

    (gf1w13, gf1w2, gwin, gwuq, gwukv, gwout, gmwq, gmwkv, gmwo, gf2w13, gf2w2) = big_grads
    grads = [g_lng, g_lnb, gf1w13, gf1w2, gwin, g_pool_w, g_pool_scale, g_gq, gwuq, g_gkv, gwukv, gwout, gmwq, gmwkv,
             gmwo, gf2w13, gf2w2]
    ws = [ln_g, ln_b, ffn1_w13, ffn1_w2, w_in, pool_w, pool_scale, q_norm_g, w_uq, kv_norm_g, w_ukv, w_out, mem_wq,
          mem_wkv, mem_wo, ffn2_w13, ffn2_w2]
    ms = [m_ln_g, m_ln_b, m_ffn1_w13, m_ffn1_w2, m_w_in, m_pool_w, m_pool_scale, m_q_norm_g, m_w_uq, m_kv_norm_g, m_w_ukv,
          m_w_out, m_mem_wq, m_mem_wkv, m_mem_wo, m_ffn2_w13, m_ffn2_w2]
    vs = [v_ln_g, v_ln_b, v_ffn1_w13, v_ffn1_w2, v_w_in, v_pool_w, v_pool_scale, v_q_norm_g, v_w_uq, v_kv_norm_g, v_w_ukv,
          v_w_out, v_mem_wq, v_mem_wkv, v_mem_wo, v_ffn2_w13, v_ffn2_w2]
    deltas, new_ms, new_vs = [], [], []
    for a, (w_, g_, m_, v_) in enumerate(zip(ws, grads, ms, vs)):
        d_, nm_, nv_ = adamw(f"adamw_{a}", w_, g_.reshape(w_.shape), m_, v_)
        deltas.append(d_)
        new_ms.append(nm_)
        new_vs.append(nv_)
    grads = [g_.reshape(w_.shape) for g_, w_ in zip(grads, ws)]
    return (loss, grad_x, *grads, *deltas, *new_ms, *new_vs)
```

```python
import functools
import math

import jax
import jax.numpy as jnp
from jax import lax
from jax.experimental import pallas as pl
from jax.experimental.pallas import tpu as pltpu

f32 = jnp.float32
bf16 = jnp.bfloat16
SDS = jax.ShapeDtypeStruct
MESH = pl.DeviceIdType.MESH

D_MODEL = 1024
DEPTH = 2
N_MEM = 256
MEM_HEADS = 4
MEM_HEAD_DIM = D_MODEL // MEM_HEADS
POOL_WINDOWS = (2, 4, 8, 16)
POOL_WIDTH = 256
POOL_GROUP = 64
QK_NOPE = 128
QK_ROPE = 64
V_HEAD = 128
MLA_HEADS = 6
Q_LORA = 256
KV_LORA = 128
ROPE_BASE = 10000.0
D_FF = 2816
D_IN = POOL_WIDTH + Q_LORA + KV_LORA + QK_ROPE
ALPHA = (2 * DEPTH) ** 0.25
LN_EPS = 1e-5
RMS_EPS = 1e-6
NEG_INF = -1e30
MLA_SCALE = (QK_NOPE + QK_ROPE) ** -0.5
MEM_SCALE = MEM_HEAD_DIM ** -0.5
ADAM_LR = 0.001
ADAM_B1 = 0.9
ADAM_B2 = 0.999
ADAM_EPS = 1e-08
ADAM_WD = 0.01
ADAM_STEP = 10

N_CHIPS = 4
V7X_VMEM_LIMIT = 56 * 2**20
HALO = 16

_NT = (((1,), (1,)), ((), ()))
_TN = (((0,), (0,)), ((), ()))


def _dot(a, b):
    return jnp.dot(a, b, preferred_element_type=f32)


def _dot_nt(a, b):
    return lax.dot_general(a, b, _NT, preferred_element_type=f32)


def _dot_tn(a, b):
    return lax.dot_general(a, b, _TN, preferred_element_type=f32)


def _cp(*sem):
    return pltpu.CompilerParams(dimension_semantics=sem if sem else None, vmem_limit_bytes=V7X_VMEM_LIMIT)


_DEP_SPEC = pl.BlockSpec(memory_space=pl.ANY)


def _with_deps(body, n_in, deps):
    nd = len(deps)
    if not nd:
        return body

    def wrapped(*refs):
        return body(*refs[:n_in], *refs[n_in + nd:])

    return wrapped


def _tile(n, t):
    t = min(n, t)
    assert n % t == 0, (n, t)
    return t


def _row_tile(rows, cols, itemsize=4, target=2 * 2**20):
    best = None
    for t in range(16, rows + 1, 16):
        if rows % t == 0 and t * cols * itemsize <= target:
            best = t
    return best if best is not None else rows


def ffn_up(name, xb, w13, l, deps=()):
    S = xb.shape[0]
    ns = w13.shape[3]
    tm = _tile(S, 512)

    def body(x_ref, wg_ref, wu_ref, g_ref, u_ref, a_ref):
        x = x_ref[...]
        g = _dot(x, wg_ref[0, 0])
        u = _dot(x, wu_ref[0, 0])
        a = g * jax.nn.sigmoid(g) * u
        g_ref[...] = g.astype(bf16)
        u_ref[...] = u.astype(bf16)
        a_ref[...] = a.astype(bf16)

    out = SDS((S, 2 * ns), bf16)
    return pl.pallas_call(
        _with_deps(body, 3, deps), name=name, grid=(2, S // tm),
        in_specs=[pl.BlockSpec((tm, D_MODEL), lambda j, i: (i, 0)),
                  pl.BlockSpec((1, 1, D_MODEL, ns), lambda j, i: (l, j, 0, 0)),
                  pl.BlockSpec((1, 1, D_MODEL, ns), lambda j, i: (l, j + 2, 0, 0))] + [_DEP_SPEC] * len(deps),
        out_specs=[pl.BlockSpec((tm, ns), lambda j, i: (i, j))] * 3,
        out_shape=[out, out, out],
        compiler_params=_cp("parallel", "parallel"),
    )(xb, w13, w13, *deps)


def proj_res_ln(name, parts, ws, wl, x, g, b, rscale, deps=()):
    S = x.shape[0]
    tm = _tile(S, 256)
    n = len(parts)

    def body(*refs):
        p_refs, w_refs = refs[:n], refs[n:2 * n]
        x_ref, g_ref, b_ref, z_ref, y_ref, yb_ref = refs[2 * n:]
        acc = _dot(p_refs[0][...], w_refs[0][0])
        for k in range(1, n):
            acc = acc + _dot(p_refs[k][...], w_refs[k][0])
        if rscale != 1.0:
            acc = rscale * acc
        z = ALPHA * x_ref[...] + acc
        mu = jnp.mean(z, axis=-1, keepdims=True)
        zc = z - mu
        var = jnp.mean(zc * zc, axis=-1, keepdims=True)
        y = zc * lax.rsqrt(var + LN_EPS) * g_ref[...] + b_ref[...]
        z_ref[...] = z
        y_ref[...] = y
        yb_ref[...] = y.astype(bf16)

    row = lambda i: (i, 0)
    in_specs = [pl.BlockSpec((tm, p.shape[1]), row) for p in parts]
    in_specs += [pl.BlockSpec((1,) + w.shape[1:], functools.partial(lambda li, i: (li, 0, 0), li)) for w, li in zip(ws, wl)]
    in_specs += [pl.BlockSpec((tm, D_MODEL), row), pl.BlockSpec((1, D_MODEL), lambda i: (0, 0)),
                 pl.BlockSpec((1, D_MODEL), lambda i: (0, 0))] + [_DEP_SPEC] * len(deps)
    return pl.pallas_call(
        _with_deps(body, 2 * n + 3, deps), name=name, grid=(S // tm,), in_specs=in_specs,
        out_specs=[pl.BlockSpec((tm, D_MODEL), row)] * 3,
        out_shape=[SDS((S, D_MODEL), f32), SDS((S, D_MODEL), f32), SDS((S, D_MODEL), bf16)],
        compiler_params=_cp("parallel"),
    )(*parts, *ws, x, g, b, *deps)


def ln_bwd(name, dy, z, g, rscale, deps=()):
    S = dy.shape[0]
    tm = _tile(S, 512)

    def body(dy_ref, z_ref, g_ref, dzb_ref, dres_ref, dg_ref, db_ref):
        z = z_ref[...]
        mu = jnp.mean(z, axis=-1, keepdims=True)
        zc = z - mu
        rstd = lax.rsqrt(jnp.mean(zc * zc, axis=-1, keepdims=True) + LN_EPS)
        xhat = zc * rstd
        dyv = dy_ref[...]
        dxh = dyv * g_ref[...]
        m1 = jnp.mean(dxh, axis=-1, keepdims=True)
        m2 = jnp.mean(dxh * xhat, axis=-1, keepdims=True)
        dz = rstd * (dxh - m1 - xhat * m2)
        dzb_ref[...] = (rscale * dz).astype(bf16)
        dres_ref[...] = ALPHA * dz

        @pl.when(pl.program_id(0) == 0)
        def _():
            dg_ref[...] = jnp.zeros_like(dg_ref)
            db_ref[...] = jnp.zeros_like(db_ref)

        dg_ref[...] += jnp.sum(dyv * xhat, axis=0, keepdims=True)
        db_ref[...] += jnp.sum(dyv, axis=0, keepdims=True)

    row = lambda i: (i, 0)
    vec = pl.BlockSpec((1, D_MODEL), lambda i: (0, 0))
    return pl.pallas_call(
        _with_deps(body, 3, deps), name=name, grid=(S // tm,),
        in_specs=[pl.BlockSpec((tm, D_MODEL), row), pl.BlockSpec((tm, D_MODEL), row), vec] + [_DEP_SPEC] * len(deps),
        out_specs=[pl.BlockSpec((tm, D_MODEL), row), pl.BlockSpec((tm, D_MODEL), row), vec, vec],
        out_shape=[SDS((S, D_MODEL), bf16), SDS((S, D_MODEL), f32), SDS((1, D_MODEL), f32), SDS((1, D_MODEL), f32)],
        compiler_params=_cp("arbitrary"),
    )(dy, z, g, *deps)


def ffn_bwd_da(name, drb, w2, l, gate, up):
    S = drb.shape[0]
    tm = _tile(S, 256)
    nh = D_FF // 2

    def body(dr_ref, w_ref, g_ref, u_ref, dh_ref):
        dr = dr_ref[...]
        for j in range(2):
            cols = slice(j * nh, (j + 1) * nh)
            da = _dot_nt(dr, w_ref[0, cols, :])
            g = g_ref[:, cols].astype(f32)
            u = u_ref[:, cols].astype(f32)
            sg = jax.nn.sigmoid(g)
            dh_ref[:, cols] = (da * u * (sg * (1.0 + g * (1.0 - sg)))).astype(bf16)
            dh_ref[:, D_FF + j * nh:D_FF + (j + 1) * nh] = (da * (g * sg)).astype(bf16)

    row = lambda i: (i, 0)
    return pl.pallas_call(
        body, name=name, grid=(S // tm,),
        in_specs=[pl.BlockSpec((tm, D_MODEL), row), pl.BlockSpec((1, D_FF, D_MODEL), lambda i: (l, 0, 0)),
                  pl.BlockSpec((tm, D_FF), row), pl.BlockSpec((tm, D_FF), row)],
        out_specs=pl.BlockSpec((tm, 2 * D_FF), row),
        out_shape=SDS((S, 2 * D_FF), bf16),
        compiler_params=_cp("parallel"),
    )(drb, w2, gate, up)


def ffn_dx(name, dh, w13, l, res):
    S = dh.shape[0]
    ns = w13.shape[3]
    tm = _tile(S, 1024)

    def body(dh_ref, w_ref, r_ref, o_ref):
        @pl.when(pl.program_id(1) == 0)
        def _():
            o_ref[...] = r_ref[...]

        o_ref[...] += _dot_nt(dh_ref[...], w_ref[0, 0])

    return pl.pallas_call(
        body, name=name, grid=(S // tm, N_CHIPS),
        in_specs=[pl.BlockSpec((tm, ns), lambda i, j: (i, j)),
                  pl.BlockSpec((1, 1, D_MODEL, ns), lambda i, j: (l, j, 0, 0)),
                  pl.BlockSpec((tm, D_MODEL), lambda i, j: (i, 0))],
        out_specs=pl.BlockSpec((tm, D_MODEL), lambda i, j: (i, 0)),
        out_shape=SDS((S, D_MODEL), f32),
        compiler_params=_cp("parallel", "arbitrary"),
    )(dh, w13, res)


def mm_nt_res(name, dys, ws, wl, res, out_dtype):
    S = dys[0].shape[0]
    K = ws[0].shape[1]
    tm = _tile(S, 512)
    n = len(dys)

    def body(*refs):
        dy_refs, w_refs = refs[:n], refs[n:2 * n]
        o_ref = refs[-1]
        acc = _dot_nt(dy_refs[0][...], w_refs[0][0])
        for k in range(1, n):
            acc = acc + _dot_nt(dy_refs[k][...], w_refs[k][0])
        if res is not None:
            acc = acc + refs[2 * n][...]
        o_ref[...] = acc.astype(out_dtype)

    row = lambda i: (i, 0)
    in_specs = [pl.BlockSpec((tm, d.shape[1]), row) for d in dys]
    in_specs += [pl.BlockSpec((1,) + w.shape[1:], functools.partial(lambda li, i: (li, 0, 0), li)) for w, li in zip(ws, wl)]
    args = list(dys) + list(ws)
    if res is not None:
        in_specs.append(pl.BlockSpec((tm, K), row))
        args.append(res)
    return pl.pallas_call(
        body, name=name, grid=(S // tm,), in_specs=in_specs,
        out_specs=pl.BlockSpec((tm, K), row), out_shape=SDS((S, K), out_dtype),
        compiler_params=_cp("parallel"),
    )(*args)


def mm_tn(name, x, dy, col_shards=False, deps=()):
    S, K = x.shape
    N = dy.shape[1]
    ts = _tile(S, 512)
    if col_shards:
        tn = N // N_CHIPS
    else:
        tn = N
        while K * tn * 4 > 6 * 2**20 and tn % 256 == 0:
            tn //= 2
    nn = N // tn
    lead = ((0,) if col_shards else ()) + (slice(None), slice(None))

    def body(x_ref, dy_ref, o_ref):
        acc = _dot_tn(x_ref[...].astype(bf16), dy_ref[...].astype(bf16))

        @pl.when(pl.program_id(1) == 0)
        def _():
            o_ref[lead] = acc

        @pl.when(pl.program_id(1) != 0)
        def _():
            o_ref[lead] += acc

    if col_shards:
        out_spec = pl.BlockSpec((1, K, tn), lambda n, s: (n, 0, 0))
        out_shape = SDS((N_CHIPS, K, tn), f32)
    else:
        out_spec = pl.BlockSpec((K, tn), lambda n, s: (0, n))
        out_shape = SDS((K, N), f32)
    return pl.pallas_call(
        _with_deps(body, 2, deps), name=name, grid=(nn, S // ts),
        in_specs=[pl.BlockSpec((ts, K), lambda n, s: (s, 0)), pl.BlockSpec((ts, tn), lambda n, s: (s, n))]
        + [_DEP_SPEC] * len(deps),
        out_specs=out_spec, out_shape=out_shape, compiler_params=_cp("parallel", "arbitrary"),
    )(x, dy, *deps)


def mm_nn_shard(name, x, w, l):
    S, K = x.shape
    ns = w.shape[3]

    def body(x_ref, w_ref, o_ref):
        o_ref[...] = _dot(x_ref[...], w_ref[0, 0]).astype(bf16)

    return pl.pallas_call(
        body, name=name, grid=(N_CHIPS,),
        in_specs=[pl.BlockSpec((S, K), lambda j: (0, 0)), pl.BlockSpec((1, 1, K, ns), lambda j: (l, j, 0, 0))],
        out_specs=pl.BlockSpec((S, ns), lambda j: (0, j)), out_shape=SDS((S, N_CHIPS * ns), bf16),
        compiler_params=_cp("parallel"),
    )(x, w)


def loss_grad(name, y, t):
    S = y.shape[0]
    tm = _tile(S, 512)

    def body(y_ref, t_ref, dy_ref, loss_ref):
        e = y_ref[...] - t_ref[...]
        dy_ref[...] = e * (1.0 / D_MODEL)

        @pl.when(pl.program_id(0) == 0)
        def _():
            loss_ref[...] = jnp.zeros_like(loss_ref)

        loss_ref[...] += jnp.full(loss_ref.shape, (0.5 / D_MODEL) * jnp.sum(e * e), f32)

    row = lambda i: (i, 0)
    return pl.pallas_call(
        body, name=name, grid=(S // tm,),
        in_specs=[pl.BlockSpec((tm, D_MODEL), row)] * 2,
        out_specs=[pl.BlockSpec((tm, D_MODEL), row), pl.BlockSpec((8, 128), lambda i: (0, 0))],
        out_shape=[SDS((S, D_MODEL), f32), SDS((8, 128), f32)],
        compiler_params=_cp("arbitrary"),
    )(y, t)


def _half_sum(t):
    return t + pltpu.roll(t, 64, axis=1)


def mix_pre(name, xb, w_in, wq, wkv, l, gq, gkv, cs):
    S = xb.shape[0]
    tm = _tile(S, 256)
    H = MLA_HEADS
    W_EXT = w_in.shape[2]

    def body(x_ref, win_ref, wq_ref, wkv_ref, gq_ref, gkv_ref, cs_ref,
             u_ref, cq_ref, ckv_ref, cqn_ref, ckvn_ref, q_ref, k_ref, v_ref):
        h = _dot(x_ref[...], win_ref[0])
        u_ref[...] = h[:, :256]
        cq = h[:, 256:512]
        ckv = h[:, 512:640]
        cq_ref[...] = cq
        ckv_ref[...] = ckv
        cqn = (cq * lax.rsqrt(jnp.mean(cq * cq, axis=-1, keepdims=True) + RMS_EPS) * gq_ref[...]).astype(bf16)
        ckvn = (ckv * lax.rsqrt(jnp.mean(ckv * ckv, axis=-1, keepdims=True) + RMS_EPS) * gkv_ref[...]).astype(bf16)
        cqn_ref[...] = cqn
        ckvn_ref[...] = ckvn
        csv = cs_ref[...]
        lane = lax.broadcasted_iota(jnp.int32, (tm, 128), 1)
        kr = jnp.where(lane < 64, _half_sum(h[:, 640:768] * csv), 0.0).astype(bf16)
        kv = _dot(ckvn, wkv_ref[0])
        for hd in range(H):
            qe = _dot(cqn, wq_ref[0, hd])
            q_ref[hd, :, :128] = qe[:, :128].astype(bf16)
            q_ref[hd, :, 128:] = _half_sum(qe[:, 128:] * csv).astype(bf16)
            k_ref[hd, :, :128] = kv[:, 256 * hd:256 * hd + 128].astype(bf16)
            k_ref[hd, :, 128:] = kr
            v_ref[hd] = kv[:, 256 * hd + 128:256 * hd + 256].astype(bf16)

    row = lambda i: (i, 0)
    hrow = lambda i: (0, i, 0)
    return pl.pallas_call(
        body, name=name, grid=(S // tm,),
        in_specs=[pl.BlockSpec((tm, D_MODEL), row),
                  pl.BlockSpec((1, D_MODEL, W_EXT), lambda i: (l, 0, 0)),
                  pl.BlockSpec((1, H, Q_LORA, 256), lambda i: (l, 0, 0, 0)),
                  pl.BlockSpec((1, KV_LORA, H * 256), lambda i: (l, 0, 0)),
                  pl.BlockSpec((1, Q_LORA), lambda i: (0, 0)), pl.BlockSpec((1, KV_LORA), lambda i: (0, 0)),
                  pl.BlockSpec((tm, 128), row)],
        out_specs=[pl.BlockSpec((tm, 256), row), pl.BlockSpec((tm, Q_LORA), row), pl.BlockSpec((tm, KV_LORA), row),
                   pl.BlockSpec((tm, Q_LORA), row), pl.BlockSpec((tm, KV_LORA), row),
                   pl.BlockSpec((H, tm, 256), hrow), pl.BlockSpec((H, tm, 256), hrow), pl.BlockSpec((H, tm, 128), hrow)],
        out_shape=[SDS((S, 256), f32), SDS((S, Q_LORA), f32), SDS((S, KV_LORA), f32),
                   SDS((S, Q_LORA), bf16), SDS((S, KV_LORA), bf16),
                   SDS((H, S, 256), bf16), SDS((H, S, 256), bf16), SDS((H, S, 128), bf16)],
        compiler_params=_cp("parallel"),
    )(xb, w_in, wq, wkv, gq, gkv, cs)


def _group_select(col, a2, a4, a8, a16):
    return jnp.where(col < 64, a2, jnp.where(col < 128, a4, jnp.where(col < 192, a8, a16)))


def pool_fwd(name, u, wbd, scale):
    S = u.shape[0]
    tm = _tile(S, 512)
    hb = tm // HALO

    def body(u_ref, halo_ref, w_ref, s_ref, d_ref, y_ref):
        i = pl.program_id(0)
        cur = u_ref[...]
        halo = jnp.where(i > 0, halo_ref[...], 0.0)
        ext = jnp.concatenate([halo, cur], axis=0)
        s2 = ext + pltpu.roll(ext, 1, axis=0)
        s4 = s2 + pltpu.roll(s2, 2, axis=0)
        s8 = s4 + pltpu.roll(s4, 4, axis=0)
        s16 = s8 + pltpu.roll(s8, 8, axis=0)
        t1 = (i * tm + 1 + lax.broadcasted_iota(jnp.int32, (tm, 1), 0)).astype(f32)
        col = lax.broadcasted_iota(jnp.int32, (tm, 256), 1)
        m = _group_select(col, s2[HALO:] / jnp.minimum(t1, 2.0), s4[HALO:] / jnp.minimum(t1, 4.0),
                          s8[HALO:] / jnp.minimum(t1, 8.0), s16[HALO:] / jnp.minimum(t1, 16.0))
        d = (m - cur).astype(bf16)
        d_ref[...] = d
        y_ref[...] = (_dot(d, w_ref[...]) * s_ref[...]).astype(bf16)

    row = lambda i: (i, 0)
    return pl.pallas_call(
        body, name=name, grid=(S // tm,),
        in_specs=[pl.BlockSpec((tm, 256), row), pl.BlockSpec((HALO, 256), lambda i: (jnp.maximum(i * hb - 1, 0), 0)),
                  pl.BlockSpec((256, 256), lambda i: (0, 0)), pl.BlockSpec((1, 256), lambda i: (0, 0))],
        out_specs=[pl.BlockSpec((tm, 256), row)] * 2,
        out_shape=[SDS((S, 256), bf16), SDS((S, 256), bf16)],
        compiler_params=_cp("parallel"),
    )(u, u, wbd, scale)


def pool_bwd(name, dyp, d, wbd, scale):
    S = dyp.shape[0]
    tm = _tile(S, 512)
    hb = tm // HALO
    n_ext = tm + HALO

    def fwd_sum(e, steps):
        k = 1
        for _ in range(steps):
            e = e + pltpu.roll(e, n_ext - k, axis=0)
            k *= 2
        return e

    def body(dy_ref, halo_ref, d_ref, w_ref, s_ref, du_ref, dyw_ref, ds_ref):
        i = pl.program_id(0)
        sc = s_ref[...]
        w = w_ref[...]
        cur = dy_ref[...].astype(f32)
        halo = jnp.where(i < pl.num_programs(0) - 1, halo_ref[...].astype(f32), 0.0)
        dyw = jnp.concatenate([cur, halo], axis=0) * sc
        dyw_ref[...] = dyw[:tm].astype(bf16)
        dd = _dot_nt(dyw.astype(bf16), w)
        t1 = (i * tm + 1 + lax.broadcasted_iota(jnp.int32, (n_ext, 1), 0)).astype(f32)
        f2 = fwd_sum(dd / jnp.minimum(t1, 2.0), 1)
        f4 = fwd_sum(dd / jnp.minimum(t1, 4.0), 2)
        f8 = fwd_sum(dd / jnp.minimum(t1, 8.0), 3)
        f16 = fwd_sum(dd / jnp.minimum(t1, 16.0), 4)
        col = lax.broadcasted_iota(jnp.int32, (tm, 256), 1)
        du_ref[...] = (_group_select(col, f2[:tm], f4[:tm], f8[:tm], f16[:tm]) - dd[:tm]).astype(bf16)

        @pl.when(i == 0)
        def _():
            ds_ref[...] = jnp.zeros_like(ds_ref)

        ds_ref[...] += jnp.sum(cur * _dot(d_ref[...], w), axis=0, keepdims=True)

    row = lambda i: (i, 0)
    nhb = S // HALO
    return pl.pallas_call(
        body, name=name, grid=(S // tm,),
        in_specs=[pl.BlockSpec((tm, 256), row), pl.BlockSpec((HALO, 256), lambda i: (jnp.minimum((i + 1) * hb, nhb - 1), 0)),
                  pl.BlockSpec((tm, 256), row), pl.BlockSpec((256, 256), lambda i: (0, 0)),
                  pl.BlockSpec((1, 256), lambda i: (0, 0))],
        out_specs=[pl.BlockSpec((tm, 256), row), pl.BlockSpec((tm, 256), row), pl.BlockSpec((1, 256), lambda i: (0, 0))],
        out_shape=[SDS((S, 256), bf16), SDS((S, 256), bf16), SDS((1, 256), f32)],
        compiler_params=_cp("arbitrary"),
    )(dyp, dyp, d, wbd, scale)


def _diag_mask(tq):
    rc = lax.broadcasted_iota(jnp.int32, (tq, 1), 0) // 64
    cc = lax.broadcasted_iota(jnp.int32, (1, tq), 1) // 64
    return rc >= cc


MLA_SCALE_LOG2 = MLA_SCALE * math.log2(math.e)


def mla_attn_fwd(name, q, k, v):
    H, S, _ = q.shape
    tq = _tile(S, 512)
    nq = S // tq
    pairs = [(i, j) for i in range(nq) for j in range(i + 1)]
    it = jnp.asarray([p_[0] for p_ in pairs], jnp.int32)
    jt = jnp.asarray([p_[1] for p_ in pairs], jnp.int32)

    def body(it_ref, jt_ref, q_ref, k_ref, v_ref, o_ref, lse_ref, m_sc, l_sc, acc_sc):
        t = pl.program_id(1)
        i, j = it_ref[t], jt_ref[t]

        @pl.when(j == 0)
        def _():
            m_sc[...] = jnp.full_like(m_sc, NEG_INF)
            l_sc[...] = jnp.zeros_like(l_sc)
            acc_sc[...] = jnp.zeros_like(acc_sc)

        def step(masked):
            s = _dot_nt(q_ref[0], k_ref[0])
            if masked:
                s = jnp.where(_diag_mask(tq), s, NEG_INF)
            m_prev = m_sc[...]
            m_new = jnp.maximum(m_prev, jnp.max(s, axis=-1, keepdims=True))
            p = jnp.exp2((s - jnp.tile(m_new, (1, tq // 128))) * MLA_SCALE_LOG2)
            a = jnp.exp2((m_prev - m_new) * MLA_SCALE_LOG2)
            l_sc[...] = a * l_sc[...] + jnp.sum(p, axis=-1, keepdims=True)
            acc_sc[...] = a * acc_sc[...] + _dot(p.astype(bf16), v_ref[0])
            m_sc[...] = m_new

        @pl.when(j < i)
        def _():
            step(False)

        @pl.when(j == i)
        def _():
            step(True)
            o_ref[...] = (acc_sc[...] / l_sc[...]).astype(bf16)
            lse_ref[0] = m_sc[...] * MLA_SCALE_LOG2 + jnp.log2(l_sc[...])

    return pl.pallas_call(
        body, name=name,
        grid_spec=pltpu.PrefetchScalarGridSpec(
            num_scalar_prefetch=2, grid=(H, len(pairs)),
            in_specs=[pl.BlockSpec((1, tq, 256), lambda h, t, it_, jt_: (h, it_[t], 0)),
                      pl.BlockSpec((1, tq, 256), lambda h, t, it_, jt_: (h, jt_[t], 0)),
                      pl.BlockSpec((1, tq, 128), lambda h, t, it_, jt_: (h, jt_[t], 0))],
            out_specs=[pl.BlockSpec((tq, 128), lambda h, t, it_, jt_: (it_[t], h)),
                       pl.BlockSpec((1, tq, 128), lambda h, t, it_, jt_: (h, it_[t], 0))],
            scratch_shapes=[pltpu.VMEM((tq, 128), f32), pltpu.VMEM((tq, 128), f32), pltpu.VMEM((tq, 128), f32)]),
        out_shape=[SDS((S, H * 128), bf16), SDS((H, S, 128), f32)],
        compiler_params=_cp("parallel", "arbitrary"),
    )(it, jt, q, k, v)


def mla_attn_bwd(name, q, k, v, o, do, lse):
    H, S, _ = q.shape
    tq = _tile(S, 512)
    nq = S // tq
    pairs = [(i, j) for j in range(nq) for i in range(j, nq)]
    it = jnp.asarray([p_[0] for p_ in pairs], jnp.int32)
    jt = jnp.asarray([p_[1] for p_ in pairs], jnp.int32)
    n_pairs = len(pairs)

    def body(it_ref, jt_ref, q_ref, k_ref, v_ref, o_ref, do_ref, lse_ref, dq_ref, dk_ref, dv_ref, dk_sc, dv_sc):
        t = pl.program_id(1)
        i, j = it_ref[t], jt_ref[t]

        @pl.when(t == 0)
        def _():
            dq_ref[...] = jnp.zeros_like(dq_ref)

        @pl.when(i == j)
        def _():
            dk_sc[...] = jnp.zeros_like(dk_sc)
            dv_sc[...] = jnp.zeros_like(dv_sc)

        def step(masked):
            qv, kv_, dov = q_ref[0], k_ref[0], do_ref[...]
            s = _dot_nt(qv, kv_)
            if masked:
                s = jnp.where(_diag_mask(tq), s, NEG_INF)
            p = jnp.exp2(s * MLA_SCALE_LOG2 - jnp.tile(lse_ref[0], (1, tq // 128)))
            dv_sc[...] += _dot_tn(p.astype(bf16), dov)
            dp = _dot_nt(dov, v_ref[0])
            delta = jnp.sum(dov.astype(f32) * o_ref[...].astype(f32), axis=-1, keepdims=True)
            ds = (p * (dp - delta)).astype(bf16)
            dk_sc[...] += _dot_tn(ds, qv)
            rows = pl.ds(pl.multiple_of(i * tq, tq), tq)
            dq_ref[0, rows, :] += _dot(ds, kv_)

        @pl.when(i > j)
        def _():
            step(False)

        @pl.when(i == j)
        def _():
            step(True)

        @pl.when(i == nq - 1)
        def _():
            dk_ref[0] = dk_sc[...] * MLA_SCALE
            dv_ref[0] = dv_sc[...]

        @pl.when(t == n_pairs - 1)
        def _():
            dq_ref[...] = dq_ref[...] * MLA_SCALE

    qi = lambda h, t, it_, jt_: (h, it_[t], 0)
    kj = lambda h, t, it_, jt_: (h, jt_[t], 0)
    oi = lambda h, t, it_, jt_: (it_[t], h)
    return pl.pallas_call(
        body, name=name,
        grid_spec=pltpu.PrefetchScalarGridSpec(
            num_scalar_prefetch=2, grid=(H, n_pairs),
            in_specs=[pl.BlockSpec((1, tq, 256), qi), pl.BlockSpec((1, tq, 256), kj), pl.BlockSpec((1, tq, 128), kj),
                      pl.BlockSpec((tq, 128), oi), pl.BlockSpec((tq, 128), oi), pl.BlockSpec((1, tq, 128), qi)],
            out_specs=[pl.BlockSpec((1, S, 256), lambda h, t, it_, jt_: (h, 0, 0)), pl.BlockSpec((1, tq, 256), kj),
                       pl.BlockSpec((1, tq, 128), kj)],
            scratch_shapes=[pltpu.VMEM((tq, 256), f32), pltpu.VMEM((tq, 128), f32)]),
        out_shape=[SDS((H, S, 256), f32), SDS((H, S, 256), f32), SDS((H, S, 128), f32)],
        compiler_params=_cp("parallel", "arbitrary"),
    )(it, jt, q, k, v, o, do, lse)


def mix_post_bwd(name, dq, dk, dv, wq, wkv, l, cq, ckv, gq, gkv, cs):
    H, S, _ = dq.shape
    tm = _tile(S, 256)

    def rms_bwd(dyn, c, g):
        r = lax.rsqrt(jnp.mean(c * c, axis=-1, keepdims=True) + RMS_EPS)
        ch = c * r
        dyg = dyn * g
        dc = r * (dyg - ch * jnp.mean(dyg * ch, axis=-1, keepdims=True))
        return dc, jnp.sum(dyn * ch, axis=0, keepdims=True)

    def body(dq_ref, dk_ref, dv_ref, wq_ref, wkv_ref, cq_ref, ckv_ref, gq_ref, gkv_ref, cs_ref,
             dqe_ref, dkv_ref, dh_ref, dgq_ref, dgkv_ref):
        csv = cs_ref[...]
        lane = lax.broadcasted_iota(jnp.int32, (tm, 128), 1)
        dcqn = jnp.zeros((tm, Q_LORA), f32)
        dkr = jnp.zeros((tm, 128), f32)
        for hd in range(H):
            dqh = dq_ref[hd]
            dqe = jnp.concatenate([dqh[:, :128], _half_sum(dqh[:, 128:]) * csv], axis=1).astype(bf16)
            dqe_ref[:, 256 * hd:256 * hd + 256] = dqe
            dcqn = dcqn + _dot_nt(dqe, wq_ref[0, hd])
            dkh = dk_ref[hd]
            dkv_ref[:, 256 * hd:256 * hd + 128] = dkh[:, :128].astype(bf16)
            dkv_ref[:, 256 * hd + 128:256 * hd + 256] = dv_ref[hd].astype(bf16)
            dkr = dkr + dkh[:, 128:]
        dckvn = _dot_nt(dkv_ref[...], wkv_ref[0])
        dblk = _half_sum(jnp.where(lane < 64, dkr, 0.0)) * csv
        dcq, dgq = rms_bwd(dcqn, cq_ref[...], gq_ref[...])
        dckv, dgkv = rms_bwd(dckvn, ckv_ref[...], gkv_ref[...])
        dh_ref[:, :256] = dcq.astype(bf16)
        dh_ref[:, 256:384] = dckv.astype(bf16)
        dh_ref[:, 384:] = dblk.astype(bf16)

        @pl.when(pl.program_id(0) == 0)
        def _():
            dgq_ref[...] = jnp.zeros_like(dgq_ref)
            dgkv_ref[...] = jnp.zeros_like(dgkv_ref)

        dgq_ref[...] += dgq
        dgkv_ref[...] += dgkv

    row = lambda i: (i, 0)
    hrow = lambda i: (0, i, 0)
    return pl.pallas_call(
        body, name=name, grid=(S // tm,),
        in_specs=[pl.BlockSpec((H, tm, 256), hrow), pl.BlockSpec((H, tm, 256), hrow), pl.BlockSpec((H, tm, 128), hrow),
                  pl.BlockSpec((1, H, Q_LORA, 256), lambda i: (l, 0, 0, 0)),
                  pl.BlockSpec((1, KV_LORA, H * 256), lambda i: (l, 0, 0)),
                  pl.BlockSpec((tm, Q_LORA), row), pl.BlockSpec((tm, KV_LORA), row),
                  pl.BlockSpec((1, Q_LORA), lambda i: (0, 0)), pl.BlockSpec((1, KV_LORA), lambda i: (0, 0)),
                  pl.BlockSpec((tm, 128), row)],
        out_specs=[pl.BlockSpec((tm, H * 256), row), pl.BlockSpec((tm, H * 256), row), pl.BlockSpec((tm, 512), row),
                   pl.BlockSpec((1, Q_LORA), lambda i: (0, 0)), pl.BlockSpec((1, KV_LORA), lambda i: (0, 0))],
        out_shape=[SDS((S, H * 256), bf16), SDS((S, H * 256), bf16), SDS((S, 512), bf16),
                   SDS((1, Q_LORA), f32), SDS((1, KV_LORA), f32)],
        compiler_params=_cp("arbitrary"),
    )(dq, dk, dv, wq, wkv, cq, ckv, gq, gkv, cs)


def _cross_probs(qb, kv_ref, hd):
    cols = slice(hd * MEM_HEAD_DIM, (hd + 1) * MEM_HEAD_DIM)
    s = _dot_nt(qb[:, cols], kv_ref[:, cols]) * MEM_SCALE
    e = jnp.exp(s - jnp.max(s, axis=-1, keepdims=True))
    return e / jnp.sum(e, axis=-1, keepdims=True)


def cross_fwd(name, xb, xf, wq, wo, l, kv, g, b):
    S = xb.shape[0]
    tm = _tile(S, 256)
    M = kv.shape[0]

    def body(x_ref, xf_ref, wq_ref, wo_ref, k_ref, v_ref, g_ref, b_ref, q_ref, o_ref, z_ref, y_ref, yb_ref):
        qb = _dot(x_ref[...], wq_ref[0]).astype(bf16)
        q_ref[...] = qb
        for hd in range(MEM_HEADS):
            cols = slice(hd * MEM_HEAD_DIM, (hd + 1) * MEM_HEAD_DIM)
            p = _cross_probs(qb, k_ref, hd)
            o_ref[:, cols] = _dot(p.astype(bf16), v_ref[:, cols]).astype(bf16)
        z = ALPHA * xf_ref[...] + _dot(o_ref[...], wo_ref[0])
        mu = jnp.mean(z, axis=-1, keepdims=True)
        zc = z - mu
        var = jnp.mean(zc * zc, axis=-1, keepdims=True)
        y = zc * lax.rsqrt(var + LN_EPS) * g_ref[...] + b_ref[...]
        z_ref[...] = z
        y_ref[...] = y
        yb_ref[...] = y.astype(bf16)

    row = lambda i: (i, 0)
    wspec = pl.BlockSpec((1, D_MODEL, D_MODEL), lambda i: (l, 0, 0))
    vec = pl.BlockSpec((1, D_MODEL), lambda i: (0, 0))
    blk = pl.BlockSpec((tm, D_MODEL), row)
    return pl.pallas_call(
        body, name=name, grid=(S // tm,),
        in_specs=[blk, blk, wspec, wspec, pl.BlockSpec((M, D_MODEL), lambda i: (0, 0)),
                  pl.BlockSpec((M, D_MODEL), lambda i: (0, 1)), vec, vec],
        out_specs=[blk] * 5,
        out_shape=[SDS((S, D_MODEL), bf16), SDS((S, D_MODEL), bf16), SDS((S, D_MODEL), f32), SDS((S, D_MODEL), f32),
                   SDS((S, D_MODEL), bf16)],
        compiler_params=_cp("parallel"),
    )(xb, xf, wq, wo, kv, kv, g, b)


def cross_bwd(name, dzb, wo, l, qb, kv):
    S = dzb.shape[0]
    tm = _tile(S, 256)
    M = kv.shape[0]

    def body(dz_ref, wo_ref, q_ref, k_ref, v_ref, dq_ref, dkv_ref):
        @pl.when(pl.program_id(0) == 0)
        def _():
            dkv_ref[...] = jnp.zeros_like(dkv_ref)

        do = _dot_nt(dz_ref[...], wo_ref[0]).astype(bf16)
        qv = q_ref[...]
        for hd in range(MEM_HEADS):
            cols = slice(hd * MEM_HEAD_DIM, (hd + 1) * MEM_HEAD_DIM)
            vcols = slice(D_MODEL + hd * MEM_HEAD_DIM, D_MODEL + (hd + 1) * MEM_HEAD_DIM)
            p = _cross_probs(qv, k_ref, hd)
            doh = do[:, cols]
            dkv_ref[:, vcols] += _dot_tn(p.astype(bf16), doh)
            dp = _dot_nt(doh, v_ref[:, cols])
            ds = (p * (dp - jnp.sum(dp * p, axis=-1, keepdims=True)) * MEM_SCALE).astype(bf16)
            dq_ref[:, cols] = _dot(ds, k_ref[:, cols]).astype(bf16)
            dkv_ref[:, cols] += _dot_tn(ds, qv[:, cols])

    row = lambda i: (i, 0)
    blk = pl.BlockSpec((tm, D_MODEL), row)
    return pl.pallas_call(
        body, name=name, grid=(S // tm,),
        in_specs=[blk, pl.BlockSpec((1, D_MODEL, D_MODEL), lambda i: (l, 0, 0)), blk,
                  pl.BlockSpec((M, D_MODEL), lambda i: (0, 0)), pl.BlockSpec((M, D_MODEL), lambda i: (0, 1))],
        out_specs=[blk, pl.BlockSpec((M, 2 * D_MODEL), lambda i: (0, 0))],
        out_shape=[SDS((S, D_MODEL), bf16), SDS((M, 2 * D_MODEL), f32)],
        compiler_params=_cp("arbitrary"),
    )(dzb, wo, qb, kv, kv)


def adamw(name, w, g, m, v):
    shape = w.shape
    cols = shape[-1]
    rows = math.prod(shape[:-1])
    tr = _row_tile(rows, cols, target=2**20)
    c1 = 1.0 - ADAM_B1 ** ADAM_STEP
    c2 = 1.0 - ADAM_B2 ** ADAM_STEP

    def body(w_ref, g_ref, m_ref, v_ref, d_ref, nm_ref, nv_ref):
        gv = g_ref[...]
        nm = ADAM_B1 * m_ref[...] + (1.0 - ADAM_B1) * gv
        nv = ADAM_B2 * v_ref[...] + (1.0 - ADAM_B2) * (gv * gv)
        d_ref[...] = -ADAM_LR * ((nm / c1) / (jnp.sqrt(nv / c2) + ADAM_EPS) + ADAM_WD * w_ref[...])
        nm_ref[...] = nm
        nv_ref[...] = nv

    blk = pl.BlockSpec((tr, cols), lambda i: (i, 0))
    flat = SDS((rows, cols), f32)
    outs = pl.pallas_call(
        body, name=name, grid=(rows // tr,), in_specs=[blk] * 4, out_specs=[blk] * 3, out_shape=[flat] * 3,
        compiler_params=_cp("parallel"),
    )(*[a.reshape(rows, cols) for a in (w, g, m, v)])
    return [o.reshape(shape) for o in outs]


def _me():
    return lax.axis_index("x"), lax.axis_index("y"), lax.axis_index("c")


def _other_chips(x, y):
    return [(1 - x, y), (x, 1 - y), (1 - x, 1 - y)]


def gather_weights(shards):
    n = len(shards)

    def body(*refs):
        in_refs, out_refs = refs[:n], refs[n:2 * n]
        send_sems, recv_sems, local_sems = refs[2 * n:]
        x, y, c = _me()
        q = 2 * x + y
        sibling = (x, y, 1 - c)
        chips = _other_chips(x, y)
        started = []
        locals_ = []
        for a in range(n):
            for lyr in range(DEPTH):
                cp = pltpu.make_async_copy(in_refs[a].at[lyr], out_refs[a].at[lyr, q], local_sems.at[a * DEPTH + lyr])
                cp.start()
                locals_.append(cp)

        def rcopy(a, k, src, dst, to):
            return pltpu.make_async_remote_copy(src_ref=src, dst_ref=dst, send_sem=send_sems.at[a * 6 + k],
                                                recv_sem=recv_sems.at[a * 6 + k], device_id=to, device_id_type=MESH)

        for a in range(n):
            for k, (cx, cy) in enumerate(chips):
                cp = rcopy(a, k, in_refs[a].at[c], out_refs[a].at[c, q], (cx, cy, c))
                cp.start()
                started.append(cp)
        for a in range(n):
            for k, (cx, cy) in enumerate(chips):
                land = out_refs[a].at[c, 2 * cx + cy]
                rcopy(a, k, land, land, sibling).wait_recv()
                fwd = rcopy(a, 3 + k, land, land, sibling)
                fwd.start()
                started.append(fwd)
        for a in range(n):
            for k, (cx, cy) in enumerate(chips):
                land = out_refs[a].at[1 - c, 2 * cx + cy]
                rcopy(a, 3 + k, land, land, sibling).wait_recv()
        for cp in started:
            cp.wait_send()
        for cp in locals_:
            cp.wait()

    any_spec = pl.BlockSpec(memory_space=pl.ANY)
    return pl.pallas_call(
        body, name="gather_weights",
        in_specs=[any_spec] * n, out_specs=[any_spec] * n,
        out_shape=[SDS((DEPTH, N_CHIPS) + s.shape[1:], s.dtype) for s in shards],
        scratch_shapes=[pltpu.SemaphoreType.DMA((6 * n,)), pltpu.SemaphoreType.DMA((6 * n,)),
                        pltpu.SemaphoreType.DMA((DEPTH * n,))],
    )(*shards)


def pair_send_other_layer(grads):
    n = len(grads)

    def body(*refs):
        in_refs, out_refs = refs[:n], refs[n:2 * n]
        send_sems, recv_sems = refs[2 * n:]
        x, y, c = _me()
        cps = [pltpu.make_async_remote_copy(src_ref=in_refs[a].at[1 - c], dst_ref=out_refs[a], send_sem=send_sems.at[a],
                                            recv_sem=recv_sems.at[a], device_id=(x, y, 1 - c), device_id_type=MESH)
               for a in range(n)]
        for cp in cps:
            cp.start()
        for cp in cps:
            cp.wait()

    any_spec = pl.BlockSpec(memory_space=pl.ANY)
    return pl.pallas_call(
        body, name="pair_send_other_layer", in_specs=[any_spec] * n, out_specs=[any_spec] * n,
        out_shape=[SDS(g.shape[1:], f32) for g in grads],
        scratch_shapes=[pltpu.SemaphoreType.DMA((n,)), pltpu.SemaphoreType.DMA((n,))],
    )(*grads)


def pair_add(name, g, recv, c_arr):
    shard = g.shape[2:]
    cols = shard[-1]
    rows = N_CHIPS * math.prod(shard[:-1])
    tr = _row_tile(rows, cols)

    def body(c_ref, g_ref, r_ref, o_ref):
        o_ref[...] = (g_ref[0] + r_ref[...]).astype(bf16)

    out = pl.pallas_call(
        body, name=name,
        grid_spec=pltpu.PrefetchScalarGridSpec(
            num_scalar_prefetch=1, grid=(rows // tr,),
            in_specs=[pl.BlockSpec((1, tr, cols), lambda i, c_ref: (c_ref[0], i, 0)),
                      pl.BlockSpec((tr, cols), lambda i, c_ref: (i, 0))],
            out_specs=pl.BlockSpec((tr, cols), lambda i, c_ref: (i, 0))),
        out_shape=SDS((rows, cols), bf16), compiler_params=_cp("parallel"),
    )(c_arr, g.reshape(DEPTH, rows, cols), recv.reshape(rows, cols))
    return out.reshape((N_CHIPS,) + shard)


def chip_exchange(psums):
    n = len(psums)

    def body(*refs):
        in_refs, out_refs = refs[:n], refs[n:2 * n]
        send_sems, recv_sems = refs[2 * n:]
        x, y, c = _me()
        chips = _other_chips(x, y)
        cps = [pltpu.make_async_remote_copy(src_ref=in_refs[a].at[2 * cx + cy], dst_ref=out_refs[a].at[k],
                                            send_sem=send_sems.at[3 * a + k], recv_sem=recv_sems.at[3 * a + k],
                                            device_id=(cx, cy, c), device_id_type=MESH)
               for a in range(n) for k, (cx, cy) in enumerate(chips)]
        for cp in cps:
            cp.start()
        for cp in cps:
            cp.wait()

    any_spec = pl.BlockSpec(memory_space=pl.ANY)
    return pl.pallas_call(
        body, name="chip_exchange", in_specs=[any_spec] * n, out_specs=[any_spec] * n,
        out_shape=[SDS((3,) + p.shape[1:], bf16) for p in psums],
        scratch_shapes=[pltpu.SemaphoreType.DMA((3 * n,)), pltpu.SemaphoreType.DMA((3 * n,))],
    )(*psums)


def chip_add(name, psum, recv, qc_arr):
    shard = psum.shape[1:]
    cols = shard[-1]
    rows = math.prod(shard[:-1])
    tr = _row_tile(rows, cols)

    def body(qc_ref, p_ref, r_ref, o_ref):
        o_ref[0] = ((p_ref[0].astype(f32) + r_ref[0].astype(f32)) + r_ref[1].astype(f32)) + r_ref[2].astype(f32)

    out = pl.pallas_call(
        body, name=name,
        grid_spec=pltpu.PrefetchScalarGridSpec(
            num_scalar_prefetch=1, grid=(rows // tr,),
            in_specs=[pl.BlockSpec((1, tr, cols), lambda i, qc_ref: (qc_ref[0], i, 0)),
                      pl.BlockSpec((3, tr, cols), lambda i, qc_ref: (0, i, 0))],
            out_specs=pl.BlockSpec((1, tr, cols), lambda i, qc_ref: (qc_ref[1], i, 0))),
        out_shape=SDS((DEPTH, rows, cols), f32), compiler_params=_cp("parallel"),
    )(qc_arr, psum.reshape(N_CHIPS, rows, cols), recv.reshape(3, rows, cols))
    return out.reshape((DEPTH,) + shard)


def pair_share(sums):
    n = len(sums)

    def body(*refs):
        out_refs = refs[n:2 * n]
        send_sems, recv_sems = refs[2 * n:]
        x, y, c = _me()
        sibling = (x, y, 1 - c)
        cps = [pltpu.make_async_remote_copy(src_ref=out_refs[a].at[c], dst_ref=out_refs[a].at[c], send_sem=send_sems.at[a],
                                            recv_sem=recv_sems.at[a], device_id=sibling, device_id_type=MESH)
               for a in range(n)]
        for cp in cps:
            cp.start()
        for cp in cps:
            cp.wait_send()
        for a in range(n):
            land = out_refs[a].at[1 - c]
            pltpu.make_async_remote_copy(src_ref=land, dst_ref=land, send_sem=send_sems.at[a], recv_sem=recv_sems.at[a],
                                         device_id=sibling, device_id_type=MESH).wait_recv()

    any_spec = pl.BlockSpec(memory_space=pl.ANY)
    return pl.pallas_call(
        body, name="pair_share", in_specs=[any_spec] * n, out_specs=[any_spec] * n,
        out_shape=[SDS(s.shape, f32) for s in sums], input_output_aliases={a: a for a in range(n)},
        scratch_shapes=[pltpu.SemaphoreType.DMA((n,)), pltpu.SemaphoreType.DMA((n,))],
    )(*sums)


def allsum_small(name, v, deps=()):
    R = v.shape[0]

    def body(v_ref, o_ref, all_ref, send_sems, recv_sems, local_sem):
        x, y, c = _me()
        me, sibling = (x, y, c), (x, y, 1 - c)
        chips = _other_chips(x, y)

        def rows(px, py, pc):
            return all_ref.at[4 * px + 2 * py + pc]

        def copy(k, block, to, src=None):
            return pltpu.make_async_remote_copy(
                src_ref=rows(*block) if src is None else src, dst_ref=rows(*block),
                send_sem=send_sems.at[k], recv_sem=recv_sems.at[k], device_id=to, device_id_type=MESH)

        mine = pltpu.make_async_copy(v_ref, rows(*me), local_sem)
        mine.start()
        first = [copy(0, me, sibling, src=v_ref)]
        first += [copy(1 + j, me, (*chip, c), src=v_ref) for j, chip in enumerate(chips)]
        for cp in first:
            cp.start()
        passed = [copy(4 + j, (*chip, c), sibling) for j, chip in enumerate(chips)]
        for j, chip in enumerate(chips):
            copy(1 + j, (*chip, c), me).wait_recv()
            passed[j].start()
        copy(0, sibling, me).wait_recv()
        for j, chip in enumerate(chips):
            copy(4 + j, (*chip, 1 - c), me).wait_recv()
        for cp in first + passed:
            cp.wait_send()
        mine.wait()
        acc = all_ref[0]
        for d in range(1, 8):
            acc = acc + all_ref[d]
        o_ref[...] = acc

    return pl.pallas_call(
        _with_deps(body, 1, deps), name=name,
        in_specs=[pl.BlockSpec(memory_space=pltpu.VMEM)] + [_DEP_SPEC] * len(deps),
        out_specs=pl.BlockSpec(memory_space=pltpu.VMEM),
        out_shape=SDS((R, 128), f32),
        scratch_shapes=[pltpu.VMEM((8, R, 128), f32), pltpu.SemaphoreType.DMA((7,)), pltpu.SemaphoreType.DMA((7,)),
                        pltpu.SemaphoreType.DMA],
        compiler_params=pltpu.CompilerParams(vmem_limit_bytes=V7X_VMEM_LIMIT),
    )(v, *deps)


def reduce_grads(grads):
    x, y, c = _me()
    c_arr = jnp.reshape(c, (1,)).astype(jnp.int32)
    qc_arr = jnp.stack([2 * x + y, c]).astype(jnp.int32)
    recv = pair_send_other_layer(grads)
    psums = [pair_add(f"pair_add_{a}", g, r, c_arr) for a, (g, r) in enumerate(zip(grads, recv))]
    got = chip_exchange(psums)
    sums = [chip_add(f"chip_add_{a}", p, r, qc_arr) for a, (p, r) in enumerate(zip(psums, got))]
    return pair_share(sums)


def _swap_half(r):
    return jnp.concatenate([-r[..., 32:], r[..., :32]], axis=-1)


def _unswap_add(p, qg):
    return p + jnp.concatenate([qg[..., 32:], -qg[..., :32]], axis=-1)


def _block_diag(pw):
    L = pw.shape[0]
    out = jnp.zeros((L, 256, 256), pw.dtype)
    for gi in range(4):
        out = out.at[:, 64 * gi:64 * gi + 64, 64 * gi:64 * gi + 64].set(pw[:, gi])
    return out


def _to_col_shards(w):
    *lead, K, N = w.shape
    nl = len(lead)
    return w.reshape(*lead, K, N_CHIPS, N // N_CHIPS).transpose(*range(nl), nl + 1, nl, nl + 2)


def _from_col_shards(w):
    *lead, C, K, n = w.shape
    nl = len(lead)
    return w.transpose(*range(nl), nl + 1, nl, nl + 2).reshape(*lead, K, C * n)


def _step_serial_comm(x, mem, positions, ln_g, ln_b, ffn1_w13, ffn1_w2, w_in, pool_w, pool_scale, q_norm_g, w_uq, kv_norm_g, w_ukv, w_out, mem_wq, mem_wkv, mem_wo, ffn2_w13, ffn2_w2, loss_target, m_ln_g, m_ln_b, m_ffn1_w13, m_ffn1_w2, m_w_in, m_pool_w, m_pool_scale, m_q_norm_g, m_w_uq, m_kv_norm_g, m_w_ukv, m_w_out, m_mem_wq, m_mem_wkv, m_mem_wo, m_ffn2_w13, m_ffn2_w2, v_ln_g, v_ln_b, v_ffn1_w13, v_ffn1_w2, v_w_in, v_pool_w, v_pool_scale, v_q_norm_g, v_w_uq, v_kv_norm_g, v_w_ukv, v_w_out, v_mem_wq, v_mem_wkv, v_mem_wo, v_ffn2_w13, v_ffn2_w2):
    L = DEPTH
    S = x.shape[1]
    qx, qy, _ = _me()
    chip = 2 * qx + qy

    big = [ffn1_w13, ffn1_w2, w_in, w_uq, w_ukv, w_out, mem_wq, mem_wkv, mem_wo, ffn2_w13, ffn2_w2]
    (g_f1w13, g_f1w2, g_win, g_wuq, g_wukv, g_wout, g_mwq, g_mwkv, g_mwo, g_f2w13, g_f2w2) = gather_weights(
        [w.astype(bf16) for w in big])
    f1w2 = g_f1w2.reshape(L, D_FF, D_MODEL)
    f2w2 = g_f2w2.reshape(L, D_FF, D_MODEL)
    win = g_win.reshape(L, D_MODEL, D_IN)
    win_ext = jnp.concatenate([win, _swap_half(win[..., D_IN - QK_ROPE:])], axis=-1)
    wuq = _from_col_shards(g_wuq).reshape(L, Q_LORA, MLA_HEADS, QK_NOPE + QK_ROPE)
    wq_ext = jnp.concatenate([wuq, _swap_half(wuq[..., QK_NOPE:])], axis=-1).transpose(0, 2, 1, 3)
    wukv = _from_col_shards(g_wukv)
    wout = g_wout.reshape(L, D_MODEL, D_MODEL)
    wout_pool, wout_mla = wout[:, :POOL_WIDTH], wout[:, POOL_WIDTH:]
    mwq = g_mwq.reshape(L, D_MODEL, D_MODEL)
    mwo = g_mwo.reshape(L, D_MODEL, D_MODEL)
    wbd = _block_diag(pool_w.astype(bf16))

    ln_pad = jnp.zeros((2, L, 4, N_CHIPS, D_MODEL // N_CHIPS), f32)
    ln_pad = lax.dynamic_update_slice(ln_pad, jnp.stack([ln_g, ln_b])[:, :, :, None, :], (0, 0, 0, chip, 0))
    ln_full = allsum_small("allsum_ln", ln_pad.reshape(-1, 128)) * 0.5
    ln_full = ln_full.reshape(2, L, 4, D_MODEL)
    lng, lnb = ln_full[0], ln_full[1]

    half = QK_ROPE // 2
    inv_freq = ROPE_BASE ** (-jnp.arange(half, dtype=f32) / half)
    ang = positions[0].astype(f32)[:, None] * inv_freq
    cos, sin = jnp.cos(ang), jnp.sin(ang)
    cs = jnp.concatenate([cos, cos, sin, sin], axis=-1)

    memb = mem[0].astype(bf16)
    xf = x[0]
    xb = xf.astype(bf16)
    vec = lambda a: a.reshape(1, -1)

    saved = []
    for l in range(L):
        sv = {}
        sv["x0b"] = xb
        gate, up, act = ffn_up(f"ffn1_up_{l}", xb, g_f1w13, l)
        z1, x1f, x1b = proj_res_ln(f"ffn1_down_{l}", [act], [f1w2], [l], xf, vec(lng[l, 0]), vec(lnb[l, 0]), 0.5)
        sv.update(gate1=gate, up1=up, act1=act, z1=z1, x1b=x1b)
        u, cq, ckv, cqn, ckvn, q, k, v = mix_pre(f"mix_pre_{l}", x1b, win_ext, wq_ext, wukv, l,
                                                   vec(q_norm_g[l]), vec(kv_norm_g[l]), cs)
        dpool, ypool = pool_fwd(f"pool_fwd_{l}", u, wbd[l], vec(pool_scale[l]))
        o, lse = mla_attn_fwd(f"mla_fwd_{l}", q, k, v)
        z2, x2f, x2b = proj_res_ln(f"mix_out_{l}", [ypool, o], [wout_pool, wout_mla], [l, l], x1f,
                                   vec(lng[l, 1]), vec(lnb[l, 1]), 1.0)
        sv.update(cq=cq, ckv=ckv, cqn=cqn, ckvn=ckvn, q=q, k=k, v=v, dpool=dpool, ypool=ypool, o=o, lse=lse, z2=z2, x2b=x2b)
        kvm = mm_nn_shard(f"mem_kv_{l}", memb, g_mwkv, l)
        cq_, co_, z3, x3f, x3b = cross_fwd(f"cross_fwd_{l}", x2b, x2f, mwq, mwo, l, kvm, vec(lng[l, 2]), vec(lnb[l, 2]))
        sv.update(kvm=kvm, crq=cq_, cro=co_, z3=z3, x3b=x3b)
        gate, up, act = ffn_up(f"ffn2_up_{l}", x3b, g_f2w13, l)
        z4, xf, xb = proj_res_ln(f"ffn2_down_{l}", [act], [f2w2], [l], x3f, vec(lng[l, 3]), vec(lnb[l, 3]), 0.5)
        sv.update(gate2=gate, up2=up, act2=act, z4=z4)
        saved.append(sv)

    dy, loss_blk = loss_grad("loss_grad", xf, loss_target[0])
    loss = lax.psum(loss_blk[0, 0], ("x", "y", "c"))

    G = dict(f1w13=None, f1w2=None, mwq=None, mwkv=None, mwo=None, f2w13=None, f2w2=None)
    small = {k_: [None] * L for k_ in ("win", "wuq", "wukv", "wout", "pool_w", "pool_scale", "gq", "gkv", "lng", "lnb")}
    for l in reversed(range(L)):
        sv = saved[l]
        dlg, dlb = [None] * 4, [None] * 4
        dzb, dres, dlg[3], dlb[3] = ln_bwd(f"ln4_bwd_{l}", dy, sv["z4"], vec(lng[l, 3]), 0.5)
        dh = ffn_bwd_da(f"ffn2_bwd_da_{l}", dzb, f2w2, l, sv["gate2"], sv["up2"])
        G["f2w2"] = mm_tn(f"ffn2_dw2_{l}", sv["act2"], dzb, "nat", l, G["f2w2"])
        G["f2w13"] = mm_tn(f"ffn2_dw13_{l}", sv["x3b"], dh, "shard", l, G["f2w13"])
        dy = ffn_dx(f"ffn2_dx_{l}", dh, g_f2w13, l, dres)
        dzb, dres, dlg[2], dlb[2] = ln_bwd(f"ln3_bwd_{l}", dy, sv["z3"], vec(lng[l, 2]), 1.0)
        dqc, dkvm = cross_bwd(f"cross_bwd_{l}", dzb, mwo, l, sv["crq"], sv["kvm"])
        G["mwo"] = mm_tn(f"cross_dwo_{l}", sv["cro"], dzb, "nat", l, G["mwo"])
        G["mwq"] = mm_tn(f"cross_dwq_{l}", sv["x2b"], dqc, "nat", l, G["mwq"])
        G["mwkv"] = mm_tn(f"cross_dwkv_{l}", memb, dkvm, "shard", l, G["mwkv"])
        dy = mm_nt_res(f"cross_dx_{l}", [dqc], [mwq], [l], dres, f32)
        dzb, dres, dlg[1], dlb[1] = ln_bwd(f"ln2_bwd_{l}", dy, sv["z2"], vec(lng[l, 1]), 1.0)
        dyp = mm_nt_res(f"mix_dpool_{l}", [dzb], [wout_pool], [l], None, bf16)
        do = mm_nt_res(f"mix_do_{l}", [dzb], [wout_mla], [l], None, bf16)
        dwo_p = mm_tn(f"mix_dwout_pool_{l}", sv["ypool"], dzb)
        dwo_m = mm_tn(f"mix_dwout_mla_{l}", sv["o"], dzb)
        small["wout"][l] = jnp.concatenate([dwo_p, dwo_m], axis=0)
        dq, dk, dv = mla_attn_bwd(f"mla_bwd_{l}", sv["q"], sv["k"], sv["v"], sv["o"], do, sv["lse"])
        dqe, dkv, dh_rest, dgq, dgkv = mix_post_bwd(f"mix_post_bwd_{l}", dq, dk, dv, wq_ext, wukv, l, sv["cq"], sv["ckv"],
                                                     vec(q_norm_g[l]), vec(kv_norm_g[l]), cs)
        du, dyw, dscale = pool_bwd(f"pool_bwd_{l}", dyp, sv["dpool"], wbd[l], vec(pool_scale[l]))
        dwq_e = mm_tn(f"mix_dwuq_{l}", sv["cqn"], dqe).reshape(Q_LORA, MLA_HEADS, 256)
        small["wuq"][l] = jnp.concatenate(
            [dwq_e[..., :QK_NOPE], _unswap_add(dwq_e[..., QK_NOPE:QK_NOPE + QK_ROPE], dwq_e[..., QK_NOPE + QK_ROPE:])],
            axis=-1).reshape(Q_LORA, MLA_HEADS * (QK_NOPE + QK_ROPE))
        small["wukv"][l] = mm_tn(f"mix_dwukv_{l}", sv["ckvn"], dkv)
        dwbd = mm_tn(f"pool_dw_{l}", sv["dpool"], dyw)
        small["pool_w"][l] = jnp.stack([dwbd[64 * gi:64 * gi + 64, 64 * gi:64 * gi + 64] for gi in range(4)])
        small["pool_scale"][l], small["gq"][l], small["gkv"][l] = dscale[0], dgq[0], dgkv[0]
        dh_ext = jnp.concatenate([du, dh_rest], axis=1)
        dwin_e = mm_tn(f"mix_dwin_{l}", sv["x1b"], dh_ext)
        small["win"][l] = jnp.concatenate(
            [dwin_e[:, :D_IN - QK_ROPE], _unswap_add(dwin_e[:, D_IN - QK_ROPE:D_IN], dwin_e[:, D_IN:])], axis=-1)
        dy = mm_nt_res(f"mix_dx_{l}", [dh_ext], [win_ext], [l], dres, f32)
        dzb, dres, dlg[0], dlb[0] = ln_bwd(f"ln1_bwd_{l}", dy, sv["z1"], vec(lng[l, 0]), 0.5)
        dh = ffn_bwd_da(f"ffn1_bwd_da_{l}", dzb, f1w2, l, sv["gate1"], sv["up1"])
        G["f1w2"] = mm_tn(f"ffn1_dw2_{l}", sv["act1"], dzb, "nat", l, G["f1w2"])
        G["f1w13"] = mm_tn(f"ffn1_dw13_{l}", sv["x0b"], dh, "shard", l, G["f1w13"])
        dy = ffn_dx(f"ffn1_dx_{l}", dh, g_f1w13, l, dres)
        small["lng"][l] = jnp.concatenate(dlg, axis=0)
        small["lnb"][l] = jnp.concatenate(dlb, axis=0)
    grad_x = dy[None]

    row_shards = lambda a, K: a.reshape(L, N_CHIPS, K // N_CHIPS, a.shape[-1])
    g_list = [G["f1w13"], row_shards(G["f1w2"], D_FF),
              jnp.stack(small["win"]).reshape(L, N_CHIPS, D_MODEL // N_CHIPS, D_IN),
              _to_col_shards(jnp.stack(small["wuq"])), _to_col_shards(jnp.stack(small["wukv"])),
              jnp.stack(small["wout"]).reshape(L, N_CHIPS, D_MODEL // N_CHIPS, D_MODEL),
              row_shards(G["mwq"], D_MODEL), G["mwkv"], row_shards(G["mwo"], D_MODEL),
              G["f2w13"], row_shards(G["f2w2"], D_FF)]
    big_grads = reduce_grads(g_list)

    rep = [jnp.stack(small["pool_w"]).reshape(-1), jnp.stack(small["pool_scale"]).reshape(-1),
           jnp.stack(small["gq"]).reshape(-1), jnp.stack(small["gkv"]).reshape(-1),
           jnp.stack(small["lng"]).reshape(-1), jnp.stack(small["lnb"]).reshape(-1)]
    sizes = [r.shape[0] for r in rep]
    packed = jnp.concatenate(rep)
    pad = (-packed.shape[0]) % 1024
    tot = allsum_small("allsum_small_grads", jnp.pad(packed, (0, pad)).reshape(-1, 128)).reshape(-1)
    offs = [0]
    for s_ in sizes:
        offs.append(offs[-1] + s_)
    parts = [tot[offs[i]:offs[i + 1]] for i in range(len(sizes))]
    g_pool_w = parts[0].reshape(pool_w.shape)
    g_pool_scale = parts[1].reshape(pool_scale.shape)
    g_gq = parts[2].reshape(q_norm_g.shape)
    g_gkv = parts[3].reshape(kv_norm_g.shape)
    shard_cols = lambda a: lax.dynamic_slice_in_dim(a.reshape(L, 4, D_MODEL), chip * (D_MODEL // N_CHIPS),
                                                    D_MODEL // N_CHIPS, axis=2)
    g_lng, g_lnb = shard_cols(parts[4]), shard_cols(parts[5])

    (gf1w13, gf1w2, gwin, gwuq, gwukv, gwout, gmwq, gmwkv, gmwo, gf2w13, gf2w2) = big_grads
    grads = [g_lng, g_lnb, gf1w13, gf1w2, gwin, g_pool_w, g_pool_scale, g_gq, gwuq, g_gkv, gwukv, gwout, gmwq, gmwkv,
             gmwo, gf2w13, gf2w2]
    ws = [ln_g, ln_b, ffn1_w13, ffn1_w2, w_in, pool_w, pool_scale, q_norm_g, w_uq, kv_norm_g, w_ukv, w_out, mem_wq,
          mem_wkv, mem_wo, ffn2_w13, ffn2_w2]
    ms = [m_ln_g, m_ln_b, m_ffn1_w13, m_ffn1_w2, m_w_in, m_pool_w, m_pool_scale, m_q_norm_g, m_w_uq, m_kv_norm_g, m_w_ukv,
          m_w_out, m_mem_wq, m_mem_wkv, m_mem_wo, m_ffn2_w13, m_ffn2_w2]
    vs = [v_ln_g, v_ln_b, v_ffn1_w13, v_ffn1_w2, v_w_in, v_pool_w, v_pool_scale, v_q_norm_g, v_w_uq, v_kv_norm_g, v_w_ukv,
          v_w_out, v_mem_wq, v_mem_wkv, v_mem_wo, v_ffn2_w13, v_ffn2_w2]
    deltas, new_ms, new_vs = [], [], []
    for a, (w_, g_, m_, v_) in enumerate(zip(ws, grads, ms, vs)):
        d_, nm_, nv_ = adamw(f"adamw_{a}", w_, g_.reshape(w_.shape), m_, v_)
        deltas.append(d_)
        new_ms.append(nm_)
        new_vs.append(nv_)
    grads = [g_.reshape(w_.shape) for g_, w_ in zip(grads, ws)]
    return (loss, grad_x, *grads, *deltas, *new_ms, *new_vs)


_HBM_SPEC = pl.BlockSpec(memory_space=pltpu.HBM)
_SEM_SPEC = pl.BlockSpec(memory_space=pltpu.SEMAPHORE)
_ANY_SPEC = pl.BlockSpec(memory_space=pl.ANY)
_DATAFLOW = pltpu.SideEffectType.DATAFLOW_SIDE_EFFECTING


def _split_call(name, body_fn, bufs, sems_in, sems_out_sizes, after):
    nb, ni, no = len(bufs), len(sems_in), len(sems_out_sizes)
    has_after = after is not None

    def body(*refs):
        k = nb + ni + (1 if has_after else 0)
        body_fn(refs[:nb], refs[nb:nb + ni], refs[k:k + no])
        refs[-1][...] = jnp.zeros((8, 128), f32)

    outs = pl.pallas_call(
        body, name=name,
        in_specs=[_HBM_SPEC] * nb + [_SEM_SPEC] * ni + ([_ANY_SPEC] if has_after else []),
        out_specs=[_SEM_SPEC] * no + [_HBM_SPEC] * nb + [pl.BlockSpec(memory_space=pltpu.VMEM)],
        out_shape=[pltpu.SemaphoreType.DMA((s,)) for s in sems_out_sizes]
        + [pltpu.HBM(b.shape, b.dtype) for b in bufs] + [SDS((8, 128), f32)],
        input_output_aliases={i: no + i for i in range(nb)},
        compiler_params=pltpu.CompilerParams(has_side_effects=_DATAFLOW),
    )(*[pltpu.with_memory_space_constraint(b, pltpu.HBM) for b in bufs], *sems_in, *([after] if has_after else []))
    return list(outs[no:no + nb]), list(outs[:no]), outs[-1]


def _rcopy(src, dst, ssem, rsem, to):
    return pltpu.make_async_remote_copy(src_ref=src, dst_ref=dst, send_sem=ssem, recv_sem=rsem, device_id=to,
                                        device_id_type=MESH)


def gather_start(name, groups, after):
    flat = [b for bufs, _ in groups for b in bufs]
    sizes = [3 * len(bufs) for bufs, _ in groups for _ in range(2)]

    def body_fn(b_in, s_in, s_out):
        x, y, c = _me()
        q = 2 * x + y
        chips = _other_chips(x, y)
        pos = 0
        for gi, (bufs, owner) in enumerate(groups):
            refs = b_in[pos:pos + len(bufs)]
            pos += len(bufs)

            @pl.when(c == owner)
            def _(refs=refs, send=s_out[2 * gi], recv=s_out[2 * gi + 1]):
                for a, r in enumerate(refs):
                    for k, (cx, cy) in enumerate(chips):
                        _rcopy(r.at[q], r.at[q], send.at[3 * a + k], recv.at[3 * a + k], (cx, cy, c)).start()

    outs, sems, token = _split_call(name, body_fn, flat, [], sizes, after)
    res, pos = [], 0
    for gi, (bufs, owner) in enumerate(groups):
        res.append((outs[pos:pos + len(bufs)], sems[2 * gi], sems[2 * gi + 1], owner))
        pos += len(bufs)
    return res, token


def gather_forward(name, grp, after):
    bufs, send, recv, owner = grp
    n3 = 3 * len(bufs)

    def body_fn(b_in, s_in, s_out):
        x, y, c = _me()
        q = 2 * x + y
        sibling = (x, y, 1 - c)
        chips = _other_chips(x, y)

        @pl.when(c == owner)
        def _():
            for a, r in enumerate(b_in):
                for k, (cx, cy) in enumerate(chips):
                    i = 3 * a + k
                    land = r.at[2 * cx + cy]
                    _rcopy(r.at[q], r.at[q], s_in[0].at[i], s_in[1].at[i], (cx, cy, c)).wait_send()
                    _rcopy(land, land, s_in[0].at[i], s_in[1].at[i], (cx, cy, c)).wait_recv()
                    _rcopy(land, land, s_out[0].at[i], s_out[1].at[i], sibling).start()

    outs, sems, token = _split_call(name, body_fn, bufs, [send, recv], [n3, n3], after)
    return (outs, sems[0], sems[1], owner), token


def gather_finish(name, grp, after):
    bufs, fsend, frecv, owner = grp

    def body_fn(b_in, s_in, s_out):
        x, y, c = _me()
        sibling = (x, y, 1 - c)
        chips = _other_chips(x, y)

        def each(wait):
            for a, r in enumerate(b_in):
                for k, (cx, cy) in enumerate(chips):
                    land = r.at[2 * cx + cy]
                    wait(_rcopy(land, land, s_in[0].at[3 * a + k], s_in[1].at[3 * a + k], sibling))

        @pl.when(c == owner)
        def _():
            each(lambda cp: cp.wait_send())

        @pl.when(c != owner)
        def _():
            each(lambda cp: cp.wait_recv())

    outs, _, _ = _split_call(name, body_fn, bufs, [fsend, frecv], [], after)
    return outs


def pair_send_start(name, gs, owner, after):
    n = len(gs)
    lands = [lax.empty(g.shape, g.dtype) for g in gs]

    def body_fn(b_in, s_in, s_out):
        x, y, c = _me()

        @pl.when(c == 1 - owner)
        def _():
            for a in range(n):
                _rcopy(b_in[a], b_in[n + a], s_out[0].at[a], s_out[1].at[a], (x, y, owner)).start()

    outs, sems, token = _split_call(name, body_fn, list(gs) + lands, [], [n, n], after)
    return (outs[:n], outs[n:], sems[0], sems[1], owner), token


def pair_send_wait(name, st, after):
    gs, lands, send, recv, owner = st
    n = len(gs)

    def body_fn(b_in, s_in, s_out):
        x, y, c = _me()

        @pl.when(c == 1 - owner)
        def _():
            for a in range(n):
                _rcopy(b_in[a], b_in[n + a], s_in[0].at[a], s_in[1].at[a], (x, y, owner)).wait_send()

        @pl.when(c == owner)
        def _():
            for a in range(n):
                _rcopy(b_in[a], b_in[n + a], s_in[0].at[a], s_in[1].at[a], (x, y, 1 - owner)).wait_recv()

    outs, _, _ = _split_call(name, body_fn, list(gs) + list(lands), [send, recv], [], after)
    return outs[:n], outs[n:]


def chip_exchange_start(name, psums, owner, after):
    n = len(psums)
    lands = [lax.empty((3,) + p.shape[1:], p.dtype) for p in psums]

    def body_fn(b_in, s_in, s_out):
        x, y, c = _me()
        chips = _other_chips(x, y)

        @pl.when(c == owner)
        def _():
            for a in range(n):
                for k, (cx, cy) in enumerate(chips):
                    _rcopy(b_in[a].at[2 * cx + cy], b_in[n + a].at[k], s_out[0].at[3 * a + k], s_out[1].at[3 * a + k],
                           (cx, cy, c)).start()

    outs, sems, token = _split_call(name, body_fn, list(psums) + lands, [], [3 * n, 3 * n], after)
    return (outs[:n], outs[n:], sems[0], sems[1], owner), token


def chip_exchange_wait(name, st, after):
    psums, lands, send, recv, owner = st
    n = len(psums)

    def body_fn(b_in, s_in, s_out):
        x, y, c = _me()
        chips = _other_chips(x, y)

        @pl.when(c == owner)
        def _():
            for a in range(n):
                for k, (cx, cy) in enumerate(chips):
                    cp = _rcopy(b_in[a].at[2 * cx + cy], b_in[n + a].at[k], s_in[0].at[3 * a + k], s_in[1].at[3 * a + k],
                                (cx, cy, c))
                    cp.wait_send()
                    cp.wait_recv()

    outs, _, _ = _split_call(name, body_fn, list(psums) + list(lands), [send, recv], [], after)
    return outs[:n], outs[n:]


def pair_sum(name, g, recv):
    shape = g.shape
    cols = shape[-1]
    rows = math.prod(shape[:-1])
    tr = _row_tile(rows, cols)

    def body(g_ref, r_ref, o_ref):
        o_ref[...] = (g_ref[...] + r_ref[...]).astype(bf16)

    blk = pl.BlockSpec((tr, cols), lambda i: (i, 0))
    out = pl.pallas_call(
        body, name=name, grid=(rows // tr,), in_specs=[blk, blk], out_specs=blk, out_shape=SDS((rows, cols), bf16),
        compiler_params=_cp("parallel"),
    )(g.reshape(rows, cols), recv.reshape(rows, cols))
    return out.reshape(shape)


def chip_sum(name, psum, recv, q_arr, layer, prev):
    shard = psum.shape[1:]
    cols = shard[-1]
    rows = math.prod(shard[:-1])
    tr = _row_tile(rows, cols)

    def body(q_ref, p_ref, r_ref, *rest):
        rest[-1][0] = ((p_ref[0].astype(f32) + r_ref[0].astype(f32)) + r_ref[1].astype(f32)) + r_ref[2].astype(f32)

    in_specs = [pl.BlockSpec((1, tr, cols), lambda i, q_ref: (q_ref[0], i, 0)),
                pl.BlockSpec((3, tr, cols), lambda i, q_ref: (0, i, 0))]
    args = [q_arr, psum.reshape(N_CHIPS, rows, cols), recv.reshape(3, rows, cols)]
    aliases = {}
    if prev is not None:
        in_specs.append(pl.BlockSpec(memory_space=pl.ANY))
        args.append(prev.reshape(DEPTH, rows, cols))
        aliases = {3: 0}
    out = pl.pallas_call(
        body, name=name,
        grid_spec=pltpu.PrefetchScalarGridSpec(
            num_scalar_prefetch=1, grid=(rows // tr,), in_specs=in_specs,
            out_specs=pl.BlockSpec((1, tr, cols), lambda i, q_ref: (layer, i, 0))),
        out_shape=SDS((DEPTH, rows, cols), f32), input_output_aliases=aliases, compiler_params=_cp("parallel"),
    )(*args)
    return out.reshape((DEPTH,) + shard)


def _after(x, *deps):
    return lax.optimization_barrier((x, *deps))[0]


W_NAMES = ("f1w13", "f1w2", "win", "wuq", "wukv", "wout", "mwq", "mwkv", "mwo", "f2w13", "f2w2")
MIX_NAMES = ("win", "wuq", "wukv")
MID_NAMES = ("wout", "mwq", "mwkv", "mwo")
FFN2_NAMES = ("f2w13", "f2w2")


def kernel(x, mem, positions, ln_g, ln_b, ffn1_w13, ffn1_w2, w_in, pool_w, pool_scale, q_norm_g, w_uq, kv_norm_g, w_ukv, w_out, mem_wq, mem_wkv, mem_wo, ffn2_w13, ffn2_w2, loss_target, m_ln_g, m_ln_b, m_ffn1_w13, m_ffn1_w2, m_w_in, m_pool_w, m_pool_scale, m_q_norm_g, m_w_uq, m_kv_norm_g, m_w_ukv, m_w_out, m_mem_wq, m_mem_wkv, m_mem_wo, m_ffn2_w13, m_ffn2_w2, v_ln_g, v_ln_b, v_ffn1_w13, v_ffn1_w2, v_w_in, v_pool_w, v_pool_scale, v_q_norm_g, v_w_uq, v_kv_norm_g, v_w_ukv, v_w_out, v_mem_wq, v_mem_wkv, v_mem_wo, v_ffn2_w13, v_ffn2_w2):
    L = DEPTH
    qx, qy, _ = _me()
    chip = 2 * qx + qy
    vec = lambda a: a.reshape(1, -1)

    shards = dict(zip(W_NAMES, (ffn1_w13, ffn1_w2, w_in, w_uq, w_ukv, w_out, mem_wq, mem_wkv, mem_wo, ffn2_w13, ffn2_w2)))

    def place(sh):
        return lax.dynamic_update_slice(jnp.zeros((N_CHIPS,) + sh.shape, bf16), sh.astype(bf16)[None],
                                        (chip,) + (0,) * sh.ndim)

    bufs = [{n: place(shards[n][l]) for n in W_NAMES} for l in range(L)]
    gw = [dict(), dict()]
    (g0,), tok = gather_start("gather_a_start", [([bufs[0]["f1w13"], bufs[0]["f1w2"]], 0)], None)
    g0, tok = gather_forward("gather_a_forward", g0, None)

    ln_pad = jnp.zeros((2, L, 4, N_CHIPS, D_MODEL // N_CHIPS), f32)
    ln_pad = lax.dynamic_update_slice(ln_pad, jnp.stack([ln_g, ln_b])[:, :, :, None, :], (0, 0, 0, chip, 0))
    ln_sum = allsum_small("allsum_ln", ln_pad.reshape(-1, 128), (tok,))
    ln_full = (ln_sum * 0.5).reshape(2, L, 4, D_MODEL)
    lng, lnb = ln_full[0], ln_full[1]

    gw[0]["f1w13"], gw[0]["f1w2"] = gather_finish("gather_a_finish", g0, ln_sum)
    (g_mix, g_mid, g_ffn2, g_l1), tok_b = gather_start(
        "gather_b_start",
        [([bufs[0][n] for n in MIX_NAMES], 0), ([bufs[0][n] for n in MID_NAMES], 0), ([bufs[0][n] for n in FFN2_NAMES], 0),
         ([bufs[1][n] for n in W_NAMES], 1)], ln_sum)

    half = QK_ROPE // 2
    inv_freq = ROPE_BASE ** (-jnp.arange(half, dtype=f32) / half)
    ang = positions[0].astype(f32)[:, None] * inv_freq
    cos, sin = jnp.cos(ang), jnp.sin(ang)
    cs = jnp.concatenate([cos, cos, sin, sin], axis=-1)

    memb = mem[0].astype(bf16)
    xf = x[0]
    xb = xf.astype(bf16)
    dep = (tok_b,)

    saved, W = [], [None, None]
    for l in range(L):
        sv = {}
        if l == 1:
            gl1 = gather_finish("gather_l1_finish", g_l1, xb)
            gw[1] = dict(zip(W_NAMES, gl1))
        sv["x0b"] = xb
        f1w13 = gw[l]["f1w13"][None]
        gate, up, act = ffn_up(f"ffn1_up_{l}", xb, f1w13, 0, dep)
        dep = ()
        if l == 0:
            g_mix, _ = gather_forward("gather_mix_forward", g_mix, act)
        z1, x1f, x1b = proj_res_ln(f"ffn1_down_{l}", [act], [gw[l]["f1w2"].reshape(1, D_FF, D_MODEL)], [0], xf,
                                   vec(lng[l, 0]), vec(lnb[l, 0]), 0.5)
        sv.update(gate1=gate, up1=up, act1=act, z1=z1, x1b=x1b)
        if l == 0:
            gw[0].update(zip(MIX_NAMES, gather_finish("gather_mix_finish", g_mix, x1b)))
            g_mid, _ = gather_forward("gather_mid_forward", g_mid, x1b)
        win = gw[l]["win"].reshape(D_MODEL, D_IN)
        win_ext = jnp.concatenate([win, _swap_half(win[:, D_IN - QK_ROPE:])], axis=-1)[None]
        wuq = _from_col_shards(gw[l]["wuq"]).reshape(Q_LORA, MLA_HEADS, QK_NOPE + QK_ROPE)
        wq_ext = jnp.concatenate([wuq, _swap_half(wuq[..., QK_NOPE:])], axis=-1).transpose(1, 0, 2)[None]
        wukv = _from_col_shards(gw[l]["wukv"])[None]
        wbd = _block_diag(pool_w[l][None].astype(bf16))[0]
        u, cq, ckv, cqn, ckvn, q, k, v = mix_pre(f"mix_pre_{l}", x1b, win_ext, wq_ext, wukv, 0,
                                                   vec(q_norm_g[l]), vec(kv_norm_g[l]), cs)
        dpool, ypool = pool_fwd(f"pool_fwd_{l}", u, wbd, vec(pool_scale[l]))
        o, lse = mla_attn_fwd(f"mla_fwd_{l}", q, k, v)
        if l == 0:
            gw[0].update(zip(MID_NAMES, gather_finish("gather_mid_finish", g_mid, o)))
            g_ffn2, tok_f = gather_forward("gather_ffn2_forward", g_ffn2, o)
            g_l1, tok_l = gather_forward("gather_l1_forward", g_l1, o)
            dep = (tok_f, tok_l)
        wout = gw[l]["wout"].reshape(D_MODEL, D_MODEL)
        wout_pool, wout_mla = wout[None, :POOL_WIDTH], wout[None, POOL_WIDTH:]
        mwq = gw[l]["mwq"].reshape(1, D_MODEL, D_MODEL)
        mwo = gw[l]["mwo"].reshape(1, D_MODEL, D_MODEL)
        mwkv = gw[l]["mwkv"][None]
        z2, x2f, x2b = proj_res_ln(f"mix_out_{l}", [ypool, o], [wout_pool, wout_mla], [0, 0], x1f,
                                   vec(lng[l, 1]), vec(lnb[l, 1]), 1.0, dep)
        dep = ()
        sv.update(cq=cq, ckv=ckv, cqn=cqn, ckvn=ckvn, q=q, k=k, v=v, dpool=dpool, ypool=ypool, o=o, lse=lse, z2=z2, x2b=x2b)
        kvm = mm_nn_shard(f"mem_kv_{l}", memb, mwkv, 0)
        cq_, co_, z3, x3f, x3b = cross_fwd(f"cross_fwd_{l}", x2b, x2f, mwq, mwo, 0, kvm, vec(lng[l, 2]), vec(lnb[l, 2]))
        sv.update(kvm=kvm, crq=cq_, cro=co_, z3=z3, x3b=x3b)
        if l == 0:
            gw[0].update(zip(FFN2_NAMES, gather_finish("gather_ffn2_finish", g_ffn2, x3b)))
        f2w13 = gw[l]["f2w13"][None]
        f2w2 = gw[l]["f2w2"].reshape(1, D_FF, D_MODEL)
        gate, up, act = ffn_up(f"ffn2_up_{l}", x3b, f2w13, 0)
        z4, xf, xb = proj_res_ln(f"ffn2_down_{l}", [act], [f2w2], [0], x3f, vec(lng[l, 3]), vec(lnb[l, 3]), 0.5)
        sv.update(gate2=gate, up2=up, act2=act, z4=z4)
        W[l] = dict(f1w13=f1w13, f1w2=gw[l]["f1w2"].reshape(1, D_FF, D_MODEL), win_ext=win_ext, wq_ext=wq_ext, wukv=wukv,
                    wbd=wbd, wout_pool=wout_pool, wout_mla=wout_mla, mwq=mwq, mwo=mwo, f2w13=f2w13, f2w2=f2w2)
        saved.append(sv)

    dy, loss_blk = loss_grad("loss_grad", xf, loss_target[0])
    loss = lax.psum(loss_blk[0, 0], ("x", "y", "c"))

    row_shards = lambda a: a.reshape(N_CHIPS, a.shape[0] // N_CHIPS, a.shape[1])
    small = {k_: [None] * L for k_ in ("pool_w", "pool_scale", "gq", "gkv", "lng", "lnb")}
    q_arr = jnp.reshape(chip, (1,)).astype(jnp.int32)
    rest_names = [n for n in W_NAMES if n not in ("f1w13", "f1w2")]

    def red_begin(tag, gs, owner):
        return pair_send_start(f"pair_send_start_{tag}", gs, owner, None)

    def red_mid(tag, st, owner, after):
        gs_, lands_ = pair_send_wait(f"pair_send_wait_{tag}", st, after)
        ps = [pair_sum(f"pair_sum_{tag}_{a}", g_, r_) for a, (g_, r_) in enumerate(zip(gs_, lands_))]
        return chip_exchange_start(f"chip_exchange_start_{tag}", ps, owner, None)

    def red_end(tag, st, layer, prevs, after):
        ps, lands_ = chip_exchange_wait(f"chip_exchange_wait_{tag}", st, after)
        return [chip_sum(f"chip_sum_{tag}_{a}", p_, r_, q_arr, layer, s_)
                for a, (p_, r_, s_) in enumerate(zip(ps, lands_, prevs))]

    st_p1 = st_c1 = st_pa = st_ca = None
    for l in reversed(range(L)):
        sv, w = saved[l], W[l]
        g = {}
        dlg, dlb = [None] * 4, [None] * 4
        dzb, dres, dlg[3], dlb[3] = ln_bwd(f"ln4_bwd_{l}", dy, sv["z4"], vec(lng[l, 3]), 0.5, dep)
        dep = ()
        dh = ffn_bwd_da(f"ffn2_bwd_da_{l}", dzb, w["f2w2"], 0, sv["gate2"], sv["up2"])
        g["f2w2"] = row_shards(mm_tn(f"ffn2_dw2_{l}", sv["act2"], dzb))
        g["f2w13"] = mm_tn(f"ffn2_dw13_{l}", sv["x3b"], dh, True)
        dy = ffn_dx(f"ffn2_dx_{l}", dh, w["f2w13"], 0, dres)
        if l == 0:
            st_c1, tok = red_mid("l1", st_p1, 1, dy)
            dep = (tok, g["f2w2"], g["f2w13"])
        dzb, dres, dlg[2], dlb[2] = ln_bwd(f"ln3_bwd_{l}", dy, sv["z3"], vec(lng[l, 2]), 1.0, dep)
        dep = ()
        dqc, dkvm = cross_bwd(f"cross_bwd_{l}", dzb, w["mwo"], 0, sv["crq"], sv["kvm"])
        g["mwo"] = row_shards(mm_tn(f"cross_dwo_{l}", sv["cro"], dzb))
        g["mwq"] = row_shards(mm_tn(f"cross_dwq_{l}", sv["x2b"], dqc))
        g["mwkv"] = mm_tn(f"cross_dwkv_{l}", memb, dkvm, True)
        dy = mm_nt_res(f"cross_dx_{l}", [dqc], [w["mwq"]], [0], dres, f32)
        dzb, dres, dlg[1], dlb[1] = ln_bwd(f"ln2_bwd_{l}", dy, sv["z2"], vec(lng[l, 1]), 1.0)
        dyp = mm_nt_res(f"mix_dpool_{l}", [dzb], [w["wout_pool"]], [0], None, bf16)
        do = mm_nt_res(f"mix_do_{l}", [dzb], [w["wout_mla"]], [0], None, bf16)
        dwo_p = mm_tn(f"mix_dwout_pool_{l}", sv["ypool"], dzb)
        dwo_m = mm_tn(f"mix_dwout_mla_{l}", sv["o"], dzb)
        g["wout"] = row_shards(jnp.concatenate([dwo_p, dwo_m], axis=0))
        dq, dk, dv = mla_attn_bwd(f"mla_bwd_{l}", sv["q"], sv["k"], sv["v"], sv["o"], do, sv["lse"])
        dqe, dkv, dh_rest, dgq, dgkv = mix_post_bwd(f"mix_post_bwd_{l}", dq, dk, dv, w["wq_ext"], w["wukv"], 0, sv["cq"],
                                                     sv["ckv"], vec(q_norm_g[l]), vec(kv_norm_g[l]), cs)
        du, dyw, dscale = pool_bwd(f"pool_bwd_{l}", dyp, sv["dpool"], w["wbd"], vec(pool_scale[l]))
        dwq_e = mm_tn(f"mix_dwuq_{l}", sv["cqn"], dqe).reshape(Q_LORA, MLA_HEADS, 256)
        g["wuq"] = _to_col_shards(jnp.concatenate(
            [dwq_e[..., :QK_NOPE], _unswap_add(dwq_e[..., QK_NOPE:QK_NOPE + QK_ROPE], dwq_e[..., QK_NOPE + QK_ROPE:])],
            axis=-1).reshape(Q_LORA, MLA_HEADS * (QK_NOPE + QK_ROPE)))
        g["wukv"] = _to_col_shards(mm_tn(f"mix_dwukv_{l}", sv["ckvn"], dkv))
        dwbd = mm_tn(f"pool_dw_{l}", sv["dpool"], dyw)
        small["pool_w"][l] = jnp.stack([dwbd[64 * gi:64 * gi + 64, 64 * gi:64 * gi + 64] for gi in range(4)])
        small["pool_scale"][l], small["gq"][l], small["gkv"][l] = dscale[0], dgq[0], dgkv[0]
        dh_ext = jnp.concatenate([du, dh_rest], axis=1)
        dwin_e = mm_tn(f"mix_dwin_{l}", sv["x1b"], dh_ext)
        g["win"] = row_shards(jnp.concatenate(
            [dwin_e[:, :D_IN - QK_ROPE], _unswap_add(dwin_e[:, D_IN - QK_ROPE:D_IN], dwin_e[:, D_IN:])], axis=-1))
        dy = mm_nt_res(f"mix_dx_{l}", [dh_ext], [w["win_ext"]], [0], dres, f32)
        if l == 0:
            st_pa, tok = red_begin("a0", [g[n] for n in rest_names], 0)
            dep = (tok,)
        dzb, dres, dlg[0], dlb[0] = ln_bwd(f"ln1_bwd_{l}", dy, sv["z1"], vec(lng[l, 0]), 0.5, dep)
        dep = ()
        dh = ffn_bwd_da(f"ffn1_bwd_da_{l}", dzb, w["f1w2"], 0, sv["gate1"], sv["up1"])
        dy = ffn_dx(f"ffn1_dx_{l}", dh, w["f1w13"], 0, dres)
        if l == 0:
            st_ca, tok = red_mid("a0", st_pa, 0, dy)
            dep = (tok,)
        g["f1w2"] = row_shards(mm_tn(f"ffn1_dw2_{l}", sv["act1"], dzb, False, dep))
        g["f1w13"] = mm_tn(f"ffn1_dw13_{l}", sv["x0b"], dh, True, dep)
        dep = ()
        small["lng"][l] = jnp.concatenate(dlg, axis=0)
        small["lnb"][l] = jnp.concatenate(dlb, axis=0)
        if l == 1:
            st_p1, tok = red_begin("l1", [g[n] for n in W_NAMES], 1)
            dep = (tok,)
    grad_x = dy[None]

    st_pb, _ = red_begin("b0", [g["f1w13"], g["f1w2"]], 0)
    sums1 = dict(zip(W_NAMES, red_end("l1", st_c1, 1, [None] * len(W_NAMES), g["f1w13"])))
    st_cb, _ = red_mid("b0", st_pb, 0, sums1["f1w13"])
    sums0 = dict(zip(rest_names, red_end("a0", st_ca, 0, [sums1[n] for n in rest_names], sums1["f1w2"])))
    sums0["f1w13"], sums0["f1w2"] = red_end("b0", st_cb, 0, [sums1["f1w13"], sums1["f1w2"]], sums0["f2w2"])
    big_grads = pair_share([sums0[n] for n in W_NAMES])

    rep = [jnp.stack(small["pool_w"]).reshape(-1), jnp.stack(small["pool_scale"]).reshape(-1),
           jnp.stack(small["gq"]).reshape(-1), jnp.stack(small["gkv"]).reshape(-1),
           jnp.stack(small["lng"]).reshape(-1), jnp.stack(small["lnb"]).reshape(-1)]
    sizes = [r.shape[0] for r in rep]
    packed = jnp.concatenate(rep)
    pad = (-packed.shape[0]) % 1024
    tot = allsum_small("allsum_small_grads", jnp.pad(packed, (0, pad)).reshape(-1, 128)).reshape(-1)
    offs = [0]
    for s_ in sizes:
        offs.append(offs[-1] + s_)
    parts = [tot[offs[i]:offs[i + 1]] for i in range(len(sizes))]
    g_pool_w = parts[0].reshape(pool_w.shape)
    g_pool_scale = parts[1].reshape(pool_scale.shape)
    g_gq = parts[2].reshape(q_norm_g.shape)
    g_gkv = parts[3].reshape(kv_norm_g.shape)
    shard_cols = lambda a: lax.dynamic_slice_in_dim(a.reshape(L, 4, D_MODEL), chip * (D_MODEL // N_CHIPS),
                                                    D_MODEL // N_CHIPS, axis=2)
    g_lng, g_lnb = shard_cols(parts[4]), shard_cols(parts[5])

    (gf1w13, gf1w2, gwin, gwuq, gwukv, gwout, gmwq, gmwkv, gmwo, gf2w13, gf2w2) = big_grads
    grads = [g_lng, g_lnb, gf1w13, gf1w2, gwin, g_pool_w, g_pool_scale, g_gq, gwuq, g_gkv, gwukv, gwout, gmwq, gmwkv,
             gmwo, gf2w13, gf2w2]
    ws = [ln_g, ln_b, ffn1_w13, ffn1_w2, w_in, pool_w, pool_scale, q_norm_g, w_uq, kv_norm_g, w_ukv, w_out, mem_wq,
          mem_wkv, mem_wo, ffn2_w13, ffn2_w2]
    ms = [m_ln_g, m_ln_b, m_ffn1_w13, m_ffn1_w2, m_w_in, m_pool_w, m_pool_scale, m_q_norm_g, m_w_uq, m_kv_norm_g, m_w_ukv,
          m_w_out, m_mem_wq, m_mem_wkv, m_mem_wo, m_ffn2_w13, m_ffn2_w2]
    vs = [v_ln_g, v_ln_b, v_ffn1_w13, v_ffn1_w2, v_w_in, v_pool_w, v_pool_scale, v_q_norm_g, v_w_uq, v_kv_norm_g, v_w_ukv,
          v_w_out, v_mem_wq, v_mem_wkv, v_mem_wo, v_ffn2_w13, v_ffn2_w2]
    deltas, new_ms, new_vs = [], [], []
    for a, (w_, g_, m_, v_) in enumerate(zip(ws, grads, ms, vs)):
        d_, nm_, nv_ = adamw(f"adamw_{a}", w_, g_.reshape(w_.shape), m_, v_)
        deltas.append(d_)
        new_ms.append(nm_)
        new_vs.append(nv_)
    grads = [g_.reshape(w_.shape) for g_, w_ in zip(grads, ws)]
    return (loss, grad_x, *grads, *deltas, *new_ms, *new_vs)
```

```python
import functools
import math

import jax
import jax.numpy as jnp
from jax import lax
from jax.experimental import pallas as pl
from jax.experimental.pallas import tpu as pltpu

f32 = jnp.float32
bf16 = jnp.bfloat16
SDS = jax.ShapeDtypeStruct
MESH = pl.DeviceIdType.MESH

D_MODEL = 1024
DEPTH = 2
N_MEM = 256
MEM_HEADS = 4
MEM_HEAD_DIM = D_MODEL // MEM_HEADS
POOL_WINDOWS = (2, 4, 8, 16)
POOL_WIDTH = 256
POOL_GROUP = 64
QK_NOPE = 128
QK_ROPE = 64
V_HEAD = 128
MLA_HEADS = 6
Q_LORA = 256
KV_LORA = 128
ROPE_BASE = 10000.0
D_FF = 2816
D_IN = POOL_WIDTH + Q_LORA + KV_LORA + QK_ROPE
ALPHA = (2 * DEPTH) ** 0.25
LN_EPS = 1e-5
RMS_EPS = 1e-6
NEG_INF = -1e30
MLA_SCALE = (QK_NOPE + QK_ROPE) ** -0.5
MEM_SCALE = MEM_HEAD_DIM ** -0.5
ADAM_LR = 0.001
ADAM_B1 = 0.9
ADAM_B2 = 0.999
ADAM_EPS = 1e-08
ADAM_WD = 0.01
ADAM_STEP = 10

N_CHIPS = 4
V7X_VMEM_LIMIT = 56 * 2**20
HALO = 16

_NT = (((1,), (1,)), ((), ()))
_TN = (((0,), (0,)), ((), ()))


def _dot(a, b):
    return jnp.dot(a, b, preferred_element_type=f32)


def _dot_nt(a, b):
    return lax.dot_general(a, b, _NT, preferred_element_type=f32)


def _dot_tn(a, b):
    return lax.dot_general(a, b, _TN, preferred_element_type=f32)


def _cp(*sem):
    return pltpu.CompilerParams(dimension_semantics=sem if sem else None, vmem_limit_bytes=V7X_VMEM_LIMIT)


_DEP_SPEC = pl.BlockSpec(memory_space=pl.ANY)


def _with_deps(body, n_in, deps):
    nd = len(deps)
    if not nd:
        return body

    def wrapped(*refs):
        return body(*refs[:n_in], *refs[n_in + nd:])

    return wrapped


def _tile(n, t):
    t = min(n, t)
    assert n % t == 0, (n, t)
    return t


def _row_tile(rows, cols, itemsize=4, target=2 * 2**20):
    best = None
    for t in range(16, rows + 1, 16):
        if rows % t == 0 and t * cols * itemsize <= target:
            best = t
    return best if best is not None else rows


def ffn_up(name, xb, w13, l, deps=()):
    S = xb.shape[0]
    ns = w13.shape[3]
    tm = _tile(S, 512)

    def body(x_ref, wg_ref, wu_ref, g_ref, u_ref, a_ref):
        x = x_ref[...]
        g = _dot(x, wg_ref[0, 0])
        u = _dot(x, wu_ref[0, 0])
        a = g * jax.nn.sigmoid(g) * u
        g_ref[...] = g.astype(bf16)
        u_ref[...] = u.astype(bf16)
        a_ref[...] = a.astype(bf16)

    out = SDS((S, 2 * ns), bf16)
    return pl.pallas_call(
        _with_deps(body, 3, deps), name=name, grid=(2, S // tm),
        in_specs=[pl.BlockSpec((tm, D_MODEL), lambda j, i: (i, 0)),
                  pl.BlockSpec((1, 1, D_MODEL, ns), lambda j, i: (l, j, 0, 0)),
                  pl.BlockSpec((1, 1, D_MODEL, ns), lambda j, i: (l, j + 2, 0, 0))] + [_DEP_SPEC] * len(deps),
        out_specs=[pl.BlockSpec((tm, ns), lambda j, i: (i, j))] * 3,
        out_shape=[out, out, out],
        compiler_params=_cp("parallel", "parallel"),
    )(xb, w13, w13, *deps)


def proj_res_ln(name, parts, ws, wl, x, g, b, rscale, deps=()):
    S = x.shape[0]
    tm = _tile(S, 256)
    n = len(parts)

    def body(*refs):
        p_refs, w_refs = refs[:n], refs[n:2 * n]
        x_ref, g_ref, b_ref, z_ref, y_ref, yb_ref = refs[2 * n:]
        acc = _dot(p_refs[0][...], w_refs[0][0])
        for k in range(1, n):
            acc = acc + _dot(p_refs[k][...], w_refs[k][0])
        if rscale != 1.0:
            acc = rscale * acc
        z = ALPHA * x_ref[...] + acc
        mu = jnp.mean(z, axis=-1, keepdims=True)
        zc = z - mu
        var = jnp.mean(zc * zc, axis=-1, keepdims=True)
        y = zc * lax.rsqrt(var + LN_EPS) * g_ref[...] + b_ref[...]
        z_ref[...] = z
        y_ref[...] = y
        yb_ref[...] = y.astype(bf16)

    row = lambda i: (i, 0)
    in_specs = [pl.BlockSpec((tm, p.shape[1]), row) for p in parts]
    in_specs += [pl.BlockSpec((1,) + w.shape[1:], functools.partial(lambda li, i: (li, 0, 0), li)) for w, li in zip(ws, wl)]
    in_specs += [pl.BlockSpec((tm, D_MODEL), row), pl.BlockSpec((1, D_MODEL), lambda i: (0, 0)),
                 pl.BlockSpec((1, D_MODEL), lambda i: (0, 0))] + [_DEP_SPEC] * len(deps)
    return pl.pallas_call(
        _with_deps(body, 2 * n + 3, deps), name=name, grid=(S // tm,), in_specs=in_specs,
        out_specs=[pl.BlockSpec((tm, D_MODEL), row)] * 3,
        out_shape=[SDS((S, D_MODEL), f32), SDS((S, D_MODEL), f32), SDS((S, D_MODEL), bf16)],
        compiler_params=_cp("parallel"),
    )(*parts, *ws, x, g, b, *deps)


def ln_bwd(name, dy, z, g, rscale, deps=()):
    S = dy.shape[0]
    tm = _tile(S, 512)

    def body(dy_ref, z_ref, g_ref, dzb_ref, dres_ref, dg_ref, db_ref):
        z = z_ref[...]
        mu = jnp.mean(z, axis=-1, keepdims=True)
        zc = z - mu
        rstd = lax.rsqrt(jnp.mean(zc * zc, axis=-1, keepdims=True) + LN_EPS)
        xhat = zc * rstd
        dyv = dy_ref[...]
        dxh = dyv * g_ref[...]
        m1 = jnp.mean(dxh, axis=-1, keepdims=True)
        m2 = jnp.mean(dxh * xhat, axis=-1, keepdims=True)
        dz = rstd * (dxh - m1 - xhat * m2)
        dzb_ref[...] = (rscale * dz).astype(bf16)
        dres_ref[...] = ALPHA * dz

        @pl.when(pl.program_id(0) == 0)
        def _():
            dg_ref[...] = jnp.zeros_like(dg_ref)
            db_ref[...] = jnp.zeros_like(db_ref)

        dg_ref[...] += jnp.sum(dyv * xhat, axis=0, keepdims=True)
        db_ref[...] += jnp.sum(dyv, axis=0, keepdims=True)

    row = lambda i: (i, 0)
    vec = pl.BlockSpec((1, D_MODEL), lambda i: (0, 0))
    return pl.pallas_call(
        _with_deps(body, 3, deps), name=name, grid=(S // tm,),
        in_specs=[pl.BlockSpec((tm, D_MODEL), row), pl.BlockSpec((tm, D_MODEL), row), vec] + [_DEP_SPEC] * len(deps),
        out_specs=[pl.BlockSpec((tm, D_MODEL), row), pl.BlockSpec((tm, D_MODEL), row), vec, vec],
        out_shape=[SDS((S, D_MODEL), bf16), SDS((S, D_MODEL), f32), SDS((1, D_MODEL), f32), SDS((1, D_MODEL), f32)],
        compiler_params=_cp("arbitrary"),
    )(dy, z, g, *deps)


def ffn_bwd_da(name, drb, w2, l, gate, up):
    S = drb.shape[0]
    tm = _tile(S, 256)
    nh = D_FF // 2

    def body(dr_ref, w_ref, g_ref, u_ref, dh_ref):
        dr = dr_ref[...]
        for j in range(2):
            cols = slice(j * nh, (j + 1) * nh)
            da = _dot_nt(dr, w_ref[0, cols, :])
            g = g_ref[:, cols].astype(f32)
            u = u_ref[:, cols].astype(f32)
            sg = jax.nn.sigmoid(g)
            dh_ref[:, cols] = (da * u * (sg * (1.0 + g * (1.0 - sg)))).astype(bf16)
            dh_ref[:, D_FF + j * nh:D_FF + (j + 1) * nh] = (da * (g * sg)).astype(bf16)

    row = lambda i: (i, 0)
    return pl.pallas_call(
        body, name=name, grid=(S // tm,),
        in_specs=[pl.BlockSpec((tm, D_MODEL), row), pl.BlockSpec((1, D_FF, D_MODEL), lambda i: (l, 0, 0)),
                  pl.BlockSpec((tm, D_FF), row), pl.BlockSpec((tm, D_FF), row)],
        out_specs=pl.BlockSpec((tm, 2 * D_FF), row),
        out_shape=SDS((S, 2 * D_FF), bf16),
        compiler_params=_cp("parallel"),
    )(drb, w2, gate, up)


def ffn_dx(name, dh, w13, l, res):
    S = dh.shape[0]
    ns = w13.shape[3]
    tm = _tile(S, 1024)

    def body(dh_ref, w_ref, r_ref, o_ref):
        @pl.when(pl.program_id(1) == 0)
        def _():
            o_ref[...] = r_ref[...]

        o_ref[...] += _dot_nt(dh_ref[...], w_ref[0, 0])

    return pl.pallas_call(
        body, name=name, grid=(S // tm, N_CHIPS),
        in_specs=[pl.BlockSpec((tm, ns), lambda i, j: (i, j)),
                  pl.BlockSpec((1, 1, D_MODEL, ns), lambda i, j: (l, j, 0, 0)),
                  pl.BlockSpec((tm, D_MODEL), lambda i, j: (i, 0))],
        out_specs=pl.BlockSpec((tm, D_MODEL), lambda i, j: (i, 0)),
        out_shape=SDS((S, D_MODEL), f32),
        compiler_params=_cp("parallel", "arbitrary"),
    )(dh, w13, res)


def mm_nt_res(name, dys, ws, wl, res, out_dtype):
    S = dys[0].shape[0]
    K = ws[0].shape[1]
    tm = _tile(S, 512)
    n = len(dys)

    def body(*refs):
        dy_refs, w_refs = refs[:n], refs[n:2 * n]
        o_ref = refs[-1]
        acc = _dot_nt(dy_refs[0][...], w_refs[0][0])
        for k in range(1, n):
            acc = acc + _dot_nt(dy_refs[k][...], w_refs[k][0])
        if res is not None:
            acc = acc + refs[2 * n][...]
        o_ref[...] = acc.astype(out_dtype)

    row = lambda i: (i, 0)
    in_specs = [pl.BlockSpec((tm, d.shape[1]), row) for d in dys]
    in_specs += [pl.BlockSpec((1,) + w.shape[1:], functools.partial(lambda li, i: (li, 0, 0), li)) for w, li in zip(ws, wl)]
    args = list(dys) + list(ws)
    if res is not None:
        in_specs.append(pl.BlockSpec((tm, K), row))
        args.append(res)
    return pl.pallas_call(
        body, name=name, grid=(S // tm,), in_specs=in_specs,
        out_specs=pl.BlockSpec((tm, K), row), out_shape=SDS((S, K), out_dtype),
        compiler_params=_cp("parallel"),
    )(*args)


def mm_tn(name, x, dy, col_shards=False, deps=()):
    S, K = x.shape
    N = dy.shape[1]
    ts = _tile(S, 512)
    if col_shards:
        tn = N // N_CHIPS
    else:
        tn = N
        while K * tn * 4 > 6 * 2**20 and tn % 256 == 0:
            tn //= 2
    nn = N // tn
    lead = ((0,) if col_shards else ()) + (slice(None), slice(None))

    def body(x_ref, dy_ref, o_ref):
        acc = _dot_tn(x_ref[...].astype(bf16), dy_ref[...].astype(bf16))

        @pl.when(pl.program_id(1) == 0)
        def _():
            o_ref[lead] = acc

        @pl.when(pl.program_id(1) != 0)
        def _():
            o_ref[lead] += acc

    if col_shards:
        out_spec = pl.BlockSpec((1, K, tn), lambda n, s: (n, 0, 0))
        out_shape = SDS((N_CHIPS, K, tn), f32)
    else:
        out_spec = pl.BlockSpec((K, tn), lambda n, s: (0, n))
        out_shape = SDS((K, N), f32)
    return pl.pallas_call(
        _with_deps(body, 2, deps), name=name, grid=(nn, S // ts),
        in_specs=[pl.BlockSpec((ts, K), lambda n, s: (s, 0)), pl.BlockSpec((ts, tn), lambda n, s: (s, n))]
        + [_DEP_SPEC] * len(deps),
        out_specs=out_spec, out_shape=out_shape, compiler_params=_cp("parallel", "arbitrary"),
    )(x, dy, *deps)


def mm_nn_shard(name, x, w, l):
    S, K = x.shape
    ns = w.shape[3]

    def body(x_ref, w_ref, o_ref):
        o_ref[...] = _dot(x_ref[...], w_ref[0, 0]).astype(bf16)

    return pl.pallas_call(
        body, name=name, grid=(N_CHIPS,),
        in_specs=[pl.BlockSpec((S, K), lambda j: (0, 0)), pl.BlockSpec((1, 1, K, ns), lambda j: (l, j, 0, 0))],
        out_specs=pl.BlockSpec((S, ns), lambda j: (0, j)), out_shape=SDS((S, N_CHIPS * ns), bf16),
        compiler_params=_cp("parallel"),
    )(x, w)


def loss_grad(name, y, t):
    S = y.shape[0]
    tm = _tile(S, 512)

    def body(y_ref, t_ref, dy_ref, loss_ref):
        e = y_ref[...] - t_ref[...]
        dy_ref[...] = e * (1.0 / D_MODEL)

        @pl.when(pl.program_id(0) == 0)
        def _():
            loss_ref[...] = jnp.zeros_like(loss_ref)

        loss_ref[...] += jnp.full(loss_ref.shape, (0.5 / D_MODEL) * jnp.sum(e * e), f32)

    row = lambda i: (i, 0)
    return pl.pallas_call(
        body, name=name, grid=(S // tm,),
        in_specs=[pl.BlockSpec((tm, D_MODEL), row)] * 2,
        out_specs=[pl.BlockSpec((tm, D_MODEL), row), pl.BlockSpec((8, 128), lambda i: (0, 0))],
        out_shape=[SDS((S, D_MODEL), f32), SDS((8, 128), f32)],
        compiler_params=_cp("arbitrary"),
    )(y, t)


def _half_sum(t):
    return t + pltpu.roll(t, 64, axis=1)


def mix_pre(name, xb, w_in, wq, wkv, l, gq, gkv, cs):
    S = xb.shape[0]
    tm = _tile(S, 256)
    H = MLA_HEADS
    W_EXT = w_in.shape[2]

    def body(x_ref, win_ref, wq_ref, wkv_ref, gq_ref, gkv_ref, cs_ref,
             u_ref, cq_ref, ckv_ref, cqn_ref, ckvn_ref, q_ref, k_ref, v_ref):
        h = _dot(x_ref[...], win_ref[0])
        u_ref[...] = h[:, :256]
        cq = h[:, 256:512]
        ckv = h[:, 512:640]
        cq_ref[...] = cq
        ckv_ref[...] = ckv
        cqn = (cq * lax.rsqrt(jnp.mean(cq * cq, axis=-1, keepdims=True) + RMS_EPS) * gq_ref[...]).astype(bf16)
        ckvn = (ckv * lax.rsqrt(jnp.mean(ckv * ckv, axis=-1, keepdims=True) + RMS_EPS) * gkv_ref[...]).astype(bf16)
        cqn_ref[...] = cqn
        ckvn_ref[...] = ckvn
        csv = cs_ref[...]
        lane = lax.broadcasted_iota(jnp.int32, (tm, 128), 1)
        kr = jnp.where(lane < 64, _half_sum(h[:, 640:768] * csv), 0.0).astype(bf16)
        kv = _dot(ckvn, wkv_ref[0])
        for hd in range(H):
            qe = _dot(cqn, wq_ref[0, hd])
            q_ref[hd, :, :128] = qe[:, :128].astype(bf16)
            q_ref[hd, :, 128:] = _half_sum(qe[:, 128:] * csv).astype(bf16)
            k_ref[hd, :, :128] = kv[:, 256 * hd:256 * hd + 128].astype(bf16)
            k_ref[hd, :, 128:] = kr
            v_ref[hd] = kv[:, 256 * hd + 128:256 * hd + 256].astype(bf16)

    row = lambda i: (i, 0)
    hrow = lambda i: (0, i, 0)
    return pl.pallas_call(
        body, name=name, grid=(S // tm,),
        in_specs=[pl.BlockSpec((tm, D_MODEL), row),
                  pl.BlockSpec((1, D_MODEL, W_EXT), lambda i: (l, 0, 0)),
                  pl.BlockSpec((1, H, Q_LORA, 256), lambda i: (l, 0, 0, 0)),
                  pl.BlockSpec((1, KV_LORA, H * 256), lambda i: (l, 0, 0)),
                  pl.BlockSpec((1, Q_LORA), lambda i: (0, 0)), pl.BlockSpec((1, KV_LORA), lambda i: (0, 0)),
                  pl.BlockSpec((tm, 128), row)],
        out_specs=[pl.BlockSpec((tm, 256), row), pl.BlockSpec((tm, Q_LORA), row), pl.BlockSpec((tm, KV_LORA), row),
                   pl.BlockSpec((tm, Q_LORA), row), pl.BlockSpec((tm, KV_LORA), row),
                   pl.BlockSpec((H, tm, 256), hrow), pl.BlockSpec((H, tm, 256), hrow), pl.BlockSpec((H, tm, 128), hrow)],
        out_shape=[SDS((S, 256), f32), SDS((S, Q_LORA), f32), SDS((S, KV_LORA), f32),
                   SDS((S, Q_LORA), bf16), SDS((S, KV_LORA), bf16),
                   SDS((H, S, 256), bf16), SDS((H, S, 256), bf16), SDS((H, S, 128), bf16)],
        compiler_params=_cp("parallel"),
    )(xb, w_in, wq, wkv, gq, gkv, cs)


def _group_select(col, a2, a4, a8, a16):
    return jnp.where(col < 64, a2, jnp.where(col < 128, a4, jnp.where(col < 192, a8, a16)))


def pool_fwd(name, u, wbd, scale):
    S = u.shape[0]
    tm = _tile(S, 512)
    hb = tm // HALO

    def body(u_ref, halo_ref, w_ref, s_ref, d_ref, y_ref):
        i = pl.program_id(0)
        cur = u_ref[...]
        halo = jnp.where(i > 0, halo_ref[...], 0.0)
        ext = jnp.concatenate([halo, cur], axis=0)
        s2 = ext + pltpu.roll(ext, 1, axis=0)
        s4 = s2 + pltpu.roll(s2, 2, axis=0)
        s8 = s4 + pltpu.roll(s4, 4, axis=0)
        s16 = s8 + pltpu.roll(s8, 8, axis=0)
        t1 = (i * tm + 1 + lax.broadcasted_iota(jnp.int32, (tm, 1), 0)).astype(f32)
        col = lax.broadcasted_iota(jnp.int32, (tm, 256), 1)
        m = _group_select(col, s2[HALO:] / jnp.minimum(t1, 2.0), s4[HALO:] / jnp.minimum(t1, 4.0),
                          s8[HALO:] / jnp.minimum(t1, 8.0), s16[HALO:] / jnp.minimum(t1, 16.0))
        d = (m - cur).astype(bf16)
        d_ref[...] = d
        y_ref[...] = (_dot(d, w_ref[...]) * s_ref[...]).astype(bf16)

    row = lambda i: (i, 0)
    return pl.pallas_call(
        body, name=name, grid=(S // tm,),
        in_specs=[pl.BlockSpec((tm, 256), row), pl.BlockSpec((HALO, 256), lambda i: (jnp.maximum(i * hb - 1, 0), 0)),
                  pl.BlockSpec((256, 256), lambda i: (0, 0)), pl.BlockSpec((1, 256), lambda i: (0, 0))],
        out_specs=[pl.BlockSpec((tm, 256), row)] * 2,
        out_shape=[SDS((S, 256), bf16), SDS((S, 256), bf16)],
        compiler_params=_cp("parallel"),
    )(u, u, wbd, scale)


def pool_bwd(name, dyp, d, wbd, scale):
    S = dyp.shape[0]
    tm = _tile(S, 512)
    hb = tm // HALO
    n_ext = tm + HALO

    def fwd_sum(e, steps):
        k = 1
        for _ in range(steps):
            e = e + pltpu.roll(e, n_ext - k, axis=0)
            k *= 2
        return e

    def body(dy_ref, halo_ref, d_ref, w_ref, s_ref, du_ref, dyw_ref, ds_ref):
        i = pl.program_id(0)
        sc = s_ref[...]
        w = w_ref[...]
        cur = dy_ref[...].astype(f32)
        halo = jnp.where(i < pl.num_programs(0) - 1, halo_ref[...].astype(f32), 0.0)
        dyw = jnp.concatenate([cur, halo], axis=0) * sc
        dyw_ref[...] = dyw[:tm].astype(bf16)
        dd = _dot_nt(dyw.astype(bf16), w)
        t1 = (i * tm + 1 + lax.broadcasted_iota(jnp.int32, (n_ext, 1), 0)).astype(f32)
        f2 = fwd_sum(dd / jnp.minimum(t1, 2.0), 1)
        f4 = fwd_sum(dd / jnp.minimum(t1, 4.0), 2)
        f8 = fwd_sum(dd / jnp.minimum(t1, 8.0), 3)
        f16 = fwd_sum(dd / jnp.minimum(t1, 16.0), 4)
        col = lax.broadcasted_iota(jnp.int32, (tm, 256), 1)
        du_ref[...] = (_group_select(col, f2[:tm], f4[:tm], f8[:tm], f16[:tm]) - dd[:tm]).astype(bf16)

        @pl.when(i == 0)
        def _():
            ds_ref[...] = jnp.zeros_like(ds_ref)

        ds_ref[...] += jnp.sum(cur * _dot(d_ref[...], w), axis=0, keepdims=True)

    row = lambda i: (i, 0)
    nhb = S // HALO
    return pl.pallas_call(
        body, name=name, grid=(S // tm,),
        in_specs=[pl.BlockSpec((tm, 256), row), pl.BlockSpec((HALO, 256), lambda i: (jnp.minimum((i + 1) * hb, nhb - 1), 0)),
                  pl.BlockSpec((tm, 256), row), pl.BlockSpec((256, 256), lambda i: (0, 0)),
                  pl.BlockSpec((1, 256), lambda i: (0, 0))],
        out_specs=[pl.BlockSpec((tm, 256), row), pl.BlockSpec((tm, 256), row), pl.BlockSpec((1, 256), lambda i: (0, 0))],
        out_shape=[SDS((S, 256), bf16), SDS((S, 256), bf16), SDS((1, 256), f32)],
        compiler_params=_cp("arbitrary"),
    )(dyp, dyp, d, wbd, scale)


def _diag_mask(tq):
    rc = lax.broadcasted_iota(jnp.int32, (tq, 1), 0) // 64
    cc = lax.broadcasted_iota(jnp.int32, (1, tq), 1) // 64
    return rc >= cc


MLA_SCALE_LOG2 = MLA_SCALE * math.log2(math.e)


def mla_attn_fwd(name, q, k, v):
    H, S, _ = q.shape
    tq = _tile(S, 1024)
    nq = S // tq
    pairs = [(i, j) for i in range(nq) for j in range(i + 1)]
    it = jnp.asarray([p_[0] for p_ in pairs], jnp.int32)
    jt = jnp.asarray([p_[1] for p_ in pairs], jnp.int32)

    def body(it_ref, jt_ref, q_ref, k_ref, v_ref, o_ref, lse_ref, m_sc, l_sc, acc_sc):
        t = pl.program_id(1)
        i, j = it_ref[t], jt_ref[t]

        @pl.when(j == 0)
        def _():
            m_sc[...] = jnp.full_like(m_sc, NEG_INF)
            l_sc[...] = jnp.zeros_like(l_sc)
            acc_sc[...] = jnp.zeros_like(acc_sc)

        def step(masked):
            s = _dot_nt(q_ref[0], k_ref[0])
            if masked:
                s = jnp.where(_diag_mask(tq), s, NEG_INF)
            m_prev = m_sc[...]
            m_new = jnp.maximum(m_prev, jnp.max(s, axis=-1, keepdims=True))
            p = jnp.exp2((s - jnp.tile(m_new, (1, tq // 128))) * MLA_SCALE_LOG2)
            a = jnp.exp2((m_prev - m_new) * MLA_SCALE_LOG2)
            l_sc[...] = a * l_sc[...] + jnp.sum(p, axis=-1, keepdims=True)
            acc_sc[...] = a * acc_sc[...] + _dot(p.astype(bf16), v_ref[0])
            m_sc[...] = m_new

        @pl.when(j < i)
        def _():
            step(False)

        @pl.when(j == i)
        def _():
            step(True)
            o_ref[...] = (acc_sc[...] / l_sc[...]).astype(bf16)
            lse_ref[0] = m_sc[...] * MLA_SCALE_LOG2 + jnp.log2(l_sc[...])

    return pl.pallas_call(
        body, name=name,
        grid_spec=pltpu.PrefetchScalarGridSpec(
            num_scalar_prefetch=2, grid=(H, len(pairs)),
            in_specs=[pl.BlockSpec((1, tq, 256), lambda h, t, it_, jt_: (h, it_[t], 0)),
                      pl.BlockSpec((1, tq, 256), lambda h, t, it_, jt_: (h, jt_[t], 0)),
                      pl.BlockSpec((1, tq, 128), lambda h, t, it_, jt_: (h, jt_[t], 0))],
            out_specs=[pl.BlockSpec((tq, 128), lambda h, t, it_, jt_: (it_[t], h)),
                       pl.BlockSpec((1, tq, 128), lambda h, t, it_, jt_: (h, it_[t], 0))],
            scratch_shapes=[pltpu.VMEM((tq, 128), f32), pltpu.VMEM((tq, 128), f32), pltpu.VMEM((tq, 128), f32)]),
        out_shape=[SDS((S, H * 128), bf16), SDS((H, S, 128), f32)],
        compiler_params=_cp("parallel", "arbitrary"),
    )(it, jt, q, k, v)


def mla_attn_bwd(name, q, k, v, o, do, lse):
    H, S, _ = q.shape
    tq = _tile(S, 1024)
    nq = S // tq
    pairs = [(i, j) for j in range(nq) for i in range(j, nq)]
    it = jnp.asarray([p_[0] for p_ in pairs], jnp.int32)
    jt = jnp.asarray([p_[1] for p_ in pairs], jnp.int32)
    n_pairs = len(pairs)

    def body(it_ref, jt_ref, q_ref, k_ref, v_ref, o_ref, do_ref, lse_ref, dq_ref, dk_ref, dv_ref, dk_sc, dv_sc):
        t = pl.program_id(1)
        i, j = it_ref[t], jt_ref[t]

        @pl.when(t == 0)
        def _():
            dq_ref[...] = jnp.zeros_like(dq_ref)

        @pl.when(i == j)
        def _():
            dk_sc[...] = jnp.zeros_like(dk_sc)
            dv_sc[...] = jnp.zeros_like(dv_sc)

        def step(masked):
            qv, kv_, dov = q_ref[0], k_ref[0], do_ref[...]
            s = _dot_nt(qv, kv_)
            if masked:
                s = jnp.where(_diag_mask(tq), s, NEG_INF)
            p = jnp.exp2(s * MLA_SCALE_LOG2 - jnp.tile(lse_ref[0], (1, tq // 128)))
            dv_sc[...] += _dot_tn(p.astype(bf16), dov)
            dp = _dot_nt(dov, v_ref[0])
            delta = jnp.sum(dov.astype(f32) * o_ref[...].astype(f32), axis=-1, keepdims=True)
            ds = (p * (dp - delta)).astype(bf16)
            dk_sc[...] += _dot_tn(ds, qv)
            rows = pl.ds(pl.multiple_of(i * tq, tq), tq)
            dq_ref[0, rows, :] += _dot(ds, kv_)

        @pl.when(i > j)
        def _():
            step(False)

        @pl.when(i == j)
        def _():
            step(True)

        @pl.when(i == nq - 1)
        def _():
            dk_ref[0] = dk_sc[...] * MLA_SCALE
            dv_ref[0] = dv_sc[...]

        @pl.when(t == n_pairs - 1)
        def _():
            dq_ref[...] = dq_ref[...] * MLA_SCALE

    qi = lambda h, t, it_, jt_: (h, it_[t], 0)
    kj = lambda h, t, it_, jt_: (h, jt_[t], 0)
    oi = lambda h, t, it_, jt_: (it_[t], h)
    return pl.pallas_call(
        body, name=name,
        grid_spec=pltpu.PrefetchScalarGridSpec(
            num_scalar_prefetch=2, grid=(H, n_pairs),
            in_specs=[pl.BlockSpec((1, tq, 256), qi), pl.BlockSpec((1, tq, 256), kj), pl.BlockSpec((1, tq, 128), kj),
                      pl.BlockSpec((tq, 128), oi), pl.BlockSpec((tq, 128), oi), pl.BlockSpec((1, tq, 128), qi)],
            out_specs=[pl.BlockSpec((1, S, 256), lambda h, t, it_, jt_: (h, 0, 0)), pl.BlockSpec((1, tq, 256), kj),
                       pl.BlockSpec((1, tq, 128), kj)],
            scratch_shapes=[pltpu.VMEM((tq, 256), f32), pltpu.VMEM((tq, 128), f32)]),
        out_shape=[SDS((H, S, 256), f32), SDS((H, S, 256), f32), SDS((H, S, 128), f32)],
        compiler_params=_cp("parallel", "arbitrary"),
    )(it, jt, q, k, v, o, do, lse)


def mix_post_bwd(name, dq, dk, dv, wq, wkv, l, cq, ckv, gq, gkv, cs):
    H, S, _ = dq.shape
    tm = _tile(S, 256)

    def rms_bwd(dyn, c, g):
        r = lax.rsqrt(jnp.mean(c * c, axis=-1, keepdims=True) + RMS_EPS)
        ch = c * r
        dyg = dyn * g
        dc = r * (dyg - ch * jnp.mean(dyg * ch, axis=-1, keepdims=True))
        return dc, jnp.sum(dyn * ch, axis=0, keepdims=True)

    def body(dq_ref, dk_ref, dv_ref, wq_ref, wkv_ref, cq_ref, ckv_ref, gq_ref, gkv_ref, cs_ref,
             dqe_ref, dkv_ref, dh_ref, dgq_ref, dgkv_ref):
        csv = cs_ref[...]
        lane = lax.broadcasted_iota(jnp.int32, (tm, 128), 1)
        dcqn = jnp.zeros((tm, Q_LORA), f32)
        dkr = jnp.zeros((tm, 128), f32)
        for hd in range(H):
            dqh = dq_ref[hd]
            dqe = jnp.concatenate([dqh[:, :128], _half_sum(dqh[:, 128:]) * csv], axis=1).astype(bf16)
            dqe_ref[:, 256 * hd:256 * hd + 256] = dqe
            dcqn = dcqn + _dot_nt(dqe, wq_ref[0, hd])
            dkh = dk_ref[hd]
            dkv_ref[:, 256 * hd:256 * hd + 128] = dkh[:, :128].astype(bf16)
            dkv_ref[:, 256 * hd + 128:256 * hd + 256] = dv_ref[hd].astype(bf16)
            dkr = dkr + dkh[:, 128:]
        dckvn = _dot_nt(dkv_ref[...], wkv_ref[0])
        dblk = _half_sum(jnp.where(lane < 64, dkr, 0.0)) * csv
        dcq, dgq = rms_bwd(dcqn, cq_ref[...], gq_ref[...])
        dckv, dgkv = rms_bwd(dckvn, ckv_ref[...], gkv_ref[...])
        dh_ref[:, :256] = dcq.astype(bf16)
        dh_ref[:, 256:384] = dckv.astype(bf16)
        dh_ref[:, 384:] = dblk.astype(bf16)

        @pl.when(pl.program_id(0) == 0)
        def _():
            dgq_ref[...] = jnp.zeros_like(dgq_ref)
            dgkv_ref[...] = jnp.zeros_like(dgkv_ref)

        dgq_ref[...] += dgq
        dgkv_ref[...] += dgkv

    row = lambda i: (i, 0)
    hrow = lambda i: (0, i, 0)
    return pl.pallas_call(
        body, name=name, grid=(S // tm,),
        in_specs=[pl.BlockSpec((H, tm, 256), hrow), pl.BlockSpec((H, tm, 256), hrow), pl.BlockSpec((H, tm, 128), hrow),
                  pl.BlockSpec((1, H, Q_LORA, 256), lambda i: (l, 0, 0, 0)),
                  pl.BlockSpec((1, KV_LORA, H * 256), lambda i: (l, 0, 0)),
                  pl.BlockSpec((tm, Q_LORA), row), pl.BlockSpec((tm, KV_LORA), row),
                  pl.BlockSpec((1, Q_LORA), lambda i: (0, 0)), pl.BlockSpec((1, KV_LORA), lambda i: (0, 0)),
                  pl.BlockSpec((tm, 128), row)],
        out_specs=[pl.BlockSpec((tm, H * 256), row), pl.BlockSpec((tm, H * 256), row), pl.BlockSpec((tm, 512), row),
                   pl.BlockSpec((1, Q_LORA), lambda i: (0, 0)), pl.BlockSpec((1, KV_LORA), lambda i: (0, 0))],
        out_shape=[SDS((S, H * 256), bf16), SDS((S, H * 256), bf16), SDS((S, 512), bf16),
                   SDS((1, Q_LORA), f32), SDS((1, KV_LORA), f32)],
        compiler_params=_cp("arbitrary"),
    )(dq, dk, dv, wq, wkv, cq, ckv, gq, gkv, cs)


def _cross_probs(qb, kv_ref, hd):
    cols = slice(hd * MEM_HEAD_DIM, (hd + 1) * MEM_HEAD_DIM)
    s = _dot_nt(qb[:, cols], kv_ref[:, cols]) * MEM_SCALE
    e = jnp.exp(s - jnp.max(s, axis=-1, keepdims=True))
    return e / jnp.sum(e, axis=-1, keepdims=True)


def cross_fwd(name, xb, xf, wq, wo, l, kv, g, b):
    S = xb.shape[0]
    tm = _tile(S, 256)
    M = kv.shape[0]

    def body(x_ref, xf_ref, wq_ref, wo_ref, k_ref, v_ref, g_ref, b_ref, q_ref, o_ref, z_ref, y_ref, yb_ref):
        qb = _dot(x_ref[...], wq_ref[0]).astype(bf16)
        q_ref[...] = qb
        for hd in range(MEM_HEADS):
            cols = slice(hd * MEM_HEAD_DIM, (hd + 1) * MEM_HEAD_DIM)
            p = _cross_probs(qb, k_ref, hd)
            o_ref[:, cols] = _dot(p.astype(bf16), v_ref[:, cols]).astype(bf16)
        z = ALPHA * xf_ref[...] + _dot(o_ref[...], wo_ref[0])
        mu = jnp.mean(z, axis=-1, keepdims=True)
        zc = z - mu
        var = jnp.mean(zc * zc, axis=-1, keepdims=True)
        y = zc * lax.rsqrt(var + LN_EPS) * g_ref[...] + b_ref[...]
        z_ref[...] = z
        y_ref[...] = y
        yb_ref[...] = y.astype(bf16)

    row = lambda i: (i, 0)
    wspec = pl.BlockSpec((1, D_MODEL, D_MODEL), lambda i: (l, 0, 0))
    vec = pl.BlockSpec((1, D_MODEL), lambda i: (0, 0))
    blk = pl.BlockSpec((tm, D_MODEL), row)
    return pl.pallas_call(
        body, name=name, grid=(S // tm,),
        in_specs=[blk, blk, wspec, wspec, pl.BlockSpec((M, D_MODEL), lambda i: (0, 0)),
                  pl.BlockSpec((M, D_MODEL), lambda i: (0, 1)), vec, vec],
        out_specs=[blk] * 5,
        out_shape=[SDS((S, D_MODEL), bf16), SDS((S, D_MODEL), bf16), SDS((S, D_MODEL), f32), SDS((S, D_MODEL), f32),
                   SDS((S, D_MODEL), bf16)],
        compiler_params=_cp("parallel"),
    )(xb, xf, wq, wo, kv, kv, g, b)


def cross_bwd(name, dzb, wo, l, qb, kv):
    S = dzb.shape[0]
    tm = _tile(S, 256)
    M = kv.shape[0]

    def body(dz_ref, wo_ref, q_ref, k_ref, v_ref, dq_ref, dkv_ref):
        @pl.when(pl.program_id(0) == 0)
        def _():
            dkv_ref[...] = jnp.zeros_like(dkv_ref)

        do = _dot_nt(dz_ref[...], wo_ref[0]).astype(bf16)
        qv = q_ref[...]
        for hd in range(MEM_HEADS):
            cols = slice(hd * MEM_HEAD_DIM, (hd + 1) * MEM_HEAD_DIM)
            vcols = slice(D_MODEL + hd * MEM_HEAD_DIM, D_MODEL + (hd + 1) * MEM_HEAD_DIM)
            p = _cross_probs(qv, k_ref, hd)
            doh = do[:, cols]
            dkv_ref[:, vcols] += _dot_tn(p.astype(bf16), doh)
            dp = _dot_nt(doh, v_ref[:, cols])
            ds = (p * (dp - jnp.sum(dp * p, axis=-1, keepdims=True)) * MEM_SCALE).astype(bf16)
            dq_ref[:, cols] = _dot(ds, k_ref[:, cols]).astype(bf16)
            dkv_ref[:, cols] += _dot_tn(ds, qv[:, cols])

    row = lambda i: (i, 0)
    blk = pl.BlockSpec((tm, D_MODEL), row)
    return pl.pallas_call(
        body, name=name, grid=(S // tm,),
        in_specs=[blk, pl.BlockSpec((1, D_MODEL, D_MODEL), lambda i: (l, 0, 0)), blk,
                  pl.BlockSpec((M, D_MODEL), lambda i: (0, 0)), pl.BlockSpec((M, D_MODEL), lambda i: (0, 1))],
        out_specs=[blk, pl.BlockSpec((M, 2 * D_MODEL), lambda i: (0, 0))],
        out_shape=[SDS((S, D_MODEL), bf16), SDS((M, 2 * D_MODEL), f32)],
        compiler_params=_cp("arbitrary"),
    )(dzb, wo, qb, kv, kv)


def adamw(name, w, g, m, v):
    shape = w.shape
    cols = shape[-1]
    rows = math.prod(shape[:-1])
    tr = _row_tile(rows, cols, target=2**20)
    c1 = 1.0 - ADAM_B1 ** ADAM_STEP
    c2 = 1.0 - ADAM_B2 ** ADAM_STEP

    def body(w_ref, g_ref, m_ref, v_ref, d_ref, nm_ref, nv_ref):
        gv = g_ref[...]
        nm = ADAM_B1 * m_ref[...] + (1.0 - ADAM_B1) * gv
        nv = ADAM_B2 * v_ref[...] + (1.0 - ADAM_B2) * (gv * gv)
        d_ref[...] = -ADAM_LR * ((nm / c1) / (jnp.sqrt(nv / c2) + ADAM_EPS) + ADAM_WD * w_ref[...])
        nm_ref[...] = nm
        nv_ref[...] = nv

    blk = pl.BlockSpec((tr, cols), lambda i: (i, 0))
    flat = SDS((rows, cols), f32)
    outs = pl.pallas_call(
        body, name=name, grid=(rows // tr,), in_specs=[blk] * 4, out_specs=[blk] * 3, out_shape=[flat] * 3,
        compiler_params=_cp("parallel"),
    )(*[a.reshape(rows, cols) for a in (w, g, m, v)])
    return [o.reshape(shape) for o in outs]


def _me():
    return lax.axis_index("x"), lax.axis_index("y"), lax.axis_index("c")


def _other_chips(x, y):
    return [(1 - x, y), (x, 1 - y), (1 - x, 1 - y)]


def pair_share(sums):
    n = len(sums)

    def body(*refs):
        out_refs = refs[n:2 * n]
        send_sems, recv_sems = refs[2 * n:]
        x, y, c = _me()
        sibling = (x, y, 1 - c)
        cps = [pltpu.make_async_remote_copy(src_ref=out_refs[a].at[c], dst_ref=out_refs[a].at[c], send_sem=send_sems.at[a],
                                            recv_sem=recv_sems.at[a], device_id=sibling, device_id_type=MESH)
               for a in range(n)]
        for cp in cps:
            cp.start()
        for cp in cps:
            cp.wait_send()
        for a in range(n):
            land = out_refs[a].at[1 - c]
            pltpu.make_async_remote_copy(src_ref=land, dst_ref=land, send_sem=send_sems.at[a], recv_sem=recv_sems.at[a],
                                         device_id=sibling, device_id_type=MESH).wait_recv()

    any_spec = pl.BlockSpec(memory_space=pl.ANY)
    return pl.pallas_call(
        body, name="pair_share", in_specs=[any_spec] * n, out_specs=[any_spec] * n,
        out_shape=[SDS(s.shape, f32) for s in sums], input_output_aliases={a: a for a in range(n)},
        scratch_shapes=[pltpu.SemaphoreType.DMA((n,)), pltpu.SemaphoreType.DMA((n,))],
    )(*sums)


def allsum_small(name, v, deps=()):
    R = v.shape[0]

    def body(v_ref, o_ref, all_ref, send_sems, recv_sems, local_sem):
        x, y, c = _me()
        me, sibling = (x, y, c), (x, y, 1 - c)
        chips = _other_chips(x, y)

        def rows(px, py, pc):
            return all_ref.at[4 * px + 2 * py + pc]

        def copy(k, block, to, src=None):
            return pltpu.make_async_remote_copy(
                src_ref=rows(*block) if src is None else src, dst_ref=rows(*block),
                send_sem=send_sems.at[k], recv_sem=recv_sems.at[k], device_id=to, device_id_type=MESH)

        mine = pltpu.make_async_copy(v_ref, rows(*me), local_sem)
        mine.start()
        first = [copy(0, me, sibling, src=v_ref)]
        first += [copy(1 + j, me, (*chip, c), src=v_ref) for j, chip in enumerate(chips)]
        for cp in first:
            cp.start()
        passed = [copy(4 + j, (*chip, c), sibling) for j, chip in enumerate(chips)]
        for j, chip in enumerate(chips):
            copy(1 + j, (*chip, c), me).wait_recv()
            passed[j].start()
        copy(0, sibling, me).wait_recv()
        for j, chip in enumerate(chips):
            copy(4 + j, (*chip, 1 - c), me).wait_recv()
        for cp in first + passed:
            cp.wait_send()
        mine.wait()
        acc = all_ref[0]
        for d in range(1, 8):
            acc = acc + all_ref[d]
        o_ref[...] = acc

    return pl.pallas_call(
        _with_deps(body, 1, deps), name=name,
        in_specs=[pl.BlockSpec(memory_space=pltpu.VMEM)] + [_DEP_SPEC] * len(deps),
        out_specs=pl.BlockSpec(memory_space=pltpu.VMEM),
        out_shape=SDS((R, 128), f32),
        scratch_shapes=[pltpu.VMEM((8, R, 128), f32), pltpu.SemaphoreType.DMA((7,)), pltpu.SemaphoreType.DMA((7,)),
                        pltpu.SemaphoreType.DMA],
        compiler_params=pltpu.CompilerParams(vmem_limit_bytes=V7X_VMEM_LIMIT),
    )(v, *deps)


def _swap_half(r):
    return jnp.concatenate([-r[..., 32:], r[..., :32]], axis=-1)


def _unswap_add(p, qg):
    return p + jnp.concatenate([qg[..., 32:], -qg[..., :32]], axis=-1)


def _block_diag(pw):
    L = pw.shape[0]
    out = jnp.zeros((L, 256, 256), pw.dtype)
    for gi in range(4):
        out = out.at[:, 64 * gi:64 * gi + 64, 64 * gi:64 * gi + 64].set(pw[:, gi])
    return out


def _to_col_shards(w):
    *lead, K, N = w.shape
    nl = len(lead)
    return w.reshape(*lead, K, N_CHIPS, N // N_CHIPS).transpose(*range(nl), nl + 1, nl, nl + 2)


def _from_col_shards(w):
    *lead, C, K, n = w.shape
    nl = len(lead)
    return w.transpose(*range(nl), nl + 1, nl, nl + 2).reshape(*lead, K, C * n)


def _step_serial_comm(x, mem, positions, ln_g, ln_b, ffn1_w13, ffn1_w2, w_in, pool_w, pool_scale, q_norm_g, w_uq, kv_norm_g, w_ukv, w_out, mem_wq, mem_wkv, mem_wo, ffn2_w13, ffn2_w2, loss_target, m_ln_g, m_ln_b, m_ffn1_w13, m_ffn1_w2, m_w_in, m_pool_w, m_pool_scale, m_q_norm_g, m_w_uq, m_kv_norm_g, m_w_ukv, m_w_out, m_mem_wq, m_mem_wkv, m_mem_wo, m_ffn2_w13, m_ffn2_w2, v_ln_g, v_ln_b, v_ffn1_w13, v_ffn1_w2, v_w_in, v_pool_w, v_pool_scale, v_q_norm_g, v_w_uq, v_kv_norm_g, v_w_ukv, v_w_out, v_mem_wq, v_mem_wkv, v_mem_wo, v_ffn2_w13, v_ffn2_w2):
    L = DEPTH
    S = x.shape[1]
    qx, qy, _ = _me()
    chip = 2 * qx + qy

    big = [ffn1_w13, ffn1_w2, w_in, w_uq, w_ukv, w_out, mem_wq, mem_wkv, mem_wo, ffn2_w13, ffn2_w2]
    (g_f1w13, g_f1w2, g_win, g_wuq, g_wukv, g_wout, g_mwq, g_mwkv, g_mwo, g_f2w13, g_f2w2) = gather_weights(
        [w.astype(bf16) for w in big])
    f1w2 = g_f1w2.reshape(L, D_FF, D_MODEL)
    f2w2 = g_f2w2.reshape(L, D_FF, D_MODEL)
    win = g_win.reshape(L, D_MODEL, D_IN)
    win_ext = jnp.concatenate([win, _swap_half(win[..., D_IN - QK_ROPE:])], axis=-1)
    wuq = _from_col_shards(g_wuq).reshape(L, Q_LORA, MLA_HEADS, QK_NOPE + QK_ROPE)
    wq_ext = jnp.concatenate([wuq, _swap_half(wuq[..., QK_NOPE:])], axis=-1).transpose(0, 2, 1, 3)
    wukv = _from_col_shards(g_wukv)
    wout = g_wout.reshape(L, D_MODEL, D_MODEL)
    wout_pool, wout_mla = wout[:, :POOL_WIDTH], wout[:, POOL_WIDTH:]
    mwq = g_mwq.reshape(L, D_MODEL, D_MODEL)
    mwo = g_mwo.reshape(L, D_MODEL, D_MODEL)
    wbd = _block_diag(pool_w.astype(bf16))

    ln_pad = jnp.zeros((2, L, 4, N_CHIPS, D_MODEL // N_CHIPS), f32)
    ln_pad = lax.dynamic_update_slice(ln_pad, jnp.stack([ln_g, ln_b])[:, :, :, None, :], (0, 0, 0, chip, 0))
    ln_full = allsum_small("allsum_ln", ln_pad.reshape(-1, 128)) * 0.5
    ln_full = ln_full.reshape(2, L, 4, D_MODEL)
    lng, lnb = ln_full[0], ln_full[1]

    half = QK_ROPE // 2
    inv_freq = ROPE_BASE ** (-jnp.arange(half, dtype=f32) / half)
    ang = positions[0].astype(f32)[:, None] * inv_freq
    cos, sin = jnp.cos(ang), jnp.sin(ang)
    cs = jnp.concatenate([cos, cos, sin, sin], axis=-1)

    memb = mem[0].astype(bf16)
    xf = x[0]
    xb = xf.astype(bf16)
    vec = lambda a: a.reshape(1, -1)

    saved = []
    for l in range(L):
        sv = {}
        sv["x0b"] = xb
        gate, up, act = ffn_up(f"ffn1_up_{l}", xb, g_f1w13, l)
        z1, x1f, x1b = proj_res_ln(f"ffn1_down_{l}", [act], [f1w2], [l], xf, vec(lng[l, 0]), vec(lnb[l, 0]), 0.5)
        sv.update(gate1=gate, up1=up, act1=act, z1=z1, x1b=x1b)
        u, cq, ckv, cqn, ckvn, q, k, v = mix_pre(f"mix_pre_{l}", x1b, win_ext, wq_ext, wukv, l,
                                                   vec(q_norm_g[l]), vec(kv_norm_g[l]), cs)
        dpool, ypool = pool_fwd(f"pool_fwd_{l}", u, wbd[l], vec(pool_scale[l]))
        o, lse = mla_attn_fwd(f"mla_fwd_{l}", q, k, v)
        z2, x2f, x2b = proj_res_ln(f"mix_out_{l}", [ypool, o], [wout_pool, wout_mla], [l, l], x1f,
                                   vec(lng[l, 1]), vec(lnb[l, 1]), 1.0)
        sv.update(cq=cq, ckv=ckv, cqn=cqn, ckvn=ckvn, q=q, k=k, v=v, dpool=dpool, ypool=ypool, o=o, lse=lse, z2=z2, x2b=x2b)
        kvm = mm_nn_shard(f"mem_kv_{l}", memb, g_mwkv, l)
        cq_, co_, z3, x3f, x3b = cross_fwd(f"cross_fwd_{l}", x2b, x2f, mwq, mwo, l, kvm, vec(lng[l, 2]), vec(lnb[l, 2]))
        sv.update(kvm=kvm, crq=cq_, cro=co_, z3=z3, x3b=x3b)
        gate, up, act = ffn_up(f"ffn2_up_{l}", x3b, g_f2w13, l)
        z4, xf, xb = proj_res_ln(f"ffn2_down_{l}", [act], [f2w2], [l], x3f, vec(lng[l, 3]), vec(lnb[l, 3]), 0.5)
        sv.update(gate2=gate, up2=up, act2=act, z4=z4)
        saved.append(sv)

    dy, loss_blk = loss_grad("loss_grad", xf, loss_target[0])
    loss = lax.psum(loss_blk[0, 0], ("x", "y", "c"))

    G = dict(f1w13=None, f1w2=None, mwq=None, mwkv=None, mwo=None, f2w13=None, f2w2=None)
    small = {k_: [None] * L for k_ in ("win", "wuq", "wukv", "wout", "pool_w", "pool_scale", "gq", "gkv", "lng", "lnb")}
    for l in reversed(range(L)):
        sv = saved[l]
        dlg, dlb = [None] * 4, [None] * 4
        dzb, dres, dlg[3], dlb[3] = ln_bwd(f"ln4_bwd_{l}", dy, sv["z4"], vec(lng[l, 3]), 0.5)
        dh = ffn_bwd_da(f"ffn2_bwd_da_{l}", dzb, f2w2, l, sv["gate2"], sv["up2"])
        G["f2w2"] = mm_tn(f"ffn2_dw2_{l}", sv["act2"], dzb, "nat", l, G["f2w2"])
        G["f2w13"] = mm_tn(f"ffn2_dw13_{l}", sv["x3b"], dh, "shard", l, G["f2w13"])
        dy = ffn_dx(f"ffn2_dx_{l}", dh, g_f2w13, l, dres)
        dzb, dres, dlg[2], dlb[2] = ln_bwd(f"ln3_bwd_{l}", dy, sv["z3"], vec(lng[l, 2]), 1.0)
        dqc, dkvm = cross_bwd(f"cross_bwd_{l}", dzb, mwo, l, sv["crq"], sv["kvm"])
        G["mwo"] = mm_tn(f"cross_dwo_{l}", sv["cro"], dzb, "nat", l, G["mwo"])
        G["mwq"] = mm_tn(f"cross_dwq_{l}", sv["x2b"], dqc, "nat", l, G["mwq"])
        G["mwkv"] = mm_tn(f"cross_dwkv_{l}", memb, dkvm, "shard", l, G["mwkv"])
        dy = mm_nt_res(f"cross_dx_{l}", [dqc], [mwq], [l], dres, f32)
        dzb, dres, dlg[1], dlb[1] = ln_bwd(f"ln2_bwd_{l}", dy, sv["z2"], vec(lng[l, 1]), 1.0)
        dyp = mm_nt_res(f"mix_dpool_{l}", [dzb], [wout_pool], [l], None, bf16)
        do = mm_nt_res(f"mix_do_{l}", [dzb], [wout_mla], [l], None, bf16)
        dwo_p = mm_tn(f"mix_dwout_pool_{l}", sv["ypool"], dzb)
        dwo_m = mm_tn(f"mix_dwout_mla_{l}", sv["o"], dzb)
        small["wout"][l] = jnp.concatenate([dwo_p, dwo_m], axis=0)
        dq, dk, dv = mla_attn_bwd(f"mla_bwd_{l}", sv["q"], sv["k"], sv["v"], sv["o"], do, sv["lse"])
        dqe, dkv, dh_rest, dgq, dgkv = mix_post_bwd(f"mix_post_bwd_{l}", dq, dk, dv, wq_ext, wukv, l, sv["cq"], sv["ckv"],
                                                     vec(q_norm_g[l]), vec(kv_norm_g[l]), cs)
        du, dyw, dscale = pool_bwd(f"pool_bwd_{l}", dyp, sv["dpool"], wbd[l], vec(pool_scale[l]))
        dwq_e = mm_tn(f"mix_dwuq_{l}", sv["cqn"], dqe).reshape(Q_LORA, MLA_HEADS, 256)
        small["wuq"][l] = jnp.concatenate(
            [dwq_e[..., :QK_NOPE], _unswap_add(dwq_e[..., QK_NOPE:QK_NOPE + QK_ROPE], dwq_e[..., QK_NOPE + QK_ROPE:])],
            axis=-1).reshape(Q_LORA, MLA_HEADS * (QK_NOPE + QK_ROPE))
        small["wukv"][l] = mm_tn(f"mix_dwukv_{l}", sv["ckvn"], dkv)
        dwbd = mm_tn(f"pool_dw_{l}", sv["dpool"], dyw)
        small["pool_w"][l] = jnp.stack([dwbd[64 * gi:64 * gi + 64, 64 * gi:64 * gi + 64] for gi in range(4)])
        small["pool_scale"][l], small["gq"][l], small["gkv"][l] = dscale[0], dgq[0], dgkv[0]
        dh_ext = jnp.concatenate([du, dh_rest], axis=1)
        dwin_e = mm_tn(f"mix_dwin_{l}", sv["x1b"], dh_ext)
        small["win"][l] = jnp.concatenate(
            [dwin_e[:, :D_IN - QK_ROPE], _unswap_add(dwin_e[:, D_IN - QK_ROPE:D_IN], dwin_e[:, D_IN:])], axis=-1)
        dy = mm_nt_res(f"mix_dx_{l}", [dh_ext], [win_ext], [l], dres, f32)
        dzb, dres, dlg[0], dlb[0] = ln_bwd(f"ln1_bwd_{l}", dy, sv["z1"], vec(lng[l, 0]), 0.5)
        dh = ffn_bwd_da(f"ffn1_bwd_da_{l}", dzb, f1w2, l, sv["gate1"], sv["up1"])
        G["f1w2"] = mm_tn(f"ffn1_dw2_{l}", sv["act1"], dzb, "nat", l, G["f1w2"])
        G["f1w13"] = mm_tn(f"ffn1_dw13_{l}", sv["x0b"], dh, "shard", l, G["f1w13"])
        dy = ffn_dx(f"ffn1_dx_{l}", dh, g_f1w13, l, dres)
        small["lng"][l] = jnp.concatenate(dlg, axis=0)
        small["lnb"][l] = jnp.concatenate(dlb, axis=0)
    grad_x = dy[None]

    row_shards = lambda a, K: a.reshape(L, N_CHIPS, K // N_CHIPS, a.shape[-1])
    g_list = [G["f1w13"], row_shards(G["f1w2"], D_FF),
              jnp.stack(small["win"]).reshape(L, N_CHIPS, D_MODEL // N_CHIPS, D_IN),
              _to_col_shards(jnp.stack(small["wuq"])), _to_col_shards(jnp.stack(small["wukv"])),
              jnp.stack(small["wout"]).reshape(L, N_CHIPS, D_MODEL // N_CHIPS, D_MODEL),
              row_shards(G["mwq"], D_MODEL), G["mwkv"], row_shards(G["mwo"], D_MODEL),
              G["f2w13"], row_shards(G["f2w2"], D_FF)]
    big_grads = reduce_grads(g_list)

    rep = [jnp.stack(small["pool_w"]).reshape(-1), jnp.stack(small["pool_scale"]).reshape(-1),
           jnp.stack(small["gq"]).reshape(-1), jnp.stack(small["gkv"]).reshape(-1),
           jnp.stack(small["lng"]).reshape(-1), jnp.stack(small["lnb"]).reshape(-1)]
    sizes = [r.shape[0] for r in rep]
    packed = jnp.concatenate(rep)
    pad = (-packed.shape[0]) % 1024
    tot = allsum_small("allsum_small_grads", jnp.pad(packed, (0, pad)).reshape(-1, 128)).reshape(-1)
    offs = [0]
    for s_ in sizes:
        offs.append(offs[-1] + s_)
    parts = [tot[offs[i]:offs[i + 1]] for i in range(len(sizes))]
    g_pool_w = parts[0].reshape(pool_w.shape)
    g_pool_scale = parts[1].reshape(pool_scale.shape)
    g_gq = parts[2].reshape(q_norm_g.shape)
    g_gkv = parts[3].reshape(kv_norm_g.shape)
    shard_cols = lambda a: lax.dynamic_slice_in_dim(a.reshape(L, 4, D_MODEL), chip * (D_MODEL // N_CHIPS),
                                                    D_MODEL // N_CHIPS, axis=2)
    g_lng, g_lnb = shard_cols(parts[4]), shard_cols(parts[5])

    (gf1w13, gf1w2, gwin, gwuq, gwukv, gwout, gmwq, gmwkv, gmwo, gf2w13, gf2w2) = big_grads
    grads = [g_lng, g_lnb, gf1w13, gf1w2, gwin, g_pool_w, g_pool_scale, g_gq, gwuq, g_gkv, gwukv, gwout, gmwq, gmwkv,
             gmwo, gf2w13, gf2w2]
    ws = [ln_g, ln_b, ffn1_w13, ffn1_w2, w_in, pool_w, pool_scale, q_norm_g, w_uq, kv_norm_g, w_ukv, w_out, mem_wq,
          mem_wkv, mem_wo, ffn2_w13, ffn2_w2]
    ms = [m_ln_g, m_ln_b, m_ffn1_w13, m_ffn1_w2, m_w_in, m_pool_w, m_pool_scale, m_q_norm_g, m_w_uq, m_kv_norm_g, m_w_ukv,
          m_w_out, m_mem_wq, m_mem_wkv, m_mem_wo, m_ffn2_w13, m_ffn2_w2]
    vs = [v_ln_g, v_ln_b, v_ffn1_w13, v_ffn1_w2, v_w_in, v_pool_w, v_pool_scale, v_q_norm_g, v_w_uq, v_kv_norm_g, v_w_ukv,
          v_w_out, v_mem_wq, v_mem_wkv, v_mem_wo, v_ffn2_w13, v_ffn2_w2]
    deltas, new_ms, new_vs = [], [], []
    for a, (w_, g_, m_, v_) in enumerate(zip(ws, grads, ms, vs)):
        d_, nm_, nv_ = adamw(f"adamw_{a}", w_, g_.reshape(w_.shape), m_, v_)
        deltas.append(d_)
        new_ms.append(nm_)
        new_vs.append(nv_)
    grads = [g_.reshape(w_.shape) for g_, w_ in zip(grads, ws)]
    return (loss, grad_x, *grads, *deltas, *new_ms, *new_vs)


_HBM_SPEC = pl.BlockSpec(memory_space=pltpu.HBM)
_SEM_SPEC = pl.BlockSpec(memory_space=pltpu.SEMAPHORE)
_ANY_SPEC = pl.BlockSpec(memory_space=pl.ANY)
_DATAFLOW = pltpu.SideEffectType.DATAFLOW_SIDE_EFFECTING


def _split_call(name, body_fn, bufs, sems_in, sems_out_sizes, after):
    nb, ni, no = len(bufs), len(sems_in), len(sems_out_sizes)
    has_after = after is not None

    def body(*refs):
        k = nb + ni + (1 if has_after else 0)
        body_fn(refs[:nb], refs[nb:nb + ni], refs[k:k + no])
        refs[-1][...] = jnp.zeros((8, 128), f32)

    outs = pl.pallas_call(
        body, name=name,
        in_specs=[_HBM_SPEC] * nb + [_SEM_SPEC] * ni + ([_ANY_SPEC] if has_after else []),
        out_specs=[_SEM_SPEC] * no + [_HBM_SPEC] * nb + [pl.BlockSpec(memory_space=pltpu.VMEM)],
        out_shape=[pltpu.SemaphoreType.DMA((s,)) for s in sems_out_sizes]
        + [pltpu.HBM(b.shape, b.dtype) for b in bufs] + [SDS((8, 128), f32)],
        input_output_aliases={i: no + i for i in range(nb)},
        compiler_params=pltpu.CompilerParams(has_side_effects=_DATAFLOW),
    )(*[pltpu.with_memory_space_constraint(b, pltpu.HBM) for b in bufs], *sems_in, *([after] if has_after else []))
    return list(outs[no:no + nb]), list(outs[:no]), outs[-1]


def _rcopy(src, dst, ssem, rsem, to):
    return pltpu.make_async_remote_copy(src_ref=src, dst_ref=dst, send_sem=ssem, recv_sem=rsem, device_id=to,
                                        device_id_type=MESH)


def gather_start(name, groups, after):
    flat = [b for bufs, _ in groups for b in bufs]
    sizes = [3 * len(bufs) for bufs, _ in groups for _ in range(2)]

    def body_fn(b_in, s_in, s_out):
        x, y, c = _me()
        q = 2 * x + y
        chips = _other_chips(x, y)
        pos = 0
        for gi, (bufs, owner) in enumerate(groups):
            refs = b_in[pos:pos + len(bufs)]
            pos += len(bufs)

            @pl.when(c == owner)
            def _(refs=refs, send=s_out[2 * gi], recv=s_out[2 * gi + 1]):
                for a, r in enumerate(refs):
                    for k, (cx, cy) in enumerate(chips):
                        _rcopy(r.at[q], r.at[q], send.at[3 * a + k], recv.at[3 * a + k], (cx, cy, c)).start()

    outs, sems, token = _split_call(name, body_fn, flat, [], sizes, after)
    res, pos = [], 0
    for gi, (bufs, owner) in enumerate(groups):
        res.append((outs[pos:pos + len(bufs)], sems[2 * gi], sems[2 * gi + 1], owner))
        pos += len(bufs)
    return res, token


def gather_forward(name, grp, after):
    bufs, send, recv, owner = grp
    n3 = 3 * len(bufs)

    def body_fn(b_in, s_in, s_out):
        x, y, c = _me()
        q = 2 * x + y
        sibling = (x, y, 1 - c)
        chips = _other_chips(x, y)

        @pl.when(c == owner)
        def _():
            for a, r in enumerate(b_in):
                for k, (cx, cy) in enumerate(chips):
                    i = 3 * a + k
                    land = r.at[2 * cx + cy]
                    _rcopy(r.at[q], r.at[q], s_in[0].at[i], s_in[1].at[i], (cx, cy, c)).wait_send()
                    _rcopy(land, land, s_in[0].at[i], s_in[1].at[i], (cx, cy, c)).wait_recv()
                    _rcopy(land, land, s_out[0].at[i], s_out[1].at[i], sibling).start()

    outs, sems, token = _split_call(name, body_fn, bufs, [send, recv], [n3, n3], after)
    return (outs, sems[0], sems[1], owner), token


def gather_finish(name, grp, after):
    bufs, fsend, frecv, owner = grp

    def body_fn(b_in, s_in, s_out):
        x, y, c = _me()
        sibling = (x, y, 1 - c)
        chips = _other_chips(x, y)

        def each(wait):
            for a, r in enumerate(b_in):
                for k, (cx, cy) in enumerate(chips):
                    land = r.at[2 * cx + cy]
                    wait(_rcopy(land, land, s_in[0].at[3 * a + k], s_in[1].at[3 * a + k], sibling))

        @pl.when(c == owner)
        def _():
            each(lambda cp: cp.wait_send())

        @pl.when(c != owner)
        def _():
            each(lambda cp: cp.wait_recv())

    outs, _, _ = _split_call(name, body_fn, bufs, [fsend, frecv], [], after)
    return outs


def pair_send_start(name, gs, owner, after):
    n = len(gs)
    lands = [lax.empty(g.shape, g.dtype) for g in gs]

    def body_fn(b_in, s_in, s_out):
        x, y, c = _me()

        @pl.when(c == 1 - owner)
        def _():
            for a in range(n):
                _rcopy(b_in[a], b_in[n + a], s_out[0].at[a], s_out[1].at[a], (x, y, owner)).start()

    outs, sems, token = _split_call(name, body_fn, list(gs) + lands, [], [n, n], after)
    return (outs[:n], outs[n:], sems[0], sems[1], owner), token


def pair_send_wait(name, st, after):
    gs, lands, send, recv, owner = st
    n = len(gs)

    def body_fn(b_in, s_in, s_out):
        x, y, c = _me()

        @pl.when(c == 1 - owner)
        def _():
            for a in range(n):
                _rcopy(b_in[a], b_in[n + a], s_in[0].at[a], s_in[1].at[a], (x, y, owner)).wait_send()

        @pl.when(c == owner)
        def _():
            for a in range(n):
                _rcopy(b_in[a], b_in[n + a], s_in[0].at[a], s_in[1].at[a], (x, y, 1 - owner)).wait_recv()

    outs, _, _ = _split_call(name, body_fn, list(gs) + list(lands), [send, recv], [], after)
    return outs[:n], outs[n:]


def chip_exchange_start(name, psums, owner, after):
    n = len(psums)
    lands = [lax.empty((3,) + p.shape[1:], p.dtype) for p in psums]

    def body_fn(b_in, s_in, s_out):
        x, y, c = _me()
        chips = _other_chips(x, y)

        @pl.when(c == owner)
        def _():
            for a in range(n):
                for k, (cx, cy) in enumerate(chips):
                    _rcopy(b_in[a].at[2 * cx + cy], b_in[n + a].at[k], s_out[0].at[3 * a + k], s_out[1].at[3 * a + k],
                           (cx, cy, c)).start()

    outs, sems, token = _split_call(name, body_fn, list(psums) + lands, [], [3 * n, 3 * n], after)
    return (outs[:n], outs[n:], sems[0], sems[1], owner), token


def chip_exchange_wait(name, st, after):
    psums, lands, send, recv, owner = st
    n = len(psums)

    def body_fn(b_in, s_in, s_out):
        x, y, c = _me()
        chips = _other_chips(x, y)

        @pl.when(c == owner)
        def _():
            for a in range(n):
                for k, (cx, cy) in enumerate(chips):
                    cp = _rcopy(b_in[a].at[2 * cx + cy], b_in[n + a].at[k], s_in[0].at[3 * a + k], s_in[1].at[3 * a + k],
                                (cx, cy, c))
                    cp.wait_send()
                    cp.wait_recv()

    outs, _, _ = _split_call(name, body_fn, list(psums) + list(lands), [send, recv], [], after)
    return outs[:n], outs[n:]


def pair_sum(name, g, recv):
    shape = g.shape
    cols = shape[-1]
    rows = math.prod(shape[:-1])
    tr = _row_tile(rows, cols)

    def body(g_ref, r_ref, o_ref):
        o_ref[...] = (g_ref[...] + r_ref[...]).astype(bf16)

    blk = pl.BlockSpec((tr, cols), lambda i: (i, 0))
    out = pl.pallas_call(
        body, name=name, grid=(rows // tr,), in_specs=[blk, blk], out_specs=blk, out_shape=SDS((rows, cols), bf16),
        compiler_params=_cp("parallel"),
    )(g.reshape(rows, cols), recv.reshape(rows, cols))
    return out.reshape(shape)


def chip_sum(name, psum, recv, q_arr, layer, prev):
    shard = psum.shape[1:]
    cols = shard[-1]
    rows = math.prod(shard[:-1])
    tr = _row_tile(rows, cols)

    def body(q_ref, p_ref, r_ref, *rest):
        rest[-1][0] = ((p_ref[0].astype(f32) + r_ref[0].astype(f32)) + r_ref[1].astype(f32)) + r_ref[2].astype(f32)

    in_specs = [pl.BlockSpec((1, tr, cols), lambda i, q_ref: (q_ref[0], i, 0)),
                pl.BlockSpec((3, tr, cols), lambda i, q_ref: (0, i, 0))]
    args = [q_arr, psum.reshape(N_CHIPS, rows, cols), recv.reshape(3, rows, cols)]
    aliases = {}
    if prev is not None:
        in_specs.append(pl.BlockSpec(memory_space=pl.ANY))
        args.append(prev.reshape(DEPTH, rows, cols))
        aliases = {3: 0}
    out = pl.pallas_call(
        body, name=name,
        grid_spec=pltpu.PrefetchScalarGridSpec(
            num_scalar_prefetch=1, grid=(rows // tr,), in_specs=in_specs,
            out_specs=pl.BlockSpec((1, tr, cols), lambda i, q_ref: (layer, i, 0))),
        out_shape=SDS((DEPTH, rows, cols), f32), input_output_aliases=aliases, compiler_params=_cp("parallel"),
    )(*args)
    return out.reshape((DEPTH,) + shard)


W_NAMES = ("f1w13", "f1w2", "win", "wuq", "wukv", "wout", "mwq", "mwkv", "mwo", "f2w13", "f2w2")
MIX_NAMES = ("win", "wuq", "wukv")
MID_NAMES = ("wout", "mwq", "mwkv", "mwo")
FFN2_NAMES = ("f2w13", "f2w2")


def kernel(x, mem, positions, ln_g, ln_b, ffn1_w13, ffn1_w2, w_in, pool_w, pool_scale, q_norm_g, w_uq, kv_norm_g, w_ukv, w_out, mem_wq, mem_wkv, mem_wo, ffn2_w13, ffn2_w2, loss_target, m_ln_g, m_ln_b, m_ffn1_w13, m_ffn1_w2, m_w_in, m_pool_w, m_pool_scale, m_q_norm_g, m_w_uq, m_kv_norm_g, m_w_ukv, m_w_out, m_mem_wq, m_mem_wkv, m_mem_wo, m_ffn2_w13, m_ffn2_w2, v_ln_g, v_ln_b, v_ffn1_w13, v_ffn1_w2, v_w_in, v_pool_w, v_pool_scale, v_q_norm_g, v_w_uq, v_kv_norm_g, v_w_ukv, v_w_out, v_mem_wq, v_mem_wkv, v_mem_wo, v_ffn2_w13, v_ffn2_w2):
    L = DEPTH
    qx, qy, _ = _me()
    chip = 2 * qx + qy
    vec = lambda a: a.reshape(1, -1)

    shards = dict(zip(W_NAMES, (ffn1_w13, ffn1_w2, w_in, w_uq, w_ukv, w_out, mem_wq, mem_wkv, mem_wo, ffn2_w13, ffn2_w2)))

    def place(sh):
        return lax.dynamic_update_slice(jnp.zeros((N_CHIPS,) + sh.shape, bf16), sh.astype(bf16)[None],
                                        (chip,) + (0,) * sh.ndim)

    bufs = [{n: place(shards[n][l]) for n in W_NAMES} for l in range(L)]
    gw = [dict(), dict()]
    (g0,), tok = gather_start("gather_a_start", [([bufs[0]["f1w13"], bufs[0]["f1w2"]], 0)], None)
    g0, tok = gather_forward("gather_a_forward", g0, None)

    ln_pad = jnp.zeros((2, L, 4, N_CHIPS, D_MODEL // N_CHIPS), f32)
    ln_pad = lax.dynamic_update_slice(ln_pad, jnp.stack([ln_g, ln_b])[:, :, :, None, :], (0, 0, 0, chip, 0))
    ln_sum = allsum_small("allsum_ln", ln_pad.reshape(-1, 128), (tok,))
    ln_full = (ln_sum * 0.5).reshape(2, L, 4, D_MODEL)
    lng, lnb = ln_full[0], ln_full[1]

    gw[0]["f1w13"], gw[0]["f1w2"] = gather_finish("gather_a_finish", g0, ln_sum)
    (g_mix, g_mid, g_ffn2, g_l1), tok_b = gather_start(
        "gather_b_start",
        [([bufs[0][n] for n in MIX_NAMES], 0), ([bufs[0][n] for n in MID_NAMES], 0), ([bufs[0][n] for n in FFN2_NAMES], 0),
         ([bufs[1][n] for n in W_NAMES], 1)], ln_sum)

    half = QK_ROPE // 2
    inv_freq = ROPE_BASE ** (-jnp.arange(half, dtype=f32) / half)
    ang = positions[0].astype(f32)[:, None] * inv_freq
    cos, sin = jnp.cos(ang), jnp.sin(ang)
    cs = jnp.concatenate([cos, cos, sin, sin], axis=-1)

    memb = mem[0].astype(bf16)
    xf = x[0]
    xb = xf.astype(bf16)
    dep = (tok_b,)

    saved, W = [], [None, None]
    for l in range(L):
        sv = {}
        if l == 1:
            gl1 = gather_finish("gather_l1_finish", g_l1, xb)
            gw[1] = dict(zip(W_NAMES, gl1))
        sv["x0b"] = xb
        f1w13 = gw[l]["f1w13"][None]
        gate, up, act = ffn_up(f"ffn1_up_{l}", xb, f1w13, 0, dep)
        dep = ()
        if l == 0:
            g_mix, _ = gather_forward("gather_mix_forward", g_mix, act)
        z1, x1f, x1b = proj_res_ln(f"ffn1_down_{l}", [act], [gw[l]["f1w2"].reshape(1, D_FF, D_MODEL)], [0], xf,
                                   vec(lng[l, 0]), vec(lnb[l, 0]), 0.5)
        sv.update(gate1=gate, up1=up, act1=act, z1=z1, x1b=x1b)
        if l == 0:
            gw[0].update(zip(MIX_NAMES, gather_finish("gather_mix_finish", g_mix, x1b)))
            g_mid, _ = gather_forward("gather_mid_forward", g_mid, x1b)
        win = gw[l]["win"].reshape(D_MODEL, D_IN)
        win_ext = jnp.concatenate([win, _swap_half(win[:, D_IN - QK_ROPE:])], axis=-1)[None]
        wuq = _from_col_shards(gw[l]["wuq"]).reshape(Q_LORA, MLA_HEADS, QK_NOPE + QK_ROPE)
        wq_ext = jnp.concatenate([wuq, _swap_half(wuq[..., QK_NOPE:])], axis=-1).transpose(1, 0, 2)[None]
        wukv = _from_col_shards(gw[l]["wukv"])[None]
        wbd = _block_diag(pool_w[l][None].astype(bf16))[0]
        u, cq, ckv, cqn, ckvn, q, k, v = mix_pre(f"mix_pre_{l}", x1b, win_ext, wq_ext, wukv, 0,
                                                   vec(q_norm_g[l]), vec(kv_norm_g[l]), cs)
        dpool, ypool = pool_fwd(f"pool_fwd_{l}", u, wbd, vec(pool_scale[l]))
        o, lse = mla_attn_fwd(f"mla_fwd_{l}", q, k, v)
        if l == 0:
            gw[0].update(zip(MID_NAMES, gather_finish("gather_mid_finish", g_mid, o)))
            g_ffn2, tok_f = gather_forward("gather_ffn2_forward", g_ffn2, o)
            g_l1, tok_l = gather_forward("gather_l1_forward", g_l1, o)
            dep = (tok_f, tok_l)
        wout = gw[l]["wout"].reshape(D_MODEL, D_MODEL)
        wout_pool, wout_mla = wout[None, :POOL_WIDTH], wout[None, POOL_WIDTH:]
        mwq = gw[l]["mwq"].reshape(1, D_MODEL, D_MODEL)
        mwo = gw[l]["mwo"].reshape(1, D_MODEL, D_MODEL)
        mwkv = gw[l]["mwkv"][None]
        z2, x2f, x2b = proj_res_ln(f"mix_out_{l}", [ypool, o], [wout_pool, wout_mla], [0, 0], x1f,
                                   vec(lng[l, 1]), vec(lnb[l, 1]), 1.0, dep)
        dep = ()
        sv.update(cq=cq, ckv=ckv, cqn=cqn, ckvn=ckvn, q=q, k=k, v=v, dpool=dpool, ypool=ypool, o=o, lse=lse, z2=z2, x2b=x2b)
        kvm = mm_nn_shard(f"mem_kv_{l}", memb, mwkv, 0)
        cq_, co_, z3, x3f, x3b = cross_fwd(f"cross_fwd_{l}", x2b, x2f, mwq, mwo, 0, kvm, vec(lng[l, 2]), vec(lnb[l, 2]))
        sv.update(kvm=kvm, crq=cq_, cro=co_, z3=z3, x3b=x3b)
        if l == 0:
            gw[0].update(zip(FFN2_NAMES, gather_finish("gather_ffn2_finish", g_ffn2, x3b)))
        f2w13 = gw[l]["f2w13"][None]
        f2w2 = gw[l]["f2w2"].reshape(1, D_FF, D_MODEL)
        gate, up, act = ffn_up(f"ffn2_up_{l}", x3b, f2w13, 0)
        z4, xf, xb = proj_res_ln(f"ffn2_down_{l}", [act], [f2w2], [0], x3f, vec(lng[l, 3]), vec(lnb[l, 3]), 0.5)
        sv.update(gate2=gate, up2=up, act2=act, z4=z4)
        W[l] = dict(f1w13=f1w13, f1w2=gw[l]["f1w2"].reshape(1, D_FF, D_MODEL), win_ext=win_ext, wq_ext=wq_ext, wukv=wukv,
                    wbd=wbd, wout_pool=wout_pool, wout_mla=wout_mla, mwq=mwq, mwo=mwo, f2w13=f2w13, f2w2=f2w2)
        saved.append(sv)

    dy, loss_blk = loss_grad("loss_grad", xf, loss_target[0])
    loss = lax.psum(loss_blk[0, 0], ("x", "y", "c"))

    row_shards = lambda a: a.reshape(N_CHIPS, a.shape[0] // N_CHIPS, a.shape[1])
    small = {k_: [None] * L for k_ in ("pool_w", "pool_scale", "gq", "gkv", "lng", "lnb")}
    q_arr = jnp.reshape(chip, (1,)).astype(jnp.int32)
    rest_names = [n for n in W_NAMES if n not in ("f1w13", "f1w2")]

    def red_begin(tag, gs, owner):
        return pair_send_start(f"pair_send_start_{tag}", gs, owner, None)

    def red_mid(tag, st, owner, after):
        gs_, lands_ = pair_send_wait(f"pair_send_wait_{tag}", st, after)
        ps = [pair_sum(f"pair_sum_{tag}_{a}", g_, r_) for a, (g_, r_) in enumerate(zip(gs_, lands_))]
        return chip_exchange_start(f"chip_exchange_start_{tag}", ps, owner, None)

    def red_end(tag, st, layer, prevs, after):
        ps, lands_ = chip_exchange_wait(f"chip_exchange_wait_{tag}", st, after)
        return [chip_sum(f"chip_sum_{tag}_{a}", p_, r_, q_arr, layer, s_)
                for a, (p_, r_, s_) in enumerate(zip(ps, lands_, prevs))]

    st_p1 = st_c1 = st_pa = st_ca = None
    for l in reversed(range(L)):
        sv, w = saved[l], W[l]
        g = {}
        dlg, dlb = [None] * 4, [None] * 4
        dzb, dres, dlg[3], dlb[3] = ln_bwd(f"ln4_bwd_{l}", dy, sv["z4"], vec(lng[l, 3]), 0.5, dep)
        dep = ()
        dh = ffn_bwd_da(f"ffn2_bwd_da_{l}", dzb, w["f2w2"], 0, sv["gate2"], sv["up2"])
        g["f2w2"] = row_shards(mm_tn(f"ffn2_dw2_{l}", sv["act2"], dzb))
        g["f2w13"] = mm_tn(f"ffn2_dw13_{l}", sv["x3b"], dh, True)
        dy = ffn_dx(f"ffn2_dx_{l}", dh, w["f2w13"], 0, dres)
        if l == 0:
            st_c1, tok = red_mid("l1", st_p1, 1, dy)
            dep = (tok, g["f2w2"], g["f2w13"])
        dzb, dres, dlg[2], dlb[2] = ln_bwd(f"ln3_bwd_{l}", dy, sv["z3"], vec(lng[l, 2]), 1.0, dep)
        dep = ()
        dqc, dkvm = cross_bwd(f"cross_bwd_{l}", dzb, w["mwo"], 0, sv["crq"], sv["kvm"])
        g["mwo"] = row_shards(mm_tn(f"cross_dwo_{l}", sv["cro"], dzb))
        g["mwq"] = row_shards(mm_tn(f"cross_dwq_{l}", sv["x2b"], dqc))
        g["mwkv"] = mm_tn(f"cross_dwkv_{l}", memb, dkvm, True)
        dy = mm_nt_res(f"cross_dx_{l}", [dqc], [w["mwq"]], [0], dres, f32)
        dzb, dres, dlg[1], dlb[1] = ln_bwd(f"ln2_bwd_{l}", dy, sv["z2"], vec(lng[l, 1]), 1.0)
        dyp = mm_nt_res(f"mix_dpool_{l}", [dzb], [w["wout_pool"]], [0], None, bf16)
        do = mm_nt_res(f"mix_do_{l}", [dzb], [w["wout_mla"]], [0], None, bf16)
        dwo_p = mm_tn(f"mix_dwout_pool_{l}", sv["ypool"], dzb)
        dwo_m = mm_tn(f"mix_dwout_mla_{l}", sv["o"], dzb)
        g["wout"] = row_shards(jnp.concatenate([dwo_p, dwo_m], axis=0))
        dq, dk, dv = mla_attn_bwd(f"mla_bwd_{l}", sv["q"], sv["k"], sv["v"], sv["o"], do, sv["lse"])
        dqe, dkv, dh_rest, dgq, dgkv = mix_post_bwd(f"mix_post_bwd_{l}", dq, dk, dv, w["wq_ext"], w["wukv"], 0, sv["cq"],
                                                     sv["ckv"], vec(q_norm_g[l]), vec(kv_norm_g[l]), cs)
        du, dyw, dscale = pool_bwd(f"pool_bwd_{l}", dyp, sv["dpool"], w["wbd"], vec(pool_scale[l]))
        dwq_e = mm_tn(f"mix_dwuq_{l}", sv["cqn"], dqe).reshape(Q_LORA, MLA_HEADS, 256)
        g["wuq"] = _to_col_shards(jnp.concatenate(
            [dwq_e[..., :QK_NOPE], _unswap_add(dwq_e[..., QK_NOPE:QK_NOPE + QK_ROPE], dwq_e[..., QK_NOPE + QK_ROPE:])],
            axis=-1).reshape(Q_LORA, MLA_HEADS * (QK_NOPE + QK_ROPE)))
        g["wukv"] = _to_col_shards(mm_tn(f"mix_dwukv_{l}", sv["ckvn"], dkv))
        dwbd = mm_tn(f"pool_dw_{l}", sv["dpool"], dyw)
        small["pool_w"][l] = jnp.stack([dwbd[64 * gi:64 * gi + 64, 64 * gi:64 * gi + 64] for gi in range(4)])
        small["pool_scale"][l], small["gq"][l], small["gkv"][l] = dscale[0], dgq[0], dgkv[0]
        dh_ext = jnp.concatenate([du, dh_rest], axis=1)
        dwin_e = mm_tn(f"mix_dwin_{l}", sv["x1b"], dh_ext)
        g["win"] = row_shards(jnp.concatenate(
            [dwin_e[:, :D_IN - QK_ROPE], _unswap_add(dwin_e[:, D_IN - QK_ROPE:D_IN], dwin_e[:, D_IN:])], axis=-1))
        dy = mm_nt_res(f"mix_dx_{l}", [dh_ext], [w["win_ext"]], [0], dres, f32)
        if l == 0:
            st_pa, tok = red_begin("a0", [g[n] for n in rest_names], 0)
            dep = (tok,)
        dzb, dres, dlg[0], dlb[0] = ln_bwd(f"ln1_bwd_{l}", dy, sv["z1"], vec(lng[l, 0]), 0.5, dep)
        dep = ()
        dh = ffn_bwd_da(f"ffn1_bwd_da_{l}", dzb, w["f1w2"], 0, sv["gate1"], sv["up1"])
        dy = ffn_dx(f"ffn1_dx_{l}", dh, w["f1w13"], 0, dres)
        if l == 0:
            st_ca, tok = red_mid("a0", st_pa, 0, dy)
            dep = (tok,)
        g["f1w2"] = row_shards(mm_tn(f"ffn1_dw2_{l}", sv["act1"], dzb, False, dep))
        g["f1w13"] = mm_tn(f"ffn1_dw13_{l}", sv["x0b"], dh, True, dep)
        dep = ()
        small["lng"][l] = jnp.concatenate(dlg, axis=0)
        small["lnb"][l] = jnp.concatenate(dlb, axis=0)
        if l == 1:
            st_p1, tok = red_begin("l1", [g[n] for n in W_NAMES], 1)
            dep = (tok,)
    grad_x = dy[None]

    st_pb, _ = red_begin("b0", [g["f1w13"], g["f1w2"]], 0)
    sums1 = dict(zip(W_NAMES, red_end("l1", st_c1, 1, [None] * len(W_NAMES), g["f1w13"])))
    st_cb, _ = red_mid("b0", st_pb, 0, sums1["f1w13"])
    sums0 = dict(zip(rest_names, red_end("a0", st_ca, 0, [sums1[n] for n in rest_names], sums1["f1w2"])))
    sums0["f1w13"], sums0["f1w2"] = red_end("b0", st_cb, 0, [sums1["f1w13"], sums1["f1w2"]], sums0["f2w2"])
    big_grads = pair_share([sums0[n] for n in W_NAMES])

    rep = [jnp.stack(small["pool_w"]).reshape(-1), jnp.stack(small["pool_scale"]).reshape(-1),
           jnp.stack(small["gq"]).reshape(-1), jnp.stack(small["gkv"]).reshape(-1),
           jnp.stack(small["lng"]).reshape(-1), jnp.stack(small["lnb"]).reshape(-1)]
    sizes = [r.shape[0] for r in rep]
    packed = jnp.concatenate(rep)
    pad = (-packed.shape[0]) % 1024
    tot = allsum_small("allsum_small_grads", jnp.pad(packed, (0, pad)).reshape(-1, 128)).reshape(-1)
    offs = [0]
    for s_ in sizes:
        offs.append(offs[-1] + s_)
    parts = [tot[offs[i]:offs[i + 1]] for i in range(len(sizes))]
    g_pool_w = parts[0].reshape(pool_w.shape)
    g_pool_scale = parts[1].reshape(pool_scale.shape)
    g_gq = parts[2].reshape(q_norm_g.shape)
    g_gkv = parts[3].reshape(kv_norm_g.shape)
    shard_cols = lambda a: lax.dynamic_slice_in_dim(a.reshape(L, 4, D_MODEL), chip * (D_MODEL // N_CHIPS),
                                                    D_MODEL // N_CHIPS, axis=2)
    g_lng, g_lnb = shard_cols(parts[4]), shard_cols(parts[5])

    (gf1w13, gf1w2, gwin, gwuq, gwukv, gwout, gmwq, gmwkv, gmwo, gf2w13, gf2w2) = big_grads
    grads = [g_lng, g_lnb, gf1w13, gf1w2, gwin, g_pool_w, g_pool_scale, g_gq, gwuq, g_gkv, gwukv, gwout, gmwq, gmwkv,
             gmwo, gf2w13, gf2w2]
    ws = [ln_g, ln_b, ffn1_w13, ffn1_w2, w_in, pool_w, pool_scale, q_norm_g, w_uq, kv_norm_g, w_ukv, w_out, mem_wq,
          mem_wkv, mem_wo, ffn2_w13, ffn2_w2]
    ms = [m_ln_g, m_ln_b, m_ffn1_w13, m_ffn1_w2, m_w_in, m_pool_w, m_pool_scale, m_q_norm_g, m_w_uq, m_kv_norm_g, m_w_ukv,
          m_w_out, m_mem_wq, m_mem_wkv, m_mem_wo, m_ffn2_w13, m_ffn2_w2]
    vs = [v_ln_g, v_ln_b, v_ffn1_w13, v_ffn1_w2, v_w_in, v_pool_w, v_pool_scale, v_q_norm_g, v_w_uq, v_kv_norm_g, v_w_ukv,
          v_w_out, v_mem_wq, v_mem_wkv, v_mem_wo, v_ffn2_w13, v_ffn2_w2]
    deltas, new_ms, new_vs = [], [], []
    for a, (w_, g_, m_, v_) in enumerate(zip(ws, grads, ms, vs)):
        d_, nm_, nv_ = adamw(f"adamw_{a}", w_, g_.reshape(w_.shape), m_, v_)
        deltas.append(d_)
        new_ms.append(nm_)
        new_vs.append(nv_)
    grads = [g_.reshape(w_.shape) for g_, w_ in zip(grads, ws)]
    return (loss, grad_x, *grads, *deltas, *new_ms, *new_vs)
```

```python
import functools
import math

import jax
import jax.numpy as jnp
from jax import lax
from jax.experimental import pallas as pl
from jax.experimental.pallas import tpu as pltpu

f32 = jnp.float32
bf16 = jnp.bfloat16
SDS = jax.ShapeDtypeStruct
MESH = pl.DeviceIdType.MESH

D_MODEL = 1024
DEPTH = 2
N_MEM = 256
MEM_HEADS = 4
MEM_HEAD_DIM = D_MODEL // MEM_HEADS
POOL_WINDOWS = (2, 4, 8, 16)
POOL_WIDTH = 256
POOL_GROUP = 64
QK_NOPE = 128
QK_ROPE = 64
V_HEAD = 128
MLA_HEADS = 6
Q_LORA = 256
KV_LORA = 128
ROPE_BASE = 10000.0
D_FF = 2816
D_IN = POOL_WIDTH + Q_LORA + KV_LORA + QK_ROPE
ALPHA = (2 * DEPTH) ** 0.25
LN_EPS = 1e-5
RMS_EPS = 1e-6
NEG_INF = -1e30
MLA_SCALE = (QK_NOPE + QK_ROPE) ** -0.5
MEM_SCALE = MEM_HEAD_DIM ** -0.5
ADAM_LR = 0.001
ADAM_B1 = 0.9
ADAM_B2 = 0.999
ADAM_EPS = 1e-08
ADAM_WD = 0.01
ADAM_STEP = 10

N_CHIPS = 4
V7X_VMEM_LIMIT = 56 * 2**20
HALO = 16

_NT = (((1,), (1,)), ((), ()))
_TN = (((0,), (0,)), ((), ()))


def _dot(a, b):
    return jnp.dot(a, b, preferred_element_type=f32)


def _dot_nt(a, b):
    return lax.dot_general(a, b, _NT, preferred_element_type=f32)


def _dot_tn(a, b):
    return lax.dot_general(a, b, _TN, preferred_element_type=f32)


def _cp(*sem):
    return pltpu.CompilerParams(dimension_semantics=sem if sem else None, vmem_limit_bytes=V7X_VMEM_LIMIT)


_DEP_SPEC = pl.BlockSpec(memory_space=pl.ANY)


def _with_deps(body, n_in, deps):
    nd = len(deps)
    if not nd:
        return body

    def wrapped(*refs):
        return body(*refs[:n_in], *refs[n_in + nd:])

    return wrapped


def _tile(n, t):
    t = min(n, t)
    assert n % t == 0, (n, t)
    return t


def _row_tile(rows, cols, itemsize=4, target=2 * 2**20):
    best = None
    for t in range(16, rows + 1, 16):
        if rows % t == 0 and t * cols * itemsize <= target:
            best = t
    return best if best is not None else rows


def ffn_up(name, xb, w13, l, deps=()):
    S = xb.shape[0]
    ns = w13.shape[3]
    tm = _tile(S, 512)

    def body(x_ref, wg_ref, wu_ref, g_ref, u_ref, a_ref):
        x = x_ref[...]
        g = _dot(x, wg_ref[0, 0])
        u = _dot(x, wu_ref[0, 0])
        a = g * jax.nn.sigmoid(g) * u
        g_ref[...] = g.astype(bf16)
        u_ref[...] = u.astype(bf16)
        a_ref[...] = a.astype(bf16)

    out = SDS((S, 2 * ns), bf16)
    return pl.pallas_call(
        _with_deps(body, 3, deps), name=name, grid=(2, S // tm),
        in_specs=[pl.BlockSpec((tm, D_MODEL), lambda j, i: (i, 0)),
                  pl.BlockSpec((1, 1, D_MODEL, ns), lambda j, i: (l, j, 0, 0)),
                  pl.BlockSpec((1, 1, D_MODEL, ns), lambda j, i: (l, j + 2, 0, 0))] + [_DEP_SPEC] * len(deps),
        out_specs=[pl.BlockSpec((tm, ns), lambda j, i: (i, j))] * 3,
        out_shape=[out, out, out],
        compiler_params=_cp("parallel", "parallel"),
    )(xb, w13, w13, *deps)


def proj_res_ln(name, parts, ws, wl, x, g, b, rscale, deps=()):
    S = x.shape[0]
    tm = _tile(S, 256)
    n = len(parts)

    def body(*refs):
        p_refs, w_refs = refs[:n], refs[n:2 * n]
        x_ref, g_ref, b_ref, z_ref, y_ref, yb_ref = refs[2 * n:]
        acc = _dot(p_refs[0][...], w_refs[0][0])
        for k in range(1, n):
            acc = acc + _dot(p_refs[k][...], w_refs[k][0])
        if rscale != 1.0:
            acc = rscale * acc
        z = ALPHA * x_ref[...] + acc
        mu = jnp.mean(z, axis=-1, keepdims=True)
        zc = z - mu
        var = jnp.mean(zc * zc, axis=-1, keepdims=True)
        y = zc * lax.rsqrt(var + LN_EPS) * g_ref[...] + b_ref[...]
        z_ref[...] = z
        y_ref[...] = y
        yb_ref[...] = y.astype(bf16)

    row = lambda i: (i, 0)
    in_specs = [pl.BlockSpec((tm, p.shape[1]), row) for p in parts]
    in_specs += [pl.BlockSpec((1,) + w.shape[1:], functools.partial(lambda li, i: (li, 0, 0), li)) for w, li in zip(ws, wl)]
    in_specs += [pl.BlockSpec((tm, D_MODEL), row), pl.BlockSpec((1, D_MODEL), lambda i: (0, 0)),
                 pl.BlockSpec((1, D_MODEL), lambda i: (0, 0))] + [_DEP_SPEC] * len(deps)
    return pl.pallas_call(
        _with_deps(body, 2 * n + 3, deps), name=name, grid=(S // tm,), in_specs=in_specs,
        out_specs=[pl.BlockSpec((tm, D_MODEL), row)] * 3,
        out_shape=[SDS((S, D_MODEL), f32), SDS((S, D_MODEL), f32), SDS((S, D_MODEL), bf16)],
        compiler_params=_cp("parallel"),
    )(*parts, *ws, x, g, b, *deps)


def ln_bwd(name, dy, z, g, rscale, deps=()):
    S = dy.shape[0]
    tm = _tile(S, 512)

    def body(dy_ref, z_ref, g_ref, dzb_ref, dres_ref, dg_ref, db_ref):
        z = z_ref[...]
        mu = jnp.mean(z, axis=-1, keepdims=True)
        zc = z - mu
        rstd = lax.rsqrt(jnp.mean(zc * zc, axis=-1, keepdims=True) + LN_EPS)
        xhat = zc * rstd
        dyv = dy_ref[...]
        dxh = dyv * g_ref[...]
        m1 = jnp.mean(dxh, axis=-1, keepdims=True)
        m2 = jnp.mean(dxh * xhat, axis=-1, keepdims=True)
        dz = rstd * (dxh - m1 - xhat * m2)
        dzb_ref[...] = (rscale * dz).astype(bf16)
        dres_ref[...] = ALPHA * dz

        @pl.when(pl.program_id(0) == 0)
        def _():
            dg_ref[...] = jnp.zeros_like(dg_ref)
            db_ref[...] = jnp.zeros_like(db_ref)

        dg_ref[...] += jnp.sum(dyv * xhat, axis=0, keepdims=True)
        db_ref[...] += jnp.sum(dyv, axis=0, keepdims=True)

    row = lambda i: (i, 0)
    vec = pl.BlockSpec((1, D_MODEL), lambda i: (0, 0))
    return pl.pallas_call(
        _with_deps(body, 3, deps), name=name, grid=(S // tm,),
        in_specs=[pl.BlockSpec((tm, D_MODEL), row), pl.BlockSpec((tm, D_MODEL), row), vec] + [_DEP_SPEC] * len(deps),
        out_specs=[pl.BlockSpec((tm, D_MODEL), row), pl.BlockSpec((tm, D_MODEL), row), vec, vec],
        out_shape=[SDS((S, D_MODEL), bf16), SDS((S, D_MODEL), f32), SDS((1, D_MODEL), f32), SDS((1, D_MODEL), f32)],
        compiler_params=_cp("arbitrary"),
    )(dy, z, g, *deps)


def ffn_bwd_da(name, drb, w2, l, gate, up):
    S = drb.shape[0]
    tm = _tile(S, 256)
    nh = D_FF // 2

    def body(dr_ref, w_ref, g_ref, u_ref, dh_ref):
        dr = dr_ref[...]
        for j in range(2):
            cols = slice(j * nh, (j + 1) * nh)
            da = _dot_nt(dr, w_ref[0, cols, :])
            g = g_ref[:, cols].astype(f32)
            u = u_ref[:, cols].astype(f32)
            sg = jax.nn.sigmoid(g)
            dh_ref[:, cols] = (da * u * (sg * (1.0 + g * (1.0 - sg)))).astype(bf16)
            dh_ref[:, D_FF + j * nh:D_FF + (j + 1) * nh] = (da * (g * sg)).astype(bf16)

    row = lambda i: (i, 0)
    return pl.pallas_call(
        body, name=name, grid=(S // tm,),
        in_specs=[pl.BlockSpec((tm, D_MODEL), row), pl.BlockSpec((1, D_FF, D_MODEL), lambda i: (l, 0, 0)),
                  pl.BlockSpec((tm, D_FF), row), pl.BlockSpec((tm, D_FF), row)],
        out_specs=pl.BlockSpec((tm, 2 * D_FF), row),
        out_shape=SDS((S, 2 * D_FF), bf16),
        compiler_params=_cp("parallel"),
    )(drb, w2, gate, up)


def ffn_dx(name, dh, w13, l, res):
    S = dh.shape[0]
    ns = w13.shape[3]
    tm = _tile(S, 1024)

    def body(dh_ref, w_ref, r_ref, o_ref):
        @pl.when(pl.program_id(1) == 0)
        def _():
            o_ref[...] = r_ref[...]

        o_ref[...] += _dot_nt(dh_ref[...], w_ref[0, 0])

    return pl.pallas_call(
        body, name=name, grid=(S // tm, N_CHIPS),
        in_specs=[pl.BlockSpec((tm, ns), lambda i, j: (i, j)),
                  pl.BlockSpec((1, 1, D_MODEL, ns), lambda i, j: (l, j, 0, 0)),
                  pl.BlockSpec((tm, D_MODEL), lambda i, j: (i, 0))],
        out_specs=pl.BlockSpec((tm, D_MODEL), lambda i, j: (i, 0)),
        out_shape=SDS((S, D_MODEL), f32),
        compiler_params=_cp("parallel", "arbitrary"),
    )(dh, w13, res)


def mm_nt_res(name, dys, ws, wl, res, out_dtype):
    S = dys[0].shape[0]
    K = ws[0].shape[1]
    tm = _tile(S, 512)
    n = len(dys)

    def body(*refs):
        dy_refs, w_refs = refs[:n], refs[n:2 * n]
        o_ref = refs[-1]
        acc = _dot_nt(dy_refs[0][...], w_refs[0][0])
        for k in range(1, n):
            acc = acc + _dot_nt(dy_refs[k][...], w_refs[k][0])
        if res is not None:
            acc = acc + refs[2 * n][...]
        o_ref[...] = acc.astype(out_dtype)

    row = lambda i: (i, 0)
    in_specs = [pl.BlockSpec((tm, d.shape[1]), row) for d in dys]
    in_specs += [pl.BlockSpec((1,) + w.shape[1:], functools.partial(lambda li, i: (li, 0, 0), li)) for w, li in zip(ws, wl)]
    args = list(dys) + list(ws)
    if res is not None:
        in_specs.append(pl.BlockSpec((tm, K), row))
        args.append(res)
    return pl.pallas_call(
        body, name=name, grid=(S // tm,), in_specs=in_specs,
        out_specs=pl.BlockSpec((tm, K), row), out_shape=SDS((S, K), out_dtype),
        compiler_params=_cp("parallel"),
    )(*args)


def mm_tn(name, x, dy, col_shards=False, deps=()):
    S, K = x.shape
    N = dy.shape[1]
    ts = _tile(S, 512)
    if col_shards:
        tn = N // N_CHIPS
    else:
        tn = N
        while K * tn * 4 > 6 * 2**20 and tn % 256 == 0:
            tn //= 2
    nn = N // tn
    lead = ((0,) if col_shards else ()) + (slice(None), slice(None))

    def body(x_ref, dy_ref, o_ref):
        acc = _dot_tn(x_ref[...].astype(bf16), dy_ref[...].astype(bf16))

        @pl.when(pl.program_id(1) == 0)
        def _():
            o_ref[lead] = acc

        @pl.when(pl.program_id(1) != 0)
        def _():
            o_ref[lead] += acc

    if col_shards:
        out_spec = pl.BlockSpec((1, K, tn), lambda n, s: (n, 0, 0))
        out_shape = SDS((N_CHIPS, K, tn), f32)
    else:
        out_spec = pl.BlockSpec((K, tn), lambda n, s: (0, n))
        out_shape = SDS((K, N), f32)
    return pl.pallas_call(
        _with_deps(body, 2, deps), name=name, grid=(nn, S // ts),
        in_specs=[pl.BlockSpec((ts, K), lambda n, s: (s, 0)), pl.BlockSpec((ts, tn), lambda n, s: (s, n))]
        + [_DEP_SPEC] * len(deps),
        out_specs=out_spec, out_shape=out_shape, compiler_params=_cp("parallel", "arbitrary"),
    )(x, dy, *deps)


def mm_nn_shard(name, x, w, l):
    S, K = x.shape
    ns = w.shape[3]

    def body(x_ref, w_ref, o_ref):
        o_ref[...] = _dot(x_ref[...], w_ref[0, 0]).astype(bf16)

    return pl.pallas_call(
        body, name=name, grid=(N_CHIPS,),
        in_specs=[pl.BlockSpec((S, K), lambda j: (0, 0)), pl.BlockSpec((1, 1, K, ns), lambda j: (l, j, 0, 0))],
        out_specs=pl.BlockSpec((S, ns), lambda j: (0, j)), out_shape=SDS((S, N_CHIPS * ns), bf16),
        compiler_params=_cp("parallel"),
    )(x, w)


def loss_grad(name, y, t):
    S = y.shape[0]
    tm = _tile(S, 512)

    def body(y_ref, t_ref, dy_ref, loss_ref):
        e = y_ref[...] - t_ref[...]
        dy_ref[...] = e * (1.0 / D_MODEL)

        @pl.when(pl.program_id(0) == 0)
        def _():
            loss_ref[...] = jnp.zeros_like(loss_ref)

        loss_ref[...] += jnp.full(loss_ref.shape, (0.5 / D_MODEL) * jnp.sum(e * e), f32)

    row = lambda i: (i, 0)
    return pl.pallas_call(
        body, name=name, grid=(S // tm,),
        in_specs=[pl.BlockSpec((tm, D_MODEL), row)] * 2,
        out_specs=[pl.BlockSpec((tm, D_MODEL), row), pl.BlockSpec((8, 128), lambda i: (0, 0))],
        out_shape=[SDS((S, D_MODEL), f32), SDS((8, 128), f32)],
        compiler_params=_cp("arbitrary"),
    )(y, t)


def _half_sum(t):
    return t + pltpu.roll(t, 64, axis=1)


def mix_pre(name, xb, w_in, wq, wkv, l, gq, gkv, cs):
    S = xb.shape[0]
    tm = _tile(S, 256)
    H = MLA_HEADS
    W_EXT = w_in.shape[2]

    def body(x_ref, win_ref, wq_ref, wkv_ref, gq_ref, gkv_ref, cs_ref,
             u_ref, cq_ref, ckv_ref, cqn_ref, ckvn_ref, q_ref, k_ref, v_ref):
        h = _dot(x_ref[...], win_ref[0])
        u_ref[...] = h[:, :256]
        cq = h[:, 256:512]
        ckv = h[:, 512:640]
        cq_ref[...] = cq
        ckv_ref[...] = ckv
        cqn = (cq * lax.rsqrt(jnp.mean(cq * cq, axis=-1, keepdims=True) + RMS_EPS) * gq_ref[...]).astype(bf16)
        ckvn = (ckv * lax.rsqrt(jnp.mean(ckv * ckv, axis=-1, keepdims=True) + RMS_EPS) * gkv_ref[...]).astype(bf16)
        cqn_ref[...] = cqn
        ckvn_ref[...] = ckvn
        csv = cs_ref[...]
        lane = lax.broadcasted_iota(jnp.int32, (tm, 128), 1)
        kr = jnp.where(lane < 64, _half_sum(h[:, 640:768] * csv), 0.0).astype(bf16)
        kv = _dot(ckvn, wkv_ref[0])
        for hd in range(H):
            qe = _dot(cqn, wq_ref[0, hd])
            q_ref[hd, :, :128] = qe[:, :128].astype(bf16)
            q_ref[hd, :, 128:] = _half_sum(qe[:, 128:] * csv).astype(bf16)
            k_ref[hd, :, :128] = kv[:, 256 * hd:256 * hd + 128].astype(bf16)
            k_ref[hd, :, 128:] = kr
            v_ref[hd] = kv[:, 256 * hd + 128:256 * hd + 256].astype(bf16)

    row = lambda i: (i, 0)
    hrow = lambda i: (0, i, 0)
    return pl.pallas_call(
        body, name=name, grid=(S // tm,),
        in_specs=[pl.BlockSpec((tm, D_MODEL), row),
                  pl.BlockSpec((1, D_MODEL, W_EXT), lambda i: (l, 0, 0)),
                  pl.BlockSpec((1, H, Q_LORA, 256), lambda i: (l, 0, 0, 0)),
                  pl.BlockSpec((1, KV_LORA, H * 256), lambda i: (l, 0, 0)),
                  pl.BlockSpec((1, Q_LORA), lambda i: (0, 0)), pl.BlockSpec((1, KV_LORA), lambda i: (0, 0)),
                  pl.BlockSpec((tm, 128), row)],
        out_specs=[pl.BlockSpec((tm, 256), row), pl.BlockSpec((tm, Q_LORA), row), pl.BlockSpec((tm, KV_LORA), row),
                   pl.BlockSpec((tm, Q_LORA), row), pl.BlockSpec((tm, KV_LORA), row),
                   pl.BlockSpec((H, tm, 256), hrow), pl.BlockSpec((H, tm, 256), hrow), pl.BlockSpec((H, tm, 128), hrow)],
        out_shape=[SDS((S, 256), f32), SDS((S, Q_LORA), f32), SDS((S, KV_LORA), f32),
                   SDS((S, Q_LORA), bf16), SDS((S, KV_LORA), bf16),
                   SDS((H, S, 256), bf16), SDS((H, S, 256), bf16), SDS((H, S, 128), bf16)],
        compiler_params=_cp("parallel"),
    )(xb, w_in, wq, wkv, gq, gkv, cs)


def _group_select(col, a2, a4, a8, a16):
    return jnp.where(col < 64, a2, jnp.where(col < 128, a4, jnp.where(col < 192, a8, a16)))


def pool_fwd(name, u, wbd, scale):
    S = u.shape[0]
    tm = _tile(S, 512)
    hb = tm // HALO

    def body(u_ref, halo_ref, w_ref, s_ref, d_ref, y_ref):
        i = pl.program_id(0)
        cur = u_ref[...]
        halo = jnp.where(i > 0, halo_ref[...], 0.0)
        ext = jnp.concatenate([halo, cur], axis=0)
        s2 = ext + pltpu.roll(ext, 1, axis=0)
        s4 = s2 + pltpu.roll(s2, 2, axis=0)
        s8 = s4 + pltpu.roll(s4, 4, axis=0)
        s16 = s8 + pltpu.roll(s8, 8, axis=0)
        t1 = (i * tm + 1 + lax.broadcasted_iota(jnp.int32, (tm, 1), 0)).astype(f32)
        col = lax.broadcasted_iota(jnp.int32, (tm, 256), 1)
        m = _group_select(col, s2[HALO:] / jnp.minimum(t1, 2.0), s4[HALO:] / jnp.minimum(t1, 4.0),
                          s8[HALO:] / jnp.minimum(t1, 8.0), s16[HALO:] / jnp.minimum(t1, 16.0))
        d = (m - cur).astype(bf16)
        d_ref[...] = d
        y_ref[...] = (_dot(d, w_ref[...]) * s_ref[...]).astype(bf16)

    row = lambda i: (i, 0)
    return pl.pallas_call(
        body, name=name, grid=(S // tm,),
        in_specs=[pl.BlockSpec((tm, 256), row), pl.BlockSpec((HALO, 256), lambda i: (jnp.maximum(i * hb - 1, 0), 0)),
                  pl.BlockSpec((256, 256), lambda i: (0, 0)), pl.BlockSpec((1, 256), lambda i: (0, 0))],
        out_specs=[pl.BlockSpec((tm, 256), row)] * 2,
        out_shape=[SDS((S, 256), bf16), SDS((S, 256), bf16)],
        compiler_params=_cp("parallel"),
    )(u, u, wbd, scale)


def pool_bwd(name, dyp, d, wbd, scale):
    S = dyp.shape[0]
    tm = _tile(S, 512)
    hb = tm // HALO
    n_ext = tm + HALO

    def fwd_sum(e, steps):
        k = 1
        for _ in range(steps):
            e = e + pltpu.roll(e, n_ext - k, axis=0)
            k *= 2
        return e

    def body(dy_ref, halo_ref, d_ref, w_ref, s_ref, du_ref, dyw_ref, ds_ref):
        i = pl.program_id(0)
        sc = s_ref[...]
        w = w_ref[...]
        cur = dy_ref[...].astype(f32)
        halo = jnp.where(i < pl.num_programs(0) - 1, halo_ref[...].astype(f32), 0.0)
        dyw = jnp.concatenate([cur, halo], axis=0) * sc
        dyw_ref[...] = dyw[:tm].astype(bf16)
        dd = _dot_nt(dyw.astype(bf16), w)
        t1 = (i * tm + 1 + lax.broadcasted_iota(jnp.int32, (n_ext, 1), 0)).astype(f32)
        f2 = fwd_sum(dd / jnp.minimum(t1, 2.0), 1)
        f4 = fwd_sum(dd / jnp.minimum(t1, 4.0), 2)
        f8 = fwd_sum(dd / jnp.minimum(t1, 8.0), 3)
        f16 = fwd_sum(dd / jnp.minimum(t1, 16.0), 4)
        col = lax.broadcasted_iota(jnp.int32, (tm, 256), 1)
        du_ref[...] = (_group_select(col, f2[:tm], f4[:tm], f8[:tm], f16[:tm]) - dd[:tm]).astype(bf16)

        @pl.when(i == 0)
        def _():
            ds_ref[...] = jnp.zeros_like(ds_ref)

        ds_ref[...] += jnp.sum(cur * _dot(d_ref[...], w), axis=0, keepdims=True)

    row = lambda i: (i, 0)
    nhb = S // HALO
    return pl.pallas_call(
        body, name=name, grid=(S // tm,),
        in_specs=[pl.BlockSpec((tm, 256), row), pl.BlockSpec((HALO, 256), lambda i: (jnp.minimum((i + 1) * hb, nhb - 1), 0)),
                  pl.BlockSpec((tm, 256), row), pl.BlockSpec((256, 256), lambda i: (0, 0)),
                  pl.BlockSpec((1, 256), lambda i: (0, 0))],
        out_specs=[pl.BlockSpec((tm, 256), row), pl.BlockSpec((tm, 256), row), pl.BlockSpec((1, 256), lambda i: (0, 0))],
        out_shape=[SDS((S, 256), bf16), SDS((S, 256), bf16), SDS((1, 256), f32)],
        compiler_params=_cp("arbitrary"),
    )(dyp, dyp, d, wbd, scale)


def _diag_mask(tq):
    rc = lax.broadcasted_iota(jnp.int32, (tq, 1), 0) // 64
    cc = lax.broadcasted_iota(jnp.int32, (1, tq), 1) // 64
    return rc >= cc


MLA_SCALE_LOG2 = MLA_SCALE * math.log2(math.e)


def mla_attn_fwd(name, q, k, v):
    H, S, _ = q.shape
    tq = _tile(S, 1024)
    nq = S // tq
    pairs = [(i, j) for i in range(nq) for j in range(i + 1)]
    it = jnp.asarray([p_[0] for p_ in pairs], jnp.int32)
    jt = jnp.asarray([p_[1] for p_ in pairs], jnp.int32)

    def body(it_ref, jt_ref, q_ref, k_ref, v_ref, o_ref, lse_ref, m_sc, l_sc, acc_sc):
        t = pl.program_id(1)
        i, j = it_ref[t], jt_ref[t]

        @pl.when(j == 0)
        def _():
            m_sc[...] = jnp.full_like(m_sc, NEG_INF)
            l_sc[...] = jnp.zeros_like(l_sc)
            acc_sc[...] = jnp.zeros_like(acc_sc)

        def step(masked):
            s = _dot_nt(q_ref[0], k_ref[0])
            if masked:
                s = jnp.where(_diag_mask(tq), s, NEG_INF)
            m_prev = m_sc[...]
            m_new = jnp.maximum(m_prev, jnp.max(s, axis=-1, keepdims=True))
            p = jnp.exp2((s - jnp.tile(m_new, (1, tq // 128))) * MLA_SCALE_LOG2)
            a = jnp.exp2((m_prev - m_new) * MLA_SCALE_LOG2)
            l_sc[...] = a * l_sc[...] + jnp.sum(p, axis=-1, keepdims=True)
            acc_sc[...] = a * acc_sc[...] + _dot(p.astype(bf16), v_ref[0])
            m_sc[...] = m_new

        @pl.when(j < i)
        def _():
            step(False)

        @pl.when(j == i)
        def _():
            step(True)
            o_ref[...] = (acc_sc[...] / l_sc[...]).astype(bf16)
            lse_ref[0] = m_sc[...] * MLA_SCALE_LOG2 + jnp.log2(l_sc[...])

    return pl.pallas_call(
        body, name=name,
        grid_spec=pltpu.PrefetchScalarGridSpec(
            num_scalar_prefetch=2, grid=(H, len(pairs)),
            in_specs=[pl.BlockSpec((1, tq, 256), lambda h, t, it_, jt_: (h, it_[t], 0)),
                      pl.BlockSpec((1, tq, 256), lambda h, t, it_, jt_: (h, jt_[t], 0)),
                      pl.BlockSpec((1, tq, 128), lambda h, t, it_, jt_: (h, jt_[t], 0))],
            out_specs=[pl.BlockSpec((tq, 128), lambda h, t, it_, jt_: (it_[t], h)),
                       pl.BlockSpec((1, tq, 128), lambda h, t, it_, jt_: (h, it_[t], 0))],
            scratch_shapes=[pltpu.VMEM((tq, 128), f32), pltpu.VMEM((tq, 128), f32), pltpu.VMEM((tq, 128), f32)]),
        out_shape=[SDS((S, H * 128), bf16), SDS((H, S, 128), f32)],
        compiler_params=_cp("parallel", "arbitrary"),
    )(it, jt, q, k, v)


def mla_attn_bwd(name, q, k, v, o, do, lse):
    H, S, _ = q.shape
    tq = _tile(S, 1024)
    nq = S // tq
    pairs = [(i, j) for j in range(nq) for i in range(j, nq)]
    it = jnp.asarray([p_[0] for p_ in pairs], jnp.int32)
    jt = jnp.asarray([p_[1] for p_ in pairs], jnp.int32)
    n_pairs = len(pairs)

    def body(it_ref, jt_ref, q_ref, k_ref, v_ref, o_ref, do_ref, lse_ref, dq_ref, dk_ref, dv_ref, dk_sc, dv_sc):
        t = pl.program_id(1)
        i, j = it_ref[t], jt_ref[t]

        @pl.when(t == 0)
        def _():
            dq_ref[...] = jnp.zeros_like(dq_ref)

        @pl.when(i == j)
        def _():
            dk_sc[...] = jnp.zeros_like(dk_sc)
            dv_sc[...] = jnp.zeros_like(dv_sc)

        def step(masked):
            qv, kv_, dov = q_ref[0], k_ref[0], do_ref[...]
            s = _dot_nt(qv, kv_)
            if masked:
                s = jnp.where(_diag_mask(tq), s, NEG_INF)
            p = jnp.exp2(s * MLA_SCALE_LOG2 - jnp.tile(lse_ref[0], (1, tq // 128)))
            dv_sc[...] += _dot_tn(p.astype(bf16), dov)
            dp = _dot_nt(dov, v_ref[0])
            delta = jnp.sum(dov.astype(f32) * o_ref[...].astype(f32), axis=-1, keepdims=True)
            ds = (p * (dp - delta)).astype(bf16)
            dk_sc[...] += _dot_tn(ds, qv)
            rows = pl.ds(pl.multiple_of(i * tq, tq), tq)
            dq_ref[0, rows, :] += _dot(ds, kv_)

        @pl.when(i > j)
        def _():
            step(False)

        @pl.when(i == j)
        def _():
            step(True)

        @pl.when(i == nq - 1)
        def _():
            dk_ref[0] = dk_sc[...] * MLA_SCALE
            dv_ref[0] = dv_sc[...]

        @pl.when(t == n_pairs - 1)
        def _():
            dq_ref[...] = dq_ref[...] * MLA_SCALE

    qi = lambda h, t, it_, jt_: (h, it_[t], 0)
    kj = lambda h, t, it_, jt_: (h, jt_[t], 0)
    oi = lambda h, t, it_, jt_: (it_[t], h)
    return pl.pallas_call(
        body, name=name,
        grid_spec=pltpu.PrefetchScalarGridSpec(
            num_scalar_prefetch=2, grid=(H, n_pairs),
            in_specs=[pl.BlockSpec((1, tq, 256), qi), pl.BlockSpec((1, tq, 256), kj), pl.BlockSpec((1, tq, 128), kj),
                      pl.BlockSpec((tq, 128), oi), pl.BlockSpec((tq, 128), oi), pl.BlockSpec((1, tq, 128), qi)],
            out_specs=[pl.BlockSpec((1, S, 256), lambda h, t, it_, jt_: (h, 0, 0)), pl.BlockSpec((1, tq, 256), kj),
                       pl.BlockSpec((1, tq, 128), kj)],
            scratch_shapes=[pltpu.VMEM((tq, 256), f32), pltpu.VMEM((tq, 128), f32)]),
        out_shape=[SDS((H, S, 256), f32), SDS((H, S, 256), f32), SDS((H, S, 128), f32)],
        compiler_params=_cp("parallel", "arbitrary"),
    )(it, jt, q, k, v, o, do, lse)


def mix_post_bwd(name, dq, dk, dv, wq, wkv, l, cq, ckv, gq, gkv, cs):
    H, S, _ = dq.shape
    tm = _tile(S, 256)

    def rms_bwd(dyn, c, g):
        r = lax.rsqrt(jnp.mean(c * c, axis=-1, keepdims=True) + RMS_EPS)
        ch = c * r
        dyg = dyn * g
        dc = r * (dyg - ch * jnp.mean(dyg * ch, axis=-1, keepdims=True))
        return dc, jnp.sum(dyn * ch, axis=0, keepdims=True)

    def body(dq_ref, dk_ref, dv_ref, wq_ref, wkv_ref, cq_ref, ckv_ref, gq_ref, gkv_ref, cs_ref,
             dqe_ref, dkv_ref, dh_ref, dgq_ref, dgkv_ref):
        csv = cs_ref[...]
        lane = lax.broadcasted_iota(jnp.int32, (tm, 128), 1)
        dcqn = jnp.zeros((tm, Q_LORA), f32)
        dkr = jnp.zeros((tm, 128), f32)
        for hd in range(H):
            dqh = dq_ref[hd]
            dqe = jnp.concatenate([dqh[:, :128], _half_sum(dqh[:, 128:]) * csv], axis=1).astype(bf16)
            dqe_ref[:, 256 * hd:256 * hd + 256] = dqe
            dcqn = dcqn + _dot_nt(dqe, wq_ref[0, hd])
            dkh = dk_ref[hd]
            dkv_ref[:, 256 * hd:256 * hd + 128] = dkh[:, :128].astype(bf16)
            dkv_ref[:, 256 * hd + 128:256 * hd + 256] = dv_ref[hd].astype(bf16)
            dkr = dkr + dkh[:, 128:]
        dckvn = _dot_nt(dkv_ref[...], wkv_ref[0])
        dblk = _half_sum(jnp.where(lane < 64, dkr, 0.0)) * csv
        dcq, dgq = rms_bwd(dcqn, cq_ref[...], gq_ref[...])
        dckv, dgkv = rms_bwd(dckvn, ckv_ref[...], gkv_ref[...])
        dh_ref[:, :256] = dcq.astype(bf16)
        dh_ref[:, 256:384] = dckv.astype(bf16)
        dh_ref[:, 384:] = dblk.astype(bf16)

        @pl.when(pl.program_id(0) == 0)
        def _():
            dgq_ref[...] = jnp.zeros_like(dgq_ref)
            dgkv_ref[...] = jnp.zeros_like(dgkv_ref)

        dgq_ref[...] += dgq
        dgkv_ref[...] += dgkv

    row = lambda i: (i, 0)
    hrow = lambda i: (0, i, 0)
    return pl.pallas_call(
        body, name=name, grid=(S // tm,),
        in_specs=[pl.BlockSpec((H, tm, 256), hrow), pl.BlockSpec((H, tm, 256), hrow), pl.BlockSpec((H, tm, 128), hrow),
                  pl.BlockSpec((1, H, Q_LORA, 256), lambda i: (l, 0, 0, 0)),
                  pl.BlockSpec((1, KV_LORA, H * 256), lambda i: (l, 0, 0)),
                  pl.BlockSpec((tm, Q_LORA), row), pl.BlockSpec((tm, KV_LORA), row),
                  pl.BlockSpec((1, Q_LORA), lambda i: (0, 0)), pl.BlockSpec((1, KV_LORA), lambda i: (0, 0)),
                  pl.BlockSpec((tm, 128), row)],
        out_specs=[pl.BlockSpec((tm, H * 256), row), pl.BlockSpec((tm, H * 256), row), pl.BlockSpec((tm, 512), row),
                   pl.BlockSpec((1, Q_LORA), lambda i: (0, 0)), pl.BlockSpec((1, KV_LORA), lambda i: (0, 0))],
        out_shape=[SDS((S, H * 256), bf16), SDS((S, H * 256), bf16), SDS((S, 512), bf16),
                   SDS((1, Q_LORA), f32), SDS((1, KV_LORA), f32)],
        compiler_params=_cp("arbitrary"),
    )(dq, dk, dv, wq, wkv, cq, ckv, gq, gkv, cs)


def _cross_probs(qb, kv_ref, hd):
    cols = slice(hd * MEM_HEAD_DIM, (hd + 1) * MEM_HEAD_DIM)
    s = _dot_nt(qb[:, cols], kv_ref[:, cols]) * MEM_SCALE
    e = jnp.exp(s - jnp.max(s, axis=-1, keepdims=True))
    return e / jnp.sum(e, axis=-1, keepdims=True)


def cross_fwd(name, xb, xf, wq, wo, l, kv, g, b):
    S = xb.shape[0]
    tm = _tile(S, 256)
    M = kv.shape[0]

    def body(x_ref, xf_ref, wq_ref, wo_ref, k_ref, v_ref, g_ref, b_ref, q_ref, o_ref, z_ref, y_ref, yb_ref):
        qb = _dot(x_ref[...], wq_ref[0]).astype(bf16)
        q_ref[...] = qb
        for hd in range(MEM_HEADS):
            cols = slice(hd * MEM_HEAD_DIM, (hd + 1) * MEM_HEAD_DIM)
            p = _cross_probs(qb, k_ref, hd)
            o_ref[:, cols] = _dot(p.astype(bf16), v_ref[:, cols]).astype(bf16)
        z = ALPHA * xf_ref[...] + _dot(o_ref[...], wo_ref[0])
        mu = jnp.mean(z, axis=-1, keepdims=True)
        zc = z - mu
        var = jnp.mean(zc * zc, axis=-1, keepdims=True)
        y = zc * lax.rsqrt(var + LN_EPS) * g_ref[...] + b_ref[...]
        z_ref[...] = z
        y_ref[...] = y
        yb_ref[...] = y.astype(bf16)

    row = lambda i: (i, 0)
    wspec = pl.BlockSpec((1, D_MODEL, D_MODEL), lambda i: (l, 0, 0))
    vec = pl.BlockSpec((1, D_MODEL), lambda i: (0, 0))
    blk = pl.BlockSpec((tm, D_MODEL), row)
    return pl.pallas_call(
        body, name=name, grid=(S // tm,),
        in_specs=[blk, blk, wspec, wspec, pl.BlockSpec((M, D_MODEL), lambda i: (0, 0)),
                  pl.BlockSpec((M, D_MODEL), lambda i: (0, 1)), vec, vec],
        out_specs=[blk] * 5,
        out_shape=[SDS((S, D_MODEL), bf16), SDS((S, D_MODEL), bf16), SDS((S, D_MODEL), f32), SDS((S, D_MODEL), f32),
                   SDS((S, D_MODEL), bf16)],
        compiler_params=_cp("parallel"),
    )(xb, xf, wq, wo, kv, kv, g, b)


def cross_bwd(name, dzb, wo, l, qb, kv):
    S = dzb.shape[0]
    tm = _tile(S, 256)
    M = kv.shape[0]

    def body(dz_ref, wo_ref, q_ref, k_ref, v_ref, dq_ref, dkv_ref):
        @pl.when(pl.program_id(0) == 0)
        def _():
            dkv_ref[...] = jnp.zeros_like(dkv_ref)

        do = _dot_nt(dz_ref[...], wo_ref[0]).astype(bf16)
        qv = q_ref[...]
        for hd in range(MEM_HEADS):
            cols = slice(hd * MEM_HEAD_DIM, (hd + 1) * MEM_HEAD_DIM)
            vcols = slice(D_MODEL + hd * MEM_HEAD_DIM, D_MODEL + (hd + 1) * MEM_HEAD_DIM)
            p = _cross_probs(qv, k_ref, hd)
            doh = do[:, cols]
            dkv_ref[:, vcols] += _dot_tn(p.astype(bf16), doh)
            dp = _dot_nt(doh, v_ref[:, cols])
            ds = (p * (dp - jnp.sum(dp * p, axis=-1, keepdims=True)) * MEM_SCALE).astype(bf16)
            dq_ref[:, cols] = _dot(ds, k_ref[:, cols]).astype(bf16)
            dkv_ref[:, cols] += _dot_tn(ds, qv[:, cols])

    row = lambda i: (i, 0)
    blk = pl.BlockSpec((tm, D_MODEL), row)
    return pl.pallas_call(
        body, name=name, grid=(S // tm,),
        in_specs=[blk, pl.BlockSpec((1, D_MODEL, D_MODEL), lambda i: (l, 0, 0)), blk,
                  pl.BlockSpec((M, D_MODEL), lambda i: (0, 0)), pl.BlockSpec((M, D_MODEL), lambda i: (0, 1))],
        out_specs=[blk, pl.BlockSpec((M, 2 * D_MODEL), lambda i: (0, 0))],
        out_shape=[SDS((S, D_MODEL), bf16), SDS((M, 2 * D_MODEL), f32)],
        compiler_params=_cp("arbitrary"),
    )(dzb, wo, qb, kv, kv)


def adamw(name, w, g, m, v):
    shape = w.shape
    cols = shape[-1]
    rows = math.prod(shape[:-1])
    tr = _row_tile(rows, cols, target=2**20)
    c1 = 1.0 - ADAM_B1 ** ADAM_STEP
    c2 = 1.0 - ADAM_B2 ** ADAM_STEP

    def body(w_ref, g_ref, m_ref, v_ref, d_ref, nm_ref, nv_ref):
        gv = g_ref[...]
        nm = ADAM_B1 * m_ref[...] + (1.0 - ADAM_B1) * gv
        nv = ADAM_B2 * v_ref[...] + (1.0 - ADAM_B2) * (gv * gv)
        d_ref[...] = -ADAM_LR * ((nm / c1) / (jnp.sqrt(nv / c2) + ADAM_EPS) + ADAM_WD * w_ref[...])
        nm_ref[...] = nm
        nv_ref[...] = nv

    blk = pl.BlockSpec((tr, cols), lambda i: (i, 0))
    flat = SDS((rows, cols), f32)
    outs = pl.pallas_call(
        body, name=name, grid=(rows // tr,), in_specs=[blk] * 4, out_specs=[blk] * 3, out_shape=[flat] * 3,
        compiler_params=_cp("parallel"),
    )(*[a.reshape(rows, cols) for a in (w, g, m, v)])
    return [o.reshape(shape) for o in outs]


def _me():
    return lax.axis_index("x"), lax.axis_index("y"), lax.axis_index("c")


def _other_chips(x, y):
    return [(1 - x, y), (x, 1 - y), (1 - x, 1 - y)]


def pair_share(name, sums):
    n = len(sums)

    def body(*refs):
        out_refs = refs[n:2 * n]
        send_sems, recv_sems = refs[2 * n:]
        x, y, c = _me()
        sibling = (x, y, 1 - c)
        cps = [pltpu.make_async_remote_copy(src_ref=out_refs[a].at[c], dst_ref=out_refs[a].at[c], send_sem=send_sems.at[a],
                                            recv_sem=recv_sems.at[a], device_id=sibling, device_id_type=MESH)
               for a in range(n)]
        for cp in cps:
            cp.start()
        for cp in cps:
            cp.wait_send()
        for a in range(n):
            land = out_refs[a].at[1 - c]
            pltpu.make_async_remote_copy(src_ref=land, dst_ref=land, send_sem=send_sems.at[a], recv_sem=recv_sems.at[a],
                                         device_id=sibling, device_id_type=MESH).wait_recv()

    any_spec = pl.BlockSpec(memory_space=pl.ANY)
    return pl.pallas_call(
        body, name=name, in_specs=[any_spec] * n, out_specs=[any_spec] * n,
        out_shape=[SDS(s.shape, f32) for s in sums], input_output_aliases={a: a for a in range(n)},
        scratch_shapes=[pltpu.SemaphoreType.DMA((n,)), pltpu.SemaphoreType.DMA((n,))],
    )(*sums)


def allsum_small(name, v, deps=()):
    R = v.shape[0]

    def body(v_ref, o_ref, all_ref, send_sems, recv_sems, local_sem):
        x, y, c = _me()
        me, sibling = (x, y, c), (x, y, 1 - c)
        chips = _other_chips(x, y)

        def rows(px, py, pc):
            return all_ref.at[4 * px + 2 * py + pc]

        def copy(k, block, to, src=None):
            return pltpu.make_async_remote_copy(
                src_ref=rows(*block) if src is None else src, dst_ref=rows(*block),
                send_sem=send_sems.at[k], recv_sem=recv_sems.at[k], device_id=to, device_id_type=MESH)

        mine = pltpu.make_async_copy(v_ref, rows(*me), local_sem)
        mine.start()
        first = [copy(0, me, sibling, src=v_ref)]
        first += [copy(1 + j, me, (*chip, c), src=v_ref) for j, chip in enumerate(chips)]
        for cp in first:
            cp.start()
        passed = [copy(4 + j, (*chip, c), sibling) for j, chip in enumerate(chips)]
        for j, chip in enumerate(chips):
            copy(1 + j, (*chip, c), me).wait_recv()
            passed[j].start()
        copy(0, sibling, me).wait_recv()
        for j, chip in enumerate(chips):
            copy(4 + j, (*chip, 1 - c), me).wait_recv()
        for cp in first + passed:
            cp.wait_send()
        mine.wait()
        acc = all_ref[0]
        for d in range(1, 8):
            acc = acc + all_ref[d]
        o_ref[...] = acc

    return pl.pallas_call(
        _with_deps(body, 1, deps), name=name,
        in_specs=[pl.BlockSpec(memory_space=pltpu.VMEM)] + [_DEP_SPEC] * len(deps),
        out_specs=pl.BlockSpec(memory_space=pltpu.VMEM),
        out_shape=SDS((R, 128), f32),
        scratch_shapes=[pltpu.VMEM((8, R, 128), f32), pltpu.SemaphoreType.DMA((7,)), pltpu.SemaphoreType.DMA((7,)),
                        pltpu.SemaphoreType.DMA],
        compiler_params=pltpu.CompilerParams(vmem_limit_bytes=V7X_VMEM_LIMIT),
    )(v, *deps)


def _swap_half(r):
    return jnp.concatenate([-r[..., 32:], r[..., :32]], axis=-1)


def _unswap_add(p, qg):
    return p + jnp.concatenate([qg[..., 32:], -qg[..., :32]], axis=-1)


def _block_diag(pw):
    L = pw.shape[0]
    out = jnp.zeros((L, 256, 256), pw.dtype)
    for gi in range(4):
        out = out.at[:, 64 * gi:64 * gi + 64, 64 * gi:64 * gi + 64].set(pw[:, gi])
    return out


def _to_col_shards(w):
    *lead, K, N = w.shape
    nl = len(lead)
    return w.reshape(*lead, K, N_CHIPS, N // N_CHIPS).transpose(*range(nl), nl + 1, nl, nl + 2)


def _from_col_shards(w):
    *lead, C, K, n = w.shape
    nl = len(lead)
    return w.transpose(*range(nl), nl + 1, nl, nl + 2).reshape(*lead, K, C * n)


def _step_serial_comm(x, mem, positions, ln_g, ln_b, ffn1_w13, ffn1_w2, w_in, pool_w, pool_scale, q_norm_g, w_uq, kv_norm_g, w_ukv, w_out, mem_wq, mem_wkv, mem_wo, ffn2_w13, ffn2_w2, loss_target, m_ln_g, m_ln_b, m_ffn1_w13, m_ffn1_w2, m_w_in, m_pool_w, m_pool_scale, m_q_norm_g, m_w_uq, m_kv_norm_g, m_w_ukv, m_w_out, m_mem_wq, m_mem_wkv, m_mem_wo, m_ffn2_w13, m_ffn2_w2, v_ln_g, v_ln_b, v_ffn1_w13, v_ffn1_w2, v_w_in, v_pool_w, v_pool_scale, v_q_norm_g, v_w_uq, v_kv_norm_g, v_w_ukv, v_w_out, v_mem_wq, v_mem_wkv, v_mem_wo, v_ffn2_w13, v_ffn2_w2):
    L = DEPTH
    S = x.shape[1]
    qx, qy, _ = _me()
    chip = 2 * qx + qy

    big = [ffn1_w13, ffn1_w2, w_in, w_uq, w_ukv, w_out, mem_wq, mem_wkv, mem_wo, ffn2_w13, ffn2_w2]
    (g_f1w13, g_f1w2, g_win, g_wuq, g_wukv, g_wout, g_mwq, g_mwkv, g_mwo, g_f2w13, g_f2w2) = gather_weights(
        [w.astype(bf16) for w in big])
    f1w2 = g_f1w2.reshape(L, D_FF, D_MODEL)
    f2w2 = g_f2w2.reshape(L, D_FF, D_MODEL)
    win = g_win.reshape(L, D_MODEL, D_IN)
    win_ext = jnp.concatenate([win, _swap_half(win[..., D_IN - QK_ROPE:])], axis=-1)
    wuq = _from_col_shards(g_wuq).reshape(L, Q_LORA, MLA_HEADS, QK_NOPE + QK_ROPE)
    wq_ext = jnp.concatenate([wuq, _swap_half(wuq[..., QK_NOPE:])], axis=-1).transpose(0, 2, 1, 3)
    wukv = _from_col_shards(g_wukv)
    wout = g_wout.reshape(L, D_MODEL, D_MODEL)
    wout_pool, wout_mla = wout[:, :POOL_WIDTH], wout[:, POOL_WIDTH:]
    mwq = g_mwq.reshape(L, D_MODEL, D_MODEL)
    mwo = g_mwo.reshape(L, D_MODEL, D_MODEL)
    wbd = _block_diag(pool_w.astype(bf16))

    ln_pad = jnp.zeros((2, L, 4, N_CHIPS, D_MODEL // N_CHIPS), f32)
    ln_pad = lax.dynamic_update_slice(ln_pad, jnp.stack([ln_g, ln_b])[:, :, :, None, :], (0, 0, 0, chip, 0))
    ln_full = allsum_small("allsum_ln", ln_pad.reshape(-1, 128)) * 0.5
    ln_full = ln_full.reshape(2, L, 4, D_MODEL)
    lng, lnb = ln_full[0], ln_full[1]

    half = QK_ROPE // 2
    inv_freq = ROPE_BASE ** (-jnp.arange(half, dtype=f32) / half)
    ang = positions[0].astype(f32)[:, None] * inv_freq
    cos, sin = jnp.cos(ang), jnp.sin(ang)
    cs = jnp.concatenate([cos, cos, sin, sin], axis=-1)

    memb = mem[0].astype(bf16)
    xf = x[0]
    xb = xf.astype(bf16)
    vec = lambda a: a.reshape(1, -1)

    saved = []
    for l in range(L):
        sv = {}
        sv["x0b"] = xb
        gate, up, act = ffn_up(f"ffn1_up_{l}", xb, g_f1w13, l)
        z1, x1f, x1b = proj_res_ln(f"ffn1_down_{l}", [act], [f1w2], [l], xf, vec(lng[l, 0]), vec(lnb[l, 0]), 0.5)
        sv.update(gate1=gate, up1=up, act1=act, z1=z1, x1b=x1b)
        u, cq, ckv, cqn, ckvn, q, k, v = mix_pre(f"mix_pre_{l}", x1b, win_ext, wq_ext, wukv, l,
                                                   vec(q_norm_g[l]), vec(kv_norm_g[l]), cs)
        dpool, ypool = pool_fwd(f"pool_fwd_{l}", u, wbd[l], vec(pool_scale[l]))
        o, lse = mla_attn_fwd(f"mla_fwd_{l}", q, k, v)
        z2, x2f, x2b = proj_res_ln(f"mix_out_{l}", [ypool, o], [wout_pool, wout_mla], [l, l], x1f,
                                   vec(lng[l, 1]), vec(lnb[l, 1]), 1.0)
        sv.update(cq=cq, ckv=ckv, cqn=cqn, ckvn=ckvn, q=q, k=k, v=v, dpool=dpool, ypool=ypool, o=o, lse=lse, z2=z2, x2b=x2b)
        kvm = mm_nn_shard(f"mem_kv_{l}", memb, g_mwkv, l)
        cq_, co_, z3, x3f, x3b = cross_fwd(f"cross_fwd_{l}", x2b, x2f, mwq, mwo, l, kvm, vec(lng[l, 2]), vec(lnb[l, 2]))
        sv.update(kvm=kvm, crq=cq_, cro=co_, z3=z3, x3b=x3b)
        gate, up, act = ffn_up(f"ffn2_up_{l}", x3b, g_f2w13, l)
        z4, xf, xb = proj_res_ln(f"ffn2_down_{l}", [act], [f2w2], [l], x3f, vec(lng[l, 3]), vec(lnb[l, 3]), 0.5)
        sv.update(gate2=gate, up2=up, act2=act, z4=z4)
        saved.append(sv)

    dy, loss_blk = loss_grad("loss_grad", xf, loss_target[0])
    loss = lax.psum(loss_blk[0, 0], ("x", "y", "c"))

    G = dict(f1w13=None, f1w2=None, mwq=None, mwkv=None, mwo=None, f2w13=None, f2w2=None)
    small = {k_: [None] * L for k_ in ("win", "wuq", "wukv", "wout", "pool_w", "pool_scale", "gq", "gkv", "lng", "lnb")}
    for l in reversed(range(L)):
        sv = saved[l]
        dlg, dlb = [None] * 4, [None] * 4
        dzb, dres, dlg[3], dlb[3] = ln_bwd(f"ln4_bwd_{l}", dy, sv["z4"], vec(lng[l, 3]), 0.5)
        dh = ffn_bwd_da(f"ffn2_bwd_da_{l}", dzb, f2w2, l, sv["gate2"], sv["up2"])
        G["f2w2"] = mm_tn(f"ffn2_dw2_{l}", sv["act2"], dzb, "nat", l, G["f2w2"])
        G["f2w13"] = mm_tn(f"ffn2_dw13_{l}", sv["x3b"], dh, "shard", l, G["f2w13"])
        dy = ffn_dx(f"ffn2_dx_{l}", dh, g_f2w13, l, dres)
        dzb, dres, dlg[2], dlb[2] = ln_bwd(f"ln3_bwd_{l}", dy, sv["z3"], vec(lng[l, 2]), 1.0)
        dqc, dkvm = cross_bwd(f"cross_bwd_{l}", dzb, mwo, l, sv["crq"], sv["kvm"])
        G["mwo"] = mm_tn(f"cross_dwo_{l}", sv["cro"], dzb, "nat", l, G["mwo"])
        G["mwq"] = mm_tn(f"cross_dwq_{l}", sv["x2b"], dqc, "nat", l, G["mwq"])
        G["mwkv"] = mm_tn(f"cross_dwkv_{l}", memb, dkvm, "shard", l, G["mwkv"])
        dy = mm_nt_res(f"cross_dx_{l}", [dqc], [mwq], [l], dres, f32)
        dzb, dres, dlg[1], dlb[1] = ln_bwd(f"ln2_bwd_{l}", dy, sv["z2"], vec(lng[l, 1]), 1.0)
        dyp = mm_nt_res(f"mix_dpool_{l}", [dzb], [wout_pool], [l], None, bf16)
        do = mm_nt_res(f"mix_do_{l}", [dzb], [wout_mla], [l], None, bf16)
        dwo_p = mm_tn(f"mix_dwout_pool_{l}", sv["ypool"], dzb)
        dwo_m = mm_tn(f"mix_dwout_mla_{l}", sv["o"], dzb)
        small["wout"][l] = jnp.concatenate([dwo_p, dwo_m], axis=0)
        dq, dk, dv = mla_attn_bwd(f"mla_bwd_{l}", sv["q"], sv["k"], sv["v"], sv["o"], do, sv["lse"])
        dqe, dkv, dh_rest, dgq, dgkv = mix_post_bwd(f"mix_post_bwd_{l}", dq, dk, dv, wq_ext, wukv, l, sv["cq"], sv["ckv"],
                                                     vec(q_norm_g[l]), vec(kv_norm_g[l]), cs)
        du, dyw, dscale = pool_bwd(f"pool_bwd_{l}", dyp, sv["dpool"], wbd[l], vec(pool_scale[l]))
        dwq_e = mm_tn(f"mix_dwuq_{l}", sv["cqn"], dqe).reshape(Q_LORA, MLA_HEADS, 256)
        small["wuq"][l] = jnp.concatenate(
            [dwq_e[..., :QK_NOPE], _unswap_add(dwq_e[..., QK_NOPE:QK_NOPE + QK_ROPE], dwq_e[..., QK_NOPE + QK_ROPE:])],
            axis=-1).reshape(Q_LORA, MLA_HEADS * (QK_NOPE + QK_ROPE))
        small["wukv"][l] = mm_tn(f"mix_dwukv_{l}", sv["ckvn"], dkv)
        dwbd = mm_tn(f"pool_dw_{l}", sv["dpool"], dyw)
        small["pool_w"][l] = jnp.stack([dwbd[64 * gi:64 * gi + 64, 64 * gi:64 * gi + 64] for gi in range(4)])
        small["pool_scale"][l], small["gq"][l], small["gkv"][l] = dscale[0], dgq[0], dgkv[0]
        dh_ext = jnp.concatenate([du, dh_rest], axis=1)
        dwin_e = mm_tn(f"mix_dwin_{l}", sv["x1b"], dh_ext)
        small["win"][l] = jnp.concatenate(
            [dwin_e[:, :D_IN - QK_ROPE], _unswap_add(dwin_e[:, D_IN - QK_ROPE:D_IN], dwin_e[:, D_IN:])], axis=-1)
        dy = mm_nt_res(f"mix_dx_{l}", [dh_ext], [win_ext], [l], dres, f32)
        dzb, dres, dlg[0], dlb[0] = ln_bwd(f"ln1_bwd_{l}", dy, sv["z1"], vec(lng[l, 0]), 0.5)
        dh = ffn_bwd_da(f"ffn1_bwd_da_{l}", dzb, f1w2, l, sv["gate1"], sv["up1"])
        G["f1w2"] = mm_tn(f"ffn1_dw2_{l}", sv["act1"], dzb, "nat", l, G["f1w2"])
        G["f1w13"] = mm_tn(f"ffn1_dw13_{l}", sv["x0b"], dh, "shard", l, G["f1w13"])
        dy = ffn_dx(f"ffn1_dx_{l}", dh, g_f1w13, l, dres)
        small["lng"][l] = jnp.concatenate(dlg, axis=0)
        small["lnb"][l] = jnp.concatenate(dlb, axis=0)
    grad_x = dy[None]

    row_shards = lambda a, K: a.reshape(L, N_CHIPS, K // N_CHIPS, a.shape[-1])
    g_list = [G["f1w13"], row_shards(G["f1w2"], D_FF),
              jnp.stack(small["win"]).reshape(L, N_CHIPS, D_MODEL // N_CHIPS, D_IN),
              _to_col_shards(jnp.stack(small["wuq"])), _to_col_shards(jnp.stack(small["wukv"])),
              jnp.stack(small["wout"]).reshape(L, N_CHIPS, D_MODEL // N_CHIPS, D_MODEL),
              row_shards(G["mwq"], D_MODEL), G["mwkv"], row_shards(G["mwo"], D_MODEL),
              G["f2w13"], row_shards(G["f2w2"], D_FF)]
    big_grads = reduce_grads(g_list)

    rep = [jnp.stack(small["pool_w"]).reshape(-1), jnp.stack(small["pool_scale"]).reshape(-1),
           jnp.stack(small["gq"]).reshape(-1), jnp.stack(small["gkv"]).reshape(-1),
           jnp.stack(small["lng"]).reshape(-1), jnp.stack(small["lnb"]).reshape(-1)]
    sizes = [r.shape[0] for r in rep]
    packed = jnp.concatenate(rep)
    pad = (-packed.shape[0]) % 1024
    tot = allsum_small("allsum_small_grads", jnp.pad(packed, (0, pad)).reshape(-1, 128)).reshape(-1)
    offs = [0]
    for s_ in sizes:
        offs.append(offs[-1] + s_)
    parts = [tot[offs[i]:offs[i + 1]] for i in range(len(sizes))]
    g_pool_w = parts[0].reshape(pool_w.shape)
    g_pool_scale = parts[1].reshape(pool_scale.shape)
    g_gq = parts[2].reshape(q_norm_g.shape)
    g_gkv = parts[3].reshape(kv_norm_g.shape)
    shard_cols = lambda a: lax.dynamic_slice_in_dim(a.reshape(L, 4, D_MODEL), chip * (D_MODEL // N_CHIPS),
                                                    D_MODEL // N_CHIPS, axis=2)
    g_lng, g_lnb = shard_cols(parts[4]), shard_cols(parts[5])

    out_names = ("lng", "lnb", "f1w13", "f1w2", "win", "pool_w", "pool_scale", "gq", "wuq", "gkv", "wukv", "wout", "mwq",
                 "mwkv", "mwo", "f2w13", "f2w2")
    big.update(lng=g_lng, lnb=g_lnb, pool_w=g_pool_w, pool_scale=g_pool_scale, gq=g_gq, gkv=g_gkv)
    late = ("f1w13", "f1w2")
    ws = [ln_g, ln_b, ffn1_w13, ffn1_w2, w_in, pool_w, pool_scale, q_norm_g, w_uq, kv_norm_g, w_ukv, w_out, mem_wq,
          mem_wkv, mem_wo, ffn2_w13, ffn2_w2]
    ms = [m_ln_g, m_ln_b, m_ffn1_w13, m_ffn1_w2, m_w_in, m_pool_w, m_pool_scale, m_q_norm_g, m_w_uq, m_kv_norm_g, m_w_ukv,
          m_w_out, m_mem_wq, m_mem_wkv, m_mem_wo, m_ffn2_w13, m_ffn2_w2]
    vs = [v_ln_g, v_ln_b, v_ffn1_w13, v_ffn1_w2, v_w_in, v_pool_w, v_pool_scale, v_q_norm_g, v_w_uq, v_kv_norm_g, v_w_ukv,
          v_w_out, v_mem_wq, v_mem_wkv, v_mem_wo, v_ffn2_w13, v_ffn2_w2]
    res = {}

    def update(a):
        res[a] = adamw(f"adamw_{a}", ws[a], big[out_names[a]].reshape(ws[a].shape), ms[a], vs[a])

    for a, n in enumerate(out_names):
        if n not in late:
            update(a)
    sums_b = red_end("b0", st_cb, 0, [sums1[n] for n in late], tuple(r[0] for r in res.values()))
    big.update(zip(late, pair_share("pair_share_b", sums_b)))
    for a, n in enumerate(out_names):
        if n in late:
            update(a)
    order = range(len(out_names))
    grads = [big[n].reshape(w_.shape) for n, w_ in zip(out_names, ws)]
    return (loss, grad_x, *grads, *[res[a][0] for a in order], *[res[a][1] for a in order], *[res[a][2] for a in order])


_HBM_SPEC = pl.BlockSpec(memory_space=pltpu.HBM)
_SEM_SPEC = pl.BlockSpec(memory_space=pltpu.SEMAPHORE)
_ANY_SPEC = pl.BlockSpec(memory_space=pl.ANY)
_DATAFLOW = pltpu.SideEffectType.DATAFLOW_SIDE_EFFECTING


def _split_call(name, body_fn, bufs, sems_in, sems_out_sizes, after):
    nb, ni, no = len(bufs), len(sems_in), len(sems_out_sizes)
    afters = () if after is None else tuple(after) if isinstance(after, (tuple, list)) else (after,)

    def body(*refs):
        k = nb + ni + len(afters)
        body_fn(refs[:nb], refs[nb:nb + ni], refs[k:k + no])
        refs[-1][...] = jnp.zeros((8, 128), f32)

    outs = pl.pallas_call(
        body, name=name,
        in_specs=[_HBM_SPEC] * nb + [_SEM_SPEC] * ni + [_ANY_SPEC] * len(afters),
        out_specs=[_SEM_SPEC] * no + [_HBM_SPEC] * nb + [pl.BlockSpec(memory_space=pltpu.VMEM)],
        out_shape=[pltpu.SemaphoreType.DMA((s,)) for s in sems_out_sizes]
        + [pltpu.HBM(b.shape, b.dtype) for b in bufs] + [SDS((8, 128), f32)],
        input_output_aliases={i: no + i for i in range(nb)},
        compiler_params=pltpu.CompilerParams(has_side_effects=_DATAFLOW),
    )(*[pltpu.with_memory_space_constraint(b, pltpu.HBM) for b in bufs], *sems_in, *afters)
    return list(outs[no:no + nb]), list(outs[:no]), outs[-1]


def _rcopy(src, dst, ssem, rsem, to):
    return pltpu.make_async_remote_copy(src_ref=src, dst_ref=dst, send_sem=ssem, recv_sem=rsem, device_id=to,
                                        device_id_type=MESH)


def gather_start(name, groups, after):
    flat = [b for bufs, _ in groups for b in bufs]
    sizes = [3 * len(bufs) for bufs, _ in groups for _ in range(2)]

    def body_fn(b_in, s_in, s_out):
        x, y, c = _me()
        q = 2 * x + y
        chips = _other_chips(x, y)
        pos = 0
        for gi, (bufs, owner) in enumerate(groups):
            refs = b_in[pos:pos + len(bufs)]
            pos += len(bufs)

            @pl.when(c == owner)
            def _(refs=refs, send=s_out[2 * gi], recv=s_out[2 * gi + 1]):
                for a, r in enumerate(refs):
                    for k, (cx, cy) in enumerate(chips):
                        _rcopy(r.at[q], r.at[q], send.at[3 * a + k], recv.at[3 * a + k], (cx, cy, c)).start()

    outs, sems, token = _split_call(name, body_fn, flat, [], sizes, after)
    res, pos = [], 0
    for gi, (bufs, owner) in enumerate(groups):
        res.append((outs[pos:pos + len(bufs)], sems[2 * gi], sems[2 * gi + 1], owner))
        pos += len(bufs)
    return res, token


def gather_forward(name, grp, after):
    bufs, send, recv, owner = grp
    n3 = 3 * len(bufs)

    def body_fn(b_in, s_in, s_out):
        x, y, c = _me()
        q = 2 * x + y
        sibling = (x, y, 1 - c)
        chips = _other_chips(x, y)

        @pl.when(c == owner)
        def _():
            for a, r in enumerate(b_in):
                for k, (cx, cy) in enumerate(chips):
                    i = 3 * a + k
                    land = r.at[2 * cx + cy]
                    _rcopy(r.at[q], r.at[q], s_in[0].at[i], s_in[1].at[i], (cx, cy, c)).wait_send()
                    _rcopy(land, land, s_in[0].at[i], s_in[1].at[i], (cx, cy, c)).wait_recv()
                    _rcopy(land, land, s_out[0].at[i], s_out[1].at[i], sibling).start()

    outs, sems, token = _split_call(name, body_fn, bufs, [send, recv], [n3, n3], after)
    return (outs, sems[0], sems[1], owner), token


def gather_finish(name, grp, after):
    bufs, fsend, frecv, owner = grp

    def body_fn(b_in, s_in, s_out):
        x, y, c = _me()
        sibling = (x, y, 1 - c)
        chips = _other_chips(x, y)

        def each(wait):
            for a, r in enumerate(b_in):
                for k, (cx, cy) in enumerate(chips):
                    land = r.at[2 * cx + cy]
                    wait(_rcopy(land, land, s_in[0].at[3 * a + k], s_in[1].at[3 * a + k], sibling))

        @pl.when(c == owner)
        def _():
            each(lambda cp: cp.wait_send())

        @pl.when(c != owner)
        def _():
            each(lambda cp: cp.wait_recv())

    outs, _, _ = _split_call(name, body_fn, bufs, [fsend, frecv], [], after)
    return outs


def pair_send_start(name, gs, owner, after):
    n = len(gs)
    lands = [lax.empty(g.shape, g.dtype) for g in gs]

    def body_fn(b_in, s_in, s_out):
        x, y, c = _me()

        @pl.when(c == 1 - owner)
        def _():
            for a in range(n):
                _rcopy(b_in[a], b_in[n + a], s_out[0].at[a], s_out[1].at[a], (x, y, owner)).start()

    outs, sems, token = _split_call(name, body_fn, list(gs) + lands, [], [n, n], after)
    return (outs[:n], outs[n:], sems[0], sems[1], owner), token


def pair_send_wait(name, st, after):
    gs, lands, send, recv, owner = st
    n = len(gs)

    def body_fn(b_in, s_in, s_out):
        x, y, c = _me()

        @pl.when(c == 1 - owner)
        def _():
            for a in range(n):
                _rcopy(b_in[a], b_in[n + a], s_in[0].at[a], s_in[1].at[a], (x, y, owner)).wait_send()

        @pl.when(c == owner)
        def _():
            for a in range(n):
                _rcopy(b_in[a], b_in[n + a], s_in[0].at[a], s_in[1].at[a], (x, y, 1 - owner)).wait_recv()

    outs, _, _ = _split_call(name, body_fn, list(gs) + list(lands), [send, recv], [], after)
    return outs[:n], outs[n:]


def chip_exchange_start(name, psums, owner, after):
    n = len(psums)
    lands = [lax.empty((3,) + p.shape[1:], p.dtype) for p in psums]

    def body_fn(b_in, s_in, s_out):
        x, y, c = _me()
        chips = _other_chips(x, y)

        @pl.when(c == owner)
        def _():
            for a in range(n):
                for k, (cx, cy) in enumerate(chips):
                    _rcopy(b_in[a].at[2 * cx + cy], b_in[n + a].at[k], s_out[0].at[3 * a + k], s_out[1].at[3 * a + k],
                           (cx, cy, c)).start()

    outs, sems, token = _split_call(name, body_fn, list(psums) + lands, [], [3 * n, 3 * n], after)
    return (outs[:n], outs[n:], sems[0], sems[1], owner), token


def chip_exchange_wait(name, st, after):
    psums, lands, send, recv, owner = st
    n = len(psums)

    def body_fn(b_in, s_in, s_out):
        x, y, c = _me()
        chips = _other_chips(x, y)

        @pl.when(c == owner)
        def _():
            for a in range(n):
                for k, (cx, cy) in enumerate(chips):
                    cp = _rcopy(b_in[a].at[2 * cx + cy], b_in[n + a].at[k], s_in[0].at[3 * a + k], s_in[1].at[3 * a + k],
                                (cx, cy, c))
                    cp.wait_send()
                    cp.wait_recv()

    outs, _, _ = _split_call(name, body_fn, list(psums) + list(lands), [send, recv], [], after)
    return outs[:n], outs[n:]


def pair_sum(name, g, recv):
    shape = g.shape
    cols = shape[-1]
    rows = math.prod(shape[:-1])
    tr = _row_tile(rows, cols)

    def body(g_ref, r_ref, o_ref):
        o_ref[...] = (g_ref[...] + r_ref[...]).astype(bf16)

    blk = pl.BlockSpec((tr, cols), lambda i: (i, 0))
    out = pl.pallas_call(
        body, name=name, grid=(rows // tr,), in_specs=[blk, blk], out_specs=blk, out_shape=SDS((rows, cols), bf16),
        compiler_params=_cp("parallel"),
    )(g.reshape(rows, cols), recv.reshape(rows, cols))
    return out.reshape(shape)


def chip_sum(name, psum, recv, q_arr, layer, prev):
    shard = psum.shape[1:]
    cols = shard[-1]
    rows = math.prod(shard[:-1])
    tr = _row_tile(rows, cols)

    def body(q_ref, p_ref, r_ref, *rest):
        rest[-1][0] = ((p_ref[0].astype(f32) + r_ref[0].astype(f32)) + r_ref[1].astype(f32)) + r_ref[2].astype(f32)

    in_specs = [pl.BlockSpec((1, tr, cols), lambda i, q_ref: (q_ref[0], i, 0)),
                pl.BlockSpec((3, tr, cols), lambda i, q_ref: (0, i, 0))]
    args = [q_arr, psum.reshape(N_CHIPS, rows, cols), recv.reshape(3, rows, cols)]
    aliases = {}
    if prev is not None:
        in_specs.append(pl.BlockSpec(memory_space=pl.ANY))
        args.append(prev.reshape(DEPTH, rows, cols))
        aliases = {3: 0}
    out = pl.pallas_call(
        body, name=name,
        grid_spec=pltpu.PrefetchScalarGridSpec(
            num_scalar_prefetch=1, grid=(rows // tr,), in_specs=in_specs,
            out_specs=pl.BlockSpec((1, tr, cols), lambda i, q_ref: (layer, i, 0))),
        out_shape=SDS((DEPTH, rows, cols), f32), input_output_aliases=aliases, compiler_params=_cp("parallel"),
    )(*args)
    return out.reshape((DEPTH,) + shard)


W_NAMES = ("f1w13", "f1w2", "win", "wuq", "wukv", "wout", "mwq", "mwkv", "mwo", "f2w13", "f2w2")
MIX_NAMES = ("win", "wuq", "wukv")
MID_NAMES = ("wout", "mwq", "mwkv", "mwo")
FFN2_NAMES = ("f2w13", "f2w2")


def kernel(x, mem, positions, ln_g, ln_b, ffn1_w13, ffn1_w2, w_in, pool_w, pool_scale, q_norm_g, w_uq, kv_norm_g, w_ukv, w_out, mem_wq, mem_wkv, mem_wo, ffn2_w13, ffn2_w2, loss_target, m_ln_g, m_ln_b, m_ffn1_w13, m_ffn1_w2, m_w_in, m_pool_w, m_pool_scale, m_q_norm_g, m_w_uq, m_kv_norm_g, m_w_ukv, m_w_out, m_mem_wq, m_mem_wkv, m_mem_wo, m_ffn2_w13, m_ffn2_w2, v_ln_g, v_ln_b, v_ffn1_w13, v_ffn1_w2, v_w_in, v_pool_w, v_pool_scale, v_q_norm_g, v_w_uq, v_kv_norm_g, v_w_ukv, v_w_out, v_mem_wq, v_mem_wkv, v_mem_wo, v_ffn2_w13, v_ffn2_w2):
    L = DEPTH
    qx, qy, _ = _me()
    chip = 2 * qx + qy
    vec = lambda a: a.reshape(1, -1)

    shards = dict(zip(W_NAMES, (ffn1_w13, ffn1_w2, w_in, w_uq, w_ukv, w_out, mem_wq, mem_wkv, mem_wo, ffn2_w13, ffn2_w2)))

    def place(sh, slot):
        return lax.dynamic_update_slice(jnp.zeros((N_CHIPS,) + sh.shape, bf16), sh.astype(bf16)[None],
                                        (slot,) + (0,) * sh.ndim)

    first = ("f1w13", "f1w2")
    bufs = [dict(), dict()]
    for n in first:
        bufs[0][n] = place(shards[n][0], chip)
    gw = [dict(), dict()]
    (g0,), tok = gather_start("gather_a_start", [([bufs[0][n] for n in first], 0)], None)
    chip_then = chip + tok[0, 0].astype(jnp.int32)
    for l in range(L):
        for n in W_NAMES:
            if n not in bufs[l]:
                bufs[l][n] = place(shards[n][l], chip_then)
    others = tuple(bufs[l][n] for l in range(L) for n in W_NAMES if (l, n) not in ((0, first[0]), (0, first[1])))
    g0, tok = gather_forward("gather_a_forward", g0, others)

    ln_pad = jnp.zeros((2, L, 4, N_CHIPS, D_MODEL // N_CHIPS), f32)
    ln_pad = lax.dynamic_update_slice(ln_pad, jnp.stack([ln_g, ln_b])[:, :, :, None, :], (0, 0, 0, chip, 0))
    ln_sum = allsum_small("allsum_ln", ln_pad.reshape(-1, 128), (tok,))
    ln_full = (ln_sum * 0.5).reshape(2, L, 4, D_MODEL)
    lng, lnb = ln_full[0], ln_full[1]

    gw[0]["f1w13"], gw[0]["f1w2"] = gather_finish("gather_a_finish", g0, ln_sum)
    (g_mix, g_mid, g_ffn2, g_l1), tok_b = gather_start(
        "gather_b_start",
        [([bufs[0][n] for n in MIX_NAMES], 0), ([bufs[0][n] for n in MID_NAMES], 0), ([bufs[0][n] for n in FFN2_NAMES], 0),
         ([bufs[1][n] for n in W_NAMES], 1)], ln_sum)

    half = QK_ROPE // 2
    inv_freq = ROPE_BASE ** (-jnp.arange(half, dtype=f32) / half)
    ang = positions[0].astype(f32)[:, None] * inv_freq
    cos, sin = jnp.cos(ang), jnp.sin(ang)
    cs = jnp.concatenate([cos, cos, sin, sin], axis=-1)

    memb = mem[0].astype(bf16)
    xf = x[0]
    xb = xf.astype(bf16)
    dep = (tok_b,)

    saved, W = [], [None, None]
    for l in range(L):
        sv = {}
        if l == 1:
            gl1 = gather_finish("gather_l1_finish", g_l1, xb)
            gw[1] = dict(zip(W_NAMES, gl1))
        sv["x0b"] = xb
        f1w13 = gw[l]["f1w13"][None]
        gate, up, act = ffn_up(f"ffn1_up_{l}", xb, f1w13, 0, dep)
        dep = ()
        if l == 0:
            g_mix, _ = gather_forward("gather_mix_forward", g_mix, act)
        z1, x1f, x1b = proj_res_ln(f"ffn1_down_{l}", [act], [gw[l]["f1w2"].reshape(1, D_FF, D_MODEL)], [0], xf,
                                   vec(lng[l, 0]), vec(lnb[l, 0]), 0.5)
        sv.update(gate1=gate, up1=up, act1=act, z1=z1, x1b=x1b)
        if l == 0:
            gw[0].update(zip(MIX_NAMES, gather_finish("gather_mix_finish", g_mix, x1b)))
            g_mid, _ = gather_forward("gather_mid_forward", g_mid, x1b)
        win = gw[l]["win"].reshape(D_MODEL, D_IN)
        win_ext = jnp.concatenate([win, _swap_half(win[:, D_IN - QK_ROPE:])], axis=-1)[None]
        wuq = _from_col_shards(gw[l]["wuq"]).reshape(Q_LORA, MLA_HEADS, QK_NOPE + QK_ROPE)
        wq_ext = jnp.concatenate([wuq, _swap_half(wuq[..., QK_NOPE:])], axis=-1).transpose(1, 0, 2)[None]
        wukv = _from_col_shards(gw[l]["wukv"])[None]
        wbd = _block_diag(pool_w[l][None].astype(bf16))[0]
        u, cq, ckv, cqn, ckvn, q, k, v = mix_pre(f"mix_pre_{l}", x1b, win_ext, wq_ext, wukv, 0,
                                                   vec(q_norm_g[l]), vec(kv_norm_g[l]), cs)
        dpool, ypool = pool_fwd(f"pool_fwd_{l}", u, wbd, vec(pool_scale[l]))
        o, lse = mla_attn_fwd(f"mla_fwd_{l}", q, k, v)
        if l == 0:
            gw[0].update(zip(MID_NAMES, gather_finish("gather_mid_finish", g_mid, o)))
            g_ffn2, tok_f = gather_forward("gather_ffn2_forward", g_ffn2, o)
            g_l1, tok_l = gather_forward("gather_l1_forward", g_l1, o)
            dep = (tok_f, tok_l)
        wout = gw[l]["wout"].reshape(D_MODEL, D_MODEL)
        wout_pool, wout_mla = wout[None, :POOL_WIDTH], wout[None, POOL_WIDTH:]
        mwq = gw[l]["mwq"].reshape(1, D_MODEL, D_MODEL)
        mwo = gw[l]["mwo"].reshape(1, D_MODEL, D_MODEL)
        mwkv = gw[l]["mwkv"][None]
        z2, x2f, x2b = proj_res_ln(f"mix_out_{l}", [ypool, o], [wout_pool, wout_mla], [0, 0], x1f,
                                   vec(lng[l, 1]), vec(lnb[l, 1]), 1.0, dep)
        dep = ()
        sv.update(cq=cq, ckv=ckv, cqn=cqn, ckvn=ckvn, q=q, k=k, v=v, dpool=dpool, ypool=ypool, o=o, lse=lse, z2=z2, x2b=x2b)
        kvm = mm_nn_shard(f"mem_kv_{l}", memb, mwkv, 0)
        cq_, co_, z3, x3f, x3b = cross_fwd(f"cross_fwd_{l}", x2b, x2f, mwq, mwo, 0, kvm, vec(lng[l, 2]), vec(lnb[l, 2]))
        sv.update(kvm=kvm, crq=cq_, cro=co_, z3=z3, x3b=x3b)
        if l == 0:
            gw[0].update(zip(FFN2_NAMES, gather_finish("gather_ffn2_finish", g_ffn2, x3b)))
        f2w13 = gw[l]["f2w13"][None]
        f2w2 = gw[l]["f2w2"].reshape(1, D_FF, D_MODEL)
        gate, up, act = ffn_up(f"ffn2_up_{l}", x3b, f2w13, 0)
        z4, xf, xb = proj_res_ln(f"ffn2_down_{l}", [act], [f2w2], [0], x3f, vec(lng[l, 3]), vec(lnb[l, 3]), 0.5)
        sv.update(gate2=gate, up2=up, act2=act, z4=z4)
        W[l] = dict(f1w13=f1w13, f1w2=gw[l]["f1w2"].reshape(1, D_FF, D_MODEL), win_ext=win_ext, wq_ext=wq_ext, wukv=wukv,
                    wbd=wbd, wout_pool=wout_pool, wout_mla=wout_mla, mwq=mwq, mwo=mwo, f2w13=f2w13, f2w2=f2w2)
        saved.append(sv)

    dy, loss_blk = loss_grad("loss_grad", xf, loss_target[0])
    loss = lax.psum(loss_blk[0, 0], ("x", "y", "c"))

    row_shards = lambda a: a.reshape(N_CHIPS, a.shape[0] // N_CHIPS, a.shape[1])
    small = {k_: [None] * L for k_ in ("pool_w", "pool_scale", "gq", "gkv", "lng", "lnb")}
    q_arr = jnp.reshape(chip, (1,)).astype(jnp.int32)
    rest_names = [n for n in W_NAMES if n not in ("f1w13", "f1w2")]

    def red_begin(tag, gs, owner):
        return pair_send_start(f"pair_send_start_{tag}", gs, owner, None)

    def red_mid(tag, st, owner, after):
        gs_, lands_ = pair_send_wait(f"pair_send_wait_{tag}", st, after)
        ps = [pair_sum(f"pair_sum_{tag}_{a}", g_, r_) for a, (g_, r_) in enumerate(zip(gs_, lands_))]
        return chip_exchange_start(f"chip_exchange_start_{tag}", ps, owner, None)

    def red_end(tag, st, layer, prevs, after):
        ps, lands_ = chip_exchange_wait(f"chip_exchange_wait_{tag}", st, after)
        return [chip_sum(f"chip_sum_{tag}_{a}", p_, r_, q_arr, layer, s_)
                for a, (p_, r_, s_) in enumerate(zip(ps, lands_, prevs))]

    st_p1 = st_c1 = st_pa = st_ca = None
    for l in reversed(range(L)):
        sv, w = saved[l], W[l]
        g = {}
        dlg, dlb = [None] * 4, [None] * 4
        dzb, dres, dlg[3], dlb[3] = ln_bwd(f"ln4_bwd_{l}", dy, sv["z4"], vec(lng[l, 3]), 0.5, dep)
        dep = ()
        dh = ffn_bwd_da(f"ffn2_bwd_da_{l}", dzb, w["f2w2"], 0, sv["gate2"], sv["up2"])
        g["f2w2"] = row_shards(mm_tn(f"ffn2_dw2_{l}", sv["act2"], dzb))
        g["f2w13"] = mm_tn(f"ffn2_dw13_{l}", sv["x3b"], dh, True)
        dy = ffn_dx(f"ffn2_dx_{l}", dh, w["f2w13"], 0, dres)
        if l == 0:
            st_c1, tok = red_mid("l1", st_p1, 1, dy)
            dep = (tok, g["f2w2"], g["f2w13"])
        dzb, dres, dlg[2], dlb[2] = ln_bwd(f"ln3_bwd_{l}", dy, sv["z3"], vec(lng[l, 2]), 1.0, dep)
        dep = ()
        dqc, dkvm = cross_bwd(f"cross_bwd_{l}", dzb, w["mwo"], 0, sv["crq"], sv["kvm"])
        g["mwo"] = row_shards(mm_tn(f"cross_dwo_{l}", sv["cro"], dzb))
        g["mwq"] = row_shards(mm_tn(f"cross_dwq_{l}", sv["x2b"], dqc))
        g["mwkv"] = mm_tn(f"cross_dwkv_{l}", memb, dkvm, True)
        dy = mm_nt_res(f"cross_dx_{l}", [dqc], [w["mwq"]], [0], dres, f32)
        dzb, dres, dlg[1], dlb[1] = ln_bwd(f"ln2_bwd_{l}", dy, sv["z2"], vec(lng[l, 1]), 1.0)
        dyp = mm_nt_res(f"mix_dpool_{l}", [dzb], [w["wout_pool"]], [0], None, bf16)
        do = mm_nt_res(f"mix_do_{l}", [dzb], [w["wout_mla"]], [0], None, bf16)
        dwo_p = mm_tn(f"mix_dwout_pool_{l}", sv["ypool"], dzb)
        dwo_m = mm_tn(f"mix_dwout_mla_{l}", sv["o"], dzb)
        g["wout"] = row_shards(jnp.concatenate([dwo_p, dwo_m], axis=0))
        dq, dk, dv = mla_attn_bwd(f"mla_bwd_{l}", sv["q"], sv["k"], sv["v"], sv["o"], do, sv["lse"])
        dqe, dkv, dh_rest, dgq, dgkv = mix_post_bwd(f"mix_post_bwd_{l}", dq, dk, dv, w["wq_ext"], w["wukv"], 0, sv["cq"],
                                                     sv["ckv"], vec(q_norm_g[l]), vec(kv_norm_g[l]), cs)
        du, dyw, dscale = pool_bwd(f"pool_bwd_{l}", dyp, sv["dpool"], w["wbd"], vec(pool_scale[l]))
        dwq_e = mm_tn(f"mix_dwuq_{l}", sv["cqn"], dqe).reshape(Q_LORA, MLA_HEADS, 256)
        g["wuq"] = _to_col_shards(jnp.concatenate(
            [dwq_e[..., :QK_NOPE], _unswap_add(dwq_e[..., QK_NOPE:QK_NOPE + QK_ROPE], dwq_e[..., QK_NOPE + QK_ROPE:])],
            axis=-1).reshape(Q_LORA, MLA_HEADS * (QK_NOPE + QK_ROPE)))
        g["wukv"] = _to_col_shards(mm_tn(f"mix_dwukv_{l}", sv["ckvn"], dkv))
        dwbd = mm_tn(f"pool_dw_{l}", sv["dpool"], dyw)
        small["pool_w"][l] = jnp.stack([dwbd[64 * gi:64 * gi + 64, 64 * gi:64 * gi + 64] for gi in range(4)])
        small["pool_scale"][l], small["gq"][l], small["gkv"][l] = dscale[0], dgq[0], dgkv[0]
        dh_ext = jnp.concatenate([du, dh_rest], axis=1)
        dwin_e = mm_tn(f"mix_dwin_{l}", sv["x1b"], dh_ext)
        g["win"] = row_shards(jnp.concatenate(
            [dwin_e[:, :D_IN - QK_ROPE], _unswap_add(dwin_e[:, D_IN - QK_ROPE:D_IN], dwin_e[:, D_IN:])], axis=-1))
        dy = mm_nt_res(f"mix_dx_{l}", [dh_ext], [w["win_ext"]], [0], dres, f32)
        if l == 0:
            st_pa, tok = red_begin("a0", [g[n] for n in rest_names], 0)
            dep = (tok,)
        dzb, dres, dlg[0], dlb[0] = ln_bwd(f"ln1_bwd_{l}", dy, sv["z1"], vec(lng[l, 0]), 0.5, dep)
        dep = ()
        dh = ffn_bwd_da(f"ffn1_bwd_da_{l}", dzb, w["f1w2"], 0, sv["gate1"], sv["up1"])
        dy = ffn_dx(f"ffn1_dx_{l}", dh, w["f1w13"], 0, dres)
        if l == 0:
            st_ca, tok = red_mid("a0", st_pa, 0, dy)
            dep = (tok,)
        g["f1w2"] = row_shards(mm_tn(f"ffn1_dw2_{l}", sv["act1"], dzb, False, dep))
        g["f1w13"] = mm_tn(f"ffn1_dw13_{l}", sv["x0b"], dh, True, dep)
        dep = ()
        small["lng"][l] = jnp.concatenate(dlg, axis=0)
        small["lnb"][l] = jnp.concatenate(dlb, axis=0)
        if l == 1:
            st_p1, tok = red_begin("l1", [g[n] for n in W_NAMES], 1)
            dep = (tok,)
    grad_x = dy[None]

    st_pb, _ = red_begin("b0", [g["f1w13"], g["f1w2"]], 0)
    sums1 = dict(zip(W_NAMES, red_end("l1", st_c1, 1, [None] * len(W_NAMES), g["f1w13"])))
    st_cb, _ = red_mid("b0", st_pb, 0, sums1["f1w13"])
    sums0 = red_end("a0", st_ca, 0, [sums1[n] for n in rest_names], sums1["f1w2"])
    big = dict(zip(rest_names, pair_share("pair_share_a", sums0)))

    rep = [jnp.stack(small["pool_w"]).reshape(-1), jnp.stack(small["pool_scale"]).reshape(-1),
           jnp.stack(small["gq"]).reshape(-1), jnp.stack(small["gkv"]).reshape(-1),
           jnp.stack(small["lng"]).reshape(-1), jnp.stack(small["lnb"]).reshape(-1)]
    sizes = [r.shape[0] for r in rep]
    packed = jnp.concatenate(rep)
    pad = (-packed.shape[0]) % 1024
    tot = allsum_small("allsum_small_grads", jnp.pad(packed, (0, pad)).reshape(-1, 128)).reshape(-1)
    offs = [0]
    for s_ in sizes:
        offs.append(offs[-1] + s_)
    parts = [tot[offs[i]:offs[i + 1]] for i in range(len(sizes))]
    g_pool_w = parts[0].reshape(pool_w.shape)
    g_pool_scale = parts[1].reshape(pool_scale.shape)
    g_gq = parts[2].reshape(q_norm_g.shape)
    g_gkv = parts[3].reshape(kv_norm_g.shape)
    shard_cols = lambda a: lax.dynamic_slice_in_dim(a.reshape(L, 4, D_MODEL), chip * (D_MODEL // N_CHIPS),
                                                    D_MODEL // N_CHIPS, axis=2)
    g_lng, g_lnb = shard_cols(parts[4]), shard_cols(parts[5])

    out_names = ("lng", "lnb", "f1w13", "f1w2", "win", "pool_w", "pool_scale", "gq", "wuq", "gkv", "wukv", "wout", "mwq",
                 "mwkv", "mwo", "f2w13", "f2w2")
    big.update(lng=g_lng, lnb=g_lnb, pool_w=g_pool_w, pool_scale=g_pool_scale, gq=g_gq, gkv=g_gkv)
    late = ("f1w13", "f1w2")
    ws = [ln_g, ln_b, ffn1_w13, ffn1_w2, w_in, pool_w, pool_scale, q_norm_g, w_uq, kv_norm_g, w_ukv, w_out, mem_wq,
          mem_wkv, mem_wo, ffn2_w13, ffn2_w2]
    ms = [m_ln_g, m_ln_b, m_ffn1_w13, m_ffn1_w2, m_w_in, m_pool_w, m_pool_scale, m_q_norm_g, m_w_uq, m_kv_norm_g, m_w_ukv,
          m_w_out, m_mem_wq, m_mem_wkv, m_mem_wo, m_ffn2_w13, m_ffn2_w2]
    vs = [v_ln_g, v_ln_b, v_ffn1_w13, v_ffn1_w2, v_w_in, v_pool_w, v_pool_scale, v_q_norm_g, v_w_uq, v_kv_norm_g, v_w_ukv,
          v_w_out, v_mem_wq, v_mem_wkv, v_mem_wo, v_ffn2_w13, v_ffn2_w2]
    res = {}

    def update(a):
        res[a] = adamw(f"adamw_{a}", ws[a], big[out_names[a]].reshape(ws[a].shape), ms[a], vs[a])

    for a, n in enumerate(out_names):
        if n not in late:
            update(a)
    sums_b = red_end("b0", st_cb, 0, [sums1[n] for n in late], tuple(r[0] for r in res.values()))
    big.update(zip(late, pair_share("pair_share_b", sums_b)))
    for a, n in enumerate(out_names):
        if n in late:
            update(a)
    order = range(len(out_names))
    grads = [big[n].reshape(w_.shape) for n, w_ in zip(out_names, ws)]
    return (loss, grad_x, *grads, *[res[a][0] for a in order], *[res[a][1] for a in order], *[res[a][2] for a in order])
```

```python
import functools
import math

import jax
import jax.numpy as jnp
from jax import lax
from jax.experimental import pallas as pl
from jax.experimental.pallas import tpu as pltpu

f32 = jnp.float32
bf16 = jnp.bfloat16
SDS = jax.ShapeDtypeStruct
MESH = pl.DeviceIdType.MESH

D_MODEL = 1024
DEPTH = 2
N_MEM = 256
MEM_HEADS = 4
MEM_HEAD_DIM = D_MODEL // MEM_HEADS
POOL_WINDOWS = (2, 4, 8, 16)
POOL_WIDTH = 256
POOL_GROUP = 64
QK_NOPE = 128
QK_ROPE = 64
V_HEAD = 128
MLA_HEADS = 6
Q_LORA = 256
KV_LORA = 128
ROPE_BASE = 10000.0
D_FF = 2816
D_IN = POOL_WIDTH + Q_LORA + KV_LORA + QK_ROPE
ALPHA = (2 * DEPTH) ** 0.25
LN_EPS = 1e-5
RMS_EPS = 1e-6
NEG_INF = -1e30
MLA_SCALE = (QK_NOPE + QK_ROPE) ** -0.5
MEM_SCALE = MEM_HEAD_DIM ** -0.5
ADAM_LR = 0.001
ADAM_B1 = 0.9
ADAM_B2 = 0.999
ADAM_EPS = 1e-08
ADAM_WD = 0.01
ADAM_STEP = 10

N_CHIPS = 4
V7X_VMEM_LIMIT = 56 * 2**20
HALO = 16

_NT = (((1,), (1,)), ((), ()))
_TN = (((0,), (0,)), ((), ()))


def _dot(a, b):
    return jnp.dot(a, b, preferred_element_type=f32)


def _dot_nt(a, b):
    return lax.dot_general(a, b, _NT, preferred_element_type=f32)


def _dot_tn(a, b):
    return lax.dot_general(a, b, _TN, preferred_element_type=f32)


def _cp(*sem):
    return pltpu.CompilerParams(dimension_semantics=sem if sem else None, vmem_limit_bytes=V7X_VMEM_LIMIT)


_DEP_SPEC = pl.BlockSpec(memory_space=pl.ANY)


def _with_deps(body, n_in, deps):
    nd = len(deps)
    if not nd:
        return body

    def wrapped(*refs):
        return body(*refs[:n_in], *refs[n_in + nd:])

    return wrapped


def _tile(n, t):
    t = min(n, t)
    assert n % t == 0, (n, t)
    return t


def _row_tile(rows, cols, itemsize=4, target=2 * 2**20):
    best = None
    for t in range(16, rows + 1, 16):
        if rows % t == 0 and t * cols * itemsize <= target:
            best = t
    return best if best is not None else rows


def ffn_up(name, xb, w13, l, deps=()):
    S = xb.shape[0]
    ns = w13.shape[3]
    tm = _tile(S, 512)

    def body(x_ref, wg_ref, wu_ref, g_ref, u_ref, a_ref):
        x = x_ref[...]
        g = _dot(x, wg_ref[0, 0])
        u = _dot(x, wu_ref[0, 0])
        a = g * jax.nn.sigmoid(g) * u
        g_ref[...] = g.astype(bf16)
        u_ref[...] = u.astype(bf16)
        a_ref[...] = a.astype(bf16)

    out = SDS((S, 2 * ns), bf16)
    return pl.pallas_call(
        _with_deps(body, 3, deps), name=name, grid=(2, S // tm),
        in_specs=[pl.BlockSpec((tm, D_MODEL), lambda j, i: (i, 0)),
                  pl.BlockSpec((1, 1, D_MODEL, ns), lambda j, i: (l, j, 0, 0)),
                  pl.BlockSpec((1, 1, D_MODEL, ns), lambda j, i: (l, j + 2, 0, 0))] + [_DEP_SPEC] * len(deps),
        out_specs=[pl.BlockSpec((tm, ns), lambda j, i: (i, j))] * 3,
        out_shape=[out, out, out],
        compiler_params=_cp("parallel", "parallel"),
    )(xb, w13, w13, *deps)


def proj_res_ln(name, parts, ws, wl, x, g, b, rscale, deps=()):
    S = x.shape[0]
    tm = _tile(S, 512)
    n = len(parts)

    def body(*refs):
        p_refs, w_refs = refs[:n], refs[n:2 * n]
        x_ref, g_ref, b_ref, z_ref, y_ref, yb_ref = refs[2 * n:]
        acc = _dot(p_refs[0][...], w_refs[0][0])
        for k in range(1, n):
            acc = acc + _dot(p_refs[k][...], w_refs[k][0])
        if rscale != 1.0:
            acc = rscale * acc
        z = ALPHA * x_ref[...] + acc
        mu = jnp.mean(z, axis=-1, keepdims=True)
        zc = z - mu
        var = jnp.mean(zc * zc, axis=-1, keepdims=True)
        y = zc * lax.rsqrt(var + LN_EPS) * g_ref[...] + b_ref[...]
        z_ref[...] = z
        y_ref[...] = y
        yb_ref[...] = y.astype(bf16)

    row = lambda i: (i, 0)
    in_specs = [pl.BlockSpec((tm, p.shape[1]), row) for p in parts]
    in_specs += [pl.BlockSpec((1,) + w.shape[1:], functools.partial(lambda li, i: (li, 0, 0), li)) for w, li in zip(ws, wl)]
    in_specs += [pl.BlockSpec((tm, D_MODEL), row), pl.BlockSpec((1, D_MODEL), lambda i: (0, 0)),
                 pl.BlockSpec((1, D_MODEL), lambda i: (0, 0))] + [_DEP_SPEC] * len(deps)
    return pl.pallas_call(
        _with_deps(body, 2 * n + 3, deps), name=name, grid=(S // tm,), in_specs=in_specs,
        out_specs=[pl.BlockSpec((tm, D_MODEL), row)] * 3,
        out_shape=[SDS((S, D_MODEL), f32), SDS((S, D_MODEL), f32), SDS((S, D_MODEL), bf16)],
        compiler_params=_cp("parallel"),
    )(*parts, *ws, x, g, b, *deps)


def ln_bwd(name, dy, z, g, rscale, deps=()):
    S = dy.shape[0]
    tm = _tile(S, 512)

    def body(dy_ref, z_ref, g_ref, dzb_ref, dres_ref, dg_ref, db_ref):
        z = z_ref[...]
        mu = jnp.mean(z, axis=-1, keepdims=True)
        zc = z - mu
        rstd = lax.rsqrt(jnp.mean(zc * zc, axis=-1, keepdims=True) + LN_EPS)
        xhat = zc * rstd
        dyv = dy_ref[...]
        dxh = dyv * g_ref[...]
        m1 = jnp.mean(dxh, axis=-1, keepdims=True)
        m2 = jnp.mean(dxh * xhat, axis=-1, keepdims=True)
        dz = rstd * (dxh - m1 - xhat * m2)
        dzb_ref[...] = (rscale * dz).astype(bf16)
        dres_ref[...] = ALPHA * dz

        @pl.when(pl.program_id(0) == 0)
        def _():
            dg_ref[...] = jnp.zeros_like(dg_ref)
            db_ref[...] = jnp.zeros_like(db_ref)

        dg_ref[...] += jnp.sum(dyv * xhat, axis=0, keepdims=True)
        db_ref[...] += jnp.sum(dyv, axis=0, keepdims=True)

    row = lambda i: (i, 0)
    vec = pl.BlockSpec((1, D_MODEL), lambda i: (0, 0))
    return pl.pallas_call(
        _with_deps(body, 3, deps), name=name, grid=(S // tm,),
        in_specs=[pl.BlockSpec((tm, D_MODEL), row), pl.BlockSpec((tm, D_MODEL), row), vec] + [_DEP_SPEC] * len(deps),
        out_specs=[pl.BlockSpec((tm, D_MODEL), row), pl.BlockSpec((tm, D_MODEL), row), vec, vec],
        out_shape=[SDS((S, D_MODEL), bf16), SDS((S, D_MODEL), f32), SDS((1, D_MODEL), f32), SDS((1, D_MODEL), f32)],
        compiler_params=_cp("arbitrary"),
    )(dy, z, g, *deps)


def ffn_bwd_da(name, drb, w2, l, gate, up):
    S = drb.shape[0]
    tm = _tile(S, 512)
    nh = D_FF // 2

    def body(dr_ref, w_ref, g_ref, u_ref, dh_ref):
        dr = dr_ref[...]
        for j in range(2):
            cols = slice(j * nh, (j + 1) * nh)
            da = _dot_nt(dr, w_ref[0, cols, :])
            g = g_ref[:, cols].astype(f32)
            u = u_ref[:, cols].astype(f32)
            sg = jax.nn.sigmoid(g)
            dh_ref[:, cols] = (da * u * (sg * (1.0 + g * (1.0 - sg)))).astype(bf16)
            dh_ref[:, D_FF + j * nh:D_FF + (j + 1) * nh] = (da * (g * sg)).astype(bf16)

    row = lambda i: (i, 0)
    return pl.pallas_call(
        body, name=name, grid=(S // tm,),
        in_specs=[pl.BlockSpec((tm, D_MODEL), row), pl.BlockSpec((1, D_FF, D_MODEL), lambda i: (l, 0, 0)),
                  pl.BlockSpec((tm, D_FF), row), pl.BlockSpec((tm, D_FF), row)],
        out_specs=pl.BlockSpec((tm, 2 * D_FF), row),
        out_shape=SDS((S, 2 * D_FF), bf16),
        compiler_params=_cp("parallel"),
    )(drb, w2, gate, up)


def ffn_dx(name, dh, w13, l, res):
    S = dh.shape[0]
    ns = w13.shape[3]
    tm = _tile(S, 1024)

    def body(dh_ref, w_ref, r_ref, o_ref):
        @pl.when(pl.program_id(1) == 0)
        def _():
            o_ref[...] = r_ref[...]

        o_ref[...] += _dot_nt(dh_ref[...], w_ref[0, 0])

    return pl.pallas_call(
        body, name=name, grid=(S // tm, N_CHIPS),
        in_specs=[pl.BlockSpec((tm, ns), lambda i, j: (i, j)),
                  pl.BlockSpec((1, 1, D_MODEL, ns), lambda i, j: (l, j, 0, 0)),
                  pl.BlockSpec((tm, D_MODEL), lambda i, j: (i, 0))],
        out_specs=pl.BlockSpec((tm, D_MODEL), lambda i, j: (i, 0)),
        out_shape=SDS((S, D_MODEL), f32),
        compiler_params=_cp("parallel", "arbitrary"),
    )(dh, w13, res)


def mm_nt_res(name, dys, ws, wl, res, out_dtype):
    S = dys[0].shape[0]
    K = ws[0].shape[1]
    tm = _tile(S, 512)
    n = len(dys)

    def body(*refs):
        dy_refs, w_refs = refs[:n], refs[n:2 * n]
        o_ref = refs[-1]
        acc = _dot_nt(dy_refs[0][...], w_refs[0][0])
        for k in range(1, n):
            acc = acc + _dot_nt(dy_refs[k][...], w_refs[k][0])
        if res is not None:
            acc = acc + refs[2 * n][...]
        o_ref[...] = acc.astype(out_dtype)

    row = lambda i: (i, 0)
    in_specs = [pl.BlockSpec((tm, d.shape[1]), row) for d in dys]
    in_specs += [pl.BlockSpec((1,) + w.shape[1:], functools.partial(lambda li, i: (li, 0, 0), li)) for w, li in zip(ws, wl)]
    args = list(dys) + list(ws)
    if res is not None:
        in_specs.append(pl.BlockSpec((tm, K), row))
        args.append(res)
    return pl.pallas_call(
        body, name=name, grid=(S // tm,), in_specs=in_specs,
        out_specs=pl.BlockSpec((tm, K), row), out_shape=SDS((S, K), out_dtype),
        compiler_params=_cp("parallel"),
    )(*args)


def mm_tn(name, x, dy, col_shards=False, deps=()):
    S, K = x.shape
    N = dy.shape[1]
    ts = 512
    while ts * 2 <= min(S, 2048) and S % (ts * 2) == 0 and ts * 2 * K * 2 <= 6 * 2**20:
        ts *= 2
    ts = _tile(S, ts)
    if col_shards:
        tn = N // N_CHIPS
    else:
        tn = N
        while K * tn * 4 > 6 * 2**20 and tn % 256 == 0:
            tn //= 2
    nn = N // tn
    lead = ((0,) if col_shards else ()) + (slice(None), slice(None))

    def body(x_ref, dy_ref, o_ref):
        acc = _dot_tn(x_ref[...].astype(bf16), dy_ref[...].astype(bf16))

        @pl.when(pl.program_id(1) == 0)
        def _():
            o_ref[lead] = acc

        @pl.when(pl.program_id(1) != 0)
        def _():
            o_ref[lead] += acc

    if col_shards:
        out_spec = pl.BlockSpec((1, K, tn), lambda n, s: (n, 0, 0))
        out_shape = SDS((N_CHIPS, K, tn), f32)
    else:
        out_spec = pl.BlockSpec((K, tn), lambda n, s: (0, n))
        out_shape = SDS((K, N), f32)
    return pl.pallas_call(
        _with_deps(body, 2, deps), name=name, grid=(nn, S // ts),
        in_specs=[pl.BlockSpec((ts, K), lambda n, s: (s, 0)), pl.BlockSpec((ts, tn), lambda n, s: (s, n))]
        + [_DEP_SPEC] * len(deps),
        out_specs=out_spec, out_shape=out_shape, compiler_params=_cp("parallel", "arbitrary"),
    )(x, dy, *deps)


def mm_nn_shard(name, x, w, l):
    S, K = x.shape
    ns = w.shape[3]

    def body(x_ref, w_ref, o_ref):
        o_ref[...] = _dot(x_ref[...], w_ref[0, 0]).astype(bf16)

    return pl.pallas_call(
        body, name=name, grid=(N_CHIPS,),
        in_specs=[pl.BlockSpec((S, K), lambda j: (0, 0)), pl.BlockSpec((1, 1, K, ns), lambda j: (l, j, 0, 0))],
        out_specs=pl.BlockSpec((S, ns), lambda j: (0, j)), out_shape=SDS((S, N_CHIPS * ns), bf16),
        compiler_params=_cp("parallel"),
    )(x, w)


def loss_grad(name, y, t):
    S = y.shape[0]
    tm = _tile(S, 512)

    def body(y_ref, t_ref, dy_ref, loss_ref):
        e = y_ref[...] - t_ref[...]
        dy_ref[...] = e * (1.0 / D_MODEL)

        @pl.when(pl.program_id(0) == 0)
        def _():
            loss_ref[...] = jnp.zeros_like(loss_ref)

        loss_ref[...] += jnp.full(loss_ref.shape, (0.5 / D_MODEL) * jnp.sum(e * e), f32)

    row = lambda i: (i, 0)
    return pl.pallas_call(
        body, name=name, grid=(S // tm,),
        in_specs=[pl.BlockSpec((tm, D_MODEL), row)] * 2,
        out_specs=[pl.BlockSpec((tm, D_MODEL), row), pl.BlockSpec((8, 128), lambda i: (0, 0))],
        out_shape=[SDS((S, D_MODEL), f32), SDS((8, 128), f32)],
        compiler_params=_cp("arbitrary"),
    )(y, t)


def _half_sum(t):
    return t + pltpu.roll(t, 64, axis=1)


def mix_pre(name, xb, w_in, wq, wkv, l, gq, gkv, cs):
    S = xb.shape[0]
    tm = _tile(S, 512)
    H = MLA_HEADS
    W_EXT = w_in.shape[2]

    def body(x_ref, win_ref, wq_ref, wkv_ref, gq_ref, gkv_ref, cs_ref,
             u_ref, cq_ref, ckv_ref, cqn_ref, ckvn_ref, q_ref, k_ref, v_ref):
        h = _dot(x_ref[...], win_ref[0])
        u_ref[...] = h[:, :256]
        cq = h[:, 256:512]
        ckv = h[:, 512:640]
        cq_ref[...] = cq
        ckv_ref[...] = ckv
        cqn = (cq * lax.rsqrt(jnp.mean(cq * cq, axis=-1, keepdims=True) + RMS_EPS) * gq_ref[...]).astype(bf16)
        ckvn = (ckv * lax.rsqrt(jnp.mean(ckv * ckv, axis=-1, keepdims=True) + RMS_EPS) * gkv_ref[...]).astype(bf16)
        cqn_ref[...] = cqn
        ckvn_ref[...] = ckvn
        csv = cs_ref[...]
        lane = lax.broadcasted_iota(jnp.int32, (tm, 128), 1)
        kr = jnp.where(lane < 64, _half_sum(h[:, 640:768] * csv), 0.0).astype(bf16)
        kv = _dot(ckvn, wkv_ref[0])
        for hd in range(H):
            qe = _dot(cqn, wq_ref[0, hd])
            q_ref[hd, :, :128] = qe[:, :128].astype(bf16)
            q_ref[hd, :, 128:] = _half_sum(qe[:, 128:] * csv).astype(bf16)
            k_ref[hd, :, :128] = kv[:, 256 * hd:256 * hd + 128].astype(bf16)
            k_ref[hd, :, 128:] = kr
            v_ref[hd] = kv[:, 256 * hd + 128:256 * hd + 256].astype(bf16)

    row = lambda i: (i, 0)
    hrow = lambda i: (0, i, 0)
    return pl.pallas_call(
        body, name=name, grid=(S // tm,),
        in_specs=[pl.BlockSpec((tm, D_MODEL), row),
                  pl.BlockSpec((1, D_MODEL, W_EXT), lambda i: (l, 0, 0)),
                  pl.BlockSpec((1, H, Q_LORA, 256), lambda i: (l, 0, 0, 0)),
                  pl.BlockSpec((1, KV_LORA, H * 256), lambda i: (l, 0, 0)),
                  pl.BlockSpec((1, Q_LORA), lambda i: (0, 0)), pl.BlockSpec((1, KV_LORA), lambda i: (0, 0)),
                  pl.BlockSpec((tm, 128), row)],
        out_specs=[pl.BlockSpec((tm, 256), row), pl.BlockSpec((tm, Q_LORA), row), pl.BlockSpec((tm, KV_LORA), row),
                   pl.BlockSpec((tm, Q_LORA), row), pl.BlockSpec((tm, KV_LORA), row),
                   pl.BlockSpec((H, tm, 256), hrow), pl.BlockSpec((H, tm, 256), hrow), pl.BlockSpec((H, tm, 128), hrow)],
        out_shape=[SDS((S, 256), f32), SDS((S, Q_LORA), f32), SDS((S, KV_LORA), f32),
                   SDS((S, Q_LORA), bf16), SDS((S, KV_LORA), bf16),
                   SDS((H, S, 256), bf16), SDS((H, S, 256), bf16), SDS((H, S, 128), bf16)],
        compiler_params=_cp("parallel"),
    )(xb, w_in, wq, wkv, gq, gkv, cs)


def _group_select(col, a2, a4, a8, a16):
    return jnp.where(col < 64, a2, jnp.where(col < 128, a4, jnp.where(col < 192, a8, a16)))


def pool_fwd(name, u, wbd, scale):
    S = u.shape[0]
    tm = _tile(S, 512)
    hb = tm // HALO

    def body(u_ref, halo_ref, w_ref, s_ref, d_ref, y_ref):
        i = pl.program_id(0)
        cur = u_ref[...]
        halo = jnp.where(i > 0, halo_ref[...], 0.0)
        ext = jnp.concatenate([halo, cur], axis=0)
        s2 = ext + pltpu.roll(ext, 1, axis=0)
        s4 = s2 + pltpu.roll(s2, 2, axis=0)
        s8 = s4 + pltpu.roll(s4, 4, axis=0)
        s16 = s8 + pltpu.roll(s8, 8, axis=0)
        t1 = (i * tm + 1 + lax.broadcasted_iota(jnp.int32, (tm, 1), 0)).astype(f32)
        col = lax.broadcasted_iota(jnp.int32, (tm, 256), 1)
        m = _group_select(col, s2[HALO:] / jnp.minimum(t1, 2.0), s4[HALO:] / jnp.minimum(t1, 4.0),
                          s8[HALO:] / jnp.minimum(t1, 8.0), s16[HALO:] / jnp.minimum(t1, 16.0))
        d = (m - cur).astype(bf16)
        d_ref[...] = d
        y_ref[...] = (_dot(d, w_ref[...]) * s_ref[...]).astype(bf16)

    row = lambda i: (i, 0)
    return pl.pallas_call(
        body, name=name, grid=(S // tm,),
        in_specs=[pl.BlockSpec((tm, 256), row), pl.BlockSpec((HALO, 256), lambda i: (jnp.maximum(i * hb - 1, 0), 0)),
                  pl.BlockSpec((256, 256), lambda i: (0, 0)), pl.BlockSpec((1, 256), lambda i: (0, 0))],
        out_specs=[pl.BlockSpec((tm, 256), row)] * 2,
        out_shape=[SDS((S, 256), bf16), SDS((S, 256), bf16)],
        compiler_params=_cp("parallel"),
    )(u, u, wbd, scale)


def pool_bwd(name, dyp, d, wbd, scale):
    S = dyp.shape[0]
    tm = _tile(S, 512)
    hb = tm // HALO
    n_ext = tm + HALO

    def fwd_sum(e, steps):
        k = 1
        for _ in range(steps):
            e = e + pltpu.roll(e, n_ext - k, axis=0)
            k *= 2
        return e

    def body(dy_ref, halo_ref, d_ref, w_ref, s_ref, du_ref, dyw_ref, ds_ref):
        i = pl.program_id(0)
        sc = s_ref[...]
        w = w_ref[...]
        cur = dy_ref[...].astype(f32)
        halo = jnp.where(i < pl.num_programs(0) - 1, halo_ref[...].astype(f32), 0.0)
        dyw = jnp.concatenate([cur, halo], axis=0) * sc
        dyw_ref[...] = dyw[:tm].astype(bf16)
        dd = _dot_nt(dyw.astype(bf16), w)
        t1 = (i * tm + 1 + lax.broadcasted_iota(jnp.int32, (n_ext, 1), 0)).astype(f32)
        f2 = fwd_sum(dd / jnp.minimum(t1, 2.0), 1)
        f4 = fwd_sum(dd / jnp.minimum(t1, 4.0), 2)
        f8 = fwd_sum(dd / jnp.minimum(t1, 8.0), 3)
        f16 = fwd_sum(dd / jnp.minimum(t1, 16.0), 4)
        col = lax.broadcasted_iota(jnp.int32, (tm, 256), 1)
        du_ref[...] = (_group_select(col, f2[:tm], f4[:tm], f8[:tm], f16[:tm]) - dd[:tm]).astype(bf16)

        @pl.when(i == 0)
        def _():
            ds_ref[...] = jnp.zeros_like(ds_ref)

        ds_ref[...] += jnp.sum(cur * _dot(d_ref[...], w), axis=0, keepdims=True)

    row = lambda i: (i, 0)
    nhb = S // HALO
    return pl.pallas_call(
        body, name=name, grid=(S // tm,),
        in_specs=[pl.BlockSpec((tm, 256), row), pl.BlockSpec((HALO, 256), lambda i: (jnp.minimum((i + 1) * hb, nhb - 1), 0)),
                  pl.BlockSpec((tm, 256), row), pl.BlockSpec((256, 256), lambda i: (0, 0)),
                  pl.BlockSpec((1, 256), lambda i: (0, 0))],
        out_specs=[pl.BlockSpec((tm, 256), row), pl.BlockSpec((tm, 256), row), pl.BlockSpec((1, 256), lambda i: (0, 0))],
        out_shape=[SDS((S, 256), bf16), SDS((S, 256), bf16), SDS((1, 256), f32)],
        compiler_params=_cp("arbitrary"),
    )(dyp, dyp, d, wbd, scale)


def _diag_mask(tq):
    rc = lax.broadcasted_iota(jnp.int32, (tq, 1), 0) // 64
    cc = lax.broadcasted_iota(jnp.int32, (1, tq), 1) // 64
    return rc >= cc


MLA_SCALE_LOG2 = MLA_SCALE * math.log2(math.e)


def mla_attn_fwd(name, q, k, v):
    H, S, _ = q.shape
    tq = _tile(S, 1024)
    nq = S // tq
    pairs = [(i, j) for i in range(nq) for j in range(i + 1)]
    it = jnp.asarray([p_[0] for p_ in pairs], jnp.int32)
    jt = jnp.asarray([p_[1] for p_ in pairs], jnp.int32)

    def body(it_ref, jt_ref, q_ref, k_ref, v_ref, o_ref, lse_ref, m_sc, l_sc, acc_sc):
        t = pl.program_id(1)
        i, j = it_ref[t], jt_ref[t]

        @pl.when(j == 0)
        def _():
            m_sc[...] = jnp.full_like(m_sc, NEG_INF)
            l_sc[...] = jnp.zeros_like(l_sc)
            acc_sc[...] = jnp.zeros_like(acc_sc)

        def step(masked):
            s = _dot_nt(q_ref[0], k_ref[0])
            if masked:
                s = jnp.where(_diag_mask(tq), s, NEG_INF)
            m_prev = m_sc[...]
            m_new = jnp.maximum(m_prev, jnp.max(s, axis=-1, keepdims=True))
            p = jnp.exp2((s - jnp.tile(m_new, (1, tq // 128))) * MLA_SCALE_LOG2)
            a = jnp.exp2((m_prev - m_new) * MLA_SCALE_LOG2)
            l_sc[...] = a * l_sc[...] + jnp.sum(p, axis=-1, keepdims=True)
            acc_sc[...] = a * acc_sc[...] + _dot(p.astype(bf16), v_ref[0])
            m_sc[...] = m_new

        @pl.when(j < i)
        def _():
            step(False)

        @pl.when(j == i)
        def _():
            step(True)
            o_ref[...] = (acc_sc[...] / l_sc[...]).astype(bf16)
            lse_ref[0] = m_sc[...] * MLA_SCALE_LOG2 + jnp.log2(l_sc[...])

    return pl.pallas_call(
        body, name=name,
        grid_spec=pltpu.PrefetchScalarGridSpec(
            num_scalar_prefetch=2, grid=(H, len(pairs)),
            in_specs=[pl.BlockSpec((1, tq, 256), lambda h, t, it_, jt_: (h, it_[t], 0)),
                      pl.BlockSpec((1, tq, 256), lambda h, t, it_, jt_: (h, jt_[t], 0)),
                      pl.BlockSpec((1, tq, 128), lambda h, t, it_, jt_: (h, jt_[t], 0))],
            out_specs=[pl.BlockSpec((tq, 128), lambda h, t, it_, jt_: (it_[t], h)),
                       pl.BlockSpec((1, tq, 128), lambda h, t, it_, jt_: (h, it_[t], 0))],
            scratch_shapes=[pltpu.VMEM((tq, 128), f32), pltpu.VMEM((tq, 128), f32), pltpu.VMEM((tq, 128), f32)]),
        out_shape=[SDS((S, H * 128), bf16), SDS((H, S, 128), f32)],
        compiler_params=_cp("parallel", "arbitrary"),
    )(it, jt, q, k, v)


def mla_attn_bwd(name, q, k, v, o, do, lse):
    H, S, _ = q.shape
    tq = _tile(S, 1024)
    nq = S // tq
    pairs = [(i, j) for j in range(nq) for i in range(j, nq)]
    it = jnp.asarray([p_[0] for p_ in pairs], jnp.int32)
    jt = jnp.asarray([p_[1] for p_ in pairs], jnp.int32)
    n_pairs = len(pairs)

    def body(it_ref, jt_ref, q_ref, k_ref, v_ref, o_ref, do_ref, lse_ref, dq_ref, dk_ref, dv_ref, dk_sc, dv_sc):
        t = pl.program_id(1)
        i, j = it_ref[t], jt_ref[t]

        @pl.when(t == 0)
        def _():
            dq_ref[...] = jnp.zeros_like(dq_ref)

        @pl.when(i == j)
        def _():
            dk_sc[...] = jnp.zeros_like(dk_sc)
            dv_sc[...] = jnp.zeros_like(dv_sc)

        def step(masked):
            qv, kv_, dov = q_ref[0], k_ref[0], do_ref[...]
            s = _dot_nt(qv, kv_)
            if masked:
                s = jnp.where(_diag_mask(tq), s, NEG_INF)
            p = jnp.exp2(s * MLA_SCALE_LOG2 - jnp.tile(lse_ref[0], (1, tq // 128)))
            dv_sc[...] += _dot_tn(p.astype(bf16), dov)
            dp = _dot_nt(dov, v_ref[0])
            delta = jnp.sum(dov.astype(f32) * o_ref[...].astype(f32), axis=-1, keepdims=True)
            ds = (p * (dp - delta)).astype(bf16)
            dk_sc[...] += _dot_tn(ds, qv)
            rows = pl.ds(pl.multiple_of(i * tq, tq), tq)
            dq_ref[0, rows, :] += _dot(ds, kv_)

        @pl.when(i > j)
        def _():
            step(False)

        @pl.when(i == j)
        def _():
            step(True)

        @pl.when(i == nq - 1)
        def _():
            dk_ref[0] = dk_sc[...] * MLA_SCALE
            dv_ref[0] = dv_sc[...]

        @pl.when(t == n_pairs - 1)
        def _():
            dq_ref[...] = dq_ref[...] * MLA_SCALE

    qi = lambda h, t, it_, jt_: (h, it_[t], 0)
    kj = lambda h, t, it_, jt_: (h, jt_[t], 0)
    oi = lambda h, t, it_, jt_: (it_[t], h)
    return pl.pallas_call(
        body, name=name,
        grid_spec=pltpu.PrefetchScalarGridSpec(
            num_scalar_prefetch=2, grid=(H, n_pairs),
            in_specs=[pl.BlockSpec((1, tq, 256), qi), pl.BlockSpec((1, tq, 256), kj), pl.BlockSpec((1, tq, 128), kj),
                      pl.BlockSpec((tq, 128), oi), pl.BlockSpec((tq, 128), oi), pl.BlockSpec((1, tq, 128), qi)],
            out_specs=[pl.BlockSpec((1, S, 256), lambda h, t, it_, jt_: (h, 0, 0)), pl.BlockSpec((1, tq, 256), kj),
                       pl.BlockSpec((1, tq, 128), kj)],
            scratch_shapes=[pltpu.VMEM((tq, 256), f32), pltpu.VMEM((tq, 128), f32)]),
        out_shape=[SDS((H, S, 256), f32), SDS((H, S, 256), f32), SDS((H, S, 128), f32)],
        compiler_params=_cp("parallel", "arbitrary"),
    )(it, jt, q, k, v, o, do, lse)


def mix_post_bwd(name, dq, dk, dv, wq, wkv, l, cq, ckv, gq, gkv, cs):
    H, S, _ = dq.shape
    tm = _tile(S, 512)

    def rms_bwd(dyn, c, g):
        r = lax.rsqrt(jnp.mean(c * c, axis=-1, keepdims=True) + RMS_EPS)
        ch = c * r
        dyg = dyn * g
        dc = r * (dyg - ch * jnp.mean(dyg * ch, axis=-1, keepdims=True))
        return dc, jnp.sum(dyn * ch, axis=0, keepdims=True)

    def body(dq_ref, dk_ref, dv_ref, wq_ref, wkv_ref, cq_ref, ckv_ref, gq_ref, gkv_ref, cs_ref,
             dqe_ref, dkv_ref, dh_ref, dgq_ref, dgkv_ref):
        csv = cs_ref[...]
        lane = lax.broadcasted_iota(jnp.int32, (tm, 128), 1)
        dcqn = jnp.zeros((tm, Q_LORA), f32)
        dkr = jnp.zeros((tm, 128), f32)
        for hd in range(H):
            dqh = dq_ref[hd]
            dqe = jnp.concatenate([dqh[:, :128], _half_sum(dqh[:, 128:]) * csv], axis=1).astype(bf16)
            dqe_ref[:, 256 * hd:256 * hd + 256] = dqe
            dcqn = dcqn + _dot_nt(dqe, wq_ref[0, hd])
            dkh = dk_ref[hd]
            dkv_ref[:, 256 * hd:256 * hd + 128] = dkh[:, :128].astype(bf16)
            dkv_ref[:, 256 * hd + 128:256 * hd + 256] = dv_ref[hd].astype(bf16)
            dkr = dkr + dkh[:, 128:]
        dckvn = _dot_nt(dkv_ref[...], wkv_ref[0])
        dblk = _half_sum(jnp.where(lane < 64, dkr, 0.0)) * csv
        dcq, dgq = rms_bwd(dcqn, cq_ref[...], gq_ref[...])
        dckv, dgkv = rms_bwd(dckvn, ckv_ref[...], gkv_ref[...])
        dh_ref[:, :256] = dcq.astype(bf16)
        dh_ref[:, 256:384] = dckv.astype(bf16)
        dh_ref[:, 384:] = dblk.astype(bf16)

        @pl.when(pl.program_id(0) == 0)
        def _():
            dgq_ref[...] = jnp.zeros_like(dgq_ref)
            dgkv_ref[...] = jnp.zeros_like(dgkv_ref)

        dgq_ref[...] += dgq
        dgkv_ref[...] += dgkv

    row = lambda i: (i, 0)
    hrow = lambda i: (0, i, 0)
    return pl.pallas_call(
        body, name=name, grid=(S // tm,),
        in_specs=[pl.BlockSpec((H, tm, 256), hrow), pl.BlockSpec((H, tm, 256), hrow), pl.BlockSpec((H, tm, 128), hrow),
                  pl.BlockSpec((1, H, Q_LORA, 256), lambda i: (l, 0, 0, 0)),
                  pl.BlockSpec((1, KV_LORA, H * 256), lambda i: (l, 0, 0)),
                  pl.BlockSpec((tm, Q_LORA), row), pl.BlockSpec((tm, KV_LORA), row),
                  pl.BlockSpec((1, Q_LORA), lambda i: (0, 0)), pl.BlockSpec((1, KV_LORA), lambda i: (0, 0)),
                  pl.BlockSpec((tm, 128), row)],
        out_specs=[pl.BlockSpec((tm, H * 256), row), pl.BlockSpec((tm, H * 256), row), pl.BlockSpec((tm, 512), row),
                   pl.BlockSpec((1, Q_LORA), lambda i: (0, 0)), pl.BlockSpec((1, KV_LORA), lambda i: (0, 0))],
        out_shape=[SDS((S, H * 256), bf16), SDS((S, H * 256), bf16), SDS((S, 512), bf16),
                   SDS((1, Q_LORA), f32), SDS((1, KV_LORA), f32)],
        compiler_params=_cp("arbitrary"),
    )(dq, dk, dv, wq, wkv, cq, ckv, gq, gkv, cs)


def _cross_probs(qb, kv_ref, hd):
    cols = slice(hd * MEM_HEAD_DIM, (hd + 1) * MEM_HEAD_DIM)
    s = _dot_nt(qb[:, cols], kv_ref[:, cols]) * MEM_SCALE
    e = jnp.exp(s - jnp.max(s, axis=-1, keepdims=True))
    return e / jnp.sum(e, axis=-1, keepdims=True)


def cross_fwd(name, xb, xf, wq, wo, l, kv, g, b):
    S = xb.shape[0]
    tm = _tile(S, 512)
    M = kv.shape[0]

    def body(x_ref, xf_ref, wq_ref, wo_ref, k_ref, v_ref, g_ref, b_ref, q_ref, o_ref, z_ref, y_ref, yb_ref):
        qb = _dot(x_ref[...], wq_ref[0]).astype(bf16)
        q_ref[...] = qb
        for hd in range(MEM_HEADS):
            cols = slice(hd * MEM_HEAD_DIM, (hd + 1) * MEM_HEAD_DIM)
            p = _cross_probs(qb, k_ref, hd)
            o_ref[:, cols] = _dot(p.astype(bf16), v_ref[:, cols]).astype(bf16)
        z = ALPHA * xf_ref[...] + _dot(o_ref[...], wo_ref[0])
        mu = jnp.mean(z, axis=-1, keepdims=True)
        zc = z - mu
        var = jnp.mean(zc * zc, axis=-1, keepdims=True)
        y = zc * lax.rsqrt(var + LN_EPS) * g_ref[...] + b_ref[...]
        z_ref[...] = z
        y_ref[...] = y
        yb_ref[...] = y.astype(bf16)

    row = lambda i: (i, 0)
    wspec = pl.BlockSpec((1, D_MODEL, D_MODEL), lambda i: (l, 0, 0))
    vec = pl.BlockSpec((1, D_MODEL), lambda i: (0, 0))
    blk = pl.BlockSpec((tm, D_MODEL), row)
    return pl.pallas_call(
        body, name=name, grid=(S // tm,),
        in_specs=[blk, blk, wspec, wspec, pl.BlockSpec((M, D_MODEL), lambda i: (0, 0)),
                  pl.BlockSpec((M, D_MODEL), lambda i: (0, 1)), vec, vec],
        out_specs=[blk] * 5,
        out_shape=[SDS((S, D_MODEL), bf16), SDS((S, D_MODEL), bf16), SDS((S, D_MODEL), f32), SDS((S, D_MODEL), f32),
                   SDS((S, D_MODEL), bf16)],
        compiler_params=_cp("parallel"),
    )(xb, xf, wq, wo, kv, kv, g, b)


def cross_bwd(name, dzb, wo, l, qb, kv):
    S = dzb.shape[0]
    tm = _tile(S, 512)
    M = kv.shape[0]

    def body(dz_ref, wo_ref, q_ref, k_ref, v_ref, dq_ref, dkv_ref):
        @pl.when(pl.program_id(0) == 0)
        def _():
            dkv_ref[...] = jnp.zeros_like(dkv_ref)

        do = _dot_nt(dz_ref[...], wo_ref[0]).astype(bf16)
        qv = q_ref[...]
        for hd in range(MEM_HEADS):
            cols = slice(hd * MEM_HEAD_DIM, (hd + 1) * MEM_HEAD_DIM)
            vcols = slice(D_MODEL + hd * MEM_HEAD_DIM, D_MODEL + (hd + 1) * MEM_HEAD_DIM)
            p = _cross_probs(qv, k_ref, hd)
            doh = do[:, cols]
            dkv_ref[:, vcols] += _dot_tn(p.astype(bf16), doh)
            dp = _dot_nt(doh, v_ref[:, cols])
            ds = (p * (dp - jnp.sum(dp * p, axis=-1, keepdims=True)) * MEM_SCALE).astype(bf16)
            dq_ref[:, cols] = _dot(ds, k_ref[:, cols]).astype(bf16)
            dkv_ref[:, cols] += _dot_tn(ds, qv[:, cols])

    row = lambda i: (i, 0)
    blk = pl.BlockSpec((tm, D_MODEL), row)
    return pl.pallas_call(
        body, name=name, grid=(S // tm,),
        in_specs=[blk, pl.BlockSpec((1, D_MODEL, D_MODEL), lambda i: (l, 0, 0)), blk,
                  pl.BlockSpec((M, D_MODEL), lambda i: (0, 0)), pl.BlockSpec((M, D_MODEL), lambda i: (0, 1))],
        out_specs=[blk, pl.BlockSpec((M, 2 * D_MODEL), lambda i: (0, 0))],
        out_shape=[SDS((S, D_MODEL), bf16), SDS((M, 2 * D_MODEL), f32)],
        compiler_params=_cp("arbitrary"),
    )(dzb, wo, qb, kv, kv)


def adamw(name, w, g, m, v):
    shape = w.shape
    cols = shape[-1]
    rows = math.prod(shape[:-1])
    tr = _row_tile(rows, cols, target=2**20)
    c1 = 1.0 - ADAM_B1 ** ADAM_STEP
    c2 = 1.0 - ADAM_B2 ** ADAM_STEP

    def body(w_ref, g_ref, m_ref, v_ref, d_ref, nm_ref, nv_ref):
        gv = g_ref[...]
        nm = ADAM_B1 * m_ref[...] + (1.0 - ADAM_B1) * gv
        nv = ADAM_B2 * v_ref[...] + (1.0 - ADAM_B2) * (gv * gv)
        d_ref[...] = -ADAM_LR * ((nm / c1) / (jnp.sqrt(nv / c2) + ADAM_EPS) + ADAM_WD * w_ref[...])
        nm_ref[...] = nm
        nv_ref[...] = nv

    blk = pl.BlockSpec((tr, cols), lambda i: (i, 0))
    flat = SDS((rows, cols), f32)
    outs = pl.pallas_call(
        body, name=name, grid=(rows // tr,), in_specs=[blk] * 4, out_specs=[blk] * 3, out_shape=[flat] * 3,
        compiler_params=_cp("parallel"),
    )(*[a.reshape(rows, cols) for a in (w, g, m, v)])
    return [o.reshape(shape) for o in outs]


def _me():
    return lax.axis_index("x"), lax.axis_index("y"), lax.axis_index("c")


def _other_chips(x, y):
    return [(1 - x, y), (x, 1 - y), (1 - x, 1 - y)]


def pair_share(name, sums):
    n = len(sums)

    def body(*refs):
        out_refs = refs[n:2 * n]
        send_sems, recv_sems = refs[2 * n:]
        x, y, c = _me()
        sibling = (x, y, 1 - c)
        cps = [pltpu.make_async_remote_copy(src_ref=out_refs[a].at[c], dst_ref=out_refs[a].at[c], send_sem=send_sems.at[a],
                                            recv_sem=recv_sems.at[a], device_id=sibling, device_id_type=MESH)
               for a in range(n)]
        for cp in cps:
            cp.start()
        for cp in cps:
            cp.wait_send()
        for a in range(n):
            land = out_refs[a].at[1 - c]
            pltpu.make_async_remote_copy(src_ref=land, dst_ref=land, send_sem=send_sems.at[a], recv_sem=recv_sems.at[a],
                                         device_id=sibling, device_id_type=MESH).wait_recv()

    any_spec = pl.BlockSpec(memory_space=pl.ANY)
    return pl.pallas_call(
        body, name=name, in_specs=[any_spec] * n, out_specs=[any_spec] * n,
        out_shape=[SDS(s.shape, f32) for s in sums], input_output_aliases={a: a for a in range(n)},
        scratch_shapes=[pltpu.SemaphoreType.DMA((n,)), pltpu.SemaphoreType.DMA((n,))],
    )(*sums)


def allsum_small(name, v, deps=()):
    R = v.shape[0]

    def body(v_ref, o_ref, all_ref, send_sems, recv_sems, local_sem):
        x, y, c = _me()
        me, sibling = (x, y, c), (x, y, 1 - c)
        chips = _other_chips(x, y)

        def rows(px, py, pc):
            return all_ref.at[4 * px + 2 * py + pc]

        def copy(k, block, to, src=None):
            return pltpu.make_async_remote_copy(
                src_ref=rows(*block) if src is None else src, dst_ref=rows(*block),
                send_sem=send_sems.at[k], recv_sem=recv_sems.at[k], device_id=to, device_id_type=MESH)

        mine = pltpu.make_async_copy(v_ref, rows(*me), local_sem)
        mine.start()
        first = [copy(0, me, sibling, src=v_ref)]
        first += [copy(1 + j, me, (*chip, c), src=v_ref) for j, chip in enumerate(chips)]
        for cp in first:
            cp.start()
        passed = [copy(4 + j, (*chip, c), sibling) for j, chip in enumerate(chips)]
        for j, chip in enumerate(chips):
            copy(1 + j, (*chip, c), me).wait_recv()
            passed[j].start()
        copy(0, sibling, me).wait_recv()
        for j, chip in enumerate(chips):
            copy(4 + j, (*chip, 1 - c), me).wait_recv()
        for cp in first + passed:
            cp.wait_send()
        mine.wait()
        acc = all_ref[0]
        for d in range(1, 8):
            acc = acc + all_ref[d]
        o_ref[...] = acc

    return pl.pallas_call(
        _with_deps(body, 1, deps), name=name,
        in_specs=[pl.BlockSpec(memory_space=pltpu.VMEM)] + [_DEP_SPEC] * len(deps),
        out_specs=pl.BlockSpec(memory_space=pltpu.VMEM),
        out_shape=SDS((R, 128), f32),
        scratch_shapes=[pltpu.VMEM((8, R, 128), f32), pltpu.SemaphoreType.DMA((7,)), pltpu.SemaphoreType.DMA((7,)),
                        pltpu.SemaphoreType.DMA],
        compiler_params=pltpu.CompilerParams(vmem_limit_bytes=V7X_VMEM_LIMIT),
    )(v, *deps)


def _swap_half(r):
    return jnp.concatenate([-r[..., 32:], r[..., :32]], axis=-1)


def _unswap_add(p, qg):
    return p + jnp.concatenate([qg[..., 32:], -qg[..., :32]], axis=-1)


def _block_diag(pw):
    L = pw.shape[0]
    out = jnp.zeros((L, 256, 256), pw.dtype)
    for gi in range(4):
        out = out.at[:, 64 * gi:64 * gi + 64, 64 * gi:64 * gi + 64].set(pw[:, gi])
    return out


def _to_col_shards(w):
    *lead, K, N = w.shape
    nl = len(lead)
    return w.reshape(*lead, K, N_CHIPS, N // N_CHIPS).transpose(*range(nl), nl + 1, nl, nl + 2)


def _from_col_shards(w):
    *lead, C, K, n = w.shape
    nl = len(lead)
    return w.transpose(*range(nl), nl + 1, nl, nl + 2).reshape(*lead, K, C * n)


def _step_serial_comm(x, mem, positions, ln_g, ln_b, ffn1_w13, ffn1_w2, w_in, pool_w, pool_scale, q_norm_g, w_uq, kv_norm_g, w_ukv, w_out, mem_wq, mem_wkv, mem_wo, ffn2_w13, ffn2_w2, loss_target, m_ln_g, m_ln_b, m_ffn1_w13, m_ffn1_w2, m_w_in, m_pool_w, m_pool_scale, m_q_norm_g, m_w_uq, m_kv_norm_g, m_w_ukv, m_w_out, m_mem_wq, m_mem_wkv, m_mem_wo, m_ffn2_w13, m_ffn2_w2, v_ln_g, v_ln_b, v_ffn1_w13, v_ffn1_w2, v_w_in, v_pool_w, v_pool_scale, v_q_norm_g, v_w_uq, v_kv_norm_g, v_w_ukv, v_w_out, v_mem_wq, v_mem_wkv, v_mem_wo, v_ffn2_w13, v_ffn2_w2):
    L = DEPTH
    S = x.shape[1]
    qx, qy, _ = _me()
    chip = 2 * qx + qy

    big = [ffn1_w13, ffn1_w2, w_in, w_uq, w_ukv, w_out, mem_wq, mem_wkv, mem_wo, ffn2_w13, ffn2_w2]
    (g_f1w13, g_f1w2, g_win, g_wuq, g_wukv, g_wout, g_mwq, g_mwkv, g_mwo, g_f2w13, g_f2w2) = gather_weights(
        [w.astype(bf16) for w in big])
    f1w2 = g_f1w2.reshape(L, D_FF, D_MODEL)
    f2w2 = g_f2w2.reshape(L, D_FF, D_MODEL)
    win = g_win.reshape(L, D_MODEL, D_IN)
    win_ext = jnp.concatenate([win, _swap_half(win[..., D_IN - QK_ROPE:])], axis=-1)
    wuq = _from_col_shards(g_wuq).reshape(L, Q_LORA, MLA_HEADS, QK_NOPE + QK_ROPE)
    wq_ext = jnp.concatenate([wuq, _swap_half(wuq[..., QK_NOPE:])], axis=-1).transpose(0, 2, 1, 3)
    wukv = _from_col_shards(g_wukv)
    wout = g_wout.reshape(L, D_MODEL, D_MODEL)
    wout_pool, wout_mla = wout[:, :POOL_WIDTH], wout[:, POOL_WIDTH:]
    mwq = g_mwq.reshape(L, D_MODEL, D_MODEL)
    mwo = g_mwo.reshape(L, D_MODEL, D_MODEL)
    wbd = _block_diag(pool_w.astype(bf16))

    ln_pad = jnp.zeros((2, L, 4, N_CHIPS, D_MODEL // N_CHIPS), f32)
    ln_pad = lax.dynamic_update_slice(ln_pad, jnp.stack([ln_g, ln_b])[:, :, :, None, :], (0, 0, 0, chip, 0))
    ln_full = allsum_small("allsum_ln", ln_pad.reshape(-1, 128)) * 0.5
    ln_full = ln_full.reshape(2, L, 4, D_MODEL)
    lng, lnb = ln_full[0], ln_full[1]

    half = QK_ROPE // 2
    inv_freq = ROPE_BASE ** (-jnp.arange(half, dtype=f32) / half)
    ang = positions[0].astype(f32)[:, None] * inv_freq
    cos, sin = jnp.cos(ang), jnp.sin(ang)
    cs = jnp.concatenate([cos, cos, sin, sin], axis=-1)

    memb = mem[0].astype(bf16)
    xf = x[0]
    xb = xf.astype(bf16)
    vec = lambda a: a.reshape(1, -1)

    saved = []
    for l in range(L):
        sv = {}
        sv["x0b"] = xb
        gate, up, act = ffn_up(f"ffn1_up_{l}", xb, g_f1w13, l)
        z1, x1f, x1b = proj_res_ln(f"ffn1_down_{l}", [act], [f1w2], [l], xf, vec(lng[l, 0]), vec(lnb[l, 0]), 0.5)
        sv.update(gate1=gate, up1=up, act1=act, z1=z1, x1b=x1b)
        u, cq, ckv, cqn, ckvn, q, k, v = mix_pre(f"mix_pre_{l}", x1b, win_ext, wq_ext, wukv, l,
                                                   vec(q_norm_g[l]), vec(kv_norm_g[l]), cs)
        dpool, ypool = pool_fwd(f"pool_fwd_{l}", u, wbd[l], vec(pool_scale[l]))
        o, lse = mla_attn_fwd(f"mla_fwd_{l}", q, k, v)
        z2, x2f, x2b = proj_res_ln(f"mix_out_{l}", [ypool, o], [wout_pool, wout_mla], [l, l], x1f,
                                   vec(lng[l, 1]), vec(lnb[l, 1]), 1.0)
        sv.update(cq=cq, ckv=ckv, cqn=cqn, ckvn=ckvn, q=q, k=k, v=v, dpool=dpool, ypool=ypool, o=o, lse=lse, z2=z2, x2b=x2b)
        kvm = mm_nn_shard(f"mem_kv_{l}", memb, g_mwkv, l)
        cq_, co_, z3, x3f, x3b = cross_fwd(f"cross_fwd_{l}", x2b, x2f, mwq, mwo, l, kvm, vec(lng[l, 2]), vec(lnb[l, 2]))
        sv.update(kvm=kvm, crq=cq_, cro=co_, z3=z3, x3b=x3b)
        gate, up, act = ffn_up(f"ffn2_up_{l}", x3b, g_f2w13, l)
        z4, xf, xb = proj_res_ln(f"ffn2_down_{l}", [act], [f2w2], [l], x3f, vec(lng[l, 3]), vec(lnb[l, 3]), 0.5)
        sv.update(gate2=gate, up2=up, act2=act, z4=z4)
        saved.append(sv)

    dy, loss_blk = loss_grad("loss_grad", xf, loss_target[0])
    loss = lax.psum(loss_blk[0, 0], ("x", "y", "c"))

    G = dict(f1w13=None, f1w2=None, mwq=None, mwkv=None, mwo=None, f2w13=None, f2w2=None)
    small = {k_: [None] * L for k_ in ("win", "wuq", "wukv", "wout", "pool_w", "pool_scale", "gq", "gkv", "lng", "lnb")}
    for l in reversed(range(L)):
        sv = saved[l]
        dlg, dlb = [None] * 4, [None] * 4
        dzb, dres, dlg[3], dlb[3] = ln_bwd(f"ln4_bwd_{l}", dy, sv["z4"], vec(lng[l, 3]), 0.5)
        dh = ffn_bwd_da(f"ffn2_bwd_da_{l}", dzb, f2w2, l, sv["gate2"], sv["up2"])
        G["f2w2"] = mm_tn(f"ffn2_dw2_{l}", sv["act2"], dzb, "nat", l, G["f2w2"])
        G["f2w13"] = mm_tn(f"ffn2_dw13_{l}", sv["x3b"], dh, "shard", l, G["f2w13"])
        dy = ffn_dx(f"ffn2_dx_{l}", dh, g_f2w13, l, dres)
        dzb, dres, dlg[2], dlb[2] = ln_bwd(f"ln3_bwd_{l}", dy, sv["z3"], vec(lng[l, 2]), 1.0)
        dqc, dkvm = cross_bwd(f"cross_bwd_{l}", dzb, mwo, l, sv["crq"], sv["kvm"])
        G["mwo"] = mm_tn(f"cross_dwo_{l}", sv["cro"], dzb, "nat", l, G["mwo"])
        G["mwq"] = mm_tn(f"cross_dwq_{l}", sv["x2b"], dqc, "nat", l, G["mwq"])
        G["mwkv"] = mm_tn(f"cross_dwkv_{l}", memb, dkvm, "shard", l, G["mwkv"])
        dy = mm_nt_res(f"cross_dx_{l}", [dqc], [mwq], [l], dres, f32)
        dzb, dres, dlg[1], dlb[1] = ln_bwd(f"ln2_bwd_{l}", dy, sv["z2"], vec(lng[l, 1]), 1.0)
        dyp = mm_nt_res(f"mix_dpool_{l}", [dzb], [wout_pool], [l], None, bf16)
        do = mm_nt_res(f"mix_do_{l}", [dzb], [wout_mla], [l], None, bf16)
        dwo_p = mm_tn(f"mix_dwout_pool_{l}", sv["ypool"], dzb)
        dwo_m = mm_tn(f"mix_dwout_mla_{l}", sv["o"], dzb)
        small["wout"][l] = jnp.concatenate([dwo_p, dwo_m], axis=0)
        dq, dk, dv = mla_attn_bwd(f"mla_bwd_{l}", sv["q"], sv["k"], sv["v"], sv["o"], do, sv["lse"])
        dqe, dkv, dh_rest, dgq, dgkv = mix_post_bwd(f"mix_post_bwd_{l}", dq, dk, dv, wq_ext, wukv, l, sv["cq"], sv["ckv"],
                                                     vec(q_norm_g[l]), vec(kv_norm_g[l]), cs)
        du, dyw, dscale = pool_bwd(f"pool_bwd_{l}", dyp, sv["dpool"], wbd[l], vec(pool_scale[l]))
        dwq_e = mm_tn(f"mix_dwuq_{l}", sv["cqn"], dqe).reshape(Q_LORA, MLA_HEADS, 256)
        small["wuq"][l] = jnp.concatenate(
            [dwq_e[..., :QK_NOPE], _unswap_add(dwq_e[..., QK_NOPE:QK_NOPE + QK_ROPE], dwq_e[..., QK_NOPE + QK_ROPE:])],
            axis=-1).reshape(Q_LORA, MLA_HEADS * (QK_NOPE + QK_ROPE))
        small["wukv"][l] = mm_tn(f"mix_dwukv_{l}", sv["ckvn"], dkv)
        dwbd = mm_tn(f"pool_dw_{l}", sv["dpool"], dyw)
        small["pool_w"][l] = jnp.stack([dwbd[64 * gi:64 * gi + 64, 64 * gi:64 * gi + 64] for gi in range(4)])
        small["pool_scale"][l], small["gq"][l], small["gkv"][l] = dscale[0], dgq[0], dgkv[0]
        dh_ext = jnp.concatenate([du, dh_rest], axis=1)
        dwin_e = mm_tn(f"mix_dwin_{l}", sv["x1b"], dh_ext)
        small["win"][l] = jnp.concatenate(
            [dwin_e[:, :D_IN - QK_ROPE], _unswap_add(dwin_e[:, D_IN - QK_ROPE:D_IN], dwin_e[:, D_IN:])], axis=-1)
        dy = mm_nt_res(f"mix_dx_{l}", [dh_ext], [win_ext], [l], dres, f32)
        dzb, dres, dlg[0], dlb[0] = ln_bwd(f"ln1_bwd_{l}", dy, sv["z1"], vec(lng[l, 0]), 0.5)
        dh = ffn_bwd_da(f"ffn1_bwd_da_{l}", dzb, f1w2, l, sv["gate1"], sv["up1"])
        G["f1w2"] = mm_tn(f"ffn1_dw2_{l}", sv["act1"], dzb, "nat", l, G["f1w2"])
        G["f1w13"] = mm_tn(f"ffn1_dw13_{l}", sv["x0b"], dh, "shard", l, G["f1w13"])
        dy = ffn_dx(f"ffn1_dx_{l}", dh, g_f1w13, l, dres)
        small["lng"][l] = jnp.concatenate(dlg, axis=0)
        small["lnb"][l] = jnp.concatenate(dlb, axis=0)
    grad_x = dy[None]

    row_shards = lambda a, K: a.reshape(L, N_CHIPS, K // N_CHIPS, a.shape[-1])
    g_list = [G["f1w13"], row_shards(G["f1w2"], D_FF),
              jnp.stack(small["win"]).reshape(L, N_CHIPS, D_MODEL // N_CHIPS, D_IN),
              _to_col_shards(jnp.stack(small["wuq"])), _to_col_shards(jnp.stack(small["wukv"])),
              jnp.stack(small["wout"]).reshape(L, N_CHIPS, D_MODEL // N_CHIPS, D_MODEL),
              row_shards(G["mwq"], D_MODEL), G["mwkv"], row_shards(G["mwo"], D_MODEL),
              G["f2w13"], row_shards(G["f2w2"], D_FF)]
    big_grads = reduce_grads(g_list)

    rep = [jnp.stack(small["pool_w"]).reshape(-1), jnp.stack(small["pool_scale"]).reshape(-1),
           jnp.stack(small["gq"]).reshape(-1), jnp.stack(small["gkv"]).reshape(-1),
           jnp.stack(small["lng"]).reshape(-1), jnp.stack(small["lnb"]).reshape(-1)]
    sizes = [r.shape[0] for r in rep]
    packed = jnp.concatenate(rep)
    pad = (-packed.shape[0]) % 1024
    tot = allsum_small("allsum_small_grads", jnp.pad(packed, (0, pad)).reshape(-1, 128)).reshape(-1)
    offs = [0]
    for s_ in sizes:
        offs.append(offs[-1] + s_)
    parts = [tot[offs[i]:offs[i + 1]] for i in range(len(sizes))]
    g_pool_w = parts[0].reshape(pool_w.shape)
    g_pool_scale = parts[1].reshape(pool_scale.shape)
    g_gq = parts[2].reshape(q_norm_g.shape)
    g_gkv = parts[3].reshape(kv_norm_g.shape)
    shard_cols = lambda a: lax.dynamic_slice_in_dim(a.reshape(L, 4, D_MODEL), chip * (D_MODEL // N_CHIPS),
                                                    D_MODEL // N_CHIPS, axis=2)
    g_lng, g_lnb = shard_cols(parts[4]), shard_cols(parts[5])

    out_names = ("lng", "lnb", "f1w13", "f1w2", "win", "pool_w", "pool_scale", "gq", "wuq", "gkv", "wukv", "wout", "mwq",
                 "mwkv", "mwo", "f2w13", "f2w2")
    big.update(lng=g_lng, lnb=g_lnb, pool_w=g_pool_w, pool_scale=g_pool_scale, gq=g_gq, gkv=g_gkv)
    late = ("f1w13", "f1w2")
    ws = [ln_g, ln_b, ffn1_w13, ffn1_w2, w_in, pool_w, pool_scale, q_norm_g, w_uq, kv_norm_g, w_ukv, w_out, mem_wq,
          mem_wkv, mem_wo, ffn2_w13, ffn2_w2]
    ms = [m_ln_g, m_ln_b, m_ffn1_w13, m_ffn1_w2, m_w_in, m_pool_w, m_pool_scale, m_q_norm_g, m_w_uq, m_kv_norm_g, m_w_ukv,
          m_w_out, m_mem_wq, m_mem_wkv, m_mem_wo, m_ffn2_w13, m_ffn2_w2]
    vs = [v_ln_g, v_ln_b, v_ffn1_w13, v_ffn1_w2, v_w_in, v_pool_w, v_pool_scale, v_q_norm_g, v_w_uq, v_kv_norm_g, v_w_ukv,
          v_w_out, v_mem_wq, v_mem_wkv, v_mem_wo, v_ffn2_w13, v_ffn2_w2]
    res = {}

    def update(a):
        res[a] = adamw(f"adamw_{a}", ws[a], big[out_names[a]].reshape(ws[a].shape), ms[a], vs[a])

    for a, n in enumerate(out_names):
        if n not in late:
            update(a)
    sums_b = red_end("b0", st_cb, 0, [sums1[n] for n in late], tuple(r[0] for r in res.values()))
    big.update(zip(late, pair_share("pair_share_b", sums_b)))
    for a, n in enumerate(out_names):
        if n in late:
            update(a)
    order = range(len(out_names))
    grads = [big[n].reshape(w_.shape) for n, w_ in zip(out_names, ws)]
    return (loss, grad_x, *grads, *[res[a][0] for a in order], *[res[a][1] for a in order], *[res[a][2] for a in order])


_HBM_SPEC = pl.BlockSpec(memory_space=pltpu.HBM)
_SEM_SPEC = pl.BlockSpec(memory_space=pltpu.SEMAPHORE)
_ANY_SPEC = pl.BlockSpec(memory_space=pl.ANY)
_DATAFLOW = pltpu.SideEffectType.DATAFLOW_SIDE_EFFECTING


def _split_call(name, body_fn, bufs, sems_in, sems_out_sizes, after):
    nb, ni, no = len(bufs), len(sems_in), len(sems_out_sizes)
    afters = () if after is None else tuple(after) if isinstance(after, (tuple, list)) else (after,)

    def body(*refs):
        k = nb + ni + len(afters)
        body_fn(refs[:nb], refs[nb:nb + ni], refs[k:k + no])
        refs[-1][...] = jnp.zeros((8, 128), f32)

    outs = pl.pallas_call(
        body, name=name,
        in_specs=[_HBM_SPEC] * nb + [_SEM_SPEC] * ni + [_ANY_SPEC] * len(afters),
        out_specs=[_SEM_SPEC] * no + [_HBM_SPEC] * nb + [pl.BlockSpec(memory_space=pltpu.VMEM)],
        out_shape=[pltpu.SemaphoreType.DMA((s,)) for s in sems_out_sizes]
        + [pltpu.HBM(b.shape, b.dtype) for b in bufs] + [SDS((8, 128), f32)],
        input_output_aliases={i: no + i for i in range(nb)},
        compiler_params=pltpu.CompilerParams(has_side_effects=_DATAFLOW),
    )(*[pltpu.with_memory_space_constraint(b, pltpu.HBM) for b in bufs], *sems_in, *afters)
    return list(outs[no:no + nb]), list(outs[:no]), outs[-1]


def _rcopy(src, dst, ssem, rsem, to):
    return pltpu.make_async_remote_copy(src_ref=src, dst_ref=dst, send_sem=ssem, recv_sem=rsem, device_id=to,
                                        device_id_type=MESH)


def gather_start(name, groups, after):
    flat = [b for bufs, _ in groups for b in bufs]
    sizes = [3 * len(bufs) for bufs, _ in groups for _ in range(2)]

    def body_fn(b_in, s_in, s_out):
        x, y, c = _me()
        q = 2 * x + y
        chips = _other_chips(x, y)
        pos = 0
        for gi, (bufs, owner) in enumerate(groups):
            refs = b_in[pos:pos + len(bufs)]
            pos += len(bufs)

            @pl.when(c == owner)
            def _(refs=refs, send=s_out[2 * gi], recv=s_out[2 * gi + 1]):
                for a, r in enumerate(refs):
                    for k, (cx, cy) in enumerate(chips):
                        _rcopy(r.at[q], r.at[q], send.at[3 * a + k], recv.at[3 * a + k], (cx, cy, c)).start()

    outs, sems, token = _split_call(name, body_fn, flat, [], sizes, after)
    res, pos = [], 0
    for gi, (bufs, owner) in enumerate(groups):
        res.append((outs[pos:pos + len(bufs)], sems[2 * gi], sems[2 * gi + 1], owner))
        pos += len(bufs)
    return res, token


def gather_forward(name, grp, after):
    bufs, send, recv, owner = grp
    n3 = 3 * len(bufs)

    def body_fn(b_in, s_in, s_out):
        x, y, c = _me()
        q = 2 * x + y
        sibling = (x, y, 1 - c)
        chips = _other_chips(x, y)

        @pl.when(c == owner)
        def _():
            for a, r in enumerate(b_in):
                for k, (cx, cy) in enumerate(chips):
                    i = 3 * a + k
                    land = r.at[2 * cx + cy]
                    _rcopy(r.at[q], r.at[q], s_in[0].at[i], s_in[1].at[i], (cx, cy, c)).wait_send()
                    _rcopy(land, land, s_in[0].at[i], s_in[1].at[i], (cx, cy, c)).wait_recv()
                    _rcopy(land, land, s_out[0].at[i], s_out[1].at[i], sibling).start()

    outs, sems, token = _split_call(name, body_fn, bufs, [send, recv], [n3, n3], after)
    return (outs, sems[0], sems[1], owner), token


def gather_finish(name, grp, after):
    bufs, fsend, frecv, owner = grp

    def body_fn(b_in, s_in, s_out):
        x, y, c = _me()
        sibling = (x, y, 1 - c)
        chips = _other_chips(x, y)

        def each(wait):
            for a, r in enumerate(b_in):
                for k, (cx, cy) in enumerate(chips):
                    land = r.at[2 * cx + cy]
                    wait(_rcopy(land, land, s_in[0].at[3 * a + k], s_in[1].at[3 * a + k], sibling))

        @pl.when(c == owner)
        def _():
            each(lambda cp: cp.wait_send())

        @pl.when(c != owner)
        def _():
            each(lambda cp: cp.wait_recv())

    outs, _, _ = _split_call(name, body_fn, bufs, [fsend, frecv], [], after)
    return outs


def pair_send_start(name, gs, owner, after):
    n = len(gs)
    lands = [lax.empty(g.shape, g.dtype) for g in gs]

    def body_fn(b_in, s_in, s_out):
        x, y, c = _me()

        @pl.when(c == 1 - owner)
        def _():
            for a in range(n):
                _rcopy(b_in[a], b_in[n + a], s_out[0].at[a], s_out[1].at[a], (x, y, owner)).start()

    outs, sems, token = _split_call(name, body_fn, list(gs) + lands, [], [n, n], after)
    return (outs[:n], outs[n:], sems[0], sems[1], owner), token


def pair_send_wait(name, st, after):
    gs, lands, send, recv, owner = st
    n = len(gs)

    def body_fn(b_in, s_in, s_out):
        x, y, c = _me()

        @pl.when(c == 1 - owner)
        def _():
            for a in range(n):
                _rcopy(b_in[a], b_in[n + a], s_in[0].at[a], s_in[1].at[a], (x, y, owner)).wait_send()

        @pl.when(c == owner)
        def _():
            for a in range(n):
                _rcopy(b_in[a], b_in[n + a], s_in[0].at[a], s_in[1].at[a], (x, y, 1 - owner)).wait_recv()

    outs, _, _ = _split_call(name, body_fn, list(gs) + list(lands), [send, recv], [], after)
    return outs[:n], outs[n:]


def chip_exchange_start(name, psums, owner, after):
    n = len(psums)
    lands = [lax.empty((3,) + p.shape[1:], p.dtype) for p in psums]

    def body_fn(b_in, s_in, s_out):
        x, y, c = _me()
        chips = _other_chips(x, y)

        @pl.when(c == owner)
        def _():
            for a in range(n):
                for k, (cx, cy) in enumerate(chips):
                    _rcopy(b_in[a].at[2 * cx + cy], b_in[n + a].at[k], s_out[0].at[3 * a + k], s_out[1].at[3 * a + k],
                           (cx, cy, c)).start()

    outs, sems, token = _split_call(name, body_fn, list(psums) + lands, [], [3 * n, 3 * n], after)
    return (outs[:n], outs[n:], sems[0], sems[1], owner), token


def chip_exchange_wait(name, st, after):
    psums, lands, send, recv, owner = st
    n = len(psums)

    def body_fn(b_in, s_in, s_out):
        x, y, c = _me()
        chips = _other_chips(x, y)

        @pl.when(c == owner)
        def _():
            for a in range(n):
                for k, (cx, cy) in enumerate(chips):
                    cp = _rcopy(b_in[a].at[2 * cx + cy], b_in[n + a].at[k], s_in[0].at[3 * a + k], s_in[1].at[3 * a + k],
                                (cx, cy, c))
                    cp.wait_send()
                    cp.wait_recv()

    outs, _, _ = _split_call(name, body_fn, list(psums) + list(lands), [send, recv], [], after)
    return outs[:n], outs[n:]


def pair_sum(name, g, recv):
    shape = g.shape
    cols = shape[-1]
    rows = math.prod(shape[:-1])
    tr = _row_tile(rows, cols)

    def body(g_ref, r_ref, o_ref):
        o_ref[...] = (g_ref[...] + r_ref[...]).astype(bf16)

    blk = pl.BlockSpec((tr, cols), lambda i: (i, 0))
    out = pl.pallas_call(
        body, name=name, grid=(rows // tr,), in_specs=[blk, blk], out_specs=blk, out_shape=SDS((rows, cols), bf16),
        compiler_params=_cp("parallel"),
    )(g.reshape(rows, cols), recv.reshape(rows, cols))
    return out.reshape(shape)


def chip_sum(name, psum, recv, q_arr, layer, prev):
    shard = psum.shape[1:]
    cols = shard[-1]
    rows = math.prod(shard[:-1])
    tr = _row_tile(rows, cols)

    def body(q_ref, p_ref, r_ref, *rest):
        rest[-1][0] = ((p_ref[0].astype(f32) + r_ref[0].astype(f32)) + r_ref[1].astype(f32)) + r_ref[2].astype(f32)

    in_specs = [pl.BlockSpec((1, tr, cols), lambda i, q_ref: (q_ref[0], i, 0)),
                pl.BlockSpec((3, tr, cols), lambda i, q_ref: (0, i, 0))]
    args = [q_arr, psum.reshape(N_CHIPS, rows, cols), recv.reshape(3, rows, cols)]
    aliases = {}
    if prev is not None:
        in_specs.append(pl.BlockSpec(memory_space=pl.ANY))
        args.append(prev.reshape(DEPTH, rows, cols))
        aliases = {3: 0}
    out = pl.pallas_call(
        body, name=name,
        grid_spec=pltpu.PrefetchScalarGridSpec(
            num_scalar_prefetch=1, grid=(rows // tr,), in_specs=in_specs,
            out_specs=pl.BlockSpec((1, tr, cols), lambda i, q_ref: (layer, i, 0))),
        out_shape=SDS((DEPTH, rows, cols), f32), input_output_aliases=aliases, compiler_params=_cp("parallel"),
    )(*args)
    return out.reshape((DEPTH,) + shard)


W_NAMES = ("f1w13", "f1w2", "win", "wuq", "wukv", "wout", "mwq", "mwkv", "mwo", "f2w13", "f2w2")
MIX_NAMES = ("win", "wuq", "wukv")
MID_NAMES = ("wout", "mwq", "mwkv", "mwo")
FFN2_NAMES = ("f2w13", "f2w2")


def kernel(x, mem, positions, ln_g, ln_b, ffn1_w13, ffn1_w2, w_in, pool_w, pool_scale, q_norm_g, w_uq, kv_norm_g, w_ukv, w_out, mem_wq, mem_wkv, mem_wo, ffn2_w13, ffn2_w2, loss_target, m_ln_g, m_ln_b, m_ffn1_w13, m_ffn1_w2, m_w_in, m_pool_w, m_pool_scale, m_q_norm_g, m_w_uq, m_kv_norm_g, m_w_ukv, m_w_out, m_mem_wq, m_mem_wkv, m_mem_wo, m_ffn2_w13, m_ffn2_w2, v_ln_g, v_ln_b, v_ffn1_w13, v_ffn1_w2, v_w_in, v_pool_w, v_pool_scale, v_q_norm_g, v_w_uq, v_kv_norm_g, v_w_ukv, v_w_out, v_mem_wq, v_mem_wkv, v_mem_wo, v_ffn2_w13, v_ffn2_w2):
    L = DEPTH
    qx, qy, _ = _me()
    chip = 2 * qx + qy
    vec = lambda a: a.reshape(1, -1)

    shards = dict(zip(W_NAMES, (ffn1_w13, ffn1_w2, w_in, w_uq, w_ukv, w_out, mem_wq, mem_wkv, mem_wo, ffn2_w13, ffn2_w2)))

    def place(sh, slot):
        return lax.dynamic_update_slice(jnp.zeros((N_CHIPS,) + sh.shape, bf16), sh.astype(bf16)[None],
                                        (slot,) + (0,) * sh.ndim)

    first = ("f1w13", "f1w2")
    bufs = [dict(), dict()]
    for n in first:
        bufs[0][n] = place(shards[n][0], chip)
    gw = [dict(), dict()]
    (g0,), tok = gather_start("gather_a_start", [([bufs[0][n] for n in first], 0)], None)
    chip_then = chip + tok[0, 0].astype(jnp.int32)
    for l in range(L):
        for n in W_NAMES:
            if n not in bufs[l]:
                bufs[l][n] = place(shards[n][l], chip_then)
    others = tuple(bufs[l][n] for l in range(L) for n in W_NAMES if (l, n) not in ((0, first[0]), (0, first[1])))
    g0, tok = gather_forward("gather_a_forward", g0, others)

    ln_pad = jnp.zeros((2, L, 4, N_CHIPS, D_MODEL // N_CHIPS), f32)
    ln_pad = lax.dynamic_update_slice(ln_pad, jnp.stack([ln_g, ln_b])[:, :, :, None, :], (0, 0, 0, chip, 0))
    ln_sum = allsum_small("allsum_ln", ln_pad.reshape(-1, 128), (tok,))
    ln_full = (ln_sum * 0.5).reshape(2, L, 4, D_MODEL)
    lng, lnb = ln_full[0], ln_full[1]

    gw[0]["f1w13"], gw[0]["f1w2"] = gather_finish("gather_a_finish", g0, ln_sum)
    (g_mix, g_mid, g_ffn2, g_l1), tok_b = gather_start(
        "gather_b_start",
        [([bufs[0][n] for n in MIX_NAMES], 0), ([bufs[0][n] for n in MID_NAMES], 0), ([bufs[0][n] for n in FFN2_NAMES], 0),
         ([bufs[1][n] for n in W_NAMES], 1)], ln_sum)

    half = QK_ROPE // 2
    inv_freq = ROPE_BASE ** (-jnp.arange(half, dtype=f32) / half)
    ang = positions[0].astype(f32)[:, None] * inv_freq
    cos, sin = jnp.cos(ang), jnp.sin(ang)
    cs = jnp.concatenate([cos, cos, sin, sin], axis=-1)

    memb = mem[0].astype(bf16)
    xf = x[0]
    xb = xf.astype(bf16)
    dep = (tok_b,)

    saved, W = [], [None, None]
    for l in range(L):
        sv = {}
        if l == 1:
            gl1 = gather_finish("gather_l1_finish", g_l1, xb)
            gw[1] = dict(zip(W_NAMES, gl1))
        sv["x0b"] = xb
        f1w13 = gw[l]["f1w13"][None]
        gate, up, act = ffn_up(f"ffn1_up_{l}", xb, f1w13, 0, dep)
        dep = ()
        if l == 0:
            g_mix, _ = gather_forward("gather_mix_forward", g_mix, act)
        z1, x1f, x1b = proj_res_ln(f"ffn1_down_{l}", [act], [gw[l]["f1w2"].reshape(1, D_FF, D_MODEL)], [0], xf,
                                   vec(lng[l, 0]), vec(lnb[l, 0]), 0.5)
        sv.update(gate1=gate, up1=up, act1=act, z1=z1, x1b=x1b)
        if l == 0:
            gw[0].update(zip(MIX_NAMES, gather_finish("gather_mix_finish", g_mix, x1b)))
            g_mid, _ = gather_forward("gather_mid_forward", g_mid, x1b)
        win = gw[l]["win"].reshape(D_MODEL, D_IN)
        win_ext = jnp.concatenate([win, _swap_half(win[:, D_IN - QK_ROPE:])], axis=-1)[None]
        wuq = _from_col_shards(gw[l]["wuq"]).reshape(Q_LORA, MLA_HEADS, QK_NOPE + QK_ROPE)
        wq_ext = jnp.concatenate([wuq, _swap_half(wuq[..., QK_NOPE:])], axis=-1).transpose(1, 0, 2)[None]
        wukv = _from_col_shards(gw[l]["wukv"])[None]
        wbd = _block_diag(pool_w[l][None].astype(bf16))[0]
        u, cq, ckv, cqn, ckvn, q, k, v = mix_pre(f"mix_pre_{l}", x1b, win_ext, wq_ext, wukv, 0,
                                                   vec(q_norm_g[l]), vec(kv_norm_g[l]), cs)
        dpool, ypool = pool_fwd(f"pool_fwd_{l}", u, wbd, vec(pool_scale[l]))
        o, lse = mla_attn_fwd(f"mla_fwd_{l}", q, k, v)
        if l == 0:
            gw[0].update(zip(MID_NAMES, gather_finish("gather_mid_finish", g_mid, o)))
            g_ffn2, tok_f = gather_forward("gather_ffn2_forward", g_ffn2, o)
            g_l1, tok_l = gather_forward("gather_l1_forward", g_l1, o)
            dep = (tok_f, tok_l)
        wout = gw[l]["wout"].reshape(D_MODEL, D_MODEL)
        wout_pool, wout_mla = wout[None, :POOL_WIDTH], wout[None, POOL_WIDTH:]
        mwq = gw[l]["mwq"].reshape(1, D_MODEL, D_MODEL)
        mwo = gw[l]["mwo"].reshape(1, D_MODEL, D_MODEL)
        mwkv = gw[l]["mwkv"][None]
        z2, x2f, x2b = proj_res_ln(f"mix_out_{l}", [ypool, o], [wout_pool, wout_mla], [0, 0], x1f,
                                   vec(lng[l, 1]), vec(lnb[l, 1]), 1.0, dep)
        dep = ()
        sv.update(cq=cq, ckv=ckv, cqn=cqn, ckvn=ckvn, q=q, k=k, v=v, dpool=dpool, ypool=ypool, o=o, lse=lse, z2=z2, x2b=x2b)
        kvm = mm_nn_shard(f"mem_kv_{l}", memb, mwkv, 0)
        cq_, co_, z3, x3f, x3b = cross_fwd(f"cross_fwd_{l}", x2b, x2f, mwq, mwo, 0, kvm, vec(lng[l, 2]), vec(lnb[l, 2]))
        sv.update(kvm=kvm, crq=cq_, cro=co_, z3=z3, x3b=x3b)
        if l == 0:
            gw[0].update(zip(FFN2_NAMES, gather_finish("gather_ffn2_finish", g_ffn2, x3b)))
        f2w13 = gw[l]["f2w13"][None]
        f2w2 = gw[l]["f2w2"].reshape(1, D_FF, D_MODEL)
        gate, up, act = ffn_up(f"ffn2_up_{l}", x3b, f2w13, 0)
        z4, xf, xb = proj_res_ln(f"ffn2_down_{l}", [act], [f2w2], [0], x3f, vec(lng[l, 3]), vec(lnb[l, 3]), 0.5)
        sv.update(gate2=gate, up2=up, act2=act, z4=z4)
        W[l] = dict(f1w13=f1w13, f1w2=gw[l]["f1w2"].reshape(1, D_FF, D_MODEL), win_ext=win_ext, wq_ext=wq_ext, wukv=wukv,
                    wbd=wbd, wout_pool=wout_pool, wout_mla=wout_mla, mwq=mwq, mwo=mwo, f2w13=f2w13, f2w2=f2w2)
        saved.append(sv)

    dy, loss_blk = loss_grad("loss_grad", xf, loss_target[0])
    loss = lax.psum(loss_blk[0, 0], ("x", "y", "c"))

    row_shards = lambda a: a.reshape(N_CHIPS, a.shape[0] // N_CHIPS, a.shape[1])
    small = {k_: [None] * L for k_ in ("pool_w", "pool_scale", "gq", "gkv", "lng", "lnb")}
    q_arr = jnp.reshape(chip, (1,)).astype(jnp.int32)
    rest_names = [n for n in W_NAMES if n not in ("f1w13", "f1w2")]

    def red_begin(tag, gs, owner):
        return pair_send_start(f"pair_send_start_{tag}", gs, owner, None)

    def red_mid(tag, st, owner, after):
        gs_, lands_ = pair_send_wait(f"pair_send_wait_{tag}", st, after)
        ps = [pair_sum(f"pair_sum_{tag}_{a}", g_, r_) for a, (g_, r_) in enumerate(zip(gs_, lands_))]
        return chip_exchange_start(f"chip_exchange_start_{tag}", ps, owner, None)

    def red_end(tag, st, layer, prevs, after):
        ps, lands_ = chip_exchange_wait(f"chip_exchange_wait_{tag}", st, after)
        return [chip_sum(f"chip_sum_{tag}_{a}", p_, r_, q_arr, layer, s_)
                for a, (p_, r_, s_) in enumerate(zip(ps, lands_, prevs))]

    st_p1 = st_c1 = st_pa = st_ca = None
    for l in reversed(range(L)):
        sv, w = saved[l], W[l]
        g = {}
        dlg, dlb = [None] * 4, [None] * 4
        dzb, dres, dlg[3], dlb[3] = ln_bwd(f"ln4_bwd_{l}", dy, sv["z4"], vec(lng[l, 3]), 0.5, dep)
        dep = ()
        dh = ffn_bwd_da(f"ffn2_bwd_da_{l}", dzb, w["f2w2"], 0, sv["gate2"], sv["up2"])
        g["f2w2"] = row_shards(mm_tn(f"ffn2_dw2_{l}", sv["act2"], dzb))
        g["f2w13"] = mm_tn(f"ffn2_dw13_{l}", sv["x3b"], dh, True)
        dy = ffn_dx(f"ffn2_dx_{l}", dh, w["f2w13"], 0, dres)
        if l == 0:
            st_c1, tok = red_mid("l1", st_p1, 1, dy)
            dep = (tok, g["f2w2"], g["f2w13"])
        dzb, dres, dlg[2], dlb[2] = ln_bwd(f"ln3_bwd_{l}", dy, sv["z3"], vec(lng[l, 2]), 1.0, dep)
        dep = ()
        dqc, dkvm = cross_bwd(f"cross_bwd_{l}", dzb, w["mwo"], 0, sv["crq"], sv["kvm"])
        g["mwo"] = row_shards(mm_tn(f"cross_dwo_{l}", sv["cro"], dzb))
        g["mwq"] = row_shards(mm_tn(f"cross_dwq_{l}", sv["x2b"], dqc))
        g["mwkv"] = mm_tn(f"cross_dwkv_{l}", memb, dkvm, True)
        dy = mm_nt_res(f"cross_dx_{l}", [dqc], [w["mwq"]], [0], dres, f32)
        dzb, dres, dlg[1], dlb[1] = ln_bwd(f"ln2_bwd_{l}", dy, sv["z2"], vec(lng[l, 1]), 1.0)
        dyp = mm_nt_res(f"mix_dpool_{l}", [dzb], [w["wout_pool"]], [0], None, bf16)
        do = mm_nt_res(f"mix_do_{l}", [dzb], [w["wout_mla"]], [0], None, bf16)
        dwo_p = mm_tn(f"mix_dwout_pool_{l}", sv["ypool"], dzb)
        dwo_m = mm_tn(f"mix_dwout_mla_{l}", sv["o"], dzb)
        g["wout"] = row_shards(jnp.concatenate([dwo_p, dwo_m], axis=0))
        dq, dk, dv = mla_attn_bwd(f"mla_bwd_{l}", sv["q"], sv["k"], sv["v"], sv["o"], do, sv["lse"])
        dqe, dkv, dh_rest, dgq, dgkv = mix_post_bwd(f"mix_post_bwd_{l}", dq, dk, dv, w["wq_ext"], w["wukv"], 0, sv["cq"],
                                                     sv["ckv"], vec(q_norm_g[l]), vec(kv_norm_g[l]), cs)
        du, dyw, dscale = pool_bwd(f"pool_bwd_{l}", dyp, sv["dpool"], w["wbd"], vec(pool_scale[l]))
        dwq_e = mm_tn(f"mix_dwuq_{l}", sv["cqn"], dqe).reshape(Q_LORA, MLA_HEADS, 256)
        g["wuq"] = _to_col_shards(jnp.concatenate(
            [dwq_e[..., :QK_NOPE], _unswap_add(dwq_e[..., QK_NOPE:QK_NOPE + QK_ROPE], dwq_e[..., QK_NOPE + QK_ROPE:])],
            axis=-1).reshape(Q_LORA, MLA_HEADS * (QK_NOPE + QK_ROPE)))
        g["wukv"] = _to_col_shards(mm_tn(f"mix_dwukv_{l}", sv["ckvn"], dkv))
        dwbd = mm_tn(f"pool_dw_{l}", sv["dpool"], dyw)
        small["pool_w"][l] = jnp.stack([dwbd[64 * gi:64 * gi + 64, 64 * gi:64 * gi + 64] for gi in range(4)])
        small["pool_scale"][l], small["gq"][l], small["gkv"][l] = dscale[0], dgq[0], dgkv[0]
        dh_ext = jnp.concatenate([du, dh_rest], axis=1)
        dwin_e = mm_tn(f"mix_dwin_{l}", sv["x1b"], dh_ext)
        g["win"] = row_shards(jnp.concatenate(
            [dwin_e[:, :D_IN - QK_ROPE], _unswap_add(dwin_e[:, D_IN - QK_ROPE:D_IN], dwin_e[:, D_IN:])], axis=-1))
        dy = mm_nt_res(f"mix_dx_{l}", [dh_ext], [w["win_ext"]], [0], dres, f32)
        if l == 0:
            st_pa, tok = red_begin("a0", [g[n] for n in rest_names], 0)
            dep = (tok,)
        dzb, dres, dlg[0], dlb[0] = ln_bwd(f"ln1_bwd_{l}", dy, sv["z1"], vec(lng[l, 0]), 0.5, dep)
        dep = ()
        dh = ffn_bwd_da(f"ffn1_bwd_da_{l}", dzb, w["f1w2"], 0, sv["gate1"], sv["up1"])
        dy = ffn_dx(f"ffn1_dx_{l}", dh, w["f1w13"], 0, dres)
        if l == 0:
            st_ca, tok = red_mid("a0", st_pa, 0, dy)
            dep = (tok,)
        g["f1w2"] = row_shards(mm_tn(f"ffn1_dw2_{l}", sv["act1"], dzb, False, dep))
        g["f1w13"] = mm_tn(f"ffn1_dw13_{l}", sv["x0b"], dh, True, dep)
        dep = ()
        small["lng"][l] = jnp.concatenate(dlg, axis=0)
        small["lnb"][l] = jnp.concatenate(dlb, axis=0)
        if l == 1:
            st_p1, tok = red_begin("l1", [g[n] for n in W_NAMES], 1)
            dep = (tok,)
    grad_x = dy[None]

    st_pb, _ = red_begin("b0", [g["f1w13"], g["f1w2"]], 0)
    sums1 = dict(zip(W_NAMES, red_end("l1", st_c1, 1, [None] * len(W_NAMES), g["f1w13"])))
    st_cb, _ = red_mid("b0", st_pb, 0, sums1["f1w13"])
    sums0 = red_end("a0", st_ca, 0, [sums1[n] for n in rest_names], sums1["f1w2"])
    big = dict(zip(rest_names, pair_share("pair_share_a", sums0)))

    rep = [jnp.stack(small["pool_w"]).reshape(-1), jnp.stack(small["pool_scale"]).reshape(-1),
           jnp.stack(small["gq"]).reshape(-1), jnp.stack(small["gkv"]).reshape(-1),
           jnp.stack(small["lng"]).reshape(-1), jnp.stack(small["lnb"]).reshape(-1)]
    sizes = [r.shape[0] for r in rep]
    packed = jnp.concatenate(rep)
    pad = (-packed.shape[0]) % 1024
    tot = allsum_small("allsum_small_grads", jnp.pad(packed, (0, pad)).reshape(-1, 128)).reshape(-1)
    offs = [0]
    for s_ in sizes:
        offs.append(offs[-1] + s_)
    parts = [tot[offs[i]:offs[i + 1]] for i in range(len(sizes))]
    g_pool_w = parts[0].reshape(pool_w.shape)
    g_pool_scale = parts[1].reshape(pool_scale.shape)
    g_gq = parts[2].reshape(q_norm_g.shape)
    g_gkv = parts[3].reshape(kv_norm_g.shape)
    shard_cols = lambda a: lax.dynamic_slice_in_dim(a.reshape(L, 4, D_MODEL), chip * (D_MODEL // N_CHIPS),
                                                    D_MODEL // N_CHIPS, axis=2)
    g_lng, g_lnb = shard_cols(parts[4]), shard_cols(parts[5])

    out_names = ("lng", "lnb", "f1w13", "f1w2", "win", "pool_w", "pool_scale", "gq", "wuq", "gkv", "wukv", "wout", "mwq",
                 "mwkv", "mwo", "f2w13", "f2w2")
    big.update(lng=g_lng, lnb=g_lnb, pool_w=g_pool_w, pool_scale=g_pool_scale, gq=g_gq, gkv=g_gkv)
    late = ("f1w13", "f1w2")
    ws = [ln_g, ln_b, ffn1_w13, ffn1_w2, w_in, pool_w, pool_scale, q_norm_g, w_uq, kv_norm_g, w_ukv, w_out, mem_wq,
          mem_wkv, mem_wo, ffn2_w13, ffn2_w2]
    ms = [m_ln_g, m_ln_b, m_ffn1_w13, m_ffn1_w2, m_w_in, m_pool_w, m_pool_scale, m_q_norm_g, m_w_uq, m_kv_norm_g, m_w_ukv,
          m_w_out, m_mem_wq, m_mem_wkv, m_mem_wo, m_ffn2_w13, m_ffn2_w2]
    vs = [v_ln_g, v_ln_b, v_ffn1_w13, v_ffn1_w2, v_w_in, v_pool_w, v_pool_scale, v_q_norm_g, v_w_uq, v_kv_norm_g, v_w_ukv,
          v_w_out, v_mem_wq, v_mem_wkv, v_mem_wo, v_ffn2_w13, v_ffn2_w2]
    res = {}

    def update(a):
        res[a] = adamw(f"adamw_{a}", ws[a], big[out_names[a]].reshape(ws[a].shape), ms[a], vs[a])

    for a, n in enumerate(out_names):
        if n not in late:
            update(a)
    sums_b = red_end("b0", st_cb, 0, [sums1[n] for n in late], tuple(r[0] for r in res.values()))
    big.update(zip(late, pair_share("pair_share_b", sums_b)))
    for a, n in enumerate(out_names):
        if n in late:
            update(a)
    order = range(len(out_names))
    grads = [big[n].reshape(w_.shape) for n, w_ in zip(out_names, ws)]
    return (loss, grad_x, *grads, *[res[a][0] for a in order], *[res[a][1] for a in order], *[res[a][2] for a in order])
```

```python
import functools
import math

import jax
import jax.numpy as jnp
from jax import lax
from jax.experimental import pallas as pl
from jax.experimental.pallas import tpu as pltpu

f32 = jnp.float32
bf16 = jnp.bfloat16
SDS = jax.ShapeDtypeStruct
MESH = pl.DeviceIdType.MESH

D_MODEL = 1024
DEPTH = 2
N_MEM = 256
MEM_HEADS = 4
MEM_HEAD_DIM = D_MODEL // MEM_HEADS
POOL_WINDOWS = (2, 4, 8, 16)
POOL_WIDTH = 256
POOL_GROUP = 64
QK_NOPE = 128
QK_ROPE = 64
V_HEAD = 128
MLA_HEADS = 6
Q_LORA = 256
KV_LORA = 128
ROPE_BASE = 10000.0
D_FF = 2816
D_IN = POOL_WIDTH + Q_LORA + KV_LORA + QK_ROPE
ALPHA = (2 * DEPTH) ** 0.25
LN_EPS = 1e-5
RMS_EPS = 1e-6
NEG_INF = -1e30
MLA_SCALE = (QK_NOPE + QK_ROPE) ** -0.5
MEM_SCALE = MEM_HEAD_DIM ** -0.5
ADAM_LR = 0.001
ADAM_B1 = 0.9
ADAM_B2 = 0.999
ADAM_EPS = 1e-08
ADAM_WD = 0.01
ADAM_STEP = 10

N_CHIPS = 4
V7X_VMEM_LIMIT = 56 * 2**20
HALO = 16

_NT = (((1,), (1,)), ((), ()))
_TN = (((0,), (0,)), ((), ()))


def _dot(a, b):
    return jnp.dot(a, b, preferred_element_type=f32)


def _dot_nt(a, b):
    return lax.dot_general(a, b, _NT, preferred_element_type=f32)


def _dot_tn(a, b):
    return lax.dot_general(a, b, _TN, preferred_element_type=f32)


def _cp(*sem):
    return pltpu.CompilerParams(dimension_semantics=sem if sem else None, vmem_limit_bytes=V7X_VMEM_LIMIT)


_DEP_SPEC = pl.BlockSpec(memory_space=pl.ANY)


def _with_deps(body, n_in, deps):
    nd = len(deps)
    if not nd:
        return body

    def wrapped(*refs):
        return body(*refs[:n_in], *refs[n_in + nd:])

    return wrapped


def _tile(n, t):
    t = min(n, t)
    assert n % t == 0, (n, t)
    return t


def _row_tile(rows, cols, itemsize=4, target=2 * 2**20):
    best = None
    for t in range(16, rows + 1, 16):
        if rows % t == 0 and t * cols * itemsize <= target:
            best = t
    return best if best is not None else rows


def ffn_up(name, xb, w13, l, deps=()):
    S = xb.shape[0]
    ns = w13.shape[3]
    tm = _tile(S, 512)

    def body(x_ref, wg_ref, wu_ref, g_ref, u_ref, a_ref):
        x = x_ref[...]
        g = _dot(x, wg_ref[0, 0])
        u = _dot(x, wu_ref[0, 0])
        a = g * jax.nn.sigmoid(g) * u
        g_ref[...] = g.astype(bf16)
        u_ref[...] = u.astype(bf16)
        a_ref[...] = a.astype(bf16)

    out = SDS((S, 2 * ns), bf16)
    return pl.pallas_call(
        _with_deps(body, 3, deps), name=name, grid=(2, S // tm),
        in_specs=[pl.BlockSpec((tm, D_MODEL), lambda j, i: (i, 0)),
                  pl.BlockSpec((1, 1, D_MODEL, ns), lambda j, i: (l, j, 0, 0)),
                  pl.BlockSpec((1, 1, D_MODEL, ns), lambda j, i: (l, j + 2, 0, 0))] + [_DEP_SPEC] * len(deps),
        out_specs=[pl.BlockSpec((tm, ns), lambda j, i: (i, j))] * 3,
        out_shape=[out, out, out],
        compiler_params=_cp("parallel", "parallel"),
    )(xb, w13, w13, *deps)


def proj_res_ln(name, parts, ws, wl, x, g, b, rscale, deps=()):
    S = x.shape[0]
    tm = _tile(S, 512)
    n = len(parts)

    def body(*refs):
        p_refs, w_refs = refs[:n], refs[n:2 * n]
        x_ref, g_ref, b_ref, z_ref, y_ref, yb_ref = refs[2 * n:]
        acc = _dot(p_refs[0][...], w_refs[0][0])
        for k in range(1, n):
            acc = acc + _dot(p_refs[k][...], w_refs[k][0])
        if rscale != 1.0:
            acc = rscale * acc
        z = ALPHA * x_ref[...] + acc
        mu = jnp.mean(z, axis=-1, keepdims=True)
        zc = z - mu
        var = jnp.mean(zc * zc, axis=-1, keepdims=True)
        y = zc * lax.rsqrt(var + LN_EPS) * g_ref[...] + b_ref[...]
        z_ref[...] = z
        y_ref[...] = y
        yb_ref[...] = y.astype(bf16)

    row = lambda i: (i, 0)
    in_specs = [pl.BlockSpec((tm, p.shape[1]), row) for p in parts]
    in_specs += [pl.BlockSpec((1,) + w.shape[1:], functools.partial(lambda li, i: (li, 0, 0), li)) for w, li in zip(ws, wl)]
    in_specs += [pl.BlockSpec((tm, D_MODEL), row), pl.BlockSpec((1, D_MODEL), lambda i: (0, 0)),
                 pl.BlockSpec((1, D_MODEL), lambda i: (0, 0))] + [_DEP_SPEC] * len(deps)
    return pl.pallas_call(
        _with_deps(body, 2 * n + 3, deps), name=name, grid=(S // tm,), in_specs=in_specs,
        out_specs=[pl.BlockSpec((tm, D_MODEL), row)] * 3,
        out_shape=[SDS((S, D_MODEL), f32), SDS((S, D_MODEL), f32), SDS((S, D_MODEL), bf16)],
        compiler_params=_cp("parallel"),
    )(*parts, *ws, x, g, b, *deps)


def ln_bwd(name, dy, z, g, rscale, deps=()):
    S = dy.shape[0]
    tm = _tile(S, 512)

    def body(dy_ref, z_ref, g_ref, dzb_ref, dres_ref, dg_ref, db_ref):
        z = z_ref[...]
        mu = jnp.mean(z, axis=-1, keepdims=True)
        zc = z - mu
        rstd = lax.rsqrt(jnp.mean(zc * zc, axis=-1, keepdims=True) + LN_EPS)
        xhat = zc * rstd
        dyv = dy_ref[...]
        dxh = dyv * g_ref[...]
        m1 = jnp.mean(dxh, axis=-1, keepdims=True)
        m2 = jnp.mean(dxh * xhat, axis=-1, keepdims=True)
        dz = rstd * (dxh - m1 - xhat * m2)
        dzb_ref[...] = (rscale * dz).astype(bf16)
        dres_ref[...] = ALPHA * dz

        @pl.when(pl.program_id(0) == 0)
        def _():
            dg_ref[...] = jnp.zeros_like(dg_ref)
            db_ref[...] = jnp.zeros_like(db_ref)

        dg_ref[...] += jnp.sum(dyv * xhat, axis=0, keepdims=True)
        db_ref[...] += jnp.sum(dyv, axis=0, keepdims=True)

    row = lambda i: (i, 0)
    vec = pl.BlockSpec((1, D_MODEL), lambda i: (0, 0))
    return pl.pallas_call(
        _with_deps(body, 3, deps), name=name, grid=(S // tm,),
        in_specs=[pl.BlockSpec((tm, D_MODEL), row), pl.BlockSpec((tm, D_MODEL), row), vec] + [_DEP_SPEC] * len(deps),
        out_specs=[pl.BlockSpec((tm, D_MODEL), row), pl.BlockSpec((tm, D_MODEL), row), vec, vec],
        out_shape=[SDS((S, D_MODEL), bf16), SDS((S, D_MODEL), f32), SDS((1, D_MODEL), f32), SDS((1, D_MODEL), f32)],
        compiler_params=_cp("arbitrary"),
    )(dy, z, g, *deps)


def ffn_bwd_da(name, drb, w2, l, gate, up):
    S = drb.shape[0]
    tm = _tile(S, 512)
    nh = D_FF // 2

    def body(dr_ref, w_ref, g_ref, u_ref, dh_ref):
        dr = dr_ref[...]
        for j in range(2):
            cols = slice(j * nh, (j + 1) * nh)
            da = _dot_nt(dr, w_ref[0, cols, :])
            g = g_ref[:, cols].astype(f32)
            u = u_ref[:, cols].astype(f32)
            sg = jax.nn.sigmoid(g)
            dh_ref[:, cols] = (da * u * (sg * (1.0 + g * (1.0 - sg)))).astype(bf16)
            dh_ref[:, D_FF + j * nh:D_FF + (j + 1) * nh] = (da * (g * sg)).astype(bf16)

    row = lambda i: (i, 0)
    return pl.pallas_call(
        body, name=name, grid=(S // tm,),
        in_specs=[pl.BlockSpec((tm, D_MODEL), row), pl.BlockSpec((1, D_FF, D_MODEL), lambda i: (l, 0, 0)),
                  pl.BlockSpec((tm, D_FF), row), pl.BlockSpec((tm, D_FF), row)],
        out_specs=pl.BlockSpec((tm, 2 * D_FF), row),
        out_shape=SDS((S, 2 * D_FF), bf16),
        compiler_params=_cp("parallel"),
    )(drb, w2, gate, up)


def ffn_dx(name, dh, w13, l, res):
    S = dh.shape[0]
    ns = w13.shape[3]
    tm = _tile(S, 1024)

    def body(dh_ref, w_ref, r_ref, o_ref):
        @pl.when(pl.program_id(1) == 0)
        def _():
            o_ref[...] = r_ref[...]

        o_ref[...] += _dot_nt(dh_ref[...], w_ref[0, 0])

    return pl.pallas_call(
        body, name=name, grid=(S // tm, N_CHIPS),
        in_specs=[pl.BlockSpec((tm, ns), lambda i, j: (i, j)),
                  pl.BlockSpec((1, 1, D_MODEL, ns), lambda i, j: (l, j, 0, 0)),
                  pl.BlockSpec((tm, D_MODEL), lambda i, j: (i, 0))],
        out_specs=pl.BlockSpec((tm, D_MODEL), lambda i, j: (i, 0)),
        out_shape=SDS((S, D_MODEL), f32),
        compiler_params=_cp("parallel", "arbitrary"),
    )(dh, w13, res)


def mm_nt_res(name, dys, ws, wl, res, out_dtype):
    S = dys[0].shape[0]
    K = ws[0].shape[1]
    tm = _tile(S, 512)
    n = len(dys)

    def body(*refs):
        dy_refs, w_refs = refs[:n], refs[n:2 * n]
        o_ref = refs[-1]
        acc = _dot_nt(dy_refs[0][...], w_refs[0][0])
        for k in range(1, n):
            acc = acc + _dot_nt(dy_refs[k][...], w_refs[k][0])
        if res is not None:
            acc = acc + refs[2 * n][...]
        o_ref[...] = acc.astype(out_dtype)

    row = lambda i: (i, 0)
    in_specs = [pl.BlockSpec((tm, d.shape[1]), row) for d in dys]
    in_specs += [pl.BlockSpec((1,) + w.shape[1:], functools.partial(lambda li, i: (li, 0, 0), li)) for w, li in zip(ws, wl)]
    args = list(dys) + list(ws)
    if res is not None:
        in_specs.append(pl.BlockSpec((tm, K), row))
        args.append(res)
    return pl.pallas_call(
        body, name=name, grid=(S // tm,), in_specs=in_specs,
        out_specs=pl.BlockSpec((tm, K), row), out_shape=SDS((S, K), out_dtype),
        compiler_params=_cp("parallel"),
    )(*args)


def mm_tn(name, x, dy, col_shards=False, deps=()):
    S, K = x.shape
    N = dy.shape[1]
    ts = 512
    while ts * 2 <= min(S, 2048) and S % (ts * 2) == 0 and ts * 2 * K * 2 <= 6 * 2**20:
        ts *= 2
    ts = _tile(S, ts)
    if col_shards:
        tn = N // N_CHIPS
    else:
        tn = N
        while K * tn * 4 > 6 * 2**20 and tn % 256 == 0:
            tn //= 2
    nn = N // tn
    lead = ((0,) if col_shards else ()) + (slice(None), slice(None))

    def body(x_ref, dy_ref, o_ref):
        acc = _dot_tn(x_ref[...].astype(bf16), dy_ref[...].astype(bf16))

        @pl.when(pl.program_id(1) == 0)
        def _():
            o_ref[lead] = acc

        @pl.when(pl.program_id(1) != 0)
        def _():
            o_ref[lead] += acc

    if col_shards:
        out_spec = pl.BlockSpec((1, K, tn), lambda n, s: (n, 0, 0))
        out_shape = SDS((N_CHIPS, K, tn), f32)
    else:
        out_spec = pl.BlockSpec((K, tn), lambda n, s: (0, n))
        out_shape = SDS((K, N), f32)
    return pl.pallas_call(
        _with_deps(body, 2, deps), name=name, grid=(nn, S // ts),
        in_specs=[pl.BlockSpec((ts, K), lambda n, s: (s, 0)), pl.BlockSpec((ts, tn), lambda n, s: (s, n))]
        + [_DEP_SPEC] * len(deps),
        out_specs=out_spec, out_shape=out_shape, compiler_params=_cp("parallel", "arbitrary"),
    )(x, dy, *deps)


def mm_nn_shard(name, x, w, l):
    S, K = x.shape
    ns = w.shape[3]

    def body(x_ref, w_ref, o_ref):
        o_ref[...] = _dot(x_ref[...], w_ref[0, 0]).astype(bf16)

    return pl.pallas_call(
        body, name=name, grid=(N_CHIPS,),
        in_specs=[pl.BlockSpec((S, K), lambda j: (0, 0)), pl.BlockSpec((1, 1, K, ns), lambda j: (l, j, 0, 0))],
        out_specs=pl.BlockSpec((S, ns), lambda j: (0, j)), out_shape=SDS((S, N_CHIPS * ns), bf16),
        compiler_params=_cp("parallel"),
    )(x, w)


def loss_grad(name, y, t):
    S = y.shape[0]
    tm = _tile(S, 512)

    def body(y_ref, t_ref, dy_ref, loss_ref):
        e = y_ref[...] - t_ref[...]
        dy_ref[...] = e * (1.0 / D_MODEL)

        @pl.when(pl.program_id(0) == 0)
        def _():
            loss_ref[...] = jnp.zeros_like(loss_ref)

        loss_ref[...] += jnp.full(loss_ref.shape, (0.5 / D_MODEL) * jnp.sum(e * e), f32)

    row = lambda i: (i, 0)
    return pl.pallas_call(
        body, name=name, grid=(S // tm,),
        in_specs=[pl.BlockSpec((tm, D_MODEL), row)] * 2,
        out_specs=[pl.BlockSpec((tm, D_MODEL), row), pl.BlockSpec((8, 128), lambda i: (0, 0))],
        out_shape=[SDS((S, D_MODEL), f32), SDS((8, 128), f32)],
        compiler_params=_cp("arbitrary"),
    )(y, t)


def _half_sum(t):
    return t + pltpu.roll(t, 64, axis=1)


def mix_pre(name, xb, w_in, wq, wkv, l, gq, gkv, cs):
    S = xb.shape[0]
    tm = _tile(S, 512)
    H = MLA_HEADS
    W_EXT = w_in.shape[2]

    def body(x_ref, win_ref, wq_ref, wkv_ref, gq_ref, gkv_ref, cs_ref,
             u_ref, cq_ref, ckv_ref, cqn_ref, ckvn_ref, q_ref, k_ref, v_ref):
        h = _dot(x_ref[...], win_ref[0])
        u_ref[...] = h[:, :256]
        cq = h[:, 256:512]
        ckv = h[:, 512:640]
        cq_ref[...] = cq
        ckv_ref[...] = ckv
        cqn = (cq * lax.rsqrt(jnp.mean(cq * cq, axis=-1, keepdims=True) + RMS_EPS) * gq_ref[...]).astype(bf16)
        ckvn = (ckv * lax.rsqrt(jnp.mean(ckv * ckv, axis=-1, keepdims=True) + RMS_EPS) * gkv_ref[...]).astype(bf16)
        cqn_ref[...] = cqn
        ckvn_ref[...] = ckvn
        csv = cs_ref[...]
        lane = lax.broadcasted_iota(jnp.int32, (tm, 128), 1)
        kr = jnp.where(lane < 64, _half_sum(h[:, 640:768] * csv), 0.0).astype(bf16)
        kv = _dot(ckvn, wkv_ref[0])
        for hd in range(H):
            qe = _dot(cqn, wq_ref[0, hd])
            q_ref[hd, :, :128] = qe[:, :128].astype(bf16)
            q_ref[hd, :, 128:] = _half_sum(qe[:, 128:] * csv).astype(bf16)
            k_ref[hd, :, :128] = kv[:, 256 * hd:256 * hd + 128].astype(bf16)
            k_ref[hd, :, 128:] = kr
            v_ref[hd] = kv[:, 256 * hd + 128:256 * hd + 256].astype(bf16)

    row = lambda i: (i, 0)
    hrow = lambda i: (0, i, 0)
    return pl.pallas_call(
        body, name=name, grid=(S // tm,),
        in_specs=[pl.BlockSpec((tm, D_MODEL), row),
                  pl.BlockSpec((1, D_MODEL, W_EXT), lambda i: (l, 0, 0)),
                  pl.BlockSpec((1, H, Q_LORA, 256), lambda i: (l, 0, 0, 0)),
                  pl.BlockSpec((1, KV_LORA, H * 256), lambda i: (l, 0, 0)),
                  pl.BlockSpec((1, Q_LORA), lambda i: (0, 0)), pl.BlockSpec((1, KV_LORA), lambda i: (0, 0)),
                  pl.BlockSpec((tm, 128), row)],
        out_specs=[pl.BlockSpec((tm, 256), row), pl.BlockSpec((tm, Q_LORA), row), pl.BlockSpec((tm, KV_LORA), row),
                   pl.BlockSpec((tm, Q_LORA), row), pl.BlockSpec((tm, KV_LORA), row),
                   pl.BlockSpec((H, tm, 256), hrow), pl.BlockSpec((H, tm, 256), hrow), pl.BlockSpec((H, tm, 128), hrow)],
        out_shape=[SDS((S, 256), f32), SDS((S, Q_LORA), f32), SDS((S, KV_LORA), f32),
                   SDS((S, Q_LORA), bf16), SDS((S, KV_LORA), bf16),
                   SDS((H, S, 256), bf16), SDS((H, S, 256), bf16), SDS((H, S, 128), bf16)],
        compiler_params=_cp("parallel"),
    )(xb, w_in, wq, wkv, gq, gkv, cs)


def _group_select(col, a2, a4, a8, a16):
    return jnp.where(col < 64, a2, jnp.where(col < 128, a4, jnp.where(col < 192, a8, a16)))


def pool_fwd(name, u, wbd, scale):
    S = u.shape[0]
    tm = _tile(S, 512)
    hb = tm // HALO

    def body(u_ref, halo_ref, w_ref, s_ref, d_ref, y_ref):
        i = pl.program_id(0)
        cur = u_ref[...]
        halo = jnp.where(i > 0, halo_ref[...], 0.0)
        ext = jnp.concatenate([halo, cur], axis=0)
        s2 = ext + pltpu.roll(ext, 1, axis=0)
        s4 = s2 + pltpu.roll(s2, 2, axis=0)
        s8 = s4 + pltpu.roll(s4, 4, axis=0)
        s16 = s8 + pltpu.roll(s8, 8, axis=0)
        t1 = (i * tm + 1 + lax.broadcasted_iota(jnp.int32, (tm, 1), 0)).astype(f32)
        col = lax.broadcasted_iota(jnp.int32, (tm, 256), 1)
        m = _group_select(col, s2[HALO:] / jnp.minimum(t1, 2.0), s4[HALO:] / jnp.minimum(t1, 4.0),
                          s8[HALO:] / jnp.minimum(t1, 8.0), s16[HALO:] / jnp.minimum(t1, 16.0))
        d = (m - cur).astype(bf16)
        d_ref[...] = d
        y_ref[...] = (_dot(d, w_ref[...]) * s_ref[...]).astype(bf16)

    row = lambda i: (i, 0)
    return pl.pallas_call(
        body, name=name, grid=(S // tm,),
        in_specs=[pl.BlockSpec((tm, 256), row), pl.BlockSpec((HALO, 256), lambda i: (jnp.maximum(i * hb - 1, 0), 0)),
                  pl.BlockSpec((256, 256), lambda i: (0, 0)), pl.BlockSpec((1, 256), lambda i: (0, 0))],
        out_specs=[pl.BlockSpec((tm, 256), row)] * 2,
        out_shape=[SDS((S, 256), bf16), SDS((S, 256), bf16)],
        compiler_params=_cp("parallel"),
    )(u, u, wbd, scale)


def pool_bwd(name, dyp, d, wbd, scale):
    S = dyp.shape[0]
    tm = _tile(S, 512)
    hb = tm // HALO
    n_ext = tm + HALO

    def fwd_sum(e, steps):
        k = 1
        for _ in range(steps):
            e = e + pltpu.roll(e, n_ext - k, axis=0)
            k *= 2
        return e

    def body(dy_ref, halo_ref, d_ref, w_ref, s_ref, du_ref, dyw_ref, ds_ref):
        i = pl.program_id(0)
        sc = s_ref[...]
        w = w_ref[...]
        cur = dy_ref[...].astype(f32)
        halo = jnp.where(i < pl.num_programs(0) - 1, halo_ref[...].astype(f32), 0.0)
        dyw = jnp.concatenate([cur, halo], axis=0) * sc
        dyw_ref[...] = dyw[:tm].astype(bf16)
        dd = _dot_nt(dyw.astype(bf16), w)
        t1 = (i * tm + 1 + lax.broadcasted_iota(jnp.int32, (n_ext, 1), 0)).astype(f32)
        f2 = fwd_sum(dd / jnp.minimum(t1, 2.0), 1)
        f4 = fwd_sum(dd / jnp.minimum(t1, 4.0), 2)
        f8 = fwd_sum(dd / jnp.minimum(t1, 8.0), 3)
        f16 = fwd_sum(dd / jnp.minimum(t1, 16.0), 4)
        col = lax.broadcasted_iota(jnp.int32, (tm, 256), 1)
        du_ref[...] = (_group_select(col, f2[:tm], f4[:tm], f8[:tm], f16[:tm]) - dd[:tm]).astype(bf16)

        @pl.when(i == 0)
        def _():
            ds_ref[...] = jnp.zeros_like(ds_ref)

        ds_ref[...] += jnp.sum(cur * _dot(d_ref[...], w), axis=0, keepdims=True)

    row = lambda i: (i, 0)
    nhb = S // HALO
    return pl.pallas_call(
        body, name=name, grid=(S // tm,),
        in_specs=[pl.BlockSpec((tm, 256), row), pl.BlockSpec((HALO, 256), lambda i: (jnp.minimum((i + 1) * hb, nhb - 1), 0)),
                  pl.BlockSpec((tm, 256), row), pl.BlockSpec((256, 256), lambda i: (0, 0)),
                  pl.BlockSpec((1, 256), lambda i: (0, 0))],
        out_specs=[pl.BlockSpec((tm, 256), row), pl.BlockSpec((tm, 256), row), pl.BlockSpec((1, 256), lambda i: (0, 0))],
        out_shape=[SDS((S, 256), bf16), SDS((S, 256), bf16), SDS((1, 256), f32)],
        compiler_params=_cp("arbitrary"),
    )(dyp, dyp, d, wbd, scale)


def _diag_mask(tq):
    rc = lax.broadcasted_iota(jnp.int32, (tq, 1), 0) // 64
    cc = lax.broadcasted_iota(jnp.int32, (1, tq), 1) // 64
    return rc >= cc


MLA_SCALE_LOG2 = MLA_SCALE * math.log2(math.e)


def mla_attn_fwd(name, q, k, v):
    H, S, _ = q.shape
    tq = _tile(S, 1024)
    nq = S // tq
    pairs = [(i, j) for i in range(nq) for j in range(i + 1)]
    it = jnp.asarray([p_[0] for p_ in pairs], jnp.int32)
    jt = jnp.asarray([p_[1] for p_ in pairs], jnp.int32)

    def body(it_ref, jt_ref, q_ref, k_ref, v_ref, o_ref, lse_ref, m_sc, l_sc, acc_sc):
        t = pl.program_id(1)
        i, j = it_ref[t], jt_ref[t]

        @pl.when(j == 0)
        def _():
            m_sc[...] = jnp.full_like(m_sc, NEG_INF)
            l_sc[...] = jnp.zeros_like(l_sc)
            acc_sc[...] = jnp.zeros_like(acc_sc)

        def step(masked):
            s = _dot_nt(q_ref[0], k_ref[0])
            if masked:
                s = jnp.where(_diag_mask(tq), s, NEG_INF)
            m_prev = m_sc[...]
            m_new = jnp.maximum(m_prev, jnp.max(s, axis=-1, keepdims=True))
            p = jnp.exp2((s - jnp.tile(m_new, (1, tq // 128))) * MLA_SCALE_LOG2)
            a = jnp.exp2((m_prev - m_new) * MLA_SCALE_LOG2)
            l_sc[...] = a * l_sc[...] + jnp.sum(p, axis=-1, keepdims=True)
            acc_sc[...] = a * acc_sc[...] + _dot(p.astype(bf16), v_ref[0])
            m_sc[...] = m_new

        @pl.when(j < i)
        def _():
            step(False)

        @pl.when(j == i)
        def _():
            step(True)
            o_ref[...] = (acc_sc[...] / l_sc[...]).astype(bf16)
            lse_ref[0] = m_sc[...] * MLA_SCALE_LOG2 + jnp.log2(l_sc[...])

    return pl.pallas_call(
        body, name=name,
        grid_spec=pltpu.PrefetchScalarGridSpec(
            num_scalar_prefetch=2, grid=(H, len(pairs)),
            in_specs=[pl.BlockSpec((1, tq, 256), lambda h, t, it_, jt_: (h, it_[t], 0)),
                      pl.BlockSpec((1, tq, 256), lambda h, t, it_, jt_: (h, jt_[t], 0)),
                      pl.BlockSpec((1, tq, 128), lambda h, t, it_, jt_: (h, jt_[t], 0))],
            out_specs=[pl.BlockSpec((tq, 128), lambda h, t, it_, jt_: (it_[t], h)),
                       pl.BlockSpec((1, tq, 128), lambda h, t, it_, jt_: (h, it_[t], 0))],
            scratch_shapes=[pltpu.VMEM((tq, 128), f32), pltpu.VMEM((tq, 128), f32), pltpu.VMEM((tq, 128), f32)]),
        out_shape=[SDS((S, H * 128), bf16), SDS((H, S, 128), f32)],
        compiler_params=_cp("parallel", "arbitrary"),
    )(it, jt, q, k, v)


def mla_attn_bwd(name, q, k, v, o, do, lse):
    H, S, _ = q.shape
    tq = _tile(S, 1024)
    nq = S // tq
    pairs = [(i, j) for j in range(nq) for i in range(j, nq)]
    it = jnp.asarray([p_[0] for p_ in pairs], jnp.int32)
    jt = jnp.asarray([p_[1] for p_ in pairs], jnp.int32)
    n_pairs = len(pairs)

    def body(it_ref, jt_ref, q_ref, k_ref, v_ref, o_ref, do_ref, lse_ref, dq_ref, dk_ref, dv_ref, dk_sc, dv_sc):
        t = pl.program_id(1)
        i, j = it_ref[t], jt_ref[t]

        @pl.when(t == 0)
        def _():
            dq_ref[...] = jnp.zeros_like(dq_ref)

        @pl.when(i == j)
        def _():
            dk_sc[...] = jnp.zeros_like(dk_sc)
            dv_sc[...] = jnp.zeros_like(dv_sc)

        def step(masked):
            qv, kv_, dov = q_ref[0], k_ref[0], do_ref[...]
            s = _dot_nt(qv, kv_)
            if masked:
                s = jnp.where(_diag_mask(tq), s, NEG_INF)
            p = jnp.exp2(s * MLA_SCALE_LOG2 - jnp.tile(lse_ref[0], (1, tq // 128)))
            dv_sc[...] += _dot_tn(p.astype(bf16), dov)
            dp = _dot_nt(dov, v_ref[0])
            delta = jnp.sum(dov.astype(f32) * o_ref[...].astype(f32), axis=-1, keepdims=True)
            ds = (p * (dp - delta)).astype(bf16)
            dk_sc[...] += _dot_tn(ds, qv)
            rows = pl.ds(pl.multiple_of(i * tq, tq), tq)
            dq_ref[0, rows, :] += _dot(ds, kv_)

        @pl.when(i > j)
        def _():
            step(False)

        @pl.when(i == j)
        def _():
            step(True)

        @pl.when(i == nq - 1)
        def _():
            dk_ref[0] = dk_sc[...] * MLA_SCALE
            dv_ref[0] = dv_sc[...]

        @pl.when(t == n_pairs - 1)
        def _():
            dq_ref[...] = dq_ref[...] * MLA_SCALE

    qi = lambda h, t, it_, jt_: (h, it_[t], 0)
    kj = lambda h, t, it_, jt_: (h, jt_[t], 0)
    oi = lambda h, t, it_, jt_: (it_[t], h)
    return pl.pallas_call(
        body, name=name,
        grid_spec=pltpu.PrefetchScalarGridSpec(
            num_scalar_prefetch=2, grid=(H, n_pairs),
            in_specs=[pl.BlockSpec((1, tq, 256), qi), pl.BlockSpec((1, tq, 256), kj), pl.BlockSpec((1, tq, 128), kj),
                      pl.BlockSpec((tq, 128), oi), pl.BlockSpec((tq, 128), oi), pl.BlockSpec((1, tq, 128), qi)],
            out_specs=[pl.BlockSpec((1, S, 256), lambda h, t, it_, jt_: (h, 0, 0)), pl.BlockSpec((1, tq, 256), kj),
                       pl.BlockSpec((1, tq, 128), kj)],
            scratch_shapes=[pltpu.VMEM((tq, 256), f32), pltpu.VMEM((tq, 128), f32)]),
        out_shape=[SDS((H, S, 256), f32), SDS((H, S, 256), f32), SDS((H, S, 128), f32)],
        compiler_params=_cp("parallel", "arbitrary"),
    )(it, jt, q, k, v, o, do, lse)


def mix_post_bwd(name, dq, dk, dv, wq, wkv, l, cq, ckv, gq, gkv, cs):
    H, S, _ = dq.shape
    tm = _tile(S, 512)

    def rms_bwd(dyn, c, g):
        r = lax.rsqrt(jnp.mean(c * c, axis=-1, keepdims=True) + RMS_EPS)
        ch = c * r
        dyg = dyn * g
        dc = r * (dyg - ch * jnp.mean(dyg * ch, axis=-1, keepdims=True))
        return dc, jnp.sum(dyn * ch, axis=0, keepdims=True)

    def body(dq_ref, dk_ref, dv_ref, wq_ref, wkv_ref, cq_ref, ckv_ref, gq_ref, gkv_ref, cs_ref,
             dqe_ref, dkv_ref, dh_ref, dgq_ref, dgkv_ref):
        csv = cs_ref[...]
        lane = lax.broadcasted_iota(jnp.int32, (tm, 128), 1)
        dcqn = jnp.zeros((tm, Q_LORA), f32)
        dkr = jnp.zeros((tm, 128), f32)
        for hd in range(H):
            dqh = dq_ref[hd]
            dqe = jnp.concatenate([dqh[:, :128], _half_sum(dqh[:, 128:]) * csv], axis=1).astype(bf16)
            dqe_ref[:, 256 * hd:256 * hd + 256] = dqe
            dcqn = dcqn + _dot_nt(dqe, wq_ref[0, hd])
            dkh = dk_ref[hd]
            dkv_ref[:, 256 * hd:256 * hd + 128] = dkh[:, :128].astype(bf16)
            dkv_ref[:, 256 * hd + 128:256 * hd + 256] = dv_ref[hd].astype(bf16)
            dkr = dkr + dkh[:, 128:]
        dckvn = _dot_nt(dkv_ref[...], wkv_ref[0])
        dblk = _half_sum(jnp.where(lane < 64, dkr, 0.0)) * csv
        dcq, dgq = rms_bwd(dcqn, cq_ref[...], gq_ref[...])
        dckv, dgkv = rms_bwd(dckvn, ckv_ref[...], gkv_ref[...])
        dh_ref[:, :256] = dcq.astype(bf16)
        dh_ref[:, 256:384] = dckv.astype(bf16)
        dh_ref[:, 384:] = dblk.astype(bf16)

        @pl.when(pl.program_id(0) == 0)
        def _():
            dgq_ref[...] = jnp.zeros_like(dgq_ref)
            dgkv_ref[...] = jnp.zeros_like(dgkv_ref)

        dgq_ref[...] += dgq
        dgkv_ref[...] += dgkv

    row = lambda i: (i, 0)
    hrow = lambda i: (0, i, 0)
    return pl.pallas_call(
        body, name=name, grid=(S // tm,),
        in_specs=[pl.BlockSpec((H, tm, 256), hrow), pl.BlockSpec((H, tm, 256), hrow), pl.BlockSpec((H, tm, 128), hrow),
                  pl.BlockSpec((1, H, Q_LORA, 256), lambda i: (l, 0, 0, 0)),
                  pl.BlockSpec((1, KV_LORA, H * 256), lambda i: (l, 0, 0)),
                  pl.BlockSpec((tm, Q_LORA), row), pl.BlockSpec((tm, KV_LORA), row),
                  pl.BlockSpec((1, Q_LORA), lambda i: (0, 0)), pl.BlockSpec((1, KV_LORA), lambda i: (0, 0)),
                  pl.BlockSpec((tm, 128), row)],
        out_specs=[pl.BlockSpec((tm, H * 256), row), pl.BlockSpec((tm, H * 256), row), pl.BlockSpec((tm, 512), row),
                   pl.BlockSpec((1, Q_LORA), lambda i: (0, 0)), pl.BlockSpec((1, KV_LORA), lambda i: (0, 0))],
        out_shape=[SDS((S, H * 256), bf16), SDS((S, H * 256), bf16), SDS((S, 512), bf16),
                   SDS((1, Q_LORA), f32), SDS((1, KV_LORA), f32)],
        compiler_params=_cp("arbitrary"),
    )(dq, dk, dv, wq, wkv, cq, ckv, gq, gkv, cs)


def _cross_probs(qb, kv_ref, hd):
    cols = slice(hd * MEM_HEAD_DIM, (hd + 1) * MEM_HEAD_DIM)
    s = _dot_nt(qb[:, cols], kv_ref[:, cols]) * MEM_SCALE
    e = jnp.exp(s - jnp.max(s, axis=-1, keepdims=True))
    return e / jnp.sum(e, axis=-1, keepdims=True)


def cross_fwd(name, xb, xf, wq, wo, l, kv, g, b):
    S = xb.shape[0]
    tm = _tile(S, 512)
    M = kv.shape[0]

    def body(x_ref, xf_ref, wq_ref, wo_ref, k_ref, v_ref, g_ref, b_ref, q_ref, o_ref, z_ref, y_ref, yb_ref):
        qb = _dot(x_ref[...], wq_ref[0]).astype(bf16)
        q_ref[...] = qb
        for hd in range(MEM_HEADS):
            cols = slice(hd * MEM_HEAD_DIM, (hd + 1) * MEM_HEAD_DIM)
            p = _cross_probs(qb, k_ref, hd)
            o_ref[:, cols] = _dot(p.astype(bf16), v_ref[:, cols]).astype(bf16)
        z = ALPHA * xf_ref[...] + _dot(o_ref[...], wo_ref[0])
        mu = jnp.mean(z, axis=-1, keepdims=True)
        zc = z - mu
        var = jnp.mean(zc * zc, axis=-1, keepdims=True)
        y = zc * lax.rsqrt(var + LN_EPS) * g_ref[...] + b_ref[...]
        z_ref[...] = z
        y_ref[...] = y
        yb_ref[...] = y.astype(bf16)

    row = lambda i: (i, 0)
    wspec = pl.BlockSpec((1, D_MODEL, D_MODEL), lambda i: (l, 0, 0))
    vec = pl.BlockSpec((1, D_MODEL), lambda i: (0, 0))
    blk = pl.BlockSpec((tm, D_MODEL), row)
    return pl.pallas_call(
        body, name=name, grid=(S // tm,),
        in_specs=[blk, blk, wspec, wspec, pl.BlockSpec((M, D_MODEL), lambda i: (0, 0)),
                  pl.BlockSpec((M, D_MODEL), lambda i: (0, 1)), vec, vec],
        out_specs=[blk] * 5,
        out_shape=[SDS((S, D_MODEL), bf16), SDS((S, D_MODEL), bf16), SDS((S, D_MODEL), f32), SDS((S, D_MODEL), f32),
                   SDS((S, D_MODEL), bf16)],
        compiler_params=_cp("parallel"),
    )(xb, xf, wq, wo, kv, kv, g, b)


def cross_bwd(name, dzb, wo, l, qb, kv):
    S = dzb.shape[0]
    tm = _tile(S, 512)
    M = kv.shape[0]

    def body(dz_ref, wo_ref, q_ref, k_ref, v_ref, dq_ref, dkv_ref):
        @pl.when(pl.program_id(0) == 0)
        def _():
            dkv_ref[...] = jnp.zeros_like(dkv_ref)

        do = _dot_nt(dz_ref[...], wo_ref[0]).astype(bf16)
        qv = q_ref[...]
        for hd in range(MEM_HEADS):
            cols = slice(hd * MEM_HEAD_DIM, (hd + 1) * MEM_HEAD_DIM)
            vcols = slice(D_MODEL + hd * MEM_HEAD_DIM, D_MODEL + (hd + 1) * MEM_HEAD_DIM)
            p = _cross_probs(qv, k_ref, hd)
            doh = do[:, cols]
            dkv_ref[:, vcols] += _dot_tn(p.astype(bf16), doh)
            dp = _dot_nt(doh, v_ref[:, cols])
            ds = (p * (dp - jnp.sum(dp * p, axis=-1, keepdims=True)) * MEM_SCALE).astype(bf16)
            dq_ref[:, cols] = _dot(ds, k_ref[:, cols]).astype(bf16)
            dkv_ref[:, cols] += _dot_tn(ds, qv[:, cols])

    row = lambda i: (i, 0)
    blk = pl.BlockSpec((tm, D_MODEL), row)
    return pl.pallas_call(
        body, name=name, grid=(S // tm,),
        in_specs=[blk, pl.BlockSpec((1, D_MODEL, D_MODEL), lambda i: (l, 0, 0)), blk,
                  pl.BlockSpec((M, D_MODEL), lambda i: (0, 0)), pl.BlockSpec((M, D_MODEL), lambda i: (0, 1))],
        out_specs=[blk, pl.BlockSpec((M, 2 * D_MODEL), lambda i: (0, 0))],
        out_shape=[SDS((S, D_MODEL), bf16), SDS((M, 2 * D_MODEL), f32)],
        compiler_params=_cp("arbitrary"),
    )(dzb, wo, qb, kv, kv)


def adamw(name, w, g, m, v):
    shape = w.shape
    cols = shape[-1]
    rows = math.prod(shape[:-1])
    tr = _row_tile(rows, cols, target=2 * 2**20)
    c1 = 1.0 - ADAM_B1 ** ADAM_STEP
    c2 = 1.0 - ADAM_B2 ** ADAM_STEP

    def body(w_ref, g_ref, m_ref, v_ref, d_ref, nm_ref, nv_ref):
        gv = g_ref[...]
        nm = ADAM_B1 * m_ref[...] + (1.0 - ADAM_B1) * gv
        nv = ADAM_B2 * v_ref[...] + (1.0 - ADAM_B2) * (gv * gv)
        d_ref[...] = -ADAM_LR * ((nm / c1) / (jnp.sqrt(nv / c2) + ADAM_EPS) + ADAM_WD * w_ref[...])
        nm_ref[...] = nm
        nv_ref[...] = nv

    blk = pl.BlockSpec((tr, cols), lambda i: (i, 0))
    flat = SDS((rows, cols), f32)
    outs = pl.pallas_call(
        body, name=name, grid=(rows // tr,), in_specs=[blk] * 4, out_specs=[blk] * 3, out_shape=[flat] * 3,
        compiler_params=_cp("parallel"),
    )(*[a.reshape(rows, cols) for a in (w, g, m, v)])
    return [o.reshape(shape) for o in outs]


def _me():
    return lax.axis_index("x"), lax.axis_index("y"), lax.axis_index("c")


def _other_chips(x, y):
    return [(1 - x, y), (x, 1 - y), (1 - x, 1 - y)]


def pair_share(name, sums, owners):
    n = len(sums)

    def body(*refs):
        out_refs = refs[n:2 * n]
        send_sems, recv_sems = refs[2 * n:]
        x, y, c = _me()
        sibling = (x, y, 1 - c)

        def copy(a, lyr):
            slot = out_refs[a].at[lyr]
            return pltpu.make_async_remote_copy(src_ref=slot, dst_ref=slot, send_sem=send_sems.at[2 * a + lyr],
                                                recv_sem=recv_sems.at[2 * a + lyr], device_id=sibling, device_id_type=MESH)

        def each(mine, act):
            for o in range(2):
                slots = [(a, lyr) for a in range(n) for lyr in range(DEPTH) if owners[a][lyr] == o]

                @pl.when((c == o) if mine else (c != o))
                def _(slots=slots):
                    for a, lyr in slots:
                        act(copy(a, lyr))

        each(True, lambda cp: cp.start())
        each(True, lambda cp: cp.wait_send())
        each(False, lambda cp: cp.wait_recv())

    any_spec = pl.BlockSpec(memory_space=pl.ANY)
    return pl.pallas_call(
        body, name=name, in_specs=[any_spec] * n, out_specs=[any_spec] * n,
        out_shape=[SDS(s.shape, f32) for s in sums], input_output_aliases={a: a for a in range(n)},
        scratch_shapes=[pltpu.SemaphoreType.DMA((2 * n,)), pltpu.SemaphoreType.DMA((2 * n,))],
    )(*sums)


def allsum_small(name, v, deps=()):
    R = v.shape[0]

    def body(v_ref, o_ref, all_ref, send_sems, recv_sems, local_sem):
        x, y, c = _me()
        me, sibling = (x, y, c), (x, y, 1 - c)
        chips = _other_chips(x, y)

        def rows(px, py, pc):
            return all_ref.at[4 * px + 2 * py + pc]

        def copy(k, block, to, src=None):
            return pltpu.make_async_remote_copy(
                src_ref=rows(*block) if src is None else src, dst_ref=rows(*block),
                send_sem=send_sems.at[k], recv_sem=recv_sems.at[k], device_id=to, device_id_type=MESH)

        mine = pltpu.make_async_copy(v_ref, rows(*me), local_sem)
        mine.start()
        first = [copy(0, me, sibling, src=v_ref)]
        first += [copy(1 + j, me, (*chip, c), src=v_ref) for j, chip in enumerate(chips)]
        for cp in first:
            cp.start()
        passed = [copy(4 + j, (*chip, c), sibling) for j, chip in enumerate(chips)]
        for j, chip in enumerate(chips):
            copy(1 + j, (*chip, c), me).wait_recv()
            passed[j].start()
        copy(0, sibling, me).wait_recv()
        for j, chip in enumerate(chips):
            copy(4 + j, (*chip, 1 - c), me).wait_recv()
        for cp in first + passed:
            cp.wait_send()
        mine.wait()
        acc = all_ref[0]
        for d in range(1, 8):
            acc = acc + all_ref[d]
        o_ref[...] = acc

    return pl.pallas_call(
        _with_deps(body, 1, deps), name=name,
        in_specs=[pl.BlockSpec(memory_space=pltpu.VMEM)] + [_DEP_SPEC] * len(deps),
        out_specs=pl.BlockSpec(memory_space=pltpu.VMEM),
        out_shape=SDS((R, 128), f32),
        scratch_shapes=[pltpu.VMEM((8, R, 128), f32), pltpu.SemaphoreType.DMA((7,)), pltpu.SemaphoreType.DMA((7,)),
                        pltpu.SemaphoreType.DMA],
        compiler_params=pltpu.CompilerParams(vmem_limit_bytes=V7X_VMEM_LIMIT),
    )(v, *deps)


def _swap_half(r):
    return jnp.concatenate([-r[..., 32:], r[..., :32]], axis=-1)


def _unswap_add(p, qg):
    return p + jnp.concatenate([qg[..., 32:], -qg[..., :32]], axis=-1)


def _block_diag(pw):
    L = pw.shape[0]
    out = jnp.zeros((L, 256, 256), pw.dtype)
    for gi in range(4):
        out = out.at[:, 64 * gi:64 * gi + 64, 64 * gi:64 * gi + 64].set(pw[:, gi])
    return out


def _to_col_shards(w):
    *lead, K, N = w.shape
    nl = len(lead)
    return w.reshape(*lead, K, N_CHIPS, N // N_CHIPS).transpose(*range(nl), nl + 1, nl, nl + 2)


def _from_col_shards(w):
    *lead, C, K, n = w.shape
    nl = len(lead)
    return w.transpose(*range(nl), nl + 1, nl, nl + 2).reshape(*lead, K, C * n)


def _step_serial_comm(x, mem, positions, ln_g, ln_b, ffn1_w13, ffn1_w2, w_in, pool_w, pool_scale, q_norm_g, w_uq, kv_norm_g, w_ukv, w_out, mem_wq, mem_wkv, mem_wo, ffn2_w13, ffn2_w2, loss_target, m_ln_g, m_ln_b, m_ffn1_w13, m_ffn1_w2, m_w_in, m_pool_w, m_pool_scale, m_q_norm_g, m_w_uq, m_kv_norm_g, m_w_ukv, m_w_out, m_mem_wq, m_mem_wkv, m_mem_wo, m_ffn2_w13, m_ffn2_w2, v_ln_g, v_ln_b, v_ffn1_w13, v_ffn1_w2, v_w_in, v_pool_w, v_pool_scale, v_q_norm_g, v_w_uq, v_kv_norm_g, v_w_ukv, v_w_out, v_mem_wq, v_mem_wkv, v_mem_wo, v_ffn2_w13, v_ffn2_w2):
    L = DEPTH
    S = x.shape[1]
    qx, qy, _ = _me()
    chip = 2 * qx + qy

    big = [ffn1_w13, ffn1_w2, w_in, w_uq, w_ukv, w_out, mem_wq, mem_wkv, mem_wo, ffn2_w13, ffn2_w2]
    (g_f1w13, g_f1w2, g_win, g_wuq, g_wukv, g_wout, g_mwq, g_mwkv, g_mwo, g_f2w13, g_f2w2) = gather_weights(
        [w.astype(bf16) for w in big])
    f1w2 = g_f1w2.reshape(L, D_FF, D_MODEL)
    f2w2 = g_f2w2.reshape(L, D_FF, D_MODEL)
    win = g_win.reshape(L, D_MODEL, D_IN)
    win_ext = jnp.concatenate([win, _swap_half(win[..., D_IN - QK_ROPE:])], axis=-1)
    wuq = _from_col_shards(g_wuq).reshape(L, Q_LORA, MLA_HEADS, QK_NOPE + QK_ROPE)
    wq_ext = jnp.concatenate([wuq, _swap_half(wuq[..., QK_NOPE:])], axis=-1).transpose(0, 2, 1, 3)
    wukv = _from_col_shards(g_wukv)
    wout = g_wout.reshape(L, D_MODEL, D_MODEL)
    wout_pool, wout_mla = wout[:, :POOL_WIDTH], wout[:, POOL_WIDTH:]
    mwq = g_mwq.reshape(L, D_MODEL, D_MODEL)
    mwo = g_mwo.reshape(L, D_MODEL, D_MODEL)
    wbd = _block_diag(pool_w.astype(bf16))

    ln_pad = jnp.zeros((2, L, 4, N_CHIPS, D_MODEL // N_CHIPS), f32)
    ln_pad = lax.dynamic_update_slice(ln_pad, jnp.stack([ln_g, ln_b])[:, :, :, None, :], (0, 0, 0, chip, 0))
    ln_full = allsum_small("allsum_ln", ln_pad.reshape(-1, 128)) * 0.5
    ln_full = ln_full.reshape(2, L, 4, D_MODEL)
    lng, lnb = ln_full[0], ln_full[1]

    half = QK_ROPE // 2
    inv_freq = ROPE_BASE ** (-jnp.arange(half, dtype=f32) / half)
    ang = positions[0].astype(f32)[:, None] * inv_freq
    cos, sin = jnp.cos(ang), jnp.sin(ang)
    cs = jnp.concatenate([cos, cos, sin, sin], axis=-1)

    memb = mem[0].astype(bf16)
    xf = x[0]
    xb = xf.astype(bf16)
    vec = lambda a: a.reshape(1, -1)

    saved = []
    for l in range(L):
        sv = {}
        sv["x0b"] = xb
        gate, up, act = ffn_up(f"ffn1_up_{l}", xb, g_f1w13, l)
        z1, x1f, x1b = proj_res_ln(f"ffn1_down_{l}", [act], [f1w2], [l], xf, vec(lng[l, 0]), vec(lnb[l, 0]), 0.5)
        sv.update(gate1=gate, up1=up, act1=act, z1=z1, x1b=x1b)
        u, cq, ckv, cqn, ckvn, q, k, v = mix_pre(f"mix_pre_{l}", x1b, win_ext, wq_ext, wukv, l,
                                                   vec(q_norm_g[l]), vec(kv_norm_g[l]), cs)
        dpool, ypool = pool_fwd(f"pool_fwd_{l}", u, wbd[l], vec(pool_scale[l]))
        o, lse = mla_attn_fwd(f"mla_fwd_{l}", q, k, v)
        z2, x2f, x2b = proj_res_ln(f"mix_out_{l}", [ypool, o], [wout_pool, wout_mla], [l, l], x1f,
                                   vec(lng[l, 1]), vec(lnb[l, 1]), 1.0)
        sv.update(cq=cq, ckv=ckv, cqn=cqn, ckvn=ckvn, q=q, k=k, v=v, dpool=dpool, ypool=ypool, o=o, lse=lse, z2=z2, x2b=x2b)
        kvm = mm_nn_shard(f"mem_kv_{l}", memb, g_mwkv, l)
        cq_, co_, z3, x3f, x3b = cross_fwd(f"cross_fwd_{l}", x2b, x2f, mwq, mwo, l, kvm, vec(lng[l, 2]), vec(lnb[l, 2]))
        sv.update(kvm=kvm, crq=cq_, cro=co_, z3=z3, x3b=x3b)
        gate, up, act = ffn_up(f"ffn2_up_{l}", x3b, g_f2w13, l)
        z4, xf, xb = proj_res_ln(f"ffn2_down_{l}", [act], [f2w2], [l], x3f, vec(lng[l, 3]), vec(lnb[l, 3]), 0.5)
        sv.update(gate2=gate, up2=up, act2=act, z4=z4)
        saved.append(sv)

    dy, loss_blk = loss_grad("loss_grad", xf, loss_target[0])
    loss = lax.psum(loss_blk[0, 0], ("x", "y", "c"))

    G = dict(f1w13=None, f1w2=None, mwq=None, mwkv=None, mwo=None, f2w13=None, f2w2=None)
    small = {k_: [None] * L for k_ in ("win", "wuq", "wukv", "wout", "pool_w", "pool_scale", "gq", "gkv", "lng", "lnb")}
    for l in reversed(range(L)):
        sv = saved[l]
        dlg, dlb = [None] * 4, [None] * 4
        dzb, dres, dlg[3], dlb[3] = ln_bwd(f"ln4_bwd_{l}", dy, sv["z4"], vec(lng[l, 3]), 0.5)
        dh = ffn_bwd_da(f"ffn2_bwd_da_{l}", dzb, f2w2, l, sv["gate2"], sv["up2"])
        G["f2w2"] = mm_tn(f"ffn2_dw2_{l}", sv["act2"], dzb, "nat", l, G["f2w2"])
        G["f2w13"] = mm_tn(f"ffn2_dw13_{l}", sv["x3b"], dh, "shard", l, G["f2w13"])
        dy = ffn_dx(f"ffn2_dx_{l}", dh, g_f2w13, l, dres)
        dzb, dres, dlg[2], dlb[2] = ln_bwd(f"ln3_bwd_{l}", dy, sv["z3"], vec(lng[l, 2]), 1.0)
        dqc, dkvm = cross_bwd(f"cross_bwd_{l}", dzb, mwo, l, sv["crq"], sv["kvm"])
        G["mwo"] = mm_tn(f"cross_dwo_{l}", sv["cro"], dzb, "nat", l, G["mwo"])
        G["mwq"] = mm_tn(f"cross_dwq_{l}", sv["x2b"], dqc, "nat", l, G["mwq"])
        G["mwkv"] = mm_tn(f"cross_dwkv_{l}", memb, dkvm, "shard", l, G["mwkv"])
        dy = mm_nt_res(f"cross_dx_{l}", [dqc], [mwq], [l], dres, f32)
        dzb, dres, dlg[1], dlb[1] = ln_bwd(f"ln2_bwd_{l}", dy, sv["z2"], vec(lng[l, 1]), 1.0)
        dyp = mm_nt_res(f"mix_dpool_{l}", [dzb], [wout_pool], [l], None, bf16)
        do = mm_nt_res(f"mix_do_{l}", [dzb], [wout_mla], [l], None, bf16)
        dwo_p = mm_tn(f"mix_dwout_pool_{l}", sv["ypool"], dzb)
        dwo_m = mm_tn(f"mix_dwout_mla_{l}", sv["o"], dzb)
        small["wout"][l] = jnp.concatenate([dwo_p, dwo_m], axis=0)
        dq, dk, dv = mla_attn_bwd(f"mla_bwd_{l}", sv["q"], sv["k"], sv["v"], sv["o"], do, sv["lse"])
        dqe, dkv, dh_rest, dgq, dgkv = mix_post_bwd(f"mix_post_bwd_{l}", dq, dk, dv, wq_ext, wukv, l, sv["cq"], sv["ckv"],
                                                     vec(q_norm_g[l]), vec(kv_norm_g[l]), cs)
        du, dyw, dscale = pool_bwd(f"pool_bwd_{l}", dyp, sv["dpool"], wbd[l], vec(pool_scale[l]))
        dwq_e = mm_tn(f"mix_dwuq_{l}", sv["cqn"], dqe).reshape(Q_LORA, MLA_HEADS, 256)
        small["wuq"][l] = jnp.concatenate(
            [dwq_e[..., :QK_NOPE], _unswap_add(dwq_e[..., QK_NOPE:QK_NOPE + QK_ROPE], dwq_e[..., QK_NOPE + QK_ROPE:])],
            axis=-1).reshape(Q_LORA, MLA_HEADS * (QK_NOPE + QK_ROPE))
        small["wukv"][l] = mm_tn(f"mix_dwukv_{l}", sv["ckvn"], dkv)
        dwbd = mm_tn(f"pool_dw_{l}", sv["dpool"], dyw)
        small["pool_w"][l] = jnp.stack([dwbd[64 * gi:64 * gi + 64, 64 * gi:64 * gi + 64] for gi in range(4)])
        small["pool_scale"][l], small["gq"][l], small["gkv"][l] = dscale[0], dgq[0], dgkv[0]
        dh_ext = jnp.concatenate([du, dh_rest], axis=1)
        dwin_e = mm_tn(f"mix_dwin_{l}", sv["x1b"], dh_ext)
        small["win"][l] = jnp.concatenate(
            [dwin_e[:, :D_IN - QK_ROPE], _unswap_add(dwin_e[:, D_IN - QK_ROPE:D_IN], dwin_e[:, D_IN:])], axis=-1)
        dy = mm_nt_res(f"mix_dx_{l}", [dh_ext], [win_ext], [l], dres, f32)
        dzb, dres, dlg[0], dlb[0] = ln_bwd(f"ln1_bwd_{l}", dy, sv["z1"], vec(lng[l, 0]), 0.5)
        dh = ffn_bwd_da(f"ffn1_bwd_da_{l}", dzb, f1w2, l, sv["gate1"], sv["up1"])
        G["f1w2"] = mm_tn(f"ffn1_dw2_{l}", sv["act1"], dzb, "nat", l, G["f1w2"])
        G["f1w13"] = mm_tn(f"ffn1_dw13_{l}", sv["x0b"], dh, "shard", l, G["f1w13"])
        dy = ffn_dx(f"ffn1_dx_{l}", dh, g_f1w13, l, dres)
        small["lng"][l] = jnp.concatenate(dlg, axis=0)
        small["lnb"][l] = jnp.concatenate(dlb, axis=0)
    grad_x = dy[None]

    row_shards = lambda a, K: a.reshape(L, N_CHIPS, K // N_CHIPS, a.shape[-1])
    g_list = [G["f1w13"], row_shards(G["f1w2"], D_FF),
              jnp.stack(small["win"]).reshape(L, N_CHIPS, D_MODEL // N_CHIPS, D_IN),
              _to_col_shards(jnp.stack(small["wuq"])), _to_col_shards(jnp.stack(small["wukv"])),
              jnp.stack(small["wout"]).reshape(L, N_CHIPS, D_MODEL // N_CHIPS, D_MODEL),
              row_shards(G["mwq"], D_MODEL), G["mwkv"], row_shards(G["mwo"], D_MODEL),
              G["f2w13"], row_shards(G["f2w2"], D_FF)]
    big_grads = reduce_grads(g_list)

    rep = [jnp.stack(small["pool_w"]).reshape(-1), jnp.stack(small["pool_scale"]).reshape(-1),
           jnp.stack(small["gq"]).reshape(-1), jnp.stack(small["gkv"]).reshape(-1),
           jnp.stack(small["lng"]).reshape(-1), jnp.stack(small["lnb"]).reshape(-1)]
    sizes = [r.shape[0] for r in rep]
    packed = jnp.concatenate(rep)
    pad = (-packed.shape[0]) % 1024
    tot = allsum_small("allsum_small_grads", jnp.pad(packed, (0, pad)).reshape(-1, 128)).reshape(-1)
    offs = [0]
    for s_ in sizes:
        offs.append(offs[-1] + s_)
    parts = [tot[offs[i]:offs[i + 1]] for i in range(len(sizes))]
    g_pool_w = parts[0].reshape(pool_w.shape)
    g_pool_scale = parts[1].reshape(pool_scale.shape)
    g_gq = parts[2].reshape(q_norm_g.shape)
    g_gkv = parts[3].reshape(kv_norm_g.shape)
    shard_cols = lambda a: lax.dynamic_slice_in_dim(a.reshape(L, 4, D_MODEL), chip * (D_MODEL // N_CHIPS),
                                                    D_MODEL // N_CHIPS, axis=2)
    g_lng, g_lnb = shard_cols(parts[4]), shard_cols(parts[5])

    out_names = ("lng", "lnb", "f1w13", "f1w2", "win", "pool_w", "pool_scale", "gq", "wuq", "gkv", "wukv", "wout", "mwq",
                 "mwkv", "mwo", "f2w13", "f2w2")
    big.update(lng=g_lng, lnb=g_lnb, pool_w=g_pool_w, pool_scale=g_pool_scale, gq=g_gq, gkv=g_gkv)
    late = ("f1w13", "f1w2")
    ws = [ln_g, ln_b, ffn1_w13, ffn1_w2, w_in, pool_w, pool_scale, q_norm_g, w_uq, kv_norm_g, w_ukv, w_out, mem_wq,
          mem_wkv, mem_wo, ffn2_w13, ffn2_w2]
    ms = [m_ln_g, m_ln_b, m_ffn1_w13, m_ffn1_w2, m_w_in, m_pool_w, m_pool_scale, m_q_norm_g, m_w_uq, m_kv_norm_g, m_w_ukv,
          m_w_out, m_mem_wq, m_mem_wkv, m_mem_wo, m_ffn2_w13, m_ffn2_w2]
    vs = [v_ln_g, v_ln_b, v_ffn1_w13, v_ffn1_w2, v_w_in, v_pool_w, v_pool_scale, v_q_norm_g, v_w_uq, v_kv_norm_g, v_w_ukv,
          v_w_out, v_mem_wq, v_mem_wkv, v_mem_wo, v_ffn2_w13, v_ffn2_w2]
    res = {}

    def update(a):
        res[a] = adamw(f"adamw_{a}", ws[a], big[out_names[a]].reshape(ws[a].shape), ms[a], vs[a])

    for a, n in enumerate(out_names):
        if n not in late:
            update(a)
    sums_b = red_end("b0", st_cb, 0, [sums1[n] for n in late], tuple(r[0] for r in res.values()))
    big.update(zip(late, share("b", late, sums_b)))
    for a, n in enumerate(out_names):
        if n in late:
            update(a)
    order = range(len(out_names))
    grads = [big[n].reshape(w_.shape) for n, w_ in zip(out_names, ws)]
    return (loss, grad_x, *grads, *[res[a][0] for a in order], *[res[a][1] for a in order], *[res[a][2] for a in order])


_HBM_SPEC = pl.BlockSpec(memory_space=pltpu.HBM)
_SEM_SPEC = pl.BlockSpec(memory_space=pltpu.SEMAPHORE)
_ANY_SPEC = pl.BlockSpec(memory_space=pl.ANY)
_DATAFLOW = pltpu.SideEffectType.DATAFLOW_SIDE_EFFECTING


def _split_call(name, body_fn, bufs, sems_in, sems_out_sizes, after):
    nb, ni, no = len(bufs), len(sems_in), len(sems_out_sizes)
    afters = () if after is None else tuple(after) if isinstance(after, (tuple, list)) else (after,)

    def body(*refs):
        k = nb + ni + len(afters)
        body_fn(refs[:nb], refs[nb:nb + ni], refs[k:k + no])
        refs[-1][...] = jnp.zeros((8, 128), f32)

    outs = pl.pallas_call(
        body, name=name,
        in_specs=[_HBM_SPEC] * nb + [_SEM_SPEC] * ni + [_ANY_SPEC] * len(afters),
        out_specs=[_SEM_SPEC] * no + [_HBM_SPEC] * nb + [pl.BlockSpec(memory_space=pltpu.VMEM)],
        out_shape=[pltpu.SemaphoreType.DMA((s,)) for s in sems_out_sizes]
        + [pltpu.HBM(b.shape, b.dtype) for b in bufs] + [SDS((8, 128), f32)],
        input_output_aliases={i: no + i for i in range(nb)},
        compiler_params=pltpu.CompilerParams(has_side_effects=_DATAFLOW),
    )(*[pltpu.with_memory_space_constraint(b, pltpu.HBM) for b in bufs], *sems_in, *afters)
    return list(outs[no:no + nb]), list(outs[:no]), outs[-1]


def _rcopy(src, dst, ssem, rsem, to):
    return pltpu.make_async_remote_copy(src_ref=src, dst_ref=dst, send_sem=ssem, recv_sem=rsem, device_id=to,
                                        device_id_type=MESH)


def gather_start(name, groups, after):
    flat = [b for bufs, _ in groups for b in bufs]
    sizes = [3 * len(bufs) for bufs, _ in groups for _ in range(2)]

    def body_fn(b_in, s_in, s_out):
        x, y, c = _me()
        q = 2 * x + y
        chips = _other_chips(x, y)
        pos = 0
        for gi, (bufs, owner) in enumerate(groups):
            refs = b_in[pos:pos + len(bufs)]
            pos += len(bufs)

            @pl.when(c == owner)
            def _(refs=refs, send=s_out[2 * gi], recv=s_out[2 * gi + 1]):
                for a, r in enumerate(refs):
                    for k, (cx, cy) in enumerate(chips):
                        _rcopy(r.at[q], r.at[q], send.at[3 * a + k], recv.at[3 * a + k], (cx, cy, c)).start()

    outs, sems, token = _split_call(name, body_fn, flat, [], sizes, after)
    res, pos = [], 0
    for gi, (bufs, owner) in enumerate(groups):
        res.append((outs[pos:pos + len(bufs)], sems[2 * gi], sems[2 * gi + 1], owner))
        pos += len(bufs)
    return res, token


def gather_forward(name, grp, after):
    bufs, send, recv, owner = grp
    n3 = 3 * len(bufs)

    def body_fn(b_in, s_in, s_out):
        x, y, c = _me()
        q = 2 * x + y
        sibling = (x, y, 1 - c)
        chips = _other_chips(x, y)

        @pl.when(c == owner)
        def _():
            for a, r in enumerate(b_in):
                for k, (cx, cy) in enumerate(chips):
                    i = 3 * a + k
                    land = r.at[2 * cx + cy]
                    _rcopy(r.at[q], r.at[q], s_in[0].at[i], s_in[1].at[i], (cx, cy, c)).wait_send()
                    _rcopy(land, land, s_in[0].at[i], s_in[1].at[i], (cx, cy, c)).wait_recv()
                    _rcopy(land, land, s_out[0].at[i], s_out[1].at[i], sibling).start()

    outs, sems, token = _split_call(name, body_fn, bufs, [send, recv], [n3, n3], after)
    return (outs, sems[0], sems[1], owner), token


def gather_finish(name, grp, after):
    bufs, fsend, frecv, owner = grp

    def body_fn(b_in, s_in, s_out):
        x, y, c = _me()
        sibling = (x, y, 1 - c)
        chips = _other_chips(x, y)

        def each(wait):
            for a, r in enumerate(b_in):
                for k, (cx, cy) in enumerate(chips):
                    land = r.at[2 * cx + cy]
                    wait(_rcopy(land, land, s_in[0].at[3 * a + k], s_in[1].at[3 * a + k], sibling))

        @pl.when(c == owner)
        def _():
            each(lambda cp: cp.wait_send())

        @pl.when(c != owner)
        def _():
            each(lambda cp: cp.wait_recv())

    outs, _, _ = _split_call(name, body_fn, bufs, [fsend, frecv], [], after)
    return outs


def _by_owner(owners):
    return [[a for a, o_ in enumerate(owners) if o_ == o] for o in range(2)]


def pair_send_start(name, gs, owners, after):
    n = len(gs)
    lands = [lax.empty(g.shape, g.dtype) for g in gs]

    def body_fn(b_in, s_in, s_out):
        x, y, c = _me()
        for o, idx in enumerate(_by_owner(owners)):
            @pl.when(c == 1 - o)
            def _(o=o, idx=idx):
                for a in idx:
                    _rcopy(b_in[a], b_in[n + a], s_out[0].at[a], s_out[1].at[a], (x, y, o)).start()

    outs, sems, token = _split_call(name, body_fn, list(gs) + lands, [], [n, n], after)
    return (outs[:n], outs[n:], sems[0], sems[1], owners), token


def pair_send_wait(name, st, after):
    gs, lands, send, recv, owners = st
    n = len(gs)

    def body_fn(b_in, s_in, s_out):
        x, y, c = _me()
        for o, idx in enumerate(_by_owner(owners)):
            @pl.when(c == 1 - o)
            def _(o=o, idx=idx):
                for a in idx:
                    _rcopy(b_in[a], b_in[n + a], s_in[0].at[a], s_in[1].at[a], (x, y, o)).wait_send()

            @pl.when(c == o)
            def _(o=o, idx=idx):
                for a in idx:
                    _rcopy(b_in[a], b_in[n + a], s_in[0].at[a], s_in[1].at[a], (x, y, 1 - o)).wait_recv()

    outs, _, _ = _split_call(name, body_fn, list(gs) + list(lands), [send, recv], [], after)
    return outs[:n], outs[n:]


def chip_exchange_start(name, psums, owners, after):
    n = len(psums)
    lands = [lax.empty((3,) + p.shape[1:], p.dtype) for p in psums]

    def body_fn(b_in, s_in, s_out):
        x, y, c = _me()
        chips = _other_chips(x, y)
        for o, idx in enumerate(_by_owner(owners)):
            @pl.when(c == o)
            def _(idx=idx):
                for a in idx:
                    for k, (cx, cy) in enumerate(chips):
                        _rcopy(b_in[a].at[2 * cx + cy], b_in[n + a].at[k], s_out[0].at[3 * a + k],
                               s_out[1].at[3 * a + k], (cx, cy, c)).start()

    outs, sems, token = _split_call(name, body_fn, list(psums) + lands, [], [3 * n, 3 * n], after)
    return (outs[:n], outs[n:], sems[0], sems[1], owners), token


def chip_exchange_wait(name, st, after):
    psums, lands, send, recv, owners = st
    n = len(psums)

    def body_fn(b_in, s_in, s_out):
        x, y, c = _me()
        chips = _other_chips(x, y)
        for o, idx in enumerate(_by_owner(owners)):
            @pl.when(c == o)
            def _(idx=idx):
                for a in idx:
                    for k, (cx, cy) in enumerate(chips):
                        cp = _rcopy(b_in[a].at[2 * cx + cy], b_in[n + a].at[k], s_in[0].at[3 * a + k],
                                    s_in[1].at[3 * a + k], (cx, cy, c))
                        cp.wait_send()
                        cp.wait_recv()

    outs, _, _ = _split_call(name, body_fn, list(psums) + list(lands), [send, recv], [], after)
    return outs[:n], outs[n:]


def pair_sum(name, g, recv, flag):
    shape = g.shape
    cols = shape[-1]
    rows = math.prod(shape[:-1])
    tr = _row_tile(rows, cols, target=4 * 2**20)

    def body(f_ref, g_ref, r_ref, o_ref):
        o_ref[...] = (g_ref[...] + r_ref[...]).astype(bf16)

    blk = pl.BlockSpec((tr, cols), lambda i, f_ref: (i * f_ref[0], 0))
    out = pl.pallas_call(
        body, name=name,
        grid_spec=pltpu.PrefetchScalarGridSpec(num_scalar_prefetch=1, grid=(rows // tr,), in_specs=[blk, blk],
                                               out_specs=blk),
        out_shape=SDS((rows, cols), bf16), compiler_params=_cp("arbitrary"),
    )(flag, g.reshape(rows, cols), recv.reshape(rows, cols))
    return out.reshape(shape)


def chip_sum(name, psum, recv, qf_arr, layer, prev):
    shard = psum.shape[1:]
    cols = shard[-1]
    rows = math.prod(shard[:-1])
    tr = _row_tile(rows, cols, target=4 * 2**20)

    def body(qf_ref, p_ref, r_ref, *rest):
        rest[-1][0] = ((p_ref[0].astype(f32) + r_ref[0].astype(f32)) + r_ref[1].astype(f32)) + r_ref[2].astype(f32)

    in_specs = [pl.BlockSpec((1, tr, cols), lambda i, qf: (qf[0], i * qf[1], 0)),
                pl.BlockSpec((3, tr, cols), lambda i, qf: (0, i * qf[1], 0))]
    args = [qf_arr, psum.reshape(N_CHIPS, rows, cols), recv.reshape(3, rows, cols)]
    aliases = {}
    if prev is not None:
        in_specs.append(pl.BlockSpec(memory_space=pl.ANY))
        args.append(prev.reshape(DEPTH, rows, cols))
        aliases = {3: 0}
    out = pl.pallas_call(
        body, name=name,
        grid_spec=pltpu.PrefetchScalarGridSpec(
            num_scalar_prefetch=1, grid=(rows // tr,), in_specs=in_specs,
            out_specs=pl.BlockSpec((1, tr, cols), lambda i, qf: (layer, i * qf[1], 0))),
        out_shape=SDS((DEPTH, rows, cols), f32), input_output_aliases=aliases, compiler_params=_cp("arbitrary"),
    )(*args)
    return out.reshape((DEPTH,) + shard)


W_NAMES = ("f1w13", "f1w2", "win", "wuq", "wukv", "wout", "mwq", "mwkv", "mwo", "f2w13", "f2w2")
MIX_NAMES = ("win", "wuq", "wukv")
MID_NAMES = ("wout", "mwq", "mwkv", "mwo")
FFN2_NAMES = ("f2w13", "f2w2")
REDUCER = (dict(f1w13=0, f1w2=1, f2w13=0, win=0, wuq=0, wukv=0, f2w2=1, mwkv=1, wout=1, mwq=1, mwo=1),
           dict(f1w13=0, f2w2=0, mwkv=0, wout=0, f2w13=1, f1w2=1, mwq=1, mwo=1, win=1, wuq=1, wukv=1))


def kernel(x, mem, positions, ln_g, ln_b, ffn1_w13, ffn1_w2, w_in, pool_w, pool_scale, q_norm_g, w_uq, kv_norm_g, w_ukv, w_out, mem_wq, mem_wkv, mem_wo, ffn2_w13, ffn2_w2, loss_target, m_ln_g, m_ln_b, m_ffn1_w13, m_ffn1_w2, m_w_in, m_pool_w, m_pool_scale, m_q_norm_g, m_w_uq, m_kv_norm_g, m_w_ukv, m_w_out, m_mem_wq, m_mem_wkv, m_mem_wo, m_ffn2_w13, m_ffn2_w2, v_ln_g, v_ln_b, v_ffn1_w13, v_ffn1_w2, v_w_in, v_pool_w, v_pool_scale, v_q_norm_g, v_w_uq, v_kv_norm_g, v_w_ukv, v_w_out, v_mem_wq, v_mem_wkv, v_mem_wo, v_ffn2_w13, v_ffn2_w2):
    L = DEPTH
    qx, qy, _ = _me()
    chip = 2 * qx + qy
    vec = lambda a: a.reshape(1, -1)

    shards = dict(zip(W_NAMES, (ffn1_w13, ffn1_w2, w_in, w_uq, w_ukv, w_out, mem_wq, mem_wkv, mem_wo, ffn2_w13, ffn2_w2)))

    def place(sh, slot):
        return lax.dynamic_update_slice(jnp.zeros((N_CHIPS,) + sh.shape, bf16), sh.astype(bf16)[None],
                                        (slot,) + (0,) * sh.ndim)

    first = ("f1w13", "f1w2")
    bufs = [dict(), dict()]
    for n in first:
        bufs[0][n] = place(shards[n][0], chip)
    gw = [dict(), dict()]
    (g0,), tok = gather_start("gather_a_start", [([bufs[0][n] for n in first], 0)], None)
    chip_then = chip + tok[0, 0].astype(jnp.int32)
    for l in range(L):
        for n in W_NAMES:
            if n not in bufs[l]:
                bufs[l][n] = place(shards[n][l], chip_then)
    others = tuple(bufs[l][n] for l in range(L) for n in W_NAMES if (l, n) not in ((0, first[0]), (0, first[1])))
    g0, tok = gather_forward("gather_a_forward", g0, others)

    ln_pad = jnp.zeros((2, L, 4, N_CHIPS, D_MODEL // N_CHIPS), f32)
    ln_pad = lax.dynamic_update_slice(ln_pad, jnp.stack([ln_g, ln_b])[:, :, :, None, :], (0, 0, 0, chip, 0))
    ln_sum = allsum_small("allsum_ln", ln_pad.reshape(-1, 128), (tok,))
    ln_full = (ln_sum * 0.5).reshape(2, L, 4, D_MODEL)
    lng, lnb = ln_full[0], ln_full[1]

    gw[0]["f1w13"], gw[0]["f1w2"] = gather_finish("gather_a_finish", g0, ln_sum)
    (g_mix, g_mid, g_ffn2, g_l1), tok_b = gather_start(
        "gather_b_start",
        [([bufs[0][n] for n in MIX_NAMES], 0), ([bufs[0][n] for n in MID_NAMES], 0), ([bufs[0][n] for n in FFN2_NAMES], 0),
         ([bufs[1][n] for n in W_NAMES], 1)], ln_sum)

    half = QK_ROPE // 2
    inv_freq = ROPE_BASE ** (-jnp.arange(half, dtype=f32) / half)
    ang = positions[0].astype(f32)[:, None] * inv_freq
    cos, sin = jnp.cos(ang), jnp.sin(ang)
    cs = jnp.concatenate([cos, cos, sin, sin], axis=-1)

    memb = mem[0].astype(bf16)
    xf = x[0]
    xb = xf.astype(bf16)
    dep = (tok_b,)

    saved, W = [], [None, None]
    for l in range(L):
        sv = {}
        if l == 1:
            gl1 = gather_finish("gather_l1_finish", g_l1, xb)
            gw[1] = dict(zip(W_NAMES, gl1))
        sv["x0b"] = xb
        f1w13 = gw[l]["f1w13"][None]
        gate, up, act = ffn_up(f"ffn1_up_{l}", xb, f1w13, 0, dep)
        dep = ()
        if l == 0:
            g_mix, _ = gather_forward("gather_mix_forward", g_mix, act)
        z1, x1f, x1b = proj_res_ln(f"ffn1_down_{l}", [act], [gw[l]["f1w2"].reshape(1, D_FF, D_MODEL)], [0], xf,
                                   vec(lng[l, 0]), vec(lnb[l, 0]), 0.5)
        sv.update(gate1=gate, up1=up, act1=act, z1=z1, x1b=x1b)
        if l == 0:
            gw[0].update(zip(MIX_NAMES, gather_finish("gather_mix_finish", g_mix, x1b)))
            g_mid, _ = gather_forward("gather_mid_forward", g_mid, x1b)
        win = gw[l]["win"].reshape(D_MODEL, D_IN)
        win_ext = jnp.concatenate([win, _swap_half(win[:, D_IN - QK_ROPE:])], axis=-1)[None]
        wuq = _from_col_shards(gw[l]["wuq"]).reshape(Q_LORA, MLA_HEADS, QK_NOPE + QK_ROPE)
        wq_ext = jnp.concatenate([wuq, _swap_half(wuq[..., QK_NOPE:])], axis=-1).transpose(1, 0, 2)[None]
        wukv = _from_col_shards(gw[l]["wukv"])[None]
        wbd = _block_diag(pool_w[l][None].astype(bf16))[0]
        u, cq, ckv, cqn, ckvn, q, k, v = mix_pre(f"mix_pre_{l}", x1b, win_ext, wq_ext, wukv, 0,
                                                   vec(q_norm_g[l]), vec(kv_norm_g[l]), cs)
        dpool, ypool = pool_fwd(f"pool_fwd_{l}", u, wbd, vec(pool_scale[l]))
        o, lse = mla_attn_fwd(f"mla_fwd_{l}", q, k, v)
        if l == 0:
            gw[0].update(zip(MID_NAMES, gather_finish("gather_mid_finish", g_mid, o)))
            g_ffn2, tok_f = gather_forward("gather_ffn2_forward", g_ffn2, o)
            g_l1, tok_l = gather_forward("gather_l1_forward", g_l1, o)
            dep = (tok_f, tok_l)
        wout = gw[l]["wout"].reshape(D_MODEL, D_MODEL)
        wout_pool, wout_mla = wout[None, :POOL_WIDTH], wout[None, POOL_WIDTH:]
        mwq = gw[l]["mwq"].reshape(1, D_MODEL, D_MODEL)
        mwo = gw[l]["mwo"].reshape(1, D_MODEL, D_MODEL)
        mwkv = gw[l]["mwkv"][None]
        z2, x2f, x2b = proj_res_ln(f"mix_out_{l}", [ypool, o], [wout_pool, wout_mla], [0, 0], x1f,
                                   vec(lng[l, 1]), vec(lnb[l, 1]), 1.0, dep)
        dep = ()
        sv.update(cq=cq, ckv=ckv, cqn=cqn, ckvn=ckvn, q=q, k=k, v=v, dpool=dpool, ypool=ypool, o=o, lse=lse, z2=z2, x2b=x2b)
        kvm = mm_nn_shard(f"mem_kv_{l}", memb, mwkv, 0)
        cq_, co_, z3, x3f, x3b = cross_fwd(f"cross_fwd_{l}", x2b, x2f, mwq, mwo, 0, kvm, vec(lng[l, 2]), vec(lnb[l, 2]))
        sv.update(kvm=kvm, crq=cq_, cro=co_, z3=z3, x3b=x3b)
        if l == 0:
            gw[0].update(zip(FFN2_NAMES, gather_finish("gather_ffn2_finish", g_ffn2, x3b)))
        f2w13 = gw[l]["f2w13"][None]
        f2w2 = gw[l]["f2w2"].reshape(1, D_FF, D_MODEL)
        gate, up, act = ffn_up(f"ffn2_up_{l}", x3b, f2w13, 0)
        z4, xf, xb = proj_res_ln(f"ffn2_down_{l}", [act], [f2w2], [0], x3f, vec(lng[l, 3]), vec(lnb[l, 3]), 0.5)
        sv.update(gate2=gate, up2=up, act2=act, z4=z4)
        W[l] = dict(f1w13=f1w13, f1w2=gw[l]["f1w2"].reshape(1, D_FF, D_MODEL), win_ext=win_ext, wq_ext=wq_ext, wukv=wukv,
                    wbd=wbd, wout_pool=wout_pool, wout_mla=wout_mla, mwq=mwq, mwo=mwo, f2w13=f2w13, f2w2=f2w2)
        saved.append(sv)

    dy, loss_blk = loss_grad("loss_grad", xf, loss_target[0])
    loss = lax.psum(loss_blk[0, 0], ("x", "y", "c"))

    row_shards = lambda a: a.reshape(N_CHIPS, a.shape[0] // N_CHIPS, a.shape[1])
    small = {k_: [None] * L for k_ in ("pool_w", "pool_scale", "gq", "gkv", "lng", "lnb")}
    rest_names = [n for n in W_NAMES if n not in ("f1w13", "f1w2")]
    core = lax.axis_index("c")
    flags = [jnp.reshape(core == o, (1,)).astype(jnp.int32) for o in range(2)]
    qfs = [jnp.stack([chip, (core == o).astype(jnp.int32)]).astype(jnp.int32) for o in range(2)]

    def red_begin(tag, names, gs, layer):
        owners = [REDUCER[layer][n] for n in names]
        st, tok_ = pair_send_start(f"pair_send_start_{tag}", gs, owners, None)
        return (st, owners), tok_

    def red_mid(tag, sto, after):
        st, owners = sto
        gs_, lands_ = pair_send_wait(f"pair_send_wait_{tag}", st, after)
        ps = [pair_sum(f"pair_sum_{tag}_{a}", g_, r_, flags[o]) for a, (g_, r_, o) in enumerate(zip(gs_, lands_, owners))]
        st, tok_ = chip_exchange_start(f"chip_exchange_start_{tag}", ps, owners, None)
        return (st, owners), tok_

    def red_end(tag, sto, layer, prevs, after):
        st, owners = sto
        ps, lands_ = chip_exchange_wait(f"chip_exchange_wait_{tag}", st, after)
        return [chip_sum(f"chip_sum_{tag}_{a}", p_, r_, qfs[o], layer, s_)
                for a, (p_, r_, s_, o) in enumerate(zip(ps, lands_, prevs, owners))]

    def share(tag, names, sums_):
        return pair_share(f"pair_share_{tag}", sums_, [(REDUCER[0][n], REDUCER[1][n]) for n in names])

    st_p1 = st_c1 = st_pa = st_ca = None
    for l in reversed(range(L)):
        sv, w = saved[l], W[l]
        g = {}
        dlg, dlb = [None] * 4, [None] * 4
        dzb, dres, dlg[3], dlb[3] = ln_bwd(f"ln4_bwd_{l}", dy, sv["z4"], vec(lng[l, 3]), 0.5, dep)
        dep = ()
        dh = ffn_bwd_da(f"ffn2_bwd_da_{l}", dzb, w["f2w2"], 0, sv["gate2"], sv["up2"])
        g["f2w2"] = row_shards(mm_tn(f"ffn2_dw2_{l}", sv["act2"], dzb))
        g["f2w13"] = mm_tn(f"ffn2_dw13_{l}", sv["x3b"], dh, True)
        dy = ffn_dx(f"ffn2_dx_{l}", dh, w["f2w13"], 0, dres)
        if l == 0:
            st_c1, tok = red_mid("l1", st_p1, dy)
            dep = (tok, g["f2w2"], g["f2w13"])
        dzb, dres, dlg[2], dlb[2] = ln_bwd(f"ln3_bwd_{l}", dy, sv["z3"], vec(lng[l, 2]), 1.0, dep)
        dep = ()
        dqc, dkvm = cross_bwd(f"cross_bwd_{l}", dzb, w["mwo"], 0, sv["crq"], sv["kvm"])
        g["mwo"] = row_shards(mm_tn(f"cross_dwo_{l}", sv["cro"], dzb))
        g["mwq"] = row_shards(mm_tn(f"cross_dwq_{l}", sv["x2b"], dqc))
        g["mwkv"] = mm_tn(f"cross_dwkv_{l}", memb, dkvm, True)
        dy = mm_nt_res(f"cross_dx_{l}", [dqc], [w["mwq"]], [0], dres, f32)
        dzb, dres, dlg[1], dlb[1] = ln_bwd(f"ln2_bwd_{l}", dy, sv["z2"], vec(lng[l, 1]), 1.0)
        dyp = mm_nt_res(f"mix_dpool_{l}", [dzb], [w["wout_pool"]], [0], None, bf16)
        do = mm_nt_res(f"mix_do_{l}", [dzb], [w["wout_mla"]], [0], None, bf16)
        dwo_p = mm_tn(f"mix_dwout_pool_{l}", sv["ypool"], dzb)
        dwo_m = mm_tn(f"mix_dwout_mla_{l}", sv["o"], dzb)
        g["wout"] = row_shards(jnp.concatenate([dwo_p, dwo_m], axis=0))
        dq, dk, dv = mla_attn_bwd(f"mla_bwd_{l}", sv["q"], sv["k"], sv["v"], sv["o"], do, sv["lse"])
        dqe, dkv, dh_rest, dgq, dgkv = mix_post_bwd(f"mix_post_bwd_{l}", dq, dk, dv, w["wq_ext"], w["wukv"], 0, sv["cq"],
                                                     sv["ckv"], vec(q_norm_g[l]), vec(kv_norm_g[l]), cs)
        du, dyw, dscale = pool_bwd(f"pool_bwd_{l}", dyp, sv["dpool"], w["wbd"], vec(pool_scale[l]))
        dwq_e = mm_tn(f"mix_dwuq_{l}", sv["cqn"], dqe).reshape(Q_LORA, MLA_HEADS, 256)
        g["wuq"] = _to_col_shards(jnp.concatenate(
            [dwq_e[..., :QK_NOPE], _unswap_add(dwq_e[..., QK_NOPE:QK_NOPE + QK_ROPE], dwq_e[..., QK_NOPE + QK_ROPE:])],
            axis=-1).reshape(Q_LORA, MLA_HEADS * (QK_NOPE + QK_ROPE)))
        g["wukv"] = _to_col_shards(mm_tn(f"mix_dwukv_{l}", sv["ckvn"], dkv))
        dwbd = mm_tn(f"pool_dw_{l}", sv["dpool"], dyw)
        small["pool_w"][l] = jnp.stack([dwbd[64 * gi:64 * gi + 64, 64 * gi:64 * gi + 64] for gi in range(4)])
        small["pool_scale"][l], small["gq"][l], small["gkv"][l] = dscale[0], dgq[0], dgkv[0]
        dh_ext = jnp.concatenate([du, dh_rest], axis=1)
        dwin_e = mm_tn(f"mix_dwin_{l}", sv["x1b"], dh_ext)
        g["win"] = row_shards(jnp.concatenate(
            [dwin_e[:, :D_IN - QK_ROPE], _unswap_add(dwin_e[:, D_IN - QK_ROPE:D_IN], dwin_e[:, D_IN:])], axis=-1))
        dy = mm_nt_res(f"mix_dx_{l}", [dh_ext], [w["win_ext"]], [0], dres, f32)
        if l == 0:
            st_pa, tok = red_begin("a0", rest_names, [g[n] for n in rest_names], 0)
            dep = (tok,)
        dzb, dres, dlg[0], dlb[0] = ln_bwd(f"ln1_bwd_{l}", dy, sv["z1"], vec(lng[l, 0]), 0.5, dep)
        dep = ()
        dh = ffn_bwd_da(f"ffn1_bwd_da_{l}", dzb, w["f1w2"], 0, sv["gate1"], sv["up1"])
        dy = ffn_dx(f"ffn1_dx_{l}", dh, w["f1w13"], 0, dres)
        if l == 0:
            st_ca, tok = red_mid("a0", st_pa, dy)
            dep = (tok,)
        g["f1w2"] = row_shards(mm_tn(f"ffn1_dw2_{l}", sv["act1"], dzb, False, dep))
        g["f1w13"] = mm_tn(f"ffn1_dw13_{l}", sv["x0b"], dh, True, dep)
        dep = ()
        small["lng"][l] = jnp.concatenate(dlg, axis=0)
        small["lnb"][l] = jnp.concatenate(dlb, axis=0)
        if l == 1:
            st_p1, tok = red_begin("l1", W_NAMES, [g[n] for n in W_NAMES], 1)
            dep = (tok,)
    grad_x = dy[None]

    st_pb, _ = red_begin("b0", ("f1w13", "f1w2"), [g["f1w13"], g["f1w2"]], 0)
    sums1 = dict(zip(W_NAMES, red_end("l1", st_c1, 1, [None] * len(W_NAMES), g["f1w13"])))
    st_cb, _ = red_mid("b0", st_pb, sums1["f1w13"])
    sums0 = red_end("a0", st_ca, 0, [sums1[n] for n in rest_names], sums1["f1w2"])
    big = dict(zip(rest_names, share("a", rest_names, sums0)))

    rep = [jnp.stack(small["pool_w"]).reshape(-1), jnp.stack(small["pool_scale"]).reshape(-1),
           jnp.stack(small["gq"]).reshape(-1), jnp.stack(small["gkv"]).reshape(-1),
           jnp.stack(small["lng"]).reshape(-1), jnp.stack(small["lnb"]).reshape(-1)]
    sizes = [r.shape[0] for r in rep]
    packed = jnp.concatenate(rep)
    pad = (-packed.shape[0]) % 1024
    tot = allsum_small("allsum_small_grads", jnp.pad(packed, (0, pad)).reshape(-1, 128)).reshape(-1)
    offs = [0]
    for s_ in sizes:
        offs.append(offs[-1] + s_)
    parts = [tot[offs[i]:offs[i + 1]] for i in range(len(sizes))]
    g_pool_w = parts[0].reshape(pool_w.shape)
    g_pool_scale = parts[1].reshape(pool_scale.shape)
    g_gq = parts[2].reshape(q_norm_g.shape)
    g_gkv = parts[3].reshape(kv_norm_g.shape)
    shard_cols = lambda a: lax.dynamic_slice_in_dim(a.reshape(L, 4, D_MODEL), chip * (D_MODEL // N_CHIPS),
                                                    D_MODEL // N_CHIPS, axis=2)
    g_lng, g_lnb = shard_cols(parts[4]), shard_cols(parts[5])

    out_names = ("lng", "lnb", "f1w13", "f1w2", "win", "pool_w", "pool_scale", "gq", "wuq", "gkv", "wukv", "wout", "mwq",
                 "mwkv", "mwo", "f2w13", "f2w2")
    big.update(lng=g_lng, lnb=g_lnb, pool_w=g_pool_w, pool_scale=g_pool_scale, gq=g_gq, gkv=g_gkv)
    late = ("f1w13", "f1w2")
    ws = [ln_g, ln_b, ffn1_w13, ffn1_w2, w_in, pool_w, pool_scale, q_norm_g, w_uq, kv_norm_g, w_ukv, w_out, mem_wq,
          mem_wkv, mem_wo, ffn2_w13, ffn2_w2]
    ms = [m_ln_g, m_ln_b, m_ffn1_w13, m_ffn1_w2, m_w_in, m_pool_w, m_pool_scale, m_q_norm_g, m_w_uq, m_kv_norm_g, m_w_ukv,
          m_w_out, m_mem_wq, m_mem_wkv, m_mem_wo, m_ffn2_w13, m_ffn2_w2]
    vs = [v_ln_g, v_ln_b, v_ffn1_w13, v_ffn1_w2, v_w_in, v_pool_w, v_pool_scale, v_q_norm_g, v_w_uq, v_kv_norm_g, v_w_ukv,
          v_w_out, v_mem_wq, v_mem_wkv, v_mem_wo, v_ffn2_w13, v_ffn2_w2]
    res = {}

    def update(a):
        res[a] = adamw(f"adamw_{a}", ws[a], big[out_names[a]].reshape(ws[a].shape), ms[a], vs[a])

    for a, n in enumerate(out_names):
        if n not in late:
            update(a)
    sums_b = red_end("b0", st_cb, 0, [sums1[n] for n in late], tuple(r[0] for r in res.values()))
    big.update(zip(late, share("b", late, sums_b)))
    for a, n in enumerate(out_names):
        if n in late:
            update(a)
    order = range(len(out_names))
    grads = [big[n].reshape(w_.shape) for n, w_ in zip(out_names, ws)]
    return (loss, grad_x, *grads, *[res[a][0] for a in order], *[res[a][1] for a in order], *[res[a][2] for a in order])
```

```python
import functools
import math

import jax
import jax.numpy as jnp
from jax import lax
from jax.experimental import pallas as pl
from jax.experimental.pallas import tpu as pltpu

f32 = jnp.float32
bf16 = jnp.bfloat16
SDS = jax.ShapeDtypeStruct
MESH = pl.DeviceIdType.MESH

D_MODEL = 1024
DEPTH = 2
N_MEM = 256
MEM_HEADS = 4
MEM_HEAD_DIM = D_MODEL // MEM_HEADS
POOL_WINDOWS = (2, 4, 8, 16)
POOL_WIDTH = 256
POOL_GROUP = 64
QK_NOPE = 128
QK_ROPE = 64
V_HEAD = 128
MLA_HEADS = 6
Q_LORA = 256
KV_LORA = 128
ROPE_BASE = 10000.0
D_FF = 2816
D_IN = POOL_WIDTH + Q_LORA + KV_LORA + QK_ROPE
ALPHA = (2 * DEPTH) ** 0.25
LN_EPS = 1e-5
RMS_EPS = 1e-6
NEG_INF = -1e30
MLA_SCALE = (QK_NOPE + QK_ROPE) ** -0.5
MEM_SCALE = MEM_HEAD_DIM ** -0.5
ADAM_LR = 0.001
ADAM_B1 = 0.9
ADAM_B2 = 0.999
ADAM_EPS = 1e-08
ADAM_WD = 0.01
ADAM_STEP = 10

N_CHIPS = 4
V7X_VMEM_LIMIT = 56 * 2**20
HALO = 16

_NT = (((1,), (1,)), ((), ()))
_TN = (((0,), (0,)), ((), ()))


def _dot(a, b):
    return jnp.dot(a, b, preferred_element_type=f32)


def _dot_nt(a, b):
    return lax.dot_general(a, b, _NT, preferred_element_type=f32)


def _dot_tn(a, b):
    return lax.dot_general(a, b, _TN, preferred_element_type=f32)


def _cp(*sem):
    return pltpu.CompilerParams(dimension_semantics=sem if sem else None, vmem_limit_bytes=V7X_VMEM_LIMIT)


_DEP_SPEC = pl.BlockSpec(memory_space=pl.ANY)


def _with_deps(body, n_in, deps):
    nd = len(deps)
    if not nd:
        return body

    def wrapped(*refs):
        return body(*refs[:n_in], *refs[n_in + nd:])

    return wrapped


def _tile(n, t):
    t = min(n, t)
    assert n % t == 0, (n, t)
    return t


def _row_tile(rows, cols, itemsize=4, target=2 * 2**20):
    best = None
    for t in range(16, rows + 1, 16):
        if rows % t == 0 and t * cols * itemsize <= target:
            best = t
    return best if best is not None else rows


def ffn_up(name, xb, w13, l, deps=()):
    S = xb.shape[0]
    ns = w13.shape[3]
    tm = _tile(S, 512)

    def body(x_ref, wg_ref, wu_ref, g_ref, u_ref, a_ref):
        x = x_ref[...]
        g = _dot(x, wg_ref[0, 0])
        u = _dot(x, wu_ref[0, 0])
        a = g * jax.nn.sigmoid(g) * u
        g_ref[...] = g.astype(bf16)
        u_ref[...] = u.astype(bf16)
        a_ref[...] = a.astype(bf16)

    out = SDS((S, 2 * ns), bf16)
    return pl.pallas_call(
        _with_deps(body, 3, deps), name=name, grid=(2, S // tm),
        in_specs=[pl.BlockSpec((tm, D_MODEL), lambda j, i: (i, 0)),
                  pl.BlockSpec((1, 1, D_MODEL, ns), lambda j, i: (l, j, 0, 0)),
                  pl.BlockSpec((1, 1, D_MODEL, ns), lambda j, i: (l, j + 2, 0, 0))] + [_DEP_SPEC] * len(deps),
        out_specs=[pl.BlockSpec((tm, ns), lambda j, i: (i, j))] * 3,
        out_shape=[out, out, out],
        compiler_params=_cp("parallel", "parallel"),
    )(xb, w13, w13, *deps)


def proj_res_ln(name, parts, ws, wl, x, g, b, rscale, deps=()):
    S = x.shape[0]
    tm = _tile(S, 512)
    n = len(parts)

    def body(*refs):
        p_refs, w_refs = refs[:n], refs[n:2 * n]
        x_ref, g_ref, b_ref, z_ref, y_ref, yb_ref = refs[2 * n:]
        acc = _dot(p_refs[0][...], w_refs[0][0])
        for k in range(1, n):
            acc = acc + _dot(p_refs[k][...], w_refs[k][0])
        if rscale != 1.0:
            acc = rscale * acc
        z = ALPHA * x_ref[...] + acc
        mu = jnp.mean(z, axis=-1, keepdims=True)
        zc = z - mu
        var = jnp.mean(zc * zc, axis=-1, keepdims=True)
        y = zc * lax.rsqrt(var + LN_EPS) * g_ref[...] + b_ref[...]
        z_ref[...] = z
        y_ref[...] = y
        yb_ref[...] = y.astype(bf16)

    row = lambda i: (i, 0)
    in_specs = [pl.BlockSpec((tm, p.shape[1]), row) for p in parts]
    in_specs += [pl.BlockSpec((1,) + w.shape[1:], functools.partial(lambda li, i: (li, 0, 0), li)) for w, li in zip(ws, wl)]
    in_specs += [pl.BlockSpec((tm, D_MODEL), row), pl.BlockSpec((1, D_MODEL), lambda i: (0, 0)),
                 pl.BlockSpec((1, D_MODEL), lambda i: (0, 0))] + [_DEP_SPEC] * len(deps)
    return pl.pallas_call(
        _with_deps(body, 2 * n + 3, deps), name=name, grid=(S // tm,), in_specs=in_specs,
        out_specs=[pl.BlockSpec((tm, D_MODEL), row)] * 3,
        out_shape=[SDS((S, D_MODEL), f32), SDS((S, D_MODEL), f32), SDS((S, D_MODEL), bf16)],
        compiler_params=_cp("parallel"),
    )(*parts, *ws, x, g, b, *deps)


def _ln_bwd_store(dyv, z_ref, g_ref, rscale, first, dzb_ref, dres_ref, dg_ref, db_ref):
    z = z_ref[...]
    mu = jnp.mean(z, axis=-1, keepdims=True)
    zc = z - mu
    rstd = lax.rsqrt(jnp.mean(zc * zc, axis=-1, keepdims=True) + LN_EPS)
    xhat = zc * rstd
    dxh = dyv * g_ref[...]
    m1 = jnp.mean(dxh, axis=-1, keepdims=True)
    m2 = jnp.mean(dxh * xhat, axis=-1, keepdims=True)
    dz = rstd * (dxh - m1 - xhat * m2)
    dzb_ref[...] = (rscale * dz).astype(bf16)
    dres_ref[...] = ALPHA * dz

    @pl.when(first)
    def _():
        dg_ref[...] = jnp.zeros_like(dg_ref)
        db_ref[...] = jnp.zeros_like(db_ref)

    dg_ref[...] += jnp.sum(dyv * xhat, axis=0, keepdims=True)
    db_ref[...] += jnp.sum(dyv, axis=0, keepdims=True)


def _ln_bwd_specs(S, tm, index):
    vec = pl.BlockSpec((1, D_MODEL), lambda *a: (0, 0))
    blk = pl.BlockSpec((tm, D_MODEL), index)
    in_specs = [blk, vec]
    out_specs = [blk, blk, vec, vec]
    out_shape = [SDS((S, D_MODEL), bf16), SDS((S, D_MODEL), f32), SDS((1, D_MODEL), f32), SDS((1, D_MODEL), f32)]
    return in_specs, out_specs, out_shape


def ffn_bwd_da(name, drb, w2, l, gate, up, deps=()):
    S = drb.shape[0]
    tm = _tile(S, 512)
    nh = D_FF // 2

    def body(dr_ref, w_ref, g_ref, u_ref, dh_ref):
        dr = dr_ref[...]
        for j in range(2):
            cols = slice(j * nh, (j + 1) * nh)
            da = _dot_nt(dr, w_ref[0, cols, :])
            g = g_ref[:, cols].astype(f32)
            u = u_ref[:, cols].astype(f32)
            sg = jax.nn.sigmoid(g)
            dh_ref[:, cols] = (da * u * (sg * (1.0 + g * (1.0 - sg)))).astype(bf16)
            dh_ref[:, D_FF + j * nh:D_FF + (j + 1) * nh] = (da * (g * sg)).astype(bf16)

    row = lambda i: (i, 0)
    return pl.pallas_call(
        _with_deps(body, 4, deps), name=name, grid=(S // tm,),
        in_specs=[pl.BlockSpec((tm, D_MODEL), row), pl.BlockSpec((1, D_FF, D_MODEL), lambda i: (l, 0, 0)),
                  pl.BlockSpec((tm, D_FF), row), pl.BlockSpec((tm, D_FF), row)] + [_DEP_SPEC] * len(deps),
        out_specs=pl.BlockSpec((tm, 2 * D_FF), row),
        out_shape=SDS((S, 2 * D_FF), bf16),
        compiler_params=_cp("parallel"),
    )(drb, w2, gate, up, *deps)


def ffn_dx(name, dh, w13, l, res, ln=None):
    S = dh.shape[0]
    ns = w13.shape[3]
    tm = _tile(S, 1024)
    last = N_CHIPS - 1
    row = lambda i, j: (i, 0)
    in_specs = [pl.BlockSpec((tm, ns), lambda i, j: (i, j)),
                pl.BlockSpec((1, 1, D_MODEL, ns), lambda i, j: (l, j, 0, 0)),
                pl.BlockSpec((tm, D_MODEL), row)]
    if ln is None:
        def body(dh_ref, w_ref, r_ref, o_ref):
            @pl.when(pl.program_id(1) == 0)
            def _():
                o_ref[...] = r_ref[...]

            o_ref[...] += _dot_nt(dh_ref[...], w_ref[0, 0])

        return pl.pallas_call(
            body, name=name, grid=(S // tm, N_CHIPS), in_specs=in_specs,
            out_specs=pl.BlockSpec((tm, D_MODEL), row), out_shape=SDS((S, D_MODEL), f32),
            compiler_params=_cp("parallel", "arbitrary"),
        )(dh, w13, res)

    z, g, rscale = ln

    def body_ln(dh_ref, w_ref, r_ref, z_ref, g_ref, dzb_ref, dres_ref, dg_ref, db_ref, acc_sc):
        i, j = pl.program_id(0), pl.program_id(1)

        @pl.when(j == 0)
        def _():
            acc_sc[...] = r_ref[...]

        acc_sc[...] += _dot_nt(dh_ref[...], w_ref[0, 0])

        @pl.when(j == last)
        def _():
            _ln_bwd_store(acc_sc[...], z_ref, g_ref, rscale, i == 0, dzb_ref, dres_ref, dg_ref, db_ref)

    ln_in, ln_out, ln_shape = _ln_bwd_specs(S, tm, row)
    return pl.pallas_call(
        body_ln, name=name, grid=(S // tm, N_CHIPS), in_specs=in_specs + ln_in, out_specs=ln_out, out_shape=ln_shape,
        scratch_shapes=[pltpu.VMEM((tm, D_MODEL), f32)], compiler_params=_cp("arbitrary", "arbitrary"),
    )(dh, w13, res, z, g)


def mm_nt_res(name, dys, ws, wl, res, out_dtype, ln=None):
    S = dys[0].shape[0]
    K = ws[0].shape[1]
    tm = _tile(S, 512)
    n = len(dys)
    n_in = 2 * n + (res is not None)

    def product(refs):
        acc = _dot_nt(refs[0][...], refs[n][0])
        for k in range(1, n):
            acc = acc + _dot_nt(refs[k][...], refs[n + k][0])
        if res is not None:
            acc = acc + refs[2 * n][...]
        return acc

    def body(*refs):
        refs[-1][...] = product(refs).astype(out_dtype)

    def body_ln(*refs):
        z_ref, g_ref, dzb_ref, dres_ref, dg_ref, db_ref = refs[n_in:]
        _ln_bwd_store(product(refs), z_ref, g_ref, ln[2], pl.program_id(0) == 0, dzb_ref, dres_ref, dg_ref, db_ref)

    row = lambda i: (i, 0)
    in_specs = [pl.BlockSpec((tm, d.shape[1]), row) for d in dys]
    in_specs += [pl.BlockSpec((1,) + w.shape[1:], functools.partial(lambda li, i: (li, 0, 0), li)) for w, li in zip(ws, wl)]
    args = list(dys) + list(ws)
    if res is not None:
        in_specs.append(pl.BlockSpec((tm, K), row))
        args.append(res)
    if ln is None:
        return pl.pallas_call(
            body, name=name, grid=(S // tm,), in_specs=in_specs,
            out_specs=pl.BlockSpec((tm, K), row), out_shape=SDS((S, K), out_dtype),
            compiler_params=_cp("parallel"),
        )(*args)
    ln_in, ln_out, ln_shape = _ln_bwd_specs(S, tm, row)
    return pl.pallas_call(
        body_ln, name=name, grid=(S // tm,), in_specs=in_specs + ln_in, out_specs=ln_out, out_shape=ln_shape,
        compiler_params=_cp("arbitrary"),
    )(*args, ln[0], ln[1])


def mm_tn(name, x, dy, col_shards=False, deps=()):
    S, K = x.shape
    N = dy.shape[1]
    ts = 512
    while ts * 2 <= min(S, 2048) and S % (ts * 2) == 0 and ts * 2 * K * 2 <= 6 * 2**20:
        ts *= 2
    ts = _tile(S, ts)
    if col_shards:
        tn = N // N_CHIPS
    else:
        tn = N
        while K * tn * 4 > 6 * 2**20 and tn % 256 == 0:
            tn //= 2
    nn = N // tn
    lead = ((0,) if col_shards else ()) + (slice(None), slice(None))

    def body(x_ref, dy_ref, o_ref):
        acc = _dot_tn(x_ref[...].astype(bf16), dy_ref[...].astype(bf16))

        @pl.when(pl.program_id(1) == 0)
        def _():
            o_ref[lead] = acc

        @pl.when(pl.program_id(1) != 0)
        def _():
            o_ref[lead] += acc

    if col_shards:
        out_spec = pl.BlockSpec((1, K, tn), lambda n, s: (n, 0, 0))
        out_shape = SDS((N_CHIPS, K, tn), f32)
    else:
        out_spec = pl.BlockSpec((K, tn), lambda n, s: (0, n))
        out_shape = SDS((K, N), f32)
    return pl.pallas_call(
        _with_deps(body, 2, deps), name=name, grid=(nn, S // ts),
        in_specs=[pl.BlockSpec((ts, K), lambda n, s: (s, 0)), pl.BlockSpec((ts, tn), lambda n, s: (s, n))]
        + [_DEP_SPEC] * len(deps),
        out_specs=out_spec, out_shape=out_shape, compiler_params=_cp("parallel", "arbitrary"),
    )(x, dy, *deps)


def mm_nn_shard(name, x, w, l):
    S, K = x.shape
    ns = w.shape[3]

    def body(x_ref, w_ref, o_ref):
        o_ref[...] = _dot(x_ref[...], w_ref[0, 0]).astype(bf16)

    return pl.pallas_call(
        body, name=name, grid=(N_CHIPS,),
        in_specs=[pl.BlockSpec((S, K), lambda j: (0, 0)), pl.BlockSpec((1, 1, K, ns), lambda j: (l, j, 0, 0))],
        out_specs=pl.BlockSpec((S, ns), lambda j: (0, j)), out_shape=SDS((S, N_CHIPS * ns), bf16),
        compiler_params=_cp("parallel"),
    )(x, w)


def loss_grad(name, y, t, ln):
    S = y.shape[0]
    tm = _tile(S, 512)
    z, g, rscale = ln

    def body(y_ref, t_ref, z_ref, g_ref, dzb_ref, dres_ref, dg_ref, db_ref, loss_ref):
        first = pl.program_id(0) == 0
        e = y_ref[...] - t_ref[...]
        _ln_bwd_store(e * (1.0 / D_MODEL), z_ref, g_ref, rscale, first, dzb_ref, dres_ref, dg_ref, db_ref)

        @pl.when(first)
        def _():
            loss_ref[...] = jnp.zeros_like(loss_ref)

        loss_ref[...] += jnp.full(loss_ref.shape, (0.5 / D_MODEL) * jnp.sum(e * e), f32)

    row = lambda i: (i, 0)
    ln_in, ln_out, ln_shape = _ln_bwd_specs(S, tm, row)
    return pl.pallas_call(
        body, name=name, grid=(S // tm,),
        in_specs=[pl.BlockSpec((tm, D_MODEL), row)] * 2 + ln_in,
        out_specs=ln_out + [pl.BlockSpec((8, 128), lambda i: (0, 0))],
        out_shape=ln_shape + [SDS((8, 128), f32)],
        compiler_params=_cp("arbitrary"),
    )(y, t, z, g)


def _half_sum(t):
    return t + pltpu.roll(t, 64, axis=1)


def mix_pre(name, xb, w_in, wq, wkv, l, gq, gkv, cs):
    S = xb.shape[0]
    tm = _tile(S, 512)
    H = MLA_HEADS
    W_EXT = w_in.shape[2]

    def body(x_ref, win_ref, wq_ref, wkv_ref, gq_ref, gkv_ref, cs_ref,
             u_ref, cq_ref, ckv_ref, cqn_ref, ckvn_ref, q_ref, k_ref, v_ref):
        h = _dot(x_ref[...], win_ref[0])
        u_ref[...] = h[:, :256]
        cq = h[:, 256:512]
        ckv = h[:, 512:640]
        cq_ref[...] = cq
        ckv_ref[...] = ckv
        cqn = (cq * lax.rsqrt(jnp.mean(cq * cq, axis=-1, keepdims=True) + RMS_EPS) * gq_ref[...]).astype(bf16)
        ckvn = (ckv * lax.rsqrt(jnp.mean(ckv * ckv, axis=-1, keepdims=True) + RMS_EPS) * gkv_ref[...]).astype(bf16)
        cqn_ref[...] = cqn
        ckvn_ref[...] = ckvn
        csv = cs_ref[...]
        lane = lax.broadcasted_iota(jnp.int32, (tm, 128), 1)
        kr = jnp.where(lane < 64, _half_sum(h[:, 640:768] * csv), 0.0).astype(bf16)
        kv = _dot(ckvn, wkv_ref[0])
        for hd in range(H):
            qe = _dot(cqn, wq_ref[0, hd])
            q_ref[hd, :, :128] = qe[:, :128].astype(bf16)
            q_ref[hd, :, 128:] = _half_sum(qe[:, 128:] * csv).astype(bf16)
            k_ref[hd, :, :128] = kv[:, 256 * hd:256 * hd + 128].astype(bf16)
            k_ref[hd, :, 128:] = kr
            v_ref[hd] = kv[:, 256 * hd + 128:256 * hd + 256].astype(bf16)

    row = lambda i: (i, 0)
    hrow = lambda i: (0, i, 0)
    return pl.pallas_call(
        body, name=name, grid=(S // tm,),
        in_specs=[pl.BlockSpec((tm, D_MODEL), row),
                  pl.BlockSpec((1, D_MODEL, W_EXT), lambda i: (l, 0, 0)),
                  pl.BlockSpec((1, H, Q_LORA, 256), lambda i: (l, 0, 0, 0)),
                  pl.BlockSpec((1, KV_LORA, H * 256), lambda i: (l, 0, 0)),
                  pl.BlockSpec((1, Q_LORA), lambda i: (0, 0)), pl.BlockSpec((1, KV_LORA), lambda i: (0, 0)),
                  pl.BlockSpec((tm, 128), row)],
        out_specs=[pl.BlockSpec((tm, 256), row), pl.BlockSpec((tm, Q_LORA), row), pl.BlockSpec((tm, KV_LORA), row),
                   pl.BlockSpec((tm, Q_LORA), row), pl.BlockSpec((tm, KV_LORA), row),
                   pl.BlockSpec((H, tm, 256), hrow), pl.BlockSpec((H, tm, 256), hrow), pl.BlockSpec((H, tm, 128), hrow)],
        out_shape=[SDS((S, 256), f32), SDS((S, Q_LORA), f32), SDS((S, KV_LORA), f32),
                   SDS((S, Q_LORA), bf16), SDS((S, KV_LORA), bf16),
                   SDS((H, S, 256), bf16), SDS((H, S, 256), bf16), SDS((H, S, 128), bf16)],
        compiler_params=_cp("parallel"),
    )(xb, w_in, wq, wkv, gq, gkv, cs)


def _group_select(col, a2, a4, a8, a16):
    return jnp.where(col < 64, a2, jnp.where(col < 128, a4, jnp.where(col < 192, a8, a16)))


def pool_fwd(name, u, wbd, scale):
    S = u.shape[0]
    tm = _tile(S, 512)
    hb = tm // HALO

    def body(u_ref, halo_ref, w_ref, s_ref, d_ref, y_ref):
        i = pl.program_id(0)
        cur = u_ref[...]
        halo = jnp.where(i > 0, halo_ref[...], 0.0)
        ext = jnp.concatenate([halo, cur], axis=0)
        s2 = ext + pltpu.roll(ext, 1, axis=0)
        s4 = s2 + pltpu.roll(s2, 2, axis=0)
        s8 = s4 + pltpu.roll(s4, 4, axis=0)
        s16 = s8 + pltpu.roll(s8, 8, axis=0)
        t1 = (i * tm + 1 + lax.broadcasted_iota(jnp.int32, (tm, 1), 0)).astype(f32)
        col = lax.broadcasted_iota(jnp.int32, (tm, 256), 1)
        m = _group_select(col, s2[HALO:] / jnp.minimum(t1, 2.0), s4[HALO:] / jnp.minimum(t1, 4.0),
                          s8[HALO:] / jnp.minimum(t1, 8.0), s16[HALO:] / jnp.minimum(t1, 16.0))
        d = (m - cur).astype(bf16)
        d_ref[...] = d
        y_ref[...] = (_dot(d, w_ref[...]) * s_ref[...]).astype(bf16)

    row = lambda i: (i, 0)
    return pl.pallas_call(
        body, name=name, grid=(S // tm,),
        in_specs=[pl.BlockSpec((tm, 256), row), pl.BlockSpec((HALO, 256), lambda i: (jnp.maximum(i * hb - 1, 0), 0)),
                  pl.BlockSpec((256, 256), lambda i: (0, 0)), pl.BlockSpec((1, 256), lambda i: (0, 0))],
        out_specs=[pl.BlockSpec((tm, 256), row)] * 2,
        out_shape=[SDS((S, 256), bf16), SDS((S, 256), bf16)],
        compiler_params=_cp("parallel"),
    )(u, u, wbd, scale)


def pool_bwd(name, dyp, d, wbd, scale):
    S = dyp.shape[0]
    tm = _tile(S, 512)
    hb = tm // HALO
    n_ext = tm + HALO

    def fwd_sum(e, steps):
        k = 1
        for _ in range(steps):
            e = e + pltpu.roll(e, n_ext - k, axis=0)
            k *= 2
        return e

    def body(dy_ref, halo_ref, d_ref, w_ref, s_ref, du_ref, dyw_ref, ds_ref):
        i = pl.program_id(0)
        sc = s_ref[...]
        w = w_ref[...]
        cur = dy_ref[...].astype(f32)
        halo = jnp.where(i < pl.num_programs(0) - 1, halo_ref[...].astype(f32), 0.0)
        dyw = jnp.concatenate([cur, halo], axis=0) * sc
        dyw_ref[...] = dyw[:tm].astype(bf16)
        dd = _dot_nt(dyw.astype(bf16), w)
        t1 = (i * tm + 1 + lax.broadcasted_iota(jnp.int32, (n_ext, 1), 0)).astype(f32)
        f2 = fwd_sum(dd / jnp.minimum(t1, 2.0), 1)
        f4 = fwd_sum(dd / jnp.minimum(t1, 4.0), 2)
        f8 = fwd_sum(dd / jnp.minimum(t1, 8.0), 3)
        f16 = fwd_sum(dd / jnp.minimum(t1, 16.0), 4)
        col = lax.broadcasted_iota(jnp.int32, (tm, 256), 1)
        du_ref[...] = (_group_select(col, f2[:tm], f4[:tm], f8[:tm], f16[:tm]) - dd[:tm]).astype(bf16)

        @pl.when(i == 0)
        def _():
            ds_ref[...] = jnp.zeros_like(ds_ref)

        ds_ref[...] += jnp.sum(cur * _dot(d_ref[...], w), axis=0, keepdims=True)

    row = lambda i: (i, 0)
    nhb = S // HALO
    return pl.pallas_call(
        body, name=name, grid=(S // tm,),
        in_specs=[pl.BlockSpec((tm, 256), row), pl.BlockSpec((HALO, 256), lambda i: (jnp.minimum((i + 1) * hb, nhb - 1), 0)),
                  pl.BlockSpec((tm, 256), row), pl.BlockSpec((256, 256), lambda i: (0, 0)),
                  pl.BlockSpec((1, 256), lambda i: (0, 0))],
        out_specs=[pl.BlockSpec((tm, 256), row), pl.BlockSpec((tm, 256), row), pl.BlockSpec((1, 256), lambda i: (0, 0))],
        out_shape=[SDS((S, 256), bf16), SDS((S, 256), bf16), SDS((1, 256), f32)],
        compiler_params=_cp("arbitrary"),
    )(dyp, dyp, d, wbd, scale)


def _diag_mask(tq):
    rc = lax.broadcasted_iota(jnp.int32, (tq, 1), 0) // 64
    cc = lax.broadcasted_iota(jnp.int32, (1, tq), 1) // 64
    return rc >= cc


MLA_SCALE_LOG2 = MLA_SCALE * math.log2(math.e)


def mla_attn_fwd(name, q, k, v):
    H, S, _ = q.shape
    tq = _tile(S, 1024)
    nq = S // tq
    pairs = [(i, j) for i in range(nq) for j in range(i + 1)]
    it = jnp.asarray([p_[0] for p_ in pairs], jnp.int32)
    jt = jnp.asarray([p_[1] for p_ in pairs], jnp.int32)

    def body(it_ref, jt_ref, q_ref, k_ref, v_ref, o_ref, lse_ref, m_sc, l_sc, acc_sc):
        t = pl.program_id(1)
        i, j = it_ref[t], jt_ref[t]

        @pl.when(j == 0)
        def _():
            m_sc[...] = jnp.full_like(m_sc, NEG_INF)
            l_sc[...] = jnp.zeros_like(l_sc)
            acc_sc[...] = jnp.zeros_like(acc_sc)

        def step(masked):
            s = _dot_nt(q_ref[0], k_ref[0])
            if masked:
                s = jnp.where(_diag_mask(tq), s, NEG_INF)
            m_prev = m_sc[...]
            m_new = jnp.maximum(m_prev, jnp.max(s, axis=-1, keepdims=True))
            p = jnp.exp2((s - jnp.tile(m_new, (1, tq // 128))) * MLA_SCALE_LOG2)
            a = jnp.exp2((m_prev - m_new) * MLA_SCALE_LOG2)
            l_sc[...] = a * l_sc[...] + jnp.sum(p, axis=-1, keepdims=True)
            acc_sc[...] = a * acc_sc[...] + _dot(p.astype(bf16), v_ref[0])
            m_sc[...] = m_new

        @pl.when(j < i)
        def _():
            step(False)

        @pl.when(j == i)
        def _():
            step(True)
            o_ref[...] = (acc_sc[...] / l_sc[...]).astype(bf16)
            lse_ref[0] = m_sc[...] * MLA_SCALE_LOG2 + jnp.log2(l_sc[...])

    return pl.pallas_call(
        body, name=name,
        grid_spec=pltpu.PrefetchScalarGridSpec(
            num_scalar_prefetch=2, grid=(H, len(pairs)),
            in_specs=[pl.BlockSpec((1, tq, 256), lambda h, t, it_, jt_: (h, it_[t], 0)),
                      pl.BlockSpec((1, tq, 256), lambda h, t, it_, jt_: (h, jt_[t], 0)),
                      pl.BlockSpec((1, tq, 128), lambda h, t, it_, jt_: (h, jt_[t], 0))],
            out_specs=[pl.BlockSpec((tq, 128), lambda h, t, it_, jt_: (it_[t], h)),
                       pl.BlockSpec((1, tq, 128), lambda h, t, it_, jt_: (h, it_[t], 0))],
            scratch_shapes=[pltpu.VMEM((tq, 128), f32), pltpu.VMEM((tq, 128), f32), pltpu.VMEM((tq, 128), f32)]),
        out_shape=[SDS((S, H * 128), bf16), SDS((H, S, 128), f32)],
        compiler_params=_cp("parallel", "arbitrary"),
    )(it, jt, q, k, v)


def mla_attn_bwd(name, q, k, v, o, do, lse):
    H, S, _ = q.shape
    tq = _tile(S, 1024)
    nq = S // tq
    pairs = [(i, j) for j in range(nq) for i in range(j, nq)]
    it = jnp.asarray([p_[0] for p_ in pairs], jnp.int32)
    jt = jnp.asarray([p_[1] for p_ in pairs], jnp.int32)
    n_pairs = len(pairs)

    def body(it_ref, jt_ref, q_ref, k_ref, v_ref, o_ref, do_ref, lse_ref, dq_ref, dk_ref, dv_ref, dk_sc, dv_sc):
        t = pl.program_id(1)
        i, j = it_ref[t], jt_ref[t]

        @pl.when(t == 0)
        def _():
            dq_ref[...] = jnp.zeros_like(dq_ref)

        @pl.when(i == j)
        def _():
            dk_sc[...] = jnp.zeros_like(dk_sc)
            dv_sc[...] = jnp.zeros_like(dv_sc)

        def step(masked):
            qv, kv_, dov = q_ref[0], k_ref[0], do_ref[...]
            s = _dot_nt(qv, kv_)
            if masked:
                s = jnp.where(_diag_mask(tq), s, NEG_INF)
            p = jnp.exp2(s * MLA_SCALE_LOG2 - jnp.tile(lse_ref[0], (1, tq // 128)))
            dv_sc[...] += _dot_tn(p.astype(bf16), dov)
            dp = _dot_nt(dov, v_ref[0])
            delta = jnp.sum(dov.astype(f32) * o_ref[...].astype(f32), axis=-1, keepdims=True)
            ds = (p * (dp - delta)).astype(bf16)
            dk_sc[...] += _dot_tn(ds, qv)
            rows = pl.ds(pl.multiple_of(i * tq, tq), tq)
            dq_ref[0, rows, :] += _dot(ds, kv_)

        @pl.when(i > j)
        def _():
            step(False)

        @pl.when(i == j)
        def _():
            step(True)

        @pl.when(i == nq - 1)
        def _():
            dk_ref[0] = dk_sc[...] * MLA_SCALE
            dv_ref[0] = dv_sc[...]

        @pl.when(t == n_pairs - 1)
        def _():
            dq_ref[...] = dq_ref[...] * MLA_SCALE

    qi = lambda h, t, it_, jt_: (h, it_[t], 0)
    kj = lambda h, t, it_, jt_: (h, jt_[t], 0)
    oi = lambda h, t, it_, jt_: (it_[t], h)
    return pl.pallas_call(
        body, name=name,
        grid_spec=pltpu.PrefetchScalarGridSpec(
            num_scalar_prefetch=2, grid=(H, n_pairs),
            in_specs=[pl.BlockSpec((1, tq, 256), qi), pl.BlockSpec((1, tq, 256), kj), pl.BlockSpec((1, tq, 128), kj),
                      pl.BlockSpec((tq, 128), oi), pl.BlockSpec((tq, 128), oi), pl.BlockSpec((1, tq, 128), qi)],
            out_specs=[pl.BlockSpec((1, S, 256), lambda h, t, it_, jt_: (h, 0, 0)), pl.BlockSpec((1, tq, 256), kj),
                       pl.BlockSpec((1, tq, 128), kj)],
            scratch_shapes=[pltpu.VMEM((tq, 256), f32), pltpu.VMEM((tq, 128), f32)]),
        out_shape=[SDS((H, S, 256), f32), SDS((H, S, 256), f32), SDS((H, S, 128), f32)],
        compiler_params=_cp("parallel", "arbitrary"),
    )(it, jt, q, k, v, o, do, lse)


def mix_post_bwd(name, dq, dk, dv, wq, wkv, l, cq, ckv, gq, gkv, cs):
    H, S, _ = dq.shape
    tm = _tile(S, 512)

    def rms_bwd(dyn, c, g):
        r = lax.rsqrt(jnp.mean(c * c, axis=-1, keepdims=True) + RMS_EPS)
        ch = c * r
        dyg = dyn * g
        dc = r * (dyg - ch * jnp.mean(dyg * ch, axis=-1, keepdims=True))
        return dc, jnp.sum(dyn * ch, axis=0, keepdims=True)

    def body(dq_ref, dk_ref, dv_ref, wq_ref, wkv_ref, cq_ref, ckv_ref, gq_ref, gkv_ref, cs_ref,
             dqe_ref, dkv_ref, dh_ref, dgq_ref, dgkv_ref):
        csv = cs_ref[...]
        lane = lax.broadcasted_iota(jnp.int32, (tm, 128), 1)
        dcqn = jnp.zeros((tm, Q_LORA), f32)
        dkr = jnp.zeros((tm, 128), f32)
        for hd in range(H):
            dqh = dq_ref[hd]
            dqe = jnp.concatenate([dqh[:, :128], _half_sum(dqh[:, 128:]) * csv], axis=1).astype(bf16)
            dqe_ref[:, 256 * hd:256 * hd + 256] = dqe
            dcqn = dcqn + _dot_nt(dqe, wq_ref[0, hd])
            dkh = dk_ref[hd]
            dkv_ref[:, 256 * hd:256 * hd + 128] = dkh[:, :128].astype(bf16)
            dkv_ref[:, 256 * hd + 128:256 * hd + 256] = dv_ref[hd].astype(bf16)
            dkr = dkr + dkh[:, 128:]
        dckvn = _dot_nt(dkv_ref[...], wkv_ref[0])
        dblk = _half_sum(jnp.where(lane < 64, dkr, 0.0)) * csv
        dcq, dgq = rms_bwd(dcqn, cq_ref[...], gq_ref[...])
        dckv, dgkv = rms_bwd(dckvn, ckv_ref[...], gkv_ref[...])
        dh_ref[:, :256] = dcq.astype(bf16)
        dh_ref[:, 256:384] = dckv.astype(bf16)
        dh_ref[:, 384:] = dblk.astype(bf16)

        @pl.when(pl.program_id(0) == 0)
        def _():
            dgq_ref[...] = jnp.zeros_like(dgq_ref)
            dgkv_ref[...] = jnp.zeros_like(dgkv_ref)

        dgq_ref[...] += dgq
        dgkv_ref[...] += dgkv

    row = lambda i: (i, 0)
    hrow = lambda i: (0, i, 0)
    return pl.pallas_call(
        body, name=name, grid=(S // tm,),
        in_specs=[pl.BlockSpec((H, tm, 256), hrow), pl.BlockSpec((H, tm, 256), hrow), pl.BlockSpec((H, tm, 128), hrow),
                  pl.BlockSpec((1, H, Q_LORA, 256), lambda i: (l, 0, 0, 0)),
                  pl.BlockSpec((1, KV_LORA, H * 256), lambda i: (l, 0, 0)),
                  pl.BlockSpec((tm, Q_LORA), row), pl.BlockSpec((tm, KV_LORA), row),
                  pl.BlockSpec((1, Q_LORA), lambda i: (0, 0)), pl.BlockSpec((1, KV_LORA), lambda i: (0, 0)),
                  pl.BlockSpec((tm, 128), row)],
        out_specs=[pl.BlockSpec((tm, H * 256), row), pl.BlockSpec((tm, H * 256), row), pl.BlockSpec((tm, 512), row),
                   pl.BlockSpec((1, Q_LORA), lambda i: (0, 0)), pl.BlockSpec((1, KV_LORA), lambda i: (0, 0))],
        out_shape=[SDS((S, H * 256), bf16), SDS((S, H * 256), bf16), SDS((S, 512), bf16),
                   SDS((1, Q_LORA), f32), SDS((1, KV_LORA), f32)],
        compiler_params=_cp("arbitrary"),
    )(dq, dk, dv, wq, wkv, cq, ckv, gq, gkv, cs)


def _cross_probs(qb, kv_ref, hd):
    cols = slice(hd * MEM_HEAD_DIM, (hd + 1) * MEM_HEAD_DIM)
    s = _dot_nt(qb[:, cols], kv_ref[:, cols]) * MEM_SCALE
    e = jnp.exp(s - jnp.max(s, axis=-1, keepdims=True))
    return e / jnp.sum(e, axis=-1, keepdims=True)


def cross_fwd(name, xb, xf, wq, wo, l, kv, g, b):
    S = xb.shape[0]
    tm = _tile(S, 512)
    M = kv.shape[0]

    def body(x_ref, xf_ref, wq_ref, wo_ref, k_ref, v_ref, g_ref, b_ref, q_ref, o_ref, z_ref, y_ref, yb_ref):
        qb = _dot(x_ref[...], wq_ref[0]).astype(bf16)
        q_ref[...] = qb
        for hd in range(MEM_HEADS):
            cols = slice(hd * MEM_HEAD_DIM, (hd + 1) * MEM_HEAD_DIM)
            p = _cross_probs(qb, k_ref, hd)
            o_ref[:, cols] = _dot(p.astype(bf16), v_ref[:, cols]).astype(bf16)
        z = ALPHA * xf_ref[...] + _dot(o_ref[...], wo_ref[0])
        mu = jnp.mean(z, axis=-1, keepdims=True)
        zc = z - mu
        var = jnp.mean(zc * zc, axis=-1, keepdims=True)
        y = zc * lax.rsqrt(var + LN_EPS) * g_ref[...] + b_ref[...]
        z_ref[...] = z
        y_ref[...] = y
        yb_ref[...] = y.astype(bf16)

    row = lambda i: (i, 0)
    wspec = pl.BlockSpec((1, D_MODEL, D_MODEL), lambda i: (l, 0, 0))
    vec = pl.BlockSpec((1, D_MODEL), lambda i: (0, 0))
    blk = pl.BlockSpec((tm, D_MODEL), row)
    return pl.pallas_call(
        body, name=name, grid=(S // tm,),
        in_specs=[blk, blk, wspec, wspec, pl.BlockSpec((M, D_MODEL), lambda i: (0, 0)),
                  pl.BlockSpec((M, D_MODEL), lambda i: (0, 1)), vec, vec],
        out_specs=[blk] * 5,
        out_shape=[SDS((S, D_MODEL), bf16), SDS((S, D_MODEL), bf16), SDS((S, D_MODEL), f32), SDS((S, D_MODEL), f32),
                   SDS((S, D_MODEL), bf16)],
        compiler_params=_cp("parallel"),
    )(xb, xf, wq, wo, kv, kv, g, b)


def cross_bwd(name, dzb, wo, l, qb, kv, deps=()):
    S = dzb.shape[0]
    tm = _tile(S, 512)
    M = kv.shape[0]

    def body(dz_ref, wo_ref, q_ref, k_ref, v_ref, dq_ref, dkv_ref):
        @pl.when(pl.program_id(0) == 0)
        def _():
            dkv_ref[...] = jnp.zeros_like(dkv_ref)

        do = _dot_nt(dz_ref[...], wo_ref[0]).astype(bf16)
        qv = q_ref[...]
        for hd in range(MEM_HEADS):
            cols = slice(hd * MEM_HEAD_DIM, (hd + 1) * MEM_HEAD_DIM)
            vcols = slice(D_MODEL + hd * MEM_HEAD_DIM, D_MODEL + (hd + 1) * MEM_HEAD_DIM)
            p = _cross_probs(qv, k_ref, hd)
            doh = do[:, cols]
            dkv_ref[:, vcols] += _dot_tn(p.astype(bf16), doh)
            dp = _dot_nt(doh, v_ref[:, cols])
            ds = (p * (dp - jnp.sum(dp * p, axis=-1, keepdims=True)) * MEM_SCALE).astype(bf16)
            dq_ref[:, cols] = _dot(ds, k_ref[:, cols]).astype(bf16)
            dkv_ref[:, cols] += _dot_tn(ds, qv[:, cols])

    row = lambda i: (i, 0)
    blk = pl.BlockSpec((tm, D_MODEL), row)
    return pl.pallas_call(
        _with_deps(body, 5, deps), name=name, grid=(S // tm,),
        in_specs=[blk, pl.BlockSpec((1, D_MODEL, D_MODEL), lambda i: (l, 0, 0)), blk,
                  pl.BlockSpec((M, D_MODEL), lambda i: (0, 0)), pl.BlockSpec((M, D_MODEL), lambda i: (0, 1))]
        + [_DEP_SPEC] * len(deps),
        out_specs=[blk, pl.BlockSpec((M, 2 * D_MODEL), lambda i: (0, 0))],
        out_shape=[SDS((S, D_MODEL), bf16), SDS((M, 2 * D_MODEL), f32)],
        compiler_params=_cp("arbitrary"),
    )(dzb, wo, qb, kv, kv, *deps)


def adamw(name, w, g, m, v):
    shape = w.shape
    cols = shape[-1]
    rows = math.prod(shape[:-1])
    tr = _row_tile(rows, cols, target=2 * 2**20)
    c1 = 1.0 - ADAM_B1 ** ADAM_STEP
    c2 = 1.0 - ADAM_B2 ** ADAM_STEP

    def body(w_ref, g_ref, m_ref, v_ref, d_ref, nm_ref, nv_ref):
        gv = g_ref[...]
        nm = ADAM_B1 * m_ref[...] + (1.0 - ADAM_B1) * gv
        nv = ADAM_B2 * v_ref[...] + (1.0 - ADAM_B2) * (gv * gv)
        d_ref[...] = -ADAM_LR * ((nm / c1) / (jnp.sqrt(nv / c2) + ADAM_EPS) + ADAM_WD * w_ref[...])
        nm_ref[...] = nm
        nv_ref[...] = nv

    blk = pl.BlockSpec((tr, cols), lambda i: (i, 0))
    flat = SDS((rows, cols), f32)
    outs = pl.pallas_call(
        body, name=name, grid=(rows // tr,), in_specs=[blk] * 4, out_specs=[blk] * 3, out_shape=[flat] * 3,
        compiler_params=_cp("parallel"),
    )(*[a.reshape(rows, cols) for a in (w, g, m, v)])
    return [o.reshape(shape) for o in outs]


def _me():
    return lax.axis_index("x"), lax.axis_index("y"), lax.axis_index("c")


def _other_chips(x, y):
    return [(1 - x, y), (x, 1 - y), (1 - x, 1 - y)]


def pair_share(name, sums, owners):
    n = len(sums)

    def body(*refs):
        out_refs = refs[n:2 * n]
        send_sems, recv_sems = refs[2 * n:]
        x, y, c = _me()
        sibling = (x, y, 1 - c)

        def copy(a, lyr):
            slot = out_refs[a].at[lyr]
            return pltpu.make_async_remote_copy(src_ref=slot, dst_ref=slot, send_sem=send_sems.at[2 * a + lyr],
                                                recv_sem=recv_sems.at[2 * a + lyr], device_id=sibling, device_id_type=MESH)

        def each(mine, act):
            for o in range(2):
                slots = [(a, lyr) for a in range(n) for lyr in range(DEPTH) if owners[a][lyr] == o]

                @pl.when((c == o) if mine else (c != o))
                def _(slots=slots):
                    for a, lyr in slots:
                        act(copy(a, lyr))

        each(True, lambda cp: cp.start())
        each(True, lambda cp: cp.wait_send())
        each(False, lambda cp: cp.wait_recv())

    any_spec = pl.BlockSpec(memory_space=pl.ANY)
    return pl.pallas_call(
        body, name=name, in_specs=[any_spec] * n, out_specs=[any_spec] * n,
        out_shape=[SDS(s.shape, f32) for s in sums], input_output_aliases={a: a for a in range(n)},
        scratch_shapes=[pltpu.SemaphoreType.DMA((2 * n,)), pltpu.SemaphoreType.DMA((2 * n,))],
    )(*sums)


def allsum_small(name, v, deps=()):
    R = v.shape[0]

    def body(v_ref, o_ref, all_ref, send_sems, recv_sems, local_sem):
        x, y, c = _me()
        me, sibling = (x, y, c), (x, y, 1 - c)
        chips = _other_chips(x, y)

        def rows(px, py, pc):
            return all_ref.at[4 * px + 2 * py + pc]

        def copy(k, block, to, src=None):
            return pltpu.make_async_remote_copy(
                src_ref=rows(*block) if src is None else src, dst_ref=rows(*block),
                send_sem=send_sems.at[k], recv_sem=recv_sems.at[k], device_id=to, device_id_type=MESH)

        mine = pltpu.make_async_copy(v_ref, rows(*me), local_sem)
        mine.start()
        first = [copy(0, me, sibling, src=v_ref)]
        first += [copy(1 + j, me, (*chip, c), src=v_ref) for j, chip in enumerate(chips)]
        for cp in first:
            cp.start()
        passed = [copy(4 + j, (*chip, c), sibling) for j, chip in enumerate(chips)]
        for j, chip in enumerate(chips):
            copy(1 + j, (*chip, c), me).wait_recv()
            passed[j].start()
        copy(0, sibling, me).wait_recv()
        for j, chip in enumerate(chips):
            copy(4 + j, (*chip, 1 - c), me).wait_recv()
        for cp in first + passed:
            cp.wait_send()
        mine.wait()
        acc = all_ref[0]
        for d in range(1, 8):
            acc = acc + all_ref[d]
        o_ref[...] = acc

    return pl.pallas_call(
        _with_deps(body, 1, deps), name=name,
        in_specs=[pl.BlockSpec(memory_space=pltpu.VMEM)] + [_DEP_SPEC] * len(deps),
        out_specs=pl.BlockSpec(memory_space=pltpu.VMEM),
        out_shape=SDS((R, 128), f32),
        scratch_shapes=[pltpu.VMEM((8, R, 128), f32), pltpu.SemaphoreType.DMA((7,)), pltpu.SemaphoreType.DMA((7,)),
                        pltpu.SemaphoreType.DMA],
        compiler_params=pltpu.CompilerParams(vmem_limit_bytes=V7X_VMEM_LIMIT),
    )(v, *deps)


def _swap_half(r):
    return jnp.concatenate([-r[..., 32:], r[..., :32]], axis=-1)


def _unswap_add(p, qg):
    return p + jnp.concatenate([qg[..., 32:], -qg[..., :32]], axis=-1)


def _block_diag(pw):
    L = pw.shape[0]
    out = jnp.zeros((L, 256, 256), pw.dtype)
    for gi in range(4):
        out = out.at[:, 64 * gi:64 * gi + 64, 64 * gi:64 * gi + 64].set(pw[:, gi])
    return out


def _to_col_shards(w):
    *lead, K, N = w.shape
    nl = len(lead)
    return w.reshape(*lead, K, N_CHIPS, N // N_CHIPS).transpose(*range(nl), nl + 1, nl, nl + 2)


def _from_col_shards(w):
    *lead, C, K, n = w.shape
    nl = len(lead)
    return w.transpose(*range(nl), nl + 1, nl, nl + 2).reshape(*lead, K, C * n)


def _step_serial_comm(x, mem, positions, ln_g, ln_b, ffn1_w13, ffn1_w2, w_in, pool_w, pool_scale, q_norm_g, w_uq, kv_norm_g, w_ukv, w_out, mem_wq, mem_wkv, mem_wo, ffn2_w13, ffn2_w2, loss_target, m_ln_g, m_ln_b, m_ffn1_w13, m_ffn1_w2, m_w_in, m_pool_w, m_pool_scale, m_q_norm_g, m_w_uq, m_kv_norm_g, m_w_ukv, m_w_out, m_mem_wq, m_mem_wkv, m_mem_wo, m_ffn2_w13, m_ffn2_w2, v_ln_g, v_ln_b, v_ffn1_w13, v_ffn1_w2, v_w_in, v_pool_w, v_pool_scale, v_q_norm_g, v_w_uq, v_kv_norm_g, v_w_ukv, v_w_out, v_mem_wq, v_mem_wkv, v_mem_wo, v_ffn2_w13, v_ffn2_w2):
    L = DEPTH
    S = x.shape[1]
    qx, qy, _ = _me()
    chip = 2 * qx + qy

    big = [ffn1_w13, ffn1_w2, w_in, w_uq, w_ukv, w_out, mem_wq, mem_wkv, mem_wo, ffn2_w13, ffn2_w2]
    (g_f1w13, g_f1w2, g_win, g_wuq, g_wukv, g_wout, g_mwq, g_mwkv, g_mwo, g_f2w13, g_f2w2) = gather_weights(
        [w.astype(bf16) for w in big])
    f1w2 = g_f1w2.reshape(L, D_FF, D_MODEL)
    f2w2 = g_f2w2.reshape(L, D_FF, D_MODEL)
    win = g_win.reshape(L, D_MODEL, D_IN)
    win_ext = jnp.concatenate([win, _swap_half(win[..., D_IN - QK_ROPE:])], axis=-1)
    wuq = _from_col_shards(g_wuq).reshape(L, Q_LORA, MLA_HEADS, QK_NOPE + QK_ROPE)
    wq_ext = jnp.concatenate([wuq, _swap_half(wuq[..., QK_NOPE:])], axis=-1).transpose(0, 2, 1, 3)
    wukv = _from_col_shards(g_wukv)
    wout = g_wout.reshape(L, D_MODEL, D_MODEL)
    wout_pool, wout_mla = wout[:, :POOL_WIDTH], wout[:, POOL_WIDTH:]
    mwq = g_mwq.reshape(L, D_MODEL, D_MODEL)
    mwo = g_mwo.reshape(L, D_MODEL, D_MODEL)
    wbd = _block_diag(pool_w.astype(bf16))

    ln_pad = jnp.zeros((2, L, 4, N_CHIPS, D_MODEL // N_CHIPS), f32)
    ln_pad = lax.dynamic_update_slice(ln_pad, jnp.stack([ln_g, ln_b])[:, :, :, None, :], (0, 0, 0, chip, 0))
    ln_full = allsum_small("allsum_ln", ln_pad.reshape(-1, 128)) * 0.5
    ln_full = ln_full.reshape(2, L, 4, D_MODEL)
    lng, lnb = ln_full[0], ln_full[1]

    half = QK_ROPE // 2
    inv_freq = ROPE_BASE ** (-jnp.arange(half, dtype=f32) / half)
    ang = positions[0].astype(f32)[:, None] * inv_freq
    cos, sin = jnp.cos(ang), jnp.sin(ang)
    cs = jnp.concatenate([cos, cos, sin, sin], axis=-1)

    memb = mem[0].astype(bf16)
    xf = x[0]
    xb = xf.astype(bf16)
    vec = lambda a: a.reshape(1, -1)

    saved = []
    for l in range(L):
        sv = {}
        sv["x0b"] = xb
        gate, up, act = ffn_up(f"ffn1_up_{l}", xb, g_f1w13, l)
        z1, x1f, x1b = proj_res_ln(f"ffn1_down_{l}", [act], [f1w2], [l], xf, vec(lng[l, 0]), vec(lnb[l, 0]), 0.5)
        sv.update(gate1=gate, up1=up, act1=act, z1=z1, x1b=x1b)
        u, cq, ckv, cqn, ckvn, q, k, v = mix_pre(f"mix_pre_{l}", x1b, win_ext, wq_ext, wukv, l,
                                                   vec(q_norm_g[l]), vec(kv_norm_g[l]), cs)
        dpool, ypool = pool_fwd(f"pool_fwd_{l}", u, wbd[l], vec(pool_scale[l]))
        o, lse = mla_attn_fwd(f"mla_fwd_{l}", q, k, v)
        z2, x2f, x2b = proj_res_ln(f"mix_out_{l}", [ypool, o], [wout_pool, wout_mla], [l, l], x1f,
                                   vec(lng[l, 1]), vec(lnb[l, 1]), 1.0)
        sv.update(cq=cq, ckv=ckv, cqn=cqn, ckvn=ckvn, q=q, k=k, v=v, dpool=dpool, ypool=ypool, o=o, lse=lse, z2=z2, x2b=x2b)
        kvm = mm_nn_shard(f"mem_kv_{l}", memb, g_mwkv, l)
        cq_, co_, z3, x3f, x3b = cross_fwd(f"cross_fwd_{l}", x2b, x2f, mwq, mwo, l, kvm, vec(lng[l, 2]), vec(lnb[l, 2]))
        sv.update(kvm=kvm, crq=cq_, cro=co_, z3=z3, x3b=x3b)
        gate, up, act = ffn_up(f"ffn2_up_{l}", x3b, g_f2w13, l)
        z4, xf, xb = proj_res_ln(f"ffn2_down_{l}", [act], [f2w2], [l], x3f, vec(lng[l, 3]), vec(lnb[l, 3]), 0.5)
        sv.update(gate2=gate, up2=up, act2=act, z4=z4)
        saved.append(sv)

    dy, loss_blk = loss_grad("loss_grad", xf, loss_target[0])
    loss = lax.psum(loss_blk[0, 0], ("x", "y", "c"))

    G = dict(f1w13=None, f1w2=None, mwq=None, mwkv=None, mwo=None, f2w13=None, f2w2=None)
    small = {k_: [None] * L for k_ in ("win", "wuq", "wukv", "wout", "pool_w", "pool_scale", "gq", "gkv", "lng", "lnb")}
    for l in reversed(range(L)):
        sv = saved[l]
        dlg, dlb = [None] * 4, [None] * 4
        dzb, dres, dlg[3], dlb[3] = ln_bwd(f"ln4_bwd_{l}", dy, sv["z4"], vec(lng[l, 3]), 0.5)
        dh = ffn_bwd_da(f"ffn2_bwd_da_{l}", dzb, f2w2, l, sv["gate2"], sv["up2"])
        G["f2w2"] = mm_tn(f"ffn2_dw2_{l}", sv["act2"], dzb, "nat", l, G["f2w2"])
        G["f2w13"] = mm_tn(f"ffn2_dw13_{l}", sv["x3b"], dh, "shard", l, G["f2w13"])
        dy = ffn_dx(f"ffn2_dx_{l}", dh, g_f2w13, l, dres)
        dzb, dres, dlg[2], dlb[2] = ln_bwd(f"ln3_bwd_{l}", dy, sv["z3"], vec(lng[l, 2]), 1.0)
        dqc, dkvm = cross_bwd(f"cross_bwd_{l}", dzb, mwo, l, sv["crq"], sv["kvm"])
        G["mwo"] = mm_tn(f"cross_dwo_{l}", sv["cro"], dzb, "nat", l, G["mwo"])
        G["mwq"] = mm_tn(f"cross_dwq_{l}", sv["x2b"], dqc, "nat", l, G["mwq"])
        G["mwkv"] = mm_tn(f"cross_dwkv_{l}", memb, dkvm, "shard", l, G["mwkv"])
        dy = mm_nt_res(f"cross_dx_{l}", [dqc], [mwq], [l], dres, f32)
        dzb, dres, dlg[1], dlb[1] = ln_bwd(f"ln2_bwd_{l}", dy, sv["z2"], vec(lng[l, 1]), 1.0)
        dyp = mm_nt_res(f"mix_dpool_{l}", [dzb], [wout_pool], [l], None, bf16)
        do = mm_nt_res(f"mix_do_{l}", [dzb], [wout_mla], [l], None, bf16)
        dwo_p = mm_tn(f"mix_dwout_pool_{l}", sv["ypool"], dzb)
        dwo_m = mm_tn(f"mix_dwout_mla_{l}", sv["o"], dzb)
        small["wout"][l] = jnp.concatenate([dwo_p, dwo_m], axis=0)
        dq, dk, dv = mla_attn_bwd(f"mla_bwd_{l}", sv["q"], sv["k"], sv["v"], sv["o"], do, sv["lse"])
        dqe, dkv, dh_rest, dgq, dgkv = mix_post_bwd(f"mix_post_bwd_{l}", dq, dk, dv, wq_ext, wukv, l, sv["cq"], sv["ckv"],
                                                     vec(q_norm_g[l]), vec(kv_norm_g[l]), cs)
        du, dyw, dscale = pool_bwd(f"pool_bwd_{l}", dyp, sv["dpool"], wbd[l], vec(pool_scale[l]))
        dwq_e = mm_tn(f"mix_dwuq_{l}", sv["cqn"], dqe).reshape(Q_LORA, MLA_HEADS, 256)
        small["wuq"][l] = jnp.concatenate(
            [dwq_e[..., :QK_NOPE], _unswap_add(dwq_e[..., QK_NOPE:QK_NOPE + QK_ROPE], dwq_e[..., QK_NOPE + QK_ROPE:])],
            axis=-1).reshape(Q_LORA, MLA_HEADS * (QK_NOPE + QK_ROPE))
        small["wukv"][l] = mm_tn(f"mix_dwukv_{l}", sv["ckvn"], dkv)
        dwbd = mm_tn(f"pool_dw_{l}", sv["dpool"], dyw)
        small["pool_w"][l] = jnp.stack([dwbd[64 * gi:64 * gi + 64, 64 * gi:64 * gi + 64] for gi in range(4)])
        small["pool_scale"][l], small["gq"][l], small["gkv"][l] = dscale[0], dgq[0], dgkv[0]
        dh_ext = jnp.concatenate([du, dh_rest], axis=1)
        dwin_e = mm_tn(f"mix_dwin_{l}", sv["x1b"], dh_ext)
        small["win"][l] = jnp.concatenate(
            [dwin_e[:, :D_IN - QK_ROPE], _unswap_add(dwin_e[:, D_IN - QK_ROPE:D_IN], dwin_e[:, D_IN:])], axis=-1)
        dy = mm_nt_res(f"mix_dx_{l}", [dh_ext], [win_ext], [l], dres, f32)
        dzb, dres, dlg[0], dlb[0] = ln_bwd(f"ln1_bwd_{l}", dy, sv["z1"], vec(lng[l, 0]), 0.5)
        dh = ffn_bwd_da(f"ffn1_bwd_da_{l}", dzb, f1w2, l, sv["gate1"], sv["up1"])
        G["f1w2"] = mm_tn(f"ffn1_dw2_{l}", sv["act1"], dzb, "nat", l, G["f1w2"])
        G["f1w13"] = mm_tn(f"ffn1_dw13_{l}", sv["x0b"], dh, "shard", l, G["f1w13"])
        dy = ffn_dx(f"ffn1_dx_{l}", dh, g_f1w13, l, dres)
        small["lng"][l] = jnp.concatenate(dlg, axis=0)
        small["lnb"][l] = jnp.concatenate(dlb, axis=0)
    grad_x = dy[None]

    row_shards = lambda a, K: a.reshape(L, N_CHIPS, K // N_CHIPS, a.shape[-1])
    g_list = [G["f1w13"], row_shards(G["f1w2"], D_FF),
              jnp.stack(small["win"]).reshape(L, N_CHIPS, D_MODEL // N_CHIPS, D_IN),
              _to_col_shards(jnp.stack(small["wuq"])), _to_col_shards(jnp.stack(small["wukv"])),
              jnp.stack(small["wout"]).reshape(L, N_CHIPS, D_MODEL // N_CHIPS, D_MODEL),
              row_shards(G["mwq"], D_MODEL), G["mwkv"], row_shards(G["mwo"], D_MODEL),
              G["f2w13"], row_shards(G["f2w2"], D_FF)]
    big_grads = reduce_grads(g_list)

    rep = [jnp.stack(small["pool_w"]).reshape(-1), jnp.stack(small["pool_scale"]).reshape(-1),
           jnp.stack(small["gq"]).reshape(-1), jnp.stack(small["gkv"]).reshape(-1),
           jnp.stack(small["lng"]).reshape(-1), jnp.stack(small["lnb"]).reshape(-1)]
    sizes = [r.shape[0] for r in rep]
    packed = jnp.concatenate(rep)
    pad = (-packed.shape[0]) % 1024
    tot = allsum_small("allsum_small_grads", jnp.pad(packed, (0, pad)).reshape(-1, 128)).reshape(-1)
    offs = [0]
    for s_ in sizes:
        offs.append(offs[-1] + s_)
    parts = [tot[offs[i]:offs[i + 1]] for i in range(len(sizes))]
    g_pool_w = parts[0].reshape(pool_w.shape)
    g_pool_scale = parts[1].reshape(pool_scale.shape)
    g_gq = parts[2].reshape(q_norm_g.shape)
    g_gkv = parts[3].reshape(kv_norm_g.shape)
    shard_cols = lambda a: lax.dynamic_slice_in_dim(a.reshape(L, 4, D_MODEL), chip * (D_MODEL // N_CHIPS),
                                                    D_MODEL // N_CHIPS, axis=2)
    g_lng, g_lnb = shard_cols(parts[4]), shard_cols(parts[5])

    out_names = ("lng", "lnb", "f1w13", "f1w2", "win", "pool_w", "pool_scale", "gq", "wuq", "gkv", "wukv", "wout", "mwq",
                 "mwkv", "mwo", "f2w13", "f2w2")
    big.update(lng=g_lng, lnb=g_lnb, pool_w=g_pool_w, pool_scale=g_pool_scale, gq=g_gq, gkv=g_gkv)
    late = ("f1w13", "f1w2")
    ws = [ln_g, ln_b, ffn1_w13, ffn1_w2, w_in, pool_w, pool_scale, q_norm_g, w_uq, kv_norm_g, w_ukv, w_out, mem_wq,
          mem_wkv, mem_wo, ffn2_w13, ffn2_w2]
    ms = [m_ln_g, m_ln_b, m_ffn1_w13, m_ffn1_w2, m_w_in, m_pool_w, m_pool_scale, m_q_norm_g, m_w_uq, m_kv_norm_g, m_w_ukv,
          m_w_out, m_mem_wq, m_mem_wkv, m_mem_wo, m_ffn2_w13, m_ffn2_w2]
    vs = [v_ln_g, v_ln_b, v_ffn1_w13, v_ffn1_w2, v_w_in, v_pool_w, v_pool_scale, v_q_norm_g, v_w_uq, v_kv_norm_g, v_w_ukv,
          v_w_out, v_mem_wq, v_mem_wkv, v_mem_wo, v_ffn2_w13, v_ffn2_w2]
    res = {}

    def update(a):
        res[a] = adamw(f"adamw_{a}", ws[a], big[out_names[a]].reshape(ws[a].shape), ms[a], vs[a])

    for a, n in enumerate(out_names):
        if n not in late:
            update(a)
    sums_b = red_end("b0", st_cb, 0, [sums1[n] for n in late], tuple(r[0] for r in res.values()))
    big.update(zip(late, share("b", late, sums_b)))
    for a, n in enumerate(out_names):
        if n in late:
            update(a)
    order = range(len(out_names))
    grads = [big[n].reshape(w_.shape) for n, w_ in zip(out_names, ws)]
    return (loss, grad_x, *grads, *[res[a][0] for a in order], *[res[a][1] for a in order], *[res[a][2] for a in order])


_HBM_SPEC = pl.BlockSpec(memory_space=pltpu.HBM)
_SEM_SPEC = pl.BlockSpec(memory_space=pltpu.SEMAPHORE)
_ANY_SPEC = pl.BlockSpec(memory_space=pl.ANY)
_DATAFLOW = pltpu.SideEffectType.DATAFLOW_SIDE_EFFECTING


def _split_call(name, body_fn, bufs, sems_in, sems_out_sizes, after):
    nb, ni, no = len(bufs), len(sems_in), len(sems_out_sizes)
    afters = () if after is None else tuple(after) if isinstance(after, (tuple, list)) else (after,)

    def body(*refs):
        k = nb + ni + len(afters)
        body_fn(refs[:nb], refs[nb:nb + ni], refs[k:k + no])
        refs[-1][...] = jnp.zeros((8, 128), f32)

    outs = pl.pallas_call(
        body, name=name,
        in_specs=[_HBM_SPEC] * nb + [_SEM_SPEC] * ni + [_ANY_SPEC] * len(afters),
        out_specs=[_SEM_SPEC] * no + [_HBM_SPEC] * nb + [pl.BlockSpec(memory_space=pltpu.VMEM)],
        out_shape=[pltpu.SemaphoreType.DMA((s,)) for s in sems_out_sizes]
        + [pltpu.HBM(b.shape, b.dtype) for b in bufs] + [SDS((8, 128), f32)],
        input_output_aliases={i: no + i for i in range(nb)},
        compiler_params=pltpu.CompilerParams(has_side_effects=_DATAFLOW),
    )(*[pltpu.with_memory_space_constraint(b, pltpu.HBM) for b in bufs], *sems_in, *afters)
    return list(outs[no:no + nb]), list(outs[:no]), outs[-1]


def _rcopy(src, dst, ssem, rsem, to):
    return pltpu.make_async_remote_copy(src_ref=src, dst_ref=dst, send_sem=ssem, recv_sem=rsem, device_id=to,
                                        device_id_type=MESH)


def gather_start(name, groups, after):
    flat = [b for bufs, _ in groups for b in bufs]
    sizes = [3 * len(bufs) for bufs, _ in groups for _ in range(2)]

    def body_fn(b_in, s_in, s_out):
        x, y, c = _me()
        q = 2 * x + y
        chips = _other_chips(x, y)
        pos = 0
        for gi, (bufs, owner) in enumerate(groups):
            refs = b_in[pos:pos + len(bufs)]
            pos += len(bufs)

            @pl.when(c == owner)
            def _(refs=refs, send=s_out[2 * gi], recv=s_out[2 * gi + 1]):
                for a, r in enumerate(refs):
                    for k, (cx, cy) in enumerate(chips):
                        _rcopy(r.at[q], r.at[q], send.at[3 * a + k], recv.at[3 * a + k], (cx, cy, c)).start()

    outs, sems, token = _split_call(name, body_fn, flat, [], sizes, after)
    res, pos = [], 0
    for gi, (bufs, owner) in enumerate(groups):
        res.append((outs[pos:pos + len(bufs)], sems[2 * gi], sems[2 * gi + 1], owner))
        pos += len(bufs)
    return res, token


def gather_forward(name, grp, after):
    bufs, send, recv, owner = grp
    n3 = 3 * len(bufs)

    def body_fn(b_in, s_in, s_out):
        x, y, c = _me()
        q = 2 * x + y
        sibling = (x, y, 1 - c)
        chips = _other_chips(x, y)

        @pl.when(c == owner)
        def _():
            for a, r in enumerate(b_in):
                for k, (cx, cy) in enumerate(chips):
                    i = 3 * a + k
                    land = r.at[2 * cx + cy]
                    _rcopy(r.at[q], r.at[q], s_in[0].at[i], s_in[1].at[i], (cx, cy, c)).wait_send()
                    _rcopy(land, land, s_in[0].at[i], s_in[1].at[i], (cx, cy, c)).wait_recv()
                    _rcopy(land, land, s_out[0].at[i], s_out[1].at[i], sibling).start()

    outs, sems, token = _split_call(name, body_fn, bufs, [send, recv], [n3, n3], after)
    return (outs, sems[0], sems[1], owner), token


def gather_finish(name, grp, after):
    bufs, fsend, frecv, owner = grp

    def body_fn(b_in, s_in, s_out):
        x, y, c = _me()
        sibling = (x, y, 1 - c)
        chips = _other_chips(x, y)

        def each(wait):
            for a, r in enumerate(b_in):
                for k, (cx, cy) in enumerate(chips):
                    land = r.at[2 * cx + cy]
                    wait(_rcopy(land, land, s_in[0].at[3 * a + k], s_in[1].at[3 * a + k], sibling))

        @pl.when(c == owner)
        def _():
            each(lambda cp: cp.wait_send())

        @pl.when(c != owner)
        def _():
            each(lambda cp: cp.wait_recv())

    outs, _, _ = _split_call(name, body_fn, bufs, [fsend, frecv], [], after)
    return outs


def _by_owner(owners):
    return [[a for a, o_ in enumerate(owners) if o_ == o] for o in range(2)]


def pair_send_start(name, gs, owners, after):
    n = len(gs)
    lands = [lax.empty(g.shape, g.dtype) for g in gs]

    def body_fn(b_in, s_in, s_out):
        x, y, c = _me()
        for o, idx in enumerate(_by_owner(owners)):
            @pl.when(c == 1 - o)
            def _(o=o, idx=idx):
                for a in idx:
                    _rcopy(b_in[a], b_in[n + a], s_out[0].at[a], s_out[1].at[a], (x, y, o)).start()

    outs, sems, token = _split_call(name, body_fn, list(gs) + lands, [], [n, n], after)
    return (outs[:n], outs[n:], sems[0], sems[1], owners), token


def pair_send_wait(name, st, after):
    gs, lands, send, recv, owners = st
    n = len(gs)

    def body_fn(b_in, s_in, s_out):
        x, y, c = _me()
        for o, idx in enumerate(_by_owner(owners)):
            @pl.when(c == 1 - o)
            def _(o=o, idx=idx):
                for a in idx:
                    _rcopy(b_in[a], b_in[n + a], s_in[0].at[a], s_in[1].at[a], (x, y, o)).wait_send()

            @pl.when(c == o)
            def _(o=o, idx=idx):
                for a in idx:
                    _rcopy(b_in[a], b_in[n + a], s_in[0].at[a], s_in[1].at[a], (x, y, 1 - o)).wait_recv()

    outs, _, _ = _split_call(name, body_fn, list(gs) + list(lands), [send, recv], [], after)
    return outs[:n], outs[n:]


def chip_exchange_start(name, psums, owners, after):
    n = len(psums)
    lands = [lax.empty((3,) + p.shape[1:], p.dtype) for p in psums]

    def body_fn(b_in, s_in, s_out):
        x, y, c = _me()
        chips = _other_chips(x, y)
        for o, idx in enumerate(_by_owner(owners)):
            @pl.when(c == o)
            def _(idx=idx):
                for a in idx:
                    for k, (cx, cy) in enumerate(chips):
                        _rcopy(b_in[a].at[2 * cx + cy], b_in[n + a].at[k], s_out[0].at[3 * a + k],
                               s_out[1].at[3 * a + k], (cx, cy, c)).start()

    outs, sems, token = _split_call(name, body_fn, list(psums) + lands, [], [3 * n, 3 * n], after)
    return (outs[:n], outs[n:], sems[0], sems[1], owners), token


def chip_exchange_wait(name, st, after):
    psums, lands, send, recv, owners = st
    n = len(psums)

    def body_fn(b_in, s_in, s_out):
        x, y, c = _me()
        chips = _other_chips(x, y)
        for o, idx in enumerate(_by_owner(owners)):
            @pl.when(c == o)
            def _(idx=idx):
                for a in idx:
                    for k, (cx, cy) in enumerate(chips):
                        cp = _rcopy(b_in[a].at[2 * cx + cy], b_in[n + a].at[k], s_in[0].at[3 * a + k],
                                    s_in[1].at[3 * a + k], (cx, cy, c))
                        cp.wait_send()
                        cp.wait_recv()

    outs, _, _ = _split_call(name, body_fn, list(psums) + list(lands), [send, recv], [], after)
    return outs[:n], outs[n:]


def pair_sum(name, g, recv, flag):
    shape = g.shape
    cols = shape[-1]
    rows = math.prod(shape[:-1])
    tr = _row_tile(rows, cols, target=4 * 2**20)

    def body(f_ref, g_ref, r_ref, o_ref):
        o_ref[...] = (g_ref[...] + r_ref[...]).astype(bf16)

    blk = pl.BlockSpec((tr, cols), lambda i, f_ref: (i * f_ref[0], 0))
    out = pl.pallas_call(
        body, name=name,
        grid_spec=pltpu.PrefetchScalarGridSpec(num_scalar_prefetch=1, grid=(rows // tr,), in_specs=[blk, blk],
                                               out_specs=blk),
        out_shape=SDS((rows, cols), bf16), compiler_params=_cp("arbitrary"),
    )(flag, g.reshape(rows, cols), recv.reshape(rows, cols))
    return out.reshape(shape)


def chip_sum(name, psum, recv, qf_arr, layer, prev):
    shard = psum.shape[1:]
    cols = shard[-1]
    rows = math.prod(shard[:-1])
    tr = _row_tile(rows, cols, target=4 * 2**20)

    def body(qf_ref, p_ref, r_ref, *rest):
        rest[-1][0] = ((p_ref[0].astype(f32) + r_ref[0].astype(f32)) + r_ref[1].astype(f32)) + r_ref[2].astype(f32)

    in_specs = [pl.BlockSpec((1, tr, cols), lambda i, qf: (qf[0], i * qf[1], 0)),
                pl.BlockSpec((3, tr, cols), lambda i, qf: (0, i * qf[1], 0))]
    args = [qf_arr, psum.reshape(N_CHIPS, rows, cols), recv.reshape(3, rows, cols)]
    aliases = {}
    if prev is not None:
        in_specs.append(pl.BlockSpec(memory_space=pl.ANY))
        args.append(prev.reshape(DEPTH, rows, cols))
        aliases = {3: 0}
    out = pl.pallas_call(
        body, name=name,
        grid_spec=pltpu.PrefetchScalarGridSpec(
            num_scalar_prefetch=1, grid=(rows // tr,), in_specs=in_specs,
            out_specs=pl.BlockSpec((1, tr, cols), lambda i, qf: (layer, i * qf[1], 0))),
        out_shape=SDS((DEPTH, rows, cols), f32), input_output_aliases=aliases, compiler_params=_cp("arbitrary"),
    )(*args)
    return out.reshape((DEPTH,) + shard)


W_NAMES = ("f1w13", "f1w2", "win", "wuq", "wukv", "wout", "mwq", "mwkv", "mwo", "f2w13", "f2w2")
MIX_NAMES = ("win", "wuq", "wukv")
MID_NAMES = ("wout", "mwq", "mwkv", "mwo")
FFN2_NAMES = ("f2w13", "f2w2")
REDUCER = (dict(f1w13=0, f1w2=1, f2w13=0, win=0, wuq=0, wukv=0, f2w2=1, mwkv=1, wout=1, mwq=1, mwo=1),
           dict(f1w13=0, f2w2=0, mwkv=0, wout=0, f2w13=1, f1w2=1, mwq=1, mwo=1, win=1, wuq=1, wukv=1))


def kernel(x, mem, positions, ln_g, ln_b, ffn1_w13, ffn1_w2, w_in, pool_w, pool_scale, q_norm_g, w_uq, kv_norm_g, w_ukv, w_out, mem_wq, mem_wkv, mem_wo, ffn2_w13, ffn2_w2, loss_target, m_ln_g, m_ln_b, m_ffn1_w13, m_ffn1_w2, m_w_in, m_pool_w, m_pool_scale, m_q_norm_g, m_w_uq, m_kv_norm_g, m_w_ukv, m_w_out, m_mem_wq, m_mem_wkv, m_mem_wo, m_ffn2_w13, m_ffn2_w2, v_ln_g, v_ln_b, v_ffn1_w13, v_ffn1_w2, v_w_in, v_pool_w, v_pool_scale, v_q_norm_g, v_w_uq, v_kv_norm_g, v_w_ukv, v_w_out, v_mem_wq, v_mem_wkv, v_mem_wo, v_ffn2_w13, v_ffn2_w2):
    L = DEPTH
    qx, qy, _ = _me()
    chip = 2 * qx + qy
    vec = lambda a: a.reshape(1, -1)

    shards = dict(zip(W_NAMES, (ffn1_w13, ffn1_w2, w_in, w_uq, w_ukv, w_out, mem_wq, mem_wkv, mem_wo, ffn2_w13, ffn2_w2)))

    def place(sh, slot):
        return lax.dynamic_update_slice(jnp.zeros((N_CHIPS,) + sh.shape, bf16), sh.astype(bf16)[None],
                                        (slot,) + (0,) * sh.ndim)

    first = ("f1w13", "f1w2")
    bufs = [dict(), dict()]
    for n in first:
        bufs[0][n] = place(shards[n][0], chip)
    gw = [dict(), dict()]
    (g0,), tok = gather_start("gather_a_start", [([bufs[0][n] for n in first], 0)], None)
    chip_then = chip + tok[0, 0].astype(jnp.int32)
    for l in range(L):
        for n in W_NAMES:
            if n not in bufs[l]:
                bufs[l][n] = place(shards[n][l], chip_then)
    others = tuple(bufs[l][n] for l in range(L) for n in W_NAMES if (l, n) not in ((0, first[0]), (0, first[1])))
    g0, tok = gather_forward("gather_a_forward", g0, others)

    ln_pad = jnp.zeros((2, L, 4, N_CHIPS, D_MODEL // N_CHIPS), f32)
    ln_pad = lax.dynamic_update_slice(ln_pad, jnp.stack([ln_g, ln_b])[:, :, :, None, :], (0, 0, 0, chip, 0))
    ln_sum = allsum_small("allsum_ln", ln_pad.reshape(-1, 128), (tok,))
    ln_full = (ln_sum * 0.5).reshape(2, L, 4, D_MODEL)
    lng, lnb = ln_full[0], ln_full[1]

    gw[0]["f1w13"], gw[0]["f1w2"] = gather_finish("gather_a_finish", g0, ln_sum)
    (g_mix, g_mid, g_ffn2, g_l1), tok_b = gather_start(
        "gather_b_start",
        [([bufs[0][n] for n in MIX_NAMES], 0), ([bufs[0][n] for n in MID_NAMES], 0), ([bufs[0][n] for n in FFN2_NAMES], 0),
         ([bufs[1][n] for n in W_NAMES], 1)], ln_sum)

    half = QK_ROPE // 2
    inv_freq = ROPE_BASE ** (-jnp.arange(half, dtype=f32) / half)
    ang = positions[0].astype(f32)[:, None] * inv_freq
    cos, sin = jnp.cos(ang), jnp.sin(ang)
    cs = jnp.concatenate([cos, cos, sin, sin], axis=-1)

    memb = mem[0].astype(bf16)
    xf = x[0]
    xb = xf.astype(bf16)
    dep = (tok_b,)

    saved, W = [], [None, None]
    for l in range(L):
        sv = {}
        if l == 1:
            gl1 = gather_finish("gather_l1_finish", g_l1, xb)
            gw[1] = dict(zip(W_NAMES, gl1))
        sv["x0b"] = xb
        f1w13 = gw[l]["f1w13"][None]
        gate, up, act = ffn_up(f"ffn1_up_{l}", xb, f1w13, 0, dep)
        dep = ()
        if l == 0:
            g_mix, _ = gather_forward("gather_mix_forward", g_mix, act)
        z1, x1f, x1b = proj_res_ln(f"ffn1_down_{l}", [act], [gw[l]["f1w2"].reshape(1, D_FF, D_MODEL)], [0], xf,
                                   vec(lng[l, 0]), vec(lnb[l, 0]), 0.5)
        sv.update(gate1=gate, up1=up, act1=act, z1=z1, x1b=x1b)
        if l == 0:
            gw[0].update(zip(MIX_NAMES, gather_finish("gather_mix_finish", g_mix, x1b)))
            g_mid, _ = gather_forward("gather_mid_forward", g_mid, x1b)
        win = gw[l]["win"].reshape(D_MODEL, D_IN)
        win_ext = jnp.concatenate([win, _swap_half(win[:, D_IN - QK_ROPE:])], axis=-1)[None]
        wuq = _from_col_shards(gw[l]["wuq"]).reshape(Q_LORA, MLA_HEADS, QK_NOPE + QK_ROPE)
        wq_ext = jnp.concatenate([wuq, _swap_half(wuq[..., QK_NOPE:])], axis=-1).transpose(1, 0, 2)[None]
        wukv = _from_col_shards(gw[l]["wukv"])[None]
        wbd = _block_diag(pool_w[l][None].astype(bf16))[0]
        u, cq, ckv, cqn, ckvn, q, k, v = mix_pre(f"mix_pre_{l}", x1b, win_ext, wq_ext, wukv, 0,
                                                   vec(q_norm_g[l]), vec(kv_norm_g[l]), cs)
        dpool, ypool = pool_fwd(f"pool_fwd_{l}", u, wbd, vec(pool_scale[l]))
        o, lse = mla_attn_fwd(f"mla_fwd_{l}", q, k, v)
        if l == 0:
            gw[0].update(zip(MID_NAMES, gather_finish("gather_mid_finish", g_mid, o)))
            g_ffn2, tok_f = gather_forward("gather_ffn2_forward", g_ffn2, o)
            g_l1, tok_l = gather_forward("gather_l1_forward", g_l1, o)
            dep = (tok_f, tok_l)
        wout = gw[l]["wout"].reshape(D_MODEL, D_MODEL)
        wout_pool, wout_mla = wout[None, :POOL_WIDTH], wout[None, POOL_WIDTH:]
        mwq = gw[l]["mwq"].reshape(1, D_MODEL, D_MODEL)
        mwo = gw[l]["mwo"].reshape(1, D_MODEL, D_MODEL)
        mwkv = gw[l]["mwkv"][None]
        z2, x2f, x2b = proj_res_ln(f"mix_out_{l}", [ypool, o], [wout_pool, wout_mla], [0, 0], x1f,
                                   vec(lng[l, 1]), vec(lnb[l, 1]), 1.0, dep)
        dep = ()
        sv.update(cq=cq, ckv=ckv, cqn=cqn, ckvn=ckvn, q=q, k=k, v=v, dpool=dpool, ypool=ypool, o=o, lse=lse, z2=z2, x2b=x2b)
        kvm = mm_nn_shard(f"mem_kv_{l}", memb, mwkv, 0)
        cq_, co_, z3, x3f, x3b = cross_fwd(f"cross_fwd_{l}", x2b, x2f, mwq, mwo, 0, kvm, vec(lng[l, 2]), vec(lnb[l, 2]))
        sv.update(kvm=kvm, crq=cq_, cro=co_, z3=z3, x3b=x3b)
        if l == 0:
            gw[0].update(zip(FFN2_NAMES, gather_finish("gather_ffn2_finish", g_ffn2, x3b)))
        f2w13 = gw[l]["f2w13"][None]
        f2w2 = gw[l]["f2w2"].reshape(1, D_FF, D_MODEL)
        gate, up, act = ffn_up(f"ffn2_up_{l}", x3b, f2w13, 0)
        z4, xf, xb = proj_res_ln(f"ffn2_down_{l}", [act], [f2w2], [0], x3f, vec(lng[l, 3]), vec(lnb[l, 3]), 0.5)
        sv.update(gate2=gate, up2=up, act2=act, z4=z4)
        W[l] = dict(f1w13=f1w13, f1w2=gw[l]["f1w2"].reshape(1, D_FF, D_MODEL), win_ext=win_ext, wq_ext=wq_ext, wukv=wukv,
                    wbd=wbd, wout_pool=wout_pool, wout_mla=wout_mla, mwq=mwq, mwo=mwo, f2w13=f2w13, f2w2=f2w2)
        saved.append(sv)

    dln = {}
    dzb, dres, *dln[L - 1, 3], loss_blk = loss_grad("loss_grad", xf, loss_target[0],
                                                   (saved[L - 1]["z4"], vec(lng[L - 1, 3]), 0.5))
    loss = lax.psum(loss_blk[0, 0], ("x", "y", "c"))

    row_shards = lambda a: a.reshape(N_CHIPS, a.shape[0] // N_CHIPS, a.shape[1])
    small = {k_: [None] * L for k_ in ("pool_w", "pool_scale", "gq", "gkv", "lng", "lnb")}
    rest_names = [n for n in W_NAMES if n not in ("f1w13", "f1w2")]
    core = lax.axis_index("c")
    flags = [jnp.reshape(core == o, (1,)).astype(jnp.int32) for o in range(2)]
    qfs = [jnp.stack([chip, (core == o).astype(jnp.int32)]).astype(jnp.int32) for o in range(2)]

    def red_begin(tag, names, gs, layer):
        owners = [REDUCER[layer][n] for n in names]
        st, tok_ = pair_send_start(f"pair_send_start_{tag}", gs, owners, None)
        return (st, owners), tok_

    def red_mid(tag, sto, after):
        st, owners = sto
        gs_, lands_ = pair_send_wait(f"pair_send_wait_{tag}", st, after)
        ps = [pair_sum(f"pair_sum_{tag}_{a}", g_, r_, flags[o]) for a, (g_, r_, o) in enumerate(zip(gs_, lands_, owners))]
        st, tok_ = chip_exchange_start(f"chip_exchange_start_{tag}", ps, owners, None)
        return (st, owners), tok_

    def red_end(tag, sto, layer, prevs, after):
        st, owners = sto
        ps, lands_ = chip_exchange_wait(f"chip_exchange_wait_{tag}", st, after)
        return [chip_sum(f"chip_sum_{tag}_{a}", p_, r_, qfs[o], layer, s_)
                for a, (p_, r_, s_, o) in enumerate(zip(ps, lands_, prevs, owners))]

    def share(tag, names, sums_):
        return pair_share(f"pair_share_{tag}", sums_, [(REDUCER[0][n], REDUCER[1][n]) for n in names])

    st_p1 = st_c1 = st_pa = st_ca = None
    for l in reversed(range(L)):
        sv, w = saved[l], W[l]
        g = {}
        dh = ffn_bwd_da(f"ffn2_bwd_da_{l}", dzb, w["f2w2"], 0, sv["gate2"], sv["up2"], dep)
        dep = ()
        g["f2w2"] = row_shards(mm_tn(f"ffn2_dw2_{l}", sv["act2"], dzb))
        g["f2w13"] = mm_tn(f"ffn2_dw13_{l}", sv["x3b"], dh, True)
        dzb, dres, *dln[l, 2] = ffn_dx(f"ffn2_dx_{l}", dh, w["f2w13"], 0, dres, (sv["z3"], vec(lng[l, 2]), 1.0))
        if l == 0:
            st_c1, tok = red_mid("l1", st_p1, dzb)
            dep = (tok, g["f2w2"], g["f2w13"])
        dqc, dkvm = cross_bwd(f"cross_bwd_{l}", dzb, w["mwo"], 0, sv["crq"], sv["kvm"], dep)
        dep = ()
        g["mwo"] = row_shards(mm_tn(f"cross_dwo_{l}", sv["cro"], dzb))
        g["mwq"] = row_shards(mm_tn(f"cross_dwq_{l}", sv["x2b"], dqc))
        g["mwkv"] = mm_tn(f"cross_dwkv_{l}", memb, dkvm, True)
        dzb, dres, *dln[l, 1] = mm_nt_res(f"cross_dx_{l}", [dqc], [w["mwq"]], [0], dres, f32,
                                          (sv["z2"], vec(lng[l, 1]), 1.0))
        dyp = mm_nt_res(f"mix_dpool_{l}", [dzb], [w["wout_pool"]], [0], None, bf16)
        do = mm_nt_res(f"mix_do_{l}", [dzb], [w["wout_mla"]], [0], None, bf16)
        dwo_p = mm_tn(f"mix_dwout_pool_{l}", sv["ypool"], dzb)
        dwo_m = mm_tn(f"mix_dwout_mla_{l}", sv["o"], dzb)
        g["wout"] = row_shards(jnp.concatenate([dwo_p, dwo_m], axis=0))
        dq, dk, dv = mla_attn_bwd(f"mla_bwd_{l}", sv["q"], sv["k"], sv["v"], sv["o"], do, sv["lse"])
        dqe, dkv, dh_rest, dgq, dgkv = mix_post_bwd(f"mix_post_bwd_{l}", dq, dk, dv, w["wq_ext"], w["wukv"], 0, sv["cq"],
                                                     sv["ckv"], vec(q_norm_g[l]), vec(kv_norm_g[l]), cs)
        du, dyw, dscale = pool_bwd(f"pool_bwd_{l}", dyp, sv["dpool"], w["wbd"], vec(pool_scale[l]))
        dwq_e = mm_tn(f"mix_dwuq_{l}", sv["cqn"], dqe).reshape(Q_LORA, MLA_HEADS, 256)
        g["wuq"] = _to_col_shards(jnp.concatenate(
            [dwq_e[..., :QK_NOPE], _unswap_add(dwq_e[..., QK_NOPE:QK_NOPE + QK_ROPE], dwq_e[..., QK_NOPE + QK_ROPE:])],
            axis=-1).reshape(Q_LORA, MLA_HEADS * (QK_NOPE + QK_ROPE)))
        g["wukv"] = _to_col_shards(mm_tn(f"mix_dwukv_{l}", sv["ckvn"], dkv))
        dwbd = mm_tn(f"pool_dw_{l}", sv["dpool"], dyw)
        small["pool_w"][l] = jnp.stack([dwbd[64 * gi:64 * gi + 64, 64 * gi:64 * gi + 64] for gi in range(4)])
        small["pool_scale"][l], small["gq"][l], small["gkv"][l] = dscale[0], dgq[0], dgkv[0]
        dh_ext = jnp.concatenate([du, dh_rest], axis=1)
        dwin_e = mm_tn(f"mix_dwin_{l}", sv["x1b"], dh_ext)
        g["win"] = row_shards(jnp.concatenate(
            [dwin_e[:, :D_IN - QK_ROPE], _unswap_add(dwin_e[:, D_IN - QK_ROPE:D_IN], dwin_e[:, D_IN:])], axis=-1))
        dzb, dres, *dln[l, 0] = mm_nt_res(f"mix_dx_{l}", [dh_ext], [w["win_ext"]], [0], dres, f32,
                                          (sv["z1"], vec(lng[l, 0]), 0.5))
        if l == 0:
            st_pa, tok = red_begin("a0", rest_names, [g[n] for n in rest_names], 0)
            dep = (tok,)
        dh = ffn_bwd_da(f"ffn1_bwd_da_{l}", dzb, w["f1w2"], 0, sv["gate1"], sv["up1"], dep)
        dep = ()
        if l == 0:
            grad_x = ffn_dx(f"ffn1_dx_{l}", dh, w["f1w13"], 0, dres)[None]
            st_ca, tok = red_mid("a0", st_pa, grad_x)
            dep = (tok,)
        else:
            below = ffn_dx(f"ffn1_dx_{l}", dh, w["f1w13"], 0, dres, (saved[l - 1]["z4"], vec(lng[l - 1, 3]), 0.5))
            dln[l - 1, 3] = below[2:]
        g["f1w2"] = row_shards(mm_tn(f"ffn1_dw2_{l}", sv["act1"], dzb, False, dep))
        g["f1w13"] = mm_tn(f"ffn1_dw13_{l}", sv["x0b"], dh, True, dep)
        dep = ()
        if l > 0:
            dzb, dres = below[:2]
        if l == 1:
            st_p1, tok = red_begin("l1", W_NAMES, [g[n] for n in W_NAMES], 1)
            dep = (tok,)
    for l in range(L):
        small["lng"][l] = jnp.concatenate([dln[l, k][0] for k in range(4)], axis=0)
        small["lnb"][l] = jnp.concatenate([dln[l, k][1] for k in range(4)], axis=0)

    st_pb, _ = red_begin("b0", ("f1w13", "f1w2"), [g["f1w13"], g["f1w2"]], 0)
    sums1 = dict(zip(W_NAMES, red_end("l1", st_c1, 1, [None] * len(W_NAMES), g["f1w13"])))
    st_cb, _ = red_mid("b0", st_pb, sums1["f1w13"])
    sums0 = red_end("a0", st_ca, 0, [sums1[n] for n in rest_names], sums1["f1w2"])
    big = dict(zip(rest_names, share("a", rest_names, sums0)))

    rep = [jnp.stack(small["pool_w"]).reshape(-1), jnp.stack(small["pool_scale"]).reshape(-1),
           jnp.stack(small["gq"]).reshape(-1), jnp.stack(small["gkv"]).reshape(-1),
           jnp.stack(small["lng"]).reshape(-1), jnp.stack(small["lnb"]).reshape(-1)]
    sizes = [r.shape[0] for r in rep]
    packed = jnp.concatenate(rep)
    pad = (-packed.shape[0]) % 1024
    tot = allsum_small("allsum_small_grads", jnp.pad(packed, (0, pad)).reshape(-1, 128)).reshape(-1)
    offs = [0]
    for s_ in sizes:
        offs.append(offs[-1] + s_)
    parts = [tot[offs[i]:offs[i + 1]] for i in range(len(sizes))]
    g_pool_w = parts[0].reshape(pool_w.shape)
    g_pool_scale = parts[1].reshape(pool_scale.shape)
    g_gq = parts[2].reshape(q_norm_g.shape)
    g_gkv = parts[3].reshape(kv_norm_g.shape)
    shard_cols = lambda a: lax.dynamic_slice_in_dim(a.reshape(L, 4, D_MODEL), chip * (D_MODEL // N_CHIPS),
                                                    D_MODEL // N_CHIPS, axis=2)
    g_lng, g_lnb = shard_cols(parts[4]), shard_cols(parts[5])

    out_names = ("lng", "lnb", "f1w13", "f1w2", "win", "pool_w", "pool_scale", "gq", "wuq", "gkv", "wukv", "wout", "mwq",
                 "mwkv", "mwo", "f2w13", "f2w2")
    big.update(lng=g_lng, lnb=g_lnb, pool_w=g_pool_w, pool_scale=g_pool_scale, gq=g_gq, gkv=g_gkv)
    late = ("f1w13", "f1w2")
    ws = [ln_g, ln_b, ffn1_w13, ffn1_w2, w_in, pool_w, pool_scale, q_norm_g, w_uq, kv_norm_g, w_ukv, w_out, mem_wq,
          mem_wkv, mem_wo, ffn2_w13, ffn2_w2]
    ms = [m_ln_g, m_ln_b, m_ffn1_w13, m_ffn1_w2, m_w_in, m_pool_w, m_pool_scale, m_q_norm_g, m_w_uq, m_kv_norm_g, m_w_ukv,
          m_w_out, m_mem_wq, m_mem_wkv, m_mem_wo, m_ffn2_w13, m_ffn2_w2]
    vs = [v_ln_g, v_ln_b, v_ffn1_w13, v_ffn1_w2, v_w_in, v_pool_w, v_pool_scale, v_q_norm_g, v_w_uq, v_kv_norm_g, v_w_ukv,
          v_w_out, v_mem_wq, v_mem_wkv, v_mem_wo, v_ffn2_w13, v_ffn2_w2]
    res = {}

    def update(a):
        res[a] = adamw(f"adamw_{a}", ws[a], big[out_names[a]].reshape(ws[a].shape), ms[a], vs[a])

    for a, n in enumerate(out_names):
        if n not in late:
            update(a)
    sums_b = red_end("b0", st_cb, 0, [sums1[n] for n in late], tuple(r[0] for r in res.values()))
    big.update(zip(late, share("b", late, sums_b)))
    for a, n in enumerate(out_names):
        if n in late:
            update(a)
    order = range(len(out_names))
    grads = [big[n].reshape(w_.shape) for n, w_ in zip(out_names, ws)]
    return (loss, grad_x, *grads, *[res[a][0] for a in order], *[res[a][1] for a in order], *[res[a][2] for a in order])
```

```python
import functools
import math

import jax
import jax.numpy as jnp
from jax import lax
from jax.experimental import pallas as pl
from jax.experimental.pallas import tpu as pltpu

f32 = jnp.float32
bf16 = jnp.bfloat16
SDS = jax.ShapeDtypeStruct
MESH = pl.DeviceIdType.MESH

D_MODEL = 1024
DEPTH = 2
N_MEM = 256
MEM_HEADS = 4
MEM_HEAD_DIM = D_MODEL // MEM_HEADS
POOL_WINDOWS = (2, 4, 8, 16)
POOL_WIDTH = 256
POOL_GROUP = 64
QK_NOPE = 128
QK_ROPE = 64
V_HEAD = 128
MLA_HEADS = 6
Q_LORA = 256
KV_LORA = 128
ROPE_BASE = 10000.0
D_FF = 2816
D_IN = POOL_WIDTH + Q_LORA + KV_LORA + QK_ROPE
ALPHA = (2 * DEPTH) ** 0.25
LN_EPS = 1e-5
RMS_EPS = 1e-6
NEG_INF = -1e30
MLA_SCALE = (QK_NOPE + QK_ROPE) ** -0.5
MEM_SCALE = MEM_HEAD_DIM ** -0.5
ADAM_LR = 0.001
ADAM_B1 = 0.9
ADAM_B2 = 0.999
ADAM_EPS = 1e-08
ADAM_WD = 0.01
ADAM_STEP = 10

N_CHIPS = 4
V7X_VMEM_LIMIT = 56 * 2**20
HALO = 16

_NT = (((1,), (1,)), ((), ()))
_TN = (((0,), (0,)), ((), ()))


def _dot(a, b):
    return jnp.dot(a, b, preferred_element_type=f32)


def _dot_nt(a, b):
    return lax.dot_general(a, b, _NT, preferred_element_type=f32)


def _dot_tn(a, b):
    return lax.dot_general(a, b, _TN, preferred_element_type=f32)


def _cp(*sem):
    return pltpu.CompilerParams(dimension_semantics=sem if sem else None, vmem_limit_bytes=V7X_VMEM_LIMIT)


_DEP_SPEC = pl.BlockSpec(memory_space=pl.ANY)


def _with_deps(body, n_in, deps):
    nd = len(deps)
    if not nd:
        return body

    def wrapped(*refs):
        return body(*refs[:n_in], *refs[n_in + nd:])

    return wrapped


def _tile(n, t):
    t = min(n, t)
    assert n % t == 0, (n, t)
    return t


def _row_tile(rows, cols, itemsize=4, target=2 * 2**20):
    best = None
    for t in range(16, rows + 1, 16):
        if rows % t == 0 and t * cols * itemsize <= target:
            best = t
    return best if best is not None else rows


def ffn_up(name, xb, w13, l, deps=()):
    S = xb.shape[0]
    ns = w13.shape[3]
    tm = _tile(S, 512)

    def body(x_ref, wg_ref, wu_ref, g_ref, u_ref, a_ref):
        x = x_ref[...]
        g = _dot(x, wg_ref[0, 0])
        u = _dot(x, wu_ref[0, 0])
        a = g * jax.nn.sigmoid(g) * u
        g_ref[...] = g.astype(bf16)
        u_ref[...] = u.astype(bf16)
        a_ref[...] = a.astype(bf16)

    out = SDS((S, 2 * ns), bf16)
    return pl.pallas_call(
        _with_deps(body, 3, deps), name=name, grid=(2, S // tm),
        in_specs=[pl.BlockSpec((tm, D_MODEL), lambda j, i: (i, 0)),
                  pl.BlockSpec((1, 1, D_MODEL, ns), lambda j, i: (l, j, 0, 0)),
                  pl.BlockSpec((1, 1, D_MODEL, ns), lambda j, i: (l, j + 2, 0, 0))] + [_DEP_SPEC] * len(deps),
        out_specs=[pl.BlockSpec((tm, ns), lambda j, i: (i, j))] * 3,
        out_shape=[out, out, out],
        compiler_params=_cp("parallel", "parallel"),
    )(xb, w13, w13, *deps)


def proj_res_ln(name, parts, ws, wl, x, g, b, rscale, deps=()):
    S = x.shape[0]
    tm = _tile(S, 512)
    n = len(parts)

    def body(*refs):
        p_refs, w_refs = refs[:n], refs[n:2 * n]
        x_ref, g_ref, b_ref, z_ref, y_ref, yb_ref = refs[2 * n:]
        acc = _dot(p_refs[0][...], w_refs[0][0])
        for k in range(1, n):
            acc = acc + _dot(p_refs[k][...], w_refs[k][0])
        if rscale != 1.0:
            acc = rscale * acc
        z = ALPHA * x_ref[...] + acc
        mu = jnp.mean(z, axis=-1, keepdims=True)
        zc = z - mu
        var = jnp.mean(zc * zc, axis=-1, keepdims=True)
        y = zc * lax.rsqrt(var + LN_EPS) * g_ref[...] + b_ref[...]
        z_ref[...] = z
        y_ref[...] = y
        yb_ref[...] = y.astype(bf16)

    row = lambda i: (i, 0)
    in_specs = [pl.BlockSpec((tm, p.shape[1]), row) for p in parts]
    in_specs += [pl.BlockSpec((1,) + w.shape[1:], functools.partial(lambda li, i: (li, 0, 0), li)) for w, li in zip(ws, wl)]
    in_specs += [pl.BlockSpec((tm, D_MODEL), row), pl.BlockSpec((1, D_MODEL), lambda i: (0, 0)),
                 pl.BlockSpec((1, D_MODEL), lambda i: (0, 0))] + [_DEP_SPEC] * len(deps)
    return pl.pallas_call(
        _with_deps(body, 2 * n + 3, deps), name=name, grid=(S // tm,), in_specs=in_specs,
        out_specs=[pl.BlockSpec((tm, D_MODEL), row)] * 3,
        out_shape=[SDS((S, D_MODEL), f32), SDS((S, D_MODEL), f32), SDS((S, D_MODEL), bf16)],
        compiler_params=_cp("parallel"),
    )(*parts, *ws, x, g, b, *deps)


def _ln_bwd_store(dyv, z_ref, g_ref, rscale, first, dzb_ref, dres_ref, dg_ref, db_ref):
    z = z_ref[...]
    mu = jnp.mean(z, axis=-1, keepdims=True)
    zc = z - mu
    rstd = lax.rsqrt(jnp.mean(zc * zc, axis=-1, keepdims=True) + LN_EPS)
    xhat = zc * rstd
    dxh = dyv * g_ref[...]
    m1 = jnp.mean(dxh, axis=-1, keepdims=True)
    m2 = jnp.mean(dxh * xhat, axis=-1, keepdims=True)
    dz = rstd * (dxh - m1 - xhat * m2)
    dzb_ref[...] = (rscale * dz).astype(bf16)
    dres_ref[...] = ALPHA * dz

    @pl.when(first)
    def _():
        dg_ref[...] = jnp.zeros_like(dg_ref)
        db_ref[...] = jnp.zeros_like(db_ref)

    dg_ref[...] += jnp.sum(dyv * xhat, axis=0, keepdims=True)
    db_ref[...] += jnp.sum(dyv, axis=0, keepdims=True)


def _ln_bwd_specs(S, tm, index):
    vec = pl.BlockSpec((1, D_MODEL), lambda *a: (0, 0))
    blk = pl.BlockSpec((tm, D_MODEL), index)
    in_specs = [blk, vec]
    out_specs = [blk, blk, vec, vec]
    out_shape = [SDS((S, D_MODEL), bf16), SDS((S, D_MODEL), f32), SDS((1, D_MODEL), f32), SDS((1, D_MODEL), f32)]
    return in_specs, out_specs, out_shape


def ffn_bwd_da(name, drb, w2, l, gate, up, deps=()):
    S = drb.shape[0]
    tm = _tile(S, 512)
    nh = D_FF // 2

    def body(dr_ref, w_ref, g_ref, u_ref, dh_ref):
        dr = dr_ref[...]
        for j in range(2):
            cols = slice(j * nh, (j + 1) * nh)
            da = _dot_nt(dr, w_ref[0, cols, :])
            g = g_ref[:, cols].astype(f32)
            u = u_ref[:, cols].astype(f32)
            sg = jax.nn.sigmoid(g)
            dh_ref[:, cols] = (da * u * (sg * (1.0 + g * (1.0 - sg)))).astype(bf16)
            dh_ref[:, D_FF + j * nh:D_FF + (j + 1) * nh] = (da * (g * sg)).astype(bf16)

    row = lambda i: (i, 0)
    return pl.pallas_call(
        _with_deps(body, 4, deps), name=name, grid=(S // tm,),
        in_specs=[pl.BlockSpec((tm, D_MODEL), row), pl.BlockSpec((1, D_FF, D_MODEL), lambda i: (l, 0, 0)),
                  pl.BlockSpec((tm, D_FF), row), pl.BlockSpec((tm, D_FF), row)] + [_DEP_SPEC] * len(deps),
        out_specs=pl.BlockSpec((tm, 2 * D_FF), row),
        out_shape=SDS((S, 2 * D_FF), bf16),
        compiler_params=_cp("parallel"),
    )(drb, w2, gate, up, *deps)


def ffn_dx(name, dh, w13, l, res, ln=None):
    S = dh.shape[0]
    ns = w13.shape[3]
    tm = _tile(S, 1024)
    last = N_CHIPS - 1
    row = lambda i, j: (i, 0)
    in_specs = [pl.BlockSpec((tm, ns), lambda i, j: (i, j)),
                pl.BlockSpec((1, 1, D_MODEL, ns), lambda i, j: (l, j, 0, 0)),
                pl.BlockSpec((tm, D_MODEL), row)]
    if ln is None:
        def body(dh_ref, w_ref, r_ref, o_ref):
            @pl.when(pl.program_id(1) == 0)
            def _():
                o_ref[...] = r_ref[...]

            o_ref[...] += _dot_nt(dh_ref[...], w_ref[0, 0])

        return pl.pallas_call(
            body, name=name, grid=(S // tm, N_CHIPS), in_specs=in_specs,
            out_specs=pl.BlockSpec((tm, D_MODEL), row), out_shape=SDS((S, D_MODEL), f32),
            compiler_params=_cp("parallel", "arbitrary"),
        )(dh, w13, res)

    z, g, rscale = ln

    def body_ln(dh_ref, w_ref, r_ref, z_ref, g_ref, dzb_ref, dres_ref, dg_ref, db_ref, acc_sc):
        i, j = pl.program_id(0), pl.program_id(1)

        @pl.when(j == 0)
        def _():
            acc_sc[...] = r_ref[...]

        acc_sc[...] += _dot_nt(dh_ref[...], w_ref[0, 0])

        @pl.when(j == last)
        def _():
            _ln_bwd_store(acc_sc[...], z_ref, g_ref, rscale, i == 0, dzb_ref, dres_ref, dg_ref, db_ref)

    ln_in, ln_out, ln_shape = _ln_bwd_specs(S, tm, row)
    return pl.pallas_call(
        body_ln, name=name, grid=(S // tm, N_CHIPS), in_specs=in_specs + ln_in, out_specs=ln_out, out_shape=ln_shape,
        scratch_shapes=[pltpu.VMEM((tm, D_MODEL), f32)], compiler_params=_cp("arbitrary", "arbitrary"),
    )(dh, w13, res, z, g)


def mm_nt_res(name, dys, ws, wl, res, out_dtype, ln=None):
    S = dys[0].shape[0]
    K = ws[0].shape[1]
    tm = _tile(S, 512)
    n = len(dys)
    n_in = 2 * n + (res is not None)

    def product(refs):
        acc = _dot_nt(refs[0][...], refs[n][0])
        for k in range(1, n):
            acc = acc + _dot_nt(refs[k][...], refs[n + k][0])
        if res is not None:
            acc = acc + refs[2 * n][...]
        return acc

    def body(*refs):
        refs[-1][...] = product(refs).astype(out_dtype)

    def body_ln(*refs):
        z_ref, g_ref, dzb_ref, dres_ref, dg_ref, db_ref = refs[n_in:]
        _ln_bwd_store(product(refs), z_ref, g_ref, ln[2], pl.program_id(0) == 0, dzb_ref, dres_ref, dg_ref, db_ref)

    row = lambda i: (i, 0)
    in_specs = [pl.BlockSpec((tm, d.shape[1]), row) for d in dys]
    in_specs += [pl.BlockSpec((1,) + w.shape[1:], functools.partial(lambda li, i: (li, 0, 0), li)) for w, li in zip(ws, wl)]
    args = list(dys) + list(ws)
    if res is not None:
        in_specs.append(pl.BlockSpec((tm, K), row))
        args.append(res)
    if ln is None:
        return pl.pallas_call(
            body, name=name, grid=(S // tm,), in_specs=in_specs,
            out_specs=pl.BlockSpec((tm, K), row), out_shape=SDS((S, K), out_dtype),
            compiler_params=_cp("parallel"),
        )(*args)
    ln_in, ln_out, ln_shape = _ln_bwd_specs(S, tm, row)
    return pl.pallas_call(
        body_ln, name=name, grid=(S // tm,), in_specs=in_specs + ln_in, out_specs=ln_out, out_shape=ln_shape,
        compiler_params=_cp("arbitrary"),
    )(*args, ln[0], ln[1])


def mm_tn(name, x, dy, col_shards=False, deps=()):
    S, K = x.shape
    N = dy.shape[1]
    ts = 512
    while ts * 2 <= min(S, 2048) and S % (ts * 2) == 0 and ts * 2 * K * 2 <= 6 * 2**20:
        ts *= 2
    ts = _tile(S, ts)
    if col_shards:
        tn = N // N_CHIPS
    else:
        tn = N
        while K * tn * 4 > 6 * 2**20 and tn % 256 == 0:
            tn //= 2
    nn = N // tn
    lead = ((0,) if col_shards else ()) + (slice(None), slice(None))

    def body(x_ref, dy_ref, o_ref):
        acc = _dot_tn(x_ref[...].astype(bf16), dy_ref[...].astype(bf16))

        @pl.when(pl.program_id(1) == 0)
        def _():
            o_ref[lead] = acc

        @pl.when(pl.program_id(1) != 0)
        def _():
            o_ref[lead] += acc

    if col_shards:
        out_spec = pl.BlockSpec((1, K, tn), lambda n, s: (n, 0, 0))
        out_shape = SDS((N_CHIPS, K, tn), f32)
    else:
        out_spec = pl.BlockSpec((K, tn), lambda n, s: (0, n))
        out_shape = SDS((K, N), f32)
    return pl.pallas_call(
        _with_deps(body, 2, deps), name=name, grid=(nn, S // ts),
        in_specs=[pl.BlockSpec((ts, K), lambda n, s: (s, 0)), pl.BlockSpec((ts, tn), lambda n, s: (s, n))]
        + [_DEP_SPEC] * len(deps),
        out_specs=out_spec, out_shape=out_shape, compiler_params=_cp("parallel", "arbitrary"),
    )(x, dy, *deps)


def mm_nn_shard(name, x, w, l):
    S, K = x.shape
    ns = w.shape[3]

    def body(x_ref, w_ref, o_ref):
        o_ref[...] = _dot(x_ref[...], w_ref[0, 0]).astype(bf16)

    return pl.pallas_call(
        body, name=name, grid=(N_CHIPS,),
        in_specs=[pl.BlockSpec((S, K), lambda j: (0, 0)), pl.BlockSpec((1, 1, K, ns), lambda j: (l, j, 0, 0))],
        out_specs=pl.BlockSpec((S, ns), lambda j: (0, j)), out_shape=SDS((S, N_CHIPS * ns), bf16),
        compiler_params=_cp("parallel"),
    )(x, w)


def loss_grad(name, y, t, ln):
    S = y.shape[0]
    tm = _tile(S, 512)
    z, g, rscale = ln

    def body(y_ref, t_ref, z_ref, g_ref, dzb_ref, dres_ref, dg_ref, db_ref, loss_ref):
        first = pl.program_id(0) == 0
        e = y_ref[...] - t_ref[...]
        _ln_bwd_store(e * (1.0 / D_MODEL), z_ref, g_ref, rscale, first, dzb_ref, dres_ref, dg_ref, db_ref)

        @pl.when(first)
        def _():
            loss_ref[...] = jnp.zeros_like(loss_ref)

        loss_ref[...] += jnp.full(loss_ref.shape, (0.5 / D_MODEL) * jnp.sum(e * e), f32)

    row = lambda i: (i, 0)
    ln_in, ln_out, ln_shape = _ln_bwd_specs(S, tm, row)
    return pl.pallas_call(
        body, name=name, grid=(S // tm,),
        in_specs=[pl.BlockSpec((tm, D_MODEL), row)] * 2 + ln_in,
        out_specs=ln_out + [pl.BlockSpec((8, 128), lambda i: (0, 0))],
        out_shape=ln_shape + [SDS((8, 128), f32)],
        compiler_params=_cp("arbitrary"),
    )(y, t, z, g)


def _half_sum(t):
    return t + pltpu.roll(t, 64, axis=1)


def mix_pre(name, xb, w_in, wq, wkv, l, gq, gkv, cs):
    S = xb.shape[0]
    tm = _tile(S, 512)
    H = MLA_HEADS
    W_EXT = w_in.shape[2]

    def body(x_ref, win_ref, wq_ref, wkv_ref, gq_ref, gkv_ref, cs_ref,
             u_ref, cq_ref, ckv_ref, cqn_ref, ckvn_ref, q_ref, k_ref, v_ref):
        h = _dot(x_ref[...], win_ref[0])
        u_ref[...] = h[:, :256]
        cq = h[:, 256:512]
        ckv = h[:, 512:640]
        cq_ref[...] = cq
        ckv_ref[...] = ckv
        cqn = (cq * lax.rsqrt(jnp.mean(cq * cq, axis=-1, keepdims=True) + RMS_EPS) * gq_ref[...]).astype(bf16)
        ckvn = (ckv * lax.rsqrt(jnp.mean(ckv * ckv, axis=-1, keepdims=True) + RMS_EPS) * gkv_ref[...]).astype(bf16)
        cqn_ref[...] = cqn
        ckvn_ref[...] = ckvn
        csv = cs_ref[...]
        lane = lax.broadcasted_iota(jnp.int32, (tm, 128), 1)
        kr = jnp.where(lane < 64, _half_sum(h[:, 640:768] * csv), 0.0).astype(bf16)
        kv = _dot(ckvn, wkv_ref[0])
        for hd in range(H):
            qe = _dot(cqn, wq_ref[0, hd])
            q_ref[hd, :, :128] = qe[:, :128].astype(bf16)
            q_ref[hd, :, 128:] = _half_sum(qe[:, 128:] * csv).astype(bf16)
            k_ref[hd, :, :128] = kv[:, 256 * hd:256 * hd + 128].astype(bf16)
            k_ref[hd, :, 128:] = kr
            v_ref[hd] = kv[:, 256 * hd + 128:256 * hd + 256].astype(bf16)

    row = lambda i: (i, 0)
    hrow = lambda i: (0, i, 0)
    return pl.pallas_call(
        body, name=name, grid=(S // tm,),
        in_specs=[pl.BlockSpec((tm, D_MODEL), row),
                  pl.BlockSpec((1, D_MODEL, W_EXT), lambda i: (l, 0, 0)),
                  pl.BlockSpec((1, H, Q_LORA, 256), lambda i: (l, 0, 0, 0)),
                  pl.BlockSpec((1, KV_LORA, H * 256), lambda i: (l, 0, 0)),
                  pl.BlockSpec((1, Q_LORA), lambda i: (0, 0)), pl.BlockSpec((1, KV_LORA), lambda i: (0, 0)),
                  pl.BlockSpec((tm, 128), row)],
        out_specs=[pl.BlockSpec((tm, 256), row), pl.BlockSpec((tm, Q_LORA), row), pl.BlockSpec((tm, KV_LORA), row),
                   pl.BlockSpec((tm, Q_LORA), row), pl.BlockSpec((tm, KV_LORA), row),
                   pl.BlockSpec((H, tm, 256), hrow), pl.BlockSpec((H, tm, 256), hrow), pl.BlockSpec((H, tm, 128), hrow)],
        out_shape=[SDS((S, 256), f32), SDS((S, Q_LORA), f32), SDS((S, KV_LORA), f32),
                   SDS((S, Q_LORA), bf16), SDS((S, KV_LORA), bf16),
                   SDS((H, S, 256), bf16), SDS((H, S, 256), bf16), SDS((H, S, 128), bf16)],
        compiler_params=_cp("parallel"),
    )(xb, w_in, wq, wkv, gq, gkv, cs)


def _group_select(col, a2, a4, a8, a16):
    return jnp.where(col < 64, a2, jnp.where(col < 128, a4, jnp.where(col < 192, a8, a16)))


def pool_fwd(name, u, wbd, scale):
    S = u.shape[0]
    tm = _tile(S, 512)
    hb = tm // HALO

    def body(u_ref, halo_ref, w_ref, s_ref, d_ref, y_ref):
        i = pl.program_id(0)
        cur = u_ref[...]
        halo = jnp.where(i > 0, halo_ref[...], 0.0)
        ext = jnp.concatenate([halo, cur], axis=0)
        s2 = ext + pltpu.roll(ext, 1, axis=0)
        s4 = s2 + pltpu.roll(s2, 2, axis=0)
        s8 = s4 + pltpu.roll(s4, 4, axis=0)
        s16 = s8 + pltpu.roll(s8, 8, axis=0)
        t1 = (i * tm + 1 + lax.broadcasted_iota(jnp.int32, (tm, 1), 0)).astype(f32)
        col = lax.broadcasted_iota(jnp.int32, (tm, 256), 1)
        m = _group_select(col, s2[HALO:] / jnp.minimum(t1, 2.0), s4[HALO:] / jnp.minimum(t1, 4.0),
                          s8[HALO:] / jnp.minimum(t1, 8.0), s16[HALO:] / jnp.minimum(t1, 16.0))
        d = (m - cur).astype(bf16)
        d_ref[...] = d
        y_ref[...] = (_dot(d, w_ref[...]) * s_ref[...]).astype(bf16)

    row = lambda i: (i, 0)
    return pl.pallas_call(
        body, name=name, grid=(S // tm,),
        in_specs=[pl.BlockSpec((tm, 256), row), pl.BlockSpec((HALO, 256), lambda i: (jnp.maximum(i * hb - 1, 0), 0)),
                  pl.BlockSpec((256, 256), lambda i: (0, 0)), pl.BlockSpec((1, 256), lambda i: (0, 0))],
        out_specs=[pl.BlockSpec((tm, 256), row)] * 2,
        out_shape=[SDS((S, 256), bf16), SDS((S, 256), bf16)],
        compiler_params=_cp("parallel"),
    )(u, u, wbd, scale)


def pool_bwd(name, dyp, d, wbd, scale):
    S = dyp.shape[0]
    tm = _tile(S, 512)
    hb = tm // HALO
    n_ext = tm + HALO

    def fwd_sum(e, steps):
        k = 1
        for _ in range(steps):
            e = e + pltpu.roll(e, n_ext - k, axis=0)
            k *= 2
        return e

    def body(dy_ref, halo_ref, d_ref, w_ref, s_ref, du_ref, dyw_ref, ds_ref):
        i = pl.program_id(0)
        sc = s_ref[...]
        w = w_ref[...]
        cur = dy_ref[...].astype(f32)
        halo = jnp.where(i < pl.num_programs(0) - 1, halo_ref[...].astype(f32), 0.0)
        dyw = jnp.concatenate([cur, halo], axis=0) * sc
        dyw_ref[...] = dyw[:tm].astype(bf16)
        dd = _dot_nt(dyw.astype(bf16), w)
        t1 = (i * tm + 1 + lax.broadcasted_iota(jnp.int32, (n_ext, 1), 0)).astype(f32)
        f2 = fwd_sum(dd / jnp.minimum(t1, 2.0), 1)
        f4 = fwd_sum(dd / jnp.minimum(t1, 4.0), 2)
        f8 = fwd_sum(dd / jnp.minimum(t1, 8.0), 3)
        f16 = fwd_sum(dd / jnp.minimum(t1, 16.0), 4)
        col = lax.broadcasted_iota(jnp.int32, (tm, 256), 1)
        du_ref[...] = (_group_select(col, f2[:tm], f4[:tm], f8[:tm], f16[:tm]) - dd[:tm]).astype(bf16)

        @pl.when(i == 0)
        def _():
            ds_ref[...] = jnp.zeros_like(ds_ref)

        ds_ref[...] += jnp.sum(cur * _dot(d_ref[...], w), axis=0, keepdims=True)

    row = lambda i: (i, 0)
    nhb = S // HALO
    return pl.pallas_call(
        body, name=name, grid=(S // tm,),
        in_specs=[pl.BlockSpec((tm, 256), row), pl.BlockSpec((HALO, 256), lambda i: (jnp.minimum((i + 1) * hb, nhb - 1), 0)),
                  pl.BlockSpec((tm, 256), row), pl.BlockSpec((256, 256), lambda i: (0, 0)),
                  pl.BlockSpec((1, 256), lambda i: (0, 0))],
        out_specs=[pl.BlockSpec((tm, 256), row), pl.BlockSpec((tm, 256), row), pl.BlockSpec((1, 256), lambda i: (0, 0))],
        out_shape=[SDS((S, 256), bf16), SDS((S, 256), bf16), SDS((1, 256), f32)],
        compiler_params=_cp("arbitrary"),
    )(dyp, dyp, d, wbd, scale)


def _diag_mask(tq):
    rc = lax.broadcasted_iota(jnp.int32, (tq, 1), 0) // 64
    cc = lax.broadcasted_iota(jnp.int32, (1, tq), 1) // 64
    return rc >= cc


MLA_SCALE_LOG2 = MLA_SCALE * math.log2(math.e)


def mla_attn_fwd(name, q, k, v):
    H, S, _ = q.shape
    tq = _tile(S, 1024)
    nq = S // tq
    pairs = [(i, j) for i in range(nq) for j in range(i + 1)]
    it = jnp.asarray([p_[0] for p_ in pairs], jnp.int32)
    jt = jnp.asarray([p_[1] for p_ in pairs], jnp.int32)

    def body(it_ref, jt_ref, q_ref, k_ref, v_ref, o_ref, lse_ref, m_sc, l_sc, acc_sc):
        t = pl.program_id(1)
        i, j = it_ref[t], jt_ref[t]

        @pl.when(j == 0)
        def _():
            m_sc[...] = jnp.full_like(m_sc, NEG_INF)
            l_sc[...] = jnp.zeros_like(l_sc)
            acc_sc[...] = jnp.zeros_like(acc_sc)

        def step(masked):
            s = _dot_nt(q_ref[0], k_ref[0])
            if masked:
                s = jnp.where(_diag_mask(tq), s, NEG_INF)
            m_prev = m_sc[...]
            m_new = jnp.maximum(m_prev, jnp.max(s, axis=-1, keepdims=True))
            p = jnp.exp2((s - jnp.tile(m_new, (1, tq // 128))) * MLA_SCALE_LOG2)
            a = jnp.exp2((m_prev - m_new) * MLA_SCALE_LOG2)
            l_sc[...] = a * l_sc[...] + jnp.sum(p, axis=-1, keepdims=True)
            acc_sc[...] = a * acc_sc[...] + _dot(p.astype(bf16), v_ref[0])
            m_sc[...] = m_new

        @pl.when(j < i)
        def _():
            step(False)

        @pl.when(j == i)
        def _():
            step(True)
            o_ref[...] = (acc_sc[...] / l_sc[...]).astype(bf16)
            lse_ref[0] = m_sc[...] * MLA_SCALE_LOG2 + jnp.log2(l_sc[...])

    return pl.pallas_call(
        body, name=name,
        grid_spec=pltpu.PrefetchScalarGridSpec(
            num_scalar_prefetch=2, grid=(H, len(pairs)),
            in_specs=[pl.BlockSpec((1, tq, 256), lambda h, t, it_, jt_: (h, it_[t], 0)),
                      pl.BlockSpec((1, tq, 256), lambda h, t, it_, jt_: (h, jt_[t], 0)),
                      pl.BlockSpec((1, tq, 128), lambda h, t, it_, jt_: (h, jt_[t], 0))],
            out_specs=[pl.BlockSpec((tq, 128), lambda h, t, it_, jt_: (it_[t], h)),
                       pl.BlockSpec((1, tq, 128), lambda h, t, it_, jt_: (h, it_[t], 0))],
            scratch_shapes=[pltpu.VMEM((tq, 128), f32), pltpu.VMEM((tq, 128), f32), pltpu.VMEM((tq, 128), f32)]),
        out_shape=[SDS((S, H * 128), bf16), SDS((H, S, 128), f32)],
        compiler_params=_cp("parallel", "arbitrary"),
    )(it, jt, q, k, v)


def mla_attn_bwd(name, q, k, v, o, do, lse):
    H, S, _ = q.shape
    tq = _tile(S, 1024)
    nq = S // tq
    pairs = [(i, j) for j in range(nq) for i in range(j, nq)]
    it = jnp.asarray([p_[0] for p_ in pairs], jnp.int32)
    jt = jnp.asarray([p_[1] for p_ in pairs], jnp.int32)
    n_pairs = len(pairs)

    def body(it_ref, jt_ref, q_ref, k_ref, v_ref, o_ref, do_ref, lse_ref, dq_ref, dk_ref, dv_ref, dk_sc, dv_sc):
        t = pl.program_id(1)
        i, j = it_ref[t], jt_ref[t]

        @pl.when(t == 0)
        def _():
            dq_ref[...] = jnp.zeros_like(dq_ref)

        @pl.when(i == j)
        def _():
            dk_sc[...] = jnp.zeros_like(dk_sc)
            dv_sc[...] = jnp.zeros_like(dv_sc)

        def step(masked):
            qv, kv_, dov = q_ref[0], k_ref[0], do_ref[...]
            s = _dot_nt(qv, kv_)
            if masked:
                s = jnp.where(_diag_mask(tq), s, NEG_INF)
            p = jnp.exp2(s * MLA_SCALE_LOG2 - jnp.tile(lse_ref[0], (1, tq // 128)))
            dv_sc[...] += _dot_tn(p.astype(bf16), dov)
            dp = _dot_nt(dov, v_ref[0])
            delta = jnp.sum(dov.astype(f32) * o_ref[...].astype(f32), axis=-1, keepdims=True)
            ds = (p * (dp - delta)).astype(bf16)
            dk_sc[...] += _dot_tn(ds, qv)
            rows = pl.ds(pl.multiple_of(i * tq, tq), tq)
            dq_ref[0, rows, :] += _dot(ds, kv_)

        @pl.when(i > j)
        def _():
            step(False)

        @pl.when(i == j)
        def _():
            step(True)

        @pl.when(i == nq - 1)
        def _():
            dk_ref[0] = dk_sc[...] * MLA_SCALE
            dv_ref[0] = dv_sc[...]

        @pl.when(t == n_pairs - 1)
        def _():
            dq_ref[...] = dq_ref[...] * MLA_SCALE

    qi = lambda h, t, it_, jt_: (h, it_[t], 0)
    kj = lambda h, t, it_, jt_: (h, jt_[t], 0)
    oi = lambda h, t, it_, jt_: (it_[t], h)
    return pl.pallas_call(
        body, name=name,
        grid_spec=pltpu.PrefetchScalarGridSpec(
            num_scalar_prefetch=2, grid=(H, n_pairs),
            in_specs=[pl.BlockSpec((1, tq, 256), qi), pl.BlockSpec((1, tq, 256), kj), pl.BlockSpec((1, tq, 128), kj),
                      pl.BlockSpec((tq, 128), oi), pl.BlockSpec((tq, 128), oi), pl.BlockSpec((1, tq, 128), qi)],
            out_specs=[pl.BlockSpec((1, S, 256), lambda h, t, it_, jt_: (h, 0, 0)), pl.BlockSpec((1, tq, 256), kj),
                       pl.BlockSpec((1, tq, 128), kj)],
            scratch_shapes=[pltpu.VMEM((tq, 256), f32), pltpu.VMEM((tq, 128), f32)]),
        out_shape=[SDS((H, S, 256), f32), SDS((H, S, 256), f32), SDS((H, S, 128), f32)],
        compiler_params=_cp("parallel", "arbitrary"),
    )(it, jt, q, k, v, o, do, lse)


def mix_post_bwd(name, dq, dk, dv, wq, wkv, l, cq, ckv, gq, gkv, cs):
    H, S, _ = dq.shape
    tm = _tile(S, 512)

    def rms_bwd(dyn, c, g):
        r = lax.rsqrt(jnp.mean(c * c, axis=-1, keepdims=True) + RMS_EPS)
        ch = c * r
        dyg = dyn * g
        dc = r * (dyg - ch * jnp.mean(dyg * ch, axis=-1, keepdims=True))
        return dc, jnp.sum(dyn * ch, axis=0, keepdims=True)

    def body(dq_ref, dk_ref, dv_ref, wq_ref, wkv_ref, cq_ref, ckv_ref, gq_ref, gkv_ref, cs_ref,
             dqe_ref, dkv_ref, dh_ref, dgq_ref, dgkv_ref):
        csv = cs_ref[...]
        lane = lax.broadcasted_iota(jnp.int32, (tm, 128), 1)
        dcqn = jnp.zeros((tm, Q_LORA), f32)
        dkr = jnp.zeros((tm, 128), f32)
        for hd in range(H):
            dqh = dq_ref[hd]
            dqe = jnp.concatenate([dqh[:, :128], _half_sum(dqh[:, 128:]) * csv], axis=1).astype(bf16)
            dqe_ref[:, 256 * hd:256 * hd + 256] = dqe
            dcqn = dcqn + _dot_nt(dqe, wq_ref[0, hd])
            dkh = dk_ref[hd]
            dkv_ref[:, 256 * hd:256 * hd + 128] = dkh[:, :128].astype(bf16)
            dkv_ref[:, 256 * hd + 128:256 * hd + 256] = dv_ref[hd].astype(bf16)
            dkr = dkr + dkh[:, 128:]
        dckvn = _dot_nt(dkv_ref[...], wkv_ref[0])
        dblk = _half_sum(jnp.where(lane < 64, dkr, 0.0)) * csv
        dcq, dgq = rms_bwd(dcqn, cq_ref[...], gq_ref[...])
        dckv, dgkv = rms_bwd(dckvn, ckv_ref[...], gkv_ref[...])
        dh_ref[:, :256] = dcq.astype(bf16)
        dh_ref[:, 256:384] = dckv.astype(bf16)
        dh_ref[:, 384:] = dblk.astype(bf16)

        @pl.when(pl.program_id(0) == 0)
        def _():
            dgq_ref[...] = jnp.zeros_like(dgq_ref)
            dgkv_ref[...] = jnp.zeros_like(dgkv_ref)

        dgq_ref[...] += dgq
        dgkv_ref[...] += dgkv

    row = lambda i: (i, 0)
    hrow = lambda i: (0, i, 0)
    return pl.pallas_call(
        body, name=name, grid=(S // tm,),
        in_specs=[pl.BlockSpec((H, tm, 256), hrow), pl.BlockSpec((H, tm, 256), hrow), pl.BlockSpec((H, tm, 128), hrow),
                  pl.BlockSpec((1, H, Q_LORA, 256), lambda i: (l, 0, 0, 0)),
                  pl.BlockSpec((1, KV_LORA, H * 256), lambda i: (l, 0, 0)),
                  pl.BlockSpec((tm, Q_LORA), row), pl.BlockSpec((tm, KV_LORA), row),
                  pl.BlockSpec((1, Q_LORA), lambda i: (0, 0)), pl.BlockSpec((1, KV_LORA), lambda i: (0, 0)),
                  pl.BlockSpec((tm, 128), row)],
        out_specs=[pl.BlockSpec((tm, H * 256), row), pl.BlockSpec((tm, H * 256), row), pl.BlockSpec((tm, 512), row),
                   pl.BlockSpec((1, Q_LORA), lambda i: (0, 0)), pl.BlockSpec((1, KV_LORA), lambda i: (0, 0))],
        out_shape=[SDS((S, H * 256), bf16), SDS((S, H * 256), bf16), SDS((S, 512), bf16),
                   SDS((1, Q_LORA), f32), SDS((1, KV_LORA), f32)],
        compiler_params=_cp("arbitrary"),
    )(dq, dk, dv, wq, wkv, cq, ckv, gq, gkv, cs)


def _cross_probs(qb, kv_ref, hd):
    cols = slice(hd * MEM_HEAD_DIM, (hd + 1) * MEM_HEAD_DIM)
    s = _dot_nt(qb[:, cols], kv_ref[:, cols]) * MEM_SCALE
    e = jnp.exp(s - jnp.max(s, axis=-1, keepdims=True))
    return e / jnp.sum(e, axis=-1, keepdims=True)


def cross_fwd(name, xb, xf, wq, wo, l, kv, g, b):
    S = xb.shape[0]
    tm = _tile(S, 512)
    M = kv.shape[0]

    def body(x_ref, xf_ref, wq_ref, wo_ref, k_ref, v_ref, g_ref, b_ref, q_ref, o_ref, z_ref, y_ref, yb_ref):
        qb = _dot(x_ref[...], wq_ref[0]).astype(bf16)
        q_ref[...] = qb
        for hd in range(MEM_HEADS):
            cols = slice(hd * MEM_HEAD_DIM, (hd + 1) * MEM_HEAD_DIM)
            p = _cross_probs(qb, k_ref, hd)
            o_ref[:, cols] = _dot(p.astype(bf16), v_ref[:, cols]).astype(bf16)
        z = ALPHA * xf_ref[...] + _dot(o_ref[...], wo_ref[0])
        mu = jnp.mean(z, axis=-1, keepdims=True)
        zc = z - mu
        var = jnp.mean(zc * zc, axis=-1, keepdims=True)
        y = zc * lax.rsqrt(var + LN_EPS) * g_ref[...] + b_ref[...]
        z_ref[...] = z
        y_ref[...] = y
        yb_ref[...] = y.astype(bf16)

    row = lambda i: (i, 0)
    wspec = pl.BlockSpec((1, D_MODEL, D_MODEL), lambda i: (l, 0, 0))
    vec = pl.BlockSpec((1, D_MODEL), lambda i: (0, 0))
    blk = pl.BlockSpec((tm, D_MODEL), row)
    return pl.pallas_call(
        body, name=name, grid=(S // tm,),
        in_specs=[blk, blk, wspec, wspec, pl.BlockSpec((M, D_MODEL), lambda i: (0, 0)),
                  pl.BlockSpec((M, D_MODEL), lambda i: (0, 1)), vec, vec],
        out_specs=[blk] * 5,
        out_shape=[SDS((S, D_MODEL), bf16), SDS((S, D_MODEL), bf16), SDS((S, D_MODEL), f32), SDS((S, D_MODEL), f32),
                   SDS((S, D_MODEL), bf16)],
        compiler_params=_cp("parallel"),
    )(xb, xf, wq, wo, kv, kv, g, b)


def cross_bwd(name, dzb, wo, l, qb, kv, deps=()):
    S = dzb.shape[0]
    tm = _tile(S, 512)
    M = kv.shape[0]

    def body(dz_ref, wo_ref, q_ref, k_ref, v_ref, dq_ref, dkv_ref):
        @pl.when(pl.program_id(0) == 0)
        def _():
            dkv_ref[...] = jnp.zeros_like(dkv_ref)

        do = _dot_nt(dz_ref[...], wo_ref[0]).astype(bf16)
        qv = q_ref[...]
        for hd in range(MEM_HEADS):
            cols = slice(hd * MEM_HEAD_DIM, (hd + 1) * MEM_HEAD_DIM)
            vcols = slice(D_MODEL + hd * MEM_HEAD_DIM, D_MODEL + (hd + 1) * MEM_HEAD_DIM)
            p = _cross_probs(qv, k_ref, hd)
            doh = do[:, cols]
            dkv_ref[:, vcols] += _dot_tn(p.astype(bf16), doh)
            dp = _dot_nt(doh, v_ref[:, cols])
            ds = (p * (dp - jnp.sum(dp * p, axis=-1, keepdims=True)) * MEM_SCALE).astype(bf16)
            dq_ref[:, cols] = _dot(ds, k_ref[:, cols]).astype(bf16)
            dkv_ref[:, cols] += _dot_tn(ds, qv[:, cols])

    row = lambda i: (i, 0)
    blk = pl.BlockSpec((tm, D_MODEL), row)
    return pl.pallas_call(
        _with_deps(body, 5, deps), name=name, grid=(S // tm,),
        in_specs=[blk, pl.BlockSpec((1, D_MODEL, D_MODEL), lambda i: (l, 0, 0)), blk,
                  pl.BlockSpec((M, D_MODEL), lambda i: (0, 0)), pl.BlockSpec((M, D_MODEL), lambda i: (0, 1))]
        + [_DEP_SPEC] * len(deps),
        out_specs=[blk, pl.BlockSpec((M, 2 * D_MODEL), lambda i: (0, 0))],
        out_shape=[SDS((S, D_MODEL), bf16), SDS((M, 2 * D_MODEL), f32)],
        compiler_params=_cp("arbitrary"),
    )(dzb, wo, qb, kv, kv, *deps)


def adamw(name, w, g, m, v, deps=()):
    shape = w.shape
    cols = shape[-1]
    rows = math.prod(shape[:-1])
    tr = _row_tile(rows, cols, target=2 * 2**20)
    c1 = 1.0 - ADAM_B1 ** ADAM_STEP
    c2 = 1.0 - ADAM_B2 ** ADAM_STEP

    def body(w_ref, g_ref, m_ref, v_ref, d_ref, nm_ref, nv_ref):
        gv = g_ref[...]
        nm = ADAM_B1 * m_ref[...] + (1.0 - ADAM_B1) * gv
        nv = ADAM_B2 * v_ref[...] + (1.0 - ADAM_B2) * (gv * gv)
        d_ref[...] = -ADAM_LR * ((nm / c1) / (jnp.sqrt(nv / c2) + ADAM_EPS) + ADAM_WD * w_ref[...])
        nm_ref[...] = nm
        nv_ref[...] = nv

    blk = pl.BlockSpec((tr, cols), lambda i: (i, 0))
    flat = SDS((rows, cols), f32)
    outs = pl.pallas_call(
        _with_deps(body, 4, deps), name=name, grid=(rows // tr,), in_specs=[blk] * 4 + [_DEP_SPEC] * len(deps),
        out_specs=[blk] * 3, out_shape=[flat] * 3, compiler_params=_cp("parallel"),
    )(*[a.reshape(rows, cols) for a in (w, g, m, v)], *deps)
    return [o.reshape(shape) for o in outs]


def _me():
    return lax.axis_index("x"), lax.axis_index("y"), lax.axis_index("c")


def _other_chips(x, y):
    return [(1 - x, y), (x, 1 - y), (1 - x, 1 - y)]


def _pair_share_each(owners, bufs, sems, mine, act):
    x, y, c = _me()
    for o in range(2):
        slots = [(a, lyr) for a in range(len(bufs)) for lyr in range(DEPTH) if owners[a][lyr] == o]

        @pl.when((c == o) if mine else (c != o))
        def _(slots=slots):
            for a, lyr in slots:
                slot = bufs[a].at[lyr]
                act(_rcopy(slot, slot, sems[0].at[2 * a + lyr], sems[1].at[2 * a + lyr], (x, y, 1 - c)))


def pair_share_start(name, sums, owners, after):
    def body_fn(b_in, s_in, s_out):
        _pair_share_each(owners, b_in, s_out, True, lambda cp: cp.start())

    outs, sems, token = _split_call(name, body_fn, list(sums), [], [2 * len(sums)] * 2, after)
    return (outs, sems[0], sems[1], owners), token


def pair_share_wait(name, st, after):
    bufs, send, recv, owners = st

    def body_fn(b_in, s_in, s_out):
        _pair_share_each(owners, b_in, s_in, True, lambda cp: cp.wait_send())
        _pair_share_each(owners, b_in, s_in, False, lambda cp: cp.wait_recv())

    outs, _, _ = _split_call(name, body_fn, list(bufs), [send, recv], [], after)
    return outs


def allsum_small(name, v, deps=()):
    R = v.shape[0]

    def body(v_ref, o_ref, all_ref, send_sems, recv_sems, local_sem):
        x, y, c = _me()
        me, sibling = (x, y, c), (x, y, 1 - c)
        chips = _other_chips(x, y)

        def rows(px, py, pc):
            return all_ref.at[4 * px + 2 * py + pc]

        def copy(k, block, to, src=None):
            return pltpu.make_async_remote_copy(
                src_ref=rows(*block) if src is None else src, dst_ref=rows(*block),
                send_sem=send_sems.at[k], recv_sem=recv_sems.at[k], device_id=to, device_id_type=MESH)

        mine = pltpu.make_async_copy(v_ref, rows(*me), local_sem)
        mine.start()
        first = [copy(0, me, sibling, src=v_ref)]
        first += [copy(1 + j, me, (*chip, c), src=v_ref) for j, chip in enumerate(chips)]
        for cp in first:
            cp.start()
        passed = [copy(4 + j, (*chip, c), sibling) for j, chip in enumerate(chips)]
        for j, chip in enumerate(chips):
            copy(1 + j, (*chip, c), me).wait_recv()
            passed[j].start()
        copy(0, sibling, me).wait_recv()
        for j, chip in enumerate(chips):
            copy(4 + j, (*chip, 1 - c), me).wait_recv()
        for cp in first + passed:
            cp.wait_send()
        mine.wait()
        acc = all_ref[0]
        for d in range(1, 8):
            acc = acc + all_ref[d]
        o_ref[...] = acc

    return pl.pallas_call(
        _with_deps(body, 1, deps), name=name,
        in_specs=[pl.BlockSpec(memory_space=pltpu.VMEM)] + [_DEP_SPEC] * len(deps),
        out_specs=pl.BlockSpec(memory_space=pltpu.VMEM),
        out_shape=SDS((R, 128), f32),
        scratch_shapes=[pltpu.VMEM((8, R, 128), f32), pltpu.SemaphoreType.DMA((7,)), pltpu.SemaphoreType.DMA((7,)),
                        pltpu.SemaphoreType.DMA],
        compiler_params=pltpu.CompilerParams(vmem_limit_bytes=V7X_VMEM_LIMIT),
    )(v, *deps)


def _swap_half(r):
    return jnp.concatenate([-r[..., 32:], r[..., :32]], axis=-1)


def _unswap_add(p, qg):
    return p + jnp.concatenate([qg[..., 32:], -qg[..., :32]], axis=-1)


def _block_diag(pw):
    L = pw.shape[0]
    out = jnp.zeros((L, 256, 256), pw.dtype)
    for gi in range(4):
        out = out.at[:, 64 * gi:64 * gi + 64, 64 * gi:64 * gi + 64].set(pw[:, gi])
    return out


def _to_col_shards(w):
    *lead, K, N = w.shape
    nl = len(lead)
    return w.reshape(*lead, K, N_CHIPS, N // N_CHIPS).transpose(*range(nl), nl + 1, nl, nl + 2)


def _from_col_shards(w):
    *lead, C, K, n = w.shape
    nl = len(lead)
    return w.transpose(*range(nl), nl + 1, nl, nl + 2).reshape(*lead, K, C * n)


def _step_serial_comm(x, mem, positions, ln_g, ln_b, ffn1_w13, ffn1_w2, w_in, pool_w, pool_scale, q_norm_g, w_uq, kv_norm_g, w_ukv, w_out, mem_wq, mem_wkv, mem_wo, ffn2_w13, ffn2_w2, loss_target, m_ln_g, m_ln_b, m_ffn1_w13, m_ffn1_w2, m_w_in, m_pool_w, m_pool_scale, m_q_norm_g, m_w_uq, m_kv_norm_g, m_w_ukv, m_w_out, m_mem_wq, m_mem_wkv, m_mem_wo, m_ffn2_w13, m_ffn2_w2, v_ln_g, v_ln_b, v_ffn1_w13, v_ffn1_w2, v_w_in, v_pool_w, v_pool_scale, v_q_norm_g, v_w_uq, v_kv_norm_g, v_w_ukv, v_w_out, v_mem_wq, v_mem_wkv, v_mem_wo, v_ffn2_w13, v_ffn2_w2):
    L = DEPTH
    S = x.shape[1]
    qx, qy, _ = _me()
    chip = 2 * qx + qy

    big = [ffn1_w13, ffn1_w2, w_in, w_uq, w_ukv, w_out, mem_wq, mem_wkv, mem_wo, ffn2_w13, ffn2_w2]
    (g_f1w13, g_f1w2, g_win, g_wuq, g_wukv, g_wout, g_mwq, g_mwkv, g_mwo, g_f2w13, g_f2w2) = gather_weights(
        [w.astype(bf16) for w in big])
    f1w2 = g_f1w2.reshape(L, D_FF, D_MODEL)
    f2w2 = g_f2w2.reshape(L, D_FF, D_MODEL)
    win = g_win.reshape(L, D_MODEL, D_IN)
    win_ext = jnp.concatenate([win, _swap_half(win[..., D_IN - QK_ROPE:])], axis=-1)
    wuq = _from_col_shards(g_wuq).reshape(L, Q_LORA, MLA_HEADS, QK_NOPE + QK_ROPE)
    wq_ext = jnp.concatenate([wuq, _swap_half(wuq[..., QK_NOPE:])], axis=-1).transpose(0, 2, 1, 3)
    wukv = _from_col_shards(g_wukv)
    wout = g_wout.reshape(L, D_MODEL, D_MODEL)
    wout_pool, wout_mla = wout[:, :POOL_WIDTH], wout[:, POOL_WIDTH:]
    mwq = g_mwq.reshape(L, D_MODEL, D_MODEL)
    mwo = g_mwo.reshape(L, D_MODEL, D_MODEL)
    wbd = _block_diag(pool_w.astype(bf16))

    ln_pad = jnp.zeros((2, L, 4, N_CHIPS, D_MODEL // N_CHIPS), f32)
    ln_pad = lax.dynamic_update_slice(ln_pad, jnp.stack([ln_g, ln_b])[:, :, :, None, :], (0, 0, 0, chip, 0))
    ln_full = allsum_small("allsum_ln", ln_pad.reshape(-1, 128)) * 0.5
    ln_full = ln_full.reshape(2, L, 4, D_MODEL)
    lng, lnb = ln_full[0], ln_full[1]

    half = QK_ROPE // 2
    inv_freq = ROPE_BASE ** (-jnp.arange(half, dtype=f32) / half)
    ang = positions[0].astype(f32)[:, None] * inv_freq
    cos, sin = jnp.cos(ang), jnp.sin(ang)
    cs = jnp.concatenate([cos, cos, sin, sin], axis=-1)

    memb = mem[0].astype(bf16)
    xf = x[0]
    xb = xf.astype(bf16)
    vec = lambda a: a.reshape(1, -1)

    saved = []
    for l in range(L):
        sv = {}
        sv["x0b"] = xb
        gate, up, act = ffn_up(f"ffn1_up_{l}", xb, g_f1w13, l)
        z1, x1f, x1b = proj_res_ln(f"ffn1_down_{l}", [act], [f1w2], [l], xf, vec(lng[l, 0]), vec(lnb[l, 0]), 0.5)
        sv.update(gate1=gate, up1=up, act1=act, z1=z1, x1b=x1b)
        u, cq, ckv, cqn, ckvn, q, k, v = mix_pre(f"mix_pre_{l}", x1b, win_ext, wq_ext, wukv, l,
                                                   vec(q_norm_g[l]), vec(kv_norm_g[l]), cs)
        dpool, ypool = pool_fwd(f"pool_fwd_{l}", u, wbd[l], vec(pool_scale[l]))
        o, lse = mla_attn_fwd(f"mla_fwd_{l}", q, k, v)
        z2, x2f, x2b = proj_res_ln(f"mix_out_{l}", [ypool, o], [wout_pool, wout_mla], [l, l], x1f,
                                   vec(lng[l, 1]), vec(lnb[l, 1]), 1.0)
        sv.update(cq=cq, ckv=ckv, cqn=cqn, ckvn=ckvn, q=q, k=k, v=v, dpool=dpool, ypool=ypool, o=o, lse=lse, z2=z2, x2b=x2b)
        kvm = mm_nn_shard(f"mem_kv_{l}", memb, g_mwkv, l)
        cq_, co_, z3, x3f, x3b = cross_fwd(f"cross_fwd_{l}", x2b, x2f, mwq, mwo, l, kvm, vec(lng[l, 2]), vec(lnb[l, 2]))
        sv.update(kvm=kvm, crq=cq_, cro=co_, z3=z3, x3b=x3b)
        gate, up, act = ffn_up(f"ffn2_up_{l}", x3b, g_f2w13, l)
        z4, xf, xb = proj_res_ln(f"ffn2_down_{l}", [act], [f2w2], [l], x3f, vec(lng[l, 3]), vec(lnb[l, 3]), 0.5)
        sv.update(gate2=gate, up2=up, act2=act, z4=z4)
        saved.append(sv)

    dy, loss_blk = loss_grad("loss_grad", xf, loss_target[0])
    loss = lax.psum(loss_blk[0, 0], ("x", "y", "c"))

    G = dict(f1w13=None, f1w2=None, mwq=None, mwkv=None, mwo=None, f2w13=None, f2w2=None)
    small = {k_: [None] * L for k_ in ("win", "wuq", "wukv", "wout", "pool_w", "pool_scale", "gq", "gkv", "lng", "lnb")}
    for l in reversed(range(L)):
        sv = saved[l]
        dlg, dlb = [None] * 4, [None] * 4
        dzb, dres, dlg[3], dlb[3] = ln_bwd(f"ln4_bwd_{l}", dy, sv["z4"], vec(lng[l, 3]), 0.5)
        dh = ffn_bwd_da(f"ffn2_bwd_da_{l}", dzb, f2w2, l, sv["gate2"], sv["up2"])
        G["f2w2"] = mm_tn(f"ffn2_dw2_{l}", sv["act2"], dzb, "nat", l, G["f2w2"])
        G["f2w13"] = mm_tn(f"ffn2_dw13_{l}", sv["x3b"], dh, "shard", l, G["f2w13"])
        dy = ffn_dx(f"ffn2_dx_{l}", dh, g_f2w13, l, dres)
        dzb, dres, dlg[2], dlb[2] = ln_bwd(f"ln3_bwd_{l}", dy, sv["z3"], vec(lng[l, 2]), 1.0)
        dqc, dkvm = cross_bwd(f"cross_bwd_{l}", dzb, mwo, l, sv["crq"], sv["kvm"])
        G["mwo"] = mm_tn(f"cross_dwo_{l}", sv["cro"], dzb, "nat", l, G["mwo"])
        G["mwq"] = mm_tn(f"cross_dwq_{l}", sv["x2b"], dqc, "nat", l, G["mwq"])
        G["mwkv"] = mm_tn(f"cross_dwkv_{l}", memb, dkvm, "shard", l, G["mwkv"])
        dy = mm_nt_res(f"cross_dx_{l}", [dqc], [mwq], [l], dres, f32)
        dzb, dres, dlg[1], dlb[1] = ln_bwd(f"ln2_bwd_{l}", dy, sv["z2"], vec(lng[l, 1]), 1.0)
        dyp = mm_nt_res(f"mix_dpool_{l}", [dzb], [wout_pool], [l], None, bf16)
        do = mm_nt_res(f"mix_do_{l}", [dzb], [wout_mla], [l], None, bf16)
        dwo_p = mm_tn(f"mix_dwout_pool_{l}", sv["ypool"], dzb)
        dwo_m = mm_tn(f"mix_dwout_mla_{l}", sv["o"], dzb)
        small["wout"][l] = jnp.concatenate([dwo_p, dwo_m], axis=0)
        dq, dk, dv = mla_attn_bwd(f"mla_bwd_{l}", sv["q"], sv["k"], sv["v"], sv["o"], do, sv["lse"])
        dqe, dkv, dh_rest, dgq, dgkv = mix_post_bwd(f"mix_post_bwd_{l}", dq, dk, dv, wq_ext, wukv, l, sv["cq"], sv["ckv"],
                                                     vec(q_norm_g[l]), vec(kv_norm_g[l]), cs)
        du, dyw, dscale = pool_bwd(f"pool_bwd_{l}", dyp, sv["dpool"], wbd[l], vec(pool_scale[l]))
        dwq_e = mm_tn(f"mix_dwuq_{l}", sv["cqn"], dqe).reshape(Q_LORA, MLA_HEADS, 256)
        small["wuq"][l] = jnp.concatenate(
            [dwq_e[..., :QK_NOPE], _unswap_add(dwq_e[..., QK_NOPE:QK_NOPE + QK_ROPE], dwq_e[..., QK_NOPE + QK_ROPE:])],
            axis=-1).reshape(Q_LORA, MLA_HEADS * (QK_NOPE + QK_ROPE))
        small["wukv"][l] = mm_tn(f"mix_dwukv_{l}", sv["ckvn"], dkv)
        dwbd = mm_tn(f"pool_dw_{l}", sv["dpool"], dyw)
        small["pool_w"][l] = jnp.stack([dwbd[64 * gi:64 * gi + 64, 64 * gi:64 * gi + 64] for gi in range(4)])
        small["pool_scale"][l], small["gq"][l], small["gkv"][l] = dscale[0], dgq[0], dgkv[0]
        dh_ext = jnp.concatenate([du, dh_rest], axis=1)
        dwin_e = mm_tn(f"mix_dwin_{l}", sv["x1b"], dh_ext)
        small["win"][l] = jnp.concatenate(
            [dwin_e[:, :D_IN - QK_ROPE], _unswap_add(dwin_e[:, D_IN - QK_ROPE:D_IN], dwin_e[:, D_IN:])], axis=-1)
        dy = mm_nt_res(f"mix_dx_{l}", [dh_ext], [win_ext], [l], dres, f32)
        dzb, dres, dlg[0], dlb[0] = ln_bwd(f"ln1_bwd_{l}", dy, sv["z1"], vec(lng[l, 0]), 0.5)
        dh = ffn_bwd_da(f"ffn1_bwd_da_{l}", dzb, f1w2, l, sv["gate1"], sv["up1"])
        G["f1w2"] = mm_tn(f"ffn1_dw2_{l}", sv["act1"], dzb, "nat", l, G["f1w2"])
        G["f1w13"] = mm_tn(f"ffn1_dw13_{l}", sv["x0b"], dh, "shard", l, G["f1w13"])
        dy = ffn_dx(f"ffn1_dx_{l}", dh, g_f1w13, l, dres)
        small["lng"][l] = jnp.concatenate(dlg, axis=0)
        small["lnb"][l] = jnp.concatenate(dlb, axis=0)
    grad_x = dy[None]

    row_shards = lambda a, K: a.reshape(L, N_CHIPS, K // N_CHIPS, a.shape[-1])
    g_list = [G["f1w13"], row_shards(G["f1w2"], D_FF),
              jnp.stack(small["win"]).reshape(L, N_CHIPS, D_MODEL // N_CHIPS, D_IN),
              _to_col_shards(jnp.stack(small["wuq"])), _to_col_shards(jnp.stack(small["wukv"])),
              jnp.stack(small["wout"]).reshape(L, N_CHIPS, D_MODEL // N_CHIPS, D_MODEL),
              row_shards(G["mwq"], D_MODEL), G["mwkv"], row_shards(G["mwo"], D_MODEL),
              G["f2w13"], row_shards(G["f2w2"], D_FF)]
    big_grads = reduce_grads(g_list)

    rep = [jnp.stack(small["pool_w"]).reshape(-1), jnp.stack(small["pool_scale"]).reshape(-1),
           jnp.stack(small["gq"]).reshape(-1), jnp.stack(small["gkv"]).reshape(-1),
           jnp.stack(small["lng"]).reshape(-1), jnp.stack(small["lnb"]).reshape(-1)]
    sizes = [r.shape[0] for r in rep]
    packed = jnp.concatenate(rep)
    pad = (-packed.shape[0]) % 1024
    tot = allsum_small("allsum_small_grads", jnp.pad(packed, (0, pad)).reshape(-1, 128)).reshape(-1)
    offs = [0]
    for s_ in sizes:
        offs.append(offs[-1] + s_)
    parts = [tot[offs[i]:offs[i + 1]] for i in range(len(sizes))]
    g_pool_w = parts[0].reshape(pool_w.shape)
    g_pool_scale = parts[1].reshape(pool_scale.shape)
    g_gq = parts[2].reshape(q_norm_g.shape)
    g_gkv = parts[3].reshape(kv_norm_g.shape)
    shard_cols = lambda a: lax.dynamic_slice_in_dim(a.reshape(L, 4, D_MODEL), chip * (D_MODEL // N_CHIPS),
                                                    D_MODEL // N_CHIPS, axis=2)
    g_lng, g_lnb = shard_cols(parts[4]), shard_cols(parts[5])

    out_names = ("lng", "lnb", "f1w13", "f1w2", "win", "pool_w", "pool_scale", "gq", "wuq", "gkv", "wukv", "wout", "mwq",
                 "mwkv", "mwo", "f2w13", "f2w2")
    big = dict(lng=g_lng, lnb=g_lnb, pool_w=g_pool_w, pool_scale=g_pool_scale, gq=g_gq, gkv=g_gkv)
    late = ("f1w13", "f1w2")
    held = ("f2w13", "f2w2")
    ws = [ln_g, ln_b, ffn1_w13, ffn1_w2, w_in, pool_w, pool_scale, q_norm_g, w_uq, kv_norm_g, w_ukv, w_out, mem_wq,
          mem_wkv, mem_wo, ffn2_w13, ffn2_w2]
    ms = [m_ln_g, m_ln_b, m_ffn1_w13, m_ffn1_w2, m_w_in, m_pool_w, m_pool_scale, m_q_norm_g, m_w_uq, m_kv_norm_g, m_w_ukv,
          m_w_out, m_mem_wq, m_mem_wkv, m_mem_wo, m_ffn2_w13, m_ffn2_w2]
    vs = [v_ln_g, v_ln_b, v_ffn1_w13, v_ffn1_w2, v_w_in, v_pool_w, v_pool_scale, v_q_norm_g, v_w_uq, v_kv_norm_g, v_w_ukv,
          v_w_out, v_mem_wq, v_mem_wkv, v_mem_wo, v_ffn2_w13, v_ffn2_w2]
    res = {}

    def update(n, deps=()):
        a = out_names.index(n)
        res[a] = adamw(f"adamw_{a}", ws[a], big[n].reshape(ws[a].shape), ms[a], vs[a], deps)
        return res[a][0]

    small_done = tuple(update(n) for n in ("lng", "lnb", "pool_w", "pool_scale", "gq", "gkv"))
    big.update(zip(rest_names, pair_share_wait("pair_share_wait_a", share_a, small_done)))
    first_done = tuple(update(n) for n in rest_names if n not in held)
    sums_b = red_end("b0", st_cb, 0, [sums1[n] for n in late], first_done)
    share_b, tok_b = share_start("b", late, sums_b)
    held_done = tuple(update(n, (tok_b,)) for n in held)
    big.update(zip(late, pair_share_wait("pair_share_wait_b", share_b, held_done)))
    for n in late:
        update(n)
    order = range(len(out_names))
    grads = [big[n].reshape(w_.shape) for n, w_ in zip(out_names, ws)]
    return (loss, grad_x, *grads, *[res[a][0] for a in order], *[res[a][1] for a in order], *[res[a][2] for a in order])


_HBM_SPEC = pl.BlockSpec(memory_space=pltpu.HBM)
_SEM_SPEC = pl.BlockSpec(memory_space=pltpu.SEMAPHORE)
_ANY_SPEC = pl.BlockSpec(memory_space=pl.ANY)
_DATAFLOW = pltpu.SideEffectType.DATAFLOW_SIDE_EFFECTING


def _split_call(name, body_fn, bufs, sems_in, sems_out_sizes, after):
    nb, ni, no = len(bufs), len(sems_in), len(sems_out_sizes)
    afters = () if after is None else tuple(after) if isinstance(after, (tuple, list)) else (after,)

    def body(*refs):
        k = nb + ni + len(afters)
        body_fn(refs[:nb], refs[nb:nb + ni], refs[k:k + no])
        refs[-1][...] = jnp.zeros((8, 128), f32)

    outs = pl.pallas_call(
        body, name=name,
        in_specs=[_HBM_SPEC] * nb + [_SEM_SPEC] * ni + [_ANY_SPEC] * len(afters),
        out_specs=[_SEM_SPEC] * no + [_HBM_SPEC] * nb + [pl.BlockSpec(memory_space=pltpu.VMEM)],
        out_shape=[pltpu.SemaphoreType.DMA((s,)) for s in sems_out_sizes]
        + [pltpu.HBM(b.shape, b.dtype) for b in bufs] + [SDS((8, 128), f32)],
        input_output_aliases={i: no + i for i in range(nb)},
        compiler_params=pltpu.CompilerParams(has_side_effects=_DATAFLOW),
    )(*[pltpu.with_memory_space_constraint(b, pltpu.HBM) for b in bufs], *sems_in, *afters)
    return list(outs[no:no + nb]), list(outs[:no]), outs[-1]


def _rcopy(src, dst, ssem, rsem, to):
    return pltpu.make_async_remote_copy(src_ref=src, dst_ref=dst, send_sem=ssem, recv_sem=rsem, device_id=to,
                                        device_id_type=MESH)


def gather_start(name, groups, after):
    flat = [b for bufs, _ in groups for b in bufs]
    sizes = [3 * len(bufs) for bufs, _ in groups for _ in range(2)]

    def body_fn(b_in, s_in, s_out):
        x, y, c = _me()
        q = 2 * x + y
        chips = _other_chips(x, y)
        pos = 0
        for gi, (bufs, owner) in enumerate(groups):
            refs = b_in[pos:pos + len(bufs)]
            pos += len(bufs)

            @pl.when(c == owner)
            def _(refs=refs, send=s_out[2 * gi], recv=s_out[2 * gi + 1]):
                for a, r in enumerate(refs):
                    for k, (cx, cy) in enumerate(chips):
                        _rcopy(r.at[q], r.at[q], send.at[3 * a + k], recv.at[3 * a + k], (cx, cy, c)).start()

    outs, sems, token = _split_call(name, body_fn, flat, [], sizes, after)
    res, pos = [], 0
    for gi, (bufs, owner) in enumerate(groups):
        res.append((outs[pos:pos + len(bufs)], sems[2 * gi], sems[2 * gi + 1], owner))
        pos += len(bufs)
    return res, token


def gather_forward(name, grp, after):
    bufs, send, recv, owner = grp
    n3 = 3 * len(bufs)

    def body_fn(b_in, s_in, s_out):
        x, y, c = _me()
        q = 2 * x + y
        sibling = (x, y, 1 - c)
        chips = _other_chips(x, y)

        @pl.when(c == owner)
        def _():
            for a, r in enumerate(b_in):
                for k, (cx, cy) in enumerate(chips):
                    i = 3 * a + k
                    land = r.at[2 * cx + cy]
                    _rcopy(r.at[q], r.at[q], s_in[0].at[i], s_in[1].at[i], (cx, cy, c)).wait_send()
                    _rcopy(land, land, s_in[0].at[i], s_in[1].at[i], (cx, cy, c)).wait_recv()
                    _rcopy(land, land, s_out[0].at[i], s_out[1].at[i], sibling).start()

    outs, sems, token = _split_call(name, body_fn, bufs, [send, recv], [n3, n3], after)
    return (outs, sems[0], sems[1], owner), token


def gather_finish(name, grp, after):
    bufs, fsend, frecv, owner = grp

    def body_fn(b_in, s_in, s_out):
        x, y, c = _me()
        sibling = (x, y, 1 - c)
        chips = _other_chips(x, y)

        def each(wait):
            for a, r in enumerate(b_in):
                for k, (cx, cy) in enumerate(chips):
                    land = r.at[2 * cx + cy]
                    wait(_rcopy(land, land, s_in[0].at[3 * a + k], s_in[1].at[3 * a + k], sibling))

        @pl.when(c == owner)
        def _():
            each(lambda cp: cp.wait_send())

        @pl.when(c != owner)
        def _():
            each(lambda cp: cp.wait_recv())

    outs, _, _ = _split_call(name, body_fn, bufs, [fsend, frecv], [], after)
    return outs


def _by_owner(owners):
    return [[a for a, o_ in enumerate(owners) if o_ == o] for o in range(2)]


def pair_send_start(name, gs, owners, after):
    n = len(gs)
    lands = [lax.empty(g.shape, g.dtype) for g in gs]

    def body_fn(b_in, s_in, s_out):
        x, y, c = _me()
        for o, idx in enumerate(_by_owner(owners)):
            @pl.when(c == 1 - o)
            def _(o=o, idx=idx):
                for a in idx:
                    _rcopy(b_in[a], b_in[n + a], s_out[0].at[a], s_out[1].at[a], (x, y, o)).start()

    outs, sems, token = _split_call(name, body_fn, list(gs) + lands, [], [n, n], after)
    return (outs[:n], outs[n:], sems[0], sems[1], owners), token


def pair_send_wait(name, st, after):
    gs, lands, send, recv, owners = st
    n = len(gs)

    def body_fn(b_in, s_in, s_out):
        x, y, c = _me()
        for o, idx in enumerate(_by_owner(owners)):
            @pl.when(c == 1 - o)
            def _(o=o, idx=idx):
                for a in idx:
                    _rcopy(b_in[a], b_in[n + a], s_in[0].at[a], s_in[1].at[a], (x, y, o)).wait_send()

            @pl.when(c == o)
            def _(o=o, idx=idx):
                for a in idx:
                    _rcopy(b_in[a], b_in[n + a], s_in[0].at[a], s_in[1].at[a], (x, y, 1 - o)).wait_recv()

    outs, _, _ = _split_call(name, body_fn, list(gs) + list(lands), [send, recv], [], after)
    return outs[:n], outs[n:]


def chip_exchange_start(name, psums, owners, after):
    n = len(psums)
    lands = [lax.empty((3,) + p.shape[1:], p.dtype) for p in psums]

    def body_fn(b_in, s_in, s_out):
        x, y, c = _me()
        chips = _other_chips(x, y)
        for o, idx in enumerate(_by_owner(owners)):
            @pl.when(c == o)
            def _(idx=idx):
                for a in idx:
                    for k, (cx, cy) in enumerate(chips):
                        _rcopy(b_in[a].at[2 * cx + cy], b_in[n + a].at[k], s_out[0].at[3 * a + k],
                               s_out[1].at[3 * a + k], (cx, cy, c)).start()

    outs, sems, token = _split_call(name, body_fn, list(psums) + lands, [], [3 * n, 3 * n], after)
    return (outs[:n], outs[n:], sems[0], sems[1], owners), token


def chip_exchange_wait(name, st, after):
    psums, lands, send, recv, owners = st
    n = len(psums)

    def body_fn(b_in, s_in, s_out):
        x, y, c = _me()
        chips = _other_chips(x, y)
        for o, idx in enumerate(_by_owner(owners)):
            @pl.when(c == o)
            def _(idx=idx):
                for a in idx:
                    for k, (cx, cy) in enumerate(chips):
                        cp = _rcopy(b_in[a].at[2 * cx + cy], b_in[n + a].at[k], s_in[0].at[3 * a + k],
                                    s_in[1].at[3 * a + k], (cx, cy, c))
                        cp.wait_send()
                        cp.wait_recv()

    outs, _, _ = _split_call(name, body_fn, list(psums) + list(lands), [send, recv], [], after)
    return outs[:n], outs[n:]


def pair_sum(name, g, recv, flag):
    shape = g.shape
    cols = shape[-1]
    rows = math.prod(shape[:-1])
    tr = _row_tile(rows, cols, target=4 * 2**20)

    def body(f_ref, g_ref, r_ref, o_ref):
        o_ref[...] = (g_ref[...] + r_ref[...]).astype(bf16)

    blk = pl.BlockSpec((tr, cols), lambda i, f_ref: (i * f_ref[0], 0))
    out = pl.pallas_call(
        body, name=name,
        grid_spec=pltpu.PrefetchScalarGridSpec(num_scalar_prefetch=1, grid=(rows // tr,), in_specs=[blk, blk],
                                               out_specs=blk),
        out_shape=SDS((rows, cols), bf16), compiler_params=_cp("arbitrary"),
    )(flag, g.reshape(rows, cols), recv.reshape(rows, cols))
    return out.reshape(shape)


def chip_sum(name, psum, recv, qf_arr, layer, prev):
    shard = psum.shape[1:]
    cols = shard[-1]
    rows = math.prod(shard[:-1])
    tr = _row_tile(rows, cols, target=4 * 2**20)

    def body(qf_ref, p_ref, r_ref, *rest):
        rest[-1][0] = ((p_ref[0].astype(f32) + r_ref[0].astype(f32)) + r_ref[1].astype(f32)) + r_ref[2].astype(f32)

    in_specs = [pl.BlockSpec((1, tr, cols), lambda i, qf: (qf[0], i * qf[1], 0)),
                pl.BlockSpec((3, tr, cols), lambda i, qf: (0, i * qf[1], 0))]
    args = [qf_arr, psum.reshape(N_CHIPS, rows, cols), recv.reshape(3, rows, cols)]
    aliases = {}
    if prev is not None:
        in_specs.append(pl.BlockSpec(memory_space=pl.ANY))
        args.append(prev.reshape(DEPTH, rows, cols))
        aliases = {3: 0}
    out = pl.pallas_call(
        body, name=name,
        grid_spec=pltpu.PrefetchScalarGridSpec(
            num_scalar_prefetch=1, grid=(rows // tr,), in_specs=in_specs,
            out_specs=pl.BlockSpec((1, tr, cols), lambda i, qf: (layer, i * qf[1], 0))),
        out_shape=SDS((DEPTH, rows, cols), f32), input_output_aliases=aliases, compiler_params=_cp("arbitrary"),
    )(*args)
    return out.reshape((DEPTH,) + shard)


W_NAMES = ("f1w13", "f1w2", "win", "wuq", "wukv", "wout", "mwq", "mwkv", "mwo", "f2w13", "f2w2")
MIX_NAMES = ("win", "wuq", "wukv")
MID_NAMES = ("wout", "mwq", "mwkv", "mwo")
FFN2_NAMES = ("f2w13", "f2w2")
REDUCER = (dict(f1w13=0, f1w2=1, f2w13=0, win=0, wuq=0, wukv=0, f2w2=1, mwkv=1, wout=1, mwq=1, mwo=1),
           dict(f1w13=0, f2w2=0, mwkv=0, wout=0, f2w13=1, f1w2=1, mwq=1, mwo=1, win=1, wuq=1, wukv=1))


def kernel(x, mem, positions, ln_g, ln_b, ffn1_w13, ffn1_w2, w_in, pool_w, pool_scale, q_norm_g, w_uq, kv_norm_g, w_ukv, w_out, mem_wq, mem_wkv, mem_wo, ffn2_w13, ffn2_w2, loss_target, m_ln_g, m_ln_b, m_ffn1_w13, m_ffn1_w2, m_w_in, m_pool_w, m_pool_scale, m_q_norm_g, m_w_uq, m_kv_norm_g, m_w_ukv, m_w_out, m_mem_wq, m_mem_wkv, m_mem_wo, m_ffn2_w13, m_ffn2_w2, v_ln_g, v_ln_b, v_ffn1_w13, v_ffn1_w2, v_w_in, v_pool_w, v_pool_scale, v_q_norm_g, v_w_uq, v_kv_norm_g, v_w_ukv, v_w_out, v_mem_wq, v_mem_wkv, v_mem_wo, v_ffn2_w13, v_ffn2_w2):
    L = DEPTH
    qx, qy, _ = _me()
    chip = 2 * qx + qy
    vec = lambda a: a.reshape(1, -1)

    shards = dict(zip(W_NAMES, (ffn1_w13, ffn1_w2, w_in, w_uq, w_ukv, w_out, mem_wq, mem_wkv, mem_wo, ffn2_w13, ffn2_w2)))

    def place(sh, slot):
        return lax.dynamic_update_slice(lax.empty((N_CHIPS,) + sh.shape, bf16), sh.astype(bf16)[None],
                                        (slot,) + (0,) * sh.ndim)

    first = ("f1w13", "f1w2")
    bufs = [dict(), dict()]
    for n in first:
        bufs[0][n] = place(shards[n][0], chip)
    gw = [dict(), dict()]
    (g0,), tok = gather_start("gather_a_start", [([bufs[0][n] for n in first], 0)], None)
    chip_then = chip + tok[0, 0].astype(jnp.int32)
    for l in range(L):
        for n in W_NAMES:
            if n not in bufs[l]:
                bufs[l][n] = place(shards[n][l], chip_then)
    others = tuple(bufs[l][n] for l in range(L) for n in W_NAMES if (l, n) not in ((0, first[0]), (0, first[1])))
    g0, tok = gather_forward("gather_a_forward", g0, others)

    ln_pad = jnp.zeros((2, L, 4, N_CHIPS, D_MODEL // N_CHIPS), f32)
    ln_pad = lax.dynamic_update_slice(ln_pad, jnp.stack([ln_g, ln_b])[:, :, :, None, :], (0, 0, 0, chip, 0))
    ln_sum = allsum_small("allsum_ln", ln_pad.reshape(-1, 128), (tok,))
    ln_full = (ln_sum * 0.5).reshape(2, L, 4, D_MODEL)
    lng, lnb = ln_full[0], ln_full[1]

    gw[0]["f1w13"], gw[0]["f1w2"] = gather_finish("gather_a_finish", g0, ln_sum)
    (g_mix, g_mid, g_ffn2, g_l1), tok_b = gather_start(
        "gather_b_start",
        [([bufs[0][n] for n in MIX_NAMES], 0), ([bufs[0][n] for n in MID_NAMES], 0), ([bufs[0][n] for n in FFN2_NAMES], 0),
         ([bufs[1][n] for n in W_NAMES], 1)], ln_sum)

    half = QK_ROPE // 2
    inv_freq = ROPE_BASE ** (-jnp.arange(half, dtype=f32) / half)
    ang = positions[0].astype(f32)[:, None] * inv_freq
    cos, sin = jnp.cos(ang), jnp.sin(ang)
    cs = jnp.concatenate([cos, cos, sin, sin], axis=-1)

    memb = mem[0].astype(bf16)
    xf = x[0]
    xb = xf.astype(bf16)
    dep = (tok_b,)

    saved, W = [], [None, None]
    for l in range(L):
        sv = {}
        if l == 1:
            gl1 = gather_finish("gather_l1_finish", g_l1, xb)
            gw[1] = dict(zip(W_NAMES, gl1))
        sv["x0b"] = xb
        f1w13 = gw[l]["f1w13"][None]
        gate, up, act = ffn_up(f"ffn1_up_{l}", xb, f1w13, 0, dep)
        dep = ()
        if l == 0:
            g_mix, _ = gather_forward("gather_mix_forward", g_mix, act)
        z1, x1f, x1b = proj_res_ln(f"ffn1_down_{l}", [act], [gw[l]["f1w2"].reshape(1, D_FF, D_MODEL)], [0], xf,
                                   vec(lng[l, 0]), vec(lnb[l, 0]), 0.5)
        sv.update(gate1=gate, up1=up, act1=act, z1=z1, x1b=x1b)
        if l == 0:
            gw[0].update(zip(MIX_NAMES, gather_finish("gather_mix_finish", g_mix, x1b)))
            g_mid, _ = gather_forward("gather_mid_forward", g_mid, x1b)
        win = gw[l]["win"].reshape(D_MODEL, D_IN)
        win_ext = jnp.concatenate([win, _swap_half(win[:, D_IN - QK_ROPE:])], axis=-1)[None]
        wuq = _from_col_shards(gw[l]["wuq"]).reshape(Q_LORA, MLA_HEADS, QK_NOPE + QK_ROPE)
        wq_ext = jnp.concatenate([wuq, _swap_half(wuq[..., QK_NOPE:])], axis=-1).transpose(1, 0, 2)[None]
        wukv = _from_col_shards(gw[l]["wukv"])[None]
        wbd = _block_diag(pool_w[l][None].astype(bf16))[0]
        u, cq, ckv, cqn, ckvn, q, k, v = mix_pre(f"mix_pre_{l}", x1b, win_ext, wq_ext, wukv, 0,
                                                   vec(q_norm_g[l]), vec(kv_norm_g[l]), cs)
        dpool, ypool = pool_fwd(f"pool_fwd_{l}", u, wbd, vec(pool_scale[l]))
        o, lse = mla_attn_fwd(f"mla_fwd_{l}", q, k, v)
        if l == 0:
            gw[0].update(zip(MID_NAMES, gather_finish("gather_mid_finish", g_mid, o)))
            g_ffn2, tok_f = gather_forward("gather_ffn2_forward", g_ffn2, o)
            g_l1, tok_l = gather_forward("gather_l1_forward", g_l1, o)
            dep = (tok_f, tok_l)
        wout = gw[l]["wout"].reshape(D_MODEL, D_MODEL)
        wout_pool, wout_mla = wout[None, :POOL_WIDTH], wout[None, POOL_WIDTH:]
        mwq = gw[l]["mwq"].reshape(1, D_MODEL, D_MODEL)
        mwo = gw[l]["mwo"].reshape(1, D_MODEL, D_MODEL)
        mwkv = gw[l]["mwkv"][None]
        z2, x2f, x2b = proj_res_ln(f"mix_out_{l}", [ypool, o], [wout_pool, wout_mla], [0, 0], x1f,
                                   vec(lng[l, 1]), vec(lnb[l, 1]), 1.0, dep)
        dep = ()
        sv.update(cq=cq, ckv=ckv, cqn=cqn, ckvn=ckvn, q=q, k=k, v=v, dpool=dpool, ypool=ypool, o=o, lse=lse, z2=z2, x2b=x2b)
        kvm = mm_nn_shard(f"mem_kv_{l}", memb, mwkv, 0)
        cq_, co_, z3, x3f, x3b = cross_fwd(f"cross_fwd_{l}", x2b, x2f, mwq, mwo, 0, kvm, vec(lng[l, 2]), vec(lnb[l, 2]))
        sv.update(kvm=kvm, crq=cq_, cro=co_, z3=z3, x3b=x3b)
        if l == 0:
            gw[0].update(zip(FFN2_NAMES, gather_finish("gather_ffn2_finish", g_ffn2, x3b)))
        f2w13 = gw[l]["f2w13"][None]
        f2w2 = gw[l]["f2w2"].reshape(1, D_FF, D_MODEL)
        gate, up, act = ffn_up(f"ffn2_up_{l}", x3b, f2w13, 0)
        z4, xf, xb = proj_res_ln(f"ffn2_down_{l}", [act], [f2w2], [0], x3f, vec(lng[l, 3]), vec(lnb[l, 3]), 0.5)
        sv.update(gate2=gate, up2=up, act2=act, z4=z4)
        W[l] = dict(f1w13=f1w13, f1w2=gw[l]["f1w2"].reshape(1, D_FF, D_MODEL), win_ext=win_ext, wq_ext=wq_ext, wukv=wukv,
                    wbd=wbd, wout_pool=wout_pool, wout_mla=wout_mla, mwq=mwq, mwo=mwo, f2w13=f2w13, f2w2=f2w2)
        saved.append(sv)

    dln = {}
    dzb, dres, *dln[L - 1, 3], loss_blk = loss_grad("loss_grad", xf, loss_target[0],
                                                   (saved[L - 1]["z4"], vec(lng[L - 1, 3]), 0.5))
    loss = lax.psum(loss_blk[0, 0], ("x", "y", "c"))

    row_shards = lambda a: a.reshape(N_CHIPS, a.shape[0] // N_CHIPS, a.shape[1])
    small = {k_: [None] * L for k_ in ("pool_w", "pool_scale", "gq", "gkv", "lng", "lnb")}
    rest_names = [n for n in W_NAMES if n not in ("f1w13", "f1w2")]
    core = lax.axis_index("c")
    flags = [jnp.reshape(core == o, (1,)).astype(jnp.int32) for o in range(2)]
    qfs = [jnp.stack([chip, (core == o).astype(jnp.int32)]).astype(jnp.int32) for o in range(2)]

    def red_begin(tag, names, gs, layer):
        owners = [REDUCER[layer][n] for n in names]
        st, tok_ = pair_send_start(f"pair_send_start_{tag}", gs, owners, None)
        return (st, owners), tok_

    def red_mid(tag, sto, after):
        st, owners = sto
        gs_, lands_ = pair_send_wait(f"pair_send_wait_{tag}", st, after)
        ps = [pair_sum(f"pair_sum_{tag}_{a}", g_, r_, flags[o]) for a, (g_, r_, o) in enumerate(zip(gs_, lands_, owners))]
        st, tok_ = chip_exchange_start(f"chip_exchange_start_{tag}", ps, owners, None)
        return (st, owners), tok_

    def red_end(tag, sto, layer, prevs, after):
        st, owners = sto
        ps, lands_ = chip_exchange_wait(f"chip_exchange_wait_{tag}", st, after)
        return [chip_sum(f"chip_sum_{tag}_{a}", p_, r_, qfs[o], layer, s_)
                for a, (p_, r_, s_, o) in enumerate(zip(ps, lands_, prevs, owners))]

    def share_start(tag, names, sums_):
        return pair_share_start(f"pair_share_start_{tag}", sums_, [(REDUCER[0][n], REDUCER[1][n]) for n in names], None)

    st_p1 = st_c1 = st_pa = st_ca = None
    for l in reversed(range(L)):
        sv, w = saved[l], W[l]
        g = {}
        dh = ffn_bwd_da(f"ffn2_bwd_da_{l}", dzb, w["f2w2"], 0, sv["gate2"], sv["up2"], dep)
        dep = ()
        g["f2w2"] = row_shards(mm_tn(f"ffn2_dw2_{l}", sv["act2"], dzb))
        g["f2w13"] = mm_tn(f"ffn2_dw13_{l}", sv["x3b"], dh, True)
        dzb, dres, *dln[l, 2] = ffn_dx(f"ffn2_dx_{l}", dh, w["f2w13"], 0, dres, (sv["z3"], vec(lng[l, 2]), 1.0))
        if l == 0:
            st_c1, tok = red_mid("l1", st_p1, dzb)
            dep = (tok, g["f2w2"], g["f2w13"])
        dqc, dkvm = cross_bwd(f"cross_bwd_{l}", dzb, w["mwo"], 0, sv["crq"], sv["kvm"], dep)
        dep = ()
        g["mwo"] = row_shards(mm_tn(f"cross_dwo_{l}", sv["cro"], dzb))
        g["mwq"] = row_shards(mm_tn(f"cross_dwq_{l}", sv["x2b"], dqc))
        g["mwkv"] = mm_tn(f"cross_dwkv_{l}", memb, dkvm, True)
        dzb, dres, *dln[l, 1] = mm_nt_res(f"cross_dx_{l}", [dqc], [w["mwq"]], [0], dres, f32,
                                          (sv["z2"], vec(lng[l, 1]), 1.0))
        dyp = mm_nt_res(f"mix_dpool_{l}", [dzb], [w["wout_pool"]], [0], None, bf16)
        do = mm_nt_res(f"mix_do_{l}", [dzb], [w["wout_mla"]], [0], None, bf16)
        dwo_p = mm_tn(f"mix_dwout_pool_{l}", sv["ypool"], dzb)
        dwo_m = mm_tn(f"mix_dwout_mla_{l}", sv["o"], dzb)
        g["wout"] = row_shards(jnp.concatenate([dwo_p, dwo_m], axis=0))
        dq, dk, dv = mla_attn_bwd(f"mla_bwd_{l}", sv["q"], sv["k"], sv["v"], sv["o"], do, sv["lse"])
        dqe, dkv, dh_rest, dgq, dgkv = mix_post_bwd(f"mix_post_bwd_{l}", dq, dk, dv, w["wq_ext"], w["wukv"], 0, sv["cq"],
                                                     sv["ckv"], vec(q_norm_g[l]), vec(kv_norm_g[l]), cs)
        du, dyw, dscale = pool_bwd(f"pool_bwd_{l}", dyp, sv["dpool"], w["wbd"], vec(pool_scale[l]))
        dwq_e = mm_tn(f"mix_dwuq_{l}", sv["cqn"], dqe).reshape(Q_LORA, MLA_HEADS, 256)
        g["wuq"] = _to_col_shards(jnp.concatenate(
            [dwq_e[..., :QK_NOPE], _unswap_add(dwq_e[..., QK_NOPE:QK_NOPE + QK_ROPE], dwq_e[..., QK_NOPE + QK_ROPE:])],
            axis=-1).reshape(Q_LORA, MLA_HEADS * (QK_NOPE + QK_ROPE)))
        g["wukv"] = _to_col_shards(mm_tn(f"mix_dwukv_{l}", sv["ckvn"], dkv))
        dwbd = mm_tn(f"pool_dw_{l}", sv["dpool"], dyw)
        small["pool_w"][l] = jnp.stack([dwbd[64 * gi:64 * gi + 64, 64 * gi:64 * gi + 64] for gi in range(4)])
        small["pool_scale"][l], small["gq"][l], small["gkv"][l] = dscale[0], dgq[0], dgkv[0]
        dh_ext = jnp.concatenate([du, dh_rest], axis=1)
        dwin_e = mm_tn(f"mix_dwin_{l}", sv["x1b"], dh_ext)
        g["win"] = row_shards(jnp.concatenate(
            [dwin_e[:, :D_IN - QK_ROPE], _unswap_add(dwin_e[:, D_IN - QK_ROPE:D_IN], dwin_e[:, D_IN:])], axis=-1))
        dzb, dres, *dln[l, 0] = mm_nt_res(f"mix_dx_{l}", [dh_ext], [w["win_ext"]], [0], dres, f32,
                                          (sv["z1"], vec(lng[l, 0]), 0.5))
        if l == 0:
            st_pa, tok = red_begin("a0", rest_names, [g[n] for n in rest_names], 0)
            dep = (tok,)
        dh = ffn_bwd_da(f"ffn1_bwd_da_{l}", dzb, w["f1w2"], 0, sv["gate1"], sv["up1"], dep)
        dep = ()
        if l == 0:
            grad_x = ffn_dx(f"ffn1_dx_{l}", dh, w["f1w13"], 0, dres)[None]
            st_ca, tok = red_mid("a0", st_pa, grad_x)
            dep = (tok,)
        else:
            below = ffn_dx(f"ffn1_dx_{l}", dh, w["f1w13"], 0, dres, (saved[l - 1]["z4"], vec(lng[l - 1, 3]), 0.5))
            dln[l - 1, 3] = below[2:]
        g["f1w2"] = row_shards(mm_tn(f"ffn1_dw2_{l}", sv["act1"], dzb, False, dep))
        g["f1w13"] = mm_tn(f"ffn1_dw13_{l}", sv["x0b"], dh, True, dep)
        dep = ()
        if l > 0:
            dzb, dres = below[:2]
        if l == 1:
            st_p1, tok = red_begin("l1", W_NAMES, [g[n] for n in W_NAMES], 1)
            dep = (tok,)
    for l in range(L):
        small["lng"][l] = jnp.concatenate([dln[l, k][0] for k in range(4)], axis=0)
        small["lnb"][l] = jnp.concatenate([dln[l, k][1] for k in range(4)], axis=0)

    st_pb, _ = red_begin("b0", ("f1w13", "f1w2"), [g["f1w13"], g["f1w2"]], 0)
    sums1 = dict(zip(W_NAMES, red_end("l1", st_c1, 1, [None] * len(W_NAMES), g["f1w13"])))
    st_cb, _ = red_mid("b0", st_pb, sums1["f1w13"])
    sums0 = red_end("a0", st_ca, 0, [sums1[n] for n in rest_names], sums1["f1w2"])
    share_a, tok_a = share_start("a", rest_names, sums0)

    rep = [jnp.stack(small["pool_w"]).reshape(-1), jnp.stack(small["pool_scale"]).reshape(-1),
           jnp.stack(small["gq"]).reshape(-1), jnp.stack(small["gkv"]).reshape(-1),
           jnp.stack(small["lng"]).reshape(-1), jnp.stack(small["lnb"]).reshape(-1)]
    sizes = [r.shape[0] for r in rep]
    packed = jnp.concatenate(rep)
    pad = (-packed.shape[0]) % 1024
    tot = allsum_small("allsum_small_grads", jnp.pad(packed, (0, pad)).reshape(-1, 128), (tok_a,)).reshape(-1)
    offs = [0]
    for s_ in sizes:
        offs.append(offs[-1] + s_)
    parts = [tot[offs[i]:offs[i + 1]] for i in range(len(sizes))]
    g_pool_w = parts[0].reshape(pool_w.shape)
    g_pool_scale = parts[1].reshape(pool_scale.shape)
    g_gq = parts[2].reshape(q_norm_g.shape)
    g_gkv = parts[3].reshape(kv_norm_g.shape)
    shard_cols = lambda a: lax.dynamic_slice_in_dim(a.reshape(L, 4, D_MODEL), chip * (D_MODEL // N_CHIPS),
                                                    D_MODEL // N_CHIPS, axis=2)
    g_lng, g_lnb = shard_cols(parts[4]), shard_cols(parts[5])

    out_names = ("lng", "lnb", "f1w13", "f1w2", "win", "pool_w", "pool_scale", "gq", "wuq", "gkv", "wukv", "wout", "mwq",
                 "mwkv", "mwo", "f2w13", "f2w2")
    big = dict(lng=g_lng, lnb=g_lnb, pool_w=g_pool_w, pool_scale=g_pool_scale, gq=g_gq, gkv=g_gkv)
    late = ("f1w13", "f1w2")
    held = ("f2w13", "f2w2")
    ws = [ln_g, ln_b, ffn1_w13, ffn1_w2, w_in, pool_w, pool_scale, q_norm_g, w_uq, kv_norm_g, w_ukv, w_out, mem_wq,
          mem_wkv, mem_wo, ffn2_w13, ffn2_w2]
    ms = [m_ln_g, m_ln_b, m_ffn1_w13, m_ffn1_w2, m_w_in, m_pool_w, m_pool_scale, m_q_norm_g, m_w_uq, m_kv_norm_g, m_w_ukv,
          m_w_out, m_mem_wq, m_mem_wkv, m_mem_wo, m_ffn2_w13, m_ffn2_w2]
    vs = [v_ln_g, v_ln_b, v_ffn1_w13, v_ffn1_w2, v_w_in, v_pool_w, v_pool_scale, v_q_norm_g, v_w_uq, v_kv_norm_g, v_w_ukv,
          v_w_out, v_mem_wq, v_mem_wkv, v_mem_wo, v_ffn2_w13, v_ffn2_w2]
    res = {}

    def update(n, deps=()):
        a = out_names.index(n)
        res[a] = adamw(f"adamw_{a}", ws[a], big[n].reshape(ws[a].shape), ms[a], vs[a], deps)
        return res[a][0]

    small_done = tuple(update(n) for n in ("lng", "lnb", "pool_w", "pool_scale", "gq", "gkv"))
    big.update(zip(rest_names, pair_share_wait("pair_share_wait_a", share_a, small_done)))
    first_done = tuple(update(n) for n in rest_names if n not in held)
    sums_b = red_end("b0", st_cb, 0, [sums1[n] for n in late], first_done)
    share_b, tok_b = share_start("b", late, sums_b)
    held_done = tuple(update(n, (tok_b,)) for n in held)
    big.update(zip(late, pair_share_wait("pair_share_wait_b", share_b, held_done)))
    for n in late:
        update(n)
    order = range(len(out_names))
    grads = [big[n].reshape(w_.shape) for n, w_ in zip(out_names, ws)]
    return (loss, grad_x, *grads, *[res[a][0] for a in order], *[res[a][1] for a in order], *[res[a][2] for a in order])
```

```python
import functools
import math

import jax
import jax.numpy as jnp
from jax import lax
from jax.experimental import pallas as pl
from jax.experimental.pallas import tpu as pltpu

f32 = jnp.float32
bf16 = jnp.bfloat16
SDS = jax.ShapeDtypeStruct
MESH = pl.DeviceIdType.MESH

D_MODEL = 1024
DEPTH = 2
N_MEM = 256
MEM_HEADS = 4
MEM_HEAD_DIM = D_MODEL // MEM_HEADS
POOL_WINDOWS = (2, 4, 8, 16)
POOL_WIDTH = 256
POOL_GROUP = 64
QK_NOPE = 128
QK_ROPE = 64
V_HEAD = 128
MLA_HEADS = 6
Q_LORA = 256
KV_LORA = 128
ROPE_BASE = 10000.0
D_FF = 2816
D_IN = POOL_WIDTH + Q_LORA + KV_LORA + QK_ROPE
ALPHA = (2 * DEPTH) ** 0.25
LN_EPS = 1e-5
RMS_EPS = 1e-6
NEG_INF = -1e30
MLA_SCALE = (QK_NOPE + QK_ROPE) ** -0.5
MEM_SCALE = MEM_HEAD_DIM ** -0.5
ADAM_LR = 0.001
ADAM_B1 = 0.9
ADAM_B2 = 0.999
ADAM_EPS = 1e-08
ADAM_WD = 0.01
ADAM_STEP = 10

N_CHIPS = 4
V7X_VMEM_LIMIT = 56 * 2**20
HALO = 16

_NT = (((1,), (1,)), ((), ()))
_TN = (((0,), (0,)), ((), ()))


def _dot(a, b):
    return jnp.dot(a, b, preferred_element_type=f32)


def _dot_nt(a, b):
    return lax.dot_general(a, b, _NT, preferred_element_type=f32)


def _dot_tn(a, b):
    return lax.dot_general(a, b, _TN, preferred_element_type=f32)


def _cp(*sem):
    return pltpu.CompilerParams(dimension_semantics=sem if sem else None, vmem_limit_bytes=V7X_VMEM_LIMIT)


_DEP_SPEC = pl.BlockSpec(memory_space=pl.ANY)


def _with_deps(body, n_in, deps):
    nd = len(deps)
    if not nd:
        return body

    def wrapped(*refs):
        return body(*refs[:n_in], *refs[n_in + nd:])

    return wrapped


def _tile(n, t):
    t = min(n, t)
    assert n % t == 0, (n, t)
    return t


def _row_tile(rows, cols, itemsize=4, target=2 * 2**20):
    best = None
    for t in range(16, rows + 1, 16):
        if rows % t == 0 and t * cols * itemsize <= target:
            best = t
    return best if best is not None else rows


def ffn_up(name, xb, w13, l, deps=()):
    S = xb.shape[0]
    ns = w13.shape[3]
    tm = _tile(S, 512)

    def body(x_ref, wg_ref, wu_ref, g_ref, u_ref, a_ref):
        x = x_ref[...]
        g = _dot(x, wg_ref[0, 0])
        u = _dot(x, wu_ref[0, 0])
        a = g * jax.nn.sigmoid(g) * u
        g_ref[...] = g.astype(bf16)
        u_ref[...] = u.astype(bf16)
        a_ref[...] = a.astype(bf16)

    out = SDS((S, 2 * ns), bf16)
    return pl.pallas_call(
        _with_deps(body, 3, deps), name=name, grid=(2, S // tm),
        in_specs=[pl.BlockSpec((tm, D_MODEL), lambda j, i: (i, 0)),
                  pl.BlockSpec((1, 1, D_MODEL, ns), lambda j, i: (l, j, 0, 0)),
                  pl.BlockSpec((1, 1, D_MODEL, ns), lambda j, i: (l, j + 2, 0, 0))] + [_DEP_SPEC] * len(deps),
        out_specs=[pl.BlockSpec((tm, ns), lambda j, i: (i, j))] * 3,
        out_shape=[out, out, out],
        compiler_params=_cp("parallel", "parallel"),
    )(xb, w13, w13, *deps)


def proj_res_ln(name, parts, ws, wl, x, g, b, rscale, deps=()):
    S = x.shape[0]
    tm = _tile(S, 512)
    n = len(parts)

    def body(*refs):
        p_refs, w_refs = refs[:n], refs[n:2 * n]
        x_ref, g_ref, b_ref, z_ref, y_ref, yb_ref = refs[2 * n:]
        acc = _dot(p_refs[0][...], w_refs[0][0])
        for k in range(1, n):
            acc = acc + _dot(p_refs[k][...], w_refs[k][0])
        if rscale != 1.0:
            acc = rscale * acc
        z = ALPHA * x_ref[...] + acc
        mu = jnp.mean(z, axis=-1, keepdims=True)
        zc = z - mu
        var = jnp.mean(zc * zc, axis=-1, keepdims=True)
        y = zc * lax.rsqrt(var + LN_EPS) * g_ref[...] + b_ref[...]
        z_ref[...] = z
        y_ref[...] = y
        yb_ref[...] = y.astype(bf16)

    row = lambda i: (i, 0)
    in_specs = [pl.BlockSpec((tm, p.shape[1]), row) for p in parts]
    in_specs += [pl.BlockSpec((1,) + w.shape[1:], functools.partial(lambda li, i: (li, 0, 0), li)) for w, li in zip(ws, wl)]
    in_specs += [pl.BlockSpec((tm, D_MODEL), row), pl.BlockSpec((1, D_MODEL), lambda i: (0, 0)),
                 pl.BlockSpec((1, D_MODEL), lambda i: (0, 0))] + [_DEP_SPEC] * len(deps)
    return pl.pallas_call(
        _with_deps(body, 2 * n + 3, deps), name=name, grid=(S // tm,), in_specs=in_specs,
        out_specs=[pl.BlockSpec((tm, D_MODEL), row)] * 3,
        out_shape=[SDS((S, D_MODEL), f32), SDS((S, D_MODEL), f32), SDS((S, D_MODEL), bf16)],
        compiler_params=_cp("parallel"),
    )(*parts, *ws, x, g, b, *deps)


def _ln_bwd_store(dyv, z_ref, g_ref, rscale, first, dzb_ref, dres_ref, dg_ref, db_ref):
    z = z_ref[...]
    mu = jnp.mean(z, axis=-1, keepdims=True)
    zc = z - mu
    rstd = lax.rsqrt(jnp.mean(zc * zc, axis=-1, keepdims=True) + LN_EPS)
    xhat = zc * rstd
    dxh = dyv * g_ref[...]
    m1 = jnp.mean(dxh, axis=-1, keepdims=True)
    m2 = jnp.mean(dxh * xhat, axis=-1, keepdims=True)
    dz = rstd * (dxh - m1 - xhat * m2)
    dzb_ref[...] = (rscale * dz).astype(bf16)
    dres_ref[...] = ALPHA * dz

    @pl.when(first)
    def _():
        dg_ref[...] = jnp.zeros_like(dg_ref)
        db_ref[...] = jnp.zeros_like(db_ref)

    dg_ref[...] += jnp.sum(dyv * xhat, axis=0, keepdims=True)
    db_ref[...] += jnp.sum(dyv, axis=0, keepdims=True)


def _ln_bwd_specs(S, tm, index):
    vec = pl.BlockSpec((1, D_MODEL), lambda *a: (0, 0))
    blk = pl.BlockSpec((tm, D_MODEL), index)
    in_specs = [blk, vec]
    out_specs = [blk, blk, vec, vec]
    out_shape = [SDS((S, D_MODEL), bf16), SDS((S, D_MODEL), f32), SDS((1, D_MODEL), f32), SDS((1, D_MODEL), f32)]
    return in_specs, out_specs, out_shape


def ffn_bwd_da(name, drb, w2, l, gate, up, deps=()):
    S = drb.shape[0]
    tm = _tile(S, 512)
    nh = D_FF // 2

    def body(dr_ref, w_ref, g_ref, u_ref, dh_ref):
        dr = dr_ref[...]
        for j in range(2):
            cols = slice(j * nh, (j + 1) * nh)
            da = _dot_nt(dr, w_ref[0, cols, :])
            g = g_ref[:, cols].astype(f32)
            u = u_ref[:, cols].astype(f32)
            sg = jax.nn.sigmoid(g)
            dh_ref[:, cols] = (da * u * (sg * (1.0 + g * (1.0 - sg)))).astype(bf16)
            dh_ref[:, D_FF + j * nh:D_FF + (j + 1) * nh] = (da * (g * sg)).astype(bf16)

    row = lambda i: (i, 0)
    return pl.pallas_call(
        _with_deps(body, 4, deps), name=name, grid=(S // tm,),
        in_specs=[pl.BlockSpec((tm, D_MODEL), row), pl.BlockSpec((1, D_FF, D_MODEL), lambda i: (l, 0, 0)),
                  pl.BlockSpec((tm, D_FF), row), pl.BlockSpec((tm, D_FF), row)] + [_DEP_SPEC] * len(deps),
        out_specs=pl.BlockSpec((tm, 2 * D_FF), row),
        out_shape=SDS((S, 2 * D_FF), bf16),
        compiler_params=_cp("parallel"),
    )(drb, w2, gate, up, *deps)


def ffn_dx(name, dh, w13, l, res, ln=None):
    S = dh.shape[0]
    ns = w13.shape[3]
    tm = _tile(S, 1024)
    last = N_CHIPS - 1
    row = lambda i, j: (i, 0)
    in_specs = [pl.BlockSpec((tm, ns), lambda i, j: (i, j)),
                pl.BlockSpec((1, 1, D_MODEL, ns), lambda i, j: (l, j, 0, 0)),
                pl.BlockSpec((tm, D_MODEL), row)]
    if ln is None:
        def body(dh_ref, w_ref, r_ref, o_ref):
            @pl.when(pl.program_id(1) == 0)
            def _():
                o_ref[...] = r_ref[...]

            o_ref[...] += _dot_nt(dh_ref[...], w_ref[0, 0])

        return pl.pallas_call(
            body, name=name, grid=(S // tm, N_CHIPS), in_specs=in_specs,
            out_specs=pl.BlockSpec((tm, D_MODEL), row), out_shape=SDS((S, D_MODEL), f32),
            compiler_params=_cp("parallel", "arbitrary"),
        )(dh, w13, res)

    z, g, rscale = ln

    def body_ln(dh_ref, w_ref, r_ref, z_ref, g_ref, dzb_ref, dres_ref, dg_ref, db_ref, acc_sc):
        i, j = pl.program_id(0), pl.program_id(1)

        @pl.when(j == 0)
        def _():
            acc_sc[...] = r_ref[...]

        acc_sc[...] += _dot_nt(dh_ref[...], w_ref[0, 0])

        @pl.when(j == last)
        def _():
            _ln_bwd_store(acc_sc[...], z_ref, g_ref, rscale, i == 0, dzb_ref, dres_ref, dg_ref, db_ref)

    ln_in, ln_out, ln_shape = _ln_bwd_specs(S, tm, row)
    return pl.pallas_call(
        body_ln, name=name, grid=(S // tm, N_CHIPS), in_specs=in_specs + ln_in, out_specs=ln_out, out_shape=ln_shape,
        scratch_shapes=[pltpu.VMEM((tm, D_MODEL), f32)], compiler_params=_cp("arbitrary", "arbitrary"),
    )(dh, w13, res, z, g)


def mm_nt_res(name, dys, ws, wl, res, out_dtype, ln=None):
    S = dys[0].shape[0]
    K = ws[0].shape[1]
    tm = _tile(S, 512)
    n = len(dys)
    n_in = 2 * n + (res is not None)

    def product(refs):
        acc = _dot_nt(refs[0][...], refs[n][0])
        for k in range(1, n):
            acc = acc + _dot_nt(refs[k][...], refs[n + k][0])
        if res is not None:
            acc = acc + refs[2 * n][...]
        return acc

    def body(*refs):
        refs[-1][...] = product(refs).astype(out_dtype)

    def body_ln(*refs):
        z_ref, g_ref, dzb_ref, dres_ref, dg_ref, db_ref = refs[n_in:]
        _ln_bwd_store(product(refs), z_ref, g_ref, ln[2], pl.program_id(0) == 0, dzb_ref, dres_ref, dg_ref, db_ref)

    row = lambda i: (i, 0)
    in_specs = [pl.BlockSpec((tm, d.shape[1]), row) for d in dys]
    in_specs += [pl.BlockSpec((1,) + w.shape[1:], functools.partial(lambda li, i: (li, 0, 0), li)) for w, li in zip(ws, wl)]
    args = list(dys) + list(ws)
    if res is not None:
        in_specs.append(pl.BlockSpec((tm, K), row))
        args.append(res)
    if ln is None:
        return pl.pallas_call(
            body, name=name, grid=(S // tm,), in_specs=in_specs,
            out_specs=pl.BlockSpec((tm, K), row), out_shape=SDS((S, K), out_dtype),
            compiler_params=_cp("parallel"),
        )(*args)
    ln_in, ln_out, ln_shape = _ln_bwd_specs(S, tm, row)
    return pl.pallas_call(
        body_ln, name=name, grid=(S // tm,), in_specs=in_specs + ln_in, out_specs=ln_out, out_shape=ln_shape,
        compiler_params=_cp("arbitrary"),
    )(*args, ln[0], ln[1])


def mm_tn(name, x, dy, col_shards=False, deps=()):
    S, K = x.shape
    N = dy.shape[1]
    ts = 512
    while ts * 2 <= min(S, 2048) and S % (ts * 2) == 0 and ts * 2 * K * 2 <= 6 * 2**20:
        ts *= 2
    ts = _tile(S, ts)
    if col_shards:
        tn = N // N_CHIPS
    else:
        tn = N
        while K * tn * 4 > 6 * 2**20 and tn % 256 == 0:
            tn //= 2
    nn = N // tn
    lead = ((0,) if col_shards else ()) + (slice(None), slice(None))

    def body(x_ref, dy_ref, o_ref):
        acc = _dot_tn(x_ref[...].astype(bf16), dy_ref[...].astype(bf16))

        @pl.when(pl.program_id(1) == 0)
        def _():
            o_ref[lead] = acc

        @pl.when(pl.program_id(1) != 0)
        def _():
            o_ref[lead] += acc

    if col_shards:
        out_spec = pl.BlockSpec((1, K, tn), lambda n, s: (n, 0, 0))
        out_shape = SDS((N_CHIPS, K, tn), f32)
    else:
        out_spec = pl.BlockSpec((K, tn), lambda n, s: (0, n))
        out_shape = SDS((K, N), f32)
    return pl.pallas_call(
        _with_deps(body, 2, deps), name=name, grid=(nn, S // ts),
        in_specs=[pl.BlockSpec((ts, K), lambda n, s: (s, 0)), pl.BlockSpec((ts, tn), lambda n, s: (s, n))]
        + [_DEP_SPEC] * len(deps),
        out_specs=out_spec, out_shape=out_shape, compiler_params=_cp("parallel", "arbitrary"),
    )(x, dy, *deps)


def mm_nn_shard(name, x, w, l):
    S, K = x.shape
    ns = w.shape[3]

    def body(x_ref, w_ref, o_ref):
        o_ref[...] = _dot(x_ref[...], w_ref[0, 0]).astype(bf16)

    return pl.pallas_call(
        body, name=name, grid=(N_CHIPS,),
        in_specs=[pl.BlockSpec((S, K), lambda j: (0, 0)), pl.BlockSpec((1, 1, K, ns), lambda j: (l, j, 0, 0))],
        out_specs=pl.BlockSpec((S, ns), lambda j: (0, j)), out_shape=SDS((S, N_CHIPS * ns), bf16),
        compiler_params=_cp("parallel"),
    )(x, w)


def loss_grad(name, y, t, ln):
    S = y.shape[0]
    tm = _tile(S, 512)
    z, g, rscale = ln

    def body(y_ref, t_ref, z_ref, g_ref, dzb_ref, dres_ref, dg_ref, db_ref, loss_ref):
        first = pl.program_id(0) == 0
        e = y_ref[...] - t_ref[...]
        _ln_bwd_store(e * (1.0 / D_MODEL), z_ref, g_ref, rscale, first, dzb_ref, dres_ref, dg_ref, db_ref)

        @pl.when(first)
        def _():
            loss_ref[...] = jnp.zeros_like(loss_ref)

        loss_ref[...] += jnp.full(loss_ref.shape, (0.5 / D_MODEL) * jnp.sum(e * e), f32)

    row = lambda i: (i, 0)
    ln_in, ln_out, ln_shape = _ln_bwd_specs(S, tm, row)
    return pl.pallas_call(
        body, name=name, grid=(S // tm,),
        in_specs=[pl.BlockSpec((tm, D_MODEL), row)] * 2 + ln_in,
        out_specs=ln_out + [pl.BlockSpec((8, 128), lambda i: (0, 0))],
        out_shape=ln_shape + [SDS((8, 128), f32)],
        compiler_params=_cp("arbitrary"),
    )(y, t, z, g)


def _half_sum(t):
    return t + pltpu.roll(t, 64, axis=1)


def mix_pre(name, xb, w_in, wq, wkv, l, gq, gkv, cs):
    S = xb.shape[0]
    tm = _tile(S, 512)
    H = MLA_HEADS
    W_EXT = w_in.shape[2]

    def body(x_ref, win_ref, wq_ref, wkv_ref, gq_ref, gkv_ref, cs_ref,
             u_ref, cq_ref, ckv_ref, cqn_ref, ckvn_ref, q_ref, k_ref, v_ref):
        h = _dot(x_ref[...], win_ref[0])
        u_ref[...] = h[:, :256]
        cq = h[:, 256:512]
        ckv = h[:, 512:640]
        cq_ref[...] = cq
        ckv_ref[...] = ckv
        cqn = (cq * lax.rsqrt(jnp.mean(cq * cq, axis=-1, keepdims=True) + RMS_EPS) * gq_ref[...]).astype(bf16)
        ckvn = (ckv * lax.rsqrt(jnp.mean(ckv * ckv, axis=-1, keepdims=True) + RMS_EPS) * gkv_ref[...]).astype(bf16)
        cqn_ref[...] = cqn
        ckvn_ref[...] = ckvn
        csv = cs_ref[...]
        lane = lax.broadcasted_iota(jnp.int32, (tm, 128), 1)
        kr = jnp.where(lane < 64, _half_sum(h[:, 640:768] * csv), 0.0).astype(bf16)
        kv = _dot(ckvn, wkv_ref[0])
        for hd in range(H):
            qe = _dot(cqn, wq_ref[0, hd])
            q_ref[hd, :, :128] = qe[:, :128].astype(bf16)
            q_ref[hd, :, 128:] = _half_sum(qe[:, 128:] * csv).astype(bf16)
            k_ref[hd, :, :128] = kv[:, 256 * hd:256 * hd + 128].astype(bf16)
            k_ref[hd, :, 128:] = kr
            v_ref[hd] = kv[:, 256 * hd + 128:256 * hd + 256].astype(bf16)

    row = lambda i: (i, 0)
    hrow = lambda i: (0, i, 0)
    return pl.pallas_call(
        body, name=name, grid=(S // tm,),
        in_specs=[pl.BlockSpec((tm, D_MODEL), row),
                  pl.BlockSpec((1, D_MODEL, W_EXT), lambda i: (l, 0, 0)),
                  pl.BlockSpec((1, H, Q_LORA, 256), lambda i: (l, 0, 0, 0)),
                  pl.BlockSpec((1, KV_LORA, H * 256), lambda i: (l, 0, 0)),
                  pl.BlockSpec((1, Q_LORA), lambda i: (0, 0)), pl.BlockSpec((1, KV_LORA), lambda i: (0, 0)),
                  pl.BlockSpec((tm, 128), row)],
        out_specs=[pl.BlockSpec((tm, 256), row), pl.BlockSpec((tm, Q_LORA), row), pl.BlockSpec((tm, KV_LORA), row),
                   pl.BlockSpec((tm, Q_LORA), row), pl.BlockSpec((tm, KV_LORA), row),
                   pl.BlockSpec((H, tm, 256), hrow), pl.BlockSpec((H, tm, 256), hrow), pl.BlockSpec((H, tm, 128), hrow)],
        out_shape=[SDS((S, 256), f32), SDS((S, Q_LORA), f32), SDS((S, KV_LORA), f32),
                   SDS((S, Q_LORA), bf16), SDS((S, KV_LORA), bf16),
                   SDS((H, S, 256), bf16), SDS((H, S, 256), bf16), SDS((H, S, 128), bf16)],
        compiler_params=_cp("parallel"),
    )(xb, w_in, wq, wkv, gq, gkv, cs)


def _group_select(col, a2, a4, a8, a16):
    return jnp.where(col < 64, a2, jnp.where(col < 128, a4, jnp.where(col < 192, a8, a16)))


def pool_fwd(name, u, wbd, scale):
    S = u.shape[0]
    tm = _tile(S, 512)
    hb = tm // HALO

    def body(u_ref, halo_ref, w_ref, s_ref, d_ref, y_ref):
        i = pl.program_id(0)
        cur = u_ref[...]
        halo = jnp.where(i > 0, halo_ref[...], 0.0)
        ext = jnp.concatenate([halo, cur], axis=0)
        s2 = ext + pltpu.roll(ext, 1, axis=0)
        s4 = s2 + pltpu.roll(s2, 2, axis=0)
        s8 = s4 + pltpu.roll(s4, 4, axis=0)
        s16 = s8 + pltpu.roll(s8, 8, axis=0)
        t1 = (i * tm + 1 + lax.broadcasted_iota(jnp.int32, (tm, 1), 0)).astype(f32)
        col = lax.broadcasted_iota(jnp.int32, (tm, 256), 1)
        m = _group_select(col, s2[HALO:] / jnp.minimum(t1, 2.0), s4[HALO:] / jnp.minimum(t1, 4.0),
                          s8[HALO:] / jnp.minimum(t1, 8.0), s16[HALO:] / jnp.minimum(t1, 16.0))
        d = (m - cur).astype(bf16)
        d_ref[...] = d
        y_ref[...] = (_dot(d, w_ref[...]) * s_ref[...]).astype(bf16)

    row = lambda i: (i, 0)
    return pl.pallas_call(
        body, name=name, grid=(S // tm,),
        in_specs=[pl.BlockSpec((tm, 256), row), pl.BlockSpec((HALO, 256), lambda i: (jnp.maximum(i * hb - 1, 0), 0)),
                  pl.BlockSpec((256, 256), lambda i: (0, 0)), pl.BlockSpec((1, 256), lambda i: (0, 0))],
        out_specs=[pl.BlockSpec((tm, 256), row)] * 2,
        out_shape=[SDS((S, 256), bf16), SDS((S, 256), bf16)],
        compiler_params=_cp("parallel"),
    )(u, u, wbd, scale)


def pool_bwd(name, dyp, d, wbd, scale):
    S = dyp.shape[0]
    tm = _tile(S, 512)
    hb = tm // HALO
    n_ext = tm + HALO

    def fwd_sum(e, steps):
        k = 1
        for _ in range(steps):
            e = e + pltpu.roll(e, n_ext - k, axis=0)
            k *= 2
        return e

    def body(dy_ref, halo_ref, d_ref, w_ref, s_ref, du_ref, dyw_ref, ds_ref):
        i = pl.program_id(0)
        sc = s_ref[...]
        w = w_ref[...]
        cur = dy_ref[...].astype(f32)
        halo = jnp.where(i < pl.num_programs(0) - 1, halo_ref[...].astype(f32), 0.0)
        dyw = jnp.concatenate([cur, halo], axis=0) * sc
        dyw_ref[...] = dyw[:tm].astype(bf16)
        dd = _dot_nt(dyw.astype(bf16), w)
        t1 = (i * tm + 1 + lax.broadcasted_iota(jnp.int32, (n_ext, 1), 0)).astype(f32)
        f2 = fwd_sum(dd / jnp.minimum(t1, 2.0), 1)
        f4 = fwd_sum(dd / jnp.minimum(t1, 4.0), 2)
        f8 = fwd_sum(dd / jnp.minimum(t1, 8.0), 3)
        f16 = fwd_sum(dd / jnp.minimum(t1, 16.0), 4)
        col = lax.broadcasted_iota(jnp.int32, (tm, 256), 1)
        du_ref[...] = (_group_select(col, f2[:tm], f4[:tm], f8[:tm], f16[:tm]) - dd[:tm]).astype(bf16)

        @pl.when(i == 0)
        def _():
            ds_ref[...] = jnp.zeros_like(ds_ref)

        ds_ref[...] += jnp.sum(cur * _dot(d_ref[...], w), axis=0, keepdims=True)

    row = lambda i: (i, 0)
    nhb = S // HALO
    return pl.pallas_call(
        body, name=name, grid=(S // tm,),
        in_specs=[pl.BlockSpec((tm, 256), row), pl.BlockSpec((HALO, 256), lambda i: (jnp.minimum((i + 1) * hb, nhb - 1), 0)),
                  pl.BlockSpec((tm, 256), row), pl.BlockSpec((256, 256), lambda i: (0, 0)),
                  pl.BlockSpec((1, 256), lambda i: (0, 0))],
        out_specs=[pl.BlockSpec((tm, 256), row), pl.BlockSpec((tm, 256), row), pl.BlockSpec((1, 256), lambda i: (0, 0))],
        out_shape=[SDS((S, 256), bf16), SDS((S, 256), bf16), SDS((1, 256), f32)],
        compiler_params=_cp("arbitrary"),
    )(dyp, dyp, d, wbd, scale)


def _diag_mask(tq):
    rc = lax.broadcasted_iota(jnp.int32, (tq, 1), 0) // 64
    cc = lax.broadcasted_iota(jnp.int32, (1, tq), 1) // 64
    return rc >= cc


MLA_SCALE_LOG2 = MLA_SCALE * math.log2(math.e)


def mla_attn_fwd(name, q, k, v):
    H, S, _ = q.shape
    tq = _tile(S, 1024)
    nq = S // tq
    pairs = [(i, j) for i in range(nq) for j in range(i + 1)]
    it = jnp.asarray([p_[0] for p_ in pairs], jnp.int32)
    jt = jnp.asarray([p_[1] for p_ in pairs], jnp.int32)

    def body(it_ref, jt_ref, q_ref, k_ref, v_ref, o_ref, lse_ref, m_sc, l_sc, acc_sc):
        t = pl.program_id(1)
        i, j = it_ref[t], jt_ref[t]

        @pl.when(j == 0)
        def _():
            m_sc[...] = jnp.full_like(m_sc, NEG_INF)
            l_sc[...] = jnp.zeros_like(l_sc)
            acc_sc[...] = jnp.zeros_like(acc_sc)

        def step(masked):
            s = _dot_nt(q_ref[0], k_ref[0])
            if masked:
                s = jnp.where(_diag_mask(tq), s, NEG_INF)
            m_prev = m_sc[...]
            m_new = jnp.maximum(m_prev, jnp.max(s, axis=-1, keepdims=True))
            p = jnp.exp2((s - jnp.tile(m_new, (1, tq // 128))) * MLA_SCALE_LOG2)
            a = jnp.exp2((m_prev - m_new) * MLA_SCALE_LOG2)
            l_sc[...] = a * l_sc[...] + jnp.sum(p, axis=-1, keepdims=True)
            acc_sc[...] = a * acc_sc[...] + _dot(p.astype(bf16), v_ref[0])
            m_sc[...] = m_new

        @pl.when(j < i)
        def _():
            step(False)

        @pl.when(j == i)
        def _():
            step(True)
            o_ref[...] = (acc_sc[...] / l_sc[...]).astype(bf16)
            lse_ref[0] = m_sc[...] * MLA_SCALE_LOG2 + jnp.log2(l_sc[...])

    return pl.pallas_call(
        body, name=name,
        grid_spec=pltpu.PrefetchScalarGridSpec(
            num_scalar_prefetch=2, grid=(H, len(pairs)),
            in_specs=[pl.BlockSpec((1, tq, 256), lambda h, t, it_, jt_: (h, it_[t], 0)),
                      pl.BlockSpec((1, tq, 256), lambda h, t, it_, jt_: (h, jt_[t], 0)),
                      pl.BlockSpec((1, tq, 128), lambda h, t, it_, jt_: (h, jt_[t], 0))],
            out_specs=[pl.BlockSpec((tq, 128), lambda h, t, it_, jt_: (it_[t], h)),
                       pl.BlockSpec((1, tq, 128), lambda h, t, it_, jt_: (h, it_[t], 0))],
            scratch_shapes=[pltpu.VMEM((tq, 128), f32), pltpu.VMEM((tq, 128), f32), pltpu.VMEM((tq, 128), f32)]),
        out_shape=[SDS((S, H * 128), bf16), SDS((H, S, 128), f32)],
        compiler_params=_cp("parallel", "arbitrary"),
    )(it, jt, q, k, v)


def mla_attn_bwd(name, q, k, v, o, do, lse):
    H, S, _ = q.shape
    tq = _tile(S, 1024)
    nq = S // tq
    pairs = [(i, j) for j in range(nq) for i in range(j, nq)]
    it = jnp.asarray([p_[0] for p_ in pairs], jnp.int32)
    jt = jnp.asarray([p_[1] for p_ in pairs], jnp.int32)
    n_pairs = len(pairs)

    def body(it_ref, jt_ref, q_ref, k_ref, v_ref, o_ref, do_ref, lse_ref, dq_ref, dk_ref, dv_ref, dk_sc, dv_sc):
        t = pl.program_id(1)
        i, j = it_ref[t], jt_ref[t]

        @pl.when(t == 0)
        def _():
            dq_ref[...] = jnp.zeros_like(dq_ref)

        @pl.when(i == j)
        def _():
            dk_sc[...] = jnp.zeros_like(dk_sc)
            dv_sc[...] = jnp.zeros_like(dv_sc)

        def step(masked):
            qv, kv_, dov = q_ref[0], k_ref[0], do_ref[...]
            s = _dot_nt(qv, kv_)
            if masked:
                s = jnp.where(_diag_mask(tq), s, NEG_INF)
            p = jnp.exp2(s * MLA_SCALE_LOG2 - jnp.tile(lse_ref[0], (1, tq // 128)))
            dv_sc[...] += _dot_tn(p.astype(bf16), dov)
            dp = _dot_nt(dov, v_ref[0])
            delta = jnp.sum(dov.astype(f32) * o_ref[...].astype(f32), axis=-1, keepdims=True)
            ds = (p * (dp - delta)).astype(bf16)
            dk_sc[...] += _dot_tn(ds, qv)
            rows = pl.ds(pl.multiple_of(i * tq, tq), tq)
            dq_ref[0, rows, :] += _dot(ds, kv_)

        @pl.when(i > j)
        def _():
            step(False)

        @pl.when(i == j)
        def _():
            step(True)

        @pl.when(i == nq - 1)
        def _():
            dk_ref[0] = dk_sc[...] * MLA_SCALE
            dv_ref[0] = dv_sc[...]

        @pl.when(t == n_pairs - 1)
        def _():
            dq_ref[...] = dq_ref[...] * MLA_SCALE

    qi = lambda h, t, it_, jt_: (h, it_[t], 0)
    kj = lambda h, t, it_, jt_: (h, jt_[t], 0)
    oi = lambda h, t, it_, jt_: (it_[t], h)
    return pl.pallas_call(
        body, name=name,
        grid_spec=pltpu.PrefetchScalarGridSpec(
            num_scalar_prefetch=2, grid=(H, n_pairs),
            in_specs=[pl.BlockSpec((1, tq, 256), qi), pl.BlockSpec((1, tq, 256), kj), pl.BlockSpec((1, tq, 128), kj),
                      pl.BlockSpec((tq, 128), oi), pl.BlockSpec((tq, 128), oi), pl.BlockSpec((1, tq, 128), qi)],
            out_specs=[pl.BlockSpec((1, S, 256), lambda h, t, it_, jt_: (h, 0, 0)), pl.BlockSpec((1, tq, 256), kj),
                       pl.BlockSpec((1, tq, 128), kj)],
            scratch_shapes=[pltpu.VMEM((tq, 256), f32), pltpu.VMEM((tq, 128), f32)]),
        out_shape=[SDS((H, S, 256), f32), SDS((H, S, 256), f32), SDS((H, S, 128), f32)],
        compiler_params=_cp("parallel", "arbitrary"),
    )(it, jt, q, k, v, o, do, lse)


def mix_post_bwd(name, dq, dk, dv, wq, wkv, l, cq, ckv, gq, gkv, cs):
    H, S, _ = dq.shape
    tm = _tile(S, 512)

    def rms_bwd(dyn, c, g):
        r = lax.rsqrt(jnp.mean(c * c, axis=-1, keepdims=True) + RMS_EPS)
        ch = c * r
        dyg = dyn * g
        dc = r * (dyg - ch * jnp.mean(dyg * ch, axis=-1, keepdims=True))
        return dc, jnp.sum(dyn * ch, axis=0, keepdims=True)

    def body(dq_ref, dk_ref, dv_ref, wq_ref, wkv_ref, cq_ref, ckv_ref, gq_ref, gkv_ref, cs_ref,
             dqe_ref, dkv_ref, dh_ref, dgq_ref, dgkv_ref):
        csv = cs_ref[...]
        lane = lax.broadcasted_iota(jnp.int32, (tm, 128), 1)
        dcqn = jnp.zeros((tm, Q_LORA), f32)
        dkr = jnp.zeros((tm, 128), f32)
        for hd in range(H):
            dqh = dq_ref[hd]
            dqe = jnp.concatenate([dqh[:, :128], _half_sum(dqh[:, 128:]) * csv], axis=1).astype(bf16)
            dqe_ref[:, 256 * hd:256 * hd + 256] = dqe
            dcqn = dcqn + _dot_nt(dqe, wq_ref[0, hd])
            dkh = dk_ref[hd]
            dkv_ref[:, 256 * hd:256 * hd + 128] = dkh[:, :128].astype(bf16)
            dkv_ref[:, 256 * hd + 128:256 * hd + 256] = dv_ref[hd].astype(bf16)
            dkr = dkr + dkh[:, 128:]
        dckvn = _dot_nt(dkv_ref[...], wkv_ref[0])
        dblk = _half_sum(jnp.where(lane < 64, dkr, 0.0)) * csv
        dcq, dgq = rms_bwd(dcqn, cq_ref[...], gq_ref[...])
        dckv, dgkv = rms_bwd(dckvn, ckv_ref[...], gkv_ref[...])
        dh_ref[:, :256] = dcq.astype(bf16)
        dh_ref[:, 256:384] = dckv.astype(bf16)
        dh_ref[:, 384:] = dblk.astype(bf16)

        @pl.when(pl.program_id(0) == 0)
        def _():
            dgq_ref[...] = jnp.zeros_like(dgq_ref)
            dgkv_ref[...] = jnp.zeros_like(dgkv_ref)

        dgq_ref[...] += dgq
        dgkv_ref[...] += dgkv

    row = lambda i: (i, 0)
    hrow = lambda i: (0, i, 0)
    return pl.pallas_call(
        body, name=name, grid=(S // tm,),
        in_specs=[pl.BlockSpec((H, tm, 256), hrow), pl.BlockSpec((H, tm, 256), hrow), pl.BlockSpec((H, tm, 128), hrow),
                  pl.BlockSpec((1, H, Q_LORA, 256), lambda i: (l, 0, 0, 0)),
                  pl.BlockSpec((1, KV_LORA, H * 256), lambda i: (l, 0, 0)),
                  pl.BlockSpec((tm, Q_LORA), row), pl.BlockSpec((tm, KV_LORA), row),
                  pl.BlockSpec((1, Q_LORA), lambda i: (0, 0)), pl.BlockSpec((1, KV_LORA), lambda i: (0, 0)),
                  pl.BlockSpec((tm, 128), row)],
        out_specs=[pl.BlockSpec((tm, H * 256), row), pl.BlockSpec((tm, H * 256), row), pl.BlockSpec((tm, 512), row),
                   pl.BlockSpec((1, Q_LORA), lambda i: (0, 0)), pl.BlockSpec((1, KV_LORA), lambda i: (0, 0))],
        out_shape=[SDS((S, H * 256), bf16), SDS((S, H * 256), bf16), SDS((S, 512), bf16),
                   SDS((1, Q_LORA), f32), SDS((1, KV_LORA), f32)],
        compiler_params=_cp("arbitrary"),
    )(dq, dk, dv, wq, wkv, cq, ckv, gq, gkv, cs)


def _cross_probs(qb, kv_ref, hd):
    cols = slice(hd * MEM_HEAD_DIM, (hd + 1) * MEM_HEAD_DIM)
    s = _dot_nt(qb[:, cols], kv_ref[:, cols]) * MEM_SCALE
    e = jnp.exp(s - jnp.max(s, axis=-1, keepdims=True))
    return e / jnp.sum(e, axis=-1, keepdims=True)


def cross_fwd(name, xb, xf, wq, wo, l, kv, g, b):
    S = xb.shape[0]
    tm = _tile(S, 512)
    M = kv.shape[0]

    def body(x_ref, xf_ref, wq_ref, wo_ref, k_ref, v_ref, g_ref, b_ref, q_ref, o_ref, z_ref, y_ref, yb_ref):
        qb = _dot(x_ref[...], wq_ref[0]).astype(bf16)
        q_ref[...] = qb
        for hd in range(MEM_HEADS):
            cols = slice(hd * MEM_HEAD_DIM, (hd + 1) * MEM_HEAD_DIM)
            p = _cross_probs(qb, k_ref, hd)
            o_ref[:, cols] = _dot(p.astype(bf16), v_ref[:, cols]).astype(bf16)
        z = ALPHA * xf_ref[...] + _dot(o_ref[...], wo_ref[0])
        mu = jnp.mean(z, axis=-1, keepdims=True)
        zc = z - mu
        var = jnp.mean(zc * zc, axis=-1, keepdims=True)
        y = zc * lax.rsqrt(var + LN_EPS) * g_ref[...] + b_ref[...]
        z_ref[...] = z
        y_ref[...] = y
        yb_ref[...] = y.astype(bf16)

    row = lambda i: (i, 0)
    wspec = pl.BlockSpec((1, D_MODEL, D_MODEL), lambda i: (l, 0, 0))
    vec = pl.BlockSpec((1, D_MODEL), lambda i: (0, 0))
    blk = pl.BlockSpec((tm, D_MODEL), row)
    return pl.pallas_call(
        body, name=name, grid=(S // tm,),
        in_specs=[blk, blk, wspec, wspec, pl.BlockSpec((M, D_MODEL), lambda i: (0, 0)),
                  pl.BlockSpec((M, D_MODEL), lambda i: (0, 1)), vec, vec],
        out_specs=[blk] * 5,
        out_shape=[SDS((S, D_MODEL), bf16), SDS((S, D_MODEL), bf16), SDS((S, D_MODEL), f32), SDS((S, D_MODEL), f32),
                   SDS((S, D_MODEL), bf16)],
        compiler_params=_cp("parallel"),
    )(xb, xf, wq, wo, kv, kv, g, b)


def cross_bwd(name, dzb, wo, l, qb, kv, deps=()):
    S = dzb.shape[0]
    tm = _tile(S, 512)
    M = kv.shape[0]

    def body(dz_ref, wo_ref, q_ref, k_ref, v_ref, dq_ref, dkv_ref):
        @pl.when(pl.program_id(0) == 0)
        def _():
            dkv_ref[...] = jnp.zeros_like(dkv_ref)

        do = _dot_nt(dz_ref[...], wo_ref[0]).astype(bf16)
        qv = q_ref[...]
        for hd in range(MEM_HEADS):
            cols = slice(hd * MEM_HEAD_DIM, (hd + 1) * MEM_HEAD_DIM)
            vcols = slice(D_MODEL + hd * MEM_HEAD_DIM, D_MODEL + (hd + 1) * MEM_HEAD_DIM)
            p = _cross_probs(qv, k_ref, hd)
            doh = do[:, cols]
            dkv_ref[:, vcols] += _dot_tn(p.astype(bf16), doh)
            dp = _dot_nt(doh, v_ref[:, cols])
            ds = (p * (dp - jnp.sum(dp * p, axis=-1, keepdims=True)) * MEM_SCALE).astype(bf16)
            dq_ref[:, cols] = _dot(ds, k_ref[:, cols]).astype(bf16)
            dkv_ref[:, cols] += _dot_tn(ds, qv[:, cols])

    row = lambda i: (i, 0)
    blk = pl.BlockSpec((tm, D_MODEL), row)
    return pl.pallas_call(
        _with_deps(body, 5, deps), name=name, grid=(S // tm,),
        in_specs=[blk, pl.BlockSpec((1, D_MODEL, D_MODEL), lambda i: (l, 0, 0)), blk,
                  pl.BlockSpec((M, D_MODEL), lambda i: (0, 0)), pl.BlockSpec((M, D_MODEL), lambda i: (0, 1))]
        + [_DEP_SPEC] * len(deps),
        out_specs=[blk, pl.BlockSpec((M, 2 * D_MODEL), lambda i: (0, 0))],
        out_shape=[SDS((S, D_MODEL), bf16), SDS((M, 2 * D_MODEL), f32)],
        compiler_params=_cp("arbitrary"),
    )(dzb, wo, qb, kv, kv, *deps)


def adamw(name, w, g, m, v, deps=()):
    shape = w.shape
    cols = shape[-1]
    rows = math.prod(shape[:-1])
    tr = _row_tile(rows, cols, target=2 * 2**20)
    c1 = 1.0 - ADAM_B1 ** ADAM_STEP
    c2 = 1.0 - ADAM_B2 ** ADAM_STEP

    def body(w_ref, g_ref, m_ref, v_ref, d_ref, nm_ref, nv_ref):
        gv = g_ref[...]
        nm = ADAM_B1 * m_ref[...] + (1.0 - ADAM_B1) * gv
        nv = ADAM_B2 * v_ref[...] + (1.0 - ADAM_B2) * (gv * gv)
        d_ref[...] = -ADAM_LR * ((nm / c1) / (jnp.sqrt(nv / c2) + ADAM_EPS) + ADAM_WD * w_ref[...])
        nm_ref[...] = nm
        nv_ref[...] = nv

    blk = pl.BlockSpec((tr, cols), lambda i: (i, 0))
    flat = SDS((rows, cols), f32)
    outs = pl.pallas_call(
        _with_deps(body, 4, deps), name=name, grid=(rows // tr,), in_specs=[blk] * 4 + [_DEP_SPEC] * len(deps),
        out_specs=[blk] * 3, out_shape=[flat] * 3, compiler_params=_cp("parallel"),
    )(*[a.reshape(rows, cols) for a in (w, g, m, v)], *deps)
    return [o.reshape(shape) for o in outs]


def _me():
    return lax.axis_index("x"), lax.axis_index("y"), lax.axis_index("c")


def _other_chips(x, y):
    return [(1 - x, y), (x, 1 - y), (1 - x, 1 - y)]


def _pair_share_each(owners, bufs, sems, mine, act):
    x, y, c = _me()
    for o in range(2):
        slots = [(a, lyr) for a in range(len(bufs)) for lyr in range(DEPTH) if owners[a][lyr] == o]

        @pl.when((c == o) if mine else (c != o))
        def _(slots=slots):
            for a, lyr in slots:
                slot = bufs[a].at[lyr]
                act(_rcopy(slot, slot, sems[0].at[2 * a + lyr], sems[1].at[2 * a + lyr], (x, y, 1 - c)))


def pair_share_start(name, sums, owners, after):
    def body_fn(b_in, s_in, s_out):
        _pair_share_each(owners, b_in, s_out, True, lambda cp: cp.start())

    outs, sems, token = _split_call(name, body_fn, list(sums), [], [2 * len(sums)] * 2, after)
    return (outs, sems[0], sems[1], owners), token


def pair_share_wait(name, st, after):
    bufs, send, recv, owners = st

    def body_fn(b_in, s_in, s_out):
        _pair_share_each(owners, b_in, s_in, True, lambda cp: cp.wait_send())
        _pair_share_each(owners, b_in, s_in, False, lambda cp: cp.wait_recv())

    outs, _, _ = _split_call(name, body_fn, list(bufs), [send, recv], [], after)
    return outs


def allsum_small(name, v, deps=()):
    R = v.shape[0]

    def body(v_ref, o_ref, all_ref, send_sems, recv_sems, local_sem):
        x, y, c = _me()
        me, sibling = (x, y, c), (x, y, 1 - c)
        chips = _other_chips(x, y)

        def rows(px, py, pc):
            return all_ref.at[4 * px + 2 * py + pc]

        def copy(k, block, to, src=None):
            return pltpu.make_async_remote_copy(
                src_ref=rows(*block) if src is None else src, dst_ref=rows(*block),
                send_sem=send_sems.at[k], recv_sem=recv_sems.at[k], device_id=to, device_id_type=MESH)

        mine = pltpu.make_async_copy(v_ref, rows(*me), local_sem)
        mine.start()
        first = [copy(0, me, sibling, src=v_ref)]
        first += [copy(1 + j, me, (*chip, c), src=v_ref) for j, chip in enumerate(chips)]
        for cp in first:
            cp.start()
        passed = [copy(4 + j, (*chip, c), sibling) for j, chip in enumerate(chips)]
        for j, chip in enumerate(chips):
            copy(1 + j, (*chip, c), me).wait_recv()
            passed[j].start()
        copy(0, sibling, me).wait_recv()
        for j, chip in enumerate(chips):
            copy(4 + j, (*chip, 1 - c), me).wait_recv()
        for cp in first + passed:
            cp.wait_send()
        mine.wait()
        acc = all_ref[0]
        for d in range(1, 8):
            acc = acc + all_ref[d]
        o_ref[...] = acc

    return pl.pallas_call(
        _with_deps(body, 1, deps), name=name,
        in_specs=[pl.BlockSpec(memory_space=pltpu.VMEM)] + [_DEP_SPEC] * len(deps),
        out_specs=pl.BlockSpec(memory_space=pltpu.VMEM),
        out_shape=SDS((R, 128), f32),
        scratch_shapes=[pltpu.VMEM((8, R, 128), f32), pltpu.SemaphoreType.DMA((7,)), pltpu.SemaphoreType.DMA((7,)),
                        pltpu.SemaphoreType.DMA],
        compiler_params=pltpu.CompilerParams(vmem_limit_bytes=V7X_VMEM_LIMIT),
    )(v, *deps)


def _swap_half(r):
    return jnp.concatenate([-r[..., 32:], r[..., :32]], axis=-1)


def _unswap_add(p, qg):
    return p + jnp.concatenate([qg[..., 32:], -qg[..., :32]], axis=-1)


def _block_diag(pw):
    L = pw.shape[0]
    out = jnp.zeros((L, 256, 256), pw.dtype)
    for gi in range(4):
        out = out.at[:, 64 * gi:64 * gi + 64, 64 * gi:64 * gi + 64].set(pw[:, gi])
    return out


def _to_col_shards(w):
    *lead, K, N = w.shape
    nl = len(lead)
    return w.reshape(*lead, K, N_CHIPS, N // N_CHIPS).transpose(*range(nl), nl + 1, nl, nl + 2)


def _from_col_shards(w):
    *lead, C, K, n = w.shape
    nl = len(lead)
    return w.transpose(*range(nl), nl + 1, nl, nl + 2).reshape(*lead, K, C * n)


def _step_serial_comm(x, mem, positions, ln_g, ln_b, ffn1_w13, ffn1_w2, w_in, pool_w, pool_scale, q_norm_g, w_uq, kv_norm_g, w_ukv, w_out, mem_wq, mem_wkv, mem_wo, ffn2_w13, ffn2_w2, loss_target, m_ln_g, m_ln_b, m_ffn1_w13, m_ffn1_w2, m_w_in, m_pool_w, m_pool_scale, m_q_norm_g, m_w_uq, m_kv_norm_g, m_w_ukv, m_w_out, m_mem_wq, m_mem_wkv, m_mem_wo, m_ffn2_w13, m_ffn2_w2, v_ln_g, v_ln_b, v_ffn1_w13, v_ffn1_w2, v_w_in, v_pool_w, v_pool_scale, v_q_norm_g, v_w_uq, v_kv_norm_g, v_w_ukv, v_w_out, v_mem_wq, v_mem_wkv, v_mem_wo, v_ffn2_w13, v_ffn2_w2):
    L = DEPTH
    S = x.shape[1]
    qx, qy, _ = _me()
    chip = 2 * qx + qy

    big = [ffn1_w13, ffn1_w2, w_in, w_uq, w_ukv, w_out, mem_wq, mem_wkv, mem_wo, ffn2_w13, ffn2_w2]
    (g_f1w13, g_f1w2, g_win, g_wuq, g_wukv, g_wout, g_mwq, g_mwkv, g_mwo, g_f2w13, g_f2w2) = gather_weights(
        [w.astype(bf16) for w in big])
    f1w2 = g_f1w2.reshape(L, D_FF, D_MODEL)
    f2w2 = g_f2w2.reshape(L, D_FF, D_MODEL)
    win = g_win.reshape(L, D_MODEL, D_IN)
    win_ext = jnp.concatenate([win, _swap_half(win[..., D_IN - QK_ROPE:])], axis=-1)
    wuq = _from_col_shards(g_wuq).reshape(L, Q_LORA, MLA_HEADS, QK_NOPE + QK_ROPE)
    wq_ext = jnp.concatenate([wuq, _swap_half(wuq[..., QK_NOPE:])], axis=-1).transpose(0, 2, 1, 3)
    wukv = _from_col_shards(g_wukv)
    wout = g_wout.reshape(L, D_MODEL, D_MODEL)
    wout_pool, wout_mla = wout[:, :POOL_WIDTH], wout[:, POOL_WIDTH:]
    mwq = g_mwq.reshape(L, D_MODEL, D_MODEL)
    mwo = g_mwo.reshape(L, D_MODEL, D_MODEL)
    wbd = _block_diag(pool_w.astype(bf16))

    ln_pad = jnp.zeros((2, L, 4, N_CHIPS, D_MODEL // N_CHIPS), f32)
    ln_pad = lax.dynamic_update_slice(ln_pad, jnp.stack([ln_g, ln_b])[:, :, :, None, :], (0, 0, 0, chip, 0))
    ln_full = allsum_small("allsum_ln", ln_pad.reshape(-1, 128)) * 0.5
    ln_full = ln_full.reshape(2, L, 4, D_MODEL)
    lng, lnb = ln_full[0], ln_full[1]

    half = QK_ROPE // 2
    inv_freq = ROPE_BASE ** (-jnp.arange(half, dtype=f32) / half)
    ang = positions[0].astype(f32)[:, None] * inv_freq
    cos, sin = jnp.cos(ang), jnp.sin(ang)
    cs = jnp.concatenate([cos, cos, sin, sin], axis=-1)

    memb = mem[0].astype(bf16)
    xf = x[0]
    xb = xf.astype(bf16)
    vec = lambda a: a.reshape(1, -1)

    saved = []
    for l in range(L):
        sv = {}
        sv["x0b"] = xb
        gate, up, act = ffn_up(f"ffn1_up_{l}", xb, g_f1w13, l)
        z1, x1f, x1b = proj_res_ln(f"ffn1_down_{l}", [act], [f1w2], [l], xf, vec(lng[l, 0]), vec(lnb[l, 0]), 0.5)
        sv.update(gate1=gate, up1=up, act1=act, z1=z1, x1b=x1b)
        u, cq, ckv, cqn, ckvn, q, k, v = mix_pre(f"mix_pre_{l}", x1b, win_ext, wq_ext, wukv, l,
                                                   vec(q_norm_g[l]), vec(kv_norm_g[l]), cs)
        dpool, ypool = pool_fwd(f"pool_fwd_{l}", u, wbd[l], vec(pool_scale[l]))
        o, lse = mla_attn_fwd(f"mla_fwd_{l}", q, k, v)
        z2, x2f, x2b = proj_res_ln(f"mix_out_{l}", [ypool, o], [wout_pool, wout_mla], [l, l], x1f,
                                   vec(lng[l, 1]), vec(lnb[l, 1]), 1.0)
        sv.update(cq=cq, ckv=ckv, cqn=cqn, ckvn=ckvn, q=q, k=k, v=v, dpool=dpool, ypool=ypool, o=o, lse=lse, z2=z2, x2b=x2b)
        kvm = mm_nn_shard(f"mem_kv_{l}", memb, g_mwkv, l)
        cq_, co_, z3, x3f, x3b = cross_fwd(f"cross_fwd_{l}", x2b, x2f, mwq, mwo, l, kvm, vec(lng[l, 2]), vec(lnb[l, 2]))
        sv.update(kvm=kvm, crq=cq_, cro=co_, z3=z3, x3b=x3b)
        gate, up, act = ffn_up(f"ffn2_up_{l}", x3b, g_f2w13, l)
        z4, xf, xb = proj_res_ln(f"ffn2_down_{l}", [act], [f2w2], [l], x3f, vec(lng[l, 3]), vec(lnb[l, 3]), 0.5)
        sv.update(gate2=gate, up2=up, act2=act, z4=z4)
        saved.append(sv)

    dy, loss_blk = loss_grad("loss_grad", xf, loss_target[0])
    loss = lax.psum(loss_blk[0, 0], ("x", "y", "c"))

    G = dict(f1w13=None, f1w2=None, mwq=None, mwkv=None, mwo=None, f2w13=None, f2w2=None)
    small = {k_: [None] * L for k_ in ("win", "wuq", "wukv", "wout", "pool_w", "pool_scale", "gq", "gkv", "lng", "lnb")}
    for l in reversed(range(L)):
        sv = saved[l]
        dlg, dlb = [None] * 4, [None] * 4
        dzb, dres, dlg[3], dlb[3] = ln_bwd(f"ln4_bwd_{l}", dy, sv["z4"], vec(lng[l, 3]), 0.5)
        dh = ffn_bwd_da(f"ffn2_bwd_da_{l}", dzb, f2w2, l, sv["gate2"], sv["up2"])
        G["f2w2"] = mm_tn(f"ffn2_dw2_{l}", sv["act2"], dzb, "nat", l, G["f2w2"])
        G["f2w13"] = mm_tn(f"ffn2_dw13_{l}", sv["x3b"], dh, "shard", l, G["f2w13"])
        dy = ffn_dx(f"ffn2_dx_{l}", dh, g_f2w13, l, dres)
        dzb, dres, dlg[2], dlb[2] = ln_bwd(f"ln3_bwd_{l}", dy, sv["z3"], vec(lng[l, 2]), 1.0)
        dqc, dkvm = cross_bwd(f"cross_bwd_{l}", dzb, mwo, l, sv["crq"], sv["kvm"])
        G["mwo"] = mm_tn(f"cross_dwo_{l}", sv["cro"], dzb, "nat", l, G["mwo"])
        G["mwq"] = mm_tn(f"cross_dwq_{l}", sv["x2b"], dqc, "nat", l, G["mwq"])
        G["mwkv"] = mm_tn(f"cross_dwkv_{l}", memb, dkvm, "shard", l, G["mwkv"])
        dy = mm_nt_res(f"cross_dx_{l}", [dqc], [mwq], [l], dres, f32)
        dzb, dres, dlg[1], dlb[1] = ln_bwd(f"ln2_bwd_{l}", dy, sv["z2"], vec(lng[l, 1]), 1.0)
        dyp = mm_nt_res(f"mix_dpool_{l}", [dzb], [wout_pool], [l], None, bf16)
        do = mm_nt_res(f"mix_do_{l}", [dzb], [wout_mla], [l], None, bf16)
        dwo_p = mm_tn(f"mix_dwout_pool_{l}", sv["ypool"], dzb)
        dwo_m = mm_tn(f"mix_dwout_mla_{l}", sv["o"], dzb)
        small["wout"][l] = jnp.concatenate([dwo_p, dwo_m], axis=0)
        dq, dk, dv = mla_attn_bwd(f"mla_bwd_{l}", sv["q"], sv["k"], sv["v"], sv["o"], do, sv["lse"])
        dqe, dkv, dh_rest, dgq, dgkv = mix_post_bwd(f"mix_post_bwd_{l}", dq, dk, dv, wq_ext, wukv, l, sv["cq"], sv["ckv"],
                                                     vec(q_norm_g[l]), vec(kv_norm_g[l]), cs)
        du, dyw, dscale = pool_bwd(f"pool_bwd_{l}", dyp, sv["dpool"], wbd[l], vec(pool_scale[l]))
        dwq_e = mm_tn(f"mix_dwuq_{l}", sv["cqn"], dqe).reshape(Q_LORA, MLA_HEADS, 256)
        small["wuq"][l] = jnp.concatenate(
            [dwq_e[..., :QK_NOPE], _unswap_add(dwq_e[..., QK_NOPE:QK_NOPE + QK_ROPE], dwq_e[..., QK_NOPE + QK_ROPE:])],
            axis=-1).reshape(Q_LORA, MLA_HEADS * (QK_NOPE + QK_ROPE))
        small["wukv"][l] = mm_tn(f"mix_dwukv_{l}", sv["ckvn"], dkv)
        dwbd = mm_tn(f"pool_dw_{l}", sv["dpool"], dyw)
        small["pool_w"][l] = jnp.stack([dwbd[64 * gi:64 * gi + 64, 64 * gi:64 * gi + 64] for gi in range(4)])
        small["pool_scale"][l], small["gq"][l], small["gkv"][l] = dscale[0], dgq[0], dgkv[0]
        dh_ext = jnp.concatenate([du, dh_rest], axis=1)
        dwin_e = mm_tn(f"mix_dwin_{l}", sv["x1b"], dh_ext)
        small["win"][l] = jnp.concatenate(
            [dwin_e[:, :D_IN - QK_ROPE], _unswap_add(dwin_e[:, D_IN - QK_ROPE:D_IN], dwin_e[:, D_IN:])], axis=-1)
        dy = mm_nt_res(f"mix_dx_{l}", [dh_ext], [win_ext], [l], dres, f32)
        dzb, dres, dlg[0], dlb[0] = ln_bwd(f"ln1_bwd_{l}", dy, sv["z1"], vec(lng[l, 0]), 0.5)
        dh = ffn_bwd_da(f"ffn1_bwd_da_{l}", dzb, f1w2, l, sv["gate1"], sv["up1"])
        G["f1w2"] = mm_tn(f"ffn1_dw2_{l}", sv["act1"], dzb, "nat", l, G["f1w2"])
        G["f1w13"] = mm_tn(f"ffn1_dw13_{l}", sv["x0b"], dh, "shard", l, G["f1w13"])
        dy = ffn_dx(f"ffn1_dx_{l}", dh, g_f1w13, l, dres)
        small["lng"][l] = jnp.concatenate(dlg, axis=0)
        small["lnb"][l] = jnp.concatenate(dlb, axis=0)
    grad_x = dy[None]

    row_shards = lambda a, K: a.reshape(L, N_CHIPS, K // N_CHIPS, a.shape[-1])
    g_list = [G["f1w13"], row_shards(G["f1w2"], D_FF),
              jnp.stack(small["win"]).reshape(L, N_CHIPS, D_MODEL // N_CHIPS, D_IN),
              _to_col_shards(jnp.stack(small["wuq"])), _to_col_shards(jnp.stack(small["wukv"])),
              jnp.stack(small["wout"]).reshape(L, N_CHIPS, D_MODEL // N_CHIPS, D_MODEL),
              row_shards(G["mwq"], D_MODEL), G["mwkv"], row_shards(G["mwo"], D_MODEL),
              G["f2w13"], row_shards(G["f2w2"], D_FF)]
    big_grads = reduce_grads(g_list)

    rep = [jnp.stack(small["pool_w"]).reshape(-1), jnp.stack(small["pool_scale"]).reshape(-1),
           jnp.stack(small["gq"]).reshape(-1), jnp.stack(small["gkv"]).reshape(-1),
           jnp.stack(small["lng"]).reshape(-1), jnp.stack(small["lnb"]).reshape(-1)]
    sizes = [r.shape[0] for r in rep]
    packed = jnp.concatenate(rep)
    pad = (-packed.shape[0]) % 1024
    tot = allsum_small("allsum_small_grads", jnp.pad(packed, (0, pad)).reshape(-1, 128)).reshape(-1)
    offs = [0]
    for s_ in sizes:
        offs.append(offs[-1] + s_)
    parts = [tot[offs[i]:offs[i + 1]] for i in range(len(sizes))]
    g_pool_w = parts[0].reshape(pool_w.shape)
    g_pool_scale = parts[1].reshape(pool_scale.shape)
    g_gq = parts[2].reshape(q_norm_g.shape)
    g_gkv = parts[3].reshape(kv_norm_g.shape)
    shard_cols = lambda a: lax.dynamic_slice_in_dim(a.reshape(L, 4, D_MODEL), chip * (D_MODEL // N_CHIPS),
                                                    D_MODEL // N_CHIPS, axis=2)
    g_lng, g_lnb = shard_cols(parts[4]), shard_cols(parts[5])

    out_names = ("lng", "lnb", "f1w13", "f1w2", "win", "pool_w", "pool_scale", "gq", "wuq", "gkv", "wukv", "wout", "mwq",
                 "mwkv", "mwo", "f2w13", "f2w2")
    big = dict(lng=g_lng, lnb=g_lnb, pool_w=g_pool_w, pool_scale=g_pool_scale, gq=g_gq, gkv=g_gkv)
    late = ("f1w13", "f1w2")
    held = ("f2w13", "f2w2")
    ws = [ln_g, ln_b, ffn1_w13, ffn1_w2, w_in, pool_w, pool_scale, q_norm_g, w_uq, kv_norm_g, w_ukv, w_out, mem_wq,
          mem_wkv, mem_wo, ffn2_w13, ffn2_w2]
    ms = [m_ln_g, m_ln_b, m_ffn1_w13, m_ffn1_w2, m_w_in, m_pool_w, m_pool_scale, m_q_norm_g, m_w_uq, m_kv_norm_g, m_w_ukv,
          m_w_out, m_mem_wq, m_mem_wkv, m_mem_wo, m_ffn2_w13, m_ffn2_w2]
    vs = [v_ln_g, v_ln_b, v_ffn1_w13, v_ffn1_w2, v_w_in, v_pool_w, v_pool_scale, v_q_norm_g, v_w_uq, v_kv_norm_g, v_w_ukv,
          v_w_out, v_mem_wq, v_mem_wkv, v_mem_wo, v_ffn2_w13, v_ffn2_w2]
    res = {}

    def update(n, deps=()):
        a = out_names.index(n)
        res[a] = adamw(f"adamw_{a}", ws[a], big[n].reshape(ws[a].shape), ms[a], vs[a], deps)
        return res[a][0]

    small_done = tuple(update(n) for n in ("lng", "lnb", "pool_w", "pool_scale", "gq", "gkv"))
    big.update(zip(rest_names, pair_share_wait("pair_share_wait_a", share_a, small_done)))
    first_done = tuple(update(n) for n in rest_names if n not in held)
    sums_b = red_end("b0", st_cb, 0, [sums1[n] for n in late], first_done)
    share_b, tok_b = share_start("b", late, sums_b)
    held_done = tuple(update(n, (tok_b,)) for n in held)
    big.update(zip(late, pair_share_wait("pair_share_wait_b", share_b, held_done)))
    for n in late:
        update(n)
    order = range(len(out_names))
    grads = [big[n].reshape(w_.shape) for n, w_ in zip(out_names, ws)]
    return (loss, grad_x, *grads, *[res[a][0] for a in order], *[res[a][1] for a in order], *[res[a][2] for a in order])


_HBM_SPEC = pl.BlockSpec(memory_space=pltpu.HBM)
_SEM_SPEC = pl.BlockSpec(memory_space=pltpu.SEMAPHORE)
_ANY_SPEC = pl.BlockSpec(memory_space=pl.ANY)
_DATAFLOW = pltpu.SideEffectType.DATAFLOW_SIDE_EFFECTING


def _split_call(name, body_fn, bufs, sems_in, sems_out_sizes, after):
    nb, ni, no = len(bufs), len(sems_in), len(sems_out_sizes)
    afters = () if after is None else tuple(after) if isinstance(after, (tuple, list)) else (after,)

    def body(*refs):
        k = nb + ni + len(afters)
        body_fn(refs[:nb], refs[nb:nb + ni], refs[k:k + no])
        refs[-1][...] = jnp.zeros((8, 128), f32)

    outs = pl.pallas_call(
        body, name=name,
        in_specs=[_HBM_SPEC] * nb + [_SEM_SPEC] * ni + [_ANY_SPEC] * len(afters),
        out_specs=[_SEM_SPEC] * no + [_HBM_SPEC] * nb + [pl.BlockSpec(memory_space=pltpu.VMEM)],
        out_shape=[pltpu.SemaphoreType.DMA((s,)) for s in sems_out_sizes]
        + [pltpu.HBM(b.shape, b.dtype) for b in bufs] + [SDS((8, 128), f32)],
        input_output_aliases={i: no + i for i in range(nb)},
        compiler_params=pltpu.CompilerParams(has_side_effects=_DATAFLOW),
    )(*[pltpu.with_memory_space_constraint(b, pltpu.HBM) for b in bufs], *sems_in, *afters)
    return list(outs[no:no + nb]), list(outs[:no]), outs[-1]


def _rcopy(src, dst, ssem, rsem, to):
    return pltpu.make_async_remote_copy(src_ref=src, dst_ref=dst, send_sem=ssem, recv_sem=rsem, device_id=to,
                                        device_id_type=MESH)


def gather_start(name, groups, after):
    flat = [b for bufs, _ in groups for b in bufs]
    sizes = [3 * len(bufs) for bufs, _ in groups for _ in range(2)]

    def body_fn(b_in, s_in, s_out):
        x, y, c = _me()
        q = 2 * x + y
        chips = _other_chips(x, y)
        pos = 0
        for gi, (bufs, owner) in enumerate(groups):
            refs = b_in[pos:pos + len(bufs)]
            pos += len(bufs)

            @pl.when(c == owner)
            def _(refs=refs, send=s_out[2 * gi], recv=s_out[2 * gi + 1]):
                for a, r in enumerate(refs):
                    for k, (cx, cy) in enumerate(chips):
                        _rcopy(r.at[q], r.at[q], send.at[3 * a + k], recv.at[3 * a + k], (cx, cy, c)).start()

    outs, sems, token = _split_call(name, body_fn, flat, [], sizes, after)
    res, pos = [], 0
    for gi, (bufs, owner) in enumerate(groups):
        res.append((outs[pos:pos + len(bufs)], sems[2 * gi], sems[2 * gi + 1], owner))
        pos += len(bufs)
    return res, token


def gather_forward(name, grp, after):
    bufs, send, recv, owner = grp
    n3 = 3 * len(bufs)

    def body_fn(b_in, s_in, s_out):
        x, y, c = _me()
        q = 2 * x + y
        sibling = (x, y, 1 - c)
        chips = _other_chips(x, y)

        @pl.when(c == owner)
        def _():
            for a, r in enumerate(b_in):
                for k, (cx, cy) in enumerate(chips):
                    i = 3 * a + k
                    land = r.at[2 * cx + cy]
                    _rcopy(r.at[q], r.at[q], s_in[0].at[i], s_in[1].at[i], (cx, cy, c)).wait_send()
                    _rcopy(land, land, s_in[0].at[i], s_in[1].at[i], (cx, cy, c)).wait_recv()
                    _rcopy(land, land, s_out[0].at[i], s_out[1].at[i], sibling).start()

    outs, sems, token = _split_call(name, body_fn, bufs, [send, recv], [n3, n3], after)
    return (outs, sems[0], sems[1], owner), token


def gather_finish(name, grp, after):
    bufs, fsend, frecv, owner = grp

    def body_fn(b_in, s_in, s_out):
        x, y, c = _me()
        sibling = (x, y, 1 - c)
        chips = _other_chips(x, y)

        def each(wait):
            for a, r in enumerate(b_in):
                for k, (cx, cy) in enumerate(chips):
                    land = r.at[2 * cx + cy]
                    wait(_rcopy(land, land, s_in[0].at[3 * a + k], s_in[1].at[3 * a + k], sibling))

        @pl.when(c == owner)
        def _():
            each(lambda cp: cp.wait_send())

        @pl.when(c != owner)
        def _():
            each(lambda cp: cp.wait_recv())

    outs, _, _ = _split_call(name, body_fn, bufs, [fsend, frecv], [], after)
    return outs


def _by_owner(owners):
    return [[a for a, o_ in enumerate(owners) if o_ == o] for o in range(2)]


def pair_send_start(name, gs, owners, after):
    n = len(gs)
    lands = [lax.empty(g.shape, g.dtype) for g in gs]

    def body_fn(b_in, s_in, s_out):
        x, y, c = _me()
        for o, idx in enumerate(_by_owner(owners)):
            @pl.when(c == 1 - o)
            def _(o=o, idx=idx):
                for a in idx:
                    _rcopy(b_in[a], b_in[n + a], s_out[0].at[a], s_out[1].at[a], (x, y, o)).start()

    outs, sems, token = _split_call(name, body_fn, list(gs) + lands, [], [n, n], after)
    return (outs[:n], outs[n:], sems[0], sems[1], owners), token


def pair_send_wait(name, st, after):
    gs, lands, send, recv, owners = st
    n = len(gs)

    def body_fn(b_in, s_in, s_out):
        x, y, c = _me()
        for o, idx in enumerate(_by_owner(owners)):
            @pl.when(c == 1 - o)
            def _(o=o, idx=idx):
                for a in idx:
                    _rcopy(b_in[a], b_in[n + a], s_in[0].at[a], s_in[1].at[a], (x, y, o)).wait_send()

            @pl.when(c == o)
            def _(o=o, idx=idx):
                for a in idx:
                    _rcopy(b_in[a], b_in[n + a], s_in[0].at[a], s_in[1].at[a], (x, y, 1 - o)).wait_recv()

    outs, _, _ = _split_call(name, body_fn, list(gs) + list(lands), [send, recv], [], after)
    return outs[:n], outs[n:]


def chip_exchange_start(name, psums, owners, after):
    n = len(psums)
    lands = [lax.empty((3,) + p.shape[1:], p.dtype) for p in psums]

    def body_fn(b_in, s_in, s_out):
        x, y, c = _me()
        chips = _other_chips(x, y)
        for o, idx in enumerate(_by_owner(owners)):
            @pl.when(c == o)
            def _(idx=idx):
                for a in idx:
                    for k, (cx, cy) in enumerate(chips):
                        _rcopy(b_in[a].at[2 * cx + cy], b_in[n + a].at[k], s_out[0].at[3 * a + k],
                               s_out[1].at[3 * a + k], (cx, cy, c)).start()

    outs, sems, token = _split_call(name, body_fn, list(psums) + lands, [], [3 * n, 3 * n], after)
    return (outs[:n], outs[n:], sems[0], sems[1], owners), token


def chip_exchange_wait(name, st, after):
    psums, lands, send, recv, owners = st
    n = len(psums)

    def body_fn(b_in, s_in, s_out):
        x, y, c = _me()
        chips = _other_chips(x, y)
        for o, idx in enumerate(_by_owner(owners)):
            @pl.when(c == o)
            def _(idx=idx):
                for a in idx:
                    for k, (cx, cy) in enumerate(chips):
                        cp = _rcopy(b_in[a].at[2 * cx + cy], b_in[n + a].at[k], s_in[0].at[3 * a + k],
                                    s_in[1].at[3 * a + k], (cx, cy, c))
                        cp.wait_send()
                        cp.wait_recv()

    outs, _, _ = _split_call(name, body_fn, list(psums) + list(lands), [send, recv], [], after)
    return outs[:n], outs[n:]


def pair_sum(name, g, recv, flag):
    shape = g.shape
    cols = shape[-1]
    rows = math.prod(shape[:-1])
    tr = _row_tile(rows, cols, target=4 * 2**20)

    def body(f_ref, g_ref, r_ref, o_ref):
        o_ref[...] = (g_ref[...] + r_ref[...]).astype(bf16)

    blk = pl.BlockSpec((tr, cols), lambda i, f_ref: (i * f_ref[0], 0))
    out = pl.pallas_call(
        body, name=name,
        grid_spec=pltpu.PrefetchScalarGridSpec(num_scalar_prefetch=1, grid=(rows // tr,), in_specs=[blk, blk],
                                               out_specs=blk),
        out_shape=SDS((rows, cols), bf16), compiler_params=_cp("arbitrary"),
    )(flag, g.reshape(rows, cols), recv.reshape(rows, cols))
    return out.reshape(shape)


def chip_sum(name, psum, recv, qf_arr, layer, prev):
    shard = psum.shape[1:]
    cols = shard[-1]
    rows = math.prod(shard[:-1])
    tr = _row_tile(rows, cols, target=4 * 2**20)

    def body(qf_ref, p_ref, r_ref, *rest):
        rest[-1][0] = ((p_ref[0].astype(f32) + r_ref[0].astype(f32)) + r_ref[1].astype(f32)) + r_ref[2].astype(f32)

    in_specs = [pl.BlockSpec((1, tr, cols), lambda i, qf: (qf[0], i * qf[1], 0)),
                pl.BlockSpec((3, tr, cols), lambda i, qf: (0, i * qf[1], 0))]
    args = [qf_arr, psum.reshape(N_CHIPS, rows, cols), recv.reshape(3, rows, cols)]
    aliases = {}
    if prev is not None:
        in_specs.append(pl.BlockSpec(memory_space=pl.ANY))
        args.append(prev.reshape(DEPTH, rows, cols))
        aliases = {3: 0}
    out = pl.pallas_call(
        body, name=name,
        grid_spec=pltpu.PrefetchScalarGridSpec(
            num_scalar_prefetch=1, grid=(rows // tr,), in_specs=in_specs,
            out_specs=pl.BlockSpec((1, tr, cols), lambda i, qf: (layer, i * qf[1], 0))),
        out_shape=SDS((DEPTH, rows, cols), f32), input_output_aliases=aliases, compiler_params=_cp("arbitrary"),
    )(*args)
    return out.reshape((DEPTH,) + shard)


W_NAMES = ("f1w13", "f1w2", "win", "wuq", "wukv", "wout", "mwq", "mwkv", "mwo", "f2w13", "f2w2")
MIX_NAMES = ("win", "wuq", "wukv")
MID_NAMES = ("wout", "mwq", "mwkv", "mwo")
FFN2_NAMES = ("f2w13", "f2w2")
REDUCER = (dict(f1w13=0, f1w2=1, f2w13=0, win=0, wuq=0, wukv=0, f2w2=1, mwkv=1, wout=1, mwq=1, mwo=1),
           dict(f1w13=0, f2w2=0, mwkv=0, wout=0, f2w13=1, f1w2=1, mwq=1, mwo=1, win=1, wuq=1, wukv=1))


def kernel(x, mem, positions, ln_g, ln_b, ffn1_w13, ffn1_w2, w_in, pool_w, pool_scale, q_norm_g, w_uq, kv_norm_g, w_ukv, w_out, mem_wq, mem_wkv, mem_wo, ffn2_w13, ffn2_w2, loss_target, m_ln_g, m_ln_b, m_ffn1_w13, m_ffn1_w2, m_w_in, m_pool_w, m_pool_scale, m_q_norm_g, m_w_uq, m_kv_norm_g, m_w_ukv, m_w_out, m_mem_wq, m_mem_wkv, m_mem_wo, m_ffn2_w13, m_ffn2_w2, v_ln_g, v_ln_b, v_ffn1_w13, v_ffn1_w2, v_w_in, v_pool_w, v_pool_scale, v_q_norm_g, v_w_uq, v_kv_norm_g, v_w_ukv, v_w_out, v_mem_wq, v_mem_wkv, v_mem_wo, v_ffn2_w13, v_ffn2_w2):
    L = DEPTH
    qx, qy, _ = _me()
    chip = 2 * qx + qy
    vec = lambda a: a.reshape(1, -1)

    shards = dict(zip(W_NAMES, (ffn1_w13, ffn1_w2, w_in, w_uq, w_ukv, w_out, mem_wq, mem_wkv, mem_wo, ffn2_w13, ffn2_w2)))

    def place(sh, slot):
        return lax.dynamic_update_slice(lax.empty((N_CHIPS,) + sh.shape, bf16), sh.astype(bf16)[None],
                                        (slot,) + (0,) * sh.ndim)

    first = ("f1w13", "f1w2")
    bufs = [dict(), dict()]
    for n in first:
        bufs[0][n] = place(shards[n][0], chip)
    gw = [dict(), dict()]
    (g0,), tok = gather_start("gather_a_start", [([bufs[0][n] for n in first], 0)], None)
    chip_then = chip + tok[0, 0].astype(jnp.int32)
    for l in range(L):
        for n in W_NAMES:
            if n not in bufs[l]:
                bufs[l][n] = place(shards[n][l], chip_then)
    others = tuple(bufs[l][n] for l in range(L) for n in W_NAMES if (l, n) not in ((0, first[0]), (0, first[1])))
    g0, tok = gather_forward("gather_a_forward", g0, others)

    ln_pad = jnp.zeros((2, L, 4, N_CHIPS, D_MODEL // N_CHIPS), f32)
    ln_pad = lax.dynamic_update_slice(ln_pad, jnp.stack([ln_g, ln_b])[:, :, :, None, :], (0, 0, 0, chip, 0))
    ln_sum = allsum_small("allsum_ln", ln_pad.reshape(-1, 128), (tok,))
    ln_full = (ln_sum * 0.5).reshape(2, L, 4, D_MODEL)
    lng, lnb = ln_full[0], ln_full[1]

    gw[0]["f1w13"], gw[0]["f1w2"] = gather_finish("gather_a_finish", g0, ln_sum)
    (g_mix, g_mid, g_ffn2, g_l1), tok_b = gather_start(
        "gather_b_start",
        [([bufs[0][n] for n in MIX_NAMES], 0), ([bufs[0][n] for n in MID_NAMES], 0), ([bufs[0][n] for n in FFN2_NAMES], 0),
         ([bufs[1][n] for n in W_NAMES], 1)], ln_sum)

    half = QK_ROPE // 2
    inv_freq = ROPE_BASE ** (-jnp.arange(half, dtype=f32) / half)
    ang = positions[0].astype(f32)[:, None] * inv_freq
    cos, sin = jnp.cos(ang), jnp.sin(ang)
    cs = jnp.concatenate([cos, cos, sin, sin], axis=-1)

    memb = mem[0].astype(bf16)
    xf = x[0]
    xb = xf.astype(bf16)
    dep = (tok_b,)

    saved, W = [], [None, None]
    for l in range(L):
        sv = {}
        if l == 1:
            gl1 = gather_finish("gather_l1_finish", g_l1, xb)
            gw[1] = dict(zip(W_NAMES, gl1))
        sv["x0b"] = xb
        f1w13 = gw[l]["f1w13"][None]
        gate, up, act = ffn_up(f"ffn1_up_{l}", xb, f1w13, 0, dep)
        dep = ()
        if l == 0:
            g_mix, _ = gather_forward("gather_mix_forward", g_mix, act)
        z1, x1f, x1b = proj_res_ln(f"ffn1_down_{l}", [act], [gw[l]["f1w2"].reshape(1, D_FF, D_MODEL)], [0], xf,
                                   vec(lng[l, 0]), vec(lnb[l, 0]), 0.5)
        sv.update(gate1=gate, up1=up, act1=act, z1=z1, x1b=x1b)
        if l == 0:
            gw[0].update(zip(MIX_NAMES, gather_finish("gather_mix_finish", g_mix, x1b)))
            g_mid, _ = gather_forward("gather_mid_forward", g_mid, x1b)
        win = gw[l]["win"].reshape(D_MODEL, D_IN)
        win_ext = jnp.concatenate([win, _swap_half(win[:, D_IN - QK_ROPE:])], axis=-1)[None]
        wuq = _from_col_shards(gw[l]["wuq"]).reshape(Q_LORA, MLA_HEADS, QK_NOPE + QK_ROPE)
        wq_ext = jnp.concatenate([wuq, _swap_half(wuq[..., QK_NOPE:])], axis=-1).transpose(1, 0, 2)[None]
        wukv = _from_col_shards(gw[l]["wukv"])[None]
        wbd = _block_diag(pool_w[l][None].astype(bf16))[0]
        u, cq, ckv, cqn, ckvn, q, k, v = mix_pre(f"mix_pre_{l}", x1b, win_ext, wq_ext, wukv, 0,
                                                   vec(q_norm_g[l]), vec(kv_norm_g[l]), cs)
        dpool, ypool = pool_fwd(f"pool_fwd_{l}", u, wbd, vec(pool_scale[l]))
        o, lse = mla_attn_fwd(f"mla_fwd_{l}", q, k, v)
        if l == 0:
            gw[0].update(zip(MID_NAMES, gather_finish("gather_mid_finish", g_mid, o)))
            g_ffn2, tok_f = gather_forward("gather_ffn2_forward", g_ffn2, o)
            g_l1, tok_l = gather_forward("gather_l1_forward", g_l1, o)
            dep = (tok_f, tok_l)
        wout = gw[l]["wout"].reshape(D_MODEL, D_MODEL)
        wout_pool, wout_mla = wout[None, :POOL_WIDTH], wout[None, POOL_WIDTH:]
        mwq = gw[l]["mwq"].reshape(1, D_MODEL, D_MODEL)
        mwo = gw[l]["mwo"].reshape(1, D_MODEL, D_MODEL)
        mwkv = gw[l]["mwkv"][None]
        z2, x2f, x2b = proj_res_ln(f"mix_out_{l}", [ypool, o], [wout_pool, wout_mla], [0, 0], x1f,
                                   vec(lng[l, 1]), vec(lnb[l, 1]), 1.0, dep)
        dep = ()
        sv.update(cq=cq, ckv=ckv, cqn=cqn, ckvn=ckvn, q=q, k=k, v=v, dpool=dpool, ypool=ypool, o=o, lse=lse, z2=z2, x2b=x2b)
        kvm = mm_nn_shard(f"mem_kv_{l}", memb, mwkv, 0)
        cq_, co_, z3, x3f, x3b = cross_fwd(f"cross_fwd_{l}", x2b, x2f, mwq, mwo, 0, kvm, vec(lng[l, 2]), vec(lnb[l, 2]))
        sv.update(kvm=kvm, crq=cq_, cro=co_, z3=z3, x3b=x3b)
        if l == 0:
            gw[0].update(zip(FFN2_NAMES, gather_finish("gather_ffn2_finish", g_ffn2, x3b)))
        f2w13 = gw[l]["f2w13"][None]
        f2w2 = gw[l]["f2w2"].reshape(1, D_FF, D_MODEL)
        gate, up, act = ffn_up(f"ffn2_up_{l}", x3b, f2w13, 0)
        z4, xf, xb = proj_res_ln(f"ffn2_down_{l}", [act], [f2w2], [0], x3f, vec(lng[l, 3]), vec(lnb[l, 3]), 0.5)
        sv.update(gate2=gate, up2=up, act2=act, z4=z4)
        W[l] = dict(f1w13=f1w13, f1w2=gw[l]["f1w2"].reshape(1, D_FF, D_MODEL), win_ext=win_ext, wq_ext=wq_ext, wukv=wukv,
                    wbd=wbd, wout_pool=wout_pool, wout_mla=wout_mla, mwq=mwq, mwo=mwo, f2w13=f2w13, f2w2=f2w2)
        saved.append(sv)

    dln = {}
    dzb, dres, *dln[L - 1, 3], loss_blk = loss_grad("loss_grad", xf, loss_target[0],
                                                   (saved[L - 1]["z4"], vec(lng[L - 1, 3]), 0.5))
    loss = lax.psum(loss_blk[0, 0], ("x", "y", "c"))

    row_shards = lambda a: a.reshape(N_CHIPS, a.shape[0] // N_CHIPS, a.shape[1])
    small = {k_: [None] * L for k_ in ("pool_w", "pool_scale", "gq", "gkv", "lng", "lnb")}
    rest_names = [n for n in W_NAMES if n not in ("f1w13", "f1w2")]
    core = lax.axis_index("c")
    flags = [jnp.reshape(core == o, (1,)).astype(jnp.int32) for o in range(2)]
    qfs = [jnp.stack([chip, (core == o).astype(jnp.int32)]).astype(jnp.int32) for o in range(2)]

    def red_begin(tag, names, gs, layer):
        owners = [REDUCER[layer][n] for n in names]
        st, tok_ = pair_send_start(f"pair_send_start_{tag}", gs, owners, None)
        return (st, owners), tok_

    def red_mid(tag, sto, after):
        st, owners = sto
        gs_, lands_ = pair_send_wait(f"pair_send_wait_{tag}", st, after)
        ps = [pair_sum(f"pair_sum_{tag}_{a}", g_, r_, flags[o]) for a, (g_, r_, o) in enumerate(zip(gs_, lands_, owners))]
        st, tok_ = chip_exchange_start(f"chip_exchange_start_{tag}", ps, owners, None)
        return (st, owners), tok_

    def red_end(tag, sto, layer, prevs, after):
        st, owners = sto
        ps, lands_ = chip_exchange_wait(f"chip_exchange_wait_{tag}", st, after)
        return [chip_sum(f"chip_sum_{tag}_{a}", p_, r_, qfs[o], layer, s_)
                for a, (p_, r_, s_, o) in enumerate(zip(ps, lands_, prevs, owners))]

    def share_start(tag, names, sums_):
        return pair_share_start(f"pair_share_start_{tag}", sums_, [(REDUCER[0][n], REDUCER[1][n]) for n in names], None)

    st_p1 = st_c1 = st_pa = st_ca = None
    for l in reversed(range(L)):
        sv, w = saved[l], W[l]
        g = {}
        dh = ffn_bwd_da(f"ffn2_bwd_da_{l}", dzb, w["f2w2"], 0, sv["gate2"], sv["up2"], dep)
        dep = ()
        g["f2w2"] = row_shards(mm_tn(f"ffn2_dw2_{l}", sv["act2"], dzb))
        g["f2w13"] = mm_tn(f"ffn2_dw13_{l}", sv["x3b"], dh, True)
        dzb, dres, *dln[l, 2] = ffn_dx(f"ffn2_dx_{l}", dh, w["f2w13"], 0, dres, (sv["z3"], vec(lng[l, 2]), 1.0))
        if l == 0:
            st_c1, tok = red_mid("l1", st_p1, dzb)
            dep = (tok, g["f2w2"], g["f2w13"])
        dqc, dkvm = cross_bwd(f"cross_bwd_{l}", dzb, w["mwo"], 0, sv["crq"], sv["kvm"], dep)
        dep = ()
        g["mwo"] = row_shards(mm_tn(f"cross_dwo_{l}", sv["cro"], dzb))
        g["mwq"] = row_shards(mm_tn(f"cross_dwq_{l}", sv["x2b"], dqc))
        g["mwkv"] = mm_tn(f"cross_dwkv_{l}", memb, dkvm, True)
        dzb, dres, *dln[l, 1] = mm_nt_res(f"cross_dx_{l}", [dqc], [w["mwq"]], [0], dres, f32,
                                          (sv["z2"], vec(lng[l, 1]), 1.0))
        dyp = mm_nt_res(f"mix_dpool_{l}", [dzb], [w["wout_pool"]], [0], None, bf16)
        do = mm_nt_res(f"mix_do_{l}", [dzb], [w["wout_mla"]], [0], None, bf16)
        dwo_p = mm_tn(f"mix_dwout_pool_{l}", sv["ypool"], dzb)
        dwo_m = mm_tn(f"mix_dwout_mla_{l}", sv["o"], dzb)
        g["wout"] = row_shards(jnp.concatenate([dwo_p, dwo_m], axis=0))
        dq, dk, dv = mla_attn_bwd(f"mla_bwd_{l}", sv["q"], sv["k"], sv["v"], sv["o"], do, sv["lse"])
        dqe, dkv, dh_rest, dgq, dgkv = mix_post_bwd(f"mix_post_bwd_{l}", dq, dk, dv, w["wq_ext"], w["wukv"], 0, sv["cq"],
                                                     sv["ckv"], vec(q_norm_g[l]), vec(kv_norm_g[l]), cs)
        du, dyw, dscale = pool_bwd(f"pool_bwd_{l}", dyp, sv["dpool"], w["wbd"], vec(pool_scale[l]))
        dwq_e = mm_tn(f"mix_dwuq_{l}", sv["cqn"], dqe).reshape(Q_LORA, MLA_HEADS, 256)
        g["wuq"] = _to_col_shards(jnp.concatenate(
            [dwq_e[..., :QK_NOPE], _unswap_add(dwq_e[..., QK_NOPE:QK_NOPE + QK_ROPE], dwq_e[..., QK_NOPE + QK_ROPE:])],
            axis=-1).reshape(Q_LORA, MLA_HEADS * (QK_NOPE + QK_ROPE)))
        g["wukv"] = _to_col_shards(mm_tn(f"mix_dwukv_{l}", sv["ckvn"], dkv))
        dwbd = mm_tn(f"pool_dw_{l}", sv["dpool"], dyw)
        small["pool_w"][l] = jnp.stack([dwbd[64 * gi:64 * gi + 64, 64 * gi:64 * gi + 64] for gi in range(4)])
        small["pool_scale"][l], small["gq"][l], small["gkv"][l] = dscale[0], dgq[0], dgkv[0]
        dh_ext = jnp.concatenate([du, dh_rest], axis=1)
        dwin_e = mm_tn(f"mix_dwin_{l}", sv["x1b"], dh_ext)
        g["win"] = row_shards(jnp.concatenate(
            [dwin_e[:, :D_IN - QK_ROPE], _unswap_add(dwin_e[:, D_IN - QK_ROPE:D_IN], dwin_e[:, D_IN:])], axis=-1))
        dzb, dres, *dln[l, 0] = mm_nt_res(f"mix_dx_{l}", [dh_ext], [w["win_ext"]], [0], dres, f32,
                                          (sv["z1"], vec(lng[l, 0]), 0.5))
        if l == 0:
            st_pa, tok = red_begin("a0", rest_names, [g[n] for n in rest_names], 0)
            dep = (tok,)
        dh = ffn_bwd_da(f"ffn1_bwd_da_{l}", dzb, w["f1w2"], 0, sv["gate1"], sv["up1"], dep)
        dep = ()
        if l == 0:
            grad_x = ffn_dx(f"ffn1_dx_{l}", dh, w["f1w13"], 0, dres)[None]
            st_ca, tok = red_mid("a0", st_pa, grad_x)
            dep = (tok,)
        else:
            below = ffn_dx(f"ffn1_dx_{l}", dh, w["f1w13"], 0, dres, (saved[l - 1]["z4"], vec(lng[l - 1, 3]), 0.5))
            dln[l - 1, 3] = below[2:]
        g["f1w2"] = row_shards(mm_tn(f"ffn1_dw2_{l}", sv["act1"], dzb, False, dep))
        g["f1w13"] = mm_tn(f"ffn1_dw13_{l}", sv["x0b"], dh, True, dep)
        dep = ()
        if l > 0:
            dzb, dres = below[:2]
        if l == 1:
            st_p1, tok = red_begin("l1", W_NAMES, [g[n] for n in W_NAMES], 1)
            dep = (tok,)
    for l in range(L):
        small["lng"][l] = jnp.concatenate([dln[l, k][0] for k in range(4)], axis=0)
        small["lnb"][l] = jnp.concatenate([dln[l, k][1] for k in range(4)], axis=0)

    st_pb, _ = red_begin("b0", ("f1w13", "f1w2"), [g["f1w13"], g["f1w2"]], 0)
    sums1 = dict(zip(W_NAMES, red_end("l1", st_c1, 1, [None] * len(W_NAMES), g["f1w13"])))
    st_cb, tok = red_mid("b0", st_pb, tuple(sums1.values()))
    sums0 = red_end("a0", st_ca, 0, [sums1[n] for n in rest_names], tok)
    share_a, tok_a = share_start("a", rest_names, sums0)

    rep = [jnp.stack(small["pool_w"]).reshape(-1), jnp.stack(small["pool_scale"]).reshape(-1),
           jnp.stack(small["gq"]).reshape(-1), jnp.stack(small["gkv"]).reshape(-1),
           jnp.stack(small["lng"]).reshape(-1), jnp.stack(small["lnb"]).reshape(-1)]
    sizes = [r.shape[0] for r in rep]
    packed = jnp.concatenate(rep)
    pad = (-packed.shape[0]) % 1024
    tot = allsum_small("allsum_small_grads", jnp.pad(packed, (0, pad)).reshape(-1, 128), (tok_a,)).reshape(-1)
    offs = [0]
    for s_ in sizes:
        offs.append(offs[-1] + s_)
    parts = [tot[offs[i]:offs[i + 1]] for i in range(len(sizes))]
    g_pool_w = parts[0].reshape(pool_w.shape)
    g_pool_scale = parts[1].reshape(pool_scale.shape)
    g_gq = parts[2].reshape(q_norm_g.shape)
    g_gkv = parts[3].reshape(kv_norm_g.shape)
    shard_cols = lambda a: lax.dynamic_slice_in_dim(a.reshape(L, 4, D_MODEL), chip * (D_MODEL // N_CHIPS),
                                                    D_MODEL // N_CHIPS, axis=2)
    g_lng, g_lnb = shard_cols(parts[4]), shard_cols(parts[5])

    out_names = ("lng", "lnb", "f1w13", "f1w2", "win", "pool_w", "pool_scale", "gq", "wuq", "gkv", "wukv", "wout", "mwq",
                 "mwkv", "mwo", "f2w13", "f2w2")
    big = dict(lng=g_lng, lnb=g_lnb, pool_w=g_pool_w, pool_scale=g_pool_scale, gq=g_gq, gkv=g_gkv)
    late = ("f1w13", "f1w2")
    held = ("f2w13", "f2w2")
    ws = [ln_g, ln_b, ffn1_w13, ffn1_w2, w_in, pool_w, pool_scale, q_norm_g, w_uq, kv_norm_g, w_ukv, w_out, mem_wq,
          mem_wkv, mem_wo, ffn2_w13, ffn2_w2]
    ms = [m_ln_g, m_ln_b, m_ffn1_w13, m_ffn1_w2, m_w_in, m_pool_w, m_pool_scale, m_q_norm_g, m_w_uq, m_kv_norm_g, m_w_ukv,
          m_w_out, m_mem_wq, m_mem_wkv, m_mem_wo, m_ffn2_w13, m_ffn2_w2]
    vs = [v_ln_g, v_ln_b, v_ffn1_w13, v_ffn1_w2, v_w_in, v_pool_w, v_pool_scale, v_q_norm_g, v_w_uq, v_kv_norm_g, v_w_ukv,
          v_w_out, v_mem_wq, v_mem_wkv, v_mem_wo, v_ffn2_w13, v_ffn2_w2]
    res = {}

    def update(n, deps=()):
        a = out_names.index(n)
        res[a] = adamw(f"adamw_{a}", ws[a], big[n].reshape(ws[a].shape), ms[a], vs[a], deps)
        return res[a][0]

    small_done = tuple(update(n) for n in ("lng", "lnb", "pool_w", "pool_scale", "gq", "gkv"))
    big.update(zip(rest_names, pair_share_wait("pair_share_wait_a", share_a, small_done)))
    first_done = tuple(update(n) for n in rest_names if n not in held)
    sums_b = red_end("b0", st_cb, 0, [sums1[n] for n in late], first_done)
    share_b, tok_b = share_start("b", late, sums_b)
    held_done = tuple(update(n, (tok_b,)) for n in held)
    big.update(zip(late, pair_share_wait("pair_share_wait_b", share_b, held_done)))
    for n in late:
        update(n)
    order = range(len(out_names))
    grads = [big[n].reshape(w_.shape) for n, w_ in zip(out_names, ws)]
    return (loss, grad_x, *grads, *[res[a][0] for a in order], *[res[a][1] for a in order], *[res[a][2] for a in order])
```

```python
import functools
import math

import jax
import jax.numpy as jnp
from jax import lax
from jax.experimental import pallas as pl
from jax.experimental.pallas import tpu as pltpu

f32 = jnp.float32
bf16 = jnp.bfloat16
SDS = jax.ShapeDtypeStruct
MESH = pl.DeviceIdType.MESH

D_MODEL = 1024
DEPTH = 2
N_MEM = 256
MEM_HEADS = 4
MEM_HEAD_DIM = D_MODEL // MEM_HEADS
POOL_WINDOWS = (2, 4, 8, 16)
POOL_WIDTH = 256
POOL_GROUP = 64
QK_NOPE = 128
QK_ROPE = 64
V_HEAD = 128
MLA_HEADS = 6
Q_LORA = 256
KV_LORA = 128
ROPE_BASE = 10000.0
D_FF = 2816
D_IN = POOL_WIDTH + Q_LORA + KV_LORA + QK_ROPE
ALPHA = (2 * DEPTH) ** 0.25
LN_EPS = 1e-5
RMS_EPS = 1e-6
NEG_INF = -1e30
MLA_SCALE = (QK_NOPE + QK_ROPE) ** -0.5
MEM_SCALE = MEM_HEAD_DIM ** -0.5
ADAM_LR = 0.001
ADAM_B1 = 0.9
ADAM_B2 = 0.999
ADAM_EPS = 1e-08
ADAM_WD = 0.01
ADAM_STEP = 10

N_CHIPS = 4
V7X_VMEM_LIMIT = 56 * 2**20
HALO = 16

_NT = (((1,), (1,)), ((), ()))
_TN = (((0,), (0,)), ((), ()))


def _dot(a, b):
    return jnp.dot(a, b, preferred_element_type=f32)


def _dot_nt(a, b):
    return lax.dot_general(a, b, _NT, preferred_element_type=f32)


def _dot_tn(a, b):
    return lax.dot_general(a, b, _TN, preferred_element_type=f32)


def _cp(*sem):
    return pltpu.CompilerParams(dimension_semantics=sem if sem else None, vmem_limit_bytes=V7X_VMEM_LIMIT)


_DEP_SPEC = pl.BlockSpec(memory_space=pl.ANY)


def _with_deps(body, n_in, deps):
    nd = len(deps)
    if not nd:
        return body

    def wrapped(*refs):
        return body(*refs[:n_in], *refs[n_in + nd:])

    return wrapped


def _tile(n, t):
    t = min(n, t)
    assert n % t == 0, (n, t)
    return t


def _row_tile(rows, cols, itemsize=4, target=2 * 2**20):
    best = None
    for t in range(16, rows + 1, 16):
        if rows % t == 0 and t * cols * itemsize <= target:
            best = t
    return best if best is not None else rows


def ffn_up(name, xb, w13, l, deps=()):
    S = xb.shape[0]
    ns = w13.shape[3]
    tm = _tile(S, 512)

    def body(x_ref, wg_ref, wu_ref, g_ref, u_ref, a_ref):
        x = x_ref[...]
        g = _dot(x, wg_ref[0, 0])
        u = _dot(x, wu_ref[0, 0])
        a = g * jax.nn.sigmoid(g) * u
        g_ref[...] = g.astype(bf16)
        u_ref[...] = u.astype(bf16)
        a_ref[...] = a.astype(bf16)

    out = SDS((S, 2 * ns), bf16)
    return pl.pallas_call(
        _with_deps(body, 3, deps), name=name, grid=(2, S // tm),
        in_specs=[pl.BlockSpec((tm, D_MODEL), lambda j, i: (i, 0)),
                  pl.BlockSpec((1, 1, D_MODEL, ns), lambda j, i: (l, j, 0, 0)),
                  pl.BlockSpec((1, 1, D_MODEL, ns), lambda j, i: (l, j + 2, 0, 0))] + [_DEP_SPEC] * len(deps),
        out_specs=[pl.BlockSpec((tm, ns), lambda j, i: (i, j))] * 3,
        out_shape=[out, out, out],
        compiler_params=_cp("parallel", "parallel"),
    )(xb, w13, w13, *deps)


def proj_res_ln(name, parts, ws, wl, x, g, b, rscale, deps=()):
    S = x.shape[0]
    tm = _tile(S, 512)
    n = len(parts)

    def body(*refs):
        p_refs, w_refs = refs[:n], refs[n:2 * n]
        x_ref, g_ref, b_ref, z_ref, y_ref, yb_ref = refs[2 * n:]
        acc = _dot(p_refs[0][...], w_refs[0][0])
        for k in range(1, n):
            acc = acc + _dot(p_refs[k][...], w_refs[k][0])
        if rscale != 1.0:
            acc = rscale * acc
        z = ALPHA * x_ref[...] + acc
        mu = jnp.mean(z, axis=-1, keepdims=True)
        zc = z - mu
        var = jnp.mean(zc * zc, axis=-1, keepdims=True)
        y = zc * lax.rsqrt(var + LN_EPS) * g_ref[...] + b_ref[...]
        z_ref[...] = z
        y_ref[...] = y
        yb_ref[...] = y.astype(bf16)

    row = lambda i: (i, 0)
    in_specs = [pl.BlockSpec((tm, p.shape[1]), row) for p in parts]
    in_specs += [pl.BlockSpec((1,) + w.shape[1:], functools.partial(lambda li, i: (li, 0, 0), li)) for w, li in zip(ws, wl)]
    in_specs += [pl.BlockSpec((tm, D_MODEL), row), pl.BlockSpec((1, D_MODEL), lambda i: (0, 0)),
                 pl.BlockSpec((1, D_MODEL), lambda i: (0, 0))] + [_DEP_SPEC] * len(deps)
    return pl.pallas_call(
        _with_deps(body, 2 * n + 3, deps), name=name, grid=(S // tm,), in_specs=in_specs,
        out_specs=[pl.BlockSpec((tm, D_MODEL), row)] * 3,
        out_shape=[SDS((S, D_MODEL), f32), SDS((S, D_MODEL), f32), SDS((S, D_MODEL), bf16)],
        compiler_params=_cp("parallel"),
    )(*parts, *ws, x, g, b, *deps)


def _ln_bwd_store(dyv, z_ref, g_ref, rscale, first, dzb_ref, dres_ref, dg_ref, db_ref):
    z = z_ref[...]
    mu = jnp.mean(z, axis=-1, keepdims=True)
    zc = z - mu
    rstd = lax.rsqrt(jnp.mean(zc * zc, axis=-1, keepdims=True) + LN_EPS)
    xhat = zc * rstd
    dxh = dyv * g_ref[...]
    m1 = jnp.mean(dxh, axis=-1, keepdims=True)
    m2 = jnp.mean(dxh * xhat, axis=-1, keepdims=True)
    dz = rstd * (dxh - m1 - xhat * m2)
    dzb_ref[...] = (rscale * dz).astype(bf16)
    dres_ref[...] = ALPHA * dz

    @pl.when(first)
    def _():
        dg_ref[...] = jnp.zeros_like(dg_ref)
        db_ref[...] = jnp.zeros_like(db_ref)

    dg_ref[...] += jnp.sum(dyv * xhat, axis=0, keepdims=True)
    db_ref[...] += jnp.sum(dyv, axis=0, keepdims=True)


def _ln_bwd_specs(S, tm, index):
    vec = pl.BlockSpec((1, D_MODEL), lambda *a: (0, 0))
    blk = pl.BlockSpec((tm, D_MODEL), index)
    in_specs = [blk, vec]
    out_specs = [blk, blk, vec, vec]
    out_shape = [SDS((S, D_MODEL), bf16), SDS((S, D_MODEL), f32), SDS((1, D_MODEL), f32), SDS((1, D_MODEL), f32)]
    return in_specs, out_specs, out_shape


def ffn_bwd_da(name, drb, w2, l, gate, up, deps=()):
    S = drb.shape[0]
    tm = _tile(S, 512)
    nh = D_FF // 2

    def body(dr_ref, w_ref, g_ref, u_ref, dh_ref):
        dr = dr_ref[...]
        for j in range(2):
            cols = slice(j * nh, (j + 1) * nh)
            da = _dot_nt(dr, w_ref[0, cols, :])
            g = g_ref[:, cols].astype(f32)
            u = u_ref[:, cols].astype(f32)
            sg = jax.nn.sigmoid(g)
            dh_ref[:, cols] = (da * u * (sg * (1.0 + g * (1.0 - sg)))).astype(bf16)
            dh_ref[:, D_FF + j * nh:D_FF + (j + 1) * nh] = (da * (g * sg)).astype(bf16)

    row = lambda i: (i, 0)
    return pl.pallas_call(
        _with_deps(body, 4, deps), name=name, grid=(S // tm,),
        in_specs=[pl.BlockSpec((tm, D_MODEL), row), pl.BlockSpec((1, D_FF, D_MODEL), lambda i: (l, 0, 0)),
                  pl.BlockSpec((tm, D_FF), row), pl.BlockSpec((tm, D_FF), row)] + [_DEP_SPEC] * len(deps),
        out_specs=pl.BlockSpec((tm, 2 * D_FF), row),
        out_shape=SDS((S, 2 * D_FF), bf16),
        compiler_params=_cp("parallel"),
    )(drb, w2, gate, up, *deps)


def ffn_dx(name, dh, w13, l, res, ln=None):
    S = dh.shape[0]
    ns = w13.shape[3]
    tm = _tile(S, 1024)
    last = N_CHIPS - 1
    row = lambda i, j: (i, 0)
    in_specs = [pl.BlockSpec((tm, ns), lambda i, j: (i, j)),
                pl.BlockSpec((1, 1, D_MODEL, ns), lambda i, j: (l, j, 0, 0)),
                pl.BlockSpec((tm, D_MODEL), row)]
    if ln is None:
        def body(dh_ref, w_ref, r_ref, o_ref):
            @pl.when(pl.program_id(1) == 0)
            def _():
                o_ref[...] = r_ref[...]

            o_ref[...] += _dot_nt(dh_ref[...], w_ref[0, 0])

        return pl.pallas_call(
            body, name=name, grid=(S // tm, N_CHIPS), in_specs=in_specs,
            out_specs=pl.BlockSpec((tm, D_MODEL), row), out_shape=SDS((S, D_MODEL), f32),
            compiler_params=_cp("parallel", "arbitrary"),
        )(dh, w13, res)

    z, g, rscale = ln

    def body_ln(dh_ref, w_ref, r_ref, z_ref, g_ref, dzb_ref, dres_ref, dg_ref, db_ref, acc_sc):
        i, j = pl.program_id(0), pl.program_id(1)

        @pl.when(j == 0)
        def _():
            acc_sc[...] = r_ref[...]

        acc_sc[...] += _dot_nt(dh_ref[...], w_ref[0, 0])

        @pl.when(j == last)
        def _():
            _ln_bwd_store(acc_sc[...], z_ref, g_ref, rscale, i == 0, dzb_ref, dres_ref, dg_ref, db_ref)

    ln_in, ln_out, ln_shape = _ln_bwd_specs(S, tm, row)
    return pl.pallas_call(
        body_ln, name=name, grid=(S // tm, N_CHIPS), in_specs=in_specs + ln_in, out_specs=ln_out, out_shape=ln_shape,
        scratch_shapes=[pltpu.VMEM((tm, D_MODEL), f32)], compiler_params=_cp("arbitrary", "arbitrary"),
    )(dh, w13, res, z, g)


def mm_nt_res(name, dys, ws, wl, res, out_dtype, ln=None):
    S = dys[0].shape[0]
    K = ws[0].shape[1]
    tm = _tile(S, 512)
    n = len(dys)
    n_in = 2 * n + (res is not None)

    def product(refs):
        acc = _dot_nt(refs[0][...], refs[n][0])
        for k in range(1, n):
            acc = acc + _dot_nt(refs[k][...], refs[n + k][0])
        if res is not None:
            acc = acc + refs[2 * n][...]
        return acc

    def body(*refs):
        refs[-1][...] = product(refs).astype(out_dtype)

    def body_ln(*refs):
        z_ref, g_ref, dzb_ref, dres_ref, dg_ref, db_ref = refs[n_in:]
        _ln_bwd_store(product(refs), z_ref, g_ref, ln[2], pl.program_id(0) == 0, dzb_ref, dres_ref, dg_ref, db_ref)

    row = lambda i: (i, 0)
    in_specs = [pl.BlockSpec((tm, d.shape[1]), row) for d in dys]
    in_specs += [pl.BlockSpec((1,) + w.shape[1:], functools.partial(lambda li, i: (li, 0, 0), li)) for w, li in zip(ws, wl)]
    args = list(dys) + list(ws)
    if res is not None:
        in_specs.append(pl.BlockSpec((tm, K), row))
        args.append(res)
    if ln is None:
        return pl.pallas_call(
            body, name=name, grid=(S // tm,), in_specs=in_specs,
            out_specs=pl.BlockSpec((tm, K), row), out_shape=SDS((S, K), out_dtype),
            compiler_params=_cp("parallel"),
        )(*args)
    ln_in, ln_out, ln_shape = _ln_bwd_specs(S, tm, row)
    return pl.pallas_call(
        body_ln, name=name, grid=(S // tm,), in_specs=in_specs + ln_in, out_specs=ln_out, out_shape=ln_shape,
        compiler_params=_cp("arbitrary"),
    )(*args, ln[0], ln[1])


def mm_tn(name, x, dy, col_shards=False, deps=()):
    S, K = x.shape
    N = dy.shape[1]
    ts = 512
    while ts * 2 <= min(S, 2048) and S % (ts * 2) == 0 and ts * 2 * K * 2 <= 6 * 2**20:
        ts *= 2
    ts = _tile(S, ts)
    if col_shards:
        tn = N // N_CHIPS
    else:
        tn = N
        while K * tn * 4 > 6 * 2**20 and tn % 256 == 0:
            tn //= 2
    nn = N // tn
    lead = ((0,) if col_shards else ()) + (slice(None), slice(None))

    def body(x_ref, dy_ref, o_ref):
        acc = _dot_tn(x_ref[...].astype(bf16), dy_ref[...].astype(bf16))

        @pl.when(pl.program_id(1) == 0)
        def _():
            o_ref[lead] = acc

        @pl.when(pl.program_id(1) != 0)
        def _():
            o_ref[lead] += acc

    if col_shards:
        out_spec = pl.BlockSpec((1, K, tn), lambda n, s: (n, 0, 0))
        out_shape = SDS((N_CHIPS, K, tn), f32)
    else:
        out_spec = pl.BlockSpec((K, tn), lambda n, s: (0, n))
        out_shape = SDS((K, N), f32)
    return pl.pallas_call(
        _with_deps(body, 2, deps), name=name, grid=(nn, S // ts),
        in_specs=[pl.BlockSpec((ts, K), lambda n, s: (s, 0)), pl.BlockSpec((ts, tn), lambda n, s: (s, n))]
        + [_DEP_SPEC] * len(deps),
        out_specs=out_spec, out_shape=out_shape, compiler_params=_cp("parallel", "arbitrary"),
    )(x, dy, *deps)


def mm_nn_shard(name, x, w, l):
    S, K = x.shape
    ns = w.shape[3]

    def body(x_ref, w_ref, o_ref):
        o_ref[...] = _dot(x_ref[...], w_ref[0, 0]).astype(bf16)

    return pl.pallas_call(
        body, name=name, grid=(N_CHIPS,),
        in_specs=[pl.BlockSpec((S, K), lambda j: (0, 0)), pl.BlockSpec((1, 1, K, ns), lambda j: (l, j, 0, 0))],
        out_specs=pl.BlockSpec((S, ns), lambda j: (0, j)), out_shape=SDS((S, N_CHIPS * ns), bf16),
        compiler_params=_cp("parallel"),
    )(x, w)


def loss_grad(name, y, t, ln):
    S = y.shape[0]
    tm = _tile(S, 512)
    z, g, rscale = ln

    def body(y_ref, t_ref, z_ref, g_ref, dzb_ref, dres_ref, dg_ref, db_ref, loss_ref):
        first = pl.program_id(0) == 0
        e = y_ref[...] - t_ref[...]
        _ln_bwd_store(e * (1.0 / D_MODEL), z_ref, g_ref, rscale, first, dzb_ref, dres_ref, dg_ref, db_ref)

        @pl.when(first)
        def _():
            loss_ref[...] = jnp.zeros_like(loss_ref)

        loss_ref[...] += jnp.full(loss_ref.shape, (0.5 / D_MODEL) * jnp.sum(e * e), f32)

    row = lambda i: (i, 0)
    ln_in, ln_out, ln_shape = _ln_bwd_specs(S, tm, row)
    return pl.pallas_call(
        body, name=name, grid=(S // tm,),
        in_specs=[pl.BlockSpec((tm, D_MODEL), row)] * 2 + ln_in,
        out_specs=ln_out + [pl.BlockSpec((8, 128), lambda i: (0, 0))],
        out_shape=ln_shape + [SDS((8, 128), f32)],
        compiler_params=_cp("arbitrary"),
    )(y, t, z, g)


def _half_sum(t):
    return t + pltpu.roll(t, 64, axis=1)


def mix_pre(name, xb, w_in, wq, wkv, l, gq, gkv, cs):
    S = xb.shape[0]
    tm = _tile(S, 512)
    H = MLA_HEADS
    W_EXT = w_in.shape[2]

    def body(x_ref, win_ref, wq_ref, wkv_ref, gq_ref, gkv_ref, cs_ref,
             u_ref, cq_ref, ckv_ref, cqn_ref, ckvn_ref, q_ref, k_ref, v_ref):
        h = _dot(x_ref[...], win_ref[0])
        u_ref[...] = h[:, :256]
        cq = h[:, 256:512]
        ckv = h[:, 512:640]
        cq_ref[...] = cq
        ckv_ref[...] = ckv
        cqn = (cq * lax.rsqrt(jnp.mean(cq * cq, axis=-1, keepdims=True) + RMS_EPS) * gq_ref[...]).astype(bf16)
        ckvn = (ckv * lax.rsqrt(jnp.mean(ckv * ckv, axis=-1, keepdims=True) + RMS_EPS) * gkv_ref[...]).astype(bf16)
        cqn_ref[...] = cqn
        ckvn_ref[...] = ckvn
        csv = cs_ref[...]
        lane = lax.broadcasted_iota(jnp.int32, (tm, 128), 1)
        kr = jnp.where(lane < 64, _half_sum(h[:, 640:768] * csv), 0.0).astype(bf16)
        kv = _dot(ckvn, wkv_ref[0])
        for hd in range(H):
            qe = _dot(cqn, wq_ref[0, hd])
            q_ref[hd, :, :128] = qe[:, :128].astype(bf16)
            q_ref[hd, :, 128:] = _half_sum(qe[:, 128:] * csv).astype(bf16)
            k_ref[hd, :, :128] = kv[:, 256 * hd:256 * hd + 128].astype(bf16)
            k_ref[hd, :, 128:] = kr
            v_ref[hd] = kv[:, 256 * hd + 128:256 * hd + 256].astype(bf16)

    row = lambda i: (i, 0)
    hrow = lambda i: (0, i, 0)
    return pl.pallas_call(
        body, name=name, grid=(S // tm,),
        in_specs=[pl.BlockSpec((tm, D_MODEL), row),
                  pl.BlockSpec((1, D_MODEL, W_EXT), lambda i: (l, 0, 0)),
                  pl.BlockSpec((1, H, Q_LORA, 256), lambda i: (l, 0, 0, 0)),
                  pl.BlockSpec((1, KV_LORA, H * 256), lambda i: (l, 0, 0)),
                  pl.BlockSpec((1, Q_LORA), lambda i: (0, 0)), pl.BlockSpec((1, KV_LORA), lambda i: (0, 0)),
                  pl.BlockSpec((tm, 128), row)],
        out_specs=[pl.BlockSpec((tm, 256), row), pl.BlockSpec((tm, Q_LORA), row), pl.BlockSpec((tm, KV_LORA), row),
                   pl.BlockSpec((tm, Q_LORA), row), pl.BlockSpec((tm, KV_LORA), row),
                   pl.BlockSpec((H, tm, 256), hrow), pl.BlockSpec((H, tm, 256), hrow), pl.BlockSpec((H, tm, 128), hrow)],
        out_shape=[SDS((S, 256), f32), SDS((S, Q_LORA), f32), SDS((S, KV_LORA), f32),
                   SDS((S, Q_LORA), bf16), SDS((S, KV_LORA), bf16),
                   SDS((H, S, 256), bf16), SDS((H, S, 256), bf16), SDS((H, S, 128), bf16)],
        compiler_params=_cp("parallel"),
    )(xb, w_in, wq, wkv, gq, gkv, cs)


def _group_select(col, a2, a4, a8, a16):
    return jnp.where(col < 64, a2, jnp.where(col < 128, a4, jnp.where(col < 192, a8, a16)))


def pool_fwd(name, u, wbd, scale):
    S = u.shape[0]
    tm = _tile(S, 512)
    hb = tm // HALO

    def body(u_ref, halo_ref, w_ref, s_ref, d_ref, y_ref):
        i = pl.program_id(0)
        cur = u_ref[...]
        halo = jnp.where(i > 0, halo_ref[...], 0.0)
        ext = jnp.concatenate([halo, cur], axis=0)
        s2 = ext + pltpu.roll(ext, 1, axis=0)
        s4 = s2 + pltpu.roll(s2, 2, axis=0)
        s8 = s4 + pltpu.roll(s4, 4, axis=0)
        s16 = s8 + pltpu.roll(s8, 8, axis=0)
        t1 = (i * tm + 1 + lax.broadcasted_iota(jnp.int32, (tm, 1), 0)).astype(f32)
        col = lax.broadcasted_iota(jnp.int32, (tm, 256), 1)
        m = _group_select(col, s2[HALO:] / jnp.minimum(t1, 2.0), s4[HALO:] / jnp.minimum(t1, 4.0),
                          s8[HALO:] / jnp.minimum(t1, 8.0), s16[HALO:] / jnp.minimum(t1, 16.0))
        d = (m - cur).astype(bf16)
        d_ref[...] = d
        y_ref[...] = (_dot(d, w_ref[...]) * s_ref[...]).astype(bf16)

    row = lambda i: (i, 0)
    return pl.pallas_call(
        body, name=name, grid=(S // tm,),
        in_specs=[pl.BlockSpec((tm, 256), row), pl.BlockSpec((HALO, 256), lambda i: (jnp.maximum(i * hb - 1, 0), 0)),
                  pl.BlockSpec((256, 256), lambda i: (0, 0)), pl.BlockSpec((1, 256), lambda i: (0, 0))],
        out_specs=[pl.BlockSpec((tm, 256), row)] * 2,
        out_shape=[SDS((S, 256), bf16), SDS((S, 256), bf16)],
        compiler_params=_cp("parallel"),
    )(u, u, wbd, scale)


def pool_bwd(name, dyp, d, wbd, scale):
    S = dyp.shape[0]
    tm = _tile(S, 512)
    hb = tm // HALO
    n_ext = tm + HALO

    def fwd_sum(e, steps):
        k = 1
        for _ in range(steps):
            e = e + pltpu.roll(e, n_ext - k, axis=0)
            k *= 2
        return e

    def body(dy_ref, halo_ref, d_ref, w_ref, s_ref, du_ref, dyw_ref, ds_ref):
        i = pl.program_id(0)
        sc = s_ref[...]
        w = w_ref[...]
        cur = dy_ref[...].astype(f32)
        halo = jnp.where(i < pl.num_programs(0) - 1, halo_ref[...].astype(f32), 0.0)
        dyw = jnp.concatenate([cur, halo], axis=0) * sc
        dyw_ref[...] = dyw[:tm].astype(bf16)
        dd = _dot_nt(dyw.astype(bf16), w)
        t1 = (i * tm + 1 + lax.broadcasted_iota(jnp.int32, (n_ext, 1), 0)).astype(f32)
        f2 = fwd_sum(dd / jnp.minimum(t1, 2.0), 1)
        f4 = fwd_sum(dd / jnp.minimum(t1, 4.0), 2)
        f8 = fwd_sum(dd / jnp.minimum(t1, 8.0), 3)
        f16 = fwd_sum(dd / jnp.minimum(t1, 16.0), 4)
        col = lax.broadcasted_iota(jnp.int32, (tm, 256), 1)
        du_ref[...] = (_group_select(col, f2[:tm], f4[:tm], f8[:tm], f16[:tm]) - dd[:tm]).astype(bf16)

        @pl.when(i == 0)
        def _():
            ds_ref[...] = jnp.zeros_like(ds_ref)

        ds_ref[...] += jnp.sum(cur * _dot(d_ref[...], w), axis=0, keepdims=True)

    row = lambda i: (i, 0)
    nhb = S // HALO
    return pl.pallas_call(
        body, name=name, grid=(S // tm,),
        in_specs=[pl.BlockSpec((tm, 256), row), pl.BlockSpec((HALO, 256), lambda i: (jnp.minimum((i + 1) * hb, nhb - 1), 0)),
                  pl.BlockSpec((tm, 256), row), pl.BlockSpec((256, 256), lambda i: (0, 0)),
                  pl.BlockSpec((1, 256), lambda i: (0, 0))],
        out_specs=[pl.BlockSpec((tm, 256), row), pl.BlockSpec((tm, 256), row), pl.BlockSpec((1, 256), lambda i: (0, 0))],
        out_shape=[SDS((S, 256), bf16), SDS((S, 256), bf16), SDS((1, 256), f32)],
        compiler_params=_cp("arbitrary"),
    )(dyp, dyp, d, wbd, scale)


def _diag_mask(r0, rn, kn):
    rc = (r0 + lax.broadcasted_iota(jnp.int32, (rn, 1), 0)) // 64
    cc = lax.broadcasted_iota(jnp.int32, (1, kn), 1) // 64
    return rc >= cc


def _diag_parts(tq):
    h = tq // 2
    return [(0, h, h), (h, h, tq)] if h % 128 == 0 else [(0, tq, tq)]


MLA_SCALE_LOG2 = MLA_SCALE * math.log2(math.e)


def mla_attn_fwd(name, q, k, v):
    H, S, _ = q.shape
    tq = _tile(S, 1024)
    nq = S // tq
    pairs = [(i, j) for i in range(nq) for j in range(i + 1)]
    it = jnp.asarray([p_[0] for p_ in pairs], jnp.int32)
    jt = jnp.asarray([p_[1] for p_ in pairs], jnp.int32)

    def body(it_ref, jt_ref, q_ref, k_ref, v_ref, o_ref, lse_ref, m_sc, l_sc, acc_sc):
        t = pl.program_id(1)
        i, j = it_ref[t], jt_ref[t]

        @pl.when(j == 0)
        def _():
            m_sc[...] = jnp.full_like(m_sc, NEG_INF)
            l_sc[...] = jnp.zeros_like(l_sc)
            acc_sc[...] = jnp.zeros_like(acc_sc)

        def part(r0, rn, kn, masked):
            rows, keys = slice(r0, r0 + rn), slice(0, kn)
            s = _dot_nt(q_ref[0, rows, :], k_ref[0, keys, :])
            if masked:
                s = jnp.where(_diag_mask(r0, rn, kn), s, NEG_INF)
            m_prev = m_sc[rows, :]
            m_new = jnp.maximum(m_prev, jnp.max(s, axis=-1, keepdims=True))
            p = jnp.exp2((s - jnp.tile(m_new, (1, kn // 128))) * MLA_SCALE_LOG2)
            a = jnp.exp2((m_prev - m_new) * MLA_SCALE_LOG2)
            l_sc[rows, :] = a * l_sc[rows, :] + jnp.sum(p, axis=-1, keepdims=True)
            acc_sc[rows, :] = a * acc_sc[rows, :] + _dot(p.astype(bf16), v_ref[0, keys, :])
            m_sc[rows, :] = m_new

        @pl.when(j < i)
        def _():
            part(0, tq, tq, False)

        @pl.when(j == i)
        def _():
            for r0, rn, kn in _diag_parts(tq):
                part(r0, rn, kn, True)
            o_ref[...] = (acc_sc[...] / l_sc[...]).astype(bf16)
            lse_ref[0] = m_sc[...] * MLA_SCALE_LOG2 + jnp.log2(l_sc[...])

    return pl.pallas_call(
        body, name=name,
        grid_spec=pltpu.PrefetchScalarGridSpec(
            num_scalar_prefetch=2, grid=(H, len(pairs)),
            in_specs=[pl.BlockSpec((1, tq, 256), lambda h, t, it_, jt_: (h, it_[t], 0)),
                      pl.BlockSpec((1, tq, 256), lambda h, t, it_, jt_: (h, jt_[t], 0)),
                      pl.BlockSpec((1, tq, 128), lambda h, t, it_, jt_: (h, jt_[t], 0))],
            out_specs=[pl.BlockSpec((tq, 128), lambda h, t, it_, jt_: (it_[t], h)),
                       pl.BlockSpec((1, tq, 128), lambda h, t, it_, jt_: (h, it_[t], 0))],
            scratch_shapes=[pltpu.VMEM((tq, 128), f32), pltpu.VMEM((tq, 128), f32), pltpu.VMEM((tq, 128), f32)]),
        out_shape=[SDS((S, H * 128), bf16), SDS((H, S, 128), f32)],
        compiler_params=_cp("parallel", "arbitrary"),
    )(it, jt, q, k, v)


def mla_attn_bwd(name, q, k, v, o, do, lse):
    H, S, _ = q.shape
    tq = _tile(S, 1024)
    nq = S // tq
    pairs = [(i, j) for j in range(nq) for i in range(j, nq)]
    it = jnp.asarray([p_[0] for p_ in pairs], jnp.int32)
    jt = jnp.asarray([p_[1] for p_ in pairs], jnp.int32)
    n_pairs = len(pairs)

    def body(it_ref, jt_ref, q_ref, k_ref, v_ref, o_ref, do_ref, lse_ref, dq_ref, dk_ref, dv_ref, dk_sc, dv_sc):
        t = pl.program_id(1)
        i, j = it_ref[t], jt_ref[t]

        @pl.when(t == 0)
        def _():
            dq_ref[...] = jnp.zeros_like(dq_ref)

        @pl.when(i == j)
        def _():
            dk_sc[...] = jnp.zeros_like(dk_sc)
            dv_sc[...] = jnp.zeros_like(dv_sc)

        def part(r0, rn, kn, masked):
            rows, keys = slice(r0, r0 + rn), slice(0, kn)
            qv, kv_, dov = q_ref[0, rows, :], k_ref[0, keys, :], do_ref[rows, :]
            s = _dot_nt(qv, kv_)
            if masked:
                s = jnp.where(_diag_mask(r0, rn, kn), s, NEG_INF)
            p = jnp.exp2(s * MLA_SCALE_LOG2 - jnp.tile(lse_ref[0, rows, :], (1, kn // 128)))
            dv_sc[keys, :] += _dot_tn(p.astype(bf16), dov)
            dp = _dot_nt(dov, v_ref[0, keys, :])
            delta = jnp.sum(dov.astype(f32) * o_ref[rows, :].astype(f32), axis=-1, keepdims=True)
            ds = (p * (dp - delta)).astype(bf16)
            dk_sc[keys, :] += _dot_tn(ds, qv)
            dq_rows = pl.ds(pl.multiple_of(i * tq + r0, 128), rn)
            dq_ref[0, dq_rows, :] += _dot(ds, kv_)

        @pl.when(i > j)
        def _():
            part(0, tq, tq, False)

        @pl.when(i == j)
        def _():
            for r0, rn, kn in _diag_parts(tq):
                part(r0, rn, kn, True)

        @pl.when(i == nq - 1)
        def _():
            dk_ref[0] = dk_sc[...] * MLA_SCALE
            dv_ref[0] = dv_sc[...]

        @pl.when(t == n_pairs - 1)
        def _():
            dq_ref[...] = dq_ref[...] * MLA_SCALE

    qi = lambda h, t, it_, jt_: (h, it_[t], 0)
    kj = lambda h, t, it_, jt_: (h, jt_[t], 0)
    oi = lambda h, t, it_, jt_: (it_[t], h)
    return pl.pallas_call(
        body, name=name,
        grid_spec=pltpu.PrefetchScalarGridSpec(
            num_scalar_prefetch=2, grid=(H, n_pairs),
            in_specs=[pl.BlockSpec((1, tq, 256), qi), pl.BlockSpec((1, tq, 256), kj), pl.BlockSpec((1, tq, 128), kj),
                      pl.BlockSpec((tq, 128), oi), pl.BlockSpec((tq, 128), oi), pl.BlockSpec((1, tq, 128), qi)],
            out_specs=[pl.BlockSpec((1, S, 256), lambda h, t, it_, jt_: (h, 0, 0)), pl.BlockSpec((1, tq, 256), kj),
                       pl.BlockSpec((1, tq, 128), kj)],
            scratch_shapes=[pltpu.VMEM((tq, 256), f32), pltpu.VMEM((tq, 128), f32)]),
        out_shape=[SDS((H, S, 256), f32), SDS((H, S, 256), f32), SDS((H, S, 128), f32)],
        compiler_params=_cp("parallel", "arbitrary"),
    )(it, jt, q, k, v, o, do, lse)


def mix_post_bwd(name, dq, dk, dv, wq, wkv, l, cq, ckv, gq, gkv, cs):
    H, S, _ = dq.shape
    tm = _tile(S, 512)

    def rms_bwd(dyn, c, g):
        r = lax.rsqrt(jnp.mean(c * c, axis=-1, keepdims=True) + RMS_EPS)
        ch = c * r
        dyg = dyn * g
        dc = r * (dyg - ch * jnp.mean(dyg * ch, axis=-1, keepdims=True))
        return dc, jnp.sum(dyn * ch, axis=0, keepdims=True)

    def body(dq_ref, dk_ref, dv_ref, wq_ref, wkv_ref, cq_ref, ckv_ref, gq_ref, gkv_ref, cs_ref,
             dqe_ref, dkv_ref, dh_ref, dgq_ref, dgkv_ref):
        csv = cs_ref[...]
        lane = lax.broadcasted_iota(jnp.int32, (tm, 128), 1)
        dcqn = jnp.zeros((tm, Q_LORA), f32)
        dkr = jnp.zeros((tm, 128), f32)
        for hd in range(H):
            dqh = dq_ref[hd]
            dqe = jnp.concatenate([dqh[:, :128], _half_sum(dqh[:, 128:]) * csv], axis=1).astype(bf16)
            dqe_ref[:, 256 * hd:256 * hd + 256] = dqe
            dcqn = dcqn + _dot_nt(dqe, wq_ref[0, hd])
            dkh = dk_ref[hd]
            dkv_ref[:, 256 * hd:256 * hd + 128] = dkh[:, :128].astype(bf16)
            dkv_ref[:, 256 * hd + 128:256 * hd + 256] = dv_ref[hd].astype(bf16)
            dkr = dkr + dkh[:, 128:]
        dckvn = _dot_nt(dkv_ref[...], wkv_ref[0])
        dblk = _half_sum(jnp.where(lane < 64, dkr, 0.0)) * csv
        dcq, dgq = rms_bwd(dcqn, cq_ref[...], gq_ref[...])
        dckv, dgkv = rms_bwd(dckvn, ckv_ref[...], gkv_ref[...])
        dh_ref[:, :256] = dcq.astype(bf16)
        dh_ref[:, 256:384] = dckv.astype(bf16)
        dh_ref[:, 384:] = dblk.astype(bf16)

        @pl.when(pl.program_id(0) == 0)
        def _():
            dgq_ref[...] = jnp.zeros_like(dgq_ref)
            dgkv_ref[...] = jnp.zeros_like(dgkv_ref)

        dgq_ref[...] += dgq
        dgkv_ref[...] += dgkv

    row = lambda i: (i, 0)
    hrow = lambda i: (0, i, 0)
    return pl.pallas_call(
        body, name=name, grid=(S // tm,),
        in_specs=[pl.BlockSpec((H, tm, 256), hrow), pl.BlockSpec((H, tm, 256), hrow), pl.BlockSpec((H, tm, 128), hrow),
                  pl.BlockSpec((1, H, Q_LORA, 256), lambda i: (l, 0, 0, 0)),
                  pl.BlockSpec((1, KV_LORA, H * 256), lambda i: (l, 0, 0)),
                  pl.BlockSpec((tm, Q_LORA), row), pl.BlockSpec((tm, KV_LORA), row),
                  pl.BlockSpec((1, Q_LORA), lambda i: (0, 0)), pl.BlockSpec((1, KV_LORA), lambda i: (0, 0)),
                  pl.BlockSpec((tm, 128), row)],
        out_specs=[pl.BlockSpec((tm, H * 256), row), pl.BlockSpec((tm, H * 256), row), pl.BlockSpec((tm, 512), row),
                   pl.BlockSpec((1, Q_LORA), lambda i: (0, 0)), pl.BlockSpec((1, KV_LORA), lambda i: (0, 0))],
        out_shape=[SDS((S, H * 256), bf16), SDS((S, H * 256), bf16), SDS((S, 512), bf16),
                   SDS((1, Q_LORA), f32), SDS((1, KV_LORA), f32)],
        compiler_params=_cp("arbitrary"),
    )(dq, dk, dv, wq, wkv, cq, ckv, gq, gkv, cs)


def _cross_probs(qb, kv_ref, hd):
    cols = slice(hd * MEM_HEAD_DIM, (hd + 1) * MEM_HEAD_DIM)
    s = _dot_nt(qb[:, cols], kv_ref[:, cols]) * MEM_SCALE
    e = jnp.exp(s - jnp.max(s, axis=-1, keepdims=True))
    return e / jnp.sum(e, axis=-1, keepdims=True)


def cross_fwd(name, xb, xf, wq, wo, l, kv, g, b):
    S = xb.shape[0]
    tm = _tile(S, 512)
    M = kv.shape[0]

    def body(x_ref, xf_ref, wq_ref, wo_ref, k_ref, v_ref, g_ref, b_ref, q_ref, o_ref, z_ref, y_ref, yb_ref):
        qb = _dot(x_ref[...], wq_ref[0]).astype(bf16)
        q_ref[...] = qb
        for hd in range(MEM_HEADS):
            cols = slice(hd * MEM_HEAD_DIM, (hd + 1) * MEM_HEAD_DIM)
            p = _cross_probs(qb, k_ref, hd)
            o_ref[:, cols] = _dot(p.astype(bf16), v_ref[:, cols]).astype(bf16)
        z = ALPHA * xf_ref[...] + _dot(o_ref[...], wo_ref[0])
        mu = jnp.mean(z, axis=-1, keepdims=True)
        zc = z - mu
        var = jnp.mean(zc * zc, axis=-1, keepdims=True)
        y = zc * lax.rsqrt(var + LN_EPS) * g_ref[...] + b_ref[...]
        z_ref[...] = z
        y_ref[...] = y
        yb_ref[...] = y.astype(bf16)

    row = lambda i: (i, 0)
    wspec = pl.BlockSpec((1, D_MODEL, D_MODEL), lambda i: (l, 0, 0))
    vec = pl.BlockSpec((1, D_MODEL), lambda i: (0, 0))
    blk = pl.BlockSpec((tm, D_MODEL), row)
    return pl.pallas_call(
        body, name=name, grid=(S // tm,),
        in_specs=[blk, blk, wspec, wspec, pl.BlockSpec((M, D_MODEL), lambda i: (0, 0)),
                  pl.BlockSpec((M, D_MODEL), lambda i: (0, 1)), vec, vec],
        out_specs=[blk] * 5,
        out_shape=[SDS((S, D_MODEL), bf16), SDS((S, D_MODEL), bf16), SDS((S, D_MODEL), f32), SDS((S, D_MODEL), f32),
                   SDS((S, D_MODEL), bf16)],
        compiler_params=_cp("parallel"),
    )(xb, xf, wq, wo, kv, kv, g, b)


def cross_bwd(name, dzb, wo, l, qb, kv, deps=()):
    S = dzb.shape[0]
    tm = _tile(S, 512)
    M = kv.shape[0]

    def body(dz_ref, wo_ref, q_ref, k_ref, v_ref, dq_ref, dkv_ref):
        @pl.when(pl.program_id(0) == 0)
        def _():
            dkv_ref[...] = jnp.zeros_like(dkv_ref)

        do = _dot_nt(dz_ref[...], wo_ref[0]).astype(bf16)
        qv = q_ref[...]
        for hd in range(MEM_HEADS):
            cols = slice(hd * MEM_HEAD_DIM, (hd + 1) * MEM_HEAD_DIM)
            vcols = slice(D_MODEL + hd * MEM_HEAD_DIM, D_MODEL + (hd + 1) * MEM_HEAD_DIM)
            p = _cross_probs(qv, k_ref, hd)
            doh = do[:, cols]
            dkv_ref[:, vcols] += _dot_tn(p.astype(bf16), doh)
            dp = _dot_nt(doh, v_ref[:, cols])
            ds = (p * (dp - jnp.sum(dp * p, axis=-1, keepdims=True)) * MEM_SCALE).astype(bf16)
            dq_ref[:, cols] = _dot(ds, k_ref[:, cols]).astype(bf16)
            dkv_ref[:, cols] += _dot_tn(ds, qv[:, cols])

    row = lambda i: (i, 0)
    blk = pl.BlockSpec((tm, D_MODEL), row)
    return pl.pallas_call(
        _with_deps(body, 5, deps), name=name, grid=(S // tm,),
        in_specs=[blk, pl.BlockSpec((1, D_MODEL, D_MODEL), lambda i: (l, 0, 0)), blk,
                  pl.BlockSpec((M, D_MODEL), lambda i: (0, 0)), pl.BlockSpec((M, D_MODEL), lambda i: (0, 1))]
        + [_DEP_SPEC] * len(deps),
        out_specs=[blk, pl.BlockSpec((M, 2 * D_MODEL), lambda i: (0, 0))],
        out_shape=[SDS((S, D_MODEL), bf16), SDS((M, 2 * D_MODEL), f32)],
        compiler_params=_cp("arbitrary"),
    )(dzb, wo, qb, kv, kv, *deps)


def adamw(name, w, g, m, v, deps=()):
    shape = w.shape
    cols = shape[-1]
    rows = math.prod(shape[:-1])
    tr = _row_tile(rows, cols, target=2 * 2**20)
    c1 = 1.0 - ADAM_B1 ** ADAM_STEP
    c2 = 1.0 - ADAM_B2 ** ADAM_STEP

    def body(w_ref, g_ref, m_ref, v_ref, d_ref, nm_ref, nv_ref):
        gv = g_ref[...]
        nm = ADAM_B1 * m_ref[...] + (1.0 - ADAM_B1) * gv
        nv = ADAM_B2 * v_ref[...] + (1.0 - ADAM_B2) * (gv * gv)
        d_ref[...] = -ADAM_LR * ((nm / c1) / (jnp.sqrt(nv / c2) + ADAM_EPS) + ADAM_WD * w_ref[...])
        nm_ref[...] = nm
        nv_ref[...] = nv

    blk = pl.BlockSpec((tr, cols), lambda i: (i, 0))
    flat = SDS((rows, cols), f32)
    outs = pl.pallas_call(
        _with_deps(body, 4, deps), name=name, grid=(rows // tr,), in_specs=[blk] * 4 + [_DEP_SPEC] * len(deps),
        out_specs=[blk] * 3, out_shape=[flat] * 3, compiler_params=_cp("parallel"),
    )(*[a.reshape(rows, cols) for a in (w, g, m, v)], *deps)
    return [o.reshape(shape) for o in outs]


def _me():
    return lax.axis_index("x"), lax.axis_index("y"), lax.axis_index("c")


def _other_chips(x, y):
    return [(1 - x, y), (x, 1 - y), (1 - x, 1 - y)]


def _pair_share_each(owners, bufs, sems, mine, act):
    x, y, c = _me()
    for o in range(2):
        slots = [(a, lyr) for a in range(len(bufs)) for lyr in range(DEPTH) if owners[a][lyr] == o]

        @pl.when((c == o) if mine else (c != o))
        def _(slots=slots):
            for a, lyr in slots:
                slot = bufs[a].at[lyr]
                act(_rcopy(slot, slot, sems[0].at[2 * a + lyr], sems[1].at[2 * a + lyr], (x, y, 1 - c)))


def pair_share_start(name, sums, owners, after):
    def body_fn(b_in, s_in, s_out):
        _pair_share_each(owners, b_in, s_out, True, lambda cp: cp.start())

    outs, sems, token = _split_call(name, body_fn, list(sums), [], [2 * len(sums)] * 2, after)
    return (outs, sems[0], sems[1], owners), token


def pair_share_wait(name, st, after):
    bufs, send, recv, owners = st

    def body_fn(b_in, s_in, s_out):
        _pair_share_each(owners, b_in, s_in, True, lambda cp: cp.wait_send())
        _pair_share_each(owners, b_in, s_in, False, lambda cp: cp.wait_recv())

    outs, _, _ = _split_call(name, body_fn, list(bufs), [send, recv], [], after)
    return outs


def allsum_small(name, v, deps=()):
    R = v.shape[0]

    def body(v_ref, o_ref, all_ref, send_sems, recv_sems, local_sem):
        x, y, c = _me()
        me, sibling = (x, y, c), (x, y, 1 - c)
        chips = _other_chips(x, y)

        def rows(px, py, pc):
            return all_ref.at[4 * px + 2 * py + pc]

        def copy(k, block, to, src=None):
            return pltpu.make_async_remote_copy(
                src_ref=rows(*block) if src is None else src, dst_ref=rows(*block),
                send_sem=send_sems.at[k], recv_sem=recv_sems.at[k], device_id=to, device_id_type=MESH)

        mine = pltpu.make_async_copy(v_ref, rows(*me), local_sem)
        mine.start()
        first = [copy(0, me, sibling, src=v_ref)]
        first += [copy(1 + j, me, (*chip, c), src=v_ref) for j, chip in enumerate(chips)]
        for cp in first:
            cp.start()
        passed = [copy(4 + j, (*chip, c), sibling) for j, chip in enumerate(chips)]
        for j, chip in enumerate(chips):
            copy(1 + j, (*chip, c), me).wait_recv()
            passed[j].start()
        copy(0, sibling, me).wait_recv()
        for j, chip in enumerate(chips):
            copy(4 + j, (*chip, 1 - c), me).wait_recv()
        for cp in first + passed:
            cp.wait_send()
        mine.wait()
        acc = all_ref[0]
        for d in range(1, 8):
            acc = acc + all_ref[d]
        o_ref[...] = acc

    return pl.pallas_call(
        _with_deps(body, 1, deps), name=name,
        in_specs=[pl.BlockSpec(memory_space=pltpu.VMEM)] + [_DEP_SPEC] * len(deps),
        out_specs=pl.BlockSpec(memory_space=pltpu.VMEM),
        out_shape=SDS((R, 128), f32),
        scratch_shapes=[pltpu.VMEM((8, R, 128), f32), pltpu.SemaphoreType.DMA((7,)), pltpu.SemaphoreType.DMA((7,)),
                        pltpu.SemaphoreType.DMA],
        compiler_params=pltpu.CompilerParams(vmem_limit_bytes=V7X_VMEM_LIMIT),
    )(v, *deps)


def _swap_half(r):
    return jnp.concatenate([-r[..., 32:], r[..., :32]], axis=-1)


def _unswap_add(p, qg):
    return p + jnp.concatenate([qg[..., 32:], -qg[..., :32]], axis=-1)


def _block_diag(pw):
    L = pw.shape[0]
    out = jnp.zeros((L, 256, 256), pw.dtype)
    for gi in range(4):
        out = out.at[:, 64 * gi:64 * gi + 64, 64 * gi:64 * gi + 64].set(pw[:, gi])
    return out


def _to_col_shards(w):
    *lead, K, N = w.shape
    nl = len(lead)
    return w.reshape(*lead, K, N_CHIPS, N // N_CHIPS).transpose(*range(nl), nl + 1, nl, nl + 2)


def _from_col_shards(w):
    *lead, C, K, n = w.shape
    nl = len(lead)
    return w.transpose(*range(nl), nl + 1, nl, nl + 2).reshape(*lead, K, C * n)


def _step_serial_comm(x, mem, positions, ln_g, ln_b, ffn1_w13, ffn1_w2, w_in, pool_w, pool_scale, q_norm_g, w_uq, kv_norm_g, w_ukv, w_out, mem_wq, mem_wkv, mem_wo, ffn2_w13, ffn2_w2, loss_target, m_ln_g, m_ln_b, m_ffn1_w13, m_ffn1_w2, m_w_in, m_pool_w, m_pool_scale, m_q_norm_g, m_w_uq, m_kv_norm_g, m_w_ukv, m_w_out, m_mem_wq, m_mem_wkv, m_mem_wo, m_ffn2_w13, m_ffn2_w2, v_ln_g, v_ln_b, v_ffn1_w13, v_ffn1_w2, v_w_in, v_pool_w, v_pool_scale, v_q_norm_g, v_w_uq, v_kv_norm_g, v_w_ukv, v_w_out, v_mem_wq, v_mem_wkv, v_mem_wo, v_ffn2_w13, v_ffn2_w2):
    L = DEPTH
    S = x.shape[1]
    qx, qy, _ = _me()
    chip = 2 * qx + qy

    big = [ffn1_w13, ffn1_w2, w_in, w_uq, w_ukv, w_out, mem_wq, mem_wkv, mem_wo, ffn2_w13, ffn2_w2]
    (g_f1w13, g_f1w2, g_win, g_wuq, g_wukv, g_wout, g_mwq, g_mwkv, g_mwo, g_f2w13, g_f2w2) = gather_weights(
        [w.astype(bf16) for w in big])
    f1w2 = g_f1w2.reshape(L, D_FF, D_MODEL)
    f2w2 = g_f2w2.reshape(L, D_FF, D_MODEL)
    win = g_win.reshape(L, D_MODEL, D_IN)
    win_ext = jnp.concatenate([win, _swap_half(win[..., D_IN - QK_ROPE:])], axis=-1)
    wuq = _from_col_shards(g_wuq).reshape(L, Q_LORA, MLA_HEADS, QK_NOPE + QK_ROPE)
    wq_ext = jnp.concatenate([wuq, _swap_half(wuq[..., QK_NOPE:])], axis=-1).transpose(0, 2, 1, 3)
    wukv = _from_col_shards(g_wukv)
    wout = g_wout.reshape(L, D_MODEL, D_MODEL)
    wout_pool, wout_mla = wout[:, :POOL_WIDTH], wout[:, POOL_WIDTH:]
    mwq = g_mwq.reshape(L, D_MODEL, D_MODEL)
    mwo = g_mwo.reshape(L, D_MODEL, D_MODEL)
    wbd = _block_diag(pool_w.astype(bf16))

    ln_pad = jnp.zeros((2, L, 4, N_CHIPS, D_MODEL // N_CHIPS), f32)
    ln_pad = lax.dynamic_update_slice(ln_pad, jnp.stack([ln_g, ln_b])[:, :, :, None, :], (0, 0, 0, chip, 0))
    ln_full = allsum_small("allsum_ln", ln_pad.reshape(-1, 128)) * 0.5
    ln_full = ln_full.reshape(2, L, 4, D_MODEL)
    lng, lnb = ln_full[0], ln_full[1]

    half = QK_ROPE // 2
    inv_freq = ROPE_BASE ** (-jnp.arange(half, dtype=f32) / half)
    ang = positions[0].astype(f32)[:, None] * inv_freq
    cos, sin = jnp.cos(ang), jnp.sin(ang)
    cs = jnp.concatenate([cos, cos, sin, sin], axis=-1)

    memb = mem[0].astype(bf16)
    xf = x[0]
    xb = xf.astype(bf16)
    vec = lambda a: a.reshape(1, -1)

    saved = []
    for l in range(L):
        sv = {}
        sv["x0b"] = xb
        gate, up, act = ffn_up(f"ffn1_up_{l}", xb, g_f1w13, l)
        z1, x1f, x1b = proj_res_ln(f"ffn1_down_{l}", [act], [f1w2], [l], xf, vec(lng[l, 0]), vec(lnb[l, 0]), 0.5)
        sv.update(gate1=gate, up1=up, act1=act, z1=z1, x1b=x1b)
        u, cq, ckv, cqn, ckvn, q, k, v = mix_pre(f"mix_pre_{l}", x1b, win_ext, wq_ext, wukv, l,
                                                   vec(q_norm_g[l]), vec(kv_norm_g[l]), cs)
        dpool, ypool = pool_fwd(f"pool_fwd_{l}", u, wbd[l], vec(pool_scale[l]))
        o, lse = mla_attn_fwd(f"mla_fwd_{l}", q, k, v)
        z2, x2f, x2b = proj_res_ln(f"mix_out_{l}", [ypool, o], [wout_pool, wout_mla], [l, l], x1f,
                                   vec(lng[l, 1]), vec(lnb[l, 1]), 1.0)
        sv.update(cq=cq, ckv=ckv, cqn=cqn, ckvn=ckvn, q=q, k=k, v=v, dpool=dpool, ypool=ypool, o=o, lse=lse, z2=z2, x2b=x2b)
        kvm = mm_nn_shard(f"mem_kv_{l}", memb, g_mwkv, l)
        cq_, co_, z3, x3f, x3b = cross_fwd(f"cross_fwd_{l}", x2b, x2f, mwq, mwo, l, kvm, vec(lng[l, 2]), vec(lnb[l, 2]))
        sv.update(kvm=kvm, crq=cq_, cro=co_, z3=z3, x3b=x3b)
        gate, up, act = ffn_up(f"ffn2_up_{l}", x3b, g_f2w13, l)
        z4, xf, xb = proj_res_ln(f"ffn2_down_{l}", [act], [f2w2], [l], x3f, vec(lng[l, 3]), vec(lnb[l, 3]), 0.5)
        sv.update(gate2=gate, up2=up, act2=act, z4=z4)
        saved.append(sv)

    dy, loss_blk = loss_grad("loss_grad", xf, loss_target[0])
    loss = lax.psum(loss_blk[0, 0], ("x", "y", "c"))

    G = dict(f1w13=None, f1w2=None, mwq=None, mwkv=None, mwo=None, f2w13=None, f2w2=None)
    small = {k_: [None] * L for k_ in ("win", "wuq", "wukv", "wout", "pool_w", "pool_scale", "gq", "gkv", "lng", "lnb")}
    for l in reversed(range(L)):
        sv = saved[l]
        dlg, dlb = [None] * 4, [None] * 4
        dzb, dres, dlg[3], dlb[3] = ln_bwd(f"ln4_bwd_{l}", dy, sv["z4"], vec(lng[l, 3]), 0.5)
        dh = ffn_bwd_da(f"ffn2_bwd_da_{l}", dzb, f2w2, l, sv["gate2"], sv["up2"])
        G["f2w2"] = mm_tn(f"ffn2_dw2_{l}", sv["act2"], dzb, "nat", l, G["f2w2"])
        G["f2w13"] = mm_tn(f"ffn2_dw13_{l}", sv["x3b"], dh, "shard", l, G["f2w13"])
        dy = ffn_dx(f"ffn2_dx_{l}", dh, g_f2w13, l, dres)
        dzb, dres, dlg[2], dlb[2] = ln_bwd(f"ln3_bwd_{l}", dy, sv["z3"], vec(lng[l, 2]), 1.0)
        dqc, dkvm = cross_bwd(f"cross_bwd_{l}", dzb, mwo, l, sv["crq"], sv["kvm"])
        G["mwo"] = mm_tn(f"cross_dwo_{l}", sv["cro"], dzb, "nat", l, G["mwo"])
        G["mwq"] = mm_tn(f"cross_dwq_{l}", sv["x2b"], dqc, "nat", l, G["mwq"])
        G["mwkv"] = mm_tn(f"cross_dwkv_{l}", memb, dkvm, "shard", l, G["mwkv"])
        dy = mm_nt_res(f"cross_dx_{l}", [dqc], [mwq], [l], dres, f32)
        dzb, dres, dlg[1], dlb[1] = ln_bwd(f"ln2_bwd_{l}", dy, sv["z2"], vec(lng[l, 1]), 1.0)
        dyp = mm_nt_res(f"mix_dpool_{l}", [dzb], [wout_pool], [l], None, bf16)
        do = mm_nt_res(f"mix_do_{l}", [dzb], [wout_mla], [l], None, bf16)
        dwo_p = mm_tn(f"mix_dwout_pool_{l}", sv["ypool"], dzb)
        dwo_m = mm_tn(f"mix_dwout_mla_{l}", sv["o"], dzb)
        small["wout"][l] = jnp.concatenate([dwo_p, dwo_m], axis=0)
        dq, dk, dv = mla_attn_bwd(f"mla_bwd_{l}", sv["q"], sv["k"], sv["v"], sv["o"], do, sv["lse"])
        dqe, dkv, dh_rest, dgq, dgkv = mix_post_bwd(f"mix_post_bwd_{l}", dq, dk, dv, wq_ext, wukv, l, sv["cq"], sv["ckv"],
                                                     vec(q_norm_g[l]), vec(kv_norm_g[l]), cs)
        du, dyw, dscale = pool_bwd(f"pool_bwd_{l}", dyp, sv["dpool"], wbd[l], vec(pool_scale[l]))
        dwq_e = mm_tn(f"mix_dwuq_{l}", sv["cqn"], dqe).reshape(Q_LORA, MLA_HEADS, 256)
        small["wuq"][l] = jnp.concatenate(
            [dwq_e[..., :QK_NOPE], _unswap_add(dwq_e[..., QK_NOPE:QK_NOPE + QK_ROPE], dwq_e[..., QK_NOPE + QK_ROPE:])],
            axis=-1).reshape(Q_LORA, MLA_HEADS * (QK_NOPE + QK_ROPE))
        small["wukv"][l] = mm_tn(f"mix_dwukv_{l}", sv["ckvn"], dkv)
        dwbd = mm_tn(f"pool_dw_{l}", sv["dpool"], dyw)
        small["pool_w"][l] = jnp.stack([dwbd[64 * gi:64 * gi + 64, 64 * gi:64 * gi + 64] for gi in range(4)])
        small["pool_scale"][l], small["gq"][l], small["gkv"][l] = dscale[0], dgq[0], dgkv[0]
        dh_ext = jnp.concatenate([du, dh_rest], axis=1)
        dwin_e = mm_tn(f"mix_dwin_{l}", sv["x1b"], dh_ext)
        small["win"][l] = jnp.concatenate(
            [dwin_e[:, :D_IN - QK_ROPE], _unswap_add(dwin_e[:, D_IN - QK_ROPE:D_IN], dwin_e[:, D_IN:])], axis=-1)
        dy = mm_nt_res(f"mix_dx_{l}", [dh_ext], [win_ext], [l], dres, f32)
        dzb, dres, dlg[0], dlb[0] = ln_bwd(f"ln1_bwd_{l}", dy, sv["z1"], vec(lng[l, 0]), 0.5)
        dh = ffn_bwd_da(f"ffn1_bwd_da_{l}", dzb, f1w2, l, sv["gate1"], sv["up1"])
        G["f1w2"] = mm_tn(f"ffn1_dw2_{l}", sv["act1"], dzb, "nat", l, G["f1w2"])
        G["f1w13"] = mm_tn(f"ffn1_dw13_{l}", sv["x0b"], dh, "shard", l, G["f1w13"])
        dy = ffn_dx(f"ffn1_dx_{l}", dh, g_f1w13, l, dres)
        small["lng"][l] = jnp.concatenate(dlg, axis=0)
        small["lnb"][l] = jnp.concatenate(dlb, axis=0)
    grad_x = dy[None]

    row_shards = lambda a, K: a.reshape(L, N_CHIPS, K // N_CHIPS, a.shape[-1])
    g_list = [G["f1w13"], row_shards(G["f1w2"], D_FF),
              jnp.stack(small["win"]).reshape(L, N_CHIPS, D_MODEL // N_CHIPS, D_IN),
              _to_col_shards(jnp.stack(small["wuq"])), _to_col_shards(jnp.stack(small["wukv"])),
              jnp.stack(small["wout"]).reshape(L, N_CHIPS, D_MODEL // N_CHIPS, D_MODEL),
              row_shards(G["mwq"], D_MODEL), G["mwkv"], row_shards(G["mwo"], D_MODEL),
              G["f2w13"], row_shards(G["f2w2"], D_FF)]
    big_grads = reduce_grads(g_list)

    rep = [jnp.stack(small["pool_w"]).reshape(-1), jnp.stack(small["pool_scale"]).reshape(-1),
           jnp.stack(small["gq"]).reshape(-1), jnp.stack(small["gkv"]).reshape(-1),
           jnp.stack(small["lng"]).reshape(-1), jnp.stack(small["lnb"]).reshape(-1)]
    sizes = [r.shape[0] for r in rep]
    packed = jnp.concatenate(rep)
    pad = (-packed.shape[0]) % 1024
    tot = allsum_small("allsum_small_grads", jnp.pad(packed, (0, pad)).reshape(-1, 128)).reshape(-1)
    offs = [0]
    for s_ in sizes:
        offs.append(offs[-1] + s_)
    parts = [tot[offs[i]:offs[i + 1]] for i in range(len(sizes))]
    g_pool_w = parts[0].reshape(pool_w.shape)
    g_pool_scale = parts[1].reshape(pool_scale.shape)
    g_gq = parts[2].reshape(q_norm_g.shape)
    g_gkv = parts[3].reshape(kv_norm_g.shape)
    shard_cols = lambda a: lax.dynamic_slice_in_dim(a.reshape(L, 4, D_MODEL), chip * (D_MODEL // N_CHIPS),
                                                    D_MODEL // N_CHIPS, axis=2)
    g_lng, g_lnb = shard_cols(parts[4]), shard_cols(parts[5])

    out_names = ("lng", "lnb", "f1w13", "f1w2", "win", "pool_w", "pool_scale", "gq", "wuq", "gkv", "wukv", "wout", "mwq",
                 "mwkv", "mwo", "f2w13", "f2w2")
    big = dict(lng=g_lng, lnb=g_lnb, pool_w=g_pool_w, pool_scale=g_pool_scale, gq=g_gq, gkv=g_gkv)
    late = ("f1w13", "f1w2")
    held = ("f2w13", "f2w2")
    ws = [ln_g, ln_b, ffn1_w13, ffn1_w2, w_in, pool_w, pool_scale, q_norm_g, w_uq, kv_norm_g, w_ukv, w_out, mem_wq,
          mem_wkv, mem_wo, ffn2_w13, ffn2_w2]
    ms = [m_ln_g, m_ln_b, m_ffn1_w13, m_ffn1_w2, m_w_in, m_pool_w, m_pool_scale, m_q_norm_g, m_w_uq, m_kv_norm_g, m_w_ukv,
          m_w_out, m_mem_wq, m_mem_wkv, m_mem_wo, m_ffn2_w13, m_ffn2_w2]
    vs = [v_ln_g, v_ln_b, v_ffn1_w13, v_ffn1_w2, v_w_in, v_pool_w, v_pool_scale, v_q_norm_g, v_w_uq, v_kv_norm_g, v_w_ukv,
          v_w_out, v_mem_wq, v_mem_wkv, v_mem_wo, v_ffn2_w13, v_ffn2_w2]
    res = {}

    def update(n, deps=()):
        a = out_names.index(n)
        res[a] = adamw(f"adamw_{a}", ws[a], big[n].reshape(ws[a].shape), ms[a], vs[a], deps)
        return res[a][0]

    small_done = tuple(update(n) for n in ("lng", "lnb", "pool_w", "pool_scale", "gq", "gkv"))
    big.update(zip(rest_names, pair_share_wait("pair_share_wait_a", share_a, small_done)))
    first_done = tuple(update(n) for n in rest_names if n not in held)
    sums_b = red_end("b0", st_cb, 0, [sums1[n] for n in late], first_done)
    share_b, tok_b = share_start("b", late, sums_b)
    held_done = tuple(update(n, (tok_b,)) for n in held)
    big.update(zip(late, pair_share_wait("pair_share_wait_b", share_b, held_done)))
    for n in late:
        update(n)
    order = range(len(out_names))
    grads = [big[n].reshape(w_.shape) for n, w_ in zip(out_names, ws)]
    return (loss, grad_x, *grads, *[res[a][0] for a in order], *[res[a][1] for a in order], *[res[a][2] for a in order])


_HBM_SPEC = pl.BlockSpec(memory_space=pltpu.HBM)
_SEM_SPEC = pl.BlockSpec(memory_space=pltpu.SEMAPHORE)
_ANY_SPEC = pl.BlockSpec(memory_space=pl.ANY)
_DATAFLOW = pltpu.SideEffectType.DATAFLOW_SIDE_EFFECTING


def _split_call(name, body_fn, bufs, sems_in, sems_out_sizes, after):
    nb, ni, no = len(bufs), len(sems_in), len(sems_out_sizes)
    afters = () if after is None else tuple(after) if isinstance(after, (tuple, list)) else (after,)

    def body(*refs):
        k = nb + ni + len(afters)
        body_fn(refs[:nb], refs[nb:nb + ni], refs[k:k + no])
        refs[-1][...] = jnp.zeros((8, 128), f32)

    outs = pl.pallas_call(
        body, name=name,
        in_specs=[_HBM_SPEC] * nb + [_SEM_SPEC] * ni + [_ANY_SPEC] * len(afters),
        out_specs=[_SEM_SPEC] * no + [_HBM_SPEC] * nb + [pl.BlockSpec(memory_space=pltpu.VMEM)],
        out_shape=[pltpu.SemaphoreType.DMA((s,)) for s in sems_out_sizes]
        + [pltpu.HBM(b.shape, b.dtype) for b in bufs] + [SDS((8, 128), f32)],
        input_output_aliases={i: no + i for i in range(nb)},
        compiler_params=pltpu.CompilerParams(has_side_effects=_DATAFLOW),
    )(*[pltpu.with_memory_space_constraint(b, pltpu.HBM) for b in bufs], *sems_in, *afters)
    return list(outs[no:no + nb]), list(outs[:no]), outs[-1]


def _rcopy(src, dst, ssem, rsem, to):
    return pltpu.make_async_remote_copy(src_ref=src, dst_ref=dst, send_sem=ssem, recv_sem=rsem, device_id=to,
                                        device_id_type=MESH)


def gather_start(name, groups, after):
    flat = [b for bufs, _ in groups for b in bufs]
    sizes = [3 * len(bufs) for bufs, _ in groups for _ in range(2)]

    def body_fn(b_in, s_in, s_out):
        x, y, c = _me()
        q = 2 * x + y
        chips = _other_chips(x, y)
        pos = 0
        for gi, (bufs, owner) in enumerate(groups):
            refs = b_in[pos:pos + len(bufs)]
            pos += len(bufs)

            @pl.when(c == owner)
            def _(refs=refs, send=s_out[2 * gi], recv=s_out[2 * gi + 1]):
                for a, r in enumerate(refs):
                    for k, (cx, cy) in enumerate(chips):
                        _rcopy(r.at[q], r.at[q], send.at[3 * a + k], recv.at[3 * a + k], (cx, cy, c)).start()

    outs, sems, token = _split_call(name, body_fn, flat, [], sizes, after)
    res, pos = [], 0
    for gi, (bufs, owner) in enumerate(groups):
        res.append((outs[pos:pos + len(bufs)], sems[2 * gi], sems[2 * gi + 1], owner))
        pos += len(bufs)
    return res, token


def gather_forward(name, grp, after):
    bufs, send, recv, owner = grp
    n3 = 3 * len(bufs)

    def body_fn(b_in, s_in, s_out):
        x, y, c = _me()
        q = 2 * x + y
        sibling = (x, y, 1 - c)
        chips = _other_chips(x, y)

        @pl.when(c == owner)
        def _():
            for a, r in enumerate(b_in):
                for k, (cx, cy) in enumerate(chips):
                    i = 3 * a + k
                    land = r.at[2 * cx + cy]
                    _rcopy(r.at[q], r.at[q], s_in[0].at[i], s_in[1].at[i], (cx, cy, c)).wait_send()
                    _rcopy(land, land, s_in[0].at[i], s_in[1].at[i], (cx, cy, c)).wait_recv()
                    _rcopy(land, land, s_out[0].at[i], s_out[1].at[i], sibling).start()

    outs, sems, token = _split_call(name, body_fn, bufs, [send, recv], [n3, n3], after)
    return (outs, sems[0], sems[1], owner), token


def gather_finish(name, grp, after):
    bufs, fsend, frecv, owner = grp

    def body_fn(b_in, s_in, s_out):
        x, y, c = _me()
        sibling = (x, y, 1 - c)
        chips = _other_chips(x, y)

        def each(wait):
            for a, r in enumerate(b_in):
                for k, (cx, cy) in enumerate(chips):
                    land = r.at[2 * cx + cy]
                    wait(_rcopy(land, land, s_in[0].at[3 * a + k], s_in[1].at[3 * a + k], sibling))

        @pl.when(c == owner)
        def _():
            each(lambda cp: cp.wait_send())

        @pl.when(c != owner)
        def _():
            each(lambda cp: cp.wait_recv())

    outs, _, _ = _split_call(name, body_fn, bufs, [fsend, frecv], [], after)
    return outs


def _by_owner(owners):
    return [[a for a, o_ in enumerate(owners) if o_ == o] for o in range(2)]


def pair_send_start(name, gs, owners, after):
    n = len(gs)
    lands = [lax.empty(g.shape, g.dtype) for g in gs]

    def body_fn(b_in, s_in, s_out):
        x, y, c = _me()
        for o, idx in enumerate(_by_owner(owners)):
            @pl.when(c == 1 - o)
            def _(o=o, idx=idx):
                for a in idx:
                    _rcopy(b_in[a], b_in[n + a], s_out[0].at[a], s_out[1].at[a], (x, y, o)).start()

    outs, sems, token = _split_call(name, body_fn, list(gs) + lands, [], [n, n], after)
    return (outs[:n], outs[n:], sems[0], sems[1], owners), token


def pair_send_wait(name, st, after):
    gs, lands, send, recv, owners = st
    n = len(gs)

    def body_fn(b_in, s_in, s_out):
        x, y, c = _me()
        for o, idx in enumerate(_by_owner(owners)):
            @pl.when(c == 1 - o)
            def _(o=o, idx=idx):
                for a in idx:
                    _rcopy(b_in[a], b_in[n + a], s_in[0].at[a], s_in[1].at[a], (x, y, o)).wait_send()

            @pl.when(c == o)
            def _(o=o, idx=idx):
                for a in idx:
                    _rcopy(b_in[a], b_in[n + a], s_in[0].at[a], s_in[1].at[a], (x, y, 1 - o)).wait_recv()

    outs, _, _ = _split_call(name, body_fn, list(gs) + list(lands), [send, recv], [], after)
    return outs[:n], outs[n:]


def chip_exchange_start(name, psums, owners, after):
    n = len(psums)
    lands = [lax.empty((3,) + p.shape[1:], p.dtype) for p in psums]

    def body_fn(b_in, s_in, s_out):
        x, y, c = _me()
        chips = _other_chips(x, y)
        for o, idx in enumerate(_by_owner(owners)):
            @pl.when(c == o)
            def _(idx=idx):
                for a in idx:
                    for k, (cx, cy) in enumerate(chips):
                        _rcopy(b_in[a].at[2 * cx + cy], b_in[n + a].at[k], s_out[0].at[3 * a + k],
                               s_out[1].at[3 * a + k], (cx, cy, c)).start()

    outs, sems, token = _split_call(name, body_fn, list(psums) + lands, [], [3 * n, 3 * n], after)
    return (outs[:n], outs[n:], sems[0], sems[1], owners), token


def chip_exchange_wait(name, st, after):
    psums, lands, send, recv, owners = st
    n = len(psums)

    def body_fn(b_in, s_in, s_out):
        x, y, c = _me()
        chips = _other_chips(x, y)
        for o, idx in enumerate(_by_owner(owners)):
            @pl.when(c == o)
            def _(idx=idx):
                for a in idx:
                    for k, (cx, cy) in enumerate(chips):
                        cp = _rcopy(b_in[a].at[2 * cx + cy], b_in[n + a].at[k], s_in[0].at[3 * a + k],
                                    s_in[1].at[3 * a + k], (cx, cy, c))
                        cp.wait_send()
                        cp.wait_recv()

    outs, _, _ = _split_call(name, body_fn, list(psums) + list(lands), [send, recv], [], after)
    return outs[:n], outs[n:]


def pair_sum(name, g, recv, flag):
    shape = g.shape
    cols = shape[-1]
    rows = math.prod(shape[:-1])
    tr = _row_tile(rows, cols, target=4 * 2**20)

    def body(f_ref, g_ref, r_ref, o_ref):
        o_ref[...] = (g_ref[...] + r_ref[...]).astype(bf16)

    blk = pl.BlockSpec((tr, cols), lambda i, f_ref: (i * f_ref[0], 0))
    out = pl.pallas_call(
        body, name=name,
        grid_spec=pltpu.PrefetchScalarGridSpec(num_scalar_prefetch=1, grid=(rows // tr,), in_specs=[blk, blk],
                                               out_specs=blk),
        out_shape=SDS((rows, cols), bf16), compiler_params=_cp("arbitrary"),
    )(flag, g.reshape(rows, cols), recv.reshape(rows, cols))
    return out.reshape(shape)


def chip_sum(name, psum, recv, qf_arr, layer, prev):
    shard = psum.shape[1:]
    cols = shard[-1]
    rows = math.prod(shard[:-1])
    tr = _row_tile(rows, cols, target=4 * 2**20)

    def body(qf_ref, p_ref, r_ref, *rest):
        rest[-1][0] = ((p_ref[0].astype(f32) + r_ref[0].astype(f32)) + r_ref[1].astype(f32)) + r_ref[2].astype(f32)

    in_specs = [pl.BlockSpec((1, tr, cols), lambda i, qf: (qf[0], i * qf[1], 0)),
                pl.BlockSpec((3, tr, cols), lambda i, qf: (0, i * qf[1], 0))]
    args = [qf_arr, psum.reshape(N_CHIPS, rows, cols), recv.reshape(3, rows, cols)]
    aliases = {}
    if prev is not None:
        in_specs.append(pl.BlockSpec(memory_space=pl.ANY))
        args.append(prev.reshape(DEPTH, rows, cols))
        aliases = {3: 0}
    out = pl.pallas_call(
        body, name=name,
        grid_spec=pltpu.PrefetchScalarGridSpec(
            num_scalar_prefetch=1, grid=(rows // tr,), in_specs=in_specs,
            out_specs=pl.BlockSpec((1, tr, cols), lambda i, qf: (layer, i * qf[1], 0))),
        out_shape=SDS((DEPTH, rows, cols), f32), input_output_aliases=aliases, compiler_params=_cp("arbitrary"),
    )(*args)
    return out.reshape((DEPTH,) + shard)


W_NAMES = ("f1w13", "f1w2", "win", "wuq", "wukv", "wout", "mwq", "mwkv", "mwo", "f2w13", "f2w2")
MIX_NAMES = ("win", "wuq", "wukv")
MID_NAMES = ("wout", "mwq", "mwkv", "mwo")
FFN2_NAMES = ("f2w13", "f2w2")
REDUCER = (dict(f1w13=0, f1w2=1, f2w13=0, win=0, wuq=0, wukv=0, f2w2=1, mwkv=1, wout=1, mwq=1, mwo=1),
           dict(f1w13=0, f2w2=0, mwkv=0, wout=0, f2w13=1, f1w2=1, mwq=1, mwo=1, win=1, wuq=1, wukv=1))


def kernel(x, mem, positions, ln_g, ln_b, ffn1_w13, ffn1_w2, w_in, pool_w, pool_scale, q_norm_g, w_uq, kv_norm_g, w_ukv, w_out, mem_wq, mem_wkv, mem_wo, ffn2_w13, ffn2_w2, loss_target, m_ln_g, m_ln_b, m_ffn1_w13, m_ffn1_w2, m_w_in, m_pool_w, m_pool_scale, m_q_norm_g, m_w_uq, m_kv_norm_g, m_w_ukv, m_w_out, m_mem_wq, m_mem_wkv, m_mem_wo, m_ffn2_w13, m_ffn2_w2, v_ln_g, v_ln_b, v_ffn1_w13, v_ffn1_w2, v_w_in, v_pool_w, v_pool_scale, v_q_norm_g, v_w_uq, v_kv_norm_g, v_w_ukv, v_w_out, v_mem_wq, v_mem_wkv, v_mem_wo, v_ffn2_w13, v_ffn2_w2):
    L = DEPTH
    qx, qy, _ = _me()
    chip = 2 * qx + qy
    vec = lambda a: a.reshape(1, -1)

    shards = dict(zip(W_NAMES, (ffn1_w13, ffn1_w2, w_in, w_uq, w_ukv, w_out, mem_wq, mem_wkv, mem_wo, ffn2_w13, ffn2_w2)))

    def place(sh, slot):
        return lax.dynamic_update_slice(lax.empty((N_CHIPS,) + sh.shape, bf16), sh.astype(bf16)[None],
                                        (slot,) + (0,) * sh.ndim)

    first = ("f1w13", "f1w2")
    bufs = [dict(), dict()]
    for n in first:
        bufs[0][n] = place(shards[n][0], chip)
    gw = [dict(), dict()]
    (g0,), tok = gather_start("gather_a_start", [([bufs[0][n] for n in first], 0)], None)
    chip_then = chip + tok[0, 0].astype(jnp.int32)
    for l in range(L):
        for n in W_NAMES:
            if n not in bufs[l]:
                bufs[l][n] = place(shards[n][l], chip_then)
    others = tuple(bufs[l][n] for l in range(L) for n in W_NAMES if (l, n) not in ((0, first[0]), (0, first[1])))
    g0, tok = gather_forward("gather_a_forward", g0, others)

    ln_pad = jnp.zeros((2, L, 4, N_CHIPS, D_MODEL // N_CHIPS), f32)
    ln_pad = lax.dynamic_update_slice(ln_pad, jnp.stack([ln_g, ln_b])[:, :, :, None, :], (0, 0, 0, chip, 0))
    ln_sum = allsum_small("allsum_ln", ln_pad.reshape(-1, 128), (tok,))
    ln_full = (ln_sum * 0.5).reshape(2, L, 4, D_MODEL)
    lng, lnb = ln_full[0], ln_full[1]

    gw[0]["f1w13"], gw[0]["f1w2"] = gather_finish("gather_a_finish", g0, ln_sum)
    (g_mix, g_mid, g_ffn2, g_l1), tok_b = gather_start(
        "gather_b_start",
        [([bufs[0][n] for n in MIX_NAMES], 0), ([bufs[0][n] for n in MID_NAMES], 0), ([bufs[0][n] for n in FFN2_NAMES], 0),
         ([bufs[1][n] for n in W_NAMES], 1)], ln_sum)

    half = QK_ROPE // 2
    inv_freq = ROPE_BASE ** (-jnp.arange(half, dtype=f32) / half)
    ang = positions[0].astype(f32)[:, None] * inv_freq
    cos, sin = jnp.cos(ang), jnp.sin(ang)
    cs = jnp.concatenate([cos, cos, sin, sin], axis=-1)

    memb = mem[0].astype(bf16)
    xf = x[0]
    xb = xf.astype(bf16)
    dep = (tok_b,)

    saved, W = [], [None, None]
    for l in range(L):
        sv = {}
        if l == 1:
            gl1 = gather_finish("gather_l1_finish", g_l1, xb)
            gw[1] = dict(zip(W_NAMES, gl1))
        sv["x0b"] = xb
        f1w13 = gw[l]["f1w13"][None]
        gate, up, act = ffn_up(f"ffn1_up_{l}", xb, f1w13, 0, dep)
        dep = ()
        if l == 0:
            g_mix, _ = gather_forward("gather_mix_forward", g_mix, act)
        z1, x1f, x1b = proj_res_ln(f"ffn1_down_{l}", [act], [gw[l]["f1w2"].reshape(1, D_FF, D_MODEL)], [0], xf,
                                   vec(lng[l, 0]), vec(lnb[l, 0]), 0.5)
        sv.update(gate1=gate, up1=up, act1=act, z1=z1, x1b=x1b)
        if l == 0:
            gw[0].update(zip(MIX_NAMES, gather_finish("gather_mix_finish", g_mix, x1b)))
            g_mid, _ = gather_forward("gather_mid_forward", g_mid, x1b)
        win = gw[l]["win"].reshape(D_MODEL, D_IN)
        win_ext = jnp.concatenate([win, _swap_half(win[:, D_IN - QK_ROPE:])], axis=-1)[None]
        wuq = _from_col_shards(gw[l]["wuq"]).reshape(Q_LORA, MLA_HEADS, QK_NOPE + QK_ROPE)
        wq_ext = jnp.concatenate([wuq, _swap_half(wuq[..., QK_NOPE:])], axis=-1).transpose(1, 0, 2)[None]
        wukv = _from_col_shards(gw[l]["wukv"])[None]
        wbd = _block_diag(pool_w[l][None].astype(bf16))[0]
        u, cq, ckv, cqn, ckvn, q, k, v = mix_pre(f"mix_pre_{l}", x1b, win_ext, wq_ext, wukv, 0,
                                                   vec(q_norm_g[l]), vec(kv_norm_g[l]), cs)
        dpool, ypool = pool_fwd(f"pool_fwd_{l}", u, wbd, vec(pool_scale[l]))
        o, lse = mla_attn_fwd(f"mla_fwd_{l}", q, k, v)
        if l == 0:
            gw[0].update(zip(MID_NAMES, gather_finish("gather_mid_finish", g_mid, o)))
            g_ffn2, tok_f = gather_forward("gather_ffn2_forward", g_ffn2, o)
            g_l1, tok_l = gather_forward("gather_l1_forward", g_l1, o)
            dep = (tok_f, tok_l)
        wout = gw[l]["wout"].reshape(D_MODEL, D_MODEL)
        wout_pool, wout_mla = wout[None, :POOL_WIDTH], wout[None, POOL_WIDTH:]
        mwq = gw[l]["mwq"].reshape(1, D_MODEL, D_MODEL)
        mwo = gw[l]["mwo"].reshape(1, D_MODEL, D_MODEL)
        mwkv = gw[l]["mwkv"][None]
        z2, x2f, x2b = proj_res_ln(f"mix_out_{l}", [ypool, o], [wout_pool, wout_mla], [0, 0], x1f,
                                   vec(lng[l, 1]), vec(lnb[l, 1]), 1.0, dep)
        dep = ()
        sv.update(cq=cq, ckv=ckv, cqn=cqn, ckvn=ckvn, q=q, k=k, v=v, dpool=dpool, ypool=ypool, o=o, lse=lse, z2=z2, x2b=x2b)
        kvm = mm_nn_shard(f"mem_kv_{l}", memb, mwkv, 0)
        cq_, co_, z3, x3f, x3b = cross_fwd(f"cross_fwd_{l}", x2b, x2f, mwq, mwo, 0, kvm, vec(lng[l, 2]), vec(lnb[l, 2]))
        sv.update(kvm=kvm, crq=cq_, cro=co_, z3=z3, x3b=x3b)
        if l == 0:
            gw[0].update(zip(FFN2_NAMES, gather_finish("gather_ffn2_finish", g_ffn2, x3b)))
        f2w13 = gw[l]["f2w13"][None]
        f2w2 = gw[l]["f2w2"].reshape(1, D_FF, D_MODEL)
        gate, up, act = ffn_up(f"ffn2_up_{l}", x3b, f2w13, 0)
        z4, xf, xb = proj_res_ln(f"ffn2_down_{l}", [act], [f2w2], [0], x3f, vec(lng[l, 3]), vec(lnb[l, 3]), 0.5)
        sv.update(gate2=gate, up2=up, act2=act, z4=z4)
        W[l] = dict(f1w13=f1w13, f1w2=gw[l]["f1w2"].reshape(1, D_FF, D_MODEL), win_ext=win_ext, wq_ext=wq_ext, wukv=wukv,
                    wbd=wbd, wout_pool=wout_pool, wout_mla=wout_mla, mwq=mwq, mwo=mwo, f2w13=f2w13, f2w2=f2w2)
        saved.append(sv)

    dln = {}
    dzb, dres, *dln[L - 1, 3], loss_blk = loss_grad("loss_grad", xf, loss_target[0],
                                                   (saved[L - 1]["z4"], vec(lng[L - 1, 3]), 0.5))
    loss = lax.psum(loss_blk[0, 0], ("x", "y", "c"))

    row_shards = lambda a: a.reshape(N_CHIPS, a.shape[0] // N_CHIPS, a.shape[1])
    small = {k_: [None] * L for k_ in ("pool_w", "pool_scale", "gq", "gkv", "lng", "lnb")}
    rest_names = [n for n in W_NAMES if n not in ("f1w13", "f1w2")]
    core = lax.axis_index("c")
    flags = [jnp.reshape(core == o, (1,)).astype(jnp.int32) for o in range(2)]
    qfs = [jnp.stack([chip, (core == o).astype(jnp.int32)]).astype(jnp.int32) for o in range(2)]

    def red_begin(tag, names, gs, layer):
        owners = [REDUCER[layer][n] for n in names]
        st, tok_ = pair_send_start(f"pair_send_start_{tag}", gs, owners, None)
        return (st, owners), tok_

    def red_mid(tag, sto, after):
        st, owners = sto
        gs_, lands_ = pair_send_wait(f"pair_send_wait_{tag}", st, after)
        ps = [pair_sum(f"pair_sum_{tag}_{a}", g_, r_, flags[o]) for a, (g_, r_, o) in enumerate(zip(gs_, lands_, owners))]
        st, tok_ = chip_exchange_start(f"chip_exchange_start_{tag}", ps, owners, None)
        return (st, owners), tok_

    def red_end(tag, sto, layer, prevs, after):
        st, owners = sto
        ps, lands_ = chip_exchange_wait(f"chip_exchange_wait_{tag}", st, after)
        return [chip_sum(f"chip_sum_{tag}_{a}", p_, r_, qfs[o], layer, s_)
                for a, (p_, r_, s_, o) in enumerate(zip(ps, lands_, prevs, owners))]

    def share_start(tag, names, sums_):
        return pair_share_start(f"pair_share_start_{tag}", sums_, [(REDUCER[0][n], REDUCER[1][n]) for n in names], None)

    st_p1 = st_c1 = st_pa = st_ca = None
    for l in reversed(range(L)):
        sv, w = saved[l], W[l]
        g = {}
        dh = ffn_bwd_da(f"ffn2_bwd_da_{l}", dzb, w["f2w2"], 0, sv["gate2"], sv["up2"], dep)
        dep = ()
        g["f2w2"] = row_shards(mm_tn(f"ffn2_dw2_{l}", sv["act2"], dzb))
        g["f2w13"] = mm_tn(f"ffn2_dw13_{l}", sv["x3b"], dh, True)
        dzb, dres, *dln[l, 2] = ffn_dx(f"ffn2_dx_{l}", dh, w["f2w13"], 0, dres, (sv["z3"], vec(lng[l, 2]), 1.0))
        if l == 0:
            st_c1, tok = red_mid("l1", st_p1, dzb)
            dep = (tok, g["f2w2"], g["f2w13"])
        dqc, dkvm = cross_bwd(f"cross_bwd_{l}", dzb, w["mwo"], 0, sv["crq"], sv["kvm"], dep)
        dep = ()
        g["mwo"] = row_shards(mm_tn(f"cross_dwo_{l}", sv["cro"], dzb))
        g["mwq"] = row_shards(mm_tn(f"cross_dwq_{l}", sv["x2b"], dqc))
        g["mwkv"] = mm_tn(f"cross_dwkv_{l}", memb, dkvm, True)
        dzb, dres, *dln[l, 1] = mm_nt_res(f"cross_dx_{l}", [dqc], [w["mwq"]], [0], dres, f32,
                                          (sv["z2"], vec(lng[l, 1]), 1.0))
        dyp = mm_nt_res(f"mix_dpool_{l}", [dzb], [w["wout_pool"]], [0], None, bf16)
        do = mm_nt_res(f"mix_do_{l}", [dzb], [w["wout_mla"]], [0], None, bf16)
        dwo_p = mm_tn(f"mix_dwout_pool_{l}", sv["ypool"], dzb)
        dwo_m = mm_tn(f"mix_dwout_mla_{l}", sv["o"], dzb)
        g["wout"] = row_shards(jnp.concatenate([dwo_p, dwo_m], axis=0))
        dq, dk, dv = mla_attn_bwd(f"mla_bwd_{l}", sv["q"], sv["k"], sv["v"], sv["o"], do, sv["lse"])
        dqe, dkv, dh_rest, dgq, dgkv = mix_post_bwd(f"mix_post_bwd_{l}", dq, dk, dv, w["wq_ext"], w["wukv"], 0, sv["cq"],
                                                     sv["ckv"], vec(q_norm_g[l]), vec(kv_norm_g[l]), cs)
        du, dyw, dscale = pool_bwd(f"pool_bwd_{l}", dyp, sv["dpool"], w["wbd"], vec(pool_scale[l]))
        dwq_e = mm_tn(f"mix_dwuq_{l}", sv["cqn"], dqe).reshape(Q_LORA, MLA_HEADS, 256)
        g["wuq"] = _to_col_shards(jnp.concatenate(
            [dwq_e[..., :QK_NOPE], _unswap_add(dwq_e[..., QK_NOPE:QK_NOPE + QK_ROPE], dwq_e[..., QK_NOPE + QK_ROPE:])],
            axis=-1).reshape(Q_LORA, MLA_HEADS * (QK_NOPE + QK_ROPE)))
        g["wukv"] = _to_col_shards(mm_tn(f"mix_dwukv_{l}", sv["ckvn"], dkv))
        dwbd = mm_tn(f"pool_dw_{l}", sv["dpool"], dyw)
        small["pool_w"][l] = jnp.stack([dwbd[64 * gi:64 * gi + 64, 64 * gi:64 * gi + 64] for gi in range(4)])
        small["pool_scale"][l], small["gq"][l], small["gkv"][l] = dscale[0], dgq[0], dgkv[0]
        dh_ext = jnp.concatenate([du, dh_rest], axis=1)
        dwin_e = mm_tn(f"mix_dwin_{l}", sv["x1b"], dh_ext)
        g["win"] = row_shards(jnp.concatenate(
            [dwin_e[:, :D_IN - QK_ROPE], _unswap_add(dwin_e[:, D_IN - QK_ROPE:D_IN], dwin_e[:, D_IN:])], axis=-1))
        dzb, dres, *dln[l, 0] = mm_nt_res(f"mix_dx_{l}", [dh_ext], [w["win_ext"]], [0], dres, f32,
                                          (sv["z1"], vec(lng[l, 0]), 0.5))
        if l == 0:
            st_pa, tok = red_begin("a0", rest_names, [g[n] for n in rest_names], 0)
            dep = (tok,)
        dh = ffn_bwd_da(f"ffn1_bwd_da_{l}", dzb, w["f1w2"], 0, sv["gate1"], sv["up1"], dep)
        dep = ()
        if l == 0:
            grad_x = ffn_dx(f"ffn1_dx_{l}", dh, w["f1w13"], 0, dres)[None]
            st_ca, tok = red_mid("a0", st_pa, grad_x)
            dep = (tok,)
        else:
            below = ffn_dx(f"ffn1_dx_{l}", dh, w["f1w13"], 0, dres, (saved[l - 1]["z4"], vec(lng[l - 1, 3]), 0.5))
            dln[l - 1, 3] = below[2:]
        g["f1w2"] = row_shards(mm_tn(f"ffn1_dw2_{l}", sv["act1"], dzb, False, dep))
        g["f1w13"] = mm_tn(f"ffn1_dw13_{l}", sv["x0b"], dh, True, dep)
        dep = ()
        if l > 0:
            dzb, dres = below[:2]
        if l == 1:
            st_p1, tok = red_begin("l1", W_NAMES, [g[n] for n in W_NAMES], 1)
            dep = (tok,)
    for l in range(L):
        small["lng"][l] = jnp.concatenate([dln[l, k][0] for k in range(4)], axis=0)
        small["lnb"][l] = jnp.concatenate([dln[l, k][1] for k in range(4)], axis=0)

    st_pb, _ = red_begin("b0", ("f1w13", "f1w2"), [g["f1w13"], g["f1w2"]], 0)
    sums1 = dict(zip(W_NAMES, red_end("l1", st_c1, 1, [None] * len(W_NAMES), g["f1w13"])))
    st_cb, tok = red_mid("b0", st_pb, tuple(sums1.values()))
    sums0 = red_end("a0", st_ca, 0, [sums1[n] for n in rest_names], tok)
    share_a, tok_a = share_start("a", rest_names, sums0)

    rep = [jnp.stack(small["pool_w"]).reshape(-1), jnp.stack(small["pool_scale"]).reshape(-1),
           jnp.stack(small["gq"]).reshape(-1), jnp.stack(small["gkv"]).reshape(-1),
           jnp.stack(small["lng"]).reshape(-1), jnp.stack(small["lnb"]).reshape(-1)]
    sizes = [r.shape[0] for r in rep]
    packed = jnp.concatenate(rep)
    pad = (-packed.shape[0]) % 1024
    tot = allsum_small("allsum_small_grads", jnp.pad(packed, (0, pad)).reshape(-1, 128), (tok_a,)).reshape(-1)
    offs = [0]
    for s_ in sizes:
        offs.append(offs[-1] + s_)
    parts = [tot[offs[i]:offs[i + 1]] for i in range(len(sizes))]
    g_pool_w = parts[0].reshape(pool_w.shape)
    g_pool_scale = parts[1].reshape(pool_scale.shape)
    g_gq = parts[2].reshape(q_norm_g.shape)
    g_gkv = parts[3].reshape(kv_norm_g.shape)
    shard_cols = lambda a: lax.dynamic_slice_in_dim(a.reshape(L, 4, D_MODEL), chip * (D_MODEL // N_CHIPS),
                                                    D_MODEL // N_CHIPS, axis=2)
    g_lng, g_lnb = shard_cols(parts[4]), shard_cols(parts[5])

    out_names = ("lng", "lnb", "f1w13", "f1w2", "win", "pool_w", "pool_scale", "gq", "wuq", "gkv", "wukv", "wout", "mwq",
                 "mwkv", "mwo", "f2w13", "f2w2")
    big = dict(lng=g_lng, lnb=g_lnb, pool_w=g_pool_w, pool_scale=g_pool_scale, gq=g_gq, gkv=g_gkv)
    late = ("f1w13", "f1w2")
    held = ("f2w13", "f2w2")
    ws = [ln_g, ln_b, ffn1_w13, ffn1_w2, w_in, pool_w, pool_scale, q_norm_g, w_uq, kv_norm_g, w_ukv, w_out, mem_wq,
          mem_wkv, mem_wo, ffn2_w13, ffn2_w2]
    ms = [m_ln_g, m_ln_b, m_ffn1_w13, m_ffn1_w2, m_w_in, m_pool_w, m_pool_scale, m_q_norm_g, m_w_uq, m_kv_norm_g, m_w_ukv,
          m_w_out, m_mem_wq, m_mem_wkv, m_mem_wo, m_ffn2_w13, m_ffn2_w2]
    vs = [v_ln_g, v_ln_b, v_ffn1_w13, v_ffn1_w2, v_w_in, v_pool_w, v_pool_scale, v_q_norm_g, v_w_uq, v_kv_norm_g, v_w_ukv,
          v_w_out, v_mem_wq, v_mem_wkv, v_mem_wo, v_ffn2_w13, v_ffn2_w2]
    res = {}

    def update(n, deps=()):
        a = out_names.index(n)
        res[a] = adamw(f"adamw_{a}", ws[a], big[n].reshape(ws[a].shape), ms[a], vs[a], deps)
        return res[a][0]

    small_done = tuple(update(n) for n in ("lng", "lnb", "pool_w", "pool_scale", "gq", "gkv"))
    big.update(zip(rest_names, pair_share_wait("pair_share_wait_a", share_a, small_done)))
    first_done = tuple(update(n) for n in rest_names if n not in held)
    sums_b = red_end("b0", st_cb, 0, [sums1[n] for n in late], first_done)
    share_b, tok_b = share_start("b", late, sums_b)
    held_done = tuple(update(n, (tok_b,)) for n in held)
    big.update(zip(late, pair_share_wait("pair_share_wait_b", share_b, held_done)))
    for n in late:
        update(n)
    order = range(len(out_names))
    grads = [big[n].reshape(w_.shape) for n, w_ in zip(out_names, ws)]
    return (loss, grad_x, *grads, *[res[a][0] for a in order], *[res[a][1] for a in order], *[res[a][2] for a in order])
```

```python
import functools
import math

import jax
import jax.numpy as jnp
from jax import lax
from jax.experimental import pallas as pl
from jax.experimental.pallas import tpu as pltpu

f32 = jnp.float32
bf16 = jnp.bfloat16
SDS = jax.ShapeDtypeStruct
MESH = pl.DeviceIdType.MESH

D_MODEL = 1024
DEPTH = 2
N_MEM = 256
MEM_HEADS = 4
MEM_HEAD_DIM = D_MODEL // MEM_HEADS
POOL_WINDOWS = (2, 4, 8, 16)
POOL_WIDTH = 256
POOL_GROUP = 64
QK_NOPE = 128
QK_ROPE = 64
V_HEAD = 128
MLA_HEADS = 6
Q_LORA = 256
KV_LORA = 128
ROPE_BASE = 10000.0
D_FF = 2816
D_IN = POOL_WIDTH + Q_LORA + KV_LORA + QK_ROPE
ALPHA = (2 * DEPTH) ** 0.25
LN_EPS = 1e-5
RMS_EPS = 1e-6
NEG_INF = -1e30
MLA_SCALE = (QK_NOPE + QK_ROPE) ** -0.5
MEM_SCALE = MEM_HEAD_DIM ** -0.5
ADAM_LR = 0.001
ADAM_B1 = 0.9
ADAM_B2 = 0.999
ADAM_EPS = 1e-08
ADAM_WD = 0.01
ADAM_STEP = 10

N_CHIPS = 4
V7X_VMEM_LIMIT = 56 * 2**20
HALO = 16

_NT = (((1,), (1,)), ((), ()))
_TN = (((0,), (0,)), ((), ()))


def _dot(a, b):
    return jnp.dot(a, b, preferred_element_type=f32)


def _dot_nt(a, b):
    return lax.dot_general(a, b, _NT, preferred_element_type=f32)


def _dot_tn(a, b):
    return lax.dot_general(a, b, _TN, preferred_element_type=f32)


def _cp(*sem):
    return pltpu.CompilerParams(dimension_semantics=sem if sem else None, vmem_limit_bytes=V7X_VMEM_LIMIT)


_DEP_SPEC = pl.BlockSpec(memory_space=pl.ANY)


def _with_deps(body, n_in, deps):
    nd = len(deps)
    if not nd:
        return body

    def wrapped(*refs):
        return body(*refs[:n_in], *refs[n_in + nd:])

    return wrapped


def _tile(n, t):
    t = min(n, t)
    assert n % t == 0, (n, t)
    return t


def _row_tile(rows, cols, itemsize=4, target=2 * 2**20):
    best = None
    for t in range(16, rows + 1, 16):
        if rows % t == 0 and t * cols * itemsize <= target:
            best = t
    return best if best is not None else rows


def ffn_up(name, xb, w13, l, deps=()):
    S = xb.shape[0]
    ns = w13.shape[3]
    tm = _tile(S, 512)

    def body(x_ref, wg_ref, wu_ref, g_ref, u_ref, a_ref):
        x = x_ref[...]
        g = _dot(x, wg_ref[0, 0])
        u = _dot(x, wu_ref[0, 0])
        a = g * jax.nn.sigmoid(g) * u
        g_ref[...] = g.astype(bf16)
        u_ref[...] = u.astype(bf16)
        a_ref[...] = a.astype(bf16)

    out = SDS((S, 2 * ns), bf16)
    return pl.pallas_call(
        _with_deps(body, 3, deps), name=name, grid=(2, S // tm),
        in_specs=[pl.BlockSpec((tm, D_MODEL), lambda j, i: (i, 0)),
                  pl.BlockSpec((1, 1, D_MODEL, ns), lambda j, i: (l, j, 0, 0)),
                  pl.BlockSpec((1, 1, D_MODEL, ns), lambda j, i: (l, j + 2, 0, 0))] + [_DEP_SPEC] * len(deps),
        out_specs=[pl.BlockSpec((tm, ns), lambda j, i: (i, j))] * 3,
        out_shape=[out, out, out],
        compiler_params=_cp("parallel", "parallel"),
    )(xb, w13, w13, *deps)


def proj_res_ln(name, parts, ws, wl, x, g, b, rscale, deps=()):
    S = x.shape[0]
    tm = _tile(S, 512)
    n = len(parts)

    def body(*refs):
        p_refs, w_refs = refs[:n], refs[n:2 * n]
        x_ref, g_ref, b_ref, z_ref, y_ref, yb_ref = refs[2 * n:]
        acc = _dot(p_refs[0][...], w_refs[0][0])
        for k in range(1, n):
            acc = acc + _dot(p_refs[k][...], w_refs[k][0])
        if rscale != 1.0:
            acc = rscale * acc
        z = ALPHA * x_ref[...] + acc
        mu = jnp.mean(z, axis=-1, keepdims=True)
        zc = z - mu
        var = jnp.mean(zc * zc, axis=-1, keepdims=True)
        y = zc * lax.rsqrt(var + LN_EPS) * g_ref[...] + b_ref[...]
        z_ref[...] = z
        y_ref[...] = y
        yb_ref[...] = y.astype(bf16)

    row = lambda i: (i, 0)
    in_specs = [pl.BlockSpec((tm, p.shape[1]), row) for p in parts]
    in_specs += [pl.BlockSpec((1,) + w.shape[1:], functools.partial(lambda li, i: (li, 0, 0), li)) for w, li in zip(ws, wl)]
    in_specs += [pl.BlockSpec((tm, D_MODEL), row), pl.BlockSpec((1, D_MODEL), lambda i: (0, 0)),
                 pl.BlockSpec((1, D_MODEL), lambda i: (0, 0))] + [_DEP_SPEC] * len(deps)
    return pl.pallas_call(
        _with_deps(body, 2 * n + 3, deps), name=name, grid=(S // tm,), in_specs=in_specs,
        out_specs=[pl.BlockSpec((tm, D_MODEL), row)] * 3,
        out_shape=[SDS((S, D_MODEL), f32), SDS((S, D_MODEL), f32), SDS((S, D_MODEL), bf16)],
        compiler_params=_cp("parallel"),
    )(*parts, *ws, x, g, b, *deps)


def _ln_bwd_store(dyv, z_ref, g_ref, rscale, first, dzb_ref, dres_ref, dg_ref, db_ref):
    z = z_ref[...]
    mu = jnp.mean(z, axis=-1, keepdims=True)
    zc = z - mu
    rstd = lax.rsqrt(jnp.mean(zc * zc, axis=-1, keepdims=True) + LN_EPS)
    xhat = zc * rstd
    dxh = dyv * g_ref[...]
    m1 = jnp.mean(dxh, axis=-1, keepdims=True)
    m2 = jnp.mean(dxh * xhat, axis=-1, keepdims=True)
    dz = rstd * (dxh - m1 - xhat * m2)
    dzb_ref[...] = (rscale * dz).astype(bf16)
    dres_ref[...] = ALPHA * dz

    @pl.when(first)
    def _():
        dg_ref[...] = jnp.zeros_like(dg_ref)
        db_ref[...] = jnp.zeros_like(db_ref)

    dg_ref[...] += jnp.sum(dyv * xhat, axis=0, keepdims=True)
    db_ref[...] += jnp.sum(dyv, axis=0, keepdims=True)


def _ln_bwd_specs(S, tm, index):
    vec = pl.BlockSpec((1, D_MODEL), lambda *a: (0, 0))
    blk = pl.BlockSpec((tm, D_MODEL), index)
    in_specs = [blk, vec]
    out_specs = [blk, blk, vec, vec]
    out_shape = [SDS((S, D_MODEL), bf16), SDS((S, D_MODEL), f32), SDS((1, D_MODEL), f32), SDS((1, D_MODEL), f32)]
    return in_specs, out_specs, out_shape


def ffn_bwd_da(name, drb, w2, l, gate, up, deps=()):
    S = drb.shape[0]
    tm = _tile(S, 512)
    nh = D_FF // 2

    def body(dr_ref, w_ref, g_ref, u_ref, dh_ref):
        dr = dr_ref[...]
        for j in range(2):
            cols = slice(j * nh, (j + 1) * nh)
            da = _dot_nt(dr, w_ref[0, cols, :])
            g = g_ref[:, cols].astype(f32)
            u = u_ref[:, cols].astype(f32)
            sg = jax.nn.sigmoid(g)
            dh_ref[:, cols] = (da * u * (sg * (1.0 + g * (1.0 - sg)))).astype(bf16)
            dh_ref[:, D_FF + j * nh:D_FF + (j + 1) * nh] = (da * (g * sg)).astype(bf16)

    row = lambda i: (i, 0)
    return pl.pallas_call(
        _with_deps(body, 4, deps), name=name, grid=(S // tm,),
        in_specs=[pl.BlockSpec((tm, D_MODEL), row), pl.BlockSpec((1, D_FF, D_MODEL), lambda i: (l, 0, 0)),
                  pl.BlockSpec((tm, D_FF), row), pl.BlockSpec((tm, D_FF), row)] + [_DEP_SPEC] * len(deps),
        out_specs=pl.BlockSpec((tm, 2 * D_FF), row),
        out_shape=SDS((S, 2 * D_FF), bf16),
        compiler_params=_cp("parallel"),
    )(drb, w2, gate, up, *deps)


def ffn_dx(name, dh, w13, l, res, ln=None):
    S = dh.shape[0]
    ns = w13.shape[3]
    tm = _tile(S, 1024)
    last = N_CHIPS - 1
    row = lambda i, j: (i, 0)
    in_specs = [pl.BlockSpec((tm, ns), lambda i, j: (i, j)),
                pl.BlockSpec((1, 1, D_MODEL, ns), lambda i, j: (l, j, 0, 0)),
                pl.BlockSpec((tm, D_MODEL), row)]
    if ln is None:
        def body(dh_ref, w_ref, r_ref, o_ref):
            @pl.when(pl.program_id(1) == 0)
            def _():
                o_ref[...] = r_ref[...]

            o_ref[...] += _dot_nt(dh_ref[...], w_ref[0, 0])

        return pl.pallas_call(
            body, name=name, grid=(S // tm, N_CHIPS), in_specs=in_specs,
            out_specs=pl.BlockSpec((tm, D_MODEL), row), out_shape=SDS((S, D_MODEL), f32),
            compiler_params=_cp("parallel", "arbitrary"),
        )(dh, w13, res)

    z, g, rscale = ln

    def body_ln(dh_ref, w_ref, r_ref, z_ref, g_ref, dzb_ref, dres_ref, dg_ref, db_ref, acc_sc):
        i, j = pl.program_id(0), pl.program_id(1)

        @pl.when(j == 0)
        def _():
            acc_sc[...] = r_ref[...]

        acc_sc[...] += _dot_nt(dh_ref[...], w_ref[0, 0])

        @pl.when(j == last)
        def _():
            _ln_bwd_store(acc_sc[...], z_ref, g_ref, rscale, i == 0, dzb_ref, dres_ref, dg_ref, db_ref)

    ln_in, ln_out, ln_shape = _ln_bwd_specs(S, tm, row)
    return pl.pallas_call(
        body_ln, name=name, grid=(S // tm, N_CHIPS), in_specs=in_specs + ln_in, out_specs=ln_out, out_shape=ln_shape,
        scratch_shapes=[pltpu.VMEM((tm, D_MODEL), f32)], compiler_params=_cp("arbitrary", "arbitrary"),
    )(dh, w13, res, z, g)


def mm_nt_res(name, dys, ws, wl, res, out_dtype, ln=None):
    S = dys[0].shape[0]
    K = ws[0].shape[1]
    tm = _tile(S, 512)
    n = len(dys)
    n_in = 2 * n + (res is not None)

    def product(refs):
        acc = _dot_nt(refs[0][...], refs[n][0])
        for k in range(1, n):
            acc = acc + _dot_nt(refs[k][...], refs[n + k][0])
        if res is not None:
            acc = acc + refs[2 * n][...]
        return acc

    def body(*refs):
        refs[-1][...] = product(refs).astype(out_dtype)

    def body_ln(*refs):
        z_ref, g_ref, dzb_ref, dres_ref, dg_ref, db_ref = refs[n_in:]
        _ln_bwd_store(product(refs), z_ref, g_ref, ln[2], pl.program_id(0) == 0, dzb_ref, dres_ref, dg_ref, db_ref)

    row = lambda i: (i, 0)
    in_specs = [pl.BlockSpec((tm, d.shape[1]), row) for d in dys]
    in_specs += [pl.BlockSpec((1,) + w.shape[1:], functools.partial(lambda li, i: (li, 0, 0), li)) for w, li in zip(ws, wl)]
    args = list(dys) + list(ws)
    if res is not None:
        in_specs.append(pl.BlockSpec((tm, K), row))
        args.append(res)
    if ln is None:
        return pl.pallas_call(
            body, name=name, grid=(S // tm,), in_specs=in_specs,
            out_specs=pl.BlockSpec((tm, K), row), out_shape=SDS((S, K), out_dtype),
            compiler_params=_cp("parallel"),
        )(*args)
    ln_in, ln_out, ln_shape = _ln_bwd_specs(S, tm, row)
    return pl.pallas_call(
        body_ln, name=name, grid=(S // tm,), in_specs=in_specs + ln_in, out_specs=ln_out, out_shape=ln_shape,
        compiler_params=_cp("arbitrary"),
    )(*args, ln[0], ln[1])


def mm_tn(name, x, dy, col_shards=False, deps=()):
    S, K = x.shape
    N = dy.shape[1]
    ts = 512
    while ts * 2 <= min(S, 2048) and S % (ts * 2) == 0 and ts * 2 * K * 2 <= 6 * 2**20:
        ts *= 2
    ts = _tile(S, ts)
    if col_shards:
        tn = N // N_CHIPS
    else:
        tn = N
        while K * tn * 4 > 6 * 2**20 and tn % 256 == 0:
            tn //= 2
    nn = N // tn
    lead = ((0,) if col_shards else ()) + (slice(None), slice(None))

    def body(x_ref, dy_ref, o_ref):
        acc = _dot_tn(x_ref[...].astype(bf16), dy_ref[...].astype(bf16))

        @pl.when(pl.program_id(1) == 0)
        def _():
            o_ref[lead] = acc

        @pl.when(pl.program_id(1) != 0)
        def _():
            o_ref[lead] += acc

    if col_shards:
        out_spec = pl.BlockSpec((1, K, tn), lambda n, s: (n, 0, 0))
        out_shape = SDS((N_CHIPS, K, tn), f32)
    else:
        out_spec = pl.BlockSpec((K, tn), lambda n, s: (0, n))
        out_shape = SDS((K, N), f32)
    return pl.pallas_call(
        _with_deps(body, 2, deps), name=name, grid=(nn, S // ts),
        in_specs=[pl.BlockSpec((ts, K), lambda n, s: (s, 0)), pl.BlockSpec((ts, tn), lambda n, s: (s, n))]
        + [_DEP_SPEC] * len(deps),
        out_specs=out_spec, out_shape=out_shape, compiler_params=_cp("parallel", "arbitrary"),
    )(x, dy, *deps)


def mm_nn_shard(name, x, w, l):
    S, K = x.shape
    ns = w.shape[3]

    def body(x_ref, w_ref, o_ref):
        o_ref[...] = _dot(x_ref[...], w_ref[0, 0]).astype(bf16)

    return pl.pallas_call(
        body, name=name, grid=(N_CHIPS,),
        in_specs=[pl.BlockSpec((S, K), lambda j: (0, 0)), pl.BlockSpec((1, 1, K, ns), lambda j: (l, j, 0, 0))],
        out_specs=pl.BlockSpec((S, ns), lambda j: (0, j)), out_shape=SDS((S, N_CHIPS * ns), bf16),
        compiler_params=_cp("parallel"),
    )(x, w)


def loss_grad(name, y, t, ln):
    S = y.shape[0]
    tm = _tile(S, 512)
    z, g, rscale = ln

    def body(y_ref, t_ref, z_ref, g_ref, dzb_ref, dres_ref, dg_ref, db_ref, loss_ref):
        first = pl.program_id(0) == 0
        e = y_ref[...] - t_ref[...]
        _ln_bwd_store(e * (1.0 / D_MODEL), z_ref, g_ref, rscale, first, dzb_ref, dres_ref, dg_ref, db_ref)

        @pl.when(first)
        def _():
            loss_ref[...] = jnp.zeros_like(loss_ref)

        loss_ref[...] += jnp.full(loss_ref.shape, (0.5 / D_MODEL) * jnp.sum(e * e), f32)

    row = lambda i: (i, 0)
    ln_in, ln_out, ln_shape = _ln_bwd_specs(S, tm, row)
    return pl.pallas_call(
        body, name=name, grid=(S // tm,),
        in_specs=[pl.BlockSpec((tm, D_MODEL), row)] * 2 + ln_in,
        out_specs=ln_out + [pl.BlockSpec((8, 128), lambda i: (0, 0))],
        out_shape=ln_shape + [SDS((8, 128), f32)],
        compiler_params=_cp("arbitrary"),
    )(y, t, z, g)


def _half_sum(t):
    return t + pltpu.roll(t, 64, axis=1)


def mix_pre(name, xb, w_in, wq, wkv, l, gq, gkv, cs):
    S = xb.shape[0]
    tm = _tile(S, 512)
    H = MLA_HEADS
    W_EXT = w_in.shape[2]

    def body(x_ref, win_ref, wq_ref, wkv_ref, gq_ref, gkv_ref, cs_ref,
             u_ref, cq_ref, ckv_ref, cqn_ref, ckvn_ref, q_ref, k_ref, v_ref):
        h = _dot(x_ref[...], win_ref[0])
        u_ref[...] = h[:, :256]
        cq = h[:, 256:512]
        ckv = h[:, 512:640]
        cq_ref[...] = cq
        ckv_ref[...] = ckv
        cqn = (cq * lax.rsqrt(jnp.mean(cq * cq, axis=-1, keepdims=True) + RMS_EPS) * gq_ref[...]).astype(bf16)
        ckvn = (ckv * lax.rsqrt(jnp.mean(ckv * ckv, axis=-1, keepdims=True) + RMS_EPS) * gkv_ref[...]).astype(bf16)
        cqn_ref[...] = cqn
        ckvn_ref[...] = ckvn
        csv = cs_ref[...]
        lane = lax.broadcasted_iota(jnp.int32, (tm, 128), 1)
        kr = jnp.where(lane < 64, _half_sum(h[:, 640:768] * csv), 0.0).astype(bf16)
        kv = _dot(ckvn, wkv_ref[0])
        for hd in range(H):
            qe = _dot(cqn, wq_ref[0, hd])
            q_ref[hd, :, :128] = qe[:, :128].astype(bf16)
            q_ref[hd, :, 128:] = _half_sum(qe[:, 128:] * csv).astype(bf16)
            k_ref[hd, :, :128] = kv[:, 256 * hd:256 * hd + 128].astype(bf16)
            k_ref[hd, :, 128:] = kr
            v_ref[hd] = kv[:, 256 * hd + 128:256 * hd + 256].astype(bf16)

    row = lambda i: (i, 0)
    hrow = lambda i: (0, i, 0)
    return pl.pallas_call(
        body, name=name, grid=(S // tm,),
        in_specs=[pl.BlockSpec((tm, D_MODEL), row),
                  pl.BlockSpec((1, D_MODEL, W_EXT), lambda i: (l, 0, 0)),
                  pl.BlockSpec((1, H, Q_LORA, 256), lambda i: (l, 0, 0, 0)),
                  pl.BlockSpec((1, KV_LORA, H * 256), lambda i: (l, 0, 0)),
                  pl.BlockSpec((1, Q_LORA), lambda i: (0, 0)), pl.BlockSpec((1, KV_LORA), lambda i: (0, 0)),
                  pl.BlockSpec((tm, 128), row)],
        out_specs=[pl.BlockSpec((tm, 256), row), pl.BlockSpec((tm, Q_LORA), row), pl.BlockSpec((tm, KV_LORA), row),
                   pl.BlockSpec((tm, Q_LORA), row), pl.BlockSpec((tm, KV_LORA), row),
                   pl.BlockSpec((H, tm, 256), hrow), pl.BlockSpec((H, tm, 256), hrow), pl.BlockSpec((H, tm, 128), hrow)],
        out_shape=[SDS((S, 256), f32), SDS((S, Q_LORA), f32), SDS((S, KV_LORA), f32),
                   SDS((S, Q_LORA), bf16), SDS((S, KV_LORA), bf16),
                   SDS((H, S, 256), bf16), SDS((H, S, 256), bf16), SDS((H, S, 128), bf16)],
        compiler_params=_cp("parallel"),
    )(xb, w_in, wq, wkv, gq, gkv, cs)


def _group_select(col, a2, a4, a8, a16):
    return jnp.where(col < 64, a2, jnp.where(col < 128, a4, jnp.where(col < 192, a8, a16)))


def pool_fwd(name, u, wbd, scale):
    S = u.shape[0]
    tm = _tile(S, 512)
    hb = tm // HALO

    def body(u_ref, halo_ref, w_ref, s_ref, d_ref, y_ref):
        i = pl.program_id(0)
        cur = u_ref[...]
        halo = jnp.where(i > 0, halo_ref[...], 0.0)
        ext = jnp.concatenate([halo, cur], axis=0)
        s2 = ext + pltpu.roll(ext, 1, axis=0)
        s4 = s2 + pltpu.roll(s2, 2, axis=0)
        s8 = s4 + pltpu.roll(s4, 4, axis=0)
        s16 = s8 + pltpu.roll(s8, 8, axis=0)
        t1 = (i * tm + 1 + lax.broadcasted_iota(jnp.int32, (tm, 1), 0)).astype(f32)
        col = lax.broadcasted_iota(jnp.int32, (tm, 256), 1)
        m = _group_select(col, s2[HALO:] / jnp.minimum(t1, 2.0), s4[HALO:] / jnp.minimum(t1, 4.0),
                          s8[HALO:] / jnp.minimum(t1, 8.0), s16[HALO:] / jnp.minimum(t1, 16.0))
        d = (m - cur).astype(bf16)
        d_ref[...] = d
        y_ref[...] = (_dot(d, w_ref[...]) * s_ref[...]).astype(bf16)

    row = lambda i: (i, 0)
    return pl.pallas_call(
        body, name=name, grid=(S // tm,),
        in_specs=[pl.BlockSpec((tm, 256), row), pl.BlockSpec((HALO, 256), lambda i: (jnp.maximum(i * hb - 1, 0), 0)),
                  pl.BlockSpec((256, 256), lambda i: (0, 0)), pl.BlockSpec((1, 256), lambda i: (0, 0))],
        out_specs=[pl.BlockSpec((tm, 256), row)] * 2,
        out_shape=[SDS((S, 256), bf16), SDS((S, 256), bf16)],
        compiler_params=_cp("parallel"),
    )(u, u, wbd, scale)


def pool_bwd(name, dyp, d, wbd, scale):
    S = dyp.shape[0]
    tm = _tile(S, 512)
    hb = tm // HALO
    n_ext = tm + HALO

    def fwd_sum(e, steps):
        k = 1
        for _ in range(steps):
            e = e + pltpu.roll(e, n_ext - k, axis=0)
            k *= 2
        return e

    def body(dy_ref, halo_ref, d_ref, w_ref, s_ref, du_ref, dyw_ref, ds_ref):
        i = pl.program_id(0)
        sc = s_ref[...]
        w = w_ref[...]
        cur = dy_ref[...].astype(f32)
        halo = jnp.where(i < pl.num_programs(0) - 1, halo_ref[...].astype(f32), 0.0)
        dyw = jnp.concatenate([cur, halo], axis=0) * sc
        dyw_ref[...] = dyw[:tm].astype(bf16)
        dd = _dot_nt(dyw.astype(bf16), w)
        t1 = (i * tm + 1 + lax.broadcasted_iota(jnp.int32, (n_ext, 1), 0)).astype(f32)
        f2 = fwd_sum(dd / jnp.minimum(t1, 2.0), 1)
        f4 = fwd_sum(dd / jnp.minimum(t1, 4.0), 2)
        f8 = fwd_sum(dd / jnp.minimum(t1, 8.0), 3)
        f16 = fwd_sum(dd / jnp.minimum(t1, 16.0), 4)
        col = lax.broadcasted_iota(jnp.int32, (tm, 256), 1)
        du_ref[...] = (_group_select(col, f2[:tm], f4[:tm], f8[:tm], f16[:tm]) - dd[:tm]).astype(bf16)

        @pl.when(i == 0)
        def _():
            ds_ref[...] = jnp.zeros_like(ds_ref)

        ds_ref[...] += jnp.sum(cur * _dot(d_ref[...], w), axis=0, keepdims=True)

    row = lambda i: (i, 0)
    nhb = S // HALO
    return pl.pallas_call(
        body, name=name, grid=(S // tm,),
        in_specs=[pl.BlockSpec((tm, 256), row), pl.BlockSpec((HALO, 256), lambda i: (jnp.minimum((i + 1) * hb, nhb - 1), 0)),
                  pl.BlockSpec((tm, 256), row), pl.BlockSpec((256, 256), lambda i: (0, 0)),
                  pl.BlockSpec((1, 256), lambda i: (0, 0))],
        out_specs=[pl.BlockSpec((tm, 256), row), pl.BlockSpec((tm, 256), row), pl.BlockSpec((1, 256), lambda i: (0, 0))],
        out_shape=[SDS((S, 256), bf16), SDS((S, 256), bf16), SDS((1, 256), f32)],
        compiler_params=_cp("arbitrary"),
    )(dyp, dyp, d, wbd, scale)


def _diag_mask(r0, rn, kn):
    rc = (r0 + lax.broadcasted_iota(jnp.int32, (rn, 1), 0)) // 64
    cc = lax.broadcasted_iota(jnp.int32, (1, kn), 1) // 64
    return rc >= cc


def _diag_parts(tq):
    h = tq // 2
    return [(0, h, h), (h, h, tq)] if h % 128 == 0 else [(0, tq, tq)]


MLA_SCALE_LOG2 = MLA_SCALE * math.log2(math.e)


def mla_attn_fwd(name, q, k, v):
    H, S, _ = q.shape
    tq = _tile(S, 1024)
    nq = S // tq
    pairs = [(i, j) for i in range(nq) for j in range(i + 1)]
    it = jnp.asarray([p_[0] for p_ in pairs], jnp.int32)
    jt = jnp.asarray([p_[1] for p_ in pairs], jnp.int32)

    def body(it_ref, jt_ref, q_ref, k_ref, v_ref, o_ref, lse_ref, m_sc, l_sc, acc_sc):
        t = pl.program_id(1)
        i, j = it_ref[t], jt_ref[t]

        @pl.when(j == 0)
        def _():
            m_sc[...] = jnp.full_like(m_sc, NEG_INF)
            l_sc[...] = jnp.zeros_like(l_sc)
            acc_sc[...] = jnp.zeros_like(acc_sc)

        def part(r0, rn, kn, masked):
            rows, keys = slice(r0, r0 + rn), slice(0, kn)
            s = _dot_nt(q_ref[0, rows, :], k_ref[0, keys, :])
            if masked:
                s = jnp.where(_diag_mask(r0, rn, kn), s, NEG_INF)
            m_prev = m_sc[rows, :]
            m_new = jnp.maximum(m_prev, jnp.max(s, axis=-1, keepdims=True))
            p = jnp.exp2((s - jnp.tile(m_new, (1, kn // 128))) * MLA_SCALE_LOG2)
            a = jnp.exp2((m_prev - m_new) * MLA_SCALE_LOG2)
            l_sc[rows, :] = a * l_sc[rows, :] + jnp.sum(p, axis=-1, keepdims=True)
            acc_sc[rows, :] = a * acc_sc[rows, :] + _dot(p.astype(bf16), v_ref[0, keys, :])
            m_sc[rows, :] = m_new

        @pl.when(j < i)
        def _():
            part(0, tq, tq, False)

        @pl.when(j == i)
        def _():
            for r0, rn, kn in _diag_parts(tq):
                part(r0, rn, kn, True)
            o_ref[...] = (acc_sc[...] / l_sc[...]).astype(bf16)
            lse_ref[0] = m_sc[...] * MLA_SCALE_LOG2 + jnp.log2(l_sc[...])

    return pl.pallas_call(
        body, name=name,
        grid_spec=pltpu.PrefetchScalarGridSpec(
            num_scalar_prefetch=2, grid=(H, len(pairs)),
            in_specs=[pl.BlockSpec((1, tq, 256), lambda h, t, it_, jt_: (h, it_[t], 0)),
                      pl.BlockSpec((1, tq, 256), lambda h, t, it_, jt_: (h, jt_[t], 0)),
                      pl.BlockSpec((1, tq, 128), lambda h, t, it_, jt_: (h, jt_[t], 0))],
            out_specs=[pl.BlockSpec((tq, 128), lambda h, t, it_, jt_: (it_[t], h)),
                       pl.BlockSpec((1, tq, 128), lambda h, t, it_, jt_: (h, it_[t], 0))],
            scratch_shapes=[pltpu.VMEM((tq, 128), f32), pltpu.VMEM((tq, 128), f32), pltpu.VMEM((tq, 128), f32)]),
        out_shape=[SDS((S, H * 128), bf16), SDS((H, S, 128), f32)],
        compiler_params=_cp("parallel", "arbitrary"),
    )(it, jt, q, k, v)


def mla_attn_bwd(name, q, k, v, o, do, lse):
    H, S, _ = q.shape
    tq = _tile(S, 1024)
    nq = S // tq
    pairs = [(i, j) for j in range(nq) for i in range(j, nq)]
    it = jnp.asarray([p_[0] for p_ in pairs], jnp.int32)
    jt = jnp.asarray([p_[1] for p_ in pairs], jnp.int32)
    n_pairs = len(pairs)

    def body(it_ref, jt_ref, q_ref, k_ref, v_ref, o_ref, do_ref, lse_ref, dq_ref, dk_ref, dv_ref, dq_sc, dk_sc, dv_sc):
        t = pl.program_id(1)
        i, j = it_ref[t], jt_ref[t]

        @pl.when(t == 0)
        def _():
            dq_sc[...] = jnp.zeros_like(dq_sc)

        @pl.when(i == j)
        def _():
            dk_sc[...] = jnp.zeros_like(dk_sc)
            dv_sc[...] = jnp.zeros_like(dv_sc)

        def part(r0, rn, kn, masked):
            rows, keys = slice(r0, r0 + rn), slice(0, kn)
            qv, kv_, dov = q_ref[0, rows, :], k_ref[0, keys, :], do_ref[rows, :]
            s = _dot_nt(qv, kv_)
            if masked:
                s = jnp.where(_diag_mask(r0, rn, kn), s, NEG_INF)
            p = jnp.exp2(s * MLA_SCALE_LOG2 - jnp.tile(lse_ref[0, rows, :], (1, kn // 128)))
            dv_sc[keys, :] += _dot_tn(p.astype(bf16), dov)
            dp = _dot_nt(dov, v_ref[0, keys, :])
            delta = jnp.sum(dov.astype(f32) * o_ref[rows, :].astype(f32), axis=-1, keepdims=True)
            ds = (p * (dp - delta)).astype(bf16)
            dk_sc[keys, :] += _dot_tn(ds, qv)
            dq_rows = pl.ds(pl.multiple_of(i * tq + r0, 128), rn)
            dq_sc[dq_rows, :] += _dot(ds, kv_)

        @pl.when(i > j)
        def _():
            part(0, tq, tq, False)

        @pl.when(i == j)
        def _():
            for r0, rn, kn in _diag_parts(tq):
                part(r0, rn, kn, True)

        @pl.when(i == nq - 1)
        def _():
            dk_ref[0] = (dk_sc[...] * MLA_SCALE).astype(bf16)
            dv_ref[0] = dv_sc[...].astype(bf16)

        @pl.when(t == n_pairs - 1)
        def _():
            dq_ref[0] = (dq_sc[...] * MLA_SCALE).astype(bf16)

    qi = lambda h, t, it_, jt_: (h, it_[t], 0)
    kj = lambda h, t, it_, jt_: (h, jt_[t], 0)
    oi = lambda h, t, it_, jt_: (it_[t], h)
    return pl.pallas_call(
        body, name=name,
        grid_spec=pltpu.PrefetchScalarGridSpec(
            num_scalar_prefetch=2, grid=(H, n_pairs),
            in_specs=[pl.BlockSpec((1, tq, 256), qi), pl.BlockSpec((1, tq, 256), kj), pl.BlockSpec((1, tq, 128), kj),
                      pl.BlockSpec((tq, 128), oi), pl.BlockSpec((tq, 128), oi), pl.BlockSpec((1, tq, 128), qi)],
            out_specs=[pl.BlockSpec((1, S, 256), lambda h, t, it_, jt_: (h, 0, 0)), pl.BlockSpec((1, tq, 256), kj),
                       pl.BlockSpec((1, tq, 128), kj)],
            scratch_shapes=[pltpu.VMEM((S, 256), f32), pltpu.VMEM((tq, 256), f32), pltpu.VMEM((tq, 128), f32)]),
        out_shape=[SDS((H, S, 256), bf16), SDS((H, S, 256), bf16), SDS((H, S, 128), bf16)],
        compiler_params=_cp("parallel", "arbitrary"),
    )(it, jt, q, k, v, o, do, lse)


def mix_post_bwd(name, dq, dk, dv, wq, wkv, l, cq, ckv, gq, gkv, cs):
    H, S, _ = dq.shape
    tm = _tile(S, 512)

    def rms_bwd(dyn, c, g):
        r = lax.rsqrt(jnp.mean(c * c, axis=-1, keepdims=True) + RMS_EPS)
        ch = c * r
        dyg = dyn * g
        dc = r * (dyg - ch * jnp.mean(dyg * ch, axis=-1, keepdims=True))
        return dc, jnp.sum(dyn * ch, axis=0, keepdims=True)

    def body(dq_ref, dk_ref, dv_ref, wq_ref, wkv_ref, cq_ref, ckv_ref, gq_ref, gkv_ref, cs_ref,
             dqe_ref, dkv_ref, dh_ref, dgq_ref, dgkv_ref):
        csv = cs_ref[...]
        lane = lax.broadcasted_iota(jnp.int32, (tm, 128), 1)
        dcqn = jnp.zeros((tm, Q_LORA), f32)
        dkr = jnp.zeros((tm, 128), f32)
        for hd in range(H):
            dqh = dq_ref[hd].astype(f32)
            dqe = jnp.concatenate([dqh[:, :128], _half_sum(dqh[:, 128:]) * csv], axis=1).astype(bf16)
            dqe_ref[:, 256 * hd:256 * hd + 256] = dqe
            dcqn = dcqn + _dot_nt(dqe, wq_ref[0, hd])
            dkh = dk_ref[hd].astype(f32)
            dkv_ref[:, 256 * hd:256 * hd + 128] = dkh[:, :128].astype(bf16)
            dkv_ref[:, 256 * hd + 128:256 * hd + 256] = dv_ref[hd].astype(bf16)
            dkr = dkr + dkh[:, 128:]
        dckvn = _dot_nt(dkv_ref[...], wkv_ref[0])
        dblk = _half_sum(jnp.where(lane < 64, dkr, 0.0)) * csv
        dcq, dgq = rms_bwd(dcqn, cq_ref[...], gq_ref[...])
        dckv, dgkv = rms_bwd(dckvn, ckv_ref[...], gkv_ref[...])
        dh_ref[:, :256] = dcq.astype(bf16)
        dh_ref[:, 256:384] = dckv.astype(bf16)
        dh_ref[:, 384:] = dblk.astype(bf16)

        @pl.when(pl.program_id(0) == 0)
        def _():
            dgq_ref[...] = jnp.zeros_like(dgq_ref)
            dgkv_ref[...] = jnp.zeros_like(dgkv_ref)

        dgq_ref[...] += dgq
        dgkv_ref[...] += dgkv

    row = lambda i: (i, 0)
    hrow = lambda i: (0, i, 0)
    return pl.pallas_call(
        body, name=name, grid=(S // tm,),
        in_specs=[pl.BlockSpec((H, tm, 256), hrow), pl.BlockSpec((H, tm, 256), hrow), pl.BlockSpec((H, tm, 128), hrow),
                  pl.BlockSpec((1, H, Q_LORA, 256), lambda i: (l, 0, 0, 0)),
                  pl.BlockSpec((1, KV_LORA, H * 256), lambda i: (l, 0, 0)),
                  pl.BlockSpec((tm, Q_LORA), row), pl.BlockSpec((tm, KV_LORA), row),
                  pl.BlockSpec((1, Q_LORA), lambda i: (0, 0)), pl.BlockSpec((1, KV_LORA), lambda i: (0, 0)),
                  pl.BlockSpec((tm, 128), row)],
        out_specs=[pl.BlockSpec((tm, H * 256), row), pl.BlockSpec((tm, H * 256), row), pl.BlockSpec((tm, 512), row),
                   pl.BlockSpec((1, Q_LORA), lambda i: (0, 0)), pl.BlockSpec((1, KV_LORA), lambda i: (0, 0))],
        out_shape=[SDS((S, H * 256), bf16), SDS((S, H * 256), bf16), SDS((S, 512), bf16),
                   SDS((1, Q_LORA), f32), SDS((1, KV_LORA), f32)],
        compiler_params=_cp("arbitrary"),
    )(dq, dk, dv, wq, wkv, cq, ckv, gq, gkv, cs)


def _cross_probs(qb, kv_ref, hd):
    cols = slice(hd * MEM_HEAD_DIM, (hd + 1) * MEM_HEAD_DIM)
    s = _dot_nt(qb[:, cols], kv_ref[:, cols]) * MEM_SCALE
    e = jnp.exp(s - jnp.max(s, axis=-1, keepdims=True))
    return e / jnp.sum(e, axis=-1, keepdims=True)


def cross_fwd(name, xb, xf, wq, wo, l, kv, g, b):
    S = xb.shape[0]
    tm = _tile(S, 512)
    M = kv.shape[0]

    def body(x_ref, xf_ref, wq_ref, wo_ref, k_ref, v_ref, g_ref, b_ref, q_ref, o_ref, z_ref, y_ref, yb_ref):
        qb = _dot(x_ref[...], wq_ref[0]).astype(bf16)
        q_ref[...] = qb
        for hd in range(MEM_HEADS):
            cols = slice(hd * MEM_HEAD_DIM, (hd + 1) * MEM_HEAD_DIM)
            p = _cross_probs(qb, k_ref, hd)
            o_ref[:, cols] = _dot(p.astype(bf16), v_ref[:, cols]).astype(bf16)
        z = ALPHA * xf_ref[...] + _dot(o_ref[...], wo_ref[0])
        mu = jnp.mean(z, axis=-1, keepdims=True)
        zc = z - mu
        var = jnp.mean(zc * zc, axis=-1, keepdims=True)
        y = zc * lax.rsqrt(var + LN_EPS) * g_ref[...] + b_ref[...]
        z_ref[...] = z
        y_ref[...] = y
        yb_ref[...] = y.astype(bf16)

    row = lambda i: (i, 0)
    wspec = pl.BlockSpec((1, D_MODEL, D_MODEL), lambda i: (l, 0, 0))
    vec = pl.BlockSpec((1, D_MODEL), lambda i: (0, 0))
    blk = pl.BlockSpec((tm, D_MODEL), row)
    return pl.pallas_call(
        body, name=name, grid=(S // tm,),
        in_specs=[blk, blk, wspec, wspec, pl.BlockSpec((M, D_MODEL), lambda i: (0, 0)),
                  pl.BlockSpec((M, D_MODEL), lambda i: (0, 1)), vec, vec],
        out_specs=[blk] * 5,
        out_shape=[SDS((S, D_MODEL), bf16), SDS((S, D_MODEL), bf16), SDS((S, D_MODEL), f32), SDS((S, D_MODEL), f32),
                   SDS((S, D_MODEL), bf16)],
        compiler_params=_cp("parallel"),
    )(xb, xf, wq, wo, kv, kv, g, b)


def cross_bwd(name, dzb, wo, l, qb, kv, deps=()):
    S = dzb.shape[0]
    tm = _tile(S, 512)
    M = kv.shape[0]

    def body(dz_ref, wo_ref, q_ref, k_ref, v_ref, dq_ref, dkv_ref):
        @pl.when(pl.program_id(0) == 0)
        def _():
            dkv_ref[...] = jnp.zeros_like(dkv_ref)

        do = _dot_nt(dz_ref[...], wo_ref[0]).astype(bf16)
        qv = q_ref[...]
        for hd in range(MEM_HEADS):
            cols = slice(hd * MEM_HEAD_DIM, (hd + 1) * MEM_HEAD_DIM)
            vcols = slice(D_MODEL + hd * MEM_HEAD_DIM, D_MODEL + (hd + 1) * MEM_HEAD_DIM)
            p = _cross_probs(qv, k_ref, hd)
            doh = do[:, cols]
            dkv_ref[:, vcols] += _dot_tn(p.astype(bf16), doh)
            dp = _dot_nt(doh, v_ref[:, cols])
            ds = (p * (dp - jnp.sum(dp * p, axis=-1, keepdims=True)) * MEM_SCALE).astype(bf16)
            dq_ref[:, cols] = _dot(ds, k_ref[:, cols]).astype(bf16)
            dkv_ref[:, cols] += _dot_tn(ds, qv[:, cols])

    row = lambda i: (i, 0)
    blk = pl.BlockSpec((tm, D_MODEL), row)
    return pl.pallas_call(
        _with_deps(body, 5, deps), name=name, grid=(S // tm,),
        in_specs=[blk, pl.BlockSpec((1, D_MODEL, D_MODEL), lambda i: (l, 0, 0)), blk,
                  pl.BlockSpec((M, D_MODEL), lambda i: (0, 0)), pl.BlockSpec((M, D_MODEL), lambda i: (0, 1))]
        + [_DEP_SPEC] * len(deps),
        out_specs=[blk, pl.BlockSpec((M, 2 * D_MODEL), lambda i: (0, 0))],
        out_shape=[SDS((S, D_MODEL), bf16), SDS((M, 2 * D_MODEL), f32)],
        compiler_params=_cp("arbitrary"),
    )(dzb, wo, qb, kv, kv, *deps)


def adamw(name, w, g, m, v, deps=()):
    shape = w.shape
    cols = shape[-1]
    rows = math.prod(shape[:-1])
    tr = _row_tile(rows, cols, target=2 * 2**20)
    c1 = 1.0 - ADAM_B1 ** ADAM_STEP
    c2 = 1.0 - ADAM_B2 ** ADAM_STEP

    def body(w_ref, g_ref, m_ref, v_ref, d_ref, nm_ref, nv_ref):
        gv = g_ref[...]
        nm = ADAM_B1 * m_ref[...] + (1.0 - ADAM_B1) * gv
        nv = ADAM_B2 * v_ref[...] + (1.0 - ADAM_B2) * (gv * gv)
        d_ref[...] = -ADAM_LR * ((nm / c1) / (jnp.sqrt(nv / c2) + ADAM_EPS) + ADAM_WD * w_ref[...])
        nm_ref[...] = nm
        nv_ref[...] = nv

    blk = pl.BlockSpec((tr, cols), lambda i: (i, 0))
    flat = SDS((rows, cols), f32)
    outs = pl.pallas_call(
        _with_deps(body, 4, deps), name=name, grid=(rows // tr,), in_specs=[blk] * 4 + [_DEP_SPEC] * len(deps),
        out_specs=[blk] * 3, out_shape=[flat] * 3, compiler_params=_cp("parallel"),
    )(*[a.reshape(rows, cols) for a in (w, g, m, v)], *deps)
    return [o.reshape(shape) for o in outs]


def _me():
    return lax.axis_index("x"), lax.axis_index("y"), lax.axis_index("c")


def _other_chips(x, y):
    return [(1 - x, y), (x, 1 - y), (1 - x, 1 - y)]


def _pair_share_each(owners, bufs, sems, mine, act):
    x, y, c = _me()
    for o in range(2):
        slots = [(a, lyr) for a in range(len(bufs)) for lyr in range(DEPTH) if owners[a][lyr] == o]

        @pl.when((c == o) if mine else (c != o))
        def _(slots=slots):
            for a, lyr in slots:
                slot = bufs[a].at[lyr]
                act(_rcopy(slot, slot, sems[0].at[2 * a + lyr], sems[1].at[2 * a + lyr], (x, y, 1 - c)))


def pair_share_start(name, sums, owners, after):
    def body_fn(b_in, s_in, s_out):
        _pair_share_each(owners, b_in, s_out, True, lambda cp: cp.start())

    outs, sems, token = _split_call(name, body_fn, list(sums), [], [2 * len(sums)] * 2, after)
    return (outs, sems[0], sems[1], owners), token


def pair_share_wait(name, st, after):
    bufs, send, recv, owners = st

    def body_fn(b_in, s_in, s_out):
        _pair_share_each(owners, b_in, s_in, True, lambda cp: cp.wait_send())
        _pair_share_each(owners, b_in, s_in, False, lambda cp: cp.wait_recv())

    outs, _, _ = _split_call(name, body_fn, list(bufs), [send, recv], [], after)
    return outs


def allsum_small(name, v, deps=()):
    R = v.shape[0]

    def body(v_ref, o_ref, all_ref, send_sems, recv_sems, local_sem):
        x, y, c = _me()
        me, sibling = (x, y, c), (x, y, 1 - c)
        chips = _other_chips(x, y)

        def rows(px, py, pc):
            return all_ref.at[4 * px + 2 * py + pc]

        def copy(k, block, to, src=None):
            return pltpu.make_async_remote_copy(
                src_ref=rows(*block) if src is None else src, dst_ref=rows(*block),
                send_sem=send_sems.at[k], recv_sem=recv_sems.at[k], device_id=to, device_id_type=MESH)

        mine = pltpu.make_async_copy(v_ref, rows(*me), local_sem)
        mine.start()
        first = [copy(0, me, sibling, src=v_ref)]
        first += [copy(1 + j, me, (*chip, c), src=v_ref) for j, chip in enumerate(chips)]
        for cp in first:
            cp.start()
        passed = [copy(4 + j, (*chip, c), sibling) for j, chip in enumerate(chips)]
        for j, chip in enumerate(chips):
            copy(1 + j, (*chip, c), me).wait_recv()
            passed[j].start()
        copy(0, sibling, me).wait_recv()
        for j, chip in enumerate(chips):
            copy(4 + j, (*chip, 1 - c), me).wait_recv()
        for cp in first + passed:
            cp.wait_send()
        mine.wait()
        acc = all_ref[0]
        for d in range(1, 8):
            acc = acc + all_ref[d]
        o_ref[...] = acc

    return pl.pallas_call(
        _with_deps(body, 1, deps), name=name,
        in_specs=[pl.BlockSpec(memory_space=pltpu.VMEM)] + [_DEP_SPEC] * len(deps),
        out_specs=pl.BlockSpec(memory_space=pltpu.VMEM),
        out_shape=SDS((R, 128), f32),
        scratch_shapes=[pltpu.VMEM((8, R, 128), f32), pltpu.SemaphoreType.DMA((7,)), pltpu.SemaphoreType.DMA((7,)),
                        pltpu.SemaphoreType.DMA],
        compiler_params=pltpu.CompilerParams(vmem_limit_bytes=V7X_VMEM_LIMIT),
    )(v, *deps)


def _swap_half(r):
    return jnp.concatenate([-r[..., 32:], r[..., :32]], axis=-1)


def _unswap_add(p, qg):
    return p + jnp.concatenate([qg[..., 32:], -qg[..., :32]], axis=-1)


def _block_diag(pw):
    L = pw.shape[0]
    out = jnp.zeros((L, 256, 256), pw.dtype)
    for gi in range(4):
        out = out.at[:, 64 * gi:64 * gi + 64, 64 * gi:64 * gi + 64].set(pw[:, gi])
    return out


def _to_col_shards(w):
    *lead, K, N = w.shape
    nl = len(lead)
    return w.reshape(*lead, K, N_CHIPS, N // N_CHIPS).transpose(*range(nl), nl + 1, nl, nl + 2)


def _from_col_shards(w):
    *lead, C, K, n = w.shape
    nl = len(lead)
    return w.transpose(*range(nl), nl + 1, nl, nl + 2).reshape(*lead, K, C * n)


def _step_serial_comm(x, mem, positions, ln_g, ln_b, ffn1_w13, ffn1_w2, w_in, pool_w, pool_scale, q_norm_g, w_uq, kv_norm_g, w_ukv, w_out, mem_wq, mem_wkv, mem_wo, ffn2_w13, ffn2_w2, loss_target, m_ln_g, m_ln_b, m_ffn1_w13, m_ffn1_w2, m_w_in, m_pool_w, m_pool_scale, m_q_norm_g, m_w_uq, m_kv_norm_g, m_w_ukv, m_w_out, m_mem_wq, m_mem_wkv, m_mem_wo, m_ffn2_w13, m_ffn2_w2, v_ln_g, v_ln_b, v_ffn1_w13, v_ffn1_w2, v_w_in, v_pool_w, v_pool_scale, v_q_norm_g, v_w_uq, v_kv_norm_g, v_w_ukv, v_w_out, v_mem_wq, v_mem_wkv, v_mem_wo, v_ffn2_w13, v_ffn2_w2):
    L = DEPTH
    S = x.shape[1]
    qx, qy, _ = _me()
    chip = 2 * qx + qy

    big = [ffn1_w13, ffn1_w2, w_in, w_uq, w_ukv, w_out, mem_wq, mem_wkv, mem_wo, ffn2_w13, ffn2_w2]
    (g_f1w13, g_f1w2, g_win, g_wuq, g_wukv, g_wout, g_mwq, g_mwkv, g_mwo, g_f2w13, g_f2w2) = gather_weights(
        [w.astype(bf16) for w in big])
    f1w2 = g_f1w2.reshape(L, D_FF, D_MODEL)
    f2w2 = g_f2w2.reshape(L, D_FF, D_MODEL)
    win = g_win.reshape(L, D_MODEL, D_IN)
    win_ext = jnp.concatenate([win, _swap_half(win[..., D_IN - QK_ROPE:])], axis=-1)
    wuq = _from_col_shards(g_wuq).reshape(L, Q_LORA, MLA_HEADS, QK_NOPE + QK_ROPE)
    wq_ext = jnp.concatenate([wuq, _swap_half(wuq[..., QK_NOPE:])], axis=-1).transpose(0, 2, 1, 3)
    wukv = _from_col_shards(g_wukv)
    wout = g_wout.reshape(L, D_MODEL, D_MODEL)
    wout_pool, wout_mla = wout[:, :POOL_WIDTH], wout[:, POOL_WIDTH:]
    mwq = g_mwq.reshape(L, D_MODEL, D_MODEL)
    mwo = g_mwo.reshape(L, D_MODEL, D_MODEL)
    wbd = _block_diag(pool_w.astype(bf16))

    ln_pad = jnp.zeros((2, L, 4, N_CHIPS, D_MODEL // N_CHIPS), f32)
    ln_pad = lax.dynamic_update_slice(ln_pad, jnp.stack([ln_g, ln_b])[:, :, :, None, :], (0, 0, 0, chip, 0))
    ln_full = allsum_small("allsum_ln", ln_pad.reshape(-1, 128)) * 0.5
    ln_full = ln_full.reshape(2, L, 4, D_MODEL)
    lng, lnb = ln_full[0], ln_full[1]

    half = QK_ROPE // 2
    inv_freq = ROPE_BASE ** (-jnp.arange(half, dtype=f32) / half)
    ang = positions[0].astype(f32)[:, None] * inv_freq
    cos, sin = jnp.cos(ang), jnp.sin(ang)
    cs = jnp.concatenate([cos, cos, sin, sin], axis=-1)

    memb = mem[0].astype(bf16)
    xf = x[0]
    xb = xf.astype(bf16)
    vec = lambda a: a.reshape(1, -1)

    saved = []
    for l in range(L):
        sv = {}
        sv["x0b"] = xb
        gate, up, act = ffn_up(f"ffn1_up_{l}", xb, g_f1w13, l)
        z1, x1f, x1b = proj_res_ln(f"ffn1_down_{l}", [act], [f1w2], [l], xf, vec(lng[l, 0]), vec(lnb[l, 0]), 0.5)
        sv.update(gate1=gate, up1=up, act1=act, z1=z1, x1b=x1b)
        u, cq, ckv, cqn, ckvn, q, k, v = mix_pre(f"mix_pre_{l}", x1b, win_ext, wq_ext, wukv, l,
                                                   vec(q_norm_g[l]), vec(kv_norm_g[l]), cs)
        dpool, ypool = pool_fwd(f"pool_fwd_{l}", u, wbd[l], vec(pool_scale[l]))
        o, lse = mla_attn_fwd(f"mla_fwd_{l}", q, k, v)
        z2, x2f, x2b = proj_res_ln(f"mix_out_{l}", [ypool, o], [wout_pool, wout_mla], [l, l], x1f,
                                   vec(lng[l, 1]), vec(lnb[l, 1]), 1.0)
        sv.update(cq=cq, ckv=ckv, cqn=cqn, ckvn=ckvn, q=q, k=k, v=v, dpool=dpool, ypool=ypool, o=o, lse=lse, z2=z2, x2b=x2b)
        kvm = mm_nn_shard(f"mem_kv_{l}", memb, g_mwkv, l)
        cq_, co_, z3, x3f, x3b = cross_fwd(f"cross_fwd_{l}", x2b, x2f, mwq, mwo, l, kvm, vec(lng[l, 2]), vec(lnb[l, 2]))
        sv.update(kvm=kvm, crq=cq_, cro=co_, z3=z3, x3b=x3b)
        gate, up, act = ffn_up(f"ffn2_up_{l}", x3b, g_f2w13, l)
        z4, xf, xb = proj_res_ln(f"ffn2_down_{l}", [act], [f2w2], [l], x3f, vec(lng[l, 3]), vec(lnb[l, 3]), 0.5)
        sv.update(gate2=gate, up2=up, act2=act, z4=z4)
        saved.append(sv)

    dy, loss_blk = loss_grad("loss_grad", xf, loss_target[0])
    loss = lax.psum(loss_blk[0, 0], ("x", "y", "c"))

    G = dict(f1w13=None, f1w2=None, mwq=None, mwkv=None, mwo=None, f2w13=None, f2w2=None)
    small = {k_: [None] * L for k_ in ("win", "wuq", "wukv", "wout", "pool_w", "pool_scale", "gq", "gkv", "lng", "lnb")}
    for l in reversed(range(L)):
        sv = saved[l]
        dlg, dlb = [None] * 4, [None] * 4
        dzb, dres, dlg[3], dlb[3] = ln_bwd(f"ln4_bwd_{l}", dy, sv["z4"], vec(lng[l, 3]), 0.5)
        dh = ffn_bwd_da(f"ffn2_bwd_da_{l}", dzb, f2w2, l, sv["gate2"], sv["up2"])
        G["f2w2"] = mm_tn(f"ffn2_dw2_{l}", sv["act2"], dzb, "nat", l, G["f2w2"])
        G["f2w13"] = mm_tn(f"ffn2_dw13_{l}", sv["x3b"], dh, "shard", l, G["f2w13"])
        dy = ffn_dx(f"ffn2_dx_{l}", dh, g_f2w13, l, dres)
        dzb, dres, dlg[2], dlb[2] = ln_bwd(f"ln3_bwd_{l}", dy, sv["z3"], vec(lng[l, 2]), 1.0)
        dqc, dkvm = cross_bwd(f"cross_bwd_{l}", dzb, mwo, l, sv["crq"], sv["kvm"])
        G["mwo"] = mm_tn(f"cross_dwo_{l}", sv["cro"], dzb, "nat", l, G["mwo"])
        G["mwq"] = mm_tn(f"cross_dwq_{l}", sv["x2b"], dqc, "nat", l, G["mwq"])
        G["mwkv"] = mm_tn(f"cross_dwkv_{l}", memb, dkvm, "shard", l, G["mwkv"])
        dy = mm_nt_res(f"cross_dx_{l}", [dqc], [mwq], [l], dres, f32)
        dzb, dres, dlg[1], dlb[1] = ln_bwd(f"ln2_bwd_{l}", dy, sv["z2"], vec(lng[l, 1]), 1.0)
        dyp = mm_nt_res(f"mix_dpool_{l}", [dzb], [wout_pool], [l], None, bf16)
        do = mm_nt_res(f"mix_do_{l}", [dzb], [wout_mla], [l], None, bf16)
        dwo_p = mm_tn(f"mix_dwout_pool_{l}", sv["ypool"], dzb)
        dwo_m = mm_tn(f"mix_dwout_mla_{l}", sv["o"], dzb)
        small["wout"][l] = jnp.concatenate([dwo_p, dwo_m], axis=0)
        dq, dk, dv = mla_attn_bwd(f"mla_bwd_{l}", sv["q"], sv["k"], sv["v"], sv["o"], do, sv["lse"])
        dqe, dkv, dh_rest, dgq, dgkv = mix_post_bwd(f"mix_post_bwd_{l}", dq, dk, dv, wq_ext, wukv, l, sv["cq"], sv["ckv"],
                                                     vec(q_norm_g[l]), vec(kv_norm_g[l]), cs)
        du, dyw, dscale = pool_bwd(f"pool_bwd_{l}", dyp, sv["dpool"], wbd[l], vec(pool_scale[l]))
        dwq_e = mm_tn(f"mix_dwuq_{l}", sv["cqn"], dqe).reshape(Q_LORA, MLA_HEADS, 256)
        small["wuq"][l] = jnp.concatenate(
            [dwq_e[..., :QK_NOPE], _unswap_add(dwq_e[..., QK_NOPE:QK_NOPE + QK_ROPE], dwq_e[..., QK_NOPE + QK_ROPE:])],
            axis=-1).reshape(Q_LORA, MLA_HEADS * (QK_NOPE + QK_ROPE))
        small["wukv"][l] = mm_tn(f"mix_dwukv_{l}", sv["ckvn"], dkv)
        dwbd = mm_tn(f"pool_dw_{l}", sv["dpool"], dyw)
        small["pool_w"][l] = jnp.stack([dwbd[64 * gi:64 * gi + 64, 64 * gi:64 * gi + 64] for gi in range(4)])
        small["pool_scale"][l], small["gq"][l], small["gkv"][l] = dscale[0], dgq[0], dgkv[0]
        dh_ext = jnp.concatenate([du, dh_rest], axis=1)
        dwin_e = mm_tn(f"mix_dwin_{l}", sv["x1b"], dh_ext)
        small["win"][l] = jnp.concatenate(
            [dwin_e[:, :D_IN - QK_ROPE], _unswap_add(dwin_e[:, D_IN - QK_ROPE:D_IN], dwin_e[:, D_IN:])], axis=-1)
        dy = mm_nt_res(f"mix_dx_{l}", [dh_ext], [win_ext], [l], dres, f32)
        dzb, dres, dlg[0], dlb[0] = ln_bwd(f"ln1_bwd_{l}", dy, sv["z1"], vec(lng[l, 0]), 0.5)
        dh = ffn_bwd_da(f"ffn1_bwd_da_{l}", dzb, f1w2, l, sv["gate1"], sv["up1"])
        G["f1w2"] = mm_tn(f"ffn1_dw2_{l}", sv["act1"], dzb, "nat", l, G["f1w2"])
        G["f1w13"] = mm_tn(f"ffn1_dw13_{l}", sv["x0b"], dh, "shard", l, G["f1w13"])
        dy = ffn_dx(f"ffn1_dx_{l}", dh, g_f1w13, l, dres)
        small["lng"][l] = jnp.concatenate(dlg, axis=0)
        small["lnb"][l] = jnp.concatenate(dlb, axis=0)
    grad_x = dy[None]

    row_shards = lambda a, K: a.reshape(L, N_CHIPS, K // N_CHIPS, a.shape[-1])
    g_list = [G["f1w13"], row_shards(G["f1w2"], D_FF),
              jnp.stack(small["win"]).reshape(L, N_CHIPS, D_MODEL // N_CHIPS, D_IN),
              _to_col_shards(jnp.stack(small["wuq"])), _to_col_shards(jnp.stack(small["wukv"])),
              jnp.stack(small["wout"]).reshape(L, N_CHIPS, D_MODEL // N_CHIPS, D_MODEL),
              row_shards(G["mwq"], D_MODEL), G["mwkv"], row_shards(G["mwo"], D_MODEL),
              G["f2w13"], row_shards(G["f2w2"], D_FF)]
    big_grads = reduce_grads(g_list)

    rep = [jnp.stack(small["pool_w"]).reshape(-1), jnp.stack(small["pool_scale"]).reshape(-1),
           jnp.stack(small["gq"]).reshape(-1), jnp.stack(small["gkv"]).reshape(-1),
           jnp.stack(small["lng"]).reshape(-1), jnp.stack(small["lnb"]).reshape(-1)]
    sizes = [r.shape[0] for r in rep]
    packed = jnp.concatenate(rep)
    pad = (-packed.shape[0]) % 1024
    tot = allsum_small("allsum_small_grads", jnp.pad(packed, (0, pad)).reshape(-1, 128)).reshape(-1)
    offs = [0]
    for s_ in sizes:
        offs.append(offs[-1] + s_)
    parts = [tot[offs[i]:offs[i + 1]] for i in range(len(sizes))]
    g_pool_w = parts[0].reshape(pool_w.shape)
    g_pool_scale = parts[1].reshape(pool_scale.shape)
    g_gq = parts[2].reshape(q_norm_g.shape)
    g_gkv = parts[3].reshape(kv_norm_g.shape)
    shard_cols = lambda a: lax.dynamic_slice_in_dim(a.reshape(L, 4, D_MODEL), chip * (D_MODEL // N_CHIPS),
                                                    D_MODEL // N_CHIPS, axis=2)
    g_lng, g_lnb = shard_cols(parts[4]), shard_cols(parts[5])

    out_names = ("lng", "lnb", "f1w13", "f1w2", "win", "pool_w", "pool_scale", "gq", "wuq", "gkv", "wukv", "wout", "mwq",
                 "mwkv", "mwo", "f2w13", "f2w2")
    big = dict(lng=g_lng, lnb=g_lnb, pool_w=g_pool_w, pool_scale=g_pool_scale, gq=g_gq, gkv=g_gkv)
    late = ("f1w13", "f1w2")
    held = ("f2w13", "f2w2")
    ws = [ln_g, ln_b, ffn1_w13, ffn1_w2, w_in, pool_w, pool_scale, q_norm_g, w_uq, kv_norm_g, w_ukv, w_out, mem_wq,
          mem_wkv, mem_wo, ffn2_w13, ffn2_w2]
    ms = [m_ln_g, m_ln_b, m_ffn1_w13, m_ffn1_w2, m_w_in, m_pool_w, m_pool_scale, m_q_norm_g, m_w_uq, m_kv_norm_g, m_w_ukv,
          m_w_out, m_mem_wq, m_mem_wkv, m_mem_wo, m_ffn2_w13, m_ffn2_w2]
    vs = [v_ln_g, v_ln_b, v_ffn1_w13, v_ffn1_w2, v_w_in, v_pool_w, v_pool_scale, v_q_norm_g, v_w_uq, v_kv_norm_g, v_w_ukv,
          v_w_out, v_mem_wq, v_mem_wkv, v_mem_wo, v_ffn2_w13, v_ffn2_w2]
    res = {}

    def update(n, deps=()):
        a = out_names.index(n)
        res[a] = adamw(f"adamw_{a}", ws[a], big[n].reshape(ws[a].shape), ms[a], vs[a], deps)
        return res[a][0]

    small_done = tuple(update(n) for n in ("lng", "lnb", "pool_w", "pool_scale", "gq", "gkv"))
    big.update(zip(rest_names, pair_share_wait("pair_share_wait_a", share_a, small_done)))
    first_done = tuple(update(n) for n in rest_names if n not in held)
    sums_b = red_end("b0", st_cb, 0, [sums1[n] for n in late], first_done)
    share_b, tok_b = share_start("b", late, sums_b)
    held_done = tuple(update(n, (tok_b,)) for n in held)
    big.update(zip(late, pair_share_wait("pair_share_wait_b", share_b, held_done)))
    for n in late:
        update(n)
    order = range(len(out_names))
    grads = [big[n].reshape(w_.shape) for n, w_ in zip(out_names, ws)]
    return (loss, grad_x, *grads, *[res[a][0] for a in order], *[res[a][1] for a in order], *[res[a][2] for a in order])


_HBM_SPEC = pl.BlockSpec(memory_space=pltpu.HBM)
_SEM_SPEC = pl.BlockSpec(memory_space=pltpu.SEMAPHORE)
_ANY_SPEC = pl.BlockSpec(memory_space=pl.ANY)
_DATAFLOW = pltpu.SideEffectType.DATAFLOW_SIDE_EFFECTING


def _split_call(name, body_fn, bufs, sems_in, sems_out_sizes, after):
    nb, ni, no = len(bufs), len(sems_in), len(sems_out_sizes)
    afters = () if after is None else tuple(after) if isinstance(after, (tuple, list)) else (after,)

    def body(*refs):
        k = nb + ni + len(afters)
        body_fn(refs[:nb], refs[nb:nb + ni], refs[k:k + no])
        refs[-1][...] = jnp.zeros((8, 128), f32)

    outs = pl.pallas_call(
        body, name=name,
        in_specs=[_HBM_SPEC] * nb + [_SEM_SPEC] * ni + [_ANY_SPEC] * len(afters),
        out_specs=[_SEM_SPEC] * no + [_HBM_SPEC] * nb + [pl.BlockSpec(memory_space=pltpu.VMEM)],
        out_shape=[pltpu.SemaphoreType.DMA((s,)) for s in sems_out_sizes]
        + [pltpu.HBM(b.shape, b.dtype) for b in bufs] + [SDS((8, 128), f32)],
        input_output_aliases={i: no + i for i in range(nb)},
        compiler_params=pltpu.CompilerParams(has_side_effects=_DATAFLOW),
    )(*[pltpu.with_memory_space_constraint(b, pltpu.HBM) for b in bufs], *sems_in, *afters)
    return list(outs[no:no + nb]), list(outs[:no]), outs[-1]


def _rcopy(src, dst, ssem, rsem, to):
    return pltpu.make_async_remote_copy(src_ref=src, dst_ref=dst, send_sem=ssem, recv_sem=rsem, device_id=to,
                                        device_id_type=MESH)


def gather_start(name, groups, after):
    flat = [b for bufs, _ in groups for b in bufs]
    sizes = [3 * len(bufs) for bufs, _ in groups for _ in range(2)]

    def body_fn(b_in, s_in, s_out):
        x, y, c = _me()
        q = 2 * x + y
        chips = _other_chips(x, y)
        pos = 0
        for gi, (bufs, owner) in enumerate(groups):
            refs = b_in[pos:pos + len(bufs)]
            pos += len(bufs)

            @pl.when(c == owner)
            def _(refs=refs, send=s_out[2 * gi], recv=s_out[2 * gi + 1]):
                for a, r in enumerate(refs):
                    for k, (cx, cy) in enumerate(chips):
                        _rcopy(r.at[q], r.at[q], send.at[3 * a + k], recv.at[3 * a + k], (cx, cy, c)).start()

    outs, sems, token = _split_call(name, body_fn, flat, [], sizes, after)
    res, pos = [], 0
    for gi, (bufs, owner) in enumerate(groups):
        res.append((outs[pos:pos + len(bufs)], sems[2 * gi], sems[2 * gi + 1], owner))
        pos += len(bufs)
    return res, token


def gather_forward(name, grp, after):
    bufs, send, recv, owner = grp
    n3 = 3 * len(bufs)

    def body_fn(b_in, s_in, s_out):
        x, y, c = _me()
        q = 2 * x + y
        sibling = (x, y, 1 - c)
        chips = _other_chips(x, y)

        @pl.when(c == owner)
        def _():
            for a, r in enumerate(b_in):
                for k, (cx, cy) in enumerate(chips):
                    i = 3 * a + k
                    land = r.at[2 * cx + cy]
                    _rcopy(r.at[q], r.at[q], s_in[0].at[i], s_in[1].at[i], (cx, cy, c)).wait_send()
                    _rcopy(land, land, s_in[0].at[i], s_in[1].at[i], (cx, cy, c)).wait_recv()
                    _rcopy(land, land, s_out[0].at[i], s_out[1].at[i], sibling).start()

    outs, sems, token = _split_call(name, body_fn, bufs, [send, recv], [n3, n3], after)
    return (outs, sems[0], sems[1], owner), token


def gather_finish(name, grp, after):
    bufs, fsend, frecv, owner = grp

    def body_fn(b_in, s_in, s_out):
        x, y, c = _me()
        sibling = (x, y, 1 - c)
        chips = _other_chips(x, y)

        def each(wait):
            for a, r in enumerate(b_in):
                for k, (cx, cy) in enumerate(chips):
                    land = r.at[2 * cx + cy]
                    wait(_rcopy(land, land, s_in[0].at[3 * a + k], s_in[1].at[3 * a + k], sibling))

        @pl.when(c == owner)
        def _():
            each(lambda cp: cp.wait_send())

        @pl.when(c != owner)
        def _():
            each(lambda cp: cp.wait_recv())

    outs, _, _ = _split_call(name, body_fn, bufs, [fsend, frecv], [], after)
    return outs


def _by_owner(owners):
    return [[a for a, o_ in enumerate(owners) if o_ == o] for o in range(2)]


def pair_send_start(name, gs, owners, after):
    n = len(gs)
    lands = [lax.empty(g.shape, g.dtype) for g in gs]

    def body_fn(b_in, s_in, s_out):
        x, y, c = _me()
        for o, idx in enumerate(_by_owner(owners)):
            @pl.when(c == 1 - o)
            def _(o=o, idx=idx):
                for a in idx:
                    _rcopy(b_in[a], b_in[n + a], s_out[0].at[a], s_out[1].at[a], (x, y, o)).start()

    outs, sems, token = _split_call(name, body_fn, list(gs) + lands, [], [n, n], after)
    return (outs[:n], outs[n:], sems[0], sems[1], owners), token


def pair_send_wait(name, st, after):
    gs, lands, send, recv, owners = st
    n = len(gs)

    def body_fn(b_in, s_in, s_out):
        x, y, c = _me()
        for o, idx in enumerate(_by_owner(owners)):
            @pl.when(c == 1 - o)
            def _(o=o, idx=idx):
                for a in idx:
                    _rcopy(b_in[a], b_in[n + a], s_in[0].at[a], s_in[1].at[a], (x, y, o)).wait_send()

            @pl.when(c == o)
            def _(o=o, idx=idx):
                for a in idx:
                    _rcopy(b_in[a], b_in[n + a], s_in[0].at[a], s_in[1].at[a], (x, y, 1 - o)).wait_recv()

    outs, _, _ = _split_call(name, body_fn, list(gs) + list(lands), [send, recv], [], after)
    return outs[:n], outs[n:]


def chip_exchange_start(name, psums, owners, after):
    n = len(psums)
    lands = [lax.empty((3,) + p.shape[1:], p.dtype) for p in psums]

    def body_fn(b_in, s_in, s_out):
        x, y, c = _me()
        chips = _other_chips(x, y)
        for o, idx in enumerate(_by_owner(owners)):
            @pl.when(c == o)
            def _(idx=idx):
                for a in idx:
                    for k, (cx, cy) in enumerate(chips):
                        _rcopy(b_in[a].at[2 * cx + cy], b_in[n + a].at[k], s_out[0].at[3 * a + k],
                               s_out[1].at[3 * a + k], (cx, cy, c)).start()

    outs, sems, token = _split_call(name, body_fn, list(psums) + lands, [], [3 * n, 3 * n], after)
    return (outs[:n], outs[n:], sems[0], sems[1], owners), token


def chip_exchange_wait(name, st, after):
    psums, lands, send, recv, owners = st
    n = len(psums)

    def body_fn(b_in, s_in, s_out):
        x, y, c = _me()
        chips = _other_chips(x, y)
        for o, idx in enumerate(_by_owner(owners)):
            @pl.when(c == o)
            def _(idx=idx):
                for a in idx:
                    for k, (cx, cy) in enumerate(chips):
                        cp = _rcopy(b_in[a].at[2 * cx + cy], b_in[n + a].at[k], s_in[0].at[3 * a + k],
                                    s_in[1].at[3 * a + k], (cx, cy, c))
                        cp.wait_send()
                        cp.wait_recv()

    outs, _, _ = _split_call(name, body_fn, list(psums) + list(lands), [send, recv], [], after)
    return outs[:n], outs[n:]


def pair_sum(name, g, recv, flag):
    shape = g.shape
    cols = shape[-1]
    rows = math.prod(shape[:-1])
    tr = _row_tile(rows, cols, target=4 * 2**20)

    def body(f_ref, g_ref, r_ref, o_ref):
        o_ref[...] = (g_ref[...] + r_ref[...]).astype(bf16)

    blk = pl.BlockSpec((tr, cols), lambda i, f_ref: (i * f_ref[0], 0))
    out = pl.pallas_call(
        body, name=name,
        grid_spec=pltpu.PrefetchScalarGridSpec(num_scalar_prefetch=1, grid=(rows // tr,), in_specs=[blk, blk],
                                               out_specs=blk),
        out_shape=SDS((rows, cols), bf16), compiler_params=_cp("arbitrary"),
    )(flag, g.reshape(rows, cols), recv.reshape(rows, cols))
    return out.reshape(shape)


def chip_sum(name, psum, recv, qf_arr, layer, prev):
    shard = psum.shape[1:]
    cols = shard[-1]
    rows = math.prod(shard[:-1])
    tr = _row_tile(rows, cols, target=4 * 2**20)

    def body(qf_ref, p_ref, r_ref, *rest):
        rest[-1][0] = ((p_ref[0].astype(f32) + r_ref[0].astype(f32)) + r_ref[1].astype(f32)) + r_ref[2].astype(f32)

    in_specs = [pl.BlockSpec((1, tr, cols), lambda i, qf: (qf[0], i * qf[1], 0)),
                pl.BlockSpec((3, tr, cols), lambda i, qf: (0, i * qf[1], 0))]
    args = [qf_arr, psum.reshape(N_CHIPS, rows, cols), recv.reshape(3, rows, cols)]
    aliases = {}
    if prev is not None:
        in_specs.append(pl.BlockSpec(memory_space=pl.ANY))
        args.append(prev.reshape(DEPTH, rows, cols))
        aliases = {3: 0}
    out = pl.pallas_call(
        body, name=name,
        grid_spec=pltpu.PrefetchScalarGridSpec(
            num_scalar_prefetch=1, grid=(rows // tr,), in_specs=in_specs,
            out_specs=pl.BlockSpec((1, tr, cols), lambda i, qf: (layer, i * qf[1], 0))),
        out_shape=SDS((DEPTH, rows, cols), f32), input_output_aliases=aliases, compiler_params=_cp("arbitrary"),
    )(*args)
    return out.reshape((DEPTH,) + shard)


W_NAMES = ("f1w13", "f1w2", "win", "wuq", "wukv", "wout", "mwq", "mwkv", "mwo", "f2w13", "f2w2")
MIX_NAMES = ("win", "wuq", "wukv")
MID_NAMES = ("wout", "mwq", "mwkv", "mwo")
FFN2_NAMES = ("f2w13", "f2w2")
REDUCER = (dict(f1w13=0, f1w2=1, f2w13=0, win=0, wuq=0, wukv=0, f2w2=1, mwkv=1, wout=1, mwq=1, mwo=1),
           dict(f1w13=0, f2w2=0, mwkv=0, wout=0, f2w13=1, f1w2=1, mwq=1, mwo=1, win=1, wuq=1, wukv=1))


def kernel(x, mem, positions, ln_g, ln_b, ffn1_w13, ffn1_w2, w_in, pool_w, pool_scale, q_norm_g, w_uq, kv_norm_g, w_ukv, w_out, mem_wq, mem_wkv, mem_wo, ffn2_w13, ffn2_w2, loss_target, m_ln_g, m_ln_b, m_ffn1_w13, m_ffn1_w2, m_w_in, m_pool_w, m_pool_scale, m_q_norm_g, m_w_uq, m_kv_norm_g, m_w_ukv, m_w_out, m_mem_wq, m_mem_wkv, m_mem_wo, m_ffn2_w13, m_ffn2_w2, v_ln_g, v_ln_b, v_ffn1_w13, v_ffn1_w2, v_w_in, v_pool_w, v_pool_scale, v_q_norm_g, v_w_uq, v_kv_norm_g, v_w_ukv, v_w_out, v_mem_wq, v_mem_wkv, v_mem_wo, v_ffn2_w13, v_ffn2_w2):
    L = DEPTH
    qx, qy, _ = _me()
    chip = 2 * qx + qy
    vec = lambda a: a.reshape(1, -1)

    shards = dict(zip(W_NAMES, (ffn1_w13, ffn1_w2, w_in, w_uq, w_ukv, w_out, mem_wq, mem_wkv, mem_wo, ffn2_w13, ffn2_w2)))

    def place(sh, slot):
        return lax.dynamic_update_slice(lax.empty((N_CHIPS,) + sh.shape, bf16), sh.astype(bf16)[None],
                                        (slot,) + (0,) * sh.ndim)

    first = ("f1w13", "f1w2")
    bufs = [dict(), dict()]
    for n in first:
        bufs[0][n] = place(shards[n][0], chip)
    gw = [dict(), dict()]
    (g0,), tok = gather_start("gather_a_start", [([bufs[0][n] for n in first], 0)], None)
    chip_then = chip + tok[0, 0].astype(jnp.int32)
    for l in range(L):
        for n in W_NAMES:
            if n not in bufs[l]:
                bufs[l][n] = place(shards[n][l], chip_then)
    others = tuple(bufs[l][n] for l in range(L) for n in W_NAMES if (l, n) not in ((0, first[0]), (0, first[1])))
    g0, tok = gather_forward("gather_a_forward", g0, others)

    ln_pad = jnp.zeros((2, L, 4, N_CHIPS, D_MODEL // N_CHIPS), f32)
    ln_pad = lax.dynamic_update_slice(ln_pad, jnp.stack([ln_g, ln_b])[:, :, :, None, :], (0, 0, 0, chip, 0))
    ln_sum = allsum_small("allsum_ln", ln_pad.reshape(-1, 128), (tok,))
    ln_full = (ln_sum * 0.5).reshape(2, L, 4, D_MODEL)
    lng, lnb = ln_full[0], ln_full[1]

    gw[0]["f1w13"], gw[0]["f1w2"] = gather_finish("gather_a_finish", g0, ln_sum)
    (g_mix, g_mid, g_ffn2, g_l1), tok_b = gather_start(
        "gather_b_start",
        [([bufs[0][n] for n in MIX_NAMES], 0), ([bufs[0][n] for n in MID_NAMES], 0), ([bufs[0][n] for n in FFN2_NAMES], 0),
         ([bufs[1][n] for n in W_NAMES], 1)], ln_sum)

    half = QK_ROPE // 2
    inv_freq = ROPE_BASE ** (-jnp.arange(half, dtype=f32) / half)
    ang = positions[0].astype(f32)[:, None] * inv_freq
    cos, sin = jnp.cos(ang), jnp.sin(ang)
    cs = jnp.concatenate([cos, cos, sin, sin], axis=-1)

    memb = mem[0].astype(bf16)
    xf = x[0]
    xb = xf.astype(bf16)
    dep = (tok_b,)

    saved, W = [], [None, None]
    for l in range(L):
        sv = {}
        if l == 1:
            gl1 = gather_finish("gather_l1_finish", g_l1, xb)
            gw[1] = dict(zip(W_NAMES, gl1))
        sv["x0b"] = xb
        f1w13 = gw[l]["f1w13"][None]
        gate, up, act = ffn_up(f"ffn1_up_{l}", xb, f1w13, 0, dep)
        dep = ()
        if l == 0:
            g_mix, _ = gather_forward("gather_mix_forward", g_mix, act)
        z1, x1f, x1b = proj_res_ln(f"ffn1_down_{l}", [act], [gw[l]["f1w2"].reshape(1, D_FF, D_MODEL)], [0], xf,
                                   vec(lng[l, 0]), vec(lnb[l, 0]), 0.5)
        sv.update(gate1=gate, up1=up, act1=act, z1=z1, x1b=x1b)
        if l == 0:
            gw[0].update(zip(MIX_NAMES, gather_finish("gather_mix_finish", g_mix, x1b)))
            g_mid, _ = gather_forward("gather_mid_forward", g_mid, x1b)
        win = gw[l]["win"].reshape(D_MODEL, D_IN)
        win_ext = jnp.concatenate([win, _swap_half(win[:, D_IN - QK_ROPE:])], axis=-1)[None]
        wuq = _from_col_shards(gw[l]["wuq"]).reshape(Q_LORA, MLA_HEADS, QK_NOPE + QK_ROPE)
        wq_ext = jnp.concatenate([wuq, _swap_half(wuq[..., QK_NOPE:])], axis=-1).transpose(1, 0, 2)[None]
        wukv = _from_col_shards(gw[l]["wukv"])[None]
        wbd = _block_diag(pool_w[l][None].astype(bf16))[0]
        u, cq, ckv, cqn, ckvn, q, k, v = mix_pre(f"mix_pre_{l}", x1b, win_ext, wq_ext, wukv, 0,
                                                   vec(q_norm_g[l]), vec(kv_norm_g[l]), cs)
        dpool, ypool = pool_fwd(f"pool_fwd_{l}", u, wbd, vec(pool_scale[l]))
        o, lse = mla_attn_fwd(f"mla_fwd_{l}", q, k, v)
        if l == 0:
            gw[0].update(zip(MID_NAMES, gather_finish("gather_mid_finish", g_mid, o)))
            g_ffn2, tok_f = gather_forward("gather_ffn2_forward", g_ffn2, o)
            g_l1, tok_l = gather_forward("gather_l1_forward", g_l1, o)
            dep = (tok_f, tok_l)
        wout = gw[l]["wout"].reshape(D_MODEL, D_MODEL)
        wout_pool, wout_mla = wout[None, :POOL_WIDTH], wout[None, POOL_WIDTH:]
        mwq = gw[l]["mwq"].reshape(1, D_MODEL, D_MODEL)
        mwo = gw[l]["mwo"].reshape(1, D_MODEL, D_MODEL)
        mwkv = gw[l]["mwkv"][None]
        z2, x2f, x2b = proj_res_ln(f"mix_out_{l}", [ypool, o], [wout_pool, wout_mla], [0, 0], x1f,
                                   vec(lng[l, 1]), vec(lnb[l, 1]), 1.0, dep)
        dep = ()
        sv.update(cq=cq, ckv=ckv, cqn=cqn, ckvn=ckvn, q=q, k=k, v=v, dpool=dpool, ypool=ypool, o=o, lse=lse, z2=z2, x2b=x2b)
        kvm = mm_nn_shard(f"mem_kv_{l}", memb, mwkv, 0)
        cq_, co_, z3, x3f, x3b = cross_fwd(f"cross_fwd_{l}", x2b, x2f, mwq, mwo, 0, kvm, vec(lng[l, 2]), vec(lnb[l, 2]))
        sv.update(kvm=kvm, crq=cq_, cro=co_, z3=z3, x3b=x3b)
        if l == 0:
            gw[0].update(zip(FFN2_NAMES, gather_finish("gather_ffn2_finish", g_ffn2, x3b)))
        f2w13 = gw[l]["f2w13"][None]
        f2w2 = gw[l]["f2w2"].reshape(1, D_FF, D_MODEL)
        gate, up, act = ffn_up(f"ffn2_up_{l}", x3b, f2w13, 0)
        z4, xf, xb = proj_res_ln(f"ffn2_down_{l}", [act], [f2w2], [0], x3f, vec(lng[l, 3]), vec(lnb[l, 3]), 0.5)
        sv.update(gate2=gate, up2=up, act2=act, z4=z4)
        W[l] = dict(f1w13=f1w13, f1w2=gw[l]["f1w2"].reshape(1, D_FF, D_MODEL), win_ext=win_ext, wq_ext=wq_ext, wukv=wukv,
                    wbd=wbd, wout_pool=wout_pool, wout_mla=wout_mla, mwq=mwq, mwo=mwo, f2w13=f2w13, f2w2=f2w2)
        saved.append(sv)

    dln = {}
    dzb, dres, *dln[L - 1, 3], loss_blk = loss_grad("loss_grad", xf, loss_target[0],
                                                   (saved[L - 1]["z4"], vec(lng[L - 1, 3]), 0.5))
    loss = lax.psum(loss_blk[0, 0], ("x", "y", "c"))

    row_shards = lambda a: a.reshape(N_CHIPS, a.shape[0] // N_CHIPS, a.shape[1])
    small = {k_: [None] * L for k_ in ("pool_w", "pool_scale", "gq", "gkv", "lng", "lnb")}
    rest_names = [n for n in W_NAMES if n not in ("f1w13", "f1w2")]
    core = lax.axis_index("c")
    flags = [jnp.reshape(core == o, (1,)).astype(jnp.int32) for o in range(2)]
    qfs = [jnp.stack([chip, (core == o).astype(jnp.int32)]).astype(jnp.int32) for o in range(2)]

    def red_begin(tag, names, gs, layer):
        owners = [REDUCER[layer][n] for n in names]
        st, tok_ = pair_send_start(f"pair_send_start_{tag}", gs, owners, None)
        return (st, owners), tok_

    def red_mid(tag, sto, after):
        st, owners = sto
        gs_, lands_ = pair_send_wait(f"pair_send_wait_{tag}", st, after)
        ps = [pair_sum(f"pair_sum_{tag}_{a}", g_, r_, flags[o]) for a, (g_, r_, o) in enumerate(zip(gs_, lands_, owners))]
        st, tok_ = chip_exchange_start(f"chip_exchange_start_{tag}", ps, owners, None)
        return (st, owners), tok_

    def red_end(tag, sto, layer, prevs, after):
        st, owners = sto
        ps, lands_ = chip_exchange_wait(f"chip_exchange_wait_{tag}", st, after)
        return [chip_sum(f"chip_sum_{tag}_{a}", p_, r_, qfs[o], layer, s_)
                for a, (p_, r_, s_, o) in enumerate(zip(ps, lands_, prevs, owners))]

    def share_start(tag, names, sums_):
        return pair_share_start(f"pair_share_start_{tag}", sums_, [(REDUCER[0][n], REDUCER[1][n]) for n in names], None)

    st_p1 = st_c1 = st_pa = st_ca = None
    for l in reversed(range(L)):
        sv, w = saved[l], W[l]
        g = {}
        dh = ffn_bwd_da(f"ffn2_bwd_da_{l}", dzb, w["f2w2"], 0, sv["gate2"], sv["up2"], dep)
        dep = ()
        g["f2w2"] = row_shards(mm_tn(f"ffn2_dw2_{l}", sv["act2"], dzb))
        g["f2w13"] = mm_tn(f"ffn2_dw13_{l}", sv["x3b"], dh, True)
        dzb, dres, *dln[l, 2] = ffn_dx(f"ffn2_dx_{l}", dh, w["f2w13"], 0, dres, (sv["z3"], vec(lng[l, 2]), 1.0))
        if l == 0:
            st_c1, tok = red_mid("l1", st_p1, dzb)
            dep = (tok, g["f2w2"], g["f2w13"])
        dqc, dkvm = cross_bwd(f"cross_bwd_{l}", dzb, w["mwo"], 0, sv["crq"], sv["kvm"], dep)
        dep = ()
        g["mwo"] = row_shards(mm_tn(f"cross_dwo_{l}", sv["cro"], dzb))
        g["mwq"] = row_shards(mm_tn(f"cross_dwq_{l}", sv["x2b"], dqc))
        g["mwkv"] = mm_tn(f"cross_dwkv_{l}", memb, dkvm, True)
        dzb, dres, *dln[l, 1] = mm_nt_res(f"cross_dx_{l}", [dqc], [w["mwq"]], [0], dres, f32,
                                          (sv["z2"], vec(lng[l, 1]), 1.0))
        dyp = mm_nt_res(f"mix_dpool_{l}", [dzb], [w["wout_pool"]], [0], None, bf16)
        do = mm_nt_res(f"mix_do_{l}", [dzb], [w["wout_mla"]], [0], None, bf16)
        dwo_p = mm_tn(f"mix_dwout_pool_{l}", sv["ypool"], dzb)
        dwo_m = mm_tn(f"mix_dwout_mla_{l}", sv["o"], dzb)
        g["wout"] = row_shards(jnp.concatenate([dwo_p, dwo_m], axis=0))
        dq, dk, dv = mla_attn_bwd(f"mla_bwd_{l}", sv["q"], sv["k"], sv["v"], sv["o"], do, sv["lse"])
        dqe, dkv, dh_rest, dgq, dgkv = mix_post_bwd(f"mix_post_bwd_{l}", dq, dk, dv, w["wq_ext"], w["wukv"], 0, sv["cq"],
                                                     sv["ckv"], vec(q_norm_g[l]), vec(kv_norm_g[l]), cs)
        du, dyw, dscale = pool_bwd(f"pool_bwd_{l}", dyp, sv["dpool"], w["wbd"], vec(pool_scale[l]))
        dwq_e = mm_tn(f"mix_dwuq_{l}", sv["cqn"], dqe).reshape(Q_LORA, MLA_HEADS, 256)
        g["wuq"] = _to_col_shards(jnp.concatenate(
            [dwq_e[..., :QK_NOPE], _unswap_add(dwq_e[..., QK_NOPE:QK_NOPE + QK_ROPE], dwq_e[..., QK_NOPE + QK_ROPE:])],
            axis=-1).reshape(Q_LORA, MLA_HEADS * (QK_NOPE + QK_ROPE)))
        g["wukv"] = _to_col_shards(mm_tn(f"mix_dwukv_{l}", sv["ckvn"], dkv))
        dwbd = mm_tn(f"pool_dw_{l}", sv["dpool"], dyw)
        small["pool_w"][l] = jnp.stack([dwbd[64 * gi:64 * gi + 64, 64 * gi:64 * gi + 64] for gi in range(4)])
        small["pool_scale"][l], small["gq"][l], small["gkv"][l] = dscale[0], dgq[0], dgkv[0]
        dh_ext = jnp.concatenate([du, dh_rest], axis=1)
        dwin_e = mm_tn(f"mix_dwin_{l}", sv["x1b"], dh_ext)
        g["win"] = row_shards(jnp.concatenate(
            [dwin_e[:, :D_IN - QK_ROPE], _unswap_add(dwin_e[:, D_IN - QK_ROPE:D_IN], dwin_e[:, D_IN:])], axis=-1))
        dzb, dres, *dln[l, 0] = mm_nt_res(f"mix_dx_{l}", [dh_ext], [w["win_ext"]], [0], dres, f32,
                                          (sv["z1"], vec(lng[l, 0]), 0.5))
        if l == 0:
            st_pa, tok = red_begin("a0", rest_names, [g[n] for n in rest_names], 0)
            dep = (tok,)
        dh = ffn_bwd_da(f"ffn1_bwd_da_{l}", dzb, w["f1w2"], 0, sv["gate1"], sv["up1"], dep)
        dep = ()
        if l == 0:
            grad_x = ffn_dx(f"ffn1_dx_{l}", dh, w["f1w13"], 0, dres)[None]
            st_ca, tok = red_mid("a0", st_pa, grad_x)
            dep = (tok,)
        else:
            below = ffn_dx(f"ffn1_dx_{l}", dh, w["f1w13"], 0, dres, (saved[l - 1]["z4"], vec(lng[l - 1, 3]), 0.5))
            dln[l - 1, 3] = below[2:]
        g["f1w2"] = row_shards(mm_tn(f"ffn1_dw2_{l}", sv["act1"], dzb, False, dep))
        g["f1w13"] = mm_tn(f"ffn1_dw13_{l}", sv["x0b"], dh, True, dep)
        dep = ()
        if l > 0:
            dzb, dres = below[:2]
        if l == 1:
            st_p1, tok = red_begin("l1", W_NAMES, [g[n] for n in W_NAMES], 1)
            dep = (tok,)
    for l in range(L):
        small["lng"][l] = jnp.concatenate([dln[l, k][0] for k in range(4)], axis=0)
        small["lnb"][l] = jnp.concatenate([dln[l, k][1] for k in range(4)], axis=0)

    st_pb, _ = red_begin("b0", ("f1w13", "f1w2"), [g["f1w13"], g["f1w2"]], 0)
    sums1 = dict(zip(W_NAMES, red_end("l1", st_c1, 1, [None] * len(W_NAMES), g["f1w13"])))
    st_cb, tok = red_mid("b0", st_pb, tuple(sums1.values()))
    sums0 = red_end("a0", st_ca, 0, [sums1[n] for n in rest_names], tok)
    share_a, tok_a = share_start("a", rest_names, sums0)

    rep = [jnp.stack(small["pool_w"]).reshape(-1), jnp.stack(small["pool_scale"]).reshape(-1),
           jnp.stack(small["gq"]).reshape(-1), jnp.stack(small["gkv"]).reshape(-1),
           jnp.stack(small["lng"]).reshape(-1), jnp.stack(small["lnb"]).reshape(-1)]
    sizes = [r.shape[0] for r in rep]
    packed = jnp.concatenate(rep)
    pad = (-packed.shape[0]) % 1024
    tot = allsum_small("allsum_small_grads", jnp.pad(packed, (0, pad)).reshape(-1, 128), (tok_a,)).reshape(-1)
    offs = [0]
    for s_ in sizes:
        offs.append(offs[-1] + s_)
    parts = [tot[offs[i]:offs[i + 1]] for i in range(len(sizes))]
    g_pool_w = parts[0].reshape(pool_w.shape)
    g_pool_scale = parts[1].reshape(pool_scale.shape)
    g_gq = parts[2].reshape(q_norm_g.shape)
    g_gkv = parts[3].reshape(kv_norm_g.shape)
    shard_cols = lambda a: lax.dynamic_slice_in_dim(a.reshape(L, 4, D_MODEL), chip * (D_MODEL // N_CHIPS),
                                                    D_MODEL // N_CHIPS, axis=2)
    g_lng, g_lnb = shard_cols(parts[4]), shard_cols(parts[5])

    out_names = ("lng", "lnb", "f1w13", "f1w2", "win", "pool_w", "pool_scale", "gq", "wuq", "gkv", "wukv", "wout", "mwq",
                 "mwkv", "mwo", "f2w13", "f2w2")
    big = dict(lng=g_lng, lnb=g_lnb, pool_w=g_pool_w, pool_scale=g_pool_scale, gq=g_gq, gkv=g_gkv)
    late = ("f1w13", "f1w2")
    held = ("f2w13", "f2w2")
    ws = [ln_g, ln_b, ffn1_w13, ffn1_w2, w_in, pool_w, pool_scale, q_norm_g, w_uq, kv_norm_g, w_ukv, w_out, mem_wq,
          mem_wkv, mem_wo, ffn2_w13, ffn2_w2]
    ms = [m_ln_g, m_ln_b, m_ffn1_w13, m_ffn1_w2, m_w_in, m_pool_w, m_pool_scale, m_q_norm_g, m_w_uq, m_kv_norm_g, m_w_ukv,
          m_w_out, m_mem_wq, m_mem_wkv, m_mem_wo, m_ffn2_w13, m_ffn2_w2]
    vs = [v_ln_g, v_ln_b, v_ffn1_w13, v_ffn1_w2, v_w_in, v_pool_w, v_pool_scale, v_q_norm_g, v_w_uq, v_kv_norm_g, v_w_ukv,
          v_w_out, v_mem_wq, v_mem_wkv, v_mem_wo, v_ffn2_w13, v_ffn2_w2]
    res = {}

    def update(n, deps=()):
        a = out_names.index(n)
        res[a] = adamw(f"adamw_{a}", ws[a], big[n].reshape(ws[a].shape), ms[a], vs[a], deps)
        return res[a][0]

    small_done = tuple(update(n) for n in ("lng", "lnb", "pool_w", "pool_scale", "gq", "gkv"))
    big.update(zip(rest_names, pair_share_wait("pair_share_wait_a", share_a, small_done)))
    first_done = tuple(update(n) for n in rest_names if n not in held)
    sums_b = red_end("b0", st_cb, 0, [sums1[n] for n in late], first_done)
    share_b, tok_b = share_start("b", late, sums_b)
    held_done = tuple(update(n, (tok_b,)) for n in held)
    big.update(zip(late, pair_share_wait("pair_share_wait_b", share_b, held_done)))
    for n in late:
        update(n)
    order = range(len(out_names))
    grads = [big[n].reshape(w_.shape) for n, w_ in zip(out_names, ws)]
    return (loss, grad_x, *grads, *[res[a][0] for a in order], *[res[a][1] for a in order], *[res[a][2] for a in order])
```

```python
import functools
import math

import jax
import jax.numpy as jnp
from jax import lax
from jax.experimental import pallas as pl
from jax.experimental.pallas import tpu as pltpu

f32 = jnp.float32
bf16 = jnp.bfloat16
SDS = jax.ShapeDtypeStruct
MESH = pl.DeviceIdType.MESH

D_MODEL = 1024
DEPTH = 2
N_MEM = 256
MEM_HEADS = 4
MEM_HEAD_DIM = D_MODEL // MEM_HEADS
POOL_WINDOWS = (2, 4, 8, 16)
POOL_WIDTH = 256
POOL_GROUP = 64
QK_NOPE = 128
QK_ROPE = 64
V_HEAD = 128
MLA_HEADS = 6
Q_LORA = 256
KV_LORA = 128
ROPE_BASE = 10000.0
D_FF = 2816
D_IN = POOL_WIDTH + Q_LORA + KV_LORA + QK_ROPE
ALPHA = (2 * DEPTH) ** 0.25
LN_EPS = 1e-5
RMS_EPS = 1e-6
NEG_INF = -1e30
MLA_SCALE = (QK_NOPE + QK_ROPE) ** -0.5
MEM_SCALE = MEM_HEAD_DIM ** -0.5
ADAM_LR = 0.001
ADAM_B1 = 0.9
ADAM_B2 = 0.999
ADAM_EPS = 1e-08
ADAM_WD = 0.01
ADAM_STEP = 10

N_CHIPS = 4
V7X_VMEM_LIMIT = 56 * 2**20
HALO = 16

_NT = (((1,), (1,)), ((), ()))
_TN = (((0,), (0,)), ((), ()))


def _dot(a, b):
    return jnp.dot(a, b, preferred_element_type=f32)


def _dot_nt(a, b):
    return lax.dot_general(a, b, _NT, preferred_element_type=f32)


def _dot_tn(a, b):
    return lax.dot_general(a, b, _TN, preferred_element_type=f32)


def _cp(*sem):
    return pltpu.CompilerParams(dimension_semantics=sem if sem else None, vmem_limit_bytes=V7X_VMEM_LIMIT)


_DEP_SPEC = pl.BlockSpec(memory_space=pl.ANY)


def _with_deps(body, n_in, deps):
    nd = len(deps)
    if not nd:
        return body

    def wrapped(*refs):
        return body(*refs[:n_in], *refs[n_in + nd:])

    return wrapped


def _tile(n, t):
    t = min(n, t)
    assert n % t == 0, (n, t)
    return t


def _row_tile(rows, cols, itemsize=4, target=2 * 2**20):
    best = None
    for t in range(16, rows + 1, 16):
        if rows % t == 0 and t * cols * itemsize <= target:
            best = t
    return best if best is not None else rows


def ffn_up(name, xb, w13, l, deps=()):
    S = xb.shape[0]
    ns = w13.shape[3]
    tm = _tile(S, 512)

    def body(x_ref, wg_ref, wu_ref, g_ref, u_ref, a_ref):
        x = x_ref[...]
        g = _dot(x, wg_ref[0, 0])
        u = _dot(x, wu_ref[0, 0])
        a = g * jax.nn.sigmoid(g) * u
        g_ref[...] = g.astype(bf16)
        u_ref[...] = u.astype(bf16)
        a_ref[...] = a.astype(bf16)

    out = SDS((S, 2 * ns), bf16)
    return pl.pallas_call(
        _with_deps(body, 3, deps), name=name, grid=(2, S // tm),
        in_specs=[pl.BlockSpec((tm, D_MODEL), lambda j, i: (i, 0)),
                  pl.BlockSpec((1, 1, D_MODEL, ns), lambda j, i: (l, j, 0, 0)),
                  pl.BlockSpec((1, 1, D_MODEL, ns), lambda j, i: (l, j + 2, 0, 0))] + [_DEP_SPEC] * len(deps),
        out_specs=[pl.BlockSpec((tm, ns), lambda j, i: (i, j))] * 3,
        out_shape=[out, out, out],
        compiler_params=_cp("parallel", "parallel"),
    )(xb, w13, w13, *deps)


def proj_res_ln(name, parts, ws, wl, x, g, b, rscale, deps=()):
    S = x.shape[0]
    tm = _tile(S, 512)
    n = len(parts)

    def body(*refs):
        p_refs, w_refs = refs[:n], refs[n:2 * n]
        x_ref, g_ref, b_ref, z_ref, y_ref, yb_ref = refs[2 * n:]
        acc = _dot(p_refs[0][...], w_refs[0][0])
        for k in range(1, n):
            acc = acc + _dot(p_refs[k][...], w_refs[k][0])
        if rscale != 1.0:
            acc = rscale * acc
        z = ALPHA * x_ref[...] + acc
        mu = jnp.mean(z, axis=-1, keepdims=True)
        zc = z - mu
        var = jnp.mean(zc * zc, axis=-1, keepdims=True)
        y = zc * lax.rsqrt(var + LN_EPS) * g_ref[...] + b_ref[...]
        z_ref[...] = z
        y_ref[...] = y
        yb_ref[...] = y.astype(bf16)

    row = lambda i: (i, 0)
    in_specs = [pl.BlockSpec((tm, p.shape[1]), row) for p in parts]
    in_specs += [pl.BlockSpec((1,) + w.shape[1:], functools.partial(lambda li, i: (li, 0, 0), li)) for w, li in zip(ws, wl)]
    in_specs += [pl.BlockSpec((tm, D_MODEL), row), pl.BlockSpec((1, D_MODEL), lambda i: (0, 0)),
                 pl.BlockSpec((1, D_MODEL), lambda i: (0, 0))] + [_DEP_SPEC] * len(deps)
    return pl.pallas_call(
        _with_deps(body, 2 * n + 3, deps), name=name, grid=(S // tm,), in_specs=in_specs,
        out_specs=[pl.BlockSpec((tm, D_MODEL), row)] * 3,
        out_shape=[SDS((S, D_MODEL), f32), SDS((S, D_MODEL), f32), SDS((S, D_MODEL), bf16)],
        compiler_params=_cp("parallel"),
    )(*parts, *ws, x, g, b, *deps)


def _ln_bwd_store(dyv, z_ref, g_ref, rscale, first, dzb_ref, dres_ref, dg_ref, db_ref):
    z = z_ref[...]
    mu = jnp.mean(z, axis=-1, keepdims=True)
    zc = z - mu
    rstd = lax.rsqrt(jnp.mean(zc * zc, axis=-1, keepdims=True) + LN_EPS)
    xhat = zc * rstd
    dxh = dyv * g_ref[...]
    m1 = jnp.mean(dxh, axis=-1, keepdims=True)
    m2 = jnp.mean(dxh * xhat, axis=-1, keepdims=True)
    dz = rstd * (dxh - m1 - xhat * m2)
    dzb_ref[...] = (rscale * dz).astype(bf16)
    dres_ref[...] = ALPHA * dz

    @pl.when(first)
    def _():
        dg_ref[...] = jnp.zeros_like(dg_ref)
        db_ref[...] = jnp.zeros_like(db_ref)

    dg_ref[...] += jnp.sum(dyv * xhat, axis=0, keepdims=True)
    db_ref[...] += jnp.sum(dyv, axis=0, keepdims=True)


def _ln_bwd_specs(S, tm, index):
    vec = pl.BlockSpec((1, D_MODEL), lambda *a: (0, 0))
    blk = pl.BlockSpec((tm, D_MODEL), index)
    in_specs = [blk, vec]
    out_specs = [blk, blk, vec, vec]
    out_shape = [SDS((S, D_MODEL), bf16), SDS((S, D_MODEL), f32), SDS((1, D_MODEL), f32), SDS((1, D_MODEL), f32)]
    return in_specs, out_specs, out_shape


def ffn_bwd_da(name, drb, w2, l, gate, up, deps=()):
    S = drb.shape[0]
    tm = _tile(S, 512)
    nh = D_FF // 2

    def body(dr_ref, w_ref, g_ref, u_ref, dh_ref):
        dr = dr_ref[...]
        for j in range(2):
            cols = slice(j * nh, (j + 1) * nh)
            da = _dot_nt(dr, w_ref[0, cols, :])
            g = g_ref[:, cols].astype(f32)
            u = u_ref[:, cols].astype(f32)
            sg = jax.nn.sigmoid(g)
            dh_ref[:, cols] = (da * u * (sg * (1.0 + g * (1.0 - sg)))).astype(bf16)
            dh_ref[:, D_FF + j * nh:D_FF + (j + 1) * nh] = (da * (g * sg)).astype(bf16)

    row = lambda i: (i, 0)
    return pl.pallas_call(
        _with_deps(body, 4, deps), name=name, grid=(S // tm,),
        in_specs=[pl.BlockSpec((tm, D_MODEL), row), pl.BlockSpec((1, D_FF, D_MODEL), lambda i: (l, 0, 0)),
                  pl.BlockSpec((tm, D_FF), row), pl.BlockSpec((tm, D_FF), row)] + [_DEP_SPEC] * len(deps),
        out_specs=pl.BlockSpec((tm, 2 * D_FF), row),
        out_shape=SDS((S, 2 * D_FF), bf16),
        compiler_params=_cp("parallel"),
    )(drb, w2, gate, up, *deps)


def ffn_dx(name, dh, w13, l, res, ln=None):
    S = dh.shape[0]
    ns = w13.shape[3]
    tm = _tile(S, 1024)
    last = N_CHIPS - 1
    row = lambda i, j: (i, 0)
    in_specs = [pl.BlockSpec((tm, ns), lambda i, j: (i, j)),
                pl.BlockSpec((1, 1, D_MODEL, ns), lambda i, j: (l, j, 0, 0)),
                pl.BlockSpec((tm, D_MODEL), row)]
    if ln is None:
        def body(dh_ref, w_ref, r_ref, o_ref):
            @pl.when(pl.program_id(1) == 0)
            def _():
                o_ref[...] = r_ref[...]

            o_ref[...] += _dot_nt(dh_ref[...], w_ref[0, 0])

        return pl.pallas_call(
            body, name=name, grid=(S // tm, N_CHIPS), in_specs=in_specs,
            out_specs=pl.BlockSpec((tm, D_MODEL), row), out_shape=SDS((S, D_MODEL), f32),
            compiler_params=_cp("parallel", "arbitrary"),
        )(dh, w13, res)

    z, g, rscale = ln

    def body_ln(dh_ref, w_ref, r_ref, z_ref, g_ref, dzb_ref, dres_ref, dg_ref, db_ref, acc_sc):
        i, j = pl.program_id(0), pl.program_id(1)

        @pl.when(j == 0)
        def _():
            acc_sc[...] = r_ref[...]

        acc_sc[...] += _dot_nt(dh_ref[...], w_ref[0, 0])

        @pl.when(j == last)
        def _():
            _ln_bwd_store(acc_sc[...], z_ref, g_ref, rscale, i == 0, dzb_ref, dres_ref, dg_ref, db_ref)

    ln_in, ln_out, ln_shape = _ln_bwd_specs(S, tm, row)
    return pl.pallas_call(
        body_ln, name=name, grid=(S // tm, N_CHIPS), in_specs=in_specs + ln_in, out_specs=ln_out, out_shape=ln_shape,
        scratch_shapes=[pltpu.VMEM((tm, D_MODEL), f32)], compiler_params=_cp("arbitrary", "arbitrary"),
    )(dh, w13, res, z, g)


def mm_nt_res(name, dys, ws, wl, res, out_dtype, ln=None):
    S = dys[0].shape[0]
    K = ws[0].shape[1]
    tm = _tile(S, 512)
    n = len(dys)
    n_in = 2 * n + (res is not None)

    def product(refs):
        acc = _dot_nt(refs[0][...], refs[n][0])
        for k in range(1, n):
            acc = acc + _dot_nt(refs[k][...], refs[n + k][0])
        if res is not None:
            acc = acc + refs[2 * n][...]
        return acc

    def body(*refs):
        refs[-1][...] = product(refs).astype(out_dtype)

    def body_ln(*refs):
        z_ref, g_ref, dzb_ref, dres_ref, dg_ref, db_ref = refs[n_in:]
        _ln_bwd_store(product(refs), z_ref, g_ref, ln[2], pl.program_id(0) == 0, dzb_ref, dres_ref, dg_ref, db_ref)

    row = lambda i: (i, 0)
    in_specs = [pl.BlockSpec((tm, d.shape[1]), row) for d in dys]
    in_specs += [pl.BlockSpec((1,) + w.shape[1:], functools.partial(lambda li, i: (li, 0, 0), li)) for w, li in zip(ws, wl)]
    args = list(dys) + list(ws)
    if res is not None:
        in_specs.append(pl.BlockSpec((tm, K), row))
        args.append(res)
    if ln is None:
        return pl.pallas_call(
            body, name=name, grid=(S // tm,), in_specs=in_specs,
            out_specs=pl.BlockSpec((tm, K), row), out_shape=SDS((S, K), out_dtype),
            compiler_params=_cp("parallel"),
        )(*args)
    ln_in, ln_out, ln_shape = _ln_bwd_specs(S, tm, row)
    return pl.pallas_call(
        body_ln, name=name, grid=(S // tm,), in_specs=in_specs + ln_in, out_specs=ln_out, out_shape=ln_shape,
        compiler_params=_cp("arbitrary"),
    )(*args, ln[0], ln[1])


def mm_tn(name, x, dy, col_shards=False, deps=()):
    S, K = x.shape
    N = dy.shape[1]
    ts = 512
    while ts * 2 <= min(S, 2048) and S % (ts * 2) == 0 and ts * 2 * K * 2 <= 6 * 2**20:
        ts *= 2
    ts = _tile(S, ts)
    if col_shards:
        tn = N // N_CHIPS
    else:
        tn = N
        while K * tn * 4 > 6 * 2**20 and tn % 256 == 0:
            tn //= 2
    nn = N // tn
    lead = ((0,) if col_shards else ()) + (slice(None), slice(None))

    def body(x_ref, dy_ref, o_ref):
        acc = _dot_tn(x_ref[...].astype(bf16), dy_ref[...].astype(bf16))

        @pl.when(pl.program_id(1) == 0)
        def _():
            o_ref[lead] = acc

        @pl.when(pl.program_id(1) != 0)
        def _():
            o_ref[lead] += acc

    if col_shards:
        out_spec = pl.BlockSpec((1, K, tn), lambda n, s: (n, 0, 0))
        out_shape = SDS((N_CHIPS, K, tn), f32)
    else:
        out_spec = pl.BlockSpec((K, tn), lambda n, s: (0, n))
        out_shape = SDS((K, N), f32)
    return pl.pallas_call(
        _with_deps(body, 2, deps), name=name, grid=(nn, S // ts),
        in_specs=[pl.BlockSpec((ts, K), lambda n, s: (s, 0)), pl.BlockSpec((ts, tn), lambda n, s: (s, n))]
        + [_DEP_SPEC] * len(deps),
        out_specs=out_spec, out_shape=out_shape, compiler_params=_cp("parallel", "arbitrary"),
    )(x, dy, *deps)


def mm_nn_shard(name, x, w, l):
    S, K = x.shape
    ns = w.shape[3]

    def body(x_ref, w_ref, o_ref):
        o_ref[...] = _dot(x_ref[...], w_ref[0, 0]).astype(bf16)

    return pl.pallas_call(
        body, name=name, grid=(N_CHIPS,),
        in_specs=[pl.BlockSpec((S, K), lambda j: (0, 0)), pl.BlockSpec((1, 1, K, ns), lambda j: (l, j, 0, 0))],
        out_specs=pl.BlockSpec((S, ns), lambda j: (0, j)), out_shape=SDS((S, N_CHIPS * ns), bf16),
        compiler_params=_cp("parallel"),
    )(x, w)


def loss_grad(name, y, t, ln):
    S = y.shape[0]
    tm = _tile(S, 512)
    z, g, rscale = ln

    def body(y_ref, t_ref, z_ref, g_ref, dzb_ref, dres_ref, dg_ref, db_ref, loss_ref):
        first = pl.program_id(0) == 0
        e = y_ref[...] - t_ref[...]
        _ln_bwd_store(e * (1.0 / D_MODEL), z_ref, g_ref, rscale, first, dzb_ref, dres_ref, dg_ref, db_ref)

        @pl.when(first)
        def _():
            loss_ref[...] = jnp.zeros_like(loss_ref)

        loss_ref[...] += jnp.full(loss_ref.shape, (0.5 / D_MODEL) * jnp.sum(e * e), f32)

    row = lambda i: (i, 0)
    ln_in, ln_out, ln_shape = _ln_bwd_specs(S, tm, row)
    return pl.pallas_call(
        body, name=name, grid=(S // tm,),
        in_specs=[pl.BlockSpec((tm, D_MODEL), row)] * 2 + ln_in,
        out_specs=ln_out + [pl.BlockSpec((8, 128), lambda i: (0, 0))],
        out_shape=ln_shape + [SDS((8, 128), f32)],
        compiler_params=_cp("arbitrary"),
    )(y, t, z, g)


def _half_sum(t):
    return t + pltpu.roll(t, 64, axis=1)


def mix_pre(name, xb, w_in, wq, wkv, l, gq, gkv, cs):
    S = xb.shape[0]
    tm = _tile(S, 512)
    H = MLA_HEADS
    W_EXT = w_in.shape[2]

    def body(x_ref, win_ref, wq_ref, wkv_ref, gq_ref, gkv_ref, cs_ref,
             u_ref, cq_ref, ckv_ref, cqn_ref, ckvn_ref, q_ref, k_ref, v_ref):
        h = _dot(x_ref[...], win_ref[0])
        u_ref[...] = h[:, :256]
        cq = h[:, 256:512]
        ckv = h[:, 512:640]
        cq_ref[...] = cq
        ckv_ref[...] = ckv
        cqn = (cq * lax.rsqrt(jnp.mean(cq * cq, axis=-1, keepdims=True) + RMS_EPS) * gq_ref[...]).astype(bf16)
        ckvn = (ckv * lax.rsqrt(jnp.mean(ckv * ckv, axis=-1, keepdims=True) + RMS_EPS) * gkv_ref[...]).astype(bf16)
        cqn_ref[...] = cqn
        ckvn_ref[...] = ckvn
        csv = cs_ref[...]
        lane = lax.broadcasted_iota(jnp.int32, (tm, 128), 1)
        kr = jnp.where(lane < 64, _half_sum(h[:, 640:768] * csv), 0.0).astype(bf16)
        kv = _dot(ckvn, wkv_ref[0])
        for hd in range(H):
            qe = _dot(cqn, wq_ref[0, hd])
            q_ref[hd, :, :128] = qe[:, :128].astype(bf16)
            q_ref[hd, :, 128:] = _half_sum(qe[:, 128:] * csv).astype(bf16)
            k_ref[hd, :, :128] = kv[:, 256 * hd:256 * hd + 128].astype(bf16)
            k_ref[hd, :, 128:] = kr
            v_ref[hd] = kv[:, 256 * hd + 128:256 * hd + 256].astype(bf16)

    row = lambda i: (i, 0)
    hrow = lambda i: (0, i, 0)
    return pl.pallas_call(
        body, name=name, grid=(S // tm,),
        in_specs=[pl.BlockSpec((tm, D_MODEL), row),
                  pl.BlockSpec((1, D_MODEL, W_EXT), lambda i: (l, 0, 0)),
                  pl.BlockSpec((1, H, Q_LORA, 256), lambda i: (l, 0, 0, 0)),
                  pl.BlockSpec((1, KV_LORA, H * 256), lambda i: (l, 0, 0)),
                  pl.BlockSpec((1, Q_LORA), lambda i: (0, 0)), pl.BlockSpec((1, KV_LORA), lambda i: (0, 0)),
                  pl.BlockSpec((tm, 128), row)],
        out_specs=[pl.BlockSpec((tm, 256), row), pl.BlockSpec((tm, Q_LORA), row), pl.BlockSpec((tm, KV_LORA), row),
                   pl.BlockSpec((tm, Q_LORA), row), pl.BlockSpec((tm, KV_LORA), row),
                   pl.BlockSpec((H, tm, 256), hrow), pl.BlockSpec((H, tm, 256), hrow), pl.BlockSpec((H, tm, 128), hrow)],
        out_shape=[SDS((S, 256), f32), SDS((S, Q_LORA), f32), SDS((S, KV_LORA), f32),
                   SDS((S, Q_LORA), bf16), SDS((S, KV_LORA), bf16),
                   SDS((H, S, 256), bf16), SDS((H, S, 256), bf16), SDS((H, S, 128), bf16)],
        compiler_params=_cp("parallel"),
    )(xb, w_in, wq, wkv, gq, gkv, cs)


def _group_select(col, a2, a4, a8, a16):
    return jnp.where(col < 64, a2, jnp.where(col < 128, a4, jnp.where(col < 192, a8, a16)))


def pool_fwd(name, u, wbd, scale):
    S = u.shape[0]
    tm = _tile(S, 512)
    hb = tm // HALO

    def body(u_ref, halo_ref, w_ref, s_ref, d_ref, y_ref):
        i = pl.program_id(0)
        cur = u_ref[...]
        halo = jnp.where(i > 0, halo_ref[...], 0.0)
        ext = jnp.concatenate([halo, cur], axis=0)
        s2 = ext + pltpu.roll(ext, 1, axis=0)
        s4 = s2 + pltpu.roll(s2, 2, axis=0)
        s8 = s4 + pltpu.roll(s4, 4, axis=0)
        s16 = s8 + pltpu.roll(s8, 8, axis=0)
        t1 = (i * tm + 1 + lax.broadcasted_iota(jnp.int32, (tm, 1), 0)).astype(f32)
        col = lax.broadcasted_iota(jnp.int32, (tm, 256), 1)
        m = _group_select(col, s2[HALO:] / jnp.minimum(t1, 2.0), s4[HALO:] / jnp.minimum(t1, 4.0),
                          s8[HALO:] / jnp.minimum(t1, 8.0), s16[HALO:] / jnp.minimum(t1, 16.0))
        d = (m - cur).astype(bf16)
        d_ref[...] = d
        y_ref[...] = (_dot(d, w_ref[...]) * s_ref[...]).astype(bf16)

    row = lambda i: (i, 0)
    return pl.pallas_call(
        body, name=name, grid=(S // tm,),
        in_specs=[pl.BlockSpec((tm, 256), row), pl.BlockSpec((HALO, 256), lambda i: (jnp.maximum(i * hb - 1, 0), 0)),
                  pl.BlockSpec((256, 256), lambda i: (0, 0)), pl.BlockSpec((1, 256), lambda i: (0, 0))],
        out_specs=[pl.BlockSpec((tm, 256), row)] * 2,
        out_shape=[SDS((S, 256), bf16), SDS((S, 256), bf16)],
        compiler_params=_cp("parallel"),
    )(u, u, wbd, scale)


def pool_bwd(name, dyp, d, wbd, scale):
    S = dyp.shape[0]
    tm = _tile(S, 512)
    hb = tm // HALO
    n_ext = tm + HALO

    def fwd_sum(e, steps):
        k = 1
        for _ in range(steps):
            e = e + pltpu.roll(e, n_ext - k, axis=0)
            k *= 2
        return e

    def body(dy_ref, halo_ref, d_ref, w_ref, s_ref, du_ref, dyw_ref, ds_ref):
        i = pl.program_id(0)
        sc = s_ref[...]
        w = w_ref[...]
        cur = dy_ref[...].astype(f32)
        halo = jnp.where(i < pl.num_programs(0) - 1, halo_ref[...].astype(f32), 0.0)
        dyw = jnp.concatenate([cur, halo], axis=0) * sc
        dyw_ref[...] = dyw[:tm].astype(bf16)
        dd = _dot_nt(dyw.astype(bf16), w)
        t1 = (i * tm + 1 + lax.broadcasted_iota(jnp.int32, (n_ext, 1), 0)).astype(f32)
        f2 = fwd_sum(dd / jnp.minimum(t1, 2.0), 1)
        f4 = fwd_sum(dd / jnp.minimum(t1, 4.0), 2)
        f8 = fwd_sum(dd / jnp.minimum(t1, 8.0), 3)
        f16 = fwd_sum(dd / jnp.minimum(t1, 16.0), 4)
        col = lax.broadcasted_iota(jnp.int32, (tm, 256), 1)
        du_ref[...] = (_group_select(col, f2[:tm], f4[:tm], f8[:tm], f16[:tm]) - dd[:tm]).astype(bf16)

        @pl.when(i == 0)
        def _():
            ds_ref[...] = jnp.zeros_like(ds_ref)

        ds_ref[...] += jnp.sum(cur * _dot(d_ref[...], w), axis=0, keepdims=True)

    row = lambda i: (i, 0)
    nhb = S // HALO
    return pl.pallas_call(
        body, name=name, grid=(S // tm,),
        in_specs=[pl.BlockSpec((tm, 256), row), pl.BlockSpec((HALO, 256), lambda i: (jnp.minimum((i + 1) * hb, nhb - 1), 0)),
                  pl.BlockSpec((tm, 256), row), pl.BlockSpec((256, 256), lambda i: (0, 0)),
                  pl.BlockSpec((1, 256), lambda i: (0, 0))],
        out_specs=[pl.BlockSpec((tm, 256), row), pl.BlockSpec((tm, 256), row), pl.BlockSpec((1, 256), lambda i: (0, 0))],
        out_shape=[SDS((S, 256), bf16), SDS((S, 256), bf16), SDS((1, 256), f32)],
        compiler_params=_cp("arbitrary"),
    )(dyp, dyp, d, wbd, scale)


def _diag_mask(r0, rn, kn):
    rc = (r0 + lax.broadcasted_iota(jnp.int32, (rn, 1), 0)) // 64
    cc = lax.broadcasted_iota(jnp.int32, (1, kn), 1) // 64
    return rc >= cc


def _diag_parts(tq):
    h = tq // 2
    return [(0, h, h), (h, h, tq)] if h % 128 == 0 else [(0, tq, tq)]


MLA_SCALE_LOG2 = MLA_SCALE * math.log2(math.e)


def mla_attn_fwd(name, q, k, v):
    H, S, _ = q.shape
    tq = _tile(S, 1024)
    nq = S // tq
    pairs = [(i, j) for i in range(nq) for j in range(i + 1)]
    it = jnp.asarray([p_[0] for p_ in pairs], jnp.int32)
    jt = jnp.asarray([p_[1] for p_ in pairs], jnp.int32)

    def body(it_ref, jt_ref, q_ref, k_ref, v_ref, o_ref, lse_ref, m_sc, l_sc, acc_sc):
        t = pl.program_id(1)
        i, j = it_ref[t], jt_ref[t]

        @pl.when(j == 0)
        def _():
            m_sc[...] = jnp.full_like(m_sc, NEG_INF)
            l_sc[...] = jnp.zeros_like(l_sc)
            acc_sc[...] = jnp.zeros_like(acc_sc)

        def part(r0, rn, kn, masked):
            rows, keys = slice(r0, r0 + rn), slice(0, kn)
            s = _dot_nt(q_ref[0, rows, :], k_ref[0, keys, :])
            if masked:
                s = jnp.where(_diag_mask(r0, rn, kn), s, NEG_INF)
            m_prev = m_sc[rows, :]
            m_new = jnp.maximum(m_prev, jnp.max(s, axis=-1, keepdims=True))
            p = jnp.exp2((s - jnp.tile(m_new, (1, kn // 128))) * MLA_SCALE_LOG2)
            a = jnp.exp2((m_prev - m_new) * MLA_SCALE_LOG2)
            l_sc[rows, :] = a * l_sc[rows, :] + jnp.sum(p, axis=-1, keepdims=True)
            acc_sc[rows, :] = a * acc_sc[rows, :] + _dot(p.astype(bf16), v_ref[0, keys, :])
            m_sc[rows, :] = m_new

        @pl.when(j < i)
        def _():
            part(0, tq, tq, False)

        @pl.when(j == i)
        def _():
            for r0, rn, kn in _diag_parts(tq):
                part(r0, rn, kn, True)
            o_ref[...] = (acc_sc[...] / l_sc[...]).astype(bf16)
            lse_ref[0] = m_sc[...] * MLA_SCALE_LOG2 + jnp.log2(l_sc[...])

    return pl.pallas_call(
        body, name=name,
        grid_spec=pltpu.PrefetchScalarGridSpec(
            num_scalar_prefetch=2, grid=(H, len(pairs)),
            in_specs=[pl.BlockSpec((1, tq, 256), lambda h, t, it_, jt_: (h, it_[t], 0)),
                      pl.BlockSpec((1, tq, 256), lambda h, t, it_, jt_: (h, jt_[t], 0)),
                      pl.BlockSpec((1, tq, 128), lambda h, t, it_, jt_: (h, jt_[t], 0))],
            out_specs=[pl.BlockSpec((tq, 128), lambda h, t, it_, jt_: (it_[t], h)),
                       pl.BlockSpec((1, tq, 128), lambda h, t, it_, jt_: (h, it_[t], 0))],
            scratch_shapes=[pltpu.VMEM((tq, 128), f32), pltpu.VMEM((tq, 128), f32), pltpu.VMEM((tq, 128), f32)]),
        out_shape=[SDS((S, H * 128), bf16), SDS((H, S, 128), f32)],
        compiler_params=_cp("parallel", "arbitrary"),
    )(it, jt, q, k, v)


def mla_attn_bwd(name, q, k, v, o, do, lse, do_col=0):
    H, S, _ = q.shape
    tq = _tile(S, 1024)
    nq = S // tq
    pairs = [(i, j) for j in range(nq) for i in range(j, nq)]
    it = jnp.asarray([p_[0] for p_ in pairs], jnp.int32)
    jt = jnp.asarray([p_[1] for p_ in pairs], jnp.int32)
    n_pairs = len(pairs)

    def body(it_ref, jt_ref, q_ref, k_ref, v_ref, o_ref, do_ref, lse_ref, dq_ref, dk_ref, dv_ref, dq_sc, dk_sc, dv_sc):
        t = pl.program_id(1)
        i, j = it_ref[t], jt_ref[t]

        @pl.when(t == 0)
        def _():
            dq_sc[...] = jnp.zeros_like(dq_sc)

        @pl.when(i == j)
        def _():
            dk_sc[...] = jnp.zeros_like(dk_sc)
            dv_sc[...] = jnp.zeros_like(dv_sc)

        def part(r0, rn, kn, masked):
            rows, keys = slice(r0, r0 + rn), slice(0, kn)
            qv, kv_, dov = q_ref[0, rows, :], k_ref[0, keys, :], do_ref[rows, :]
            s = _dot_nt(qv, kv_)
            if masked:
                s = jnp.where(_diag_mask(r0, rn, kn), s, NEG_INF)
            p = jnp.exp2(s * MLA_SCALE_LOG2 - jnp.tile(lse_ref[0, rows, :], (1, kn // 128)))
            dv_sc[keys, :] += _dot_tn(p.astype(bf16), dov)
            dp = _dot_nt(dov, v_ref[0, keys, :])
            delta = jnp.sum(dov.astype(f32) * o_ref[rows, :].astype(f32), axis=-1, keepdims=True)
            ds = (p * (dp - delta)).astype(bf16)
            dk_sc[keys, :] += _dot_tn(ds, qv)
            dq_rows = pl.ds(pl.multiple_of(i * tq + r0, 128), rn)
            dq_sc[dq_rows, :] += _dot(ds, kv_)

        @pl.when(i > j)
        def _():
            part(0, tq, tq, False)

        @pl.when(i == j)
        def _():
            for r0, rn, kn in _diag_parts(tq):
                part(r0, rn, kn, True)

        @pl.when(i == nq - 1)
        def _():
            dk_ref[0] = (dk_sc[...] * MLA_SCALE).astype(bf16)
            dv_ref[0] = dv_sc[...].astype(bf16)

        @pl.when(t == n_pairs - 1)
        def _():
            dq_ref[0] = (dq_sc[...] * MLA_SCALE).astype(bf16)

    qi = lambda h, t, it_, jt_: (h, it_[t], 0)
    kj = lambda h, t, it_, jt_: (h, jt_[t], 0)
    oi = lambda h, t, it_, jt_: (it_[t], h)
    doi = lambda h, t, it_, jt_: (it_[t], h + do_col)
    return pl.pallas_call(
        body, name=name,
        grid_spec=pltpu.PrefetchScalarGridSpec(
            num_scalar_prefetch=2, grid=(H, n_pairs),
            in_specs=[pl.BlockSpec((1, tq, 256), qi), pl.BlockSpec((1, tq, 256), kj), pl.BlockSpec((1, tq, 128), kj),
                      pl.BlockSpec((tq, 128), oi), pl.BlockSpec((tq, 128), doi), pl.BlockSpec((1, tq, 128), qi)],
            out_specs=[pl.BlockSpec((1, S, 256), lambda h, t, it_, jt_: (h, 0, 0)), pl.BlockSpec((1, tq, 256), kj),
                       pl.BlockSpec((1, tq, 128), kj)],
            scratch_shapes=[pltpu.VMEM((S, 256), f32), pltpu.VMEM((tq, 256), f32), pltpu.VMEM((tq, 128), f32)]),
        out_shape=[SDS((H, S, 256), bf16), SDS((H, S, 256), bf16), SDS((H, S, 128), bf16)],
        compiler_params=_cp("parallel", "arbitrary"),
    )(it, jt, q, k, v, o, do, lse)


def mix_post_bwd(name, dq, dk, dv, wq, wkv, l, cq, ckv, gq, gkv, cs):
    H, S, _ = dq.shape
    tm = _tile(S, 512)

    def rms_bwd(dyn, c, g):
        r = lax.rsqrt(jnp.mean(c * c, axis=-1, keepdims=True) + RMS_EPS)
        ch = c * r
        dyg = dyn * g
        dc = r * (dyg - ch * jnp.mean(dyg * ch, axis=-1, keepdims=True))
        return dc, jnp.sum(dyn * ch, axis=0, keepdims=True)

    def body(dq_ref, dk_ref, dv_ref, wq_ref, wkv_ref, cq_ref, ckv_ref, gq_ref, gkv_ref, cs_ref,
             dqe_ref, dkv_ref, dh_ref, dgq_ref, dgkv_ref):
        csv = cs_ref[...]
        lane = lax.broadcasted_iota(jnp.int32, (tm, 128), 1)
        dcqn = jnp.zeros((tm, Q_LORA), f32)
        dkr = jnp.zeros((tm, 128), f32)
        for hd in range(H):
            dqh = dq_ref[hd].astype(f32)
            dqe = jnp.concatenate([dqh[:, :128], _half_sum(dqh[:, 128:]) * csv], axis=1).astype(bf16)
            dqe_ref[:, 256 * hd:256 * hd + 256] = dqe
            dcqn = dcqn + _dot_nt(dqe, wq_ref[0, hd])
            dkh = dk_ref[hd].astype(f32)
            dkv_ref[:, 256 * hd:256 * hd + 128] = dkh[:, :128].astype(bf16)
            dkv_ref[:, 256 * hd + 128:256 * hd + 256] = dv_ref[hd].astype(bf16)
            dkr = dkr + dkh[:, 128:]
        dckvn = _dot_nt(dkv_ref[...], wkv_ref[0])
        dblk = _half_sum(jnp.where(lane < 64, dkr, 0.0)) * csv
        dcq, dgq = rms_bwd(dcqn, cq_ref[...], gq_ref[...])
        dckv, dgkv = rms_bwd(dckvn, ckv_ref[...], gkv_ref[...])
        dh_ref[:, :256] = dcq.astype(bf16)
        dh_ref[:, 256:384] = dckv.astype(bf16)
        dh_ref[:, 384:] = dblk.astype(bf16)

        @pl.when(pl.program_id(0) == 0)
        def _():
            dgq_ref[...] = jnp.zeros_like(dgq_ref)
            dgkv_ref[...] = jnp.zeros_like(dgkv_ref)

        dgq_ref[...] += dgq
        dgkv_ref[...] += dgkv

    row = lambda i: (i, 0)
    hrow = lambda i: (0, i, 0)
    return pl.pallas_call(
        body, name=name, grid=(S // tm,),
        in_specs=[pl.BlockSpec((H, tm, 256), hrow), pl.BlockSpec((H, tm, 256), hrow), pl.BlockSpec((H, tm, 128), hrow),
                  pl.BlockSpec((1, H, Q_LORA, 256), lambda i: (l, 0, 0, 0)),
                  pl.BlockSpec((1, KV_LORA, H * 256), lambda i: (l, 0, 0)),
                  pl.BlockSpec((tm, Q_LORA), row), pl.BlockSpec((tm, KV_LORA), row),
                  pl.BlockSpec((1, Q_LORA), lambda i: (0, 0)), pl.BlockSpec((1, KV_LORA), lambda i: (0, 0)),
                  pl.BlockSpec((tm, 128), row)],
        out_specs=[pl.BlockSpec((tm, H * 256), row), pl.BlockSpec((tm, H * 256), row), pl.BlockSpec((tm, 512), row),
                   pl.BlockSpec((1, Q_LORA), lambda i: (0, 0)), pl.BlockSpec((1, KV_LORA), lambda i: (0, 0))],
        out_shape=[SDS((S, H * 256), bf16), SDS((S, H * 256), bf16), SDS((S, 512), bf16),
                   SDS((1, Q_LORA), f32), SDS((1, KV_LORA), f32)],
        compiler_params=_cp("arbitrary"),
    )(dq, dk, dv, wq, wkv, cq, ckv, gq, gkv, cs)


def _cross_probs(qb, kv_ref, hd):
    cols = slice(hd * MEM_HEAD_DIM, (hd + 1) * MEM_HEAD_DIM)
    s = _dot_nt(qb[:, cols], kv_ref[:, cols]) * MEM_SCALE
    e = jnp.exp(s - jnp.max(s, axis=-1, keepdims=True))
    return e / jnp.sum(e, axis=-1, keepdims=True)


def cross_fwd(name, xb, xf, wq, wo, l, kv, g, b):
    S = xb.shape[0]
    tm = _tile(S, 512)
    M = kv.shape[0]

    def body(x_ref, xf_ref, wq_ref, wo_ref, k_ref, v_ref, g_ref, b_ref, q_ref, o_ref, z_ref, y_ref, yb_ref):
        qb = _dot(x_ref[...], wq_ref[0]).astype(bf16)
        q_ref[...] = qb
        for hd in range(MEM_HEADS):
            cols = slice(hd * MEM_HEAD_DIM, (hd + 1) * MEM_HEAD_DIM)
            p = _cross_probs(qb, k_ref, hd)
            o_ref[:, cols] = _dot(p.astype(bf16), v_ref[:, cols]).astype(bf16)
        z = ALPHA * xf_ref[...] + _dot(o_ref[...], wo_ref[0])
        mu = jnp.mean(z, axis=-1, keepdims=True)
        zc = z - mu
        var = jnp.mean(zc * zc, axis=-1, keepdims=True)
        y = zc * lax.rsqrt(var + LN_EPS) * g_ref[...] + b_ref[...]
        z_ref[...] = z
        y_ref[...] = y
        yb_ref[...] = y.astype(bf16)

    row = lambda i: (i, 0)
    wspec = pl.BlockSpec((1, D_MODEL, D_MODEL), lambda i: (l, 0, 0))
    vec = pl.BlockSpec((1, D_MODEL), lambda i: (0, 0))
    blk = pl.BlockSpec((tm, D_MODEL), row)
    return pl.pallas_call(
        body, name=name, grid=(S // tm,),
        in_specs=[blk, blk, wspec, wspec, pl.BlockSpec((M, D_MODEL), lambda i: (0, 0)),
                  pl.BlockSpec((M, D_MODEL), lambda i: (0, 1)), vec, vec],
        out_specs=[blk] * 5,
        out_shape=[SDS((S, D_MODEL), bf16), SDS((S, D_MODEL), bf16), SDS((S, D_MODEL), f32), SDS((S, D_MODEL), f32),
                   SDS((S, D_MODEL), bf16)],
        compiler_params=_cp("parallel"),
    )(xb, xf, wq, wo, kv, kv, g, b)


def cross_bwd(name, dzb, wo, l, qb, kv, deps=()):
    S = dzb.shape[0]
    tm = _tile(S, 512)
    M = kv.shape[0]

    def body(dz_ref, wo_ref, q_ref, k_ref, v_ref, dq_ref, dkv_ref):
        @pl.when(pl.program_id(0) == 0)
        def _():
            dkv_ref[...] = jnp.zeros_like(dkv_ref)

        do = _dot_nt(dz_ref[...], wo_ref[0]).astype(bf16)
        qv = q_ref[...]
        for hd in range(MEM_HEADS):
            cols = slice(hd * MEM_HEAD_DIM, (hd + 1) * MEM_HEAD_DIM)
            vcols = slice(D_MODEL + hd * MEM_HEAD_DIM, D_MODEL + (hd + 1) * MEM_HEAD_DIM)
            p = _cross_probs(qv, k_ref, hd)
            doh = do[:, cols]
            dkv_ref[:, vcols] += _dot_tn(p.astype(bf16), doh)
            dp = _dot_nt(doh, v_ref[:, cols])
            ds = (p * (dp - jnp.sum(dp * p, axis=-1, keepdims=True)) * MEM_SCALE).astype(bf16)
            dq_ref[:, cols] = _dot(ds, k_ref[:, cols]).astype(bf16)
            dkv_ref[:, cols] += _dot_tn(ds, qv[:, cols])

    row = lambda i: (i, 0)
    blk = pl.BlockSpec((tm, D_MODEL), row)
    return pl.pallas_call(
        _with_deps(body, 5, deps), name=name, grid=(S // tm,),
        in_specs=[blk, pl.BlockSpec((1, D_MODEL, D_MODEL), lambda i: (l, 0, 0)), blk,
                  pl.BlockSpec((M, D_MODEL), lambda i: (0, 0)), pl.BlockSpec((M, D_MODEL), lambda i: (0, 1))]
        + [_DEP_SPEC] * len(deps),
        out_specs=[blk, pl.BlockSpec((M, 2 * D_MODEL), lambda i: (0, 0))],
        out_shape=[SDS((S, D_MODEL), bf16), SDS((M, 2 * D_MODEL), f32)],
        compiler_params=_cp("arbitrary"),
    )(dzb, wo, qb, kv, kv, *deps)


def adamw(name, w, g, m, v, deps=()):
    shape = w.shape
    cols = shape[-1]
    rows = math.prod(shape[:-1])
    tr = _row_tile(rows, cols, target=2 * 2**20)
    c1 = 1.0 - ADAM_B1 ** ADAM_STEP
    c2 = 1.0 - ADAM_B2 ** ADAM_STEP

    def body(w_ref, g_ref, m_ref, v_ref, d_ref, nm_ref, nv_ref):
        gv = g_ref[...]
        nm = ADAM_B1 * m_ref[...] + (1.0 - ADAM_B1) * gv
        nv = ADAM_B2 * v_ref[...] + (1.0 - ADAM_B2) * (gv * gv)
        d_ref[...] = -ADAM_LR * ((nm / c1) / (jnp.sqrt(nv / c2) + ADAM_EPS) + ADAM_WD * w_ref[...])
        nm_ref[...] = nm
        nv_ref[...] = nv

    blk = pl.BlockSpec((tr, cols), lambda i: (i, 0))
    flat = SDS((rows, cols), f32)
    outs = pl.pallas_call(
        _with_deps(body, 4, deps), name=name, grid=(rows // tr,), in_specs=[blk] * 4 + [_DEP_SPEC] * len(deps),
        out_specs=[blk] * 3, out_shape=[flat] * 3, compiler_params=_cp("parallel"),
    )(*[a.reshape(rows, cols) for a in (w, g, m, v)], *deps)
    return [o.reshape(shape) for o in outs]


def _me():
    return lax.axis_index("x"), lax.axis_index("y"), lax.axis_index("c")


def _other_chips(x, y):
    return [(1 - x, y), (x, 1 - y), (1 - x, 1 - y)]


def _pair_share_each(owners, bufs, sems, mine, act):
    x, y, c = _me()
    for o in range(2):
        slots = [(a, lyr) for a in range(len(bufs)) for lyr in range(DEPTH) if owners[a][lyr] == o]

        @pl.when((c == o) if mine else (c != o))
        def _(slots=slots):
            for a, lyr in slots:
                slot = bufs[a].at[lyr]
                act(_rcopy(slot, slot, sems[0].at[2 * a + lyr], sems[1].at[2 * a + lyr], (x, y, 1 - c)))


def pair_share_start(name, sums, owners, after):
    def body_fn(b_in, s_in, s_out):
        _pair_share_each(owners, b_in, s_out, True, lambda cp: cp.start())

    outs, sems, token = _split_call(name, body_fn, list(sums), [], [2 * len(sums)] * 2, after)
    return (outs, sems[0], sems[1], owners), token


def pair_share_wait(name, st, after):
    bufs, send, recv, owners = st

    def body_fn(b_in, s_in, s_out):
        _pair_share_each(owners, b_in, s_in, True, lambda cp: cp.wait_send())
        _pair_share_each(owners, b_in, s_in, False, lambda cp: cp.wait_recv())

    outs, _, _ = _split_call(name, body_fn, list(bufs), [send, recv], [], after)
    return outs


def allsum_small(name, v, deps=()):
    R = v.shape[0]

    def body(v_ref, o_ref, all_ref, send_sems, recv_sems, local_sem):
        x, y, c = _me()
        me, sibling = (x, y, c), (x, y, 1 - c)
        chips = _other_chips(x, y)

        def rows(px, py, pc):
            return all_ref.at[4 * px + 2 * py + pc]

        def copy(k, block, to, src=None):
            return pltpu.make_async_remote_copy(
                src_ref=rows(*block) if src is None else src, dst_ref=rows(*block),
                send_sem=send_sems.at[k], recv_sem=recv_sems.at[k], device_id=to, device_id_type=MESH)

        mine = pltpu.make_async_copy(v_ref, rows(*me), local_sem)
        mine.start()
        first = [copy(0, me, sibling, src=v_ref)]
        first += [copy(1 + j, me, (*chip, c), src=v_ref) for j, chip in enumerate(chips)]
        for cp in first:
            cp.start()
        passed = [copy(4 + j, (*chip, c), sibling) for j, chip in enumerate(chips)]
        for j, chip in enumerate(chips):
            copy(1 + j, (*chip, c), me).wait_recv()
            passed[j].start()
        copy(0, sibling, me).wait_recv()
        for j, chip in enumerate(chips):
            copy(4 + j, (*chip, 1 - c), me).wait_recv()
        for cp in first + passed:
            cp.wait_send()
        mine.wait()
        acc = all_ref[0]
        for d in range(1, 8):
            acc = acc + all_ref[d]
        o_ref[...] = acc

    return pl.pallas_call(
        _with_deps(body, 1, deps), name=name,
        in_specs=[pl.BlockSpec(memory_space=pltpu.VMEM)] + [_DEP_SPEC] * len(deps),
        out_specs=pl.BlockSpec(memory_space=pltpu.VMEM),
        out_shape=SDS((R, 128), f32),
        scratch_shapes=[pltpu.VMEM((8, R, 128), f32), pltpu.SemaphoreType.DMA((7,)), pltpu.SemaphoreType.DMA((7,)),
                        pltpu.SemaphoreType.DMA],
        compiler_params=pltpu.CompilerParams(vmem_limit_bytes=V7X_VMEM_LIMIT),
    )(v, *deps)


def _swap_half(r):
    return jnp.concatenate([-r[..., 32:], r[..., :32]], axis=-1)


def _unswap_add(p, qg):
    return p + jnp.concatenate([qg[..., 32:], -qg[..., :32]], axis=-1)


def _block_diag(pw):
    L = pw.shape[0]
    out = jnp.zeros((L, 256, 256), pw.dtype)
    for gi in range(4):
        out = out.at[:, 64 * gi:64 * gi + 64, 64 * gi:64 * gi + 64].set(pw[:, gi])
    return out


def _to_col_shards(w):
    *lead, K, N = w.shape
    nl = len(lead)
    return w.reshape(*lead, K, N_CHIPS, N // N_CHIPS).transpose(*range(nl), nl + 1, nl, nl + 2)


def _from_col_shards(w):
    *lead, C, K, n = w.shape
    nl = len(lead)
    return w.transpose(*range(nl), nl + 1, nl, nl + 2).reshape(*lead, K, C * n)


_HBM_SPEC = pl.BlockSpec(memory_space=pltpu.HBM)
_SEM_SPEC = pl.BlockSpec(memory_space=pltpu.SEMAPHORE)
_ANY_SPEC = pl.BlockSpec(memory_space=pl.ANY)
_DATAFLOW = pltpu.SideEffectType.DATAFLOW_SIDE_EFFECTING


def _split_call(name, body_fn, bufs, sems_in, sems_out_sizes, after):
    nb, ni, no = len(bufs), len(sems_in), len(sems_out_sizes)
    afters = () if after is None else tuple(after) if isinstance(after, (tuple, list)) else (after,)

    def body(*refs):
        k = nb + ni + len(afters)
        body_fn(refs[:nb], refs[nb:nb + ni], refs[k:k + no])
        refs[-1][...] = jnp.zeros((8, 128), f32)

    outs = pl.pallas_call(
        body, name=name,
        in_specs=[_HBM_SPEC] * nb + [_SEM_SPEC] * ni + [_ANY_SPEC] * len(afters),
        out_specs=[_SEM_SPEC] * no + [_HBM_SPEC] * nb + [pl.BlockSpec(memory_space=pltpu.VMEM)],
        out_shape=[pltpu.SemaphoreType.DMA((s,)) for s in sems_out_sizes]
        + [pltpu.HBM(b.shape, b.dtype) for b in bufs] + [SDS((8, 128), f32)],
        input_output_aliases={i: no + i for i in range(nb)},
        compiler_params=pltpu.CompilerParams(has_side_effects=_DATAFLOW),
    )(*[pltpu.with_memory_space_constraint(b, pltpu.HBM) for b in bufs], *sems_in, *afters)
    return list(outs[no:no + nb]), list(outs[:no]), outs[-1]


def _rcopy(src, dst, ssem, rsem, to):
    return pltpu.make_async_remote_copy(src_ref=src, dst_ref=dst, send_sem=ssem, recv_sem=rsem, device_id=to,
                                        device_id_type=MESH)


def gather_start(name, groups, after):
    flat = [b for bufs, _ in groups for b in bufs]
    sizes = [3 * len(bufs) for bufs, _ in groups for _ in range(2)]

    def body_fn(b_in, s_in, s_out):
        x, y, c = _me()
        q = 2 * x + y
        chips = _other_chips(x, y)
        pos = 0
        for gi, (bufs, owner) in enumerate(groups):
            refs = b_in[pos:pos + len(bufs)]
            pos += len(bufs)

            @pl.when(c == owner)
            def _(refs=refs, send=s_out[2 * gi], recv=s_out[2 * gi + 1]):
                for a, r in enumerate(refs):
                    for k, (cx, cy) in enumerate(chips):
                        _rcopy(r.at[q], r.at[q], send.at[3 * a + k], recv.at[3 * a + k], (cx, cy, c)).start()

    outs, sems, token = _split_call(name, body_fn, flat, [], sizes, after)
    res, pos = [], 0
    for gi, (bufs, owner) in enumerate(groups):
        res.append((outs[pos:pos + len(bufs)], sems[2 * gi], sems[2 * gi + 1], owner))
        pos += len(bufs)
    return res, token


def gather_forward(name, grp, after):
    bufs, send, recv, owner = grp
    n3 = 3 * len(bufs)

    def body_fn(b_in, s_in, s_out):
        x, y, c = _me()
        q = 2 * x + y
        sibling = (x, y, 1 - c)
        chips = _other_chips(x, y)

        @pl.when(c == owner)
        def _():
            for a, r in enumerate(b_in):
                for k, (cx, cy) in enumerate(chips):
                    i = 3 * a + k
                    land = r.at[2 * cx + cy]
                    _rcopy(r.at[q], r.at[q], s_in[0].at[i], s_in[1].at[i], (cx, cy, c)).wait_send()
                    _rcopy(land, land, s_in[0].at[i], s_in[1].at[i], (cx, cy, c)).wait_recv()
                    _rcopy(land, land, s_out[0].at[i], s_out[1].at[i], sibling).start()

    outs, sems, token = _split_call(name, body_fn, bufs, [send, recv], [n3, n3], after)
    return (outs, sems[0], sems[1], owner), token


def gather_finish(name, grp, after):
    bufs, fsend, frecv, owner = grp

    def body_fn(b_in, s_in, s_out):
        x, y, c = _me()
        sibling = (x, y, 1 - c)
        chips = _other_chips(x, y)

        def each(wait):
            for a, r in enumerate(b_in):
                for k, (cx, cy) in enumerate(chips):
                    land = r.at[2 * cx + cy]
                    wait(_rcopy(land, land, s_in[0].at[3 * a + k], s_in[1].at[3 * a + k], sibling))

        @pl.when(c == owner)
        def _():
            each(lambda cp: cp.wait_send())

        @pl.when(c != owner)
        def _():
            each(lambda cp: cp.wait_recv())

    outs, _, _ = _split_call(name, body_fn, bufs, [fsend, frecv], [], after)
    return outs


def _by_owner(owners):
    return [[a for a, o_ in enumerate(owners) if o_ == o] for o in range(2)]


def pair_send_start(name, gs, owners, after):
    n = len(gs)
    lands = [lax.empty(g.shape, g.dtype) for g in gs]

    def body_fn(b_in, s_in, s_out):
        x, y, c = _me()
        for o, idx in enumerate(_by_owner(owners)):
            @pl.when(c == 1 - o)
            def _(o=o, idx=idx):
                for a in idx:
                    _rcopy(b_in[a], b_in[n + a], s_out[0].at[a], s_out[1].at[a], (x, y, o)).start()

    outs, sems, token = _split_call(name, body_fn, list(gs) + lands, [], [n, n], after)
    return (outs[:n], outs[n:], sems[0], sems[1], owners), token


def pair_send_wait(name, st, after):
    gs, lands, send, recv, owners = st
    n = len(gs)

    def body_fn(b_in, s_in, s_out):
        x, y, c = _me()
        for o, idx in enumerate(_by_owner(owners)):
            @pl.when(c == 1 - o)
            def _(o=o, idx=idx):
                for a in idx:
                    _rcopy(b_in[a], b_in[n + a], s_in[0].at[a], s_in[1].at[a], (x, y, o)).wait_send()

            @pl.when(c == o)
            def _(o=o, idx=idx):
                for a in idx:
                    _rcopy(b_in[a], b_in[n + a], s_in[0].at[a], s_in[1].at[a], (x, y, 1 - o)).wait_recv()

    outs, _, _ = _split_call(name, body_fn, list(gs) + list(lands), [send, recv], [], after)
    return outs[:n], outs[n:]


def chip_exchange_start(name, psums, owners, after):
    n = len(psums)
    lands = [lax.empty((3,) + p.shape[1:], p.dtype) for p in psums]

    def body_fn(b_in, s_in, s_out):
        x, y, c = _me()
        chips = _other_chips(x, y)
        for o, idx in enumerate(_by_owner(owners)):
            @pl.when(c == o)
            def _(idx=idx):
                for a in idx:
                    for k, (cx, cy) in enumerate(chips):
                        _rcopy(b_in[a].at[2 * cx + cy], b_in[n + a].at[k], s_out[0].at[3 * a + k],
                               s_out[1].at[3 * a + k], (cx, cy, c)).start()

    outs, sems, token = _split_call(name, body_fn, list(psums) + lands, [], [3 * n, 3 * n], after)
    return (outs[:n], outs[n:], sems[0], sems[1], owners), token


def chip_exchange_wait(name, st, after):
    psums, lands, send, recv, owners = st
    n = len(psums)

    def body_fn(b_in, s_in, s_out):
        x, y, c = _me()
        chips = _other_chips(x, y)
        for o, idx in enumerate(_by_owner(owners)):
            @pl.when(c == o)
            def _(idx=idx):
                for a in idx:
                    for k, (cx, cy) in enumerate(chips):
                        cp = _rcopy(b_in[a].at[2 * cx + cy], b_in[n + a].at[k], s_in[0].at[3 * a + k],
                                    s_in[1].at[3 * a + k], (cx, cy, c))
                        cp.wait_send()
                        cp.wait_recv()

    outs, _, _ = _split_call(name, body_fn, list(psums) + list(lands), [send, recv], [], after)
    return outs[:n], outs[n:]


def pair_sum(name, g, recv, flag):
    shape = g.shape
    cols = shape[-1]
    rows = math.prod(shape[:-1])
    tr = _row_tile(rows, cols, target=4 * 2**20)

    def body(f_ref, g_ref, r_ref, o_ref):
        o_ref[...] = (g_ref[...] + r_ref[...]).astype(bf16)

    blk = pl.BlockSpec((tr, cols), lambda i, f_ref: (i * f_ref[0], 0))
    out = pl.pallas_call(
        body, name=name,
        grid_spec=pltpu.PrefetchScalarGridSpec(num_scalar_prefetch=1, grid=(rows // tr,), in_specs=[blk, blk],
                                               out_specs=blk),
        out_shape=SDS((rows, cols), bf16), compiler_params=_cp("arbitrary"),
    )(flag, g.reshape(rows, cols), recv.reshape(rows, cols))
    return out.reshape(shape)


def chip_sum(name, psum, recv, qf_arr, layer, prev):
    shard = psum.shape[1:]
    cols = shard[-1]
    rows = math.prod(shard[:-1])
    tr = _row_tile(rows, cols, target=4 * 2**20)

    def body(qf_ref, p_ref, r_ref, *rest):
        rest[-1][0] = ((p_ref[0].astype(f32) + r_ref[0].astype(f32)) + r_ref[1].astype(f32)) + r_ref[2].astype(f32)

    in_specs = [pl.BlockSpec((1, tr, cols), lambda i, qf: (qf[0], i * qf[1], 0)),
                pl.BlockSpec((3, tr, cols), lambda i, qf: (0, i * qf[1], 0))]
    args = [qf_arr, psum.reshape(N_CHIPS, rows, cols), recv.reshape(3, rows, cols)]
    aliases = {}
    if prev is not None:
        in_specs.append(pl.BlockSpec(memory_space=pl.ANY))
        args.append(prev.reshape(DEPTH, rows, cols))
        aliases = {3: 0}
    out = pl.pallas_call(
        body, name=name,
        grid_spec=pltpu.PrefetchScalarGridSpec(
            num_scalar_prefetch=1, grid=(rows // tr,), in_specs=in_specs,
            out_specs=pl.BlockSpec((1, tr, cols), lambda i, qf: (layer, i * qf[1], 0))),
        out_shape=SDS((DEPTH, rows, cols), f32), input_output_aliases=aliases, compiler_params=_cp("arbitrary"),
    )(*args)
    return out.reshape((DEPTH,) + shard)


W_NAMES = ("f1w13", "f1w2", "win", "wuq", "wukv", "wout", "mwq", "mwkv", "mwo", "f2w13", "f2w2")
MIX_NAMES = ("win", "wuq", "wukv")
MID_NAMES = ("wout", "mwq", "mwkv", "mwo")
FFN2_NAMES = ("f2w13", "f2w2")
REDUCER = (dict(f1w13=0, f1w2=1, f2w13=0, win=0, wuq=0, wukv=0, f2w2=1, mwkv=1, wout=1, mwq=1, mwo=1),
           dict(f1w13=0, f2w2=0, mwkv=0, wout=0, f2w13=1, f1w2=1, mwq=1, mwo=1, win=1, wuq=1, wukv=1))


def kernel(x, mem, positions, ln_g, ln_b, ffn1_w13, ffn1_w2, w_in, pool_w, pool_scale, q_norm_g, w_uq, kv_norm_g, w_ukv, w_out, mem_wq, mem_wkv, mem_wo, ffn2_w13, ffn2_w2, loss_target, m_ln_g, m_ln_b, m_ffn1_w13, m_ffn1_w2, m_w_in, m_pool_w, m_pool_scale, m_q_norm_g, m_w_uq, m_kv_norm_g, m_w_ukv, m_w_out, m_mem_wq, m_mem_wkv, m_mem_wo, m_ffn2_w13, m_ffn2_w2, v_ln_g, v_ln_b, v_ffn1_w13, v_ffn1_w2, v_w_in, v_pool_w, v_pool_scale, v_q_norm_g, v_w_uq, v_kv_norm_g, v_w_ukv, v_w_out, v_mem_wq, v_mem_wkv, v_mem_wo, v_ffn2_w13, v_ffn2_w2):
    L = DEPTH
    qx, qy, _ = _me()
    chip = 2 * qx + qy
    vec = lambda a: a.reshape(1, -1)

    shards = dict(zip(W_NAMES, (ffn1_w13, ffn1_w2, w_in, w_uq, w_ukv, w_out, mem_wq, mem_wkv, mem_wo, ffn2_w13, ffn2_w2)))

    def place(sh, slot):
        return lax.dynamic_update_slice(lax.empty((N_CHIPS,) + sh.shape, bf16), sh.astype(bf16)[None],
                                        (slot,) + (0,) * sh.ndim)

    first = ("f1w13", "f1w2")
    bufs = [dict(), dict()]
    for n in first:
        bufs[0][n] = place(shards[n][0], chip)
    gw = [dict(), dict()]
    (g0,), tok = gather_start("gather_a_start", [([bufs[0][n] for n in first], 0)], None)
    chip_then = chip + tok[0, 0].astype(jnp.int32)
    for l in range(L):
        for n in W_NAMES:
            if n not in bufs[l]:
                bufs[l][n] = place(shards[n][l], chip_then)
    others = tuple(bufs[l][n] for l in range(L) for n in W_NAMES if (l, n) not in ((0, first[0]), (0, first[1])))
    g0, tok = gather_forward("gather_a_forward", g0, others)

    ln_pad = jnp.zeros((2, L, 4, N_CHIPS, D_MODEL // N_CHIPS), f32)
    ln_pad = lax.dynamic_update_slice(ln_pad, jnp.stack([ln_g, ln_b])[:, :, :, None, :], (0, 0, 0, chip, 0))
    ln_sum = allsum_small("allsum_ln", ln_pad.reshape(-1, 128), (tok,))
    ln_full = (ln_sum * 0.5).reshape(2, L, 4, D_MODEL)
    lng, lnb = ln_full[0], ln_full[1]

    gw[0]["f1w13"], gw[0]["f1w2"] = gather_finish("gather_a_finish", g0, ln_sum)
    (g_mix, g_mid, g_ffn2, g_l1), tok_b = gather_start(
        "gather_b_start",
        [([bufs[0][n] for n in MIX_NAMES], 0), ([bufs[0][n] for n in MID_NAMES], 0), ([bufs[0][n] for n in FFN2_NAMES], 0),
         ([bufs[1][n] for n in W_NAMES], 1)], ln_sum)

    half = QK_ROPE // 2
    inv_freq = ROPE_BASE ** (-jnp.arange(half, dtype=f32) / half)
    ang = positions[0].astype(f32)[:, None] * inv_freq
    cos, sin = jnp.cos(ang), jnp.sin(ang)
    cs = jnp.concatenate([cos, cos, sin, sin], axis=-1)

    memb = mem[0].astype(bf16)
    xf = x[0]
    xb = xf.astype(bf16)
    dep = (tok_b,)

    saved, W = [], [None, None]
    for l in range(L):
        sv = {}
        if l == 1:
            gl1 = gather_finish("gather_l1_finish", g_l1, xb)
            gw[1] = dict(zip(W_NAMES, gl1))
        sv["x0b"] = xb
        f1w13 = gw[l]["f1w13"][None]
        gate, up, act = ffn_up(f"ffn1_up_{l}", xb, f1w13, 0, dep)
        dep = ()
        if l == 0:
            g_mix, _ = gather_forward("gather_mix_forward", g_mix, act)
        z1, x1f, x1b = proj_res_ln(f"ffn1_down_{l}", [act], [gw[l]["f1w2"].reshape(1, D_FF, D_MODEL)], [0], xf,
                                   vec(lng[l, 0]), vec(lnb[l, 0]), 0.5)
        sv.update(gate1=gate, up1=up, act1=act, z1=z1, x1b=x1b)
        if l == 0:
            gw[0].update(zip(MIX_NAMES, gather_finish("gather_mix_finish", g_mix, x1b)))
            g_mid, _ = gather_forward("gather_mid_forward", g_mid, x1b)
        win = gw[l]["win"].reshape(D_MODEL, D_IN)
        win_ext = jnp.concatenate([win, _swap_half(win[:, D_IN - QK_ROPE:])], axis=-1)[None]
        wuq = _from_col_shards(gw[l]["wuq"]).reshape(Q_LORA, MLA_HEADS, QK_NOPE + QK_ROPE)
        wq_ext = jnp.concatenate([wuq, _swap_half(wuq[..., QK_NOPE:])], axis=-1).transpose(1, 0, 2)[None]
        wukv = _from_col_shards(gw[l]["wukv"])[None]
        wbd = _block_diag(pool_w[l][None].astype(bf16))[0]
        u, cq, ckv, cqn, ckvn, q, k, v = mix_pre(f"mix_pre_{l}", x1b, win_ext, wq_ext, wukv, 0,
                                                   vec(q_norm_g[l]), vec(kv_norm_g[l]), cs)
        dpool, ypool = pool_fwd(f"pool_fwd_{l}", u, wbd, vec(pool_scale[l]))
        o, lse = mla_attn_fwd(f"mla_fwd_{l}", q, k, v)
        if l == 0:
            gw[0].update(zip(MID_NAMES, gather_finish("gather_mid_finish", g_mid, o)))
            g_ffn2, tok_f = gather_forward("gather_ffn2_forward", g_ffn2, o)
            g_l1, tok_l = gather_forward("gather_l1_forward", g_l1, o)
            dep = (tok_f, tok_l)
        wout = gw[l]["wout"].reshape(D_MODEL, D_MODEL)
        wout_pool, wout_mla = wout[None, :POOL_WIDTH], wout[None, POOL_WIDTH:]
        mwq = gw[l]["mwq"].reshape(1, D_MODEL, D_MODEL)
        mwo = gw[l]["mwo"].reshape(1, D_MODEL, D_MODEL)
        mwkv = gw[l]["mwkv"][None]
        z2, x2f, x2b = proj_res_ln(f"mix_out_{l}", [ypool, o], [wout_pool, wout_mla], [0, 0], x1f,
                                   vec(lng[l, 1]), vec(lnb[l, 1]), 1.0, dep)
        dep = ()
        sv.update(cq=cq, ckv=ckv, cqn=cqn, ckvn=ckvn, q=q, k=k, v=v, dpool=dpool, ypool=ypool, o=o, lse=lse, z2=z2, x2b=x2b)
        kvm = mm_nn_shard(f"mem_kv_{l}", memb, mwkv, 0)
        cq_, co_, z3, x3f, x3b = cross_fwd(f"cross_fwd_{l}", x2b, x2f, mwq, mwo, 0, kvm, vec(lng[l, 2]), vec(lnb[l, 2]))
        sv.update(kvm=kvm, crq=cq_, cro=co_, z3=z3, x3b=x3b)
        if l == 0:
            gw[0].update(zip(FFN2_NAMES, gather_finish("gather_ffn2_finish", g_ffn2, x3b)))
        f2w13 = gw[l]["f2w13"][None]
        f2w2 = gw[l]["f2w2"].reshape(1, D_FF, D_MODEL)
        gate, up, act = ffn_up(f"ffn2_up_{l}", x3b, f2w13, 0)
        z4, xf, xb = proj_res_ln(f"ffn2_down_{l}", [act], [f2w2], [0], x3f, vec(lng[l, 3]), vec(lnb[l, 3]), 0.5)
        sv.update(gate2=gate, up2=up, act2=act, z4=z4)
        W[l] = dict(f1w13=f1w13, f1w2=gw[l]["f1w2"].reshape(1, D_FF, D_MODEL), win_ext=win_ext, wq_ext=wq_ext, wukv=wukv,
                    wbd=wbd, wout=wout[None], mwq=mwq, mwo=mwo, f2w13=f2w13, f2w2=f2w2)
        saved.append(sv)

    dln = {}
    dzb, dres, *dln[L - 1, 3], loss_blk = loss_grad("loss_grad", xf, loss_target[0],
                                                   (saved[L - 1]["z4"], vec(lng[L - 1, 3]), 0.5))
    loss = lax.psum(loss_blk[0, 0], ("x", "y", "c"))

    row_shards = lambda a: a.reshape(N_CHIPS, a.shape[0] // N_CHIPS, a.shape[1])
    small = {k_: [None] * L for k_ in ("pool_w", "pool_scale", "gq", "gkv", "lng", "lnb")}
    rest_names = [n for n in W_NAMES if n not in ("f1w13", "f1w2")]
    core = lax.axis_index("c")
    flags = [jnp.reshape(core == o, (1,)).astype(jnp.int32) for o in range(2)]
    qfs = [jnp.stack([chip, (core == o).astype(jnp.int32)]).astype(jnp.int32) for o in range(2)]

    def red_begin(tag, names, gs, layer):
        owners = [REDUCER[layer][n] for n in names]
        st, tok_ = pair_send_start(f"pair_send_start_{tag}", gs, owners, None)
        return (st, owners), tok_

    def red_mid(tag, sto, after):
        st, owners = sto
        gs_, lands_ = pair_send_wait(f"pair_send_wait_{tag}", st, after)
        ps = [pair_sum(f"pair_sum_{tag}_{a}", g_, r_, flags[o]) for a, (g_, r_, o) in enumerate(zip(gs_, lands_, owners))]
        st, tok_ = chip_exchange_start(f"chip_exchange_start_{tag}", ps, owners, None)
        return (st, owners), tok_

    def red_end(tag, sto, layer, prevs, after):
        st, owners = sto
        ps, lands_ = chip_exchange_wait(f"chip_exchange_wait_{tag}", st, after)
        return [chip_sum(f"chip_sum_{tag}_{a}", p_, r_, qfs[o], layer, s_)
                for a, (p_, r_, s_, o) in enumerate(zip(ps, lands_, prevs, owners))]

    def share_start(tag, names, sums_):
        return pair_share_start(f"pair_share_start_{tag}", sums_, [(REDUCER[0][n], REDUCER[1][n]) for n in names], None)

    st_p1 = st_c1 = st_pa = st_ca = None
    for l in reversed(range(L)):
        sv, w = saved[l], W[l]
        g = {}
        dh = ffn_bwd_da(f"ffn2_bwd_da_{l}", dzb, w["f2w2"], 0, sv["gate2"], sv["up2"], dep)
        dep = ()
        g["f2w2"] = row_shards(mm_tn(f"ffn2_dw2_{l}", sv["act2"], dzb))
        g["f2w13"] = mm_tn(f"ffn2_dw13_{l}", sv["x3b"], dh, True)
        dzb, dres, *dln[l, 2] = ffn_dx(f"ffn2_dx_{l}", dh, w["f2w13"], 0, dres, (sv["z3"], vec(lng[l, 2]), 1.0))
        if l == 0:
            st_c1, tok = red_mid("l1", st_p1, dzb)
            dep = (tok, g["f2w2"], g["f2w13"])
        dqc, dkvm = cross_bwd(f"cross_bwd_{l}", dzb, w["mwo"], 0, sv["crq"], sv["kvm"], dep)
        dep = ()
        g["mwo"] = row_shards(mm_tn(f"cross_dwo_{l}", sv["cro"], dzb))
        g["mwq"] = row_shards(mm_tn(f"cross_dwq_{l}", sv["x2b"], dqc))
        g["mwkv"] = mm_tn(f"cross_dwkv_{l}", memb, dkvm, True)
        dzb, dres, *dln[l, 1] = mm_nt_res(f"cross_dx_{l}", [dqc], [w["mwq"]], [0], dres, f32,
                                          (sv["z2"], vec(lng[l, 1]), 1.0))
        dcat = mm_nt_res(f"mix_dcat_{l}", [dzb], [w["wout"]], [0], None, bf16)
        dwo_p = mm_tn(f"mix_dwout_pool_{l}", sv["ypool"], dzb)
        dwo_m = mm_tn(f"mix_dwout_mla_{l}", sv["o"], dzb)
        g["wout"] = row_shards(jnp.concatenate([dwo_p, dwo_m], axis=0))
        dq, dk, dv = mla_attn_bwd(f"mla_bwd_{l}", sv["q"], sv["k"], sv["v"], sv["o"], dcat, sv["lse"], POOL_WIDTH // 128)
        dqe, dkv, dh_rest, dgq, dgkv = mix_post_bwd(f"mix_post_bwd_{l}", dq, dk, dv, w["wq_ext"], w["wukv"], 0, sv["cq"],
                                                     sv["ckv"], vec(q_norm_g[l]), vec(kv_norm_g[l]), cs)
        du, dyw, dscale = pool_bwd(f"pool_bwd_{l}", dcat, sv["dpool"], w["wbd"], vec(pool_scale[l]))
        dwq_e = mm_tn(f"mix_dwuq_{l}", sv["cqn"], dqe).reshape(Q_LORA, MLA_HEADS, 256)
        g["wuq"] = _to_col_shards(jnp.concatenate(
            [dwq_e[..., :QK_NOPE], _unswap_add(dwq_e[..., QK_NOPE:QK_NOPE + QK_ROPE], dwq_e[..., QK_NOPE + QK_ROPE:])],
            axis=-1).reshape(Q_LORA, MLA_HEADS * (QK_NOPE + QK_ROPE)))
        g["wukv"] = _to_col_shards(mm_tn(f"mix_dwukv_{l}", sv["ckvn"], dkv))
        dwbd = mm_tn(f"pool_dw_{l}", sv["dpool"], dyw)
        small["pool_w"][l] = jnp.stack([dwbd[64 * gi:64 * gi + 64, 64 * gi:64 * gi + 64] for gi in range(4)])
        small["pool_scale"][l], small["gq"][l], small["gkv"][l] = dscale[0], dgq[0], dgkv[0]
        dh_ext = jnp.concatenate([du, dh_rest], axis=1)
        dwin_e = mm_tn(f"mix_dwin_{l}", sv["x1b"], dh_ext)
        g["win"] = row_shards(jnp.concatenate(
            [dwin_e[:, :D_IN - QK_ROPE], _unswap_add(dwin_e[:, D_IN - QK_ROPE:D_IN], dwin_e[:, D_IN:])], axis=-1))
        dzb, dres, *dln[l, 0] = mm_nt_res(f"mix_dx_{l}", [dh_ext], [w["win_ext"]], [0], dres, f32,
                                          (sv["z1"], vec(lng[l, 0]), 0.5))
        if l == 0:
            st_pa, tok = red_begin("a0", rest_names, [g[n] for n in rest_names], 0)
            dep = (tok,)
        dh = ffn_bwd_da(f"ffn1_bwd_da_{l}", dzb, w["f1w2"], 0, sv["gate1"], sv["up1"], dep)
        dep = ()
        if l == 0:
            grad_x = ffn_dx(f"ffn1_dx_{l}", dh, w["f1w13"], 0, dres)[None]
            st_ca, tok = red_mid("a0", st_pa, grad_x)
            dep = (tok,)
        else:
            below = ffn_dx(f"ffn1_dx_{l}", dh, w["f1w13"], 0, dres, (saved[l - 1]["z4"], vec(lng[l - 1, 3]), 0.5))
            dln[l - 1, 3] = below[2:]
        g["f1w2"] = row_shards(mm_tn(f"ffn1_dw2_{l}", sv["act1"], dzb, False, dep))
        g["f1w13"] = mm_tn(f"ffn1_dw13_{l}", sv["x0b"], dh, True, dep)
        dep = ()
        if l > 0:
            dzb, dres = below[:2]
        if l == 1:
            st_p1, tok = red_begin("l1", W_NAMES, [g[n] for n in W_NAMES], 1)
            dep = (tok,)
    for l in range(L):
        small["lng"][l] = jnp.concatenate([dln[l, k][0] for k in range(4)], axis=0)
        small["lnb"][l] = jnp.concatenate([dln[l, k][1] for k in range(4)], axis=0)

    st_pb, _ = red_begin("b0", ("f1w13", "f1w2"), [g["f1w13"], g["f1w2"]], 0)
    sums1 = dict(zip(W_NAMES, red_end("l1", st_c1, 1, [None] * len(W_NAMES), g["f1w13"])))
    st_cb, tok = red_mid("b0", st_pb, tuple(sums1.values()))
    sums0 = red_end("a0", st_ca, 0, [sums1[n] for n in rest_names], tok)
    share_a, tok_a = share_start("a", rest_names, sums0)

    rep = [jnp.stack(small["pool_w"]).reshape(-1), jnp.stack(small["pool_scale"]).reshape(-1),
           jnp.stack(small["gq"]).reshape(-1), jnp.stack(small["gkv"]).reshape(-1),
           jnp.stack(small["lng"]).reshape(-1), jnp.stack(small["lnb"]).reshape(-1)]
    sizes = [r.shape[0] for r in rep]
    packed = jnp.concatenate(rep)
    pad = (-packed.shape[0]) % 1024
    tot = allsum_small("allsum_small_grads", jnp.pad(packed, (0, pad)).reshape(-1, 128), (tok_a,)).reshape(-1)
    offs = [0]
    for s_ in sizes:
        offs.append(offs[-1] + s_)
    parts = [tot[offs[i]:offs[i + 1]] for i in range(len(sizes))]
    g_pool_w = parts[0].reshape(pool_w.shape)
    g_pool_scale = parts[1].reshape(pool_scale.shape)
    g_gq = parts[2].reshape(q_norm_g.shape)
    g_gkv = parts[3].reshape(kv_norm_g.shape)
    shard_cols = lambda a: lax.dynamic_slice_in_dim(a.reshape(L, 4, D_MODEL), chip * (D_MODEL // N_CHIPS),
                                                    D_MODEL // N_CHIPS, axis=2)
    g_lng, g_lnb = shard_cols(parts[4]), shard_cols(parts[5])

    out_names = ("lng", "lnb", "f1w13", "f1w2", "win", "pool_w", "pool_scale", "gq", "wuq", "gkv", "wukv", "wout", "mwq",
                 "mwkv", "mwo", "f2w13", "f2w2")
    big = dict(lng=g_lng, lnb=g_lnb, pool_w=g_pool_w, pool_scale=g_pool_scale, gq=g_gq, gkv=g_gkv)
    late = ("f1w13", "f1w2")
    held = ("f2w13", "f2w2")
    ws = [ln_g, ln_b, ffn1_w13, ffn1_w2, w_in, pool_w, pool_scale, q_norm_g, w_uq, kv_norm_g, w_ukv, w_out, mem_wq,
          mem_wkv, mem_wo, ffn2_w13, ffn2_w2]
    ms = [m_ln_g, m_ln_b, m_ffn1_w13, m_ffn1_w2, m_w_in, m_pool_w, m_pool_scale, m_q_norm_g, m_w_uq, m_kv_norm_g, m_w_ukv,
          m_w_out, m_mem_wq, m_mem_wkv, m_mem_wo, m_ffn2_w13, m_ffn2_w2]
    vs = [v_ln_g, v_ln_b, v_ffn1_w13, v_ffn1_w2, v_w_in, v_pool_w, v_pool_scale, v_q_norm_g, v_w_uq, v_kv_norm_g, v_w_ukv,
          v_w_out, v_mem_wq, v_mem_wkv, v_mem_wo, v_ffn2_w13, v_ffn2_w2]
    res = {}

    def update(n, deps=()):
        a = out_names.index(n)
        res[a] = adamw(f"adamw_{a}", ws[a], big[n].reshape(ws[a].shape), ms[a], vs[a], deps)
        return res[a][0]

    small_done = tuple(update(n) for n in ("lng", "lnb", "pool_w", "pool_scale", "gq", "gkv"))
    big.update(zip(rest_names, pair_share_wait("pair_share_wait_a", share_a, small_done)))
    first_done = tuple(update(n) for n in rest_names if n not in held)
    sums_b = red_end("b0", st_cb, 0, [sums1[n] for n in late], first_done)
    share_b, tok_b = share_start("b", late, sums_b)
    held_done = tuple(update(n, (tok_b,)) for n in held)
    big.update(zip(late, pair_share_wait("pair_share_wait_b", share_b, held_done)))
    for n in late:
        update(n)
    order = range(len(out_names))
    grads = [big[n].reshape(w_.shape) for n, w_ in zip(out_names, ws)]
    return (loss, grad_x, *grads, *[res[a][0] for a in order], *[res[a][1] for a in order], *[res[a][2] for a in order])
```

```python
import functools
import math

import jax
import jax.numpy as jnp
from jax import lax
from jax.experimental import pallas as pl
from jax.experimental.pallas import tpu as pltpu

f32 = jnp.float32
bf16 = jnp.bfloat16
SDS = jax.ShapeDtypeStruct
MESH = pl.DeviceIdType.MESH

D_MODEL = 1024
DEPTH = 2
N_MEM = 256
MEM_HEADS = 4
MEM_HEAD_DIM = D_MODEL // MEM_HEADS
POOL_WINDOWS = (2, 4, 8, 16)
POOL_WIDTH = 256
POOL_GROUP = 64
QK_NOPE = 128
QK_ROPE = 64
V_HEAD = 128
MLA_HEADS = 6
Q_LORA = 256
KV_LORA = 128
ROPE_BASE = 10000.0
D_FF = 2816
D_IN = POOL_WIDTH + Q_LORA + KV_LORA + QK_ROPE
ALPHA = (2 * DEPTH) ** 0.25
LN_EPS = 1e-5
RMS_EPS = 1e-6
NEG_INF = -1e30
MLA_SCALE = (QK_NOPE + QK_ROPE) ** -0.5
MLA_SCALE_LOG2 = MLA_SCALE * math.log2(math.e)
MEM_SCALE = MEM_HEAD_DIM ** -0.5
ADAM_LR = 0.001
ADAM_B1 = 0.9
ADAM_B2 = 0.999
ADAM_EPS = 1e-08
ADAM_WD = 0.01
ADAM_STEP = 10

N_CHIPS = 4
V7X_VMEM_LIMIT = 56 * 2**20
HALO = 16

_NT = (((1,), (1,)), ((), ()))
_TN = (((0,), (0,)), ((), ()))


def _dot(a, b):
    return jnp.dot(a, b, preferred_element_type=f32)


def _dot_nt(a, b):
    return lax.dot_general(a, b, _NT, preferred_element_type=f32)


def _dot_tn(a, b):
    return lax.dot_general(a, b, _TN, preferred_element_type=f32)


def _cp(*sem):
    return pltpu.CompilerParams(dimension_semantics=sem if sem else None, vmem_limit_bytes=V7X_VMEM_LIMIT)


_DEP_SPEC = pl.BlockSpec(memory_space=pl.ANY)


def _with_deps(body, n_in, deps):
    nd = len(deps)
    if not nd:
        return body

    def wrapped(*refs):
        return body(*refs[:n_in], *refs[n_in + nd:])

    return wrapped


def _tile(n, t):
    t = min(n, t)
    assert n % t == 0, (n, t)
    return t


def _row_tile(rows, cols, itemsize=4, target=2 * 2**20):
    best = None
    for t in range(16, rows + 1, 16):
        if rows % t == 0 and t * cols * itemsize <= target:
            best = t
    return best if best is not None else rows


def ffn_up(name, xb, w13, l, deps=()):
    S = xb.shape[0]
    ns = w13.shape[3]
    tm = _tile(S, 512)

    def body(x_ref, wg_ref, wu_ref, g_ref, u_ref, a_ref):
        x = x_ref[...]
        g = _dot(x, wg_ref[0, 0])
        u = _dot(x, wu_ref[0, 0])
        a = g * jax.nn.sigmoid(g) * u
        g_ref[...] = g.astype(bf16)
        u_ref[...] = u.astype(bf16)
        a_ref[...] = a.astype(bf16)

    out = SDS((S, 2 * ns), bf16)
    return pl.pallas_call(
        _with_deps(body, 3, deps), name=name, grid=(2, S // tm),
        in_specs=[pl.BlockSpec((tm, D_MODEL), lambda j, i: (i, 0)),
                  pl.BlockSpec((1, 1, D_MODEL, ns), lambda j, i: (l, j, 0, 0)),
                  pl.BlockSpec((1, 1, D_MODEL, ns), lambda j, i: (l, j + 2, 0, 0))] + [_DEP_SPEC] * len(deps),
        out_specs=[pl.BlockSpec((tm, ns), lambda j, i: (i, j))] * 3,
        out_shape=[out, out, out],
        compiler_params=_cp("parallel", "parallel"),
    )(xb, w13, w13, *deps)


def proj_res_ln(name, parts, ws, wl, x, g, b, rscale, deps=()):
    S = x.shape[0]
    tm = _tile(S, 512)
    n = len(parts)

    def body(*refs):
        p_refs, w_refs = refs[:n], refs[n:2 * n]
        x_ref, g_ref, b_ref, z_ref, y_ref, yb_ref = refs[2 * n:]
        acc = _dot(p_refs[0][...], w_refs[0][0])
        for k in range(1, n):
            acc = acc + _dot(p_refs[k][...], w_refs[k][0])
        if rscale != 1.0:
            acc = rscale * acc
        z = ALPHA * x_ref[...] + acc
        mu = jnp.mean(z, axis=-1, keepdims=True)
        zc = z - mu
        var = jnp.mean(zc * zc, axis=-1, keepdims=True)
        y = zc * lax.rsqrt(var + LN_EPS) * g_ref[...] + b_ref[...]
        z_ref[...] = z
        y_ref[...] = y
        yb_ref[...] = y.astype(bf16)

    row = lambda i: (i, 0)
    in_specs = [pl.BlockSpec((tm, p.shape[1]), row) for p in parts]
    in_specs += [pl.BlockSpec((1,) + w.shape[1:], functools.partial(lambda li, i: (li, 0, 0), li)) for w, li in zip(ws, wl)]
    in_specs += [pl.BlockSpec((tm, D_MODEL), row), pl.BlockSpec((1, D_MODEL), lambda i: (0, 0)),
                 pl.BlockSpec((1, D_MODEL), lambda i: (0, 0))] + [_DEP_SPEC] * len(deps)
    return pl.pallas_call(
        _with_deps(body, 2 * n + 3, deps), name=name, grid=(S // tm,), in_specs=in_specs,
        out_specs=[pl.BlockSpec((tm, D_MODEL), row)] * 3,
        out_shape=[SDS((S, D_MODEL), f32), SDS((S, D_MODEL), f32), SDS((S, D_MODEL), bf16)],
        compiler_params=_cp("parallel"),
    )(*parts, *ws, x, g, b, *deps)


def _ln_bwd_store(dyv, z_ref, g_ref, rscale, first, dzb_ref, dres_ref, dg_ref, db_ref):
    z = z_ref[...]
    mu = jnp.mean(z, axis=-1, keepdims=True)
    zc = z - mu
    rstd = lax.rsqrt(jnp.mean(zc * zc, axis=-1, keepdims=True) + LN_EPS)
    xhat = zc * rstd
    dxh = dyv * g_ref[...]
    m1 = jnp.mean(dxh, axis=-1, keepdims=True)
    m2 = jnp.mean(dxh * xhat, axis=-1, keepdims=True)
    dz = rstd * (dxh - m1 - xhat * m2)
    dzb_ref[...] = (rscale * dz).astype(bf16)
    dres_ref[...] = ALPHA * dz

    @pl.when(first)
    def _():
        dg_ref[...] = jnp.zeros_like(dg_ref)
        db_ref[...] = jnp.zeros_like(db_ref)

    dg_ref[...] += jnp.sum(dyv * xhat, axis=0, keepdims=True)
    db_ref[...] += jnp.sum(dyv, axis=0, keepdims=True)


def _ln_bwd_specs(S, tm, index):
    vec = pl.BlockSpec((1, D_MODEL), lambda *a: (0, 0))
    blk = pl.BlockSpec((tm, D_MODEL), index)
    in_specs = [blk, vec]
    out_specs = [blk, blk, vec, vec]
    out_shape = [SDS((S, D_MODEL), bf16), SDS((S, D_MODEL), f32), SDS((1, D_MODEL), f32), SDS((1, D_MODEL), f32)]
    return in_specs, out_specs, out_shape


def ffn_bwd_da(name, drb, w2, l, gate, up, deps=()):
    S = drb.shape[0]
    tm = _tile(S, 512)
    nh = D_FF // 2

    def body(dr_ref, w_ref, g_ref, u_ref, dh_ref):
        dr = dr_ref[...]
        for j in range(2):
            cols = slice(j * nh, (j + 1) * nh)
            da = _dot_nt(dr, w_ref[0, cols, :])
            g = g_ref[:, cols].astype(f32)
            u = u_ref[:, cols].astype(f32)
            sg = jax.nn.sigmoid(g)
            dh_ref[:, cols] = (da * u * (sg * (1.0 + g * (1.0 - sg)))).astype(bf16)
            dh_ref[:, D_FF + j * nh:D_FF + (j + 1) * nh] = (da * (g * sg)).astype(bf16)

    row = lambda i: (i, 0)
    return pl.pallas_call(
        _with_deps(body, 4, deps), name=name, grid=(S // tm,),
        in_specs=[pl.BlockSpec((tm, D_MODEL), row), pl.BlockSpec((1, D_FF, D_MODEL), lambda i: (l, 0, 0)),
                  pl.BlockSpec((tm, D_FF), row), pl.BlockSpec((tm, D_FF), row)] + [_DEP_SPEC] * len(deps),
        out_specs=pl.BlockSpec((tm, 2 * D_FF), row),
        out_shape=SDS((S, 2 * D_FF), bf16),
        compiler_params=_cp("parallel"),
    )(drb, w2, gate, up, *deps)


def ffn_dx(name, dh, w13, l, res, ln=None):
    S = dh.shape[0]
    ns = w13.shape[3]
    tm = _tile(S, 1024)
    last = N_CHIPS - 1
    row = lambda i, j: (i, 0)
    in_specs = [pl.BlockSpec((tm, ns), lambda i, j: (i, j)),
                pl.BlockSpec((1, 1, D_MODEL, ns), lambda i, j: (l, j, 0, 0)),
                pl.BlockSpec((tm, D_MODEL), row)]
    if ln is None:
        def body(dh_ref, w_ref, r_ref, o_ref):
            @pl.when(pl.program_id(1) == 0)
            def _():
                o_ref[...] = r_ref[...]

            o_ref[...] += _dot_nt(dh_ref[...], w_ref[0, 0])

        return pl.pallas_call(
            body, name=name, grid=(S // tm, N_CHIPS), in_specs=in_specs,
            out_specs=pl.BlockSpec((tm, D_MODEL), row), out_shape=SDS((S, D_MODEL), f32),
            compiler_params=_cp("parallel", "arbitrary"),
        )(dh, w13, res)

    z, g, rscale = ln

    def body_ln(dh_ref, w_ref, r_ref, z_ref, g_ref, dzb_ref, dres_ref, dg_ref, db_ref, acc_sc):
        i, j = pl.program_id(0), pl.program_id(1)

        @pl.when(j == 0)
        def _():
            acc_sc[...] = r_ref[...]

        acc_sc[...] += _dot_nt(dh_ref[...], w_ref[0, 0])

        @pl.when(j == last)
        def _():
            _ln_bwd_store(acc_sc[...], z_ref, g_ref, rscale, i == 0, dzb_ref, dres_ref, dg_ref, db_ref)

    ln_in, ln_out, ln_shape = _ln_bwd_specs(S, tm, row)
    return pl.pallas_call(
        body_ln, name=name, grid=(S // tm, N_CHIPS), in_specs=in_specs + ln_in, out_specs=ln_out, out_shape=ln_shape,
        scratch_shapes=[pltpu.VMEM((tm, D_MODEL), f32)], compiler_params=_cp("arbitrary", "arbitrary"),
    )(dh, w13, res, z, g)


def mm_nt_res(name, dys, ws, wl, res, out_dtype, ln=None):
    S = dys[0].shape[0]
    K = ws[0].shape[1]
    tm = _tile(S, 512)
    n = len(dys)
    n_in = 2 * n + (res is not None)

    def product(refs):
        acc = _dot_nt(refs[0][...], refs[n][0])
        for k in range(1, n):
            acc = acc + _dot_nt(refs[k][...], refs[n + k][0])
        if res is not None:
            acc = acc + refs[2 * n][...]
        return acc

    def body(*refs):
        refs[-1][...] = product(refs).astype(out_dtype)

    def body_ln(*refs):
        z_ref, g_ref, dzb_ref, dres_ref, dg_ref, db_ref = refs[n_in:]
        _ln_bwd_store(product(refs), z_ref, g_ref, ln[2], pl.program_id(0) == 0, dzb_ref, dres_ref, dg_ref, db_ref)

    row = lambda i: (i, 0)
    in_specs = [pl.BlockSpec((tm, d.shape[1]), row) for d in dys]
    in_specs += [pl.BlockSpec((1,) + w.shape[1:], functools.partial(lambda li, i: (li, 0, 0), li)) for w, li in zip(ws, wl)]
    args = list(dys) + list(ws)
    if res is not None:
        in_specs.append(pl.BlockSpec((tm, K), row))
        args.append(res)
    if ln is None:
        return pl.pallas_call(
            body, name=name, grid=(S // tm,), in_specs=in_specs,
            out_specs=pl.BlockSpec((tm, K), row), out_shape=SDS((S, K), out_dtype),
            compiler_params=_cp("parallel"),
        )(*args)
    ln_in, ln_out, ln_shape = _ln_bwd_specs(S, tm, row)
    return pl.pallas_call(
        body_ln, name=name, grid=(S // tm,), in_specs=in_specs + ln_in, out_specs=ln_out, out_shape=ln_shape,
        compiler_params=_cp("arbitrary"),
    )(*args, ln[0], ln[1])


def mm_tn(name, x, dy, col_shards=False, deps=()):
    S, K = x.shape
    N = dy.shape[1]
    ts = 512
    while ts * 2 <= min(S, 2048) and S % (ts * 2) == 0 and ts * 2 * K * 2 <= 6 * 2**20:
        ts *= 2
    ts = _tile(S, ts)
    if col_shards:
        tn = N // N_CHIPS
    else:
        tn = N
        while K * tn * 4 > 6 * 2**20 and tn % 256 == 0:
            tn //= 2
    nn = N // tn
    lead = ((0,) if col_shards else ()) + (slice(None), slice(None))

    def body(x_ref, dy_ref, o_ref):
        acc = _dot_tn(x_ref[...].astype(bf16), dy_ref[...].astype(bf16))

        @pl.when(pl.program_id(1) == 0)
        def _():
            o_ref[lead] = acc

        @pl.when(pl.program_id(1) != 0)
        def _():
            o_ref[lead] += acc

    if col_shards:
        out_spec = pl.BlockSpec((1, K, tn), lambda n, s: (n, 0, 0))
        out_shape = SDS((N_CHIPS, K, tn), f32)
    else:
        out_spec = pl.BlockSpec((K, tn), lambda n, s: (0, n))
        out_shape = SDS((K, N), f32)
    return pl.pallas_call(
        _with_deps(body, 2, deps), name=name, grid=(nn, S // ts),
        in_specs=[pl.BlockSpec((ts, K), lambda n, s: (s, 0)), pl.BlockSpec((ts, tn), lambda n, s: (s, n))]
        + [_DEP_SPEC] * len(deps),
        out_specs=out_spec, out_shape=out_shape, compiler_params=_cp("parallel", "arbitrary"),
    )(x, dy, *deps)


def mm_nn_shard(name, x, w, l):
    S, K = x.shape
    ns = w.shape[3]

    def body(x_ref, w_ref, o_ref):
        o_ref[...] = _dot(x_ref[...], w_ref[0, 0]).astype(bf16)

    return pl.pallas_call(
        body, name=name, grid=(N_CHIPS,),
        in_specs=[pl.BlockSpec((S, K), lambda j: (0, 0)), pl.BlockSpec((1, 1, K, ns), lambda j: (l, j, 0, 0))],
        out_specs=pl.BlockSpec((S, ns), lambda j: (0, j)), out_shape=SDS((S, N_CHIPS * ns), bf16),
        compiler_params=_cp("parallel"),
    )(x, w)


def loss_grad(name, y, t, ln):
    S = y.shape[0]
    tm = _tile(S, 512)
    z, g, rscale = ln

    def body(y_ref, t_ref, z_ref, g_ref, dzb_ref, dres_ref, dg_ref, db_ref, loss_ref):
        first = pl.program_id(0) == 0
        e = y_ref[...] - t_ref[...]
        _ln_bwd_store(e * (1.0 / D_MODEL), z_ref, g_ref, rscale, first, dzb_ref, dres_ref, dg_ref, db_ref)

        @pl.when(first)
        def _():
            loss_ref[...] = jnp.zeros_like(loss_ref)

        loss_ref[...] += jnp.full(loss_ref.shape, (0.5 / D_MODEL) * jnp.sum(e * e), f32)

    row = lambda i: (i, 0)
    ln_in, ln_out, ln_shape = _ln_bwd_specs(S, tm, row)
    return pl.pallas_call(
        body, name=name, grid=(S // tm,),
        in_specs=[pl.BlockSpec((tm, D_MODEL), row)] * 2 + ln_in,
        out_specs=ln_out + [pl.BlockSpec((8, 128), lambda i: (0, 0))],
        out_shape=ln_shape + [SDS((8, 128), f32)],
        compiler_params=_cp("arbitrary"),
    )(y, t, z, g)


def _half_sum(t):
    return t + pltpu.roll(t, 64, axis=1)


def mix_pre(name, xb, w_in, wq, wkv, l, gq, gkv, cs):
    S = xb.shape[0]
    tm = _tile(S, 512)
    H = MLA_HEADS
    W_EXT = w_in.shape[2]

    def body(x_ref, win_ref, wq_ref, wkv_ref, gq_ref, gkv_ref, cs_ref,
             u_ref, cq_ref, ckv_ref, cqn_ref, ckvn_ref, q_ref, k_ref, v_ref):
        h = _dot(x_ref[...], win_ref[0])
        u_ref[...] = h[:, :256]
        cq = h[:, 256:512]
        ckv = h[:, 512:640]
        cq_ref[...] = cq
        ckv_ref[...] = ckv
        cqn = (cq * lax.rsqrt(jnp.mean(cq * cq, axis=-1, keepdims=True) + RMS_EPS) * gq_ref[...]).astype(bf16)
        ckvn = (ckv * lax.rsqrt(jnp.mean(ckv * ckv, axis=-1, keepdims=True) + RMS_EPS) * gkv_ref[...]).astype(bf16)
        cqn_ref[...] = cqn
        ckvn_ref[...] = ckvn
        csv = cs_ref[...]
        lane = lax.broadcasted_iota(jnp.int32, (tm, 128), 1)
        kr = jnp.where(lane < 64, _half_sum(h[:, 640:768] * csv), 0.0).astype(bf16)
        kv = _dot(ckvn, wkv_ref[0])
        for hd in range(H):
            qe = _dot(cqn, wq_ref[0, hd])
            q_ref[hd, :, :128] = (qe[:, :128] * MLA_SCALE_LOG2).astype(bf16)
            q_ref[hd, :, 128:] = (_half_sum(qe[:, 128:] * csv) * MLA_SCALE_LOG2).astype(bf16)
            k_ref[hd, :, :128] = kv[:, 256 * hd:256 * hd + 128].astype(bf16)
            k_ref[hd, :, 128:] = kr
            v_ref[hd] = kv[:, 256 * hd + 128:256 * hd + 256].astype(bf16)

    row = lambda i: (i, 0)
    hrow = lambda i: (0, i, 0)
    return pl.pallas_call(
        body, name=name, grid=(S // tm,),
        in_specs=[pl.BlockSpec((tm, D_MODEL), row),
                  pl.BlockSpec((1, D_MODEL, W_EXT), lambda i: (l, 0, 0)),
                  pl.BlockSpec((1, H, Q_LORA, 256), lambda i: (l, 0, 0, 0)),
                  pl.BlockSpec((1, KV_LORA, H * 256), lambda i: (l, 0, 0)),
                  pl.BlockSpec((1, Q_LORA), lambda i: (0, 0)), pl.BlockSpec((1, KV_LORA), lambda i: (0, 0)),
                  pl.BlockSpec((tm, 128), row)],
        out_specs=[pl.BlockSpec((tm, 256), row), pl.BlockSpec((tm, Q_LORA), row), pl.BlockSpec((tm, KV_LORA), row),
                   pl.BlockSpec((tm, Q_LORA), row), pl.BlockSpec((tm, KV_LORA), row),
                   pl.BlockSpec((H, tm, 256), hrow), pl.BlockSpec((H, tm, 256), hrow), pl.BlockSpec((H, tm, 128), hrow)],
        out_shape=[SDS((S, 256), f32), SDS((S, Q_LORA), f32), SDS((S, KV_LORA), f32),
                   SDS((S, Q_LORA), bf16), SDS((S, KV_LORA), bf16),
                   SDS((H, S, 256), bf16), SDS((H, S, 256), bf16), SDS((H, S, 128), bf16)],
        compiler_params=_cp("parallel"),
    )(xb, w_in, wq, wkv, gq, gkv, cs)


def _group_select(col, a2, a4, a8, a16):
    return jnp.where(col < 64, a2, jnp.where(col < 128, a4, jnp.where(col < 192, a8, a16)))


def pool_fwd(name, u, wbd, scale):
    S = u.shape[0]
    tm = _tile(S, 512)
    hb = tm // HALO

    def body(u_ref, halo_ref, w_ref, s_ref, d_ref, y_ref):
        i = pl.program_id(0)
        cur = u_ref[...]
        halo = jnp.where(i > 0, halo_ref[...], 0.0)
        ext = jnp.concatenate([halo, cur], axis=0)
        s2 = ext + pltpu.roll(ext, 1, axis=0)
        s4 = s2 + pltpu.roll(s2, 2, axis=0)
        s8 = s4 + pltpu.roll(s4, 4, axis=0)
        s16 = s8 + pltpu.roll(s8, 8, axis=0)
        t1 = (i * tm + 1 + lax.broadcasted_iota(jnp.int32, (tm, 1), 0)).astype(f32)
        col = lax.broadcasted_iota(jnp.int32, (tm, 256), 1)
        m = _group_select(col, s2[HALO:] / jnp.minimum(t1, 2.0), s4[HALO:] / jnp.minimum(t1, 4.0),
                          s8[HALO:] / jnp.minimum(t1, 8.0), s16[HALO:] / jnp.minimum(t1, 16.0))
        d = (m - cur).astype(bf16)
        d_ref[...] = d
        y_ref[...] = (_dot(d, w_ref[...]) * s_ref[...]).astype(bf16)

    row = lambda i: (i, 0)
    return pl.pallas_call(
        body, name=name, grid=(S // tm,),
        in_specs=[pl.BlockSpec((tm, 256), row), pl.BlockSpec((HALO, 256), lambda i: (jnp.maximum(i * hb - 1, 0), 0)),
                  pl.BlockSpec((256, 256), lambda i: (0, 0)), pl.BlockSpec((1, 256), lambda i: (0, 0))],
        out_specs=[pl.BlockSpec((tm, 256), row)] * 2,
        out_shape=[SDS((S, 256), bf16), SDS((S, 256), bf16)],
        compiler_params=_cp("parallel"),
    )(u, u, wbd, scale)


def pool_bwd(name, dyp, d, wbd, scale):
    S = dyp.shape[0]
    tm = _tile(S, 512)
    hb = tm // HALO
    n_ext = tm + HALO

    def fwd_sum(e, steps):
        k = 1
        for _ in range(steps):
            e = e + pltpu.roll(e, n_ext - k, axis=0)
            k *= 2
        return e

    def body(dy_ref, halo_ref, d_ref, w_ref, s_ref, du_ref, dyw_ref, ds_ref):
        i = pl.program_id(0)
        sc = s_ref[...]
        w = w_ref[...]
        cur = dy_ref[...].astype(f32)
        halo = jnp.where(i < pl.num_programs(0) - 1, halo_ref[...].astype(f32), 0.0)
        dyw = jnp.concatenate([cur, halo], axis=0) * sc
        dyw_ref[...] = dyw[:tm].astype(bf16)
        dd = _dot_nt(dyw.astype(bf16), w)
        t1 = (i * tm + 1 + lax.broadcasted_iota(jnp.int32, (n_ext, 1), 0)).astype(f32)
        f2 = fwd_sum(dd / jnp.minimum(t1, 2.0), 1)
        f4 = fwd_sum(dd / jnp.minimum(t1, 4.0), 2)
        f8 = fwd_sum(dd / jnp.minimum(t1, 8.0), 3)
        f16 = fwd_sum(dd / jnp.minimum(t1, 16.0), 4)
        col = lax.broadcasted_iota(jnp.int32, (tm, 256), 1)
        du_ref[...] = (_group_select(col, f2[:tm], f4[:tm], f8[:tm], f16[:tm]) - dd[:tm]).astype(bf16)

        @pl.when(i == 0)
        def _():
            ds_ref[...] = jnp.zeros_like(ds_ref)

        ds_ref[...] += jnp.sum(cur * _dot(d_ref[...], w), axis=0, keepdims=True)

    row = lambda i: (i, 0)
    nhb = S // HALO
    return pl.pallas_call(
        body, name=name, grid=(S // tm,),
        in_specs=[pl.BlockSpec((tm, 256), row), pl.BlockSpec((HALO, 256), lambda i: (jnp.minimum((i + 1) * hb, nhb - 1), 0)),
                  pl.BlockSpec((tm, 256), row), pl.BlockSpec((256, 256), lambda i: (0, 0)),
                  pl.BlockSpec((1, 256), lambda i: (0, 0))],
        out_specs=[pl.BlockSpec((tm, 256), row), pl.BlockSpec((tm, 256), row), pl.BlockSpec((1, 256), lambda i: (0, 0))],
        out_shape=[SDS((S, 256), bf16), SDS((S, 256), bf16), SDS((1, 256), f32)],
        compiler_params=_cp("arbitrary"),
    )(dyp, dyp, d, wbd, scale)


def _diag_mask(r0, rn, kn):
    rc = (r0 + lax.broadcasted_iota(jnp.int32, (rn, 1), 0)) // 64
    cc = lax.broadcasted_iota(jnp.int32, (1, kn), 1) // 64
    return rc >= cc


def _diag_parts(tq):
    h = tq // 2
    return [(0, h, h), (h, h, tq)] if h % 128 == 0 else [(0, tq, tq)]


def mla_attn_fwd(name, q, k, v):
    H, S, _ = q.shape
    tq = _tile(S, 1024)
    nq = S // tq
    pairs = [(i, j) for i in range(nq) for j in range(i + 1)]
    it = jnp.asarray([p_[0] for p_ in pairs], jnp.int32)
    jt = jnp.asarray([p_[1] for p_ in pairs], jnp.int32)

    def body(it_ref, jt_ref, q_ref, k_ref, v_ref, o_ref, lse_ref, m_sc, l_sc, acc_sc):
        t = pl.program_id(1)
        i, j = it_ref[t], jt_ref[t]

        @pl.when(j == 0)
        def _():
            m_sc[...] = jnp.full_like(m_sc, NEG_INF)
            l_sc[...] = jnp.zeros_like(l_sc)
            acc_sc[...] = jnp.zeros_like(acc_sc)

        def part(r0, rn, kn, masked):
            rows, keys = slice(r0, r0 + rn), slice(0, kn)
            s = _dot_nt(q_ref[0, rows, :], k_ref[0, keys, :])
            if masked:
                s = jnp.where(_diag_mask(r0, rn, kn), s, NEG_INF)
            m_prev = m_sc[rows, :]
            m_new = jnp.maximum(m_prev, jnp.max(s, axis=-1, keepdims=True))
            p = jnp.exp2(s - jnp.tile(m_new, (1, kn // 128)))
            a = jnp.exp2(m_prev - m_new)
            l_sc[rows, :] = a * l_sc[rows, :] + jnp.sum(p, axis=-1, keepdims=True)
            acc_sc[rows, :] = a * acc_sc[rows, :] + _dot(p.astype(bf16), v_ref[0, keys, :])
            m_sc[rows, :] = m_new

        @pl.when(j < i)
        def _():
            part(0, tq, tq, False)

        @pl.when(j == i)
        def _():
            for r0, rn, kn in _diag_parts(tq):
                part(r0, rn, kn, True)
            o_ref[...] = (acc_sc[...] / l_sc[...]).astype(bf16)
            lse_ref[0] = m_sc[...] + jnp.log2(l_sc[...])

    return pl.pallas_call(
        body, name=name,
        grid_spec=pltpu.PrefetchScalarGridSpec(
            num_scalar_prefetch=2, grid=(H, len(pairs)),
            in_specs=[pl.BlockSpec((1, tq, 256), lambda h, t, it_, jt_: (h, it_[t], 0)),
                      pl.BlockSpec((1, tq, 256), lambda h, t, it_, jt_: (h, jt_[t], 0)),
                      pl.BlockSpec((1, tq, 128), lambda h, t, it_, jt_: (h, jt_[t], 0))],
            out_specs=[pl.BlockSpec((tq, 128), lambda h, t, it_, jt_: (it_[t], h)),
                       pl.BlockSpec((1, tq, 128), lambda h, t, it_, jt_: (h, it_[t], 0))],
            scratch_shapes=[pltpu.VMEM((tq, 128), f32), pltpu.VMEM((tq, 128), f32), pltpu.VMEM((tq, 128), f32)]),
        out_shape=[SDS((S, H * 128), bf16), SDS((H, S, 128), f32)],
        compiler_params=_cp("parallel", "arbitrary"),
    )(it, jt, q, k, v)


def mla_attn_bwd(name, q, k, v, o, do, lse, do_col=0):
    H, S, _ = q.shape
    tq = _tile(S, 1024)
    nq = S // tq
    pairs = [(i, j) for j in range(nq) for i in range(j, nq)]
    it = jnp.asarray([p_[0] for p_ in pairs], jnp.int32)
    jt = jnp.asarray([p_[1] for p_ in pairs], jnp.int32)
    n_pairs = len(pairs)

    def body(it_ref, jt_ref, q_ref, k_ref, v_ref, o_ref, do_ref, lse_ref, dq_ref, dk_ref, dv_ref, dq_sc, dk_sc, dv_sc):
        t = pl.program_id(1)
        i, j = it_ref[t], jt_ref[t]

        @pl.when(t == 0)
        def _():
            dq_sc[...] = jnp.zeros_like(dq_sc)

        @pl.when(i == j)
        def _():
            dk_sc[...] = jnp.zeros_like(dk_sc)
            dv_sc[...] = jnp.zeros_like(dv_sc)

        def part(r0, rn, kn, masked):
            rows, keys = slice(r0, r0 + rn), slice(0, kn)
            qv, kv_, dov = q_ref[0, rows, :], k_ref[0, keys, :], do_ref[rows, :]
            s = _dot_nt(qv, kv_)
            if masked:
                s = jnp.where(_diag_mask(r0, rn, kn), s, NEG_INF)
            p = jnp.exp2(s - jnp.tile(lse_ref[0, rows, :], (1, kn // 128)))
            dv_sc[keys, :] += _dot_tn(p.astype(bf16), dov)
            dp = _dot_nt(dov, v_ref[0, keys, :])
            delta = jnp.sum(dov.astype(f32) * o_ref[rows, :].astype(f32), axis=-1, keepdims=True)
            ds = (p * (dp - delta)).astype(bf16)
            dk_sc[keys, :] += _dot_tn(ds, qv)
            dq_rows = pl.ds(pl.multiple_of(i * tq + r0, 128), rn)
            dq_sc[dq_rows, :] += _dot(ds, kv_)

        @pl.when(i > j)
        def _():
            part(0, tq, tq, False)

        @pl.when(i == j)
        def _():
            for r0, rn, kn in _diag_parts(tq):
                part(r0, rn, kn, True)

        @pl.when(i == nq - 1)
        def _():
            dk_ref[0] = (dk_sc[...] * math.log(2.0)).astype(bf16)
            dv_ref[0] = dv_sc[...].astype(bf16)

        @pl.when(t == n_pairs - 1)
        def _():
            dq_ref[0] = (dq_sc[...] * MLA_SCALE).astype(bf16)

    qi = lambda h, t, it_, jt_: (h, it_[t], 0)
    kj = lambda h, t, it_, jt_: (h, jt_[t], 0)
    oi = lambda h, t, it_, jt_: (it_[t], h)
    doi = lambda h, t, it_, jt_: (it_[t], h + do_col)
    return pl.pallas_call(
        body, name=name,
        grid_spec=pltpu.PrefetchScalarGridSpec(
            num_scalar_prefetch=2, grid=(H, n_pairs),
            in_specs=[pl.BlockSpec((1, tq, 256), qi), pl.BlockSpec((1, tq, 256), kj), pl.BlockSpec((1, tq, 128), kj),
                      pl.BlockSpec((tq, 128), oi), pl.BlockSpec((tq, 128), doi), pl.BlockSpec((1, tq, 128), qi)],
            out_specs=[pl.BlockSpec((1, S, 256), lambda h, t, it_, jt_: (h, 0, 0)), pl.BlockSpec((1, tq, 256), kj),
                       pl.BlockSpec((1, tq, 128), kj)],
            scratch_shapes=[pltpu.VMEM((S, 256), f32), pltpu.VMEM((tq, 256), f32), pltpu.VMEM((tq, 128), f32)]),
        out_shape=[SDS((H, S, 256), bf16), SDS((H, S, 256), bf16), SDS((H, S, 128), bf16)],
        compiler_params=_cp("parallel", "arbitrary"),
    )(it, jt, q, k, v, o, do, lse)


def mix_post_bwd(name, dq, dk, dv, wq, wkv, l, cq, ckv, gq, gkv, cs):
    H, S, _ = dq.shape
    tm = _tile(S, 512)

    def rms_bwd(dyn, c, g):
        r = lax.rsqrt(jnp.mean(c * c, axis=-1, keepdims=True) + RMS_EPS)
        ch = c * r
        dyg = dyn * g
        dc = r * (dyg - ch * jnp.mean(dyg * ch, axis=-1, keepdims=True))
        return dc, jnp.sum(dyn * ch, axis=0, keepdims=True)

    def body(dq_ref, dk_ref, dv_ref, wq_ref, wkv_ref, cq_ref, ckv_ref, gq_ref, gkv_ref, cs_ref,
             dqe_ref, dkv_ref, dh_ref, dgq_ref, dgkv_ref):
        csv = cs_ref[...]
        lane = lax.broadcasted_iota(jnp.int32, (tm, 128), 1)
        dcqn = jnp.zeros((tm, Q_LORA), f32)
        dkr = jnp.zeros((tm, 128), f32)
        for hd in range(H):
            dqh = dq_ref[hd].astype(f32)
            dqe = jnp.concatenate([dqh[:, :128], _half_sum(dqh[:, 128:]) * csv], axis=1).astype(bf16)
            dqe_ref[:, 256 * hd:256 * hd + 256] = dqe
            dcqn = dcqn + _dot_nt(dqe, wq_ref[0, hd])
            dkh = dk_ref[hd].astype(f32)
            dkv_ref[:, 256 * hd:256 * hd + 128] = dkh[:, :128].astype(bf16)
            dkv_ref[:, 256 * hd + 128:256 * hd + 256] = dv_ref[hd].astype(bf16)
            dkr = dkr + dkh[:, 128:]
        dckvn = _dot_nt(dkv_ref[...], wkv_ref[0])
        dblk = _half_sum(jnp.where(lane < 64, dkr, 0.0)) * csv
        dcq, dgq = rms_bwd(dcqn, cq_ref[...], gq_ref[...])
        dckv, dgkv = rms_bwd(dckvn, ckv_ref[...], gkv_ref[...])
        dh_ref[:, :256] = dcq.astype(bf16)
        dh_ref[:, 256:384] = dckv.astype(bf16)
        dh_ref[:, 384:] = dblk.astype(bf16)

        @pl.when(pl.program_id(0) == 0)
        def _():
            dgq_ref[...] = jnp.zeros_like(dgq_ref)
            dgkv_ref[...] = jnp.zeros_like(dgkv_ref)

        dgq_ref[...] += dgq
        dgkv_ref[...] += dgkv

    row = lambda i: (i, 0)
    hrow = lambda i: (0, i, 0)
    return pl.pallas_call(
        body, name=name, grid=(S // tm,),
        in_specs=[pl.BlockSpec((H, tm, 256), hrow), pl.BlockSpec((H, tm, 256), hrow), pl.BlockSpec((H, tm, 128), hrow),
                  pl.BlockSpec((1, H, Q_LORA, 256), lambda i: (l, 0, 0, 0)),
                  pl.BlockSpec((1, KV_LORA, H * 256), lambda i: (l, 0, 0)),
                  pl.BlockSpec((tm, Q_LORA), row), pl.BlockSpec((tm, KV_LORA), row),
                  pl.BlockSpec((1, Q_LORA), lambda i: (0, 0)), pl.BlockSpec((1, KV_LORA), lambda i: (0, 0)),
                  pl.BlockSpec((tm, 128), row)],
        out_specs=[pl.BlockSpec((tm, H * 256), row), pl.BlockSpec((tm, H * 256), row), pl.BlockSpec((tm, 512), row),
                   pl.BlockSpec((1, Q_LORA), lambda i: (0, 0)), pl.BlockSpec((1, KV_LORA), lambda i: (0, 0))],
        out_shape=[SDS((S, H * 256), bf16), SDS((S, H * 256), bf16), SDS((S, 512), bf16),
                   SDS((1, Q_LORA), f32), SDS((1, KV_LORA), f32)],
        compiler_params=_cp("arbitrary"),
    )(dq, dk, dv, wq, wkv, cq, ckv, gq, gkv, cs)


def _cross_probs(qb, kv_ref, hd):
    cols = slice(hd * MEM_HEAD_DIM, (hd + 1) * MEM_HEAD_DIM)
    s = _dot_nt(qb[:, cols], kv_ref[:, cols]) * MEM_SCALE
    e = jnp.exp(s - jnp.max(s, axis=-1, keepdims=True))
    return e / jnp.sum(e, axis=-1, keepdims=True)


def cross_fwd(name, xb, xf, wq, wo, l, kv, g, b):
    S = xb.shape[0]
    tm = _tile(S, 512)
    M = kv.shape[0]

    def body(x_ref, xf_ref, wq_ref, wo_ref, k_ref, v_ref, g_ref, b_ref, q_ref, o_ref, z_ref, y_ref, yb_ref):
        qb = _dot(x_ref[...], wq_ref[0]).astype(bf16)
        q_ref[...] = qb
        for hd in range(MEM_HEADS):
            cols = slice(hd * MEM_HEAD_DIM, (hd + 1) * MEM_HEAD_DIM)
            p = _cross_probs(qb, k_ref, hd)
            o_ref[:, cols] = _dot(p.astype(bf16), v_ref[:, cols]).astype(bf16)
        z = ALPHA * xf_ref[...] + _dot(o_ref[...], wo_ref[0])
        mu = jnp.mean(z, axis=-1, keepdims=True)
        zc = z - mu
        var = jnp.mean(zc * zc, axis=-1, keepdims=True)
        y = zc * lax.rsqrt(var + LN_EPS) * g_ref[...] + b_ref[...]
        z_ref[...] = z
        y_ref[...] = y
        yb_ref[...] = y.astype(bf16)

    row = lambda i: (i, 0)
    wspec = pl.BlockSpec((1, D_MODEL, D_MODEL), lambda i: (l, 0, 0))
    vec = pl.BlockSpec((1, D_MODEL), lambda i: (0, 0))
    blk = pl.BlockSpec((tm, D_MODEL), row)
    return pl.pallas_call(
        body, name=name, grid=(S // tm,),
        in_specs=[blk, blk, wspec, wspec, pl.BlockSpec((M, D_MODEL), lambda i: (0, 0)),
                  pl.BlockSpec((M, D_MODEL), lambda i: (0, 1)), vec, vec],
        out_specs=[blk] * 5,
        out_shape=[SDS((S, D_MODEL), bf16), SDS((S, D_MODEL), bf16), SDS((S, D_MODEL), f32), SDS((S, D_MODEL), f32),
                   SDS((S, D_MODEL), bf16)],
        compiler_params=_cp("parallel"),
    )(xb, xf, wq, wo, kv, kv, g, b)


def cross_bwd(name, dzb, wo, l, qb, kv, deps=()):
    S = dzb.shape[0]
    tm = _tile(S, 512)
    M = kv.shape[0]

    def body(dz_ref, wo_ref, q_ref, k_ref, v_ref, dq_ref, dkv_ref):
        @pl.when(pl.program_id(0) == 0)
        def _():
            dkv_ref[...] = jnp.zeros_like(dkv_ref)

        do = _dot_nt(dz_ref[...], wo_ref[0]).astype(bf16)
        qv = q_ref[...]
        for hd in range(MEM_HEADS):
            cols = slice(hd * MEM_HEAD_DIM, (hd + 1) * MEM_HEAD_DIM)
            vcols = slice(D_MODEL + hd * MEM_HEAD_DIM, D_MODEL + (hd + 1) * MEM_HEAD_DIM)
            p = _cross_probs(qv, k_ref, hd)
            doh = do[:, cols]
            dkv_ref[:, vcols] += _dot_tn(p.astype(bf16), doh)
            dp = _dot_nt(doh, v_ref[:, cols])
            ds = (p * (dp - jnp.sum(dp * p, axis=-1, keepdims=True)) * MEM_SCALE).astype(bf16)
            dq_ref[:, cols] = _dot(ds, k_ref[:, cols]).astype(bf16)
            dkv_ref[:, cols] += _dot_tn(ds, qv[:, cols])

    row = lambda i: (i, 0)
    blk = pl.BlockSpec((tm, D_MODEL), row)
    return pl.pallas_call(
        _with_deps(body, 5, deps), name=name, grid=(S // tm,),
        in_specs=[blk, pl.BlockSpec((1, D_MODEL, D_MODEL), lambda i: (l, 0, 0)), blk,
                  pl.BlockSpec((M, D_MODEL), lambda i: (0, 0)), pl.BlockSpec((M, D_MODEL), lambda i: (0, 1))]
        + [_DEP_SPEC] * len(deps),
        out_specs=[blk, pl.BlockSpec((M, 2 * D_MODEL), lambda i: (0, 0))],
        out_shape=[SDS((S, D_MODEL), bf16), SDS((M, 2 * D_MODEL), f32)],
        compiler_params=_cp("arbitrary"),
    )(dzb, wo, qb, kv, kv, *deps)


def adamw(name, w, g, m, v, deps=()):
    shape = w.shape
    cols = shape[-1]
    rows = math.prod(shape[:-1])
    tr = _row_tile(rows, cols, target=2 * 2**20)
    c1 = 1.0 - ADAM_B1 ** ADAM_STEP
    c2 = 1.0 - ADAM_B2 ** ADAM_STEP

    def body(w_ref, g_ref, m_ref, v_ref, d_ref, nm_ref, nv_ref):
        gv = g_ref[...]
        nm = ADAM_B1 * m_ref[...] + (1.0 - ADAM_B1) * gv
        nv = ADAM_B2 * v_ref[...] + (1.0 - ADAM_B2) * (gv * gv)
        d_ref[...] = -ADAM_LR * ((nm / c1) / (jnp.sqrt(nv / c2) + ADAM_EPS) + ADAM_WD * w_ref[...])
        nm_ref[...] = nm
        nv_ref[...] = nv

    blk = pl.BlockSpec((tr, cols), lambda i: (i, 0))
    flat = SDS((rows, cols), f32)
    outs = pl.pallas_call(
        _with_deps(body, 4, deps), name=name, grid=(rows // tr,), in_specs=[blk] * 4 + [_DEP_SPEC] * len(deps),
        out_specs=[blk] * 3, out_shape=[flat] * 3, compiler_params=_cp("parallel"),
    )(*[a.reshape(rows, cols) for a in (w, g, m, v)], *deps)
    return [o.reshape(shape) for o in outs]


def _me():
    return lax.axis_index("x"), lax.axis_index("y"), lax.axis_index("c")


def _other_chips(x, y):
    return [(1 - x, y), (x, 1 - y), (1 - x, 1 - y)]


def _pair_share_each(owners, bufs, sems, mine, act):
    x, y, c = _me()
    for o in range(2):
        slots = [(a, lyr) for a in range(len(bufs)) for lyr in range(DEPTH) if owners[a][lyr] == o]

        @pl.when((c == o) if mine else (c != o))
        def _(slots=slots):
            for a, lyr in slots:
                slot = bufs[a].at[lyr]
                act(_rcopy(slot, slot, sems[0].at[2 * a + lyr], sems[1].at[2 * a + lyr], (x, y, 1 - c)))


def pair_share_start(name, sums, owners, after):
    def body_fn(b_in, s_in, s_out):
        _pair_share_each(owners, b_in, s_out, True, lambda cp: cp.start())

    outs, sems, token = _split_call(name, body_fn, list(sums), [], [2 * len(sums)] * 2, after)
    return (outs, sems[0], sems[1], owners), token


def pair_share_wait(name, st, after):
    bufs, send, recv, owners = st

    def body_fn(b_in, s_in, s_out):
        _pair_share_each(owners, b_in, s_in, True, lambda cp: cp.wait_send())
        _pair_share_each(owners, b_in, s_in, False, lambda cp: cp.wait_recv())

    outs, _, _ = _split_call(name, body_fn, list(bufs), [send, recv], [], after)
    return outs


def allsum_small(name, v, deps=()):
    R = v.shape[0]

    def body(v_ref, o_ref, all_ref, send_sems, recv_sems, local_sem):
        x, y, c = _me()
        me, sibling = (x, y, c), (x, y, 1 - c)
        chips = _other_chips(x, y)

        def rows(px, py, pc):
            return all_ref.at[4 * px + 2 * py + pc]

        def copy(k, block, to, src=None):
            return pltpu.make_async_remote_copy(
                src_ref=rows(*block) if src is None else src, dst_ref=rows(*block),
                send_sem=send_sems.at[k], recv_sem=recv_sems.at[k], device_id=to, device_id_type=MESH)

        mine = pltpu.make_async_copy(v_ref, rows(*me), local_sem)
        mine.start()
        first = [copy(0, me, sibling, src=v_ref)]
        first += [copy(1 + j, me, (*chip, c), src=v_ref) for j, chip in enumerate(chips)]
        for cp in first:
            cp.start()
        passed = [copy(4 + j, (*chip, c), sibling) for j, chip in enumerate(chips)]
        for j, chip in enumerate(chips):
            copy(1 + j, (*chip, c), me).wait_recv()
            passed[j].start()
        copy(0, sibling, me).wait_recv()
        for j, chip in enumerate(chips):
            copy(4 + j, (*chip, 1 - c), me).wait_recv()
        for cp in first + passed:
            cp.wait_send()
        mine.wait()
        acc = all_ref[0]
        for d in range(1, 8):
            acc = acc + all_ref[d]
        o_ref[...] = acc

    return pl.pallas_call(
        _with_deps(body, 1, deps), name=name,
        in_specs=[pl.BlockSpec(memory_space=pltpu.VMEM)] + [_DEP_SPEC] * len(deps),
        out_specs=pl.BlockSpec(memory_space=pltpu.VMEM),
        out_shape=SDS((R, 128), f32),
        scratch_shapes=[pltpu.VMEM((8, R, 128), f32), pltpu.SemaphoreType.DMA((7,)), pltpu.SemaphoreType.DMA((7,)),
                        pltpu.SemaphoreType.DMA],
        compiler_params=pltpu.CompilerParams(vmem_limit_bytes=V7X_VMEM_LIMIT),
    )(v, *deps)


def _swap_half(r):
    return jnp.concatenate([-r[..., 32:], r[..., :32]], axis=-1)


def _unswap_add(p, qg):
    return p + jnp.concatenate([qg[..., 32:], -qg[..., :32]], axis=-1)


def _block_diag(pw):
    L = pw.shape[0]
    out = jnp.zeros((L, 256, 256), pw.dtype)
    for gi in range(4):
        out = out.at[:, 64 * gi:64 * gi + 64, 64 * gi:64 * gi + 64].set(pw[:, gi])
    return out


def _to_col_shards(w):
    *lead, K, N = w.shape
    nl = len(lead)
    return w.reshape(*lead, K, N_CHIPS, N // N_CHIPS).transpose(*range(nl), nl + 1, nl, nl + 2)


def _from_col_shards(w):
    *lead, C, K, n = w.shape
    nl = len(lead)
    return w.transpose(*range(nl), nl + 1, nl, nl + 2).reshape(*lead, K, C * n)


_HBM_SPEC = pl.BlockSpec(memory_space=pltpu.HBM)
_SEM_SPEC = pl.BlockSpec(memory_space=pltpu.SEMAPHORE)
_ANY_SPEC = pl.BlockSpec(memory_space=pl.ANY)
_DATAFLOW = pltpu.SideEffectType.DATAFLOW_SIDE_EFFECTING


def _split_call(name, body_fn, bufs, sems_in, sems_out_sizes, after):
    nb, ni, no = len(bufs), len(sems_in), len(sems_out_sizes)
    afters = () if after is None else tuple(after) if isinstance(after, (tuple, list)) else (after,)

    def body(*refs):
        k = nb + ni + len(afters)
        body_fn(refs[:nb], refs[nb:nb + ni], refs[k:k + no])
        refs[-1][...] = jnp.zeros((8, 128), f32)

    outs = pl.pallas_call(
        body, name=name,
        in_specs=[_HBM_SPEC] * nb + [_SEM_SPEC] * ni + [_ANY_SPEC] * len(afters),
        out_specs=[_SEM_SPEC] * no + [_HBM_SPEC] * nb + [pl.BlockSpec(memory_space=pltpu.VMEM)],
        out_shape=[pltpu.SemaphoreType.DMA((s,)) for s in sems_out_sizes]
        + [pltpu.HBM(b.shape, b.dtype) for b in bufs] + [SDS((8, 128), f32)],
        input_output_aliases={i: no + i for i in range(nb)},
        compiler_params=pltpu.CompilerParams(has_side_effects=_DATAFLOW),
    )(*[pltpu.with_memory_space_constraint(b, pltpu.HBM) for b in bufs], *sems_in, *afters)
    return list(outs[no:no + nb]), list(outs[:no]), outs[-1]


def _rcopy(src, dst, ssem, rsem, to):
    return pltpu.make_async_remote_copy(src_ref=src, dst_ref=dst, send_sem=ssem, recv_sem=rsem, device_id=to,
                                        device_id_type=MESH)


def gather_start(name, groups, after):
    flat = [b for bufs, _ in groups for b in bufs]
    sizes = [3 * len(bufs) for bufs, _ in groups for _ in range(2)]

    def body_fn(b_in, s_in, s_out):
        x, y, c = _me()
        q = 2 * x + y
        chips = _other_chips(x, y)
        pos = 0
        for gi, (bufs, owner) in enumerate(groups):
            refs = b_in[pos:pos + len(bufs)]
            pos += len(bufs)

            @pl.when(c == owner)
            def _(refs=refs, send=s_out[2 * gi], recv=s_out[2 * gi + 1]):
                for a, r in enumerate(refs):
                    for k, (cx, cy) in enumerate(chips):
                        _rcopy(r.at[q], r.at[q], send.at[3 * a + k], recv.at[3 * a + k], (cx, cy, c)).start()

    outs, sems, token = _split_call(name, body_fn, flat, [], sizes, after)
    res, pos = [], 0
    for gi, (bufs, owner) in enumerate(groups):
        res.append((outs[pos:pos + len(bufs)], sems[2 * gi], sems[2 * gi + 1], owner))
        pos += len(bufs)
    return res, token


def gather_forward(name, grp, after):
    bufs, send, recv, owner = grp
    n3 = 3 * len(bufs)

    def body_fn(b_in, s_in, s_out):
        x, y, c = _me()
        q = 2 * x + y
        sibling = (x, y, 1 - c)
        chips = _other_chips(x, y)

        @pl.when(c == owner)
        def _():
            for a, r in enumerate(b_in):
                for k, (cx, cy) in enumerate(chips):
                    i = 3 * a + k
                    land = r.at[2 * cx + cy]
                    _rcopy(r.at[q], r.at[q], s_in[0].at[i], s_in[1].at[i], (cx, cy, c)).wait_send()
                    _rcopy(land, land, s_in[0].at[i], s_in[1].at[i], (cx, cy, c)).wait_recv()
                    _rcopy(land, land, s_out[0].at[i], s_out[1].at[i], sibling).start()

    outs, sems, token = _split_call(name, body_fn, bufs, [send, recv], [n3, n3], after)
    return (outs, sems[0], sems[1], owner), token


def gather_finish(name, grp, after):
    bufs, fsend, frecv, owner = grp

    def body_fn(b_in, s_in, s_out):
        x, y, c = _me()
        sibling = (x, y, 1 - c)
        chips = _other_chips(x, y)

        def each(wait):
            for a, r in enumerate(b_in):
                for k, (cx, cy) in enumerate(chips):
                    land = r.at[2 * cx + cy]
                    wait(_rcopy(land, land, s_in[0].at[3 * a + k], s_in[1].at[3 * a + k], sibling))

        @pl.when(c == owner)
        def _():
            each(lambda cp: cp.wait_send())

        @pl.when(c != owner)
        def _():
            each(lambda cp: cp.wait_recv())

    outs, _, _ = _split_call(name, body_fn, bufs, [fsend, frecv], [], after)
    return outs


def _by_owner(owners):
    return [[a for a, o_ in enumerate(owners) if o_ == o] for o in range(2)]


def pair_send_start(name, gs, owners, after):
    n = len(gs)
    lands = [lax.empty(g.shape, g.dtype) for g in gs]

    def body_fn(b_in, s_in, s_out):
        x, y, c = _me()
        for o, idx in enumerate(_by_owner(owners)):
            @pl.when(c == 1 - o)
            def _(o=o, idx=idx):
                for a in idx:
                    _rcopy(b_in[a], b_in[n + a], s_out[0].at[a], s_out[1].at[a], (x, y, o)).start()

    outs, sems, token = _split_call(name, body_fn, list(gs) + lands, [], [n, n], after)
    return (outs[:n], outs[n:], sems[0], sems[1], owners), token


def pair_send_wait(name, st, after):
    gs, lands, send, recv, owners = st
    n = len(gs)

    def body_fn(b_in, s_in, s_out):
        x, y, c = _me()
        for o, idx in enumerate(_by_owner(owners)):
            @pl.when(c == 1 - o)
            def _(o=o, idx=idx):
                for a in idx:
                    _rcopy(b_in[a], b_in[n + a], s_in[0].at[a], s_in[1].at[a], (x, y, o)).wait_send()

            @pl.when(c == o)
            def _(o=o, idx=idx):
                for a in idx:
                    _rcopy(b_in[a], b_in[n + a], s_in[0].at[a], s_in[1].at[a], (x, y, 1 - o)).wait_recv()

    outs, _, _ = _split_call(name, body_fn, list(gs) + list(lands), [send, recv], [], after)
    return outs[:n], outs[n:]


def chip_exchange_start(name, psums, owners, after):
    n = len(psums)
    lands = [lax.empty((3,) + p.shape[1:], p.dtype) for p in psums]

    def body_fn(b_in, s_in, s_out):
        x, y, c = _me()
        chips = _other_chips(x, y)
        for o, idx in enumerate(_by_owner(owners)):
            @pl.when(c == o)
            def _(idx=idx):
                for a in idx:
                    for k, (cx, cy) in enumerate(chips):
                        _rcopy(b_in[a].at[2 * cx + cy], b_in[n + a].at[k], s_out[0].at[3 * a + k],
                               s_out[1].at[3 * a + k], (cx, cy, c)).start()

    outs, sems, token = _split_call(name, body_fn, list(psums) + lands, [], [3 * n, 3 * n], after)
    return (outs[:n], outs[n:], sems[0], sems[1], owners), token


def chip_exchange_wait(name, st, after):
    psums, lands, send, recv, owners = st
    n = len(psums)

    def body_fn(b_in, s_in, s_out):
        x, y, c = _me()
        chips = _other_chips(x, y)
        for o, idx in enumerate(_by_owner(owners)):
            @pl.when(c == o)
            def _(idx=idx):
                for a in idx:
                    for k, (cx, cy) in enumerate(chips):
                        cp = _rcopy(b_in[a].at[2 * cx + cy], b_in[n + a].at[k], s_in[0].at[3 * a + k],
                                    s_in[1].at[3 * a + k], (cx, cy, c))
                        cp.wait_send()
                        cp.wait_recv()

    outs, _, _ = _split_call(name, body_fn, list(psums) + list(lands), [send, recv], [], after)
    return outs[:n], outs[n:]


def pair_sum(name, g, recv, flag):
    shape = g.shape
    cols = shape[-1]
    rows = math.prod(shape[:-1])
    tr = _row_tile(rows, cols, target=4 * 2**20)

    def body(f_ref, g_ref, r_ref, o_ref):
        o_ref[...] = (g_ref[...] + r_ref[...]).astype(bf16)

    blk = pl.BlockSpec((tr, cols), lambda i, f_ref: (i * f_ref[0], 0))
    out = pl.pallas_call(
        body, name=name,
        grid_spec=pltpu.PrefetchScalarGridSpec(num_scalar_prefetch=1, grid=(rows // tr,), in_specs=[blk, blk],
                                               out_specs=blk),
        out_shape=SDS((rows, cols), bf16), compiler_params=_cp("arbitrary"),
    )(flag, g.reshape(rows, cols), recv.reshape(rows, cols))
    return out.reshape(shape)


def chip_sum(name, psum, recv, qf_arr, layer, prev):
    shard = psum.shape[1:]
    cols = shard[-1]
    rows = math.prod(shard[:-1])
    tr = _row_tile(rows, cols, target=4 * 2**20)

    def body(qf_ref, p_ref, r_ref, *rest):
        rest[-1][0] = ((p_ref[0].astype(f32) + r_ref[0].astype(f32)) + r_ref[1].astype(f32)) + r_ref[2].astype(f32)

    in_specs = [pl.BlockSpec((1, tr, cols), lambda i, qf: (qf[0], i * qf[1], 0)),
                pl.BlockSpec((3, tr, cols), lambda i, qf: (0, i * qf[1], 0))]
    args = [qf_arr, psum.reshape(N_CHIPS, rows, cols), recv.reshape(3, rows, cols)]
    aliases = {}
    if prev is not None:
        in_specs.append(pl.BlockSpec(memory_space=pl.ANY))
        args.append(prev.reshape(DEPTH, rows, cols))
        aliases = {3: 0}
    out = pl.pallas_call(
        body, name=name,
        grid_spec=pltpu.PrefetchScalarGridSpec(
            num_scalar_prefetch=1, grid=(rows // tr,), in_specs=in_specs,
            out_specs=pl.BlockSpec((1, tr, cols), lambda i, qf: (layer, i * qf[1], 0))),
        out_shape=SDS((DEPTH, rows, cols), f32), input_output_aliases=aliases, compiler_params=_cp("arbitrary"),
    )(*args)
    return out.reshape((DEPTH,) + shard)


W_NAMES = ("f1w13", "f1w2", "win", "wuq", "wukv", "wout", "mwq", "mwkv", "mwo", "f2w13", "f2w2")
MIX_NAMES = ("win", "wuq", "wukv")
MID_NAMES = ("wout", "mwq", "mwkv", "mwo")
FFN2_NAMES = ("f2w13", "f2w2")
REDUCER = (dict(f1w13=0, f1w2=1, f2w13=0, win=0, wuq=0, wukv=0, f2w2=1, mwkv=1, wout=1, mwq=1, mwo=1),
           dict(f1w13=0, f2w2=0, mwkv=0, wout=0, f2w13=1, f1w2=1, mwq=1, mwo=1, win=1, wuq=1, wukv=1))


def kernel(x, mem, positions, ln_g, ln_b, ffn1_w13, ffn1_w2, w_in, pool_w, pool_scale, q_norm_g, w_uq, kv_norm_g, w_ukv, w_out, mem_wq, mem_wkv, mem_wo, ffn2_w13, ffn2_w2, loss_target, m_ln_g, m_ln_b, m_ffn1_w13, m_ffn1_w2, m_w_in, m_pool_w, m_pool_scale, m_q_norm_g, m_w_uq, m_kv_norm_g, m_w_ukv, m_w_out, m_mem_wq, m_mem_wkv, m_mem_wo, m_ffn2_w13, m_ffn2_w2, v_ln_g, v_ln_b, v_ffn1_w13, v_ffn1_w2, v_w_in, v_pool_w, v_pool_scale, v_q_norm_g, v_w_uq, v_kv_norm_g, v_w_ukv, v_w_out, v_mem_wq, v_mem_wkv, v_mem_wo, v_ffn2_w13, v_ffn2_w2):
    L = DEPTH
    qx, qy, _ = _me()
    chip = 2 * qx + qy
    vec = lambda a: a.reshape(1, -1)

    shards = dict(zip(W_NAMES, (ffn1_w13, ffn1_w2, w_in, w_uq, w_ukv, w_out, mem_wq, mem_wkv, mem_wo, ffn2_w13, ffn2_w2)))

    def place(sh, slot):
        return lax.dynamic_update_slice(lax.empty((N_CHIPS,) + sh.shape, bf16), sh.astype(bf16)[None],
                                        (slot,) + (0,) * sh.ndim)

    first = ("f1w13", "f1w2")
    bufs = [dict(), dict()]
    for n in first:
        bufs[0][n] = place(shards[n][0], chip)
    gw = [dict(), dict()]
    (g0,), tok = gather_start("gather_a_start", [([bufs[0][n] for n in first], 0)], None)
    chip_then = chip + tok[0, 0].astype(jnp.int32)
    for l in range(L):
        for n in W_NAMES:
            if n not in bufs[l]:
                bufs[l][n] = place(shards[n][l], chip_then)
    others = tuple(bufs[l][n] for l in range(L) for n in W_NAMES if (l, n) not in ((0, first[0]), (0, first[1])))
    g0, tok = gather_forward("gather_a_forward", g0, others)

    ln_pad = jnp.zeros((2, L, 4, N_CHIPS, D_MODEL // N_CHIPS), f32)
    ln_pad = lax.dynamic_update_slice(ln_pad, jnp.stack([ln_g, ln_b])[:, :, :, None, :], (0, 0, 0, chip, 0))
    ln_sum = allsum_small("allsum_ln", ln_pad.reshape(-1, 128), (tok,))
    ln_full = (ln_sum * 0.5).reshape(2, L, 4, D_MODEL)
    lng, lnb = ln_full[0], ln_full[1]

    gw[0]["f1w13"], gw[0]["f1w2"] = gather_finish("gather_a_finish", g0, ln_sum)
    (g_mix, g_mid, g_ffn2, g_l1), tok_b = gather_start(
        "gather_b_start",
        [([bufs[0][n] for n in MIX_NAMES], 0), ([bufs[0][n] for n in MID_NAMES], 0), ([bufs[0][n] for n in FFN2_NAMES], 0),
         ([bufs[1][n] for n in W_NAMES], 1)], ln_sum)

    half = QK_ROPE // 2
    inv_freq = ROPE_BASE ** (-jnp.arange(half, dtype=f32) / half)
    ang = positions[0].astype(f32)[:, None] * inv_freq
    cos, sin = jnp.cos(ang), jnp.sin(ang)
    cs = jnp.concatenate([cos, cos, sin, sin], axis=-1)

    memb = mem[0].astype(bf16)
    xf = x[0]
    xb = xf.astype(bf16)
    dep = (tok_b,)

    saved, W = [], [None, None]
    for l in range(L):
        sv = {}
        if l == 1:
            gl1 = gather_finish("gather_l1_finish", g_l1, xb)
            gw[1] = dict(zip(W_NAMES, gl1))
        sv["x0b"] = xb
        f1w13 = gw[l]["f1w13"][None]
        gate, up, act = ffn_up(f"ffn1_up_{l}", xb, f1w13, 0, dep)
        dep = ()
        if l == 0:
            g_mix, _ = gather_forward("gather_mix_forward", g_mix, act)
        z1, x1f, x1b = proj_res_ln(f"ffn1_down_{l}", [act], [gw[l]["f1w2"].reshape(1, D_FF, D_MODEL)], [0], xf,
                                   vec(lng[l, 0]), vec(lnb[l, 0]), 0.5)
        sv.update(gate1=gate, up1=up, act1=act, z1=z1, x1b=x1b)
        if l == 0:
            gw[0].update(zip(MIX_NAMES, gather_finish("gather_mix_finish", g_mix, x1b)))
            g_mid, _ = gather_forward("gather_mid_forward", g_mid, x1b)
        win = gw[l]["win"].reshape(D_MODEL, D_IN)
        win_ext = jnp.concatenate([win, _swap_half(win[:, D_IN - QK_ROPE:])], axis=-1)[None]
        wuq = _from_col_shards(gw[l]["wuq"]).reshape(Q_LORA, MLA_HEADS, QK_NOPE + QK_ROPE)
        wq_ext = jnp.concatenate([wuq, _swap_half(wuq[..., QK_NOPE:])], axis=-1).transpose(1, 0, 2)[None]
        wukv = _from_col_shards(gw[l]["wukv"])[None]
        wbd = _block_diag(pool_w[l][None].astype(bf16))[0]
        u, cq, ckv, cqn, ckvn, q, k, v = mix_pre(f"mix_pre_{l}", x1b, win_ext, wq_ext, wukv, 0,
                                                   vec(q_norm_g[l]), vec(kv_norm_g[l]), cs)
        dpool, ypool = pool_fwd(f"pool_fwd_{l}", u, wbd, vec(pool_scale[l]))
        o, lse = mla_attn_fwd(f"mla_fwd_{l}", q, k, v)
        if l == 0:
            gw[0].update(zip(MID_NAMES, gather_finish("gather_mid_finish", g_mid, o)))
            g_ffn2, tok_f = gather_forward("gather_ffn2_forward", g_ffn2, o)
            g_l1, tok_l = gather_forward("gather_l1_forward", g_l1, o)
            dep = (tok_f, tok_l)
        wout = gw[l]["wout"].reshape(D_MODEL, D_MODEL)
        wout_pool, wout_mla = wout[None, :POOL_WIDTH], wout[None, POOL_WIDTH:]
        mwq = gw[l]["mwq"].reshape(1, D_MODEL, D_MODEL)
        mwo = gw[l]["mwo"].reshape(1, D_MODEL, D_MODEL)
        mwkv = gw[l]["mwkv"][None]
        z2, x2f, x2b = proj_res_ln(f"mix_out_{l}", [ypool, o], [wout_pool, wout_mla], [0, 0], x1f,
                                   vec(lng[l, 1]), vec(lnb[l, 1]), 1.0, dep)
        dep = ()
        sv.update(cq=cq, ckv=ckv, cqn=cqn, ckvn=ckvn, q=q, k=k, v=v, dpool=dpool, ypool=ypool, o=o, lse=lse, z2=z2, x2b=x2b)
        kvm = mm_nn_shard(f"mem_kv_{l}", memb, mwkv, 0)
        cq_, co_, z3, x3f, x3b = cross_fwd(f"cross_fwd_{l}", x2b, x2f, mwq, mwo, 0, kvm, vec(lng[l, 2]), vec(lnb[l, 2]))
        sv.update(kvm=kvm, crq=cq_, cro=co_, z3=z3, x3b=x3b)
        if l == 0:
            gw[0].update(zip(FFN2_NAMES, gather_finish("gather_ffn2_finish", g_ffn2, x3b)))
        f2w13 = gw[l]["f2w13"][None]
        f2w2 = gw[l]["f2w2"].reshape(1, D_FF, D_MODEL)
        gate, up, act = ffn_up(f"ffn2_up_{l}", x3b, f2w13, 0)
        z4, xf, xb = proj_res_ln(f"ffn2_down_{l}", [act], [f2w2], [0], x3f, vec(lng[l, 3]), vec(lnb[l, 3]), 0.5)
        sv.update(gate2=gate, up2=up, act2=act, z4=z4)
        W[l] = dict(f1w13=f1w13, f1w2=gw[l]["f1w2"].reshape(1, D_FF, D_MODEL), win_ext=win_ext, wq_ext=wq_ext, wukv=wukv,
                    wbd=wbd, wout=wout[None], mwq=mwq, mwo=mwo, f2w13=f2w13, f2w2=f2w2)
        saved.append(sv)

    dln = {}
    dzb, dres, *dln[L - 1, 3], loss_blk = loss_grad("loss_grad", xf, loss_target[0],
                                                   (saved[L - 1]["z4"], vec(lng[L - 1, 3]), 0.5))
    loss = lax.psum(loss_blk[0, 0], ("x", "y", "c"))

    row_shards = lambda a: a.reshape(N_CHIPS, a.shape[0] // N_CHIPS, a.shape[1])
    small = {k_: [None] * L for k_ in ("pool_w", "pool_scale", "gq", "gkv", "lng", "lnb")}
    rest_names = [n for n in W_NAMES if n not in ("f1w13", "f1w2")]
    core = lax.axis_index("c")
    flags = [jnp.reshape(core == o, (1,)).astype(jnp.int32) for o in range(2)]
    qfs = [jnp.stack([chip, (core == o).astype(jnp.int32)]).astype(jnp.int32) for o in range(2)]

    def red_begin(tag, names, gs, layer):
        owners = [REDUCER[layer][n] for n in names]
        st, tok_ = pair_send_start(f"pair_send_start_{tag}", gs, owners, None)
        return (st, owners), tok_

    def red_mid(tag, sto, after):
        st, owners = sto
        gs_, lands_ = pair_send_wait(f"pair_send_wait_{tag}", st, after)
        ps = [pair_sum(f"pair_sum_{tag}_{a}", g_, r_, flags[o]) for a, (g_, r_, o) in enumerate(zip(gs_, lands_, owners))]
        st, tok_ = chip_exchange_start(f"chip_exchange_start_{tag}", ps, owners, None)
        return (st, owners), tok_

    def red_end(tag, sto, layer, prevs, after):
        st, owners = sto
        ps, lands_ = chip_exchange_wait(f"chip_exchange_wait_{tag}", st, after)
        return [chip_sum(f"chip_sum_{tag}_{a}", p_, r_, qfs[o], layer, s_)
                for a, (p_, r_, s_, o) in enumerate(zip(ps, lands_, prevs, owners))]

    def share_start(tag, names, sums_):
        return pair_share_start(f"pair_share_start_{tag}", sums_, [(REDUCER[0][n], REDUCER[1][n]) for n in names], None)

    st_p1 = st_c1 = st_pa = st_ca = None
    for l in reversed(range(L)):
        sv, w = saved[l], W[l]
        g = {}
        dh = ffn_bwd_da(f"ffn2_bwd_da_{l}", dzb, w["f2w2"], 0, sv["gate2"], sv["up2"], dep)
        dep = ()
        g["f2w2"] = row_shards(mm_tn(f"ffn2_dw2_{l}", sv["act2"], dzb))
        g["f2w13"] = mm_tn(f"ffn2_dw13_{l}", sv["x3b"], dh, True)
        dzb, dres, *dln[l, 2] = ffn_dx(f"ffn2_dx_{l}", dh, w["f2w13"], 0, dres, (sv["z3"], vec(lng[l, 2]), 1.0))
        if l == 0:
            st_c1, tok = red_mid("l1", st_p1, dzb)
            dep = (tok, g["f2w2"], g["f2w13"])
        dqc, dkvm = cross_bwd(f"cross_bwd_{l}", dzb, w["mwo"], 0, sv["crq"], sv["kvm"], dep)
        dep = ()
        g["mwo"] = row_shards(mm_tn(f"cross_dwo_{l}", sv["cro"], dzb))
        g["mwq"] = row_shards(mm_tn(f"cross_dwq_{l}", sv["x2b"], dqc))
        g["mwkv"] = mm_tn(f"cross_dwkv_{l}", memb, dkvm, True)
        dzb, dres, *dln[l, 1] = mm_nt_res(f"cross_dx_{l}", [dqc], [w["mwq"]], [0], dres, f32,
                                          (sv["z2"], vec(lng[l, 1]), 1.0))
        dcat = mm_nt_res(f"mix_dcat_{l}", [dzb], [w["wout"]], [0], None, bf16)
        dwo_p = mm_tn(f"mix_dwout_pool_{l}", sv["ypool"], dzb)
        dwo_m = mm_tn(f"mix_dwout_mla_{l}", sv["o"], dzb)
        g["wout"] = row_shards(jnp.concatenate([dwo_p, dwo_m], axis=0))
        dq, dk, dv = mla_attn_bwd(f"mla_bwd_{l}", sv["q"], sv["k"], sv["v"], sv["o"], dcat, sv["lse"], POOL_WIDTH // 128)
        dqe, dkv, dh_rest, dgq, dgkv = mix_post_bwd(f"mix_post_bwd_{l}", dq, dk, dv, w["wq_ext"], w["wukv"], 0, sv["cq"],
                                                     sv["ckv"], vec(q_norm_g[l]), vec(kv_norm_g[l]), cs)
        du, dyw, dscale = pool_bwd(f"pool_bwd_{l}", dcat, sv["dpool"], w["wbd"], vec(pool_scale[l]))
        dwq_e = mm_tn(f"mix_dwuq_{l}", sv["cqn"], dqe).reshape(Q_LORA, MLA_HEADS, 256)
        g["wuq"] = _to_col_shards(jnp.concatenate(
            [dwq_e[..., :QK_NOPE], _unswap_add(dwq_e[..., QK_NOPE:QK_NOPE + QK_ROPE], dwq_e[..., QK_NOPE + QK_ROPE:])],
            axis=-1).reshape(Q_LORA, MLA_HEADS * (QK_NOPE + QK_ROPE)))
        g["wukv"] = _to_col_shards(mm_tn(f"mix_dwukv_{l}", sv["ckvn"], dkv))
        dwbd = mm_tn(f"pool_dw_{l}", sv["dpool"], dyw)
        small["pool_w"][l] = jnp.stack([dwbd[64 * gi:64 * gi + 64, 64 * gi:64 * gi + 64] for gi in range(4)])
        small["pool_scale"][l], small["gq"][l], small["gkv"][l] = dscale[0], dgq[0], dgkv[0]
        dh_ext = jnp.concatenate([du, dh_rest], axis=1)
        dwin_e = mm_tn(f"mix_dwin_{l}", sv["x1b"], dh_ext)
        g["win"] = row_shards(jnp.concatenate(
            [dwin_e[:, :D_IN - QK_ROPE], _unswap_add(dwin_e[:, D_IN - QK_ROPE:D_IN], dwin_e[:, D_IN:])], axis=-1))
        dzb, dres, *dln[l, 0] = mm_nt_res(f"mix_dx_{l}", [dh_ext], [w["win_ext"]], [0], dres, f32,
                                          (sv["z1"], vec(lng[l, 0]), 0.5))
        if l == 0:
            st_pa, tok = red_begin("a0", rest_names, [g[n] for n in rest_names], 0)
            dep = (tok,)
        dh = ffn_bwd_da(f"ffn1_bwd_da_{l}", dzb, w["f1w2"], 0, sv["gate1"], sv["up1"], dep)
        dep = ()
        if l == 0:
            grad_x = ffn_dx(f"ffn1_dx_{l}", dh, w["f1w13"], 0, dres)[None]
            st_ca, tok = red_mid("a0", st_pa, grad_x)
            dep = (tok,)
        else:
            below = ffn_dx(f"ffn1_dx_{l}", dh, w["f1w13"], 0, dres, (saved[l - 1]["z4"], vec(lng[l - 1, 3]), 0.5))
            dln[l - 1, 3] = below[2:]
        g["f1w2"] = row_shards(mm_tn(f"ffn1_dw2_{l}", sv["act1"], dzb, False, dep))
        g["f1w13"] = mm_tn(f"ffn1_dw13_{l}", sv["x0b"], dh, True, dep)
        dep = ()
        if l > 0:
            dzb, dres = below[:2]
        if l == 1:
            st_p1, tok = red_begin("l1", W_NAMES, [g[n] for n in W_NAMES], 1)
            dep = (tok,)
    for l in range(L):
        small["lng"][l] = jnp.concatenate([dln[l, k][0] for k in range(4)], axis=0)
        small["lnb"][l] = jnp.concatenate([dln[l, k][1] for k in range(4)], axis=0)

    st_pb, _ = red_begin("b0", ("f1w13", "f1w2"), [g["f1w13"], g["f1w2"]], 0)
    sums1 = dict(zip(W_NAMES, red_end("l1", st_c1, 1, [None] * len(W_NAMES), g["f1w13"])))
    st_cb, tok = red_mid("b0", st_pb, tuple(sums1.values()))
    sums0 = red_end("a0", st_ca, 0, [sums1[n] for n in rest_names], tok)
    share_a, tok_a = share_start("a", rest_names, sums0)

    rep = [jnp.stack(small["pool_w"]).reshape(-1), jnp.stack(small["pool_scale"]).reshape(-1),
           jnp.stack(small["gq"]).reshape(-1), jnp.stack(small["gkv"]).reshape(-1),
           jnp.stack(small["lng"]).reshape(-1), jnp.stack(small["lnb"]).reshape(-1)]
    sizes = [r.shape[0] for r in rep]
    packed = jnp.concatenate(rep)
    pad = (-packed.shape[0]) % 1024
    tot = allsum_small("allsum_small_grads", jnp.pad(packed, (0, pad)).reshape(-1, 128), (tok_a,)).reshape(-1)
    offs = [0]
    for s_ in sizes:
        offs.append(offs[-1] + s_)
    parts = [tot[offs[i]:offs[i + 1]] for i in range(len(sizes))]
    g_pool_w = parts[0].reshape(pool_w.shape)
    g_pool_scale = parts[1].reshape(pool_scale.shape)
    g_gq = parts[2].reshape(q_norm_g.shape)
    g_gkv = parts[3].reshape(kv_norm_g.shape)
    shard_cols = lambda a: lax.dynamic_slice_in_dim(a.reshape(L, 4, D_MODEL), chip * (D_MODEL // N_CHIPS),
                                                    D_MODEL // N_CHIPS, axis=2)
    g_lng, g_lnb = shard_cols(parts[4]), shard_cols(parts[5])

    out_names = ("lng", "lnb", "f1w13", "f1w2", "win", "pool_w", "pool_scale", "gq", "wuq", "gkv", "wukv", "wout", "mwq",
                 "mwkv", "mwo", "f2w13", "f2w2")
    big = dict(lng=g_lng, lnb=g_lnb, pool_w=g_pool_w, pool_scale=g_pool_scale, gq=g_gq, gkv=g_gkv)
    late = ("f1w13", "f1w2")
    held = ("f2w13", "f2w2")
    ws = [ln_g, ln_b, ffn1_w13, ffn1_w2, w_in, pool_w, pool_scale, q_norm_g, w_uq, kv_norm_g, w_ukv, w_out, mem_wq,
          mem_wkv, mem_wo, ffn2_w13, ffn2_w2]
    ms = [m_ln_g, m_ln_b, m_ffn1_w13, m_ffn1_w2, m_w_in, m_pool_w, m_pool_scale, m_q_norm_g, m_w_uq, m_kv_norm_g, m_w_ukv,
          m_w_out, m_mem_wq, m_mem_wkv, m_mem_wo, m_ffn2_w13, m_ffn2_w2]
    vs = [v_ln_g, v_ln_b, v_ffn1_w13, v_ffn1_w2, v_w_in, v_pool_w, v_pool_scale, v_q_norm_g, v_w_uq, v_kv_norm_g, v_w_ukv,
          v_w_out, v_mem_wq, v_mem_wkv, v_mem_wo, v_ffn2_w13, v_ffn2_w2]
    res = {}

    def update(n, deps=()):
        a = out_names.index(n)
        res[a] = adamw(f"adamw_{a}", ws[a], big[n].reshape(ws[a].shape), ms[a], vs[a], deps)
        return res[a][0]

    small_done = tuple(update(n) for n in ("lng", "lnb", "pool_w", "pool_scale", "gq", "gkv"))
    big.update(zip(rest_names, pair_share_wait("pair_share_wait_a", share_a, small_done)))
    first_done = tuple(update(n) for n in rest_names if n not in held)
    sums_b = red_end("b0", st_cb, 0, [sums1[n] for n in late], first_done)
    share_b, tok_b = share_start("b", late, sums_b)
    held_done = tuple(update(n, (tok_b,)) for n in held)
    big.update(zip(late, pair_share_wait("pair_share_wait_b", share_b, held_done)))
    for n in late:
        update(n)
    order = range(len(out_names))
    grads = [big[n].reshape(w_.shape) for n, w_ in zip(out_names, ws)]
    return (loss, grad_x, *grads, *[res[a][0] for a in order], *[res[a][1] for a in order], *[res[a][2] for a in order])
```

```python
import functools
import math

import jax
import jax.numpy as jnp
from jax import lax
from jax.experimental import pallas as pl
from jax.experimental.pallas import tpu as pltpu

f32 = jnp.float32
bf16 = jnp.bfloat16
SDS = jax.ShapeDtypeStruct
MESH = pl.DeviceIdType.MESH

D_MODEL = 1024
DEPTH = 2
N_MEM = 256
MEM_HEADS = 4
MEM_HEAD_DIM = D_MODEL // MEM_HEADS
POOL_WINDOWS = (2, 4, 8, 16)
POOL_WIDTH = 256
POOL_GROUP = 64
QK_NOPE = 128
QK_ROPE = 64
V_HEAD = 128
MLA_HEADS = 6
Q_LORA = 256
KV_LORA = 128
ROPE_BASE = 10000.0
D_FF = 2816
D_IN = POOL_WIDTH + Q_LORA + KV_LORA + QK_ROPE
ALPHA = (2 * DEPTH) ** 0.25
LN_EPS = 1e-5
RMS_EPS = 1e-6
NEG_INF = -1e30
MLA_SCALE = (QK_NOPE + QK_ROPE) ** -0.5
MLA_SCALE_LOG2 = MLA_SCALE * math.log2(math.e)
MEM_SCALE = MEM_HEAD_DIM ** -0.5
ADAM_LR = 0.001
ADAM_B1 = 0.9
ADAM_B2 = 0.999
ADAM_EPS = 1e-08
ADAM_WD = 0.01
ADAM_STEP = 10

N_CHIPS = 4
V7X_VMEM_LIMIT = 56 * 2**20
HALO = 16

_NT = (((1,), (1,)), ((), ()))
_TN = (((0,), (0,)), ((), ()))


def _dot(a, b):
    return jnp.dot(a, b, preferred_element_type=f32)


def _dot_nt(a, b):
    return lax.dot_general(a, b, _NT, preferred_element_type=f32)


def _dot_tn(a, b):
    return lax.dot_general(a, b, _TN, preferred_element_type=f32)


def _cp(*sem):
    return pltpu.CompilerParams(dimension_semantics=sem if sem else None, vmem_limit_bytes=V7X_VMEM_LIMIT)


_DEP_SPEC = pl.BlockSpec(memory_space=pl.ANY)


def _with_deps(body, n_in, deps):
    nd = len(deps)
    if not nd:
        return body

    def wrapped(*refs):
        return body(*refs[:n_in], *refs[n_in + nd:])

    return wrapped


def _tile(n, t):
    t = min(n, t)
    assert n % t == 0, (n, t)
    return t


def _row_tile(rows, cols, itemsize=4, target=2 * 2**20):
    best = None
    for t in range(16, rows + 1, 16):
        if rows % t == 0 and t * cols * itemsize <= target:
            best = t
    return best if best is not None else rows


def ffn_up(name, xb, w13, l, deps=()):
    S = xb.shape[0]
    ns = w13.shape[3]
    tm = _tile(S, 512)

    def body(x_ref, wg_ref, wu_ref, g_ref, u_ref, a_ref):
        x = x_ref[...]
        g = _dot(x, wg_ref[0, 0])
        u = _dot(x, wu_ref[0, 0])
        a = g * jax.nn.sigmoid(g) * u
        g_ref[...] = g.astype(bf16)
        u_ref[...] = u.astype(bf16)
        a_ref[...] = a.astype(bf16)

    out = SDS((S, 2 * ns), bf16)
    return pl.pallas_call(
        _with_deps(body, 3, deps), name=name, grid=(2, S // tm),
        in_specs=[pl.BlockSpec((tm, D_MODEL), lambda j, i: (i, 0)),
                  pl.BlockSpec((1, 1, D_MODEL, ns), lambda j, i: (l, j, 0, 0)),
                  pl.BlockSpec((1, 1, D_MODEL, ns), lambda j, i: (l, j + 2, 0, 0))] + [_DEP_SPEC] * len(deps),
        out_specs=[pl.BlockSpec((tm, ns), lambda j, i: (i, j))] * 3,
        out_shape=[out, out, out],
        compiler_params=_cp("parallel", "parallel"),
    )(xb, w13, w13, *deps)


def proj_res_ln(name, parts, ws, wl, x, g, b, rscale, deps=()):
    S = x.shape[0]
    tm = _tile(S, 512)
    n = len(parts)

    def body(*refs):
        p_refs, w_refs = refs[:n], refs[n:2 * n]
        x_ref, g_ref, b_ref, z_ref, y_ref, yb_ref = refs[2 * n:]
        acc = _dot(p_refs[0][...], w_refs[0][0])
        for k in range(1, n):
            acc = acc + _dot(p_refs[k][...], w_refs[k][0])
        if rscale != 1.0:
            acc = rscale * acc
        z = ALPHA * x_ref[...] + acc
        mu = jnp.mean(z, axis=-1, keepdims=True)
        zc = z - mu
        var = jnp.mean(zc * zc, axis=-1, keepdims=True)
        y = zc * lax.rsqrt(var + LN_EPS) * g_ref[...] + b_ref[...]
        z_ref[...] = z
        y_ref[...] = y
        yb_ref[...] = y.astype(bf16)

    row = lambda i: (i, 0)
    in_specs = [pl.BlockSpec((tm, p.shape[1]), row) for p in parts]
    in_specs += [pl.BlockSpec((1,) + w.shape[1:], functools.partial(lambda li, i: (li, 0, 0), li)) for w, li in zip(ws, wl)]
    in_specs += [pl.BlockSpec((tm, D_MODEL), row), pl.BlockSpec((1, D_MODEL), lambda i: (0, 0)),
                 pl.BlockSpec((1, D_MODEL), lambda i: (0, 0))] + [_DEP_SPEC] * len(deps)
    return pl.pallas_call(
        _with_deps(body, 2 * n + 3, deps), name=name, grid=(S // tm,), in_specs=in_specs,
        out_specs=[pl.BlockSpec((tm, D_MODEL), row)] * 3,
        out_shape=[SDS((S, D_MODEL), f32), SDS((S, D_MODEL), f32), SDS((S, D_MODEL), bf16)],
        compiler_params=_cp("parallel"),
    )(*parts, *ws, x, g, b, *deps)


def _ln_bwd_store(dyv, z_ref, g_ref, rscale, first, dzb_ref, dres_ref, dg_ref, db_ref):
    z = z_ref[...]
    mu = jnp.mean(z, axis=-1, keepdims=True)
    zc = z - mu
    rstd = lax.rsqrt(jnp.mean(zc * zc, axis=-1, keepdims=True) + LN_EPS)
    xhat = zc * rstd
    dxh = dyv * g_ref[...]
    m1 = jnp.mean(dxh, axis=-1, keepdims=True)
    m2 = jnp.mean(dxh * xhat, axis=-1, keepdims=True)
    dz = rstd * (dxh - m1 - xhat * m2)
    dzb_ref[...] = (rscale * dz).astype(bf16)
    dres_ref[...] = ALPHA * dz

    @pl.when(first)
    def _():
        dg_ref[...] = jnp.zeros_like(dg_ref)
        db_ref[...] = jnp.zeros_like(db_ref)

    dg_ref[...] += jnp.sum(dyv * xhat, axis=0, keepdims=True)
    db_ref[...] += jnp.sum(dyv, axis=0, keepdims=True)


def _ln_bwd_specs(S, tm, index):
    vec = pl.BlockSpec((1, D_MODEL), lambda *a: (0, 0))
    blk = pl.BlockSpec((tm, D_MODEL), index)
    in_specs = [blk, vec]
    out_specs = [blk, blk, vec, vec]
    out_shape = [SDS((S, D_MODEL), bf16), SDS((S, D_MODEL), f32), SDS((1, D_MODEL), f32), SDS((1, D_MODEL), f32)]
    return in_specs, out_specs, out_shape


def ffn_bwd_da(name, drb, w2, l, gate, up, deps=()):
    S = drb.shape[0]
    tm = _tile(S, 512)
    nh = D_FF // 2

    def body(dr_ref, w_ref, g_ref, u_ref, dh_ref):
        dr = dr_ref[...]
        for j in range(2):
            cols = slice(j * nh, (j + 1) * nh)
            da = _dot_nt(dr, w_ref[0, cols, :])
            g = g_ref[:, cols].astype(f32)
            u = u_ref[:, cols].astype(f32)
            sg = jax.nn.sigmoid(g)
            dh_ref[:, cols] = (da * u * (sg * (1.0 + g * (1.0 - sg)))).astype(bf16)
            dh_ref[:, D_FF + j * nh:D_FF + (j + 1) * nh] = (da * (g * sg)).astype(bf16)

    row = lambda i: (i, 0)
    return pl.pallas_call(
        _with_deps(body, 4, deps), name=name, grid=(S // tm,),
        in_specs=[pl.BlockSpec((tm, D_MODEL), row), pl.BlockSpec((1, D_FF, D_MODEL), lambda i: (l, 0, 0)),
                  pl.BlockSpec((tm, D_FF), row), pl.BlockSpec((tm, D_FF), row)] + [_DEP_SPEC] * len(deps),
        out_specs=pl.BlockSpec((tm, 2 * D_FF), row),
        out_shape=SDS((S, 2 * D_FF), bf16),
        compiler_params=_cp("parallel"),
    )(drb, w2, gate, up, *deps)


def ffn_dx(name, dh, w13, l, res, ln=None):
    S = dh.shape[0]
    ns = w13.shape[3]
    tm = _tile(S, 1024)
    last = N_CHIPS - 1
    row = lambda i, j: (i, 0)
    in_specs = [pl.BlockSpec((tm, ns), lambda i, j: (i, j)),
                pl.BlockSpec((1, 1, D_MODEL, ns), lambda i, j: (l, j, 0, 0)),
                pl.BlockSpec((tm, D_MODEL), row)]
    if ln is None:
        def body(dh_ref, w_ref, r_ref, o_ref):
            @pl.when(pl.program_id(1) == 0)
            def _():
                o_ref[...] = r_ref[...]

            o_ref[...] += _dot_nt(dh_ref[...], w_ref[0, 0])

        return pl.pallas_call(
            body, name=name, grid=(S // tm, N_CHIPS), in_specs=in_specs,
            out_specs=pl.BlockSpec((tm, D_MODEL), row), out_shape=SDS((S, D_MODEL), f32),
            compiler_params=_cp("parallel", "arbitrary"),
        )(dh, w13, res)

    z, g, rscale = ln

    def body_ln(dh_ref, w_ref, r_ref, z_ref, g_ref, dzb_ref, dres_ref, dg_ref, db_ref, acc_sc):
        i, j = pl.program_id(0), pl.program_id(1)

        @pl.when(j == 0)
        def _():
            acc_sc[...] = r_ref[...]

        acc_sc[...] += _dot_nt(dh_ref[...], w_ref[0, 0])

        @pl.when(j == last)
        def _():
            _ln_bwd_store(acc_sc[...], z_ref, g_ref, rscale, i == 0, dzb_ref, dres_ref, dg_ref, db_ref)

    ln_in, ln_out, ln_shape = _ln_bwd_specs(S, tm, row)
    return pl.pallas_call(
        body_ln, name=name, grid=(S // tm, N_CHIPS), in_specs=in_specs + ln_in, out_specs=ln_out, out_shape=ln_shape,
        scratch_shapes=[pltpu.VMEM((tm, D_MODEL), f32)], compiler_params=_cp("arbitrary", "arbitrary"),
    )(dh, w13, res, z, g)


def mm_nt_res(name, dys, ws, wl, res, out_dtype, ln=None):
    S = dys[0].shape[0]
    K = ws[0].shape[1]
    tm = _tile(S, 512)
    n = len(dys)
    n_in = 2 * n + (res is not None)

    def product(refs):
        acc = _dot_nt(refs[0][...], refs[n][0])
        for k in range(1, n):
            acc = acc + _dot_nt(refs[k][...], refs[n + k][0])
        if res is not None:
            acc = acc + refs[2 * n][...]
        return acc

    def body(*refs):
        refs[-1][...] = product(refs).astype(out_dtype)

    def body_ln(*refs):
        z_ref, g_ref, dzb_ref, dres_ref, dg_ref, db_ref = refs[n_in:]
        _ln_bwd_store(product(refs), z_ref, g_ref, ln[2], pl.program_id(0) == 0, dzb_ref, dres_ref, dg_ref, db_ref)

    row = lambda i: (i, 0)
    in_specs = [pl.BlockSpec((tm, d.shape[1]), row) for d in dys]
    in_specs += [pl.BlockSpec((1,) + w.shape[1:], functools.partial(lambda li, i: (li, 0, 0), li)) for w, li in zip(ws, wl)]
    args = list(dys) + list(ws)
    if res is not None:
        in_specs.append(pl.BlockSpec((tm, K), row))
        args.append(res)
    if ln is None:
        return pl.pallas_call(
            body, name=name, grid=(S // tm,), in_specs=in_specs,
            out_specs=pl.BlockSpec((tm, K), row), out_shape=SDS((S, K), out_dtype),
            compiler_params=_cp("parallel"),
        )(*args)
    ln_in, ln_out, ln_shape = _ln_bwd_specs(S, tm, row)
    return pl.pallas_call(
        body_ln, name=name, grid=(S // tm,), in_specs=in_specs + ln_in, out_specs=ln_out, out_shape=ln_shape,
        compiler_params=_cp("arbitrary"),
    )(*args, ln[0], ln[1])


def mm_tn(name, x, dy, col_shards=False, deps=()):
    S, K = x.shape
    N = dy.shape[1]
    ts = 512
    while ts * 2 <= min(S, 2048) and S % (ts * 2) == 0 and ts * 2 * K * 2 <= 6 * 2**20:
        ts *= 2
    ts = _tile(S, ts)
    if col_shards:
        tn = N // N_CHIPS
    else:
        tn = N
        while K * tn * 4 > 6 * 2**20 and tn % 256 == 0:
            tn //= 2
    nn = N // tn
    lead = ((0,) if col_shards else ()) + (slice(None), slice(None))

    def body(x_ref, dy_ref, o_ref):
        acc = _dot_tn(x_ref[...].astype(bf16), dy_ref[...].astype(bf16))

        @pl.when(pl.program_id(1) == 0)
        def _():
            o_ref[lead] = acc

        @pl.when(pl.program_id(1) != 0)
        def _():
            o_ref[lead] += acc

    if col_shards:
        out_spec = pl.BlockSpec((1, K, tn), lambda n, s: (n, 0, 0))
        out_shape = SDS((N_CHIPS, K, tn), f32)
    else:
        out_spec = pl.BlockSpec((K, tn), lambda n, s: (0, n))
        out_shape = SDS((K, N), f32)
    return pl.pallas_call(
        _with_deps(body, 2, deps), name=name, grid=(nn, S // ts),
        in_specs=[pl.BlockSpec((ts, K), lambda n, s: (s, 0)), pl.BlockSpec((ts, tn), lambda n, s: (s, n))]
        + [_DEP_SPEC] * len(deps),
        out_specs=out_spec, out_shape=out_shape, compiler_params=_cp("parallel", "arbitrary"),
    )(x, dy, *deps)


def mm_nn_shard(name, x, w, l):
    S, K = x.shape
    ns = w.shape[3]

    def body(x_ref, w_ref, o_ref):
        o_ref[...] = _dot(x_ref[...], w_ref[0, 0]).astype(bf16)

    return pl.pallas_call(
        body, name=name, grid=(N_CHIPS,),
        in_specs=[pl.BlockSpec((S, K), lambda j: (0, 0)), pl.BlockSpec((1, 1, K, ns), lambda j: (l, j, 0, 0))],
        out_specs=pl.BlockSpec((S, ns), lambda j: (0, j)), out_shape=SDS((S, N_CHIPS * ns), bf16),
        compiler_params=_cp("parallel"),
    )(x, w)


def loss_grad(name, y, t, ln):
    S = y.shape[0]
    tm = _tile(S, 512)
    z, g, rscale = ln

    def body(y_ref, t_ref, z_ref, g_ref, dzb_ref, dres_ref, dg_ref, db_ref, loss_ref):
        first = pl.program_id(0) == 0
        e = y_ref[...] - t_ref[...]
        _ln_bwd_store(e * (1.0 / D_MODEL), z_ref, g_ref, rscale, first, dzb_ref, dres_ref, dg_ref, db_ref)

        @pl.when(first)
        def _():
            loss_ref[...] = jnp.zeros_like(loss_ref)

        loss_ref[...] += jnp.full(loss_ref.shape, (0.5 / D_MODEL) * jnp.sum(e * e), f32)

    row = lambda i: (i, 0)
    ln_in, ln_out, ln_shape = _ln_bwd_specs(S, tm, row)
    return pl.pallas_call(
        body, name=name, grid=(S // tm,),
        in_specs=[pl.BlockSpec((tm, D_MODEL), row)] * 2 + ln_in,
        out_specs=ln_out + [pl.BlockSpec((8, 128), lambda i: (0, 0))],
        out_shape=ln_shape + [SDS((8, 128), f32)],
        compiler_params=_cp("arbitrary"),
    )(y, t, z, g)


def _half_sum(t):
    return t + pltpu.roll(t, 64, axis=1)


def mix_pre(name, xb, w_in, wq, wkv, l, gq, gkv, cs):
    S = xb.shape[0]
    tm = _tile(S, 512)
    H = MLA_HEADS
    W_EXT = w_in.shape[2]

    def body(x_ref, win_ref, wq_ref, wkv_ref, gq_ref, gkv_ref, cs_ref,
             u_ref, cq_ref, ckv_ref, cqn_ref, ckvn_ref, q_ref, k_ref, v_ref):
        h = _dot(x_ref[...], win_ref[0])
        u_ref[...] = h[:, :256]
        cq = h[:, 256:512]
        ckv = h[:, 512:640]
        cq_ref[...] = cq
        ckv_ref[...] = ckv
        cqn = (cq * lax.rsqrt(jnp.mean(cq * cq, axis=-1, keepdims=True) + RMS_EPS) * gq_ref[...]).astype(bf16)
        ckvn = (ckv * lax.rsqrt(jnp.mean(ckv * ckv, axis=-1, keepdims=True) + RMS_EPS) * gkv_ref[...]).astype(bf16)
        cqn_ref[...] = cqn
        ckvn_ref[...] = ckvn
        csv = cs_ref[...]
        lane = lax.broadcasted_iota(jnp.int32, (tm, 128), 1)
        kr = jnp.where(lane < 64, _half_sum(h[:, 640:768] * csv), 0.0).astype(bf16)
        kv = _dot(ckvn, wkv_ref[0])
        for hd in range(H):
            qe = _dot(cqn, wq_ref[0, hd])
            q_ref[hd, :, :128] = (qe[:, :128] * MLA_SCALE_LOG2).astype(bf16)
            q_ref[hd, :, 128:] = (_half_sum(qe[:, 128:] * csv) * MLA_SCALE_LOG2).astype(bf16)
            k_ref[hd, :, :128] = kv[:, 256 * hd:256 * hd + 128].astype(bf16)
            k_ref[hd, :, 128:] = kr
            v_ref[hd] = kv[:, 256 * hd + 128:256 * hd + 256].astype(bf16)

    row = lambda i: (i, 0)
    hrow = lambda i: (0, i, 0)
    return pl.pallas_call(
        body, name=name, grid=(S // tm,),
        in_specs=[pl.BlockSpec((tm, D_MODEL), row),
                  pl.BlockSpec((1, D_MODEL, W_EXT), lambda i: (l, 0, 0)),
                  pl.BlockSpec((1, H, Q_LORA, 256), lambda i: (l, 0, 0, 0)),
                  pl.BlockSpec((1, KV_LORA, H * 256), lambda i: (l, 0, 0)),
                  pl.BlockSpec((1, Q_LORA), lambda i: (0, 0)), pl.BlockSpec((1, KV_LORA), lambda i: (0, 0)),
                  pl.BlockSpec((tm, 128), row)],
        out_specs=[pl.BlockSpec((tm, 256), row), pl.BlockSpec((tm, Q_LORA), row), pl.BlockSpec((tm, KV_LORA), row),
                   pl.BlockSpec((tm, Q_LORA), row), pl.BlockSpec((tm, KV_LORA), row),
                   pl.BlockSpec((H, tm, 256), hrow), pl.BlockSpec((H, tm, 256), hrow), pl.BlockSpec((H, tm, 128), hrow)],
        out_shape=[SDS((S, 256), f32), SDS((S, Q_LORA), f32), SDS((S, KV_LORA), f32),
                   SDS((S, Q_LORA), bf16), SDS((S, KV_LORA), bf16),
                   SDS((H, S, 256), bf16), SDS((H, S, 256), bf16), SDS((H, S, 128), bf16)],
        compiler_params=_cp("parallel"),
    )(xb, w_in, wq, wkv, gq, gkv, cs)


def _group_select(col, a2, a4, a8, a16):
    return jnp.where(col < 64, a2, jnp.where(col < 128, a4, jnp.where(col < 192, a8, a16)))


def pool_fwd(name, u, wbd, scale):
    S = u.shape[0]
    tm = _tile(S, 512)
    hb = tm // HALO

    def body(u_ref, halo_ref, w_ref, s_ref, d_ref, y_ref):
        i = pl.program_id(0)
        cur = u_ref[...]
        halo = jnp.where(i > 0, halo_ref[...], 0.0)
        ext = jnp.concatenate([halo, cur], axis=0)
        s2 = ext + pltpu.roll(ext, 1, axis=0)
        s4 = s2 + pltpu.roll(s2, 2, axis=0)
        s8 = s4 + pltpu.roll(s4, 4, axis=0)
        s16 = s8 + pltpu.roll(s8, 8, axis=0)
        t1 = (i * tm + 1 + lax.broadcasted_iota(jnp.int32, (tm, 1), 0)).astype(f32)
        col = lax.broadcasted_iota(jnp.int32, (tm, 256), 1)
        m = _group_select(col, s2[HALO:] / jnp.minimum(t1, 2.0), s4[HALO:] / jnp.minimum(t1, 4.0),
                          s8[HALO:] / jnp.minimum(t1, 8.0), s16[HALO:] / jnp.minimum(t1, 16.0))
        d = (m - cur).astype(bf16)
        d_ref[...] = d
        y_ref[...] = (_dot(d, w_ref[...]) * s_ref[...]).astype(bf16)

    row = lambda i: (i, 0)
    return pl.pallas_call(
        body, name=name, grid=(S // tm,),
        in_specs=[pl.BlockSpec((tm, 256), row), pl.BlockSpec((HALO, 256), lambda i: (jnp.maximum(i * hb - 1, 0), 0)),
                  pl.BlockSpec((256, 256), lambda i: (0, 0)), pl.BlockSpec((1, 256), lambda i: (0, 0))],
        out_specs=[pl.BlockSpec((tm, 256), row)] * 2,
        out_shape=[SDS((S, 256), bf16), SDS((S, 256), bf16)],
        compiler_params=_cp("parallel"),
    )(u, u, wbd, scale)


def pool_bwd(name, dyp, d, wbd, scale):
    S = dyp.shape[0]
    tm = _tile(S, 512)
    hb = tm // HALO
    n_ext = tm + HALO

    def fwd_sum(e, steps):
        k = 1
        for _ in range(steps):
            e = e + pltpu.roll(e, n_ext - k, axis=0)
            k *= 2
        return e

    def body(dy_ref, halo_ref, d_ref, w_ref, s_ref, du_ref, dyw_ref, ds_ref):
        i = pl.program_id(0)
        sc = s_ref[...]
        w = w_ref[...]
        cur = dy_ref[...].astype(f32)
        halo = jnp.where(i < pl.num_programs(0) - 1, halo_ref[...].astype(f32), 0.0)
        dyw = jnp.concatenate([cur, halo], axis=0) * sc
        dyw_ref[...] = dyw[:tm].astype(bf16)
        dd = _dot_nt(dyw.astype(bf16), w)
        t1 = (i * tm + 1 + lax.broadcasted_iota(jnp.int32, (n_ext, 1), 0)).astype(f32)
        f2 = fwd_sum(dd / jnp.minimum(t1, 2.0), 1)
        f4 = fwd_sum(dd / jnp.minimum(t1, 4.0), 2)
        f8 = fwd_sum(dd / jnp.minimum(t1, 8.0), 3)
        f16 = fwd_sum(dd / jnp.minimum(t1, 16.0), 4)
        col = lax.broadcasted_iota(jnp.int32, (tm, 256), 1)
        du_ref[...] = (_group_select(col, f2[:tm], f4[:tm], f8[:tm], f16[:tm]) - dd[:tm]).astype(bf16)

        @pl.when(i == 0)
        def _():
            ds_ref[...] = jnp.zeros_like(ds_ref)

        ds_ref[...] += jnp.sum(cur * _dot(d_ref[...], w), axis=0, keepdims=True)

    row = lambda i: (i, 0)
    nhb = S // HALO
    return pl.pallas_call(
        body, name=name, grid=(S // tm,),
        in_specs=[pl.BlockSpec((tm, 256), row), pl.BlockSpec((HALO, 256), lambda i: (jnp.minimum((i + 1) * hb, nhb - 1), 0)),
                  pl.BlockSpec((tm, 256), row), pl.BlockSpec((256, 256), lambda i: (0, 0)),
                  pl.BlockSpec((1, 256), lambda i: (0, 0))],
        out_specs=[pl.BlockSpec((tm, 256), row), pl.BlockSpec((tm, 256), row), pl.BlockSpec((1, 256), lambda i: (0, 0))],
        out_shape=[SDS((S, 256), bf16), SDS((S, 256), bf16), SDS((1, 256), f32)],
        compiler_params=_cp("arbitrary"),
    )(dyp, dyp, d, wbd, scale)


def _diag_mask(r0, rn, kn):
    rc = (r0 + lax.broadcasted_iota(jnp.int32, (rn, 1), 0)) // 64
    cc = lax.broadcasted_iota(jnp.int32, (1, kn), 1) // 64
    return rc >= cc


def _diag_parts(tq):
    h = tq // 2
    return [(0, h, h), (h, h, tq)] if h % 128 == 0 else [(0, tq, tq)]


def mla_attn_fwd(name, q, k, v):
    H, S, _ = q.shape
    tq = _tile(S, 1024)
    nq = S // tq
    pairs = [(i, j) for i in range(nq) for j in range(i + 1)]
    it = jnp.asarray([p_[0] for p_ in pairs], jnp.int32)
    jt = jnp.asarray([p_[1] for p_ in pairs], jnp.int32)

    def body(it_ref, jt_ref, q_ref, k_ref, v_ref, o_ref, lse_ref, m_sc, l_sc, acc_sc):
        t = pl.program_id(1)
        i, j = it_ref[t], jt_ref[t]

        @pl.when(j == 0)
        def _():
            m_sc[...] = jnp.full_like(m_sc, NEG_INF)
            l_sc[...] = jnp.zeros_like(l_sc)
            acc_sc[...] = jnp.zeros_like(acc_sc)

        def part(r0, rn, kn, masked):
            rows, keys = slice(r0, r0 + rn), slice(0, kn)
            s = _dot_nt(q_ref[0, rows, :], k_ref[0, keys, :])
            if masked:
                s = jnp.where(_diag_mask(r0, rn, kn), s, NEG_INF)
            m_prev = m_sc[rows, :]
            m_new = jnp.maximum(m_prev, jnp.max(s, axis=-1, keepdims=True))
            p = jnp.exp2(s - jnp.tile(m_new, (1, kn // 128)))
            a = jnp.exp2(m_prev - m_new)
            l_sc[rows, :] = a * l_sc[rows, :] + jnp.sum(p, axis=-1, keepdims=True)
            acc_sc[rows, :] = a * acc_sc[rows, :] + _dot(p.astype(bf16), v_ref[0, keys, :])
            m_sc[rows, :] = m_new

        @pl.when(j < i)
        def _():
            part(0, tq, tq, False)

        @pl.when(j == i)
        def _():
            for r0, rn, kn in _diag_parts(tq):
                part(r0, rn, kn, True)
            o_ref[...] = (acc_sc[...] / l_sc[...]).astype(bf16)
            lse_ref[0] = m_sc[...] + jnp.log2(l_sc[...])

    return pl.pallas_call(
        body, name=name,
        grid_spec=pltpu.PrefetchScalarGridSpec(
            num_scalar_prefetch=2, grid=(H, len(pairs)),
            in_specs=[pl.BlockSpec((1, tq, 256), lambda h, t, it_, jt_: (h, it_[t], 0)),
                      pl.BlockSpec((1, tq, 256), lambda h, t, it_, jt_: (h, jt_[t], 0)),
                      pl.BlockSpec((1, tq, 128), lambda h, t, it_, jt_: (h, jt_[t], 0))],
            out_specs=[pl.BlockSpec((tq, 128), lambda h, t, it_, jt_: (it_[t], h)),
                       pl.BlockSpec((1, tq, 128), lambda h, t, it_, jt_: (h, it_[t], 0))],
            scratch_shapes=[pltpu.VMEM((tq, 128), f32), pltpu.VMEM((tq, 128), f32), pltpu.VMEM((tq, 128), f32)]),
        out_shape=[SDS((S, H * 128), bf16), SDS((H, S, 128), f32)],
        compiler_params=_cp("parallel", "arbitrary"),
    )(it, jt, q, k, v)


def mla_attn_bwd(name, q, k, v, o, do, lse, do_col=0):
    H, S, _ = q.shape
    tq = _tile(S, 1024)
    nq = S // tq
    pairs = [(i, j) for j in range(nq) for i in range(j, nq)]
    it = jnp.asarray([p_[0] for p_ in pairs], jnp.int32)
    jt = jnp.asarray([p_[1] for p_ in pairs], jnp.int32)
    n_pairs = len(pairs)

    def body(it_ref, jt_ref, q_ref, k_ref, v_ref, o_ref, do_ref, lse_ref, dq_ref, dk_ref, dv_ref, dq_sc, dk_sc, dv_sc):
        t = pl.program_id(1)
        i, j = it_ref[t], jt_ref[t]

        @pl.when(t == 0)
        def _():
            dq_sc[...] = jnp.zeros_like(dq_sc)

        @pl.when(i == j)
        def _():
            dk_sc[...] = jnp.zeros_like(dk_sc)
            dv_sc[...] = jnp.zeros_like(dv_sc)

        def part(r0, rn, kn, masked):
            rows, keys = slice(r0, r0 + rn), slice(0, kn)
            qv, kv_, dov = q_ref[0, rows, :], k_ref[0, keys, :], do_ref[rows, :]
            s = _dot_nt(qv, kv_)
            if masked:
                s = jnp.where(_diag_mask(r0, rn, kn), s, NEG_INF)
            p = jnp.exp2(s - jnp.tile(lse_ref[0, rows, :], (1, kn // 128)))
            dv_sc[keys, :] += _dot_tn(p.astype(bf16), dov)
            dp = _dot_nt(dov, v_ref[0, keys, :])
            delta = jnp.sum(dov.astype(f32) * o_ref[rows, :].astype(f32), axis=-1, keepdims=True)
            ds = (p * (dp - delta)).astype(bf16)
            dk_sc[keys, :] += _dot_tn(ds, qv)
            dq_rows = pl.ds(pl.multiple_of(i * tq + r0, 128), rn)
            dq_sc[dq_rows, :] += _dot(ds, kv_)

        @pl.when(i > j)
        def _():
            part(0, tq, tq, False)

        @pl.when(i == j)
        def _():
            for r0, rn, kn in _diag_parts(tq):
                part(r0, rn, kn, True)

        @pl.when(i == nq - 1)
        def _():
            dk_ref[0] = (dk_sc[...] * math.log(2.0)).astype(bf16)
            dv_ref[0] = dv_sc[...].astype(bf16)

        @pl.when(t == n_pairs - 1)
        def _():
            dq_ref[0] = (dq_sc[...] * MLA_SCALE).astype(bf16)

    qi = lambda h, t, it_, jt_: (h, it_[t], 0)
    kj = lambda h, t, it_, jt_: (h, jt_[t], 0)
    oi = lambda h, t, it_, jt_: (it_[t], h)
    doi = lambda h, t, it_, jt_: (it_[t], h + do_col)
    return pl.pallas_call(
        body, name=name,
        grid_spec=pltpu.PrefetchScalarGridSpec(
            num_scalar_prefetch=2, grid=(H, n_pairs),
            in_specs=[pl.BlockSpec((1, tq, 256), qi), pl.BlockSpec((1, tq, 256), kj), pl.BlockSpec((1, tq, 128), kj),
                      pl.BlockSpec((tq, 128), oi), pl.BlockSpec((tq, 128), doi), pl.BlockSpec((1, tq, 128), qi)],
            out_specs=[pl.BlockSpec((1, S, 256), lambda h, t, it_, jt_: (h, 0, 0)), pl.BlockSpec((1, tq, 256), kj),
                       pl.BlockSpec((1, tq, 128), kj)],
            scratch_shapes=[pltpu.VMEM((S, 256), f32), pltpu.VMEM((tq, 256), f32), pltpu.VMEM((tq, 128), f32)]),
        out_shape=[SDS((H, S, 256), bf16), SDS((H, S, 256), bf16), SDS((H, S, 128), bf16)],
        compiler_params=_cp("parallel", "arbitrary"),
    )(it, jt, q, k, v, o, do, lse)


def mix_post_bwd(name, dq, dk, dv, wq, wkv, l, cq, ckv, gq, gkv, cs):
    H, S, _ = dq.shape
    tm = _tile(S, 512)

    def rms_bwd(dyn, c, g):
        r = lax.rsqrt(jnp.mean(c * c, axis=-1, keepdims=True) + RMS_EPS)
        ch = c * r
        dyg = dyn * g
        dc = r * (dyg - ch * jnp.mean(dyg * ch, axis=-1, keepdims=True))
        return dc, jnp.sum(dyn * ch, axis=0, keepdims=True)

    def body(dq_ref, dk_ref, dv_ref, wq_ref, wkv_ref, cq_ref, ckv_ref, gq_ref, gkv_ref, cs_ref,
             dqe_ref, dkv_ref, dh_ref, dgq_ref, dgkv_ref):
        csv = cs_ref[...]
        lane = lax.broadcasted_iota(jnp.int32, (tm, 128), 1)
        dcqn = jnp.zeros((tm, Q_LORA), f32)
        dkr = jnp.zeros((tm, 128), f32)
        for hd in range(H):
            dqh = dq_ref[hd].astype(f32)
            dqe = jnp.concatenate([dqh[:, :128], _half_sum(dqh[:, 128:]) * csv], axis=1).astype(bf16)
            dqe_ref[:, 256 * hd:256 * hd + 256] = dqe
            dcqn = dcqn + _dot_nt(dqe, wq_ref[0, hd])
            dkh = dk_ref[hd].astype(f32)
            dkv_ref[:, 256 * hd:256 * hd + 128] = dkh[:, :128].astype(bf16)
            dkv_ref[:, 256 * hd + 128:256 * hd + 256] = dv_ref[hd].astype(bf16)
            dkr = dkr + dkh[:, 128:]
        dckvn = _dot_nt(dkv_ref[...], wkv_ref[0])
        dblk = _half_sum(jnp.where(lane < 64, dkr, 0.0)) * csv
        dcq, dgq = rms_bwd(dcqn, cq_ref[...], gq_ref[...])
        dckv, dgkv = rms_bwd(dckvn, ckv_ref[...], gkv_ref[...])
        dh_ref[:, :256] = dcq.astype(bf16)
        dh_ref[:, 256:384] = dckv.astype(bf16)
        dh_ref[:, 384:] = dblk.astype(bf16)

        @pl.when(pl.program_id(0) == 0)
        def _():
            dgq_ref[...] = jnp.zeros_like(dgq_ref)
            dgkv_ref[...] = jnp.zeros_like(dgkv_ref)

        dgq_ref[...] += dgq
        dgkv_ref[...] += dgkv

    row = lambda i: (i, 0)
    hrow = lambda i: (0, i, 0)
    return pl.pallas_call(
        body, name=name, grid=(S // tm,),
        in_specs=[pl.BlockSpec((H, tm, 256), hrow), pl.BlockSpec((H, tm, 256), hrow), pl.BlockSpec((H, tm, 128), hrow),
                  pl.BlockSpec((1, H, Q_LORA, 256), lambda i: (l, 0, 0, 0)),
                  pl.BlockSpec((1, KV_LORA, H * 256), lambda i: (l, 0, 0)),
                  pl.BlockSpec((tm, Q_LORA), row), pl.BlockSpec((tm, KV_LORA), row),
                  pl.BlockSpec((1, Q_LORA), lambda i: (0, 0)), pl.BlockSpec((1, KV_LORA), lambda i: (0, 0)),
                  pl.BlockSpec((tm, 128), row)],
        out_specs=[pl.BlockSpec((tm, H * 256), row), pl.BlockSpec((tm, H * 256), row), pl.BlockSpec((tm, 512), row),
                   pl.BlockSpec((1, Q_LORA), lambda i: (0, 0)), pl.BlockSpec((1, KV_LORA), lambda i: (0, 0))],
        out_shape=[SDS((S, H * 256), bf16), SDS((S, H * 256), bf16), SDS((S, 512), bf16),
                   SDS((1, Q_LORA), f32), SDS((1, KV_LORA), f32)],
        compiler_params=_cp("arbitrary"),
    )(dq, dk, dv, wq, wkv, cq, ckv, gq, gkv, cs)


def _cross_probs(qb, kv_ref, hd):
    cols = slice(hd * MEM_HEAD_DIM, (hd + 1) * MEM_HEAD_DIM)
    s = _dot_nt(qb[:, cols], kv_ref[:, cols]) * MEM_SCALE
    e = jnp.exp(s - jnp.max(s, axis=-1, keepdims=True))
    return e / jnp.sum(e, axis=-1, keepdims=True)


def cross_fwd(name, xb, xf, wq, wo, l, kv, g, b):
    S = xb.shape[0]
    tm = _tile(S, 512)
    M = kv.shape[0]

    def body(x_ref, xf_ref, wq_ref, wo_ref, k_ref, v_ref, g_ref, b_ref, q_ref, o_ref, z_ref, y_ref, yb_ref):
        qb = _dot(x_ref[...], wq_ref[0]).astype(bf16)
        q_ref[...] = qb
        for hd in range(MEM_HEADS):
            cols = slice(hd * MEM_HEAD_DIM, (hd + 1) * MEM_HEAD_DIM)
            p = _cross_probs(qb, k_ref, hd)
            o_ref[:, cols] = _dot(p.astype(bf16), v_ref[:, cols]).astype(bf16)
        z = ALPHA * xf_ref[...] + _dot(o_ref[...], wo_ref[0])
        mu = jnp.mean(z, axis=-1, keepdims=True)
        zc = z - mu
        var = jnp.mean(zc * zc, axis=-1, keepdims=True)
        y = zc * lax.rsqrt(var + LN_EPS) * g_ref[...] + b_ref[...]
        z_ref[...] = z
        y_ref[...] = y
        yb_ref[...] = y.astype(bf16)

    row = lambda i: (i, 0)
    wspec = pl.BlockSpec((1, D_MODEL, D_MODEL), lambda i: (l, 0, 0))
    vec = pl.BlockSpec((1, D_MODEL), lambda i: (0, 0))
    blk = pl.BlockSpec((tm, D_MODEL), row)
    return pl.pallas_call(
        body, name=name, grid=(S // tm,),
        in_specs=[blk, blk, wspec, wspec, pl.BlockSpec((M, D_MODEL), lambda i: (0, 0)),
                  pl.BlockSpec((M, D_MODEL), lambda i: (0, 1)), vec, vec],
        out_specs=[blk] * 5,
        out_shape=[SDS((S, D_MODEL), bf16), SDS((S, D_MODEL), bf16), SDS((S, D_MODEL), f32), SDS((S, D_MODEL), f32),
                   SDS((S, D_MODEL), bf16)],
        compiler_params=_cp("parallel"),
    )(xb, xf, wq, wo, kv, kv, g, b)


def cross_bwd(name, dzb, wo, l, qb, kv, deps=()):
    S = dzb.shape[0]
    tm = _tile(S, 512)
    M = kv.shape[0]

    def body(dz_ref, wo_ref, q_ref, k_ref, v_ref, dq_ref, dkv_ref):
        @pl.when(pl.program_id(0) == 0)
        def _():
            dkv_ref[...] = jnp.zeros_like(dkv_ref)

        do = _dot_nt(dz_ref[...], wo_ref[0]).astype(bf16)
        qv = q_ref[...]
        for hd in range(MEM_HEADS):
            cols = slice(hd * MEM_HEAD_DIM, (hd + 1) * MEM_HEAD_DIM)
            vcols = slice(D_MODEL + hd * MEM_HEAD_DIM, D_MODEL + (hd + 1) * MEM_HEAD_DIM)
            p = _cross_probs(qv, k_ref, hd)
            doh = do[:, cols]
            dkv_ref[:, vcols] += _dot_tn(p.astype(bf16), doh)
            dp = _dot_nt(doh, v_ref[:, cols])
            ds = (p * (dp - jnp.sum(dp * p, axis=-1, keepdims=True)) * MEM_SCALE).astype(bf16)
            dq_ref[:, cols] = _dot(ds, k_ref[:, cols]).astype(bf16)
            dkv_ref[:, cols] += _dot_tn(ds, qv[:, cols])

    row = lambda i: (i, 0)
    blk = pl.BlockSpec((tm, D_MODEL), row)
    return pl.pallas_call(
        _with_deps(body, 5, deps), name=name, grid=(S // tm,),
        in_specs=[blk, pl.BlockSpec((1, D_MODEL, D_MODEL), lambda i: (l, 0, 0)), blk,
                  pl.BlockSpec((M, D_MODEL), lambda i: (0, 0)), pl.BlockSpec((M, D_MODEL), lambda i: (0, 1))]
        + [_DEP_SPEC] * len(deps),
        out_specs=[blk, pl.BlockSpec((M, 2 * D_MODEL), lambda i: (0, 0))],
        out_shape=[SDS((S, D_MODEL), bf16), SDS((M, 2 * D_MODEL), f32)],
        compiler_params=_cp("arbitrary"),
    )(dzb, wo, qb, kv, kv, *deps)


def adamw(name, w, g, m, v, deps=()):
    shape = w.shape
    cols = shape[-1]
    rows = math.prod(shape[:-1])
    tr = _row_tile(rows, cols, target=2 * 2**20)
    c1 = 1.0 - ADAM_B1 ** ADAM_STEP
    c2 = 1.0 - ADAM_B2 ** ADAM_STEP

    def body(w_ref, g_ref, m_ref, v_ref, d_ref, nm_ref, nv_ref):
        gv = g_ref[...]
        nm = ADAM_B1 * m_ref[...] + (1.0 - ADAM_B1) * gv
        nv = ADAM_B2 * v_ref[...] + (1.0 - ADAM_B2) * (gv * gv)
        d_ref[...] = -ADAM_LR * ((nm / c1) / (jnp.sqrt(nv / c2) + ADAM_EPS) + ADAM_WD * w_ref[...])
        nm_ref[...] = nm
        nv_ref[...] = nv

    blk = pl.BlockSpec((tr, cols), lambda i: (i, 0))
    flat = SDS((rows, cols), f32)
    outs = pl.pallas_call(
        _with_deps(body, 4, deps), name=name, grid=(rows // tr,), in_specs=[blk] * 4 + [_DEP_SPEC] * len(deps),
        out_specs=[blk] * 3, out_shape=[flat] * 3, compiler_params=_cp("parallel"),
    )(*[a.reshape(rows, cols) for a in (w, g, m, v)], *deps)
    return [o.reshape(shape) for o in outs]


def _me():
    return lax.axis_index("x"), lax.axis_index("y"), lax.axis_index("c")


def _other_chips(x, y):
    return [(1 - x, y), (x, 1 - y), (1 - x, 1 - y)]


def _pair_share_each(owners, bufs, sems, mine, act):
    x, y, c = _me()
    for o in range(2):
        slots = [(a, lyr) for a in range(len(bufs)) for lyr in range(DEPTH) if owners[a][lyr] == o]

        @pl.when((c == o) if mine else (c != o))
        def _(slots=slots):
            for a, lyr in slots:
                slot = bufs[a].at[lyr]
                act(_rcopy(slot, slot, sems[0].at[2 * a + lyr], sems[1].at[2 * a + lyr], (x, y, 1 - c)))


def pair_share_start(name, sums, owners, after):
    def body_fn(b_in, s_in, s_out):
        _pair_share_each(owners, b_in, s_out, True, lambda cp: cp.start())

    outs, sems, token = _split_call(name, body_fn, list(sums), [], [2 * len(sums)] * 2, after)
    return (outs, sems[0], sems[1], owners), token


def pair_share_wait(name, st, after):
    bufs, send, recv, owners = st

    def body_fn(b_in, s_in, s_out):
        _pair_share_each(owners, b_in, s_in, True, lambda cp: cp.wait_send())
        _pair_share_each(owners, b_in, s_in, False, lambda cp: cp.wait_recv())

    outs, _, _ = _split_call(name, body_fn, list(bufs), [send, recv], [], after)
    return outs


def allsum_small(name, v, deps=()):
    R = v.shape[0]

    def body(v_ref, o_ref, all_ref, send_sems, recv_sems, local_sem):
        x, y, c = _me()
        me, sibling = (x, y, c), (x, y, 1 - c)
        chips = _other_chips(x, y)

        def rows(px, py, pc):
            return all_ref.at[4 * px + 2 * py + pc]

        def copy(k, block, to, src=None):
            return pltpu.make_async_remote_copy(
                src_ref=rows(*block) if src is None else src, dst_ref=rows(*block),
                send_sem=send_sems.at[k], recv_sem=recv_sems.at[k], device_id=to, device_id_type=MESH)

        mine = pltpu.make_async_copy(v_ref, rows(*me), local_sem)
        mine.start()
        first = [copy(0, me, sibling, src=v_ref)]
        first += [copy(1 + j, me, (*chip, c), src=v_ref) for j, chip in enumerate(chips)]
        for cp in first:
            cp.start()
        passed = [copy(4 + j, (*chip, c), sibling) for j, chip in enumerate(chips)]
        for j, chip in enumerate(chips):
            copy(1 + j, (*chip, c), me).wait_recv()
            passed[j].start()
        copy(0, sibling, me).wait_recv()
        for j, chip in enumerate(chips):
            copy(4 + j, (*chip, 1 - c), me).wait_recv()
        for cp in first + passed:
            cp.wait_send()
        mine.wait()
        acc = all_ref[0]
        for d in range(1, 8):
            acc = acc + all_ref[d]
        o_ref[...] = acc

    return pl.pallas_call(
        _with_deps(body, 1, deps), name=name,
        in_specs=[pl.BlockSpec(memory_space=pltpu.VMEM)] + [_DEP_SPEC] * len(deps),
        out_specs=pl.BlockSpec(memory_space=pltpu.VMEM),
        out_shape=SDS((R, 128), f32),
        scratch_shapes=[pltpu.VMEM((8, R, 128), f32), pltpu.SemaphoreType.DMA((7,)), pltpu.SemaphoreType.DMA((7,)),
                        pltpu.SemaphoreType.DMA],
        compiler_params=pltpu.CompilerParams(vmem_limit_bytes=V7X_VMEM_LIMIT),
    )(v, *deps)


def _swap_half(r):
    return jnp.concatenate([-r[..., 32:], r[..., :32]], axis=-1)


def _unswap_add(p, qg):
    return p + jnp.concatenate([qg[..., 32:], -qg[..., :32]], axis=-1)


def _block_diag(pw):
    L = pw.shape[0]
    out = jnp.zeros((L, 256, 256), pw.dtype)
    for gi in range(4):
        out = out.at[:, 64 * gi:64 * gi + 64, 64 * gi:64 * gi + 64].set(pw[:, gi])
    return out


def _to_col_shards(w):
    *lead, K, N = w.shape
    nl = len(lead)
    return w.reshape(*lead, K, N_CHIPS, N // N_CHIPS).transpose(*range(nl), nl + 1, nl, nl + 2)


def _from_col_shards(w):
    *lead, C, K, n = w.shape
    nl = len(lead)
    return w.transpose(*range(nl), nl + 1, nl, nl + 2).reshape(*lead, K, C * n)


_HBM_SPEC = pl.BlockSpec(memory_space=pltpu.HBM)
_SEM_SPEC = pl.BlockSpec(memory_space=pltpu.SEMAPHORE)
_ANY_SPEC = pl.BlockSpec(memory_space=pl.ANY)
_DATAFLOW = pltpu.SideEffectType.DATAFLOW_SIDE_EFFECTING


def _split_call(name, body_fn, bufs, sems_in, sems_out_sizes, after):
    nb, ni, no = len(bufs), len(sems_in), len(sems_out_sizes)
    afters = () if after is None else tuple(after) if isinstance(after, (tuple, list)) else (after,)

    def body(*refs):
        k = nb + ni + len(afters)
        body_fn(refs[:nb], refs[nb:nb + ni], refs[k:k + no])
        refs[-1][...] = jnp.zeros((8, 128), f32)

    outs = pl.pallas_call(
        body, name=name,
        in_specs=[_HBM_SPEC] * nb + [_SEM_SPEC] * ni + [_ANY_SPEC] * len(afters),
        out_specs=[_SEM_SPEC] * no + [_HBM_SPEC] * nb + [pl.BlockSpec(memory_space=pltpu.VMEM)],
        out_shape=[pltpu.SemaphoreType.DMA((s,)) for s in sems_out_sizes]
        + [pltpu.HBM(b.shape, b.dtype) for b in bufs] + [SDS((8, 128), f32)],
        input_output_aliases={i: no + i for i in range(nb)},
        compiler_params=pltpu.CompilerParams(has_side_effects=_DATAFLOW),
    )(*[pltpu.with_memory_space_constraint(b, pltpu.HBM) for b in bufs], *sems_in, *afters)
    return list(outs[no:no + nb]), list(outs[:no]), outs[-1]


def _rcopy(src, dst, ssem, rsem, to):
    return pltpu.make_async_remote_copy(src_ref=src, dst_ref=dst, send_sem=ssem, recv_sem=rsem, device_id=to,
                                        device_id_type=MESH)


def gather_start(name, groups, after):
    flat = [b for bufs, _ in groups for b in bufs]
    sizes = [3 * len(bufs) for bufs, _ in groups for _ in range(2)]

    def body_fn(b_in, s_in, s_out):
        x, y, c = _me()
        q = 2 * x + y
        chips = _other_chips(x, y)
        pos = 0
        for gi, (bufs, owner) in enumerate(groups):
            refs = b_in[pos:pos + len(bufs)]
            pos += len(bufs)

            @pl.when(c == owner)
            def _(refs=refs, send=s_out[2 * gi], recv=s_out[2 * gi + 1]):
                for a, r in enumerate(refs):
                    for k, (cx, cy) in enumerate(chips):
                        _rcopy(r.at[q], r.at[q], send.at[3 * a + k], recv.at[3 * a + k], (cx, cy, c)).start()

    outs, sems, token = _split_call(name, body_fn, flat, [], sizes, after)
    res, pos = [], 0
    for gi, (bufs, owner) in enumerate(groups):
        res.append((outs[pos:pos + len(bufs)], sems[2 * gi], sems[2 * gi + 1], owner))
        pos += len(bufs)
    return res, token


def gather_forward(name, grp, after):
    bufs, send, recv, owner = grp
    n3 = 3 * len(bufs)

    def body_fn(b_in, s_in, s_out):
        x, y, c = _me()
        q = 2 * x + y
        sibling = (x, y, 1 - c)
        chips = _other_chips(x, y)

        @pl.when(c == owner)
        def _():
            for a, r in enumerate(b_in):
                for k, (cx, cy) in enumerate(chips):
                    i = 3 * a + k
                    land = r.at[2 * cx + cy]
                    _rcopy(r.at[q], r.at[q], s_in[0].at[i], s_in[1].at[i], (cx, cy, c)).wait_send()
                    _rcopy(land, land, s_in[0].at[i], s_in[1].at[i], (cx, cy, c)).wait_recv()
                    _rcopy(land, land, s_out[0].at[i], s_out[1].at[i], sibling).start()

    outs, sems, token = _split_call(name, body_fn, bufs, [send, recv], [n3, n3], after)
    return (outs, sems[0], sems[1], owner), token


def gather_finish(name, grp, after):
    bufs, fsend, frecv, owner = grp

    def body_fn(b_in, s_in, s_out):
        x, y, c = _me()
        sibling = (x, y, 1 - c)
        chips = _other_chips(x, y)

        def each(wait):
            for a, r in enumerate(b_in):
                for k, (cx, cy) in enumerate(chips):
                    land = r.at[2 * cx + cy]
                    wait(_rcopy(land, land, s_in[0].at[3 * a + k], s_in[1].at[3 * a + k], sibling))

        @pl.when(c == owner)
        def _():
            each(lambda cp: cp.wait_send())

        @pl.when(c != owner)
        def _():
            each(lambda cp: cp.wait_recv())

    outs, _, _ = _split_call(name, body_fn, bufs, [fsend, frecv], [], after)
    return outs


def _by_owner(owners):
    return [[a for a, o_ in enumerate(owners) if o_ == o] for o in range(2)]


def pair_send_start(name, gs, owners, after):
    n = len(gs)
    lands = [lax.empty(g.shape, g.dtype) for g in gs]

    def body_fn(b_in, s_in, s_out):
        x, y, c = _me()
        for o, idx in enumerate(_by_owner(owners)):
            @pl.when(c == 1 - o)
            def _(o=o, idx=idx):
                for a in idx:
                    _rcopy(b_in[a], b_in[n + a], s_out[0].at[a], s_out[1].at[a], (x, y, o)).start()

    outs, sems, token = _split_call(name, body_fn, list(gs) + lands, [], [n, n], after)
    return (outs[:n], outs[n:], sems[0], sems[1], owners), token


def pair_send_wait(name, st, after):
    gs, lands, send, recv, owners = st
    n = len(gs)

    def body_fn(b_in, s_in, s_out):
        x, y, c = _me()
        for o, idx in enumerate(_by_owner(owners)):
            @pl.when(c == 1 - o)
            def _(o=o, idx=idx):
                for a in idx:
                    _rcopy(b_in[a], b_in[n + a], s_in[0].at[a], s_in[1].at[a], (x, y, o)).wait_send()

            @pl.when(c == o)
            def _(o=o, idx=idx):
                for a in idx:
                    _rcopy(b_in[a], b_in[n + a], s_in[0].at[a], s_in[1].at[a], (x, y, 1 - o)).wait_recv()

    outs, _, _ = _split_call(name, body_fn, list(gs) + list(lands), [send, recv], [], after)
    return outs[:n], outs[n:]


def chip_exchange_start(name, psums, owners, after):
    n = len(psums)
    lands = [lax.empty((3,) + p.shape[1:], p.dtype) for p in psums]

    def body_fn(b_in, s_in, s_out):
        x, y, c = _me()
        chips = _other_chips(x, y)
        for o, idx in enumerate(_by_owner(owners)):
            @pl.when(c == o)
            def _(idx=idx):
                for a in idx:
                    for k, (cx, cy) in enumerate(chips):
                        _rcopy(b_in[a].at[2 * cx + cy], b_in[n + a].at[k], s_out[0].at[3 * a + k],
                               s_out[1].at[3 * a + k], (cx, cy, c)).start()

    outs, sems, token = _split_call(name, body_fn, list(psums) + lands, [], [3 * n, 3 * n], after)
    return (outs[:n], outs[n:], sems[0], sems[1], owners), token


def chip_exchange_wait(name, st, after):
    psums, lands, send, recv, owners = st
    n = len(psums)

    def body_fn(b_in, s_in, s_out):
        x, y, c = _me()
        chips = _other_chips(x, y)
        for o, idx in enumerate(_by_owner(owners)):
            @pl.when(c == o)
            def _(idx=idx):
                for a in idx:
                    for k, (cx, cy) in enumerate(chips):
                        cp = _rcopy(b_in[a].at[2 * cx + cy], b_in[n + a].at[k], s_in[0].at[3 * a + k],
                                    s_in[1].at[3 * a + k], (cx, cy, c))
                        cp.wait_send()
                        cp.wait_recv()

    outs, _, _ = _split_call(name, body_fn, list(psums) + list(lands), [send, recv], [], after)
    return outs[:n], outs[n:]


def pair_sum(name, g, recv, flag):
    shape = g.shape
    cols = shape[-1]
    rows = math.prod(shape[:-1])
    tr = _row_tile(rows, cols, target=4 * 2**20)

    def body(f_ref, g_ref, r_ref, o_ref):
        o_ref[...] = (g_ref[...] + r_ref[...]).astype(bf16)

    blk = pl.BlockSpec((tr, cols), lambda i, f_ref: (i * f_ref[0], 0))
    out = pl.pallas_call(
        body, name=name,
        grid_spec=pltpu.PrefetchScalarGridSpec(num_scalar_prefetch=1, grid=(rows // tr,), in_specs=[blk, blk],
                                               out_specs=blk),
        out_shape=SDS((rows, cols), bf16), compiler_params=_cp("arbitrary"),
    )(flag, g.reshape(rows, cols), recv.reshape(rows, cols))
    return out.reshape(shape)


def chip_sum(name, psum, recv, qf_arr, layer, prev):
    shard = psum.shape[1:]
    cols = shard[-1]
    rows = math.prod(shard[:-1])
    tr = _row_tile(rows, cols, target=4 * 2**20)

    def body(qf_ref, p_ref, r_ref, *rest):
        rest[-1][0] = ((p_ref[0].astype(f32) + r_ref[0].astype(f32)) + r_ref[1].astype(f32)) + r_ref[2].astype(f32)

    in_specs = [pl.BlockSpec((1, tr, cols), lambda i, qf: (qf[0], i * qf[1], 0)),
                pl.BlockSpec((3, tr, cols), lambda i, qf: (0, i * qf[1], 0))]
    args = [qf_arr, psum.reshape(N_CHIPS, rows, cols), recv.reshape(3, rows, cols)]
    aliases = {}
    if prev is not None:
        in_specs.append(pl.BlockSpec(memory_space=pl.ANY))
        args.append(prev.reshape(DEPTH, rows, cols))
        aliases = {3: 0}
    out = pl.pallas_call(
        body, name=name,
        grid_spec=pltpu.PrefetchScalarGridSpec(
            num_scalar_prefetch=1, grid=(rows // tr,), in_specs=in_specs,
            out_specs=pl.BlockSpec((1, tr, cols), lambda i, qf: (layer, i * qf[1], 0))),
        out_shape=SDS((DEPTH, rows, cols), f32), input_output_aliases=aliases, compiler_params=_cp("arbitrary"),
    )(*args)
    return out.reshape((DEPTH,) + shard)


W_NAMES = ("f1w13", "f1w2", "win", "wuq", "wukv", "wout", "mwq", "mwkv", "mwo", "f2w13", "f2w2")
MIX_NAMES = ("win", "wuq", "wukv")
MID_NAMES = ("wout", "mwq", "mwkv", "mwo")
FFN2_NAMES = ("f2w13", "f2w2")
REDUCER = (dict(f1w13=0, f1w2=1, f2w13=0, win=0, wuq=0, wukv=0, f2w2=1, mwkv=1, wout=1, mwq=1, mwo=1),
           dict(f1w13=0, f2w2=0, mwkv=0, wout=0, f2w13=1, f1w2=1, mwq=1, mwo=1, win=1, wuq=1, wukv=1))


def kernel(x, mem, positions, ln_g, ln_b, ffn1_w13, ffn1_w2, w_in, pool_w, pool_scale, q_norm_g, w_uq, kv_norm_g, w_ukv, w_out, mem_wq, mem_wkv, mem_wo, ffn2_w13, ffn2_w2, loss_target, m_ln_g, m_ln_b, m_ffn1_w13, m_ffn1_w2, m_w_in, m_pool_w, m_pool_scale, m_q_norm_g, m_w_uq, m_kv_norm_g, m_w_ukv, m_w_out, m_mem_wq, m_mem_wkv, m_mem_wo, m_ffn2_w13, m_ffn2_w2, v_ln_g, v_ln_b, v_ffn1_w13, v_ffn1_w2, v_w_in, v_pool_w, v_pool_scale, v_q_norm_g, v_w_uq, v_kv_norm_g, v_w_ukv, v_w_out, v_mem_wq, v_mem_wkv, v_mem_wo, v_ffn2_w13, v_ffn2_w2):
    L = DEPTH
    qx, qy, _ = _me()
    chip = 2 * qx + qy
    vec = lambda a: a.reshape(1, -1)

    shards = dict(zip(W_NAMES, (ffn1_w13, ffn1_w2, w_in, w_uq, w_ukv, w_out, mem_wq, mem_wkv, mem_wo, ffn2_w13, ffn2_w2)))

    def place(sh, slot):
        return lax.dynamic_update_slice(lax.empty((N_CHIPS,) + sh.shape, bf16), sh.astype(bf16)[None],
                                        (slot,) + (0,) * sh.ndim)

    first = ("f1w13", "f1w2")
    bufs = [dict(), dict()]
    for n in first:
        bufs[0][n] = place(shards[n][0], chip)
    gw = [dict(), dict()]
    (g0,), tok = gather_start("gather_a_start", [([bufs[0][n] for n in first], 0)], None)
    chip_then = chip + tok[0, 0].astype(jnp.int32)
    for l in range(L):
        for n in W_NAMES:
            if n not in bufs[l]:
                bufs[l][n] = place(shards[n][l], chip_then)
    others = tuple(bufs[l][n] for l in range(L) for n in W_NAMES if (l, n) not in ((0, first[0]), (0, first[1])))
    g0, tok = gather_forward("gather_a_forward", g0, others)

    ln_pad = jnp.zeros((2, L, 4, N_CHIPS, D_MODEL // N_CHIPS), f32)
    ln_pad = lax.dynamic_update_slice(ln_pad, jnp.stack([ln_g, ln_b])[:, :, :, None, :], (0, 0, 0, chip, 0))
    ln_sum = allsum_small("allsum_ln", ln_pad.reshape(-1, 128), (tok,))
    ln_full = (ln_sum * 0.5).reshape(2, L, 4, D_MODEL)
    lng, lnb = ln_full[0], ln_full[1]

    gw[0]["f1w13"], gw[0]["f1w2"] = gather_finish("gather_a_finish", g0, ln_sum)
    (g_mix, g_mid, g_ffn2, g_l1), tok_b = gather_start(
        "gather_b_start",
        [([bufs[0][n] for n in MIX_NAMES], 0), ([bufs[0][n] for n in MID_NAMES], 0), ([bufs[0][n] for n in FFN2_NAMES], 0),
         ([bufs[1][n] for n in W_NAMES], 1)], ln_sum)

    half = QK_ROPE // 2
    inv_freq = ROPE_BASE ** (-jnp.arange(half, dtype=f32) / half)
    ang = positions[0].astype(f32)[:, None] * inv_freq
    cos, sin = jnp.cos(ang), jnp.sin(ang)
    cs = jnp.concatenate([cos, cos, sin, sin], axis=-1)

    memb = mem[0].astype(bf16)
    xf = x[0]
    xb = xf.astype(bf16)
    dep = (tok_b,)

    saved, W = [], [None, None]
    for l in range(L):
        sv = {}
        if l == 1:
            gl1 = gather_finish("gather_l1_finish", g_l1, xb)
            gw[1] = dict(zip(W_NAMES, gl1))
        sv["x0b"] = xb
        f1w13 = gw[l]["f1w13"][None]
        gate, up, act = ffn_up(f"ffn1_up_{l}", xb, f1w13, 0, dep)
        dep = ()
        if l == 0:
            g_mix, _ = gather_forward("gather_mix_forward", g_mix, act)
        z1, x1f, x1b = proj_res_ln(f"ffn1_down_{l}", [act], [gw[l]["f1w2"].reshape(1, D_FF, D_MODEL)], [0], xf,
                                   vec(lng[l, 0]), vec(lnb[l, 0]), 0.5)
        sv.update(gate1=gate, up1=up, act1=act, z1=z1, x1b=x1b)
        if l == 0:
            gw[0].update(zip(MIX_NAMES, gather_finish("gather_mix_finish", g_mix, x1b)))
            g_mid, _ = gather_forward("gather_mid_forward", g_mid, x1b)
        win = gw[l]["win"].reshape(D_MODEL, D_IN)
        win_ext = jnp.concatenate([win, _swap_half(win[:, D_IN - QK_ROPE:])], axis=-1)[None]
        wuq = _from_col_shards(gw[l]["wuq"]).reshape(Q_LORA, MLA_HEADS, QK_NOPE + QK_ROPE)
        wq_ext = jnp.concatenate([wuq, _swap_half(wuq[..., QK_NOPE:])], axis=-1).transpose(1, 0, 2)[None]
        wukv = _from_col_shards(gw[l]["wukv"])[None]
        wbd = _block_diag(pool_w[l][None].astype(bf16))[0]
        u, cq, ckv, cqn, ckvn, q, k, v = mix_pre(f"mix_pre_{l}", x1b, win_ext, wq_ext, wukv, 0,
                                                   vec(q_norm_g[l]), vec(kv_norm_g[l]), cs)
        dpool, ypool = pool_fwd(f"pool_fwd_{l}", u, wbd, vec(pool_scale[l]))
        o, lse = mla_attn_fwd(f"mla_fwd_{l}", q, k, v)
        if l == 0:
            gw[0].update(zip(MID_NAMES, gather_finish("gather_mid_finish", g_mid, o)))
            g_ffn2, tok_f = gather_forward("gather_ffn2_forward", g_ffn2, o)
            g_l1, tok_l = gather_forward("gather_l1_forward", g_l1, o)
            dep = (tok_f, tok_l)
        wout = gw[l]["wout"].reshape(D_MODEL, D_MODEL)
        wout_pool, wout_mla = wout[None, :POOL_WIDTH], wout[None, POOL_WIDTH:]
        mwq = gw[l]["mwq"].reshape(1, D_MODEL, D_MODEL)
        mwo = gw[l]["mwo"].reshape(1, D_MODEL, D_MODEL)
        mwkv = gw[l]["mwkv"][None]
        z2, x2f, x2b = proj_res_ln(f"mix_out_{l}", [ypool, o], [wout_pool, wout_mla], [0, 0], x1f,
                                   vec(lng[l, 1]), vec(lnb[l, 1]), 1.0, dep)
        dep = ()
        sv.update(cq=cq, ckv=ckv, cqn=cqn, ckvn=ckvn, q=q, k=k, v=v, dpool=dpool, ypool=ypool, o=o, lse=lse, z2=z2, x2b=x2b)
        kvm = mm_nn_shard(f"mem_kv_{l}", memb, mwkv, 0)
        cq_, co_, z3, x3f, x3b = cross_fwd(f"cross_fwd_{l}", x2b, x2f, mwq, mwo, 0, kvm, vec(lng[l, 2]), vec(lnb[l, 2]))
        sv.update(kvm=kvm, crq=cq_, cro=co_, z3=z3, x3b=x3b)
        if l == 0:
            gw[0].update(zip(FFN2_NAMES, gather_finish("gather_ffn2_finish", g_ffn2, x3b)))
        f2w13 = gw[l]["f2w13"][None]
        f2w2 = gw[l]["f2w2"].reshape(1, D_FF, D_MODEL)
        gate, up, act = ffn_up(f"ffn2_up_{l}", x3b, f2w13, 0)
        z4, xf, xb = proj_res_ln(f"ffn2_down_{l}", [act], [f2w2], [0], x3f, vec(lng[l, 3]), vec(lnb[l, 3]), 0.5)
        sv.update(gate2=gate, up2=up, act2=act, z4=z4)
        W[l] = dict(f1w13=f1w13, f1w2=gw[l]["f1w2"].reshape(1, D_FF, D_MODEL), win_ext=win_ext, wq_ext=wq_ext, wukv=wukv,
                    wbd=wbd, wout=wout[None], mwq=mwq, mwo=mwo, f2w13=f2w13, f2w2=f2w2)
        saved.append(sv)

    dln = {}
    dzb, dres, *dln[L - 1, 3], loss_blk = loss_grad("loss_grad", xf, loss_target[0],
                                                   (saved[L - 1]["z4"], vec(lng[L - 1, 3]), 0.5))
    loss = lax.psum(loss_blk[0, 0], ("x", "y", "c"))

    row_shards = lambda a: a.reshape(N_CHIPS, a.shape[0] // N_CHIPS, a.shape[1])
    small = {k_: [None] * L for k_ in ("pool_w", "pool_scale", "gq", "gkv", "lng", "lnb")}
    rest_names = [n for n in W_NAMES if n not in ("f1w13", "f1w2")]
    core = lax.axis_index("c")
    flags = [jnp.reshape(core == o, (1,)).astype(jnp.int32) for o in range(2)]
    qfs = [jnp.stack([chip, (core == o).astype(jnp.int32)]).astype(jnp.int32) for o in range(2)]

    def red_begin(tag, names, gs, layer):
        owners = [REDUCER[layer][n] for n in names]
        st, tok_ = pair_send_start(f"pair_send_start_{tag}", gs, owners, None)
        return (st, owners), tok_

    def red_mid(tag, sto, after):
        st, owners = sto
        gs_, lands_ = pair_send_wait(f"pair_send_wait_{tag}", st, after)
        ps = [pair_sum(f"pair_sum_{tag}_{a}", g_, r_, flags[o]) for a, (g_, r_, o) in enumerate(zip(gs_, lands_, owners))]
        st, tok_ = chip_exchange_start(f"chip_exchange_start_{tag}", ps, owners, None)
        return (st, owners), tok_

    def red_end(tag, sto, layer, prevs, after):
        st, owners = sto
        ps, lands_ = chip_exchange_wait(f"chip_exchange_wait_{tag}", st, after)
        return [chip_sum(f"chip_sum_{tag}_{a}", p_, r_, qfs[o], layer, s_)
                for a, (p_, r_, s_, o) in enumerate(zip(ps, lands_, prevs, owners))]

    def share_start(tag, names, sums_):
        return pair_share_start(f"pair_share_start_{tag}", sums_, [(REDUCER[0][n], REDUCER[1][n]) for n in names], None)

    st_p1 = st_c1 = st_pa = st_ca = None
    for l in reversed(range(L)):
        sv, w = saved[l], W[l]
        g = {}
        dh = ffn_bwd_da(f"ffn2_bwd_da_{l}", dzb, w["f2w2"], 0, sv["gate2"], sv["up2"], dep)
        dep = ()
        g["f2w2"] = row_shards(mm_tn(f"ffn2_dw2_{l}", sv["act2"], dzb))
        g["f2w13"] = mm_tn(f"ffn2_dw13_{l}", sv["x3b"], dh, True)
        dzb, dres, *dln[l, 2] = ffn_dx(f"ffn2_dx_{l}", dh, w["f2w13"], 0, dres, (sv["z3"], vec(lng[l, 2]), 1.0))
        if l == 0:
            st_c1, tok = red_mid("l1", st_p1, dzb)
            dep = (tok, g["f2w2"], g["f2w13"])
        dqc, dkvm = cross_bwd(f"cross_bwd_{l}", dzb, w["mwo"], 0, sv["crq"], sv["kvm"], dep)
        dep = ()
        g["mwo"] = row_shards(mm_tn(f"cross_dwo_{l}", sv["cro"], dzb))
        g["mwq"] = row_shards(mm_tn(f"cross_dwq_{l}", sv["x2b"], dqc))
        g["mwkv"] = mm_tn(f"cross_dwkv_{l}", memb, dkvm, True)
        dzb, dres, *dln[l, 1] = mm_nt_res(f"cross_dx_{l}", [dqc], [w["mwq"]], [0], dres, f32,
                                          (sv["z2"], vec(lng[l, 1]), 1.0))
        dcat = mm_nt_res(f"mix_dcat_{l}", [dzb], [w["wout"]], [0], None, bf16)
        dwo_p = mm_tn(f"mix_dwout_pool_{l}", sv["ypool"], dzb)
        dwo_m = mm_tn(f"mix_dwout_mla_{l}", sv["o"], dzb)
        g["wout"] = row_shards(jnp.concatenate([dwo_p, dwo_m], axis=0))
        dq, dk, dv = mla_attn_bwd(f"mla_bwd_{l}", sv["q"], sv["k"], sv["v"], sv["o"], dcat, sv["lse"], POOL_WIDTH // 128)
        dqe, dkv, dh_rest, dgq, dgkv = mix_post_bwd(f"mix_post_bwd_{l}", dq, dk, dv, w["wq_ext"], w["wukv"], 0, sv["cq"],
                                                     sv["ckv"], vec(q_norm_g[l]), vec(kv_norm_g[l]), cs)
        du, dyw, dscale = pool_bwd(f"pool_bwd_{l}", dcat, sv["dpool"], w["wbd"], vec(pool_scale[l]))
        dwq_e = mm_tn(f"mix_dwuq_{l}", sv["cqn"], dqe).reshape(Q_LORA, MLA_HEADS, 256)
        g["wuq"] = _to_col_shards(jnp.concatenate(
            [dwq_e[..., :QK_NOPE], _unswap_add(dwq_e[..., QK_NOPE:QK_NOPE + QK_ROPE], dwq_e[..., QK_NOPE + QK_ROPE:])],
            axis=-1).reshape(Q_LORA, MLA_HEADS * (QK_NOPE + QK_ROPE)))
        g["wukv"] = _to_col_shards(mm_tn(f"mix_dwukv_{l}", sv["ckvn"], dkv))
        dwbd = mm_tn(f"pool_dw_{l}", sv["dpool"], dyw)
        small["pool_w"][l] = jnp.stack([dwbd[64 * gi:64 * gi + 64, 64 * gi:64 * gi + 64] for gi in range(4)])
        small["pool_scale"][l], small["gq"][l], small["gkv"][l] = dscale[0], dgq[0], dgkv[0]
        dh_ext = jnp.concatenate([du, dh_rest], axis=1)
        dwin_e = mm_tn(f"mix_dwin_{l}", sv["x1b"], dh_ext)
        g["win"] = row_shards(jnp.concatenate(
            [dwin_e[:, :D_IN - QK_ROPE], _unswap_add(dwin_e[:, D_IN - QK_ROPE:D_IN], dwin_e[:, D_IN:])], axis=-1))
        dzb, dres, *dln[l, 0] = mm_nt_res(f"mix_dx_{l}", [dh_ext], [w["win_ext"]], [0], dres, f32,
                                          (sv["z1"], vec(lng[l, 0]), 0.5))
        if l == 0:
            st_pa, tok = red_begin("a0", rest_names, [g[n] for n in rest_names], 0)
            dep = (tok,)
        dh = ffn_bwd_da(f"ffn1_bwd_da_{l}", dzb, w["f1w2"], 0, sv["gate1"], sv["up1"], dep)
        dep = ()
        if l == 0:
            grad_x = ffn_dx(f"ffn1_dx_{l}", dh, w["f1w13"], 0, dres)[None]
            for l_ in range(L):
                small["lng"][l_] = jnp.concatenate([dln[l_, k_][0] for k_ in range(4)], axis=0)
                small["lnb"][l_] = jnp.concatenate([dln[l_, k_][1] for k_ in range(4)], axis=0)
            rep = [jnp.stack(small[k_]).reshape(-1) for k_ in ("pool_w", "pool_scale", "gq", "gkv", "lng", "lnb")]
            sizes = [r.shape[0] for r in rep]
            packed = jnp.concatenate(rep)
            packed = jnp.pad(packed, (0, (-packed.shape[0]) % 1024)).reshape(-1, 128)
            tot = allsum_small("allsum_small_grads", packed, (grad_x,)).reshape(-1)
            st_ca, tok = red_mid("a0", st_pa, (grad_x, tot))
            dep = (tok,)
        else:
            below = ffn_dx(f"ffn1_dx_{l}", dh, w["f1w13"], 0, dres, (saved[l - 1]["z4"], vec(lng[l - 1, 3]), 0.5))
            dln[l - 1, 3] = below[2:]
        g["f1w2"] = row_shards(mm_tn(f"ffn1_dw2_{l}", sv["act1"], dzb, False, dep))
        g["f1w13"] = mm_tn(f"ffn1_dw13_{l}", sv["x0b"], dh, True, dep)
        dep = ()
        if l > 0:
            dzb, dres = below[:2]
        if l == 1:
            st_p1, tok = red_begin("l1", W_NAMES, [g[n] for n in W_NAMES], 1)
            dep = (tok,)

    st_pb, _ = red_begin("b0", ("f1w13", "f1w2"), [g["f1w13"], g["f1w2"]], 0)
    sums1 = dict(zip(W_NAMES, red_end("l1", st_c1, 1, [None] * len(W_NAMES), g["f1w13"])))
    st_cb, tok = red_mid("b0", st_pb, tuple(sums1.values()))
    sums0 = red_end("a0", st_ca, 0, [sums1[n] for n in rest_names], tok)
    share_a, tok_a = share_start("a", rest_names, sums0)

    offs = [0]
    for s_ in sizes:
        offs.append(offs[-1] + s_)
    parts = [tot[offs[i]:offs[i + 1]] for i in range(len(sizes))]
    g_pool_w = parts[0].reshape(pool_w.shape)
    g_pool_scale = parts[1].reshape(pool_scale.shape)
    g_gq = parts[2].reshape(q_norm_g.shape)
    g_gkv = parts[3].reshape(kv_norm_g.shape)
    shard_cols = lambda a: lax.dynamic_slice_in_dim(a.reshape(L, 4, D_MODEL), chip * (D_MODEL // N_CHIPS),
                                                    D_MODEL // N_CHIPS, axis=2)
    g_lng, g_lnb = shard_cols(parts[4]), shard_cols(parts[5])

    out_names = ("lng", "lnb", "f1w13", "f1w2", "win", "pool_w", "pool_scale", "gq", "wuq", "gkv", "wukv", "wout", "mwq",
                 "mwkv", "mwo", "f2w13", "f2w2")
    big = dict(lng=g_lng, lnb=g_lnb, pool_w=g_pool_w, pool_scale=g_pool_scale, gq=g_gq, gkv=g_gkv)
    late = ("f1w13", "f1w2")
    held = ("f2w13", "f2w2")
    ws = [ln_g, ln_b, ffn1_w13, ffn1_w2, w_in, pool_w, pool_scale, q_norm_g, w_uq, kv_norm_g, w_ukv, w_out, mem_wq,
          mem_wkv, mem_wo, ffn2_w13, ffn2_w2]
    ms = [m_ln_g, m_ln_b, m_ffn1_w13, m_ffn1_w2, m_w_in, m_pool_w, m_pool_scale, m_q_norm_g, m_w_uq, m_kv_norm_g, m_w_ukv,
          m_w_out, m_mem_wq, m_mem_wkv, m_mem_wo, m_ffn2_w13, m_ffn2_w2]
    vs = [v_ln_g, v_ln_b, v_ffn1_w13, v_ffn1_w2, v_w_in, v_pool_w, v_pool_scale, v_q_norm_g, v_w_uq, v_kv_norm_g, v_w_ukv,
          v_w_out, v_mem_wq, v_mem_wkv, v_mem_wo, v_ffn2_w13, v_ffn2_w2]
    res = {}

    def update(n, deps=()):
        a = out_names.index(n)
        res[a] = adamw(f"adamw_{a}", ws[a], big[n].reshape(ws[a].shape), ms[a], vs[a], deps)
        return res[a][0]

    small_done = tuple(update(n, (tok_a,)) for n in ("lng", "lnb", "pool_w", "pool_scale", "gq", "gkv"))
    big.update(zip(rest_names, pair_share_wait("pair_share_wait_a", share_a, small_done)))
    first_done = tuple(update(n) for n in rest_names if n not in held)
    sums_b = red_end("b0", st_cb, 0, [sums1[n] for n in late], first_done)
    share_b, tok_b = share_start("b", late, sums_b)
    held_done = tuple(update(n, (tok_b,)) for n in held)
    big.update(zip(late, pair_share_wait("pair_share_wait_b", share_b, held_done)))
    for n in late:
        update(n)
    order = range(len(out_names))
    grads = [big[n].reshape(w_.shape) for n, w_ in zip(out_names, ws)]
    return (loss, grad_x, *grads, *[res[a][0] for a in order], *[res[a][1] for a in order], *[res[a][2] for a in order])
```

```python
import functools
import math

import jax
import jax.numpy as jnp
from jax import lax
from jax.experimental import pallas as pl
from jax.experimental.pallas import tpu as pltpu

f32 = jnp.float32
bf16 = jnp.bfloat16
SDS = jax.ShapeDtypeStruct
MESH = pl.DeviceIdType.MESH

D_MODEL = 1024
DEPTH = 2
N_MEM = 256
MEM_HEADS = 4
MEM_HEAD_DIM = D_MODEL // MEM_HEADS
POOL_WINDOWS = (2, 4, 8, 16)
POOL_WIDTH = 256
POOL_GROUP = 64
QK_NOPE = 128
QK_ROPE = 64
V_HEAD = 128
MLA_HEADS = 6
Q_LORA = 256
KV_LORA = 128
ROPE_BASE = 10000.0
D_FF = 2816
D_IN = POOL_WIDTH + Q_LORA + KV_LORA + QK_ROPE
ALPHA = (2 * DEPTH) ** 0.25
LN_EPS = 1e-5
RMS_EPS = 1e-6
NEG_INF = -1e30
MLA_SCALE = (QK_NOPE + QK_ROPE) ** -0.5
MLA_SCALE_LOG2 = MLA_SCALE * math.log2(math.e)
MEM_SCALE = MEM_HEAD_DIM ** -0.5
ADAM_LR = 0.001
ADAM_B1 = 0.9
ADAM_B2 = 0.999
ADAM_EPS = 1e-08
ADAM_WD = 0.01
ADAM_STEP = 10

N_CHIPS = 4
V7X_VMEM_LIMIT = 56 * 2**20
HALO = 16

_NT = (((1,), (1,)), ((), ()))
_TN = (((0,), (0,)), ((), ()))


def _dot(a, b):
    return jnp.dot(a, b, preferred_element_type=f32)


def _dot_nt(a, b):
    return lax.dot_general(a, b, _NT, preferred_element_type=f32)


def _dot_tn(a, b):
    return lax.dot_general(a, b, _TN, preferred_element_type=f32)


def _cp(*sem):
    return pltpu.CompilerParams(dimension_semantics=sem if sem else None, vmem_limit_bytes=V7X_VMEM_LIMIT)


_DEP_SPEC = pl.BlockSpec(memory_space=pl.ANY)


def _with_deps(body, n_in, deps):
    nd = len(deps)
    if not nd:
        return body

    def wrapped(*refs):
        return body(*refs[:n_in], *refs[n_in + nd:])

    return wrapped


def _tile(n, t):
    t = min(n, t)
    assert n % t == 0, (n, t)
    return t


def _row_tile(rows, cols, itemsize=4, target=2 * 2**20):
    best = None
    for t in range(16, rows + 1, 16):
        if rows % t == 0 and t * cols * itemsize <= target:
            best = t
    return best if best is not None else rows


def ffn_up(name, xb, w13, l, deps=()):
    S = xb.shape[0]
    ns = w13.shape[3]
    tm = _tile(S, 512)

    def body(x_ref, wg_ref, wu_ref, g_ref, u_ref, a_ref):
        x = x_ref[...]
        g = _dot(x, wg_ref[0, 0])
        u = _dot(x, wu_ref[0, 0])
        a = g * jax.nn.sigmoid(g) * u
        g_ref[...] = g.astype(bf16)
        u_ref[...] = u.astype(bf16)
        a_ref[...] = a.astype(bf16)

    out = SDS((S, 2 * ns), bf16)
    return pl.pallas_call(
        _with_deps(body, 3, deps), name=name, grid=(2, S // tm),
        in_specs=[pl.BlockSpec((tm, D_MODEL), lambda j, i: (i, 0)),
                  pl.BlockSpec((1, 1, D_MODEL, ns), lambda j, i: (l, j, 0, 0)),
                  pl.BlockSpec((1, 1, D_MODEL, ns), lambda j, i: (l, j + 2, 0, 0))] + [_DEP_SPEC] * len(deps),
        out_specs=[pl.BlockSpec((tm, ns), lambda j, i: (i, j))] * 3,
        out_shape=[out, out, out],
        compiler_params=_cp("parallel", "parallel"),
    )(xb, w13, w13, *deps)


def proj_res_ln(name, parts, ws, wl, x, g, b, rscale, deps=()):
    S = x.shape[0]
    tm = _tile(S, 512)
    n = len(parts)

    def body(*refs):
        p_refs, w_refs = refs[:n], refs[n:2 * n]
        x_ref, g_ref, b_ref, z_ref, y_ref, yb_ref = refs[2 * n:]
        acc = _dot(p_refs[0][...], w_refs[0][0])
        for k in range(1, n):
            acc = acc + _dot(p_refs[k][...], w_refs[k][0])
        if rscale != 1.0:
            acc = rscale * acc
        z = ALPHA * x_ref[...] + acc
        mu = jnp.mean(z, axis=-1, keepdims=True)
        zc = z - mu
        var = jnp.mean(zc * zc, axis=-1, keepdims=True)
        y = zc * lax.rsqrt(var + LN_EPS) * g_ref[...] + b_ref[...]
        z_ref[...] = z
        y_ref[...] = y
        yb_ref[...] = y.astype(bf16)

    row = lambda i: (i, 0)
    in_specs = [pl.BlockSpec((tm, p.shape[1]), row) for p in parts]
    in_specs += [pl.BlockSpec((1,) + w.shape[1:], functools.partial(lambda li, i: (li, 0, 0), li)) for w, li in zip(ws, wl)]
    in_specs += [pl.BlockSpec((tm, D_MODEL), row), pl.BlockSpec((1, D_MODEL), lambda i: (0, 0)),
                 pl.BlockSpec((1, D_MODEL), lambda i: (0, 0))] + [_DEP_SPEC] * len(deps)
    return pl.pallas_call(
        _with_deps(body, 2 * n + 3, deps), name=name, grid=(S // tm,), in_specs=in_specs,
        out_specs=[pl.BlockSpec((tm, D_MODEL), row)] * 3,
        out_shape=[SDS((S, D_MODEL), f32), SDS((S, D_MODEL), f32), SDS((S, D_MODEL), bf16)],
        compiler_params=_cp("parallel"),
    )(*parts, *ws, x, g, b, *deps)


def _ln_bwd_store(dyv, z_ref, g_ref, rscale, first, dzb_ref, dres_ref, dg_ref, db_ref):
    z = z_ref[...]
    mu = jnp.mean(z, axis=-1, keepdims=True)
    zc = z - mu
    rstd = lax.rsqrt(jnp.mean(zc * zc, axis=-1, keepdims=True) + LN_EPS)
    xhat = zc * rstd
    dxh = dyv * g_ref[...]
    m1 = jnp.mean(dxh, axis=-1, keepdims=True)
    m2 = jnp.mean(dxh * xhat, axis=-1, keepdims=True)
    dz = rstd * (dxh - m1 - xhat * m2)
    dzb_ref[...] = (rscale * dz).astype(bf16)
    dres_ref[...] = ALPHA * dz

    @pl.when(first)
    def _():
        dg_ref[...] = jnp.zeros_like(dg_ref)
        db_ref[...] = jnp.zeros_like(db_ref)

    dg_ref[...] += jnp.sum(dyv * xhat, axis=0, keepdims=True)
    db_ref[...] += jnp.sum(dyv, axis=0, keepdims=True)


def _ln_bwd_specs(S, tm, index):
    vec = pl.BlockSpec((1, D_MODEL), lambda *a: (0, 0))
    blk = pl.BlockSpec((tm, D_MODEL), index)
    in_specs = [blk, vec]
    out_specs = [blk, blk, vec, vec]
    out_shape = [SDS((S, D_MODEL), bf16), SDS((S, D_MODEL), f32), SDS((1, D_MODEL), f32), SDS((1, D_MODEL), f32)]
    return in_specs, out_specs, out_shape


def ffn_bwd_da(name, drb, w2, l, gate, up, deps=()):
    S = drb.shape[0]
    tm = _tile(S, 512)
    nh = D_FF // 2

    def body(dr_ref, w_ref, g_ref, u_ref, dh_ref):
        dr = dr_ref[...]
        for j in range(2):
            cols = slice(j * nh, (j + 1) * nh)
            da = _dot_nt(dr, w_ref[0, cols, :])
            g = g_ref[:, cols].astype(f32)
            u = u_ref[:, cols].astype(f32)
            sg = jax.nn.sigmoid(g)
            dh_ref[:, cols] = (da * u * (sg * (1.0 + g * (1.0 - sg)))).astype(bf16)
            dh_ref[:, D_FF + j * nh:D_FF + (j + 1) * nh] = (da * (g * sg)).astype(bf16)

    row = lambda i: (i, 0)
    return pl.pallas_call(
        _with_deps(body, 4, deps), name=name, grid=(S // tm,),
        in_specs=[pl.BlockSpec((tm, D_MODEL), row), pl.BlockSpec((1, D_FF, D_MODEL), lambda i: (l, 0, 0)),
                  pl.BlockSpec((tm, D_FF), row), pl.BlockSpec((tm, D_FF), row)] + [_DEP_SPEC] * len(deps),
        out_specs=pl.BlockSpec((tm, 2 * D_FF), row),
        out_shape=SDS((S, 2 * D_FF), bf16),
        compiler_params=_cp("parallel"),
    )(drb, w2, gate, up, *deps)


def ffn_dx(name, dh, w13, l, res, ln=None):
    S = dh.shape[0]
    ns = w13.shape[3]
    tm = _tile(S, 1024)
    last = N_CHIPS - 1
    row = lambda i, j: (i, 0)
    in_specs = [pl.BlockSpec((tm, ns), lambda i, j: (i, j)),
                pl.BlockSpec((1, 1, D_MODEL, ns), lambda i, j: (l, j, 0, 0)),
                pl.BlockSpec((tm, D_MODEL), row)]
    if ln is None:
        def body(dh_ref, w_ref, r_ref, o_ref):
            @pl.when(pl.program_id(1) == 0)
            def _():
                o_ref[...] = r_ref[...]

            o_ref[...] += _dot_nt(dh_ref[...], w_ref[0, 0])

        return pl.pallas_call(
            body, name=name, grid=(S // tm, N_CHIPS), in_specs=in_specs,
            out_specs=pl.BlockSpec((tm, D_MODEL), row), out_shape=SDS((S, D_MODEL), f32),
            compiler_params=_cp("parallel", "arbitrary"),
        )(dh, w13, res)

    z, g, rscale = ln

    def body_ln(dh_ref, w_ref, r_ref, z_ref, g_ref, dzb_ref, dres_ref, dg_ref, db_ref, acc_sc):
        i, j = pl.program_id(0), pl.program_id(1)

        @pl.when(j == 0)
        def _():
            acc_sc[...] = r_ref[...]

        acc_sc[...] += _dot_nt(dh_ref[...], w_ref[0, 0])

        @pl.when(j == last)
        def _():
            _ln_bwd_store(acc_sc[...], z_ref, g_ref, rscale, i == 0, dzb_ref, dres_ref, dg_ref, db_ref)

    ln_in, ln_out, ln_shape = _ln_bwd_specs(S, tm, row)
    return pl.pallas_call(
        body_ln, name=name, grid=(S // tm, N_CHIPS), in_specs=in_specs + ln_in, out_specs=ln_out, out_shape=ln_shape,
        scratch_shapes=[pltpu.VMEM((tm, D_MODEL), f32)], compiler_params=_cp("arbitrary", "arbitrary"),
    )(dh, w13, res, z, g)


def mm_nt_res(name, dys, ws, wl, res, out_dtype, ln=None):
    S = dys[0].shape[0]
    K = ws[0].shape[1]
    tm = _tile(S, 512)
    n = len(dys)
    n_in = 2 * n + (res is not None)

    def product(refs):
        acc = _dot_nt(refs[0][...], refs[n][0])
        for k in range(1, n):
            acc = acc + _dot_nt(refs[k][...], refs[n + k][0])
        if res is not None:
            acc = acc + refs[2 * n][...]
        return acc

    def body(*refs):
        refs[-1][...] = product(refs).astype(out_dtype)

    def body_ln(*refs):
        z_ref, g_ref, dzb_ref, dres_ref, dg_ref, db_ref = refs[n_in:]
        _ln_bwd_store(product(refs), z_ref, g_ref, ln[2], pl.program_id(0) == 0, dzb_ref, dres_ref, dg_ref, db_ref)

    row = lambda i: (i, 0)
    in_specs = [pl.BlockSpec((tm, d.shape[1]), row) for d in dys]
    in_specs += [pl.BlockSpec((1,) + w.shape[1:], functools.partial(lambda li, i: (li, 0, 0), li)) for w, li in zip(ws, wl)]
    args = list(dys) + list(ws)
    if res is not None:
        in_specs.append(pl.BlockSpec((tm, K), row))
        args.append(res)
    if ln is None:
        return pl.pallas_call(
            body, name=name, grid=(S // tm,), in_specs=in_specs,
            out_specs=pl.BlockSpec((tm, K), row), out_shape=SDS((S, K), out_dtype),
            compiler_params=_cp("parallel"),
        )(*args)
    ln_in, ln_out, ln_shape = _ln_bwd_specs(S, tm, row)
    return pl.pallas_call(
        body_ln, name=name, grid=(S // tm,), in_specs=in_specs + ln_in, out_specs=ln_out, out_shape=ln_shape,
        compiler_params=_cp("arbitrary"),
    )(*args, ln[0], ln[1])


def mm_tn(name, x, dy, col_shards=False, deps=()):
    S, K = x.shape
    N = dy.shape[1]
    ts = 512
    while ts * 2 <= min(S, 2048) and S % (ts * 2) == 0 and ts * 2 * K * 2 <= 6 * 2**20:
        ts *= 2
    ts = _tile(S, ts)
    if col_shards:
        tn = N // N_CHIPS
    else:
        tn = N
        while K * tn * 4 > 6 * 2**20 and tn % 256 == 0:
            tn //= 2
    nn = N // tn
    lead = ((0,) if col_shards else ()) + (slice(None), slice(None))

    def body(x_ref, dy_ref, o_ref):
        acc = _dot_tn(x_ref[...].astype(bf16), dy_ref[...].astype(bf16))

        @pl.when(pl.program_id(1) == 0)
        def _():
            o_ref[lead] = acc

        @pl.when(pl.program_id(1) != 0)
        def _():
            o_ref[lead] += acc

    if col_shards:
        out_spec = pl.BlockSpec((1, K, tn), lambda n, s: (n, 0, 0))
        out_shape = SDS((N_CHIPS, K, tn), f32)
    else:
        out_spec = pl.BlockSpec((K, tn), lambda n, s: (0, n))
        out_shape = SDS((K, N), f32)
    return pl.pallas_call(
        _with_deps(body, 2, deps), name=name, grid=(nn, S // ts),
        in_specs=[pl.BlockSpec((ts, K), lambda n, s: (s, 0)), pl.BlockSpec((ts, tn), lambda n, s: (s, n))]
        + [_DEP_SPEC] * len(deps),
        out_specs=out_spec, out_shape=out_shape, compiler_params=_cp("parallel", "arbitrary"),
    )(x, dy, *deps)


def mm_nn_shard(name, x, w, l):
    S, K = x.shape
    ns = w.shape[3]

    def body(x_ref, w_ref, o_ref):
        o_ref[...] = _dot(x_ref[...], w_ref[0, 0]).astype(bf16)

    return pl.pallas_call(
        body, name=name, grid=(N_CHIPS,),
        in_specs=[pl.BlockSpec((S, K), lambda j: (0, 0)), pl.BlockSpec((1, 1, K, ns), lambda j: (l, j, 0, 0))],
        out_specs=pl.BlockSpec((S, ns), lambda j: (0, j)), out_shape=SDS((S, N_CHIPS * ns), bf16),
        compiler_params=_cp("parallel"),
    )(x, w)


def loss_grad(name, y, t, ln):
    S = y.shape[0]
    tm = _tile(S, 512)
    z, g, rscale = ln

    def body(y_ref, t_ref, z_ref, g_ref, dzb_ref, dres_ref, dg_ref, db_ref, loss_ref):
        first = pl.program_id(0) == 0
        e = y_ref[...] - t_ref[...]
        _ln_bwd_store(e * (1.0 / D_MODEL), z_ref, g_ref, rscale, first, dzb_ref, dres_ref, dg_ref, db_ref)

        @pl.when(first)
        def _():
            loss_ref[...] = jnp.zeros_like(loss_ref)

        loss_ref[...] += jnp.full(loss_ref.shape, (0.5 / D_MODEL) * jnp.sum(e * e), f32)

    row = lambda i: (i, 0)
    ln_in, ln_out, ln_shape = _ln_bwd_specs(S, tm, row)
    return pl.pallas_call(
        body, name=name, grid=(S // tm,),
        in_specs=[pl.BlockSpec((tm, D_MODEL), row)] * 2 + ln_in,
        out_specs=ln_out + [pl.BlockSpec((8, 128), lambda i: (0, 0))],
        out_shape=ln_shape + [SDS((8, 128), f32)],
        compiler_params=_cp("arbitrary"),
    )(y, t, z, g)


def _half_sum(t):
    return t + pltpu.roll(t, 64, axis=1)


def mix_pre(name, xb, w_in, wq, wkv, l, gq, gkv, cs):
    S = xb.shape[0]
    tm = _tile(S, 512)
    H = MLA_HEADS
    W_EXT = w_in.shape[2]

    def body(x_ref, win_ref, wq_ref, wkv_ref, gq_ref, gkv_ref, cs_ref,
             u_ref, cq_ref, ckv_ref, cqn_ref, ckvn_ref, q_ref, k_ref, v_ref):
        h = _dot(x_ref[...], win_ref[0])
        u_ref[...] = h[:, :256]
        cq = h[:, 256:512]
        ckv = h[:, 512:640]
        cq_ref[...] = cq
        ckv_ref[...] = ckv
        cqn = (cq * lax.rsqrt(jnp.mean(cq * cq, axis=-1, keepdims=True) + RMS_EPS) * gq_ref[...]).astype(bf16)
        ckvn = (ckv * lax.rsqrt(jnp.mean(ckv * ckv, axis=-1, keepdims=True) + RMS_EPS) * gkv_ref[...]).astype(bf16)
        cqn_ref[...] = cqn
        ckvn_ref[...] = ckvn
        csv = cs_ref[...]
        lane = lax.broadcasted_iota(jnp.int32, (tm, 128), 1)
        kr = jnp.where(lane < 64, _half_sum(h[:, 640:768] * csv), 0.0).astype(bf16)
        kv = _dot(ckvn, wkv_ref[0])
        for hd in range(H):
            qe = _dot(cqn, wq_ref[0, hd])
            q_ref[hd, :, :128] = (qe[:, :128] * MLA_SCALE_LOG2).astype(bf16)
            q_ref[hd, :, 128:] = (_half_sum(qe[:, 128:] * csv) * MLA_SCALE_LOG2).astype(bf16)
            k_ref[hd, :, :128] = kv[:, 256 * hd:256 * hd + 128].astype(bf16)
            k_ref[hd, :, 128:] = kr
            v_ref[hd] = kv[:, 256 * hd + 128:256 * hd + 256].astype(bf16)

    row = lambda i: (i, 0)
    hrow = lambda i: (0, i, 0)
    return pl.pallas_call(
        body, name=name, grid=(S // tm,),
        in_specs=[pl.BlockSpec((tm, D_MODEL), row),
                  pl.BlockSpec((1, D_MODEL, W_EXT), lambda i: (l, 0, 0)),
                  pl.BlockSpec((1, H, Q_LORA, 256), lambda i: (l, 0, 0, 0)),
                  pl.BlockSpec((1, KV_LORA, H * 256), lambda i: (l, 0, 0)),
                  pl.BlockSpec((1, Q_LORA), lambda i: (0, 0)), pl.BlockSpec((1, KV_LORA), lambda i: (0, 0)),
                  pl.BlockSpec((tm, 128), row)],
        out_specs=[pl.BlockSpec((tm, 256), row), pl.BlockSpec((tm, Q_LORA), row), pl.BlockSpec((tm, KV_LORA), row),
                   pl.BlockSpec((tm, Q_LORA), row), pl.BlockSpec((tm, KV_LORA), row),
                   pl.BlockSpec((H, tm, 256), hrow), pl.BlockSpec((H, tm, 256), hrow), pl.BlockSpec((H, tm, 128), hrow)],
        out_shape=[SDS((S, 256), f32), SDS((S, Q_LORA), f32), SDS((S, KV_LORA), f32),
                   SDS((S, Q_LORA), bf16), SDS((S, KV_LORA), bf16),
                   SDS((H, S, 256), bf16), SDS((H, S, 256), bf16), SDS((H, S, 128), bf16)],
        compiler_params=_cp("parallel"),
    )(xb, w_in, wq, wkv, gq, gkv, cs)


def _group_select(col, a2, a4, a8, a16):
    return jnp.where(col < 64, a2, jnp.where(col < 128, a4, jnp.where(col < 192, a8, a16)))


def pool_fwd(name, u, wbd, scale):
    S = u.shape[0]
    tm = _tile(S, 512)
    hb = tm // HALO

    def body(u_ref, halo_ref, w_ref, s_ref, d_ref, y_ref):
        i = pl.program_id(0)
        cur = u_ref[...]
        halo = jnp.where(i > 0, halo_ref[...], 0.0)
        ext = jnp.concatenate([halo, cur], axis=0)
        s2 = ext + pltpu.roll(ext, 1, axis=0)
        s4 = s2 + pltpu.roll(s2, 2, axis=0)
        s8 = s4 + pltpu.roll(s4, 4, axis=0)
        s16 = s8 + pltpu.roll(s8, 8, axis=0)
        t1 = (i * tm + 1 + lax.broadcasted_iota(jnp.int32, (tm, 1), 0)).astype(f32)
        col = lax.broadcasted_iota(jnp.int32, (tm, 256), 1)
        m = _group_select(col, s2[HALO:] / jnp.minimum(t1, 2.0), s4[HALO:] / jnp.minimum(t1, 4.0),
                          s8[HALO:] / jnp.minimum(t1, 8.0), s16[HALO:] / jnp.minimum(t1, 16.0))
        d = (m - cur).astype(bf16)
        d_ref[...] = d
        y_ref[...] = (_dot(d, w_ref[...]) * s_ref[...]).astype(bf16)

    row = lambda i: (i, 0)
    return pl.pallas_call(
        body, name=name, grid=(S // tm,),
        in_specs=[pl.BlockSpec((tm, 256), row), pl.BlockSpec((HALO, 256), lambda i: (jnp.maximum(i * hb - 1, 0), 0)),
                  pl.BlockSpec((256, 256), lambda i: (0, 0)), pl.BlockSpec((1, 256), lambda i: (0, 0))],
        out_specs=[pl.BlockSpec((tm, 256), row)] * 2,
        out_shape=[SDS((S, 256), bf16), SDS((S, 256), bf16)],
        compiler_params=_cp("parallel"),
    )(u, u, wbd, scale)


def pool_bwd(name, dyp, d, wbd, scale):
    S = dyp.shape[0]
    tm = _tile(S, 512)
    hb = tm // HALO
    n_ext = tm + HALO

    def fwd_sum(e, steps):
        k = 1
        for _ in range(steps):
            e = e + pltpu.roll(e, n_ext - k, axis=0)
            k *= 2
        return e

    def body(dy_ref, halo_ref, d_ref, w_ref, s_ref, du_ref, dyw_ref, ds_ref):
        i = pl.program_id(0)
        sc = s_ref[...]
        w = w_ref[...]
        cur = dy_ref[...].astype(f32)
        halo = jnp.where(i < pl.num_programs(0) - 1, halo_ref[...].astype(f32), 0.0)
        dyw = jnp.concatenate([cur, halo], axis=0) * sc
        dyw_ref[...] = dyw[:tm].astype(bf16)
        dd = _dot_nt(dyw.astype(bf16), w)
        t1 = (i * tm + 1 + lax.broadcasted_iota(jnp.int32, (n_ext, 1), 0)).astype(f32)
        f2 = fwd_sum(dd / jnp.minimum(t1, 2.0), 1)
        f4 = fwd_sum(dd / jnp.minimum(t1, 4.0), 2)
        f8 = fwd_sum(dd / jnp.minimum(t1, 8.0), 3)
        f16 = fwd_sum(dd / jnp.minimum(t1, 16.0), 4)
        col = lax.broadcasted_iota(jnp.int32, (tm, 256), 1)
        du_ref[...] = (_group_select(col, f2[:tm], f4[:tm], f8[:tm], f16[:tm]) - dd[:tm]).astype(bf16)

        @pl.when(i == 0)
        def _():
            ds_ref[...] = jnp.zeros_like(ds_ref)

        ds_ref[...] += jnp.sum(cur * _dot(d_ref[...], w), axis=0, keepdims=True)

    row = lambda i: (i, 0)
    nhb = S // HALO
    return pl.pallas_call(
        body, name=name, grid=(S // tm,),
        in_specs=[pl.BlockSpec((tm, 256), row), pl.BlockSpec((HALO, 256), lambda i: (jnp.minimum((i + 1) * hb, nhb - 1), 0)),
                  pl.BlockSpec((tm, 256), row), pl.BlockSpec((256, 256), lambda i: (0, 0)),
                  pl.BlockSpec((1, 256), lambda i: (0, 0))],
        out_specs=[pl.BlockSpec((tm, 256), row), pl.BlockSpec((tm, 256), row), pl.BlockSpec((1, 256), lambda i: (0, 0))],
        out_shape=[SDS((S, 256), bf16), SDS((S, 256), bf16), SDS((1, 256), f32)],
        compiler_params=_cp("arbitrary"),
    )(dyp, dyp, d, wbd, scale)


def _diag_mask(r0, rn, kn):
    rc = (r0 + lax.broadcasted_iota(jnp.int32, (rn, 1), 0)) // 64
    cc = lax.broadcasted_iota(jnp.int32, (1, kn), 1) // 64
    return rc >= cc


def _diag_parts(tq):
    h = tq // 2
    return [(0, h, h), (h, h, tq)] if h % 128 == 0 else [(0, tq, tq)]


def mla_attn_fwd(name, q, k, v):
    H, S, _ = q.shape
    tq = _tile(S, 1024)
    nq = S // tq
    pairs = [(i, j) for i in range(nq) for j in range(i + 1)]
    it = jnp.asarray([p_[0] for p_ in pairs], jnp.int32)
    jt = jnp.asarray([p_[1] for p_ in pairs], jnp.int32)

    def body(it_ref, jt_ref, q_ref, k_ref, v_ref, o_ref, lse_ref, m_sc, l_sc, acc_sc):
        t = pl.program_id(1)
        i, j = it_ref[t], jt_ref[t]

        @pl.when(j == 0)
        def _():
            m_sc[...] = jnp.full_like(m_sc, NEG_INF)
            l_sc[...] = jnp.zeros_like(l_sc)
            acc_sc[...] = jnp.zeros_like(acc_sc)

        def part(r0, rn, kn, masked):
            rows, keys = slice(r0, r0 + rn), slice(0, kn)
            s = _dot_nt(q_ref[0, rows, :], k_ref[0, keys, :])
            if masked:
                s = jnp.where(_diag_mask(r0, rn, kn), s, NEG_INF)
            m_prev = m_sc[rows, :]
            m_new = jnp.maximum(m_prev, jnp.max(s, axis=-1, keepdims=True))
            p = jnp.exp2(s - jnp.tile(m_new, (1, kn // 128)))
            a = jnp.exp2(m_prev - m_new)
            l_sc[rows, :] = a * l_sc[rows, :] + jnp.sum(p, axis=-1, keepdims=True)
            acc_sc[rows, :] = a * acc_sc[rows, :] + _dot(p.astype(bf16), v_ref[0, keys, :])
            m_sc[rows, :] = m_new

        @pl.when(j < i)
        def _():
            part(0, tq, tq, False)

        @pl.when(j == i)
        def _():
            for r0, rn, kn in _diag_parts(tq):
                part(r0, rn, kn, True)
            o_ref[...] = (acc_sc[...] / l_sc[...]).astype(bf16)
            lse_ref[0] = m_sc[...] + jnp.log2(l_sc[...])

    return pl.pallas_call(
        body, name=name,
        grid_spec=pltpu.PrefetchScalarGridSpec(
            num_scalar_prefetch=2, grid=(H, len(pairs)),
            in_specs=[pl.BlockSpec((1, tq, 256), lambda h, t, it_, jt_: (h, it_[t], 0)),
                      pl.BlockSpec((1, tq, 256), lambda h, t, it_, jt_: (h, jt_[t], 0)),
                      pl.BlockSpec((1, tq, 128), lambda h, t, it_, jt_: (h, jt_[t], 0))],
            out_specs=[pl.BlockSpec((tq, 128), lambda h, t, it_, jt_: (it_[t], h)),
                       pl.BlockSpec((1, tq, 128), lambda h, t, it_, jt_: (h, it_[t], 0))],
            scratch_shapes=[pltpu.VMEM((tq, 128), f32), pltpu.VMEM((tq, 128), f32), pltpu.VMEM((tq, 128), f32)]),
        out_shape=[SDS((S, H * 128), bf16), SDS((H, S, 128), f32)],
        compiler_params=_cp("parallel", "arbitrary"),
    )(it, jt, q, k, v)


def mla_attn_bwd(name, q, k, v, o, do, lse, do_col=0):
    H, S, _ = q.shape
    tq = _tile(S, 1024)
    nq = S // tq
    pairs = [(i, j) for j in range(nq) for i in range(j, nq)]
    it = jnp.asarray([p_[0] for p_ in pairs], jnp.int32)
    jt = jnp.asarray([p_[1] for p_ in pairs], jnp.int32)
    n_pairs = len(pairs)

    def body(it_ref, jt_ref, q_ref, k_ref, v_ref, o_ref, do_ref, lse_ref, dq_ref, dk_ref, dv_ref, dq_sc, dk_sc, dv_sc):
        t = pl.program_id(1)
        i, j = it_ref[t], jt_ref[t]

        @pl.when(t == 0)
        def _():
            dq_sc[...] = jnp.zeros_like(dq_sc)

        @pl.when(i == j)
        def _():
            dk_sc[...] = jnp.zeros_like(dk_sc)
            dv_sc[...] = jnp.zeros_like(dv_sc)

        def part(r0, rn, kn, masked):
            rows, keys = slice(r0, r0 + rn), slice(0, kn)
            qv, kv_, dov = q_ref[0, rows, :], k_ref[0, keys, :], do_ref[rows, :]
            s = _dot_nt(qv, kv_)
            if masked:
                s = jnp.where(_diag_mask(r0, rn, kn), s, NEG_INF)
            p = jnp.exp2(s - jnp.tile(lse_ref[0, rows, :], (1, kn // 128)))
            dv_sc[keys, :] += _dot_tn(p.astype(bf16), dov)
            dp = _dot_nt(dov, v_ref[0, keys, :])
            delta = jnp.sum(dov.astype(f32) * o_ref[rows, :].astype(f32), axis=-1, keepdims=True)
            ds = (p * (dp - delta)).astype(bf16)
            dk_sc[keys, :] += _dot_tn(ds, qv)
            dq_rows = pl.ds(pl.multiple_of(i * tq + r0, 128), rn)
            dq_sc[dq_rows, :] += _dot(ds, kv_)

        @pl.when(i > j)
        def _():
            part(0, tq, tq, False)

        @pl.when(i == j)
        def _():
            for r0, rn, kn in _diag_parts(tq):
                part(r0, rn, kn, True)

        @pl.when(i == nq - 1)
        def _():
            dk_ref[0] = (dk_sc[...] * math.log(2.0)).astype(bf16)
            dv_ref[0] = dv_sc[...].astype(bf16)

        @pl.when(t == n_pairs - 1)
        def _():
            dq_ref[0] = (dq_sc[...] * MLA_SCALE).astype(bf16)

    qi = lambda h, t, it_, jt_: (h, it_[t], 0)
    kj = lambda h, t, it_, jt_: (h, jt_[t], 0)
    oi = lambda h, t, it_, jt_: (it_[t], h)
    doi = lambda h, t, it_, jt_: (it_[t], h + do_col)
    return pl.pallas_call(
        body, name=name,
        grid_spec=pltpu.PrefetchScalarGridSpec(
            num_scalar_prefetch=2, grid=(H, n_pairs),
            in_specs=[pl.BlockSpec((1, tq, 256), qi), pl.BlockSpec((1, tq, 256), kj), pl.BlockSpec((1, tq, 128), kj),
                      pl.BlockSpec((tq, 128), oi), pl.BlockSpec((tq, 128), doi), pl.BlockSpec((1, tq, 128), qi)],
            out_specs=[pl.BlockSpec((1, S, 256), lambda h, t, it_, jt_: (h, 0, 0)), pl.BlockSpec((1, tq, 256), kj),
                       pl.BlockSpec((1, tq, 128), kj)],
            scratch_shapes=[pltpu.VMEM((S, 256), f32), pltpu.VMEM((tq, 256), f32), pltpu.VMEM((tq, 128), f32)]),
        out_shape=[SDS((H, S, 256), bf16), SDS((H, S, 256), bf16), SDS((H, S, 128), bf16)],
        compiler_params=_cp("parallel", "arbitrary"),
    )(it, jt, q, k, v, o, do, lse)


def mix_post_bwd(name, dq, dk, dv, wq, wkv, l, cq, ckv, gq, gkv, cs):
    H, S, _ = dq.shape
    tm = _tile(S, 512)

    def rms_bwd(dyn, c, g):
        r = lax.rsqrt(jnp.mean(c * c, axis=-1, keepdims=True) + RMS_EPS)
        ch = c * r
        dyg = dyn * g
        dc = r * (dyg - ch * jnp.mean(dyg * ch, axis=-1, keepdims=True))
        return dc, jnp.sum(dyn * ch, axis=0, keepdims=True)

    def body(dq_ref, dk_ref, dv_ref, wq_ref, wkv_ref, cq_ref, ckv_ref, gq_ref, gkv_ref, cs_ref,
             dqe_ref, dkv_ref, dh_ref, dgq_ref, dgkv_ref):
        csv = cs_ref[...]
        lane = lax.broadcasted_iota(jnp.int32, (tm, 128), 1)
        dcqn = jnp.zeros((tm, Q_LORA), f32)
        dkr = jnp.zeros((tm, 128), f32)
        for hd in range(H):
            dqh = dq_ref[hd].astype(f32)
            dqe = jnp.concatenate([dqh[:, :128], _half_sum(dqh[:, 128:]) * csv], axis=1).astype(bf16)
            dqe_ref[:, 256 * hd:256 * hd + 256] = dqe
            dcqn = dcqn + _dot_nt(dqe, wq_ref[0, hd])
            dkh = dk_ref[hd].astype(f32)
            dkv_ref[:, 256 * hd:256 * hd + 128] = dkh[:, :128].astype(bf16)
            dkv_ref[:, 256 * hd + 128:256 * hd + 256] = dv_ref[hd].astype(bf16)
            dkr = dkr + dkh[:, 128:]
        dckvn = _dot_nt(dkv_ref[...], wkv_ref[0])
        dblk = _half_sum(jnp.where(lane < 64, dkr, 0.0)) * csv
        dcq, dgq = rms_bwd(dcqn, cq_ref[...], gq_ref[...])
        dckv, dgkv = rms_bwd(dckvn, ckv_ref[...], gkv_ref[...])
        dh_ref[:, :256] = dcq.astype(bf16)
        dh_ref[:, 256:384] = dckv.astype(bf16)
        dh_ref[:, 384:] = dblk.astype(bf16)

        @pl.when(pl.program_id(0) == 0)
        def _():
            dgq_ref[...] = jnp.zeros_like(dgq_ref)
            dgkv_ref[...] = jnp.zeros_like(dgkv_ref)

        dgq_ref[...] += dgq
        dgkv_ref[...] += dgkv

    row = lambda i: (i, 0)
    hrow = lambda i: (0, i, 0)
    return pl.pallas_call(
        body, name=name, grid=(S // tm,),
        in_specs=[pl.BlockSpec((H, tm, 256), hrow), pl.BlockSpec((H, tm, 256), hrow), pl.BlockSpec((H, tm, 128), hrow),
                  pl.BlockSpec((1, H, Q_LORA, 256), lambda i: (l, 0, 0, 0)),
                  pl.BlockSpec((1, KV_LORA, H * 256), lambda i: (l, 0, 0)),
                  pl.BlockSpec((tm, Q_LORA), row), pl.BlockSpec((tm, KV_LORA), row),
                  pl.BlockSpec((1, Q_LORA), lambda i: (0, 0)), pl.BlockSpec((1, KV_LORA), lambda i: (0, 0)),
                  pl.BlockSpec((tm, 128), row)],
        out_specs=[pl.BlockSpec((tm, H * 256), row), pl.BlockSpec((tm, H * 256), row), pl.BlockSpec((tm, 512), row),
                   pl.BlockSpec((1, Q_LORA), lambda i: (0, 0)), pl.BlockSpec((1, KV_LORA), lambda i: (0, 0))],
        out_shape=[SDS((S, H * 256), bf16), SDS((S, H * 256), bf16), SDS((S, 512), bf16),
                   SDS((1, Q_LORA), f32), SDS((1, KV_LORA), f32)],
        compiler_params=_cp("arbitrary"),
    )(dq, dk, dv, wq, wkv, cq, ckv, gq, gkv, cs)


def _cross_probs(qb, kv_ref, hd):
    cols = slice(hd * MEM_HEAD_DIM, (hd + 1) * MEM_HEAD_DIM)
    s = _dot_nt(qb[:, cols], kv_ref[:, cols]) * MEM_SCALE
    e = jnp.exp(s - jnp.max(s, axis=-1, keepdims=True))
    return e / jnp.sum(e, axis=-1, keepdims=True)


def cross_fwd(name, xb, xf, wq, wo, l, kv, g, b):
    S = xb.shape[0]
    tm = _tile(S, 512)
    M = kv.shape[0]

    def body(x_ref, xf_ref, wq_ref, wo_ref, k_ref, v_ref, g_ref, b_ref, q_ref, o_ref, z_ref, y_ref, yb_ref):
        qb = _dot(x_ref[...], wq_ref[0]).astype(bf16)
        q_ref[...] = qb
        for hd in range(MEM_HEADS):
            cols = slice(hd * MEM_HEAD_DIM, (hd + 1) * MEM_HEAD_DIM)
            p = _cross_probs(qb, k_ref, hd)
            o_ref[:, cols] = _dot(p.astype(bf16), v_ref[:, cols]).astype(bf16)
        z = ALPHA * xf_ref[...] + _dot(o_ref[...], wo_ref[0])
        mu = jnp.mean(z, axis=-1, keepdims=True)
        zc = z - mu
        var = jnp.mean(zc * zc, axis=-1, keepdims=True)
        y = zc * lax.rsqrt(var + LN_EPS) * g_ref[...] + b_ref[...]
        z_ref[...] = z
        y_ref[...] = y
        yb_ref[...] = y.astype(bf16)

    row = lambda i: (i, 0)
    wspec = pl.BlockSpec((1, D_MODEL, D_MODEL), lambda i: (l, 0, 0))
    vec = pl.BlockSpec((1, D_MODEL), lambda i: (0, 0))
    blk = pl.BlockSpec((tm, D_MODEL), row)
    return pl.pallas_call(
        body, name=name, grid=(S // tm,),
        in_specs=[blk, blk, wspec, wspec, pl.BlockSpec((M, D_MODEL), lambda i: (0, 0)),
                  pl.BlockSpec((M, D_MODEL), lambda i: (0, 1)), vec, vec],
        out_specs=[blk] * 5,
        out_shape=[SDS((S, D_MODEL), bf16), SDS((S, D_MODEL), bf16), SDS((S, D_MODEL), f32), SDS((S, D_MODEL), f32),
                   SDS((S, D_MODEL), bf16)],
        compiler_params=_cp("parallel"),
    )(xb, xf, wq, wo, kv, kv, g, b)


def cross_bwd(name, dzb, wo, l, qb, kv, deps=()):
    S = dzb.shape[0]
    tm = _tile(S, 512)
    M = kv.shape[0]

    def body(dz_ref, wo_ref, q_ref, k_ref, v_ref, dq_ref, dkv_ref):
        @pl.when(pl.program_id(0) == 0)
        def _():
            dkv_ref[...] = jnp.zeros_like(dkv_ref)

        do = _dot_nt(dz_ref[...], wo_ref[0]).astype(bf16)
        qv = q_ref[...]
        for hd in range(MEM_HEADS):
            cols = slice(hd * MEM_HEAD_DIM, (hd + 1) * MEM_HEAD_DIM)
            vcols = slice(D_MODEL + hd * MEM_HEAD_DIM, D_MODEL + (hd + 1) * MEM_HEAD_DIM)
            p = _cross_probs(qv, k_ref, hd)
            doh = do[:, cols]
            dkv_ref[:, vcols] += _dot_tn(p.astype(bf16), doh)
            dp = _dot_nt(doh, v_ref[:, cols])
            ds = (p * (dp - jnp.sum(dp * p, axis=-1, keepdims=True)) * MEM_SCALE).astype(bf16)
            dq_ref[:, cols] = _dot(ds, k_ref[:, cols]).astype(bf16)
            dkv_ref[:, cols] += _dot_tn(ds, qv[:, cols])

    row = lambda i: (i, 0)
    blk = pl.BlockSpec((tm, D_MODEL), row)
    return pl.pallas_call(
        _with_deps(body, 5, deps), name=name, grid=(S // tm,),
        in_specs=[blk, pl.BlockSpec((1, D_MODEL, D_MODEL), lambda i: (l, 0, 0)), blk,
                  pl.BlockSpec((M, D_MODEL), lambda i: (0, 0)), pl.BlockSpec((M, D_MODEL), lambda i: (0, 1))]
        + [_DEP_SPEC] * len(deps),
        out_specs=[blk, pl.BlockSpec((M, 2 * D_MODEL), lambda i: (0, 0))],
        out_shape=[SDS((S, D_MODEL), bf16), SDS((M, 2 * D_MODEL), f32)],
        compiler_params=_cp("arbitrary"),
    )(dzb, wo, qb, kv, kv, *deps)


def adamw(name, w, g, m, v, deps=()):
    shape = w.shape
    cols = shape[-1]
    rows = math.prod(shape[:-1])
    tr = _row_tile(rows, cols, target=2 * 2**20)
    c1 = 1.0 - ADAM_B1 ** ADAM_STEP
    c2 = 1.0 - ADAM_B2 ** ADAM_STEP

    def body(w_ref, g_ref, m_ref, v_ref, d_ref, nm_ref, nv_ref):
        gv = g_ref[...]
        nm = ADAM_B1 * m_ref[...] + (1.0 - ADAM_B1) * gv
        nv = ADAM_B2 * v_ref[...] + (1.0 - ADAM_B2) * (gv * gv)
        d_ref[...] = -ADAM_LR * ((nm / c1) / (jnp.sqrt(nv / c2) + ADAM_EPS) + ADAM_WD * w_ref[...])
        nm_ref[...] = nm
        nv_ref[...] = nv

    blk = pl.BlockSpec((tr, cols), lambda i: (i, 0))
    flat = SDS((rows, cols), f32)
    outs = pl.pallas_call(
        _with_deps(body, 4, deps), name=name, grid=(rows // tr,), in_specs=[blk] * 4 + [_DEP_SPEC] * len(deps),
        out_specs=[blk] * 3, out_shape=[flat] * 3, compiler_params=_cp("parallel"),
    )(*[a.reshape(rows, cols) for a in (w, g, m, v)], *deps)
    return [o.reshape(shape) for o in outs]


def _me():
    return lax.axis_index("x"), lax.axis_index("y"), lax.axis_index("c")


def _other_chips(x, y):
    return [(1 - x, y), (x, 1 - y), (1 - x, 1 - y)]


def _pair_share_each(owners, bufs, sems, mine, act):
    x, y, c = _me()
    for o in range(2):
        slots = [(a, lyr) for a in range(len(bufs)) for lyr in range(DEPTH) if owners[a][lyr] == o]

        @pl.when((c == o) if mine else (c != o))
        def _(slots=slots):
            for a, lyr in slots:
                slot = bufs[a].at[lyr]
                act(_rcopy(slot, slot, sems[0].at[2 * a + lyr], sems[1].at[2 * a + lyr], (x, y, 1 - c)))


def pair_share_start(name, sums, owners, after):
    def body_fn(b_in, s_in, s_out):
        _pair_share_each(owners, b_in, s_out, True, lambda cp: cp.start())

    outs, sems, token = _split_call(name, body_fn, list(sums), [], [2 * len(sums)] * 2, after)
    return (outs, sems[0], sems[1], owners), token


def pair_share_wait(name, st, after):
    bufs, send, recv, owners = st

    def body_fn(b_in, s_in, s_out):
        _pair_share_each(owners, b_in, s_in, True, lambda cp: cp.wait_send())
        _pair_share_each(owners, b_in, s_in, False, lambda cp: cp.wait_recv())

    outs, _, _ = _split_call(name, body_fn, list(bufs), [send, recv], [], after)
    return outs


def allsum_small(name, v, deps=()):
    R = v.shape[0]

    def body(v_ref, o_ref, all_ref, send_sems, recv_sems, local_sem):
        x, y, c = _me()
        me, sibling = (x, y, c), (x, y, 1 - c)
        chips = _other_chips(x, y)

        def rows(px, py, pc):
            return all_ref.at[4 * px + 2 * py + pc]

        def copy(k, block, to, src=None):
            return pltpu.make_async_remote_copy(
                src_ref=rows(*block) if src is None else src, dst_ref=rows(*block),
                send_sem=send_sems.at[k], recv_sem=recv_sems.at[k], device_id=to, device_id_type=MESH)

        mine = pltpu.make_async_copy(v_ref, rows(*me), local_sem)
        mine.start()
        first = [copy(0, me, sibling, src=v_ref)]
        first += [copy(1 + j, me, (*chip, c), src=v_ref) for j, chip in enumerate(chips)]
        for cp in first:
            cp.start()
        passed = [copy(4 + j, (*chip, c), sibling) for j, chip in enumerate(chips)]
        for j, chip in enumerate(chips):
            copy(1 + j, (*chip, c), me).wait_recv()
            passed[j].start()
        copy(0, sibling, me).wait_recv()
        for j, chip in enumerate(chips):
            copy(4 + j, (*chip, 1 - c), me).wait_recv()
        for cp in first + passed:
            cp.wait_send()
        mine.wait()
        acc = all_ref[0]
        for d in range(1, 8):
            acc = acc + all_ref[d]
        o_ref[...] = acc

    return pl.pallas_call(
        _with_deps(body, 1, deps), name=name,
        in_specs=[pl.BlockSpec(memory_space=pltpu.VMEM)] + [_DEP_SPEC] * len(deps),
        out_specs=pl.BlockSpec(memory_space=pltpu.VMEM),
        out_shape=SDS((R, 128), f32),
        scratch_shapes=[pltpu.VMEM((8, R, 128), f32), pltpu.SemaphoreType.DMA((7,)), pltpu.SemaphoreType.DMA((7,)),
                        pltpu.SemaphoreType.DMA],
        compiler_params=pltpu.CompilerParams(vmem_limit_bytes=V7X_VMEM_LIMIT),
    )(v, *deps)


def _swap_half(r):
    return jnp.concatenate([-r[..., 32:], r[..., :32]], axis=-1)


def _unswap_add(p, qg):
    return p + jnp.concatenate([qg[..., 32:], -qg[..., :32]], axis=-1)


def _block_diag(pw):
    L = pw.shape[0]
    out = jnp.zeros((L, 256, 256), pw.dtype)
    for gi in range(4):
        out = out.at[:, 64 * gi:64 * gi + 64, 64 * gi:64 * gi + 64].set(pw[:, gi])
    return out


def _to_col_shards(w):
    *lead, K, N = w.shape
    nl = len(lead)
    return w.reshape(*lead, K, N_CHIPS, N // N_CHIPS).transpose(*range(nl), nl + 1, nl, nl + 2)


def _from_col_shards(w):
    *lead, C, K, n = w.shape
    nl = len(lead)
    return w.transpose(*range(nl), nl + 1, nl, nl + 2).reshape(*lead, K, C * n)


_HBM_SPEC = pl.BlockSpec(memory_space=pltpu.HBM)
_SEM_SPEC = pl.BlockSpec(memory_space=pltpu.SEMAPHORE)
_ANY_SPEC = pl.BlockSpec(memory_space=pl.ANY)
_DATAFLOW = pltpu.SideEffectType.DATAFLOW_SIDE_EFFECTING


def _split_call(name, body_fn, bufs, sems_in, sems_out_sizes, after):
    nb, ni, no = len(bufs), len(sems_in), len(sems_out_sizes)
    afters = () if after is None else tuple(after) if isinstance(after, (tuple, list)) else (after,)

    def body(*refs):
        k = nb + ni + len(afters)
        body_fn(refs[:nb], refs[nb:nb + ni], refs[k:k + no])
        refs[-1][...] = jnp.zeros((8, 128), f32)

    outs = pl.pallas_call(
        body, name=name,
        in_specs=[_HBM_SPEC] * nb + [_SEM_SPEC] * ni + [_ANY_SPEC] * len(afters),
        out_specs=[_SEM_SPEC] * no + [_HBM_SPEC] * nb + [pl.BlockSpec(memory_space=pltpu.VMEM)],
        out_shape=[pltpu.SemaphoreType.DMA((s,)) for s in sems_out_sizes]
        + [pltpu.HBM(b.shape, b.dtype) for b in bufs] + [SDS((8, 128), f32)],
        input_output_aliases={i: no + i for i in range(nb)},
        compiler_params=pltpu.CompilerParams(has_side_effects=_DATAFLOW),
    )(*[pltpu.with_memory_space_constraint(b, pltpu.HBM) for b in bufs], *sems_in, *afters)
    return list(outs[no:no + nb]), list(outs[:no]), outs[-1]


def _rcopy(src, dst, ssem, rsem, to):
    return pltpu.make_async_remote_copy(src_ref=src, dst_ref=dst, send_sem=ssem, recv_sem=rsem, device_id=to,
                                        device_id_type=MESH)


def gather_start(name, groups, after):
    flat = [b for bufs, _ in groups for b in bufs]
    sizes = [3 * len(bufs) for bufs, _ in groups for _ in range(2)]

    def body_fn(b_in, s_in, s_out):
        x, y, c = _me()
        q = 2 * x + y
        chips = _other_chips(x, y)
        pos = 0
        for gi, (bufs, owner) in enumerate(groups):
            refs = b_in[pos:pos + len(bufs)]
            pos += len(bufs)

            @pl.when(c == owner)
            def _(refs=refs, send=s_out[2 * gi], recv=s_out[2 * gi + 1]):
                for a, r in enumerate(refs):
                    for k, (cx, cy) in enumerate(chips):
                        _rcopy(r.at[q], r.at[q], send.at[3 * a + k], recv.at[3 * a + k], (cx, cy, c)).start()

    outs, sems, token = _split_call(name, body_fn, flat, [], sizes, after)
    res, pos = [], 0
    for gi, (bufs, owner) in enumerate(groups):
        res.append((outs[pos:pos + len(bufs)], sems[2 * gi], sems[2 * gi + 1], owner))
        pos += len(bufs)
    return res, token


def gather_forward(name, grp, after):
    bufs, send, recv, owner = grp
    n3 = 3 * len(bufs)

    def body_fn(b_in, s_in, s_out):
        x, y, c = _me()
        q = 2 * x + y
        sibling = (x, y, 1 - c)
        chips = _other_chips(x, y)

        @pl.when(c == owner)
        def _():
            for a, r in enumerate(b_in):
                for k, (cx, cy) in enumerate(chips):
                    i = 3 * a + k
                    land = r.at[2 * cx + cy]
                    _rcopy(r.at[q], r.at[q], s_in[0].at[i], s_in[1].at[i], (cx, cy, c)).wait_send()
                    _rcopy(land, land, s_in[0].at[i], s_in[1].at[i], (cx, cy, c)).wait_recv()
                    _rcopy(land, land, s_out[0].at[i], s_out[1].at[i], sibling).start()

    outs, sems, token = _split_call(name, body_fn, bufs, [send, recv], [n3, n3], after)
    return (outs, sems[0], sems[1], owner), token


def gather_finish(name, grp, after):
    bufs, fsend, frecv, owner = grp

    def body_fn(b_in, s_in, s_out):
        x, y, c = _me()
        sibling = (x, y, 1 - c)
        chips = _other_chips(x, y)

        def each(wait):
            for a, r in enumerate(b_in):
                for k, (cx, cy) in enumerate(chips):
                    land = r.at[2 * cx + cy]
                    wait(_rcopy(land, land, s_in[0].at[3 * a + k], s_in[1].at[3 * a + k], sibling))

        @pl.when(c == owner)
        def _():
            each(lambda cp: cp.wait_send())

        @pl.when(c != owner)
        def _():
            each(lambda cp: cp.wait_recv())

    outs, _, _ = _split_call(name, body_fn, bufs, [fsend, frecv], [], after)
    return outs


def _by_owner(owners):
    return [[a for a, o_ in enumerate(owners) if o_ == o] for o in range(2)]


def pair_send_start(name, gs, owners, after):
    n = len(gs)
    lands = [lax.empty(g.shape, g.dtype) for g in gs]

    def body_fn(b_in, s_in, s_out):
        x, y, c = _me()
        for o, idx in enumerate(_by_owner(owners)):
            @pl.when(c == 1 - o)
            def _(o=o, idx=idx):
                for a in idx:
                    _rcopy(b_in[a], b_in[n + a], s_out[0].at[a], s_out[1].at[a], (x, y, o)).start()

    outs, sems, token = _split_call(name, body_fn, list(gs) + lands, [], [n, n], after)
    return (outs[:n], outs[n:], sems[0], sems[1], owners), token


def pair_send_wait(name, st, after):
    gs, lands, send, recv, owners = st
    n = len(gs)

    def body_fn(b_in, s_in, s_out):
        x, y, c = _me()
        for o, idx in enumerate(_by_owner(owners)):
            @pl.when(c == 1 - o)
            def _(o=o, idx=idx):
                for a in idx:
                    _rcopy(b_in[a], b_in[n + a], s_in[0].at[a], s_in[1].at[a], (x, y, o)).wait_send()

            @pl.when(c == o)
            def _(o=o, idx=idx):
                for a in idx:
                    _rcopy(b_in[a], b_in[n + a], s_in[0].at[a], s_in[1].at[a], (x, y, 1 - o)).wait_recv()

    outs, _, _ = _split_call(name, body_fn, list(gs) + list(lands), [send, recv], [], after)
    return outs[:n], outs[n:]


def chip_exchange_start(name, psums, owners, after):
    n = len(psums)
    lands = [lax.empty((3,) + p.shape[1:], p.dtype) for p in psums]

    def body_fn(b_in, s_in, s_out):
        x, y, c = _me()
        chips = _other_chips(x, y)
        for o, idx in enumerate(_by_owner(owners)):
            @pl.when(c == o)
            def _(idx=idx):
                for a in idx:
                    for k, (cx, cy) in enumerate(chips):
                        _rcopy(b_in[a].at[2 * cx + cy], b_in[n + a].at[k], s_out[0].at[3 * a + k],
                               s_out[1].at[3 * a + k], (cx, cy, c)).start()

    outs, sems, token = _split_call(name, body_fn, list(psums) + lands, [], [3 * n, 3 * n], after)
    return (outs[:n], outs[n:], sems[0], sems[1], owners), token


def chip_exchange_wait(name, st, after):
    psums, lands, send, recv, owners = st
    n = len(psums)

    def body_fn(b_in, s_in, s_out):
        x, y, c = _me()
        chips = _other_chips(x, y)
        for o, idx in enumerate(_by_owner(owners)):
            @pl.when(c == o)
            def _(idx=idx):
                for a in idx:
                    for k, (cx, cy) in enumerate(chips):
                        cp = _rcopy(b_in[a].at[2 * cx + cy], b_in[n + a].at[k], s_in[0].at[3 * a + k],
                                    s_in[1].at[3 * a + k], (cx, cy, c))
                        cp.wait_send()
                        cp.wait_recv()

    outs, _, _ = _split_call(name, body_fn, list(psums) + list(lands), [send, recv], [], after)
    return outs[:n], outs[n:]


def pair_sum(name, g, recv, flag):
    shape = g.shape
    cols = shape[-1]
    rows = math.prod(shape[:-1])
    tr = _row_tile(rows, cols, target=4 * 2**20)

    def body(f_ref, g_ref, r_ref, o_ref):
        o_ref[...] = (g_ref[...] + r_ref[...]).astype(bf16)

    blk = pl.BlockSpec((tr, cols), lambda i, f_ref: (i * f_ref[0], 0))
    out = pl.pallas_call(
        body, name=name,
        grid_spec=pltpu.PrefetchScalarGridSpec(num_scalar_prefetch=1, grid=(rows // tr,), in_specs=[blk, blk],
                                               out_specs=blk),
        out_shape=SDS((rows, cols), bf16), compiler_params=_cp("arbitrary"),
    )(flag, g.reshape(rows, cols), recv.reshape(rows, cols))
    return out.reshape(shape)


def chip_sum(name, psum, recv, qf_arr, layer, prev):
    shard = psum.shape[1:]
    cols = shard[-1]
    rows = math.prod(shard[:-1])
    tr = _row_tile(rows, cols, target=4 * 2**20)

    def body(qf_ref, p_ref, r_ref, *rest):
        rest[-1][0] = ((p_ref[0].astype(f32) + r_ref[0].astype(f32)) + r_ref[1].astype(f32)) + r_ref[2].astype(f32)

    in_specs = [pl.BlockSpec((1, tr, cols), lambda i, qf: (qf[0], i * qf[1], 0)),
                pl.BlockSpec((3, tr, cols), lambda i, qf: (0, i * qf[1], 0))]
    args = [qf_arr, psum.reshape(N_CHIPS, rows, cols), recv.reshape(3, rows, cols)]
    aliases = {}
    if prev is not None:
        in_specs.append(pl.BlockSpec(memory_space=pl.ANY))
        args.append(prev.reshape(DEPTH, rows, cols))
        aliases = {3: 0}
    out = pl.pallas_call(
        body, name=name,
        grid_spec=pltpu.PrefetchScalarGridSpec(
            num_scalar_prefetch=1, grid=(rows // tr,), in_specs=in_specs,
            out_specs=pl.BlockSpec((1, tr, cols), lambda i, qf: (layer, i * qf[1], 0))),
        out_shape=SDS((DEPTH, rows, cols), f32), input_output_aliases=aliases, compiler_params=_cp("arbitrary"),
    )(*args)
    return out.reshape((DEPTH,) + shard)


W_NAMES = ("f1w13", "f1w2", "win", "wuq", "wukv", "wout", "mwq", "mwkv", "mwo", "f2w13", "f2w2")
MIX_NAMES = ("win", "wuq", "wukv")
MID_NAMES = ("wout", "mwq", "mwkv", "mwo")
FFN2_NAMES = ("f2w13", "f2w2")
REDUCER = (dict(f1w13=0, f1w2=1, f2w13=0, win=0, wuq=0, wukv=0, f2w2=1, mwkv=1, wout=1, mwq=1, mwo=1),
           dict(f1w13=0, f2w2=0, mwkv=0, wout=0, f2w13=1, f1w2=1, mwq=1, mwo=1, win=1, wuq=1, wukv=1))


def kernel(x, mem, positions, ln_g, ln_b, ffn1_w13, ffn1_w2, w_in, pool_w, pool_scale, q_norm_g, w_uq, kv_norm_g, w_ukv, w_out, mem_wq, mem_wkv, mem_wo, ffn2_w13, ffn2_w2, loss_target, m_ln_g, m_ln_b, m_ffn1_w13, m_ffn1_w2, m_w_in, m_pool_w, m_pool_scale, m_q_norm_g, m_w_uq, m_kv_norm_g, m_w_ukv, m_w_out, m_mem_wq, m_mem_wkv, m_mem_wo, m_ffn2_w13, m_ffn2_w2, v_ln_g, v_ln_b, v_ffn1_w13, v_ffn1_w2, v_w_in, v_pool_w, v_pool_scale, v_q_norm_g, v_w_uq, v_kv_norm_g, v_w_ukv, v_w_out, v_mem_wq, v_mem_wkv, v_mem_wo, v_ffn2_w13, v_ffn2_w2):
    L = DEPTH
    qx, qy, _ = _me()
    chip = 2 * qx + qy
    vec = lambda a: a.reshape(1, -1)

    shards = dict(zip(W_NAMES, (ffn1_w13, ffn1_w2, w_in, w_uq, w_ukv, w_out, mem_wq, mem_wkv, mem_wo, ffn2_w13, ffn2_w2)))

    def place(sh, slot):
        return lax.dynamic_update_slice(lax.empty((N_CHIPS,) + sh.shape, bf16), sh.astype(bf16)[None],
                                        (slot,) + (0,) * sh.ndim)

    first = ("f1w13", "f1w2")
    bufs = [dict(), dict()]
    for n in first:
        bufs[0][n] = place(shards[n][0], chip)
    gw = [dict(), dict()]
    (g0, g_w2), tok = gather_start("gather_a_start", [([bufs[0]["f1w13"]], 0), ([bufs[0]["f1w2"]], 0)], None)
    chip_then = chip + tok[0, 0].astype(jnp.int32)
    for l in range(L):
        for n in W_NAMES:
            if n not in bufs[l]:
                bufs[l][n] = place(shards[n][l], chip_then)
    others = tuple(bufs[l][n] for l in range(L) for n in W_NAMES if (l, n) not in ((0, first[0]), (0, first[1])))
    g0, tok = gather_forward("gather_a_forward", g0, others)

    ln_pad = jnp.zeros((2, L, 4, N_CHIPS, D_MODEL // N_CHIPS), f32)
    ln_pad = lax.dynamic_update_slice(ln_pad, jnp.stack([ln_g, ln_b])[:, :, :, None, :], (0, 0, 0, chip, 0))
    ln_sum = allsum_small("allsum_ln", ln_pad.reshape(-1, 128), (tok,))
    ln_full = (ln_sum * 0.5).reshape(2, L, 4, D_MODEL)
    lng, lnb = ln_full[0], ln_full[1]

    (gw[0]["f1w13"],) = gather_finish("gather_a_finish", g0, ln_sum)
    (g_mix, g_mid, g_ffn2, g_l1), tok_b = gather_start(
        "gather_b_start",
        [([bufs[0][n] for n in MIX_NAMES], 0), ([bufs[0][n] for n in MID_NAMES], 0), ([bufs[0][n] for n in FFN2_NAMES], 0),
         ([bufs[1][n] for n in W_NAMES], 1)], ln_sum)

    half = QK_ROPE // 2
    inv_freq = ROPE_BASE ** (-jnp.arange(half, dtype=f32) / half)
    ang = positions[0].astype(f32)[:, None] * inv_freq
    cos, sin = jnp.cos(ang), jnp.sin(ang)
    cs = jnp.concatenate([cos, cos, sin, sin], axis=-1)

    memb = mem[0].astype(bf16)
    xf = x[0]
    xb = xf.astype(bf16)
    dep = (tok_b,)

    saved, W = [], [None, None]
    for l in range(L):
        sv = {}
        if l == 1:
            gl1 = gather_finish("gather_l1_finish", g_l1, xb)
            gw[1] = dict(zip(W_NAMES, gl1))
        sv["x0b"] = xb
        f1w13 = gw[l]["f1w13"][None]
        gate, up, act = ffn_up(f"ffn1_up_{l}", xb, f1w13, 0, dep)
        dep = ()
        if l == 0:
            g_w2, _ = gather_forward("gather_w2_forward", g_w2, act)
            (gw[0]["f1w2"],) = gather_finish("gather_w2_finish", g_w2, act)
            g_mix, _ = gather_forward("gather_mix_forward", g_mix, act)
        z1, x1f, x1b = proj_res_ln(f"ffn1_down_{l}", [act], [gw[l]["f1w2"].reshape(1, D_FF, D_MODEL)], [0], xf,
                                   vec(lng[l, 0]), vec(lnb[l, 0]), 0.5)
        sv.update(gate1=gate, up1=up, act1=act, z1=z1, x1b=x1b)
        if l == 0:
            gw[0].update(zip(MIX_NAMES, gather_finish("gather_mix_finish", g_mix, x1b)))
            g_mid, _ = gather_forward("gather_mid_forward", g_mid, x1b)
        win = gw[l]["win"].reshape(D_MODEL, D_IN)
        win_ext = jnp.concatenate([win, _swap_half(win[:, D_IN - QK_ROPE:])], axis=-1)[None]
        wuq = _from_col_shards(gw[l]["wuq"]).reshape(Q_LORA, MLA_HEADS, QK_NOPE + QK_ROPE)
        wq_ext = jnp.concatenate([wuq, _swap_half(wuq[..., QK_NOPE:])], axis=-1).transpose(1, 0, 2)[None]
        wukv = _from_col_shards(gw[l]["wukv"])[None]
        wbd = _block_diag(pool_w[l][None].astype(bf16))[0]
        u, cq, ckv, cqn, ckvn, q, k, v = mix_pre(f"mix_pre_{l}", x1b, win_ext, wq_ext, wukv, 0,
                                                   vec(q_norm_g[l]), vec(kv_norm_g[l]), cs)
        dpool, ypool = pool_fwd(f"pool_fwd_{l}", u, wbd, vec(pool_scale[l]))
        o, lse = mla_attn_fwd(f"mla_fwd_{l}", q, k, v)
        if l == 0:
            gw[0].update(zip(MID_NAMES, gather_finish("gather_mid_finish", g_mid, o)))
            g_ffn2, tok_f = gather_forward("gather_ffn2_forward", g_ffn2, o)
            g_l1, tok_l = gather_forward("gather_l1_forward", g_l1, o)
            dep = (tok_f, tok_l)
        wout = gw[l]["wout"].reshape(D_MODEL, D_MODEL)
        wout_pool, wout_mla = wout[None, :POOL_WIDTH], wout[None, POOL_WIDTH:]
        mwq = gw[l]["mwq"].reshape(1, D_MODEL, D_MODEL)
        mwo = gw[l]["mwo"].reshape(1, D_MODEL, D_MODEL)
        mwkv = gw[l]["mwkv"][None]
        z2, x2f, x2b = proj_res_ln(f"mix_out_{l}", [ypool, o], [wout_pool, wout_mla], [0, 0], x1f,
                                   vec(lng[l, 1]), vec(lnb[l, 1]), 1.0, dep)
        dep = ()
        sv.update(cq=cq, ckv=ckv, cqn=cqn, ckvn=ckvn, q=q, k=k, v=v, dpool=dpool, ypool=ypool, o=o, lse=lse, z2=z2, x2b=x2b)
        kvm = mm_nn_shard(f"mem_kv_{l}", memb, mwkv, 0)
        cq_, co_, z3, x3f, x3b = cross_fwd(f"cross_fwd_{l}", x2b, x2f, mwq, mwo, 0, kvm, vec(lng[l, 2]), vec(lnb[l, 2]))
        sv.update(kvm=kvm, crq=cq_, cro=co_, z3=z3, x3b=x3b)
        if l == 0:
            gw[0].update(zip(FFN2_NAMES, gather_finish("gather_ffn2_finish", g_ffn2, x3b)))
        f2w13 = gw[l]["f2w13"][None]
        f2w2 = gw[l]["f2w2"].reshape(1, D_FF, D_MODEL)
        gate, up, act = ffn_up(f"ffn2_up_{l}", x3b, f2w13, 0)
        z4, xf, xb = proj_res_ln(f"ffn2_down_{l}", [act], [f2w2], [0], x3f, vec(lng[l, 3]), vec(lnb[l, 3]), 0.5)
        sv.update(gate2=gate, up2=up, act2=act, z4=z4)
        W[l] = dict(f1w13=f1w13, f1w2=gw[l]["f1w2"].reshape(1, D_FF, D_MODEL), win_ext=win_ext, wq_ext=wq_ext, wukv=wukv,
                    wbd=wbd, wout=wout[None], mwq=mwq, mwo=mwo, f2w13=f2w13, f2w2=f2w2)
        saved.append(sv)

    dln = {}
    dzb, dres, *dln[L - 1, 3], loss_blk = loss_grad("loss_grad", xf, loss_target[0],
                                                   (saved[L - 1]["z4"], vec(lng[L - 1, 3]), 0.5))
    loss = lax.psum(loss_blk[0, 0], ("x", "y", "c"))

    row_shards = lambda a: a.reshape(N_CHIPS, a.shape[0] // N_CHIPS, a.shape[1])
    small = {k_: [None] * L for k_ in ("pool_w", "pool_scale", "gq", "gkv", "lng", "lnb")}
    rest_names = [n for n in W_NAMES if n not in ("f1w13", "f1w2")]
    core = lax.axis_index("c")
    flags = [jnp.reshape(core == o, (1,)).astype(jnp.int32) for o in range(2)]
    qfs = [jnp.stack([chip, (core == o).astype(jnp.int32)]).astype(jnp.int32) for o in range(2)]

    def red_begin(tag, names, gs, layer):
        owners = [REDUCER[layer][n] for n in names]
        st, tok_ = pair_send_start(f"pair_send_start_{tag}", gs, owners, None)
        return (st, owners), tok_

    def red_mid(tag, sto, after):
        st, owners = sto
        gs_, lands_ = pair_send_wait(f"pair_send_wait_{tag}", st, after)
        ps = [pair_sum(f"pair_sum_{tag}_{a}", g_, r_, flags[o]) for a, (g_, r_, o) in enumerate(zip(gs_, lands_, owners))]
        st, tok_ = chip_exchange_start(f"chip_exchange_start_{tag}", ps, owners, None)
        return (st, owners), tok_

    def red_end(tag, sto, layer, prevs, after):
        st, owners = sto
        ps, lands_ = chip_exchange_wait(f"chip_exchange_wait_{tag}", st, after)
        return [chip_sum(f"chip_sum_{tag}_{a}", p_, r_, qfs[o], layer, s_)
                for a, (p_, r_, s_, o) in enumerate(zip(ps, lands_, prevs, owners))]

    def share_start(tag, names, sums_):
        return pair_share_start(f"pair_share_start_{tag}", sums_, [(REDUCER[0][n], REDUCER[1][n]) for n in names], None)

    st_p1 = st_c1 = st_pa = st_ca = None
    for l in reversed(range(L)):
        sv, w = saved[l], W[l]
        g = {}
        dh = ffn_bwd_da(f"ffn2_bwd_da_{l}", dzb, w["f2w2"], 0, sv["gate2"], sv["up2"], dep)
        dep = ()
        g["f2w2"] = row_shards(mm_tn(f"ffn2_dw2_{l}", sv["act2"], dzb))
        g["f2w13"] = mm_tn(f"ffn2_dw13_{l}", sv["x3b"], dh, True)
        dzb, dres, *dln[l, 2] = ffn_dx(f"ffn2_dx_{l}", dh, w["f2w13"], 0, dres, (sv["z3"], vec(lng[l, 2]), 1.0))
        if l == 0:
            st_c1, tok = red_mid("l1", st_p1, dzb)
            dep = (tok, g["f2w2"], g["f2w13"])
        dqc, dkvm = cross_bwd(f"cross_bwd_{l}", dzb, w["mwo"], 0, sv["crq"], sv["kvm"], dep)
        dep = ()
        g["mwo"] = row_shards(mm_tn(f"cross_dwo_{l}", sv["cro"], dzb))
        g["mwq"] = row_shards(mm_tn(f"cross_dwq_{l}", sv["x2b"], dqc))
        g["mwkv"] = mm_tn(f"cross_dwkv_{l}", memb, dkvm, True)
        dzb, dres, *dln[l, 1] = mm_nt_res(f"cross_dx_{l}", [dqc], [w["mwq"]], [0], dres, f32,
                                          (sv["z2"], vec(lng[l, 1]), 1.0))
        dcat = mm_nt_res(f"mix_dcat_{l}", [dzb], [w["wout"]], [0], None, bf16)
        dwo_p = mm_tn(f"mix_dwout_pool_{l}", sv["ypool"], dzb)
        dwo_m = mm_tn(f"mix_dwout_mla_{l}", sv["o"], dzb)
        g["wout"] = row_shards(jnp.concatenate([dwo_p, dwo_m], axis=0))
        dq, dk, dv = mla_attn_bwd(f"mla_bwd_{l}", sv["q"], sv["k"], sv["v"], sv["o"], dcat, sv["lse"], POOL_WIDTH // 128)
        dqe, dkv, dh_rest, dgq, dgkv = mix_post_bwd(f"mix_post_bwd_{l}", dq, dk, dv, w["wq_ext"], w["wukv"], 0, sv["cq"],
                                                     sv["ckv"], vec(q_norm_g[l]), vec(kv_norm_g[l]), cs)
        du, dyw, dscale = pool_bwd(f"pool_bwd_{l}", dcat, sv["dpool"], w["wbd"], vec(pool_scale[l]))
        dwq_e = mm_tn(f"mix_dwuq_{l}", sv["cqn"], dqe).reshape(Q_LORA, MLA_HEADS, 256)
        g["wuq"] = _to_col_shards(jnp.concatenate(
            [dwq_e[..., :QK_NOPE], _unswap_add(dwq_e[..., QK_NOPE:QK_NOPE + QK_ROPE], dwq_e[..., QK_NOPE + QK_ROPE:])],
            axis=-1).reshape(Q_LORA, MLA_HEADS * (QK_NOPE + QK_ROPE)))
        g["wukv"] = _to_col_shards(mm_tn(f"mix_dwukv_{l}", sv["ckvn"], dkv))
        dwbd = mm_tn(f"pool_dw_{l}", sv["dpool"], dyw)
        small["pool_w"][l] = jnp.stack([dwbd[64 * gi:64 * gi + 64, 64 * gi:64 * gi + 64] for gi in range(4)])
        small["pool_scale"][l], small["gq"][l], small["gkv"][l] = dscale[0], dgq[0], dgkv[0]
        dh_ext = jnp.concatenate([du, dh_rest], axis=1)
        dwin_e = mm_tn(f"mix_dwin_{l}", sv["x1b"], dh_ext)
        g["win"] = row_shards(jnp.concatenate(
            [dwin_e[:, :D_IN - QK_ROPE], _unswap_add(dwin_e[:, D_IN - QK_ROPE:D_IN], dwin_e[:, D_IN:])], axis=-1))
        dzb, dres, *dln[l, 0] = mm_nt_res(f"mix_dx_{l}", [dh_ext], [w["win_ext"]], [0], dres, f32,
                                          (sv["z1"], vec(lng[l, 0]), 0.5))
        if l == 0:
            st_pa, tok = red_begin("a0", rest_names, [g[n] for n in rest_names], 0)
            dep = (tok,)
        dh = ffn_bwd_da(f"ffn1_bwd_da_{l}", dzb, w["f1w2"], 0, sv["gate1"], sv["up1"], dep)
        dep = ()
        if l == 0:
            grad_x = ffn_dx(f"ffn1_dx_{l}", dh, w["f1w13"], 0, dres)[None]
            for l_ in range(L):
                small["lng"][l_] = jnp.concatenate([dln[l_, k_][0] for k_ in range(4)], axis=0)
                small["lnb"][l_] = jnp.concatenate([dln[l_, k_][1] for k_ in range(4)], axis=0)
            rep = [jnp.stack(small[k_]).reshape(-1) for k_ in ("pool_w", "pool_scale", "gq", "gkv", "lng", "lnb")]
            sizes = [r.shape[0] for r in rep]
            packed = jnp.concatenate(rep)
            packed = jnp.pad(packed, (0, (-packed.shape[0]) % 1024)).reshape(-1, 128)
            tot = allsum_small("allsum_small_grads", packed, (grad_x,)).reshape(-1)
            st_ca, tok = red_mid("a0", st_pa, (grad_x, tot))
            dep = (tok,)
        else:
            below = ffn_dx(f"ffn1_dx_{l}", dh, w["f1w13"], 0, dres, (saved[l - 1]["z4"], vec(lng[l - 1, 3]), 0.5))
            dln[l - 1, 3] = below[2:]
        g["f1w2"] = row_shards(mm_tn(f"ffn1_dw2_{l}", sv["act1"], dzb, False, dep))
        g["f1w13"] = mm_tn(f"ffn1_dw13_{l}", sv["x0b"], dh, True, dep)
        dep = ()
        if l > 0:
            dzb, dres = below[:2]
        if l == 1:
            st_p1, tok = red_begin("l1", W_NAMES, [g[n] for n in W_NAMES], 1)
            dep = (tok,)

    st_pb, _ = red_begin("b0", ("f1w13", "f1w2"), [g["f1w13"], g["f1w2"]], 0)
    sums1 = dict(zip(W_NAMES, red_end("l1", st_c1, 1, [None] * len(W_NAMES), g["f1w13"])))
    st_cb, tok = red_mid("b0", st_pb, tuple(sums1.values()))
    sums0 = red_end("a0", st_ca, 0, [sums1[n] for n in rest_names], tok)
    share_a, tok_a = share_start("a", rest_names, sums0)

    offs = [0]
    for s_ in sizes:
        offs.append(offs[-1] + s_)
    parts = [tot[offs[i]:offs[i + 1]] for i in range(len(sizes))]
    g_pool_w = parts[0].reshape(pool_w.shape)
    g_pool_scale = parts[1].reshape(pool_scale.shape)
    g_gq = parts[2].reshape(q_norm_g.shape)
    g_gkv = parts[3].reshape(kv_norm_g.shape)
    shard_cols = lambda a: lax.dynamic_slice_in_dim(a.reshape(L, 4, D_MODEL), chip * (D_MODEL // N_CHIPS),
                                                    D_MODEL // N_CHIPS, axis=2)
    g_lng, g_lnb = shard_cols(parts[4]), shard_cols(parts[5])

    out_names = ("lng", "lnb", "f1w13", "f1w2", "win", "pool_w", "pool_scale", "gq", "wuq", "gkv", "wukv", "wout", "mwq",
                 "mwkv", "mwo", "f2w13", "f2w2")
    big = dict(lng=g_lng, lnb=g_lnb, pool_w=g_pool_w, pool_scale=g_pool_scale, gq=g_gq, gkv=g_gkv)
    late = ("f1w13", "f1w2")
    held = ("f2w13", "f2w2")
    ws = [ln_g, ln_b, ffn1_w13, ffn1_w2, w_in, pool_w, pool_scale, q_norm_g, w_uq, kv_norm_g, w_ukv, w_out, mem_wq,
          mem_wkv, mem_wo, ffn2_w13, ffn2_w2]
    ms = [m_ln_g, m_ln_b, m_ffn1_w13, m_ffn1_w2, m_w_in, m_pool_w, m_pool_scale, m_q_norm_g, m_w_uq, m_kv_norm_g, m_w_ukv,
          m_w_out, m_mem_wq, m_mem_wkv, m_mem_wo, m_ffn2_w13, m_ffn2_w2]
    vs = [v_ln_g, v_ln_b, v_ffn1_w13, v_ffn1_w2, v_w_in, v_pool_w, v_pool_scale, v_q_norm_g, v_w_uq, v_kv_norm_g, v_w_ukv,
          v_w_out, v_mem_wq, v_mem_wkv, v_mem_wo, v_ffn2_w13, v_ffn2_w2]
    res = {}

    def update(n, deps=()):
        a = out_names.index(n)
        res[a] = adamw(f"adamw_{a}", ws[a], big[n].reshape(ws[a].shape), ms[a], vs[a], deps)
        return res[a][0]

    small_done = tuple(update(n, (tok_a,)) for n in ("lng", "lnb", "pool_w", "pool_scale", "gq", "gkv"))
    big.update(zip(rest_names, pair_share_wait("pair_share_wait_a", share_a, small_done)))
    first_done = tuple(update(n) for n in rest_names if n not in held)
    sums_b = red_end("b0", st_cb, 0, [sums1[n] for n in late], first_done)
    share_b, tok_b = share_start("b", late, sums_b)
    held_done = tuple(update(n, (tok_b,)) for n in held)
    big.update(zip(late, pair_share_wait("pair_share_wait_b", share_b, held_done)))
    for n in late:
        update(n)
    order = range(len(out_names))
    grads = [big[n].reshape(w_.shape) for n, w_ in zip(out_names, ws)]
    return (loss, grad_x, *grads, *[res[a][0] for a in order], *[res[a][1] for a in order], *[res[a][2] for a in order])
```

```python
import functools
import math

import jax
import jax.numpy as jnp
from jax import lax
from jax.experimental import pallas as pl
from jax.experimental.pallas import tpu as pltpu

f32 = jnp.float32
bf16 = jnp.bfloat16
SDS = jax.ShapeDtypeStruct
MESH = pl.DeviceIdType.MESH

D_MODEL = 1024
DEPTH = 2
N_MEM = 256
MEM_HEADS = 4
MEM_HEAD_DIM = D_MODEL // MEM_HEADS
POOL_WINDOWS = (2, 4, 8, 16)
POOL_WIDTH = 256
POOL_GROUP = 64
QK_NOPE = 128
QK_ROPE = 64
V_HEAD = 128
MLA_HEADS = 6
Q_LORA = 256
KV_LORA = 128
ROPE_BASE = 10000.0
D_FF = 2816
D_IN = POOL_WIDTH + Q_LORA + KV_LORA + QK_ROPE
ALPHA = (2 * DEPTH) ** 0.25
LN_EPS = 1e-5
RMS_EPS = 1e-6
NEG_INF = -1e30
MLA_SCALE = (QK_NOPE + QK_ROPE) ** -0.5
MLA_SCALE_LOG2 = MLA_SCALE * math.log2(math.e)
MEM_SCALE = MEM_HEAD_DIM ** -0.5
ADAM_LR = 0.001
ADAM_B1 = 0.9
ADAM_B2 = 0.999
ADAM_EPS = 1e-08
ADAM_WD = 0.01
ADAM_STEP = 10

N_CHIPS = 4
V7X_VMEM_LIMIT = 56 * 2**20
HALO = 16

_NT = (((1,), (1,)), ((), ()))
_TN = (((0,), (0,)), ((), ()))


def _dot(a, b):
    return jnp.dot(a, b, preferred_element_type=f32)


def _dot_nt(a, b):
    return lax.dot_general(a, b, _NT, preferred_element_type=f32)


def _dot_tn(a, b):
    return lax.dot_general(a, b, _TN, preferred_element_type=f32)


def _cp(*sem):
    return pltpu.CompilerParams(dimension_semantics=sem if sem else None, vmem_limit_bytes=V7X_VMEM_LIMIT)


_DEP_SPEC = pl.BlockSpec(memory_space=pl.ANY)


def _with_deps(body, n_in, deps):
    nd = len(deps)
    if not nd:
        return body

    def wrapped(*refs):
        return body(*refs[:n_in], *refs[n_in + nd:])

    return wrapped


def _tile(n, t):
    t = min(n, t)
    assert n % t == 0, (n, t)
    return t


def _row_tile(rows, cols, itemsize=4, target=2 * 2**20):
    best = None
    for t in range(16, rows + 1, 16):
        if rows % t == 0 and t * cols * itemsize <= target:
            best = t
    return best if best is not None else rows


def ffn_up(name, xb, w13, l, deps=()):
    S = xb.shape[0]
    ns = w13.shape[3]
    tm = _tile(S, 512)

    def body(x_ref, wg_ref, wu_ref, g_ref, u_ref, a_ref):
        x = x_ref[...]
        g = _dot(x, wg_ref[0, 0])
        u = _dot(x, wu_ref[0, 0])
        a = g * jax.nn.sigmoid(g) * u
        g_ref[...] = g.astype(bf16)
        u_ref[...] = u.astype(bf16)
        a_ref[...] = a.astype(bf16)

    out = SDS((S, 2 * ns), bf16)
    return pl.pallas_call(
        _with_deps(body, 3, deps), name=name, grid=(2, S // tm),
        in_specs=[pl.BlockSpec((tm, D_MODEL), lambda j, i: (i, 0)),
                  pl.BlockSpec((1, 1, D_MODEL, ns), lambda j, i: (l, j, 0, 0)),
                  pl.BlockSpec((1, 1, D_MODEL, ns), lambda j, i: (l, j + 2, 0, 0))] + [_DEP_SPEC] * len(deps),
        out_specs=[pl.BlockSpec((tm, ns), lambda j, i: (i, j))] * 3,
        out_shape=[out, out, out],
        compiler_params=_cp("parallel", "parallel"),
    )(xb, w13, w13, *deps)


def proj_res_ln(name, parts, ws, wl, x, g, b, rscale, deps=()):
    S = x.shape[0]
    tm = _tile(S, 512)
    n = len(parts)

    def body(*refs):
        p_refs, w_refs = refs[:n], refs[n:2 * n]
        x_ref, g_ref, b_ref, z_ref, y_ref, yb_ref = refs[2 * n:]
        acc = _dot(p_refs[0][...], w_refs[0][0])
        for k in range(1, n):
            acc = acc + _dot(p_refs[k][...], w_refs[k][0])
        if rscale != 1.0:
            acc = rscale * acc
        z = ALPHA * x_ref[...] + acc
        mu = jnp.mean(z, axis=-1, keepdims=True)
        zc = z - mu
        var = jnp.mean(zc * zc, axis=-1, keepdims=True)
        y = zc * lax.rsqrt(var + LN_EPS) * g_ref[...] + b_ref[...]
        z_ref[...] = z
        y_ref[...] = y
        yb_ref[...] = y.astype(bf16)

    row = lambda i: (i, 0)
    in_specs = [pl.BlockSpec((tm, p.shape[1]), row) for p in parts]
    in_specs += [pl.BlockSpec((1,) + w.shape[1:], functools.partial(lambda li, i: (li, 0, 0), li)) for w, li in zip(ws, wl)]
    in_specs += [pl.BlockSpec((tm, D_MODEL), row), pl.BlockSpec((1, D_MODEL), lambda i: (0, 0)),
                 pl.BlockSpec((1, D_MODEL), lambda i: (0, 0))] + [_DEP_SPEC] * len(deps)
    return pl.pallas_call(
        _with_deps(body, 2 * n + 3, deps), name=name, grid=(S // tm,), in_specs=in_specs,
        out_specs=[pl.BlockSpec((tm, D_MODEL), row)] * 3,
        out_shape=[SDS((S, D_MODEL), f32), SDS((S, D_MODEL), f32), SDS((S, D_MODEL), bf16)],
        compiler_params=_cp("parallel"),
    )(*parts, *ws, x, g, b, *deps)


def _ln_bwd_store(dyv, z_ref, g_ref, rscale, first, dzb_ref, dres_ref, dg_ref, db_ref):
    z = z_ref[...]
    mu = jnp.mean(z, axis=-1, keepdims=True)
    zc = z - mu
    rstd = lax.rsqrt(jnp.mean(zc * zc, axis=-1, keepdims=True) + LN_EPS)
    xhat = zc * rstd
    dxh = dyv * g_ref[...]
    m1 = jnp.mean(dxh, axis=-1, keepdims=True)
    m2 = jnp.mean(dxh * xhat, axis=-1, keepdims=True)
    dz = rstd * (dxh - m1 - xhat * m2)
    dzb_ref[...] = (rscale * dz).astype(bf16)
    dres_ref[...] = ALPHA * dz

    @pl.when(first)
    def _():
        dg_ref[...] = jnp.zeros_like(dg_ref)
        db_ref[...] = jnp.zeros_like(db_ref)

    dg_ref[...] += jnp.sum(dyv * xhat, axis=0, keepdims=True)
    db_ref[...] += jnp.sum(dyv, axis=0, keepdims=True)


def _ln_bwd_specs(S, tm, index):
    vec = pl.BlockSpec((1, D_MODEL), lambda *a: (0, 0))
    blk = pl.BlockSpec((tm, D_MODEL), index)
    in_specs = [blk, vec]
    out_specs = [blk, blk, vec, vec]
    out_shape = [SDS((S, D_MODEL), bf16), SDS((S, D_MODEL), f32), SDS((1, D_MODEL), f32), SDS((1, D_MODEL), f32)]
    return in_specs, out_specs, out_shape


def ffn_bwd_da(name, drb, w2, l, gate, up, deps=()):
    S = drb.shape[0]
    tm = _tile(S, 512)
    nh = D_FF // 2

    def body(dr_ref, w_ref, g_ref, u_ref, dh_ref):
        dr = dr_ref[...]
        for j in range(2):
            cols = slice(j * nh, (j + 1) * nh)
            da = _dot_nt(dr, w_ref[0, cols, :])
            g = g_ref[:, cols].astype(f32)
            u = u_ref[:, cols].astype(f32)
            sg = jax.nn.sigmoid(g)
            dh_ref[:, cols] = (da * u * (sg * (1.0 + g * (1.0 - sg)))).astype(bf16)
            dh_ref[:, D_FF + j * nh:D_FF + (j + 1) * nh] = (da * (g * sg)).astype(bf16)

    row = lambda i: (i, 0)
    return pl.pallas_call(
        _with_deps(body, 4, deps), name=name, grid=(S // tm,),
        in_specs=[pl.BlockSpec((tm, D_MODEL), row), pl.BlockSpec((1, D_FF, D_MODEL), lambda i: (l, 0, 0)),
                  pl.BlockSpec((tm, D_FF), row), pl.BlockSpec((tm, D_FF), row)] + [_DEP_SPEC] * len(deps),
        out_specs=pl.BlockSpec((tm, 2 * D_FF), row),
        out_shape=SDS((S, 2 * D_FF), bf16),
        compiler_params=_cp("parallel"),
    )(drb, w2, gate, up, *deps)


def ffn_dx(name, dh, w13, l, res, ln=None):
    S = dh.shape[0]
    ns = w13.shape[3]
    tm = _tile(S, 1024)
    last = N_CHIPS - 1
    row = lambda i, j: (i, 0)
    in_specs = [pl.BlockSpec((tm, ns), lambda i, j: (i, j)),
                pl.BlockSpec((1, 1, D_MODEL, ns), lambda i, j: (l, j, 0, 0)),
                pl.BlockSpec((tm, D_MODEL), row)]
    if ln is None:
        def body(dh_ref, w_ref, r_ref, o_ref):
            @pl.when(pl.program_id(1) == 0)
            def _():
                o_ref[...] = r_ref[...]

            o_ref[...] += _dot_nt(dh_ref[...], w_ref[0, 0])

        return pl.pallas_call(
            body, name=name, grid=(S // tm, N_CHIPS), in_specs=in_specs,
            out_specs=pl.BlockSpec((tm, D_MODEL), row), out_shape=SDS((S, D_MODEL), f32),
            compiler_params=_cp("parallel", "arbitrary"),
        )(dh, w13, res)

    z, g, rscale = ln

    def body_ln(dh_ref, w_ref, r_ref, z_ref, g_ref, dzb_ref, dres_ref, dg_ref, db_ref, acc_sc):
        i, j = pl.program_id(0), pl.program_id(1)

        @pl.when(j == 0)
        def _():
            acc_sc[...] = r_ref[...]

        acc_sc[...] += _dot_nt(dh_ref[...], w_ref[0, 0])

        @pl.when(j == last)
        def _():
            _ln_bwd_store(acc_sc[...], z_ref, g_ref, rscale, i == 0, dzb_ref, dres_ref, dg_ref, db_ref)

    ln_in, ln_out, ln_shape = _ln_bwd_specs(S, tm, row)
    return pl.pallas_call(
        body_ln, name=name, grid=(S // tm, N_CHIPS), in_specs=in_specs + ln_in, out_specs=ln_out, out_shape=ln_shape,
        scratch_shapes=[pltpu.VMEM((tm, D_MODEL), f32)], compiler_params=_cp("arbitrary", "arbitrary"),
    )(dh, w13, res, z, g)


def mm_nt_res(name, dys, ws, wl, res, out_dtype, ln=None):
    S = dys[0].shape[0]
    K = ws[0].shape[1]
    tm = _tile(S, 512)
    n = len(dys)
    n_in = 2 * n + (res is not None)

    def product(refs):
        acc = _dot_nt(refs[0][...], refs[n][0])
        for k in range(1, n):
            acc = acc + _dot_nt(refs[k][...], refs[n + k][0])
        if res is not None:
            acc = acc + refs[2 * n][...]
        return acc

    def body(*refs):
        refs[-1][...] = product(refs).astype(out_dtype)

    def body_ln(*refs):
        z_ref, g_ref, dzb_ref, dres_ref, dg_ref, db_ref = refs[n_in:]
        _ln_bwd_store(product(refs), z_ref, g_ref, ln[2], pl.program_id(0) == 0, dzb_ref, dres_ref, dg_ref, db_ref)

    row = lambda i: (i, 0)
    in_specs = [pl.BlockSpec((tm, d.shape[1]), row) for d in dys]
    in_specs += [pl.BlockSpec((1,) + w.shape[1:], functools.partial(lambda li, i: (li, 0, 0), li)) for w, li in zip(ws, wl)]
    args = list(dys) + list(ws)
    if res is not None:
        in_specs.append(pl.BlockSpec((tm, K), row))
        args.append(res)
    if ln is None:
        return pl.pallas_call(
            body, name=name, grid=(S // tm,), in_specs=in_specs,
            out_specs=pl.BlockSpec((tm, K), row), out_shape=SDS((S, K), out_dtype),
            compiler_params=_cp("parallel"),
        )(*args)
    ln_in, ln_out, ln_shape = _ln_bwd_specs(S, tm, row)
    return pl.pallas_call(
        body_ln, name=name, grid=(S // tm,), in_specs=in_specs + ln_in, out_specs=ln_out, out_shape=ln_shape,
        compiler_params=_cp("arbitrary"),
    )(*args, ln[0], ln[1])


def mm_tn(name, x, dy, col_shards=False, deps=()):
    S, K = x.shape
    N = dy.shape[1]
    ts = 512
    while ts * 2 <= min(S, 2048) and S % (ts * 2) == 0 and ts * 2 * K * 2 <= 6 * 2**20:
        ts *= 2
    ts = _tile(S, ts)
    if col_shards:
        tn = N // N_CHIPS
    else:
        tn = N
        while K * tn * 4 > 6 * 2**20 and tn % 256 == 0:
            tn //= 2
    nn = N // tn
    lead = ((0,) if col_shards else ()) + (slice(None), slice(None))

    def body(x_ref, dy_ref, o_ref):
        acc = _dot_tn(x_ref[...].astype(bf16), dy_ref[...].astype(bf16))

        @pl.when(pl.program_id(1) == 0)
        def _():
            o_ref[lead] = acc

        @pl.when(pl.program_id(1) != 0)
        def _():
            o_ref[lead] += acc

    if col_shards:
        out_spec = pl.BlockSpec((1, K, tn), lambda n, s: (n, 0, 0))
        out_shape = SDS((N_CHIPS, K, tn), f32)
    else:
        out_spec = pl.BlockSpec((K, tn), lambda n, s: (0, n))
        out_shape = SDS((K, N), f32)
    return pl.pallas_call(
        _with_deps(body, 2, deps), name=name, grid=(nn, S // ts),
        in_specs=[pl.BlockSpec((ts, K), lambda n, s: (s, 0)), pl.BlockSpec((ts, tn), lambda n, s: (s, n))]
        + [_DEP_SPEC] * len(deps),
        out_specs=out_spec, out_shape=out_shape, compiler_params=_cp("parallel", "arbitrary"),
    )(x, dy, *deps)


def mm_nn_shard(name, x, w, l):
    S, K = x.shape
    ns = w.shape[3]

    def body(x_ref, w_ref, o_ref):
        o_ref[...] = _dot(x_ref[...], w_ref[0, 0]).astype(bf16)

    return pl.pallas_call(
        body, name=name, grid=(N_CHIPS,),
        in_specs=[pl.BlockSpec((S, K), lambda j: (0, 0)), pl.BlockSpec((1, 1, K, ns), lambda j: (l, j, 0, 0))],
        out_specs=pl.BlockSpec((S, ns), lambda j: (0, j)), out_shape=SDS((S, N_CHIPS * ns), bf16),
        compiler_params=_cp("parallel"),
    )(x, w)


def loss_grad(name, y, t, ln):
    S = y.shape[0]
    tm = _tile(S, 512)
    z, g, rscale = ln

    def body(y_ref, t_ref, z_ref, g_ref, dzb_ref, dres_ref, dg_ref, db_ref, loss_ref):
        first = pl.program_id(0) == 0
        e = y_ref[...] - t_ref[...]
        _ln_bwd_store(e * (1.0 / D_MODEL), z_ref, g_ref, rscale, first, dzb_ref, dres_ref, dg_ref, db_ref)

        @pl.when(first)
        def _():
            loss_ref[...] = jnp.zeros_like(loss_ref)

        loss_ref[...] += jnp.full(loss_ref.shape, (0.5 / D_MODEL) * jnp.sum(e * e), f32)

    row = lambda i: (i, 0)
    ln_in, ln_out, ln_shape = _ln_bwd_specs(S, tm, row)
    return pl.pallas_call(
        body, name=name, grid=(S // tm,),
        in_specs=[pl.BlockSpec((tm, D_MODEL), row)] * 2 + ln_in,
        out_specs=ln_out + [pl.BlockSpec((8, 128), lambda i: (0, 0))],
        out_shape=ln_shape + [SDS((8, 128), f32)],
        compiler_params=_cp("arbitrary"),
    )(y, t, z, g)


def _half_sum(t):
    return t + pltpu.roll(t, 64, axis=1)


def mix_pre(name, xb, w_in, wq, wkv, l, gq, gkv, cs):
    S = xb.shape[0]
    tm = _tile(S, 512)
    H = MLA_HEADS
    W_EXT = w_in.shape[2]

    def body(x_ref, win_ref, wq_ref, wkv_ref, gq_ref, gkv_ref, cs_ref,
             u_ref, cq_ref, ckv_ref, cqn_ref, ckvn_ref, q_ref, k_ref, v_ref):
        h = _dot(x_ref[...], win_ref[0])
        u_ref[...] = h[:, :256]
        cq = h[:, 256:512]
        ckv = h[:, 512:640]
        cq_ref[...] = cq
        ckv_ref[...] = ckv
        cqn = (cq * lax.rsqrt(jnp.mean(cq * cq, axis=-1, keepdims=True) + RMS_EPS) * gq_ref[...]).astype(bf16)
        ckvn = (ckv * lax.rsqrt(jnp.mean(ckv * ckv, axis=-1, keepdims=True) + RMS_EPS) * gkv_ref[...]).astype(bf16)
        cqn_ref[...] = cqn
        ckvn_ref[...] = ckvn
        csv = cs_ref[...]
        lane = lax.broadcasted_iota(jnp.int32, (tm, 128), 1)
        kr = jnp.where(lane < 64, _half_sum(h[:, 640:768] * csv), 0.0).astype(bf16)
        kv = _dot(ckvn, wkv_ref[0])
        for hd in range(H):
            qe = _dot(cqn, wq_ref[0, hd])
            q_ref[hd, :, :128] = (qe[:, :128] * MLA_SCALE_LOG2).astype(bf16)
            q_ref[hd, :, 128:] = (_half_sum(qe[:, 128:] * csv) * MLA_SCALE_LOG2).astype(bf16)
            k_ref[hd, :, :128] = kv[:, 256 * hd:256 * hd + 128].astype(bf16)
            k_ref[hd, :, 128:] = kr
            v_ref[hd] = kv[:, 256 * hd + 128:256 * hd + 256].astype(bf16)

    row = lambda i: (i, 0)
    hrow = lambda i: (0, i, 0)
    return pl.pallas_call(
        body, name=name, grid=(S // tm,),
        in_specs=[pl.BlockSpec((tm, D_MODEL), row),
                  pl.BlockSpec((1, D_MODEL, W_EXT), lambda i: (l, 0, 0)),
                  pl.BlockSpec((1, H, Q_LORA, 256), lambda i: (l, 0, 0, 0)),
                  pl.BlockSpec((1, KV_LORA, H * 256), lambda i: (l, 0, 0)),
                  pl.BlockSpec((1, Q_LORA), lambda i: (0, 0)), pl.BlockSpec((1, KV_LORA), lambda i: (0, 0)),
                  pl.BlockSpec((tm, 128), row)],
        out_specs=[pl.BlockSpec((tm, 256), row), pl.BlockSpec((tm, Q_LORA), row), pl.BlockSpec((tm, KV_LORA), row),
                   pl.BlockSpec((tm, Q_LORA), row), pl.BlockSpec((tm, KV_LORA), row),
                   pl.BlockSpec((H, tm, 256), hrow), pl.BlockSpec((H, tm, 256), hrow), pl.BlockSpec((H, tm, 128), hrow)],
        out_shape=[SDS((S, 256), f32), SDS((S, Q_LORA), f32), SDS((S, KV_LORA), f32),
                   SDS((S, Q_LORA), bf16), SDS((S, KV_LORA), bf16),
                   SDS((H, S, 256), bf16), SDS((H, S, 256), bf16), SDS((H, S, 128), bf16)],
        compiler_params=_cp("parallel"),
    )(xb, w_in, wq, wkv, gq, gkv, cs)


def _group_select(col, a2, a4, a8, a16):
    return jnp.where(col < 64, a2, jnp.where(col < 128, a4, jnp.where(col < 192, a8, a16)))


def pool_fwd(name, u, wbd, scale):
    S = u.shape[0]
    tm = _tile(S, 512)
    hb = tm // HALO

    def body(u_ref, halo_ref, w_ref, s_ref, d_ref, y_ref):
        i = pl.program_id(0)
        cur = u_ref[...]
        halo = jnp.where(i > 0, halo_ref[...], 0.0)
        ext = jnp.concatenate([halo, cur], axis=0)
        s2 = ext + pltpu.roll(ext, 1, axis=0)
        s4 = s2 + pltpu.roll(s2, 2, axis=0)
        s8 = s4 + pltpu.roll(s4, 4, axis=0)
        s16 = s8 + pltpu.roll(s8, 8, axis=0)
        t1 = (i * tm + 1 + lax.broadcasted_iota(jnp.int32, (tm, 1), 0)).astype(f32)
        col = lax.broadcasted_iota(jnp.int32, (tm, 256), 1)
        m = _group_select(col, s2[HALO:] / jnp.minimum(t1, 2.0), s4[HALO:] / jnp.minimum(t1, 4.0),
                          s8[HALO:] / jnp.minimum(t1, 8.0), s16[HALO:] / jnp.minimum(t1, 16.0))
        d = (m - cur).astype(bf16)
        d_ref[...] = d
        y_ref[...] = (_dot(d, w_ref[...]) * s_ref[...]).astype(bf16)

    row = lambda i: (i, 0)
    return pl.pallas_call(
        body, name=name, grid=(S // tm,),
        in_specs=[pl.BlockSpec((tm, 256), row), pl.BlockSpec((HALO, 256), lambda i: (jnp.maximum(i * hb - 1, 0), 0)),
                  pl.BlockSpec((256, 256), lambda i: (0, 0)), pl.BlockSpec((1, 256), lambda i: (0, 0))],
        out_specs=[pl.BlockSpec((tm, 256), row)] * 2,
        out_shape=[SDS((S, 256), bf16), SDS((S, 256), bf16)],
        compiler_params=_cp("parallel"),
    )(u, u, wbd, scale)


def pool_bwd(name, dyp, d, wbd, scale):
    S = dyp.shape[0]
    tm = _tile(S, 512)
    hb = tm // HALO
    n_ext = tm + HALO

    def fwd_sum(e, steps):
        k = 1
        for _ in range(steps):
            e = e + pltpu.roll(e, n_ext - k, axis=0)
            k *= 2
        return e

    def body(dy_ref, halo_ref, d_ref, w_ref, s_ref, du_ref, dyw_ref, ds_ref):
        i = pl.program_id(0)
        sc = s_ref[...]
        w = w_ref[...]
        cur = dy_ref[...].astype(f32)
        halo = jnp.where(i < pl.num_programs(0) - 1, halo_ref[...].astype(f32), 0.0)
        dyw = jnp.concatenate([cur, halo], axis=0) * sc
        dyw_ref[...] = dyw[:tm].astype(bf16)
        dd = _dot_nt(dyw.astype(bf16), w)
        t1 = (i * tm + 1 + lax.broadcasted_iota(jnp.int32, (n_ext, 1), 0)).astype(f32)
        f2 = fwd_sum(dd / jnp.minimum(t1, 2.0), 1)
        f4 = fwd_sum(dd / jnp.minimum(t1, 4.0), 2)
        f8 = fwd_sum(dd / jnp.minimum(t1, 8.0), 3)
        f16 = fwd_sum(dd / jnp.minimum(t1, 16.0), 4)
        col = lax.broadcasted_iota(jnp.int32, (tm, 256), 1)
        du_ref[...] = (_group_select(col, f2[:tm], f4[:tm], f8[:tm], f16[:tm]) - dd[:tm]).astype(bf16)

        @pl.when(i == 0)
        def _():
            ds_ref[...] = jnp.zeros_like(ds_ref)

        ds_ref[...] += jnp.sum(cur * _dot(d_ref[...], w), axis=0, keepdims=True)

    row = lambda i: (i, 0)
    nhb = S // HALO
    return pl.pallas_call(
        body, name=name, grid=(S // tm,),
        in_specs=[pl.BlockSpec((tm, 256), row), pl.BlockSpec((HALO, 256), lambda i: (jnp.minimum((i + 1) * hb, nhb - 1), 0)),
                  pl.BlockSpec((tm, 256), row), pl.BlockSpec((256, 256), lambda i: (0, 0)),
                  pl.BlockSpec((1, 256), lambda i: (0, 0))],
        out_specs=[pl.BlockSpec((tm, 256), row), pl.BlockSpec((tm, 256), row), pl.BlockSpec((1, 256), lambda i: (0, 0))],
        out_shape=[SDS((S, 256), bf16), SDS((S, 256), bf16), SDS((1, 256), f32)],
        compiler_params=_cp("arbitrary"),
    )(dyp, dyp, d, wbd, scale)


def _diag_mask(r0, rn, kn):
    rc = (r0 + lax.broadcasted_iota(jnp.int32, (rn, 1), 0)) // 64
    cc = lax.broadcasted_iota(jnp.int32, (1, kn), 1) // 64
    return rc >= cc


def _diag_parts(tq):
    h = tq // 2
    return [(0, h, h), (h, h, tq)] if h % 128 == 0 else [(0, tq, tq)]


def mla_attn_fwd(name, q, k, v):
    H, S, _ = q.shape
    tq = _tile(S, 1024)
    nq = S // tq
    pairs = [(i, j) for i in range(nq) for j in range(i + 1)]
    it = jnp.asarray([p_[0] for p_ in pairs], jnp.int32)
    jt = jnp.asarray([p_[1] for p_ in pairs], jnp.int32)

    def body(it_ref, jt_ref, q_ref, k_ref, v_ref, o_ref, lse_ref, m_sc, l_sc, acc_sc):
        t = pl.program_id(1)
        i, j = it_ref[t], jt_ref[t]

        @pl.when(j == 0)
        def _():
            m_sc[...] = jnp.full_like(m_sc, NEG_INF)
            l_sc[...] = jnp.zeros_like(l_sc)
            acc_sc[...] = jnp.zeros_like(acc_sc)

        def part(r0, rn, kn, masked):
            rows, keys = slice(r0, r0 + rn), slice(0, kn)
            s = _dot_nt(q_ref[0, rows, :], k_ref[0, keys, :])
            if masked:
                s = jnp.where(_diag_mask(r0, rn, kn), s, NEG_INF)
            m_prev = m_sc[rows, :]
            m_new = jnp.maximum(m_prev, jnp.max(s, axis=-1, keepdims=True))
            p = jnp.exp2(s - jnp.tile(m_new, (1, kn // 128)))
            a = jnp.exp2(m_prev - m_new)
            l_sc[rows, :] = a * l_sc[rows, :] + jnp.sum(p, axis=-1, keepdims=True)
            acc_sc[rows, :] = a * acc_sc[rows, :] + _dot(p.astype(bf16), v_ref[0, keys, :])
            m_sc[rows, :] = m_new

        @pl.when(j < i)
        def _():
            part(0, tq, tq, False)

        @pl.when(j == i)
        def _():
            for r0, rn, kn in _diag_parts(tq):
                part(r0, rn, kn, True)
            o_ref[...] = (acc_sc[...] / l_sc[...]).astype(bf16)
            lse_ref[0] = m_sc[...] + jnp.log2(l_sc[...])

    return pl.pallas_call(
        body, name=name,
        grid_spec=pltpu.PrefetchScalarGridSpec(
            num_scalar_prefetch=2, grid=(H, len(pairs)),
            in_specs=[pl.BlockSpec((1, tq, 256), lambda h, t, it_, jt_: (h, it_[t], 0)),
                      pl.BlockSpec((1, tq, 256), lambda h, t, it_, jt_: (h, jt_[t], 0)),
                      pl.BlockSpec((1, tq, 128), lambda h, t, it_, jt_: (h, jt_[t], 0))],
            out_specs=[pl.BlockSpec((tq, 128), lambda h, t, it_, jt_: (it_[t], h)),
                       pl.BlockSpec((1, tq, 128), lambda h, t, it_, jt_: (h, it_[t], 0))],
            scratch_shapes=[pltpu.VMEM((tq, 128), f32), pltpu.VMEM((tq, 128), f32), pltpu.VMEM((tq, 128), f32)]),
        out_shape=[SDS((S, H * 128), bf16), SDS((H, S, 128), f32)],
        compiler_params=_cp("parallel", "arbitrary"),
    )(it, jt, q, k, v)


def mla_attn_bwd(name, q, k, v, o, do, lse, do_col=0):
    H, S, _ = q.shape
    tq = _tile(S, 1024)
    nq = S // tq
    pairs = [(i, j) for j in range(nq) for i in range(j, nq)]
    it = jnp.asarray([p_[0] for p_ in pairs], jnp.int32)
    jt = jnp.asarray([p_[1] for p_ in pairs], jnp.int32)
    n_pairs = len(pairs)

    def body(it_ref, jt_ref, q_ref, k_ref, v_ref, o_ref, do_ref, lse_ref, dq_ref, dk_ref, dv_ref, dq_sc, dk_sc, dv_sc):
        t = pl.program_id(1)
        i, j = it_ref[t], jt_ref[t]

        @pl.when(t == 0)
        def _():
            dq_sc[...] = jnp.zeros_like(dq_sc)

        @pl.when(i == j)
        def _():
            dk_sc[...] = jnp.zeros_like(dk_sc)
            dv_sc[...] = jnp.zeros_like(dv_sc)

        def part(r0, rn, kn, masked):
            rows, keys = slice(r0, r0 + rn), slice(0, kn)
            qv, kv_, dov = q_ref[0, rows, :], k_ref[0, keys, :], do_ref[rows, :]
            s = _dot_nt(qv, kv_)
            if masked:
                s = jnp.where(_diag_mask(r0, rn, kn), s, NEG_INF)
            p = jnp.exp2(s - jnp.tile(lse_ref[0, rows, :], (1, kn // 128)))
            dv_sc[keys, :] += _dot_tn(p.astype(bf16), dov)
            dp = _dot_nt(dov, v_ref[0, keys, :])
            delta = jnp.sum(dov.astype(f32) * o_ref[rows, :].astype(f32), axis=-1, keepdims=True)
            ds = (p * (dp - delta)).astype(bf16)
            dk_sc[keys, :] += _dot_tn(ds, qv)
            dq_rows = pl.ds(pl.multiple_of(i * tq + r0, 128), rn)
            dq_sc[dq_rows, :] += _dot(ds, kv_)

        @pl.when(i > j)
        def _():
            part(0, tq, tq, False)

        @pl.when(i == j)
        def _():
            for r0, rn, kn in _diag_parts(tq):
                part(r0, rn, kn, True)

        @pl.when(i == nq - 1)
        def _():
            dk_ref[0] = (dk_sc[...] * math.log(2.0)).astype(bf16)
            dv_ref[0] = dv_sc[...].astype(bf16)

        @pl.when(t == n_pairs - 1)
        def _():
            dq_ref[0] = (dq_sc[...] * MLA_SCALE).astype(bf16)

    qi = lambda h, t, it_, jt_: (h, it_[t], 0)
    kj = lambda h, t, it_, jt_: (h, jt_[t], 0)
    oi = lambda h, t, it_, jt_: (it_[t], h)
    doi = lambda h, t, it_, jt_: (it_[t], h + do_col)
    return pl.pallas_call(
        body, name=name,
        grid_spec=pltpu.PrefetchScalarGridSpec(
            num_scalar_prefetch=2, grid=(H, n_pairs),
            in_specs=[pl.BlockSpec((1, tq, 256), qi), pl.BlockSpec((1, tq, 256), kj), pl.BlockSpec((1, tq, 128), kj),
                      pl.BlockSpec((tq, 128), oi), pl.BlockSpec((tq, 128), doi), pl.BlockSpec((1, tq, 128), qi)],
            out_specs=[pl.BlockSpec((1, S, 256), lambda h, t, it_, jt_: (h, 0, 0)), pl.BlockSpec((1, tq, 256), kj),
                       pl.BlockSpec((1, tq, 128), kj)],
            scratch_shapes=[pltpu.VMEM((S, 256), f32), pltpu.VMEM((tq, 256), f32), pltpu.VMEM((tq, 128), f32)]),
        out_shape=[SDS((H, S, 256), bf16), SDS((H, S, 256), bf16), SDS((H, S, 128), bf16)],
        compiler_params=_cp("parallel", "arbitrary"),
    )(it, jt, q, k, v, o, do, lse)


def mix_post_bwd(name, dq, dk, dv, wq, wkv, l, cq, ckv, gq, gkv, cs):
    H, S, _ = dq.shape
    tm = _tile(S, 512)

    def rms_bwd(dyn, c, g):
        r = lax.rsqrt(jnp.mean(c * c, axis=-1, keepdims=True) + RMS_EPS)
        ch = c * r
        dyg = dyn * g
        dc = r * (dyg - ch * jnp.mean(dyg * ch, axis=-1, keepdims=True))
        return dc, jnp.sum(dyn * ch, axis=0, keepdims=True)

    def body(dq_ref, dk_ref, dv_ref, wq_ref, wkv_ref, cq_ref, ckv_ref, gq_ref, gkv_ref, cs_ref,
             dqe_ref, dkv_ref, dh_ref, dgq_ref, dgkv_ref):
        csv = cs_ref[...]
        lane = lax.broadcasted_iota(jnp.int32, (tm, 128), 1)
        dcqn = jnp.zeros((tm, Q_LORA), f32)
        dkr = jnp.zeros((tm, 128), f32)
        for hd in range(H):
            dqh = dq_ref[hd].astype(f32)
            dqe = jnp.concatenate([dqh[:, :128], _half_sum(dqh[:, 128:]) * csv], axis=1).astype(bf16)
            dqe_ref[:, 256 * hd:256 * hd + 256] = dqe
            dcqn = dcqn + _dot_nt(dqe, wq_ref[0, hd])
            dkh = dk_ref[hd].astype(f32)
            dkv_ref[:, 256 * hd:256 * hd + 128] = dkh[:, :128].astype(bf16)
            dkv_ref[:, 256 * hd + 128:256 * hd + 256] = dv_ref[hd].astype(bf16)
            dkr = dkr + dkh[:, 128:]
        dckvn = _dot_nt(dkv_ref[...], wkv_ref[0])
        dblk = _half_sum(jnp.where(lane < 64, dkr, 0.0)) * csv
        dcq, dgq = rms_bwd(dcqn, cq_ref[...], gq_ref[...])
        dckv, dgkv = rms_bwd(dckvn, ckv_ref[...], gkv_ref[...])
        dh_ref[:, :256] = dcq.astype(bf16)
        dh_ref[:, 256:384] = dckv.astype(bf16)
        dh_ref[:, 384:] = dblk.astype(bf16)

        @pl.when(pl.program_id(0) == 0)
        def _():
            dgq_ref[...] = jnp.zeros_like(dgq_ref)
            dgkv_ref[...] = jnp.zeros_like(dgkv_ref)

        dgq_ref[...] += dgq
        dgkv_ref[...] += dgkv

    row = lambda i: (i, 0)
    hrow = lambda i: (0, i, 0)
    return pl.pallas_call(
        body, name=name, grid=(S // tm,),
        in_specs=[pl.BlockSpec((H, tm, 256), hrow), pl.BlockSpec((H, tm, 256), hrow), pl.BlockSpec((H, tm, 128), hrow),
                  pl.BlockSpec((1, H, Q_LORA, 256), lambda i: (l, 0, 0, 0)),
                  pl.BlockSpec((1, KV_LORA, H * 256), lambda i: (l, 0, 0)),
                  pl.BlockSpec((tm, Q_LORA), row), pl.BlockSpec((tm, KV_LORA), row),
                  pl.BlockSpec((1, Q_LORA), lambda i: (0, 0)), pl.BlockSpec((1, KV_LORA), lambda i: (0, 0)),
                  pl.BlockSpec((tm, 128), row)],
        out_specs=[pl.BlockSpec((tm, H * 256), row), pl.BlockSpec((tm, H * 256), row), pl.BlockSpec((tm, 512), row),
                   pl.BlockSpec((1, Q_LORA), lambda i: (0, 0)), pl.BlockSpec((1, KV_LORA), lambda i: (0, 0))],
        out_shape=[SDS((S, H * 256), bf16), SDS((S, H * 256), bf16), SDS((S, 512), bf16),
                   SDS((1, Q_LORA), f32), SDS((1, KV_LORA), f32)],
        compiler_params=_cp("arbitrary"),
    )(dq, dk, dv, wq, wkv, cq, ckv, gq, gkv, cs)


def _cross_probs(qb, kv_ref, hd):
    cols = slice(hd * MEM_HEAD_DIM, (hd + 1) * MEM_HEAD_DIM)
    s = _dot_nt(qb[:, cols], kv_ref[:, cols]) * MEM_SCALE
    e = jnp.exp(s - jnp.max(s, axis=-1, keepdims=True))
    return e / jnp.sum(e, axis=-1, keepdims=True)


def cross_fwd(name, xb, xf, wq, wo, l, kv, g, b):
    S = xb.shape[0]
    tm = _tile(S, 512)
    M = kv.shape[0]

    def body(x_ref, xf_ref, wq_ref, wo_ref, k_ref, v_ref, g_ref, b_ref, q_ref, o_ref, z_ref, y_ref, yb_ref):
        qb = _dot(x_ref[...], wq_ref[0]).astype(bf16)
        q_ref[...] = qb
        for hd in range(MEM_HEADS):
            cols = slice(hd * MEM_HEAD_DIM, (hd + 1) * MEM_HEAD_DIM)
            p = _cross_probs(qb, k_ref, hd)
            o_ref[:, cols] = _dot(p.astype(bf16), v_ref[:, cols]).astype(bf16)
        z = ALPHA * xf_ref[...] + _dot(o_ref[...], wo_ref[0])
        mu = jnp.mean(z, axis=-1, keepdims=True)
        zc = z - mu
        var = jnp.mean(zc * zc, axis=-1, keepdims=True)
        y = zc * lax.rsqrt(var + LN_EPS) * g_ref[...] + b_ref[...]
        z_ref[...] = z
        y_ref[...] = y
        yb_ref[...] = y.astype(bf16)

    row = lambda i: (i, 0)
    wspec = pl.BlockSpec((1, D_MODEL, D_MODEL), lambda i: (l, 0, 0))
    vec = pl.BlockSpec((1, D_MODEL), lambda i: (0, 0))
    blk = pl.BlockSpec((tm, D_MODEL), row)
    return pl.pallas_call(
        body, name=name, grid=(S // tm,),
        in_specs=[blk, blk, wspec, wspec, pl.BlockSpec((M, D_MODEL), lambda i: (0, 0)),
                  pl.BlockSpec((M, D_MODEL), lambda i: (0, 1)), vec, vec],
        out_specs=[blk] * 5,
        out_shape=[SDS((S, D_MODEL), bf16), SDS((S, D_MODEL), bf16), SDS((S, D_MODEL), f32), SDS((S, D_MODEL), f32),
                   SDS((S, D_MODEL), bf16)],
        compiler_params=_cp("parallel"),
    )(xb, xf, wq, wo, kv, kv, g, b)


def cross_bwd(name, dzb, wo, l, qb, kv, deps=()):
    S = dzb.shape[0]
    tm = _tile(S, 512)
    M = kv.shape[0]

    def body(dz_ref, wo_ref, q_ref, k_ref, v_ref, dq_ref, dkv_ref):
        @pl.when(pl.program_id(0) == 0)
        def _():
            dkv_ref[...] = jnp.zeros_like(dkv_ref)

        do = _dot_nt(dz_ref[...], wo_ref[0]).astype(bf16)
        qv = q_ref[...]
        for hd in range(MEM_HEADS):
            cols = slice(hd * MEM_HEAD_DIM, (hd + 1) * MEM_HEAD_DIM)
            vcols = slice(D_MODEL + hd * MEM_HEAD_DIM, D_MODEL + (hd + 1) * MEM_HEAD_DIM)
            p = _cross_probs(qv, k_ref, hd)
            doh = do[:, cols]
            dkv_ref[:, vcols] += _dot_tn(p.astype(bf16), doh)
            dp = _dot_nt(doh, v_ref[:, cols])
            ds = (p * (dp - jnp.sum(dp * p, axis=-1, keepdims=True)) * MEM_SCALE).astype(bf16)
            dq_ref[:, cols] = _dot(ds, k_ref[:, cols]).astype(bf16)
            dkv_ref[:, cols] += _dot_tn(ds, qv[:, cols])

    row = lambda i: (i, 0)
    blk = pl.BlockSpec((tm, D_MODEL), row)
    return pl.pallas_call(
        _with_deps(body, 5, deps), name=name, grid=(S // tm,),
        in_specs=[blk, pl.BlockSpec((1, D_MODEL, D_MODEL), lambda i: (l, 0, 0)), blk,
                  pl.BlockSpec((M, D_MODEL), lambda i: (0, 0)), pl.BlockSpec((M, D_MODEL), lambda i: (0, 1))]
        + [_DEP_SPEC] * len(deps),
        out_specs=[blk, pl.BlockSpec((M, 2 * D_MODEL), lambda i: (0, 0))],
        out_shape=[SDS((S, D_MODEL), bf16), SDS((M, 2 * D_MODEL), f32)],
        compiler_params=_cp("arbitrary"),
    )(dzb, wo, qb, kv, kv, *deps)


def adamw(name, w, g, m, v, deps=()):
    shape = w.shape
    cols = shape[-1]
    rows = math.prod(shape[:-1])
    tr = _row_tile(rows, cols, target=2 * 2**20)
    c1 = 1.0 - ADAM_B1 ** ADAM_STEP
    c2 = 1.0 - ADAM_B2 ** ADAM_STEP

    def body(w_ref, g_ref, m_ref, v_ref, d_ref, nm_ref, nv_ref):
        gv = g_ref[...]
        nm = ADAM_B1 * m_ref[...] + (1.0 - ADAM_B1) * gv
        nv = ADAM_B2 * v_ref[...] + (1.0 - ADAM_B2) * (gv * gv)
        d_ref[...] = -ADAM_LR * ((nm / c1) / (jnp.sqrt(nv / c2) + ADAM_EPS) + ADAM_WD * w_ref[...])
        nm_ref[...] = nm
        nv_ref[...] = nv

    blk = pl.BlockSpec((tr, cols), lambda i: (i, 0))
    flat = SDS((rows, cols), f32)
    outs = pl.pallas_call(
        _with_deps(body, 4, deps), name=name, grid=(rows // tr,), in_specs=[blk] * 4 + [_DEP_SPEC] * len(deps),
        out_specs=[blk] * 3, out_shape=[flat] * 3, compiler_params=_cp("parallel"),
    )(*[a.reshape(rows, cols) for a in (w, g, m, v)], *deps)
    return [o.reshape(shape) for o in outs]


def _me():
    return lax.axis_index("x"), lax.axis_index("y"), lax.axis_index("c")


def _other_chips(x, y):
    return [(1 - x, y), (x, 1 - y), (1 - x, 1 - y)]


def _pair_share_each(owners, bufs, sems, mine, act):
    x, y, c = _me()
    for o in range(2):
        slots = [(a, lyr) for a in range(len(bufs)) for lyr in range(DEPTH) if owners[a][lyr] == o]

        @pl.when((c == o) if mine else (c != o))
        def _(slots=slots):
            for a, lyr in slots:
                slot = bufs[a].at[lyr]
                act(_rcopy(slot, slot, sems[0].at[2 * a + lyr], sems[1].at[2 * a + lyr], (x, y, 1 - c)))


def pair_share_start(name, sums, owners, after):
    def body_fn(b_in, s_in, s_out):
        _pair_share_each(owners, b_in, s_out, True, lambda cp: cp.start())

    outs, sems, token = _split_call(name, body_fn, list(sums), [], [2 * len(sums)] * 2, after)
    return (outs, sems[0], sems[1], owners), token


def pair_share_wait(name, st, after):
    bufs, send, recv, owners = st

    def body_fn(b_in, s_in, s_out):
        _pair_share_each(owners, b_in, s_in, True, lambda cp: cp.wait_send())
        _pair_share_each(owners, b_in, s_in, False, lambda cp: cp.wait_recv())

    outs, _, _ = _split_call(name, body_fn, list(bufs), [send, recv], [], after)
    return outs


def allsum_small(name, v, deps=()):
    R = v.shape[0]

    def body(v_ref, o_ref, all_ref, send_sems, recv_sems, local_sem):
        x, y, c = _me()
        me, sibling = (x, y, c), (x, y, 1 - c)
        chips = _other_chips(x, y)

        def rows(px, py, pc):
            return all_ref.at[4 * px + 2 * py + pc]

        def copy(k, block, to, src=None):
            return pltpu.make_async_remote_copy(
                src_ref=rows(*block) if src is None else src, dst_ref=rows(*block),
                send_sem=send_sems.at[k], recv_sem=recv_sems.at[k], device_id=to, device_id_type=MESH)

        mine = pltpu.make_async_copy(v_ref, rows(*me), local_sem)
        mine.start()
        first = [copy(0, me, sibling, src=v_ref)]
        first += [copy(1 + j, me, (*chip, c), src=v_ref) for j, chip in enumerate(chips)]
        for cp in first:
            cp.start()
        passed = [copy(4 + j, (*chip, c), sibling) for j, chip in enumerate(chips)]
        for j, chip in enumerate(chips):
            copy(1 + j, (*chip, c), me).wait_recv()
            passed[j].start()
        copy(0, sibling, me).wait_recv()
        for j, chip in enumerate(chips):
            copy(4 + j, (*chip, 1 - c), me).wait_recv()
        for cp in first + passed:
            cp.wait_send()
        mine.wait()
        acc = all_ref[0]
        for d in range(1, 8):
            acc = acc + all_ref[d]
        o_ref[...] = acc

    return pl.pallas_call(
        _with_deps(body, 1, deps), name=name,
        in_specs=[pl.BlockSpec(memory_space=pltpu.VMEM)] + [_DEP_SPEC] * len(deps),
        out_specs=pl.BlockSpec(memory_space=pltpu.VMEM),
        out_shape=SDS((R, 128), f32),
        scratch_shapes=[pltpu.VMEM((8, R, 128), f32), pltpu.SemaphoreType.DMA((7,)), pltpu.SemaphoreType.DMA((7,)),
                        pltpu.SemaphoreType.DMA],
        compiler_params=pltpu.CompilerParams(vmem_limit_bytes=V7X_VMEM_LIMIT),
    )(v, *deps)


def _swap_half(r):
    return jnp.concatenate([-r[..., 32:], r[..., :32]], axis=-1)


def _unswap_add(p, qg):
    return p + jnp.concatenate([qg[..., 32:], -qg[..., :32]], axis=-1)


def _block_diag(pw):
    L = pw.shape[0]
    out = jnp.zeros((L, 256, 256), pw.dtype)
    for gi in range(4):
        out = out.at[:, 64 * gi:64 * gi + 64, 64 * gi:64 * gi + 64].set(pw[:, gi])
    return out


def _to_col_shards(w):
    *lead, K, N = w.shape
    nl = len(lead)
    return w.reshape(*lead, K, N_CHIPS, N // N_CHIPS).transpose(*range(nl), nl + 1, nl, nl + 2)


def _from_col_shards(w):
    *lead, C, K, n = w.shape
    nl = len(lead)
    return w.transpose(*range(nl), nl + 1, nl, nl + 2).reshape(*lead, K, C * n)


_HBM_SPEC = pl.BlockSpec(memory_space=pltpu.HBM)
_SEM_SPEC = pl.BlockSpec(memory_space=pltpu.SEMAPHORE)
_ANY_SPEC = pl.BlockSpec(memory_space=pl.ANY)
_DATAFLOW = pltpu.SideEffectType.DATAFLOW_SIDE_EFFECTING


def _split_call(name, body_fn, bufs, sems_in, sems_out_sizes, after):
    nb, ni, no = len(bufs), len(sems_in), len(sems_out_sizes)
    afters = () if after is None else tuple(after) if isinstance(after, (tuple, list)) else (after,)

    def body(*refs):
        k = nb + ni + len(afters)
        body_fn(refs[:nb], refs[nb:nb + ni], refs[k:k + no])
        refs[-1][...] = jnp.zeros((8, 128), f32)

    outs = pl.pallas_call(
        body, name=name,
        in_specs=[_HBM_SPEC] * nb + [_SEM_SPEC] * ni + [_ANY_SPEC] * len(afters),
        out_specs=[_SEM_SPEC] * no + [_HBM_SPEC] * nb + [pl.BlockSpec(memory_space=pltpu.VMEM)],
        out_shape=[pltpu.SemaphoreType.DMA((s,)) for s in sems_out_sizes]
        + [pltpu.HBM(b.shape, b.dtype) for b in bufs] + [SDS((8, 128), f32)],
        input_output_aliases={i: no + i for i in range(nb)},
        compiler_params=pltpu.CompilerParams(has_side_effects=_DATAFLOW),
    )(*[pltpu.with_memory_space_constraint(b, pltpu.HBM) for b in bufs], *sems_in, *afters)
    return list(outs[no:no + nb]), list(outs[:no]), outs[-1]


def _rcopy(src, dst, ssem, rsem, to):
    return pltpu.make_async_remote_copy(src_ref=src, dst_ref=dst, send_sem=ssem, recv_sem=rsem, device_id=to,
                                        device_id_type=MESH)


def gather_start(name, groups, after):
    flat = [b for bufs, _ in groups for b in bufs]
    sizes = [3 * len(bufs) for bufs, _ in groups for _ in range(2)]

    def body_fn(b_in, s_in, s_out):
        x, y, c = _me()
        q = 2 * x + y
        chips = _other_chips(x, y)
        pos = 0
        for gi, (bufs, owner) in enumerate(groups):
            refs = b_in[pos:pos + len(bufs)]
            pos += len(bufs)

            @pl.when(c == owner)
            def _(refs=refs, send=s_out[2 * gi], recv=s_out[2 * gi + 1]):
                for a, r in enumerate(refs):
                    for k, (cx, cy) in enumerate(chips):
                        _rcopy(r.at[q], r.at[q], send.at[3 * a + k], recv.at[3 * a + k], (cx, cy, c)).start()

    outs, sems, token = _split_call(name, body_fn, flat, [], sizes, after)
    res, pos = [], 0
    for gi, (bufs, owner) in enumerate(groups):
        res.append((outs[pos:pos + len(bufs)], sems[2 * gi], sems[2 * gi + 1], owner))
        pos += len(bufs)
    return res, token


def gather_forward(name, grp, after):
    bufs, send, recv, owner = grp
    n3 = 3 * len(bufs)

    def body_fn(b_in, s_in, s_out):
        x, y, c = _me()
        q = 2 * x + y
        sibling = (x, y, 1 - c)
        chips = _other_chips(x, y)

        @pl.when(c == owner)
        def _():
            for a, r in enumerate(b_in):
                for k, (cx, cy) in enumerate(chips):
                    i = 3 * a + k
                    land = r.at[2 * cx + cy]
                    _rcopy(r.at[q], r.at[q], s_in[0].at[i], s_in[1].at[i], (cx, cy, c)).wait_send()
                    _rcopy(land, land, s_in[0].at[i], s_in[1].at[i], (cx, cy, c)).wait_recv()
                    _rcopy(land, land, s_out[0].at[i], s_out[1].at[i], sibling).start()

    outs, sems, token = _split_call(name, body_fn, bufs, [send, recv], [n3, n3], after)
    return (outs, sems[0], sems[1], owner), token


def gather_finish(name, grp, after):
    bufs, fsend, frecv, owner = grp

    def body_fn(b_in, s_in, s_out):
        x, y, c = _me()
        sibling = (x, y, 1 - c)
        chips = _other_chips(x, y)

        def each(wait):
            for a, r in enumerate(b_in):
                for k, (cx, cy) in enumerate(chips):
                    land = r.at[2 * cx + cy]
                    wait(_rcopy(land, land, s_in[0].at[3 * a + k], s_in[1].at[3 * a + k], sibling))

        @pl.when(c == owner)
        def _():
            each(lambda cp: cp.wait_send())

        @pl.when(c != owner)
        def _():
            each(lambda cp: cp.wait_recv())

    outs, _, _ = _split_call(name, body_fn, bufs, [fsend, frecv], [], after)
    return outs


def _by_owner(owners):
    return [[a for a, o_ in enumerate(owners) if o_ == o] for o in range(2)]


def pair_send_start(name, gs, owners, after):
    n = len(gs)
    lands = [lax.empty(g.shape, g.dtype) for g in gs]

    def body_fn(b_in, s_in, s_out):
        x, y, c = _me()
        for o, idx in enumerate(_by_owner(owners)):
            @pl.when(c == 1 - o)
            def _(o=o, idx=idx):
                for a in idx:
                    _rcopy(b_in[a], b_in[n + a], s_out[0].at[a], s_out[1].at[a], (x, y, o)).start()

    outs, sems, token = _split_call(name, body_fn, list(gs) + lands, [], [n, n], after)
    return (outs[:n], outs[n:], sems[0], sems[1], owners), token


def pair_send_wait(name, st, after):
    gs, lands, send, recv, owners = st
    n = len(gs)

    def body_fn(b_in, s_in, s_out):
        x, y, c = _me()
        for o, idx in enumerate(_by_owner(owners)):
            @pl.when(c == 1 - o)
            def _(o=o, idx=idx):
                for a in idx:
                    _rcopy(b_in[a], b_in[n + a], s_in[0].at[a], s_in[1].at[a], (x, y, o)).wait_send()

            @pl.when(c == o)
            def _(o=o, idx=idx):
                for a in idx:
                    _rcopy(b_in[a], b_in[n + a], s_in[0].at[a], s_in[1].at[a], (x, y, 1 - o)).wait_recv()

    outs, _, _ = _split_call(name, body_fn, list(gs) + list(lands), [send, recv], [], after)
    return outs[:n], outs[n:]


def chip_exchange_start(name, psums, owners, after):
    n = len(psums)
    lands = [lax.empty((3,) + p.shape[1:], p.dtype) for p in psums]

    def body_fn(b_in, s_in, s_out):
        x, y, c = _me()
        chips = _other_chips(x, y)
        for o, idx in enumerate(_by_owner(owners)):
            @pl.when(c == o)
            def _(idx=idx):
                for a in idx:
                    for k, (cx, cy) in enumerate(chips):
                        _rcopy(b_in[a].at[2 * cx + cy], b_in[n + a].at[k], s_out[0].at[3 * a + k],
                               s_out[1].at[3 * a + k], (cx, cy, c)).start()

    outs, sems, token = _split_call(name, body_fn, list(psums) + lands, [], [3 * n, 3 * n], after)
    return (outs[:n], outs[n:], sems[0], sems[1], owners), token


def chip_exchange_wait(name, st, after):
    psums, lands, send, recv, owners = st
    n = len(psums)

    def body_fn(b_in, s_in, s_out):
        x, y, c = _me()
        chips = _other_chips(x, y)
        for o, idx in enumerate(_by_owner(owners)):
            @pl.when(c == o)
            def _(idx=idx):
                for a in idx:
                    for k, (cx, cy) in enumerate(chips):
                        cp = _rcopy(b_in[a].at[2 * cx + cy], b_in[n + a].at[k], s_in[0].at[3 * a + k],
                                    s_in[1].at[3 * a + k], (cx, cy, c))
                        cp.wait_send()
                        cp.wait_recv()

    outs, _, _ = _split_call(name, body_fn, list(psums) + list(lands), [send, recv], [], after)
    return outs[:n], outs[n:]


def pair_sum(name, g, recv, flag):
    shape = g.shape
    cols = shape[-1]
    rows = math.prod(shape[:-1])
    tr = _row_tile(rows, cols, target=4 * 2**20)

    def body(f_ref, g_ref, r_ref, o_ref):
        o_ref[...] = (g_ref[...] + r_ref[...]).astype(bf16)

    blk = pl.BlockSpec((tr, cols), lambda i, f_ref: (i * f_ref[0], 0))
    out = pl.pallas_call(
        body, name=name,
        grid_spec=pltpu.PrefetchScalarGridSpec(num_scalar_prefetch=1, grid=(rows // tr,), in_specs=[blk, blk],
                                               out_specs=blk),
        out_shape=SDS((rows, cols), bf16), compiler_params=_cp("arbitrary"),
    )(flag, g.reshape(rows, cols), recv.reshape(rows, cols))
    return out.reshape(shape)


def chip_sum(name, psum, recv, qf_arr, layer, prev):
    shard = psum.shape[1:]
    cols = shard[-1]
    rows = math.prod(shard[:-1])
    tr = _row_tile(rows, cols, target=4 * 2**20)

    def body(qf_ref, p_ref, r_ref, *rest):
        rest[-1][0] = ((p_ref[0].astype(f32) + r_ref[0].astype(f32)) + r_ref[1].astype(f32)) + r_ref[2].astype(f32)

    in_specs = [pl.BlockSpec((1, tr, cols), lambda i, qf: (qf[0], i * qf[1], 0)),
                pl.BlockSpec((3, tr, cols), lambda i, qf: (0, i * qf[1], 0))]
    args = [qf_arr, psum.reshape(N_CHIPS, rows, cols), recv.reshape(3, rows, cols)]
    aliases = {}
    if prev is not None:
        in_specs.append(pl.BlockSpec(memory_space=pl.ANY))
        args.append(prev.reshape(DEPTH, rows, cols))
        aliases = {3: 0}
    out = pl.pallas_call(
        body, name=name,
        grid_spec=pltpu.PrefetchScalarGridSpec(
            num_scalar_prefetch=1, grid=(rows // tr,), in_specs=in_specs,
            out_specs=pl.BlockSpec((1, tr, cols), lambda i, qf: (layer, i * qf[1], 0))),
        out_shape=SDS((DEPTH, rows, cols), f32), input_output_aliases=aliases, compiler_params=_cp("arbitrary"),
    )(*args)
    return out.reshape((DEPTH,) + shard)


W_NAMES = ("f1w13", "f1w2", "win", "wuq", "wukv", "wout", "mwq", "mwkv", "mwo", "f2w13", "f2w2")
MIX_NAMES = ("win", "wuq", "wukv")
MID_NAMES = ("wout", "mwq", "mwkv", "mwo")
FFN2_NAMES = ("f2w13", "f2w2")
REDUCER = (dict(f1w13=0, f1w2=1, f2w13=0, win=0, wuq=0, wukv=0, f2w2=1, mwkv=1, wout=1, mwq=1, mwo=1),
           dict(f1w13=0, f2w2=0, mwkv=0, wout=0, f2w13=1, f1w2=1, mwq=1, mwo=1, win=1, wuq=1, wukv=1))


def kernel(x, mem, positions, ln_g, ln_b, ffn1_w13, ffn1_w2, w_in, pool_w, pool_scale, q_norm_g, w_uq, kv_norm_g, w_ukv, w_out, mem_wq, mem_wkv, mem_wo, ffn2_w13, ffn2_w2, loss_target, m_ln_g, m_ln_b, m_ffn1_w13, m_ffn1_w2, m_w_in, m_pool_w, m_pool_scale, m_q_norm_g, m_w_uq, m_kv_norm_g, m_w_ukv, m_w_out, m_mem_wq, m_mem_wkv, m_mem_wo, m_ffn2_w13, m_ffn2_w2, v_ln_g, v_ln_b, v_ffn1_w13, v_ffn1_w2, v_w_in, v_pool_w, v_pool_scale, v_q_norm_g, v_w_uq, v_kv_norm_g, v_w_ukv, v_w_out, v_mem_wq, v_mem_wkv, v_mem_wo, v_ffn2_w13, v_ffn2_w2):
    L = DEPTH
    qx, qy, _ = _me()
    chip = 2 * qx + qy
    vec = lambda a: a.reshape(1, -1)

    shards = dict(zip(W_NAMES, (ffn1_w13, ffn1_w2, w_in, w_uq, w_ukv, w_out, mem_wq, mem_wkv, mem_wo, ffn2_w13, ffn2_w2)))

    def place(sh, slot):
        return lax.dynamic_update_slice(lax.empty((N_CHIPS,) + sh.shape, bf16), sh.astype(bf16)[None],
                                        (slot,) + (0,) * sh.ndim)

    first = ("f1w13", "f1w2")
    bufs = [dict(), dict()]
    for n in first:
        bufs[0][n] = place(shards[n][0], chip)
    gw = [dict(), dict()]

    ln_pad = jnp.zeros((2, L, 4, N_CHIPS, D_MODEL // N_CHIPS), f32)
    ln_pad = lax.dynamic_update_slice(ln_pad, jnp.stack([ln_g, ln_b])[:, :, :, None, :], (0, 0, 0, chip, 0))
    ln_sum = allsum_small("allsum_ln", ln_pad.reshape(-1, 128))
    ln_full = (ln_sum * 0.5).reshape(2, L, 4, D_MODEL)
    lng, lnb = ln_full[0], ln_full[1]

    (g0, g_w2), tok = gather_start("gather_a_start", [([bufs[0]["f1w13"]], 0), ([bufs[0]["f1w2"]], 0)], ln_sum)
    chip_then = chip + tok[0, 0].astype(jnp.int32)
    for l in range(L):
        for n in W_NAMES:
            if n not in bufs[l]:
                bufs[l][n] = place(shards[n][l], chip_then)
    others = tuple(bufs[l][n] for l in range(L) for n in W_NAMES if (l, n) not in ((0, first[0]), (0, first[1])))
    g0, tok = gather_forward("gather_a_forward", g0, others)
    (gw[0]["f1w13"],) = gather_finish("gather_a_finish", g0, None)
    (g_mix, g_mid, g_ffn2, g_l1), tok_b = gather_start(
        "gather_b_start",
        [([bufs[0][n] for n in MIX_NAMES], 0), ([bufs[0][n] for n in MID_NAMES], 0), ([bufs[0][n] for n in FFN2_NAMES], 0),
         ([bufs[1][n] for n in W_NAMES], 1)], tok)

    half = QK_ROPE // 2
    inv_freq = ROPE_BASE ** (-jnp.arange(half, dtype=f32) / half)
    ang = positions[0].astype(f32)[:, None] * inv_freq
    cos, sin = jnp.cos(ang), jnp.sin(ang)
    cs = jnp.concatenate([cos, cos, sin, sin], axis=-1)

    memb = mem[0].astype(bf16)
    xf = x[0]
    xb = xf.astype(bf16)
    dep = (tok_b,)

    saved, W = [], [None, None]
    for l in range(L):
        sv = {}
        if l == 1:
            gl1 = gather_finish("gather_l1_finish", g_l1, xb)
            gw[1] = dict(zip(W_NAMES, gl1))
        sv["x0b"] = xb
        f1w13 = gw[l]["f1w13"][None]
        gate, up, act = ffn_up(f"ffn1_up_{l}", xb, f1w13, 0, dep)
        dep = ()
        if l == 0:
            g_w2, _ = gather_forward("gather_w2_forward", g_w2, act)
            (gw[0]["f1w2"],) = gather_finish("gather_w2_finish", g_w2, act)
            g_mix, _ = gather_forward("gather_mix_forward", g_mix, act)
        z1, x1f, x1b = proj_res_ln(f"ffn1_down_{l}", [act], [gw[l]["f1w2"].reshape(1, D_FF, D_MODEL)], [0], xf,
                                   vec(lng[l, 0]), vec(lnb[l, 0]), 0.5)
        sv.update(gate1=gate, up1=up, act1=act, z1=z1, x1b=x1b)
        if l == 0:
            gw[0].update(zip(MIX_NAMES, gather_finish("gather_mix_finish", g_mix, x1b)))
            g_mid, _ = gather_forward("gather_mid_forward", g_mid, x1b)
        win = gw[l]["win"].reshape(D_MODEL, D_IN)
        win_ext = jnp.concatenate([win, _swap_half(win[:, D_IN - QK_ROPE:])], axis=-1)[None]
        wuq = _from_col_shards(gw[l]["wuq"]).reshape(Q_LORA, MLA_HEADS, QK_NOPE + QK_ROPE)
        wq_ext = jnp.concatenate([wuq, _swap_half(wuq[..., QK_NOPE:])], axis=-1).transpose(1, 0, 2)[None]
        wukv = _from_col_shards(gw[l]["wukv"])[None]
        wbd = _block_diag(pool_w[l][None].astype(bf16))[0]
        u, cq, ckv, cqn, ckvn, q, k, v = mix_pre(f"mix_pre_{l}", x1b, win_ext, wq_ext, wukv, 0,
                                                   vec(q_norm_g[l]), vec(kv_norm_g[l]), cs)
        dpool, ypool = pool_fwd(f"pool_fwd_{l}", u, wbd, vec(pool_scale[l]))
        o, lse = mla_attn_fwd(f"mla_fwd_{l}", q, k, v)
        if l == 0:
            gw[0].update(zip(MID_NAMES, gather_finish("gather_mid_finish", g_mid, o)))
            g_ffn2, tok_f = gather_forward("gather_ffn2_forward", g_ffn2, o)
            dep = (tok_f,)
        wout = gw[l]["wout"].reshape(D_MODEL, D_MODEL)
        wout_pool, wout_mla = wout[None, :POOL_WIDTH], wout[None, POOL_WIDTH:]
        mwq = gw[l]["mwq"].reshape(1, D_MODEL, D_MODEL)
        mwo = gw[l]["mwo"].reshape(1, D_MODEL, D_MODEL)
        mwkv = gw[l]["mwkv"][None]
        z2, x2f, x2b = proj_res_ln(f"mix_out_{l}", [ypool, o], [wout_pool, wout_mla], [0, 0], x1f,
                                   vec(lng[l, 1]), vec(lnb[l, 1]), 1.0, dep)
        dep = ()
        sv.update(cq=cq, ckv=ckv, cqn=cqn, ckvn=ckvn, q=q, k=k, v=v, dpool=dpool, ypool=ypool, o=o, lse=lse, z2=z2, x2b=x2b)
        kvm = mm_nn_shard(f"mem_kv_{l}", memb, mwkv, 0)
        cq_, co_, z3, x3f, x3b = cross_fwd(f"cross_fwd_{l}", x2b, x2f, mwq, mwo, 0, kvm, vec(lng[l, 2]), vec(lnb[l, 2]))
        sv.update(kvm=kvm, crq=cq_, cro=co_, z3=z3, x3b=x3b)
        if l == 0:
            gw[0].update(zip(FFN2_NAMES, gather_finish("gather_ffn2_finish", g_ffn2, x3b)))
            g_l1, tok_l = gather_forward("gather_l1_forward", g_l1, x3b)
            dep = (tok_l,)
        f2w13 = gw[l]["f2w13"][None]
        f2w2 = gw[l]["f2w2"].reshape(1, D_FF, D_MODEL)
        gate, up, act = ffn_up(f"ffn2_up_{l}", x3b, f2w13, 0, dep)
        dep = ()
        z4, xf, xb = proj_res_ln(f"ffn2_down_{l}", [act], [f2w2], [0], x3f, vec(lng[l, 3]), vec(lnb[l, 3]), 0.5)
        sv.update(gate2=gate, up2=up, act2=act, z4=z4)
        W[l] = dict(f1w13=f1w13, f1w2=gw[l]["f1w2"].reshape(1, D_FF, D_MODEL), win_ext=win_ext, wq_ext=wq_ext, wukv=wukv,
                    wbd=wbd, wout=wout[None], mwq=mwq, mwo=mwo, f2w13=f2w13, f2w2=f2w2)
        saved.append(sv)

    dln = {}
    dzb, dres, *dln[L - 1, 3], loss_blk = loss_grad("loss_grad", xf, loss_target[0],
                                                   (saved[L - 1]["z4"], vec(lng[L - 1, 3]), 0.5))
    loss = lax.psum(loss_blk[0, 0], ("x", "y", "c"))

    row_shards = lambda a: a.reshape(N_CHIPS, a.shape[0] // N_CHIPS, a.shape[1])
    small = {k_: [None] * L for k_ in ("pool_w", "pool_scale", "gq", "gkv", "lng", "lnb")}
    rest_names = [n for n in W_NAMES if n not in ("f1w13", "f1w2")]
    core = lax.axis_index("c")
    flags = [jnp.reshape(core == o, (1,)).astype(jnp.int32) for o in range(2)]
    qfs = [jnp.stack([chip, (core == o).astype(jnp.int32)]).astype(jnp.int32) for o in range(2)]

    def red_begin(tag, names, gs, layer):
        owners = [REDUCER[layer][n] for n in names]
        st, tok_ = pair_send_start(f"pair_send_start_{tag}", gs, owners, None)
        return (st, owners), tok_

    def red_mid(tag, sto, after):
        st, owners = sto
        gs_, lands_ = pair_send_wait(f"pair_send_wait_{tag}", st, after)
        ps = [pair_sum(f"pair_sum_{tag}_{a}", g_, r_, flags[o]) for a, (g_, r_, o) in enumerate(zip(gs_, lands_, owners))]
        st, tok_ = chip_exchange_start(f"chip_exchange_start_{tag}", ps, owners, None)
        return (st, owners), tok_

    def red_end(tag, sto, layer, prevs, after):
        st, owners = sto
        ps, lands_ = chip_exchange_wait(f"chip_exchange_wait_{tag}", st, after)
        return [chip_sum(f"chip_sum_{tag}_{a}", p_, r_, qfs[o], layer, s_)
                for a, (p_, r_, s_, o) in enumerate(zip(ps, lands_, prevs, owners))]

    def share_start(tag, names, sums_):
        return pair_share_start(f"pair_share_start_{tag}", sums_, [(REDUCER[0][n], REDUCER[1][n]) for n in names], None)

    st_p1 = st_c1 = st_pa = st_ca = None
    for l in reversed(range(L)):
        sv, w = saved[l], W[l]
        g = {}
        dh = ffn_bwd_da(f"ffn2_bwd_da_{l}", dzb, w["f2w2"], 0, sv["gate2"], sv["up2"], dep)
        dep = ()
        g["f2w2"] = row_shards(mm_tn(f"ffn2_dw2_{l}", sv["act2"], dzb))
        g["f2w13"] = mm_tn(f"ffn2_dw13_{l}", sv["x3b"], dh, True)
        dzb, dres, *dln[l, 2] = ffn_dx(f"ffn2_dx_{l}", dh, w["f2w13"], 0, dres, (sv["z3"], vec(lng[l, 2]), 1.0))
        if l == 0:
            st_c1, tok = red_mid("l1", st_p1, dzb)
            dep = (tok, g["f2w2"], g["f2w13"])
        dqc, dkvm = cross_bwd(f"cross_bwd_{l}", dzb, w["mwo"], 0, sv["crq"], sv["kvm"], dep)
        dep = ()
        g["mwo"] = row_shards(mm_tn(f"cross_dwo_{l}", sv["cro"], dzb))
        g["mwq"] = row_shards(mm_tn(f"cross_dwq_{l}", sv["x2b"], dqc))
        g["mwkv"] = mm_tn(f"cross_dwkv_{l}", memb, dkvm, True)
        dzb, dres, *dln[l, 1] = mm_nt_res(f"cross_dx_{l}", [dqc], [w["mwq"]], [0], dres, f32,
                                          (sv["z2"], vec(lng[l, 1]), 1.0))
        dcat = mm_nt_res(f"mix_dcat_{l}", [dzb], [w["wout"]], [0], None, bf16)
        dwo_p = mm_tn(f"mix_dwout_pool_{l}", sv["ypool"], dzb)
        dwo_m = mm_tn(f"mix_dwout_mla_{l}", sv["o"], dzb)
        g["wout"] = row_shards(jnp.concatenate([dwo_p, dwo_m], axis=0))
        dq, dk, dv = mla_attn_bwd(f"mla_bwd_{l}", sv["q"], sv["k"], sv["v"], sv["o"], dcat, sv["lse"], POOL_WIDTH // 128)
        dqe, dkv, dh_rest, dgq, dgkv = mix_post_bwd(f"mix_post_bwd_{l}", dq, dk, dv, w["wq_ext"], w["wukv"], 0, sv["cq"],
                                                     sv["ckv"], vec(q_norm_g[l]), vec(kv_norm_g[l]), cs)
        du, dyw, dscale = pool_bwd(f"pool_bwd_{l}", dcat, sv["dpool"], w["wbd"], vec(pool_scale[l]))
        dwq_e = mm_tn(f"mix_dwuq_{l}", sv["cqn"], dqe).reshape(Q_LORA, MLA_HEADS, 256)
        g["wuq"] = _to_col_shards(jnp.concatenate(
            [dwq_e[..., :QK_NOPE], _unswap_add(dwq_e[..., QK_NOPE:QK_NOPE + QK_ROPE], dwq_e[..., QK_NOPE + QK_ROPE:])],
            axis=-1).reshape(Q_LORA, MLA_HEADS * (QK_NOPE + QK_ROPE)))
        g["wukv"] = _to_col_shards(mm_tn(f"mix_dwukv_{l}", sv["ckvn"], dkv))
        dwbd = mm_tn(f"pool_dw_{l}", sv["dpool"], dyw)
        small["pool_w"][l] = jnp.stack([dwbd[64 * gi:64 * gi + 64, 64 * gi:64 * gi + 64] for gi in range(4)])
        small["pool_scale"][l], small["gq"][l], small["gkv"][l] = dscale[0], dgq[0], dgkv[0]
        dh_ext = jnp.concatenate([du, dh_rest], axis=1)
        dwin_e = mm_tn(f"mix_dwin_{l}", sv["x1b"], dh_ext)
        g["win"] = row_shards(jnp.concatenate(
            [dwin_e[:, :D_IN - QK_ROPE], _unswap_add(dwin_e[:, D_IN - QK_ROPE:D_IN], dwin_e[:, D_IN:])], axis=-1))
        dzb, dres, *dln[l, 0] = mm_nt_res(f"mix_dx_{l}", [dh_ext], [w["win_ext"]], [0], dres, f32,
                                          (sv["z1"], vec(lng[l, 0]), 0.5))
        if l == 0:
            st_pa, tok = red_begin("a0", rest_names, [g[n] for n in rest_names], 0)
            dep = (tok,)
        dh = ffn_bwd_da(f"ffn1_bwd_da_{l}", dzb, w["f1w2"], 0, sv["gate1"], sv["up1"], dep)
        dep = ()
        if l == 0:
            grad_x = ffn_dx(f"ffn1_dx_{l}", dh, w["f1w13"], 0, dres)[None]
            for l_ in range(L):
                small["lng"][l_] = jnp.concatenate([dln[l_, k_][0] for k_ in range(4)], axis=0)
                small["lnb"][l_] = jnp.concatenate([dln[l_, k_][1] for k_ in range(4)], axis=0)
            rep = [jnp.stack(small[k_]).reshape(-1) for k_ in ("pool_w", "pool_scale", "gq", "gkv", "lng", "lnb")]
            sizes = [r.shape[0] for r in rep]
            packed = jnp.concatenate(rep)
            packed = jnp.pad(packed, (0, (-packed.shape[0]) % 1024)).reshape(-1, 128)
            tot = allsum_small("allsum_small_grads", packed, (grad_x,)).reshape(-1)
            st_ca, tok = red_mid("a0", st_pa, (grad_x, tot))
            dep = (tok,)
        else:
            below = ffn_dx(f"ffn1_dx_{l}", dh, w["f1w13"], 0, dres, (saved[l - 1]["z4"], vec(lng[l - 1, 3]), 0.5))
            dln[l - 1, 3] = below[2:]
        g["f1w2"] = row_shards(mm_tn(f"ffn1_dw2_{l}", sv["act1"], dzb, False, dep))
        g["f1w13"] = mm_tn(f"ffn1_dw13_{l}", sv["x0b"], dh, True, dep)
        dep = ()
        if l > 0:
            dzb, dres = below[:2]
        if l == 1:
            st_p1, tok = red_begin("l1", W_NAMES, [g[n] for n in W_NAMES], 1)
            dep = (tok,)

    st_pb, _ = red_begin("b0", ("f1w13", "f1w2"), [g["f1w13"], g["f1w2"]], 0)
    sums1 = dict(zip(W_NAMES, red_end("l1", st_c1, 1, [None] * len(W_NAMES), g["f1w13"])))
    st_cb, tok = red_mid("b0", st_pb, tuple(sums1.values()))
    sums0 = red_end("a0", st_ca, 0, [sums1[n] for n in rest_names], tok)
    share_a, tok_a = share_start("a", rest_names, sums0)

    offs = [0]
    for s_ in sizes:
        offs.append(offs[-1] + s_)
    parts = [tot[offs[i]:offs[i + 1]] for i in range(len(sizes))]
    g_pool_w = parts[0].reshape(pool_w.shape)
    g_pool_scale = parts[1].reshape(pool_scale.shape)
    g_gq = parts[2].reshape(q_norm_g.shape)
    g_gkv = parts[3].reshape(kv_norm_g.shape)
    shard_cols = lambda a: lax.dynamic_slice_in_dim(a.reshape(L, 4, D_MODEL), chip * (D_MODEL // N_CHIPS),
                                                    D_MODEL // N_CHIPS, axis=2)
    g_lng, g_lnb = shard_cols(parts[4]), shard_cols(parts[5])

    out_names = ("lng", "lnb", "f1w13", "f1w2", "win", "pool_w", "pool_scale", "gq", "wuq", "gkv", "wukv", "wout", "mwq",
                 "mwkv", "mwo", "f2w13", "f2w2")
    big = dict(lng=g_lng, lnb=g_lnb, pool_w=g_pool_w, pool_scale=g_pool_scale, gq=g_gq, gkv=g_gkv)
    late = ("f1w13", "f1w2")
    held = ("f2w13", "f2w2")
    ws = [ln_g, ln_b, ffn1_w13, ffn1_w2, w_in, pool_w, pool_scale, q_norm_g, w_uq, kv_norm_g, w_ukv, w_out, mem_wq,
          mem_wkv, mem_wo, ffn2_w13, ffn2_w2]
    ms = [m_ln_g, m_ln_b, m_ffn1_w13, m_ffn1_w2, m_w_in, m_pool_w, m_pool_scale, m_q_norm_g, m_w_uq, m_kv_norm_g, m_w_ukv,
          m_w_out, m_mem_wq, m_mem_wkv, m_mem_wo, m_ffn2_w13, m_ffn2_w2]
    vs = [v_ln_g, v_ln_b, v_ffn1_w13, v_ffn1_w2, v_w_in, v_pool_w, v_pool_scale, v_q_norm_g, v_w_uq, v_kv_norm_g, v_w_ukv,
          v_w_out, v_mem_wq, v_mem_wkv, v_mem_wo, v_ffn2_w13, v_ffn2_w2]
    res = {}

    def update(n, deps=()):
        a = out_names.index(n)
        res[a] = adamw(f"adamw_{a}", ws[a], big[n].reshape(ws[a].shape), ms[a], vs[a], deps)
        return res[a][0]

    small_done = tuple(update(n, (tok_a,)) for n in ("lng", "lnb", "pool_w", "pool_scale", "gq", "gkv"))
    big.update(zip(rest_names, pair_share_wait("pair_share_wait_a", share_a, small_done)))
    first_done = tuple(update(n) for n in rest_names if n not in held)
    sums_b = red_end("b0", st_cb, 0, [sums1[n] for n in late], first_done)
    share_b, tok_b = share_start("b", late, sums_b)
    held_done = tuple(update(n, (tok_b,)) for n in held)
    big.update(zip(late, pair_share_wait("pair_share_wait_b", share_b, held_done)))
    for n in late:
        update(n)
    order = range(len(out_names))
    grads = [big[n].reshape(w_.shape) for n, w_ in zip(out_names, ws)]
    return (loss, grad_x, *grads, *[res[a][0] for a in order], *[res[a][1] for a in order], *[res[a][2] for a in order])
```

```python
import functools
import math

import jax
import jax.numpy as jnp
from jax import lax
from jax.experimental import pallas as pl
from jax.experimental.pallas import tpu as pltpu

f32 = jnp.float32
bf16 = jnp.bfloat16
SDS = jax.ShapeDtypeStruct
MESH = pl.DeviceIdType.MESH

D_MODEL = 1024
DEPTH = 2
N_MEM = 256
MEM_HEADS = 4
MEM_HEAD_DIM = D_MODEL // MEM_HEADS
POOL_WINDOWS = (2, 4, 8, 16)
POOL_WIDTH = 256
POOL_GROUP = 64
QK_NOPE = 128
QK_ROPE = 64
V_HEAD = 128
MLA_HEADS = 6
Q_LORA = 256
KV_LORA = 128
ROPE_BASE = 10000.0
D_FF = 2816
D_IN = POOL_WIDTH + Q_LORA + KV_LORA + QK_ROPE
ALPHA = (2 * DEPTH) ** 0.25
LN_EPS = 1e-5
RMS_EPS = 1e-6
NEG_INF = -1e30
MLA_SCALE = (QK_NOPE + QK_ROPE) ** -0.5
MLA_SCALE_LOG2 = MLA_SCALE * math.log2(math.e)
MEM_SCALE = MEM_HEAD_DIM ** -0.5
ADAM_LR = 0.001
ADAM_B1 = 0.9
ADAM_B2 = 0.999
ADAM_EPS = 1e-08
ADAM_WD = 0.01
ADAM_STEP = 10

N_CHIPS = 4
V7X_VMEM_LIMIT = 56 * 2**20
HALO = 16

_NT = (((1,), (1,)), ((), ()))
_TN = (((0,), (0,)), ((), ()))


def _dot(a, b):
    return jnp.dot(a, b, preferred_element_type=f32)


def _dot_nt(a, b):
    return lax.dot_general(a, b, _NT, preferred_element_type=f32)


def _dot_tn(a, b):
    return lax.dot_general(a, b, _TN, preferred_element_type=f32)


def _cp(*sem):
    return pltpu.CompilerParams(dimension_semantics=sem if sem else None, vmem_limit_bytes=V7X_VMEM_LIMIT)


_DEP_SPEC = pl.BlockSpec(memory_space=pl.ANY)


def _with_deps(body, n_in, deps):
    nd = len(deps)
    if not nd:
        return body

    def wrapped(*refs):
        return body(*refs[:n_in], *refs[n_in + nd:])

    return wrapped


def _tile(n, t):
    t = min(n, t)
    assert n % t == 0, (n, t)
    return t


def _row_tile(rows, cols, itemsize=4, target=2 * 2**20):
    best = None
    for t in range(16, rows + 1, 16):
        if rows % t == 0 and t * cols * itemsize <= target:
            best = t
    return best if best is not None else rows


def ffn_up(name, xb, w13, l, deps=()):
    S = xb.shape[0]
    ns = w13.shape[3]
    tm = _tile(S, 512)

    def body(x_ref, wg_ref, wu_ref, g_ref, u_ref, a_ref):
        x = x_ref[...]
        g = _dot(x, wg_ref[0, 0])
        u = _dot(x, wu_ref[0, 0])
        a = g * jax.nn.sigmoid(g) * u
        g_ref[...] = g.astype(bf16)
        u_ref[...] = u.astype(bf16)
        a_ref[...] = a.astype(bf16)

    out = SDS((S, 2 * ns), bf16)
    return pl.pallas_call(
        _with_deps(body, 3, deps), name=name, grid=(2, S // tm),
        in_specs=[pl.BlockSpec((tm, D_MODEL), lambda j, i: (i, 0)),
                  pl.BlockSpec((1, 1, D_MODEL, ns), lambda j, i: (l, j, 0, 0)),
                  pl.BlockSpec((1, 1, D_MODEL, ns), lambda j, i: (l, j + 2, 0, 0))] + [_DEP_SPEC] * len(deps),
        out_specs=[pl.BlockSpec((tm, ns), lambda j, i: (i, j))] * 3,
        out_shape=[out, out, out],
        compiler_params=_cp("parallel", "parallel"),
    )(xb, w13, w13, *deps)


def proj_res_ln(name, parts, ws, wl, x, g, b, rscale, deps=()):
    S = x.shape[0]
    tm = _tile(S, 512)
    n = len(parts)

    def body(*refs):
        p_refs, w_refs = refs[:n], refs[n:2 * n]
        x_ref, g_ref, b_ref, z_ref, y_ref, yb_ref = refs[2 * n:]
        acc = _dot(p_refs[0][...], w_refs[0][0])
        for k in range(1, n):
            acc = acc + _dot(p_refs[k][...], w_refs[k][0])
        if rscale != 1.0:
            acc = rscale * acc
        z = ALPHA * x_ref[...] + acc
        mu = jnp.mean(z, axis=-1, keepdims=True)
        zc = z - mu
        var = jnp.mean(zc * zc, axis=-1, keepdims=True)
        y = zc * lax.rsqrt(var + LN_EPS) * g_ref[...] + b_ref[...]
        z_ref[...] = z
        y_ref[...] = y
        yb_ref[...] = y.astype(bf16)

    row = lambda i: (i, 0)
    in_specs = [pl.BlockSpec((tm, p.shape[1]), row) for p in parts]
    in_specs += [pl.BlockSpec((1,) + w.shape[1:], functools.partial(lambda li, i: (li, 0, 0), li)) for w, li in zip(ws, wl)]
    in_specs += [pl.BlockSpec((tm, D_MODEL), row), pl.BlockSpec((1, D_MODEL), lambda i: (0, 0)),
                 pl.BlockSpec((1, D_MODEL), lambda i: (0, 0))] + [_DEP_SPEC] * len(deps)
    return pl.pallas_call(
        _with_deps(body, 2 * n + 3, deps), name=name, grid=(S // tm,), in_specs=in_specs,
        out_specs=[pl.BlockSpec((tm, D_MODEL), row)] * 3,
        out_shape=[SDS((S, D_MODEL), f32), SDS((S, D_MODEL), f32), SDS((S, D_MODEL), bf16)],
        compiler_params=_cp("parallel"),
    )(*parts, *ws, x, g, b, *deps)


def _ln_bwd_store(dyv, z_ref, g_ref, rscale, first, dzb_ref, dres_ref, dg_ref, db_ref):
    z = z_ref[...]
    mu = jnp.mean(z, axis=-1, keepdims=True)
    zc = z - mu
    rstd = lax.rsqrt(jnp.mean(zc * zc, axis=-1, keepdims=True) + LN_EPS)
    xhat = zc * rstd
    dxh = dyv * g_ref[...]
    m1 = jnp.mean(dxh, axis=-1, keepdims=True)
    m2 = jnp.mean(dxh * xhat, axis=-1, keepdims=True)
    dz = rstd * (dxh - m1 - xhat * m2)
    dzb_ref[...] = (rscale * dz).astype(bf16)
    dres_ref[...] = ALPHA * dz

    @pl.when(first)
    def _():
        dg_ref[...] = jnp.zeros_like(dg_ref)
        db_ref[...] = jnp.zeros_like(db_ref)

    dg_ref[...] += jnp.sum(dyv * xhat, axis=0, keepdims=True)
    db_ref[...] += jnp.sum(dyv, axis=0, keepdims=True)


def _ln_bwd_specs(S, tm, index):
    vec = pl.BlockSpec((1, D_MODEL), lambda *a: (0, 0))
    blk = pl.BlockSpec((tm, D_MODEL), index)
    in_specs = [blk, vec]
    out_specs = [blk, blk, vec, vec]
    out_shape = [SDS((S, D_MODEL), bf16), SDS((S, D_MODEL), f32), SDS((1, D_MODEL), f32), SDS((1, D_MODEL), f32)]
    return in_specs, out_specs, out_shape


def ffn_bwd_da(name, drb, w2, l, gate, up, deps=()):
    S = drb.shape[0]
    tm = _tile(S, 512)
    nh = D_FF // 2

    def body(dr_ref, w_ref, g_ref, u_ref, dh_ref):
        dr = dr_ref[...]
        for j in range(2):
            cols = slice(j * nh, (j + 1) * nh)
            da = _dot_nt(dr, w_ref[0, cols, :])
            g = g_ref[:, cols].astype(f32)
            u = u_ref[:, cols].astype(f32)
            sg = jax.nn.sigmoid(g)
            dh_ref[:, cols] = (da * u * (sg * (1.0 + g * (1.0 - sg)))).astype(bf16)
            dh_ref[:, D_FF + j * nh:D_FF + (j + 1) * nh] = (da * (g * sg)).astype(bf16)

    row = lambda i: (i, 0)
    return pl.pallas_call(
        _with_deps(body, 4, deps), name=name, grid=(S // tm,),
        in_specs=[pl.BlockSpec((tm, D_MODEL), row), pl.BlockSpec((1, D_FF, D_MODEL), lambda i: (l, 0, 0)),
                  pl.BlockSpec((tm, D_FF), row), pl.BlockSpec((tm, D_FF), row)] + [_DEP_SPEC] * len(deps),
        out_specs=pl.BlockSpec((tm, 2 * D_FF), row),
        out_shape=SDS((S, 2 * D_FF), bf16),
        compiler_params=_cp("parallel"),
    )(drb, w2, gate, up, *deps)


def ffn_dx(name, dh, w13, l, res, ln=None):
    S = dh.shape[0]
    ns = w13.shape[3]
    tm = _tile(S, 1024)
    last = N_CHIPS - 1
    row = lambda i, j: (i, 0)
    in_specs = [pl.BlockSpec((tm, ns), lambda i, j: (i, j)),
                pl.BlockSpec((1, 1, D_MODEL, ns), lambda i, j: (l, j, 0, 0)),
                pl.BlockSpec((tm, D_MODEL), row)]
    if ln is None:
        def body(dh_ref, w_ref, r_ref, o_ref):
            @pl.when(pl.program_id(1) == 0)
            def _():
                o_ref[...] = r_ref[...]

            o_ref[...] += _dot_nt(dh_ref[...], w_ref[0, 0])

        return pl.pallas_call(
            body, name=name, grid=(S // tm, N_CHIPS), in_specs=in_specs,
            out_specs=pl.BlockSpec((tm, D_MODEL), row), out_shape=SDS((S, D_MODEL), f32),
            compiler_params=_cp("parallel", "arbitrary"),
        )(dh, w13, res)

    z, g, rscale = ln

    def body_ln(dh_ref, w_ref, r_ref, z_ref, g_ref, dzb_ref, dres_ref, dg_ref, db_ref, acc_sc):
        i, j = pl.program_id(0), pl.program_id(1)

        @pl.when(j == 0)
        def _():
            acc_sc[...] = r_ref[...]

        acc_sc[...] += _dot_nt(dh_ref[...], w_ref[0, 0])

        @pl.when(j == last)
        def _():
            _ln_bwd_store(acc_sc[...], z_ref, g_ref, rscale, i == 0, dzb_ref, dres_ref, dg_ref, db_ref)

    ln_in, ln_out, ln_shape = _ln_bwd_specs(S, tm, row)
    return pl.pallas_call(
        body_ln, name=name, grid=(S // tm, N_CHIPS), in_specs=in_specs + ln_in, out_specs=ln_out, out_shape=ln_shape,
        scratch_shapes=[pltpu.VMEM((tm, D_MODEL), f32)], compiler_params=_cp("arbitrary", "arbitrary"),
    )(dh, w13, res, z, g)


def mm_nt_res(name, dys, ws, wl, res, out_dtype, ln=None):
    S = dys[0].shape[0]
    K = ws[0].shape[1]
    tm = _tile(S, 512)
    n = len(dys)
    n_in = 2 * n + (res is not None)

    def product(refs):
        acc = _dot_nt(refs[0][...], refs[n][0])
        for k in range(1, n):
            acc = acc + _dot_nt(refs[k][...], refs[n + k][0])
        if res is not None:
            acc = acc + refs[2 * n][...]
        return acc

    def body(*refs):
        refs[-1][...] = product(refs).astype(out_dtype)

    def body_ln(*refs):
        z_ref, g_ref, dzb_ref, dres_ref, dg_ref, db_ref = refs[n_in:]
        _ln_bwd_store(product(refs), z_ref, g_ref, ln[2], pl.program_id(0) == 0, dzb_ref, dres_ref, dg_ref, db_ref)

    row = lambda i: (i, 0)
    in_specs = [pl.BlockSpec((tm, d.shape[1]), row) for d in dys]
    in_specs += [pl.BlockSpec((1,) + w.shape[1:], functools.partial(lambda li, i: (li, 0, 0), li)) for w, li in zip(ws, wl)]
    args = list(dys) + list(ws)
    if res is not None:
        in_specs.append(pl.BlockSpec((tm, K), row))
        args.append(res)
    if ln is None:
        return pl.pallas_call(
            body, name=name, grid=(S // tm,), in_specs=in_specs,
            out_specs=pl.BlockSpec((tm, K), row), out_shape=SDS((S, K), out_dtype),
            compiler_params=_cp("parallel"),
        )(*args)
    ln_in, ln_out, ln_shape = _ln_bwd_specs(S, tm, row)
    return pl.pallas_call(
        body_ln, name=name, grid=(S // tm,), in_specs=in_specs + ln_in, out_specs=ln_out, out_shape=ln_shape,
        compiler_params=_cp("arbitrary"),
    )(*args, ln[0], ln[1])


def mm_tn(name, x, dy, col_shards=False, deps=()):
    S, K = x.shape
    N = dy.shape[1]
    ts = 512
    while ts * 2 <= min(S, 2048) and S % (ts * 2) == 0 and ts * 2 * K * 2 <= 6 * 2**20:
        ts *= 2
    ts = _tile(S, ts)
    if col_shards:
        tn = N // N_CHIPS
    else:
        tn = N
        while K * tn * 4 > 6 * 2**20 and tn % 256 == 0:
            tn //= 2
    nn = N // tn
    lead = ((0,) if col_shards else ()) + (slice(None), slice(None))

    def body(x_ref, dy_ref, o_ref):
        acc = _dot_tn(x_ref[...].astype(bf16), dy_ref[...].astype(bf16))

        @pl.when(pl.program_id(1) == 0)
        def _():
            o_ref[lead] = acc

        @pl.when(pl.program_id(1) != 0)
        def _():
            o_ref[lead] += acc

    if col_shards:
        out_spec = pl.BlockSpec((1, K, tn), lambda n, s: (n, 0, 0))
        out_shape = SDS((N_CHIPS, K, tn), f32)
    else:
        out_spec = pl.BlockSpec((K, tn), lambda n, s: (0, n))
        out_shape = SDS((K, N), f32)
    return pl.pallas_call(
        _with_deps(body, 2, deps), name=name, grid=(nn, S // ts),
        in_specs=[pl.BlockSpec((ts, K), lambda n, s: (s, 0)), pl.BlockSpec((ts, tn), lambda n, s: (s, n))]
        + [_DEP_SPEC] * len(deps),
        out_specs=out_spec, out_shape=out_shape, compiler_params=_cp("parallel", "arbitrary"),
    )(x, dy, *deps)


def mm_nn_shard(name, x, w, l):
    S, K = x.shape
    ns = w.shape[3]

    def body(x_ref, w_ref, o_ref):
        o_ref[...] = _dot(x_ref[...], w_ref[0, 0]).astype(bf16)

    return pl.pallas_call(
        body, name=name, grid=(N_CHIPS,),
        in_specs=[pl.BlockSpec((S, K), lambda j: (0, 0)), pl.BlockSpec((1, 1, K, ns), lambda j: (l, j, 0, 0))],
        out_specs=pl.BlockSpec((S, ns), lambda j: (0, j)), out_shape=SDS((S, N_CHIPS * ns), bf16),
        compiler_params=_cp("parallel"),
    )(x, w)


def loss_grad(name, y, t, ln):
    S = y.shape[0]
    tm = _tile(S, 512)
    z, g, rscale = ln

    def body(y_ref, t_ref, z_ref, g_ref, dzb_ref, dres_ref, dg_ref, db_ref, loss_ref):
        first = pl.program_id(0) == 0
        e = y_ref[...] - t_ref[...]
        _ln_bwd_store(e * (1.0 / D_MODEL), z_ref, g_ref, rscale, first, dzb_ref, dres_ref, dg_ref, db_ref)

        @pl.when(first)
        def _():
            loss_ref[...] = jnp.zeros_like(loss_ref)

        loss_ref[...] += jnp.full(loss_ref.shape, (0.5 / D_MODEL) * jnp.sum(e * e), f32)

    row = lambda i: (i, 0)
    ln_in, ln_out, ln_shape = _ln_bwd_specs(S, tm, row)
    return pl.pallas_call(
        body, name=name, grid=(S // tm,),
        in_specs=[pl.BlockSpec((tm, D_MODEL), row)] * 2 + ln_in,
        out_specs=ln_out + [pl.BlockSpec((8, 128), lambda i: (0, 0))],
        out_shape=ln_shape + [SDS((8, 128), f32)],
        compiler_params=_cp("arbitrary"),
    )(y, t, z, g)


def _half_sum(t):
    return t + pltpu.roll(t, 64, axis=1)


def mix_pre(name, xb, w_in, wq, wkv, l, gq, gkv, cs):
    S = xb.shape[0]
    tm = _tile(S, 512)
    H = MLA_HEADS
    W_EXT = w_in.shape[2]

    def body(x_ref, win_ref, wq_ref, wkv_ref, gq_ref, gkv_ref, cs_ref,
             u_ref, cq_ref, ckv_ref, cqn_ref, ckvn_ref, q_ref, k_ref, v_ref):
        h = _dot(x_ref[...], win_ref[0])
        u_ref[...] = h[:, :256]
        cq = h[:, 256:512]
        ckv = h[:, 512:640]
        cq_ref[...] = cq
        ckv_ref[...] = ckv
        cqn = (cq * lax.rsqrt(jnp.mean(cq * cq, axis=-1, keepdims=True) + RMS_EPS) * gq_ref[...]).astype(bf16)
        ckvn = (ckv * lax.rsqrt(jnp.mean(ckv * ckv, axis=-1, keepdims=True) + RMS_EPS) * gkv_ref[...]).astype(bf16)
        cqn_ref[...] = cqn
        ckvn_ref[...] = ckvn
        csv = cs_ref[...]
        lane = lax.broadcasted_iota(jnp.int32, (tm, 128), 1)
        kr = jnp.where(lane < 64, _half_sum(h[:, 640:768] * csv), 0.0).astype(bf16)
        kv = _dot(ckvn, wkv_ref[0])
        for hd in range(H):
            qe = _dot(cqn, wq_ref[0, hd])
            q_ref[hd, :, :128] = (qe[:, :128] * MLA_SCALE_LOG2).astype(bf16)
            q_ref[hd, :, 128:] = (_half_sum(qe[:, 128:] * csv) * MLA_SCALE_LOG2).astype(bf16)
            k_ref[hd, :, :128] = kv[:, 256 * hd:256 * hd + 128].astype(bf16)
            k_ref[hd, :, 128:] = kr
            v_ref[hd] = kv[:, 256 * hd + 128:256 * hd + 256].astype(bf16)

    row = lambda i: (i, 0)
    hrow = lambda i: (0, i, 0)
    return pl.pallas_call(
        body, name=name, grid=(S // tm,),
        in_specs=[pl.BlockSpec((tm, D_MODEL), row),
                  pl.BlockSpec((1, D_MODEL, W_EXT), lambda i: (l, 0, 0)),
                  pl.BlockSpec((1, H, Q_LORA, 256), lambda i: (l, 0, 0, 0)),
                  pl.BlockSpec((1, KV_LORA, H * 256), lambda i: (l, 0, 0)),
                  pl.BlockSpec((1, Q_LORA), lambda i: (0, 0)), pl.BlockSpec((1, KV_LORA), lambda i: (0, 0)),
                  pl.BlockSpec((tm, 128), row)],
        out_specs=[pl.BlockSpec((tm, 256), row), pl.BlockSpec((tm, Q_LORA), row), pl.BlockSpec((tm, KV_LORA), row),
                   pl.BlockSpec((tm, Q_LORA), row), pl.BlockSpec((tm, KV_LORA), row),
                   pl.BlockSpec((H, tm, 256), hrow), pl.BlockSpec((H, tm, 256), hrow), pl.BlockSpec((H, tm, 128), hrow)],
        out_shape=[SDS((S, 256), f32), SDS((S, Q_LORA), f32), SDS((S, KV_LORA), f32),
                   SDS((S, Q_LORA), bf16), SDS((S, KV_LORA), bf16),
                   SDS((H, S, 256), bf16), SDS((H, S, 256), bf16), SDS((H, S, 128), bf16)],
        compiler_params=_cp("parallel"),
    )(xb, w_in, wq, wkv, gq, gkv, cs)


def _group_select(col, a2, a4, a8, a16):
    return jnp.where(col < 64, a2, jnp.where(col < 128, a4, jnp.where(col < 192, a8, a16)))


def pool_fwd(name, u, wbd, scale):
    S = u.shape[0]
    tm = _tile(S, 512)
    hb = tm // HALO

    def body(u_ref, halo_ref, w_ref, s_ref, d_ref, y_ref):
        i = pl.program_id(0)
        cur = u_ref[...]
        halo = jnp.where(i > 0, halo_ref[...], 0.0)
        ext = jnp.concatenate([halo, cur], axis=0)
        s2 = ext + pltpu.roll(ext, 1, axis=0)
        s4 = s2 + pltpu.roll(s2, 2, axis=0)
        s8 = s4 + pltpu.roll(s4, 4, axis=0)
        s16 = s8 + pltpu.roll(s8, 8, axis=0)
        t1 = (i * tm + 1 + lax.broadcasted_iota(jnp.int32, (tm, 1), 0)).astype(f32)
        col = lax.broadcasted_iota(jnp.int32, (tm, 256), 1)
        m = _group_select(col, s2[HALO:] / jnp.minimum(t1, 2.0), s4[HALO:] / jnp.minimum(t1, 4.0),
                          s8[HALO:] / jnp.minimum(t1, 8.0), s16[HALO:] / jnp.minimum(t1, 16.0))
        d = (m - cur).astype(bf16)
        d_ref[...] = d
        y_ref[...] = (_dot(d, w_ref[...]) * s_ref[...]).astype(bf16)

    row = lambda i: (i, 0)
    return pl.pallas_call(
        body, name=name, grid=(S // tm,),
        in_specs=[pl.BlockSpec((tm, 256), row), pl.BlockSpec((HALO, 256), lambda i: (jnp.maximum(i * hb - 1, 0), 0)),
                  pl.BlockSpec((256, 256), lambda i: (0, 0)), pl.BlockSpec((1, 256), lambda i: (0, 0))],
        out_specs=[pl.BlockSpec((tm, 256), row)] * 2,
        out_shape=[SDS((S, 256), bf16), SDS((S, 256), bf16)],
        compiler_params=_cp("parallel"),
    )(u, u, wbd, scale)


def pool_bwd(name, dyp, d, wbd, scale):
    S = dyp.shape[0]
    tm = _tile(S, 512)
    hb = tm // HALO
    n_ext = tm + HALO

    def fwd_sum(e, steps):
        k = 1
        for _ in range(steps):
            e = e + pltpu.roll(e, n_ext - k, axis=0)
            k *= 2
        return e

    def body(dy_ref, halo_ref, d_ref, w_ref, s_ref, du_ref, dyw_ref, ds_ref):
        i = pl.program_id(0)
        sc = s_ref[...]
        w = w_ref[...]
        cur = dy_ref[...].astype(f32)
        halo = jnp.where(i < pl.num_programs(0) - 1, halo_ref[...].astype(f32), 0.0)
        dyw = jnp.concatenate([cur, halo], axis=0) * sc
        dyw_ref[...] = dyw[:tm].astype(bf16)
        dd = _dot_nt(dyw.astype(bf16), w)
        t1 = (i * tm + 1 + lax.broadcasted_iota(jnp.int32, (n_ext, 1), 0)).astype(f32)
        f2 = fwd_sum(dd / jnp.minimum(t1, 2.0), 1)
        f4 = fwd_sum(dd / jnp.minimum(t1, 4.0), 2)
        f8 = fwd_sum(dd / jnp.minimum(t1, 8.0), 3)
        f16 = fwd_sum(dd / jnp.minimum(t1, 16.0), 4)
        col = lax.broadcasted_iota(jnp.int32, (tm, 256), 1)
        du_ref[...] = (_group_select(col, f2[:tm], f4[:tm], f8[:tm], f16[:tm]) - dd[:tm]).astype(bf16)

        @pl.when(i == 0)
        def _():
            ds_ref[...] = jnp.zeros_like(ds_ref)

        ds_ref[...] += jnp.sum(cur * _dot(d_ref[...], w), axis=0, keepdims=True)

    row = lambda i: (i, 0)
    nhb = S // HALO
    return pl.pallas_call(
        body, name=name, grid=(S // tm,),
        in_specs=[pl.BlockSpec((tm, 256), row), pl.BlockSpec((HALO, 256), lambda i: (jnp.minimum((i + 1) * hb, nhb - 1), 0)),
                  pl.BlockSpec((tm, 256), row), pl.BlockSpec((256, 256), lambda i: (0, 0)),
                  pl.BlockSpec((1, 256), lambda i: (0, 0))],
        out_specs=[pl.BlockSpec((tm, 256), row), pl.BlockSpec((tm, 256), row), pl.BlockSpec((1, 256), lambda i: (0, 0))],
        out_shape=[SDS((S, 256), bf16), SDS((S, 256), bf16), SDS((1, 256), f32)],
        compiler_params=_cp("arbitrary"),
    )(dyp, dyp, d, wbd, scale)


def _diag_mask(r0, rn, kn):
    rc = (r0 + lax.broadcasted_iota(jnp.int32, (rn, 1), 0)) // 64
    cc = lax.broadcasted_iota(jnp.int32, (1, kn), 1) // 64
    return rc >= cc


def _diag_parts(tq):
    h = tq // 2
    return [(0, h, h), (h, h, tq)] if h % 128 == 0 else [(0, tq, tq)]


def mla_attn_fwd(name, q, k, v):
    H, S, _ = q.shape
    tq = _tile(S, 1024)
    nq = S // tq
    pairs = [(i, j) for i in range(nq) for j in range(i + 1)]
    it = jnp.asarray([p_[0] for p_ in pairs], jnp.int32)
    jt = jnp.asarray([p_[1] for p_ in pairs], jnp.int32)

    def body(it_ref, jt_ref, q_ref, k_ref, v_ref, o_ref, lse_ref, m_sc, l_sc, acc_sc):
        t = pl.program_id(1)
        i, j = it_ref[t], jt_ref[t]

        @pl.when(j == 0)
        def _():
            m_sc[...] = jnp.full_like(m_sc, NEG_INF)
            l_sc[...] = jnp.zeros_like(l_sc)
            acc_sc[...] = jnp.zeros_like(acc_sc)

        def part(r0, rn, kn, masked):
            rows, keys = slice(r0, r0 + rn), slice(0, kn)
            s = _dot_nt(q_ref[0, rows, :], k_ref[0, keys, :])
            if masked:
                s = jnp.where(_diag_mask(r0, rn, kn), s, NEG_INF)
            m_prev = m_sc[rows, :]
            m_new = jnp.maximum(m_prev, jnp.max(s, axis=-1, keepdims=True))
            p = jnp.exp2(s - jnp.tile(m_new, (1, kn // 128)))
            a = jnp.exp2(m_prev - m_new)
            l_sc[rows, :] = a * l_sc[rows, :] + jnp.sum(p, axis=-1, keepdims=True)
            acc_sc[rows, :] = a * acc_sc[rows, :] + _dot(p.astype(bf16), v_ref[0, keys, :])
            m_sc[rows, :] = m_new

        @pl.when(j < i)
        def _():
            part(0, tq, tq, False)

        @pl.when(j == i)
        def _():
            for r0, rn, kn in _diag_parts(tq):
                part(r0, rn, kn, True)
            o_ref[...] = (acc_sc[...] / l_sc[...]).astype(bf16)
            lse_ref[0] = m_sc[...] + jnp.log2(l_sc[...])

    return pl.pallas_call(
        body, name=name,
        grid_spec=pltpu.PrefetchScalarGridSpec(
            num_scalar_prefetch=2, grid=(H, len(pairs)),
            in_specs=[pl.BlockSpec((1, tq, 256), lambda h, t, it_, jt_: (h, it_[t], 0)),
                      pl.BlockSpec((1, tq, 256), lambda h, t, it_, jt_: (h, jt_[t], 0)),
                      pl.BlockSpec((1, tq, 128), lambda h, t, it_, jt_: (h, jt_[t], 0))],
            out_specs=[pl.BlockSpec((tq, 128), lambda h, t, it_, jt_: (it_[t], h)),
                       pl.BlockSpec((1, tq, 128), lambda h, t, it_, jt_: (h, it_[t], 0))],
            scratch_shapes=[pltpu.VMEM((tq, 128), f32), pltpu.VMEM((tq, 128), f32), pltpu.VMEM((tq, 128), f32)]),
        out_shape=[SDS((S, H * 128), bf16), SDS((H, S, 128), f32)],
        compiler_params=_cp("parallel", "arbitrary"),
    )(it, jt, q, k, v)


def mla_attn_bwd(name, q, k, v, o, do, lse, do_col=0):
    H, S, _ = q.shape
    tq = _tile(S, 1024)
    nq = S // tq
    pairs = [(i, j) for j in range(nq) for i in range(j, nq)]
    it = jnp.asarray([p_[0] for p_ in pairs], jnp.int32)
    jt = jnp.asarray([p_[1] for p_ in pairs], jnp.int32)
    n_pairs = len(pairs)

    def body(it_ref, jt_ref, q_ref, k_ref, v_ref, o_ref, do_ref, lse_ref, dq_ref, dk_ref, dv_ref, dq_sc, dk_sc, dv_sc):
        t = pl.program_id(1)
        i, j = it_ref[t], jt_ref[t]

        @pl.when(t == 0)
        def _():
            dq_sc[...] = jnp.zeros_like(dq_sc)

        @pl.when(i == j)
        def _():
            dk_sc[...] = jnp.zeros_like(dk_sc)
            dv_sc[...] = jnp.zeros_like(dv_sc)

        def part(r0, rn, kn, masked):
            rows, keys = slice(r0, r0 + rn), slice(0, kn)
            qv, kv_, dov = q_ref[0, rows, :], k_ref[0, keys, :], do_ref[rows, :]
            s = _dot_nt(qv, kv_)
            if masked:
                s = jnp.where(_diag_mask(r0, rn, kn), s, NEG_INF)
            p = jnp.exp2(s - jnp.tile(lse_ref[0, rows, :], (1, kn // 128)))
            dv_sc[keys, :] += _dot_tn(p.astype(bf16), dov)
            dp = _dot_nt(dov, v_ref[0, keys, :])
            delta = jnp.sum(dov.astype(f32) * o_ref[rows, :].astype(f32), axis=-1, keepdims=True)
            ds = (p * (dp - delta)).astype(bf16)
            dk_sc[keys, :] += _dot_tn(ds, qv)
            dq_rows = pl.ds(pl.multiple_of(i * tq + r0, 128), rn)
            dq_sc[dq_rows, :] += _dot(ds, kv_)

        @pl.when(i > j)
        def _():
            part(0, tq, tq, False)

        @pl.when(i == j)
        def _():
            for r0, rn, kn in _diag_parts(tq):
                part(r0, rn, kn, True)

        @pl.when(i == nq - 1)
        def _():
            dk_ref[0] = (dk_sc[...] * math.log(2.0)).astype(bf16)
            dv_ref[0] = dv_sc[...].astype(bf16)

        @pl.when(t == n_pairs - 1)
        def _():
            dq_ref[0] = (dq_sc[...] * MLA_SCALE).astype(bf16)

    qi = lambda h, t, it_, jt_: (h, it_[t], 0)
    kj = lambda h, t, it_, jt_: (h, jt_[t], 0)
    oi = lambda h, t, it_, jt_: (it_[t], h)
    doi = lambda h, t, it_, jt_: (it_[t], h + do_col)
    return pl.pallas_call(
        body, name=name,
        grid_spec=pltpu.PrefetchScalarGridSpec(
            num_scalar_prefetch=2, grid=(H, n_pairs),
            in_specs=[pl.BlockSpec((1, tq, 256), qi), pl.BlockSpec((1, tq, 256), kj), pl.BlockSpec((1, tq, 128), kj),
                      pl.BlockSpec((tq, 128), oi), pl.BlockSpec((tq, 128), doi), pl.BlockSpec((1, tq, 128), qi)],
            out_specs=[pl.BlockSpec((1, S, 256), lambda h, t, it_, jt_: (h, 0, 0)), pl.BlockSpec((1, tq, 256), kj),
                       pl.BlockSpec((1, tq, 128), kj)],
            scratch_shapes=[pltpu.VMEM((S, 256), f32), pltpu.VMEM((tq, 256), f32), pltpu.VMEM((tq, 128), f32)]),
        out_shape=[SDS((H, S, 256), bf16), SDS((H, S, 256), bf16), SDS((H, S, 128), bf16)],
        compiler_params=_cp("parallel", "arbitrary"),
    )(it, jt, q, k, v, o, do, lse)


def mix_post_bwd(name, dq, dk, dv, wq, wkv, l, cq, ckv, gq, gkv, cs):
    H, S, _ = dq.shape
    tm = _tile(S, 512)

    def rms_bwd(dyn, c, g):
        r = lax.rsqrt(jnp.mean(c * c, axis=-1, keepdims=True) + RMS_EPS)
        ch = c * r
        dyg = dyn * g
        dc = r * (dyg - ch * jnp.mean(dyg * ch, axis=-1, keepdims=True))
        return dc, jnp.sum(dyn * ch, axis=0, keepdims=True)

    def body(dq_ref, dk_ref, dv_ref, wq_ref, wkv_ref, cq_ref, ckv_ref, gq_ref, gkv_ref, cs_ref,
             dqe_ref, dkv_ref, dh_ref, dgq_ref, dgkv_ref):
        csv = cs_ref[...]
        lane = lax.broadcasted_iota(jnp.int32, (tm, 128), 1)
        dcqn = jnp.zeros((tm, Q_LORA), f32)
        dkr = jnp.zeros((tm, 128), f32)
        for hd in range(H):
            dqh = dq_ref[hd].astype(f32)
            dqe = jnp.concatenate([dqh[:, :128], _half_sum(dqh[:, 128:]) * csv], axis=1).astype(bf16)
            dqe_ref[:, 256 * hd:256 * hd + 256] = dqe
            dcqn = dcqn + _dot_nt(dqe, wq_ref[0, hd])
            dkh = dk_ref[hd].astype(f32)
            dkv_ref[:, 256 * hd:256 * hd + 128] = dkh[:, :128].astype(bf16)
            dkv_ref[:, 256 * hd + 128:256 * hd + 256] = dv_ref[hd].astype(bf16)
            dkr = dkr + dkh[:, 128:]
        dckvn = _dot_nt(dkv_ref[...], wkv_ref[0])
        dblk = _half_sum(jnp.where(lane < 64, dkr, 0.0)) * csv
        dcq, dgq = rms_bwd(dcqn, cq_ref[...], gq_ref[...])
        dckv, dgkv = rms_bwd(dckvn, ckv_ref[...], gkv_ref[...])
        dh_ref[:, :256] = dcq.astype(bf16)
        dh_ref[:, 256:384] = dckv.astype(bf16)
        dh_ref[:, 384:] = dblk.astype(bf16)

        @pl.when(pl.program_id(0) == 0)
        def _():
            dgq_ref[...] = jnp.zeros_like(dgq_ref)
            dgkv_ref[...] = jnp.zeros_like(dgkv_ref)

        dgq_ref[...] += dgq
        dgkv_ref[...] += dgkv

    row = lambda i: (i, 0)
    hrow = lambda i: (0, i, 0)
    return pl.pallas_call(
        body, name=name, grid=(S // tm,),
        in_specs=[pl.BlockSpec((H, tm, 256), hrow), pl.BlockSpec((H, tm, 256), hrow), pl.BlockSpec((H, tm, 128), hrow),
                  pl.BlockSpec((1, H, Q_LORA, 256), lambda i: (l, 0, 0, 0)),
                  pl.BlockSpec((1, KV_LORA, H * 256), lambda i: (l, 0, 0)),
                  pl.BlockSpec((tm, Q_LORA), row), pl.BlockSpec((tm, KV_LORA), row),
                  pl.BlockSpec((1, Q_LORA), lambda i: (0, 0)), pl.BlockSpec((1, KV_LORA), lambda i: (0, 0)),
                  pl.BlockSpec((tm, 128), row)],
        out_specs=[pl.BlockSpec((tm, H * 256), row), pl.BlockSpec((tm, H * 256), row), pl.BlockSpec((tm, 512), row),
                   pl.BlockSpec((1, Q_LORA), lambda i: (0, 0)), pl.BlockSpec((1, KV_LORA), lambda i: (0, 0))],
        out_shape=[SDS((S, H * 256), bf16), SDS((S, H * 256), bf16), SDS((S, 512), bf16),
                   SDS((1, Q_LORA), f32), SDS((1, KV_LORA), f32)],
        compiler_params=_cp("arbitrary"),
    )(dq, dk, dv, wq, wkv, cq, ckv, gq, gkv, cs)


def _cross_probs(qb, kv_ref, hd):
    cols = slice(hd * MEM_HEAD_DIM, (hd + 1) * MEM_HEAD_DIM)
    s = _dot_nt(qb[:, cols], kv_ref[:, cols]) * MEM_SCALE
    e = jnp.exp(s - jnp.max(s, axis=-1, keepdims=True))
    return e / jnp.sum(e, axis=-1, keepdims=True)


def cross_fwd(name, xb, xf, wq, wo, l, kv, g, b):
    S = xb.shape[0]
    tm = _tile(S, 512)
    M = kv.shape[0]

    def body(x_ref, xf_ref, wq_ref, wo_ref, k_ref, v_ref, g_ref, b_ref, q_ref, o_ref, z_ref, y_ref, yb_ref):
        qb = _dot(x_ref[...], wq_ref[0]).astype(bf16)
        q_ref[...] = qb
        for hd in range(MEM_HEADS):
            cols = slice(hd * MEM_HEAD_DIM, (hd + 1) * MEM_HEAD_DIM)
            p = _cross_probs(qb, k_ref, hd)
            o_ref[:, cols] = _dot(p.astype(bf16), v_ref[:, cols]).astype(bf16)
        z = ALPHA * xf_ref[...] + _dot(o_ref[...], wo_ref[0])
        mu = jnp.mean(z, axis=-1, keepdims=True)
        zc = z - mu
        var = jnp.mean(zc * zc, axis=-1, keepdims=True)
        y = zc * lax.rsqrt(var + LN_EPS) * g_ref[...] + b_ref[...]
        z_ref[...] = z
        y_ref[...] = y
        yb_ref[...] = y.astype(bf16)

    row = lambda i: (i, 0)
    wspec = pl.BlockSpec((1, D_MODEL, D_MODEL), lambda i: (l, 0, 0))
    vec = pl.BlockSpec((1, D_MODEL), lambda i: (0, 0))
    blk = pl.BlockSpec((tm, D_MODEL), row)
    return pl.pallas_call(
        body, name=name, grid=(S // tm,),
        in_specs=[blk, blk, wspec, wspec, pl.BlockSpec((M, D_MODEL), lambda i: (0, 0)),
                  pl.BlockSpec((M, D_MODEL), lambda i: (0, 1)), vec, vec],
        out_specs=[blk] * 5,
        out_shape=[SDS((S, D_MODEL), bf16), SDS((S, D_MODEL), bf16), SDS((S, D_MODEL), f32), SDS((S, D_MODEL), f32),
                   SDS((S, D_MODEL), bf16)],
        compiler_params=_cp("parallel"),
    )(xb, xf, wq, wo, kv, kv, g, b)


def cross_bwd(name, dzb, wo, l, qb, kv, deps=()):
    S = dzb.shape[0]
    tm = _tile(S, 512)
    M = kv.shape[0]

    def body(dz_ref, wo_ref, q_ref, k_ref, v_ref, dq_ref, dkv_ref):
        @pl.when(pl.program_id(0) == 0)
        def _():
            dkv_ref[...] = jnp.zeros_like(dkv_ref)

        do = _dot_nt(dz_ref[...], wo_ref[0]).astype(bf16)
        qv = q_ref[...]
        for hd in range(MEM_HEADS):
            cols = slice(hd * MEM_HEAD_DIM, (hd + 1) * MEM_HEAD_DIM)
            vcols = slice(D_MODEL + hd * MEM_HEAD_DIM, D_MODEL + (hd + 1) * MEM_HEAD_DIM)
            p = _cross_probs(qv, k_ref, hd)
            doh = do[:, cols]
            dkv_ref[:, vcols] += _dot_tn(p.astype(bf16), doh)
            dp = _dot_nt(doh, v_ref[:, cols])
            ds = (p * (dp - jnp.sum(dp * p, axis=-1, keepdims=True)) * MEM_SCALE).astype(bf16)
            dq_ref[:, cols] = _dot(ds, k_ref[:, cols]).astype(bf16)
            dkv_ref[:, cols] += _dot_tn(ds, qv[:, cols])

    row = lambda i: (i, 0)
    blk = pl.BlockSpec((tm, D_MODEL), row)
    return pl.pallas_call(
        _with_deps(body, 5, deps), name=name, grid=(S // tm,),
        in_specs=[blk, pl.BlockSpec((1, D_MODEL, D_MODEL), lambda i: (l, 0, 0)), blk,
                  pl.BlockSpec((M, D_MODEL), lambda i: (0, 0)), pl.BlockSpec((M, D_MODEL), lambda i: (0, 1))]
        + [_DEP_SPEC] * len(deps),
        out_specs=[blk, pl.BlockSpec((M, 2 * D_MODEL), lambda i: (0, 0))],
        out_shape=[SDS((S, D_MODEL), bf16), SDS((M, 2 * D_MODEL), f32)],
        compiler_params=_cp("arbitrary"),
    )(dzb, wo, qb, kv, kv, *deps)


def adamw(name, w, g, m, v, deps=()):
    shape = w.shape
    cols = shape[-1]
    rows = math.prod(shape[:-1])
    tr = _row_tile(rows, cols, target=2 * 2**20)
    c1 = 1.0 - ADAM_B1 ** ADAM_STEP
    c2 = 1.0 - ADAM_B2 ** ADAM_STEP

    def body(w_ref, g_ref, m_ref, v_ref, d_ref, nm_ref, nv_ref):
        gv = g_ref[...]
        nm = ADAM_B1 * m_ref[...] + (1.0 - ADAM_B1) * gv
        nv = ADAM_B2 * v_ref[...] + (1.0 - ADAM_B2) * (gv * gv)
        d_ref[...] = -ADAM_LR * ((nm / c1) / (jnp.sqrt(nv / c2) + ADAM_EPS) + ADAM_WD * w_ref[...])
        nm_ref[...] = nm
        nv_ref[...] = nv

    blk = pl.BlockSpec((tr, cols), lambda i: (i, 0))
    flat = SDS((rows, cols), f32)
    outs = pl.pallas_call(
        _with_deps(body, 4, deps), name=name, grid=(rows // tr,), in_specs=[blk] * 4 + [_DEP_SPEC] * len(deps),
        out_specs=[blk] * 3, out_shape=[flat] * 3, compiler_params=_cp("parallel"),
    )(*[a.reshape(rows, cols) for a in (w, g, m, v)], *deps)
    return [o.reshape(shape) for o in outs]


def _me():
    return lax.axis_index("x"), lax.axis_index("y"), lax.axis_index("c")


def _other_chips(x, y):
    return [(1 - x, y), (x, 1 - y), (1 - x, 1 - y)]


def _pair_share_each(owners, bufs, sems, mine, act):
    x, y, c = _me()
    for o in range(2):
        slots = [(a, lyr) for a in range(len(bufs)) for lyr in range(DEPTH) if owners[a][lyr] == o]

        @pl.when((c == o) if mine else (c != o))
        def _(slots=slots):
            for a, lyr in slots:
                slot = bufs[a].at[lyr]
                act(_rcopy(slot, slot, sems[0].at[2 * a + lyr], sems[1].at[2 * a + lyr], (x, y, 1 - c)))


def pair_share_start(name, sums, owners, after):
    def body_fn(b_in, s_in, s_out):
        _pair_share_each(owners, b_in, s_out, True, lambda cp: cp.start())

    outs, sems, token = _split_call(name, body_fn, list(sums), [], [2 * len(sums)] * 2, after)
    return (outs, sems[0], sems[1], owners), token


def pair_share_wait(name, st, after):
    bufs, send, recv, owners = st

    def body_fn(b_in, s_in, s_out):
        _pair_share_each(owners, b_in, s_in, True, lambda cp: cp.wait_send())
        _pair_share_each(owners, b_in, s_in, False, lambda cp: cp.wait_recv())

    outs, _, _ = _split_call(name, body_fn, list(bufs), [send, recv], [], after)
    return outs


def allsum_small(name, v, deps=()):
    R = v.shape[0]

    def body(v_ref, o_ref, all_ref, send_sems, recv_sems, local_sem):
        x, y, c = _me()
        me, sibling = (x, y, c), (x, y, 1 - c)
        chips = _other_chips(x, y)

        def rows(px, py, pc):
            return all_ref.at[4 * px + 2 * py + pc]

        def copy(k, block, to, src=None):
            return pltpu.make_async_remote_copy(
                src_ref=rows(*block) if src is None else src, dst_ref=rows(*block),
                send_sem=send_sems.at[k], recv_sem=recv_sems.at[k], device_id=to, device_id_type=MESH)

        mine = pltpu.make_async_copy(v_ref, rows(*me), local_sem)
        mine.start()
        first = [copy(0, me, sibling, src=v_ref)]
        first += [copy(1 + j, me, (*chip, c), src=v_ref) for j, chip in enumerate(chips)]
        for cp in first:
            cp.start()
        passed = [copy(4 + j, (*chip, c), sibling) for j, chip in enumerate(chips)]
        for j, chip in enumerate(chips):
            copy(1 + j, (*chip, c), me).wait_recv()
            passed[j].start()
        copy(0, sibling, me).wait_recv()
        for j, chip in enumerate(chips):
            copy(4 + j, (*chip, 1 - c), me).wait_recv()
        for cp in first + passed:
            cp.wait_send()
        mine.wait()
        acc = all_ref[0]
        for d in range(1, 8):
            acc = acc + all_ref[d]
        o_ref[...] = acc

    return pl.pallas_call(
        _with_deps(body, 1, deps), name=name,
        in_specs=[pl.BlockSpec(memory_space=pltpu.VMEM)] + [_DEP_SPEC] * len(deps),
        out_specs=pl.BlockSpec(memory_space=pltpu.VMEM),
        out_shape=SDS((R, 128), f32),
        scratch_shapes=[pltpu.VMEM((8, R, 128), f32), pltpu.SemaphoreType.DMA((7,)), pltpu.SemaphoreType.DMA((7,)),
                        pltpu.SemaphoreType.DMA],
        compiler_params=pltpu.CompilerParams(vmem_limit_bytes=V7X_VMEM_LIMIT),
    )(v, *deps)


def _swap_half(r):
    return jnp.concatenate([-r[..., 32:], r[..., :32]], axis=-1)


def _unswap_add(p, qg):
    return p + jnp.concatenate([qg[..., 32:], -qg[..., :32]], axis=-1)


def _block_diag(pw):
    L = pw.shape[0]
    out = jnp.zeros((L, 256, 256), pw.dtype)
    for gi in range(4):
        out = out.at[:, 64 * gi:64 * gi + 64, 64 * gi:64 * gi + 64].set(pw[:, gi])
    return out


def _to_col_shards(w):
    *lead, K, N = w.shape
    nl = len(lead)
    return w.reshape(*lead, K, N_CHIPS, N // N_CHIPS).transpose(*range(nl), nl + 1, nl, nl + 2)


def _from_col_shards(w):
    *lead, C, K, n = w.shape
    nl = len(lead)
    return w.transpose(*range(nl), nl + 1, nl, nl + 2).reshape(*lead, K, C * n)


_HBM_SPEC = pl.BlockSpec(memory_space=pltpu.HBM)
_SEM_SPEC = pl.BlockSpec(memory_space=pltpu.SEMAPHORE)
_ANY_SPEC = pl.BlockSpec(memory_space=pl.ANY)
_DATAFLOW = pltpu.SideEffectType.DATAFLOW_SIDE_EFFECTING


def _split_call(name, body_fn, bufs, sems_in, sems_out_sizes, after):
    nb, ni, no = len(bufs), len(sems_in), len(sems_out_sizes)
    afters = () if after is None else tuple(after) if isinstance(after, (tuple, list)) else (after,)

    def body(*refs):
        k = nb + ni + len(afters)
        body_fn(refs[:nb], refs[nb:nb + ni], refs[k:k + no])
        refs[-1][...] = jnp.zeros((8, 128), f32)

    outs = pl.pallas_call(
        body, name=name,
        in_specs=[_HBM_SPEC] * nb + [_SEM_SPEC] * ni + [_ANY_SPEC] * len(afters),
        out_specs=[_SEM_SPEC] * no + [_HBM_SPEC] * nb + [pl.BlockSpec(memory_space=pltpu.VMEM)],
        out_shape=[pltpu.SemaphoreType.DMA((s,)) for s in sems_out_sizes]
        + [pltpu.HBM(b.shape, b.dtype) for b in bufs] + [SDS((8, 128), f32)],
        input_output_aliases={i: no + i for i in range(nb)},
        compiler_params=pltpu.CompilerParams(has_side_effects=_DATAFLOW),
    )(*[pltpu.with_memory_space_constraint(b, pltpu.HBM) for b in bufs], *sems_in, *afters)
    return list(outs[no:no + nb]), list(outs[:no]), outs[-1]


def _rcopy(src, dst, ssem, rsem, to):
    return pltpu.make_async_remote_copy(src_ref=src, dst_ref=dst, send_sem=ssem, recv_sem=rsem, device_id=to,
                                        device_id_type=MESH)


def gather_start(name, groups, after):
    flat = [b for bufs, _ in groups for b in bufs]
    sizes = [3 * len(bufs) for bufs, _ in groups for _ in range(2)]

    def body_fn(b_in, s_in, s_out):
        x, y, c = _me()
        q = 2 * x + y
        chips = _other_chips(x, y)
        pos = 0
        for gi, (bufs, owner) in enumerate(groups):
            refs = b_in[pos:pos + len(bufs)]
            pos += len(bufs)

            @pl.when(c == owner)
            def _(refs=refs, send=s_out[2 * gi], recv=s_out[2 * gi + 1]):
                for a, r in enumerate(refs):
                    for k, (cx, cy) in enumerate(chips):
                        _rcopy(r.at[q], r.at[q], send.at[3 * a + k], recv.at[3 * a + k], (cx, cy, c)).start()

    outs, sems, token = _split_call(name, body_fn, flat, [], sizes, after)
    res, pos = [], 0
    for gi, (bufs, owner) in enumerate(groups):
        res.append((outs[pos:pos + len(bufs)], sems[2 * gi], sems[2 * gi + 1], owner))
        pos += len(bufs)
    return res, token


def gather_forward(name, grp, after):
    bufs, send, recv, owner = grp
    n3 = 3 * len(bufs)

    def body_fn(b_in, s_in, s_out):
        x, y, c = _me()
        q = 2 * x + y
        sibling = (x, y, 1 - c)
        chips = _other_chips(x, y)

        @pl.when(c == owner)
        def _():
            for a, r in enumerate(b_in):
                for k, (cx, cy) in enumerate(chips):
                    i = 3 * a + k
                    land = r.at[2 * cx + cy]
                    _rcopy(r.at[q], r.at[q], s_in[0].at[i], s_in[1].at[i], (cx, cy, c)).wait_send()
                    _rcopy(land, land, s_in[0].at[i], s_in[1].at[i], (cx, cy, c)).wait_recv()
                    _rcopy(land, land, s_out[0].at[i], s_out[1].at[i], sibling).start()

    outs, sems, token = _split_call(name, body_fn, bufs, [send, recv], [n3, n3], after)
    return (outs, sems[0], sems[1], owner), token


def gather_finish(name, grp, after):
    bufs, fsend, frecv, owner = grp

    def body_fn(b_in, s_in, s_out):
        x, y, c = _me()
        sibling = (x, y, 1 - c)
        chips = _other_chips(x, y)

        def each(wait):
            for a, r in enumerate(b_in):
                for k, (cx, cy) in enumerate(chips):
                    land = r.at[2 * cx + cy]
                    wait(_rcopy(land, land, s_in[0].at[3 * a + k], s_in[1].at[3 * a + k], sibling))

        @pl.when(c == owner)
        def _():
            each(lambda cp: cp.wait_send())

        @pl.when(c != owner)
        def _():
            each(lambda cp: cp.wait_recv())

    outs, _, _ = _split_call(name, body_fn, bufs, [fsend, frecv], [], after)
    return outs


def _by_owner(owners):
    return [[a for a, o_ in enumerate(owners) if o_ == o] for o in range(2)]


def pair_send_start(name, gs, owners, after):
    n = len(gs)
    lands = [lax.empty(g.shape, g.dtype) for g in gs]

    def body_fn(b_in, s_in, s_out):
        x, y, c = _me()
        for o, idx in enumerate(_by_owner(owners)):
            @pl.when(c == 1 - o)
            def _(o=o, idx=idx):
                for a in idx:
                    _rcopy(b_in[a], b_in[n + a], s_out[0].at[a], s_out[1].at[a], (x, y, o)).start()

    outs, sems, token = _split_call(name, body_fn, list(gs) + lands, [], [n, n], after)
    return (outs[:n], outs[n:], sems[0], sems[1], owners), token


def pair_send_wait(name, st, after):
    gs, lands, send, recv, owners = st
    n = len(gs)

    def body_fn(b_in, s_in, s_out):
        x, y, c = _me()
        for o, idx in enumerate(_by_owner(owners)):
            @pl.when(c == 1 - o)
            def _(o=o, idx=idx):
                for a in idx:
                    _rcopy(b_in[a], b_in[n + a], s_in[0].at[a], s_in[1].at[a], (x, y, o)).wait_send()

            @pl.when(c == o)
            def _(o=o, idx=idx):
                for a in idx:
                    _rcopy(b_in[a], b_in[n + a], s_in[0].at[a], s_in[1].at[a], (x, y, 1 - o)).wait_recv()

    outs, _, _ = _split_call(name, body_fn, list(gs) + list(lands), [send, recv], [], after)
    return outs[:n], outs[n:]


def chip_exchange_start(name, psums, owners, after):
    n = len(psums)
    lands = [lax.empty((3,) + p.shape[1:], p.dtype) for p in psums]

    def body_fn(b_in, s_in, s_out):
        x, y, c = _me()
        chips = _other_chips(x, y)
        for o, idx in enumerate(_by_owner(owners)):
            @pl.when(c == o)
            def _(idx=idx):
                for a in idx:
                    for k, (cx, cy) in enumerate(chips):
                        _rcopy(b_in[a].at[2 * cx + cy], b_in[n + a].at[k], s_out[0].at[3 * a + k],
                               s_out[1].at[3 * a + k], (cx, cy, c)).start()

    outs, sems, token = _split_call(name, body_fn, list(psums) + lands, [], [3 * n, 3 * n], after)
    return (outs[:n], outs[n:], sems[0], sems[1], owners), token


def chip_exchange_wait(name, st, after):
    psums, lands, send, recv, owners = st
    n = len(psums)

    def body_fn(b_in, s_in, s_out):
        x, y, c = _me()
        chips = _other_chips(x, y)
        for o, idx in enumerate(_by_owner(owners)):
            @pl.when(c == o)
            def _(idx=idx):
                for a in idx:
                    for k, (cx, cy) in enumerate(chips):
                        cp = _rcopy(b_in[a].at[2 * cx + cy], b_in[n + a].at[k], s_in[0].at[3 * a + k],
                                    s_in[1].at[3 * a + k], (cx, cy, c))
                        cp.wait_send()
                        cp.wait_recv()

    outs, _, _ = _split_call(name, body_fn, list(psums) + list(lands), [send, recv], [], after)
    return outs[:n], outs[n:]


def pair_sum(name, g, recv, flag):
    shape = g.shape
    cols = shape[-1]
    rows = math.prod(shape[:-1])
    tr = _row_tile(rows, cols, target=4 * 2**20)

    def body(f_ref, g_ref, r_ref, o_ref):
        o_ref[...] = (g_ref[...] + r_ref[...]).astype(bf16)

    blk = pl.BlockSpec((tr, cols), lambda i, f_ref: (i * f_ref[0], 0))
    out = pl.pallas_call(
        body, name=name,
        grid_spec=pltpu.PrefetchScalarGridSpec(num_scalar_prefetch=1, grid=(rows // tr,), in_specs=[blk, blk],
                                               out_specs=blk),
        out_shape=SDS((rows, cols), bf16), compiler_params=_cp("arbitrary"),
    )(flag, g.reshape(rows, cols), recv.reshape(rows, cols))
    return out.reshape(shape)


def chip_sum(name, psum, recv, qf_arr, layer, prev):
    shard = psum.shape[1:]
    cols = shard[-1]
    rows = math.prod(shard[:-1])
    tr = _row_tile(rows, cols, target=4 * 2**20)

    def body(qf_ref, p_ref, r_ref, *rest):
        rest[-1][0] = ((p_ref[0].astype(f32) + r_ref[0].astype(f32)) + r_ref[1].astype(f32)) + r_ref[2].astype(f32)

    in_specs = [pl.BlockSpec((1, tr, cols), lambda i, qf: (qf[0], i * qf[1], 0)),
                pl.BlockSpec((3, tr, cols), lambda i, qf: (0, i * qf[1], 0))]
    args = [qf_arr, psum.reshape(N_CHIPS, rows, cols), recv.reshape(3, rows, cols)]
    aliases = {}
    if prev is not None:
        in_specs.append(pl.BlockSpec(memory_space=pl.ANY))
        args.append(prev.reshape(DEPTH, rows, cols))
        aliases = {3: 0}
    out = pl.pallas_call(
        body, name=name,
        grid_spec=pltpu.PrefetchScalarGridSpec(
            num_scalar_prefetch=1, grid=(rows // tr,), in_specs=in_specs,
            out_specs=pl.BlockSpec((1, tr, cols), lambda i, qf: (layer, i * qf[1], 0))),
        out_shape=SDS((DEPTH, rows, cols), f32), input_output_aliases=aliases, compiler_params=_cp("arbitrary"),
    )(*args)
    return out.reshape((DEPTH,) + shard)


W_NAMES = ("f1w13", "f1w2", "win", "wuq", "wukv", "wout", "mwq", "mwkv", "mwo", "f2w13", "f2w2")
MIX_NAMES = ("win", "wuq", "wukv")
MID_NAMES = ("wout", "mwq", "mwkv", "mwo")
FFN2_NAMES = ("f2w13", "f2w2")
REDUCER = (dict(f1w13=0, f1w2=1, f2w13=0, win=0, wuq=0, wukv=0, f2w2=1, mwkv=1, wout=1, mwq=1, mwo=1),
           dict(f1w13=0, f2w2=0, mwkv=0, wout=0, f2w13=1, f1w2=1, mwq=1, mwo=1, win=1, wuq=1, wukv=1))


def kernel(x, mem, positions, ln_g, ln_b, ffn1_w13, ffn1_w2, w_in, pool_w, pool_scale, q_norm_g, w_uq, kv_norm_g, w_ukv, w_out, mem_wq, mem_wkv, mem_wo, ffn2_w13, ffn2_w2, loss_target, m_ln_g, m_ln_b, m_ffn1_w13, m_ffn1_w2, m_w_in, m_pool_w, m_pool_scale, m_q_norm_g, m_w_uq, m_kv_norm_g, m_w_ukv, m_w_out, m_mem_wq, m_mem_wkv, m_mem_wo, m_ffn2_w13, m_ffn2_w2, v_ln_g, v_ln_b, v_ffn1_w13, v_ffn1_w2, v_w_in, v_pool_w, v_pool_scale, v_q_norm_g, v_w_uq, v_kv_norm_g, v_w_ukv, v_w_out, v_mem_wq, v_mem_wkv, v_mem_wo, v_ffn2_w13, v_ffn2_w2):
    L = DEPTH
    qx, qy, _ = _me()
    chip = 2 * qx + qy
    vec = lambda a: a.reshape(1, -1)

    shards = dict(zip(W_NAMES, (ffn1_w13, ffn1_w2, w_in, w_uq, w_ukv, w_out, mem_wq, mem_wkv, mem_wo, ffn2_w13, ffn2_w2)))

    def place(sh, slot):
        return lax.dynamic_update_slice(lax.empty((N_CHIPS,) + sh.shape, bf16), sh.astype(bf16)[None],
                                        (slot,) + (0,) * sh.ndim)

    first = ("f1w13", "f1w2")
    bufs = [dict(), dict()]
    for n in first:
        bufs[0][n] = place(shards[n][0], chip)
    gw = [dict(), dict()]

    ln_pad = jnp.zeros((2, L, 4, N_CHIPS, D_MODEL // N_CHIPS), f32)
    ln_pad = lax.dynamic_update_slice(ln_pad, jnp.stack([ln_g, ln_b])[:, :, :, None, :], (0, 0, 0, chip, 0))
    ln_sum = allsum_small("allsum_ln", ln_pad.reshape(-1, 128))
    ln_full = (ln_sum * 0.5).reshape(2, L, 4, D_MODEL)
    lng, lnb = ln_full[0], ln_full[1]

    (g0, g_w2), tok = gather_start("gather_a_start", [([bufs[0]["f1w13"]], 0), ([bufs[0]["f1w2"]], 0)], ln_sum)
    chip_then = chip + tok[0, 0].astype(jnp.int32)
    for l in range(L):
        for n in W_NAMES:
            if n not in bufs[l]:
                bufs[l][n] = place(shards[n][l], chip_then)
    others = tuple(bufs[l][n] for l in range(L) for n in W_NAMES if (l, n) not in ((0, first[0]), (0, first[1])))
    g0, tok = gather_forward("gather_a_forward", g0, others)
    (gw[0]["f1w13"],) = gather_finish("gather_a_finish", g0, None)
    l1_early = ("f1w13", "f1w2") + MIX_NAMES
    l1_late = MID_NAMES + FFN2_NAMES
    (g_mix, g_mid, g_ffn2, g_l1a, g_l1b), tok_b = gather_start(
        "gather_b_start",
        [([bufs[0][n] for n in MIX_NAMES], 0), ([bufs[0][n] for n in MID_NAMES], 0), ([bufs[0][n] for n in FFN2_NAMES], 0),
         ([bufs[1][n] for n in l1_early], 1), ([bufs[1][n] for n in l1_late], 1)], tok)

    half = QK_ROPE // 2
    inv_freq = ROPE_BASE ** (-jnp.arange(half, dtype=f32) / half)
    ang = positions[0].astype(f32)[:, None] * inv_freq
    cos, sin = jnp.cos(ang), jnp.sin(ang)
    cs = jnp.concatenate([cos, cos, sin, sin], axis=-1)

    memb = mem[0].astype(bf16)
    xf = x[0]
    xb = xf.astype(bf16)
    dep = (tok_b,)

    saved, W = [], [None, None]
    for l in range(L):
        sv = {}
        if l == 1:
            gw[1] = dict(zip(l1_early, gather_finish("gather_l1a_finish", g_l1a, xb)))
        sv["x0b"] = xb
        f1w13 = gw[l]["f1w13"][None]
        gate, up, act = ffn_up(f"ffn1_up_{l}", xb, f1w13, 0, dep)
        dep = ()
        if l == 0:
            g_w2, _ = gather_forward("gather_w2_forward", g_w2, act)
            (gw[0]["f1w2"],) = gather_finish("gather_w2_finish", g_w2, act)
            g_mix, _ = gather_forward("gather_mix_forward", g_mix, act)
        z1, x1f, x1b = proj_res_ln(f"ffn1_down_{l}", [act], [gw[l]["f1w2"].reshape(1, D_FF, D_MODEL)], [0], xf,
                                   vec(lng[l, 0]), vec(lnb[l, 0]), 0.5)
        sv.update(gate1=gate, up1=up, act1=act, z1=z1, x1b=x1b)
        if l == 0:
            gw[0].update(zip(MIX_NAMES, gather_finish("gather_mix_finish", g_mix, x1b)))
            g_mid, _ = gather_forward("gather_mid_forward", g_mid, x1b)
        win = gw[l]["win"].reshape(D_MODEL, D_IN)
        win_ext = jnp.concatenate([win, _swap_half(win[:, D_IN - QK_ROPE:])], axis=-1)[None]
        wuq = _from_col_shards(gw[l]["wuq"]).reshape(Q_LORA, MLA_HEADS, QK_NOPE + QK_ROPE)
        wq_ext = jnp.concatenate([wuq, _swap_half(wuq[..., QK_NOPE:])], axis=-1).transpose(1, 0, 2)[None]
        wukv = _from_col_shards(gw[l]["wukv"])[None]
        wbd = _block_diag(pool_w[l][None].astype(bf16))[0]
        u, cq, ckv, cqn, ckvn, q, k, v = mix_pre(f"mix_pre_{l}", x1b, win_ext, wq_ext, wukv, 0,
                                                   vec(q_norm_g[l]), vec(kv_norm_g[l]), cs)
        dpool, ypool = pool_fwd(f"pool_fwd_{l}", u, wbd, vec(pool_scale[l]))
        o, lse = mla_attn_fwd(f"mla_fwd_{l}", q, k, v)
        if l == 0:
            gw[0].update(zip(MID_NAMES, gather_finish("gather_mid_finish", g_mid, o)))
            g_ffn2, tok_f = gather_forward("gather_ffn2_forward", g_ffn2, o)
            dep = (tok_f,)
        else:
            gw[1].update(zip(l1_late, gather_finish("gather_l1b_finish", g_l1b, o)))
        wout = gw[l]["wout"].reshape(D_MODEL, D_MODEL)
        wout_pool, wout_mla = wout[None, :POOL_WIDTH], wout[None, POOL_WIDTH:]
        mwq = gw[l]["mwq"].reshape(1, D_MODEL, D_MODEL)
        mwo = gw[l]["mwo"].reshape(1, D_MODEL, D_MODEL)
        mwkv = gw[l]["mwkv"][None]
        z2, x2f, x2b = proj_res_ln(f"mix_out_{l}", [ypool, o], [wout_pool, wout_mla], [0, 0], x1f,
                                   vec(lng[l, 1]), vec(lnb[l, 1]), 1.0, dep)
        dep = ()
        sv.update(cq=cq, ckv=ckv, cqn=cqn, ckvn=ckvn, q=q, k=k, v=v, dpool=dpool, ypool=ypool, o=o, lse=lse, z2=z2, x2b=x2b)
        kvm = mm_nn_shard(f"mem_kv_{l}", memb, mwkv, 0)
        cq_, co_, z3, x3f, x3b = cross_fwd(f"cross_fwd_{l}", x2b, x2f, mwq, mwo, 0, kvm, vec(lng[l, 2]), vec(lnb[l, 2]))
        sv.update(kvm=kvm, crq=cq_, cro=co_, z3=z3, x3b=x3b)
        if l == 0:
            gw[0].update(zip(FFN2_NAMES, gather_finish("gather_ffn2_finish", g_ffn2, x3b)))
            g_l1a, tok_l = gather_forward("gather_l1a_forward", g_l1a, x3b)
            dep = (tok_l,)
        f2w13 = gw[l]["f2w13"][None]
        f2w2 = gw[l]["f2w2"].reshape(1, D_FF, D_MODEL)
        gate, up, act = ffn_up(f"ffn2_up_{l}", x3b, f2w13, 0, dep)
        dep = ()
        if l == 0:
            g_l1b, tok_l = gather_forward("gather_l1b_forward", g_l1b, act)
            dep = (tok_l,)
        z4, xf, xb = proj_res_ln(f"ffn2_down_{l}", [act], [f2w2], [0], x3f, vec(lng[l, 3]), vec(lnb[l, 3]), 0.5, dep)
        dep = ()
        sv.update(gate2=gate, up2=up, act2=act, z4=z4)
        W[l] = dict(f1w13=f1w13, f1w2=gw[l]["f1w2"].reshape(1, D_FF, D_MODEL), win_ext=win_ext, wq_ext=wq_ext, wukv=wukv,
                    wbd=wbd, wout=wout[None], mwq=mwq, mwo=mwo, f2w13=f2w13, f2w2=f2w2)
        saved.append(sv)

    dln = {}
    dzb, dres, *dln[L - 1, 3], loss_blk = loss_grad("loss_grad", xf, loss_target[0],
                                                   (saved[L - 1]["z4"], vec(lng[L - 1, 3]), 0.5))
    loss = lax.psum(loss_blk[0, 0], ("x", "y", "c"))

    row_shards = lambda a: a.reshape(N_CHIPS, a.shape[0] // N_CHIPS, a.shape[1])
    small = {k_: [None] * L for k_ in ("pool_w", "pool_scale", "gq", "gkv", "lng", "lnb")}
    rest_names = [n for n in W_NAMES if n not in ("f1w13", "f1w2")]
    core = lax.axis_index("c")
    flags = [jnp.reshape(core == o, (1,)).astype(jnp.int32) for o in range(2)]
    qfs = [jnp.stack([chip, (core == o).astype(jnp.int32)]).astype(jnp.int32) for o in range(2)]

    def red_begin(tag, names, gs, layer):
        owners = [REDUCER[layer][n] for n in names]
        st, tok_ = pair_send_start(f"pair_send_start_{tag}", gs, owners, None)
        return (st, owners), tok_

    def red_mid(tag, sto, after):
        st, owners = sto
        gs_, lands_ = pair_send_wait(f"pair_send_wait_{tag}", st, after)
        ps = [pair_sum(f"pair_sum_{tag}_{a}", g_, r_, flags[o]) for a, (g_, r_, o) in enumerate(zip(gs_, lands_, owners))]
        st, tok_ = chip_exchange_start(f"chip_exchange_start_{tag}", ps, owners, None)
        return (st, owners), tok_

    def red_end(tag, sto, layer, prevs, after):
        st, owners = sto
        ps, lands_ = chip_exchange_wait(f"chip_exchange_wait_{tag}", st, after)
        return [chip_sum(f"chip_sum_{tag}_{a}", p_, r_, qfs[o], layer, s_)
                for a, (p_, r_, s_, o) in enumerate(zip(ps, lands_, prevs, owners))]

    def share_start(tag, names, sums_):
        return pair_share_start(f"pair_share_start_{tag}", sums_, [(REDUCER[0][n], REDUCER[1][n]) for n in names], None)

    st_p1 = st_c1 = st_pa = st_ca = None
    for l in reversed(range(L)):
        sv, w = saved[l], W[l]
        g = {}
        dh = ffn_bwd_da(f"ffn2_bwd_da_{l}", dzb, w["f2w2"], 0, sv["gate2"], sv["up2"], dep)
        dep = ()
        g["f2w2"] = row_shards(mm_tn(f"ffn2_dw2_{l}", sv["act2"], dzb))
        g["f2w13"] = mm_tn(f"ffn2_dw13_{l}", sv["x3b"], dh, True)
        dzb, dres, *dln[l, 2] = ffn_dx(f"ffn2_dx_{l}", dh, w["f2w13"], 0, dres, (sv["z3"], vec(lng[l, 2]), 1.0))
        if l == 0:
            st_c1, tok = red_mid("l1", st_p1, dzb)
            dep = (tok, g["f2w2"], g["f2w13"])
        dqc, dkvm = cross_bwd(f"cross_bwd_{l}", dzb, w["mwo"], 0, sv["crq"], sv["kvm"], dep)
        dep = ()
        g["mwo"] = row_shards(mm_tn(f"cross_dwo_{l}", sv["cro"], dzb))
        g["mwq"] = row_shards(mm_tn(f"cross_dwq_{l}", sv["x2b"], dqc))
        g["mwkv"] = mm_tn(f"cross_dwkv_{l}", memb, dkvm, True)
        dzb, dres, *dln[l, 1] = mm_nt_res(f"cross_dx_{l}", [dqc], [w["mwq"]], [0], dres, f32,
                                          (sv["z2"], vec(lng[l, 1]), 1.0))
        dcat = mm_nt_res(f"mix_dcat_{l}", [dzb], [w["wout"]], [0], None, bf16)
        dwo_p = mm_tn(f"mix_dwout_pool_{l}", sv["ypool"], dzb)
        dwo_m = mm_tn(f"mix_dwout_mla_{l}", sv["o"], dzb)
        g["wout"] = row_shards(jnp.concatenate([dwo_p, dwo_m], axis=0))
        dq, dk, dv = mla_attn_bwd(f"mla_bwd_{l}", sv["q"], sv["k"], sv["v"], sv["o"], dcat, sv["lse"], POOL_WIDTH // 128)
        dqe, dkv, dh_rest, dgq, dgkv = mix_post_bwd(f"mix_post_bwd_{l}", dq, dk, dv, w["wq_ext"], w["wukv"], 0, sv["cq"],
                                                     sv["ckv"], vec(q_norm_g[l]), vec(kv_norm_g[l]), cs)
        du, dyw, dscale = pool_bwd(f"pool_bwd_{l}", dcat, sv["dpool"], w["wbd"], vec(pool_scale[l]))
        dwq_e = mm_tn(f"mix_dwuq_{l}", sv["cqn"], dqe).reshape(Q_LORA, MLA_HEADS, 256)
        g["wuq"] = _to_col_shards(jnp.concatenate(
            [dwq_e[..., :QK_NOPE], _unswap_add(dwq_e[..., QK_NOPE:QK_NOPE + QK_ROPE], dwq_e[..., QK_NOPE + QK_ROPE:])],
            axis=-1).reshape(Q_LORA, MLA_HEADS * (QK_NOPE + QK_ROPE)))
        g["wukv"] = _to_col_shards(mm_tn(f"mix_dwukv_{l}", sv["ckvn"], dkv))
        dwbd = mm_tn(f"pool_dw_{l}", sv["dpool"], dyw)
        small["pool_w"][l] = jnp.stack([dwbd[64 * gi:64 * gi + 64, 64 * gi:64 * gi + 64] for gi in range(4)])
        small["pool_scale"][l], small["gq"][l], small["gkv"][l] = dscale[0], dgq[0], dgkv[0]
        dh_ext = jnp.concatenate([du, dh_rest], axis=1)
        dwin_e = mm_tn(f"mix_dwin_{l}", sv["x1b"], dh_ext)
        g["win"] = row_shards(jnp.concatenate(
            [dwin_e[:, :D_IN - QK_ROPE], _unswap_add(dwin_e[:, D_IN - QK_ROPE:D_IN], dwin_e[:, D_IN:])], axis=-1))
        dzb, dres, *dln[l, 0] = mm_nt_res(f"mix_dx_{l}", [dh_ext], [w["win_ext"]], [0], dres, f32,
                                          (sv["z1"], vec(lng[l, 0]), 0.5))
        if l == 0:
            st_pa, tok = red_begin("a0", rest_names, [g[n] for n in rest_names], 0)
            dep = (tok,)
        dh = ffn_bwd_da(f"ffn1_bwd_da_{l}", dzb, w["f1w2"], 0, sv["gate1"], sv["up1"], dep)
        dep = ()
        if l == 0:
            grad_x = ffn_dx(f"ffn1_dx_{l}", dh, w["f1w13"], 0, dres)[None]
            for l_ in range(L):
                small["lng"][l_] = jnp.concatenate([dln[l_, k_][0] for k_ in range(4)], axis=0)
                small["lnb"][l_] = jnp.concatenate([dln[l_, k_][1] for k_ in range(4)], axis=0)
            rep = [jnp.stack(small[k_]).reshape(-1) for k_ in ("pool_w", "pool_scale", "gq", "gkv", "lng", "lnb")]
            sizes = [r.shape[0] for r in rep]
            packed = jnp.concatenate(rep)
            packed = jnp.pad(packed, (0, (-packed.shape[0]) % 1024)).reshape(-1, 128)
            tot = allsum_small("allsum_small_grads", packed, (grad_x,)).reshape(-1)
            st_ca, tok = red_mid("a0", st_pa, (grad_x, tot))
            dep = (tok,)
        else:
            below = ffn_dx(f"ffn1_dx_{l}", dh, w["f1w13"], 0, dres, (saved[l - 1]["z4"], vec(lng[l - 1, 3]), 0.5))
            dln[l - 1, 3] = below[2:]
        g["f1w2"] = row_shards(mm_tn(f"ffn1_dw2_{l}", sv["act1"], dzb, False, dep))
        g["f1w13"] = mm_tn(f"ffn1_dw13_{l}", sv["x0b"], dh, True, dep)
        dep = ()
        if l > 0:
            dzb, dres = below[:2]
        if l == 1:
            st_p1, tok = red_begin("l1", W_NAMES, [g[n] for n in W_NAMES], 1)
            dep = (tok,)

    st_pb, _ = red_begin("b0", ("f1w13", "f1w2"), [g["f1w13"], g["f1w2"]], 0)
    sums1 = dict(zip(W_NAMES, red_end("l1", st_c1, 1, [None] * len(W_NAMES), g["f1w13"])))
    st_cb, tok = red_mid("b0", st_pb, tuple(sums1.values()))
    sums0 = red_end("a0", st_ca, 0, [sums1[n] for n in rest_names], tok)
    share_a, tok_a = share_start("a", rest_names, sums0)

    offs = [0]
    for s_ in sizes:
        offs.append(offs[-1] + s_)
    parts = [tot[offs[i]:offs[i + 1]] for i in range(len(sizes))]
    g_pool_w = parts[0].reshape(pool_w.shape)
    g_pool_scale = parts[1].reshape(pool_scale.shape)
    g_gq = parts[2].reshape(q_norm_g.shape)
    g_gkv = parts[3].reshape(kv_norm_g.shape)
    shard_cols = lambda a: lax.dynamic_slice_in_dim(a.reshape(L, 4, D_MODEL), chip * (D_MODEL // N_CHIPS),
                                                    D_MODEL // N_CHIPS, axis=2)
    g_lng, g_lnb = shard_cols(parts[4]), shard_cols(parts[5])

    out_names = ("lng", "lnb", "f1w13", "f1w2", "win", "pool_w", "pool_scale", "gq", "wuq", "gkv", "wukv", "wout", "mwq",
                 "mwkv", "mwo", "f2w13", "f2w2")
    big = dict(lng=g_lng, lnb=g_lnb, pool_w=g_pool_w, pool_scale=g_pool_scale, gq=g_gq, gkv=g_gkv)
    late = ("f1w13", "f1w2")
    held = ("f2w13", "f2w2")
    ws = [ln_g, ln_b, ffn1_w13, ffn1_w2, w_in, pool_w, pool_scale, q_norm_g, w_uq, kv_norm_g, w_ukv, w_out, mem_wq,
          mem_wkv, mem_wo, ffn2_w13, ffn2_w2]
    ms = [m_ln_g, m_ln_b, m_ffn1_w13, m_ffn1_w2, m_w_in, m_pool_w, m_pool_scale, m_q_norm_g, m_w_uq, m_kv_norm_g, m_w_ukv,
          m_w_out, m_mem_wq, m_mem_wkv, m_mem_wo, m_ffn2_w13, m_ffn2_w2]
    vs = [v_ln_g, v_ln_b, v_ffn1_w13, v_ffn1_w2, v_w_in, v_pool_w, v_pool_scale, v_q_norm_g, v_w_uq, v_kv_norm_g, v_w_ukv,
          v_w_out, v_mem_wq, v_mem_wkv, v_mem_wo, v_ffn2_w13, v_ffn2_w2]
    res = {}

    def update(n, deps=()):
        a = out_names.index(n)
        res[a] = adamw(f"adamw_{a}", ws[a], big[n].reshape(ws[a].shape), ms[a], vs[a], deps)
        return res[a][0]

    small_done = tuple(update(n, (tok_a,)) for n in ("lng", "lnb", "pool_w", "pool_scale", "gq", "gkv"))
    big.update(zip(rest_names, pair_share_wait("pair_share_wait_a", share_a, small_done)))
    first_done = tuple(update(n) for n in rest_names if n not in held)
    sums_b = red_end("b0", st_cb, 0, [sums1[n] for n in late], first_done)
    share_b, tok_b = share_start("b", late, sums_b)
    held_done = tuple(update(n, (tok_b,)) for n in held)
    big.update(zip(late, pair_share_wait("pair_share_wait_b", share_b, held_done)))
    for n in late:
        update(n)
    order = range(len(out_names))
    grads = [big[n].reshape(w_.shape) for n, w_ in zip(out_names, ws)]
    return (loss, grad_x, *grads, *[res[a][0] for a in order], *[res[a][1] for a in order], *[res[a][2] for a in order])
```

```python
import functools
import math

import jax
import jax.numpy as jnp
from jax import lax
from jax.experimental import pallas as pl
from jax.experimental.pallas import tpu as pltpu

f32 = jnp.float32
bf16 = jnp.bfloat16
SDS = jax.ShapeDtypeStruct
MESH = pl.DeviceIdType.MESH

D_MODEL = 1024
DEPTH = 2
N_MEM = 256
MEM_HEADS = 4
MEM_HEAD_DIM = D_MODEL // MEM_HEADS
POOL_WINDOWS = (2, 4, 8, 16)
POOL_WIDTH = 256
POOL_GROUP = 64
QK_NOPE = 128
QK_ROPE = 64
V_HEAD = 128
MLA_HEADS = 6
Q_LORA = 256
KV_LORA = 128
ROPE_BASE = 10000.0
D_FF = 2816
D_IN = POOL_WIDTH + Q_LORA + KV_LORA + QK_ROPE
ALPHA = (2 * DEPTH) ** 0.25
LN_EPS = 1e-5
RMS_EPS = 1e-6
NEG_INF = -1e30
MLA_SCALE = (QK_NOPE + QK_ROPE) ** -0.5
MLA_SCALE_LOG2 = MLA_SCALE * math.log2(math.e)
MEM_SCALE = MEM_HEAD_DIM ** -0.5
ADAM_LR = 0.001
ADAM_B1 = 0.9
ADAM_B2 = 0.999
ADAM_EPS = 1e-08
ADAM_WD = 0.01
ADAM_STEP = 10

N_CHIPS = 4
V7X_VMEM_LIMIT = 56 * 2**20
HALO = 16

_NT = (((1,), (1,)), ((), ()))
_TN = (((0,), (0,)), ((), ()))


def _dot(a, b):
    return jnp.dot(a, b, preferred_element_type=f32)


def _dot_nt(a, b):
    return lax.dot_general(a, b, _NT, preferred_element_type=f32)


def _dot_tn(a, b):
    return lax.dot_general(a, b, _TN, preferred_element_type=f32)


def _cp(*sem):
    return pltpu.CompilerParams(dimension_semantics=sem if sem else None, vmem_limit_bytes=V7X_VMEM_LIMIT)


_DEP_SPEC = pl.BlockSpec(memory_space=pl.ANY)


def _with_deps(body, n_in, deps):
    nd = len(deps)
    if not nd:
        return body

    def wrapped(*refs):
        return body(*refs[:n_in], *refs[n_in + nd:])

    return wrapped


def _tile(n, t):
    t = min(n, t)
    assert n % t == 0, (n, t)
    return t


def _row_tile(rows, cols, itemsize=4, target=2 * 2**20):
    best = None
    for t in range(16, rows + 1, 16):
        if rows % t == 0 and t * cols * itemsize <= target:
            best = t
    return best if best is not None else rows


def ffn_up(name, xb, w13, l, deps=()):
    S = xb.shape[0]
    ns = w13.shape[3]
    tm = _tile(S, 512)

    def body(x_ref, wg_ref, wu_ref, g_ref, u_ref, a_ref):
        x = x_ref[...]
        g = _dot(x, wg_ref[0, 0])
        u = _dot(x, wu_ref[0, 0])
        a = g * jax.nn.sigmoid(g) * u
        g_ref[...] = g.astype(bf16)
        u_ref[...] = u.astype(bf16)
        a_ref[...] = a.astype(bf16)

    out = SDS((S, 2 * ns), bf16)
    return pl.pallas_call(
        _with_deps(body, 3, deps), name=name, grid=(2, S // tm),
        in_specs=[pl.BlockSpec((tm, D_MODEL), lambda j, i: (i, 0)),
                  pl.BlockSpec((1, 1, D_MODEL, ns), lambda j, i: (l, j, 0, 0)),
                  pl.BlockSpec((1, 1, D_MODEL, ns), lambda j, i: (l, j + 2, 0, 0))] + [_DEP_SPEC] * len(deps),
        out_specs=[pl.BlockSpec((tm, ns), lambda j, i: (i, j))] * 3,
        out_shape=[out, out, out],
        compiler_params=_cp("parallel", "parallel"),
    )(xb, w13, w13, *deps)


def proj_res_ln(name, parts, ws, wl, x, g, b, rscale, deps=()):
    S = x.shape[0]
    tm = _tile(S, 512)
    n = len(parts)

    def body(*refs):
        p_refs, w_refs = refs[:n], refs[n:2 * n]
        x_ref, g_ref, b_ref, z_ref, y_ref, yb_ref = refs[2 * n:]
        acc = _dot(p_refs[0][...], w_refs[0][0])
        for k in range(1, n):
            acc = acc + _dot(p_refs[k][...], w_refs[k][0])
        if rscale != 1.0:
            acc = rscale * acc
        z = ALPHA * x_ref[...] + acc
        mu = jnp.mean(z, axis=-1, keepdims=True)
        zc = z - mu
        var = jnp.mean(zc * zc, axis=-1, keepdims=True)
        y = zc * lax.rsqrt(var + LN_EPS) * g_ref[...] + b_ref[...]
        z_ref[...] = z
        y_ref[...] = y
        yb_ref[...] = y.astype(bf16)

    row = lambda i: (i, 0)
    in_specs = [pl.BlockSpec((tm, p.shape[1]), row) for p in parts]
    in_specs += [pl.BlockSpec((1,) + w.shape[1:], functools.partial(lambda li, i: (li, 0, 0), li)) for w, li in zip(ws, wl)]
    in_specs += [pl.BlockSpec((tm, D_MODEL), row), pl.BlockSpec((1, D_MODEL), lambda i: (0, 0)),
                 pl.BlockSpec((1, D_MODEL), lambda i: (0, 0))] + [_DEP_SPEC] * len(deps)
    return pl.pallas_call(
        _with_deps(body, 2 * n + 3, deps), name=name, grid=(S // tm,), in_specs=in_specs,
        out_specs=[pl.BlockSpec((tm, D_MODEL), row)] * 3,
        out_shape=[SDS((S, D_MODEL), f32), SDS((S, D_MODEL), f32), SDS((S, D_MODEL), bf16)],
        compiler_params=_cp("parallel"),
    )(*parts, *ws, x, g, b, *deps)


def _ln_bwd_store(dyv, z_ref, g_ref, rscale, first, dzb_ref, dres_ref, dg_ref, db_ref):
    z = z_ref[...]
    mu = jnp.mean(z, axis=-1, keepdims=True)
    zc = z - mu
    rstd = lax.rsqrt(jnp.mean(zc * zc, axis=-1, keepdims=True) + LN_EPS)
    xhat = zc * rstd
    dxh = dyv * g_ref[...]
    m1 = jnp.mean(dxh, axis=-1, keepdims=True)
    m2 = jnp.mean(dxh * xhat, axis=-1, keepdims=True)
    dz = rstd * (dxh - m1 - xhat * m2)
    dzb_ref[...] = (rscale * dz).astype(bf16)
    dres_ref[...] = ALPHA * dz

    @pl.when(first)
    def _():
        dg_ref[...] = jnp.zeros_like(dg_ref)
        db_ref[...] = jnp.zeros_like(db_ref)

    dg_ref[...] += jnp.sum(dyv * xhat, axis=0, keepdims=True)
    db_ref[...] += jnp.sum(dyv, axis=0, keepdims=True)


def _ln_bwd_specs(S, tm, index):
    vec = pl.BlockSpec((1, D_MODEL), lambda *a: (0, 0))
    blk = pl.BlockSpec((tm, D_MODEL), index)
    in_specs = [blk, vec]
    out_specs = [blk, blk, vec, vec]
    out_shape = [SDS((S, D_MODEL), bf16), SDS((S, D_MODEL), f32), SDS((1, D_MODEL), f32), SDS((1, D_MODEL), f32)]
    return in_specs, out_specs, out_shape


def ffn_bwd_da(name, drb, w2, l, gate, up, deps=()):
    S = drb.shape[0]
    tm = _tile(S, 512)
    nh = D_FF // 2

    def body(dr_ref, w_ref, g_ref, u_ref, dh_ref):
        dr = dr_ref[...]
        for j in range(2):
            cols = slice(j * nh, (j + 1) * nh)
            da = _dot_nt(dr, w_ref[0, cols, :])
            g = g_ref[:, cols].astype(f32)
            u = u_ref[:, cols].astype(f32)
            sg = jax.nn.sigmoid(g)
            dh_ref[:, cols] = (da * u * (sg * (1.0 + g * (1.0 - sg)))).astype(bf16)
            dh_ref[:, D_FF + j * nh:D_FF + (j + 1) * nh] = (da * (g * sg)).astype(bf16)

    row = lambda i: (i, 0)
    return pl.pallas_call(
        _with_deps(body, 4, deps), name=name, grid=(S // tm,),
        in_specs=[pl.BlockSpec((tm, D_MODEL), row), pl.BlockSpec((1, D_FF, D_MODEL), lambda i: (l, 0, 0)),
                  pl.BlockSpec((tm, D_FF), row), pl.BlockSpec((tm, D_FF), row)] + [_DEP_SPEC] * len(deps),
        out_specs=pl.BlockSpec((tm, 2 * D_FF), row),
        out_shape=SDS((S, 2 * D_FF), bf16),
        compiler_params=_cp("parallel"),
    )(drb, w2, gate, up, *deps)


def ffn_dx(name, dh, w13, l, res, ln=None):
    S = dh.shape[0]
    ns = w13.shape[3]
    tm = _tile(S, 1024)
    last = N_CHIPS - 1
    row = lambda i, j: (i, 0)
    in_specs = [pl.BlockSpec((tm, ns), lambda i, j: (i, j)),
                pl.BlockSpec((1, 1, D_MODEL, ns), lambda i, j: (l, j, 0, 0)),
                pl.BlockSpec((tm, D_MODEL), row)]
    if ln is None:
        def body(dh_ref, w_ref, r_ref, o_ref):
            @pl.when(pl.program_id(1) == 0)
            def _():
                o_ref[...] = r_ref[...]

            o_ref[...] += _dot_nt(dh_ref[...], w_ref[0, 0])

        return pl.pallas_call(
            body, name=name, grid=(S // tm, N_CHIPS), in_specs=in_specs,
            out_specs=pl.BlockSpec((tm, D_MODEL), row), out_shape=SDS((S, D_MODEL), f32),
            compiler_params=_cp("parallel", "arbitrary"),
        )(dh, w13, res)

    z, g, rscale = ln

    def body_ln(dh_ref, w_ref, r_ref, z_ref, g_ref, dzb_ref, dres_ref, dg_ref, db_ref, acc_sc):
        i, j = pl.program_id(0), pl.program_id(1)

        @pl.when(j == 0)
        def _():
            acc_sc[...] = r_ref[...]

        acc_sc[...] += _dot_nt(dh_ref[...], w_ref[0, 0])

        @pl.when(j == last)
        def _():
            _ln_bwd_store(acc_sc[...], z_ref, g_ref, rscale, i == 0, dzb_ref, dres_ref, dg_ref, db_ref)

    ln_in, ln_out, ln_shape = _ln_bwd_specs(S, tm, row)
    return pl.pallas_call(
        body_ln, name=name, grid=(S // tm, N_CHIPS), in_specs=in_specs + ln_in, out_specs=ln_out, out_shape=ln_shape,
        scratch_shapes=[pltpu.VMEM((tm, D_MODEL), f32)], compiler_params=_cp("arbitrary", "arbitrary"),
    )(dh, w13, res, z, g)


def mm_nt_res(name, dys, ws, wl, res, out_dtype, ln=None):
    S = dys[0].shape[0]
    K = ws[0].shape[1]
    tm = _tile(S, 1024)
    n = len(dys)
    n_in = 2 * n + (res is not None)

    def product(refs):
        acc = _dot_nt(refs[0][...], refs[n][0])
        for k in range(1, n):
            acc = acc + _dot_nt(refs[k][...], refs[n + k][0])
        if res is not None:
            acc = acc + refs[2 * n][...]
        return acc

    def body(*refs):
        refs[-1][...] = product(refs).astype(out_dtype)

    def body_ln(*refs):
        z_ref, g_ref, dzb_ref, dres_ref, dg_ref, db_ref = refs[n_in:]
        _ln_bwd_store(product(refs), z_ref, g_ref, ln[2], pl.program_id(0) == 0, dzb_ref, dres_ref, dg_ref, db_ref)

    row = lambda i: (i, 0)
    in_specs = [pl.BlockSpec((tm, d.shape[1]), row) for d in dys]
    in_specs += [pl.BlockSpec((1,) + w.shape[1:], functools.partial(lambda li, i: (li, 0, 0), li)) for w, li in zip(ws, wl)]
    args = list(dys) + list(ws)
    if res is not None:
        in_specs.append(pl.BlockSpec((tm, K), row))
        args.append(res)
    if ln is None:
        return pl.pallas_call(
            body, name=name, grid=(S // tm,), in_specs=in_specs,
            out_specs=pl.BlockSpec((tm, K), row), out_shape=SDS((S, K), out_dtype),
            compiler_params=_cp("parallel"),
        )(*args)
    ln_in, ln_out, ln_shape = _ln_bwd_specs(S, tm, row)
    return pl.pallas_call(
        body_ln, name=name, grid=(S // tm,), in_specs=in_specs + ln_in, out_specs=ln_out, out_shape=ln_shape,
        compiler_params=_cp("arbitrary"),
    )(*args, ln[0], ln[1])


def mm_tn(name, x, dy, col_shards=False, deps=()):
    S, K = x.shape
    N = dy.shape[1]
    ts = 512
    while ts * 2 <= min(S, 2048) and S % (ts * 2) == 0 and ts * 2 * K * 2 <= 6 * 2**20:
        ts *= 2
    ts = _tile(S, ts)
    if col_shards:
        tn = N // N_CHIPS
    else:
        tn = N
        while K * tn * 4 > 6 * 2**20 and tn % 256 == 0:
            tn //= 2
    nn = N // tn
    lead = ((0,) if col_shards else ()) + (slice(None), slice(None))

    def body(x_ref, dy_ref, o_ref):
        acc = _dot_tn(x_ref[...].astype(bf16), dy_ref[...].astype(bf16))

        @pl.when(pl.program_id(1) == 0)
        def _():
            o_ref[lead] = acc

        @pl.when(pl.program_id(1) != 0)
        def _():
            o_ref[lead] += acc

    if col_shards:
        out_spec = pl.BlockSpec((1, K, tn), lambda n, s: (n, 0, 0))
        out_shape = SDS((N_CHIPS, K, tn), f32)
    else:
        out_spec = pl.BlockSpec((K, tn), lambda n, s: (0, n))
        out_shape = SDS((K, N), f32)
    return pl.pallas_call(
        _with_deps(body, 2, deps), name=name, grid=(nn, S // ts),
        in_specs=[pl.BlockSpec((ts, K), lambda n, s: (s, 0)), pl.BlockSpec((ts, tn), lambda n, s: (s, n))]
        + [_DEP_SPEC] * len(deps),
        out_specs=out_spec, out_shape=out_shape, compiler_params=_cp("parallel", "arbitrary"),
    )(x, dy, *deps)


def mm_nn_shard(name, x, w, l):
    S, K = x.shape
    ns = w.shape[3]

    def body(x_ref, w_ref, o_ref):
        o_ref[...] = _dot(x_ref[...], w_ref[0, 0]).astype(bf16)

    return pl.pallas_call(
        body, name=name, grid=(N_CHIPS,),
        in_specs=[pl.BlockSpec((S, K), lambda j: (0, 0)), pl.BlockSpec((1, 1, K, ns), lambda j: (l, j, 0, 0))],
        out_specs=pl.BlockSpec((S, ns), lambda j: (0, j)), out_shape=SDS((S, N_CHIPS * ns), bf16),
        compiler_params=_cp("parallel"),
    )(x, w)


def loss_grad(name, y, t, ln):
    S = y.shape[0]
    tm = _tile(S, 512)
    z, g, rscale = ln

    def body(y_ref, t_ref, z_ref, g_ref, dzb_ref, dres_ref, dg_ref, db_ref, loss_ref):
        first = pl.program_id(0) == 0
        e = y_ref[...] - t_ref[...]
        _ln_bwd_store(e * (1.0 / D_MODEL), z_ref, g_ref, rscale, first, dzb_ref, dres_ref, dg_ref, db_ref)

        @pl.when(first)
        def _():
            loss_ref[...] = jnp.zeros_like(loss_ref)

        loss_ref[...] += jnp.full(loss_ref.shape, (0.5 / D_MODEL) * jnp.sum(e * e), f32)

    row = lambda i: (i, 0)
    ln_in, ln_out, ln_shape = _ln_bwd_specs(S, tm, row)
    return pl.pallas_call(
        body, name=name, grid=(S // tm,),
        in_specs=[pl.BlockSpec((tm, D_MODEL), row)] * 2 + ln_in,
        out_specs=ln_out + [pl.BlockSpec((8, 128), lambda i: (0, 0))],
        out_shape=ln_shape + [SDS((8, 128), f32)],
        compiler_params=_cp("arbitrary"),
    )(y, t, z, g)


def _half_sum(t):
    return t + pltpu.roll(t, 64, axis=1)


def mix_pre(name, xb, w_in, wq, wkv, l, gq, gkv, cs):
    S = xb.shape[0]
    tm = _tile(S, 512)
    H = MLA_HEADS
    W_EXT = w_in.shape[2]

    def body(x_ref, win_ref, wq_ref, wkv_ref, gq_ref, gkv_ref, cs_ref,
             u_ref, cq_ref, ckv_ref, cqn_ref, ckvn_ref, q_ref, k_ref, v_ref):
        h = _dot(x_ref[...], win_ref[0])
        u_ref[...] = h[:, :256]
        cq = h[:, 256:512]
        ckv = h[:, 512:640]
        cq_ref[...] = cq
        ckv_ref[...] = ckv
        cqn = (cq * lax.rsqrt(jnp.mean(cq * cq, axis=-1, keepdims=True) + RMS_EPS) * gq_ref[...]).astype(bf16)
        ckvn = (ckv * lax.rsqrt(jnp.mean(ckv * ckv, axis=-1, keepdims=True) + RMS_EPS) * gkv_ref[...]).astype(bf16)
        cqn_ref[...] = cqn
        ckvn_ref[...] = ckvn
        csv = cs_ref[...]
        lane = lax.broadcasted_iota(jnp.int32, (tm, 128), 1)
        kr = jnp.where(lane < 64, _half_sum(h[:, 640:768] * csv), 0.0).astype(bf16)
        kv = _dot(ckvn, wkv_ref[0])
        for hd in range(H):
            qe = _dot(cqn, wq_ref[0, hd])
            q_ref[hd, :, :128] = (qe[:, :128] * MLA_SCALE_LOG2).astype(bf16)
            q_ref[hd, :, 128:] = (_half_sum(qe[:, 128:] * csv) * MLA_SCALE_LOG2).astype(bf16)
            k_ref[hd, :, :128] = kv[:, 256 * hd:256 * hd + 128].astype(bf16)
            k_ref[hd, :, 128:] = kr
            v_ref[hd] = kv[:, 256 * hd + 128:256 * hd + 256].astype(bf16)

    row = lambda i: (i, 0)
    hrow = lambda i: (0, i, 0)
    return pl.pallas_call(
        body, name=name, grid=(S // tm,),
        in_specs=[pl.BlockSpec((tm, D_MODEL), row),
                  pl.BlockSpec((1, D_MODEL, W_EXT), lambda i: (l, 0, 0)),
                  pl.BlockSpec((1, H, Q_LORA, 256), lambda i: (l, 0, 0, 0)),
                  pl.BlockSpec((1, KV_LORA, H * 256), lambda i: (l, 0, 0)),
                  pl.BlockSpec((1, Q_LORA), lambda i: (0, 0)), pl.BlockSpec((1, KV_LORA), lambda i: (0, 0)),
                  pl.BlockSpec((tm, 128), row)],
        out_specs=[pl.BlockSpec((tm, 256), row), pl.BlockSpec((tm, Q_LORA), row), pl.BlockSpec((tm, KV_LORA), row),
                   pl.BlockSpec((tm, Q_LORA), row), pl.BlockSpec((tm, KV_LORA), row),
                   pl.BlockSpec((H, tm, 256), hrow), pl.BlockSpec((H, tm, 256), hrow), pl.BlockSpec((H, tm, 128), hrow)],
        out_shape=[SDS((S, 256), f32), SDS((S, Q_LORA), f32), SDS((S, KV_LORA), f32),
                   SDS((S, Q_LORA), bf16), SDS((S, KV_LORA), bf16),
                   SDS((H, S, 256), bf16), SDS((H, S, 256), bf16), SDS((H, S, 128), bf16)],
        compiler_params=_cp("parallel"),
    )(xb, w_in, wq, wkv, gq, gkv, cs)


def _group_select(col, a2, a4, a8, a16):
    return jnp.where(col < 64, a2, jnp.where(col < 128, a4, jnp.where(col < 192, a8, a16)))


def pool_fwd(name, u, wbd, scale):
    S = u.shape[0]
    tm = _tile(S, 512)
    hb = tm // HALO

    def body(u_ref, halo_ref, w_ref, s_ref, d_ref, y_ref):
        i = pl.program_id(0)
        cur = u_ref[...]
        halo = jnp.where(i > 0, halo_ref[...], 0.0)
        ext = jnp.concatenate([halo, cur], axis=0)
        s2 = ext + pltpu.roll(ext, 1, axis=0)
        s4 = s2 + pltpu.roll(s2, 2, axis=0)
        s8 = s4 + pltpu.roll(s4, 4, axis=0)
        s16 = s8 + pltpu.roll(s8, 8, axis=0)
        t1 = (i * tm + 1 + lax.broadcasted_iota(jnp.int32, (tm, 1), 0)).astype(f32)
        col = lax.broadcasted_iota(jnp.int32, (tm, 256), 1)
        m = _group_select(col, s2[HALO:] / jnp.minimum(t1, 2.0), s4[HALO:] / jnp.minimum(t1, 4.0),
                          s8[HALO:] / jnp.minimum(t1, 8.0), s16[HALO:] / jnp.minimum(t1, 16.0))
        d = (m - cur).astype(bf16)
        d_ref[...] = d
        y_ref[...] = (_dot(d, w_ref[...]) * s_ref[...]).astype(bf16)

    row = lambda i: (i, 0)
    return pl.pallas_call(
        body, name=name, grid=(S // tm,),
        in_specs=[pl.BlockSpec((tm, 256), row), pl.BlockSpec((HALO, 256), lambda i: (jnp.maximum(i * hb - 1, 0), 0)),
                  pl.BlockSpec((256, 256), lambda i: (0, 0)), pl.BlockSpec((1, 256), lambda i: (0, 0))],
        out_specs=[pl.BlockSpec((tm, 256), row)] * 2,
        out_shape=[SDS((S, 256), bf16), SDS((S, 256), bf16)],
        compiler_params=_cp("parallel"),
    )(u, u, wbd, scale)


def pool_bwd(name, dyp, d, wbd, scale):
    S = dyp.shape[0]
    tm = _tile(S, 512)
    hb = tm // HALO
    n_ext = tm + HALO

    def fwd_sum(e, steps):
        k = 1
        for _ in range(steps):
            e = e + pltpu.roll(e, n_ext - k, axis=0)
            k *= 2
        return e

    def body(dy_ref, halo_ref, d_ref, w_ref, s_ref, du_ref, dyw_ref, ds_ref):
        i = pl.program_id(0)
        sc = s_ref[...]
        w = w_ref[...]
        cur = dy_ref[...].astype(f32)
        halo = jnp.where(i < pl.num_programs(0) - 1, halo_ref[...].astype(f32), 0.0)
        dyw = jnp.concatenate([cur, halo], axis=0) * sc
        dyw_ref[...] = dyw[:tm].astype(bf16)
        dd = _dot_nt(dyw.astype(bf16), w)
        t1 = (i * tm + 1 + lax.broadcasted_iota(jnp.int32, (n_ext, 1), 0)).astype(f32)
        f2 = fwd_sum(dd / jnp.minimum(t1, 2.0), 1)
        f4 = fwd_sum(dd / jnp.minimum(t1, 4.0), 2)
        f8 = fwd_sum(dd / jnp.minimum(t1, 8.0), 3)
        f16 = fwd_sum(dd / jnp.minimum(t1, 16.0), 4)
        col = lax.broadcasted_iota(jnp.int32, (tm, 256), 1)
        du_ref[...] = (_group_select(col, f2[:tm], f4[:tm], f8[:tm], f16[:tm]) - dd[:tm]).astype(bf16)

        @pl.when(i == 0)
        def _():
            ds_ref[...] = jnp.zeros_like(ds_ref)

        ds_ref[...] += jnp.sum(cur * _dot(d_ref[...], w), axis=0, keepdims=True)

    row = lambda i: (i, 0)
    nhb = S // HALO
    return pl.pallas_call(
        body, name=name, grid=(S // tm,),
        in_specs=[pl.BlockSpec((tm, 256), row), pl.BlockSpec((HALO, 256), lambda i: (jnp.minimum((i + 1) * hb, nhb - 1), 0)),
                  pl.BlockSpec((tm, 256), row), pl.BlockSpec((256, 256), lambda i: (0, 0)),
                  pl.BlockSpec((1, 256), lambda i: (0, 0))],
        out_specs=[pl.BlockSpec((tm, 256), row), pl.BlockSpec((tm, 256), row), pl.BlockSpec((1, 256), lambda i: (0, 0))],
        out_shape=[SDS((S, 256), bf16), SDS((S, 256), bf16), SDS((1, 256), f32)],
        compiler_params=_cp("arbitrary"),
    )(dyp, dyp, d, wbd, scale)


def _diag_mask(r0, rn, kn):
    rc = (r0 + lax.broadcasted_iota(jnp.int32, (rn, 1), 0)) // 64
    cc = lax.broadcasted_iota(jnp.int32, (1, kn), 1) // 64
    return rc >= cc


def _diag_parts(tq):
    h = tq // 2
    return [(0, h, h), (h, h, tq)] if h % 128 == 0 else [(0, tq, tq)]


def mla_attn_fwd(name, q, k, v):
    H, S, _ = q.shape
    tq = _tile(S, 1024)
    nq = S // tq
    pairs = [(i, j) for i in range(nq) for j in range(i + 1)]
    it = jnp.asarray([p_[0] for p_ in pairs], jnp.int32)
    jt = jnp.asarray([p_[1] for p_ in pairs], jnp.int32)

    def body(it_ref, jt_ref, q_ref, k_ref, v_ref, o_ref, lse_ref, m_sc, l_sc, acc_sc):
        t = pl.program_id(1)
        i, j = it_ref[t], jt_ref[t]

        @pl.when(j == 0)
        def _():
            m_sc[...] = jnp.full_like(m_sc, NEG_INF)
            l_sc[...] = jnp.zeros_like(l_sc)
            acc_sc[...] = jnp.zeros_like(acc_sc)

        def part(r0, rn, kn, masked):
            rows, keys = slice(r0, r0 + rn), slice(0, kn)
            s = _dot_nt(q_ref[0, rows, :], k_ref[0, keys, :])
            if masked:
                s = jnp.where(_diag_mask(r0, rn, kn), s, NEG_INF)
            m_prev = m_sc[rows, :]
            m_new = jnp.maximum(m_prev, jnp.max(s, axis=-1, keepdims=True))
            p = jnp.exp2(s - jnp.tile(m_new, (1, kn // 128)))
            a = jnp.exp2(m_prev - m_new)
            l_sc[rows, :] = a * l_sc[rows, :] + jnp.sum(p, axis=-1, keepdims=True)
            acc_sc[rows, :] = a * acc_sc[rows, :] + _dot(p.astype(bf16), v_ref[0, keys, :])
            m_sc[rows, :] = m_new

        @pl.when(j < i)
        def _():
            part(0, tq, tq, False)

        @pl.when(j == i)
        def _():
            for r0, rn, kn in _diag_parts(tq):
                part(r0, rn, kn, True)
            o_ref[...] = (acc_sc[...] / l_sc[...]).astype(bf16)
            lse_ref[0] = m_sc[...] + jnp.log2(l_sc[...])

    return pl.pallas_call(
        body, name=name,
        grid_spec=pltpu.PrefetchScalarGridSpec(
            num_scalar_prefetch=2, grid=(H, len(pairs)),
            in_specs=[pl.BlockSpec((1, tq, 256), lambda h, t, it_, jt_: (h, it_[t], 0)),
                      pl.BlockSpec((1, tq, 256), lambda h, t, it_, jt_: (h, jt_[t], 0)),
                      pl.BlockSpec((1, tq, 128), lambda h, t, it_, jt_: (h, jt_[t], 0))],
            out_specs=[pl.BlockSpec((tq, 128), lambda h, t, it_, jt_: (it_[t], h)),
                       pl.BlockSpec((1, tq, 128), lambda h, t, it_, jt_: (h, it_[t], 0))],
            scratch_shapes=[pltpu.VMEM((tq, 128), f32), pltpu.VMEM((tq, 128), f32), pltpu.VMEM((tq, 128), f32)]),
        out_shape=[SDS((S, H * 128), bf16), SDS((H, S, 128), f32)],
        compiler_params=_cp("parallel", "arbitrary"),
    )(it, jt, q, k, v)


def mla_attn_bwd(name, q, k, v, o, do, lse, do_col=0):
    H, S, _ = q.shape
    tq = _tile(S, 1024)
    nq = S // tq
    pairs = [(i, j) for j in range(nq) for i in range(j, nq)]
    it = jnp.asarray([p_[0] for p_ in pairs], jnp.int32)
    jt = jnp.asarray([p_[1] for p_ in pairs], jnp.int32)
    n_pairs = len(pairs)

    def body(it_ref, jt_ref, q_ref, k_ref, v_ref, o_ref, do_ref, lse_ref, dq_ref, dk_ref, dv_ref, dq_sc, dk_sc, dv_sc):
        t = pl.program_id(1)
        i, j = it_ref[t], jt_ref[t]

        @pl.when(t == 0)
        def _():
            dq_sc[...] = jnp.zeros_like(dq_sc)

        @pl.when(i == j)
        def _():
            dk_sc[...] = jnp.zeros_like(dk_sc)
            dv_sc[...] = jnp.zeros_like(dv_sc)

        def part(r0, rn, kn, masked):
            rows, keys = slice(r0, r0 + rn), slice(0, kn)
            qv, kv_, dov = q_ref[0, rows, :], k_ref[0, keys, :], do_ref[rows, :]
            s = _dot_nt(qv, kv_)
            if masked:
                s = jnp.where(_diag_mask(r0, rn, kn), s, NEG_INF)
            p = jnp.exp2(s - jnp.tile(lse_ref[0, rows, :], (1, kn // 128)))
            dv_sc[keys, :] += _dot_tn(p.astype(bf16), dov)
            dp = _dot_nt(dov, v_ref[0, keys, :])
            delta = jnp.sum(dov.astype(f32) * o_ref[rows, :].astype(f32), axis=-1, keepdims=True)
            ds = (p * (dp - delta)).astype(bf16)
            dk_sc[keys, :] += _dot_tn(ds, qv)
            dq_rows = pl.ds(pl.multiple_of(i * tq + r0, 128), rn)
            dq_sc[dq_rows, :] += _dot(ds, kv_)

        @pl.when(i > j)
        def _():
            part(0, tq, tq, False)

        @pl.when(i == j)
        def _():
            for r0, rn, kn in _diag_parts(tq):
                part(r0, rn, kn, True)

        @pl.when(i == nq - 1)
        def _():
            dk_ref[0] = (dk_sc[...] * math.log(2.0)).astype(bf16)
            dv_ref[0] = dv_sc[...].astype(bf16)

        @pl.when(t == n_pairs - 1)
        def _():
            dq_ref[0] = (dq_sc[...] * MLA_SCALE).astype(bf16)

    qi = lambda h, t, it_, jt_: (h, it_[t], 0)
    kj = lambda h, t, it_, jt_: (h, jt_[t], 0)
    oi = lambda h, t, it_, jt_: (it_[t], h)
    doi = lambda h, t, it_, jt_: (it_[t], h + do_col)
    return pl.pallas_call(
        body, name=name,
        grid_spec=pltpu.PrefetchScalarGridSpec(
            num_scalar_prefetch=2, grid=(H, n_pairs),
            in_specs=[pl.BlockSpec((1, tq, 256), qi), pl.BlockSpec((1, tq, 256), kj), pl.BlockSpec((1, tq, 128), kj),
                      pl.BlockSpec((tq, 128), oi), pl.BlockSpec((tq, 128), doi), pl.BlockSpec((1, tq, 128), qi)],
            out_specs=[pl.BlockSpec((1, S, 256), lambda h, t, it_, jt_: (h, 0, 0)), pl.BlockSpec((1, tq, 256), kj),
                       pl.BlockSpec((1, tq, 128), kj)],
            scratch_shapes=[pltpu.VMEM((S, 256), f32), pltpu.VMEM((tq, 256), f32), pltpu.VMEM((tq, 128), f32)]),
        out_shape=[SDS((H, S, 256), bf16), SDS((H, S, 256), bf16), SDS((H, S, 128), bf16)],
        compiler_params=_cp("parallel", "arbitrary"),
    )(it, jt, q, k, v, o, do, lse)


def mix_post_bwd(name, dq, dk, dv, wq, wkv, l, cq, ckv, gq, gkv, cs):
    H, S, _ = dq.shape
    tm = _tile(S, 512)

    def rms_bwd(dyn, c, g):
        r = lax.rsqrt(jnp.mean(c * c, axis=-1, keepdims=True) + RMS_EPS)
        ch = c * r
        dyg = dyn * g
        dc = r * (dyg - ch * jnp.mean(dyg * ch, axis=-1, keepdims=True))
        return dc, jnp.sum(dyn * ch, axis=0, keepdims=True)

    def body(dq_ref, dk_ref, dv_ref, wq_ref, wkv_ref, cq_ref, ckv_ref, gq_ref, gkv_ref, cs_ref,
             dqe_ref, dkv_ref, dh_ref, dgq_ref, dgkv_ref):
        csv = cs_ref[...]
        lane = lax.broadcasted_iota(jnp.int32, (tm, 128), 1)
        dcqn = jnp.zeros((tm, Q_LORA), f32)
        dkr = jnp.zeros((tm, 128), f32)
        for hd in range(H):
            dqh = dq_ref[hd].astype(f32)
            dqe = jnp.concatenate([dqh[:, :128], _half_sum(dqh[:, 128:]) * csv], axis=1).astype(bf16)
            dqe_ref[:, 256 * hd:256 * hd + 256] = dqe
            dcqn = dcqn + _dot_nt(dqe, wq_ref[0, hd])
            dkh = dk_ref[hd].astype(f32)
            dkv_ref[:, 256 * hd:256 * hd + 128] = dkh[:, :128].astype(bf16)
            dkv_ref[:, 256 * hd + 128:256 * hd + 256] = dv_ref[hd].astype(bf16)
            dkr = dkr + dkh[:, 128:]
        dckvn = _dot_nt(dkv_ref[...], wkv_ref[0])
        dblk = _half_sum(jnp.where(lane < 64, dkr, 0.0)) * csv
        dcq, dgq = rms_bwd(dcqn, cq_ref[...], gq_ref[...])
        dckv, dgkv = rms_bwd(dckvn, ckv_ref[...], gkv_ref[...])
        dh_ref[:, :256] = dcq.astype(bf16)
        dh_ref[:, 256:384] = dckv.astype(bf16)
        dh_ref[:, 384:] = dblk.astype(bf16)

        @pl.when(pl.program_id(0) == 0)
        def _():
            dgq_ref[...] = jnp.zeros_like(dgq_ref)
            dgkv_ref[...] = jnp.zeros_like(dgkv_ref)

        dgq_ref[...] += dgq
        dgkv_ref[...] += dgkv

    row = lambda i: (i, 0)
    hrow = lambda i: (0, i, 0)
    return pl.pallas_call(
        body, name=name, grid=(S // tm,),
        in_specs=[pl.BlockSpec((H, tm, 256), hrow), pl.BlockSpec((H, tm, 256), hrow), pl.BlockSpec((H, tm, 128), hrow),
                  pl.BlockSpec((1, H, Q_LORA, 256), lambda i: (l, 0, 0, 0)),
                  pl.BlockSpec((1, KV_LORA, H * 256), lambda i: (l, 0, 0)),
                  pl.BlockSpec((tm, Q_LORA), row), pl.BlockSpec((tm, KV_LORA), row),
                  pl.BlockSpec((1, Q_LORA), lambda i: (0, 0)), pl.BlockSpec((1, KV_LORA), lambda i: (0, 0)),
                  pl.BlockSpec((tm, 128), row)],
        out_specs=[pl.BlockSpec((tm, H * 256), row), pl.BlockSpec((tm, H * 256), row), pl.BlockSpec((tm, 512), row),
                   pl.BlockSpec((1, Q_LORA), lambda i: (0, 0)), pl.BlockSpec((1, KV_LORA), lambda i: (0, 0))],
        out_shape=[SDS((S, H * 256), bf16), SDS((S, H * 256), bf16), SDS((S, 512), bf16),
                   SDS((1, Q_LORA), f32), SDS((1, KV_LORA), f32)],
        compiler_params=_cp("arbitrary"),
    )(dq, dk, dv, wq, wkv, cq, ckv, gq, gkv, cs)


def _cross_probs(qb, kv_ref, hd):
    cols = slice(hd * MEM_HEAD_DIM, (hd + 1) * MEM_HEAD_DIM)
    s = _dot_nt(qb[:, cols], kv_ref[:, cols]) * MEM_SCALE
    e = jnp.exp(s - jnp.max(s, axis=-1, keepdims=True))
    return e / jnp.sum(e, axis=-1, keepdims=True)


def cross_fwd(name, xb, xf, wq, wo, l, kv, g, b):
    S = xb.shape[0]
    tm = _tile(S, 512)
    M = kv.shape[0]

    def body(x_ref, xf_ref, wq_ref, wo_ref, k_ref, v_ref, g_ref, b_ref, q_ref, o_ref, z_ref, y_ref, yb_ref):
        qb = _dot(x_ref[...], wq_ref[0]).astype(bf16)
        q_ref[...] = qb
        for hd in range(MEM_HEADS):
            cols = slice(hd * MEM_HEAD_DIM, (hd + 1) * MEM_HEAD_DIM)
            p = _cross_probs(qb, k_ref, hd)
            o_ref[:, cols] = _dot(p.astype(bf16), v_ref[:, cols]).astype(bf16)
        z = ALPHA * xf_ref[...] + _dot(o_ref[...], wo_ref[0])
        mu = jnp.mean(z, axis=-1, keepdims=True)
        zc = z - mu
        var = jnp.mean(zc * zc, axis=-1, keepdims=True)
        y = zc * lax.rsqrt(var + LN_EPS) * g_ref[...] + b_ref[...]
        z_ref[...] = z
        y_ref[...] = y
        yb_ref[...] = y.astype(bf16)

    row = lambda i: (i, 0)
    wspec = pl.BlockSpec((1, D_MODEL, D_MODEL), lambda i: (l, 0, 0))
    vec = pl.BlockSpec((1, D_MODEL), lambda i: (0, 0))
    blk = pl.BlockSpec((tm, D_MODEL), row)
    return pl.pallas_call(
        body, name=name, grid=(S // tm,),
        in_specs=[blk, blk, wspec, wspec, pl.BlockSpec((M, D_MODEL), lambda i: (0, 0)),
                  pl.BlockSpec((M, D_MODEL), lambda i: (0, 1)), vec, vec],
        out_specs=[blk] * 5,
        out_shape=[SDS((S, D_MODEL), bf16), SDS((S, D_MODEL), bf16), SDS((S, D_MODEL), f32), SDS((S, D_MODEL), f32),
                   SDS((S, D_MODEL), bf16)],
        compiler_params=_cp("parallel"),
    )(xb, xf, wq, wo, kv, kv, g, b)


def cross_bwd(name, dzb, wo, l, qb, kv, deps=()):
    S = dzb.shape[0]
    tm = _tile(S, 512)
    M = kv.shape[0]

    def body(dz_ref, wo_ref, q_ref, k_ref, v_ref, dq_ref, dkv_ref):
        @pl.when(pl.program_id(0) == 0)
        def _():
            dkv_ref[...] = jnp.zeros_like(dkv_ref)

        do = _dot_nt(dz_ref[...], wo_ref[0]).astype(bf16)
        qv = q_ref[...]
        for hd in range(MEM_HEADS):
            cols = slice(hd * MEM_HEAD_DIM, (hd + 1) * MEM_HEAD_DIM)
            vcols = slice(D_MODEL + hd * MEM_HEAD_DIM, D_MODEL + (hd + 1) * MEM_HEAD_DIM)
            p = _cross_probs(qv, k_ref, hd)
            doh = do[:, cols]
            dkv_ref[:, vcols] += _dot_tn(p.astype(bf16), doh)
            dp = _dot_nt(doh, v_ref[:, cols])
            ds = (p * (dp - jnp.sum(dp * p, axis=-1, keepdims=True)) * MEM_SCALE).astype(bf16)
            dq_ref[:, cols] = _dot(ds, k_ref[:, cols]).astype(bf16)
            dkv_ref[:, cols] += _dot_tn(ds, qv[:, cols])

    row = lambda i: (i, 0)
    blk = pl.BlockSpec((tm, D_MODEL), row)
    return pl.pallas_call(
        _with_deps(body, 5, deps), name=name, grid=(S // tm,),
        in_specs=[blk, pl.BlockSpec((1, D_MODEL, D_MODEL), lambda i: (l, 0, 0)), blk,
                  pl.BlockSpec((M, D_MODEL), lambda i: (0, 0)), pl.BlockSpec((M, D_MODEL), lambda i: (0, 1))]
        + [_DEP_SPEC] * len(deps),
        out_specs=[blk, pl.BlockSpec((M, 2 * D_MODEL), lambda i: (0, 0))],
        out_shape=[SDS((S, D_MODEL), bf16), SDS((M, 2 * D_MODEL), f32)],
        compiler_params=_cp("arbitrary"),
    )(dzb, wo, qb, kv, kv, *deps)


def adamw(name, w, g, m, v, deps=()):
    shape = w.shape
    cols = shape[-1]
    rows = math.prod(shape[:-1])
    tr = _row_tile(rows, cols, target=2 * 2**20)
    c1 = 1.0 - ADAM_B1 ** ADAM_STEP
    c2 = 1.0 - ADAM_B2 ** ADAM_STEP

    def body(w_ref, g_ref, m_ref, v_ref, d_ref, nm_ref, nv_ref):
        gv = g_ref[...]
        nm = ADAM_B1 * m_ref[...] + (1.0 - ADAM_B1) * gv
        nv = ADAM_B2 * v_ref[...] + (1.0 - ADAM_B2) * (gv * gv)
        d_ref[...] = -ADAM_LR * ((nm / c1) / (jnp.sqrt(nv / c2) + ADAM_EPS) + ADAM_WD * w_ref[...])
        nm_ref[...] = nm
        nv_ref[...] = nv

    blk = pl.BlockSpec((tr, cols), lambda i: (i, 0))
    flat = SDS((rows, cols), f32)
    outs = pl.pallas_call(
        _with_deps(body, 4, deps), name=name, grid=(rows // tr,), in_specs=[blk] * 4 + [_DEP_SPEC] * len(deps),
        out_specs=[blk] * 3, out_shape=[flat] * 3, compiler_params=_cp("parallel"),
    )(*[a.reshape(rows, cols) for a in (w, g, m, v)], *deps)
    return [o.reshape(shape) for o in outs]


def _me():
    return lax.axis_index("x"), lax.axis_index("y"), lax.axis_index("c")


def _other_chips(x, y):
    return [(1 - x, y), (x, 1 - y), (1 - x, 1 - y)]


def _pair_share_each(owners, bufs, sems, mine, act):
    x, y, c = _me()
    for o in range(2):
        slots = [(a, lyr) for a in range(len(bufs)) for lyr in range(DEPTH) if owners[a][lyr] == o]

        @pl.when((c == o) if mine else (c != o))
        def _(slots=slots):
            for a, lyr in slots:
                slot = bufs[a].at[lyr]
                act(_rcopy(slot, slot, sems[0].at[2 * a + lyr], sems[1].at[2 * a + lyr], (x, y, 1 - c)))


def pair_share_start(name, sums, owners, after):
    def body_fn(b_in, s_in, s_out):
        _pair_share_each(owners, b_in, s_out, True, lambda cp: cp.start())

    outs, sems, token = _split_call(name, body_fn, list(sums), [], [2 * len(sums)] * 2, after)
    return (outs, sems[0], sems[1], owners), token


def pair_share_wait(name, st, after):
    bufs, send, recv, owners = st

    def body_fn(b_in, s_in, s_out):
        _pair_share_each(owners, b_in, s_in, True, lambda cp: cp.wait_send())
        _pair_share_each(owners, b_in, s_in, False, lambda cp: cp.wait_recv())

    outs, _, _ = _split_call(name, body_fn, list(bufs), [send, recv], [], after)
    return outs


def allsum_small(name, v, deps=()):
    R = v.shape[0]

    def body(v_ref, o_ref, all_ref, send_sems, recv_sems, local_sem):
        x, y, c = _me()
        me, sibling = (x, y, c), (x, y, 1 - c)
        chips = _other_chips(x, y)

        def rows(px, py, pc):
            return all_ref.at[4 * px + 2 * py + pc]

        def copy(k, block, to, src=None):
            return pltpu.make_async_remote_copy(
                src_ref=rows(*block) if src is None else src, dst_ref=rows(*block),
                send_sem=send_sems.at[k], recv_sem=recv_sems.at[k], device_id=to, device_id_type=MESH)

        mine = pltpu.make_async_copy(v_ref, rows(*me), local_sem)
        mine.start()
        first = [copy(0, me, sibling, src=v_ref)]
        first += [copy(1 + j, me, (*chip, c), src=v_ref) for j, chip in enumerate(chips)]
        for cp in first:
            cp.start()
        passed = [copy(4 + j, (*chip, c), sibling) for j, chip in enumerate(chips)]
        for j, chip in enumerate(chips):
            copy(1 + j, (*chip, c), me).wait_recv()
            passed[j].start()
        copy(0, sibling, me).wait_recv()
        for j, chip in enumerate(chips):
            copy(4 + j, (*chip, 1 - c), me).wait_recv()
        for cp in first + passed:
            cp.wait_send()
        mine.wait()
        acc = all_ref[0]
        for d in range(1, 8):
            acc = acc + all_ref[d]
        o_ref[...] = acc

    return pl.pallas_call(
        _with_deps(body, 1, deps), name=name,
        in_specs=[pl.BlockSpec(memory_space=pltpu.VMEM)] + [_DEP_SPEC] * len(deps),
        out_specs=pl.BlockSpec(memory_space=pltpu.VMEM),
        out_shape=SDS((R, 128), f32),
        scratch_shapes=[pltpu.VMEM((8, R, 128), f32), pltpu.SemaphoreType.DMA((7,)), pltpu.SemaphoreType.DMA((7,)),
                        pltpu.SemaphoreType.DMA],
        compiler_params=pltpu.CompilerParams(vmem_limit_bytes=V7X_VMEM_LIMIT),
    )(v, *deps)


def _swap_half(r):
    return jnp.concatenate([-r[..., 32:], r[..., :32]], axis=-1)


def _unswap_add(p, qg):
    return p + jnp.concatenate([qg[..., 32:], -qg[..., :32]], axis=-1)


def _block_diag(pw):
    L = pw.shape[0]
    out = jnp.zeros((L, 256, 256), pw.dtype)
    for gi in range(4):
        out = out.at[:, 64 * gi:64 * gi + 64, 64 * gi:64 * gi + 64].set(pw[:, gi])
    return out


def _to_col_shards(w):
    *lead, K, N = w.shape
    nl = len(lead)
    return w.reshape(*lead, K, N_CHIPS, N // N_CHIPS).transpose(*range(nl), nl + 1, nl, nl + 2)


def _from_col_shards(w):
    *lead, C, K, n = w.shape
    nl = len(lead)
    return w.transpose(*range(nl), nl + 1, nl, nl + 2).reshape(*lead, K, C * n)


_HBM_SPEC = pl.BlockSpec(memory_space=pltpu.HBM)
_SEM_SPEC = pl.BlockSpec(memory_space=pltpu.SEMAPHORE)
_ANY_SPEC = pl.BlockSpec(memory_space=pl.ANY)
_DATAFLOW = pltpu.SideEffectType.DATAFLOW_SIDE_EFFECTING


def _split_call(name, body_fn, bufs, sems_in, sems_out_sizes, after):
    nb, ni, no = len(bufs), len(sems_in), len(sems_out_sizes)
    afters = () if after is None else tuple(after) if isinstance(after, (tuple, list)) else (after,)

    def body(*refs):
        k = nb + ni + len(afters)
        body_fn(refs[:nb], refs[nb:nb + ni], refs[k:k + no])
        refs[-1][...] = jnp.zeros((8, 128), f32)

    outs = pl.pallas_call(
        body, name=name,
        in_specs=[_HBM_SPEC] * nb + [_SEM_SPEC] * ni + [_ANY_SPEC] * len(afters),
        out_specs=[_SEM_SPEC] * no + [_HBM_SPEC] * nb + [pl.BlockSpec(memory_space=pltpu.VMEM)],
        out_shape=[pltpu.SemaphoreType.DMA((s,)) for s in sems_out_sizes]
        + [pltpu.HBM(b.shape, b.dtype) for b in bufs] + [SDS((8, 128), f32)],
        input_output_aliases={i: no + i for i in range(nb)},
        compiler_params=pltpu.CompilerParams(has_side_effects=_DATAFLOW),
    )(*[pltpu.with_memory_space_constraint(b, pltpu.HBM) for b in bufs], *sems_in, *afters)
    return list(outs[no:no + nb]), list(outs[:no]), outs[-1]


def _rcopy(src, dst, ssem, rsem, to):
    return pltpu.make_async_remote_copy(src_ref=src, dst_ref=dst, send_sem=ssem, recv_sem=rsem, device_id=to,
                                        device_id_type=MESH)


def gather_start(name, groups, after):
    flat = [b for bufs, _ in groups for b in bufs]
    sizes = [3 * len(bufs) for bufs, _ in groups for _ in range(2)]

    def body_fn(b_in, s_in, s_out):
        x, y, c = _me()
        q = 2 * x + y
        chips = _other_chips(x, y)
        pos = 0
        for gi, (bufs, owner) in enumerate(groups):
            refs = b_in[pos:pos + len(bufs)]
            pos += len(bufs)

            @pl.when(c == owner)
            def _(refs=refs, send=s_out[2 * gi], recv=s_out[2 * gi + 1]):
                for a, r in enumerate(refs):
                    for k, (cx, cy) in enumerate(chips):
                        _rcopy(r.at[q], r.at[q], send.at[3 * a + k], recv.at[3 * a + k], (cx, cy, c)).start()

    outs, sems, token = _split_call(name, body_fn, flat, [], sizes, after)
    res, pos = [], 0
    for gi, (bufs, owner) in enumerate(groups):
        res.append((outs[pos:pos + len(bufs)], sems[2 * gi], sems[2 * gi + 1], owner))
        pos += len(bufs)
    return res, token


def gather_forward(name, grp, after):
    bufs, send, recv, owner = grp
    n3 = 3 * len(bufs)

    def body_fn(b_in, s_in, s_out):
        x, y, c = _me()
        q = 2 * x + y
        sibling = (x, y, 1 - c)
        chips = _other_chips(x, y)

        @pl.when(c == owner)
        def _():
            for a, r in enumerate(b_in):
                for k, (cx, cy) in enumerate(chips):
                    i = 3 * a + k
                    land = r.at[2 * cx + cy]
                    _rcopy(r.at[q], r.at[q], s_in[0].at[i], s_in[1].at[i], (cx, cy, c)).wait_send()
                    _rcopy(land, land, s_in[0].at[i], s_in[1].at[i], (cx, cy, c)).wait_recv()
                    _rcopy(land, land, s_out[0].at[i], s_out[1].at[i], sibling).start()

    outs, sems, token = _split_call(name, body_fn, bufs, [send, recv], [n3, n3], after)
    return (outs, sems[0], sems[1], owner), token


def gather_finish(name, grp, after):
    bufs, fsend, frecv, owner = grp

    def body_fn(b_in, s_in, s_out):
        x, y, c = _me()
        sibling = (x, y, 1 - c)
        chips = _other_chips(x, y)

        def each(wait):
            for a, r in enumerate(b_in):
                for k, (cx, cy) in enumerate(chips):
                    land = r.at[2 * cx + cy]
                    wait(_rcopy(land, land, s_in[0].at[3 * a + k], s_in[1].at[3 * a + k], sibling))

        @pl.when(c == owner)
        def _():
            each(lambda cp: cp.wait_send())

        @pl.when(c != owner)
        def _():
            each(lambda cp: cp.wait_recv())

    outs, _, _ = _split_call(name, body_fn, bufs, [fsend, frecv], [], after)
    return outs


def _by_owner(owners):
    return [[a for a, o_ in enumerate(owners) if o_ == o] for o in range(2)]


def pair_send_start(name, gs, owners, after):
    n = len(gs)
    lands = [lax.empty(g.shape, g.dtype) for g in gs]

    def body_fn(b_in, s_in, s_out):
        x, y, c = _me()
        for o, idx in enumerate(_by_owner(owners)):
            @pl.when(c == 1 - o)
            def _(o=o, idx=idx):
                for a in idx:
                    _rcopy(b_in[a], b_in[n + a], s_out[0].at[a], s_out[1].at[a], (x, y, o)).start()

    outs, sems, token = _split_call(name, body_fn, list(gs) + lands, [], [n, n], after)
    return (outs[:n], outs[n:], sems[0], sems[1], owners), token


def pair_send_wait(name, st, after):
    gs, lands, send, recv, owners = st
    n = len(gs)

    def body_fn(b_in, s_in, s_out):
        x, y, c = _me()
        for o, idx in enumerate(_by_owner(owners)):
            @pl.when(c == 1 - o)
            def _(o=o, idx=idx):
                for a in idx:
                    _rcopy(b_in[a], b_in[n + a], s_in[0].at[a], s_in[1].at[a], (x, y, o)).wait_send()

            @pl.when(c == o)
            def _(o=o, idx=idx):
                for a in idx:
                    _rcopy(b_in[a], b_in[n + a], s_in[0].at[a], s_in[1].at[a], (x, y, 1 - o)).wait_recv()

    outs, _, _ = _split_call(name, body_fn, list(gs) + list(lands), [send, recv], [], after)
    return outs[:n], outs[n:]


def chip_exchange_start(name, psums, owners, after):
    n = len(psums)
    lands = [lax.empty((3,) + p.shape[1:], p.dtype) for p in psums]

    def body_fn(b_in, s_in, s_out):
        x, y, c = _me()
        chips = _other_chips(x, y)
        for o, idx in enumerate(_by_owner(owners)):
            @pl.when(c == o)
            def _(idx=idx):
                for a in idx:
                    for k, (cx, cy) in enumerate(chips):
                        _rcopy(b_in[a].at[2 * cx + cy], b_in[n + a].at[k], s_out[0].at[3 * a + k],
                               s_out[1].at[3 * a + k], (cx, cy, c)).start()

    outs, sems, token = _split_call(name, body_fn, list(psums) + lands, [], [3 * n, 3 * n], after)
    return (outs[:n], outs[n:], sems[0], sems[1], owners), token


def chip_exchange_wait(name, st, after):
    psums, lands, send, recv, owners = st
    n = len(psums)

    def body_fn(b_in, s_in, s_out):
        x, y, c = _me()
        chips = _other_chips(x, y)
        for o, idx in enumerate(_by_owner(owners)):
            @pl.when(c == o)
            def _(idx=idx):
                for a in idx:
                    for k, (cx, cy) in enumerate(chips):
                        cp = _rcopy(b_in[a].at[2 * cx + cy], b_in[n + a].at[k], s_in[0].at[3 * a + k],
                                    s_in[1].at[3 * a + k], (cx, cy, c))
                        cp.wait_send()
                        cp.wait_recv()

    outs, _, _ = _split_call(name, body_fn, list(psums) + list(lands), [send, recv], [], after)
    return outs[:n], outs[n:]


def pair_sum(name, g, recv, flag):
    shape = g.shape
    cols = shape[-1]
    rows = math.prod(shape[:-1])
    tr = _row_tile(rows, cols, target=4 * 2**20)

    def body(f_ref, g_ref, r_ref, o_ref):
        o_ref[...] = (g_ref[...] + r_ref[...]).astype(bf16)

    blk = pl.BlockSpec((tr, cols), lambda i, f_ref: (i * f_ref[0], 0))
    out = pl.pallas_call(
        body, name=name,
        grid_spec=pltpu.PrefetchScalarGridSpec(num_scalar_prefetch=1, grid=(rows // tr,), in_specs=[blk, blk],
                                               out_specs=blk),
        out_shape=SDS((rows, cols), bf16), compiler_params=_cp("arbitrary"),
    )(flag, g.reshape(rows, cols), recv.reshape(rows, cols))
    return out.reshape(shape)


def chip_sum(name, psum, recv, qf_arr, layer, prev):
    shard = psum.shape[1:]
    cols = shard[-1]
    rows = math.prod(shard[:-1])
    tr = _row_tile(rows, cols, target=4 * 2**20)

    def body(qf_ref, p_ref, r_ref, *rest):
        rest[-1][0] = ((p_ref[0].astype(f32) + r_ref[0].astype(f32)) + r_ref[1].astype(f32)) + r_ref[2].astype(f32)

    in_specs = [pl.BlockSpec((1, tr, cols), lambda i, qf: (qf[0], i * qf[1], 0)),
                pl.BlockSpec((3, tr, cols), lambda i, qf: (0, i * qf[1], 0))]
    args = [qf_arr, psum.reshape(N_CHIPS, rows, cols), recv.reshape(3, rows, cols)]
    aliases = {}
    if prev is not None:
        in_specs.append(pl.BlockSpec(memory_space=pl.ANY))
        args.append(prev.reshape(DEPTH, rows, cols))
        aliases = {3: 0}
    out = pl.pallas_call(
        body, name=name,
        grid_spec=pltpu.PrefetchScalarGridSpec(
            num_scalar_prefetch=1, grid=(rows // tr,), in_specs=in_specs,
            out_specs=pl.BlockSpec((1, tr, cols), lambda i, qf: (layer, i * qf[1], 0))),
        out_shape=SDS((DEPTH, rows, cols), f32), input_output_aliases=aliases, compiler_params=_cp("arbitrary"),
    )(*args)
    return out.reshape((DEPTH,) + shard)


W_NAMES = ("f1w13", "f1w2", "win", "wuq", "wukv", "wout", "mwq", "mwkv", "mwo", "f2w13", "f2w2")
MIX_NAMES = ("win", "wuq", "wukv")
MID_NAMES = ("wout", "mwq", "mwkv", "mwo")
FFN2_NAMES = ("f2w13", "f2w2")
REDUCER = (dict(f1w13=0, f1w2=1, f2w13=0, win=0, wuq=0, wukv=0, f2w2=1, mwkv=1, wout=1, mwq=1, mwo=1),
           dict(f1w13=0, f2w2=0, mwkv=0, wout=0, f2w13=1, f1w2=1, mwq=1, mwo=1, win=1, wuq=1, wukv=1))


def kernel(x, mem, positions, ln_g, ln_b, ffn1_w13, ffn1_w2, w_in, pool_w, pool_scale, q_norm_g, w_uq, kv_norm_g, w_ukv, w_out, mem_wq, mem_wkv, mem_wo, ffn2_w13, ffn2_w2, loss_target, m_ln_g, m_ln_b, m_ffn1_w13, m_ffn1_w2, m_w_in, m_pool_w, m_pool_scale, m_q_norm_g, m_w_uq, m_kv_norm_g, m_w_ukv, m_w_out, m_mem_wq, m_mem_wkv, m_mem_wo, m_ffn2_w13, m_ffn2_w2, v_ln_g, v_ln_b, v_ffn1_w13, v_ffn1_w2, v_w_in, v_pool_w, v_pool_scale, v_q_norm_g, v_w_uq, v_kv_norm_g, v_w_ukv, v_w_out, v_mem_wq, v_mem_wkv, v_mem_wo, v_ffn2_w13, v_ffn2_w2):
    L = DEPTH
    qx, qy, _ = _me()
    chip = 2 * qx + qy
    vec = lambda a: a.reshape(1, -1)

    shards = dict(zip(W_NAMES, (ffn1_w13, ffn1_w2, w_in, w_uq, w_ukv, w_out, mem_wq, mem_wkv, mem_wo, ffn2_w13, ffn2_w2)))

    def place(sh, slot):
        return lax.dynamic_update_slice(lax.empty((N_CHIPS,) + sh.shape, bf16), sh.astype(bf16)[None],
                                        (slot,) + (0,) * sh.ndim)

    first = ("f1w13", "f1w2")
    bufs = [dict(), dict()]
    for n in first:
        bufs[0][n] = place(shards[n][0], chip)
    gw = [dict(), dict()]

    ln_pad = jnp.zeros((2, L, 4, N_CHIPS, D_MODEL // N_CHIPS), f32)
    ln_pad = lax.dynamic_update_slice(ln_pad, jnp.stack([ln_g, ln_b])[:, :, :, None, :], (0, 0, 0, chip, 0))
    ln_sum = allsum_small("allsum_ln", ln_pad.reshape(-1, 128))
    ln_full = (ln_sum * 0.5).reshape(2, L, 4, D_MODEL)
    lng, lnb = ln_full[0], ln_full[1]

    (g0, g_w2), tok = gather_start("gather_a_start", [([bufs[0]["f1w13"]], 0), ([bufs[0]["f1w2"]], 0)], ln_sum)
    chip_then = chip + tok[0, 0].astype(jnp.int32)
    for l in range(L):
        for n in W_NAMES:
            if n not in bufs[l]:
                bufs[l][n] = place(shards[n][l], chip_then)
    others = tuple(bufs[l][n] for l in range(L) for n in W_NAMES if (l, n) not in ((0, first[0]), (0, first[1])))
    g0, tok = gather_forward("gather_a_forward", g0, others)
    (gw[0]["f1w13"],) = gather_finish("gather_a_finish", g0, None)
    (g_mix, g_mid, g_ffn2, g_l1), tok_b = gather_start(
        "gather_b_start",
        [([bufs[0][n] for n in MIX_NAMES], 0), ([bufs[0][n] for n in MID_NAMES], 0), ([bufs[0][n] for n in FFN2_NAMES], 0),
         ([bufs[1][n] for n in W_NAMES], 1)], tok)

    half = QK_ROPE // 2
    inv_freq = ROPE_BASE ** (-jnp.arange(half, dtype=f32) / half)
    ang = positions[0].astype(f32)[:, None] * inv_freq
    cos, sin = jnp.cos(ang), jnp.sin(ang)
    cs = jnp.concatenate([cos, cos, sin, sin], axis=-1)

    memb = mem[0].astype(bf16)
    xf = x[0]
    xb = xf.astype(bf16)
    dep = (tok_b,)

    saved, W = [], [None, None]
    for l in range(L):
        sv = {}
        if l == 1:
            gl1 = gather_finish("gather_l1_finish", g_l1, xb)
            gw[1] = dict(zip(W_NAMES, gl1))
        sv["x0b"] = xb
        f1w13 = gw[l]["f1w13"][None]
        gate, up, act = ffn_up(f"ffn1_up_{l}", xb, f1w13, 0, dep)
        dep = ()
        if l == 0:
            g_w2, _ = gather_forward("gather_w2_forward", g_w2, act)
            (gw[0]["f1w2"],) = gather_finish("gather_w2_finish", g_w2, act)
            g_mix, _ = gather_forward("gather_mix_forward", g_mix, act)
        z1, x1f, x1b = proj_res_ln(f"ffn1_down_{l}", [act], [gw[l]["f1w2"].reshape(1, D_FF, D_MODEL)], [0], xf,
                                   vec(lng[l, 0]), vec(lnb[l, 0]), 0.5)
        sv.update(gate1=gate, up1=up, act1=act, z1=z1, x1b=x1b)
        if l == 0:
            gw[0].update(zip(MIX_NAMES, gather_finish("gather_mix_finish", g_mix, x1b)))
            g_mid, _ = gather_forward("gather_mid_forward", g_mid, x1b)
        win = gw[l]["win"].reshape(D_MODEL, D_IN)
        win_ext = jnp.concatenate([win, _swap_half(win[:, D_IN - QK_ROPE:])], axis=-1)[None]
        wuq = _from_col_shards(gw[l]["wuq"]).reshape(Q_LORA, MLA_HEADS, QK_NOPE + QK_ROPE)
        wq_ext = jnp.concatenate([wuq, _swap_half(wuq[..., QK_NOPE:])], axis=-1).transpose(1, 0, 2)[None]
        wukv = _from_col_shards(gw[l]["wukv"])[None]
        wbd = _block_diag(pool_w[l][None].astype(bf16))[0]
        u, cq, ckv, cqn, ckvn, q, k, v = mix_pre(f"mix_pre_{l}", x1b, win_ext, wq_ext, wukv, 0,
                                                   vec(q_norm_g[l]), vec(kv_norm_g[l]), cs)
        dpool, ypool = pool_fwd(f"pool_fwd_{l}", u, wbd, vec(pool_scale[l]))
        o, lse = mla_attn_fwd(f"mla_fwd_{l}", q, k, v)
        if l == 0:
            gw[0].update(zip(MID_NAMES, gather_finish("gather_mid_finish", g_mid, o)))
            g_ffn2, tok_f = gather_forward("gather_ffn2_forward", g_ffn2, o)
            dep = (tok_f,)
        wout = gw[l]["wout"].reshape(D_MODEL, D_MODEL)
        wout_pool, wout_mla = wout[None, :POOL_WIDTH], wout[None, POOL_WIDTH:]
        mwq = gw[l]["mwq"].reshape(1, D_MODEL, D_MODEL)
        mwo = gw[l]["mwo"].reshape(1, D_MODEL, D_MODEL)
        mwkv = gw[l]["mwkv"][None]
        z2, x2f, x2b = proj_res_ln(f"mix_out_{l}", [ypool, o], [wout_pool, wout_mla], [0, 0], x1f,
                                   vec(lng[l, 1]), vec(lnb[l, 1]), 1.0, dep)
        dep = ()
        sv.update(cq=cq, ckv=ckv, cqn=cqn, ckvn=ckvn, q=q, k=k, v=v, dpool=dpool, ypool=ypool, o=o, lse=lse, z2=z2, x2b=x2b)
        kvm = mm_nn_shard(f"mem_kv_{l}", memb, mwkv, 0)
        cq_, co_, z3, x3f, x3b = cross_fwd(f"cross_fwd_{l}", x2b, x2f, mwq, mwo, 0, kvm, vec(lng[l, 2]), vec(lnb[l, 2]))
        sv.update(kvm=kvm, crq=cq_, cro=co_, z3=z3, x3b=x3b)
        if l == 0:
            gw[0].update(zip(FFN2_NAMES, gather_finish("gather_ffn2_finish", g_ffn2, x3b)))
            g_l1, tok_l = gather_forward("gather_l1_forward", g_l1, x3b)
            dep = (tok_l,)
        f2w13 = gw[l]["f2w13"][None]
        f2w2 = gw[l]["f2w2"].reshape(1, D_FF, D_MODEL)
        gate, up, act = ffn_up(f"ffn2_up_{l}", x3b, f2w13, 0, dep)
        dep = ()
        z4, xf, xb = proj_res_ln(f"ffn2_down_{l}", [act], [f2w2], [0], x3f, vec(lng[l, 3]), vec(lnb[l, 3]), 0.5)
        sv.update(gate2=gate, up2=up, act2=act, z4=z4)
        W[l] = dict(f1w13=f1w13, f1w2=gw[l]["f1w2"].reshape(1, D_FF, D_MODEL), win_ext=win_ext, wq_ext=wq_ext, wukv=wukv,
                    wbd=wbd, wout=wout[None], mwq=mwq, mwo=mwo, f2w13=f2w13, f2w2=f2w2)
        saved.append(sv)

    dln = {}
    dzb, dres, *dln[L - 1, 3], loss_blk = loss_grad("loss_grad", xf, loss_target[0],
                                                   (saved[L - 1]["z4"], vec(lng[L - 1, 3]), 0.5))
    loss = lax.psum(loss_blk[0, 0], ("x", "y", "c"))

    row_shards = lambda a: a.reshape(N_CHIPS, a.shape[0] // N_CHIPS, a.shape[1])
    small = {k_: [None] * L for k_ in ("pool_w", "pool_scale", "gq", "gkv", "lng", "lnb")}
    rest_names = [n for n in W_NAMES if n not in ("f1w13", "f1w2")]
    core = lax.axis_index("c")
    flags = [jnp.reshape(core == o, (1,)).astype(jnp.int32) for o in range(2)]
    qfs = [jnp.stack([chip, (core == o).astype(jnp.int32)]).astype(jnp.int32) for o in range(2)]

    def red_begin(tag, names, gs, layer):
        owners = [REDUCER[layer][n] for n in names]
        st, tok_ = pair_send_start(f"pair_send_start_{tag}", gs, owners, None)
        return (st, owners), tok_

    def red_mid(tag, sto, after):
        st, owners = sto
        gs_, lands_ = pair_send_wait(f"pair_send_wait_{tag}", st, after)
        ps = [pair_sum(f"pair_sum_{tag}_{a}", g_, r_, flags[o]) for a, (g_, r_, o) in enumerate(zip(gs_, lands_, owners))]
        st, tok_ = chip_exchange_start(f"chip_exchange_start_{tag}", ps, owners, None)
        return (st, owners), tok_

    def red_end(tag, sto, layer, prevs, after):
        st, owners = sto
        ps, lands_ = chip_exchange_wait(f"chip_exchange_wait_{tag}", st, after)
        return [chip_sum(f"chip_sum_{tag}_{a}", p_, r_, qfs[o], layer, s_)
                for a, (p_, r_, s_, o) in enumerate(zip(ps, lands_, prevs, owners))]

    def share_start(tag, names, sums_):
        return pair_share_start(f"pair_share_start_{tag}", sums_, [(REDUCER[0][n], REDUCER[1][n]) for n in names], None)

    st_p1 = st_c1 = st_pa = st_ca = None
    for l in reversed(range(L)):
        sv, w = saved[l], W[l]
        g = {}
        dh = ffn_bwd_da(f"ffn2_bwd_da_{l}", dzb, w["f2w2"], 0, sv["gate2"], sv["up2"], dep)
        dep = ()
        g["f2w2"] = row_shards(mm_tn(f"ffn2_dw2_{l}", sv["act2"], dzb))
        g["f2w13"] = mm_tn(f"ffn2_dw13_{l}", sv["x3b"], dh, True)
        dzb, dres, *dln[l, 2] = ffn_dx(f"ffn2_dx_{l}", dh, w["f2w13"], 0, dres, (sv["z3"], vec(lng[l, 2]), 1.0))
        if l == 0:
            st_c1, tok = red_mid("l1", st_p1, dzb)
            dep = (tok, g["f2w2"], g["f2w13"])
        dqc, dkvm = cross_bwd(f"cross_bwd_{l}", dzb, w["mwo"], 0, sv["crq"], sv["kvm"], dep)
        dep = ()
        g["mwo"] = row_shards(mm_tn(f"cross_dwo_{l}", sv["cro"], dzb))
        g["mwq"] = row_shards(mm_tn(f"cross_dwq_{l}", sv["x2b"], dqc))
        g["mwkv"] = mm_tn(f"cross_dwkv_{l}", memb, dkvm, True)
        dzb, dres, *dln[l, 1] = mm_nt_res(f"cross_dx_{l}", [dqc], [w["mwq"]], [0], dres, f32,
                                          (sv["z2"], vec(lng[l, 1]), 1.0))
        dcat = mm_nt_res(f"mix_dcat_{l}", [dzb], [w["wout"]], [0], None, bf16)
        dwo_p = mm_tn(f"mix_dwout_pool_{l}", sv["ypool"], dzb)
        dwo_m = mm_tn(f"mix_dwout_mla_{l}", sv["o"], dzb)
        g["wout"] = row_shards(jnp.concatenate([dwo_p, dwo_m], axis=0))
        dq, dk, dv = mla_attn_bwd(f"mla_bwd_{l}", sv["q"], sv["k"], sv["v"], sv["o"], dcat, sv["lse"], POOL_WIDTH // 128)
        dqe, dkv, dh_rest, dgq, dgkv = mix_post_bwd(f"mix_post_bwd_{l}", dq, dk, dv, w["wq_ext"], w["wukv"], 0, sv["cq"],
                                                     sv["ckv"], vec(q_norm_g[l]), vec(kv_norm_g[l]), cs)
        du, dyw, dscale = pool_bwd(f"pool_bwd_{l}", dcat, sv["dpool"], w["wbd"], vec(pool_scale[l]))
        dwq_e = mm_tn(f"mix_dwuq_{l}", sv["cqn"], dqe).reshape(Q_LORA, MLA_HEADS, 256)
        g["wuq"] = _to_col_shards(jnp.concatenate(
            [dwq_e[..., :QK_NOPE], _unswap_add(dwq_e[..., QK_NOPE:QK_NOPE + QK_ROPE], dwq_e[..., QK_NOPE + QK_ROPE:])],
            axis=-1).reshape(Q_LORA, MLA_HEADS * (QK_NOPE + QK_ROPE)))
        g["wukv"] = _to_col_shards(mm_tn(f"mix_dwukv_{l}", sv["ckvn"], dkv))
        dwbd = mm_tn(f"pool_dw_{l}", sv["dpool"], dyw)
        small["pool_w"][l] = jnp.stack([dwbd[64 * gi:64 * gi + 64, 64 * gi:64 * gi + 64] for gi in range(4)])
        small["pool_scale"][l], small["gq"][l], small["gkv"][l] = dscale[0], dgq[0], dgkv[0]
        dh_ext = jnp.concatenate([du, dh_rest], axis=1)
        dwin_e = mm_tn(f"mix_dwin_{l}", sv["x1b"], dh_ext)
        g["win"] = row_shards(jnp.concatenate(
            [dwin_e[:, :D_IN - QK_ROPE], _unswap_add(dwin_e[:, D_IN - QK_ROPE:D_IN], dwin_e[:, D_IN:])], axis=-1))
        dzb, dres, *dln[l, 0] = mm_nt_res(f"mix_dx_{l}", [dh_ext], [w["win_ext"]], [0], dres, f32,
                                          (sv["z1"], vec(lng[l, 0]), 0.5))
        if l == 0:
            st_pa, tok = red_begin("a0", rest_names, [g[n] for n in rest_names], 0)
            dep = (tok,)
        dh = ffn_bwd_da(f"ffn1_bwd_da_{l}", dzb, w["f1w2"], 0, sv["gate1"], sv["up1"], dep)
        dep = ()
        if l == 0:
            grad_x = ffn_dx(f"ffn1_dx_{l}", dh, w["f1w13"], 0, dres)[None]
            for l_ in range(L):
                small["lng"][l_] = jnp.concatenate([dln[l_, k_][0] for k_ in range(4)], axis=0)
                small["lnb"][l_] = jnp.concatenate([dln[l_, k_][1] for k_ in range(4)], axis=0)
            rep = [jnp.stack(small[k_]).reshape(-1) for k_ in ("pool_w", "pool_scale", "gq", "gkv", "lng", "lnb")]
            sizes = [r.shape[0] for r in rep]
            packed = jnp.concatenate(rep)
            packed = jnp.pad(packed, (0, (-packed.shape[0]) % 1024)).reshape(-1, 128)
            tot = allsum_small("allsum_small_grads", packed, (grad_x,)).reshape(-1)
            st_ca, tok = red_mid("a0", st_pa, (grad_x, tot))
            dep = (tok,)
        else:
            below = ffn_dx(f"ffn1_dx_{l}", dh, w["f1w13"], 0, dres, (saved[l - 1]["z4"], vec(lng[l - 1, 3]), 0.5))
            dln[l - 1, 3] = below[2:]
        g["f1w2"] = row_shards(mm_tn(f"ffn1_dw2_{l}", sv["act1"], dzb, False, dep))
        g["f1w13"] = mm_tn(f"ffn1_dw13_{l}", sv["x0b"], dh, True, dep)
        dep = ()
        if l > 0:
            dzb, dres = below[:2]
        if l == 1:
            st_p1, tok = red_begin("l1", W_NAMES, [g[n] for n in W_NAMES], 1)
            dep = (tok,)

    st_pb, _ = red_begin("b0", ("f1w13", "f1w2"), [g["f1w13"], g["f1w2"]], 0)
    sums1 = dict(zip(W_NAMES, red_end("l1", st_c1, 1, [None] * len(W_NAMES), g["f1w13"])))
    st_cb, tok = red_mid("b0", st_pb, tuple(sums1.values()))
    sums0 = red_end("a0", st_ca, 0, [sums1[n] for n in rest_names], tok)
    share_a, tok_a = share_start("a", rest_names, sums0)

    offs = [0]
    for s_ in sizes:
        offs.append(offs[-1] + s_)
    parts = [tot[offs[i]:offs[i + 1]] for i in range(len(sizes))]
    g_pool_w = parts[0].reshape(pool_w.shape)
    g_pool_scale = parts[1].reshape(pool_scale.shape)
    g_gq = parts[2].reshape(q_norm_g.shape)
    g_gkv = parts[3].reshape(kv_norm_g.shape)
    shard_cols = lambda a: lax.dynamic_slice_in_dim(a.reshape(L, 4, D_MODEL), chip * (D_MODEL // N_CHIPS),
                                                    D_MODEL // N_CHIPS, axis=2)
    g_lng, g_lnb = shard_cols(parts[4]), shard_cols(parts[5])

    out_names = ("lng", "lnb", "f1w13", "f1w2", "win", "pool_w", "pool_scale", "gq", "wuq", "gkv", "wukv", "wout", "mwq",
                 "mwkv", "mwo", "f2w13", "f2w2")
    big = dict(lng=g_lng, lnb=g_lnb, pool_w=g_pool_w, pool_scale=g_pool_scale, gq=g_gq, gkv=g_gkv)
    late = ("f1w13", "f1w2")
    held = ("f2w13", "f2w2")
    ws = [ln_g, ln_b, ffn1_w13, ffn1_w2, w_in, pool_w, pool_scale, q_norm_g, w_uq, kv_norm_g, w_ukv, w_out, mem_wq,
          mem_wkv, mem_wo, ffn2_w13, ffn2_w2]
    ms = [m_ln_g, m_ln_b, m_ffn1_w13, m_ffn1_w2, m_w_in, m_pool_w, m_pool_scale, m_q_norm_g, m_w_uq, m_kv_norm_g, m_w_ukv,
          m_w_out, m_mem_wq, m_mem_wkv, m_mem_wo, m_ffn2_w13, m_ffn2_w2]
    vs = [v_ln_g, v_ln_b, v_ffn1_w13, v_ffn1_w2, v_w_in, v_pool_w, v_pool_scale, v_q_norm_g, v_w_uq, v_kv_norm_g, v_w_ukv,
          v_w_out, v_mem_wq, v_mem_wkv, v_mem_wo, v_ffn2_w13, v_ffn2_w2]
    res = {}

    def update(n, deps=()):
        a = out_names.index(n)
        res[a] = adamw(f"adamw_{a}", ws[a], big[n].reshape(ws[a].shape), ms[a], vs[a], deps)
        return res[a][0]

    small_done = tuple(update(n, (tok_a,)) for n in ("lng", "lnb", "pool_w", "pool_scale", "gq", "gkv"))
    big.update(zip(rest_names, pair_share_wait("pair_share_wait_a", share_a, small_done)))
    first_done = tuple(update(n) for n in rest_names if n not in held)
    sums_b = red_end("b0", st_cb, 0, [sums1[n] for n in late], first_done)
    share_b, tok_b = share_start("b", late, sums_b)
    held_done = tuple(update(n, (tok_b,)) for n in held)
    big.update(zip(late, pair_share_wait("pair_share_wait_b", share_b, held_done)))
    for n in late:
        update(n)
    order = range(len(out_names))
    grads = [big[n].reshape(w_.shape) for n, w_ in zip(out_names, ws)]
    return (loss, grad_x, *grads, *[res[a][0] for a in order], *[res[a][1] for a in order], *[res[a][2] for a in order])
```

```python
import functools
import math

import jax
import jax.numpy as jnp
from jax import lax
from jax.experimental import pallas as pl
from jax.experimental.pallas import tpu as pltpu

f32 = jnp.float32
bf16 = jnp.bfloat16
SDS = jax.ShapeDtypeStruct
MESH = pl.DeviceIdType.MESH

D_MODEL = 1024
DEPTH = 2
N_MEM = 256
MEM_HEADS = 4
MEM_HEAD_DIM = D_MODEL // MEM_HEADS
POOL_WINDOWS = (2, 4, 8, 16)
POOL_WIDTH = 256
POOL_GROUP = 64
QK_NOPE = 128
QK_ROPE = 64
V_HEAD = 128
MLA_HEADS = 6
Q_LORA = 256
KV_LORA = 128
ROPE_BASE = 10000.0
D_FF = 2816
D_IN = POOL_WIDTH + Q_LORA + KV_LORA + QK_ROPE
ALPHA = (2 * DEPTH) ** 0.25
LN_EPS = 1e-5
RMS_EPS = 1e-6
NEG_INF = -1e30
MLA_SCALE = (QK_NOPE + QK_ROPE) ** -0.5
MLA_SCALE_LOG2 = MLA_SCALE * math.log2(math.e)
MEM_SCALE = MEM_HEAD_DIM ** -0.5
ADAM_LR = 0.001
ADAM_B1 = 0.9
ADAM_B2 = 0.999
ADAM_EPS = 1e-08
ADAM_WD = 0.01
ADAM_STEP = 10

N_CHIPS = 4
V7X_VMEM_LIMIT = 56 * 2**20
HALO = 16

_NT = (((1,), (1,)), ((), ()))
_TN = (((0,), (0,)), ((), ()))


def _dot(a, b):
    return jnp.dot(a, b, preferred_element_type=f32)


def _dot_nt(a, b):
    return lax.dot_general(a, b, _NT, preferred_element_type=f32)


def _dot_tn(a, b):
    return lax.dot_general(a, b, _TN, preferred_element_type=f32)


def _cp(*sem):
    return pltpu.CompilerParams(dimension_semantics=sem if sem else None, vmem_limit_bytes=V7X_VMEM_LIMIT)


_DEP_SPEC = pl.BlockSpec(memory_space=pl.ANY)


def _with_deps(body, n_in, deps):
    nd = len(deps)
    if not nd:
        return body

    def wrapped(*refs):
        return body(*refs[:n_in], *refs[n_in + nd:])

    return wrapped


def _tile(n, t):
    t = min(n, t)
    assert n % t == 0, (n, t)
    return t


def _row_tile(rows, cols, itemsize=4, target=2 * 2**20):
    best = None
    for t in range(16, rows + 1, 16):
        if rows % t == 0 and t * cols * itemsize <= target:
            best = t
    return best if best is not None else rows


def ffn_up(name, xb, w13, l, deps=()):
    S = xb.shape[0]
    ns = w13.shape[3]
    tm = _tile(S, 512)

    def body(x_ref, wg_ref, wu_ref, g_ref, u_ref, a_ref):
        x = x_ref[...]
        g = _dot(x, wg_ref[0, 0])
        u = _dot(x, wu_ref[0, 0])
        a = g * jax.nn.sigmoid(g) * u
        g_ref[...] = g.astype(bf16)
        u_ref[...] = u.astype(bf16)
        a_ref[...] = a.astype(bf16)

    out = SDS((S, 2 * ns), bf16)
    return pl.pallas_call(
        _with_deps(body, 3, deps), name=name, grid=(2, S // tm),
        in_specs=[pl.BlockSpec((tm, D_MODEL), lambda j, i: (i, 0)),
                  pl.BlockSpec((1, 1, D_MODEL, ns), lambda j, i: (l, j, 0, 0)),
                  pl.BlockSpec((1, 1, D_MODEL, ns), lambda j, i: (l, j + 2, 0, 0))] + [_DEP_SPEC] * len(deps),
        out_specs=[pl.BlockSpec((tm, ns), lambda j, i: (i, j))] * 3,
        out_shape=[out, out, out],
        compiler_params=_cp("parallel", "parallel"),
    )(xb, w13, w13, *deps)


def proj_res_ln(name, parts, ws, wl, x, g, b, rscale, deps=()):
    S = x.shape[0]
    tm = _tile(S, 512)
    n = len(parts)

    def body(*refs):
        p_refs, w_refs = refs[:n], refs[n:2 * n]
        x_ref, g_ref, b_ref, z_ref, y_ref, yb_ref = refs[2 * n:]
        acc = _dot(p_refs[0][...], w_refs[0][0])
        for k in range(1, n):
            acc = acc + _dot(p_refs[k][...], w_refs[k][0])
        if rscale != 1.0:
            acc = rscale * acc
        z = ALPHA * x_ref[...] + acc
        mu = jnp.mean(z, axis=-1, keepdims=True)
        zc = z - mu
        var = jnp.mean(zc * zc, axis=-1, keepdims=True)
        y = zc * lax.rsqrt(var + LN_EPS) * g_ref[...] + b_ref[...]
        z_ref[...] = z
        y_ref[...] = y
        yb_ref[...] = y.astype(bf16)

    row = lambda i: (i, 0)
    in_specs = [pl.BlockSpec((tm, p.shape[1]), row) for p in parts]
    in_specs += [pl.BlockSpec((1,) + w.shape[1:], functools.partial(lambda li, i: (li, 0, 0), li)) for w, li in zip(ws, wl)]
    in_specs += [pl.BlockSpec((tm, D_MODEL), row), pl.BlockSpec((1, D_MODEL), lambda i: (0, 0)),
                 pl.BlockSpec((1, D_MODEL), lambda i: (0, 0))] + [_DEP_SPEC] * len(deps)
    return pl.pallas_call(
        _with_deps(body, 2 * n + 3, deps), name=name, grid=(S // tm,), in_specs=in_specs,
        out_specs=[pl.BlockSpec((tm, D_MODEL), row)] * 3,
        out_shape=[SDS((S, D_MODEL), f32), SDS((S, D_MODEL), f32), SDS((S, D_MODEL), bf16)],
        compiler_params=_cp("parallel"),
    )(*parts, *ws, x, g, b, *deps)


def _ln_bwd_store(dyv, z_ref, g_ref, rscale, first, dzb_ref, dres_ref, dg_ref, db_ref):
    z = z_ref[...]
    mu = jnp.mean(z, axis=-1, keepdims=True)
    zc = z - mu
    rstd = lax.rsqrt(jnp.mean(zc * zc, axis=-1, keepdims=True) + LN_EPS)
    xhat = zc * rstd
    dxh = dyv * g_ref[...]
    m1 = jnp.mean(dxh, axis=-1, keepdims=True)
    m2 = jnp.mean(dxh * xhat, axis=-1, keepdims=True)
    dz = rstd * (dxh - m1 - xhat * m2)
    dzb_ref[...] = (rscale * dz).astype(bf16)
    dres_ref[...] = ALPHA * dz

    @pl.when(first)
    def _():
        dg_ref[...] = jnp.zeros_like(dg_ref)
        db_ref[...] = jnp.zeros_like(db_ref)

    dg_ref[...] += jnp.sum(dyv * xhat, axis=0, keepdims=True)
    db_ref[...] += jnp.sum(dyv, axis=0, keepdims=True)


def _ln_bwd_specs(S, tm, index):
    vec = pl.BlockSpec((1, D_MODEL), lambda *a: (0, 0))
    blk = pl.BlockSpec((tm, D_MODEL), index)
    in_specs = [blk, vec]
    out_specs = [blk, blk, vec, vec]
    out_shape = [SDS((S, D_MODEL), bf16), SDS((S, D_MODEL), f32), SDS((1, D_MODEL), f32), SDS((1, D_MODEL), f32)]
    return in_specs, out_specs, out_shape


def ffn_bwd_da(name, drb, w2, l, gate, up, deps=()):
    S = drb.shape[0]
    tm = _tile(S, 512)
    nh = D_FF // 2

    def body(dr_ref, w_ref, g_ref, u_ref, dh_ref):
        dr = dr_ref[...]
        for j in range(2):
            cols = slice(j * nh, (j + 1) * nh)
            da = _dot_nt(dr, w_ref[0, cols, :])
            g = g_ref[:, cols].astype(f32)
            u = u_ref[:, cols].astype(f32)
            sg = jax.nn.sigmoid(g)
            dh_ref[:, cols] = (da * u * (sg * (1.0 + g * (1.0 - sg)))).astype(bf16)
            dh_ref[:, D_FF + j * nh:D_FF + (j + 1) * nh] = (da * (g * sg)).astype(bf16)

    row = lambda i: (i, 0)
    return pl.pallas_call(
        _with_deps(body, 4, deps), name=name, grid=(S // tm,),
        in_specs=[pl.BlockSpec((tm, D_MODEL), row), pl.BlockSpec((1, D_FF, D_MODEL), lambda i: (l, 0, 0)),
                  pl.BlockSpec((tm, D_FF), row), pl.BlockSpec((tm, D_FF), row)] + [_DEP_SPEC] * len(deps),
        out_specs=pl.BlockSpec((tm, 2 * D_FF), row),
        out_shape=SDS((S, 2 * D_FF), bf16),
        compiler_params=_cp("parallel"),
    )(drb, w2, gate, up, *deps)


def ffn_dx(name, dh, w13, l, res, ln=None):
    S = dh.shape[0]
    ns = w13.shape[3]
    tm = _tile(S, 1024)
    last = N_CHIPS - 1
    row = lambda i, j: (i, 0)
    in_specs = [pl.BlockSpec((tm, ns), lambda i, j: (i, j)),
                pl.BlockSpec((1, 1, D_MODEL, ns), lambda i, j: (l, j, 0, 0)),
                pl.BlockSpec((tm, D_MODEL), row)]
    if ln is None:
        def body(dh_ref, w_ref, r_ref, o_ref):
            @pl.when(pl.program_id(1) == 0)
            def _():
                o_ref[...] = r_ref[...]

            o_ref[...] += _dot_nt(dh_ref[...], w_ref[0, 0])

        return pl.pallas_call(
            body, name=name, grid=(S // tm, N_CHIPS), in_specs=in_specs,
            out_specs=pl.BlockSpec((tm, D_MODEL), row), out_shape=SDS((S, D_MODEL), f32),
            compiler_params=_cp("parallel", "arbitrary"),
        )(dh, w13, res)

    z, g, rscale = ln

    def body_ln(dh_ref, w_ref, r_ref, z_ref, g_ref, dzb_ref, dres_ref, dg_ref, db_ref, acc_sc):
        i, j = pl.program_id(0), pl.program_id(1)

        @pl.when(j == 0)
        def _():
            acc_sc[...] = r_ref[...]

        acc_sc[...] += _dot_nt(dh_ref[...], w_ref[0, 0])

        @pl.when(j == last)
        def _():
            _ln_bwd_store(acc_sc[...], z_ref, g_ref, rscale, i == 0, dzb_ref, dres_ref, dg_ref, db_ref)

    ln_in, ln_out, ln_shape = _ln_bwd_specs(S, tm, row)
    return pl.pallas_call(
        body_ln, name=name, grid=(S // tm, N_CHIPS), in_specs=in_specs + ln_in, out_specs=ln_out, out_shape=ln_shape,
        scratch_shapes=[pltpu.VMEM((tm, D_MODEL), f32)], compiler_params=_cp("arbitrary", "arbitrary"),
    )(dh, w13, res, z, g)


def mm_nt_res(name, dys, ws, wl, res, out_dtype, ln=None):
    S = dys[0].shape[0]
    K = ws[0].shape[1]
    tm = _tile(S, 1024)
    n = len(dys)
    n_in = 2 * n + (res is not None)

    def product(refs):
        acc = _dot_nt(refs[0][...], refs[n][0])
        for k in range(1, n):
            acc = acc + _dot_nt(refs[k][...], refs[n + k][0])
        if res is not None:
            acc = acc + refs[2 * n][...]
        return acc

    def body(*refs):
        refs[-1][...] = product(refs).astype(out_dtype)

    def body_ln(*refs):
        z_ref, g_ref, dzb_ref, dres_ref, dg_ref, db_ref = refs[n_in:]
        _ln_bwd_store(product(refs), z_ref, g_ref, ln[2], pl.program_id(0) == 0, dzb_ref, dres_ref, dg_ref, db_ref)

    row = lambda i: (i, 0)
    in_specs = [pl.BlockSpec((tm, d.shape[1]), row) for d in dys]
    in_specs += [pl.BlockSpec((1,) + w.shape[1:], functools.partial(lambda li, i: (li, 0, 0), li)) for w, li in zip(ws, wl)]
    args = list(dys) + list(ws)
    if res is not None:
        in_specs.append(pl.BlockSpec((tm, K), row))
        args.append(res)
    if ln is None:
        return pl.pallas_call(
            body, name=name, grid=(S // tm,), in_specs=in_specs,
            out_specs=pl.BlockSpec((tm, K), row), out_shape=SDS((S, K), out_dtype),
            compiler_params=_cp("parallel"),
        )(*args)
    ln_in, ln_out, ln_shape = _ln_bwd_specs(S, tm, row)
    return pl.pallas_call(
        body_ln, name=name, grid=(S // tm,), in_specs=in_specs + ln_in, out_specs=ln_out, out_shape=ln_shape,
        compiler_params=_cp("arbitrary"),
    )(*args, ln[0], ln[1])


def mm_tn(name, x, dy, col_shards=False, deps=()):
    S, K = x.shape
    N = dy.shape[1]
    ts = 512
    while ts * 2 <= min(S, 2048) and S % (ts * 2) == 0 and ts * 2 * K * 2 <= 6 * 2**20:
        ts *= 2
    ts = _tile(S, ts)
    if col_shards:
        tn = N // N_CHIPS
    else:
        tn = N
        while K * tn * 4 > 6 * 2**20 and tn % 256 == 0:
            tn //= 2
    nn = N // tn
    lead = ((0,) if col_shards else ()) + (slice(None), slice(None))

    def body(x_ref, dy_ref, o_ref):
        acc = _dot_tn(x_ref[...].astype(bf16), dy_ref[...].astype(bf16))

        @pl.when(pl.program_id(1) == 0)
        def _():
            o_ref[lead] = acc

        @pl.when(pl.program_id(1) != 0)
        def _():
            o_ref[lead] += acc

    if col_shards:
        out_spec = pl.BlockSpec((1, K, tn), lambda n, s: (n, 0, 0))
        out_shape = SDS((N_CHIPS, K, tn), f32)
    else:
        out_spec = pl.BlockSpec((K, tn), lambda n, s: (0, n))
        out_shape = SDS((K, N), f32)
    return pl.pallas_call(
        _with_deps(body, 2, deps), name=name, grid=(nn, S // ts),
        in_specs=[pl.BlockSpec((ts, K), lambda n, s: (s, 0)), pl.BlockSpec((ts, tn), lambda n, s: (s, n))]
        + [_DEP_SPEC] * len(deps),
        out_specs=out_spec, out_shape=out_shape, compiler_params=_cp("parallel", "arbitrary"),
    )(x, dy, *deps)


def mm_nn_shard(name, x, w, l):
    S, K = x.shape
    ns = w.shape[3]

    def body(x_ref, w_ref, o_ref):
        o_ref[...] = _dot(x_ref[...], w_ref[0, 0]).astype(bf16)

    return pl.pallas_call(
        body, name=name, grid=(N_CHIPS,),
        in_specs=[pl.BlockSpec((S, K), lambda j: (0, 0)), pl.BlockSpec((1, 1, K, ns), lambda j: (l, j, 0, 0))],
        out_specs=pl.BlockSpec((S, ns), lambda j: (0, j)), out_shape=SDS((S, N_CHIPS * ns), bf16),
        compiler_params=_cp("parallel"),
    )(x, w)


def loss_grad(name, y, t, ln):
    S = y.shape[0]
    tm = _tile(S, 512)
    z, g, rscale = ln

    def body(y_ref, t_ref, z_ref, g_ref, dzb_ref, dres_ref, dg_ref, db_ref, loss_ref):
        first = pl.program_id(0) == 0
        e = y_ref[...] - t_ref[...]
        _ln_bwd_store(e * (1.0 / D_MODEL), z_ref, g_ref, rscale, first, dzb_ref, dres_ref, dg_ref, db_ref)

        @pl.when(first)
        def _():
            loss_ref[...] = jnp.zeros_like(loss_ref)

        loss_ref[...] += jnp.full(loss_ref.shape, (0.5 / D_MODEL) * jnp.sum(e * e), f32)

    row = lambda i: (i, 0)
    ln_in, ln_out, ln_shape = _ln_bwd_specs(S, tm, row)
    return pl.pallas_call(
        body, name=name, grid=(S // tm,),
        in_specs=[pl.BlockSpec((tm, D_MODEL), row)] * 2 + ln_in,
        out_specs=ln_out + [pl.BlockSpec((8, 128), lambda i: (0, 0))],
        out_shape=ln_shape + [SDS((8, 128), f32)],
        compiler_params=_cp("arbitrary"),
    )(y, t, z, g)


def _half_sum(t):
    return t + pltpu.roll(t, 64, axis=1)


def mix_pre(name, xb, w_in, wq, wkv, l, gq, gkv, cs):
    S = xb.shape[0]
    tm = _tile(S, 512)
    H = MLA_HEADS
    W_EXT = w_in.shape[2]

    def body(x_ref, win_ref, wq_ref, wkv_ref, gq_ref, gkv_ref, cs_ref,
             u_ref, cq_ref, ckv_ref, cqn_ref, ckvn_ref, q_ref, k_ref, v_ref):
        h = _dot(x_ref[...], win_ref[0])
        u_ref[...] = h[:, :256]
        cq = h[:, 256:512]
        ckv = h[:, 512:640]
        cq_ref[...] = cq
        ckv_ref[...] = ckv
        cqn = (cq * lax.rsqrt(jnp.mean(cq * cq, axis=-1, keepdims=True) + RMS_EPS) * gq_ref[...]).astype(bf16)
        ckvn = (ckv * lax.rsqrt(jnp.mean(ckv * ckv, axis=-1, keepdims=True) + RMS_EPS) * gkv_ref[...]).astype(bf16)
        cqn_ref[...] = cqn
        ckvn_ref[...] = ckvn
        csv = cs_ref[...]
        lane = lax.broadcasted_iota(jnp.int32, (tm, 128), 1)
        kr = jnp.where(lane < 64, _half_sum(h[:, 640:768] * csv), 0.0).astype(bf16)
        kv = _dot(ckvn, wkv_ref[0])
        for hd in range(H):
            qe = _dot(cqn, wq_ref[0, hd])
            q_ref[hd, :, :128] = (qe[:, :128] * MLA_SCALE_LOG2).astype(bf16)
            q_ref[hd, :, 128:] = (_half_sum(qe[:, 128:] * csv) * MLA_SCALE_LOG2).astype(bf16)
            k_ref[hd, :, :128] = kv[:, 256 * hd:256 * hd + 128].astype(bf16)
            k_ref[hd, :, 128:] = kr
            v_ref[hd] = kv[:, 256 * hd + 128:256 * hd + 256].astype(bf16)

    row = lambda i: (i, 0)
    hrow = lambda i: (0, i, 0)
    return pl.pallas_call(
        body, name=name, grid=(S // tm,),
        in_specs=[pl.BlockSpec((tm, D_MODEL), row),
                  pl.BlockSpec((1, D_MODEL, W_EXT), lambda i: (l, 0, 0)),
                  pl.BlockSpec((1, H, Q_LORA, 256), lambda i: (l, 0, 0, 0)),
                  pl.BlockSpec((1, KV_LORA, H * 256), lambda i: (l, 0, 0)),
                  pl.BlockSpec((1, Q_LORA), lambda i: (0, 0)), pl.BlockSpec((1, KV_LORA), lambda i: (0, 0)),
                  pl.BlockSpec((tm, 128), row)],
        out_specs=[pl.BlockSpec((tm, 256), row), pl.BlockSpec((tm, Q_LORA), row), pl.BlockSpec((tm, KV_LORA), row),
                   pl.BlockSpec((tm, Q_LORA), row), pl.BlockSpec((tm, KV_LORA), row),
                   pl.BlockSpec((H, tm, 256), hrow), pl.BlockSpec((H, tm, 256), hrow), pl.BlockSpec((H, tm, 128), hrow)],
        out_shape=[SDS((S, 256), f32), SDS((S, Q_LORA), f32), SDS((S, KV_LORA), f32),
                   SDS((S, Q_LORA), bf16), SDS((S, KV_LORA), bf16),
                   SDS((H, S, 256), bf16), SDS((H, S, 256), bf16), SDS((H, S, 128), bf16)],
        compiler_params=_cp("parallel"),
    )(xb, w_in, wq, wkv, gq, gkv, cs)


def _group_select(col, a2, a4, a8, a16):
    return jnp.where(col < 64, a2, jnp.where(col < 128, a4, jnp.where(col < 192, a8, a16)))


def pool_fwd(name, u, wbd, scale):
    S = u.shape[0]
    tm = _tile(S, 512)
    hb = tm // HALO

    def body(u_ref, halo_ref, w_ref, s_ref, d_ref, y_ref):
        i = pl.program_id(0)
        cur = u_ref[...]
        halo = jnp.where(i > 0, halo_ref[...], 0.0)
        ext = jnp.concatenate([halo, cur], axis=0)
        s2 = ext + pltpu.roll(ext, 1, axis=0)
        s4 = s2 + pltpu.roll(s2, 2, axis=0)
        s8 = s4 + pltpu.roll(s4, 4, axis=0)
        s16 = s8 + pltpu.roll(s8, 8, axis=0)
        t1 = (i * tm + 1 + lax.broadcasted_iota(jnp.int32, (tm, 1), 0)).astype(f32)
        col = lax.broadcasted_iota(jnp.int32, (tm, 256), 1)
        m = _group_select(col, s2[HALO:] / jnp.minimum(t1, 2.0), s4[HALO:] / jnp.minimum(t1, 4.0),
                          s8[HALO:] / jnp.minimum(t1, 8.0), s16[HALO:] / jnp.minimum(t1, 16.0))
        d = (m - cur).astype(bf16)
        d_ref[...] = d
        y_ref[...] = (_dot(d, w_ref[...]) * s_ref[...]).astype(bf16)

    row = lambda i: (i, 0)
    return pl.pallas_call(
        body, name=name, grid=(S // tm,),
        in_specs=[pl.BlockSpec((tm, 256), row), pl.BlockSpec((HALO, 256), lambda i: (jnp.maximum(i * hb - 1, 0), 0)),
                  pl.BlockSpec((256, 256), lambda i: (0, 0)), pl.BlockSpec((1, 256), lambda i: (0, 0))],
        out_specs=[pl.BlockSpec((tm, 256), row)] * 2,
        out_shape=[SDS((S, 256), bf16), SDS((S, 256), bf16)],
        compiler_params=_cp("parallel"),
    )(u, u, wbd, scale)


def pool_bwd(name, dyp, d, wbd, scale):
    S = dyp.shape[0]
    tm = _tile(S, 512)
    hb = tm // HALO
    n_ext = tm + HALO

    def fwd_sum(e, steps):
        k = 1
        for _ in range(steps):
            e = e + pltpu.roll(e, n_ext - k, axis=0)
            k *= 2
        return e

    def body(dy_ref, halo_ref, d_ref, w_ref, s_ref, du_ref, dyw_ref, ds_ref):
        i = pl.program_id(0)
        sc = s_ref[...]
        w = w_ref[...]
        cur = dy_ref[...].astype(f32)
        halo = jnp.where(i < pl.num_programs(0) - 1, halo_ref[...].astype(f32), 0.0)
        dyw = jnp.concatenate([cur, halo], axis=0) * sc
        dyw_ref[...] = dyw[:tm].astype(bf16)
        dd = _dot_nt(dyw.astype(bf16), w)
        t1 = (i * tm + 1 + lax.broadcasted_iota(jnp.int32, (n_ext, 1), 0)).astype(f32)
        f2 = fwd_sum(dd / jnp.minimum(t1, 2.0), 1)
        f4 = fwd_sum(dd / jnp.minimum(t1, 4.0), 2)
        f8 = fwd_sum(dd / jnp.minimum(t1, 8.0), 3)
        f16 = fwd_sum(dd / jnp.minimum(t1, 16.0), 4)
        col = lax.broadcasted_iota(jnp.int32, (tm, 256), 1)
        du_ref[...] = (_group_select(col, f2[:tm], f4[:tm], f8[:tm], f16[:tm]) - dd[:tm]).astype(bf16)

        @pl.when(i == 0)
        def _():
            ds_ref[...] = jnp.zeros_like(ds_ref)

        ds_ref[...] += jnp.sum(cur * _dot(d_ref[...], w), axis=0, keepdims=True)

    row = lambda i: (i, 0)
    nhb = S // HALO
    return pl.pallas_call(
        body, name=name, grid=(S // tm,),
        in_specs=[pl.BlockSpec((tm, 256), row), pl.BlockSpec((HALO, 256), lambda i: (jnp.minimum((i + 1) * hb, nhb - 1), 0)),
                  pl.BlockSpec((tm, 256), row), pl.BlockSpec((256, 256), lambda i: (0, 0)),
                  pl.BlockSpec((1, 256), lambda i: (0, 0))],
        out_specs=[pl.BlockSpec((tm, 256), row), pl.BlockSpec((tm, 256), row), pl.BlockSpec((1, 256), lambda i: (0, 0))],
        out_shape=[SDS((S, 256), bf16), SDS((S, 256), bf16), SDS((1, 256), f32)],
        compiler_params=_cp("arbitrary"),
    )(dyp, dyp, d, wbd, scale)


def _diag_mask(r0, rn, kn):
    rc = (r0 + lax.broadcasted_iota(jnp.int32, (rn, 1), 0)) // 64
    cc = lax.broadcasted_iota(jnp.int32, (1, kn), 1) // 64
    return rc >= cc


def _diag_parts(tq):
    h = tq // 2
    return [(0, h, h), (h, h, tq)] if h % 128 == 0 else [(0, tq, tq)]


def mla_attn_fwd(name, q, k, v):
    H, S, _ = q.shape
    tq = _tile(S, 1024)
    nq = S // tq
    pairs = [(i, j) for i in range(nq) for j in range(i + 1)]
    it = jnp.asarray([p_[0] for p_ in pairs], jnp.int32)
    jt = jnp.asarray([p_[1] for p_ in pairs], jnp.int32)

    def body(it_ref, jt_ref, q_ref, k_ref, v_ref, o_ref, lse_ref, m_sc, l_sc, acc_sc):
        t = pl.program_id(1)
        i, j = it_ref[t], jt_ref[t]

        @pl.when(j == 0)
        def _():
            m_sc[...] = jnp.full_like(m_sc, NEG_INF)
            l_sc[...] = jnp.zeros_like(l_sc)
            acc_sc[...] = jnp.zeros_like(acc_sc)

        def part(r0, rn, kn, masked):
            rows, keys = slice(r0, r0 + rn), slice(0, kn)
            s = _dot_nt(q_ref[0, rows, :], k_ref[0, keys, :])
            if masked:
                s = jnp.where(_diag_mask(r0, rn, kn), s, NEG_INF)
            m_prev = m_sc[rows, :]
            m_new = jnp.maximum(m_prev, jnp.max(s, axis=-1, keepdims=True))
            p = jnp.exp2(s - jnp.tile(m_new, (1, kn // 128)))
            a = jnp.exp2(m_prev - m_new)
            l_sc[rows, :] = a * l_sc[rows, :] + jnp.sum(p, axis=-1, keepdims=True)
            acc_sc[rows, :] = a * acc_sc[rows, :] + _dot(p.astype(bf16), v_ref[0, keys, :])
            m_sc[rows, :] = m_new

        @pl.when(j < i)
        def _():
            part(0, tq, tq, False)

        @pl.when(j == i)
        def _():
            for r0, rn, kn in _diag_parts(tq):
                part(r0, rn, kn, True)
            o_ref[...] = (acc_sc[...] / l_sc[...]).astype(bf16)
            lse_ref[0] = m_sc[...] + jnp.log2(l_sc[...])

    return pl.pallas_call(
        body, name=name,
        grid_spec=pltpu.PrefetchScalarGridSpec(
            num_scalar_prefetch=2, grid=(H, len(pairs)),
            in_specs=[pl.BlockSpec((1, tq, 256), lambda h, t, it_, jt_: (h, it_[t], 0)),
                      pl.BlockSpec((1, tq, 256), lambda h, t, it_, jt_: (h, jt_[t], 0)),
                      pl.BlockSpec((1, tq, 128), lambda h, t, it_, jt_: (h, jt_[t], 0))],
            out_specs=[pl.BlockSpec((tq, 128), lambda h, t, it_, jt_: (it_[t], h)),
                       pl.BlockSpec((1, tq, 128), lambda h, t, it_, jt_: (h, it_[t], 0))],
            scratch_shapes=[pltpu.VMEM((tq, 128), f32), pltpu.VMEM((tq, 128), f32), pltpu.VMEM((tq, 128), f32)]),
        out_shape=[SDS((S, H * 128), bf16), SDS((H, S, 128), f32)],
        compiler_params=_cp("parallel", "arbitrary"),
    )(it, jt, q, k, v)


def mla_attn_bwd(name, q, k, v, o, do, lse, do_col=0):
    H, S, _ = q.shape
    tq = _tile(S, 1024)
    nq = S // tq
    pairs = [(i, j) for j in range(nq) for i in range(j, nq)]
    it = jnp.asarray([p_[0] for p_ in pairs], jnp.int32)
    jt = jnp.asarray([p_[1] for p_ in pairs], jnp.int32)
    n_pairs = len(pairs)

    def body(it_ref, jt_ref, q_ref, k_ref, v_ref, o_ref, do_ref, lse_ref, dq_ref, dk_ref, dv_ref, dq_sc, dk_sc, dv_sc):
        t = pl.program_id(1)
        i, j = it_ref[t], jt_ref[t]

        @pl.when(t == 0)
        def _():
            dq_sc[...] = jnp.zeros_like(dq_sc)

        @pl.when(i == j)
        def _():
            dk_sc[...] = jnp.zeros_like(dk_sc)
            dv_sc[...] = jnp.zeros_like(dv_sc)

        def part(r0, rn, kn, masked):
            rows, keys = slice(r0, r0 + rn), slice(0, kn)
            qv, kv_, dov = q_ref[0, rows, :], k_ref[0, keys, :], do_ref[rows, :]
            s = _dot_nt(qv, kv_)
            if masked:
                s = jnp.where(_diag_mask(r0, rn, kn), s, NEG_INF)
            p = jnp.exp2(s - jnp.tile(lse_ref[0, rows, :], (1, kn // 128)))
            dv_sc[keys, :] += _dot_tn(p.astype(bf16), dov)
            dp = _dot_nt(dov, v_ref[0, keys, :])
            delta = jnp.sum(dov.astype(f32) * o_ref[rows, :].astype(f32), axis=-1, keepdims=True)
            ds = (p * (dp - delta)).astype(bf16)
            dk_sc[keys, :] += _dot_tn(ds, qv)
            dq_rows = pl.ds(pl.multiple_of(i * tq + r0, 128), rn)
            dq_sc[dq_rows, :] += _dot(ds, kv_)

        @pl.when(i > j)
        def _():
            part(0, tq, tq, False)

        @pl.when(i == j)
        def _():
            for r0, rn, kn in _diag_parts(tq):
                part(r0, rn, kn, True)

        @pl.when(i == nq - 1)
        def _():
            dk_ref[0] = (dk_sc[...] * math.log(2.0)).astype(bf16)
            dv_ref[0] = dv_sc[...].astype(bf16)

        @pl.when(t == n_pairs - 1)
        def _():
            dq_ref[0] = (dq_sc[...] * MLA_SCALE).astype(bf16)

    qi = lambda h, t, it_, jt_: (h, it_[t], 0)
    kj = lambda h, t, it_, jt_: (h, jt_[t], 0)
    oi = lambda h, t, it_, jt_: (it_[t], h)
    doi = lambda h, t, it_, jt_: (it_[t], h + do_col)
    return pl.pallas_call(
        body, name=name,
        grid_spec=pltpu.PrefetchScalarGridSpec(
            num_scalar_prefetch=2, grid=(H, n_pairs),
            in_specs=[pl.BlockSpec((1, tq, 256), qi), pl.BlockSpec((1, tq, 256), kj), pl.BlockSpec((1, tq, 128), kj),
                      pl.BlockSpec((tq, 128), oi), pl.BlockSpec((tq, 128), doi), pl.BlockSpec((1, tq, 128), qi)],
            out_specs=[pl.BlockSpec((1, S, 256), lambda h, t, it_, jt_: (h, 0, 0)), pl.BlockSpec((1, tq, 256), kj),
                       pl.BlockSpec((1, tq, 128), kj)],
            scratch_shapes=[pltpu.VMEM((S, 256), f32), pltpu.VMEM((tq, 256), f32), pltpu.VMEM((tq, 128), f32)]),
        out_shape=[SDS((H, S, 256), bf16), SDS((H, S, 256), bf16), SDS((H, S, 128), bf16)],
        compiler_params=_cp("parallel", "arbitrary"),
    )(it, jt, q, k, v, o, do, lse)


def mix_post_bwd(name, dq, dk, dv, wq, wkv, l, cq, ckv, gq, gkv, cs):
    H, S, _ = dq.shape
    tm = _tile(S, 1024)

    def rms_bwd(dyn, c, g):
        r = lax.rsqrt(jnp.mean(c * c, axis=-1, keepdims=True) + RMS_EPS)
        ch = c * r
        dyg = dyn * g
        dc = r * (dyg - ch * jnp.mean(dyg * ch, axis=-1, keepdims=True))
        return dc, jnp.sum(dyn * ch, axis=0, keepdims=True)

    def body(dq_ref, dk_ref, dv_ref, wq_ref, wkv_ref, cq_ref, ckv_ref, gq_ref, gkv_ref, cs_ref,
             dqe_ref, dkv_ref, dh_ref, dgq_ref, dgkv_ref):
        csv = cs_ref[...]
        lane = lax.broadcasted_iota(jnp.int32, (tm, 128), 1)
        dcqn = jnp.zeros((tm, Q_LORA), f32)
        dkr = jnp.zeros((tm, 128), f32)
        for hd in range(H):
            dqh = dq_ref[hd].astype(f32)
            dqe = jnp.concatenate([dqh[:, :128], _half_sum(dqh[:, 128:]) * csv], axis=1).astype(bf16)
            dqe_ref[:, 256 * hd:256 * hd + 256] = dqe
            dcqn = dcqn + _dot_nt(dqe, wq_ref[0, hd])
            dkh = dk_ref[hd].astype(f32)
            dkv_ref[:, 256 * hd:256 * hd + 128] = dkh[:, :128].astype(bf16)
            dkv_ref[:, 256 * hd + 128:256 * hd + 256] = dv_ref[hd].astype(bf16)
            dkr = dkr + dkh[:, 128:]
        dckvn = _dot_nt(dkv_ref[...], wkv_ref[0])
        dblk = _half_sum(jnp.where(lane < 64, dkr, 0.0)) * csv
        dcq, dgq = rms_bwd(dcqn, cq_ref[...], gq_ref[...])
        dckv, dgkv = rms_bwd(dckvn, ckv_ref[...], gkv_ref[...])
        dh_ref[:, :256] = dcq.astype(bf16)
        dh_ref[:, 256:384] = dckv.astype(bf16)
        dh_ref[:, 384:] = dblk.astype(bf16)

        @pl.when(pl.program_id(0) == 0)
        def _():
            dgq_ref[...] = jnp.zeros_like(dgq_ref)
            dgkv_ref[...] = jnp.zeros_like(dgkv_ref)

        dgq_ref[...] += dgq
        dgkv_ref[...] += dgkv

    row = lambda i: (i, 0)
    hrow = lambda i: (0, i, 0)
    return pl.pallas_call(
        body, name=name, grid=(S // tm,),
        in_specs=[pl.BlockSpec((H, tm, 256), hrow), pl.BlockSpec((H, tm, 256), hrow), pl.BlockSpec((H, tm, 128), hrow),
                  pl.BlockSpec((1, H, Q_LORA, 256), lambda i: (l, 0, 0, 0)),
                  pl.BlockSpec((1, KV_LORA, H * 256), lambda i: (l, 0, 0)),
                  pl.BlockSpec((tm, Q_LORA), row), pl.BlockSpec((tm, KV_LORA), row),
                  pl.BlockSpec((1, Q_LORA), lambda i: (0, 0)), pl.BlockSpec((1, KV_LORA), lambda i: (0, 0)),
                  pl.BlockSpec((tm, 128), row)],
        out_specs=[pl.BlockSpec((tm, H * 256), row), pl.BlockSpec((tm, H * 256), row), pl.BlockSpec((tm, 512), row),
                   pl.BlockSpec((1, Q_LORA), lambda i: (0, 0)), pl.BlockSpec((1, KV_LORA), lambda i: (0, 0))],
        out_shape=[SDS((S, H * 256), bf16), SDS((S, H * 256), bf16), SDS((S, 512), bf16),
                   SDS((1, Q_LORA), f32), SDS((1, KV_LORA), f32)],
        compiler_params=_cp("arbitrary"),
    )(dq, dk, dv, wq, wkv, cq, ckv, gq, gkv, cs)


def _cross_probs(qb, kv_ref, hd):
    cols = slice(hd * MEM_HEAD_DIM, (hd + 1) * MEM_HEAD_DIM)
    s = _dot_nt(qb[:, cols], kv_ref[:, cols]) * MEM_SCALE
    e = jnp.exp(s - jnp.max(s, axis=-1, keepdims=True))
    return e / jnp.sum(e, axis=-1, keepdims=True)


def cross_fwd(name, xb, xf, wq, wo, l, kv, g, b):
    S = xb.shape[0]
    tm = _tile(S, 512)
    M = kv.shape[0]

    def body(x_ref, xf_ref, wq_ref, wo_ref, k_ref, v_ref, g_ref, b_ref, q_ref, o_ref, z_ref, y_ref, yb_ref):
        qb = _dot(x_ref[...], wq_ref[0]).astype(bf16)
        q_ref[...] = qb
        for hd in range(MEM_HEADS):
            cols = slice(hd * MEM_HEAD_DIM, (hd + 1) * MEM_HEAD_DIM)
            p = _cross_probs(qb, k_ref, hd)
            o_ref[:, cols] = _dot(p.astype(bf16), v_ref[:, cols]).astype(bf16)
        z = ALPHA * xf_ref[...] + _dot(o_ref[...], wo_ref[0])
        mu = jnp.mean(z, axis=-1, keepdims=True)
        zc = z - mu
        var = jnp.mean(zc * zc, axis=-1, keepdims=True)
        y = zc * lax.rsqrt(var + LN_EPS) * g_ref[...] + b_ref[...]
        z_ref[...] = z
        y_ref[...] = y
        yb_ref[...] = y.astype(bf16)

    row = lambda i: (i, 0)
    wspec = pl.BlockSpec((1, D_MODEL, D_MODEL), lambda i: (l, 0, 0))
    vec = pl.BlockSpec((1, D_MODEL), lambda i: (0, 0))
    blk = pl.BlockSpec((tm, D_MODEL), row)
    return pl.pallas_call(
        body, name=name, grid=(S // tm,),
        in_specs=[blk, blk, wspec, wspec, pl.BlockSpec((M, D_MODEL), lambda i: (0, 0)),
                  pl.BlockSpec((M, D_MODEL), lambda i: (0, 1)), vec, vec],
        out_specs=[blk] * 5,
        out_shape=[SDS((S, D_MODEL), bf16), SDS((S, D_MODEL), bf16), SDS((S, D_MODEL), f32), SDS((S, D_MODEL), f32),
                   SDS((S, D_MODEL), bf16)],
        compiler_params=_cp("parallel"),
    )(xb, xf, wq, wo, kv, kv, g, b)


def cross_bwd(name, dzb, wo, l, qb, kv, deps=()):
    S = dzb.shape[0]
    tm = _tile(S, 1024)
    M = kv.shape[0]

    def body(dz_ref, wo_ref, q_ref, k_ref, v_ref, dq_ref, dkv_ref):
        @pl.when(pl.program_id(0) == 0)
        def _():
            dkv_ref[...] = jnp.zeros_like(dkv_ref)

        do = _dot_nt(dz_ref[...], wo_ref[0]).astype(bf16)
        qv = q_ref[...]
        for hd in range(MEM_HEADS):
            cols = slice(hd * MEM_HEAD_DIM, (hd + 1) * MEM_HEAD_DIM)
            vcols = slice(D_MODEL + hd * MEM_HEAD_DIM, D_MODEL + (hd + 1) * MEM_HEAD_DIM)
            p = _cross_probs(qv, k_ref, hd)
            doh = do[:, cols]
            dkv_ref[:, vcols] += _dot_tn(p.astype(bf16), doh)
            dp = _dot_nt(doh, v_ref[:, cols])
            ds = (p * (dp - jnp.sum(dp * p, axis=-1, keepdims=True)) * MEM_SCALE).astype(bf16)
            dq_ref[:, cols] = _dot(ds, k_ref[:, cols]).astype(bf16)
            dkv_ref[:, cols] += _dot_tn(ds, qv[:, cols])

    row = lambda i: (i, 0)
    blk = pl.BlockSpec((tm, D_MODEL), row)
    return pl.pallas_call(
        _with_deps(body, 5, deps), name=name, grid=(S // tm,),
        in_specs=[blk, pl.BlockSpec((1, D_MODEL, D_MODEL), lambda i: (l, 0, 0)), blk,
                  pl.BlockSpec((M, D_MODEL), lambda i: (0, 0)), pl.BlockSpec((M, D_MODEL), lambda i: (0, 1))]
        + [_DEP_SPEC] * len(deps),
        out_specs=[blk, pl.BlockSpec((M, 2 * D_MODEL), lambda i: (0, 0))],
        out_shape=[SDS((S, D_MODEL), bf16), SDS((M, 2 * D_MODEL), f32)],
        compiler_params=_cp("arbitrary"),
    )(dzb, wo, qb, kv, kv, *deps)


def adamw(name, w, g, m, v, deps=()):
    shape = w.shape
    cols = shape[-1]
    rows = math.prod(shape[:-1])
    tr = _row_tile(rows, cols, target=2 * 2**20)
    c1 = 1.0 - ADAM_B1 ** ADAM_STEP
    c2 = 1.0 - ADAM_B2 ** ADAM_STEP

    def body(w_ref, g_ref, m_ref, v_ref, d_ref, nm_ref, nv_ref):
        gv = g_ref[...]
        nm = ADAM_B1 * m_ref[...] + (1.0 - ADAM_B1) * gv
        nv = ADAM_B2 * v_ref[...] + (1.0 - ADAM_B2) * (gv * gv)
        d_ref[...] = -ADAM_LR * ((nm / c1) / (jnp.sqrt(nv / c2) + ADAM_EPS) + ADAM_WD * w_ref[...])
        nm_ref[...] = nm
        nv_ref[...] = nv

    blk = pl.BlockSpec((tr, cols), lambda i: (i, 0))
    flat = SDS((rows, cols), f32)
    outs = pl.pallas_call(
        _with_deps(body, 4, deps), name=name, grid=(rows // tr,), in_specs=[blk] * 4 + [_DEP_SPEC] * len(deps),
        out_specs=[blk] * 3, out_shape=[flat] * 3, compiler_params=_cp("parallel"),
    )(*[a.reshape(rows, cols) for a in (w, g, m, v)], *deps)
    return [o.reshape(shape) for o in outs]


def _me():
    return lax.axis_index("x"), lax.axis_index("y"), lax.axis_index("c")


def _other_chips(x, y):
    return [(1 - x, y), (x, 1 - y), (1 - x, 1 - y)]


def _pair_share_each(owners, bufs, sems, mine, act):
    x, y, c = _me()
    for o in range(2):
        slots = [(a, lyr) for a in range(len(bufs)) for lyr in range(DEPTH) if owners[a][lyr] == o]

        @pl.when((c == o) if mine else (c != o))
        def _(slots=slots):
            for a, lyr in slots:
                slot = bufs[a].at[lyr]
                act(_rcopy(slot, slot, sems[0].at[2 * a + lyr], sems[1].at[2 * a + lyr], (x, y, 1 - c)))


def pair_share_start(name, sums, owners, after):
    def body_fn(b_in, s_in, s_out):
        _pair_share_each(owners, b_in, s_out, True, lambda cp: cp.start())

    outs, sems, token = _split_call(name, body_fn, list(sums), [], [2 * len(sums)] * 2, after)
    return (outs, sems[0], sems[1], owners), token


def pair_share_wait(name, st, after):
    bufs, send, recv, owners = st

    def body_fn(b_in, s_in, s_out):
        _pair_share_each(owners, b_in, s_in, True, lambda cp: cp.wait_send())
        _pair_share_each(owners, b_in, s_in, False, lambda cp: cp.wait_recv())

    outs, _, _ = _split_call(name, body_fn, list(bufs), [send, recv], [], after)
    return outs


def allsum_small(name, v, deps=()):
    R = v.shape[0]

    def body(v_ref, o_ref, all_ref, send_sems, recv_sems, local_sem):
        x, y, c = _me()
        me, sibling = (x, y, c), (x, y, 1 - c)
        chips = _other_chips(x, y)

        def rows(px, py, pc):
            return all_ref.at[4 * px + 2 * py + pc]

        def copy(k, block, to, src=None):
            return pltpu.make_async_remote_copy(
                src_ref=rows(*block) if src is None else src, dst_ref=rows(*block),
                send_sem=send_sems.at[k], recv_sem=recv_sems.at[k], device_id=to, device_id_type=MESH)

        mine = pltpu.make_async_copy(v_ref, rows(*me), local_sem)
        mine.start()
        first = [copy(0, me, sibling, src=v_ref)]
        first += [copy(1 + j, me, (*chip, c), src=v_ref) for j, chip in enumerate(chips)]
        for cp in first:
            cp.start()
        passed = [copy(4 + j, (*chip, c), sibling) for j, chip in enumerate(chips)]
        for j, chip in enumerate(chips):
            copy(1 + j, (*chip, c), me).wait_recv()
            passed[j].start()
        copy(0, sibling, me).wait_recv()
        for j, chip in enumerate(chips):
            copy(4 + j, (*chip, 1 - c), me).wait_recv()
        for cp in first + passed:
            cp.wait_send()
        mine.wait()
        acc = all_ref[0]
        for d in range(1, 8):
            acc = acc + all_ref[d]
        o_ref[...] = acc

    return pl.pallas_call(
        _with_deps(body, 1, deps), name=name,
        in_specs=[pl.BlockSpec(memory_space=pltpu.VMEM)] + [_DEP_SPEC] * len(deps),
        out_specs=pl.BlockSpec(memory_space=pltpu.VMEM),
        out_shape=SDS((R, 128), f32),
        scratch_shapes=[pltpu.VMEM((8, R, 128), f32), pltpu.SemaphoreType.DMA((7,)), pltpu.SemaphoreType.DMA((7,)),
                        pltpu.SemaphoreType.DMA],
        compiler_params=pltpu.CompilerParams(vmem_limit_bytes=V7X_VMEM_LIMIT),
    )(v, *deps)


def _swap_half(r):
    return jnp.concatenate([-r[..., 32:], r[..., :32]], axis=-1)


def _unswap_add(p, qg):
    return p + jnp.concatenate([qg[..., 32:], -qg[..., :32]], axis=-1)


def _block_diag(pw):
    L = pw.shape[0]
    out = jnp.zeros((L, 256, 256), pw.dtype)
    for gi in range(4):
        out = out.at[:, 64 * gi:64 * gi + 64, 64 * gi:64 * gi + 64].set(pw[:, gi])
    return out


def _to_col_shards(w):
    *lead, K, N = w.shape
    nl = len(lead)
    return w.reshape(*lead, K, N_CHIPS, N // N_CHIPS).transpose(*range(nl), nl + 1, nl, nl + 2)


def _from_col_shards(w):
    *lead, C, K, n = w.shape
    nl = len(lead)
    return w.transpose(*range(nl), nl + 1, nl, nl + 2).reshape(*lead, K, C * n)


_HBM_SPEC = pl.BlockSpec(memory_space=pltpu.HBM)
_SEM_SPEC = pl.BlockSpec(memory_space=pltpu.SEMAPHORE)
_ANY_SPEC = pl.BlockSpec(memory_space=pl.ANY)
_DATAFLOW = pltpu.SideEffectType.DATAFLOW_SIDE_EFFECTING


def _split_call(name, body_fn, bufs, sems_in, sems_out_sizes, after):
    nb, ni, no = len(bufs), len(sems_in), len(sems_out_sizes)
    afters = () if after is None else tuple(after) if isinstance(after, (tuple, list)) else (after,)

    def body(*refs):
        k = nb + ni + len(afters)
        body_fn(refs[:nb], refs[nb:nb + ni], refs[k:k + no])
        refs[-1][...] = jnp.zeros((8, 128), f32)

    outs = pl.pallas_call(
        body, name=name,
        in_specs=[_HBM_SPEC] * nb + [_SEM_SPEC] * ni + [_ANY_SPEC] * len(afters),
        out_specs=[_SEM_SPEC] * no + [_HBM_SPEC] * nb + [pl.BlockSpec(memory_space=pltpu.VMEM)],
        out_shape=[pltpu.SemaphoreType.DMA((s,)) for s in sems_out_sizes]
        + [pltpu.HBM(b.shape, b.dtype) for b in bufs] + [SDS((8, 128), f32)],
        input_output_aliases={i: no + i for i in range(nb)},
        compiler_params=pltpu.CompilerParams(has_side_effects=_DATAFLOW),
    )(*[pltpu.with_memory_space_constraint(b, pltpu.HBM) for b in bufs], *sems_in, *afters)
    return list(outs[no:no + nb]), list(outs[:no]), outs[-1]


def _rcopy(src, dst, ssem, rsem, to):
    return pltpu.make_async_remote_copy(src_ref=src, dst_ref=dst, send_sem=ssem, recv_sem=rsem, device_id=to,
                                        device_id_type=MESH)


def gather_start(name, groups, after):
    flat = [b for bufs, _ in groups for b in bufs]
    sizes = [3 * len(bufs) for bufs, _ in groups for _ in range(2)]

    def body_fn(b_in, s_in, s_out):
        x, y, c = _me()
        q = 2 * x + y
        chips = _other_chips(x, y)
        pos = 0
        for gi, (bufs, owner) in enumerate(groups):
            refs = b_in[pos:pos + len(bufs)]
            pos += len(bufs)

            @pl.when(c == owner)
            def _(refs=refs, send=s_out[2 * gi], recv=s_out[2 * gi + 1]):
                for a, r in enumerate(refs):
                    for k, (cx, cy) in enumerate(chips):
                        _rcopy(r.at[q], r.at[q], send.at[3 * a + k], recv.at[3 * a + k], (cx, cy, c)).start()

    outs, sems, token = _split_call(name, body_fn, flat, [], sizes, after)
    res, pos = [], 0
    for gi, (bufs, owner) in enumerate(groups):
        res.append((outs[pos:pos + len(bufs)], sems[2 * gi], sems[2 * gi + 1], owner))
        pos += len(bufs)
    return res, token


def gather_forward(name, grp, after):
    bufs, send, recv, owner = grp
    n3 = 3 * len(bufs)

    def body_fn(b_in, s_in, s_out):
        x, y, c = _me()
        q = 2 * x + y
        sibling = (x, y, 1 - c)
        chips = _other_chips(x, y)

        @pl.when(c == owner)
        def _():
            for a, r in enumerate(b_in):
                for k, (cx, cy) in enumerate(chips):
                    i = 3 * a + k
                    land = r.at[2 * cx + cy]
                    _rcopy(r.at[q], r.at[q], s_in[0].at[i], s_in[1].at[i], (cx, cy, c)).wait_send()
                    _rcopy(land, land, s_in[0].at[i], s_in[1].at[i], (cx, cy, c)).wait_recv()
                    _rcopy(land, land, s_out[0].at[i], s_out[1].at[i], sibling).start()

    outs, sems, token = _split_call(name, body_fn, bufs, [send, recv], [n3, n3], after)
    return (outs, sems[0], sems[1], owner), token


def gather_finish(name, grp, after):
    bufs, fsend, frecv, owner = grp

    def body_fn(b_in, s_in, s_out):
        x, y, c = _me()
        sibling = (x, y, 1 - c)
        chips = _other_chips(x, y)

        def each(wait):
            for a, r in enumerate(b_in):
                for k, (cx, cy) in enumerate(chips):
                    land = r.at[2 * cx + cy]
                    wait(_rcopy(land, land, s_in[0].at[3 * a + k], s_in[1].at[3 * a + k], sibling))

        @pl.when(c == owner)
        def _():
            each(lambda cp: cp.wait_send())

        @pl.when(c != owner)
        def _():
            each(lambda cp: cp.wait_recv())

    outs, _, _ = _split_call(name, body_fn, bufs, [fsend, frecv], [], after)
    return outs


def _by_owner(owners):
    return [[a for a, o_ in enumerate(owners) if o_ == o] for o in range(2)]


def pair_send_start(name, gs, owners, after):
    n = len(gs)
    lands = [lax.empty(g.shape, g.dtype) for g in gs]

    def body_fn(b_in, s_in, s_out):
        x, y, c = _me()
        for o, idx in enumerate(_by_owner(owners)):
            @pl.when(c == 1 - o)
            def _(o=o, idx=idx):
                for a in idx:
                    _rcopy(b_in[a], b_in[n + a], s_out[0].at[a], s_out[1].at[a], (x, y, o)).start()

    outs, sems, token = _split_call(name, body_fn, list(gs) + lands, [], [n, n], after)
    return (outs[:n], outs[n:], sems[0], sems[1], owners), token


def pair_send_wait(name, st, after):
    gs, lands, send, recv, owners = st
    n = len(gs)

    def body_fn(b_in, s_in, s_out):
        x, y, c = _me()
        for o, idx in enumerate(_by_owner(owners)):
            @pl.when(c == 1 - o)
            def _(o=o, idx=idx):
                for a in idx:
                    _rcopy(b_in[a], b_in[n + a], s_in[0].at[a], s_in[1].at[a], (x, y, o)).wait_send()

            @pl.when(c == o)
            def _(o=o, idx=idx):
                for a in idx:
                    _rcopy(b_in[a], b_in[n + a], s_in[0].at[a], s_in[1].at[a], (x, y, 1 - o)).wait_recv()

    outs, _, _ = _split_call(name, body_fn, list(gs) + list(lands), [send, recv], [], after)
    return outs[:n], outs[n:]


def chip_exchange_start(name, psums, owners, after):
    n = len(psums)
    lands = [lax.empty((3,) + p.shape[1:], p.dtype) for p in psums]

    def body_fn(b_in, s_in, s_out):
        x, y, c = _me()
        chips = _other_chips(x, y)
        for o, idx in enumerate(_by_owner(owners)):
            @pl.when(c == o)
            def _(idx=idx):
                for a in idx:
                    for k, (cx, cy) in enumerate(chips):
                        _rcopy(b_in[a].at[2 * cx + cy], b_in[n + a].at[k], s_out[0].at[3 * a + k],
                               s_out[1].at[3 * a + k], (cx, cy, c)).start()

    outs, sems, token = _split_call(name, body_fn, list(psums) + lands, [], [3 * n, 3 * n], after)
    return (outs[:n], outs[n:], sems[0], sems[1], owners), token


def chip_exchange_wait(name, st, after):
    psums, lands, send, recv, owners = st
    n = len(psums)

    def body_fn(b_in, s_in, s_out):
        x, y, c = _me()
        chips = _other_chips(x, y)
        for o, idx in enumerate(_by_owner(owners)):
            @pl.when(c == o)
            def _(idx=idx):
                for a in idx:
                    for k, (cx, cy) in enumerate(chips):
                        cp = _rcopy(b_in[a].at[2 * cx + cy], b_in[n + a].at[k], s_in[0].at[3 * a + k],
                                    s_in[1].at[3 * a + k], (cx, cy, c))
                        cp.wait_send()
                        cp.wait_recv()

    outs, _, _ = _split_call(name, body_fn, list(psums) + list(lands), [send, recv], [], after)
    return outs[:n], outs[n:]


def pair_sum(name, g, recv, flag):
    shape = g.shape
    cols = shape[-1]
    rows = math.prod(shape[:-1])
    tr = _row_tile(rows, cols, target=4 * 2**20)

    def body(f_ref, g_ref, r_ref, o_ref):
        o_ref[...] = (g_ref[...] + r_ref[...]).astype(bf16)

    blk = pl.BlockSpec((tr, cols), lambda i, f_ref: (i * f_ref[0], 0))
    out = pl.pallas_call(
        body, name=name,
        grid_spec=pltpu.PrefetchScalarGridSpec(num_scalar_prefetch=1, grid=(rows // tr,), in_specs=[blk, blk],
                                               out_specs=blk),
        out_shape=SDS((rows, cols), bf16), compiler_params=_cp("arbitrary"),
    )(flag, g.reshape(rows, cols), recv.reshape(rows, cols))
    return out.reshape(shape)


def chip_sum(name, psum, recv, qf_arr, layer, prev):
    shard = psum.shape[1:]
    cols = shard[-1]
    rows = math.prod(shard[:-1])
    tr = _row_tile(rows, cols, target=4 * 2**20)

    def body(qf_ref, p_ref, r_ref, *rest):
        rest[-1][0] = ((p_ref[0].astype(f32) + r_ref[0].astype(f32)) + r_ref[1].astype(f32)) + r_ref[2].astype(f32)

    in_specs = [pl.BlockSpec((1, tr, cols), lambda i, qf: (qf[0], i * qf[1], 0)),
                pl.BlockSpec((3, tr, cols), lambda i, qf: (0, i * qf[1], 0))]
    args = [qf_arr, psum.reshape(N_CHIPS, rows, cols), recv.reshape(3, rows, cols)]
    aliases = {}
    if prev is not None:
        in_specs.append(pl.BlockSpec(memory_space=pl.ANY))
        args.append(prev.reshape(DEPTH, rows, cols))
        aliases = {3: 0}
    out = pl.pallas_call(
        body, name=name,
        grid_spec=pltpu.PrefetchScalarGridSpec(
            num_scalar_prefetch=1, grid=(rows // tr,), in_specs=in_specs,
            out_specs=pl.BlockSpec((1, tr, cols), lambda i, qf: (layer, i * qf[1], 0))),
        out_shape=SDS((DEPTH, rows, cols), f32), input_output_aliases=aliases, compiler_params=_cp("arbitrary"),
    )(*args)
    return out.reshape((DEPTH,) + shard)


W_NAMES = ("f1w13", "f1w2", "win", "wuq", "wukv", "wout", "mwq", "mwkv", "mwo", "f2w13", "f2w2")
MIX_NAMES = ("win", "wuq", "wukv")
MID_NAMES = ("wout", "mwq", "mwkv", "mwo")
FFN2_NAMES = ("f2w13", "f2w2")
REDUCER = (dict(f1w13=0, f1w2=1, f2w13=0, win=0, wuq=0, wukv=0, f2w2=1, mwkv=1, wout=1, mwq=1, mwo=1),
           dict(f1w13=0, f2w2=0, mwkv=0, wout=0, f2w13=1, f1w2=1, mwq=1, mwo=1, win=1, wuq=1, wukv=1))


def kernel(x, mem, positions, ln_g, ln_b, ffn1_w13, ffn1_w2, w_in, pool_w, pool_scale, q_norm_g, w_uq, kv_norm_g, w_ukv, w_out, mem_wq, mem_wkv, mem_wo, ffn2_w13, ffn2_w2, loss_target, m_ln_g, m_ln_b, m_ffn1_w13, m_ffn1_w2, m_w_in, m_pool_w, m_pool_scale, m_q_norm_g, m_w_uq, m_kv_norm_g, m_w_ukv, m_w_out, m_mem_wq, m_mem_wkv, m_mem_wo, m_ffn2_w13, m_ffn2_w2, v_ln_g, v_ln_b, v_ffn1_w13, v_ffn1_w2, v_w_in, v_pool_w, v_pool_scale, v_q_norm_g, v_w_uq, v_kv_norm_g, v_w_ukv, v_w_out, v_mem_wq, v_mem_wkv, v_mem_wo, v_ffn2_w13, v_ffn2_w2):
    L = DEPTH
    qx, qy, _ = _me()
    chip = 2 * qx + qy
    vec = lambda a: a.reshape(1, -1)

    shards = dict(zip(W_NAMES, (ffn1_w13, ffn1_w2, w_in, w_uq, w_ukv, w_out, mem_wq, mem_wkv, mem_wo, ffn2_w13, ffn2_w2)))

    def place(sh, slot):
        return lax.dynamic_update_slice(lax.empty((N_CHIPS,) + sh.shape, bf16), sh.astype(bf16)[None],
                                        (slot,) + (0,) * sh.ndim)

    first = ("f1w13", "f1w2")
    bufs = [dict(), dict()]
    for n in first:
        bufs[0][n] = place(shards[n][0], chip)
    gw = [dict(), dict()]

    ln_pad = jnp.zeros((2, L, 4, N_CHIPS, D_MODEL // N_CHIPS), f32)
    ln_pad = lax.dynamic_update_slice(ln_pad, jnp.stack([ln_g, ln_b])[:, :, :, None, :], (0, 0, 0, chip, 0))
    ln_sum = allsum_small("allsum_ln", ln_pad.reshape(-1, 128))
    ln_full = (ln_sum * 0.5).reshape(2, L, 4, D_MODEL)
    lng, lnb = ln_full[0], ln_full[1]

    (g0, g_w2), tok = gather_start("gather_a_start", [([bufs[0]["f1w13"]], 0), ([bufs[0]["f1w2"]], 0)], ln_sum)
    chip_then = chip + tok[0, 0].astype(jnp.int32)
    for l in range(L):
        for n in W_NAMES:
            if n not in bufs[l]:
                bufs[l][n] = place(shards[n][l], chip_then)
    others = tuple(bufs[l][n] for l in range(L) for n in W_NAMES if (l, n) not in ((0, first[0]), (0, first[1])))
    g0, tok = gather_forward("gather_a_forward", g0, others)
    (gw[0]["f1w13"],) = gather_finish("gather_a_finish", g0, None)
    (g_mix, g_mid, g_ffn2, g_l1), tok_b = gather_start(
        "gather_b_start",
        [([bufs[0][n] for n in MIX_NAMES], 0), ([bufs[0][n] for n in MID_NAMES], 0), ([bufs[0][n] for n in FFN2_NAMES], 0),
         ([bufs[1][n] for n in W_NAMES], 1)], tok)

    half = QK_ROPE // 2
    inv_freq = ROPE_BASE ** (-jnp.arange(half, dtype=f32) / half)
    ang = positions[0].astype(f32)[:, None] * inv_freq
    cos, sin = jnp.cos(ang), jnp.sin(ang)
    cs = jnp.concatenate([cos, cos, sin, sin], axis=-1)

    memb = mem[0].astype(bf16)
    xf = x[0]
    xb = xf.astype(bf16)
    dep = (tok_b,)

    saved, W = [], [None, None]
    for l in range(L):
        sv = {}
        if l == 1:
            gl1 = gather_finish("gather_l1_finish", g_l1, xb)
            gw[1] = dict(zip(W_NAMES, gl1))
        sv["x0b"] = xb
        f1w13 = gw[l]["f1w13"][None]
        gate, up, act = ffn_up(f"ffn1_up_{l}", xb, f1w13, 0, dep)
        dep = ()
        if l == 0:
            g_w2, _ = gather_forward("gather_w2_forward", g_w2, act)
            (gw[0]["f1w2"],) = gather_finish("gather_w2_finish", g_w2, act)
            g_mix, _ = gather_forward("gather_mix_forward", g_mix, act)
        z1, x1f, x1b = proj_res_ln(f"ffn1_down_{l}", [act], [gw[l]["f1w2"].reshape(1, D_FF, D_MODEL)], [0], xf,
                                   vec(lng[l, 0]), vec(lnb[l, 0]), 0.5)
        sv.update(gate1=gate, up1=up, act1=act, z1=z1, x1b=x1b)
        if l == 0:
            gw[0].update(zip(MIX_NAMES, gather_finish("gather_mix_finish", g_mix, x1b)))
            g_mid, _ = gather_forward("gather_mid_forward", g_mid, x1b)
        win = gw[l]["win"].reshape(D_MODEL, D_IN)
        win_ext = jnp.concatenate([win, _swap_half(win[:, D_IN - QK_ROPE:])], axis=-1)[None]
        wuq = _from_col_shards(gw[l]["wuq"]).reshape(Q_LORA, MLA_HEADS, QK_NOPE + QK_ROPE)
        wq_ext = jnp.concatenate([wuq, _swap_half(wuq[..., QK_NOPE:])], axis=-1).transpose(1, 0, 2)[None]
        wukv = _from_col_shards(gw[l]["wukv"])[None]
        wbd = _block_diag(pool_w[l][None].astype(bf16))[0]
        u, cq, ckv, cqn, ckvn, q, k, v = mix_pre(f"mix_pre_{l}", x1b, win_ext, wq_ext, wukv, 0,
                                                   vec(q_norm_g[l]), vec(kv_norm_g[l]), cs)
        dpool, ypool = pool_fwd(f"pool_fwd_{l}", u, wbd, vec(pool_scale[l]))
        o, lse = mla_attn_fwd(f"mla_fwd_{l}", q, k, v)
        if l == 0:
            gw[0].update(zip(MID_NAMES, gather_finish("gather_mid_finish", g_mid, o)))
            g_ffn2, tok_f = gather_forward("gather_ffn2_forward", g_ffn2, o)
            dep = (tok_f,)
        wout = gw[l]["wout"].reshape(D_MODEL, D_MODEL)
        wout_pool, wout_mla = wout[None, :POOL_WIDTH], wout[None, POOL_WIDTH:]
        mwq = gw[l]["mwq"].reshape(1, D_MODEL, D_MODEL)
        mwo = gw[l]["mwo"].reshape(1, D_MODEL, D_MODEL)
        mwkv = gw[l]["mwkv"][None]
        z2, x2f, x2b = proj_res_ln(f"mix_out_{l}", [ypool, o], [wout_pool, wout_mla], [0, 0], x1f,
                                   vec(lng[l, 1]), vec(lnb[l, 1]), 1.0, dep)
        dep = ()
        sv.update(cq=cq, ckv=ckv, cqn=cqn, ckvn=ckvn, q=q, k=k, v=v, dpool=dpool, ypool=ypool, o=o, lse=lse, z2=z2, x2b=x2b)
        kvm = mm_nn_shard(f"mem_kv_{l}", memb, mwkv, 0)
        cq_, co_, z3, x3f, x3b = cross_fwd(f"cross_fwd_{l}", x2b, x2f, mwq, mwo, 0, kvm, vec(lng[l, 2]), vec(lnb[l, 2]))
        sv.update(kvm=kvm, crq=cq_, cro=co_, z3=z3, x3b=x3b)
        if l == 0:
            gw[0].update(zip(FFN2_NAMES, gather_finish("gather_ffn2_finish", g_ffn2, x3b)))
            g_l1, tok_l = gather_forward("gather_l1_forward", g_l1, x3b)
            dep = (tok_l,)
        f2w13 = gw[l]["f2w13"][None]
        f2w2 = gw[l]["f2w2"].reshape(1, D_FF, D_MODEL)
        gate, up, act = ffn_up(f"ffn2_up_{l}", x3b, f2w13, 0, dep)
        dep = ()
        z4, xf, xb = proj_res_ln(f"ffn2_down_{l}", [act], [f2w2], [0], x3f, vec(lng[l, 3]), vec(lnb[l, 3]), 0.5)
        sv.update(gate2=gate, up2=up, act2=act, z4=z4)
        W[l] = dict(f1w13=f1w13, f1w2=gw[l]["f1w2"].reshape(1, D_FF, D_MODEL), win_ext=win_ext, wq_ext=wq_ext, wukv=wukv,
                    wbd=wbd, wout=wout[None], mwq=mwq, mwo=mwo, f2w13=f2w13, f2w2=f2w2)
        saved.append(sv)

    dln = {}
    dzb, dres, *dln[L - 1, 3], loss_blk = loss_grad("loss_grad", xf, loss_target[0],
                                                   (saved[L - 1]["z4"], vec(lng[L - 1, 3]), 0.5))
    loss = lax.psum(loss_blk[0, 0], ("x", "y", "c"))

    row_shards = lambda a: a.reshape(N_CHIPS, a.shape[0] // N_CHIPS, a.shape[1])
    small = {k_: [None] * L for k_ in ("pool_w", "pool_scale", "gq", "gkv", "lng", "lnb")}
    rest_names = [n for n in W_NAMES if n not in ("f1w13", "f1w2")]
    core = lax.axis_index("c")
    flags = [jnp.reshape(core == o, (1,)).astype(jnp.int32) for o in range(2)]
    qfs = [jnp.stack([chip, (core == o).astype(jnp.int32)]).astype(jnp.int32) for o in range(2)]

    def red_begin(tag, names, gs, layer):
        owners = [REDUCER[layer][n] for n in names]
        st, tok_ = pair_send_start(f"pair_send_start_{tag}", gs, owners, None)
        return (st, owners), tok_

    def red_mid(tag, sto, after):
        st, owners = sto
        gs_, lands_ = pair_send_wait(f"pair_send_wait_{tag}", st, after)
        ps = [pair_sum(f"pair_sum_{tag}_{a}", g_, r_, flags[o]) for a, (g_, r_, o) in enumerate(zip(gs_, lands_, owners))]
        st, tok_ = chip_exchange_start(f"chip_exchange_start_{tag}", ps, owners, None)
        return (st, owners), tok_

    def red_end(tag, sto, layer, prevs, after):
        st, owners = sto
        ps, lands_ = chip_exchange_wait(f"chip_exchange_wait_{tag}", st, after)
        return [chip_sum(f"chip_sum_{tag}_{a}", p_, r_, qfs[o], layer, s_)
                for a, (p_, r_, s_, o) in enumerate(zip(ps, lands_, prevs, owners))]

    def share_start(tag, names, sums_):
        return pair_share_start(f"pair_share_start_{tag}", sums_, [(REDUCER[0][n], REDUCER[1][n]) for n in names], None)

    st_p1 = st_c1 = st_pa = st_ca = None
    for l in reversed(range(L)):
        sv, w = saved[l], W[l]
        g = {}
        dh = ffn_bwd_da(f"ffn2_bwd_da_{l}", dzb, w["f2w2"], 0, sv["gate2"], sv["up2"], dep)
        dep = ()
        g["f2w2"] = row_shards(mm_tn(f"ffn2_dw2_{l}", sv["act2"], dzb))
        g["f2w13"] = mm_tn(f"ffn2_dw13_{l}", sv["x3b"], dh, True)
        dzb, dres, *dln[l, 2] = ffn_dx(f"ffn2_dx_{l}", dh, w["f2w13"], 0, dres, (sv["z3"], vec(lng[l, 2]), 1.0))
        if l == 0:
            st_c1, tok = red_mid("l1", st_p1, dzb)
            dep = (tok, g["f2w2"], g["f2w13"])
        dqc, dkvm = cross_bwd(f"cross_bwd_{l}", dzb, w["mwo"], 0, sv["crq"], sv["kvm"], dep)
        dep = ()
        g["mwo"] = row_shards(mm_tn(f"cross_dwo_{l}", sv["cro"], dzb))
        g["mwq"] = row_shards(mm_tn(f"cross_dwq_{l}", sv["x2b"], dqc))
        g["mwkv"] = mm_tn(f"cross_dwkv_{l}", memb, dkvm, True)
        dzb, dres, *dln[l, 1] = mm_nt_res(f"cross_dx_{l}", [dqc], [w["mwq"]], [0], dres, f32,
                                          (sv["z2"], vec(lng[l, 1]), 1.0))
        dcat = mm_nt_res(f"mix_dcat_{l}", [dzb], [w["wout"]], [0], None, bf16)
        dwo_p = mm_tn(f"mix_dwout_pool_{l}", sv["ypool"], dzb)
        dwo_m = mm_tn(f"mix_dwout_mla_{l}", sv["o"], dzb)
        g["wout"] = row_shards(jnp.concatenate([dwo_p, dwo_m], axis=0))
        dq, dk, dv = mla_attn_bwd(f"mla_bwd_{l}", sv["q"], sv["k"], sv["v"], sv["o"], dcat, sv["lse"], POOL_WIDTH // 128)
        dqe, dkv, dh_rest, dgq, dgkv = mix_post_bwd(f"mix_post_bwd_{l}", dq, dk, dv, w["wq_ext"], w["wukv"], 0, sv["cq"],
                                                     sv["ckv"], vec(q_norm_g[l]), vec(kv_norm_g[l]), cs)
        du, dyw, dscale = pool_bwd(f"pool_bwd_{l}", dcat, sv["dpool"], w["wbd"], vec(pool_scale[l]))
        dwq_e = mm_tn(f"mix_dwuq_{l}", sv["cqn"], dqe).reshape(Q_LORA, MLA_HEADS, 256)
        g["wuq"] = _to_col_shards(jnp.concatenate(
            [dwq_e[..., :QK_NOPE], _unswap_add(dwq_e[..., QK_NOPE:QK_NOPE + QK_ROPE], dwq_e[..., QK_NOPE + QK_ROPE:])],
            axis=-1).reshape(Q_LORA, MLA_HEADS * (QK_NOPE + QK_ROPE)))
        g["wukv"] = _to_col_shards(mm_tn(f"mix_dwukv_{l}", sv["ckvn"], dkv))
        dwbd = mm_tn(f"pool_dw_{l}", sv["dpool"], dyw)
        small["pool_w"][l] = jnp.stack([dwbd[64 * gi:64 * gi + 64, 64 * gi:64 * gi + 64] for gi in range(4)])
        small["pool_scale"][l], small["gq"][l], small["gkv"][l] = dscale[0], dgq[0], dgkv[0]
        dh_ext = jnp.concatenate([du, dh_rest], axis=1)
        dwin_e = mm_tn(f"mix_dwin_{l}", sv["x1b"], dh_ext)
        g["win"] = row_shards(jnp.concatenate(
            [dwin_e[:, :D_IN - QK_ROPE], _unswap_add(dwin_e[:, D_IN - QK_ROPE:D_IN], dwin_e[:, D_IN:])], axis=-1))
        dzb, dres, *dln[l, 0] = mm_nt_res(f"mix_dx_{l}", [dh_ext], [w["win_ext"]], [0], dres, f32,
                                          (sv["z1"], vec(lng[l, 0]), 0.5))
        if l == 0:
            st_pa, tok = red_begin("a0", rest_names, [g[n] for n in rest_names], 0)
            dep = (tok,)
        dh = ffn_bwd_da(f"ffn1_bwd_da_{l}", dzb, w["f1w2"], 0, sv["gate1"], sv["up1"], dep)
        dep = ()
        if l == 0:
            grad_x = ffn_dx(f"ffn1_dx_{l}", dh, w["f1w13"], 0, dres)[None]
            for l_ in range(L):
                small["lng"][l_] = jnp.concatenate([dln[l_, k_][0] for k_ in range(4)], axis=0)
                small["lnb"][l_] = jnp.concatenate([dln[l_, k_][1] for k_ in range(4)], axis=0)
            rep = [jnp.stack(small[k_]).reshape(-1) for k_ in ("pool_w", "pool_scale", "gq", "gkv", "lng", "lnb")]
            sizes = [r.shape[0] for r in rep]
            packed = jnp.concatenate(rep)
            packed = jnp.pad(packed, (0, (-packed.shape[0]) % 1024)).reshape(-1, 128)
            tot = allsum_small("allsum_small_grads", packed, (grad_x,)).reshape(-1)
            st_ca, tok = red_mid("a0", st_pa, (grad_x, tot))
            dep = (tok,)
        else:
            below = ffn_dx(f"ffn1_dx_{l}", dh, w["f1w13"], 0, dres, (saved[l - 1]["z4"], vec(lng[l - 1, 3]), 0.5))
            dln[l - 1, 3] = below[2:]
        g["f1w2"] = row_shards(mm_tn(f"ffn1_dw2_{l}", sv["act1"], dzb, False, dep))
        g["f1w13"] = mm_tn(f"ffn1_dw13_{l}", sv["x0b"], dh, True, dep)
        dep = ()
        if l > 0:
            dzb, dres = below[:2]
        if l == 1:
            st_p1, tok = red_begin("l1", W_NAMES, [g[n] for n in W_NAMES], 1)
            dep = (tok,)

    st_pb, _ = red_begin("b0", ("f1w13", "f1w2"), [g["f1w13"], g["f1w2"]], 0)
    sums1 = dict(zip(W_NAMES, red_end("l1", st_c1, 1, [None] * len(W_NAMES), g["f1w13"])))
    st_cb, tok = red_mid("b0", st_pb, tuple(sums1.values()))
    sums0 = red_end("a0", st_ca, 0, [sums1[n] for n in rest_names], tok)
    share_a, tok_a = share_start("a", rest_names, sums0)

    offs = [0]
    for s_ in sizes:
        offs.append(offs[-1] + s_)
    parts = [tot[offs[i]:offs[i + 1]] for i in range(len(sizes))]
    g_pool_w = parts[0].reshape(pool_w.shape)
    g_pool_scale = parts[1].reshape(pool_scale.shape)
    g_gq = parts[2].reshape(q_norm_g.shape)
    g_gkv = parts[3].reshape(kv_norm_g.shape)
    shard_cols = lambda a: lax.dynamic_slice_in_dim(a.reshape(L, 4, D_MODEL), chip * (D_MODEL // N_CHIPS),
                                                    D_MODEL // N_CHIPS, axis=2)
    g_lng, g_lnb = shard_cols(parts[4]), shard_cols(parts[5])

    out_names = ("lng", "lnb", "f1w13", "f1w2", "win", "pool_w", "pool_scale", "gq", "wuq", "gkv", "wukv", "wout", "mwq",
                 "mwkv", "mwo", "f2w13", "f2w2")
    big = dict(lng=g_lng, lnb=g_lnb, pool_w=g_pool_w, pool_scale=g_pool_scale, gq=g_gq, gkv=g_gkv)
    late = ("f1w13", "f1w2")
    held = ("f2w13", "f2w2")
    ws = [ln_g, ln_b, ffn1_w13, ffn1_w2, w_in, pool_w, pool_scale, q_norm_g, w_uq, kv_norm_g, w_ukv, w_out, mem_wq,
          mem_wkv, mem_wo, ffn2_w13, ffn2_w2]
    ms = [m_ln_g, m_ln_b, m_ffn1_w13, m_ffn1_w2, m_w_in, m_pool_w, m_pool_scale, m_q_norm_g, m_w_uq, m_kv_norm_g, m_w_ukv,
          m_w_out, m_mem_wq, m_mem_wkv, m_mem_wo, m_ffn2_w13, m_ffn2_w2]
    vs = [v_ln_g, v_ln_b, v_ffn1_w13, v_ffn1_w2, v_w_in, v_pool_w, v_pool_scale, v_q_norm_g, v_w_uq, v_kv_norm_g, v_w_ukv,
          v_w_out, v_mem_wq, v_mem_wkv, v_mem_wo, v_ffn2_w13, v_ffn2_w2]
    res = {}

    def update(n, deps=()):
        a = out_names.index(n)
        res[a] = adamw(f"adamw_{a}", ws[a], big[n].reshape(ws[a].shape), ms[a], vs[a], deps)
        return res[a][0]

    small_done = tuple(update(n, (tok_a,)) for n in ("lng", "lnb", "pool_w", "pool_scale", "gq", "gkv"))
    big.update(zip(rest_names, pair_share_wait("pair_share_wait_a", share_a, small_done)))
    first_done = tuple(update(n) for n in rest_names if n not in held)
    sums_b = red_end("b0", st_cb, 0, [sums1[n] for n in late], first_done)
    share_b, tok_b = share_start("b", late, sums_b)
    held_done = tuple(update(n, (tok_b,)) for n in held)
    big.update(zip(late, pair_share_wait("pair_share_wait_b", share_b, held_done)))
    for n in late:
        update(n)
    order = range(len(out_names))
    grads = [big[n].reshape(w_.shape) for n, w_ in zip(out_names, ws)]
    return (loss, grad_x, *grads, *[res[a][0] for a in order], *[res[a][1] for a in order], *[res[a][2] for a in order])
```
